```python
import jax, jax.numpy as jnp
from jax import lax
import numpy as np

D_MODEL = 1024
BATCH = 8
SEQ = 2048
DEPTH = 2

D_MIX = D_MODEL
D_POOL = D_MIX // 2
POOL_WINDOWS = (2, 4, 8, 16)
N_POOL_GROUPS = len(POOL_WINDOWS)
POOL_GROUP = D_POOL // N_POOL_GROUPS
HEAD_DIM = 64
D_ATTN = D_MIX - D_POOL
N_HEADS = D_ATTN // HEAD_DIM
N_KV_HEADS = 2
GQA_GROUP = N_HEADS // N_KV_HEADS
D_KV = N_KV_HEADS * HEAD_DIM
WINDOW = 128
BLOCK = 128
IN_WIDTHS = (D_POOL, D_POOL, D_ATTN, D_KV, D_KV, D_ATTN)
D_IN = sum(IN_WIDTHS)
EPS = 1e-6
NEG_INF = -1e30

kernel_name = "hybrid_pool_swa_sink_parallel_heads"


def rmsnorm(x, gain):
    x32 = x.astype(jnp.float32)
    y = x32 * lax.rsqrt(jnp.mean(x32 * x32, axis=-1, keepdims=True) + EPS) * gain.astype(jnp.float32)
    return y.astype(x.dtype)


def alibi_slopes():
    return jnp.exp2(-8.0 * jnp.arange(1, N_HEADS + 1, dtype=jnp.float32) / N_HEADS)


def pool_mixer(u, w_grp, scale):
    B, S, _ = u.shape
    u32 = u.astype(jnp.float32).reshape(B, S, N_POOL_GROUPS, POOL_GROUP)
    csum = jnp.cumsum(u32, axis=1)
    csum = jnp.concatenate([jnp.zeros_like(csum[:, :1]), csum], axis=1)
    pos = jnp.arange(1, S + 1, dtype=jnp.float32)
    means = []
    for g, w in enumerate(POOL_WINDOWS):
        c = csum[:, :, g]
        lo = jnp.concatenate([jnp.zeros_like(c[:, :w - 1]), c[:, :S + 1 - w]], axis=1)
        count = jnp.minimum(pos, float(w))[None, :, None]
        means.append((c[:, 1:] - lo) / count)
    pooled = jnp.stack(means, axis=2) - u32
    mixed = jnp.einsum('bsgc,gcd->bsgd', pooled.astype(u.dtype), w_grp)
    return mixed.reshape(B, S, D_POOL) * scale


def swa_sink_attention(q, k, v, sinks):
    B, S, _ = q.shape
    NB = S // BLOCK
    q = q.reshape(B, NB, BLOCK, N_KV_HEADS, GQA_GROUP, HEAD_DIM)
    k = k.reshape(B, NB, BLOCK, N_KV_HEADS, HEAD_DIM)
    v = v.reshape(B, NB, BLOCK, N_KV_HEADS, HEAD_DIM)

    def with_prev(t):
        prev = jnp.concatenate([jnp.zeros_like(t[:, :1]), t[:, :-1]], axis=1)
        return jnp.concatenate([prev, t], axis=2)

    kb, vb = with_prev(k), with_prev(v)
    scores = jnp.einsum('bnqhgd,bnkhd->bnhgqk', q, kb).astype(jnp.float32) * (HEAD_DIM ** -0.5)
    qi = jnp.arange(BLOCK)[:, None]
    kj = jnp.arange(2 * BLOCK)[None, :]
    dist = qi + BLOCK - kj
    in_win = (dist >= 0) & (dist < WINDOW)
    key_exists = (jnp.arange(NB)[:, None, None] > 0) | (kj >= BLOCK)[None]
    valid = in_win[None] & key_exists
    slopes = alibi_slopes().reshape(N_KV_HEADS, GQA_GROUP)
    bias = -slopes[:, :, None, None] * dist.astype(jnp.float32)
    scores = jnp.where(valid[None, :, None, None], scores + bias, NEG_INF)
    sink = jnp.broadcast_to(sinks.astype(jnp.float32).reshape(N_KV_HEADS, GQA_GROUP, 1, 1),
                            scores.shape[:-1] + (1,))
    probs = jax.nn.softmax(jnp.concatenate([scores, sink], axis=-1), axis=-1)[..., :-1]
    out = jnp.einsum('bnhgqk,bnkhd->bnqhgd', probs.astype(v.dtype), vb)
    return out.reshape(B, S, D_ATTN)


def _fwd_setup_inputs(seed: int = 0) -> dict:
    key = jax.random.key(seed)
    ks = jax.random.split(key, 9)
    x = jax.random.normal(ks[0], (BATCH, SEQ, D_MODEL), jnp.float32)
    w_in = jax.random.normal(ks[1], (DEPTH, D_MODEL, D_IN), jnp.float32) * D_MODEL ** -0.5
    pool_w = jax.random.normal(ks[2], (DEPTH, N_POOL_GROUPS, POOL_GROUP, POOL_GROUP), jnp.float32) * POOL_GROUP ** -0.5
    pool_scale = 1.0 + 0.1 * jax.random.normal(ks[3], (DEPTH, D_POOL), jnp.float32)
    attn_sinks = 0.5 * jax.random.normal(ks[4], (DEPTH, N_HEADS), jnp.float32)
    w_out = jax.random.normal(ks[5], (DEPTH, D_MIX, D_MODEL), jnp.float32) * D_MIX ** -0.5
    norm_pre = 1.0 + 0.1 * jax.random.normal(ks[6], (DEPTH, D_MODEL), jnp.float32)
    norm_post = 1.0 + 0.1 * jax.random.normal(ks[7], (DEPTH, D_MODEL), jnp.float32)
    return {"x": x, "w_in": w_in, "pool_w": pool_w, "pool_scale": pool_scale,
            "attn_sinks": attn_sinks, "w_out": w_out, "norm_pre": norm_pre, "norm_post": norm_post}


def _fwd_reference(x, w_in, pool_w, pool_scale, attn_sinks, w_out, norm_pre, norm_post):
    splits = [int(s) for s in np.cumsum(IN_WIDTHS)[:-1]]
    for layer in range(DEPTH):
        h = rmsnorm(x, norm_pre[layer])
        proj = h @ w_in[layer]
        pool_u, pool_gate, q, k, v, attn_gate = jnp.split(proj, splits, axis=-1)
        pool_out = pool_mixer(pool_u, pool_w[layer], pool_scale[layer]) * jax.nn.silu(pool_gate)
        attn_out = swa_sink_attention(q, k, v, attn_sinks[layer]) * jax.nn.silu(attn_gate)
        y = jnp.concatenate([pool_out, attn_out], axis=-1) @ w_out[layer]
        x = x + rmsnorm(y, norm_post[layer])
    return x


import jax as _jax
import jax.numpy as _jnp

TWIN_FORMAT = 'train_step'
FWD_PARAMS = ['x', 'w_in', 'pool_w', 'pool_scale', 'attn_sinks', 'w_out', 'norm_pre', 'norm_post']
TWIN_WEIGHTS = ['w_in', 'pool_w', 'pool_scale', 'attn_sinks', 'w_out', 'norm_pre', 'norm_post']
TWIN_DIFF_INPUT = 'x'
TWIN_INPUTS = ['x', 'w_in', 'pool_w', 'pool_scale', 'attn_sinks', 'w_out', 'norm_pre', 'norm_post', 'loss_target', 'm_w_in', 'm_pool_w', 'm_pool_scale', 'm_attn_sinks', 'm_w_out', 'm_norm_pre', 'm_norm_post', 'v_w_in', 'v_pool_w', 'v_pool_scale', 'v_attn_sinks', 'v_w_out', 'v_norm_pre', 'v_norm_post']
TWIN_OUTPUTS = ['loss', 'grad_x', 'grad_w_in', 'grad_pool_w', 'grad_pool_scale', 'grad_attn_sinks', 'grad_w_out', 'grad_norm_pre', 'grad_norm_post', 'delta_w_in', 'delta_pool_w', 'delta_pool_scale', 'delta_attn_sinks', 'delta_w_out', 'delta_norm_pre', 'delta_norm_post', 'new_m_w_in', 'new_m_pool_w', 'new_m_pool_scale', 'new_m_attn_sinks', 'new_m_w_out', 'new_m_norm_pre', 'new_m_norm_post', 'new_v_w_in', 'new_v_pool_w', 'new_v_pool_scale', 'new_v_attn_sinks', 'new_v_w_out', 'new_v_norm_pre', 'new_v_norm_post']
TWIN_LEAF_KINDS = {'loss': 'loss', 'grad_x': 'grad_x', 'grad_w_in': 'grad_w', 'grad_pool_w': 'grad_w', 'grad_pool_scale': 'grad_w', 'grad_attn_sinks': 'grad_w', 'grad_w_out': 'grad_w', 'grad_norm_pre': 'grad_w', 'grad_norm_post': 'grad_w', 'delta_w_in': 'delta_w', 'delta_pool_w': 'delta_w', 'delta_pool_scale': 'delta_w', 'delta_attn_sinks': 'delta_w', 'delta_w_out': 'delta_w', 'delta_norm_pre': 'delta_w', 'delta_norm_post': 'delta_w', 'new_m_w_in': 'new_m', 'new_m_pool_w': 'new_m', 'new_m_pool_scale': 'new_m', 'new_m_attn_sinks': 'new_m', 'new_m_w_out': 'new_m', 'new_m_norm_pre': 'new_m', 'new_m_norm_post': 'new_m', 'new_v_w_in': 'new_v', 'new_v_pool_w': 'new_v', 'new_v_pool_scale': 'new_v', 'new_v_attn_sinks': 'new_v', 'new_v_w_out': 'new_v', 'new_v_norm_pre': 'new_v', 'new_v_norm_post': 'new_v'}


def _forward(args):
    return _fwd_reference(*[args[k] for k in FWD_PARAMS])


def _output_shape():
    out = _jax.eval_shape(lambda: _forward(_fwd_setup_inputs(0)))
    return out.shape, out.dtype

N_MICROBATCH = 1
ADAM_LR = 0.001
ADAM_B1 = 0.9
ADAM_B2 = 0.999
ADAM_EPS = 1e-08
ADAM_WD = 0.01
ADAM_STEP = 10
PER_EXAMPLE_BATCH_AXIS = {'x': 0, 'loss_target': 0}
SHARED_INPUTS = []
_WEIGHT_DTYPES = {'w_in': _jnp.float32, 'pool_w': _jnp.float32, 'pool_scale': _jnp.float32, 'attn_sinks': _jnp.float32, 'w_out': _jnp.float32, 'norm_pre': _jnp.float32, 'norm_post': _jnp.float32}
MOMENT_SCALE = {'w_in': 2.879034e-01, 'pool_w': 4.249461e-01, 'pool_scale': 5.068469e-01, 'attn_sinks': 1.805778e-01, 'w_out': 3.411053e-01, 'norm_pre': 4.284252e-01, 'norm_post': 1.607725e+01}


def _to_microbatches(a, axis):
    t = _jnp.moveaxis(a, axis, 0)
    t = t.reshape((N_MICROBATCH, t.shape[0] // N_MICROBATCH) + t.shape[1:])
    return _jnp.moveaxis(t, 1, axis + 1)


def setup_inputs(seed: int = 0) -> dict:
    inp = _fwd_setup_inputs(seed)
    key = _jax.random.fold_in(_jax.random.key(seed), 7919)
    shape, _ = _output_shape()
    out = dict(inp)
    out["loss_target"] = _jax.random.normal(_jax.random.fold_in(key, 0), shape, _jnp.float32)
    for i, name in enumerate(TWIN_WEIGHTS):
        w = inp[name].astype(_jnp.float32)
        if MOMENT_SCALE is None:
            s = _jnp.sqrt(_jnp.mean(_jnp.square(w)) + 1e-30)
        else:
            s = MOMENT_SCALE[name]
        km, kv = _jax.random.split(_jax.random.fold_in(key, i + 1))
        out[name] = w
        out["m_" + name] = s * _jax.random.normal(km, w.shape, _jnp.float32)
        out["v_" + name] = (s * s) * _jax.random.uniform(kv, w.shape, _jnp.float32, 0.5, 1.5)
    if N_MICROBATCH > 1:
        for name, axis in PER_EXAMPLE_BATCH_AXIS.items():
            out[name] = _to_microbatches(out[name], axis)
    return {'x': out['x'], 'w_in': out['w_in'], 'pool_w': out['pool_w'], 'pool_scale': out['pool_scale'], 'attn_sinks': out['attn_sinks'], 'w_out': out['w_out'], 'norm_pre': out['norm_pre'], 'norm_post': out['norm_post'], 'loss_target': out['loss_target'], 'm_w_in': out['m_w_in'], 'm_pool_w': out['m_pool_w'], 'm_pool_scale': out['m_pool_scale'], 'm_attn_sinks': out['m_attn_sinks'], 'm_w_out': out['m_w_out'], 'm_norm_pre': out['m_norm_pre'], 'm_norm_post': out['m_norm_post'], 'v_w_in': out['v_w_in'], 'v_pool_w': out['v_pool_w'], 'v_pool_scale': out['v_pool_scale'], 'v_attn_sinks': out['v_attn_sinks'], 'v_w_out': out['v_w_out'], 'v_norm_pre': out['v_norm_pre'], 'v_norm_post': out['v_norm_post']}


def _loss(weights, diff, rest, loss_target):
    with _jax.named_scope("forward"):
        args = {**rest, TWIN_DIFF_INPUT: diff, **{k: w.astype(_WEIGHT_DTYPES[k]) for k, w in weights.items()}}
        y = _forward(args)
    with _jax.named_scope("loss_head"):
        err = _jnp.square(y.astype(_jnp.float32) - loss_target)
        return 0.5 * _jnp.sum(_jnp.mean(err, axis=-1)) if err.ndim else 0.5 * err


def _adamw(w, g, m, v):
    m = ADAM_B1 * m + (1.0 - ADAM_B1) * g
    v = ADAM_B2 * v + (1.0 - ADAM_B2) * _jnp.square(g)
    m_hat = m / (1.0 - ADAM_B1 ** ADAM_STEP)
    v_hat = v / (1.0 - ADAM_B2 ** ADAM_STEP)
    delta = -ADAM_LR * (m_hat / (_jnp.sqrt(v_hat) + ADAM_EPS) + ADAM_WD * w)
    return delta, m, v


def reference(x, w_in, pool_w, pool_scale, attn_sinks, w_out, norm_pre, norm_post, loss_target, m_w_in, m_pool_w, m_pool_scale, m_attn_sinks, m_w_out, m_norm_pre, m_norm_post, v_w_in, v_pool_w, v_pool_scale, v_attn_sinks, v_w_out, v_norm_pre, v_norm_post):
    given = dict(x=x, w_in=w_in, pool_w=pool_w, pool_scale=pool_scale, attn_sinks=attn_sinks, w_out=w_out, norm_pre=norm_pre, norm_post=norm_post, loss_target=loss_target, m_w_in=m_w_in, m_pool_w=m_pool_w, m_pool_scale=m_pool_scale, m_attn_sinks=m_attn_sinks, m_w_out=m_w_out, m_norm_pre=m_norm_pre, m_norm_post=m_norm_post, v_w_in=v_w_in, v_pool_w=v_pool_w, v_pool_scale=v_pool_scale, v_attn_sinks=v_attn_sinks, v_w_out=v_w_out, v_norm_pre=v_norm_pre, v_norm_post=v_norm_post)
    weights = {n: given[n] for n in TWIN_WEIGHTS}
    shared = {n: given[n] for n in SHARED_INPUTS}
    per_example = {n: given[n] for n in ['x']}
    grad_fn = _jax.value_and_grad(_loss, argnums=(0, 1))

    def one_microbatch(ex, loss_target):
        ex = dict(ex)
        diff = ex.pop(TWIN_DIFF_INPUT)
        return grad_fn(weights, diff, {**shared, **ex}, loss_target)

    if N_MICROBATCH == 1:
        loss, (grad_w, grad_x) = one_microbatch(per_example, given["loss_target"])
    else:
        def body(carry, xs):
            loss_sum, grad_sum = carry
            l_k, (gw_k, gx_k) = one_microbatch(xs[0], xs[1])
            with _jax.named_scope("update"):
                return (loss_sum + l_k, _jax.tree.map(_jnp.add, grad_sum, gw_k)), gx_k

        init = (_jnp.zeros((), _jnp.float32), _jax.tree.map(_jnp.zeros_like, weights))
        (loss, grad_w), grad_x = _jax.lax.scan(body, init, (per_example, given["loss_target"]))
    with _jax.named_scope("update"):
        delta_w, new_m, new_v = {}, {}, {}
        for n in TWIN_WEIGHTS:
            delta_w[n], new_m[n], new_v[n] = _adamw(weights[n], grad_w[n], given["m_" + n], given["v_" + n])
    return (loss, grad_x, *[grad_w[n] for n in TWIN_WEIGHTS], *[delta_w[n] for n in TWIN_WEIGHTS],
            *[new_m[n] for n in TWIN_WEIGHTS], *[new_v[n] for n in TWIN_WEIGHTS])
```

```python
import functools

import jax
import jax.numpy as jnp
from jax import lax
from jax.experimental import pallas as pl
from jax.experimental.pallas import tpu as pltpu

F32 = jnp.float32
BF16 = jnp.bfloat16

S = 2048
D = 1024
DEPTH = 2
D_POOL = 512
POOL_WINDOWS = (2, 4, 8, 16)
N_HEADS = 8
D_IN = 2304
N_SHARDS = 4
W_IN_SHARD = D_IN // N_SHARDS
W_OUT_SHARD = D // N_SHARDS
BLK = 128
NB = S // BLK
HALO = 16
EPS = 1e-6
NEG_INF = -1e30
C_PU, C_PG, C_Q, C_K, C_V, C_AG = 0, 512, 1024, 1536, 1664, 1792

ADAM_LR = 0.001
ADAM_B1 = 0.9
ADAM_B2 = 0.999
ADAM_EPS = 1e-08
ADAM_WD = 0.01
ADAM_STEP = 10

TM = 256
VMEM_LIMIT = 56 * 1024 * 1024

NT = (((1,), (1,)), ((), ()))
TN = (((0,), (0,)), ((), ()))

MESH = pl.DeviceIdType.MESH
ANY = pl.BlockSpec(memory_space=pl.ANY)

ROWS_PW, ROWS_SC, ROWS_NORM, ROWS_SINK = 1024, 8, 16, 8
SMALL_ROWS = 1088
PIECE_ROWS = SMALL_ROWS // 8


def _params(sem=("arbitrary",)):
    return pltpu.CompilerParams(dimension_semantics=sem, vmem_limit_bytes=VMEM_LIMIT)


def _sigmoid(v):
    return 1.0 / (1.0 + jnp.exp(-v))


def _rows8(v):
    r, c = v.shape
    return v.reshape(r // 8, 8, c).sum(axis=0)


def _fwd_in(x, g_pre, w_in):
    def body(x_ref, g_ref, w_ref, pu_ref, pg_ref, q_ref, kv_ref, ag_ref):
        xt = x_ref[...]
        r = lax.rsqrt(jnp.mean(xt * xt, axis=-1, keepdims=True) + EPS)
        h = (xt * r * g_ref[...]).astype(BF16)

        def proj(lo, hi):
            return jnp.dot(h, w_ref[:, lo:hi], preferred_element_type=F32)

        pu_ref[...] = proj(C_PU, C_PG)
        pg_ref[...] = proj(C_PG, C_Q)
        q_ref[...] = proj(C_Q, C_K).astype(BF16)
        kv_ref[...] = proj(C_K, C_AG).astype(BF16)
        ag_ref[...] = proj(C_AG, D_IN)

    row = lambda w: pl.BlockSpec((TM, w), lambda i: (i, 0))
    full = lambda a: pl.BlockSpec(a.shape, lambda i: (0, 0))
    return pl.pallas_call(
        body, name="fwd_in", grid=(S // TM,),
        in_specs=[row(D), full(g_pre), full(w_in)],
        out_specs=[row(512), row(512), row(512), row(256), row(512)],
        out_shape=[jax.ShapeDtypeStruct((S, 512), F32), jax.ShapeDtypeStruct((S, 512), F32),
                   jax.ShapeDtypeStruct((S, 512), BF16), jax.ShapeDtypeStruct((S, 256), BF16),
                   jax.ShapeDtypeStruct((S, 512), F32)],
        compiler_params=_params(),
    )(x, g_pre, w_in)


def _mask_terms(i):
    qi = lax.broadcasted_iota(jnp.int32, (BLK, 2 * BLK), 0)
    kj = lax.broadcasted_iota(jnp.int32, (BLK, 2 * BLK), 1)
    dist = qi + BLK - kj
    valid = (dist >= 0) & (dist < BLK) & ((kj >= BLK) | (i > 0))
    return dist.astype(F32), valid


def _head_variants(cur, prev):
    both = jnp.concatenate([prev, cur], axis=0).astype(F32)
    swapped = pltpu.roll(both, 64, axis=1)
    low = lax.broadcasted_iota(jnp.int32, both.shape, 1) < 64
    zero = jnp.zeros_like(both)
    return ((jnp.where(low, both, zero).astype(BF16), jnp.where(low, zero, swapped).astype(BF16)),
            (jnp.where(low, swapped, zero).astype(BF16), jnp.where(low, zero, both).astype(BF16)))


def _probs(q_tile, k_var, slope, sink, dist, valid):
    s = lax.dot_general(q_tile, k_var, NT, preferred_element_type=F32)
    s = jnp.where(valid, s * 0.125 - slope * dist, NEG_INF)
    m = jnp.maximum(jnp.max(s, axis=-1, keepdims=True), sink)
    p = jnp.exp(s - m)
    e_sink = jnp.exp(sink - m)
    inv = 1.0 / (jnp.sum(p, axis=-1, keepdims=True) + e_sink)
    return p * inv, e_sink * inv


def _pool_block(ext_ref, i, g, w):
    lanes = slice(g * 128, (g + 1) * 128)
    u = ext_ref[HALO:HALO + BLK, lanes]
    acc = u
    for j in range(1, w):
        acc = acc + ext_ref[HALO - j:HALO - j + BLK, lanes]
    t = (i * BLK + lax.broadcasted_iota(jnp.int32, (BLK, 1), 0)).astype(F32)
    inv = 1.0 / jnp.minimum(t + 1.0, float(w))
    return acc * inv - u, inv


def _fwd_mix(pu, pg, q, kv, ag, pool_w, pool_scale, sinks):
    def body(pu_ref, pup_ref, pg_ref, q_ref, kv_ref, kvp_ref, ag_ref, pw_ref, sc_ref, sink_ref, cat_ref, ext_ref):
        i = pl.program_id(0)
        ext_ref[0:HALO, :] = jnp.where(i > 0, pup_ref[...], 0.0)
        ext_ref[HALO:HALO + BLK, :] = pu_ref[...]
        for g, w in enumerate(POOL_WINDOWS):
            lanes = slice(g * 128, (g + 1) * 128)
            pooled, _ = _pool_block(ext_ref, i, g, w)
            mixed = jnp.dot(pooled.astype(BF16), pw_ref[g], preferred_element_type=F32)
            gate = pg_ref[:, lanes]
            cat_ref[:, lanes] = (mixed * sc_ref[:, lanes] * (gate * _sigmoid(gate))).astype(BF16)

        dist, valid = _mask_terms(i)
        k_var = _head_variants(kv_ref[:, 0:128], kvp_ref[:, 0:128])
        v_var = _head_variants(kv_ref[:, 128:256], kvp_ref[:, 128:256])
        for j in range(4):
            hkv = j // 2
            lanes = slice(j * 128, (j + 1) * 128)
            q_tile = q_ref[:, lanes]
            o = jnp.zeros((BLK, 128), F32)
            for half in range(2):
                head = hkv * 4 + 2 * (j % 2) + half
                p, _ = _probs(q_tile, k_var[hkv][half], 2.0 ** -(head + 1), sink_ref[head], dist, valid)
                o = o + jnp.dot(p.astype(BF16), v_var[hkv][half], preferred_element_type=F32)
            gate = ag_ref[:, lanes]
            cat_ref[:, D_POOL + j * 128:D_POOL + (j + 1) * 128] = (o * (gate * _sigmoid(gate))).astype(BF16)

    blk = lambda w: pl.BlockSpec((BLK, w), lambda i: (i, 0))
    prev = lambda w: pl.BlockSpec((BLK, w), lambda i: (jnp.maximum(i - 1, 0), 0))
    halo = pl.BlockSpec((HALO, 512), lambda i: (jnp.maximum(i * (BLK // HALO) - 1, 0), 0))
    return pl.pallas_call(
        body, name="fwd_mix", grid=(NB,),
        in_specs=[blk(512), halo, blk(512), blk(512), blk(256), prev(256), blk(512),
                  pl.BlockSpec(pool_w.shape, lambda i: (0, 0, 0)), pl.BlockSpec(pool_scale.shape, lambda i: (0, 0)),
                  pl.BlockSpec(memory_space=pltpu.SMEM)],
        out_specs=blk(D),
        out_shape=jax.ShapeDtypeStruct((S, D), BF16),
        scratch_shapes=[pltpu.VMEM((HALO + BLK, 512), F32)],
        compiler_params=_params(),
    )(pu, pu, pg, q, kv, kv, ag, pool_w, pool_scale, sinks)


def _fwd_out(cat, w_out, x, g_post, target=None):
    last = target is not None
    n_steps = S // TM

    def body(*refs):
        if last:
            cat_ref, w_ref, x_ref, g_ref, t_ref, y_ref, dx_ref, loss_ref, acc_ref = refs
        else:
            cat_ref, w_ref, x_ref, g_ref, y_ref, xn_ref = refs
        y = jnp.dot(cat_ref[...], w_ref[...], preferred_element_type=F32)
        y_ref[...] = y
        r = lax.rsqrt(jnp.mean(y * y, axis=-1, keepdims=True) + EPS)
        xn = x_ref[...] + y * r * g_ref[...]
        if not last:
            xn_ref[...] = xn
            return
        step = pl.program_id(0)
        err = xn - t_ref[...]
        dx_ref[...] = err * (1.0 / D)

        @pl.when(step == 0)
        def _():
            acc_ref[...] = jnp.zeros_like(acc_ref)

        acc_ref[...] += _rows8(err * err)

        @pl.when(step == n_steps - 1)
        def _():
            loss_ref[...] = jnp.full((1, 128), (0.5 / D) * jnp.sum(acc_ref[...]), F32)

    row = lambda: pl.BlockSpec((TM, D), lambda i: (i, 0))
    full = lambda a: pl.BlockSpec(a.shape, lambda i: (0, 0))
    act = jax.ShapeDtypeStruct((S, D), F32)
    in_specs = [row(), full(w_out), row(), full(g_post)]
    args = [cat, w_out, x, g_post]
    if last:
        return pl.pallas_call(
            body, name="fwd_out_loss", grid=(n_steps,),
            in_specs=in_specs + [row()],
            out_specs=[row(), row(), pl.BlockSpec((1, 128), lambda i: (0, 0))],
            out_shape=[act, act, jax.ShapeDtypeStruct((1, 128), F32)],
            scratch_shapes=[pltpu.VMEM((8, D), F32)],
            compiler_params=_params(),
        )(*args, target)
    return pl.pallas_call(
        body, name="fwd_out", grid=(n_steps,),
        in_specs=in_specs, out_specs=[row(), row()], out_shape=[act, act],
        compiler_params=_params(),
    )(*args)


def _bwd_out(dxn, y, g_post, cat, w_out):
    n_steps = S // TM

    def body(dz_ref, y_ref, g_ref, cat_ref, w_ref, dcat_ref, dw_ref, dg_ref, acc_ref):
        step = pl.program_id(0)

        @pl.when(step == 0)
        def _():
            dw_ref[...] = jnp.zeros_like(dw_ref)
            acc_ref[...] = jnp.zeros_like(acc_ref)

        y = y_ref[...]
        dz = dz_ref[...]
        r = lax.rsqrt(jnp.mean(y * y, axis=-1, keepdims=True) + EPS)
        a = dz * g_ref[...]
        dy = r * a - y * (r * r * r) * jnp.mean(a * y, axis=-1, keepdims=True)
        acc_ref[...] += _rows8(dz * (y * r))
        dyb = dy.astype(BF16)
        dcat_ref[...] = lax.dot_general(dyb, w_ref[...], NT, preferred_element_type=F32)
        dw_ref[...] += lax.dot_general(cat_ref[...], dyb, TN, preferred_element_type=F32)

        @pl.when(step == n_steps - 1)
        def _():
            dg_ref[...] = jnp.sum(acc_ref[...], axis=0, keepdims=True)

    row = lambda: pl.BlockSpec((TM, D), lambda i: (i, 0))
    full = lambda shape: pl.BlockSpec(shape, lambda i: (0, 0))
    return pl.pallas_call(
        body, name="bwd_out", grid=(n_steps,),
        in_specs=[row(), row(), full((1, D)), row(), full((D, D))],
        out_specs=[row(), full((D, D)), full((1, D))],
        out_shape=[jax.ShapeDtypeStruct((S, D), F32), jax.ShapeDtypeStruct((D, D), F32),
                   jax.ShapeDtypeStruct((1, D), F32)],
        scratch_shapes=[pltpu.VMEM((8, D), F32)],
        compiler_params=_params(),
    )(dxn, y, g_post, cat, w_out)


def _bwd_mix(pu, pg, q, kv, ag, dcat, pool_w, pool_scale, sinks):
    def body(pu_ref, pup_ref, pg_ref, q_ref, kv_ref, kvp_ref, ag_ref, dcat_ref, pw_ref, sc_ref, sink_ref,
             dproj_ref, dpw_ref, dsc_ref, dsink_ref, ext_ref, dext_ref, dkv_ref):
        step = pl.program_id(0)
        i = NB - 1 - step

        @pl.when(step == 0)
        def _():
            dpw_ref[...] = jnp.zeros_like(dpw_ref)
            dsc_ref[...] = jnp.zeros_like(dsc_ref)
            dsink_ref[...] = jnp.zeros_like(dsink_ref)
            dext_ref[BLK:BLK + HALO, :] = jnp.zeros((HALO, 512), F32)
            dkv_ref[...] = jnp.zeros_like(dkv_ref)

        ext_ref[0:HALO, :] = jnp.where(i > 0, pup_ref[...], 0.0)
        ext_ref[HALO:HALO + BLK, :] = pu_ref[...]
        for g, w in enumerate(POOL_WINDOWS):
            lanes = slice(g * 128, (g + 1) * 128)
            pooled, inv = _pool_block(ext_ref, i, g, w)
            pooled_b = pooled.astype(BF16)
            mixed = jnp.dot(pooled_b, pw_ref[g], preferred_element_type=F32)
            scale = sc_ref[:, lanes]
            gate = pg_ref[:, lanes]
            sg = _sigmoid(gate)
            dpo = dcat_ref[:, lanes]
            dproj_ref[:, C_PG + g * 128:C_PG + (g + 1) * 128] = (
                dpo * (mixed * scale) * (sg * (1.0 + gate * (1.0 - sg)))).astype(BF16)
            dms = dpo * (gate * sg)
            dsc_ref[:, lanes] += jnp.sum(dms * mixed, axis=0, keepdims=True)
            dmixed = (dms * scale).astype(BF16)
            dpw_ref[g] += lax.dot_general(pooled_b, dmixed, TN, preferred_element_type=F32)
            dpooled = lax.dot_general(dmixed, pw_ref[g], NT, preferred_element_type=F32)
            dext_ref[0:BLK, lanes] = dpooled * inv
            acc = dext_ref[0:BLK, lanes]
            for j in range(1, w):
                acc = acc + dext_ref[j:j + BLK, lanes]
            dproj_ref[:, C_PU + g * 128:C_PU + (g + 1) * 128] = (acc - dpooled).astype(BF16)
        dext_ref[BLK:BLK + HALO, :] = dext_ref[0:HALO, :]

        dist, valid = _mask_terms(i)
        k_var = _head_variants(kv_ref[:, 0:128], kvp_ref[:, 0:128])
        v_var = _head_variants(kv_ref[:, 128:256], kvp_ref[:, 128:256])
        zero = jnp.zeros((2 * BLK, 128), F32)
        dk_acc = [[zero, zero], [zero, zero]]
        dv_acc = [[zero, zero], [zero, zero]]
        for j in range(4):
            hkv = j // 2
            lanes = slice(j * 128, (j + 1) * 128)
            q_tile = q_ref[:, lanes]
            gate = ag_ref[:, lanes]
            sg = _sigmoid(gate)
            dca = dcat_ref[:, D_POOL + j * 128:D_POOL + (j + 1) * 128]
            do_b = (dca * (gate * sg)).astype(BF16)
            probs = []
            o = jnp.zeros((BLK, 128), F32)
            for half in range(2):
                head = hkv * 4 + 2 * (j % 2) + half
                p, p_sink = _probs(q_tile, k_var[hkv][half], 2.0 ** -(head + 1), sink_ref[head], dist, valid)
                probs.append((p, p_sink))
                o = o + jnp.dot(p.astype(BF16), v_var[hkv][half], preferred_element_type=F32)
            dproj_ref[:, C_AG + j * 128:C_AG + (j + 1) * 128] = (
                dca * o * (sg * (1.0 + gate * (1.0 - sg)))).astype(BF16)
            dq = jnp.zeros((BLK, 128), F32)
            for half in range(2):
                head = hkv * 4 + 2 * (j % 2) + half
                p, p_sink = probs[half]
                dp = lax.dot_general(do_b, v_var[hkv][half], NT, preferred_element_type=F32)
                delta = jnp.sum(p * dp, axis=-1, keepdims=True)
                ds_b = (p * (dp - delta) * 0.125).astype(BF16)
                dsink_ref[head:head + 1, :] += jnp.broadcast_to(
                    -jnp.sum(p_sink * delta, axis=0, keepdims=True), (1, 128))
                dq = dq + jnp.dot(ds_b, k_var[hkv][half], preferred_element_type=F32)
                dk_acc[hkv][half] = dk_acc[hkv][half] + lax.dot_general(
                    ds_b, q_tile, TN, preferred_element_type=F32)
                dv_acc[hkv][half] = dv_acc[hkv][half] + lax.dot_general(
                    p.astype(BF16), do_b, TN, preferred_element_type=F32)
            dproj_ref[:, C_Q + j * 128:C_Q + (j + 1) * 128] = dq.astype(BF16)

        low = lax.broadcasted_iota(jnp.int32, (2 * BLK, 128), 1) < 64

        def gather_heads(acc):
            return jnp.where(low, acc[0][0] + pltpu.roll(acc[0][1], 64, axis=1),
                             pltpu.roll(acc[1][0], 64, axis=1) + acc[1][1])

        dk = gather_heads(dk_acc)
        dv = gather_heads(dv_acc)
        dproj_ref[:, C_K:C_V] = (dk[BLK:, :] + dkv_ref[:, 0:128]).astype(BF16)
        dproj_ref[:, C_V:C_AG] = (dv[BLK:, :] + dkv_ref[:, 128:256]).astype(BF16)
        dkv_ref[:, 0:128] = dk[:BLK, :]
        dkv_ref[:, 128:256] = dv[:BLK, :]

    rev = lambda w: pl.BlockSpec((BLK, w), lambda s: (NB - 1 - s, 0))
    prev = lambda w: pl.BlockSpec((BLK, w), lambda s: (jnp.maximum(NB - 2 - s, 0), 0))
    halo = pl.BlockSpec((HALO, 512), lambda s: (jnp.maximum((NB - 1 - s) * (BLK // HALO) - 1, 0), 0))
    return pl.pallas_call(
        body, name="bwd_mix", grid=(NB,),
        in_specs=[rev(512), halo, rev(512), rev(512), rev(256), prev(256), rev(512), rev(D),
                  pl.BlockSpec(pool_w.shape, lambda s: (0, 0, 0)), pl.BlockSpec(pool_scale.shape, lambda s: (0, 0)),
                  pl.BlockSpec(memory_space=pltpu.SMEM)],
        out_specs=[rev(D_IN), pl.BlockSpec((4, 128, 128), lambda s: (0, 0, 0)),
                   pl.BlockSpec((1, 512), lambda s: (0, 0)), pl.BlockSpec((8, 128), lambda s: (0, 0))],
        out_shape=[jax.ShapeDtypeStruct((S, D_IN), BF16), jax.ShapeDtypeStruct((4, 128, 128), F32),
                   jax.ShapeDtypeStruct((1, 512), F32), jax.ShapeDtypeStruct((8, 128), F32)],
        scratch_shapes=[pltpu.VMEM((HALO + BLK, 512), F32), pltpu.VMEM((BLK + HALO, 512), F32),
                        pltpu.VMEM((BLK, 256), F32)],
        compiler_params=_params(),
    )(pu, pu, pg, q, kv, kv, ag, dcat, pool_w, pool_scale, sinks)


def _bwd_in(dproj, w_in, x, g_pre, dres):
    n_steps = S // TM

    def body(dp_ref, w_ref, x_ref, g_ref, dres_ref, dx_ref, dw_ref, dg_ref, acc_ref):
        step = pl.program_id(0)

        @pl.when(step == 0)
        def _():
            dw_ref[...] = jnp.zeros_like(dw_ref)
            acc_ref[...] = jnp.zeros_like(acc_ref)

        dp = dp_ref[...]
        dh = lax.dot_general(dp, w_ref[...], NT, preferred_element_type=F32)
        xt = x_ref[...]
        r = lax.rsqrt(jnp.mean(xt * xt, axis=-1, keepdims=True) + EPS)
        xn = xt * r
        g = g_ref[...]
        acc_ref[...] += _rows8(dh * xn)
        a = dh * g
        dx_ref[...] = dres_ref[...] + (r * a - xt * (r * r * r) * jnp.mean(a * xt, axis=-1, keepdims=True))
        dw_ref[...] += lax.dot_general((xn * g).astype(BF16), dp, TN, preferred_element_type=F32)

        @pl.when(step == n_steps - 1)
        def _():
            dg_ref[...] = jnp.sum(acc_ref[...], axis=0, keepdims=True)

    row = lambda w: pl.BlockSpec((TM, w), lambda i: (i, 0))
    full = lambda shape: pl.BlockSpec(shape, lambda i: (0, 0))
    return pl.pallas_call(
        body, name="bwd_in", grid=(n_steps,),
        in_specs=[row(D_IN), full((D, D_IN)), row(D), full((1, D)), row(D)],
        out_specs=[row(D), full((D, D_IN)), full((1, D))],
        out_shape=[jax.ShapeDtypeStruct((S, D), F32), jax.ShapeDtypeStruct((D, D_IN), F32),
                   jax.ShapeDtypeStruct((1, D), F32)],
        scratch_shapes=[pltpu.VMEM((8, D), F32)],
        compiler_params=_params(),
    )(dproj, w_in, x, g_pre, dres)


def _local_step(x, target, w_in, w_out, pool_w, pool_scale, sinks, norm_pre, norm_post):
    pool_w_b = pool_w.astype(BF16)
    saved = []
    xs = x
    for l in range(DEPTH):
        g_pre, g_post, scale = norm_pre[l][None, :], norm_post[l][None, :], pool_scale[l][None, :]
        pu, pg, q, kv, ag = _fwd_in(xs, g_pre, w_in[l])
        cat = _fwd_mix(pu, pg, q, kv, ag, pool_w_b[l], scale, sinks[l])
        if l < DEPTH - 1:
            y, x_next = _fwd_out(cat, w_out[l], xs, g_post)
        else:
            y, x_next, loss = _fwd_out(cat, w_out[l], xs, g_post, target)
        saved.append((xs, pu, pg, q, kv, ag, cat, y))
        xs = x_next

    dx = xs
    grads = [None] * DEPTH
    for l in reversed(range(DEPTH)):
        g_pre, g_post, scale = norm_pre[l][None, :], norm_post[l][None, :], pool_scale[l][None, :]
        x_in, pu, pg, q, kv, ag, cat, y = saved[l]
        dcat, dw_out, dg_post = _bwd_out(dx, y, g_post, cat, w_out[l])
        dproj, dpw, dsc, dsink = _bwd_mix(pu, pg, q, kv, ag, dcat, pool_w_b[l], scale, sinks[l])
        dx, dw_in, dg_pre = _bwd_in(dproj, w_in[l], x_in, g_pre, dx)
        grads[l] = (dw_in, dw_out, dpw, dsc, dsink, dg_pre, dg_post)
    return loss, dx, grads


def _place():
    return lax.axis_index("x"), lax.axis_index("y"), lax.axis_index("c")


def _other_chips(x, y):
    return [(1 - x, y), (x, 1 - y), (1 - x, 1 - y)]


def _gather_weights(w_in_b, w_out_b):
    def body(wi_ref, wo_ref, gi_ref, go_ref, send_sems, recv_sems, local_sems):
        x, y, c = _place()
        mine = 2 * x + y
        sibling = (x, y, 1 - c)
        chips = _other_chips(x, y)
        pairs = [(wi_ref, gi_ref), (wo_ref, go_ref)]

        local = [pltpu.make_async_copy(src, dst.at[mine], local_sems.at[a]) for a, (src, dst) in enumerate(pairs)]
        for cp in local:
            cp.start()

        def remote(k, src, dst, to):
            return pltpu.make_async_remote_copy(src_ref=src, dst_ref=dst, send_sem=send_sems.at[k],
                                                recv_sem=recv_sems.at[k], device_id=to, device_id_type=MESH)

        sends = []
        for a, (src, dst) in enumerate(pairs):
            for j, chip in enumerate(chips):
                sends.append(remote(6 * a + j, src.at[c], dst.at[mine, c], (*chip, c)))
        for cp in sends:
            cp.start()
        for a, (src, dst) in enumerate(pairs):
            for j, chip in enumerate(chips):
                theirs = dst.at[2 * chip[0] + chip[1], c]
                remote(6 * a + j, theirs, theirs, (x, y, c)).wait_recv()
                fwd = remote(6 * a + 3 + j, theirs, theirs, sibling)
                fwd.start()
                sends.append(fwd)
        for a, (src, dst) in enumerate(pairs):
            for j, chip in enumerate(chips):
                theirs = dst.at[2 * chip[0] + chip[1], 1 - c]
                remote(6 * a + 3 + j, theirs, theirs, (x, y, c)).wait_recv()
        for cp in sends:
            cp.wait_send()
        for cp in local:
            cp.wait()

    return pl.pallas_call(
        body, name="gather_weights",
        in_specs=[ANY, ANY], out_specs=[ANY, ANY],
        out_shape=[jax.ShapeDtypeStruct((N_SHARDS,) + w_in_b.shape, BF16),
                   jax.ShapeDtypeStruct((N_SHARDS,) + w_out_b.shape, BF16)],
        scratch_shapes=[pltpu.SemaphoreType.DMA((12,)), pltpu.SemaphoreType.DMA((12,)), pltpu.SemaphoreType.DMA((2,))],
    )(w_in_b, w_out_b)


def _sibling_exchange(name, arrays, pick):
    n = len(arrays)

    def body(*refs):
        srcs, dsts, send_sems, recv_sems = refs[:n], refs[n:2 * n], refs[2 * n], refs[2 * n + 1]
        x, y, c = _place()
        copies = [pltpu.make_async_remote_copy(src_ref=src.at[pick(c)], dst_ref=dst, send_sem=send_sems.at[k],
                                               recv_sem=recv_sems.at[k], device_id=(x, y, 1 - c), device_id_type=MESH)
                  for k, (src, dst) in enumerate(zip(srcs, dsts))]
        for cp in copies:
            cp.start()
        for cp in copies:
            cp.wait()

    return pl.pallas_call(
        body, name=name, in_specs=[ANY] * n, out_specs=[ANY] * n,
        out_shape=[jax.ShapeDtypeStruct(a.shape[1:], a.dtype) for a in arrays],
        scratch_shapes=[pltpu.SemaphoreType.DMA((n,)), pltpu.SemaphoreType.DMA((n,))],
    )(*arrays)


def _add_layer(name, stacked, other, c_arr):
    _, rows, cols = stacked.shape
    tr = 256

    def body(c_ref, a_ref, b_ref, o_ref):
        o_ref[...] = a_ref[0] + b_ref[...]

    return pl.pallas_call(
        body, name=name,
        grid_spec=pltpu.PrefetchScalarGridSpec(
            num_scalar_prefetch=1, grid=(rows // tr,),
            in_specs=[pl.BlockSpec((1, tr, cols), lambda i, c_ref: (c_ref[0], i, 0)),
                      pl.BlockSpec((tr, cols), lambda i, c_ref: (i, 0))],
            out_specs=pl.BlockSpec((tr, cols), lambda i, c_ref: (i, 0))),
        out_shape=jax.ShapeDtypeStruct((rows, cols), F32),
        compiler_params=_params(),
    )(c_arr, stacked, other)


def _scatter_pieces(pieces_in, pieces_out):
    def body(pi_ref, po_ref, ri_ref, ro_ref, send_sems, recv_sems):
        x, y, c = _place()
        copies = []
        for a, (src, dst) in enumerate([(pi_ref, ri_ref), (po_ref, ro_ref)]):
            for j, chip in enumerate(_other_chips(x, y)):
                copies.append(pltpu.make_async_remote_copy(
                    src_ref=src.at[2 * chip[0] + chip[1]], dst_ref=dst.at[j], send_sem=send_sems.at[3 * a + j],
                    recv_sem=recv_sems.at[3 * a + j], device_id=(*chip, c), device_id_type=MESH))
        for cp in copies:
            cp.start()
        for cp in copies:
            cp.wait()

    return pl.pallas_call(
        body, name="scatter_pieces", in_specs=[ANY, ANY], out_specs=[ANY, ANY],
        out_shape=[jax.ShapeDtypeStruct((3,) + pieces_in.shape[1:], BF16),
                   jax.ShapeDtypeStruct((3,) + pieces_out.shape[1:], BF16)],
        scratch_shapes=[pltpu.SemaphoreType.DMA((6,)), pltpu.SemaphoreType.DMA((6,))],
    )(pieces_in, pieces_out)


def _sum_pieces(name, own, recv):
    rows, cols = own.shape
    tr = 256

    def body(o_ref, r_ref, out_ref):
        out_ref[...] = ((o_ref[...] + r_ref[0].astype(F32)) + r_ref[1].astype(F32)) + r_ref[2].astype(F32)

    return pl.pallas_call(
        body, name=name, grid=(rows // tr,),
        in_specs=[pl.BlockSpec((tr, cols), lambda i: (i, 0)), pl.BlockSpec((3, tr, cols), lambda i: (0, i, 0))],
        out_specs=pl.BlockSpec((tr, cols), lambda i: (i, 0)),
        out_shape=jax.ShapeDtypeStruct((rows, cols), F32),
        compiler_params=_params(),
    )(own, recv)


def _swap_layers(red_in, red_out):
    arrays = [red_in, red_out]

    def body(a_ref, b_ref, oa_ref, ob_ref, send_sems, recv_sems, local_sems):
        x, y, c = _place()
        copies = []
        local = []
        for k, (src, dst) in enumerate([(a_ref, oa_ref), (b_ref, ob_ref)]):
            local.append(pltpu.make_async_copy(src, dst.at[c], local_sems.at[k]))
            copies.append(pltpu.make_async_remote_copy(src_ref=src, dst_ref=dst.at[c], send_sem=send_sems.at[k],
                                                       recv_sem=recv_sems.at[k], device_id=(x, y, 1 - c),
                                                       device_id_type=MESH))
        for cp in local + copies:
            cp.start()
        for cp in copies:
            cp.wait_send()
        for k, (src, dst) in enumerate([(a_ref, oa_ref), (b_ref, ob_ref)]):
            pltpu.make_async_remote_copy(src_ref=src, dst_ref=dst.at[1 - c], send_sem=send_sems.at[k],
                                         recv_sem=recv_sems.at[k], device_id=(x, y, 1 - c),
                                         device_id_type=MESH).wait_recv()
        for cp in local:
            cp.wait()

    return pl.pallas_call(
        body, name="swap_layers", in_specs=[ANY, ANY], out_specs=[ANY, ANY],
        out_shape=[jax.ShapeDtypeStruct((DEPTH,) + a.shape, F32) for a in arrays],
        scratch_shapes=[pltpu.SemaphoreType.DMA((2,)), pltpu.SemaphoreType.DMA((2,)), pltpu.SemaphoreType.DMA((2,))],
    )(*arrays)


def _allreduce_small(packed):
    def body(p_ref, out_ref, recv_ref, send1, recv1, send2, recv2):
        x, y, c = _place()
        me = 4 * x + 2 * y + c

        def peer(m):
            return (x ^ (m >> 2), y ^ ((m >> 1) & 1), c ^ (m & 1))

        def index(p):
            return 4 * p[0] + 2 * p[1] + p[2]

        def copy(src, dst, sems, m, to):
            return pltpu.make_async_remote_copy(src_ref=src, dst_ref=dst, send_sem=sems[0].at[m - 1],
                                                recv_sem=sems[1].at[m - 1], device_id=to, device_id_type=MESH)

        first = [copy(p_ref.at[index(peer(m))], recv_ref.at[me], (send1, recv1), m, peer(m)) for m in range(1, 8)]
        for cp in first:
            cp.start()
        recv_ref[me] = p_ref[me]
        for m in range(1, 8):
            slot = recv_ref.at[index(peer(m))]
            copy(slot, slot, (send1, recv1), m, peer(m)).wait_recv()
        total = recv_ref[0]
        for d in range(1, 8):
            total = total + recv_ref[d]
        out_ref[me] = total
        second = [copy(out_ref.at[me], out_ref.at[me], (send2, recv2), m, peer(m)) for m in range(1, 8)]
        for cp in second:
            cp.start()
        for m in range(1, 8):
            slot = out_ref.at[index(peer(m))]
            copy(slot, slot, (send2, recv2), m, peer(m)).wait_recv()
        for cp in first + second:
            cp.wait_send()

    vmem = pl.BlockSpec(memory_space=pltpu.VMEM)
    return pl.pallas_call(
        body, name="allreduce_small", in_specs=[vmem], out_specs=vmem,
        out_shape=jax.ShapeDtypeStruct(packed.shape, F32),
        scratch_shapes=[pltpu.VMEM(packed.shape, F32)] + [pltpu.SemaphoreType.DMA((7,))] * 4,
    )(packed)


def _adamw(name, w, g, m, v, rows_per_step):
    layers, rows, cols = w.shape

    def body(w_ref, g_ref, m_ref, v_ref, d_ref, nm_ref, nv_ref):
        gt = g_ref[...]
        nm = ADAM_B1 * m_ref[...] + (1.0 - ADAM_B1) * gt
        nv = ADAM_B2 * v_ref[...] + (1.0 - ADAM_B2) * (gt * gt)
        m_hat = nm / (1.0 - ADAM_B1 ** ADAM_STEP)
        v_hat = nv / (1.0 - ADAM_B2 ** ADAM_STEP)
        d_ref[...] = -ADAM_LR * (m_hat / (jnp.sqrt(v_hat) + ADAM_EPS) + ADAM_WD * w_ref[...])
        nm_ref[...] = nm
        nv_ref[...] = nv

    spec = pl.BlockSpec((1, rows_per_step, cols), lambda l, i: (l, i, 0))
    shape = jax.ShapeDtypeStruct(w.shape, F32)
    return pl.pallas_call(
        body, name=name, grid=(layers, rows // rows_per_step),
        in_specs=[spec] * 4, out_specs=[spec] * 3, out_shape=[shape] * 3,
        compiler_params=_params(("arbitrary", "arbitrary")),
    )(w, g, m, v)


def _pack_small(pool_w, pool_scale, sinks, norm_pre, norm_post):
    sink_rows = jnp.zeros((ROWS_SINK, 128), F32).at[0, 0:2 * N_HEADS].set(sinks.reshape(-1))
    used = ROWS_PW + ROWS_SC + 2 * ROWS_NORM + ROWS_SINK
    return jnp.concatenate([
        pool_w.reshape(ROWS_PW, 128), pool_scale.reshape(ROWS_SC, 128), norm_pre.reshape(ROWS_NORM, 128),
        norm_post.reshape(ROWS_NORM, 128), sink_rows, jnp.zeros((SMALL_ROWS - used, 128), F32)], axis=0)


def _unpack_small(packed):
    o = 0
    pool_w = packed[o:o + ROWS_PW].reshape(DEPTH, 4, 128, 128)
    o += ROWS_PW
    pool_scale = packed[o:o + ROWS_SC].reshape(DEPTH, 512)
    o += ROWS_SC
    norm_pre = packed[o:o + ROWS_NORM].reshape(DEPTH, D)
    o += ROWS_NORM
    norm_post = packed[o:o + ROWS_NORM].reshape(DEPTH, D)
    o += ROWS_NORM
    sinks = packed[o, 0:2 * N_HEADS].reshape(DEPTH, N_HEADS)
    return pool_w, pool_scale, sinks, norm_pre, norm_post


def kernel(x, w_in, pool_w, pool_scale, attn_sinks, w_out, norm_pre, norm_post, loss_target, m_w_in, m_pool_w, m_pool_scale, m_attn_sinks, m_w_out, m_norm_pre, m_norm_post, v_w_in, v_pool_w, v_pool_scale, v_attn_sinks, v_w_out, v_norm_pre, v_norm_post):
    cx, cy, cc = _place()
    mine = 2 * cx + cy
    c_arr = jnp.reshape(cc, (1,)).astype(jnp.int32)

    g_in, g_out = _gather_weights(w_in.astype(BF16), w_out.astype(BF16))
    w_in_full = jnp.transpose(g_in, (1, 2, 0, 3)).reshape(DEPTH, D, D_IN)
    w_out_full = jnp.transpose(g_out, (1, 0, 2, 3)).reshape(DEPTH, D, D)

    loss, grad_x, grads = _local_step(x[0], loss_target[0], w_in_full, w_out_full, pool_w, pool_scale, attn_sinks,
                                      norm_pre, norm_post)
    loss = lax.psum(loss[0, 0], ("x", "y", "c"))

    dw_in = jnp.stack([g[0] for g in grads])
    dw_out = jnp.stack([g[1] for g in grads])
    from_sib_in, from_sib_out = _sibling_exchange("sibling_partials", [dw_in, dw_out], lambda c: 1 - c)
    chip_in = _add_layer("chip_sum_in", dw_in, from_sib_in, c_arr)
    chip_out = _add_layer("chip_sum_out", dw_out, from_sib_out, c_arr)
    pieces_in = jnp.transpose(chip_in.reshape(D, N_SHARDS, W_IN_SHARD), (1, 0, 2))
    pieces_out = chip_out.reshape(N_SHARDS, W_OUT_SHARD, D)
    recv_in, recv_out = _scatter_pieces(pieces_in.astype(BF16), pieces_out.astype(BF16))
    red_in = _sum_pieces("sum_pieces_in", lax.dynamic_index_in_dim(pieces_in, mine, 0, keepdims=False), recv_in)
    red_out = _sum_pieces("sum_pieces_out", lax.dynamic_index_in_dim(pieces_out, mine, 0, keepdims=False), recv_out)
    grad_w_in, grad_w_out = _swap_layers(red_in, red_out)

    dpw = jnp.stack([g[2] for g in grads])
    dsc = jnp.stack([g[3][0] for g in grads])
    dsink = jnp.stack([g[4][:, 0] for g in grads])
    dpre = jnp.stack([g[5][0] for g in grads])
    dpost = jnp.stack([g[6][0] for g in grads])
    packed = _pack_small(dpw, dsc, dsink, dpre, dpost).reshape(8, PIECE_ROWS, 128)
    g_small = _allreduce_small(packed).reshape(1, SMALL_ROWS, 128)

    d_in, nm_in, nv_in = _adamw("adamw_w_in", w_in, grad_w_in, m_w_in, v_w_in, 256)
    d_out, nm_out, nv_out = _adamw("adamw_w_out", w_out, grad_w_out, m_w_out, v_w_out, 256)
    small = lambda *a: _pack_small(*a).reshape(1, SMALL_ROWS, 128)
    d_s, nm_s, nv_s = _adamw(
        "adamw_small", small(pool_w, pool_scale, attn_sinks, norm_pre, norm_post), g_small,
        small(m_pool_w, m_pool_scale, m_attn_sinks, m_norm_pre, m_norm_post),
        small(v_pool_w, v_pool_scale, v_attn_sinks, v_norm_pre, v_norm_post), SMALL_ROWS)

    g_pw, g_sc, g_sk, g_pre, g_post = _unpack_small(g_small[0])
    d_pw, d_sc, d_sk, d_pre, d_post = _unpack_small(d_s[0])
    m_pw, m_sc, m_sk, m_pre, m_post = _unpack_small(nm_s[0])
    v_pw, v_sc, v_sk, v_pre, v_post = _unpack_small(nv_s[0])
    return (loss, grad_x[None], grad_w_in, g_pw, g_sc, g_sk, grad_w_out, g_pre, g_post,
            d_in, d_pw, d_sc, d_sk, d_out, d_pre, d_post,
            nm_in, m_pw, m_sc, m_sk, nm_out, m_pre, m_post,
            nv_in, v_pw, v_sc, v_sk, nv_out, v_pre, v_post)
```

```python
import jax
import jax.numpy as jnp
from jax import lax
from jax.experimental import pallas as pl
from jax.experimental.pallas import tpu as pltpu

F32 = jnp.float32
BF16 = jnp.bfloat16

S = 2048
D = 1024
DEPTH = 2
D_POOL = 512
POOL_WINDOWS = (2, 4, 8, 16)
N_HEADS = 8
D_IN = 2304
N_SHARDS = 4
W_IN_SHARD = D_IN // N_SHARDS
W_OUT_SHARD = D // N_SHARDS
BLK = 128
NB = S // BLK
HALO = 16
EPS = 1e-6
NEG_INF = -1e30
C_PU, C_PG, C_Q, C_K, C_V, C_AG = 0, 512, 1024, 1536, 1664, 1792

ADAM_LR = 0.001
ADAM_B1 = 0.9
ADAM_B2 = 0.999
ADAM_EPS = 1e-08
ADAM_WD = 0.01
ADAM_STEP = 10

TM = 256
VMEM_LIMIT = 56 * 1024 * 1024

NT = (((1,), (1,)), ((), ()))
TN = (((0,), (0,)), ((), ()))

MESH = pl.DeviceIdType.MESH
ANY = pl.BlockSpec(memory_space=pl.ANY)

ROWS_PW, ROWS_SC, ROWS_NORM, ROWS_SINK = 1024, 8, 16, 8
ROW_LOSS = ROWS_PW + ROWS_SC + 2 * ROWS_NORM + ROWS_SINK
SMALL_ROWS = 1088
PIECE_ROWS = SMALL_ROWS // 8


def _params(sem=("arbitrary",)):
    return pltpu.CompilerParams(dimension_semantics=sem, vmem_limit_bytes=VMEM_LIMIT)


def _sigmoid(v):
    return 1.0 / (1.0 + jnp.exp(-v))


def _rows8(v):
    r, c = v.shape
    return v.reshape(r // 8, 8, c).sum(axis=0)


def _layer(l, *shape):
    zeros = (0,) * len(shape)
    return pl.BlockSpec((None,) + shape, lambda i: (l,) + zeros)


def _fwd_in(l, x, g_pre, w_in_t):
    def body(x_ref, g_ref, w_ref, pu_ref, pg_ref, q_ref, kv_ref, ag_ref):
        xt = x_ref[...]
        r = lax.rsqrt(jnp.mean(xt * xt, axis=-1, keepdims=True) + EPS)
        h = (xt * r * g_ref[...]).astype(BF16)

        def proj(lo, hi):
            return lax.dot_general(h, w_ref[lo:hi, :], NT, preferred_element_type=F32)

        pu_ref[...] = proj(C_PU, C_PG)
        pg_ref[...] = proj(C_PG, C_Q)
        q_ref[...] = proj(C_Q, C_K).astype(BF16)
        kv_ref[...] = proj(C_K, C_AG).astype(BF16)
        ag_ref[...] = proj(C_AG, D_IN)

    row = lambda w: pl.BlockSpec((TM, w), lambda i: (i, 0))
    return pl.pallas_call(
        body, name="fwd_in", grid=(S // TM,),
        in_specs=[row(D), _layer(l, 1, D), _layer(l, D_IN, D)],
        out_specs=[row(512), row(512), row(512), row(256), row(512)],
        out_shape=[jax.ShapeDtypeStruct((S, 512), F32), jax.ShapeDtypeStruct((S, 512), F32),
                   jax.ShapeDtypeStruct((S, 512), BF16), jax.ShapeDtypeStruct((S, 256), BF16),
                   jax.ShapeDtypeStruct((S, 512), F32)],
        compiler_params=_params(),
    )(x, g_pre, w_in_t)


def _mask_terms(i):
    qi = lax.broadcasted_iota(jnp.int32, (BLK, 2 * BLK), 0)
    kj = lax.broadcasted_iota(jnp.int32, (BLK, 2 * BLK), 1)
    dist = qi + BLK - kj
    valid = (dist >= 0) & (dist < BLK) & ((kj >= BLK) | (i > 0))
    return dist.astype(F32), valid


def _head_variants(cur, prev):
    both = jnp.concatenate([prev, cur], axis=0).astype(F32)
    swapped = pltpu.roll(both, 64, axis=1)
    low = lax.broadcasted_iota(jnp.int32, both.shape, 1) < 64
    zero = jnp.zeros_like(both)
    return ((jnp.where(low, both, zero).astype(BF16), jnp.where(low, zero, swapped).astype(BF16)),
            (jnp.where(low, swapped, zero).astype(BF16), jnp.where(low, zero, both).astype(BF16)))


def _probs(q_tile, k_var, slope, sink, dist, valid):
    s = lax.dot_general(q_tile, k_var, NT, preferred_element_type=F32)
    s = jnp.where(valid, s * 0.125 - slope * dist, NEG_INF)
    m = jnp.maximum(jnp.max(s, axis=-1, keepdims=True), sink)
    p = jnp.exp(s - m)
    e_sink = jnp.exp(sink - m)
    inv = 1.0 / (jnp.sum(p, axis=-1, keepdims=True) + e_sink)
    return p * inv, e_sink * inv


def _pool_block(ext_ref, i, g, w):
    lanes = slice(g * 128, (g + 1) * 128)
    u = ext_ref[HALO:HALO + BLK, lanes]
    acc = u
    for j in range(1, w):
        acc = acc + ext_ref[HALO - j:HALO - j + BLK, lanes]
    t = (i * BLK + lax.broadcasted_iota(jnp.int32, (BLK, 1), 0)).astype(F32)
    inv = 1.0 / jnp.minimum(t + 1.0, float(w))
    return acc * inv - u, inv


def _fwd_mix(l, pu, pg, q, kv, ag, pool_w, pool_scale, sinks):
    def body(pu_ref, pup_ref, pg_ref, q_ref, kv_ref, kvp_ref, ag_ref, pw_ref, sc_ref, sink_ref, cat_ref, ext_ref):
        i = pl.program_id(0)
        ext_ref[0:HALO, :] = jnp.where(i > 0, pup_ref[...], 0.0)
        ext_ref[HALO:HALO + BLK, :] = pu_ref[...]
        for g, w in enumerate(POOL_WINDOWS):
            lanes = slice(g * 128, (g + 1) * 128)
            pooled, _ = _pool_block(ext_ref, i, g, w)
            mixed = jnp.dot(pooled.astype(BF16), pw_ref[g], preferred_element_type=F32)
            gate = pg_ref[:, lanes]
            cat_ref[:, lanes] = (mixed * sc_ref[:, lanes] * (gate * _sigmoid(gate))).astype(BF16)

        dist, valid = _mask_terms(i)
        k_var = _head_variants(kv_ref[:, 0:128], kvp_ref[:, 0:128])
        v_var = _head_variants(kv_ref[:, 128:256], kvp_ref[:, 128:256])
        for j in range(4):
            hkv = j // 2
            lanes = slice(j * 128, (j + 1) * 128)
            q_tile = q_ref[:, lanes]
            o = jnp.zeros((BLK, 128), F32)
            for half in range(2):
                head = hkv * 4 + 2 * (j % 2) + half
                p, _ = _probs(q_tile, k_var[hkv][half], 2.0 ** -(head + 1), sink_ref[l, head], dist, valid)
                o = o + jnp.dot(p.astype(BF16), v_var[hkv][half], preferred_element_type=F32)
            gate = ag_ref[:, lanes]
            cat_ref[:, D_POOL + j * 128:D_POOL + (j + 1) * 128] = (o * (gate * _sigmoid(gate))).astype(BF16)

    blk = lambda w: pl.BlockSpec((BLK, w), lambda i: (i, 0))
    prev = lambda w: pl.BlockSpec((BLK, w), lambda i: (jnp.maximum(i - 1, 0), 0))
    halo = pl.BlockSpec((HALO, 512), lambda i: (jnp.maximum(i * (BLK // HALO) - 1, 0), 0))
    return pl.pallas_call(
        body, name="fwd_mix", grid=(NB,),
        in_specs=[blk(512), halo, blk(512), blk(512), blk(256), prev(256), blk(512),
                  _layer(l, 4, 128, 128), _layer(l, 1, 512), pl.BlockSpec(memory_space=pltpu.SMEM)],
        out_specs=blk(D),
        out_shape=jax.ShapeDtypeStruct((S, D), BF16),
        scratch_shapes=[pltpu.VMEM((HALO + BLK, 512), F32)],
        compiler_params=_params(),
    )(pu, pu, pg, q, kv, kv, ag, pool_w, pool_scale, sinks)


def _fwd_out(l, cat, w_out, x, g_post, target=None):
    last = target is not None
    n_steps = S // TM

    def body(*refs):
        if last:
            cat_ref, w_ref, x_ref, g_ref, t_ref, y_ref, dx_ref, loss_ref, acc_ref = refs
        else:
            cat_ref, w_ref, x_ref, g_ref, y_ref, xn_ref = refs
        y = jnp.dot(cat_ref[...], w_ref[...], preferred_element_type=F32)
        y_ref[...] = y
        r = lax.rsqrt(jnp.mean(y * y, axis=-1, keepdims=True) + EPS)
        xn = x_ref[...] + y * r * g_ref[...]
        if not last:
            xn_ref[...] = xn
            return
        step = pl.program_id(0)
        err = xn - t_ref[...]
        dx_ref[...] = err * (1.0 / D)

        @pl.when(step == 0)
        def _():
            acc_ref[...] = jnp.zeros_like(acc_ref)

        acc_ref[...] += _rows8(err * err)

        @pl.when(step == n_steps - 1)
        def _():
            loss_ref[...] = jnp.full((8, 128), (0.5 / D) * jnp.sum(acc_ref[...]), F32)

    row = lambda: pl.BlockSpec((TM, D), lambda i: (i, 0))
    act = jax.ShapeDtypeStruct((S, D), F32)
    in_specs = [row(), _layer(l, D, D), row(), _layer(l, 1, D)]
    args = [cat, w_out, x, g_post]
    if last:
        return pl.pallas_call(
            body, name="fwd_out_loss", grid=(n_steps,),
            in_specs=in_specs + [row()],
            out_specs=[row(), row(), pl.BlockSpec((8, 128), lambda i: (0, 0))],
            out_shape=[act, act, jax.ShapeDtypeStruct((8, 128), F32)],
            scratch_shapes=[pltpu.VMEM((8, D), F32)],
            compiler_params=_params(),
        )(*args, target)
    return pl.pallas_call(
        body, name="fwd_out", grid=(n_steps,),
        in_specs=in_specs, out_specs=[row(), row()], out_shape=[act, act],
        compiler_params=_params(),
    )(*args)


def _store_lane_rows(ref, acc):
    total = jnp.sum(acc, axis=0, keepdims=True)
    for k in range(ref.shape[0]):
        ref[k:k + 1, :] = total[:, k * 128:(k + 1) * 128]


def _bwd_out(l, dxn, y, g_post, cat, w_out):
    n_steps = S // TM

    def body(dz_ref, y_ref, g_ref, cat_ref, w_ref, dcat_ref, dw_ref, dg_ref, acc_ref):
        step = pl.program_id(0)

        @pl.when(step == 0)
        def _():
            dw_ref[...] = jnp.zeros_like(dw_ref)
            acc_ref[...] = jnp.zeros_like(acc_ref)

        y = y_ref[...]
        dz = dz_ref[...]
        r = lax.rsqrt(jnp.mean(y * y, axis=-1, keepdims=True) + EPS)
        a = dz * g_ref[...]
        dy = r * a - y * (r * r * r) * jnp.mean(a * y, axis=-1, keepdims=True)
        acc_ref[...] += _rows8(dz * (y * r))
        dyb = dy.astype(BF16)
        dcat_ref[...] = lax.dot_general(dyb, w_ref[...], NT, preferred_element_type=F32)
        dw_ref[...] += lax.dot_general(cat_ref[...], dyb, TN, preferred_element_type=F32)

        @pl.when(step == n_steps - 1)
        def _():
            _store_lane_rows(dg_ref, acc_ref[...])

    row = lambda: pl.BlockSpec((TM, D), lambda i: (i, 0))
    full = lambda shape: pl.BlockSpec(shape, lambda i: (0, 0))
    return pl.pallas_call(
        body, name="bwd_out", grid=(n_steps,),
        in_specs=[row(), row(), _layer(l, 1, D), row(), _layer(l, D, D)],
        out_specs=[row(), full((D, D)), full((8, 128))],
        out_shape=[jax.ShapeDtypeStruct((S, D), F32), jax.ShapeDtypeStruct((D, D), F32),
                   jax.ShapeDtypeStruct((8, 128), F32)],
        scratch_shapes=[pltpu.VMEM((8, D), F32)],
        compiler_params=_params(),
    )(dxn, y, g_post, cat, w_out)


def _bwd_mix(l, pu, pg, q, kv, ag, dcat, pool_w, pool_scale, sinks):
    def body(pu_ref, pup_ref, pg_ref, q_ref, kv_ref, kvp_ref, ag_ref, dcat_ref, pw_ref, sc_ref, sink_ref,
             dproj_ref, dpw_ref, dsc_ref, dsink_ref, ext_ref, dext_ref, dkv_ref):
        step = pl.program_id(0)
        i = NB - 1 - step

        @pl.when(step == 0)
        def _():
            dpw_ref[...] = jnp.zeros_like(dpw_ref)
            dsc_ref[...] = jnp.zeros_like(dsc_ref)
            dsink_ref[...] = jnp.zeros_like(dsink_ref)
            dext_ref[BLK:BLK + HALO, :] = jnp.zeros((HALO, 512), F32)
            dkv_ref[...] = jnp.zeros_like(dkv_ref)

        ext_ref[0:HALO, :] = jnp.where(i > 0, pup_ref[...], 0.0)
        ext_ref[HALO:HALO + BLK, :] = pu_ref[...]
        for g, w in enumerate(POOL_WINDOWS):
            lanes = slice(g * 128, (g + 1) * 128)
            pooled, inv = _pool_block(ext_ref, i, g, w)
            pooled_b = pooled.astype(BF16)
            mixed = jnp.dot(pooled_b, pw_ref[g], preferred_element_type=F32)
            scale = sc_ref[:, lanes]
            gate = pg_ref[:, lanes]
            sg = _sigmoid(gate)
            dpo = dcat_ref[:, lanes]
            dproj_ref[:, C_PG + g * 128:C_PG + (g + 1) * 128] = (
                dpo * (mixed * scale) * (sg * (1.0 + gate * (1.0 - sg)))).astype(BF16)
            dms = dpo * (gate * sg)
            dsc_ref[g:g + 1, :] += jnp.sum(dms * mixed, axis=0, keepdims=True)
            dmixed = (dms * scale).astype(BF16)
            dpw_ref[g] += lax.dot_general(pooled_b, dmixed, TN, preferred_element_type=F32)
            dpooled = lax.dot_general(dmixed, pw_ref[g], NT, preferred_element_type=F32)
            dext_ref[0:BLK, lanes] = dpooled * inv
            acc = dext_ref[0:BLK, lanes]
            for j in range(1, w):
                acc = acc + dext_ref[j:j + BLK, lanes]
            dproj_ref[:, C_PU + g * 128:C_PU + (g + 1) * 128] = (acc - dpooled).astype(BF16)
        dext_ref[BLK:BLK + HALO, :] = dext_ref[0:HALO, :]

        dist, valid = _mask_terms(i)
        k_var = _head_variants(kv_ref[:, 0:128], kvp_ref[:, 0:128])
        v_var = _head_variants(kv_ref[:, 128:256], kvp_ref[:, 128:256])
        zero = jnp.zeros((2 * BLK, 128), F32)
        dk_acc = [[zero, zero], [zero, zero]]
        dv_acc = [[zero, zero], [zero, zero]]
        for j in range(4):
            hkv = j // 2
            lanes = slice(j * 128, (j + 1) * 128)
            q_tile = q_ref[:, lanes]
            gate = ag_ref[:, lanes]
            sg = _sigmoid(gate)
            dca = dcat_ref[:, D_POOL + j * 128:D_POOL + (j + 1) * 128]
            do_b = (dca * (gate * sg)).astype(BF16)
            probs = []
            o = jnp.zeros((BLK, 128), F32)
            for half in range(2):
                head = hkv * 4 + 2 * (j % 2) + half
                p, p_sink = _probs(q_tile, k_var[hkv][half], 2.0 ** -(head + 1), sink_ref[l, head], dist, valid)
                probs.append((p, p_sink))
                o = o + jnp.dot(p.astype(BF16), v_var[hkv][half], preferred_element_type=F32)
            dproj_ref[:, C_AG + j * 128:C_AG + (j + 1) * 128] = (
                dca * o * (sg * (1.0 + gate * (1.0 - sg)))).astype(BF16)
            dq = jnp.zeros((BLK, 128), F32)
            for half in range(2):
                head = hkv * 4 + 2 * (j % 2) + half
                p, p_sink = probs[half]
                dp = lax.dot_general(do_b, v_var[hkv][half], NT, preferred_element_type=F32)
                delta = jnp.sum(p * dp, axis=-1, keepdims=True)
                ds_b = (p * (dp - delta) * 0.125).astype(BF16)
                dsink_ref[head:head + 1, :] += jnp.broadcast_to(
                    -jnp.sum(p_sink * delta, axis=0, keepdims=True), (1, 128))
                dq = dq + jnp.dot(ds_b, k_var[hkv][half], preferred_element_type=F32)
                dk_acc[hkv][half] = dk_acc[hkv][half] + lax.dot_general(
                    ds_b, q_tile, TN, preferred_element_type=F32)
                dv_acc[hkv][half] = dv_acc[hkv][half] + lax.dot_general(
                    p.astype(BF16), do_b, TN, preferred_element_type=F32)
            dproj_ref[:, C_Q + j * 128:C_Q + (j + 1) * 128] = dq.astype(BF16)

        low = lax.broadcasted_iota(jnp.int32, (2 * BLK, 128), 1) < 64

        def gather_heads(acc):
            return jnp.where(low, acc[0][0] + pltpu.roll(acc[0][1], 64, axis=1),
                             pltpu.roll(acc[1][0], 64, axis=1) + acc[1][1])

        dk = gather_heads(dk_acc)
        dv = gather_heads(dv_acc)
        dproj_ref[:, C_K:C_V] = (dk[BLK:, :] + dkv_ref[:, 0:128]).astype(BF16)
        dproj_ref[:, C_V:C_AG] = (dv[BLK:, :] + dkv_ref[:, 128:256]).astype(BF16)
        dkv_ref[:, 0:128] = dk[:BLK, :]
        dkv_ref[:, 128:256] = dv[:BLK, :]

    rev = lambda w: pl.BlockSpec((BLK, w), lambda s: (NB - 1 - s, 0))
    prev = lambda w: pl.BlockSpec((BLK, w), lambda s: (jnp.maximum(NB - 2 - s, 0), 0))
    halo = pl.BlockSpec((HALO, 512), lambda s: (jnp.maximum((NB - 1 - s) * (BLK // HALO) - 1, 0), 0))
    return pl.pallas_call(
        body, name="bwd_mix", grid=(NB,),
        in_specs=[rev(512), halo, rev(512), rev(512), rev(256), prev(256), rev(512), rev(D),
                  _layer(l, 4, 128, 128), _layer(l, 1, 512), pl.BlockSpec(memory_space=pltpu.SMEM)],
        out_specs=[rev(D_IN), pl.BlockSpec((4, 128, 128), lambda s: (0, 0, 0)),
                   pl.BlockSpec((4, 128), lambda s: (0, 0)), pl.BlockSpec((8, 128), lambda s: (0, 0))],
        out_shape=[jax.ShapeDtypeStruct((S, D_IN), BF16), jax.ShapeDtypeStruct((4, 128, 128), F32),
                   jax.ShapeDtypeStruct((4, 128), F32), jax.ShapeDtypeStruct((8, 128), F32)],
        scratch_shapes=[pltpu.VMEM((HALO + BLK, 512), F32), pltpu.VMEM((BLK + HALO, 512), F32),
                        pltpu.VMEM((BLK, 256), F32)],
        compiler_params=_params(),
    )(pu, pu, pg, q, kv, kv, ag, dcat, pool_w, pool_scale, sinks)


def _bwd_in(l, dproj, w_in_t, x, g_pre, dres):
    n_steps = S // TM

    def body(dp_ref, w_ref, x_ref, g_ref, dres_ref, dx_ref, dw_ref, dg_ref, acc_ref):
        step = pl.program_id(0)

        @pl.when(step == 0)
        def _():
            dw_ref[...] = jnp.zeros_like(dw_ref)
            acc_ref[...] = jnp.zeros_like(acc_ref)

        dp = dp_ref[...]
        dh = jnp.dot(dp, w_ref[...], preferred_element_type=F32)
        xt = x_ref[...]
        r = lax.rsqrt(jnp.mean(xt * xt, axis=-1, keepdims=True) + EPS)
        xn = xt * r
        g = g_ref[...]
        acc_ref[...] += _rows8(dh * xn)
        a = dh * g
        dx_ref[...] = dres_ref[...] + (r * a - xt * (r * r * r) * jnp.mean(a * xt, axis=-1, keepdims=True))
        dw_ref[...] += lax.dot_general(dp, (xn * g).astype(BF16), TN, preferred_element_type=F32)

        @pl.when(step == n_steps - 1)
        def _():
            _store_lane_rows(dg_ref, acc_ref[...])

    row = lambda w: pl.BlockSpec((TM, w), lambda i: (i, 0))
    full = lambda shape: pl.BlockSpec(shape, lambda i: (0, 0))
    return pl.pallas_call(
        body, name="bwd_in", grid=(n_steps,),
        in_specs=[row(D_IN), _layer(l, D_IN, D), row(D), _layer(l, 1, D), row(D)],
        out_specs=[row(D), full((D_IN, D)), full((8, 128))],
        out_shape=[jax.ShapeDtypeStruct((S, D), F32), jax.ShapeDtypeStruct((D_IN, D), F32),
                   jax.ShapeDtypeStruct((8, 128), F32)],
        scratch_shapes=[pltpu.VMEM((8, D), F32)],
        compiler_params=_params(),
    )(dproj, w_in_t, x, g_pre, dres)


def _local_step(x, target, w_in_t, w_out, pool_w, pool_scale, sinks, norm_pre, norm_post):
    pool_w_b = pool_w.astype(BF16)
    scale3 = pool_scale.reshape(DEPTH, 1, D_POOL)
    pre3 = norm_pre.reshape(DEPTH, 1, D)
    post3 = norm_post.reshape(DEPTH, 1, D)
    saved = []
    xs = x
    for l in range(DEPTH):
        pu, pg, q, kv, ag = _fwd_in(l, xs, pre3, w_in_t)
        cat = _fwd_mix(l, pu, pg, q, kv, ag, pool_w_b, scale3, sinks)
        if l < DEPTH - 1:
            y, x_next = _fwd_out(l, cat, w_out, xs, post3)
        else:
            y, x_next, loss = _fwd_out(l, cat, w_out, xs, post3, target)
        saved.append((xs, pu, pg, q, kv, ag, cat, y))
        xs = x_next

    dx = xs
    grads = [None] * DEPTH
    for l in reversed(range(DEPTH)):
        x_in, pu, pg, q, kv, ag, cat, y = saved[l]
        dcat, dw_out, dg_post = _bwd_out(l, dx, y, post3, cat, w_out)
        dproj, dpw, dsc, dsink = _bwd_mix(l, pu, pg, q, kv, ag, dcat, pool_w_b, scale3, sinks)
        dx, dw_in_t, dg_pre = _bwd_in(l, dproj, w_in_t, x_in, pre3, dx)
        grads[l] = (dw_in_t, dw_out, dpw, dsc, dsink, dg_pre, dg_post)
    return loss, dx, grads


def _place():
    return lax.axis_index("x"), lax.axis_index("y"), lax.axis_index("c")


def _other_chips(x, y):
    return [(1 - x, y), (x, 1 - y), (1 - x, 1 - y)]


def _peer(x, y, c, m):
    return (x ^ (m >> 2), y ^ ((m >> 1) & 1), c ^ (m & 1))


def _gather_weights(w_in_b, w_out_b):
    def body(wi_ref, wo_ref, gi_ref, go_ref, send_sems, recv_sems, local_sems):
        x, y, c = _place()
        mine = 2 * x + y
        sibling = (x, y, 1 - c)
        chips = _other_chips(x, y)
        pairs = [(wi_ref, gi_ref, W_IN_SHARD), (wo_ref, go_ref, W_OUT_SHARD)]

        def rows(dst, n, chip, layer):
            return dst.at[layer, pl.ds(pl.multiple_of(chip * n, 16), n), :]

        local = []
        for a, (src, dst, n) in enumerate(pairs):
            for layer in range(DEPTH):
                local.append(pltpu.make_async_copy(src.at[layer], rows(dst, n, mine, layer),
                                                   local_sems.at[2 * a + layer]))
        for cp in local:
            cp.start()

        def remote(k, src, dst, to):
            return pltpu.make_async_remote_copy(src_ref=src, dst_ref=dst, send_sem=send_sems.at[k],
                                                recv_sem=recv_sems.at[k], device_id=to, device_id_type=MESH)

        sends = []
        for a, (src, dst, n) in enumerate(pairs):
            for j, chip in enumerate(chips):
                sends.append(remote(6 * a + j, src.at[c], rows(dst, n, mine, c), (*chip, c)))
        for cp in sends:
            cp.start()
        for a, (src, dst, n) in enumerate(pairs):
            for j, chip in enumerate(chips):
                theirs = rows(dst, n, 2 * chip[0] + chip[1], c)
                remote(6 * a + j, theirs, theirs, (x, y, c)).wait_recv()
                fwd = remote(6 * a + 3 + j, theirs, theirs, sibling)
                fwd.start()
                sends.append(fwd)
        for a, (src, dst, n) in enumerate(pairs):
            for j, chip in enumerate(chips):
                theirs = rows(dst, n, 2 * chip[0] + chip[1], 1 - c)
                remote(6 * a + 3 + j, theirs, theirs, (x, y, c)).wait_recv()
        for cp in sends:
            cp.wait_send()
        for cp in local:
            cp.wait()

    return pl.pallas_call(
        body, name="gather_weights",
        in_specs=[ANY, ANY], out_specs=[ANY, ANY],
        out_shape=[jax.ShapeDtypeStruct((DEPTH, D_IN, D), BF16), jax.ShapeDtypeStruct((DEPTH, D, D), BF16)],
        scratch_shapes=[pltpu.SemaphoreType.DMA((12,)), pltpu.SemaphoreType.DMA((12,)), pltpu.SemaphoreType.DMA((4,))],
    )(w_in_b, w_out_b)


def _exchange_pieces(name, arrays):
    n = len(arrays)
    piece = [a.shape[0] // 8 for a in arrays]

    def body(*refs):
        srcs, dsts, send_sems, recv_sems = refs[:n], refs[n:2 * n], refs[2 * n], refs[2 * n + 1]
        x, y, c = _place()
        copies = []
        for a, (src, dst) in enumerate(zip(srcs, dsts)):
            for m in range(1, 8):
                px, py, pc = _peer(x, y, c, m)
                k = 4 * px + 2 * py + pc
                copies.append(pltpu.make_async_remote_copy(
                    src_ref=src.at[pl.ds(pl.multiple_of(k * piece[a], 16), piece[a]), :], dst_ref=dst.at[m - 1],
                    send_sem=send_sems.at[7 * a + m - 1], recv_sem=recv_sems.at[7 * a + m - 1],
                    device_id=(px, py, pc), device_id_type=MESH))
        for cp in copies:
            cp.start()
        for cp in copies:
            cp.wait()

    return pl.pallas_call(
        body, name=name, in_specs=[ANY] * n, out_specs=[ANY] * n,
        out_shape=[jax.ShapeDtypeStruct((7, p, a.shape[1]), BF16) for a, p in zip(arrays, piece)],
        scratch_shapes=[pltpu.SemaphoreType.DMA((7 * n,)), pltpu.SemaphoreType.DMA((7 * n,))],
    )(*arrays)


def _sum_pieces(name, partial, recv, k_arr, tile):
    p, cols = recv.shape[1:]
    steps = p // tile

    def body(k_ref, o_ref, r_ref, out_ref):
        total = o_ref[...]
        for m in range(7):
            total = total + r_ref[m].astype(F32)
        out_ref[...] = total

    return pl.pallas_call(
        body, name=name,
        grid_spec=pltpu.PrefetchScalarGridSpec(
            num_scalar_prefetch=1, grid=(steps,),
            in_specs=[pl.BlockSpec((tile, cols), lambda i, k_ref: (k_ref[0] * steps + i, 0)),
                      pl.BlockSpec((7, tile, cols), lambda i, k_ref: (0, i, 0))],
            out_specs=pl.BlockSpec((tile, cols), lambda i, k_ref: (i, 0))),
        out_shape=jax.ShapeDtypeStruct((p, cols), F32),
        compiler_params=_params(),
    )(k_arr, partial, recv)


def _swap_halves(halves_in, halves_out):
    srcs = list(halves_in) + list(halves_out)
    n = len(srcs)

    def body(*refs):
        src_refs, (oi_ref, oo_ref), send_sems, recv_sems, local_sems = refs[:n], refs[n:n + 2], *refs[n + 2:]
        x, y, c = _place()
        dsts = [(oi_ref, l, W_IN_SHARD // 2) for l in range(DEPTH)] + [(oo_ref, l, W_OUT_SHARD // 2) for l in range(DEPTH)]

        def half(dst, l, p, which):
            return dst.at[l, pl.ds(pl.multiple_of(which * p, 8), p), :]

        local, remote = [], []
        for k, (src, (dst, l, p)) in enumerate(zip(src_refs, dsts)):
            local.append(pltpu.make_async_copy(src, half(dst, l, p, c), local_sems.at[k]))
            remote.append(pltpu.make_async_remote_copy(src_ref=src, dst_ref=half(dst, l, p, c), send_sem=send_sems.at[k],
                                                       recv_sem=recv_sems.at[k], device_id=(x, y, 1 - c),
                                                       device_id_type=MESH))
        for cp in local + remote:
            cp.start()
        for cp in remote:
            cp.wait_send()
        for k, (src, (dst, l, p)) in enumerate(zip(src_refs, dsts)):
            pltpu.make_async_remote_copy(src_ref=src, dst_ref=half(dst, l, p, 1 - c), send_sem=send_sems.at[k],
                                         recv_sem=recv_sems.at[k], device_id=(x, y, 1 - c),
                                         device_id_type=MESH).wait_recv()
        for cp in local:
            cp.wait()

    return pl.pallas_call(
        body, name="swap_halves", in_specs=[ANY] * n, out_specs=[ANY, ANY],
        out_shape=[jax.ShapeDtypeStruct((DEPTH, W_IN_SHARD, D), F32), jax.ShapeDtypeStruct((DEPTH, W_OUT_SHARD, D), F32)],
        scratch_shapes=[pltpu.SemaphoreType.DMA((n,))] * 3,
    )(*srcs)


def _allreduce_small(packed):
    def body(p_ref, out_ref, recv_ref, send1, recv1, send2, recv2):
        x, y, c = _place()
        me = 4 * x + 2 * y + c

        def index(p):
            return 4 * p[0] + 2 * p[1] + p[2]

        def copy(src, dst, sems, m):
            return pltpu.make_async_remote_copy(src_ref=src, dst_ref=dst, send_sem=sems[0].at[m - 1],
                                                recv_sem=sems[1].at[m - 1], device_id=_peer(x, y, c, m),
                                                device_id_type=MESH)

        first = [copy(p_ref.at[index(_peer(x, y, c, m))], recv_ref.at[me], (send1, recv1), m) for m in range(1, 8)]
        for cp in first:
            cp.start()
        recv_ref[me] = p_ref[me]
        for m in range(1, 8):
            slot = recv_ref.at[index(_peer(x, y, c, m))]
            copy(slot, slot, (send1, recv1), m).wait_recv()
        total = recv_ref[0]
        for d in range(1, 8):
            total = total + recv_ref[d]
        out_ref[me] = total
        second = [copy(out_ref.at[me], out_ref.at[me], (send2, recv2), m) for m in range(1, 8)]
        for cp in second:
            cp.start()
        for m in range(1, 8):
            slot = out_ref.at[index(_peer(x, y, c, m))]
            copy(slot, slot, (send2, recv2), m).wait_recv()
        for cp in first + second:
            cp.wait_send()

    vmem = pl.BlockSpec(memory_space=pltpu.VMEM)
    return pl.pallas_call(
        body, name="allreduce_small", in_specs=[vmem], out_specs=vmem,
        out_shape=jax.ShapeDtypeStruct(packed.shape, F32),
        scratch_shapes=[pltpu.VMEM(packed.shape, F32)] + [pltpu.SemaphoreType.DMA((7,))] * 4,
    )(packed)


def _adamw(name, w, g, m, v, rows_per_step):
    layers, rows, cols = w.shape

    def body(w_ref, g_ref, m_ref, v_ref, d_ref, nm_ref, nv_ref):
        gt = g_ref[...]
        nm = ADAM_B1 * m_ref[...] + (1.0 - ADAM_B1) * gt
        nv = ADAM_B2 * v_ref[...] + (1.0 - ADAM_B2) * (gt * gt)
        m_hat = nm / (1.0 - ADAM_B1 ** ADAM_STEP)
        v_hat = nv / (1.0 - ADAM_B2 ** ADAM_STEP)
        d_ref[...] = -ADAM_LR * (m_hat / (jnp.sqrt(v_hat) + ADAM_EPS) + ADAM_WD * w_ref[...])
        nm_ref[...] = nm
        nv_ref[...] = nv

    spec = pl.BlockSpec((1, rows_per_step, cols), lambda l, i: (l, i, 0))
    shape = jax.ShapeDtypeStruct(w.shape, F32)
    return pl.pallas_call(
        body, name=name, grid=(layers, rows // rows_per_step),
        in_specs=[spec] * 4, out_specs=[spec] * 3, out_shape=[shape] * 3,
        compiler_params=_params(("arbitrary", "arbitrary")),
    )(w, g, m, v)


def _pack_small(pool_w, pool_scale, sinks, norm_pre, norm_post):
    sink_rows = jnp.zeros((ROWS_SINK, 128), F32).at[0, 0:2 * N_HEADS].set(sinks.reshape(-1))
    return jnp.concatenate([
        pool_w.reshape(ROWS_PW, 128), pool_scale.reshape(ROWS_SC, 128), norm_pre.reshape(ROWS_NORM, 128),
        norm_post.reshape(ROWS_NORM, 128), sink_rows, jnp.zeros((SMALL_ROWS - ROW_LOSS, 128), F32)], axis=0)


def _pack_small_grads(grads, loss):
    sink_rows = jnp.zeros((ROWS_SINK, 128), F32).at[0, 0:2 * N_HEADS].set(
        jnp.concatenate([g[4][:, 0] for g in grads]))
    return jnp.concatenate(
        [g[2].reshape(ROWS_PW // DEPTH, 128) for g in grads] + [g[3] for g in grads] + [g[5] for g in grads]
        + [g[6] for g in grads] + [sink_rows, loss, jnp.zeros((SMALL_ROWS - ROW_LOSS - 8, 128), F32)], axis=0)


def _unpack_small(packed):
    o = 0
    pool_w = packed[o:o + ROWS_PW].reshape(DEPTH, 4, 128, 128)
    o += ROWS_PW
    pool_scale = packed[o:o + ROWS_SC].reshape(DEPTH, 512)
    o += ROWS_SC
    norm_pre = packed[o:o + ROWS_NORM].reshape(DEPTH, D)
    o += ROWS_NORM
    norm_post = packed[o:o + ROWS_NORM].reshape(DEPTH, D)
    o += ROWS_NORM
    sinks = packed[o, 0:2 * N_HEADS].reshape(DEPTH, N_HEADS)
    return pool_w, pool_scale, sinks, norm_pre, norm_post


def kernel(x, w_in, pool_w, pool_scale, attn_sinks, w_out, norm_pre, norm_post, loss_target, m_w_in, m_pool_w, m_pool_scale, m_attn_sinks, m_w_out, m_norm_pre, m_norm_post, v_w_in, v_pool_w, v_pool_scale, v_attn_sinks, v_w_out, v_norm_pre, v_norm_post):
    cx, cy, cc = _place()
    k_arr = jnp.reshape(4 * cx + 2 * cy + cc, (1,)).astype(jnp.int32)
    t = lambda a: jnp.transpose(a, (0, 2, 1))
    w_in_t = t(w_in)

    w_in_full, w_out_full = _gather_weights(w_in_t.astype(BF16), w_out.astype(BF16))

    loss, grad_x, grads = _local_step(x[0], loss_target[0], w_in_full, w_out_full, pool_w, pool_scale, attn_sinks,
                                      norm_pre, norm_post)

    halves_in, halves_out = [], []
    for l in range(DEPTH):
        dw_in_t, dw_out = grads[l][0], grads[l][1]
        recv_in, recv_out = _exchange_pieces("exchange_pieces", [dw_in_t.astype(BF16), dw_out.astype(BF16)])
        halves_in.append(_sum_pieces("sum_pieces_in", dw_in_t, recv_in, k_arr, 96))
        halves_out.append(_sum_pieces("sum_pieces_out", dw_out, recv_out, k_arr, 64))
    grad_w_in_t, grad_w_out = _swap_halves(halves_in, halves_out)

    packed = _pack_small_grads(grads, loss).reshape(8, PIECE_ROWS, 128)
    g_small = _allreduce_small(packed).reshape(1, SMALL_ROWS, 128)

    d_in, nm_in, nv_in = _adamw("adamw_w_in", w_in_t, grad_w_in_t, t(m_w_in), t(v_w_in), 288)
    d_out, nm_out, nv_out = _adamw("adamw_w_out", w_out, grad_w_out, m_w_out, v_w_out, 256)
    small = lambda *a: _pack_small(*a).reshape(1, SMALL_ROWS, 128)
    d_s, nm_s, nv_s = _adamw(
        "adamw_small", small(pool_w, pool_scale, attn_sinks, norm_pre, norm_post), g_small,
        small(m_pool_w, m_pool_scale, m_attn_sinks, m_norm_pre, m_norm_post),
        small(v_pool_w, v_pool_scale, v_attn_sinks, v_norm_pre, v_norm_post), SMALL_ROWS)

    g_pw, g_sc, g_sk, g_pre, g_post = _unpack_small(g_small[0])
    d_pw, d_sc, d_sk, d_pre, d_post = _unpack_small(d_s[0])
    m_pw, m_sc, m_sk, m_pre, m_post = _unpack_small(nm_s[0])
    v_pw, v_sc, v_sk, v_pre, v_post = _unpack_small(nv_s[0])
    return (g_small[0, ROW_LOSS, 0], grad_x[None], t(grad_w_in_t), g_pw, g_sc, g_sk, grad_w_out, g_pre, g_post,
            t(d_in), d_pw, d_sc, d_sk, d_out, d_pre, d_post,
            t(nm_in), m_pw, m_sc, m_sk, nm_out, m_pre, m_post,
            t(nv_in), v_pw, v_sc, v_sk, nv_out, v_pre, v_post)
```

```python
import jax
import jax.numpy as jnp
from jax import lax
from jax.experimental import pallas as pl
from jax.experimental.pallas import tpu as pltpu

F32 = jnp.float32
BF16 = jnp.bfloat16

S = 2048
D = 1024
DEPTH = 2
D_POOL = 512
POOL_WINDOWS = (2, 4, 8, 16)
N_HEADS = 8
D_IN = 2304
N_SHARDS = 4
W_IN_SHARD = D_IN // N_SHARDS
W_OUT_SHARD = D // N_SHARDS
BLK = 128
NB = S // BLK
HALO = 16
EPS = 1e-6
NEG_INF = -1e30
C_PU, C_PG, C_Q, C_K, C_V, C_AG = 0, 512, 1024, 1536, 1664, 1792

ADAM_LR = 0.001
ADAM_B1 = 0.9
ADAM_B2 = 0.999
ADAM_EPS = 1e-08
ADAM_WD = 0.01
ADAM_STEP = 10

TM = 256
VMEM_LIMIT = 56 * 1024 * 1024

NT = (((1,), (1,)), ((), ()))
TN = (((0,), (0,)), ((), ()))

MESH = pl.DeviceIdType.MESH
ANY = pl.BlockSpec(memory_space=pl.ANY)

ROWS_PW, ROWS_SC, ROWS_NORM, ROWS_SINK = 1024, 8, 16, 8
ROW_LOSS = ROWS_PW + ROWS_SC + 2 * ROWS_NORM + ROWS_SINK
SMALL_ROWS = 1088
PIECE_ROWS = SMALL_ROWS // 8


def _params(sem=("arbitrary",)):
    return pltpu.CompilerParams(dimension_semantics=sem, vmem_limit_bytes=VMEM_LIMIT)


def _sigmoid(v):
    return 1.0 / (1.0 + jnp.exp(-v))


def _rows8(v):
    r, c = v.shape
    return v.reshape(r // 8, 8, c).sum(axis=0)


def _layer(l, *shape):
    zeros = (0,) * len(shape)
    return pl.BlockSpec((None,) + shape, lambda i: (l,) + zeros)


def _fwd_in(l, x, g_pre, w_in_t):
    def body(x_ref, g_ref, w_ref, pu_ref, pg_ref, q_ref, kv_ref, ag_ref):
        xt = x_ref[...]
        r = lax.rsqrt(jnp.mean(xt * xt, axis=-1, keepdims=True) + EPS)
        h = (xt * r * g_ref[...]).astype(BF16)

        def proj(lo, hi):
            return lax.dot_general(h, w_ref[lo:hi, :], NT, preferred_element_type=F32)

        pu_ref[...] = proj(C_PU, C_PG)
        pg_ref[...] = proj(C_PG, C_Q)
        q_ref[...] = proj(C_Q, C_K).astype(BF16)
        kv_ref[...] = proj(C_K, C_AG).astype(BF16)
        ag_ref[...] = proj(C_AG, D_IN)

    row = lambda w: pl.BlockSpec((TM, w), lambda i: (i, 0))
    return pl.pallas_call(
        body, name="fwd_in", grid=(S // TM,),
        in_specs=[row(D), _layer(l, 1, D), pl.BlockSpec((D_IN, D), lambda i: (0, 0))],
        out_specs=[row(512), row(512), row(512), row(256), row(512)],
        out_shape=[jax.ShapeDtypeStruct((S, 512), F32), jax.ShapeDtypeStruct((S, 512), F32),
                   jax.ShapeDtypeStruct((S, 512), BF16), jax.ShapeDtypeStruct((S, 256), BF16),
                   jax.ShapeDtypeStruct((S, 512), F32)],
        compiler_params=_params(),
    )(x, g_pre, w_in_t)


def _mask_terms(i):
    qi = lax.broadcasted_iota(jnp.int32, (BLK, 2 * BLK), 0)
    kj = lax.broadcasted_iota(jnp.int32, (BLK, 2 * BLK), 1)
    dist = qi + BLK - kj
    valid = (dist >= 0) & (dist < BLK) & ((kj >= BLK) | (i > 0))
    return dist.astype(F32), valid


def _head_variants(cur, prev):
    both = jnp.concatenate([prev, cur], axis=0).astype(F32)
    swapped = pltpu.roll(both, 64, axis=1)
    low = lax.broadcasted_iota(jnp.int32, both.shape, 1) < 64
    zero = jnp.zeros_like(both)
    return ((jnp.where(low, both, zero).astype(BF16), jnp.where(low, zero, swapped).astype(BF16)),
            (jnp.where(low, swapped, zero).astype(BF16), jnp.where(low, zero, both).astype(BF16)))


def _probs(q_tile, k_var, slope, sink, dist, valid):
    s = lax.dot_general(q_tile, k_var, NT, preferred_element_type=F32)
    s = jnp.where(valid, s * 0.125 - slope * dist, NEG_INF)
    m = jnp.maximum(jnp.max(s, axis=-1, keepdims=True), sink)
    p = jnp.exp(s - m)
    e_sink = jnp.exp(sink - m)
    inv = 1.0 / (jnp.sum(p, axis=-1, keepdims=True) + e_sink)
    return p * inv, e_sink * inv


def _pool_block(ext_ref, i, g, w):
    lanes = slice(g * 128, (g + 1) * 128)
    u = ext_ref[HALO:HALO + BLK, lanes]
    acc = u
    for j in range(1, w):
        acc = acc + ext_ref[HALO - j:HALO - j + BLK, lanes]
    t = (i * BLK + lax.broadcasted_iota(jnp.int32, (BLK, 1), 0)).astype(F32)
    inv = 1.0 / jnp.minimum(t + 1.0, float(w))
    return acc * inv - u, inv


def _fwd_mix(l, pu, pg, q, kv, ag, pool_w, pool_scale, sinks):
    def body(pu_ref, pup_ref, pg_ref, q_ref, kv_ref, kvp_ref, ag_ref, pw_ref, sc_ref, sink_ref, cat_ref, ext_ref):
        i = pl.program_id(0)
        ext_ref[0:HALO, :] = jnp.where(i > 0, pup_ref[...], 0.0)
        ext_ref[HALO:HALO + BLK, :] = pu_ref[...]
        for g, w in enumerate(POOL_WINDOWS):
            lanes = slice(g * 128, (g + 1) * 128)
            pooled, _ = _pool_block(ext_ref, i, g, w)
            mixed = jnp.dot(pooled.astype(BF16), pw_ref[g], preferred_element_type=F32)
            gate = pg_ref[:, lanes]
            cat_ref[:, lanes] = (mixed * sc_ref[:, lanes] * (gate * _sigmoid(gate))).astype(BF16)

        dist, valid = _mask_terms(i)
        k_var = _head_variants(kv_ref[:, 0:128], kvp_ref[:, 0:128])
        v_var = _head_variants(kv_ref[:, 128:256], kvp_ref[:, 128:256])
        for j in range(4):
            hkv = j // 2
            lanes = slice(j * 128, (j + 1) * 128)
            q_tile = q_ref[:, lanes]
            o = jnp.zeros((BLK, 128), F32)
            for half in range(2):
                head = hkv * 4 + 2 * (j % 2) + half
                p, _ = _probs(q_tile, k_var[hkv][half], 2.0 ** -(head + 1), sink_ref[l, head], dist, valid)
                o = o + jnp.dot(p.astype(BF16), v_var[hkv][half], preferred_element_type=F32)
            gate = ag_ref[:, lanes]
            cat_ref[:, D_POOL + j * 128:D_POOL + (j + 1) * 128] = (o * (gate * _sigmoid(gate))).astype(BF16)

    blk = lambda w: pl.BlockSpec((BLK, w), lambda i: (i, 0))
    prev = lambda w: pl.BlockSpec((BLK, w), lambda i: (jnp.maximum(i - 1, 0), 0))
    halo = pl.BlockSpec((HALO, 512), lambda i: (jnp.maximum(i * (BLK // HALO) - 1, 0), 0))
    return pl.pallas_call(
        body, name="fwd_mix", grid=(NB,),
        in_specs=[blk(512), halo, blk(512), blk(512), blk(256), prev(256), blk(512),
                  _layer(l, 4, 128, 128), _layer(l, 1, 512), pl.BlockSpec(memory_space=pltpu.SMEM)],
        out_specs=blk(D),
        out_shape=jax.ShapeDtypeStruct((S, D), BF16),
        scratch_shapes=[pltpu.VMEM((HALO + BLK, 512), F32)],
        compiler_params=_params(),
    )(pu, pu, pg, q, kv, kv, ag, pool_w, pool_scale, sinks)


def _fwd_out(l, cat, w_out, x, g_post, target=None):
    last = target is not None
    n_steps = S // TM

    def body(*refs):
        if last:
            cat_ref, w_ref, x_ref, g_ref, t_ref, y_ref, dx_ref, loss_ref, acc_ref = refs
        else:
            cat_ref, w_ref, x_ref, g_ref, y_ref, xn_ref = refs
        y = jnp.dot(cat_ref[...], w_ref[...], preferred_element_type=F32)
        y_ref[...] = y
        r = lax.rsqrt(jnp.mean(y * y, axis=-1, keepdims=True) + EPS)
        xn = x_ref[...] + y * r * g_ref[...]
        if not last:
            xn_ref[...] = xn
            return
        step = pl.program_id(0)
        err = xn - t_ref[...]
        dx_ref[...] = err * (1.0 / D)

        @pl.when(step == 0)
        def _():
            acc_ref[...] = jnp.zeros_like(acc_ref)

        acc_ref[...] += _rows8(err * err)

        @pl.when(step == n_steps - 1)
        def _():
            loss_ref[...] = jnp.full((8, 128), (0.5 / D) * jnp.sum(acc_ref[...]), F32)

    row = lambda: pl.BlockSpec((TM, D), lambda i: (i, 0))
    act = jax.ShapeDtypeStruct((S, D), F32)
    in_specs = [row(), pl.BlockSpec((D, D), lambda i: (0, 0)), row(), _layer(l, 1, D)]
    args = [cat, w_out, x, g_post]
    if last:
        return pl.pallas_call(
            body, name="fwd_out_loss", grid=(n_steps,),
            in_specs=in_specs + [row()],
            out_specs=[row(), row(), pl.BlockSpec((8, 128), lambda i: (0, 0))],
            out_shape=[act, act, jax.ShapeDtypeStruct((8, 128), F32)],
            scratch_shapes=[pltpu.VMEM((8, D), F32)],
            compiler_params=_params(),
        )(*args, target)
    return pl.pallas_call(
        body, name="fwd_out", grid=(n_steps,),
        in_specs=in_specs, out_specs=[row(), row()], out_shape=[act, act],
        compiler_params=_params(),
    )(*args)


def _store_lane_rows(ref, acc):
    total = jnp.sum(acc, axis=0, keepdims=True)
    for k in range(ref.shape[0]):
        ref[k:k + 1, :] = total[:, k * 128:(k + 1) * 128]


def _bwd_out(l, dxn, y, g_post, cat, w_out, deps=()):
    n_steps = S // TM

    def body(dz_ref, y_ref, g_ref, cat_ref, w_ref, *rest):
        dcat_ref, dw_ref, dwb_ref, dg_ref, acc_ref = rest[len(deps):]
        step = pl.program_id(0)

        @pl.when(step == 0)
        def _():
            dw_ref[...] = jnp.zeros_like(dw_ref)
            acc_ref[...] = jnp.zeros_like(acc_ref)

        y = y_ref[...]
        dz = dz_ref[...]
        r = lax.rsqrt(jnp.mean(y * y, axis=-1, keepdims=True) + EPS)
        a = dz * g_ref[...]
        dy = r * a - y * (r * r * r) * jnp.mean(a * y, axis=-1, keepdims=True)
        acc_ref[...] += _rows8(dz * (y * r))
        dyb = dy.astype(BF16)
        dcat_ref[...] = lax.dot_general(dyb, w_ref[...], NT, preferred_element_type=F32)
        dw_ref[...] += lax.dot_general(cat_ref[...], dyb, TN, preferred_element_type=F32)

        @pl.when(step == n_steps - 1)
        def _():
            _store_lane_rows(dg_ref, acc_ref[...])
            dwb_ref[...] = dw_ref[...].astype(BF16)

    row = lambda: pl.BlockSpec((TM, D), lambda i: (i, 0))
    full = lambda shape: pl.BlockSpec(shape, lambda i: (0, 0))
    return pl.pallas_call(
        body, name="bwd_out", grid=(n_steps,),
        in_specs=[row(), row(), _layer(l, 1, D), row(), full((D, D))] + [ANY] * len(deps),
        out_specs=[row(), full((D, D)), full((D, D)), full((8, 128))],
        out_shape=[jax.ShapeDtypeStruct((S, D), F32), jax.ShapeDtypeStruct((D, D), F32),
                   jax.ShapeDtypeStruct((D, D), BF16), jax.ShapeDtypeStruct((8, 128), F32)],
        scratch_shapes=[pltpu.VMEM((8, D), F32)],
        compiler_params=_params(),
    )(dxn, y, g_post, cat, w_out, *deps)


def _bwd_mix(l, pu, pg, q, kv, ag, dcat, pool_w, pool_scale, sinks, deps=()):
    def body(pu_ref, pup_ref, pg_ref, q_ref, kv_ref, kvp_ref, ag_ref, dcat_ref, pw_ref, sc_ref, sink_ref, *rest):
        dproj_ref, dpw_ref, dsc_ref, dsink_ref, ext_ref, dext_ref, dkv_ref = rest[len(deps):]
        step = pl.program_id(0)
        i = NB - 1 - step

        @pl.when(step == 0)
        def _():
            dpw_ref[...] = jnp.zeros_like(dpw_ref)
            dsc_ref[...] = jnp.zeros_like(dsc_ref)
            dsink_ref[...] = jnp.zeros_like(dsink_ref)
            dext_ref[BLK:BLK + HALO, :] = jnp.zeros((HALO, 512), F32)
            dkv_ref[...] = jnp.zeros_like(dkv_ref)

        ext_ref[0:HALO, :] = jnp.where(i > 0, pup_ref[...], 0.0)
        ext_ref[HALO:HALO + BLK, :] = pu_ref[...]
        for g, w in enumerate(POOL_WINDOWS):
            lanes = slice(g * 128, (g + 1) * 128)
            pooled, inv = _pool_block(ext_ref, i, g, w)
            pooled_b = pooled.astype(BF16)
            mixed = jnp.dot(pooled_b, pw_ref[g], preferred_element_type=F32)
            scale = sc_ref[:, lanes]
            gate = pg_ref[:, lanes]
            sg = _sigmoid(gate)
            dpo = dcat_ref[:, lanes]
            dproj_ref[:, C_PG + g * 128:C_PG + (g + 1) * 128] = (
                dpo * (mixed * scale) * (sg * (1.0 + gate * (1.0 - sg)))).astype(BF16)
            dms = dpo * (gate * sg)
            dsc_ref[g:g + 1, :] += jnp.sum(dms * mixed, axis=0, keepdims=True)
            dmixed = (dms * scale).astype(BF16)
            dpw_ref[g] += lax.dot_general(pooled_b, dmixed, TN, preferred_element_type=F32)
            dpooled = lax.dot_general(dmixed, pw_ref[g], NT, preferred_element_type=F32)
            dext_ref[0:BLK, lanes] = dpooled * inv
            acc = dext_ref[0:BLK, lanes]
            for j in range(1, w):
                acc = acc + dext_ref[j:j + BLK, lanes]
            dproj_ref[:, C_PU + g * 128:C_PU + (g + 1) * 128] = (acc - dpooled).astype(BF16)
        dext_ref[BLK:BLK + HALO, :] = dext_ref[0:HALO, :]

        dist, valid = _mask_terms(i)
        k_var = _head_variants(kv_ref[:, 0:128], kvp_ref[:, 0:128])
        v_var = _head_variants(kv_ref[:, 128:256], kvp_ref[:, 128:256])
        zero = jnp.zeros((2 * BLK, 128), F32)
        dk_acc = [[zero, zero], [zero, zero]]
        dv_acc = [[zero, zero], [zero, zero]]
        for j in range(4):
            hkv = j // 2
            lanes = slice(j * 128, (j + 1) * 128)
            q_tile = q_ref[:, lanes]
            gate = ag_ref[:, lanes]
            sg = _sigmoid(gate)
            dca = dcat_ref[:, D_POOL + j * 128:D_POOL + (j + 1) * 128]
            do_b = (dca * (gate * sg)).astype(BF16)
            probs = []
            o = jnp.zeros((BLK, 128), F32)
            for half in range(2):
                head = hkv * 4 + 2 * (j % 2) + half
                p, p_sink = _probs(q_tile, k_var[hkv][half], 2.0 ** -(head + 1), sink_ref[l, head], dist, valid)
                probs.append((p, p_sink))
                o = o + jnp.dot(p.astype(BF16), v_var[hkv][half], preferred_element_type=F32)
            dproj_ref[:, C_AG + j * 128:C_AG + (j + 1) * 128] = (
                dca * o * (sg * (1.0 + gate * (1.0 - sg)))).astype(BF16)
            dq = jnp.zeros((BLK, 128), F32)
            for half in range(2):
                head = hkv * 4 + 2 * (j % 2) + half
                p, p_sink = probs[half]
                dp = lax.dot_general(do_b, v_var[hkv][half], NT, preferred_element_type=F32)
                delta = jnp.sum(p * dp, axis=-1, keepdims=True)
                ds_b = (p * (dp - delta) * 0.125).astype(BF16)
                dsink_ref[head:head + 1, :] += jnp.broadcast_to(
                    -jnp.sum(p_sink * delta, axis=0, keepdims=True), (1, 128))
                dq = dq + jnp.dot(ds_b, k_var[hkv][half], preferred_element_type=F32)
                dk_acc[hkv][half] = dk_acc[hkv][half] + lax.dot_general(
                    ds_b, q_tile, TN, preferred_element_type=F32)
                dv_acc[hkv][half] = dv_acc[hkv][half] + lax.dot_general(
                    p.astype(BF16), do_b, TN, preferred_element_type=F32)
            dproj_ref[:, C_Q + j * 128:C_Q + (j + 1) * 128] = dq.astype(BF16)

        low = lax.broadcasted_iota(jnp.int32, (2 * BLK, 128), 1) < 64

        def gather_heads(acc):
            return jnp.where(low, acc[0][0] + pltpu.roll(acc[0][1], 64, axis=1),
                             pltpu.roll(acc[1][0], 64, axis=1) + acc[1][1])

        dk = gather_heads(dk_acc)
        dv = gather_heads(dv_acc)
        dproj_ref[:, C_K:C_V] = (dk[BLK:, :] + dkv_ref[:, 0:128]).astype(BF16)
        dproj_ref[:, C_V:C_AG] = (dv[BLK:, :] + dkv_ref[:, 128:256]).astype(BF16)
        dkv_ref[:, 0:128] = dk[:BLK, :]
        dkv_ref[:, 128:256] = dv[:BLK, :]

    rev = lambda w: pl.BlockSpec((BLK, w), lambda s: (NB - 1 - s, 0))
    prev = lambda w: pl.BlockSpec((BLK, w), lambda s: (jnp.maximum(NB - 2 - s, 0), 0))
    halo = pl.BlockSpec((HALO, 512), lambda s: (jnp.maximum((NB - 1 - s) * (BLK // HALO) - 1, 0), 0))
    return pl.pallas_call(
        body, name="bwd_mix", grid=(NB,),
        in_specs=[rev(512), halo, rev(512), rev(512), rev(256), prev(256), rev(512), rev(D),
                  _layer(l, 4, 128, 128), _layer(l, 1, 512), pl.BlockSpec(memory_space=pltpu.SMEM)] + [ANY] * len(deps),
        out_specs=[rev(D_IN), pl.BlockSpec((4, 128, 128), lambda s: (0, 0, 0)),
                   pl.BlockSpec((4, 128), lambda s: (0, 0)), pl.BlockSpec((8, 128), lambda s: (0, 0))],
        out_shape=[jax.ShapeDtypeStruct((S, D_IN), BF16), jax.ShapeDtypeStruct((4, 128, 128), F32),
                   jax.ShapeDtypeStruct((4, 128), F32), jax.ShapeDtypeStruct((8, 128), F32)],
        scratch_shapes=[pltpu.VMEM((HALO + BLK, 512), F32), pltpu.VMEM((BLK + HALO, 512), F32),
                        pltpu.VMEM((BLK, 256), F32)],
        compiler_params=_params(),
    )(pu, pu, pg, q, kv, kv, ag, dcat, pool_w, pool_scale, sinks, *deps)


def _bwd_in(l, dproj, w_in_t, x, g_pre, dres, deps=()):
    n_steps = S // TM

    def body(dp_ref, w_ref, x_ref, g_ref, dres_ref, *rest):
        dx_ref, dw_ref, dwb_ref, dg_ref, acc_ref = rest[len(deps):]
        step = pl.program_id(0)

        @pl.when(step == 0)
        def _():
            dw_ref[...] = jnp.zeros_like(dw_ref)
            acc_ref[...] = jnp.zeros_like(acc_ref)

        dp = dp_ref[...]
        dh = jnp.dot(dp, w_ref[...], preferred_element_type=F32)
        xt = x_ref[...]
        r = lax.rsqrt(jnp.mean(xt * xt, axis=-1, keepdims=True) + EPS)
        xn = xt * r
        g = g_ref[...]
        acc_ref[...] += _rows8(dh * xn)
        a = dh * g
        dx_ref[...] = dres_ref[...] + (r * a - xt * (r * r * r) * jnp.mean(a * xt, axis=-1, keepdims=True))
        dw_ref[...] += lax.dot_general(dp, (xn * g).astype(BF16), TN, preferred_element_type=F32)

        @pl.when(step == n_steps - 1)
        def _():
            _store_lane_rows(dg_ref, acc_ref[...])
            dwb_ref[...] = dw_ref[...].astype(BF16)

    row = lambda w: pl.BlockSpec((TM, w), lambda i: (i, 0))
    full = lambda shape: pl.BlockSpec(shape, lambda i: (0, 0))
    return pl.pallas_call(
        body, name="bwd_in", grid=(n_steps,),
        in_specs=[row(D_IN), full((D_IN, D)), row(D), _layer(l, 1, D), row(D)] + [ANY] * len(deps),
        out_specs=[row(D), full((D_IN, D)), full((D_IN, D)), full((8, 128))],
        out_shape=[jax.ShapeDtypeStruct((S, D), F32), jax.ShapeDtypeStruct((D_IN, D), F32),
                   jax.ShapeDtypeStruct((D_IN, D), BF16), jax.ShapeDtypeStruct((8, 128), F32)],
        scratch_shapes=[pltpu.VMEM((8, D), F32)],
        compiler_params=_params(),
    )(dproj, w_in_t, x, g_pre, dres, *deps)


HBM = pl.BlockSpec(memory_space=pltpu.HBM)
SEM = pl.BlockSpec(memory_space=pltpu.SEMAPHORE)
SPLIT_COPY = pltpu.CompilerParams(has_side_effects=pltpu.SideEffectType.DATAFLOW_SIDE_EFFECTING)


def _in_hbm(a):
    return pltpu.with_memory_space_constraint(a, pltpu.HBM)

def _place():
    return lax.axis_index("x"), lax.axis_index("y"), lax.axis_index("c")


def _other_chips(x, y):
    return [(1 - x, y), (x, 1 - y), (1 - x, 1 - y)]


def _peer(x, y, c, m):
    return (x ^ (m >> 2), y ^ ((m >> 1) & 1), c ^ (m & 1))


def _place_cast(name, src, chip_arr, tile):
    _, n, cols = src.shape
    steps = n // tile

    def body(chip_ref, s0_ref, s1_ref, o0_ref, o1_ref):
        o0_ref[...] = s0_ref[...].astype(BF16)
        o1_ref[...] = s1_ref[...].astype(BF16)

    return pl.pallas_call(
        body, name=name,
        grid_spec=pltpu.PrefetchScalarGridSpec(
            num_scalar_prefetch=1, grid=(steps,),
            in_specs=[pl.BlockSpec((None, tile, cols), lambda i, chip: (0, i, 0)),
                      pl.BlockSpec((None, tile, cols), lambda i, chip: (1, i, 0))],
            out_specs=[pl.BlockSpec((tile, cols), lambda i, chip: (chip[0] * steps + i, 0))] * 2),
        out_shape=[jax.ShapeDtypeStruct((N_SHARDS * n, cols), BF16)] * 2,
        compiler_params=_params(),
    )(chip_arr, src, src)


def _chip_rows(ref, chip):
    n = ref.shape[0] // N_SHARDS
    return ref.at[pl.ds(pl.multiple_of(chip * n, 16), n), :]


def _gather_start(bufs):
    n = len(bufs)

    def body(*refs):
        ins, send, recv, token = refs[:n], refs[n:2 * n], refs[2 * n:3 * n], refs[-1]
        x, y, c = _place()
        for a, buf in enumerate(ins):
            own = _chip_rows(buf, 2 * x + y)
            for j, chip in enumerate(_other_chips(x, y)):
                pltpu.make_async_remote_copy(src_ref=own, dst_ref=own, send_sem=send[a].at[j], recv_sem=recv[a].at[j],
                                             device_id=(*chip, c), device_id_type=MESH).start()
        token[...] = jnp.zeros_like(token)

    outs = pl.pallas_call(
        body, name="gather_start", in_specs=[HBM] * n,
        out_specs=[SEM] * (2 * n) + [HBM] * n + [pl.BlockSpec(memory_space=pltpu.VMEM)],
        out_shape=[pltpu.SemaphoreType.DMA((3,))] * (2 * n) + [pltpu.HBM(b.shape, b.dtype) for b in bufs]
        + [jax.ShapeDtypeStruct((8, 128), F32)],
        input_output_aliases={a: 2 * n + a for a in range(n)},
        compiler_params=SPLIT_COPY,
    )(*[_in_hbm(b) for b in bufs])
    return outs[:n], outs[n:2 * n], outs[2 * n:3 * n], outs[-1]


def _gather_wait(name, buf, send_sem, recv_sem, after):
    def body(buf_ref, send_ref, recv_ref, after_ref, out_ref):
        x, y, c = _place()
        own = _chip_rows(buf_ref, 2 * x + y)
        for j, chip in enumerate(_other_chips(x, y)):
            copy = pltpu.make_async_remote_copy(src_ref=own, dst_ref=_chip_rows(buf_ref, 2 * chip[0] + chip[1]),
                                                send_sem=send_ref.at[j], recv_sem=recv_ref.at[j],
                                                device_id=(*chip, c), device_id_type=MESH)
            copy.wait_send()
            copy.wait_recv()

    return pl.pallas_call(
        body, name=name, in_specs=[HBM, SEM, SEM, ANY], out_specs=HBM, out_shape=pltpu.HBM(buf.shape, buf.dtype),
        input_output_aliases={0: 0}, compiler_params=SPLIT_COPY,
    )(buf, send_sem, recv_sem, after)


def _piece_rows(ref, k):
    p = ref.shape[0] // 8
    return ref.at[pl.ds(pl.multiple_of(k * p, 16), p), :]


def _exchange_start(name, arrays):
    n = len(arrays)
    zones = [lax.empty((7, a.shape[0] // 8, a.shape[1]), BF16) for a in arrays]

    def body(*refs):
        srcs, lands = refs[:n], refs[n:2 * n]
        send, recv, token = refs[2 * n:3 * n], refs[3 * n:4 * n], refs[-1]
        x, y, c = _place()
        for a, (src, land) in enumerate(zip(srcs, lands)):
            for m in range(1, 8):
                px, py, pc = _peer(x, y, c, m)
                pltpu.make_async_remote_copy(
                    src_ref=_piece_rows(src, 4 * px + 2 * py + pc), dst_ref=land.at[m - 1], send_sem=send[a].at[m - 1],
                    recv_sem=recv[a].at[m - 1], device_id=(px, py, pc), device_id_type=MESH).start()
        token[...] = jnp.zeros_like(token)

    outs = pl.pallas_call(
        body, name=name, in_specs=[HBM] * (2 * n),
        out_specs=[SEM] * (2 * n) + [HBM] * (2 * n) + [pl.BlockSpec(memory_space=pltpu.VMEM)],
        out_shape=[pltpu.SemaphoreType.DMA((7,))] * (2 * n) + [pltpu.HBM(a.shape, a.dtype) for a in arrays + zones]
        + [jax.ShapeDtypeStruct((8, 128), F32)],
        input_output_aliases={a: 2 * n + a for a in range(2 * n)},
        compiler_params=SPLIT_COPY,
    )(*[_in_hbm(a) for a in arrays + zones])
    return outs[:n], outs[n:2 * n], outs[2 * n:3 * n], outs[3 * n:4 * n], outs[-1]


def _exchange_wait(name, started, after):
    send_sems, recv_sems, arrays, zones, _ = started
    n = len(arrays)

    def body(*refs):
        srcs, lands = refs[:n], refs[n:2 * n]
        send, recv = refs[2 * n:3 * n], refs[3 * n:4 * n]
        x, y, c = _place()
        for a, (src, land) in enumerate(zip(srcs, lands)):
            for m in range(1, 8):
                px, py, pc = _peer(x, y, c, m)
                copy = pltpu.make_async_remote_copy(
                    src_ref=_piece_rows(src, 4 * px + 2 * py + pc), dst_ref=land.at[m - 1], send_sem=send[a].at[m - 1],
                    recv_sem=recv[a].at[m - 1], device_id=(px, py, pc), device_id_type=MESH)
                copy.wait_send()
                copy.wait_recv()

    outs = pl.pallas_call(
        body, name=name, in_specs=[HBM] * (2 * n) + [SEM] * (2 * n) + [ANY], out_specs=[HBM] * (2 * n),
        out_shape=[pltpu.HBM(a.shape, a.dtype) for a in list(arrays) + list(zones)],
        input_output_aliases={a: a for a in range(2 * n)}, compiler_params=SPLIT_COPY,
    )(*arrays, *zones, *send_sems, *recv_sems, after)
    return outs[n:]


def _sum_pieces(name, partial, recv, place_arr, tile, layer, dest=None):
    p, cols = recv.shape[1:]
    steps = p // tile

    def body(place_ref, o_ref, r_ref, *rest):
        total = o_ref[...]
        for m in range(7):
            total = total + r_ref[m].astype(F32)
        rest[-1][...] = total

    return pl.pallas_call(
        body, name=name,
        grid_spec=pltpu.PrefetchScalarGridSpec(
            num_scalar_prefetch=1, grid=(steps,),
            in_specs=[pl.BlockSpec((tile, cols), lambda i, place: (place[0] * steps + i, 0)),
                      pl.BlockSpec((7, tile, cols), lambda i, place: (0, i, 0))] + ([] if dest is None else [ANY]),
            out_specs=pl.BlockSpec((None, tile, cols), lambda i, place: (layer, place[1] * steps + i, 0))),
        out_shape=jax.ShapeDtypeStruct((DEPTH, 2 * p, cols), F32),
        input_output_aliases={} if dest is None else {3: 0},
        compiler_params=_params(),
    )(place_arr, partial, recv, *(() if dest is None else (dest,)))


def _swap_halves(g_in, g_out):
    bufs = [g_in, g_out]

    def body(*refs):
        ins, outs, send_sems, recv_sems = refs[:2], refs[2:4], refs[4], refs[5]
        x, y, c = _place()

        def half(ref, l, which):
            p = ref.shape[1] // 2
            return ref.at[l, pl.ds(pl.multiple_of(which * p, 8), p), :]

        def copy(a, l, which):
            return pltpu.make_async_remote_copy(
                src_ref=half(ins[a], l, which), dst_ref=half(outs[a], l, which), send_sem=send_sems.at[2 * a + l],
                recv_sem=recv_sems.at[2 * a + l], device_id=(x, y, 1 - c), device_id_type=MESH)

        for a in range(2):
            for l in range(DEPTH):
                copy(a, l, c).start()
        for a in range(2):
            for l in range(DEPTH):
                copy(a, l, c).wait_send()
                copy(a, l, 1 - c).wait_recv()

    return pl.pallas_call(
        body, name="swap_halves", in_specs=[ANY, ANY], out_specs=[ANY, ANY],
        out_shape=[jax.ShapeDtypeStruct(b.shape, F32) for b in bufs],
        input_output_aliases={0: 0, 1: 1},
        scratch_shapes=[pltpu.SemaphoreType.DMA((4,))] * 2,
    )(*bufs)


def _allreduce_small(packed):
    def body(p_ref, out_ref, recv_ref, send1, recv1, send2, recv2):
        x, y, c = _place()
        me = 4 * x + 2 * y + c

        def index(p):
            return 4 * p[0] + 2 * p[1] + p[2]

        def copy(src, dst, sems, m):
            return pltpu.make_async_remote_copy(src_ref=src, dst_ref=dst, send_sem=sems[0].at[m - 1],
                                                recv_sem=sems[1].at[m - 1], device_id=_peer(x, y, c, m),
                                                device_id_type=MESH)

        first = [copy(p_ref.at[index(_peer(x, y, c, m))], recv_ref.at[me], (send1, recv1), m) for m in range(1, 8)]
        for cp in first:
            cp.start()
        recv_ref[me] = p_ref[me]
        for m in range(1, 8):
            slot = recv_ref.at[index(_peer(x, y, c, m))]
            copy(slot, slot, (send1, recv1), m).wait_recv()
        total = recv_ref[0]
        for d in range(1, 8):
            total = total + recv_ref[d]
        out_ref[me] = total
        second = [copy(out_ref.at[me], out_ref.at[me], (send2, recv2), m) for m in range(1, 8)]
        for cp in second:
            cp.start()
        for m in range(1, 8):
            slot = out_ref.at[index(_peer(x, y, c, m))]
            copy(slot, slot, (send2, recv2), m).wait_recv()
        for cp in first + second:
            cp.wait_send()

    vmem = pl.BlockSpec(memory_space=pltpu.VMEM)
    return pl.pallas_call(
        body, name="allreduce_small", in_specs=[vmem], out_specs=vmem,
        out_shape=jax.ShapeDtypeStruct(packed.shape, F32),
        scratch_shapes=[pltpu.VMEM(packed.shape, F32)] + [pltpu.SemaphoreType.DMA((7,))] * 4,
    )(packed)


def _adamw(name, w, g, m, v, rows_per_step):
    layers, rows, cols = w.shape

    def body(w_ref, g_ref, m_ref, v_ref, d_ref, nm_ref, nv_ref):
        gt = g_ref[...]
        nm = ADAM_B1 * m_ref[...] + (1.0 - ADAM_B1) * gt
        nv = ADAM_B2 * v_ref[...] + (1.0 - ADAM_B2) * (gt * gt)
        m_hat = nm / (1.0 - ADAM_B1 ** ADAM_STEP)
        v_hat = nv / (1.0 - ADAM_B2 ** ADAM_STEP)
        d_ref[...] = -ADAM_LR * (m_hat / (jnp.sqrt(v_hat) + ADAM_EPS) + ADAM_WD * w_ref[...])
        nm_ref[...] = nm
        nv_ref[...] = nv

    spec = pl.BlockSpec((1, rows_per_step, cols), lambda l, i: (l, i, 0))
    shape = jax.ShapeDtypeStruct(w.shape, F32)
    return pl.pallas_call(
        body, name=name, grid=(layers, rows // rows_per_step),
        in_specs=[spec] * 4, out_specs=[spec] * 3, out_shape=[shape] * 3,
        compiler_params=_params(("arbitrary", "arbitrary")),
    )(w, g, m, v)


def _pack_small(pool_w, pool_scale, sinks, norm_pre, norm_post):
    sink_rows = jnp.zeros((ROWS_SINK, 128), F32).at[0, 0:2 * N_HEADS].set(sinks.reshape(-1))
    return jnp.concatenate([
        pool_w.reshape(ROWS_PW, 128), pool_scale.reshape(ROWS_SC, 128), norm_pre.reshape(ROWS_NORM, 128),
        norm_post.reshape(ROWS_NORM, 128), sink_rows, jnp.zeros((SMALL_ROWS - ROW_LOSS, 128), F32)], axis=0)


def _pack_small_grads(grads, loss):
    sink_rows = jnp.zeros((ROWS_SINK, 128), F32).at[0, 0:2 * N_HEADS].set(
        jnp.concatenate([g[4][:, 0] for g in grads]))
    return jnp.concatenate(
        [g[2].reshape(ROWS_PW // DEPTH, 128) for g in grads] + [g[3] for g in grads] + [g[5] for g in grads]
        + [g[6] for g in grads] + [sink_rows, loss, jnp.zeros((SMALL_ROWS - ROW_LOSS - 8, 128), F32)], axis=0)


def _unpack_small(packed):
    o = 0
    pool_w = packed[o:o + ROWS_PW].reshape(DEPTH, 4, 128, 128)
    o += ROWS_PW
    pool_scale = packed[o:o + ROWS_SC].reshape(DEPTH, 512)
    o += ROWS_SC
    norm_pre = packed[o:o + ROWS_NORM].reshape(DEPTH, D)
    o += ROWS_NORM
    norm_post = packed[o:o + ROWS_NORM].reshape(DEPTH, D)
    o += ROWS_NORM
    sinks = packed[o, 0:2 * N_HEADS].reshape(DEPTH, N_HEADS)
    return pool_w, pool_scale, sinks, norm_pre, norm_post


def kernel(x, w_in, pool_w, pool_scale, attn_sinks, w_out, norm_pre, norm_post, loss_target, m_w_in, m_pool_w, m_pool_scale, m_attn_sinks, m_w_out, m_norm_pre, m_norm_post, v_w_in, v_pool_w, v_pool_scale, v_attn_sinks, v_w_out, v_norm_pre, v_norm_post):
    cx, cy, cc = _place()
    chip_arr = jnp.reshape(2 * cx + cy, (1,)).astype(jnp.int32)
    place_arr = jnp.stack([4 * cx + 2 * cy + cc, cc]).astype(jnp.int32)
    t = lambda a: jnp.transpose(a, (0, 2, 1))
    w_in_t = t(w_in)
    xs, target = x[0], loss_target[0]
    pool_w_b = pool_w.astype(BF16)
    scale3 = pool_scale.reshape(DEPTH, 1, D_POOL)
    pre3 = norm_pre.reshape(DEPTH, 1, D)
    post3 = norm_post.reshape(DEPTH, 1, D)

    wi = _place_cast("place_w_in", w_in_t, chip_arr, 288)
    wo = _place_cast("place_w_out", w_out, chip_arr, 256)
    send, recv, bufs, token = _gather_start([wi[0], wo[0], wi[1], wo[1]])

    saved = []
    after = token
    for l in range(DEPTH):
        w_in_l = _gather_wait(f"gather_wait_in{l}", bufs[2 * l], send[2 * l], recv[2 * l], after)
        pu, pg, q, kv, ag = _fwd_in(l, xs, pre3, w_in_l)
        cat = _fwd_mix(l, pu, pg, q, kv, ag, pool_w_b, scale3, attn_sinks)
        w_out_l = _gather_wait(f"gather_wait_out{l}", bufs[2 * l + 1], send[2 * l + 1], recv[2 * l + 1], cat)
        if l < DEPTH - 1:
            y, x_next = _fwd_out(l, cat, w_out_l, xs, post3)
        else:
            y, x_next, loss = _fwd_out(l, cat, w_out_l, xs, post3, target)
        saved.append((xs, pu, pg, q, kv, ag, cat, y, w_in_l, w_out_l))
        xs = after = x_next

    x_in, pu, pg, q, kv, ag, cat, y, w_in_l, w_out_l = saved[1]
    dcat, dw_out1, dw_out1_b, dg_post1 = _bwd_out(1, xs, y, post3, cat, w_out_l)
    dproj, dpw1, dsc1, dsink1 = _bwd_mix(1, pu, pg, q, kv, ag, dcat, pool_w_b, scale3, attn_sinks)
    dx, dw_in1, dw_in1_b, dg_pre1 = _bwd_in(1, dproj, w_in_l, x_in, pre3, xs)
    ex1 = _exchange_start("exchange_start_1", [dw_in1_b, dw_out1_b])

    x_in, pu, pg, q, kv, ag, cat, y, w_in_l, w_out_l = saved[0]
    dcat, dw_out0, dw_out0_b, dg_post0 = _bwd_out(0, dx, y, post3, cat, w_out_l, deps=(ex1[4],))
    ex0_out = _exchange_start("exchange_start_out0", [dw_out0_b])
    dproj, dpw0, dsc0, dsink0 = _bwd_mix(0, pu, pg, q, kv, ag, dcat, pool_w_b, scale3, attn_sinks, deps=(ex0_out[4],))
    recv_in1, recv_out1 = _exchange_wait("exchange_wait_1", ex1, dproj)
    g_in = _sum_pieces("sum_pieces_in1", dw_in1, recv_in1, place_arr, 96, 1)
    g_out = _sum_pieces("sum_pieces_out1", dw_out1, recv_out1, place_arr, 64, 1)
    grad_x, dw_in0, dw_in0_b, dg_pre0 = _bwd_in(0, dproj, w_in_l, x_in, pre3, dx, deps=(g_in, g_out))
    ex0_in = _exchange_start("exchange_start_in0", [dw_in0_b])

    grads = [(None, None, dpw0, dsc0, dsink0, dg_pre0, dg_post0), (None, None, dpw1, dsc1, dsink1, dg_pre1, dg_post1)]
    packed = (_pack_small_grads(grads, loss) + ex0_in[4][0, 0]).reshape(8, PIECE_ROWS, 128)
    g_small = _allreduce_small(packed).reshape(1, SMALL_ROWS, 128)

    (recv_out0,) = _exchange_wait("exchange_wait_out0", ex0_out, g_small)
    g_out = _sum_pieces("sum_pieces_out0", dw_out0, recv_out0, place_arr, 64, 0, dest=g_out)
    (recv_in0,) = _exchange_wait("exchange_wait_in0", ex0_in, g_out)
    g_in = _sum_pieces("sum_pieces_in0", dw_in0, recv_in0, place_arr, 96, 0, dest=g_in)
    grad_w_in_t, grad_w_out = _swap_halves(g_in, g_out)

    d_in, nm_in, nv_in = _adamw("adamw_w_in", w_in_t, grad_w_in_t, t(m_w_in), t(v_w_in), 288)
    d_out, nm_out, nv_out = _adamw("adamw_w_out", w_out, grad_w_out, m_w_out, v_w_out, 256)
    small = lambda *a: _pack_small(*a).reshape(1, SMALL_ROWS, 128)
    d_s, nm_s, nv_s = _adamw(
        "adamw_small", small(pool_w, pool_scale, attn_sinks, norm_pre, norm_post), g_small,
        small(m_pool_w, m_pool_scale, m_attn_sinks, m_norm_pre, m_norm_post),
        small(v_pool_w, v_pool_scale, v_attn_sinks, v_norm_pre, v_norm_post), SMALL_ROWS)

    g_pw, g_sc, g_sk, g_pre, g_post = _unpack_small(g_small[0])
    d_pw, d_sc, d_sk, d_pre, d_post = _unpack_small(d_s[0])
    m_pw, m_sc, m_sk, m_pre, m_post = _unpack_small(nm_s[0])
    v_pw, v_sc, v_sk, v_pre, v_post = _unpack_small(nv_s[0])
    return (g_small[0, ROW_LOSS, 0], grad_x[None], t(grad_w_in_t), g_pw, g_sc, g_sk, grad_w_out, g_pre, g_post,
            t(d_in), d_pw, d_sc, d_sk, d_out, d_pre, d_post,
            t(nm_in), m_pw, m_sc, m_sk, nm_out, m_pre, m_post,
            t(nv_in), v_pw, v_sc, v_sk, nv_out, v_pre, v_post)
```

```python
import jax
import jax.numpy as jnp
from jax import lax
from jax.experimental import pallas as pl
from jax.experimental.pallas import tpu as pltpu

F32 = jnp.float32
BF16 = jnp.bfloat16

S = 2048
D = 1024
DEPTH = 2
D_POOL = 512
POOL_WINDOWS = (2, 4, 8, 16)
N_HEADS = 8
D_IN = 2304
N_SHARDS = 4
W_IN_SHARD = D_IN // N_SHARDS
W_OUT_SHARD = D // N_SHARDS
BLK = 128
NB = S // BLK
HALO = 16
EPS = 1e-6
NEG_INF = -1e30
C_PU, C_PG, C_Q, C_K, C_V, C_AG = 0, 512, 1024, 1536, 1664, 1792

ADAM_LR = 0.001
ADAM_B1 = 0.9
ADAM_B2 = 0.999
ADAM_EPS = 1e-08
ADAM_WD = 0.01
ADAM_STEP = 10

TM = 256
VMEM_LIMIT = 56 * 1024 * 1024

NT = (((1,), (1,)), ((), ()))
TN = (((0,), (0,)), ((), ()))

MESH = pl.DeviceIdType.MESH
ANY = pl.BlockSpec(memory_space=pl.ANY)

ROWS_PW, ROWS_SC, ROWS_NORM, ROWS_SINK = 1024, 8, 16, 8
ROW_LOSS = ROWS_PW + ROWS_SC + 2 * ROWS_NORM + ROWS_SINK
SMALL_ROWS = 1088
PIECE_ROWS = SMALL_ROWS // 8


def _params(sem=("arbitrary",)):
    return pltpu.CompilerParams(dimension_semantics=sem, vmem_limit_bytes=VMEM_LIMIT)


def _sigmoid(v):
    return 1.0 / (1.0 + jnp.exp(-v))


def _rows8(v):
    r, c = v.shape
    return v.reshape(r // 8, 8, c).sum(axis=0)


def _layer(l, *shape):
    zeros = (0,) * len(shape)
    return pl.BlockSpec((None,) + shape, lambda i: (l,) + zeros)


def _fwd_in(l, x, g_pre, w_in_t):
    def body(x_ref, g_ref, w_ref, pu_ref, pg_ref, q_ref, kv_ref, ag_ref):
        xt = x_ref[...]
        r = lax.rsqrt(jnp.mean(xt * xt, axis=-1, keepdims=True) + EPS)
        h = (xt * r * g_ref[...]).astype(BF16)

        def proj(lo, hi):
            return lax.dot_general(h, w_ref[lo:hi, :], NT, preferred_element_type=F32)

        pu_ref[...] = proj(C_PU, C_PG)
        pg_ref[...] = proj(C_PG, C_Q)
        q_ref[...] = proj(C_Q, C_K).astype(BF16)
        kv_ref[...] = proj(C_K, C_AG).astype(BF16)
        ag_ref[...] = proj(C_AG, D_IN)

    row = lambda w: pl.BlockSpec((TM, w), lambda i: (i, 0))
    return pl.pallas_call(
        body, name="fwd_in", grid=(S // TM,),
        in_specs=[row(D), _layer(l, 1, D), pl.BlockSpec((D_IN, D), lambda i: (0, 0))],
        out_specs=[row(512), row(512), row(512), row(256), row(512)],
        out_shape=[jax.ShapeDtypeStruct((S, 512), F32), jax.ShapeDtypeStruct((S, 512), F32),
                   jax.ShapeDtypeStruct((S, 512), BF16), jax.ShapeDtypeStruct((S, 256), BF16),
                   jax.ShapeDtypeStruct((S, 512), F32)],
        compiler_params=_params(),
    )(x, g_pre, w_in_t)


def _mask_terms(i):
    qi = lax.broadcasted_iota(jnp.int32, (BLK, 2 * BLK), 0)
    kj = lax.broadcasted_iota(jnp.int32, (BLK, 2 * BLK), 1)
    dist = qi + BLK - kj
    valid = (dist >= 0) & (dist < BLK) & ((kj >= BLK) | (i > 0))
    return dist.astype(F32), valid


def _head_variants(cur, prev):
    both = jnp.concatenate([prev, cur], axis=0).astype(F32)
    swapped = pltpu.roll(both, 64, axis=1)
    low = lax.broadcasted_iota(jnp.int32, both.shape, 1) < 64
    zero = jnp.zeros_like(both)
    return ((jnp.where(low, both, zero).astype(BF16), jnp.where(low, zero, swapped).astype(BF16)),
            (jnp.where(low, swapped, zero).astype(BF16), jnp.where(low, zero, both).astype(BF16)))


def _probs(q_tile, k_var, slope, sink, dist, valid):
    s = lax.dot_general(q_tile, k_var, NT, preferred_element_type=F32)
    s = jnp.where(valid, s * 0.125 - slope * dist, NEG_INF)
    m = jnp.maximum(jnp.max(s, axis=-1, keepdims=True), sink)
    p = jnp.exp(s - m)
    e_sink = jnp.exp(sink - m)
    inv = 1.0 / (jnp.sum(p, axis=-1, keepdims=True) + e_sink)
    return p * inv, e_sink * inv


def _pool_block(ext_ref, i, g, w):
    lanes = slice(g * 128, (g + 1) * 128)
    u = ext_ref[HALO:HALO + BLK, lanes]
    acc = u
    for j in range(1, w):
        acc = acc + ext_ref[HALO - j:HALO - j + BLK, lanes]
    t = (i * BLK + lax.broadcasted_iota(jnp.int32, (BLK, 1), 0)).astype(F32)
    inv = 1.0 / jnp.minimum(t + 1.0, float(w))
    return acc * inv - u, inv


def _fwd_mix(l, pu, pg, q, kv, ag, pool_w, pool_scale, sinks):
    def body(pu_ref, pup_ref, pg_ref, q_ref, kv_ref, kvp_ref, ag_ref, pw_ref, sc_ref, sink_ref, cat_ref, ext_ref):
        i = pl.program_id(0)
        ext_ref[0:HALO, :] = jnp.where(i > 0, pup_ref[...], 0.0)
        ext_ref[HALO:HALO + BLK, :] = pu_ref[...]
        for g, w in enumerate(POOL_WINDOWS):
            lanes = slice(g * 128, (g + 1) * 128)
            pooled, _ = _pool_block(ext_ref, i, g, w)
            mixed = jnp.dot(pooled.astype(BF16), pw_ref[g], preferred_element_type=F32)
            gate = pg_ref[:, lanes]
            cat_ref[:, lanes] = (mixed * sc_ref[:, lanes] * (gate * _sigmoid(gate))).astype(BF16)

        dist, valid = _mask_terms(i)
        k_var = _head_variants(kv_ref[:, 0:128], kvp_ref[:, 0:128])
        v_var = _head_variants(kv_ref[:, 128:256], kvp_ref[:, 128:256])
        for j in range(4):
            hkv = j // 2
            lanes = slice(j * 128, (j + 1) * 128)
            q_tile = q_ref[:, lanes]
            o = jnp.zeros((BLK, 128), F32)
            for half in range(2):
                head = hkv * 4 + 2 * (j % 2) + half
                p, _ = _probs(q_tile, k_var[hkv][half], 2.0 ** -(head + 1), sink_ref[l, head], dist, valid)
                o = o + jnp.dot(p.astype(BF16), v_var[hkv][half], preferred_element_type=F32)
            gate = ag_ref[:, lanes]
            cat_ref[:, D_POOL + j * 128:D_POOL + (j + 1) * 128] = (o * (gate * _sigmoid(gate))).astype(BF16)

    blk = lambda w: pl.BlockSpec((BLK, w), lambda i: (i, 0))
    prev = lambda w: pl.BlockSpec((BLK, w), lambda i: (jnp.maximum(i - 1, 0), 0))
    halo = pl.BlockSpec((HALO, 512), lambda i: (jnp.maximum(i * (BLK // HALO) - 1, 0), 0))
    return pl.pallas_call(
        body, name="fwd_mix", grid=(NB,),
        in_specs=[blk(512), halo, blk(512), blk(512), blk(256), prev(256), blk(512),
                  _layer(l, 4, 128, 128), _layer(l, 1, 512), pl.BlockSpec(memory_space=pltpu.SMEM)],
        out_specs=blk(D),
        out_shape=jax.ShapeDtypeStruct((S, D), BF16),
        scratch_shapes=[pltpu.VMEM((HALO + BLK, 512), F32)],
        compiler_params=_params(),
    )(pu, pu, pg, q, kv, kv, ag, pool_w, pool_scale, sinks)


def _fwd_out(l, cat, w_out, x, g_post, target=None):
    last = target is not None
    n_steps = S // TM

    def body(*refs):
        if last:
            cat_ref, w_ref, x_ref, g_ref, t_ref, y_ref, dx_ref, loss_ref, acc_ref = refs
        else:
            cat_ref, w_ref, x_ref, g_ref, y_ref, xn_ref = refs
        y = jnp.dot(cat_ref[...], w_ref[...], preferred_element_type=F32)
        y_ref[...] = y
        r = lax.rsqrt(jnp.mean(y * y, axis=-1, keepdims=True) + EPS)
        xn = x_ref[...] + y * r * g_ref[...]
        if not last:
            xn_ref[...] = xn
            return
        step = pl.program_id(0)
        err = xn - t_ref[...]
        dx_ref[...] = err * (1.0 / D)

        @pl.when(step == 0)
        def _():
            acc_ref[...] = jnp.zeros_like(acc_ref)

        acc_ref[...] += _rows8(err * err)

        @pl.when(step == n_steps - 1)
        def _():
            loss_ref[...] = jnp.full((8, 128), (0.5 / D) * jnp.sum(acc_ref[...]), F32)

    row = lambda: pl.BlockSpec((TM, D), lambda i: (i, 0))
    act = jax.ShapeDtypeStruct((S, D), F32)
    in_specs = [row(), pl.BlockSpec((D, D), lambda i: (0, 0)), row(), _layer(l, 1, D)]
    args = [cat, w_out, x, g_post]
    if last:
        return pl.pallas_call(
            body, name="fwd_out_loss", grid=(n_steps,),
            in_specs=in_specs + [row()],
            out_specs=[row(), row(), pl.BlockSpec((8, 128), lambda i: (0, 0))],
            out_shape=[act, act, jax.ShapeDtypeStruct((8, 128), F32)],
            scratch_shapes=[pltpu.VMEM((8, D), F32)],
            compiler_params=_params(),
        )(*args, target)
    return pl.pallas_call(
        body, name="fwd_out", grid=(n_steps,),
        in_specs=in_specs, out_specs=[row(), row()], out_shape=[act, act],
        compiler_params=_params(),
    )(*args)


def _store_lane_rows(ref, acc):
    total = jnp.sum(acc, axis=0, keepdims=True)
    for k in range(ref.shape[0]):
        ref[k:k + 1, :] = total[:, k * 128:(k + 1) * 128]


def _bwd_out(l, dxn, y, g_post, cat, w_out, deps=()):
    n_steps = S // TM

    def body(dz_ref, y_ref, g_ref, cat_ref, w_ref, *rest):
        dcat_ref, dw_ref, dwb_ref, dg_ref, acc_ref = rest[len(deps):]
        step = pl.program_id(0)

        @pl.when(step == 0)
        def _():
            dw_ref[...] = jnp.zeros_like(dw_ref)
            acc_ref[...] = jnp.zeros_like(acc_ref)

        y = y_ref[...]
        dz = dz_ref[...]
        r = lax.rsqrt(jnp.mean(y * y, axis=-1, keepdims=True) + EPS)
        a = dz * g_ref[...]
        dy = r * a - y * (r * r * r) * jnp.mean(a * y, axis=-1, keepdims=True)
        acc_ref[...] += _rows8(dz * (y * r))
        dyb = dy.astype(BF16)
        dcat_ref[...] = lax.dot_general(dyb, w_ref[...], NT, preferred_element_type=F32)
        dw_ref[...] += lax.dot_general(cat_ref[...], dyb, TN, preferred_element_type=F32)

        @pl.when(step == n_steps - 1)
        def _():
            _store_lane_rows(dg_ref, acc_ref[...])
            dwb_ref[...] = dw_ref[...].astype(BF16)

    row = lambda: pl.BlockSpec((TM, D), lambda i: (i, 0))
    full = lambda shape: pl.BlockSpec(shape, lambda i: (0, 0))
    return pl.pallas_call(
        body, name="bwd_out", grid=(n_steps,),
        in_specs=[row(), row(), _layer(l, 1, D), row(), full((D, D))] + [ANY] * len(deps),
        out_specs=[row(), full((D, D)), full((D, D)), full((8, 128))],
        out_shape=[jax.ShapeDtypeStruct((S, D), F32), jax.ShapeDtypeStruct((D, D), F32),
                   jax.ShapeDtypeStruct((D, D), BF16), jax.ShapeDtypeStruct((8, 128), F32)],
        scratch_shapes=[pltpu.VMEM((8, D), F32)],
        compiler_params=_params(),
    )(dxn, y, g_post, cat, w_out, *deps)


def _bwd_mix(l, pu, pg, q, kv, ag, dcat, pool_w, pool_scale, sinks, deps=()):
    def body(pu_ref, pup_ref, pg_ref, q_ref, kv_ref, kvp_ref, ag_ref, dcat_ref, pw_ref, sc_ref, sink_ref, *rest):
        dproj_ref, dpw_ref, dsc_ref, dsink_ref, ext_ref, dext_ref, dkv_ref = rest[len(deps):]
        step = pl.program_id(0)
        i = NB - 1 - step

        @pl.when(step == 0)
        def _():
            dpw_ref[...] = jnp.zeros_like(dpw_ref)
            dsc_ref[...] = jnp.zeros_like(dsc_ref)
            dsink_ref[...] = jnp.zeros_like(dsink_ref)
            dext_ref[BLK:BLK + HALO, :] = jnp.zeros((HALO, 512), F32)
            dkv_ref[...] = jnp.zeros_like(dkv_ref)

        ext_ref[0:HALO, :] = jnp.where(i > 0, pup_ref[...], 0.0)
        ext_ref[HALO:HALO + BLK, :] = pu_ref[...]
        for g, w in enumerate(POOL_WINDOWS):
            lanes = slice(g * 128, (g + 1) * 128)
            pooled, inv = _pool_block(ext_ref, i, g, w)
            pooled_b = pooled.astype(BF16)
            mixed = jnp.dot(pooled_b, pw_ref[g], preferred_element_type=F32)
            scale = sc_ref[:, lanes]
            gate = pg_ref[:, lanes]
            sg = _sigmoid(gate)
            dpo = dcat_ref[:, lanes]
            dproj_ref[:, C_PG + g * 128:C_PG + (g + 1) * 128] = (
                dpo * (mixed * scale) * (sg * (1.0 + gate * (1.0 - sg)))).astype(BF16)
            dms = dpo * (gate * sg)
            dsc_ref[g:g + 1, :] += jnp.sum(dms * mixed, axis=0, keepdims=True)
            dmixed = (dms * scale).astype(BF16)
            dpw_ref[g] += lax.dot_general(pooled_b, dmixed, TN, preferred_element_type=F32)
            dpooled = lax.dot_general(dmixed, pw_ref[g], NT, preferred_element_type=F32)
            dext_ref[0:BLK, lanes] = dpooled * inv
            acc = dext_ref[0:BLK, lanes]
            for j in range(1, w):
                acc = acc + dext_ref[j:j + BLK, lanes]
            dproj_ref[:, C_PU + g * 128:C_PU + (g + 1) * 128] = (acc - dpooled).astype(BF16)
        dext_ref[BLK:BLK + HALO, :] = dext_ref[0:HALO, :]

        dist, valid = _mask_terms(i)
        k_var = _head_variants(kv_ref[:, 0:128], kvp_ref[:, 0:128])
        v_var = _head_variants(kv_ref[:, 128:256], kvp_ref[:, 128:256])
        zero = jnp.zeros((2 * BLK, 128), F32)
        dk_acc = [[zero, zero], [zero, zero]]
        dv_acc = [[zero, zero], [zero, zero]]
        for j in range(4):
            hkv = j // 2
            lanes = slice(j * 128, (j + 1) * 128)
            q_tile = q_ref[:, lanes]
            gate = ag_ref[:, lanes]
            sg = _sigmoid(gate)
            dca = dcat_ref[:, D_POOL + j * 128:D_POOL + (j + 1) * 128]
            do_b = (dca * (gate * sg)).astype(BF16)
            probs = []
            o = jnp.zeros((BLK, 128), F32)
            for half in range(2):
                head = hkv * 4 + 2 * (j % 2) + half
                p, p_sink = _probs(q_tile, k_var[hkv][half], 2.0 ** -(head + 1), sink_ref[l, head], dist, valid)
                probs.append((p, p_sink))
                o = o + jnp.dot(p.astype(BF16), v_var[hkv][half], preferred_element_type=F32)
            dproj_ref[:, C_AG + j * 128:C_AG + (j + 1) * 128] = (
                dca * o * (sg * (1.0 + gate * (1.0 - sg)))).astype(BF16)
            dq = jnp.zeros((BLK, 128), F32)
            for half in range(2):
                head = hkv * 4 + 2 * (j % 2) + half
                p, p_sink = probs[half]
                dp = lax.dot_general(do_b, v_var[hkv][half], NT, preferred_element_type=F32)
                delta = jnp.sum(p * dp, axis=-1, keepdims=True)
                ds_b = (p * (dp - delta) * 0.125).astype(BF16)
                dsink_ref[head:head + 1, :] += jnp.broadcast_to(
                    -jnp.sum(p_sink * delta, axis=0, keepdims=True), (1, 128))
                dq = dq + jnp.dot(ds_b, k_var[hkv][half], preferred_element_type=F32)
                dk_acc[hkv][half] = dk_acc[hkv][half] + lax.dot_general(
                    ds_b, q_tile, TN, preferred_element_type=F32)
                dv_acc[hkv][half] = dv_acc[hkv][half] + lax.dot_general(
                    p.astype(BF16), do_b, TN, preferred_element_type=F32)
            dproj_ref[:, C_Q + j * 128:C_Q + (j + 1) * 128] = dq.astype(BF16)

        low = lax.broadcasted_iota(jnp.int32, (2 * BLK, 128), 1) < 64

        def gather_heads(acc):
            return jnp.where(low, acc[0][0] + pltpu.roll(acc[0][1], 64, axis=1),
                             pltpu.roll(acc[1][0], 64, axis=1) + acc[1][1])

        dk = gather_heads(dk_acc)
        dv = gather_heads(dv_acc)
        dproj_ref[:, C_K:C_V] = (dk[BLK:, :] + dkv_ref[:, 0:128]).astype(BF16)
        dproj_ref[:, C_V:C_AG] = (dv[BLK:, :] + dkv_ref[:, 128:256]).astype(BF16)
        dkv_ref[:, 0:128] = dk[:BLK, :]
        dkv_ref[:, 128:256] = dv[:BLK, :]

    rev = lambda w: pl.BlockSpec((BLK, w), lambda s: (NB - 1 - s, 0))
    prev = lambda w: pl.BlockSpec((BLK, w), lambda s: (jnp.maximum(NB - 2 - s, 0), 0))
    halo = pl.BlockSpec((HALO, 512), lambda s: (jnp.maximum((NB - 1 - s) * (BLK // HALO) - 1, 0), 0))
    return pl.pallas_call(
        body, name="bwd_mix", grid=(NB,),
        in_specs=[rev(512), halo, rev(512), rev(512), rev(256), prev(256), rev(512), rev(D),
                  _layer(l, 4, 128, 128), _layer(l, 1, 512), pl.BlockSpec(memory_space=pltpu.SMEM)] + [ANY] * len(deps),
        out_specs=[rev(D_IN), pl.BlockSpec((4, 128, 128), lambda s: (0, 0, 0)),
                   pl.BlockSpec((4, 128), lambda s: (0, 0)), pl.BlockSpec((8, 128), lambda s: (0, 0))],
        out_shape=[jax.ShapeDtypeStruct((S, D_IN), BF16), jax.ShapeDtypeStruct((4, 128, 128), F32),
                   jax.ShapeDtypeStruct((4, 128), F32), jax.ShapeDtypeStruct((8, 128), F32)],
        scratch_shapes=[pltpu.VMEM((HALO + BLK, 512), F32), pltpu.VMEM((BLK + HALO, 512), F32),
                        pltpu.VMEM((BLK, 256), F32)],
        compiler_params=_params(),
    )(pu, pu, pg, q, kv, kv, ag, dcat, pool_w, pool_scale, sinks, *deps)


def _bwd_in(l, dproj, w_in_t, x, g_pre, dres, deps=()):
    n_steps = S // TM

    def body(dp_ref, w_ref, x_ref, g_ref, dres_ref, *rest):
        dx_ref, dw_ref, dwb_ref, dg_ref, acc_ref = rest[len(deps):]
        step = pl.program_id(0)

        @pl.when(step == 0)
        def _():
            dw_ref[...] = jnp.zeros_like(dw_ref)
            acc_ref[...] = jnp.zeros_like(acc_ref)

        dp = dp_ref[...]
        dh = jnp.dot(dp, w_ref[...], preferred_element_type=F32)
        xt = x_ref[...]
        r = lax.rsqrt(jnp.mean(xt * xt, axis=-1, keepdims=True) + EPS)
        xn = xt * r
        g = g_ref[...]
        acc_ref[...] += _rows8(dh * xn)
        a = dh * g
        dx_ref[...] = dres_ref[...] + (r * a - xt * (r * r * r) * jnp.mean(a * xt, axis=-1, keepdims=True))
        dw_ref[...] += lax.dot_general(dp, (xn * g).astype(BF16), TN, preferred_element_type=F32)

        @pl.when(step == n_steps - 1)
        def _():
            _store_lane_rows(dg_ref, acc_ref[...])
            dwb_ref[...] = dw_ref[...].astype(BF16)

    row = lambda w: pl.BlockSpec((TM, w), lambda i: (i, 0))
    full = lambda shape: pl.BlockSpec(shape, lambda i: (0, 0))
    return pl.pallas_call(
        body, name="bwd_in", grid=(n_steps,),
        in_specs=[row(D_IN), full((D_IN, D)), row(D), _layer(l, 1, D), row(D)] + [ANY] * len(deps),
        out_specs=[row(D), full((D_IN, D)), full((D_IN, D)), full((8, 128))],
        out_shape=[jax.ShapeDtypeStruct((S, D), F32), jax.ShapeDtypeStruct((D_IN, D), F32),
                   jax.ShapeDtypeStruct((D_IN, D), BF16), jax.ShapeDtypeStruct((8, 128), F32)],
        scratch_shapes=[pltpu.VMEM((8, D), F32)],
        compiler_params=_params(),
    )(dproj, w_in_t, x, g_pre, dres, *deps)


def _bwd_in_dw(l, dproj, x, g_pre, deps=()):
    n_steps = S // TM

    def body(dp_ref, x_ref, g_ref, *rest):
        dw_ref, dwb_ref = rest[len(deps):]
        step = pl.program_id(0)

        @pl.when(step == 0)
        def _():
            dw_ref[...] = jnp.zeros_like(dw_ref)

        xt = x_ref[...]
        r = lax.rsqrt(jnp.mean(xt * xt, axis=-1, keepdims=True) + EPS)
        h = (xt * r * g_ref[...]).astype(BF16)
        dw_ref[...] += lax.dot_general(dp_ref[...], h, TN, preferred_element_type=F32)

        @pl.when(step == n_steps - 1)
        def _():
            dwb_ref[...] = dw_ref[...].astype(BF16)

    row = lambda w: pl.BlockSpec((TM, w), lambda i: (i, 0))
    full = lambda shape: pl.BlockSpec(shape, lambda i: (0, 0))
    return pl.pallas_call(
        body, name="bwd_in_dw", grid=(n_steps,),
        in_specs=[row(D_IN), row(D), _layer(l, 1, D)] + [ANY] * len(deps),
        out_specs=[full((D_IN, D)), full((D_IN, D))],
        out_shape=[jax.ShapeDtypeStruct((D_IN, D), F32), jax.ShapeDtypeStruct((D_IN, D), BF16)],
        compiler_params=_params(),
    )(dproj, x, g_pre, *deps)


def _bwd_in_dx(l, dproj, w_in_t, x, g_pre, dres, deps=()):
    n_steps = S // TM

    def body(dp_ref, w_ref, x_ref, g_ref, dres_ref, *rest):
        dx_ref, dg_ref, acc_ref = rest[len(deps):]
        step = pl.program_id(0)

        @pl.when(step == 0)
        def _():
            acc_ref[...] = jnp.zeros_like(acc_ref)

        dh = jnp.dot(dp_ref[...], w_ref[...], preferred_element_type=F32)
        xt = x_ref[...]
        r = lax.rsqrt(jnp.mean(xt * xt, axis=-1, keepdims=True) + EPS)
        xn = xt * r
        acc_ref[...] += _rows8(dh * xn)
        a = dh * g_ref[...]
        dx_ref[...] = dres_ref[...] + (r * a - xt * (r * r * r) * jnp.mean(a * xt, axis=-1, keepdims=True))

        @pl.when(step == n_steps - 1)
        def _():
            _store_lane_rows(dg_ref, acc_ref[...])

    row = lambda w: pl.BlockSpec((TM, w), lambda i: (i, 0))
    full = lambda shape: pl.BlockSpec(shape, lambda i: (0, 0))
    return pl.pallas_call(
        body, name="bwd_in_dx", grid=(n_steps,),
        in_specs=[row(D_IN), full((D_IN, D)), row(D), _layer(l, 1, D), row(D)] + [ANY] * len(deps),
        out_specs=[row(D), full((8, 128))],
        out_shape=[jax.ShapeDtypeStruct((S, D), F32), jax.ShapeDtypeStruct((8, 128), F32)],
        scratch_shapes=[pltpu.VMEM((8, D), F32)],
        compiler_params=_params(),
    )(dproj, w_in_t, x, g_pre, dres, *deps)


HBM =pl.BlockSpec(memory_space=pltpu.HBM)
SEM = pl.BlockSpec(memory_space=pltpu.SEMAPHORE)
SPLIT_COPY = pltpu.CompilerParams(has_side_effects=pltpu.SideEffectType.DATAFLOW_SIDE_EFFECTING)


def _in_hbm(a):
    return pltpu.with_memory_space_constraint(a, pltpu.HBM)

def _place():
    return lax.axis_index("x"), lax.axis_index("y"), lax.axis_index("c")


def _other_chips(x, y):
    return [(1 - x, y), (x, 1 - y), (1 - x, 1 - y)]


def _peer(x, y, c, m):
    return (x ^ (m >> 2), y ^ ((m >> 1) & 1), c ^ (m & 1))


def _place_cast(name, src, chip_arr, tile):
    _, n, cols = src.shape
    steps = n // tile

    def body(chip_ref, s0_ref, s1_ref, o0_ref, o1_ref):
        o0_ref[...] = s0_ref[...].astype(BF16)
        o1_ref[...] = s1_ref[...].astype(BF16)

    return pl.pallas_call(
        body, name=name,
        grid_spec=pltpu.PrefetchScalarGridSpec(
            num_scalar_prefetch=1, grid=(steps,),
            in_specs=[pl.BlockSpec((None, tile, cols), lambda i, chip: (0, i, 0)),
                      pl.BlockSpec((None, tile, cols), lambda i, chip: (1, i, 0))],
            out_specs=[pl.BlockSpec((tile, cols), lambda i, chip: (chip[0] * steps + i, 0))] * 2),
        out_shape=[jax.ShapeDtypeStruct((N_SHARDS * n, cols), BF16)] * 2,
        compiler_params=_params(),
    )(chip_arr, src, src)


def _chip_rows(ref, chip):
    n = ref.shape[0] // N_SHARDS
    return ref.at[pl.ds(pl.multiple_of(chip * n, 16), n), :]


def _gather_start(bufs):
    n = len(bufs)

    def body(*refs):
        ins, send, recv, token = refs[:n], refs[n:2 * n], refs[2 * n:3 * n], refs[-1]
        x, y, c = _place()
        for a, buf in enumerate(ins):
            own = _chip_rows(buf, 2 * x + y)
            for j, chip in enumerate(_other_chips(x, y)):
                pltpu.make_async_remote_copy(src_ref=own, dst_ref=own, send_sem=send[a].at[j], recv_sem=recv[a].at[j],
                                             device_id=(*chip, c), device_id_type=MESH).start()
        token[...] = jnp.zeros_like(token)

    outs = pl.pallas_call(
        body, name="gather_start", in_specs=[HBM] * n,
        out_specs=[SEM] * (2 * n) + [HBM] * n + [pl.BlockSpec(memory_space=pltpu.VMEM)],
        out_shape=[pltpu.SemaphoreType.DMA((3,))] * (2 * n) + [pltpu.HBM(b.shape, b.dtype) for b in bufs]
        + [jax.ShapeDtypeStruct((8, 128), F32)],
        input_output_aliases={a: 2 * n + a for a in range(n)},
        compiler_params=SPLIT_COPY,
    )(*[_in_hbm(b) for b in bufs])
    return outs[:n], outs[n:2 * n], outs[2 * n:3 * n], outs[-1]


def _gather_wait(name, buf, send_sem, recv_sem, after):
    def body(buf_ref, send_ref, recv_ref, after_ref, out_ref):
        x, y, c = _place()
        own = _chip_rows(buf_ref, 2 * x + y)
        for j, chip in enumerate(_other_chips(x, y)):
            copy = pltpu.make_async_remote_copy(src_ref=own, dst_ref=_chip_rows(buf_ref, 2 * chip[0] + chip[1]),
                                                send_sem=send_ref.at[j], recv_sem=recv_ref.at[j],
                                                device_id=(*chip, c), device_id_type=MESH)
            copy.wait_send()
            copy.wait_recv()

    return pl.pallas_call(
        body, name=name, in_specs=[HBM, SEM, SEM, ANY], out_specs=HBM, out_shape=pltpu.HBM(buf.shape, buf.dtype),
        input_output_aliases={0: 0}, compiler_params=SPLIT_COPY,
    )(buf, send_sem, recv_sem, after)


def _piece_rows(ref, k):
    p = ref.shape[0] // 8
    return ref.at[pl.ds(pl.multiple_of(k * p, 16), p), :]


def _exchange_start(name, arrays):
    n = len(arrays)
    zones = [lax.empty((7, a.shape[0] // 8, a.shape[1]), BF16) for a in arrays]

    def body(*refs):
        srcs, lands = refs[:n], refs[n:2 * n]
        send, recv, token = refs[2 * n:3 * n], refs[3 * n:4 * n], refs[-1]
        x, y, c = _place()
        for a, (src, land) in enumerate(zip(srcs, lands)):
            for m in range(1, 8):
                px, py, pc = _peer(x, y, c, m)
                pltpu.make_async_remote_copy(
                    src_ref=_piece_rows(src, 4 * px + 2 * py + pc), dst_ref=land.at[m - 1], send_sem=send[a].at[m - 1],
                    recv_sem=recv[a].at[m - 1], device_id=(px, py, pc), device_id_type=MESH).start()
        token[...] = jnp.zeros_like(token)

    outs = pl.pallas_call(
        body, name=name, in_specs=[HBM] * (2 * n),
        out_specs=[SEM] * (2 * n) + [HBM] * (2 * n) + [pl.BlockSpec(memory_space=pltpu.VMEM)],
        out_shape=[pltpu.SemaphoreType.DMA((7,))] * (2 * n) + [pltpu.HBM(a.shape, a.dtype) for a in arrays + zones]
        + [jax.ShapeDtypeStruct((8, 128), F32)],
        input_output_aliases={a: 2 * n + a for a in range(2 * n)},
        compiler_params=SPLIT_COPY,
    )(*[_in_hbm(a) for a in arrays + zones])
    return outs[:n], outs[n:2 * n], outs[2 * n:3 * n], outs[3 * n:4 * n], outs[-1]


def _exchange_wait(name, started, after):
    send_sems, recv_sems, arrays, zones, _ = started
    n = len(arrays)

    def body(*refs):
        srcs, lands = refs[:n], refs[n:2 * n]
        send, recv = refs[2 * n:3 * n], refs[3 * n:4 * n]
        x, y, c = _place()
        for a, (src, land) in enumerate(zip(srcs, lands)):
            for m in range(1, 8):
                px, py, pc = _peer(x, y, c, m)
                copy = pltpu.make_async_remote_copy(
                    src_ref=_piece_rows(src, 4 * px + 2 * py + pc), dst_ref=land.at[m - 1], send_sem=send[a].at[m - 1],
                    recv_sem=recv[a].at[m - 1], device_id=(px, py, pc), device_id_type=MESH)
                copy.wait_send()
                copy.wait_recv()

    outs = pl.pallas_call(
        body, name=name, in_specs=[HBM] * (2 * n) + [SEM] * (2 * n) + [ANY], out_specs=[HBM] * (2 * n),
        out_shape=[pltpu.HBM(a.shape, a.dtype) for a in list(arrays) + list(zones)],
        input_output_aliases={a: a for a in range(2 * n)}, compiler_params=SPLIT_COPY,
    )(*arrays, *zones, *send_sems, *recv_sems, after)
    return outs[n:]


def _sum_pieces(name, partial, recv, place_arr, tile, layer, dest=None):
    p, cols = recv.shape[1:]
    steps = p // tile

    def body(place_ref, o_ref, r_ref, *rest):
        total = o_ref[...]
        for m in range(7):
            total = total + r_ref[m].astype(F32)
        rest[-1][...] = total

    return pl.pallas_call(
        body, name=name,
        grid_spec=pltpu.PrefetchScalarGridSpec(
            num_scalar_prefetch=1, grid=(steps,),
            in_specs=[pl.BlockSpec((tile, cols), lambda i, place: (place[0] * steps + i, 0)),
                      pl.BlockSpec((7, tile, cols), lambda i, place: (0, i, 0))] + ([] if dest is None else [ANY]),
            out_specs=pl.BlockSpec((None, tile, cols), lambda i, place: (layer, place[1] * steps + i, 0))),
        out_shape=jax.ShapeDtypeStruct((DEPTH, 2 * p, cols), F32),
        input_output_aliases={} if dest is None else {3: 0},
        compiler_params=_params(),
    )(place_arr, partial, recv, *(() if dest is None else (dest,)))


def _swap_halves(name, bufs, parts):
    n = len(bufs)

    def body(*refs):
        ins, outs, send_sems, recv_sems = refs[:n], refs[n:2 * n], refs[2 * n], refs[2 * n + 1]
        x, y, c = _place()

        def half(ref, l, which):
            p = ref.shape[1] // 2
            return ref.at[l, pl.ds(pl.multiple_of(which * p, 8), p), :]

        def copy(k, which):
            a, l = parts[k]
            return pltpu.make_async_remote_copy(
                src_ref=half(ins[a], l, which), dst_ref=half(outs[a], l, which), send_sem=send_sems.at[k],
                recv_sem=recv_sems.at[k], device_id=(x, y, 1 - c), device_id_type=MESH)

        for k in range(len(parts)):
            copy(k, c).start()
        for k in range(len(parts)):
            copy(k, c).wait_send()
            copy(k, 1 - c).wait_recv()

    return pl.pallas_call(
        body, name=name, in_specs=[ANY] * n, out_specs=[ANY] * n,
        out_shape=[jax.ShapeDtypeStruct(b.shape, F32) for b in bufs],
        input_output_aliases={a: a for a in range(n)},
        scratch_shapes=[pltpu.SemaphoreType.DMA((len(parts),))] * 2,
    )(*bufs)


def _allreduce_small(packed, deps=()):
    def body(p_ref, *rest):
        out_ref, recv_ref, send1, recv1, send2, recv2 = rest[len(deps):]
        x, y, c = _place()
        me = 4 * x + 2 * y + c

        def index(p):
            return 4 * p[0] + 2 * p[1] + p[2]

        def copy(src, dst, sems, m):
            return pltpu.make_async_remote_copy(src_ref=src, dst_ref=dst, send_sem=sems[0].at[m - 1],
                                                recv_sem=sems[1].at[m - 1], device_id=_peer(x, y, c, m),
                                                device_id_type=MESH)

        first = [copy(p_ref.at[index(_peer(x, y, c, m))], recv_ref.at[me], (send1, recv1), m) for m in range(1, 8)]
        for cp in first:
            cp.start()
        recv_ref[me] = p_ref[me]
        for m in range(1, 8):
            slot = recv_ref.at[index(_peer(x, y, c, m))]
            copy(slot, slot, (send1, recv1), m).wait_recv()
        total = recv_ref[0]
        for d in range(1, 8):
            total = total + recv_ref[d]
        out_ref[me] = total
        second = [copy(out_ref.at[me], out_ref.at[me], (send2, recv2), m) for m in range(1, 8)]
        for cp in second:
            cp.start()
        for m in range(1, 8):
            slot = out_ref.at[index(_peer(x, y, c, m))]
            copy(slot, slot, (send2, recv2), m).wait_recv()
        for cp in first + second:
            cp.wait_send()

    vmem = pl.BlockSpec(memory_space=pltpu.VMEM)
    return pl.pallas_call(
        body, name="allreduce_small", in_specs=[vmem] + [ANY] * len(deps), out_specs=vmem,
        out_shape=jax.ShapeDtypeStruct(packed.shape, F32),
        scratch_shapes=[pltpu.VMEM(packed.shape, F32)] + [pltpu.SemaphoreType.DMA((7,))] * 4,
    )(packed, *deps)


def _adamw(name, w, g, m, v, rows_per_step, first=0, count=None, dests=None):
    layers, rows, cols = w.shape
    count = layers if count is None else count

    def body(w_ref, g_ref, m_ref, v_ref, *rest):
        d_ref, nm_ref, nv_ref = rest[-3:]
        gt = g_ref[...]
        nm = ADAM_B1 * m_ref[...] + (1.0 - ADAM_B1) * gt
        nv = ADAM_B2 * v_ref[...] + (1.0 - ADAM_B2) * (gt * gt)
        m_hat = nm / (1.0 - ADAM_B1 ** ADAM_STEP)
        v_hat = nv / (1.0 - ADAM_B2 ** ADAM_STEP)
        d_ref[...] = -ADAM_LR * (m_hat / (jnp.sqrt(v_hat) + ADAM_EPS) + ADAM_WD * w_ref[...])
        nm_ref[...] = nm
        nv_ref[...] = nv

    spec = pl.BlockSpec((1, rows_per_step, cols), lambda l, i: (first + l, i, 0))
    shape = jax.ShapeDtypeStruct(w.shape, F32)
    dests = () if dests is None else tuple(dests)
    return pl.pallas_call(
        body, name=name, grid=(count, rows // rows_per_step),
        in_specs=[spec] * 4 + [ANY] * len(dests), out_specs=[spec] * 3, out_shape=[shape] * 3,
        input_output_aliases={4 + k: k for k in range(len(dests))},
        compiler_params=_params(("arbitrary", "arbitrary")),
    )(w, g, m, v, *dests)


def _pack_small(pool_w, pool_scale, sinks, norm_pre, norm_post):
    sink_rows = jnp.zeros((ROWS_SINK, 128), F32).at[0, 0:2 * N_HEADS].set(sinks.reshape(-1))
    return jnp.concatenate([
        pool_w.reshape(ROWS_PW, 128), pool_scale.reshape(ROWS_SC, 128), norm_pre.reshape(ROWS_NORM, 128),
        norm_post.reshape(ROWS_NORM, 128), sink_rows, jnp.zeros((SMALL_ROWS - ROW_LOSS, 128), F32)], axis=0)


def _pack_small_grads(grads, loss):
    sink_rows = jnp.zeros((ROWS_SINK, 128), F32).at[0, 0:2 * N_HEADS].set(
        jnp.concatenate([g[4][:, 0] for g in grads]))
    return jnp.concatenate(
        [g[2].reshape(ROWS_PW // DEPTH, 128) for g in grads] + [g[3] for g in grads] + [g[5] for g in grads]
        + [g[6] for g in grads] + [sink_rows, loss, jnp.zeros((SMALL_ROWS - ROW_LOSS - 8, 128), F32)], axis=0)


def _unpack_small(packed):
    o = 0
    pool_w = packed[o:o + ROWS_PW].reshape(DEPTH, 4, 128, 128)
    o += ROWS_PW
    pool_scale = packed[o:o + ROWS_SC].reshape(DEPTH, 512)
    o += ROWS_SC
    norm_pre = packed[o:o + ROWS_NORM].reshape(DEPTH, D)
    o += ROWS_NORM
    norm_post = packed[o:o + ROWS_NORM].reshape(DEPTH, D)
    o += ROWS_NORM
    sinks = packed[o, 0:2 * N_HEADS].reshape(DEPTH, N_HEADS)
    return pool_w, pool_scale, sinks, norm_pre, norm_post


def kernel(x, w_in, pool_w, pool_scale, attn_sinks, w_out, norm_pre, norm_post, loss_target, m_w_in, m_pool_w, m_pool_scale, m_attn_sinks, m_w_out, m_norm_pre, m_norm_post, v_w_in, v_pool_w, v_pool_scale, v_attn_sinks, v_w_out, v_norm_pre, v_norm_post):
    cx, cy, cc = _place()
    chip_arr = jnp.reshape(2 * cx + cy, (1,)).astype(jnp.int32)
    place_arr = jnp.stack([4 * cx + 2 * cy + cc, cc]).astype(jnp.int32)
    t = lambda a: jnp.transpose(a, (0, 2, 1))
    w_in_t = t(w_in)
    xs, target = x[0], loss_target[0]
    pool_w_b = pool_w.astype(BF16)
    scale3 = pool_scale.reshape(DEPTH, 1, D_POOL)
    pre3 = norm_pre.reshape(DEPTH, 1, D)
    post3 = norm_post.reshape(DEPTH, 1, D)

    wi = _place_cast("place_w_in", w_in_t, chip_arr, 288)
    wo = _place_cast("place_w_out", w_out, chip_arr, 256)
    send, recv, bufs, token = _gather_start([wi[0], wo[0], wi[1], wo[1]])

    saved = []
    after = token
    for l in range(DEPTH):
        w_in_l = _gather_wait(f"gather_wait_in{l}", bufs[2 * l], send[2 * l], recv[2 * l], after)
        pu, pg, q, kv, ag = _fwd_in(l, xs, pre3, w_in_l)
        cat = _fwd_mix(l, pu, pg, q, kv, ag, pool_w_b, scale3, attn_sinks)
        w_out_l = _gather_wait(f"gather_wait_out{l}", bufs[2 * l + 1], send[2 * l + 1], recv[2 * l + 1], cat)
        if l < DEPTH - 1:
            y, x_next = _fwd_out(l, cat, w_out_l, xs, post3)
        else:
            y, x_next, loss = _fwd_out(l, cat, w_out_l, xs, post3, target)
        saved.append((xs, pu, pg, q, kv, ag, cat, y, w_in_l, w_out_l))
        xs = after = x_next

    x_in, pu, pg, q, kv, ag, cat, y, w_in_l, w_out_l = saved[1]
    dcat, dw_out1, dw_out1_b, dg_post1 = _bwd_out(1, xs, y, post3, cat, w_out_l)
    dproj, dpw1, dsc1, dsink1 = _bwd_mix(1, pu, pg, q, kv, ag, dcat, pool_w_b, scale3, attn_sinks)
    dx, dw_in1, dw_in1_b, dg_pre1 = _bwd_in(1, dproj, w_in_l, x_in, pre3, xs)
    ex1 = _exchange_start("exchange_start_1", [dw_in1_b, dw_out1_b])

    x_in, pu, pg, q, kv, ag, cat, y, w_in_l, w_out_l = saved[0]
    dcat, dw_out0, dw_out0_b, dg_post0 = _bwd_out(0, dx, y, post3, cat, w_out_l, deps=(ex1[4],))
    ex0_out = _exchange_start("exchange_start_out0", [dw_out0_b])
    dproj, dpw0, dsc0, dsink0 = _bwd_mix(0, pu, pg, q, kv, ag, dcat, pool_w_b, scale3, attn_sinks, deps=(ex0_out[4],))
    recv_in1, recv_out1 = _exchange_wait("exchange_wait_1", ex1, dproj)
    g_in = _sum_pieces("sum_pieces_in1", dw_in1, recv_in1, place_arr, 96, 1)
    g_out = _sum_pieces("sum_pieces_out1", dw_out1, recv_out1, place_arr, 64, 1)
    dw_in0, dw_in0_b = _bwd_in_dw(0, dproj, x_in, pre3, deps=(g_in, g_out))
    ex0_in = _exchange_start("exchange_start_in0", [dw_in0_b])

    grad_x, dg_pre0 = _bwd_in_dx(0, dproj, w_in_l, x_in, pre3, dx, deps=(ex0_in[4],))
    (recv_out0,) = _exchange_wait("exchange_wait_out0", ex0_out, grad_x)
    g_out = _sum_pieces("sum_pieces_out0", dw_out0, recv_out0, place_arr, 64, 0, dest=g_out)
    g_in, grad_w_out = _swap_halves("swap_halves_a", [g_in, g_out], [(0, 1), (1, 0), (1, 1)])
    m_in_t, v_in_t = t(m_w_in), t(v_w_in)
    d_out, nm_out, nv_out = _adamw("adamw_w_out", w_out, grad_w_out, m_w_out, v_w_out, 256)
    upd_in = _adamw("adamw_w_in1", w_in_t, g_in, m_in_t, v_in_t, 288, first=1, count=1)

    grads = [(None, None, dpw0, dsc0, dsink0, dg_pre0, dg_post0), (None, None, dpw1, dsc1, dsink1, dg_pre1, dg_post1)]
    packed = _pack_small_grads(grads, loss).reshape(8, PIECE_ROWS, 128)
    g_small = _allreduce_small(packed, deps=(upd_in[0], d_out)).reshape(1, SMALL_ROWS, 128)
    small = lambda *a: _pack_small(*a).reshape(1, SMALL_ROWS, 128)
    d_s, nm_s, nv_s = _adamw(
        "adamw_small", small(pool_w, pool_scale, attn_sinks, norm_pre, norm_post), g_small,
        small(m_pool_w, m_pool_scale, m_attn_sinks, m_norm_pre, m_norm_post),
        small(v_pool_w, v_pool_scale, v_attn_sinks, v_norm_pre, v_norm_post), SMALL_ROWS)

    (recv_in0,) = _exchange_wait("exchange_wait_in0", ex0_in, d_s)
    g_in = _sum_pieces("sum_pieces_in0", dw_in0, recv_in0, place_arr, 96, 0, dest=g_in)
    (grad_w_in_t,) = _swap_halves("swap_halves_b", [g_in], [(0, 0)])
    d_in, nm_in, nv_in = _adamw("adamw_w_in0", w_in_t, grad_w_in_t, m_in_t, v_in_t, 288, first=0, count=1, dests=upd_in)

    g_pw, g_sc, g_sk, g_pre, g_post = _unpack_small(g_small[0])
    d_pw, d_sc, d_sk, d_pre, d_post = _unpack_small(d_s[0])
    m_pw, m_sc, m_sk, m_pre, m_post = _unpack_small(nm_s[0])
    v_pw, v_sc, v_sk, v_pre, v_post = _unpack_small(nv_s[0])
    return (g_small[0, ROW_LOSS, 0], grad_x[None], t(grad_w_in_t), g_pw, g_sc, g_sk, grad_w_out, g_pre, g_post,
            t(d_in), d_pw, d_sc, d_sk, d_out, d_pre, d_post,
            t(nm_in), m_pw, m_sc, m_sk, nm_out, m_pre, m_post,
            t(nv_in), v_pw, v_sc, v_sk, nv_out, v_pre, v_post)
```

```python
import jax
import jax.numpy as jnp
from jax import lax
from jax.experimental import pallas as pl
from jax.experimental.pallas import tpu as pltpu

F32 = jnp.float32
BF16 = jnp.bfloat16

S = 2048
D = 1024
DEPTH = 2
D_POOL = 512
POOL_WINDOWS = (2, 4, 8, 16)
N_HEADS = 8
D_IN = 2304
N_SHARDS = 4
W_IN_SHARD = D_IN // N_SHARDS
W_OUT_SHARD = D // N_SHARDS
BLK = 128
NB = S // BLK
HALO = 16
EPS = 1e-6
NEG_INF = -1e30
C_PU, C_PG, C_Q, C_K, C_V, C_AG = 0, 512, 1024, 1536, 1664, 1792

ADAM_LR = 0.001
ADAM_B1 = 0.9
ADAM_B2 = 0.999
ADAM_EPS = 1e-08
ADAM_WD = 0.01
ADAM_STEP = 10

TM = 512
VMEM_LIMIT = 56 * 1024 * 1024

NT = (((1,), (1,)), ((), ()))
TN = (((0,), (0,)), ((), ()))

MESH = pl.DeviceIdType.MESH
ANY = pl.BlockSpec(memory_space=pl.ANY)

ROWS_PW, ROWS_SC, ROWS_NORM, ROWS_SINK = 1024, 8, 16, 8
ROW_LOSS = ROWS_PW + ROWS_SC + 2 * ROWS_NORM + ROWS_SINK
SMALL_ROWS = 1088
PIECE_ROWS = SMALL_ROWS // 8


def _params(sem=("arbitrary",)):
    return pltpu.CompilerParams(dimension_semantics=sem, vmem_limit_bytes=VMEM_LIMIT)


def _sigmoid(v):
    return 1.0 / (1.0 + jnp.exp(-v))


def _rows8(v):
    r, c = v.shape
    return v.reshape(r // 8, 8, c).sum(axis=0)


def _layer(l, *shape):
    zeros = (0,) * len(shape)
    return pl.BlockSpec((None,) + shape, lambda i: (l,) + zeros)


def _whole(shape):
    zeros = (0,) * len(shape)
    return pl.BlockSpec(shape, lambda i: zeros, pipeline_mode=pl.Buffered(1))


def _fwd_in(l, x, g_pre, w_in_t):
    def body(x_ref, g_ref, w_ref, pu_ref, pg_ref, q_ref, kv_ref, ag_ref):
        xt = x_ref[...]
        r = lax.rsqrt(jnp.mean(xt * xt, axis=-1, keepdims=True) + EPS)
        h = (xt * r * g_ref[...]).astype(BF16)

        def proj(lo, hi):
            return lax.dot_general(h, w_ref[lo:hi, :], NT, preferred_element_type=F32)

        pu_ref[...] = proj(C_PU, C_PG)
        pg_ref[...] = proj(C_PG, C_Q)
        q_ref[...] = proj(C_Q, C_K).astype(BF16)
        kv_ref[...] = proj(C_K, C_AG).astype(BF16)
        ag_ref[...] = proj(C_AG, D_IN)

    row = lambda w: pl.BlockSpec((TM, w), lambda i: (i, 0))
    return pl.pallas_call(
        body, name="fwd_in", grid=(S // TM,),
        in_specs=[row(D), _layer(l, 1, D), _whole((D_IN, D))],
        out_specs=[row(512), row(512), row(512), row(256), row(512)],
        out_shape=[jax.ShapeDtypeStruct((S, 512), F32), jax.ShapeDtypeStruct((S, 512), F32),
                   jax.ShapeDtypeStruct((S, 512), BF16), jax.ShapeDtypeStruct((S, 256), BF16),
                   jax.ShapeDtypeStruct((S, 512), F32)],
        compiler_params=_params(),
    )(x, g_pre, w_in_t)


def _mask_terms(i):
    qi = lax.broadcasted_iota(jnp.int32, (BLK, 2 * BLK), 0)
    kj = lax.broadcasted_iota(jnp.int32, (BLK, 2 * BLK), 1)
    dist = qi + BLK - kj
    valid = (dist >= 0) & (dist < BLK) & ((kj >= BLK) | (i > 0))
    return dist.astype(F32), valid


def _head_variants(cur, prev):
    both = jnp.concatenate([prev, cur], axis=0).astype(F32)
    swapped = pltpu.roll(both, 64, axis=1)
    low = lax.broadcasted_iota(jnp.int32, both.shape, 1) < 64
    zero = jnp.zeros_like(both)
    return ((jnp.where(low, both, zero).astype(BF16), jnp.where(low, zero, swapped).astype(BF16)),
            (jnp.where(low, swapped, zero).astype(BF16), jnp.where(low, zero, both).astype(BF16)))


def _probs(q_tile, k_var, slope, sink, dist, valid):
    s = lax.dot_general(q_tile, k_var, NT, preferred_element_type=F32)
    s = jnp.where(valid, s * 0.125 - slope * dist, NEG_INF)
    m = jnp.maximum(jnp.max(s, axis=-1, keepdims=True), sink)
    p = jnp.exp(s - m)
    e_sink = jnp.exp(sink - m)
    inv = 1.0 / (jnp.sum(p, axis=-1, keepdims=True) + e_sink)
    return p * inv, e_sink * inv


def _pool_block(ext_ref, i, g, w):
    lanes = slice(g * 128, (g + 1) * 128)
    u = ext_ref[HALO:HALO + BLK, lanes]
    acc = u
    for j in range(1, w):
        acc = acc + ext_ref[HALO - j:HALO - j + BLK, lanes]
    t = (i * BLK + lax.broadcasted_iota(jnp.int32, (BLK, 1), 0)).astype(F32)
    inv = 1.0 / jnp.minimum(t + 1.0, float(w))
    return acc * inv - u, inv


def _fwd_mix(l, pu, pg, q, kv, ag, pool_w, pool_scale, sinks):
    def body(pu_ref, pup_ref, pg_ref, q_ref, kv_ref, kvp_ref, ag_ref, pw_ref, sc_ref, sink_ref, cat_ref, ext_ref):
        i = pl.program_id(0)
        ext_ref[0:HALO, :] = jnp.where(i > 0, pup_ref[...], 0.0)
        ext_ref[HALO:HALO + BLK, :] = pu_ref[...]
        for g, w in enumerate(POOL_WINDOWS):
            lanes = slice(g * 128, (g + 1) * 128)
            pooled, _ = _pool_block(ext_ref, i, g, w)
            mixed = jnp.dot(pooled.astype(BF16), pw_ref[g], preferred_element_type=F32)
            gate = pg_ref[:, lanes]
            cat_ref[:, lanes] = (mixed * sc_ref[:, lanes] * (gate * _sigmoid(gate))).astype(BF16)

        dist, valid = _mask_terms(i)
        k_var = _head_variants(kv_ref[:, 0:128], kvp_ref[:, 0:128])
        v_var = _head_variants(kv_ref[:, 128:256], kvp_ref[:, 128:256])
        for j in range(4):
            hkv = j // 2
            lanes = slice(j * 128, (j + 1) * 128)
            q_tile = q_ref[:, lanes]
            o = jnp.zeros((BLK, 128), F32)
            for half in range(2):
                head = hkv * 4 + 2 * (j % 2) + half
                p, _ = _probs(q_tile, k_var[hkv][half], 2.0 ** -(head + 1), sink_ref[l, head], dist, valid)
                o = o + jnp.dot(p.astype(BF16), v_var[hkv][half], preferred_element_type=F32)
            gate = ag_ref[:, lanes]
            cat_ref[:, D_POOL + j * 128:D_POOL + (j + 1) * 128] = (o * (gate * _sigmoid(gate))).astype(BF16)

    blk = lambda w: pl.BlockSpec((BLK, w), lambda i: (i, 0))
    prev = lambda w: pl.BlockSpec((BLK, w), lambda i: (jnp.maximum(i - 1, 0), 0))
    halo = pl.BlockSpec((HALO, 512), lambda i: (jnp.maximum(i * (BLK // HALO) - 1, 0), 0))
    return pl.pallas_call(
        body, name="fwd_mix", grid=(NB,),
        in_specs=[blk(512), halo, blk(512), blk(512), blk(256), prev(256), blk(512),
                  _layer(l, 4, 128, 128), _layer(l, 1, 512), pl.BlockSpec(memory_space=pltpu.SMEM)],
        out_specs=blk(D),
        out_shape=jax.ShapeDtypeStruct((S, D), BF16),
        scratch_shapes=[pltpu.VMEM((HALO + BLK, 512), F32)],
        compiler_params=_params(),
    )(pu, pu, pg, q, kv, kv, ag, pool_w, pool_scale, sinks)


def _fwd_out(l, cat, w_out, x, g_post, target=None):
    last = target is not None
    n_steps = S // TM

    def body(*refs):
        if last:
            cat_ref, w_ref, x_ref, g_ref, t_ref, y_ref, dx_ref, loss_ref, acc_ref = refs
        else:
            cat_ref, w_ref, x_ref, g_ref, y_ref, xn_ref = refs
        y = jnp.dot(cat_ref[...], w_ref[...], preferred_element_type=F32)
        y_ref[...] = y
        r = lax.rsqrt(jnp.mean(y * y, axis=-1, keepdims=True) + EPS)
        xn = x_ref[...] + y * r * g_ref[...]
        if not last:
            xn_ref[...] = xn
            return
        step = pl.program_id(0)
        err = xn - t_ref[...]
        dx_ref[...] = err * (1.0 / D)

        @pl.when(step == 0)
        def _():
            acc_ref[...] = jnp.zeros_like(acc_ref)

        acc_ref[...] += _rows8(err * err)

        @pl.when(step == n_steps - 1)
        def _():
            loss_ref[...] = jnp.full((8, 128), (0.5 / D) * jnp.sum(acc_ref[...]), F32)

    row = lambda: pl.BlockSpec((TM, D), lambda i: (i, 0))
    act = jax.ShapeDtypeStruct((S, D), F32)
    in_specs = [row(), _whole((D, D)), row(), _layer(l, 1, D)]
    args = [cat, w_out, x, g_post]
    if last:
        return pl.pallas_call(
            body, name="fwd_out_loss", grid=(n_steps,),
            in_specs=in_specs + [row()],
            out_specs=[row(), row(), pl.BlockSpec((8, 128), lambda i: (0, 0))],
            out_shape=[act, act, jax.ShapeDtypeStruct((8, 128), F32)],
            scratch_shapes=[pltpu.VMEM((8, D), F32)],
            compiler_params=_params(),
        )(*args, target)
    return pl.pallas_call(
        body, name="fwd_out", grid=(n_steps,),
        in_specs=in_specs, out_specs=[row(), row()], out_shape=[act, act],
        compiler_params=_params(),
    )(*args)


def _store_lane_rows(ref, acc):
    total = jnp.sum(acc, axis=0, keepdims=True)
    for k in range(ref.shape[0]):
        ref[k:k + 1, :] = total[:, k * 128:(k + 1) * 128]


def _bwd_out(l, dxn, y, g_post, cat, w_out, deps=()):
    n_steps = S // TM

    def body(dz_ref, y_ref, g_ref, cat_ref, w_ref, *rest):
        dcat_ref, dw_ref, dwb_ref, dg_ref, acc_ref = rest[len(deps):]
        step = pl.program_id(0)

        @pl.when(step == 0)
        def _():
            dw_ref[...] = jnp.zeros_like(dw_ref)
            acc_ref[...] = jnp.zeros_like(acc_ref)

        y = y_ref[...]
        dz = dz_ref[...]
        r = lax.rsqrt(jnp.mean(y * y, axis=-1, keepdims=True) + EPS)
        a = dz * g_ref[...]
        dy = r * a - y * (r * r * r) * jnp.mean(a * y, axis=-1, keepdims=True)
        acc_ref[...] += _rows8(dz * (y * r))
        dyb = dy.astype(BF16)
        dcat_ref[...] = lax.dot_general(dyb, w_ref[...], NT, preferred_element_type=F32)
        dw_ref[...] += lax.dot_general(cat_ref[...], dyb, TN, preferred_element_type=F32)

        @pl.when(step == n_steps - 1)
        def _():
            _store_lane_rows(dg_ref, acc_ref[...])
            dwb_ref[...] = dw_ref[...].astype(BF16)

    row = lambda: pl.BlockSpec((TM, D), lambda i: (i, 0))
    full = _whole
    return pl.pallas_call(
        body, name="bwd_out", grid=(n_steps,),
        in_specs=[row(), row(), _layer(l, 1, D), row(), full((D, D))] + [ANY] * len(deps),
        out_specs=[row(), full((D, D)), full((D, D)), full((8, 128))],
        out_shape=[jax.ShapeDtypeStruct((S, D), F32), jax.ShapeDtypeStruct((D, D), F32),
                   jax.ShapeDtypeStruct((D, D), BF16), jax.ShapeDtypeStruct((8, 128), F32)],
        scratch_shapes=[pltpu.VMEM((8, D), F32)],
        compiler_params=_params(),
    )(dxn, y, g_post, cat, w_out, *deps)


def _bwd_mix(l, pu, pg, q, kv, ag, dcat, pool_w, pool_scale, sinks, deps=()):
    def body(pu_ref, pup_ref, pg_ref, q_ref, kv_ref, kvp_ref, ag_ref, dcat_ref, pw_ref, sc_ref, sink_ref, *rest):
        dproj_ref, dpw_ref, dsc_ref, dsink_ref, ext_ref, dext_ref, dkv_ref = rest[len(deps):]
        step = pl.program_id(0)
        i = NB - 1 - step

        @pl.when(step == 0)
        def _():
            dpw_ref[...] = jnp.zeros_like(dpw_ref)
            dsc_ref[...] = jnp.zeros_like(dsc_ref)
            dsink_ref[...] = jnp.zeros_like(dsink_ref)
            dext_ref[BLK:BLK + HALO, :] = jnp.zeros((HALO, 512), F32)
            dkv_ref[...] = jnp.zeros_like(dkv_ref)

        ext_ref[0:HALO, :] = jnp.where(i > 0, pup_ref[...], 0.0)
        ext_ref[HALO:HALO + BLK, :] = pu_ref[...]
        for g, w in enumerate(POOL_WINDOWS):
            lanes = slice(g * 128, (g + 1) * 128)
            pooled, inv = _pool_block(ext_ref, i, g, w)
            pooled_b = pooled.astype(BF16)
            mixed = jnp.dot(pooled_b, pw_ref[g], preferred_element_type=F32)
            scale = sc_ref[:, lanes]
            gate = pg_ref[:, lanes]
            sg = _sigmoid(gate)
            dpo = dcat_ref[:, lanes]
            dproj_ref[:, C_PG + g * 128:C_PG + (g + 1) * 128] = (
                dpo * (mixed * scale) * (sg * (1.0 + gate * (1.0 - sg)))).astype(BF16)
            dms = dpo * (gate * sg)
            dsc_ref[g:g + 1, :] += jnp.sum(dms * mixed, axis=0, keepdims=True)
            dmixed = (dms * scale).astype(BF16)
            dpw_ref[g] += lax.dot_general(pooled_b, dmixed, TN, preferred_element_type=F32)
            dpooled = lax.dot_general(dmixed, pw_ref[g], NT, preferred_element_type=F32)
            dext_ref[0:BLK, lanes] = dpooled * inv
            acc = dext_ref[0:BLK, lanes]
            for j in range(1, w):
                acc = acc + dext_ref[j:j + BLK, lanes]
            dproj_ref[:, C_PU + g * 128:C_PU + (g + 1) * 128] = (acc - dpooled).astype(BF16)
        dext_ref[BLK:BLK + HALO, :] = dext_ref[0:HALO, :]

        dist, valid = _mask_terms(i)
        k_var = _head_variants(kv_ref[:, 0:128], kvp_ref[:, 0:128])
        v_var = _head_variants(kv_ref[:, 128:256], kvp_ref[:, 128:256])
        zero = jnp.zeros((2 * BLK, 128), F32)
        dk_acc = [[zero, zero], [zero, zero]]
        dv_acc = [[zero, zero], [zero, zero]]
        for j in range(4):
            hkv = j // 2
            lanes = slice(j * 128, (j + 1) * 128)
            q_tile = q_ref[:, lanes]
            gate = ag_ref[:, lanes]
            sg = _sigmoid(gate)
            dca = dcat_ref[:, D_POOL + j * 128:D_POOL + (j + 1) * 128]
            do_b = (dca * (gate * sg)).astype(BF16)
            probs = []
            o = jnp.zeros((BLK, 128), F32)
            for half in range(2):
                head = hkv * 4 + 2 * (j % 2) + half
                p, p_sink = _probs(q_tile, k_var[hkv][half], 2.0 ** -(head + 1), sink_ref[l, head], dist, valid)
                probs.append((p, p_sink))
                o = o + jnp.dot(p.astype(BF16), v_var[hkv][half], preferred_element_type=F32)
            dproj_ref[:, C_AG + j * 128:C_AG + (j + 1) * 128] = (
                dca * o * (sg * (1.0 + gate * (1.0 - sg)))).astype(BF16)
            dq = jnp.zeros((BLK, 128), F32)
            for half in range(2):
                head = hkv * 4 + 2 * (j % 2) + half
                p, p_sink = probs[half]
                dp = lax.dot_general(do_b, v_var[hkv][half], NT, preferred_element_type=F32)
                delta = jnp.sum(p * dp, axis=-1, keepdims=True)
                ds_b = (p * (dp - delta) * 0.125).astype(BF16)
                dsink_ref[head:head + 1, :] += jnp.broadcast_to(
                    -jnp.sum(p_sink * delta, axis=0, keepdims=True), (1, 128))
                dq = dq + jnp.dot(ds_b, k_var[hkv][half], preferred_element_type=F32)
                dk_acc[hkv][half] = dk_acc[hkv][half] + lax.dot_general(
                    ds_b, q_tile, TN, preferred_element_type=F32)
                dv_acc[hkv][half] = dv_acc[hkv][half] + lax.dot_general(
                    p.astype(BF16), do_b, TN, preferred_element_type=F32)
            dproj_ref[:, C_Q + j * 128:C_Q + (j + 1) * 128] = dq.astype(BF16)

        low = lax.broadcasted_iota(jnp.int32, (2 * BLK, 128), 1) < 64

        def gather_heads(acc):
            return jnp.where(low, acc[0][0] + pltpu.roll(acc[0][1], 64, axis=1),
                             pltpu.roll(acc[1][0], 64, axis=1) + acc[1][1])

        dk = gather_heads(dk_acc)
        dv = gather_heads(dv_acc)
        dproj_ref[:, C_K:C_V] = (dk[BLK:, :] + dkv_ref[:, 0:128]).astype(BF16)
        dproj_ref[:, C_V:C_AG] = (dv[BLK:, :] + dkv_ref[:, 128:256]).astype(BF16)
        dkv_ref[:, 0:128] = dk[:BLK, :]
        dkv_ref[:, 128:256] = dv[:BLK, :]

    rev = lambda w: pl.BlockSpec((BLK, w), lambda s: (NB - 1 - s, 0))
    prev = lambda w: pl.BlockSpec((BLK, w), lambda s: (jnp.maximum(NB - 2 - s, 0), 0))
    halo = pl.BlockSpec((HALO, 512), lambda s: (jnp.maximum((NB - 1 - s) * (BLK // HALO) - 1, 0), 0))
    return pl.pallas_call(
        body, name="bwd_mix", grid=(NB,),
        in_specs=[rev(512), halo, rev(512), rev(512), rev(256), prev(256), rev(512), rev(D),
                  _layer(l, 4, 128, 128), _layer(l, 1, 512), pl.BlockSpec(memory_space=pltpu.SMEM)] + [ANY] * len(deps),
        out_specs=[rev(D_IN), pl.BlockSpec((4, 128, 128), lambda s: (0, 0, 0)),
                   pl.BlockSpec((4, 128), lambda s: (0, 0)), pl.BlockSpec((8, 128), lambda s: (0, 0))],
        out_shape=[jax.ShapeDtypeStruct((S, D_IN), BF16), jax.ShapeDtypeStruct((4, 128, 128), F32),
                   jax.ShapeDtypeStruct((4, 128), F32), jax.ShapeDtypeStruct((8, 128), F32)],
        scratch_shapes=[pltpu.VMEM((HALO + BLK, 512), F32), pltpu.VMEM((BLK + HALO, 512), F32),
                        pltpu.VMEM((BLK, 256), F32)],
        compiler_params=_params(),
    )(pu, pu, pg, q, kv, kv, ag, dcat, pool_w, pool_scale, sinks, *deps)


def _bwd_in(l, dproj, w_in_t, x, g_pre, dres, deps=()):
    n_steps = S // TM

    def body(dp_ref, w_ref, x_ref, g_ref, dres_ref, *rest):
        dx_ref, dw_ref, dwb_ref, dg_ref, acc_ref = rest[len(deps):]
        step = pl.program_id(0)

        @pl.when(step == 0)
        def _():
            dw_ref[...] = jnp.zeros_like(dw_ref)
            acc_ref[...] = jnp.zeros_like(acc_ref)

        dp = dp_ref[...]
        dh = jnp.dot(dp, w_ref[...], preferred_element_type=F32)
        xt = x_ref[...]
        r = lax.rsqrt(jnp.mean(xt * xt, axis=-1, keepdims=True) + EPS)
        xn = xt * r
        g = g_ref[...]
        acc_ref[...] += _rows8(dh * xn)
        a = dh * g
        dx_ref[...] = dres_ref[...] + (r * a - xt * (r * r * r) * jnp.mean(a * xt, axis=-1, keepdims=True))
        dw_ref[...] += lax.dot_general(dp, (xn * g).astype(BF16), TN, preferred_element_type=F32)

        @pl.when(step == n_steps - 1)
        def _():
            _store_lane_rows(dg_ref, acc_ref[...])
            dwb_ref[...] = dw_ref[...].astype(BF16)

    row = lambda w: pl.BlockSpec((TM, w), lambda i: (i, 0))
    full = _whole
    return pl.pallas_call(
        body, name="bwd_in", grid=(n_steps,),
        in_specs=[row(D_IN), full((D_IN, D)), row(D), _layer(l, 1, D), row(D)] + [ANY] * len(deps),
        out_specs=[row(D), full((D_IN, D)), full((D_IN, D)), full((8, 128))],
        out_shape=[jax.ShapeDtypeStruct((S, D), F32), jax.ShapeDtypeStruct((D_IN, D), F32),
                   jax.ShapeDtypeStruct((D_IN, D), BF16), jax.ShapeDtypeStruct((8, 128), F32)],
        scratch_shapes=[pltpu.VMEM((8, D), F32)],
        compiler_params=_params(),
    )(dproj, w_in_t, x, g_pre, dres, *deps)


def _bwd_in_dw(l, dproj, x, g_pre, deps=()):
    n_steps = S // TM

    def body(dp_ref, x_ref, g_ref, *rest):
        dw_ref, dwb_ref = rest[len(deps):]
        step = pl.program_id(0)

        @pl.when(step == 0)
        def _():
            dw_ref[...] = jnp.zeros_like(dw_ref)

        xt = x_ref[...]
        r = lax.rsqrt(jnp.mean(xt * xt, axis=-1, keepdims=True) + EPS)
        h = (xt * r * g_ref[...]).astype(BF16)
        dw_ref[...] += lax.dot_general(dp_ref[...], h, TN, preferred_element_type=F32)

        @pl.when(step == n_steps - 1)
        def _():
            dwb_ref[...] = dw_ref[...].astype(BF16)

    row = lambda w: pl.BlockSpec((TM, w), lambda i: (i, 0))
    full = _whole
    return pl.pallas_call(
        body, name="bwd_in_dw", grid=(n_steps,),
        in_specs=[row(D_IN), row(D), _layer(l, 1, D)] + [ANY] * len(deps),
        out_specs=[full((D_IN, D)), full((D_IN, D))],
        out_shape=[jax.ShapeDtypeStruct((D_IN, D), F32), jax.ShapeDtypeStruct((D_IN, D), BF16)],
        compiler_params=_params(),
    )(dproj, x, g_pre, *deps)


def _bwd_in_dx(l, dproj, w_in_t, x, g_pre, dres, deps=()):
    n_steps = S // TM

    def body(dp_ref, w_ref, x_ref, g_ref, dres_ref, *rest):
        dx_ref, dg_ref, acc_ref = rest[len(deps):]
        step = pl.program_id(0)

        @pl.when(step == 0)
        def _():
            acc_ref[...] = jnp.zeros_like(acc_ref)

        dh = jnp.dot(dp_ref[...], w_ref[...], preferred_element_type=F32)
        xt = x_ref[...]
        r = lax.rsqrt(jnp.mean(xt * xt, axis=-1, keepdims=True) + EPS)
        xn = xt * r
        acc_ref[...] += _rows8(dh * xn)
        a = dh * g_ref[...]
        dx_ref[...] = dres_ref[...] + (r * a - xt * (r * r * r) * jnp.mean(a * xt, axis=-1, keepdims=True))

        @pl.when(step == n_steps - 1)
        def _():
            _store_lane_rows(dg_ref, acc_ref[...])

    row = lambda w: pl.BlockSpec((TM, w), lambda i: (i, 0))
    full = _whole
    return pl.pallas_call(
        body, name="bwd_in_dx", grid=(n_steps,),
        in_specs=[row(D_IN), full((D_IN, D)), row(D), _layer(l, 1, D), row(D)] + [ANY] * len(deps),
        out_specs=[row(D), full((8, 128))],
        out_shape=[jax.ShapeDtypeStruct((S, D), F32), jax.ShapeDtypeStruct((8, 128), F32)],
        scratch_shapes=[pltpu.VMEM((8, D), F32)],
        compiler_params=_params(),
    )(dproj, w_in_t, x, g_pre, dres, *deps)


HBM =pl.BlockSpec(memory_space=pltpu.HBM)
SEM = pl.BlockSpec(memory_space=pltpu.SEMAPHORE)
SPLIT_COPY = pltpu.CompilerParams(has_side_effects=pltpu.SideEffectType.DATAFLOW_SIDE_EFFECTING)


def _in_hbm(a):
    return pltpu.with_memory_space_constraint(a, pltpu.HBM)

def _place():
    return lax.axis_index("x"), lax.axis_index("y"), lax.axis_index("c")


def _other_chips(x, y):
    return [(1 - x, y), (x, 1 - y), (1 - x, 1 - y)]


def _peer(x, y, c, m):
    return (x ^ (m >> 2), y ^ ((m >> 1) & 1), c ^ (m & 1))


def _place_cast(name, src, chip_arr, tile):
    _, n, cols = src.shape
    steps = n // tile

    def body(chip_ref, s0_ref, s1_ref, o0_ref, o1_ref):
        o0_ref[...] = s0_ref[...].astype(BF16)
        o1_ref[...] = s1_ref[...].astype(BF16)

    return pl.pallas_call(
        body, name=name,
        grid_spec=pltpu.PrefetchScalarGridSpec(
            num_scalar_prefetch=1, grid=(steps,),
            in_specs=[pl.BlockSpec((None, tile, cols), lambda i, chip: (0, i, 0)),
                      pl.BlockSpec((None, tile, cols), lambda i, chip: (1, i, 0))],
            out_specs=[pl.BlockSpec((tile, cols), lambda i, chip: (chip[0] * steps + i, 0))] * 2),
        out_shape=[jax.ShapeDtypeStruct((N_SHARDS * n, cols), BF16)] * 2,
        compiler_params=_params(),
    )(chip_arr, src, src)


def _chip_rows(ref, chip, half=None):
    n = ref.shape[0] // N_SHARDS
    if half is None:
        return ref.at[pl.ds(pl.multiple_of(chip * n, 16), n), :]
    return ref.at[pl.ds(pl.multiple_of(chip * n + half * (n // 2), 16), n // 2), :]


def _gather_start(bufs, halved):
    n = len(bufs)

    def body(*refs):
        ins, send, recv, token = refs[:n], refs[n:2 * n], refs[2 * n:3 * n], refs[-1]
        x, y, c = _place()
        for a, buf in enumerate(ins):
            own = _chip_rows(buf, 2 * x + y, c if a in halved else None)
            for j, chip in enumerate(_other_chips(x, y)):
                pltpu.make_async_remote_copy(src_ref=own, dst_ref=own, send_sem=send[a].at[j], recv_sem=recv[a].at[j],
                                             device_id=(*chip, c), device_id_type=MESH).start()
        token[...] = jnp.zeros_like(token)

    outs = pl.pallas_call(
        body, name="gather_start", in_specs=[HBM] * n,
        out_specs=[SEM] * (2 * n) + [HBM] * n + [pl.BlockSpec(memory_space=pltpu.VMEM)],
        out_shape=[pltpu.SemaphoreType.DMA((3,))] * (2 * n) + [pltpu.HBM(b.shape, b.dtype) for b in bufs]
        + [jax.ShapeDtypeStruct((8, 128), F32)],
        input_output_aliases={a: 2 * n + a for a in range(n)},
        compiler_params=SPLIT_COPY,
    )(*[_in_hbm(b) for b in bufs])
    return outs[:n], outs[n:2 * n], outs[2 * n:3 * n], outs[-1]


def _gather_wait(name, buf, send_sem, recv_sem, after, halved=False):
    def body(buf_ref, send_ref, recv_ref, after_ref, out_ref):
        x, y, c = _place()
        half = c if halved else None
        own = _chip_rows(buf_ref, 2 * x + y, half)
        for j, chip in enumerate(_other_chips(x, y)):
            copy = pltpu.make_async_remote_copy(src_ref=own, dst_ref=_chip_rows(buf_ref, 2 * chip[0] + chip[1], half),
                                                send_sem=send_ref.at[j], recv_sem=recv_ref.at[j],
                                                device_id=(*chip, c), device_id_type=MESH)
            copy.wait_send()
            copy.wait_recv()

    return pl.pallas_call(
        body, name=name, in_specs=[HBM, SEM, SEM, ANY], out_specs=HBM, out_shape=pltpu.HBM(buf.shape, buf.dtype),
        input_output_aliases={0: 0}, compiler_params=SPLIT_COPY,
    )(buf, send_sem, recv_sem, after)


def _forward_halves(name, buf):
    def body(in_ref, out_ref, send_sems, recv_sems):
        x, y, c = _place()

        def copy(j, chip, half):
            rows = 2 * chip[0] + chip[1]
            return pltpu.make_async_remote_copy(
                src_ref=_chip_rows(in_ref, rows, half), dst_ref=_chip_rows(out_ref, rows, half), send_sem=send_sems.at[j],
                recv_sem=recv_sems.at[j], device_id=(x, y, 1 - c), device_id_type=MESH)

        chips = _other_chips(x, y)
        for j, chip in enumerate(chips):
            copy(j, chip, c).start()
        for j, chip in enumerate(chips):
            copy(j, chip, c).wait_send()
            copy(j, chip, 1 - c).wait_recv()

    return pl.pallas_call(
        body, name=name, in_specs=[ANY], out_specs=ANY, out_shape=jax.ShapeDtypeStruct(buf.shape, buf.dtype),
        input_output_aliases={0: 0},
        scratch_shapes=[pltpu.SemaphoreType.DMA((3,))] * 2,
    )(buf)


def _piece_rows(ref, k):
    p = ref.shape[0] // 8
    return ref.at[pl.ds(pl.multiple_of(k * p, 16), p), :]


def _exchange_start(name, arrays):
    n = len(arrays)
    zones = [lax.empty((7, a.shape[0] // 8, a.shape[1]), BF16) for a in arrays]

    def body(*refs):
        srcs, lands = refs[:n], refs[n:2 * n]
        send, recv, token = refs[2 * n:3 * n], refs[3 * n:4 * n], refs[-1]
        x, y, c = _place()
        for a, (src, land) in enumerate(zip(srcs, lands)):
            for m in range(1, 8):
                px, py, pc = _peer(x, y, c, m)
                pltpu.make_async_remote_copy(
                    src_ref=_piece_rows(src, 4 * px + 2 * py + pc), dst_ref=land.at[m - 1], send_sem=send[a].at[m - 1],
                    recv_sem=recv[a].at[m - 1], device_id=(px, py, pc), device_id_type=MESH).start()
        token[...] = jnp.zeros_like(token)

    outs = pl.pallas_call(
        body, name=name, in_specs=[HBM] * (2 * n),
        out_specs=[SEM] * (2 * n) + [HBM] * (2 * n) + [pl.BlockSpec(memory_space=pltpu.VMEM)],
        out_shape=[pltpu.SemaphoreType.DMA((7,))] * (2 * n) + [pltpu.HBM(a.shape, a.dtype) for a in arrays + zones]
        + [jax.ShapeDtypeStruct((8, 128), F32)],
        input_output_aliases={a: 2 * n + a for a in range(2 * n)},
        compiler_params=SPLIT_COPY,
    )(*[_in_hbm(a) for a in arrays + zones])
    return outs[:n], outs[n:2 * n], outs[2 * n:3 * n], outs[3 * n:4 * n], outs[-1]


def _exchange_wait(name, started, after):
    send_sems, recv_sems, arrays, zones, _ = started
    n = len(arrays)

    def body(*refs):
        srcs, lands = refs[:n], refs[n:2 * n]
        send, recv = refs[2 * n:3 * n], refs[3 * n:4 * n]
        x, y, c = _place()
        for a, (src, land) in enumerate(zip(srcs, lands)):
            for m in range(1, 8):
                px, py, pc = _peer(x, y, c, m)
                copy = pltpu.make_async_remote_copy(
                    src_ref=_piece_rows(src, 4 * px + 2 * py + pc), dst_ref=land.at[m - 1], send_sem=send[a].at[m - 1],
                    recv_sem=recv[a].at[m - 1], device_id=(px, py, pc), device_id_type=MESH)
                copy.wait_send()
                copy.wait_recv()

    outs = pl.pallas_call(
        body, name=name, in_specs=[HBM] * (2 * n) + [SEM] * (2 * n) + [ANY], out_specs=[HBM] * (2 * n),
        out_shape=[pltpu.HBM(a.shape, a.dtype) for a in list(arrays) + list(zones)],
        input_output_aliases={a: a for a in range(2 * n)}, compiler_params=SPLIT_COPY,
    )(*arrays, *zones, *send_sems, *recv_sems, after)
    return outs[n:]


def _sum_pieces(name, partial, recv, place_arr, tile, layer, dest=None):
    p, cols = recv.shape[1:]
    steps = p // tile

    def body(place_ref, o_ref, r_ref, *rest):
        total = o_ref[...]
        for m in range(7):
            total = total + r_ref[m].astype(F32)
        rest[-1][...] = total

    return pl.pallas_call(
        body, name=name,
        grid_spec=pltpu.PrefetchScalarGridSpec(
            num_scalar_prefetch=1, grid=(steps,),
            in_specs=[pl.BlockSpec((tile, cols), lambda i, place: (place[0] * steps + i, 0)),
                      pl.BlockSpec((7, tile, cols), lambda i, place: (0, i, 0))] + ([] if dest is None else [ANY]),
            out_specs=pl.BlockSpec((None, tile, cols), lambda i, place: (layer, place[1] * steps + i, 0))),
        out_shape=jax.ShapeDtypeStruct((DEPTH, 2 * p, cols), F32),
        input_output_aliases={} if dest is None else {3: 0},
        compiler_params=_params(),
    )(place_arr, partial, recv, *(() if dest is None else (dest,)))


def _swap_halves(name, bufs, parts):
    n = len(bufs)

    def body(*refs):
        ins, outs, send_sems, recv_sems = refs[:n], refs[n:2 * n], refs[2 * n], refs[2 * n + 1]
        x, y, c = _place()

        def half(ref, l, which):
            p = ref.shape[1] // 2
            return ref.at[l, pl.ds(pl.multiple_of(which * p, 8), p), :]

        def copy(k, which):
            a, l = parts[k]
            return pltpu.make_async_remote_copy(
                src_ref=half(ins[a], l, which), dst_ref=half(outs[a], l, which), send_sem=send_sems.at[k],
                recv_sem=recv_sems.at[k], device_id=(x, y, 1 - c), device_id_type=MESH)

        for k in range(len(parts)):
            copy(k, c).start()
        for k in range(len(parts)):
            copy(k, c).wait_send()
            copy(k, 1 - c).wait_recv()

    return pl.pallas_call(
        body, name=name, in_specs=[ANY] * n, out_specs=[ANY] * n,
        out_shape=[jax.ShapeDtypeStruct(b.shape, F32) for b in bufs],
        input_output_aliases={a: a for a in range(n)},
        scratch_shapes=[pltpu.SemaphoreType.DMA((len(parts),))] * 2,
    )(*bufs)


def _allreduce_small(packed, deps=()):
    def body(p_ref, *rest):
        out_ref, recv_ref, send1, recv1, send2, recv2 = rest[len(deps):]
        x, y, c = _place()
        me = 4 * x + 2 * y + c

        def index(p):
            return 4 * p[0] + 2 * p[1] + p[2]

        def copy(src, dst, sems, m):
            return pltpu.make_async_remote_copy(src_ref=src, dst_ref=dst, send_sem=sems[0].at[m - 1],
                                                recv_sem=sems[1].at[m - 1], device_id=_peer(x, y, c, m),
                                                device_id_type=MESH)

        first = [copy(p_ref.at[index(_peer(x, y, c, m))], recv_ref.at[me], (send1, recv1), m) for m in range(1, 8)]
        for cp in first:
            cp.start()
        recv_ref[me] = p_ref[me]
        for m in range(1, 8):
            slot = recv_ref.at[index(_peer(x, y, c, m))]
            copy(slot, slot, (send1, recv1), m).wait_recv()
        total = recv_ref[0]
        for d in range(1, 8):
            total = total + recv_ref[d]
        out_ref[me] = total
        second = [copy(out_ref.at[me], out_ref.at[me], (send2, recv2), m) for m in range(1, 8)]
        for cp in second:
            cp.start()
        for m in range(1, 8):
            slot = out_ref.at[index(_peer(x, y, c, m))]
            copy(slot, slot, (send2, recv2), m).wait_recv()
        for cp in first + second:
            cp.wait_send()

    vmem = pl.BlockSpec(memory_space=pltpu.VMEM)
    return pl.pallas_call(
        body, name="allreduce_small", in_specs=[vmem] + [ANY] * len(deps), out_specs=vmem,
        out_shape=jax.ShapeDtypeStruct(packed.shape, F32),
        scratch_shapes=[pltpu.VMEM(packed.shape, F32)] + [pltpu.SemaphoreType.DMA((7,))] * 4,
    )(packed, *deps)


def _adamw(name, w, g, m, v, rows_per_step, first=0, count=None, dests=None):
    layers, rows, cols = w.shape
    count = layers if count is None else count

    def body(w_ref, g_ref, m_ref, v_ref, *rest):
        d_ref, nm_ref, nv_ref = rest[-3:]
        gt = g_ref[...]
        nm = ADAM_B1 * m_ref[...] + (1.0 - ADAM_B1) * gt
        nv = ADAM_B2 * v_ref[...] + (1.0 - ADAM_B2) * (gt * gt)
        m_hat = nm / (1.0 - ADAM_B1 ** ADAM_STEP)
        v_hat = nv / (1.0 - ADAM_B2 ** ADAM_STEP)
        d_ref[...] = -ADAM_LR * (m_hat / (jnp.sqrt(v_hat) + ADAM_EPS) + ADAM_WD * w_ref[...])
        nm_ref[...] = nm
        nv_ref[...] = nv

    spec = pl.BlockSpec((1, rows_per_step, cols), lambda l, i: (first + l, i, 0))
    shape = jax.ShapeDtypeStruct(w.shape, F32)
    dests = () if dests is None else tuple(dests)
    return pl.pallas_call(
        body, name=name, grid=(count, rows // rows_per_step),
        in_specs=[spec] * 4 + [ANY] * len(dests), out_specs=[spec] * 3, out_shape=[shape] * 3,
        input_output_aliases={4 + k: k for k in range(len(dests))},
        compiler_params=_params(("arbitrary", "arbitrary")),
    )(w, g, m, v, *dests)


def _pack_small(pool_w, pool_scale, sinks, norm_pre, norm_post):
    sink_rows = jnp.zeros((ROWS_SINK, 128), F32).at[0, 0:2 * N_HEADS].set(sinks.reshape(-1))
    return jnp.concatenate([
        pool_w.reshape(ROWS_PW, 128), pool_scale.reshape(ROWS_SC, 128), norm_pre.reshape(ROWS_NORM, 128),
        norm_post.reshape(ROWS_NORM, 128), sink_rows, jnp.zeros((SMALL_ROWS - ROW_LOSS, 128), F32)], axis=0)


def _pack_small_grads(grads, loss):
    sink_rows = jnp.zeros((ROWS_SINK, 128), F32).at[0, 0:2 * N_HEADS].set(
        jnp.concatenate([g[4][:, 0] for g in grads]))
    return jnp.concatenate(
        [g[2].reshape(ROWS_PW // DEPTH, 128) for g in grads] + [g[3] for g in grads] + [g[5] for g in grads]
        + [g[6] for g in grads] + [sink_rows, loss, jnp.zeros((SMALL_ROWS - ROW_LOSS - 8, 128), F32)], axis=0)


def _unpack_small(packed):
    o = 0
    pool_w = packed[o:o + ROWS_PW].reshape(DEPTH, 4, 128, 128)
    o += ROWS_PW
    pool_scale = packed[o:o + ROWS_SC].reshape(DEPTH, 512)
    o += ROWS_SC
    norm_pre = packed[o:o + ROWS_NORM].reshape(DEPTH, D)
    o += ROWS_NORM
    norm_post = packed[o:o + ROWS_NORM].reshape(DEPTH, D)
    o += ROWS_NORM
    sinks = packed[o, 0:2 * N_HEADS].reshape(DEPTH, N_HEADS)
    return pool_w, pool_scale, sinks, norm_pre, norm_post


def kernel(x, w_in, pool_w, pool_scale, attn_sinks, w_out, norm_pre, norm_post, loss_target, m_w_in, m_pool_w, m_pool_scale, m_attn_sinks, m_w_out, m_norm_pre, m_norm_post, v_w_in, v_pool_w, v_pool_scale, v_attn_sinks, v_w_out, v_norm_pre, v_norm_post):
    cx, cy, cc = _place()
    chip_arr = jnp.reshape(2 * cx + cy, (1,)).astype(jnp.int32)
    place_arr = jnp.stack([4 * cx + 2 * cy + cc, cc]).astype(jnp.int32)
    t = lambda a: jnp.transpose(a, (0, 2, 1))
    w_in_t = t(w_in)
    xs, target = x[0], loss_target[0]
    pool_w_b = pool_w.astype(BF16)
    scale3 = pool_scale.reshape(DEPTH, 1, D_POOL)
    pre3 = norm_pre.reshape(DEPTH, 1, D)
    post3 = norm_post.reshape(DEPTH, 1, D)

    wi = _place_cast("place_w_in", w_in_t, chip_arr, 288)
    wo = _place_cast("place_w_out", w_out, chip_arr, 256)
    send, recv, bufs, token = _gather_start([wi[0], wo[0], wi[1], wo[1]], halved=(0,))

    saved = []
    after = token
    for l in range(DEPTH):
        w_in_l = _gather_wait(f"gather_wait_in{l}", bufs[2 * l], send[2 * l], recv[2 * l], after, halved=(l == 0))
        if l == 0:
            w_in_l = _forward_halves("forward_w_in0", w_in_l)
        pu, pg, q, kv, ag = _fwd_in(l, xs, pre3, w_in_l)
        cat = _fwd_mix(l, pu, pg, q, kv, ag, pool_w_b, scale3, attn_sinks)
        w_out_l = _gather_wait(f"gather_wait_out{l}", bufs[2 * l + 1], send[2 * l + 1], recv[2 * l + 1], cat)
        if l < DEPTH - 1:
            y, x_next = _fwd_out(l, cat, w_out_l, xs, post3)
        else:
            y, x_next, loss = _fwd_out(l, cat, w_out_l, xs, post3, target)
        saved.append((xs, pu, pg, q, kv, ag, cat, y, w_in_l, w_out_l))
        xs = after = x_next

    x_in, pu, pg, q, kv, ag, cat, y, w_in_l, w_out_l = saved[1]
    dcat, dw_out1, dw_out1_b, dg_post1 = _bwd_out(1, xs, y, post3, cat, w_out_l)
    dproj, dpw1, dsc1, dsink1 = _bwd_mix(1, pu, pg, q, kv, ag, dcat, pool_w_b, scale3, attn_sinks)
    dx, dw_in1, dw_in1_b, dg_pre1 = _bwd_in(1, dproj, w_in_l, x_in, pre3, xs)
    ex1 = _exchange_start("exchange_start_1", [dw_in1_b, dw_out1_b])

    x_in, pu, pg, q, kv, ag, cat, y, w_in_l, w_out_l = saved[0]
    dcat, dw_out0, dw_out0_b, dg_post0 = _bwd_out(0, dx, y, post3, cat, w_out_l, deps=(ex1[4],))
    ex0_out = _exchange_start("exchange_start_out0", [dw_out0_b])
    dproj, dpw0, dsc0, dsink0 = _bwd_mix(0, pu, pg, q, kv, ag, dcat, pool_w_b, scale3, attn_sinks, deps=(ex0_out[4],))
    recv_in1, recv_out1 = _exchange_wait("exchange_wait_1", ex1, dproj)
    g_in = _sum_pieces("sum_pieces_in1", dw_in1, recv_in1, place_arr, 96, 1)
    g_out = _sum_pieces("sum_pieces_out1", dw_out1, recv_out1, place_arr, 64, 1)
    dw_in0, dw_in0_b = _bwd_in_dw(0, dproj, x_in, pre3, deps=(g_in, g_out))
    ex0_in = _exchange_start("exchange_start_in0", [dw_in0_b])

    grad_x, dg_pre0 = _bwd_in_dx(0, dproj, w_in_l, x_in, pre3, dx, deps=(ex0_in[4],))
    (recv_out0,) = _exchange_wait("exchange_wait_out0", ex0_out, grad_x)
    g_out = _sum_pieces("sum_pieces_out0", dw_out0, recv_out0, place_arr, 64, 0, dest=g_out)
    g_in, grad_w_out = _swap_halves("swap_halves_a", [g_in, g_out], [(0, 1), (1, 0), (1, 1)])
    m_in_t, v_in_t = t(m_w_in), t(v_w_in)
    d_out, nm_out, nv_out = _adamw("adamw_w_out", w_out, grad_w_out, m_w_out, v_w_out, 256)
    upd_in = _adamw("adamw_w_in1", w_in_t, g_in, m_in_t, v_in_t, 288, first=1, count=1)

    grads = [(None, None, dpw0, dsc0, dsink0, dg_pre0, dg_post0), (None, None, dpw1, dsc1, dsink1, dg_pre1, dg_post1)]
    packed = _pack_small_grads(grads, loss).reshape(8, PIECE_ROWS, 128)
    g_small = _allreduce_small(packed, deps=(upd_in[0], d_out)).reshape(1, SMALL_ROWS, 128)
    small = lambda *a: _pack_small(*a).reshape(1, SMALL_ROWS, 128)
    d_s, nm_s, nv_s = _adamw(
        "adamw_small", small(pool_w, pool_scale, attn_sinks, norm_pre, norm_post), g_small,
        small(m_pool_w, m_pool_scale, m_attn_sinks, m_norm_pre, m_norm_post),
        small(v_pool_w, v_pool_scale, v_attn_sinks, v_norm_pre, v_norm_post), SMALL_ROWS)

    (recv_in0,) = _exchange_wait("exchange_wait_in0", ex0_in, d_s)
    g_in = _sum_pieces("sum_pieces_in0", dw_in0, recv_in0, place_arr, 96, 0, dest=g_in)
    (grad_w_in_t,) = _swap_halves("swap_halves_b", [g_in], [(0, 0)])
    d_in, nm_in, nv_in = _adamw("adamw_w_in0", w_in_t, grad_w_in_t, m_in_t, v_in_t, 288, first=0, count=1, dests=upd_in)

    g_pw, g_sc, g_sk, g_pre, g_post = _unpack_small(g_small[0])
    d_pw, d_sc, d_sk, d_pre, d_post = _unpack_small(d_s[0])
    m_pw, m_sc, m_sk, m_pre, m_post = _unpack_small(nm_s[0])
    v_pw, v_sc, v_sk, v_pre, v_post = _unpack_small(nv_s[0])
    return (g_small[0, ROW_LOSS, 0], grad_x[None], t(grad_w_in_t), g_pw, g_sc, g_sk, grad_w_out, g_pre, g_post,
            t(d_in), d_pw, d_sc, d_sk, d_out, d_pre, d_post,
            t(nm_in), m_pw, m_sc, m_sk, nm_out, m_pre, m_post,
            t(nv_in), v_pw, v_sc, v_sk, nv_out, v_pre, v_post)
```

```python
import jax
import jax.numpy as jnp
from jax import lax
from jax.experimental import pallas as pl
from jax.experimental.pallas import tpu as pltpu

F32 = jnp.float32
BF16 = jnp.bfloat16

S = 2048
D = 1024
DEPTH = 2
D_POOL = 512
POOL_WINDOWS = (2, 4, 8, 16)
N_HEADS = 8
D_IN = 2304
N_SHARDS = 4
W_IN_SHARD = D_IN // N_SHARDS
W_OUT_SHARD = D // N_SHARDS
BLK = 128
NB = S // BLK
HALO = 16
EPS = 1e-6
NEG_INF = -1e30
C_PU, C_PG, C_Q, C_K, C_V, C_AG = 0, 512, 1024, 1536, 1664, 1792

ADAM_LR = 0.001
ADAM_B1 = 0.9
ADAM_B2 = 0.999
ADAM_EPS = 1e-08
ADAM_WD = 0.01
ADAM_STEP = 10

TM = 512
VMEM_LIMIT = 56 * 1024 * 1024

NT = (((1,), (1,)), ((), ()))
TN = (((0,), (0,)), ((), ()))

MESH = pl.DeviceIdType.MESH
ANY = pl.BlockSpec(memory_space=pl.ANY)

ROWS_PW, ROWS_SC, ROWS_NORM, ROWS_SINK = 1024, 8, 16, 8
ROW_LOSS = ROWS_PW + ROWS_SC + 2 * ROWS_NORM + ROWS_SINK
SMALL_ROWS = 1088
PIECE_ROWS = SMALL_ROWS // 8


def _params(sem=("arbitrary",)):
    return pltpu.CompilerParams(dimension_semantics=sem, vmem_limit_bytes=VMEM_LIMIT)


def _sigmoid(v):
    return 1.0 / (1.0 + jnp.exp(-v))


def _rows8(v):
    r, c = v.shape
    return v.reshape(r // 8, 8, c).sum(axis=0)


def _layer(l, *shape):
    zeros = (0,) * len(shape)
    return pl.BlockSpec((None,) + shape, lambda i: (l,) + zeros)


def _whole(shape):
    zeros = (0,) * len(shape)
    return pl.BlockSpec(shape, lambda i: zeros, pipeline_mode=pl.Buffered(1))


def _fwd_in(l, x, g_pre, w_in_t):
    def body(x_ref, g_ref, w_ref, pu_ref, pg_ref, q_ref, kv_ref, ag_ref):
        xt = x_ref[...]
        r = lax.rsqrt(jnp.mean(xt * xt, axis=-1, keepdims=True) + EPS)
        h = (xt * r * g_ref[...]).astype(BF16)

        def proj(lo, hi):
            return lax.dot_general(h, w_ref[lo:hi, :], NT, preferred_element_type=F32)

        pu_ref[...] = proj(C_PU, C_PG)
        pg_ref[...] = proj(C_PG, C_Q)
        q_ref[...] = proj(C_Q, C_K).astype(BF16)
        kv_ref[...] = proj(C_K, C_AG).astype(BF16)
        ag_ref[...] = proj(C_AG, D_IN)

    row = lambda w: pl.BlockSpec((TM, w), lambda i: (i, 0))
    return pl.pallas_call(
        body, name="fwd_in", grid=(S // TM,),
        in_specs=[row(D), _layer(l, 1, D), _whole((D_IN, D))],
        out_specs=[row(512), row(512), row(512), row(256), row(512)],
        out_shape=[jax.ShapeDtypeStruct((S, 512), F32), jax.ShapeDtypeStruct((S, 512), F32),
                   jax.ShapeDtypeStruct((S, 512), BF16), jax.ShapeDtypeStruct((S, 256), BF16),
                   jax.ShapeDtypeStruct((S, 512), F32)],
        compiler_params=_params(),
    )(x, g_pre, w_in_t)


LOG2E = 1.4426950408889634
SCORE_SCALE = 0.125 * LOG2E


def _attention_tables():
    qi = jnp.arange(BLK)[:, None]
    kj = jnp.arange(BLK)[None, :]
    dist = ((qi - kj) % BLK).astype(F32)
    slopes = jnp.exp2(-jnp.arange(1, N_HEADS + 1, dtype=F32))
    bias = -(slopes * LOG2E)[:, None, None] * dist[None]
    first = jnp.where(kj > qi, NEG_INF, bias)
    return jnp.stack([first, bias]), (kj <= qi).astype(BF16)


def _own_block_mask():
    return lax.broadcasted_iota(jnp.int32, (BLK, BLK), 1) <= lax.broadcasted_iota(jnp.int32, (BLK, BLK), 0)


def _merge(full, own):
    return jnp.where(own, full[:, BLK:], full[:, :BLK])


def _spread(v, tri):
    own = v * tri
    return jnp.concatenate([v - own, own], axis=1)


def _head_variants(cur, prev):
    both = jnp.concatenate([prev, cur], axis=0).astype(F32)
    swapped = pltpu.roll(both, 64, axis=1)
    low = lax.broadcasted_iota(jnp.int32, both.shape, 1) < 64
    zero = jnp.zeros_like(both)
    return ((jnp.where(low, both, zero).astype(BF16), jnp.where(low, zero, swapped).astype(BF16)),
            (jnp.where(low, swapped, zero).astype(BF16), jnp.where(low, zero, both).astype(BF16)))


def _probs(q_tile, k_var, bias, sink, own):
    s = _merge(lax.dot_general(q_tile, k_var, NT, preferred_element_type=F32), own) * SCORE_SCALE + bias
    sink2 = sink * LOG2E
    m = jnp.maximum(jnp.max(s, axis=-1, keepdims=True), sink2)
    p = jnp.exp2(s - m)
    e_sink = jnp.exp2(sink2 - m)
    inv = 1.0 / (jnp.sum(p, axis=-1, keepdims=True) + e_sink)
    return p * inv, e_sink * inv


def _pool_block(ext_ref, i, g, w):
    lanes = slice(g * 128, (g + 1) * 128)
    u = ext_ref[HALO:HALO + BLK, lanes]
    acc = u
    for j in range(1, w):
        acc = acc + ext_ref[HALO - j:HALO - j + BLK, lanes]
    t = (i * BLK + lax.broadcasted_iota(jnp.int32, (BLK, 1), 0)).astype(F32)
    inv = 1.0 / jnp.minimum(t + 1.0, float(w))
    return acc * inv - u, inv


def _fwd_mix(l, pu, pg, q, kv, ag, pool_w, pool_scale, sinks, bias, tri):
    def body(pu_ref, pup_ref, pg_ref, q_ref, kv_ref, kvp_ref, ag_ref, pw_ref, sc_ref, sink_ref, bias_ref, tri_ref,
             cat_ref, ext_ref):
        i = pl.program_id(0)
        ext_ref[0:HALO, :] = jnp.where(i > 0, pup_ref[...], 0.0)
        ext_ref[HALO:HALO + BLK, :] = pu_ref[...]
        for g, w in enumerate(POOL_WINDOWS):
            lanes = slice(g * 128, (g + 1) * 128)
            pooled, _ = _pool_block(ext_ref, i, g, w)
            mixed = jnp.dot(pooled.astype(BF16), pw_ref[g], preferred_element_type=F32)
            gate = pg_ref[:, lanes]
            cat_ref[:, lanes] = (mixed * sc_ref[:, lanes] * (gate * _sigmoid(gate))).astype(BF16)

        own = _own_block_mask()
        tri = tri_ref[...]
        k_var = _head_variants(kv_ref[:, 0:128], kvp_ref[:, 0:128])
        v_var = _head_variants(kv_ref[:, 128:256], kvp_ref[:, 128:256])
        for j in range(4):
            hkv = j // 2
            lanes = slice(j * 128, (j + 1) * 128)
            q_tile = q_ref[:, lanes]
            o = jnp.zeros((BLK, 128), F32)
            for half in range(2):
                head = hkv * 4 + 2 * (j % 2) + half
                p, _ = _probs(q_tile, k_var[hkv][half], bias_ref[head], sink_ref[l, head], own)
                o = o + jnp.dot(_spread(p.astype(BF16), tri), v_var[hkv][half], preferred_element_type=F32)
            gate = ag_ref[:, lanes]
            cat_ref[:, D_POOL + j * 128:D_POOL + (j + 1) * 128] = (o * (gate * _sigmoid(gate))).astype(BF16)

    blk = lambda w: pl.BlockSpec((BLK, w), lambda i: (i, 0))
    prev = lambda w: pl.BlockSpec((BLK, w), lambda i: (jnp.maximum(i - 1, 0), 0))
    halo = pl.BlockSpec((HALO, 512), lambda i: (jnp.maximum(i * (BLK // HALO) - 1, 0), 0))
    return pl.pallas_call(
        body, name="fwd_mix", grid=(NB,),
        in_specs=[blk(512), halo, blk(512), blk(512), blk(256), prev(256), blk(512),
                  _layer(l, 4, 128, 128), _layer(l, 1, 512), pl.BlockSpec(memory_space=pltpu.SMEM),
                  pl.BlockSpec((None, N_HEADS, BLK, BLK), lambda i: (jnp.minimum(i, 1), 0, 0, 0)), _whole((BLK, BLK))],
        out_specs=blk(D),
        out_shape=jax.ShapeDtypeStruct((S, D), BF16),
        scratch_shapes=[pltpu.VMEM((HALO + BLK, 512), F32)],
        compiler_params=_params(),
    )(pu, pu, pg, q, kv, kv, ag, pool_w, pool_scale, sinks, bias, tri)


def _fwd_out(l, cat, w_out, x, g_post, target=None):
    last = target is not None
    n_steps = S // TM

    def body(*refs):
        if last:
            cat_ref, w_ref, x_ref, g_ref, t_ref, y_ref, dx_ref, loss_ref, acc_ref = refs
        else:
            cat_ref, w_ref, x_ref, g_ref, y_ref, xn_ref = refs
        y = jnp.dot(cat_ref[...], w_ref[...], preferred_element_type=F32)
        y_ref[...] = y
        r = lax.rsqrt(jnp.mean(y * y, axis=-1, keepdims=True) + EPS)
        xn = x_ref[...] + y * r * g_ref[...]
        if not last:
            xn_ref[...] = xn
            return
        step = pl.program_id(0)
        err = xn - t_ref[...]
        dx_ref[...] = err * (1.0 / D)

        @pl.when(step == 0)
        def _():
            acc_ref[...] = jnp.zeros_like(acc_ref)

        acc_ref[...] += _rows8(err * err)

        @pl.when(step == n_steps - 1)
        def _():
            loss_ref[...] = jnp.full((8, 128), (0.5 / D) * jnp.sum(acc_ref[...]), F32)

    row = lambda: pl.BlockSpec((TM, D), lambda i: (i, 0))
    act = jax.ShapeDtypeStruct((S, D), F32)
    in_specs = [row(), _whole((D, D)), row(), _layer(l, 1, D)]
    args = [cat, w_out, x, g_post]
    if last:
        return pl.pallas_call(
            body, name="fwd_out_loss", grid=(n_steps,),
            in_specs=in_specs + [row()],
            out_specs=[row(), row(), pl.BlockSpec((8, 128), lambda i: (0, 0))],
            out_shape=[act, act, jax.ShapeDtypeStruct((8, 128), F32)],
            scratch_shapes=[pltpu.VMEM((8, D), F32)],
            compiler_params=_params(),
        )(*args, target)
    return pl.pallas_call(
        body, name="fwd_out", grid=(n_steps,),
        in_specs=in_specs, out_specs=[row(), row()], out_shape=[act, act],
        compiler_params=_params(),
    )(*args)


def _store_lane_rows(ref, acc):
    total = jnp.sum(acc, axis=0, keepdims=True)
    for k in range(ref.shape[0]):
        ref[k:k + 1, :] = total[:, k * 128:(k + 1) * 128]


def _bwd_out(l, dxn, y, g_post, cat, w_out, deps=()):
    n_steps = S // TM

    def body(dz_ref, y_ref, g_ref, cat_ref, w_ref, *rest):
        dcat_ref, dw_ref, dwb_ref, dg_ref, acc_ref = rest[len(deps):]
        step = pl.program_id(0)

        @pl.when(step == 0)
        def _():
            dw_ref[...] = jnp.zeros_like(dw_ref)
            acc_ref[...] = jnp.zeros_like(acc_ref)

        y = y_ref[...]
        dz = dz_ref[...]
        r = lax.rsqrt(jnp.mean(y * y, axis=-1, keepdims=True) + EPS)
        a = dz * g_ref[...]
        dy = r * a - y * (r * r * r) * jnp.mean(a * y, axis=-1, keepdims=True)
        acc_ref[...] += _rows8(dz * (y * r))
        dyb = dy.astype(BF16)
        dcat_ref[...] = lax.dot_general(dyb, w_ref[...], NT, preferred_element_type=F32)
        dw_ref[...] += lax.dot_general(cat_ref[...], dyb, TN, preferred_element_type=F32)

        @pl.when(step == n_steps - 1)
        def _():
            _store_lane_rows(dg_ref, acc_ref[...])
            dwb_ref[...] = dw_ref[...].astype(BF16)

    row = lambda: pl.BlockSpec((TM, D), lambda i: (i, 0))
    full = _whole
    return pl.pallas_call(
        body, name="bwd_out", grid=(n_steps,),
        in_specs=[row(), row(), _layer(l, 1, D), row(), full((D, D))] + [ANY] * len(deps),
        out_specs=[row(), full((D, D)), full((D, D)), full((8, 128))],
        out_shape=[jax.ShapeDtypeStruct((S, D), F32), jax.ShapeDtypeStruct((D, D), F32),
                   jax.ShapeDtypeStruct((D, D), BF16), jax.ShapeDtypeStruct((8, 128), F32)],
        scratch_shapes=[pltpu.VMEM((8, D), F32)],
        compiler_params=_params(),
    )(dxn, y, g_post, cat, w_out, *deps)


def _bwd_mix(l, pu, pg, q, kv, ag, dcat, pool_w, pool_scale, sinks, bias, tri, deps=()):
    def body(pu_ref, pup_ref, pg_ref, q_ref, kv_ref, kvp_ref, ag_ref, dcat_ref, pw_ref, sc_ref, sink_ref, bias_ref,
             tri_ref, *rest):
        dproj_ref, dpw_ref, dsc_ref, dsink_ref, ext_ref, dext_ref, dkv_ref = rest[len(deps):]
        step = pl.program_id(0)
        i = NB - 1 - step

        @pl.when(step == 0)
        def _():
            dpw_ref[...] = jnp.zeros_like(dpw_ref)
            dsc_ref[...] = jnp.zeros_like(dsc_ref)
            dsink_ref[...] = jnp.zeros_like(dsink_ref)
            dext_ref[BLK:BLK + HALO, :] = jnp.zeros((HALO, 512), F32)
            dkv_ref[...] = jnp.zeros_like(dkv_ref)

        ext_ref[0:HALO, :] = jnp.where(i > 0, pup_ref[...], 0.0)
        ext_ref[HALO:HALO + BLK, :] = pu_ref[...]
        for g, w in enumerate(POOL_WINDOWS):
            lanes = slice(g * 128, (g + 1) * 128)
            pooled, inv = _pool_block(ext_ref, i, g, w)
            pooled_b = pooled.astype(BF16)
            mixed = jnp.dot(pooled_b, pw_ref[g], preferred_element_type=F32)
            scale = sc_ref[:, lanes]
            gate = pg_ref[:, lanes]
            sg = _sigmoid(gate)
            dpo = dcat_ref[:, lanes]
            dproj_ref[:, C_PG + g * 128:C_PG + (g + 1) * 128] = (
                dpo * (mixed * scale) * (sg * (1.0 + gate * (1.0 - sg)))).astype(BF16)
            dms = dpo * (gate * sg)
            dsc_ref[g:g + 1, :] += jnp.sum(dms * mixed, axis=0, keepdims=True)
            dmixed = (dms * scale).astype(BF16)
            dpw_ref[g] += lax.dot_general(pooled_b, dmixed, TN, preferred_element_type=F32)
            dpooled = lax.dot_general(dmixed, pw_ref[g], NT, preferred_element_type=F32)
            dext_ref[0:BLK, lanes] = dpooled * inv
            acc = dext_ref[0:BLK, lanes]
            for j in range(1, w):
                acc = acc + dext_ref[j:j + BLK, lanes]
            dproj_ref[:, C_PU + g * 128:C_PU + (g + 1) * 128] = (acc - dpooled).astype(BF16)
        dext_ref[BLK:BLK + HALO, :] = dext_ref[0:HALO, :]

        own = _own_block_mask()
        tri = tri_ref[...]
        k_var = _head_variants(kv_ref[:, 0:128], kvp_ref[:, 0:128])
        v_var = _head_variants(kv_ref[:, 128:256], kvp_ref[:, 128:256])
        zero = jnp.zeros((2 * BLK, 128), F32)
        dk_acc = [[zero, zero], [zero, zero]]
        dv_acc = [[zero, zero], [zero, zero]]
        for j in range(4):
            hkv = j // 2
            lanes = slice(j * 128, (j + 1) * 128)
            q_tile = q_ref[:, lanes]
            gate = ag_ref[:, lanes]
            sg = _sigmoid(gate)
            dca = dcat_ref[:, D_POOL + j * 128:D_POOL + (j + 1) * 128]
            do_b = (dca * (gate * sg)).astype(BF16)
            probs = []
            o = jnp.zeros((BLK, 128), F32)
            for half in range(2):
                head = hkv * 4 + 2 * (j % 2) + half
                p, p_sink = _probs(q_tile, k_var[hkv][half], bias_ref[head], sink_ref[l, head], own)
                p_b = _spread(p.astype(BF16), tri)
                probs.append((p, p_sink, p_b))
                o = o + jnp.dot(p_b, v_var[hkv][half], preferred_element_type=F32)
            dproj_ref[:, C_AG + j * 128:C_AG + (j + 1) * 128] = (
                dca * o * (sg * (1.0 + gate * (1.0 - sg)))).astype(BF16)
            dq = jnp.zeros((BLK, 128), F32)
            for half in range(2):
                head = hkv * 4 + 2 * (j % 2) + half
                p, p_sink, p_b = probs[half]
                dp = _merge(lax.dot_general(do_b, v_var[hkv][half], NT, preferred_element_type=F32), own)
                delta = jnp.sum(p * dp, axis=-1, keepdims=True)
                ds_b = _spread((p * (dp - delta)).astype(BF16), tri)
                dsink_ref[head:head + 1, :] += jnp.broadcast_to(
                    -jnp.sum(p_sink * delta, axis=0, keepdims=True), (1, 128))
                dq = dq + jnp.dot(ds_b, k_var[hkv][half], preferred_element_type=F32)
                dk_acc[hkv][half] = dk_acc[hkv][half] + lax.dot_general(
                    ds_b, q_tile, TN, preferred_element_type=F32)
                dv_acc[hkv][half] = dv_acc[hkv][half] + lax.dot_general(
                    p_b, do_b, TN, preferred_element_type=F32)
            dproj_ref[:, C_Q + j * 128:C_Q + (j + 1) * 128] = (dq * 0.125).astype(BF16)

        low = lax.broadcasted_iota(jnp.int32, (2 * BLK, 128), 1) < 64

        def gather_heads(acc):
            return jnp.where(low, acc[0][0] + pltpu.roll(acc[0][1], 64, axis=1),
                             pltpu.roll(acc[1][0], 64, axis=1) + acc[1][1])

        dk = gather_heads(dk_acc) * 0.125
        dv = gather_heads(dv_acc)
        dproj_ref[:, C_K:C_V] = (dk[BLK:, :] + dkv_ref[:, 0:128]).astype(BF16)
        dproj_ref[:, C_V:C_AG] = (dv[BLK:, :] + dkv_ref[:, 128:256]).astype(BF16)
        dkv_ref[:, 0:128] = dk[:BLK, :]
        dkv_ref[:, 128:256] = dv[:BLK, :]

    rev = lambda w: pl.BlockSpec((BLK, w), lambda s: (NB - 1 - s, 0))
    prev = lambda w: pl.BlockSpec((BLK, w), lambda s: (jnp.maximum(NB - 2 - s, 0), 0))
    halo = pl.BlockSpec((HALO, 512), lambda s: (jnp.maximum((NB - 1 - s) * (BLK // HALO) - 1, 0), 0))
    return pl.pallas_call(
        body, name="bwd_mix", grid=(NB,),
        in_specs=[rev(512), halo, rev(512), rev(512), rev(256), prev(256), rev(512), rev(D),
                  _layer(l, 4, 128, 128), _layer(l, 1, 512), pl.BlockSpec(memory_space=pltpu.SMEM),
                  pl.BlockSpec((None, N_HEADS, BLK, BLK), lambda s: (jnp.minimum(NB - 1 - s, 1), 0, 0, 0)),
                  _whole((BLK, BLK))] + [ANY] * len(deps),
        out_specs=[rev(D_IN), pl.BlockSpec((4, 128, 128), lambda s: (0, 0, 0)),
                   pl.BlockSpec((4, 128), lambda s: (0, 0)), pl.BlockSpec((8, 128), lambda s: (0, 0))],
        out_shape=[jax.ShapeDtypeStruct((S, D_IN), BF16), jax.ShapeDtypeStruct((4, 128, 128), F32),
                   jax.ShapeDtypeStruct((4, 128), F32), jax.ShapeDtypeStruct((8, 128), F32)],
        scratch_shapes=[pltpu.VMEM((HALO + BLK, 512), F32), pltpu.VMEM((BLK + HALO, 512), F32),
                        pltpu.VMEM((BLK, 256), F32)],
        compiler_params=_params(),
    )(pu, pu, pg, q, kv, kv, ag, dcat, pool_w, pool_scale, sinks, bias, tri, *deps)


def _bwd_in(l, dproj, w_in_t, x, g_pre, dres, deps=()):
    n_steps = S // TM

    def body(dp_ref, w_ref, x_ref, g_ref, dres_ref, *rest):
        dx_ref, dw_ref, dwb_ref, dg_ref, acc_ref = rest[len(deps):]
        step = pl.program_id(0)

        @pl.when(step == 0)
        def _():
            dw_ref[...] = jnp.zeros_like(dw_ref)
            acc_ref[...] = jnp.zeros_like(acc_ref)

        dp = dp_ref[...]
        dh = jnp.dot(dp, w_ref[...], preferred_element_type=F32)
        xt = x_ref[...]
        r = lax.rsqrt(jnp.mean(xt * xt, axis=-1, keepdims=True) + EPS)
        xn = xt * r
        g = g_ref[...]
        acc_ref[...] += _rows8(dh * xn)
        a = dh * g
        dx_ref[...] = dres_ref[...] + (r * a - xt * (r * r * r) * jnp.mean(a * xt, axis=-1, keepdims=True))
        dw_ref[...] += lax.dot_general(dp, (xn * g).astype(BF16), TN, preferred_element_type=F32)

        @pl.when(step == n_steps - 1)
        def _():
            _store_lane_rows(dg_ref, acc_ref[...])
            dwb_ref[...] = dw_ref[...].astype(BF16)

    row = lambda w: pl.BlockSpec((TM, w), lambda i: (i, 0))
    full = _whole
    return pl.pallas_call(
        body, name="bwd_in", grid=(n_steps,),
        in_specs=[row(D_IN), full((D_IN, D)), row(D), _layer(l, 1, D), row(D)] + [ANY] * len(deps),
        out_specs=[row(D), full((D_IN, D)), full((D_IN, D)), full((8, 128))],
        out_shape=[jax.ShapeDtypeStruct((S, D), F32), jax.ShapeDtypeStruct((D_IN, D), F32),
                   jax.ShapeDtypeStruct((D_IN, D), BF16), jax.ShapeDtypeStruct((8, 128), F32)],
        scratch_shapes=[pltpu.VMEM((8, D), F32)],
        compiler_params=_params(),
    )(dproj, w_in_t, x, g_pre, dres, *deps)


def _bwd_in_dw(l, dproj, x, g_pre, deps=()):
    n_steps = S // TM

    def body(dp_ref, x_ref, g_ref, *rest):
        dw_ref, dwb_ref = rest[len(deps):]
        step = pl.program_id(0)

        @pl.when(step == 0)
        def _():
            dw_ref[...] = jnp.zeros_like(dw_ref)

        xt = x_ref[...]
        r = lax.rsqrt(jnp.mean(xt * xt, axis=-1, keepdims=True) + EPS)
        h = (xt * r * g_ref[...]).astype(BF16)
        dw_ref[...] += lax.dot_general(dp_ref[...], h, TN, preferred_element_type=F32)

        @pl.when(step == n_steps - 1)
        def _():
            dwb_ref[...] = dw_ref[...].astype(BF16)

    row = lambda w: pl.BlockSpec((TM, w), lambda i: (i, 0))
    full = _whole
    return pl.pallas_call(
        body, name="bwd_in_dw", grid=(n_steps,),
        in_specs=[row(D_IN), row(D), _layer(l, 1, D)] + [ANY] * len(deps),
        out_specs=[full((D_IN, D)), full((D_IN, D))],
        out_shape=[jax.ShapeDtypeStruct((D_IN, D), F32), jax.ShapeDtypeStruct((D_IN, D), BF16)],
        compiler_params=_params(),
    )(dproj, x, g_pre, *deps)


def _bwd_in_dx(l, dproj, w_in_t, x, g_pre, dres, deps=()):
    n_steps = S // TM

    def body(dp_ref, w_ref, x_ref, g_ref, dres_ref, *rest):
        dx_ref, dg_ref, acc_ref = rest[len(deps):]
        step = pl.program_id(0)

        @pl.when(step == 0)
        def _():
            acc_ref[...] = jnp.zeros_like(acc_ref)

        dh = jnp.dot(dp_ref[...], w_ref[...], preferred_element_type=F32)
        xt = x_ref[...]
        r = lax.rsqrt(jnp.mean(xt * xt, axis=-1, keepdims=True) + EPS)
        xn = xt * r
        acc_ref[...] += _rows8(dh * xn)
        a = dh * g_ref[...]
        dx_ref[...] = dres_ref[...] + (r * a - xt * (r * r * r) * jnp.mean(a * xt, axis=-1, keepdims=True))

        @pl.when(step == n_steps - 1)
        def _():
            _store_lane_rows(dg_ref, acc_ref[...])

    row = lambda w: pl.BlockSpec((TM, w), lambda i: (i, 0))
    full = _whole
    return pl.pallas_call(
        body, name="bwd_in_dx", grid=(n_steps,),
        in_specs=[row(D_IN), full((D_IN, D)), row(D), _layer(l, 1, D), row(D)] + [ANY] * len(deps),
        out_specs=[row(D), full((8, 128))],
        out_shape=[jax.ShapeDtypeStruct((S, D), F32), jax.ShapeDtypeStruct((8, 128), F32)],
        scratch_shapes=[pltpu.VMEM((8, D), F32)],
        compiler_params=_params(),
    )(dproj, w_in_t, x, g_pre, dres, *deps)


HBM =pl.BlockSpec(memory_space=pltpu.HBM)
SEM = pl.BlockSpec(memory_space=pltpu.SEMAPHORE)
SPLIT_COPY = pltpu.CompilerParams(has_side_effects=pltpu.SideEffectType.DATAFLOW_SIDE_EFFECTING)


def _in_hbm(a):
    return pltpu.with_memory_space_constraint(a, pltpu.HBM)

def _place():
    return lax.axis_index("x"), lax.axis_index("y"), lax.axis_index("c")


def _other_chips(x, y):
    return [(1 - x, y), (x, 1 - y), (1 - x, 1 - y)]


def _peer(x, y, c, m):
    return (x ^ (m >> 2), y ^ ((m >> 1) & 1), c ^ (m & 1))


def _place_cast(name, src, chip_arr, tile):
    _, n, cols = src.shape
    steps = n // tile

    def body(chip_ref, s0_ref, s1_ref, o0_ref, o1_ref):
        o0_ref[...] = s0_ref[...].astype(BF16)
        o1_ref[...] = s1_ref[...].astype(BF16)

    return pl.pallas_call(
        body, name=name,
        grid_spec=pltpu.PrefetchScalarGridSpec(
            num_scalar_prefetch=1, grid=(steps,),
            in_specs=[pl.BlockSpec((None, tile, cols), lambda i, chip: (0, i, 0)),
                      pl.BlockSpec((None, tile, cols), lambda i, chip: (1, i, 0))],
            out_specs=[pl.BlockSpec((tile, cols), lambda i, chip: (chip[0] * steps + i, 0))] * 2),
        out_shape=[jax.ShapeDtypeStruct((N_SHARDS * n, cols), BF16)] * 2,
        compiler_params=_params(),
    )(chip_arr, src, src)


def _chip_rows(ref, chip, half=None):
    n = ref.shape[0] // N_SHARDS
    if half is None:
        return ref.at[pl.ds(pl.multiple_of(chip * n, 16), n), :]
    return ref.at[pl.ds(pl.multiple_of(chip * n + half * (n // 2), 16), n // 2), :]


def _gather_start(bufs, halved):
    n = len(bufs)

    def body(*refs):
        ins, send, recv, token = refs[:n], refs[n:2 * n], refs[2 * n:3 * n], refs[-1]
        x, y, c = _place()
        for a, buf in enumerate(ins):
            own = _chip_rows(buf, 2 * x + y, c if a in halved else None)
            for j, chip in enumerate(_other_chips(x, y)):
                pltpu.make_async_remote_copy(src_ref=own, dst_ref=own, send_sem=send[a].at[j], recv_sem=recv[a].at[j],
                                             device_id=(*chip, c), device_id_type=MESH).start()
        token[...] = jnp.zeros_like(token)

    outs = pl.pallas_call(
        body, name="gather_start", in_specs=[HBM] * n,
        out_specs=[SEM] * (2 * n) + [HBM] * n + [pl.BlockSpec(memory_space=pltpu.VMEM)],
        out_shape=[pltpu.SemaphoreType.DMA((3,))] * (2 * n) + [pltpu.HBM(b.shape, b.dtype) for b in bufs]
        + [jax.ShapeDtypeStruct((8, 128), F32)],
        input_output_aliases={a: 2 * n + a for a in range(n)},
        compiler_params=SPLIT_COPY,
    )(*[_in_hbm(b) for b in bufs])
    return outs[:n], outs[n:2 * n], outs[2 * n:3 * n], outs[-1]


def _gather_wait(name, buf, send_sem, recv_sem, after, halved=False):
    def body(buf_ref, send_ref, recv_ref, after_ref, out_ref):
        x, y, c = _place()
        half = c if halved else None
        own = _chip_rows(buf_ref, 2 * x + y, half)
        for j, chip in enumerate(_other_chips(x, y)):
            copy = pltpu.make_async_remote_copy(src_ref=own, dst_ref=_chip_rows(buf_ref, 2 * chip[0] + chip[1], half),
                                                send_sem=send_ref.at[j], recv_sem=recv_ref.at[j],
                                                device_id=(*chip, c), device_id_type=MESH)
            copy.wait_send()
            copy.wait_recv()

    return pl.pallas_call(
        body, name=name, in_specs=[HBM, SEM, SEM, ANY], out_specs=HBM, out_shape=pltpu.HBM(buf.shape, buf.dtype),
        input_output_aliases={0: 0}, compiler_params=SPLIT_COPY,
    )(buf, send_sem, recv_sem, after)


def _forward_halves(name, buf):
    def body(in_ref, out_ref, send_sems, recv_sems):
        x, y, c = _place()

        def copy(j, chip, half):
            rows = 2 * chip[0] + chip[1]
            return pltpu.make_async_remote_copy(
                src_ref=_chip_rows(in_ref, rows, half), dst_ref=_chip_rows(out_ref, rows, half), send_sem=send_sems.at[j],
                recv_sem=recv_sems.at[j], device_id=(x, y, 1 - c), device_id_type=MESH)

        chips = _other_chips(x, y)
        for j, chip in enumerate(chips):
            copy(j, chip, c).start()
        for j, chip in enumerate(chips):
            copy(j, chip, c).wait_send()
            copy(j, chip, 1 - c).wait_recv()

    return pl.pallas_call(
        body, name=name, in_specs=[ANY], out_specs=ANY, out_shape=jax.ShapeDtypeStruct(buf.shape, buf.dtype),
        input_output_aliases={0: 0},
        scratch_shapes=[pltpu.SemaphoreType.DMA((3,))] * 2,
    )(buf)


def _piece_rows(ref, k):
    p = ref.shape[0] // 8
    return ref.at[pl.ds(pl.multiple_of(k * p, 16), p), :]


def _exchange_start(name, arrays):
    n = len(arrays)
    zones = [lax.empty((7, a.shape[0] // 8, a.shape[1]), BF16) for a in arrays]

    def body(*refs):
        srcs, lands = refs[:n], refs[n:2 * n]
        send, recv, token = refs[2 * n:3 * n], refs[3 * n:4 * n], refs[-1]
        x, y, c = _place()
        for a, (src, land) in enumerate(zip(srcs, lands)):
            for m in range(1, 8):
                px, py, pc = _peer(x, y, c, m)
                pltpu.make_async_remote_copy(
                    src_ref=_piece_rows(src, 4 * px + 2 * py + pc), dst_ref=land.at[m - 1], send_sem=send[a].at[m - 1],
                    recv_sem=recv[a].at[m - 1], device_id=(px, py, pc), device_id_type=MESH).start()
        token[...] = jnp.zeros_like(token)

    outs = pl.pallas_call(
        body, name=name, in_specs=[HBM] * (2 * n),
        out_specs=[SEM] * (2 * n) + [HBM] * (2 * n) + [pl.BlockSpec(memory_space=pltpu.VMEM)],
        out_shape=[pltpu.SemaphoreType.DMA((7,))] * (2 * n) + [pltpu.HBM(a.shape, a.dtype) for a in arrays + zones]
        + [jax.ShapeDtypeStruct((8, 128), F32)],
        input_output_aliases={a: 2 * n + a for a in range(2 * n)},
        compiler_params=SPLIT_COPY,
    )(*[_in_hbm(a) for a in arrays + zones])
    return outs[:n], outs[n:2 * n], outs[2 * n:3 * n], outs[3 * n:4 * n], outs[-1]


def _exchange_wait(name, started, after):
    send_sems, recv_sems, arrays, zones, _ = started
    n = len(arrays)

    def body(*refs):
        srcs, lands = refs[:n], refs[n:2 * n]
        send, recv = refs[2 * n:3 * n], refs[3 * n:4 * n]
        x, y, c = _place()
        for a, (src, land) in enumerate(zip(srcs, lands)):
            for m in range(1, 8):
                px, py, pc = _peer(x, y, c, m)
                copy = pltpu.make_async_remote_copy(
                    src_ref=_piece_rows(src, 4 * px + 2 * py + pc), dst_ref=land.at[m - 1], send_sem=send[a].at[m - 1],
                    recv_sem=recv[a].at[m - 1], device_id=(px, py, pc), device_id_type=MESH)
                copy.wait_send()
                copy.wait_recv()

    outs = pl.pallas_call(
        body, name=name, in_specs=[HBM] * (2 * n) + [SEM] * (2 * n) + [ANY], out_specs=[HBM] * (2 * n),
        out_shape=[pltpu.HBM(a.shape, a.dtype) for a in list(arrays) + list(zones)],
        input_output_aliases={a: a for a in range(2 * n)}, compiler_params=SPLIT_COPY,
    )(*arrays, *zones, *send_sems, *recv_sems, after)
    return outs[n:]


def _sum_pieces(name, partial, recv, place_arr, tile, layer, dest=None):
    p, cols = recv.shape[1:]
    steps = p // tile

    def body(place_ref, o_ref, r_ref, *rest):
        total = o_ref[...]
        for m in range(7):
            total = total + r_ref[m].astype(F32)
        rest[-1][...] = total

    return pl.pallas_call(
        body, name=name,
        grid_spec=pltpu.PrefetchScalarGridSpec(
            num_scalar_prefetch=1, grid=(steps,),
            in_specs=[pl.BlockSpec((tile, cols), lambda i, place: (place[0] * steps + i, 0)),
                      pl.BlockSpec((7, tile, cols), lambda i, place: (0, i, 0))] + ([] if dest is None else [ANY]),
            out_specs=pl.BlockSpec((None, tile, cols), lambda i, place: (layer, place[1] * steps + i, 0))),
        out_shape=jax.ShapeDtypeStruct((DEPTH, 2 * p, cols), F32),
        input_output_aliases={} if dest is None else {3: 0},
        compiler_params=_params(),
    )(place_arr, partial, recv, *(() if dest is None else (dest,)))


def _swap_halves(name, bufs, parts):
    n = len(bufs)

    def body(*refs):
        ins, outs, send_sems, recv_sems = refs[:n], refs[n:2 * n], refs[2 * n], refs[2 * n + 1]
        x, y, c = _place()

        def half(ref, l, which):
            p = ref.shape[1] // 2
            return ref.at[l, pl.ds(pl.multiple_of(which * p, 8), p), :]

        def copy(k, which):
            a, l = parts[k]
            return pltpu.make_async_remote_copy(
                src_ref=half(ins[a], l, which), dst_ref=half(outs[a], l, which), send_sem=send_sems.at[k],
                recv_sem=recv_sems.at[k], device_id=(x, y, 1 - c), device_id_type=MESH)

        for k in range(len(parts)):
            copy(k, c).start()
        for k in range(len(parts)):
            copy(k, c).wait_send()
            copy(k, 1 - c).wait_recv()

    return pl.pallas_call(
        body, name=name, in_specs=[ANY] * n, out_specs=[ANY] * n,
        out_shape=[jax.ShapeDtypeStruct(b.shape, F32) for b in bufs],
        input_output_aliases={a: a for a in range(n)},
        scratch_shapes=[pltpu.SemaphoreType.DMA((len(parts),))] * 2,
    )(*bufs)


def _allreduce_small(packed, deps=()):
    def body(p_ref, *rest):
        out_ref, recv_ref, send1, recv1, send2, recv2 = rest[len(deps):]
        x, y, c = _place()
        me = 4 * x + 2 * y + c

        def index(p):
            return 4 * p[0] + 2 * p[1] + p[2]

        def copy(src, dst, sems, m):
            return pltpu.make_async_remote_copy(src_ref=src, dst_ref=dst, send_sem=sems[0].at[m - 1],
                                                recv_sem=sems[1].at[m - 1], device_id=_peer(x, y, c, m),
                                                device_id_type=MESH)

        first = [copy(p_ref.at[index(_peer(x, y, c, m))], recv_ref.at[me], (send1, recv1), m) for m in range(1, 8)]
        for cp in first:
            cp.start()
        recv_ref[me] = p_ref[me]
        for m in range(1, 8):
            slot = recv_ref.at[index(_peer(x, y, c, m))]
            copy(slot, slot, (send1, recv1), m).wait_recv()
        total = recv_ref[0]
        for d in range(1, 8):
            total = total + recv_ref[d]
        out_ref[me] = total
        second = [copy(out_ref.at[me], out_ref.at[me], (send2, recv2), m) for m in range(1, 8)]
        for cp in second:
            cp.start()
        for m in range(1, 8):
            slot = out_ref.at[index(_peer(x, y, c, m))]
            copy(slot, slot, (send2, recv2), m).wait_recv()
        for cp in first + second:
            cp.wait_send()

    vmem = pl.BlockSpec(memory_space=pltpu.VMEM)
    return pl.pallas_call(
        body, name="allreduce_small", in_specs=[vmem] + [ANY] * len(deps), out_specs=vmem,
        out_shape=jax.ShapeDtypeStruct(packed.shape, F32),
        scratch_shapes=[pltpu.VMEM(packed.shape, F32)] + [pltpu.SemaphoreType.DMA((7,))] * 4,
    )(packed, *deps)


def _adamw(name, w, g, m, v, rows_per_step, first=0, count=None, dests=None):
    layers, rows, cols = w.shape
    count = layers if count is None else count

    def body(w_ref, g_ref, m_ref, v_ref, *rest):
        d_ref, nm_ref, nv_ref = rest[-3:]
        gt = g_ref[...]
        nm = ADAM_B1 * m_ref[...] + (1.0 - ADAM_B1) * gt
        nv = ADAM_B2 * v_ref[...] + (1.0 - ADAM_B2) * (gt * gt)
        m_hat = nm / (1.0 - ADAM_B1 ** ADAM_STEP)
        v_hat = nv / (1.0 - ADAM_B2 ** ADAM_STEP)
        d_ref[...] = -ADAM_LR * (m_hat / (jnp.sqrt(v_hat) + ADAM_EPS) + ADAM_WD * w_ref[...])
        nm_ref[...] = nm
        nv_ref[...] = nv

    spec = pl.BlockSpec((1, rows_per_step, cols), lambda l, i: (first + l, i, 0))
    shape = jax.ShapeDtypeStruct(w.shape, F32)
    dests = () if dests is None else tuple(dests)
    return pl.pallas_call(
        body, name=name, grid=(count, rows // rows_per_step),
        in_specs=[spec] * 4 + [ANY] * len(dests), out_specs=[spec] * 3, out_shape=[shape] * 3,
        input_output_aliases={4 + k: k for k in range(len(dests))},
        compiler_params=_params(("arbitrary", "arbitrary")),
    )(w, g, m, v, *dests)


def _pack_small(pool_w, pool_scale, sinks, norm_pre, norm_post):
    sink_rows = jnp.zeros((ROWS_SINK, 128), F32).at[0, 0:2 * N_HEADS].set(sinks.reshape(-1))
    return jnp.concatenate([
        pool_w.reshape(ROWS_PW, 128), pool_scale.reshape(ROWS_SC, 128), norm_pre.reshape(ROWS_NORM, 128),
        norm_post.reshape(ROWS_NORM, 128), sink_rows, jnp.zeros((SMALL_ROWS - ROW_LOSS, 128), F32)], axis=0)


def _pack_small_grads(grads, loss):
    sink_rows = jnp.zeros((ROWS_SINK, 128), F32).at[0, 0:2 * N_HEADS].set(
        jnp.concatenate([g[4][:, 0] for g in grads]))
    return jnp.concatenate(
        [g[2].reshape(ROWS_PW // DEPTH, 128) for g in grads] + [g[3] for g in grads] + [g[5] for g in grads]
        + [g[6] for g in grads] + [sink_rows, loss, jnp.zeros((SMALL_ROWS - ROW_LOSS - 8, 128), F32)], axis=0)


def _unpack_small(packed):
    o = 0
    pool_w = packed[o:o + ROWS_PW].reshape(DEPTH, 4, 128, 128)
    o += ROWS_PW
    pool_scale = packed[o:o + ROWS_SC].reshape(DEPTH, 512)
    o += ROWS_SC
    norm_pre = packed[o:o + ROWS_NORM].reshape(DEPTH, D)
    o += ROWS_NORM
    norm_post = packed[o:o + ROWS_NORM].reshape(DEPTH, D)
    o += ROWS_NORM
    sinks = packed[o, 0:2 * N_HEADS].reshape(DEPTH, N_HEADS)
    return pool_w, pool_scale, sinks, norm_pre, norm_post


def kernel(x, w_in, pool_w, pool_scale, attn_sinks, w_out, norm_pre, norm_post, loss_target, m_w_in, m_pool_w, m_pool_scale, m_attn_sinks, m_w_out, m_norm_pre, m_norm_post, v_w_in, v_pool_w, v_pool_scale, v_attn_sinks, v_w_out, v_norm_pre, v_norm_post):
    cx, cy, cc = _place()
    chip_arr = jnp.reshape(2 * cx + cy, (1,)).astype(jnp.int32)
    place_arr = jnp.stack([4 * cx + 2 * cy + cc, cc]).astype(jnp.int32)
    t = lambda a: jnp.transpose(a, (0, 2, 1))
    w_in_t = t(w_in)
    xs, target = x[0], loss_target[0]
    pool_w_b = pool_w.astype(BF16)
    bias, tri = _attention_tables()
    scale3 = pool_scale.reshape(DEPTH, 1, D_POOL)
    pre3 = norm_pre.reshape(DEPTH, 1, D)
    post3 = norm_post.reshape(DEPTH, 1, D)

    wi = _place_cast("place_w_in", w_in_t, chip_arr, 288)
    wo = _place_cast("place_w_out", w_out, chip_arr, 256)
    send, recv, bufs, token = _gather_start([wi[0], wo[0], wi[1], wo[1]], halved=(0,))

    saved = []
    after = token
    for l in range(DEPTH):
        w_in_l = _gather_wait(f"gather_wait_in{l}", bufs[2 * l], send[2 * l], recv[2 * l], after, halved=(l == 0))
        if l == 0:
            w_in_l = _forward_halves("forward_w_in0", w_in_l)
        pu, pg, q, kv, ag = _fwd_in(l, xs, pre3, w_in_l)
        cat = _fwd_mix(l, pu, pg, q, kv, ag, pool_w_b, scale3, attn_sinks, bias, tri)
        w_out_l = _gather_wait(f"gather_wait_out{l}", bufs[2 * l + 1], send[2 * l + 1], recv[2 * l + 1], cat)
        if l < DEPTH - 1:
            y, x_next = _fwd_out(l, cat, w_out_l, xs, post3)
        else:
            y, x_next, loss = _fwd_out(l, cat, w_out_l, xs, post3, target)
        saved.append((xs, pu, pg, q, kv, ag, cat, y, w_in_l, w_out_l))
        xs = after = x_next

    x_in, pu, pg, q, kv, ag, cat, y, w_in_l, w_out_l = saved[1]
    dcat, dw_out1, dw_out1_b, dg_post1 = _bwd_out(1, xs, y, post3, cat, w_out_l)
    dproj, dpw1, dsc1, dsink1 = _bwd_mix(1, pu, pg, q, kv, ag, dcat, pool_w_b, scale3, attn_sinks, bias, tri)
    dx, dw_in1, dw_in1_b, dg_pre1 = _bwd_in(1, dproj, w_in_l, x_in, pre3, xs)
    ex1 = _exchange_start("exchange_start_1", [dw_in1_b, dw_out1_b])

    x_in, pu, pg, q, kv, ag, cat, y, w_in_l, w_out_l = saved[0]
    dcat, dw_out0, dw_out0_b, dg_post0 = _bwd_out(0, dx, y, post3, cat, w_out_l, deps=(ex1[4],))
    ex0_out = _exchange_start("exchange_start_out0", [dw_out0_b])
    dproj, dpw0, dsc0, dsink0 = _bwd_mix(0, pu, pg, q, kv, ag, dcat, pool_w_b, scale3, attn_sinks, bias, tri,
                                         deps=(ex0_out[4],))
    recv_in1, recv_out1 = _exchange_wait("exchange_wait_1", ex1, dproj)
    g_in = _sum_pieces("sum_pieces_in1", dw_in1, recv_in1, place_arr, 96, 1)
    g_out = _sum_pieces("sum_pieces_out1", dw_out1, recv_out1, place_arr, 64, 1)
    dw_in0, dw_in0_b = _bwd_in_dw(0, dproj, x_in, pre3, deps=(g_in, g_out))
    ex0_in = _exchange_start("exchange_start_in0", [dw_in0_b])

    grad_x, dg_pre0 = _bwd_in_dx(0, dproj, w_in_l, x_in, pre3, dx, deps=(ex0_in[4],))
    (recv_out0,) = _exchange_wait("exchange_wait_out0", ex0_out, grad_x)
    g_out = _sum_pieces("sum_pieces_out0", dw_out0, recv_out0, place_arr, 64, 0, dest=g_out)
    g_in, grad_w_out = _swap_halves("swap_halves_a", [g_in, g_out], [(0, 1), (1, 0), (1, 1)])
    m_in_t, v_in_t = t(m_w_in), t(v_w_in)
    d_out, nm_out, nv_out = _adamw("adamw_w_out", w_out, grad_w_out, m_w_out, v_w_out, 256)
    upd_in = _adamw("adamw_w_in1", w_in_t, g_in, m_in_t, v_in_t, 288, first=1, count=1)

    grads = [(None, None, dpw0, dsc0, dsink0, dg_pre0, dg_post0), (None, None, dpw1, dsc1, dsink1, dg_pre1, dg_post1)]
    packed = _pack_small_grads(grads, loss).reshape(8, PIECE_ROWS, 128)
    g_small = _allreduce_small(packed, deps=(upd_in[0], d_out)).reshape(1, SMALL_ROWS, 128)
    small = lambda *a: _pack_small(*a).reshape(1, SMALL_ROWS, 128)
    d_s, nm_s, nv_s = _adamw(
        "adamw_small", small(pool_w, pool_scale, attn_sinks, norm_pre, norm_post), g_small,
        small(m_pool_w, m_pool_scale, m_attn_sinks, m_norm_pre, m_norm_post),
        small(v_pool_w, v_pool_scale, v_attn_sinks, v_norm_pre, v_norm_post), SMALL_ROWS)

    (recv_in0,) = _exchange_wait("exchange_wait_in0", ex0_in, d_s)
    g_in = _sum_pieces("sum_pieces_in0", dw_in0, recv_in0, place_arr, 96, 0, dest=g_in)
    (grad_w_in_t,) = _swap_halves("swap_halves_b", [g_in], [(0, 0)])
    d_in, nm_in, nv_in = _adamw("adamw_w_in0", w_in_t, grad_w_in_t, m_in_t, v_in_t, 288, first=0, count=1, dests=upd_in)

    g_pw, g_sc, g_sk, g_pre, g_post = _unpack_small(g_small[0])
    d_pw, d_sc, d_sk, d_pre, d_post = _unpack_small(d_s[0])
    m_pw, m_sc, m_sk, m_pre, m_post = _unpack_small(nm_s[0])
    v_pw, v_sc, v_sk, v_pre, v_post = _unpack_small(nv_s[0])
    return (g_small[0, ROW_LOSS, 0], grad_x[None], t(grad_w_in_t), g_pw, g_sc, g_sk, grad_w_out, g_pre, g_post,
            t(d_in), d_pw, d_sc, d_sk, d_out, d_pre, d_post,
            t(nm_in), m_pw, m_sc, m_sk, nm_out, m_pre, m_post,
            t(nv_in), v_pw, v_sc, v_sk, nv_out, v_pre, v_post)
```

```python
import jax
import jax.numpy as jnp
from jax import lax
from jax.experimental import pallas as pl
from jax.experimental.pallas import tpu as pltpu

F32 = jnp.float32
BF16 = jnp.bfloat16

S = 2048
D = 1024
DEPTH = 2
D_POOL = 512
POOL_WINDOWS = (2, 4, 8, 16)
N_HEADS = 8
D_IN = 2304
N_SHARDS = 4
W_IN_SHARD = D_IN // N_SHARDS
W_OUT_SHARD = D // N_SHARDS
BLK = 128
NB = S // BLK
HALO = 16
EPS = 1e-6
NEG_INF = -1e30
C_PU, C_PG, C_Q, C_K, C_V, C_AG = 0, 512, 1024, 1536, 1664, 1792

ADAM_LR = 0.001
ADAM_B1 = 0.9
ADAM_B2 = 0.999
ADAM_EPS = 1e-08
ADAM_WD = 0.01
ADAM_STEP = 10

TM = 512
VMEM_LIMIT = 56 * 1024 * 1024

NT = (((1,), (1,)), ((), ()))
TN = (((0,), (0,)), ((), ()))

MESH = pl.DeviceIdType.MESH
ANY = pl.BlockSpec(memory_space=pl.ANY)

ROWS_PW, ROWS_SC, ROWS_NORM, ROWS_SINK = 1024, 8, 16, 8
ROW_LOSS = ROWS_PW + ROWS_SC + 2 * ROWS_NORM + ROWS_SINK
SMALL_ROWS = 1088
PIECE_ROWS = SMALL_ROWS // 8


def _params(sem=("arbitrary",)):
    return pltpu.CompilerParams(dimension_semantics=sem, vmem_limit_bytes=VMEM_LIMIT)


def _sigmoid(v):
    return 1.0 / (1.0 + jnp.exp(-v))


def _rows8(v):
    r, c = v.shape
    return v.reshape(r // 8, 8, c).sum(axis=0)


def _layer(l, *shape):
    zeros = (0,) * len(shape)
    return pl.BlockSpec((None,) + shape, lambda i: (l,) + zeros)


def _whole(shape):
    zeros = (0,) * len(shape)
    return pl.BlockSpec(shape, lambda i: zeros, pipeline_mode=pl.Buffered(1))


def _fwd_in(l, x, g_pre, w_in_t):
    def body(x_ref, g_ref, w_ref, pu_ref, pg_ref, q_ref, kv_ref, ag_ref):
        xt = x_ref[...]
        r = lax.rsqrt(jnp.mean(xt * xt, axis=-1, keepdims=True) + EPS)
        h = (xt * r * g_ref[...]).astype(BF16)

        def proj(lo, hi):
            return lax.dot_general(h, w_ref[lo:hi, :], NT, preferred_element_type=F32)

        pu_ref[...] = proj(C_PU, C_PG)
        pg_ref[...] = proj(C_PG, C_Q)
        q_ref[...] = proj(C_Q, C_K).astype(BF16)
        kv_ref[...] = proj(C_K, C_AG).astype(BF16)
        ag_ref[...] = proj(C_AG, D_IN)

    row = lambda w: pl.BlockSpec((TM, w), lambda i: (i, 0))
    return pl.pallas_call(
        body, name="fwd_in", grid=(S // TM,),
        in_specs=[row(D), _layer(l, 1, D), _whole((D_IN, D))],
        out_specs=[row(512), row(512), row(512), row(256), row(512)],
        out_shape=[jax.ShapeDtypeStruct((S, 512), F32), jax.ShapeDtypeStruct((S, 512), F32),
                   jax.ShapeDtypeStruct((S, 512), BF16), jax.ShapeDtypeStruct((S, 256), BF16),
                   jax.ShapeDtypeStruct((S, 512), F32)],
        compiler_params=_params(),
    )(x, g_pre, w_in_t)


LOG2E = 1.4426950408889634
SCORE_SCALE = 0.125 * LOG2E


def _attention_tables():
    qi = jnp.arange(BLK)[:, None]
    kj = jnp.arange(BLK)[None, :]
    dist = ((qi - kj) % BLK).astype(F32)
    slopes = jnp.exp2(-jnp.arange(1, N_HEADS + 1, dtype=F32))
    bias = -(slopes * LOG2E)[:, None, None] * dist[None]
    first = jnp.where(kj > qi, NEG_INF, bias)
    return jnp.stack([first, bias]), (kj <= qi).astype(BF16)


def _own_block_mask():
    return lax.broadcasted_iota(jnp.int32, (BLK, BLK), 1) <= lax.broadcasted_iota(jnp.int32, (BLK, BLK), 0)


def _merge(full, own):
    return jnp.where(own, full[:, BLK:], full[:, :BLK])


def _spread(v, tri):
    own = v * tri
    return jnp.concatenate([v - own, own], axis=1)


def _head_variants(cur, prev):
    both = jnp.concatenate([prev, cur], axis=0).astype(F32)
    swapped = pltpu.roll(both, 64, axis=1)
    low = lax.broadcasted_iota(jnp.int32, both.shape, 1) < 64
    zero = jnp.zeros_like(both)
    return ((jnp.where(low, both, zero).astype(BF16), jnp.where(low, zero, swapped).astype(BF16)),
            (jnp.where(low, swapped, zero).astype(BF16), jnp.where(low, zero, both).astype(BF16)))


def _head_of(hkv, t, half):
    return hkv * 4 + 2 * t + half


def _rows(v, t):
    return v[t * BLK:(t + 1) * BLK]


def _stack_tiles(ref, hkv, offset=0):
    lo = offset + 2 * hkv * 128
    return jnp.concatenate([ref[:, lo:lo + 128], ref[:, lo + 128:lo + 256]], axis=0)


def _scores(q2, k_var, own):
    s = {}
    for hkv in range(2):
        for half in range(2):
            full = lax.dot_general(q2[hkv], k_var[hkv][half], NT, preferred_element_type=F32)
            for t in range(2):
                s[hkv, t, half] = _merge(_rows(full, t), own)
    return s


def _softmax(s, bias, sink):
    s = s * SCORE_SCALE + bias
    sink2 = sink * LOG2E
    m = jnp.maximum(jnp.max(s, axis=-1, keepdims=True), sink2)
    p = jnp.exp2(s - m)
    e_sink = jnp.exp2(sink2 - m)
    inv = 1.0 / (jnp.sum(p, axis=-1, keepdims=True) + e_sink)
    return p * inv, e_sink * inv


def _spread_pair(v, hkv, half, tri):
    return jnp.concatenate([_spread(v[hkv, t, half].astype(BF16), tri) for t in range(2)], axis=0)


def _pool_block(ext_ref, i, g, w):
    lanes = slice(g * 128, (g + 1) * 128)
    u = ext_ref[HALO:HALO + BLK, lanes]
    acc = u
    for j in range(1, w):
        acc = acc + ext_ref[HALO - j:HALO - j + BLK, lanes]
    t = (i * BLK + lax.broadcasted_iota(jnp.int32, (BLK, 1), 0)).astype(F32)
    inv = 1.0 / jnp.minimum(t + 1.0, float(w))
    return acc * inv - u, inv


def _fwd_mix(l, pu, pg, q, kv, ag, pool_w, pool_scale, sinks, bias, tri):
    def body(pu_ref, pup_ref, pg_ref, q_ref, kv_ref, kvp_ref, ag_ref, pw_ref, sc_ref, sink_ref, bias_ref, tri_ref,
             cat_ref, ext_ref):
        i = pl.program_id(0)
        ext_ref[0:HALO, :] = jnp.where(i > 0, pup_ref[...], 0.0)
        ext_ref[HALO:HALO + BLK, :] = pu_ref[...]
        for g, w in enumerate(POOL_WINDOWS):
            lanes = slice(g * 128, (g + 1) * 128)
            pooled, _ = _pool_block(ext_ref, i, g, w)
            mixed = jnp.dot(pooled.astype(BF16), pw_ref[g], preferred_element_type=F32)
            gate = pg_ref[:, lanes]
            cat_ref[:, lanes] = (mixed * sc_ref[:, lanes] * (gate * _sigmoid(gate))).astype(BF16)

        own = _own_block_mask()
        tri = tri_ref[...]
        k_var = _head_variants(kv_ref[:, 0:128], kvp_ref[:, 0:128])
        v_var = _head_variants(kv_ref[:, 128:256], kvp_ref[:, 128:256])
        s = _scores([_stack_tiles(q_ref, hkv) for hkv in range(2)], k_var, own)
        p = {}
        for (hkv, t, half), s_head in s.items():
            head = _head_of(hkv, t, half)
            p[hkv, t, half], _ = _softmax(s_head, bias_ref[head], sink_ref[l, head])
        for hkv in range(2):
            o2 = jnp.zeros((2 * BLK, 128), F32)
            for half in range(2):
                o2 = o2 + jnp.dot(_spread_pair(p, hkv, half, tri), v_var[hkv][half], preferred_element_type=F32)
            for t in range(2):
                lo = (2 * hkv + t) * 128
                gate = ag_ref[:, lo:lo + 128]
                cat_ref[:, D_POOL + lo:D_POOL + lo + 128] = (_rows(o2, t) * (gate * _sigmoid(gate))).astype(BF16)

    blk = lambda w: pl.BlockSpec((BLK, w), lambda i: (i, 0))
    prev = lambda w: pl.BlockSpec((BLK, w), lambda i: (jnp.maximum(i - 1, 0), 0))
    halo = pl.BlockSpec((HALO, 512), lambda i: (jnp.maximum(i * (BLK // HALO) - 1, 0), 0))
    return pl.pallas_call(
        body, name="fwd_mix", grid=(NB,),
        in_specs=[blk(512), halo, blk(512), blk(512), blk(256), prev(256), blk(512),
                  _layer(l, 4, 128, 128), _layer(l, 1, 512), pl.BlockSpec(memory_space=pltpu.SMEM),
                  pl.BlockSpec((None, N_HEADS, BLK, BLK), lambda i: (jnp.minimum(i, 1), 0, 0, 0)), _whole((BLK, BLK))],
        out_specs=blk(D),
        out_shape=jax.ShapeDtypeStruct((S, D), BF16),
        scratch_shapes=[pltpu.VMEM((HALO + BLK, 512), F32)],
        compiler_params=_params(),
    )(pu, pu, pg, q, kv, kv, ag, pool_w, pool_scale, sinks, bias, tri)


def _fwd_out(l, cat, w_out, x, g_post, target=None):
    last = target is not None
    n_steps = S // TM

    def body(*refs):
        if last:
            cat_ref, w_ref, x_ref, g_ref, t_ref, y_ref, dx_ref, loss_ref, acc_ref = refs
        else:
            cat_ref, w_ref, x_ref, g_ref, y_ref, xn_ref = refs
        y = jnp.dot(cat_ref[...], w_ref[...], preferred_element_type=F32)
        y_ref[...] = y
        r = lax.rsqrt(jnp.mean(y * y, axis=-1, keepdims=True) + EPS)
        xn = x_ref[...] + y * r * g_ref[...]
        if not last:
            xn_ref[...] = xn
            return
        step = pl.program_id(0)
        err = xn - t_ref[...]
        dx_ref[...] = err * (1.0 / D)

        @pl.when(step == 0)
        def _():
            acc_ref[...] = jnp.zeros_like(acc_ref)

        acc_ref[...] += _rows8(err * err)

        @pl.when(step == n_steps - 1)
        def _():
            loss_ref[...] = jnp.full((8, 128), (0.5 / D) * jnp.sum(acc_ref[...]), F32)

    row = lambda: pl.BlockSpec((TM, D), lambda i: (i, 0))
    act = jax.ShapeDtypeStruct((S, D), F32)
    in_specs = [row(), _whole((D, D)), row(), _layer(l, 1, D)]
    args = [cat, w_out, x, g_post]
    if last:
        return pl.pallas_call(
            body, name="fwd_out_loss", grid=(n_steps,),
            in_specs=in_specs + [row()],
            out_specs=[row(), row(), pl.BlockSpec((8, 128), lambda i: (0, 0))],
            out_shape=[act, act, jax.ShapeDtypeStruct((8, 128), F32)],
            scratch_shapes=[pltpu.VMEM((8, D), F32)],
            compiler_params=_params(),
        )(*args, target)
    return pl.pallas_call(
        body, name="fwd_out", grid=(n_steps,),
        in_specs=in_specs, out_specs=[row(), row()], out_shape=[act, act],
        compiler_params=_params(),
    )(*args)


def _store_lane_rows(ref, acc):
    total = jnp.sum(acc, axis=0, keepdims=True)
    for k in range(ref.shape[0]):
        ref[k:k + 1, :] = total[:, k * 128:(k + 1) * 128]


def _bwd_out(l, dxn, y, g_post, cat, w_out, deps=()):
    n_steps = S // TM

    def body(dz_ref, y_ref, g_ref, cat_ref, w_ref, *rest):
        dcat_ref, dw_ref, dwb_ref, dg_ref, acc_ref = rest[len(deps):]
        step = pl.program_id(0)

        @pl.when(step == 0)
        def _():
            dw_ref[...] = jnp.zeros_like(dw_ref)
            acc_ref[...] = jnp.zeros_like(acc_ref)

        y = y_ref[...]
        dz = dz_ref[...]
        r = lax.rsqrt(jnp.mean(y * y, axis=-1, keepdims=True) + EPS)
        a = dz * g_ref[...]
        dy = r * a - y * (r * r * r) * jnp.mean(a * y, axis=-1, keepdims=True)
        acc_ref[...] += _rows8(dz * (y * r))
        dyb = dy.astype(BF16)
        dcat_ref[...] = lax.dot_general(dyb, w_ref[...], NT, preferred_element_type=F32)
        dw_ref[...] += lax.dot_general(cat_ref[...], dyb, TN, preferred_element_type=F32)

        @pl.when(step == n_steps - 1)
        def _():
            _store_lane_rows(dg_ref, acc_ref[...])
            dwb_ref[...] = dw_ref[...].astype(BF16)

    row = lambda: pl.BlockSpec((TM, D), lambda i: (i, 0))
    full = _whole
    return pl.pallas_call(
        body, name="bwd_out", grid=(n_steps,),
        in_specs=[row(), row(), _layer(l, 1, D), row(), full((D, D))] + [ANY] * len(deps),
        out_specs=[row(), full((D, D)), full((D, D)), full((8, 128))],
        out_shape=[jax.ShapeDtypeStruct((S, D), F32), jax.ShapeDtypeStruct((D, D), F32),
                   jax.ShapeDtypeStruct((D, D), BF16), jax.ShapeDtypeStruct((8, 128), F32)],
        scratch_shapes=[pltpu.VMEM((8, D), F32)],
        compiler_params=_params(),
    )(dxn, y, g_post, cat, w_out, *deps)


def _bwd_mix(l, pu, pg, q, kv, ag, dcat, pool_w, pool_scale, sinks, bias, tri, deps=()):
    def body(pu_ref, pup_ref, pg_ref, q_ref, kv_ref, kvp_ref, ag_ref, dcat_ref, pw_ref, sc_ref, sink_ref, bias_ref,
             tri_ref, *rest):
        dproj_ref, dpw_ref, dsc_ref, dsink_ref, ext_ref, dext_ref, dkv_ref = rest[len(deps):]
        step = pl.program_id(0)
        i = NB - 1 - step

        @pl.when(step == 0)
        def _():
            dpw_ref[...] = jnp.zeros_like(dpw_ref)
            dsc_ref[...] = jnp.zeros_like(dsc_ref)
            dsink_ref[...] = jnp.zeros_like(dsink_ref)
            dext_ref[BLK:BLK + HALO, :] = jnp.zeros((HALO, 512), F32)
            dkv_ref[...] = jnp.zeros_like(dkv_ref)

        ext_ref[0:HALO, :] = jnp.where(i > 0, pup_ref[...], 0.0)
        ext_ref[HALO:HALO + BLK, :] = pu_ref[...]
        for g, w in enumerate(POOL_WINDOWS):
            lanes = slice(g * 128, (g + 1) * 128)
            pooled, inv = _pool_block(ext_ref, i, g, w)
            pooled_b = pooled.astype(BF16)
            mixed = jnp.dot(pooled_b, pw_ref[g], preferred_element_type=F32)
            scale = sc_ref[:, lanes]
            gate = pg_ref[:, lanes]
            sg = _sigmoid(gate)
            dpo = dcat_ref[:, lanes]
            dproj_ref[:, C_PG + g * 128:C_PG + (g + 1) * 128] = (
                dpo * (mixed * scale) * (sg * (1.0 + gate * (1.0 - sg)))).astype(BF16)
            dms = dpo * (gate * sg)
            dsc_ref[g:g + 1, :] += jnp.sum(dms * mixed, axis=0, keepdims=True)
            dmixed = (dms * scale).astype(BF16)
            dpw_ref[g] += lax.dot_general(pooled_b, dmixed, TN, preferred_element_type=F32)
            dpooled = lax.dot_general(dmixed, pw_ref[g], NT, preferred_element_type=F32)
            dext_ref[0:BLK, lanes] = dpooled * inv
            acc = dext_ref[0:BLK, lanes]
            for j in range(1, w):
                acc = acc + dext_ref[j:j + BLK, lanes]
            dproj_ref[:, C_PU + g * 128:C_PU + (g + 1) * 128] = (acc - dpooled).astype(BF16)
        dext_ref[BLK:BLK + HALO, :] = dext_ref[0:HALO, :]

        own = _own_block_mask()
        tri = tri_ref[...]
        k_var = _head_variants(kv_ref[:, 0:128], kvp_ref[:, 0:128])
        v_var = _head_variants(kv_ref[:, 128:256], kvp_ref[:, 128:256])
        q2 = [_stack_tiles(q_ref, hkv) for hkv in range(2)]
        s = _scores(q2, k_var, own)
        p, p_sink = {}, {}
        for key, s_head in s.items():
            head = _head_of(*key)
            p[key], p_sink[key] = _softmax(s_head, bias_ref[head], sink_ref[l, head])

        do2, p_b, dp = [], {}, {}
        for hkv in range(2):
            gate = _stack_tiles(ag_ref, hkv)
            sg = _sigmoid(gate)
            dca = _stack_tiles(dcat_ref, hkv, D_POOL)
            do2.append((dca * (gate * sg)).astype(BF16))
            o2 = jnp.zeros((2 * BLK, 128), F32)
            for half in range(2):
                p_b[hkv, half] = _spread_pair(p, hkv, half, tri)
                o2 = o2 + jnp.dot(p_b[hkv, half], v_var[hkv][half], preferred_element_type=F32)
                full = lax.dot_general(do2[hkv], v_var[hkv][half], NT, preferred_element_type=F32)
                for t in range(2):
                    dp[hkv, t, half] = _merge(_rows(full, t), own)
            dag = dca * o2 * (sg * (1.0 + gate * (1.0 - sg)))
            for t in range(2):
                lo = C_AG + (2 * hkv + t) * 128
                dproj_ref[:, lo:lo + 128] = _rows(dag, t).astype(BF16)

        ds = {}
        for key in p:
            delta = jnp.sum(p[key] * dp[key], axis=-1, keepdims=True)
            ds[key] = p[key] * (dp[key] - delta)
            head = _head_of(*key)
            dsink_ref[head:head + 1, :] += jnp.broadcast_to(
                -jnp.sum(p_sink[key] * delta, axis=0, keepdims=True), (1, 128))

        dk_acc = [[None, None], [None, None]]
        dv_acc = [[None, None], [None, None]]
        for hkv in range(2):
            dq2 = jnp.zeros((2 * BLK, 128), F32)
            for half in range(2):
                ds_b = _spread_pair(ds, hkv, half, tri)
                dq2 = dq2 + jnp.dot(ds_b, k_var[hkv][half], preferred_element_type=F32)
                dk_acc[hkv][half] = lax.dot_general(ds_b, q2[hkv], TN, preferred_element_type=F32)
                dv_acc[hkv][half] = lax.dot_general(p_b[hkv, half], do2[hkv], TN, preferred_element_type=F32)
            for t in range(2):
                lo = C_Q + (2 * hkv + t) * 128
                dproj_ref[:, lo:lo + 128] = (_rows(dq2, t) * 0.125).astype(BF16)

        low = lax.broadcasted_iota(jnp.int32, (2 * BLK, 128), 1) < 64

        def gather_heads(acc):
            return jnp.where(low, acc[0][0] + pltpu.roll(acc[0][1], 64, axis=1),
                             pltpu.roll(acc[1][0], 64, axis=1) + acc[1][1])

        dk = gather_heads(dk_acc) * 0.125
        dv = gather_heads(dv_acc)
        dproj_ref[:, C_K:C_V] = (dk[BLK:, :] + dkv_ref[:, 0:128]).astype(BF16)
        dproj_ref[:, C_V:C_AG] = (dv[BLK:, :] + dkv_ref[:, 128:256]).astype(BF16)
        dkv_ref[:, 0:128] = dk[:BLK, :]
        dkv_ref[:, 128:256] = dv[:BLK, :]

    rev = lambda w: pl.BlockSpec((BLK, w), lambda s: (NB - 1 - s, 0))
    prev = lambda w: pl.BlockSpec((BLK, w), lambda s: (jnp.maximum(NB - 2 - s, 0), 0))
    halo = pl.BlockSpec((HALO, 512), lambda s: (jnp.maximum((NB - 1 - s) * (BLK // HALO) - 1, 0), 0))
    return pl.pallas_call(
        body, name="bwd_mix", grid=(NB,),
        in_specs=[rev(512), halo, rev(512), rev(512), rev(256), prev(256), rev(512), rev(D),
                  _layer(l, 4, 128, 128), _layer(l, 1, 512), pl.BlockSpec(memory_space=pltpu.SMEM),
                  pl.BlockSpec((None, N_HEADS, BLK, BLK), lambda s: (jnp.minimum(NB - 1 - s, 1), 0, 0, 0)),
                  _whole((BLK, BLK))] + [ANY] * len(deps),
        out_specs=[rev(D_IN), pl.BlockSpec((4, 128, 128), lambda s: (0, 0, 0)),
                   pl.BlockSpec((4, 128), lambda s: (0, 0)), pl.BlockSpec((8, 128), lambda s: (0, 0))],
        out_shape=[jax.ShapeDtypeStruct((S, D_IN), BF16), jax.ShapeDtypeStruct((4, 128, 128), F32),
                   jax.ShapeDtypeStruct((4, 128), F32), jax.ShapeDtypeStruct((8, 128), F32)],
        scratch_shapes=[pltpu.VMEM((HALO + BLK, 512), F32), pltpu.VMEM((BLK + HALO, 512), F32),
                        pltpu.VMEM((BLK, 256), F32)],
        compiler_params=_params(),
    )(pu, pu, pg, q, kv, kv, ag, dcat, pool_w, pool_scale, sinks, bias, tri, *deps)


def _bwd_in(l, dproj, w_in_t, x, g_pre, dres, deps=()):
    n_steps = S // TM

    def body(dp_ref, w_ref, x_ref, g_ref, dres_ref, *rest):
        dx_ref, dw_ref, dwb_ref, dg_ref, acc_ref = rest[len(deps):]
        step = pl.program_id(0)

        @pl.when(step == 0)
        def _():
            dw_ref[...] = jnp.zeros_like(dw_ref)
            acc_ref[...] = jnp.zeros_like(acc_ref)

        dp = dp_ref[...]
        dh = jnp.dot(dp, w_ref[...], preferred_element_type=F32)
        xt = x_ref[...]
        r = lax.rsqrt(jnp.mean(xt * xt, axis=-1, keepdims=True) + EPS)
        xn = xt * r
        g = g_ref[...]
        acc_ref[...] += _rows8(dh * xn)
        a = dh * g
        dx_ref[...] = dres_ref[...] + (r * a - xt * (r * r * r) * jnp.mean(a * xt, axis=-1, keepdims=True))
        dw_ref[...] += lax.dot_general(dp, (xn * g).astype(BF16), TN, preferred_element_type=F32)

        @pl.when(step == n_steps - 1)
        def _():
            _store_lane_rows(dg_ref, acc_ref[...])
            dwb_ref[...] = dw_ref[...].astype(BF16)

    row = lambda w: pl.BlockSpec((TM, w), lambda i: (i, 0))
    full = _whole
    return pl.pallas_call(
        body, name="bwd_in", grid=(n_steps,),
        in_specs=[row(D_IN), full((D_IN, D)), row(D), _layer(l, 1, D), row(D)] + [ANY] * len(deps),
        out_specs=[row(D), full((D_IN, D)), full((D_IN, D)), full((8, 128))],
        out_shape=[jax.ShapeDtypeStruct((S, D), F32), jax.ShapeDtypeStruct((D_IN, D), F32),
                   jax.ShapeDtypeStruct((D_IN, D), BF16), jax.ShapeDtypeStruct((8, 128), F32)],
        scratch_shapes=[pltpu.VMEM((8, D), F32)],
        compiler_params=_params(),
    )(dproj, w_in_t, x, g_pre, dres, *deps)


def _bwd_in_dw(l, dproj, x, g_pre, deps=()):
    n_steps = S // TM

    def body(dp_ref, x_ref, g_ref, *rest):
        dw_ref, dwb_ref = rest[len(deps):]
        step = pl.program_id(0)

        @pl.when(step == 0)
        def _():
            dw_ref[...] = jnp.zeros_like(dw_ref)

        xt = x_ref[...]
        r = lax.rsqrt(jnp.mean(xt * xt, axis=-1, keepdims=True) + EPS)
        h = (xt * r * g_ref[...]).astype(BF16)
        dw_ref[...] += lax.dot_general(dp_ref[...], h, TN, preferred_element_type=F32)

        @pl.when(step == n_steps - 1)
        def _():
            dwb_ref[...] = dw_ref[...].astype(BF16)

    row = lambda w: pl.BlockSpec((TM, w), lambda i: (i, 0))
    full = _whole
    return pl.pallas_call(
        body, name="bwd_in_dw", grid=(n_steps,),
        in_specs=[row(D_IN), row(D), _layer(l, 1, D)] + [ANY] * len(deps),
        out_specs=[full((D_IN, D)), full((D_IN, D))],
        out_shape=[jax.ShapeDtypeStruct((D_IN, D), F32), jax.ShapeDtypeStruct((D_IN, D), BF16)],
        compiler_params=_params(),
    )(dproj, x, g_pre, *deps)


def _bwd_in_dx(l, dproj, w_in_t, x, g_pre, dres, deps=()):
    n_steps = S // TM

    def body(dp_ref, w_ref, x_ref, g_ref, dres_ref, *rest):
        dx_ref, dg_ref, acc_ref = rest[len(deps):]
        step = pl.program_id(0)

        @pl.when(step == 0)
        def _():
            acc_ref[...] = jnp.zeros_like(acc_ref)

        dh = jnp.dot(dp_ref[...], w_ref[...], preferred_element_type=F32)
        xt = x_ref[...]
        r = lax.rsqrt(jnp.mean(xt * xt, axis=-1, keepdims=True) + EPS)
        xn = xt * r
        acc_ref[...] += _rows8(dh * xn)
        a = dh * g_ref[...]
        dx_ref[...] = dres_ref[...] + (r * a - xt * (r * r * r) * jnp.mean(a * xt, axis=-1, keepdims=True))

        @pl.when(step == n_steps - 1)
        def _():
            _store_lane_rows(dg_ref, acc_ref[...])

    row = lambda w: pl.BlockSpec((TM, w), lambda i: (i, 0))
    full = _whole
    return pl.pallas_call(
        body, name="bwd_in_dx", grid=(n_steps,),
        in_specs=[row(D_IN), full((D_IN, D)), row(D), _layer(l, 1, D), row(D)] + [ANY] * len(deps),
        out_specs=[row(D), full((8, 128))],
        out_shape=[jax.ShapeDtypeStruct((S, D), F32), jax.ShapeDtypeStruct((8, 128), F32)],
        scratch_shapes=[pltpu.VMEM((8, D), F32)],
        compiler_params=_params(),
    )(dproj, w_in_t, x, g_pre, dres, *deps)


HBM =pl.BlockSpec(memory_space=pltpu.HBM)
SEM = pl.BlockSpec(memory_space=pltpu.SEMAPHORE)
SPLIT_COPY = pltpu.CompilerParams(has_side_effects=pltpu.SideEffectType.DATAFLOW_SIDE_EFFECTING)


def _in_hbm(a):
    return pltpu.with_memory_space_constraint(a, pltpu.HBM)

def _place():
    return lax.axis_index("x"), lax.axis_index("y"), lax.axis_index("c")


def _other_chips(x, y):
    return [(1 - x, y), (x, 1 - y), (1 - x, 1 - y)]


def _peer(x, y, c, m):
    return (x ^ (m >> 2), y ^ ((m >> 1) & 1), c ^ (m & 1))


def _place_cast(name, src, chip_arr, tile):
    _, n, cols = src.shape
    steps = n // tile

    def body(chip_ref, s0_ref, s1_ref, o0_ref, o1_ref):
        o0_ref[...] = s0_ref[...].astype(BF16)
        o1_ref[...] = s1_ref[...].astype(BF16)

    return pl.pallas_call(
        body, name=name,
        grid_spec=pltpu.PrefetchScalarGridSpec(
            num_scalar_prefetch=1, grid=(steps,),
            in_specs=[pl.BlockSpec((None, tile, cols), lambda i, chip: (0, i, 0)),
                      pl.BlockSpec((None, tile, cols), lambda i, chip: (1, i, 0))],
            out_specs=[pl.BlockSpec((tile, cols), lambda i, chip: (chip[0] * steps + i, 0))] * 2),
        out_shape=[jax.ShapeDtypeStruct((N_SHARDS * n, cols), BF16)] * 2,
        compiler_params=_params(),
    )(chip_arr, src, src)


def _chip_rows(ref, chip, half=None):
    n = ref.shape[0] // N_SHARDS
    if half is None:
        return ref.at[pl.ds(pl.multiple_of(chip * n, 16), n), :]
    return ref.at[pl.ds(pl.multiple_of(chip * n + half * (n // 2), 16), n // 2), :]


def _gather_start(bufs, halved):
    n = len(bufs)

    def body(*refs):
        ins, send, recv, token = refs[:n], refs[n:2 * n], refs[2 * n:3 * n], refs[-1]
        x, y, c = _place()
        for a, buf in enumerate(ins):
            own = _chip_rows(buf, 2 * x + y, c if a in halved else None)
            for j, chip in enumerate(_other_chips(x, y)):
                pltpu.make_async_remote_copy(src_ref=own, dst_ref=own, send_sem=send[a].at[j], recv_sem=recv[a].at[j],
                                             device_id=(*chip, c), device_id_type=MESH).start()
        token[...] = jnp.zeros_like(token)

    outs = pl.pallas_call(
        body, name="gather_start", in_specs=[HBM] * n,
        out_specs=[SEM] * (2 * n) + [HBM] * n + [pl.BlockSpec(memory_space=pltpu.VMEM)],
        out_shape=[pltpu.SemaphoreType.DMA((3,))] * (2 * n) + [pltpu.HBM(b.shape, b.dtype) for b in bufs]
        + [jax.ShapeDtypeStruct((8, 128), F32)],
        input_output_aliases={a: 2 * n + a for a in range(n)},
        compiler_params=SPLIT_COPY,
    )(*[_in_hbm(b) for b in bufs])
    return outs[:n], outs[n:2 * n], outs[2 * n:3 * n], outs[-1]


def _gather_wait(name, buf, send_sem, recv_sem, after, halved=False):
    def body(buf_ref, send_ref, recv_ref, after_ref, out_ref):
        x, y, c = _place()
        half = c if halved else None
        own = _chip_rows(buf_ref, 2 * x + y, half)
        for j, chip in enumerate(_other_chips(x, y)):
            copy = pltpu.make_async_remote_copy(src_ref=own, dst_ref=_chip_rows(buf_ref, 2 * chip[0] + chip[1], half),
                                                send_sem=send_ref.at[j], recv_sem=recv_ref.at[j],
                                                device_id=(*chip, c), device_id_type=MESH)
            copy.wait_send()
            copy.wait_recv()

    return pl.pallas_call(
        body, name=name, in_specs=[HBM, SEM, SEM, ANY], out_specs=HBM, out_shape=pltpu.HBM(buf.shape, buf.dtype),
        input_output_aliases={0: 0}, compiler_params=SPLIT_COPY,
    )(buf, send_sem, recv_sem, after)


def _forward_halves(name, buf):
    def body(in_ref, out_ref, send_sems, recv_sems):
        x, y, c = _place()

        def copy(j, chip, half):
            rows = 2 * chip[0] + chip[1]
            return pltpu.make_async_remote_copy(
                src_ref=_chip_rows(in_ref, rows, half), dst_ref=_chip_rows(out_ref, rows, half), send_sem=send_sems.at[j],
                recv_sem=recv_sems.at[j], device_id=(x, y, 1 - c), device_id_type=MESH)

        chips = _other_chips(x, y)
        for j, chip in enumerate(chips):
            copy(j, chip, c).start()
        for j, chip in enumerate(chips):
            copy(j, chip, c).wait_send()
            copy(j, chip, 1 - c).wait_recv()

    return pl.pallas_call(
        body, name=name, in_specs=[ANY], out_specs=ANY, out_shape=jax.ShapeDtypeStruct(buf.shape, buf.dtype),
        input_output_aliases={0: 0},
        scratch_shapes=[pltpu.SemaphoreType.DMA((3,))] * 2,
    )(buf)


def _piece_rows(ref, k):
    p = ref.shape[0] // 8
    return ref.at[pl.ds(pl.multiple_of(k * p, 16), p), :]


def _exchange_start(name, arrays):
    n = len(arrays)
    zones = [lax.empty((7, a.shape[0] // 8, a.shape[1]), BF16) for a in arrays]

    def body(*refs):
        srcs, lands = refs[:n], refs[n:2 * n]
        send, recv, token = refs[2 * n:3 * n], refs[3 * n:4 * n], refs[-1]
        x, y, c = _place()
        for a, (src, land) in enumerate(zip(srcs, lands)):
            for m in range(1, 8):
                px, py, pc = _peer(x, y, c, m)
                pltpu.make_async_remote_copy(
                    src_ref=_piece_rows(src, 4 * px + 2 * py + pc), dst_ref=land.at[m - 1], send_sem=send[a].at[m - 1],
                    recv_sem=recv[a].at[m - 1], device_id=(px, py, pc), device_id_type=MESH).start()
        token[...] = jnp.zeros_like(token)

    outs = pl.pallas_call(
        body, name=name, in_specs=[HBM] * (2 * n),
        out_specs=[SEM] * (2 * n) + [HBM] * (2 * n) + [pl.BlockSpec(memory_space=pltpu.VMEM)],
        out_shape=[pltpu.SemaphoreType.DMA((7,))] * (2 * n) + [pltpu.HBM(a.shape, a.dtype) for a in arrays + zones]
        + [jax.ShapeDtypeStruct((8, 128), F32)],
        input_output_aliases={a: 2 * n + a for a in range(2 * n)},
        compiler_params=SPLIT_COPY,
    )(*[_in_hbm(a) for a in arrays + zones])
    return outs[:n], outs[n:2 * n], outs[2 * n:3 * n], outs[3 * n:4 * n], outs[-1]


def _exchange_wait(name, started, after):
    send_sems, recv_sems, arrays, zones, _ = started
    n = len(arrays)

    def body(*refs):
        srcs, lands = refs[:n], refs[n:2 * n]
        send, recv = refs[2 * n:3 * n], refs[3 * n:4 * n]
        x, y, c = _place()
        for a, (src, land) in enumerate(zip(srcs, lands)):
            for m in range(1, 8):
                px, py, pc = _peer(x, y, c, m)
                copy = pltpu.make_async_remote_copy(
                    src_ref=_piece_rows(src, 4 * px + 2 * py + pc), dst_ref=land.at[m - 1], send_sem=send[a].at[m - 1],
                    recv_sem=recv[a].at[m - 1], device_id=(px, py, pc), device_id_type=MESH)
                copy.wait_send()
                copy.wait_recv()

    outs = pl.pallas_call(
        body, name=name, in_specs=[HBM] * (2 * n) + [SEM] * (2 * n) + [ANY], out_specs=[HBM] * (2 * n),
        out_shape=[pltpu.HBM(a.shape, a.dtype) for a in list(arrays) + list(zones)],
        input_output_aliases={a: a for a in range(2 * n)}, compiler_params=SPLIT_COPY,
    )(*arrays, *zones, *send_sems, *recv_sems, after)
    return outs[n:]


def _sum_pieces(name, partial, recv, place_arr, tile, layer, dest=None):
    p, cols = recv.shape[1:]
    steps = p // tile

    def body(place_ref, o_ref, r_ref, *rest):
        total = o_ref[...]
        for m in range(7):
            total = total + r_ref[m].astype(F32)
        rest[-1][...] = total

    return pl.pallas_call(
        body, name=name,
        grid_spec=pltpu.PrefetchScalarGridSpec(
            num_scalar_prefetch=1, grid=(steps,),
            in_specs=[pl.BlockSpec((tile, cols), lambda i, place: (place[0] * steps + i, 0)),
                      pl.BlockSpec((7, tile, cols), lambda i, place: (0, i, 0))] + ([] if dest is None else [ANY]),
            out_specs=pl.BlockSpec((None, tile, cols), lambda i, place: (layer, place[1] * steps + i, 0))),
        out_shape=jax.ShapeDtypeStruct((DEPTH, 2 * p, cols), F32),
        input_output_aliases={} if dest is None else {3: 0},
        compiler_params=_params(),
    )(place_arr, partial, recv, *(() if dest is None else (dest,)))


def _swap_halves(name, bufs, parts):
    n = len(bufs)

    def body(*refs):
        ins, outs, send_sems, recv_sems = refs[:n], refs[n:2 * n], refs[2 * n], refs[2 * n + 1]
        x, y, c = _place()

        def half(ref, l, which):
            p = ref.shape[1] // 2
            return ref.at[l, pl.ds(pl.multiple_of(which * p, 8), p), :]

        def copy(k, which):
            a, l = parts[k]
            return pltpu.make_async_remote_copy(
                src_ref=half(ins[a], l, which), dst_ref=half(outs[a], l, which), send_sem=send_sems.at[k],
                recv_sem=recv_sems.at[k], device_id=(x, y, 1 - c), device_id_type=MESH)

        for k in range(len(parts)):
            copy(k, c).start()
        for k in range(len(parts)):
            copy(k, c).wait_send()
            copy(k, 1 - c).wait_recv()

    return pl.pallas_call(
        body, name=name, in_specs=[ANY] * n, out_specs=[ANY] * n,
        out_shape=[jax.ShapeDtypeStruct(b.shape, F32) for b in bufs],
        input_output_aliases={a: a for a in range(n)},
        scratch_shapes=[pltpu.SemaphoreType.DMA((len(parts),))] * 2,
    )(*bufs)


def _allreduce_small(packed, deps=()):
    def body(p_ref, *rest):
        out_ref, recv_ref, send1, recv1, send2, recv2 = rest[len(deps):]
        x, y, c = _place()
        me = 4 * x + 2 * y + c

        def index(p):
            return 4 * p[0] + 2 * p[1] + p[2]

        def copy(src, dst, sems, m):
            return pltpu.make_async_remote_copy(src_ref=src, dst_ref=dst, send_sem=sems[0].at[m - 1],
                                                recv_sem=sems[1].at[m - 1], device_id=_peer(x, y, c, m),
                                                device_id_type=MESH)

        first = [copy(p_ref.at[index(_peer(x, y, c, m))], recv_ref.at[me], (send1, recv1), m) for m in range(1, 8)]
        for cp in first:
            cp.start()
        recv_ref[me] = p_ref[me]
        for m in range(1, 8):
            slot = recv_ref.at[index(_peer(x, y, c, m))]
            copy(slot, slot, (send1, recv1), m).wait_recv()
        total = recv_ref[0]
        for d in range(1, 8):
            total = total + recv_ref[d]
        out_ref[me] = total
        second = [copy(out_ref.at[me], out_ref.at[me], (send2, recv2), m) for m in range(1, 8)]
        for cp in second:
            cp.start()
        for m in range(1, 8):
            slot = out_ref.at[index(_peer(x, y, c, m))]
            copy(slot, slot, (send2, recv2), m).wait_recv()
        for cp in first + second:
            cp.wait_send()

    vmem = pl.BlockSpec(memory_space=pltpu.VMEM)
    return pl.pallas_call(
        body, name="allreduce_small", in_specs=[vmem] + [ANY] * len(deps), out_specs=vmem,
        out_shape=jax.ShapeDtypeStruct(packed.shape, F32),
        scratch_shapes=[pltpu.VMEM(packed.shape, F32)] + [pltpu.SemaphoreType.DMA((7,))] * 4,
    )(packed, *deps)


def _adamw(name, w, g, m, v, rows_per_step, first=0, count=None, dests=None):
    layers, rows, cols = w.shape
    count = layers if count is None else count

    def body(w_ref, g_ref, m_ref, v_ref, *rest):
        d_ref, nm_ref, nv_ref = rest[-3:]
        gt = g_ref[...]
        nm = ADAM_B1 * m_ref[...] + (1.0 - ADAM_B1) * gt
        nv = ADAM_B2 * v_ref[...] + (1.0 - ADAM_B2) * (gt * gt)
        m_hat = nm / (1.0 - ADAM_B1 ** ADAM_STEP)
        v_hat = nv / (1.0 - ADAM_B2 ** ADAM_STEP)
        d_ref[...] = -ADAM_LR * (m_hat / (jnp.sqrt(v_hat) + ADAM_EPS) + ADAM_WD * w_ref[...])
        nm_ref[...] = nm
        nv_ref[...] = nv

    spec = pl.BlockSpec((1, rows_per_step, cols), lambda l, i: (first + l, i, 0))
    shape = jax.ShapeDtypeStruct(w.shape, F32)
    dests = () if dests is None else tuple(dests)
    return pl.pallas_call(
        body, name=name, grid=(count, rows // rows_per_step),
        in_specs=[spec] * 4 + [ANY] * len(dests), out_specs=[spec] * 3, out_shape=[shape] * 3,
        input_output_aliases={4 + k: k for k in range(len(dests))},
        compiler_params=_params(("arbitrary", "arbitrary")),
    )(w, g, m, v, *dests)


def _pack_small(pool_w, pool_scale, sinks, norm_pre, norm_post):
    sink_rows = jnp.zeros((ROWS_SINK, 128), F32).at[0, 0:2 * N_HEADS].set(sinks.reshape(-1))
    return jnp.concatenate([
        pool_w.reshape(ROWS_PW, 128), pool_scale.reshape(ROWS_SC, 128), norm_pre.reshape(ROWS_NORM, 128),
        norm_post.reshape(ROWS_NORM, 128), sink_rows, jnp.zeros((SMALL_ROWS - ROW_LOSS, 128), F32)], axis=0)


def _pack_small_grads(grads, loss):
    sink_rows = jnp.zeros((ROWS_SINK, 128), F32).at[0, 0:2 * N_HEADS].set(
        jnp.concatenate([g[4][:, 0] for g in grads]))
    return jnp.concatenate(
        [g[2].reshape(ROWS_PW // DEPTH, 128) for g in grads] + [g[3] for g in grads] + [g[5] for g in grads]
        + [g[6] for g in grads] + [sink_rows, loss, jnp.zeros((SMALL_ROWS - ROW_LOSS - 8, 128), F32)], axis=0)


def _unpack_small(packed):
    o = 0
    pool_w = packed[o:o + ROWS_PW].reshape(DEPTH, 4, 128, 128)
    o += ROWS_PW
    pool_scale = packed[o:o + ROWS_SC].reshape(DEPTH, 512)
    o += ROWS_SC
    norm_pre = packed[o:o + ROWS_NORM].reshape(DEPTH, D)
    o += ROWS_NORM
    norm_post = packed[o:o + ROWS_NORM].reshape(DEPTH, D)
    o += ROWS_NORM
    sinks = packed[o, 0:2 * N_HEADS].reshape(DEPTH, N_HEADS)
    return pool_w, pool_scale, sinks, norm_pre, norm_post


def kernel(x, w_in, pool_w, pool_scale, attn_sinks, w_out, norm_pre, norm_post, loss_target, m_w_in, m_pool_w, m_pool_scale, m_attn_sinks, m_w_out, m_norm_pre, m_norm_post, v_w_in, v_pool_w, v_pool_scale, v_attn_sinks, v_w_out, v_norm_pre, v_norm_post):
    cx, cy, cc = _place()
    chip_arr = jnp.reshape(2 * cx + cy, (1,)).astype(jnp.int32)
    place_arr = jnp.stack([4 * cx + 2 * cy + cc, cc]).astype(jnp.int32)
    t = lambda a: jnp.transpose(a, (0, 2, 1))
    w_in_t = t(w_in)
    xs, target = x[0], loss_target[0]
    pool_w_b = pool_w.astype(BF16)
    bias, tri = _attention_tables()
    scale3 = pool_scale.reshape(DEPTH, 1, D_POOL)
    pre3 = norm_pre.reshape(DEPTH, 1, D)
    post3 = norm_post.reshape(DEPTH, 1, D)

    wi = _place_cast("place_w_in", w_in_t, chip_arr, 288)
    wo = _place_cast("place_w_out", w_out, chip_arr, 256)
    send, recv, bufs, token = _gather_start([wi[0], wo[0], wi[1], wo[1]], halved=(0,))

    saved = []
    after = token
    for l in range(DEPTH):
        w_in_l = _gather_wait(f"gather_wait_in{l}", bufs[2 * l], send[2 * l], recv[2 * l], after, halved=(l == 0))
        if l == 0:
            w_in_l = _forward_halves("forward_w_in0", w_in_l)
        pu, pg, q, kv, ag = _fwd_in(l, xs, pre3, w_in_l)
        cat = _fwd_mix(l, pu, pg, q, kv, ag, pool_w_b, scale3, attn_sinks, bias, tri)
        w_out_l = _gather_wait(f"gather_wait_out{l}", bufs[2 * l + 1], send[2 * l + 1], recv[2 * l + 1], cat)
        if l < DEPTH - 1:
            y, x_next = _fwd_out(l, cat, w_out_l, xs, post3)
        else:
            y, x_next, loss = _fwd_out(l, cat, w_out_l, xs, post3, target)
        saved.append((xs, pu, pg, q, kv, ag, cat, y, w_in_l, w_out_l))
        xs = after = x_next

    x_in, pu, pg, q, kv, ag, cat, y, w_in_l, w_out_l = saved[1]
    dcat, dw_out1, dw_out1_b, dg_post1 = _bwd_out(1, xs, y, post3, cat, w_out_l)
    dproj, dpw1, dsc1, dsink1 = _bwd_mix(1, pu, pg, q, kv, ag, dcat, pool_w_b, scale3, attn_sinks, bias, tri)
    dx, dw_in1, dw_in1_b, dg_pre1 = _bwd_in(1, dproj, w_in_l, x_in, pre3, xs)
    ex1 = _exchange_start("exchange_start_1", [dw_in1_b, dw_out1_b])

    x_in, pu, pg, q, kv, ag, cat, y, w_in_l, w_out_l = saved[0]
    dcat, dw_out0, dw_out0_b, dg_post0 = _bwd_out(0, dx, y, post3, cat, w_out_l, deps=(ex1[4],))
    ex0_out = _exchange_start("exchange_start_out0", [dw_out0_b])
    dproj, dpw0, dsc0, dsink0 = _bwd_mix(0, pu, pg, q, kv, ag, dcat, pool_w_b, scale3, attn_sinks, bias, tri,
                                         deps=(ex0_out[4],))
    recv_in1, recv_out1 = _exchange_wait("exchange_wait_1", ex1, dproj)
    g_in = _sum_pieces("sum_pieces_in1", dw_in1, recv_in1, place_arr, 96, 1)
    g_out = _sum_pieces("sum_pieces_out1", dw_out1, recv_out1, place_arr, 64, 1)
    dw_in0, dw_in0_b = _bwd_in_dw(0, dproj, x_in, pre3, deps=(g_in, g_out))
    ex0_in = _exchange_start("exchange_start_in0", [dw_in0_b])

    grad_x, dg_pre0 = _bwd_in_dx(0, dproj, w_in_l, x_in, pre3, dx, deps=(ex0_in[4],))
    (recv_out0,) = _exchange_wait("exchange_wait_out0", ex0_out, grad_x)
    g_out = _sum_pieces("sum_pieces_out0", dw_out0, recv_out0, place_arr, 64, 0, dest=g_out)
    g_in, grad_w_out = _swap_halves("swap_halves_a", [g_in, g_out], [(0, 1), (1, 0), (1, 1)])
    m_in_t, v_in_t = t(m_w_in), t(v_w_in)
    d_out, nm_out, nv_out = _adamw("adamw_w_out", w_out, grad_w_out, m_w_out, v_w_out, 256)
    upd_in = _adamw("adamw_w_in1", w_in_t, g_in, m_in_t, v_in_t, 288, first=1, count=1)

    grads = [(None, None, dpw0, dsc0, dsink0, dg_pre0, dg_post0), (None, None, dpw1, dsc1, dsink1, dg_pre1, dg_post1)]
    packed = _pack_small_grads(grads, loss).reshape(8, PIECE_ROWS, 128)
    g_small = _allreduce_small(packed, deps=(upd_in[0], d_out)).reshape(1, SMALL_ROWS, 128)
    small = lambda *a: _pack_small(*a).reshape(1, SMALL_ROWS, 128)
    d_s, nm_s, nv_s = _adamw(
        "adamw_small", small(pool_w, pool_scale, attn_sinks, norm_pre, norm_post), g_small,
        small(m_pool_w, m_pool_scale, m_attn_sinks, m_norm_pre, m_norm_post),
        small(v_pool_w, v_pool_scale, v_attn_sinks, v_norm_pre, v_norm_post), SMALL_ROWS)

    (recv_in0,) = _exchange_wait("exchange_wait_in0", ex0_in, d_s)
    g_in = _sum_pieces("sum_pieces_in0", dw_in0, recv_in0, place_arr, 96, 0, dest=g_in)
    (grad_w_in_t,) = _swap_halves("swap_halves_b", [g_in], [(0, 0)])
    d_in, nm_in, nv_in = _adamw("adamw_w_in0", w_in_t, grad_w_in_t, m_in_t, v_in_t, 288, first=0, count=1, dests=upd_in)

    g_pw, g_sc, g_sk, g_pre, g_post = _unpack_small(g_small[0])
    d_pw, d_sc, d_sk, d_pre, d_post = _unpack_small(d_s[0])
    m_pw, m_sc, m_sk, m_pre, m_post = _unpack_small(nm_s[0])
    v_pw, v_sc, v_sk, v_pre, v_post = _unpack_small(nv_s[0])
    return (g_small[0, ROW_LOSS, 0], grad_x[None], t(grad_w_in_t), g_pw, g_sc, g_sk, grad_w_out, g_pre, g_post,
            t(d_in), d_pw, d_sc, d_sk, d_out, d_pre, d_post,
            t(nm_in), m_pw, m_sc, m_sk, nm_out, m_pre, m_post,
            t(nv_in), v_pw, v_sc, v_sk, nv_out, v_pre, v_post)
```

```python
import jax
import jax.numpy as jnp
from jax import lax
from jax.experimental import pallas as pl
from jax.experimental.pallas import tpu as pltpu

F32 = jnp.float32
BF16 = jnp.bfloat16

S = 2048
D = 1024
DEPTH = 2
D_POOL = 512
POOL_WINDOWS = (2, 4, 8, 16)
N_HEADS = 8
D_IN = 2304
N_SHARDS = 4
W_IN_SHARD = D_IN // N_SHARDS
W_OUT_SHARD = D // N_SHARDS
BLK = 128
NB = S // BLK
HALO = 16
EPS = 1e-6
NEG_INF = -1e30
C_PU, C_PG, C_Q, C_K, C_V, C_AG = 0, 512, 1024, 1536, 1664, 1792

ADAM_LR = 0.001
ADAM_B1 = 0.9
ADAM_B2 = 0.999
ADAM_EPS = 1e-08
ADAM_WD = 0.01
ADAM_STEP = 10

TM = 512
VMEM_LIMIT = 56 * 1024 * 1024

NT = (((1,), (1,)), ((), ()))
TN = (((0,), (0,)), ((), ()))

MESH = pl.DeviceIdType.MESH
ANY = pl.BlockSpec(memory_space=pl.ANY)

MISC_LOSS = 48
MISC_ROWS = 64


def _params(sem=("arbitrary",)):
    return pltpu.CompilerParams(dimension_semantics=sem, vmem_limit_bytes=VMEM_LIMIT)


def _sigmoid(v):
    return 1.0 / (1.0 + jnp.exp(-v))


def _rows8(v):
    r, c = v.shape
    return v.reshape(r // 8, 8, c).sum(axis=0)


def _layer(l, *shape):
    zeros = (0,) * len(shape)
    return pl.BlockSpec((None,) + shape, lambda i: (l,) + zeros)


def _whole(shape):
    zeros = (0,) * len(shape)
    return pl.BlockSpec(shape, lambda i: zeros, pipeline_mode=pl.Buffered(1))


def _fwd_in(l, x, g_pre, w_in_t):
    def body(x_ref, g_ref, w_ref, pu_ref, pg_ref, q_ref, kv_ref, ag_ref):
        xt = x_ref[...]
        r = lax.rsqrt(jnp.mean(xt * xt, axis=-1, keepdims=True) + EPS)
        h = (xt * r * g_ref[...]).astype(BF16)

        def proj(lo, hi):
            return lax.dot_general(h, w_ref[lo:hi, :], NT, preferred_element_type=F32)

        pu_ref[...] = proj(C_PU, C_PG)
        pg_ref[...] = proj(C_PG, C_Q)
        q_ref[...] = proj(C_Q, C_K).astype(BF16)
        kv_ref[...] = proj(C_K, C_AG).astype(BF16)
        ag_ref[...] = proj(C_AG, D_IN)

    row = lambda w: pl.BlockSpec((TM, w), lambda i: (i, 0))
    return pl.pallas_call(
        body, name="fwd_in", grid=(S // TM,),
        in_specs=[row(D), _layer(l, 1, D), _whole((D_IN, D))],
        out_specs=[row(512), row(512), row(512), row(256), row(512)],
        out_shape=[jax.ShapeDtypeStruct((S, 512), F32), jax.ShapeDtypeStruct((S, 512), F32),
                   jax.ShapeDtypeStruct((S, 512), BF16), jax.ShapeDtypeStruct((S, 256), BF16),
                   jax.ShapeDtypeStruct((S, 512), F32)],
        compiler_params=_params(),
    )(x, g_pre, w_in_t)


LOG2E = 1.4426950408889634
SCORE_SCALE = 0.125 * LOG2E


def _attention_tables():
    qi = jnp.arange(BLK)[:, None]
    kj = jnp.arange(BLK)[None, :]
    dist = ((qi - kj) % BLK).astype(F32)
    slopes = jnp.exp2(-jnp.arange(1, N_HEADS + 1, dtype=F32))
    bias = -(slopes * LOG2E)[:, None, None] * dist[None]
    first = jnp.where(kj > qi, NEG_INF, bias)
    return jnp.stack([first, bias]), (kj <= qi).astype(BF16)


def _own_block_mask():
    return lax.broadcasted_iota(jnp.int32, (BLK, BLK), 1) <= lax.broadcasted_iota(jnp.int32, (BLK, BLK), 0)


def _merge(full, own):
    return jnp.where(own, full[:, BLK:], full[:, :BLK])


def _spread(v, tri):
    own = v * tri
    return jnp.concatenate([v - own, own], axis=1)


def _head_variants(cur, prev):
    both = jnp.concatenate([prev, cur], axis=0).astype(F32)
    swapped = pltpu.roll(both, 64, axis=1)
    low = lax.broadcasted_iota(jnp.int32, both.shape, 1) < 64
    zero = jnp.zeros_like(both)
    return ((jnp.where(low, both, zero).astype(BF16), jnp.where(low, zero, swapped).astype(BF16)),
            (jnp.where(low, swapped, zero).astype(BF16), jnp.where(low, zero, both).astype(BF16)))


def _head_of(hkv, t, half):
    return hkv * 4 + 2 * t + half


def _rows(v, t):
    return v[t * BLK:(t + 1) * BLK]


def _stack_tiles(ref, hkv, offset=0):
    lo = offset + 2 * hkv * 128
    return jnp.concatenate([ref[:, lo:lo + 128], ref[:, lo + 128:lo + 256]], axis=0)


def _scores(q2, k_var, own):
    s = {}
    for hkv in range(2):
        for half in range(2):
            full = lax.dot_general(q2[hkv], k_var[hkv][half], NT, preferred_element_type=F32)
            for t in range(2):
                s[hkv, t, half] = _merge(_rows(full, t), own)
    return s


def _softmax(s, bias, sink):
    s = s * SCORE_SCALE + bias
    sink2 = sink * LOG2E
    m = jnp.maximum(jnp.max(s, axis=-1, keepdims=True), sink2)
    p = jnp.exp2(s - m)
    e_sink = jnp.exp2(sink2 - m)
    inv = 1.0 / (jnp.sum(p, axis=-1, keepdims=True) + e_sink)
    return p * inv, e_sink * inv


def _spread_pair(v, hkv, half, tri):
    return jnp.concatenate([_spread(v[hkv, t, half].astype(BF16), tri) for t in range(2)], axis=0)


def _pool_block(ext_ref, i, g, w):
    lanes = slice(g * 128, (g + 1) * 128)
    u = ext_ref[HALO:HALO + BLK, lanes]
    acc = u
    for j in range(1, w):
        acc = acc + ext_ref[HALO - j:HALO - j + BLK, lanes]
    t = (i * BLK + lax.broadcasted_iota(jnp.int32, (BLK, 1), 0)).astype(F32)
    inv = 1.0 / jnp.minimum(t + 1.0, float(w))
    return acc * inv - u, inv


def _fwd_mix(l, pu, pg, q, kv, ag, pool_w, pool_scale, sinks, bias, tri):
    def body(pu_ref, pup_ref, pg_ref, q_ref, kv_ref, kvp_ref, ag_ref, pw_ref, sc_ref, sink_ref, bias_ref, tri_ref,
             cat_ref, ext_ref):
        i = pl.program_id(0)
        ext_ref[0:HALO, :] = jnp.where(i > 0, pup_ref[...], 0.0)
        ext_ref[HALO:HALO + BLK, :] = pu_ref[...]
        for g, w in enumerate(POOL_WINDOWS):
            lanes = slice(g * 128, (g + 1) * 128)
            pooled, _ = _pool_block(ext_ref, i, g, w)
            mixed = jnp.dot(pooled.astype(BF16), pw_ref[g], preferred_element_type=F32)
            gate = pg_ref[:, lanes]
            cat_ref[:, lanes] = (mixed * sc_ref[:, lanes] * (gate * _sigmoid(gate))).astype(BF16)

        own = _own_block_mask()
        tri = tri_ref[...]
        k_var = _head_variants(kv_ref[:, 0:128], kvp_ref[:, 0:128])
        v_var = _head_variants(kv_ref[:, 128:256], kvp_ref[:, 128:256])
        s = _scores([_stack_tiles(q_ref, hkv) for hkv in range(2)], k_var, own)
        p = {}
        for (hkv, t, half), s_head in s.items():
            head = _head_of(hkv, t, half)
            p[hkv, t, half], _ = _softmax(s_head, bias_ref[head], sink_ref[l, head])
        for hkv in range(2):
            o2 = jnp.zeros((2 * BLK, 128), F32)
            for half in range(2):
                o2 = o2 + jnp.dot(_spread_pair(p, hkv, half, tri), v_var[hkv][half], preferred_element_type=F32)
            for t in range(2):
                lo = (2 * hkv + t) * 128
                gate = ag_ref[:, lo:lo + 128]
                cat_ref[:, D_POOL + lo:D_POOL + lo + 128] = (_rows(o2, t) * (gate * _sigmoid(gate))).astype(BF16)

    blk = lambda w: pl.BlockSpec((BLK, w), lambda i: (i, 0))
    prev = lambda w: pl.BlockSpec((BLK, w), lambda i: (jnp.maximum(i - 1, 0), 0))
    halo = pl.BlockSpec((HALO, 512), lambda i: (jnp.maximum(i * (BLK // HALO) - 1, 0), 0))
    return pl.pallas_call(
        body, name="fwd_mix", grid=(NB,),
        in_specs=[blk(512), halo, blk(512), blk(512), blk(256), prev(256), blk(512),
                  _layer(l, 4, 128, 128), _layer(l, 1, 512), pl.BlockSpec(memory_space=pltpu.SMEM),
                  pl.BlockSpec((None, N_HEADS, BLK, BLK), lambda i: (jnp.minimum(i, 1), 0, 0, 0)), _whole((BLK, BLK))],
        out_specs=blk(D),
        out_shape=jax.ShapeDtypeStruct((S, D), BF16),
        scratch_shapes=[pltpu.VMEM((HALO + BLK, 512), F32)],
        compiler_params=_params(),
    )(pu, pu, pg, q, kv, kv, ag, pool_w, pool_scale, sinks, bias, tri)


def _fwd_out(l, cat, w_out, x, g_post, target=None):
    last = target is not None
    n_steps = S // TM

    def body(*refs):
        if last:
            cat_ref, w_ref, x_ref, g_ref, t_ref, y_ref, dx_ref, loss_ref, acc_ref = refs
        else:
            cat_ref, w_ref, x_ref, g_ref, y_ref, xn_ref = refs
        y = jnp.dot(cat_ref[...], w_ref[...], preferred_element_type=F32)
        y_ref[...] = y
        r = lax.rsqrt(jnp.mean(y * y, axis=-1, keepdims=True) + EPS)
        xn = x_ref[...] + y * r * g_ref[...]
        if not last:
            xn_ref[...] = xn
            return
        step = pl.program_id(0)
        err = xn - t_ref[...]
        dx_ref[...] = err * (1.0 / D)

        @pl.when(step == 0)
        def _():
            acc_ref[...] = jnp.zeros_like(acc_ref)

        acc_ref[...] += _rows8(err * err)

        @pl.when(step == n_steps - 1)
        def _():
            loss_ref[...] = jnp.full((8, 128), (0.5 / D) * jnp.sum(acc_ref[...]), F32)

    row = lambda: pl.BlockSpec((TM, D), lambda i: (i, 0))
    act = jax.ShapeDtypeStruct((S, D), F32)
    in_specs = [row(), _whole((D, D)), row(), _layer(l, 1, D)]
    args = [cat, w_out, x, g_post]
    if last:
        return pl.pallas_call(
            body, name="fwd_out_loss", grid=(n_steps,),
            in_specs=in_specs + [row()],
            out_specs=[row(), row(), pl.BlockSpec((8, 128), lambda i: (0, 0))],
            out_shape=[act, act, jax.ShapeDtypeStruct((8, 128), F32)],
            scratch_shapes=[pltpu.VMEM((8, D), F32)],
            compiler_params=_params(),
        )(*args, target)
    return pl.pallas_call(
        body, name="fwd_out", grid=(n_steps,),
        in_specs=in_specs, out_specs=[row(), row()], out_shape=[act, act],
        compiler_params=_params(),
    )(*args)


def _store_lane_rows(ref, acc):
    total = jnp.sum(acc, axis=0, keepdims=True)
    for k in range(ref.shape[0]):
        ref[k:k + 1, :] = total[:, k * 128:(k + 1) * 128]


def _bwd_out(l, dxn, y, g_post, cat, w_out, deps=()):
    n_steps = S // TM

    def body(dz_ref, y_ref, g_ref, cat_ref, w_ref, *rest):
        dcat_ref, dw_ref, dwb_ref, dg_ref, acc_ref = rest[len(deps):]
        step = pl.program_id(0)

        @pl.when(step == 0)
        def _():
            dw_ref[...] = jnp.zeros_like(dw_ref)
            acc_ref[...] = jnp.zeros_like(acc_ref)

        y = y_ref[...]
        dz = dz_ref[...]
        r = lax.rsqrt(jnp.mean(y * y, axis=-1, keepdims=True) + EPS)
        a = dz * g_ref[...]
        dy = r * a - y * (r * r * r) * jnp.mean(a * y, axis=-1, keepdims=True)
        acc_ref[...] += _rows8(dz * (y * r))
        dyb = dy.astype(BF16)
        dcat_ref[...] = lax.dot_general(dyb, w_ref[...], NT, preferred_element_type=F32)
        dw_ref[...] += lax.dot_general(cat_ref[...], dyb, TN, preferred_element_type=F32)

        @pl.when(step == n_steps - 1)
        def _():
            _store_lane_rows(dg_ref, acc_ref[...])
            dwb_ref[...] = dw_ref[...].astype(BF16)

    row = lambda: pl.BlockSpec((TM, D), lambda i: (i, 0))
    full = _whole
    return pl.pallas_call(
        body, name="bwd_out", grid=(n_steps,),
        in_specs=[row(), row(), _layer(l, 1, D), row(), full((D, D))] + [ANY] * len(deps),
        out_specs=[row(), full((D, D)), full((D, D)), full((8, 128))],
        out_shape=[jax.ShapeDtypeStruct((S, D), F32), jax.ShapeDtypeStruct((D, D), F32),
                   jax.ShapeDtypeStruct((D, D), BF16), jax.ShapeDtypeStruct((8, 128), F32)],
        scratch_shapes=[pltpu.VMEM((8, D), F32)],
        compiler_params=_params(),
    )(dxn, y, g_post, cat, w_out, *deps)


def _bwd_mix(l, pu, pg, q, kv, ag, dcat, pool_w, pool_scale, sinks, bias, tri, deps=(), dpw_dest=None):
    deps = tuple(deps) + (() if dpw_dest is None else (dpw_dest,))

    def body(pu_ref, pup_ref, pg_ref, q_ref, kv_ref, kvp_ref, ag_ref, dcat_ref, pw_ref, sc_ref, sink_ref, bias_ref,
             tri_ref, *rest):
        dproj_ref, dpw_ref, dsc_ref, dsink_ref, ext_ref, dext_ref, dkv_ref = rest[len(deps):]
        step = pl.program_id(0)
        i = NB - 1 - step

        @pl.when(step == 0)
        def _():
            dpw_ref[...] = jnp.zeros_like(dpw_ref)
            dsc_ref[...] = jnp.zeros_like(dsc_ref)
            dsink_ref[...] = jnp.zeros_like(dsink_ref)
            dext_ref[BLK:BLK + HALO, :] = jnp.zeros((HALO, 512), F32)
            dkv_ref[...] = jnp.zeros_like(dkv_ref)

        ext_ref[0:HALO, :] = jnp.where(i > 0, pup_ref[...], 0.0)
        ext_ref[HALO:HALO + BLK, :] = pu_ref[...]
        for g, w in enumerate(POOL_WINDOWS):
            lanes = slice(g * 128, (g + 1) * 128)
            pooled, inv = _pool_block(ext_ref, i, g, w)
            pooled_b = pooled.astype(BF16)
            mixed = jnp.dot(pooled_b, pw_ref[g], preferred_element_type=F32)
            scale = sc_ref[:, lanes]
            gate = pg_ref[:, lanes]
            sg = _sigmoid(gate)
            dpo = dcat_ref[:, lanes]
            dproj_ref[:, C_PG + g * 128:C_PG + (g + 1) * 128] = (
                dpo * (mixed * scale) * (sg * (1.0 + gate * (1.0 - sg)))).astype(BF16)
            dms = dpo * (gate * sg)
            dsc_ref[g:g + 1, :] += jnp.sum(dms * mixed, axis=0, keepdims=True)
            dmixed = (dms * scale).astype(BF16)
            dpw_ref[g] += lax.dot_general(pooled_b, dmixed, TN, preferred_element_type=F32)
            dpooled = lax.dot_general(dmixed, pw_ref[g], NT, preferred_element_type=F32)
            dext_ref[0:BLK, lanes] = dpooled * inv
            acc = dext_ref[0:BLK, lanes]
            for j in range(1, w):
                acc = acc + dext_ref[j:j + BLK, lanes]
            dproj_ref[:, C_PU + g * 128:C_PU + (g + 1) * 128] = (acc - dpooled).astype(BF16)
        dext_ref[BLK:BLK + HALO, :] = dext_ref[0:HALO, :]

        own = _own_block_mask()
        tri = tri_ref[...]
        k_var = _head_variants(kv_ref[:, 0:128], kvp_ref[:, 0:128])
        v_var = _head_variants(kv_ref[:, 128:256], kvp_ref[:, 128:256])
        q2 = [_stack_tiles(q_ref, hkv) for hkv in range(2)]
        s = _scores(q2, k_var, own)
        p, p_sink = {}, {}
        for key, s_head in s.items():
            head = _head_of(*key)
            p[key], p_sink[key] = _softmax(s_head, bias_ref[head], sink_ref[l, head])

        do2, p_b, dp = [], {}, {}
        for hkv in range(2):
            gate = _stack_tiles(ag_ref, hkv)
            sg = _sigmoid(gate)
            dca = _stack_tiles(dcat_ref, hkv, D_POOL)
            do2.append((dca * (gate * sg)).astype(BF16))
            o2 = jnp.zeros((2 * BLK, 128), F32)
            for half in range(2):
                p_b[hkv, half] = _spread_pair(p, hkv, half, tri)
                o2 = o2 + jnp.dot(p_b[hkv, half], v_var[hkv][half], preferred_element_type=F32)
                full = lax.dot_general(do2[hkv], v_var[hkv][half], NT, preferred_element_type=F32)
                for t in range(2):
                    dp[hkv, t, half] = _merge(_rows(full, t), own)
            dag = dca * o2 * (sg * (1.0 + gate * (1.0 - sg)))
            for t in range(2):
                lo = C_AG + (2 * hkv + t) * 128
                dproj_ref[:, lo:lo + 128] = _rows(dag, t).astype(BF16)

        ds = {}
        for key in p:
            delta = jnp.sum(p[key] * dp[key], axis=-1, keepdims=True)
            ds[key] = p[key] * (dp[key] - delta)
            head = _head_of(*key)
            dsink_ref[head:head + 1, :] += jnp.broadcast_to(
                -jnp.sum(p_sink[key] * delta, axis=0, keepdims=True), (1, 128))

        dk_acc = [[None, None], [None, None]]
        dv_acc = [[None, None], [None, None]]
        for hkv in range(2):
            dq2 = jnp.zeros((2 * BLK, 128), F32)
            for half in range(2):
                ds_b = _spread_pair(ds, hkv, half, tri)
                dq2 = dq2 + jnp.dot(ds_b, k_var[hkv][half], preferred_element_type=F32)
                dk_acc[hkv][half] = lax.dot_general(ds_b, q2[hkv], TN, preferred_element_type=F32)
                dv_acc[hkv][half] = lax.dot_general(p_b[hkv, half], do2[hkv], TN, preferred_element_type=F32)
            for t in range(2):
                lo = C_Q + (2 * hkv + t) * 128
                dproj_ref[:, lo:lo + 128] = (_rows(dq2, t) * 0.125).astype(BF16)

        low = lax.broadcasted_iota(jnp.int32, (2 * BLK, 128), 1) < 64

        def gather_heads(acc):
            return jnp.where(low, acc[0][0] + pltpu.roll(acc[0][1], 64, axis=1),
                             pltpu.roll(acc[1][0], 64, axis=1) + acc[1][1])

        dk = gather_heads(dk_acc) * 0.125
        dv = gather_heads(dv_acc)
        dproj_ref[:, C_K:C_V] = (dk[BLK:, :] + dkv_ref[:, 0:128]).astype(BF16)
        dproj_ref[:, C_V:C_AG] = (dv[BLK:, :] + dkv_ref[:, 128:256]).astype(BF16)
        dkv_ref[:, 0:128] = dk[:BLK, :]
        dkv_ref[:, 128:256] = dv[:BLK, :]

    rev = lambda w: pl.BlockSpec((BLK, w), lambda s: (NB - 1 - s, 0))
    prev = lambda w: pl.BlockSpec((BLK, w), lambda s: (jnp.maximum(NB - 2 - s, 0), 0))
    halo = pl.BlockSpec((HALO, 512), lambda s: (jnp.maximum((NB - 1 - s) * (BLK // HALO) - 1, 0), 0))
    return pl.pallas_call(
        body, name="bwd_mix", grid=(NB,),
        in_specs=[rev(512), halo, rev(512), rev(512), rev(256), prev(256), rev(512), rev(D),
                  _layer(l, 4, 128, 128), _layer(l, 1, 512), pl.BlockSpec(memory_space=pltpu.SMEM),
                  pl.BlockSpec((None, N_HEADS, BLK, BLK), lambda s: (jnp.minimum(NB - 1 - s, 1), 0, 0, 0)),
                  _whole((BLK, BLK))] + [ANY] * len(deps),
        out_specs=[rev(D_IN), _layer(l, 4, 128, 128),
                   pl.BlockSpec((4, 128), lambda s: (0, 0)), pl.BlockSpec((8, 128), lambda s: (0, 0))],
        out_shape=[jax.ShapeDtypeStruct((S, D_IN), BF16), jax.ShapeDtypeStruct((DEPTH, 4, 128, 128), F32),
                   jax.ShapeDtypeStruct((4, 128), F32), jax.ShapeDtypeStruct((8, 128), F32)],
        input_output_aliases={} if dpw_dest is None else {12 + len(deps): 1},
        scratch_shapes=[pltpu.VMEM((HALO + BLK, 512), F32), pltpu.VMEM((BLK + HALO, 512), F32),
                        pltpu.VMEM((BLK, 256), F32)],
        compiler_params=_params(),
    )(pu, pu, pg, q, kv, kv, ag, dcat, pool_w, pool_scale, sinks, bias, tri, *deps)


def _bwd_in_dw(l, dproj, x, g_pre, deps=()):
    n_steps = S // TM

    def body(dp_ref, x_ref, g_ref, *rest):
        dw_ref, dwb_ref = rest[len(deps):]
        step = pl.program_id(0)

        @pl.when(step == 0)
        def _():
            dw_ref[...] = jnp.zeros_like(dw_ref)

        xt = x_ref[...]
        r = lax.rsqrt(jnp.mean(xt * xt, axis=-1, keepdims=True) + EPS)
        h = (xt * r * g_ref[...]).astype(BF16)
        dw_ref[...] += lax.dot_general(dp_ref[...], h, TN, preferred_element_type=F32)

        @pl.when(step == n_steps - 1)
        def _():
            dwb_ref[...] = dw_ref[...].astype(BF16)

    row = lambda w: pl.BlockSpec((TM, w), lambda i: (i, 0))
    full = _whole
    return pl.pallas_call(
        body, name="bwd_in_dw", grid=(n_steps,),
        in_specs=[row(D_IN), row(D), _layer(l, 1, D)] + [ANY] * len(deps),
        out_specs=[full((D_IN, D)), full((D_IN, D))],
        out_shape=[jax.ShapeDtypeStruct((D_IN, D), F32), jax.ShapeDtypeStruct((D_IN, D), BF16)],
        compiler_params=_params(),
    )(dproj, x, g_pre, *deps)


def _bwd_in_dx(l, dproj, w_in_t, x, g_pre, dres, deps=()):
    n_steps = S // TM

    def body(dp_ref, w_ref, x_ref, g_ref, dres_ref, *rest):
        dx_ref, dg_ref, acc_ref = rest[len(deps):]
        step = pl.program_id(0)

        @pl.when(step == 0)
        def _():
            acc_ref[...] = jnp.zeros_like(acc_ref)

        dh = jnp.dot(dp_ref[...], w_ref[...], preferred_element_type=F32)
        xt = x_ref[...]
        r = lax.rsqrt(jnp.mean(xt * xt, axis=-1, keepdims=True) + EPS)
        xn = xt * r
        acc_ref[...] += _rows8(dh * xn)
        a = dh * g_ref[...]
        dx_ref[...] = dres_ref[...] + (r * a - xt * (r * r * r) * jnp.mean(a * xt, axis=-1, keepdims=True))

        @pl.when(step == n_steps - 1)
        def _():
            _store_lane_rows(dg_ref, acc_ref[...])

    row = lambda w: pl.BlockSpec((TM, w), lambda i: (i, 0))
    full = _whole
    return pl.pallas_call(
        body, name="bwd_in_dx", grid=(n_steps,),
        in_specs=[row(D_IN), full((D_IN, D)), row(D), _layer(l, 1, D), row(D)] + [ANY] * len(deps),
        out_specs=[row(D), full((8, 128))],
        out_shape=[jax.ShapeDtypeStruct((S, D), F32), jax.ShapeDtypeStruct((8, 128), F32)],
        scratch_shapes=[pltpu.VMEM((8, D), F32)],
        compiler_params=_params(),
    )(dproj, w_in_t, x, g_pre, dres, *deps)


HBM =pl.BlockSpec(memory_space=pltpu.HBM)
SEM = pl.BlockSpec(memory_space=pltpu.SEMAPHORE)
SPLIT_COPY = pltpu.CompilerParams(has_side_effects=pltpu.SideEffectType.DATAFLOW_SIDE_EFFECTING)


def _in_hbm(a):
    return pltpu.with_memory_space_constraint(a, pltpu.HBM)

def _place():
    return lax.axis_index("x"), lax.axis_index("y"), lax.axis_index("c")


def _other_chips(x, y):
    return [(1 - x, y), (x, 1 - y), (1 - x, 1 - y)]


def _peer(x, y, c, m):
    return (x ^ (m >> 2), y ^ ((m >> 1) & 1), c ^ (m & 1))


def _place_cast(name, src, chip_arr, tile):
    _, n, cols = src.shape
    steps = n // tile

    def body(chip_ref, s0_ref, s1_ref, o0_ref, o1_ref):
        o0_ref[...] = s0_ref[...].astype(BF16)
        o1_ref[...] = s1_ref[...].astype(BF16)

    return pl.pallas_call(
        body, name=name,
        grid_spec=pltpu.PrefetchScalarGridSpec(
            num_scalar_prefetch=1, grid=(steps,),
            in_specs=[pl.BlockSpec((None, tile, cols), lambda i, chip: (0, i, 0)),
                      pl.BlockSpec((None, tile, cols), lambda i, chip: (1, i, 0))],
            out_specs=[pl.BlockSpec((tile, cols), lambda i, chip: (chip[0] * steps + i, 0))] * 2),
        out_shape=[jax.ShapeDtypeStruct((N_SHARDS * n, cols), BF16)] * 2,
        compiler_params=_params(),
    )(chip_arr, src, src)


def _chip_rows(ref, chip, half=None):
    n = ref.shape[0] // N_SHARDS
    if half is None:
        return ref.at[pl.ds(pl.multiple_of(chip * n, 16), n), :]
    return ref.at[pl.ds(pl.multiple_of(chip * n + half * (n // 2), 16), n // 2), :]


def _gather_start(bufs, halved):
    n = len(bufs)

    def body(*refs):
        ins, send, recv, token = refs[:n], refs[n:2 * n], refs[2 * n:3 * n], refs[-1]
        x, y, c = _place()
        for a, buf in enumerate(ins):
            own = _chip_rows(buf, 2 * x + y, c if a in halved else None)
            for j, chip in enumerate(_other_chips(x, y)):
                pltpu.make_async_remote_copy(src_ref=own, dst_ref=own, send_sem=send[a].at[j], recv_sem=recv[a].at[j],
                                             device_id=(*chip, c), device_id_type=MESH).start()
        token[...] = jnp.zeros_like(token)

    outs = pl.pallas_call(
        body, name="gather_start", in_specs=[HBM] * n,
        out_specs=[SEM] * (2 * n) + [HBM] * n + [pl.BlockSpec(memory_space=pltpu.VMEM)],
        out_shape=[pltpu.SemaphoreType.DMA((3,))] * (2 * n) + [pltpu.HBM(b.shape, b.dtype) for b in bufs]
        + [jax.ShapeDtypeStruct((8, 128), F32)],
        input_output_aliases={a: 2 * n + a for a in range(n)},
        compiler_params=SPLIT_COPY,
    )(*[_in_hbm(b) for b in bufs])
    return outs[:n], outs[n:2 * n], outs[2 * n:3 * n], outs[-1]


def _gather_wait(name, buf, send_sem, recv_sem, after, halved=False):
    def body(buf_ref, send_ref, recv_ref, after_ref, out_ref):
        x, y, c = _place()
        half = c if halved else None
        own = _chip_rows(buf_ref, 2 * x + y, half)
        for j, chip in enumerate(_other_chips(x, y)):
            copy = pltpu.make_async_remote_copy(src_ref=own, dst_ref=_chip_rows(buf_ref, 2 * chip[0] + chip[1], half),
                                                send_sem=send_ref.at[j], recv_sem=recv_ref.at[j],
                                                device_id=(*chip, c), device_id_type=MESH)
            copy.wait_send()
            copy.wait_recv()

    return pl.pallas_call(
        body, name=name, in_specs=[HBM, SEM, SEM, ANY], out_specs=HBM, out_shape=pltpu.HBM(buf.shape, buf.dtype),
        input_output_aliases={0: 0}, compiler_params=SPLIT_COPY,
    )(buf, send_sem, recv_sem, after)


def _forward_halves(name, buf):
    def body(in_ref, out_ref, send_sems, recv_sems):
        x, y, c = _place()

        def copy(j, chip, half):
            rows = 2 * chip[0] + chip[1]
            return pltpu.make_async_remote_copy(
                src_ref=_chip_rows(in_ref, rows, half), dst_ref=_chip_rows(out_ref, rows, half), send_sem=send_sems.at[j],
                recv_sem=recv_sems.at[j], device_id=(x, y, 1 - c), device_id_type=MESH)

        chips = _other_chips(x, y)
        for j, chip in enumerate(chips):
            copy(j, chip, c).start()
        for j, chip in enumerate(chips):
            copy(j, chip, c).wait_send()
            copy(j, chip, 1 - c).wait_recv()

    return pl.pallas_call(
        body, name=name, in_specs=[ANY], out_specs=ANY, out_shape=jax.ShapeDtypeStruct(buf.shape, buf.dtype),
        input_output_aliases={0: 0},
        scratch_shapes=[pltpu.SemaphoreType.DMA((3,))] * 2,
    )(buf)


def _piece_rows(ref, k):
    p = ref.shape[0] // 8
    return ref.at[pl.ds(pl.multiple_of(k * p, 32 // jnp.dtype(ref.dtype).itemsize), p), :]


def _exchange_start(name, arrays):
    n = len(arrays)
    zones = [lax.empty((7, a.shape[0] // 8, a.shape[1]), a.dtype) for a in arrays]

    def body(*refs):
        srcs, lands = refs[:n], refs[n:2 * n]
        send, recv, token = refs[2 * n:3 * n], refs[3 * n:4 * n], refs[-1]
        x, y, c = _place()
        for a, (src, land) in enumerate(zip(srcs, lands)):
            for m in range(1, 8):
                px, py, pc = _peer(x, y, c, m)
                pltpu.make_async_remote_copy(
                    src_ref=_piece_rows(src, 4 * px + 2 * py + pc), dst_ref=land.at[m - 1], send_sem=send[a].at[m - 1],
                    recv_sem=recv[a].at[m - 1], device_id=(px, py, pc), device_id_type=MESH).start()
        token[...] = jnp.zeros_like(token)

    outs = pl.pallas_call(
        body, name=name, in_specs=[HBM] * (2 * n),
        out_specs=[SEM] * (2 * n) + [HBM] * (2 * n) + [pl.BlockSpec(memory_space=pltpu.VMEM)],
        out_shape=[pltpu.SemaphoreType.DMA((7,))] * (2 * n) + [pltpu.HBM(a.shape, a.dtype) for a in arrays + zones]
        + [jax.ShapeDtypeStruct((8, 128), F32)],
        input_output_aliases={a: 2 * n + a for a in range(2 * n)},
        compiler_params=SPLIT_COPY,
    )(*[_in_hbm(a) for a in arrays + zones])
    return outs[:n], outs[n:2 * n], outs[2 * n:3 * n], outs[3 * n:4 * n], outs[-1]


def _exchange_wait(name, started, after):
    send_sems, recv_sems, arrays, zones, _ = started
    n = len(arrays)

    def body(*refs):
        srcs, lands = refs[:n], refs[n:2 * n]
        send, recv = refs[2 * n:3 * n], refs[3 * n:4 * n]
        x, y, c = _place()
        for a, (src, land) in enumerate(zip(srcs, lands)):
            for m in range(1, 8):
                px, py, pc = _peer(x, y, c, m)
                copy = pltpu.make_async_remote_copy(
                    src_ref=_piece_rows(src, 4 * px + 2 * py + pc), dst_ref=land.at[m - 1], send_sem=send[a].at[m - 1],
                    recv_sem=recv[a].at[m - 1], device_id=(px, py, pc), device_id_type=MESH)
                copy.wait_send()
                copy.wait_recv()

    outs = pl.pallas_call(
        body, name=name, in_specs=[HBM] * (2 * n) + [SEM] * (2 * n) + [ANY], out_specs=[HBM] * (2 * n),
        out_shape=[pltpu.HBM(a.shape, a.dtype) for a in list(arrays) + list(zones)],
        input_output_aliases={a: a for a in range(2 * n)}, compiler_params=SPLIT_COPY,
    )(*arrays, *zones, *send_sems, *recv_sems, after)
    return outs[n:]


def _sum_pieces(name, partial, recv, place_arr, tile, layer, dest=None):
    p, cols = recv.shape[1:]
    steps = p // tile

    def body(place_ref, o_ref, r_ref, *rest):
        total = o_ref[...]
        for m in range(7):
            total = total + r_ref[m].astype(F32)
        rest[-1][...] = total

    return pl.pallas_call(
        body, name=name,
        grid_spec=pltpu.PrefetchScalarGridSpec(
            num_scalar_prefetch=1, grid=(steps,),
            in_specs=[pl.BlockSpec((tile, cols), lambda i, place: (place[0] * steps + i, 0)),
                      pl.BlockSpec((7, tile, cols), lambda i, place: (0, i, 0))] + ([] if dest is None else [ANY]),
            out_specs=pl.BlockSpec((None, tile, cols), lambda i, place: (layer, place[1] * steps + i, 0))),
        out_shape=jax.ShapeDtypeStruct((DEPTH, 2 * p, cols), F32),
        input_output_aliases={} if dest is None else {3: 0},
        compiler_params=_params(),
    )(place_arr, partial, recv, *(() if dest is None else (dest,)))


def _sum_small(partials, recvs, place_arr):
    n = len(partials)

    def body(place_ref, *refs):
        for o_ref, r_ref, out_ref in zip(refs[:n], refs[n:2 * n], refs[2 * n:]):
            total = o_ref[...]
            for m in range(7):
                total = total + r_ref[m]
            out_ref[...] = total

    piece = lambda a: pl.BlockSpec((a.shape[0] // 8, a.shape[1]), lambda i, place: (place[0], 0))
    return pl.pallas_call(
        body, name="sum_small",
        grid_spec=pltpu.PrefetchScalarGridSpec(
            num_scalar_prefetch=1, grid=(1,),
            in_specs=[piece(a) for a in partials] + [pl.BlockSpec(r.shape, lambda i, place: (0, 0, 0)) for r in recvs],
            out_specs=[piece(a) for a in partials]),
        out_shape=[jax.ShapeDtypeStruct(a.shape, F32) for a in partials],
        compiler_params=_params(),
    )(place_arr, *partials, *recvs)


def _share(name, bufs, parts, gathered=()):
    n, n_g = len(bufs), len(gathered)
    total = n + n_g

    def body(*refs):
        ins, outs = refs[:total], refs[total:2 * total]
        send_sems, recv_sems, send_g, recv_g = refs[2 * total:]
        x, y, c = _place()

        def half(ref, l, which):
            p = ref.shape[1] // 2
            return ref.at[l, pl.ds(pl.multiple_of(which * p, 8), p), :]

        def swap(k, which):
            a, l = parts[k]
            return pltpu.make_async_remote_copy(
                src_ref=half(ins[a], l, which), dst_ref=half(outs[a], l, which), send_sem=send_sems.at[k],
                recv_sem=recv_sems.at[k], device_id=(x, y, 1 - c), device_id_type=MESH)

        def spread(a, m, sender):
            k = 4 * sender[0] + 2 * sender[1] + sender[2]
            return pltpu.make_async_remote_copy(
                src_ref=_piece_rows(ins[n + a], k), dst_ref=_piece_rows(outs[n + a], k), send_sem=send_g.at[7 * a + m - 1],
                recv_sem=recv_g.at[7 * a + m - 1], device_id=_peer(x, y, c, m), device_id_type=MESH)

        for k in range(len(parts)):
            swap(k, c).start()
        for a in range(n_g):
            for m in range(1, 8):
                spread(a, m, (x, y, c)).start()
        for k in range(len(parts)):
            swap(k, c).wait_send()
            swap(k, 1 - c).wait_recv()
        for a in range(n_g):
            for m in range(1, 8):
                spread(a, m, (x, y, c)).wait_send()
                spread(a, m, _peer(x, y, c, m)).wait_recv()

    arrays = list(bufs) + list(gathered)
    return pl.pallas_call(
        body, name=name, in_specs=[ANY] * total, out_specs=[ANY] * total,
        out_shape=[jax.ShapeDtypeStruct(b.shape, F32) for b in arrays],
        input_output_aliases={a: a for a in range(total)},
        scratch_shapes=[pltpu.SemaphoreType.DMA((max(len(parts), 1),))] * 2
        + [pltpu.SemaphoreType.DMA((max(7 * n_g, 1),))] * 2,
    )(*arrays)


def _adamw(name, w, g, m, v, rows_per_step, first=0, count=None, dests=None):
    layers, rows, cols = w.shape
    count = layers if count is None else count

    def body(w_ref, g_ref, m_ref, v_ref, *rest):
        d_ref, nm_ref, nv_ref = rest[-3:]
        gt = g_ref[...]
        nm = ADAM_B1 * m_ref[...] + (1.0 - ADAM_B1) * gt
        nv = ADAM_B2 * v_ref[...] + (1.0 - ADAM_B2) * (gt * gt)
        m_hat = nm / (1.0 - ADAM_B1 ** ADAM_STEP)
        v_hat = nv / (1.0 - ADAM_B2 ** ADAM_STEP)
        d_ref[...] = -ADAM_LR * (m_hat / (jnp.sqrt(v_hat) + ADAM_EPS) + ADAM_WD * w_ref[...])
        nm_ref[...] = nm
        nv_ref[...] = nv

    spec = pl.BlockSpec((1, rows_per_step, cols), lambda l, i: (first + l, i, 0))
    shape = jax.ShapeDtypeStruct(w.shape, F32)
    dests = () if dests is None else tuple(dests)
    return pl.pallas_call(
        body, name=name, grid=(count, rows // rows_per_step),
        in_specs=[spec] * 4 + [ANY] * len(dests), out_specs=[spec] * 3, out_shape=[shape] * 3,
        input_output_aliases={4 + k: k for k in range(len(dests))},
        compiler_params=_params(("arbitrary", "arbitrary")),
    )(w, g, m, v, *dests)


def _sink_rows(flat):
    return jnp.zeros((8, 128), F32).at[0, 0:2 * N_HEADS].set(flat)


def _pack_misc(pool_scale, sinks, norm_pre, norm_post):
    return jnp.concatenate([pool_scale.reshape(8, 128), norm_pre.reshape(16, 128), norm_post.reshape(16, 128),
                            _sink_rows(sinks.reshape(-1)), jnp.zeros((MISC_ROWS - MISC_LOSS, 128), F32)], axis=0)


def _pack_misc_grads(dsc, dpre, dpost, dsink, loss):
    sinks = _sink_rows(jnp.concatenate([d[:, 0] for d in dsink]))
    return jnp.concatenate(dsc + dpre + dpost + [sinks, loss, jnp.zeros((MISC_ROWS - MISC_LOSS - 8, 128), F32)], axis=0)


def _unpack_misc(packed):
    return (packed[0:8].reshape(DEPTH, 512), packed[40, 0:2 * N_HEADS].reshape(DEPTH, N_HEADS),
            packed[8:24].reshape(DEPTH, D), packed[24:40].reshape(DEPTH, D))


def kernel(x, w_in, pool_w, pool_scale, attn_sinks, w_out, norm_pre, norm_post, loss_target, m_w_in, m_pool_w, m_pool_scale, m_attn_sinks, m_w_out, m_norm_pre, m_norm_post, v_w_in, v_pool_w, v_pool_scale, v_attn_sinks, v_w_out, v_norm_pre, v_norm_post):
    cx, cy, cc = _place()
    chip_arr = jnp.reshape(2 * cx + cy, (1,)).astype(jnp.int32)
    place_arr = jnp.stack([4 * cx + 2 * cy + cc, cc]).astype(jnp.int32)
    t = lambda a: jnp.transpose(a, (0, 2, 1))
    w_in_t = t(w_in)
    xs, target = x[0], loss_target[0]
    pool_w_b = pool_w.astype(BF16)
    bias, tri = _attention_tables()
    scale3 = pool_scale.reshape(DEPTH, 1, D_POOL)
    pre3 = norm_pre.reshape(DEPTH, 1, D)
    post3 = norm_post.reshape(DEPTH, 1, D)

    wi = _place_cast("place_w_in", w_in_t, chip_arr, 288)
    wo = _place_cast("place_w_out", w_out, chip_arr, 256)
    send, recv, bufs, token = _gather_start([wi[0], wo[0], wi[1], wo[1]], halved=(0, 2))

    saved = []
    after = token
    for l in range(DEPTH):
        w_in_l = _gather_wait(f"gather_wait_in{l}", bufs[2 * l], send[2 * l], recv[2 * l], after, halved=True)
        w_in_l = _forward_halves(f"forward_w_in{l}", w_in_l)
        pu, pg, q, kv, ag = _fwd_in(l, xs, pre3, w_in_l)
        cat = _fwd_mix(l, pu, pg, q, kv, ag, pool_w_b, scale3, attn_sinks, bias, tri)
        w_out_l = _gather_wait(f"gather_wait_out{l}", bufs[2 * l + 1], send[2 * l + 1], recv[2 * l + 1], cat)
        if l < DEPTH - 1:
            y, x_next = _fwd_out(l, cat, w_out_l, xs, post3)
        else:
            y, x_next, loss = _fwd_out(l, cat, w_out_l, xs, post3, target)
        saved.append((xs, pu, pg, q, kv, ag, cat, y, w_in_l, w_out_l))
        xs = after = x_next

    x_in, pu, pg, q, kv, ag, cat, y, w_in_l, w_out_l = saved[1]
    dcat, dw_out1, dw_out1_b, dg_post1 = _bwd_out(1, xs, y, post3, cat, w_out_l)
    ex1_out = _exchange_start("exchange_start_out1", [dw_out1_b])
    dproj, dpw, dsc1, dsink1 = _bwd_mix(1, pu, pg, q, kv, ag, dcat, pool_w_b, scale3, attn_sinks, bias, tri,
                                        deps=(ex1_out[4],))
    dw_in1, dw_in1_b = _bwd_in_dw(1, dproj, x_in, pre3)
    ex1_in = _exchange_start("exchange_start_in1", [dw_in1_b])
    dx, dg_pre1 = _bwd_in_dx(1, dproj, w_in_l, x_in, pre3, xs, deps=(ex1_in[4],))

    x_in, pu, pg, q, kv, ag, cat, y, w_in_l, w_out_l = saved[0]
    dcat, dw_out0, dw_out0_b, dg_post0 = _bwd_out(0, dx, y, post3, cat, w_out_l)
    ex0_out = _exchange_start("exchange_start_out0", [dw_out0_b])
    dproj, dpw, dsc0, dsink0 = _bwd_mix(0, pu, pg, q, kv, ag, dcat, pool_w_b, scale3, attn_sinks, bias, tri,
                                        deps=(ex0_out[4],), dpw_dest=dpw)
    (recv_out1,) = _exchange_wait("exchange_wait_out1", ex1_out, dproj)
    (recv_in1,) = _exchange_wait("exchange_wait_in1", ex1_in, recv_out1)
    g_in = _sum_pieces("sum_pieces_in1", dw_in1, recv_in1, place_arr, 96, 1)
    g_out = _sum_pieces("sum_pieces_out1", dw_out1, recv_out1, place_arr, 64, 1)
    dw_in0, dw_in0_b = _bwd_in_dw(0, dproj, x_in, pre3, deps=(g_in, g_out))
    ex0_in = _exchange_start("exchange_start_in0", [dw_in0_b])

    grad_x, dg_pre0 = _bwd_in_dx(0, dproj, w_in_l, x_in, pre3, dx, deps=(ex0_in[4],))
    small = [dpw.reshape(DEPTH * 4 * 128, 128),
             _pack_misc_grads([dsc0, dsc1], [dg_pre0, dg_pre1], [dg_post0, dg_post1], [dsink0, dsink1], loss)]
    ex_small = _exchange_start("exchange_start_small", small)
    (recv_out0,) = _exchange_wait("exchange_wait_out0", ex0_out, ex_small[4])
    g_out = _sum_pieces("sum_pieces_out0", dw_out0, recv_out0, place_arr, 64, 0, dest=g_out)
    g_in, grad_w_out = _share("share_a", [g_in, g_out], [(0, 1), (1, 0), (1, 1)])
    m_in_t, v_in_t = t(m_w_in), t(v_w_in)
    d_out, nm_out, nv_out = _adamw("adamw_w_out", w_out, grad_w_out, m_w_out, v_w_out, 256)
    upd_in = _adamw("adamw_w_in1", w_in_t, g_in, m_in_t, v_in_t, 288, first=1, count=1)

    (recv_in0,) = _exchange_wait("exchange_wait_in0", ex0_in, upd_in[0])
    recv_small = _exchange_wait("exchange_wait_small", ex_small, recv_in0)
    g_in = _sum_pieces("sum_pieces_in0", dw_in0, recv_in0, place_arr, 96, 0, dest=g_in)
    grad_w_in_t, g_pw, g_misc = _share("share_b", [g_in], [(0, 0)], _sum_small(small, recv_small, place_arr))
    d_in, nm_in, nv_in = _adamw("adamw_w_in0", w_in_t, grad_w_in_t, m_in_t, v_in_t, 288, first=0, count=1, dests=upd_in)
    flat = lambda a: a.reshape(1, DEPTH * 4 * 128, 128)
    d_pw, m_pw, v_pw = _adamw("adamw_pool_w", flat(pool_w), flat(g_pw), flat(m_pool_w), flat(v_pool_w), 1024)
    misc = lambda *a: _pack_misc(*a)[None]
    d_misc, m_misc, v_misc = _adamw(
        "adamw_misc", misc(pool_scale, attn_sinks, norm_pre, norm_post), g_misc[None],
        misc(m_pool_scale, m_attn_sinks, m_norm_pre, m_norm_post),
        misc(v_pool_scale, v_attn_sinks, v_norm_pre, v_norm_post), MISC_ROWS)

    g_sc, g_sk, g_pre, g_post = _unpack_misc(g_misc)
    d_sc, d_sk, d_pre, d_post = _unpack_misc(d_misc[0])
    m_sc, m_sk, m_pre, m_post = _unpack_misc(m_misc[0])
    v_sc, v_sk, v_pre, v_post = _unpack_misc(v_misc[0])
    like_pw = lambda a: a.reshape(pool_w.shape)
    return (g_misc[MISC_LOSS, 0], grad_x[None], t(grad_w_in_t), like_pw(g_pw), g_sc, g_sk, grad_w_out, g_pre, g_post,
            t(d_in), like_pw(d_pw), d_sc, d_sk, d_out, d_pre, d_post,
            t(nm_in), like_pw(m_pw), m_sc, m_sk, nm_out, m_pre, m_post,
            t(nv_in), like_pw(v_pw), v_sc, v_sk, nv_out, v_pre, v_post)
```

```python
import jax
import jax.numpy as jnp
from jax import lax
from jax.experimental import pallas as pl
from jax.experimental.pallas import tpu as pltpu

F32 = jnp.float32
BF16 = jnp.bfloat16

S = 2048
D = 1024
DEPTH = 2
D_POOL = 512
POOL_WINDOWS = (2, 4, 8, 16)
N_HEADS = 8
D_IN = 2304
N_SHARDS = 4
W_IN_SHARD = D_IN // N_SHARDS
W_OUT_SHARD = D // N_SHARDS
BLK = 128
NB = S // BLK
HALO = 16
EPS = 1e-6
NEG_INF = -1e30
C_PU, C_PG, C_Q, C_K, C_V, C_AG = 0, 512, 1024, 1536, 1664, 1792

ADAM_LR = 0.001
ADAM_B1 = 0.9
ADAM_B2 = 0.999
ADAM_EPS = 1e-08
ADAM_WD = 0.01
ADAM_STEP = 10

TM = 512
VMEM_LIMIT = 56 * 1024 * 1024

NT = (((1,), (1,)), ((), ()))
TN = (((0,), (0,)), ((), ()))

MESH = pl.DeviceIdType.MESH
ANY = pl.BlockSpec(memory_space=pl.ANY)

MISC_SCALE, MISC_PRE, MISC_POST, MISC_SINKS, MISC_LOSS = 0, 8, 24, 40, 56
MISC_ROWS = 64


def _params(sem=("arbitrary",)):
    return pltpu.CompilerParams(dimension_semantics=sem, vmem_limit_bytes=VMEM_LIMIT)


def _sigmoid(v):
    return 1.0 / (1.0 + jnp.exp(-v))


def _rows8(v):
    r, c = v.shape
    return v.reshape(r // 8, 8, c).sum(axis=0)


def _layer(l, *shape):
    zeros = (0,) * len(shape)
    return pl.BlockSpec((None,) + shape, lambda i: (l,) + zeros)


def _whole(shape):
    zeros = (0,) * len(shape)
    return pl.BlockSpec(shape, lambda i: zeros, pipeline_mode=pl.Buffered(1))


def _fwd_in(l, x, g_pre, w_in_t):
    def body(x_ref, g_ref, w_ref, pu_ref, pg_ref, q_ref, kv_ref, ag_ref):
        xt = x_ref[...]
        r = lax.rsqrt(jnp.mean(xt * xt, axis=-1, keepdims=True) + EPS)
        h = (xt * r * g_ref[...]).astype(BF16)

        def proj(lo, hi):
            return lax.dot_general(h, w_ref[lo:hi, :], NT, preferred_element_type=F32)

        pu_ref[...] = proj(C_PU, C_PG)
        pg_ref[...] = proj(C_PG, C_Q)
        q_ref[...] = proj(C_Q, C_K).astype(BF16)
        kv_ref[...] = proj(C_K, C_AG).astype(BF16)
        ag_ref[...] = proj(C_AG, D_IN)

    row = lambda w: pl.BlockSpec((TM, w), lambda i: (i, 0))
    return pl.pallas_call(
        body, name="fwd_in", grid=(S // TM,),
        in_specs=[row(D), _layer(l, 1, D), _whole((D_IN, D))],
        out_specs=[row(512), row(512), row(512), row(256), row(512)],
        out_shape=[jax.ShapeDtypeStruct((S, 512), F32), jax.ShapeDtypeStruct((S, 512), F32),
                   jax.ShapeDtypeStruct((S, 512), BF16), jax.ShapeDtypeStruct((S, 256), BF16),
                   jax.ShapeDtypeStruct((S, 512), F32)],
        compiler_params=_params(),
    )(x, g_pre, w_in_t)


LOG2E = 1.4426950408889634
SCORE_SCALE = 0.125 * LOG2E


def _attention_tables():
    qi = jnp.arange(BLK)[:, None]
    kj = jnp.arange(BLK)[None, :]
    dist = ((qi - kj) % BLK).astype(F32)
    slopes = jnp.exp2(-jnp.arange(1, N_HEADS + 1, dtype=F32))
    bias = -(slopes * LOG2E)[:, None, None] * dist[None]
    first = jnp.where(kj > qi, NEG_INF, bias)
    return jnp.stack([first, bias]), (kj <= qi).astype(BF16)


def _own_block_mask():
    return lax.broadcasted_iota(jnp.int32, (BLK, BLK), 1) <= lax.broadcasted_iota(jnp.int32, (BLK, BLK), 0)


def _merge(full, own):
    return jnp.where(own, full[:, BLK:], full[:, :BLK])


def _spread(v, tri):
    own = v * tri
    return jnp.concatenate([v - own, own], axis=1)


def _head_variants(cur, prev):
    both = jnp.concatenate([prev, cur], axis=0).astype(F32)
    swapped = pltpu.roll(both, 64, axis=1)
    low = lax.broadcasted_iota(jnp.int32, both.shape, 1) < 64
    zero = jnp.zeros_like(both)
    return ((jnp.where(low, both, zero).astype(BF16), jnp.where(low, zero, swapped).astype(BF16)),
            (jnp.where(low, swapped, zero).astype(BF16), jnp.where(low, zero, both).astype(BF16)))


def _head_of(hkv, t, half):
    return hkv * 4 + 2 * t + half


def _rows(v, t):
    return v[t * BLK:(t + 1) * BLK]


def _stack_tiles(ref, hkv, offset=0):
    lo = offset + 2 * hkv * 128
    return jnp.concatenate([ref[:, lo:lo + 128], ref[:, lo + 128:lo + 256]], axis=0)


def _scores(q2, k_var, own):
    s = {}
    for hkv in range(2):
        for half in range(2):
            full = lax.dot_general(q2[hkv], k_var[hkv][half], NT, preferred_element_type=F32)
            for t in range(2):
                s[hkv, t, half] = _merge(_rows(full, t), own)
    return s


def _softmax(s, bias, sink):
    s = s * SCORE_SCALE + bias
    sink2 = sink * LOG2E
    m = jnp.maximum(jnp.max(s, axis=-1, keepdims=True), sink2)
    p = jnp.exp2(s - m)
    e_sink = jnp.exp2(sink2 - m)
    inv = 1.0 / (jnp.sum(p, axis=-1, keepdims=True) + e_sink)
    return p * inv, e_sink * inv


def _spread_pair(v, hkv, half, tri):
    return jnp.concatenate([_spread(v[hkv, t, half].astype(BF16), tri) for t in range(2)], axis=0)


def _pool_block(ext_ref, i, g, w):
    lanes = slice(g * 128, (g + 1) * 128)
    u = ext_ref[HALO:HALO + BLK, lanes]
    acc = u
    for j in range(1, w):
        acc = acc + ext_ref[HALO - j:HALO - j + BLK, lanes]
    t = (i * BLK + lax.broadcasted_iota(jnp.int32, (BLK, 1), 0)).astype(F32)
    inv = 1.0 / jnp.minimum(t + 1.0, float(w))
    return acc * inv - u, inv


def _fwd_mix(l, pu, pg, q, kv, ag, pool_w, pool_scale, sinks, tables):
    bias, tri = tables

    def body(pu_ref, pup_ref, pg_ref, q_ref, kv_ref, kvp_ref, ag_ref, pw_ref, sc_ref, sink_ref, bias_ref, tri_ref,
             cat_ref, ext_ref):
        i = pl.program_id(0)
        ext_ref[0:HALO, :] = jnp.where(i > 0, pup_ref[...], 0.0)
        ext_ref[HALO:HALO + BLK, :] = pu_ref[...]
        for g, w in enumerate(POOL_WINDOWS):
            lanes = slice(g * 128, (g + 1) * 128)
            pooled, _ = _pool_block(ext_ref, i, g, w)
            mixed = jnp.dot(pooled.astype(BF16), pw_ref[g], preferred_element_type=F32)
            gate = pg_ref[:, lanes]
            cat_ref[:, lanes] = (mixed * sc_ref[:, lanes] * (gate * _sigmoid(gate))).astype(BF16)

        own = _own_block_mask()
        tri = tri_ref[...]
        k_var = _head_variants(kv_ref[:, 0:128], kvp_ref[:, 0:128])
        v_var = _head_variants(kv_ref[:, 128:256], kvp_ref[:, 128:256])
        s = _scores([_stack_tiles(q_ref, hkv) for hkv in range(2)], k_var, own)
        p = {}
        for (hkv, t, half), s_head in s.items():
            head = _head_of(hkv, t, half)
            p[hkv, t, half], _ = _softmax(s_head, bias_ref[head], sink_ref[l, head])
        for hkv in range(2):
            o2 = jnp.zeros((2 * BLK, 128), F32)
            for half in range(2):
                o2 = o2 + jnp.dot(_spread_pair(p, hkv, half, tri), v_var[hkv][half], preferred_element_type=F32)
            for t in range(2):
                lo = (2 * hkv + t) * 128
                gate = ag_ref[:, lo:lo + 128]
                cat_ref[:, D_POOL + lo:D_POOL + lo + 128] = (_rows(o2, t) * (gate * _sigmoid(gate))).astype(BF16)

    blk = lambda w: pl.BlockSpec((BLK, w), lambda i: (i, 0))
    prev = lambda w: pl.BlockSpec((BLK, w), lambda i: (jnp.maximum(i - 1, 0), 0))
    halo = pl.BlockSpec((HALO, 512), lambda i: (jnp.maximum(i * (BLK // HALO) - 1, 0), 0))
    return pl.pallas_call(
        body, name="fwd_mix", grid=(NB,),
        in_specs=[blk(512), halo, blk(512), blk(512), blk(256), prev(256), blk(512),
                  _layer(l, 4, 128, 128), _layer(l, 1, 512), pl.BlockSpec(memory_space=pltpu.SMEM),
                  pl.BlockSpec((None, N_HEADS, BLK, BLK), lambda i: (jnp.minimum(i, 1), 0, 0, 0)), _whole((BLK, BLK))],
        out_specs=blk(D),
        out_shape=jax.ShapeDtypeStruct((S, D), BF16),
        scratch_shapes=[pltpu.VMEM((HALO + BLK, 512), F32)],
        compiler_params=_params(),
    )(pu, pu, pg, q, kv, kv, ag, pool_w, pool_scale, sinks, bias, tri)


def _fwd_out(l, cat, w_out, x, g_post, target=None):
    last = target is not None
    n_steps = S // TM

    def body(*refs):
        if last:
            cat_ref, w_ref, x_ref, g_ref, t_ref, y_ref, dx_ref, loss_ref, acc_ref = refs
        else:
            cat_ref, w_ref, x_ref, g_ref, y_ref, xn_ref = refs
        y = jnp.dot(cat_ref[...], w_ref[...], preferred_element_type=F32)
        y_ref[...] = y
        r = lax.rsqrt(jnp.mean(y * y, axis=-1, keepdims=True) + EPS)
        xn = x_ref[...] + y * r * g_ref[...]
        if not last:
            xn_ref[...] = xn
            return
        step = pl.program_id(0)
        err = xn - t_ref[...]
        dx_ref[...] = err * (1.0 / D)

        @pl.when(step == 0)
        def _():
            acc_ref[...] = jnp.zeros_like(acc_ref)

        acc_ref[...] += _rows8(err * err)

        @pl.when(step == n_steps - 1)
        def _():
            loss_ref[...] = jnp.full((8, 128), (0.5 / D) * jnp.sum(acc_ref[...]), F32)

    row = lambda: pl.BlockSpec((TM, D), lambda i: (i, 0))
    act = jax.ShapeDtypeStruct((S, D), F32)
    in_specs = [row(), _whole((D, D)), row(), _layer(l, 1, D)]
    args = [cat, w_out, x, g_post]
    if last:
        return pl.pallas_call(
            body, name="fwd_out_loss", grid=(n_steps,),
            in_specs=in_specs + [row()],
            out_specs=[row(), row(), pl.BlockSpec((8, 128), lambda i: (0, 0))],
            out_shape=[act, act, jax.ShapeDtypeStruct((8, 128), F32)],
            scratch_shapes=[pltpu.VMEM((8, D), F32)],
            compiler_params=_params(),
        )(*args, target)
    return pl.pallas_call(
        body, name="fwd_out", grid=(n_steps,),
        in_specs=in_specs, out_specs=[row(), row()], out_shape=[act, act],
        compiler_params=_params(),
    )(*args)


def _store_lane_rows(ref, acc):
    total = jnp.sum(acc, axis=0, keepdims=True)
    for k in range(ref.shape[0]):
        ref[k:k + 1, :] = total[:, k * 128:(k + 1) * 128]


def _bwd_out(l, dxn, y, g_post, cat, w_out, deps=()):
    n_steps = S // TM

    def body(dz_ref, y_ref, g_ref, cat_ref, w_ref, *rest):
        dcat_ref, dw_ref, dwb_ref, dg_ref, acc_ref = rest[len(deps):]
        step = pl.program_id(0)

        @pl.when(step == 0)
        def _():
            dw_ref[...] = jnp.zeros_like(dw_ref)
            acc_ref[...] = jnp.zeros_like(acc_ref)

        y = y_ref[...]
        dz = dz_ref[...]
        r = lax.rsqrt(jnp.mean(y * y, axis=-1, keepdims=True) + EPS)
        a = dz * g_ref[...]
        dy = r * a - y * (r * r * r) * jnp.mean(a * y, axis=-1, keepdims=True)
        acc_ref[...] += _rows8(dz * (y * r))
        dyb = dy.astype(BF16)
        dcat_ref[...] = lax.dot_general(dyb, w_ref[...], NT, preferred_element_type=F32)
        dw_ref[...] += lax.dot_general(cat_ref[...], dyb, TN, preferred_element_type=F32)

        @pl.when(step == n_steps - 1)
        def _():
            _store_lane_rows(dg_ref, acc_ref[...])
            dwb_ref[...] = dw_ref[...].astype(BF16)

    row = lambda: pl.BlockSpec((TM, D), lambda i: (i, 0))
    full = _whole
    return pl.pallas_call(
        body, name="bwd_out", grid=(n_steps,),
        in_specs=[row(), row(), _layer(l, 1, D), row(), full((D, D))] + [ANY] * len(deps),
        out_specs=[row(), full((D, D)), full((D, D)), full((8, 128))],
        out_shape=[jax.ShapeDtypeStruct((S, D), F32), jax.ShapeDtypeStruct((D, D), F32),
                   jax.ShapeDtypeStruct((D, D), BF16), jax.ShapeDtypeStruct((8, 128), F32)],
        scratch_shapes=[pltpu.VMEM((8, D), F32)],
        compiler_params=_params(),
    )(dxn, y, g_post, cat, w_out, *deps)


def _bwd_mix(l, pu, pg, q, kv, ag, dcat, pool_w, pool_scale, sinks, tables, deps=(), dpw_dest=None):
    bias, tri = tables
    deps = tuple(deps) + (() if dpw_dest is None else (dpw_dest,))

    def body(pu_ref, pup_ref, pg_ref, q_ref, kv_ref, kvp_ref, ag_ref, dcat_ref, pw_ref, sc_ref, sink_ref, bias_ref,
             tri_ref, *rest):
        dproj_ref, dpw_ref, dsc_ref, dsink_ref, ext_ref, dext_ref, dkv_ref = rest[len(deps):]
        step = pl.program_id(0)
        i = NB - 1 - step

        @pl.when(step == 0)
        def _():
            dpw_ref[...] = jnp.zeros_like(dpw_ref)
            dsc_ref[...] = jnp.zeros_like(dsc_ref)
            dsink_ref[...] = jnp.zeros_like(dsink_ref)
            dext_ref[BLK:BLK + HALO, :] = jnp.zeros((HALO, 512), F32)
            dkv_ref[...] = jnp.zeros_like(dkv_ref)

        ext_ref[0:HALO, :] = jnp.where(i > 0, pup_ref[...], 0.0)
        ext_ref[HALO:HALO + BLK, :] = pu_ref[...]
        for g, w in enumerate(POOL_WINDOWS):
            lanes = slice(g * 128, (g + 1) * 128)
            pooled, inv = _pool_block(ext_ref, i, g, w)
            pooled_b = pooled.astype(BF16)
            mixed = jnp.dot(pooled_b, pw_ref[g], preferred_element_type=F32)
            scale = sc_ref[:, lanes]
            gate = pg_ref[:, lanes]
            sg = _sigmoid(gate)
            dpo = dcat_ref[:, lanes]
            dproj_ref[:, C_PG + g * 128:C_PG + (g + 1) * 128] = (
                dpo * (mixed * scale) * (sg * (1.0 + gate * (1.0 - sg)))).astype(BF16)
            dms = dpo * (gate * sg)
            dsc_ref[g:g + 1, :] += jnp.sum(dms * mixed, axis=0, keepdims=True)
            dmixed = (dms * scale).astype(BF16)
            dpw_ref[g] += lax.dot_general(pooled_b, dmixed, TN, preferred_element_type=F32)
            dpooled = lax.dot_general(dmixed, pw_ref[g], NT, preferred_element_type=F32)
            dext_ref[0:BLK, lanes] = dpooled * inv
            acc = dext_ref[0:BLK, lanes]
            for j in range(1, w):
                acc = acc + dext_ref[j:j + BLK, lanes]
            dproj_ref[:, C_PU + g * 128:C_PU + (g + 1) * 128] = (acc - dpooled).astype(BF16)
        dext_ref[BLK:BLK + HALO, :] = dext_ref[0:HALO, :]

        own = _own_block_mask()
        tri = tri_ref[...]
        k_var = _head_variants(kv_ref[:, 0:128], kvp_ref[:, 0:128])
        v_var = _head_variants(kv_ref[:, 128:256], kvp_ref[:, 128:256])
        q2 = [_stack_tiles(q_ref, hkv) for hkv in range(2)]
        s = _scores(q2, k_var, own)
        p, p_sink = {}, {}
        for key, s_head in s.items():
            head = _head_of(*key)
            p[key], p_sink[key] = _softmax(s_head, bias_ref[head], sink_ref[l, head])

        do2, p_b, dp = [], {}, {}
        for hkv in range(2):
            gate = _stack_tiles(ag_ref, hkv)
            sg = _sigmoid(gate)
            dca = _stack_tiles(dcat_ref, hkv, D_POOL)
            do2.append((dca * (gate * sg)).astype(BF16))
            o2 = jnp.zeros((2 * BLK, 128), F32)
            for half in range(2):
                p_b[hkv, half] = _spread_pair(p, hkv, half, tri)
                o2 = o2 + jnp.dot(p_b[hkv, half], v_var[hkv][half], preferred_element_type=F32)
                full = lax.dot_general(do2[hkv], v_var[hkv][half], NT, preferred_element_type=F32)
                for t in range(2):
                    dp[hkv, t, half] = _merge(_rows(full, t), own)
            dag = dca * o2 * (sg * (1.0 + gate * (1.0 - sg)))
            for t in range(2):
                lo = C_AG + (2 * hkv + t) * 128
                dproj_ref[:, lo:lo + 128] = _rows(dag, t).astype(BF16)

        ds = {}
        for key in p:
            delta = jnp.sum(p[key] * dp[key], axis=-1, keepdims=True)
            ds[key] = p[key] * (dp[key] - delta)
            head = _head_of(*key)
            dsink_ref[0:1, :] += jnp.where(lax.broadcasted_iota(jnp.int32, (1, 128), 1) == head,
                                           -jnp.sum(p_sink[key] * delta, axis=0, keepdims=True), 0.0)

        dk_acc = [[None, None], [None, None]]
        dv_acc = [[None, None], [None, None]]
        for hkv in range(2):
            dq2 = jnp.zeros((2 * BLK, 128), F32)
            for half in range(2):
                ds_b = _spread_pair(ds, hkv, half, tri)
                dq2 = dq2 + jnp.dot(ds_b, k_var[hkv][half], preferred_element_type=F32)
                dk_acc[hkv][half] = lax.dot_general(ds_b, q2[hkv], TN, preferred_element_type=F32)
                dv_acc[hkv][half] = lax.dot_general(p_b[hkv, half], do2[hkv], TN, preferred_element_type=F32)
            for t in range(2):
                lo = C_Q + (2 * hkv + t) * 128
                dproj_ref[:, lo:lo + 128] = (_rows(dq2, t) * 0.125).astype(BF16)

        low = lax.broadcasted_iota(jnp.int32, (2 * BLK, 128), 1) < 64

        def gather_heads(acc):
            return jnp.where(low, acc[0][0] + pltpu.roll(acc[0][1], 64, axis=1),
                             pltpu.roll(acc[1][0], 64, axis=1) + acc[1][1])

        dk = gather_heads(dk_acc) * 0.125
        dv = gather_heads(dv_acc)
        dproj_ref[:, C_K:C_V] = (dk[BLK:, :] + dkv_ref[:, 0:128]).astype(BF16)
        dproj_ref[:, C_V:C_AG] = (dv[BLK:, :] + dkv_ref[:, 128:256]).astype(BF16)
        dkv_ref[:, 0:128] = dk[:BLK, :]
        dkv_ref[:, 128:256] = dv[:BLK, :]

    rev = lambda w: pl.BlockSpec((BLK, w), lambda s: (NB - 1 - s, 0))
    prev = lambda w: pl.BlockSpec((BLK, w), lambda s: (jnp.maximum(NB - 2 - s, 0), 0))
    halo = pl.BlockSpec((HALO, 512), lambda s: (jnp.maximum((NB - 1 - s) * (BLK // HALO) - 1, 0), 0))
    return pl.pallas_call(
        body, name="bwd_mix", grid=(NB,),
        in_specs=[rev(512), halo, rev(512), rev(512), rev(256), prev(256), rev(512), rev(D),
                  _layer(l, 4, 128, 128), _layer(l, 1, 512), pl.BlockSpec(memory_space=pltpu.SMEM),
                  pl.BlockSpec((None, N_HEADS, BLK, BLK), lambda s: (jnp.minimum(NB - 1 - s, 1), 0, 0, 0)),
                  _whole((BLK, BLK))] + [ANY] * len(deps),
        out_specs=[rev(D_IN), _layer(l, 4, 128, 128),
                   pl.BlockSpec((4, 128), lambda s: (0, 0)), pl.BlockSpec((8, 128), lambda s: (0, 0))],
        out_shape=[jax.ShapeDtypeStruct((S, D_IN), BF16), jax.ShapeDtypeStruct((DEPTH, 4, 128, 128), F32),
                   jax.ShapeDtypeStruct((4, 128), F32), jax.ShapeDtypeStruct((8, 128), F32)],
        input_output_aliases={} if dpw_dest is None else {12 + len(deps): 1},
        scratch_shapes=[pltpu.VMEM((HALO + BLK, 512), F32), pltpu.VMEM((BLK + HALO, 512), F32),
                        pltpu.VMEM((BLK, 256), F32)],
        compiler_params=_params(),
    )(pu, pu, pg, q, kv, kv, ag, dcat, pool_w, pool_scale, sinks, bias, tri, *deps)


def _bwd_in_dw(l, dproj, x, g_pre, deps=()):
    n_steps = S // TM

    def body(dp_ref, x_ref, g_ref, *rest):
        dw_ref, dwb_ref = rest[len(deps):]
        step = pl.program_id(0)

        @pl.when(step == 0)
        def _():
            dw_ref[...] = jnp.zeros_like(dw_ref)

        xt = x_ref[...]
        r = lax.rsqrt(jnp.mean(xt * xt, axis=-1, keepdims=True) + EPS)
        h = (xt * r * g_ref[...]).astype(BF16)
        dw_ref[...] += lax.dot_general(dp_ref[...], h, TN, preferred_element_type=F32)

        @pl.when(step == n_steps - 1)
        def _():
            dwb_ref[...] = dw_ref[...].astype(BF16)

    row = lambda w: pl.BlockSpec((TM, w), lambda i: (i, 0))
    full = _whole
    return pl.pallas_call(
        body, name="bwd_in_dw", grid=(n_steps,),
        in_specs=[row(D_IN), row(D), _layer(l, 1, D)] + [ANY] * len(deps),
        out_specs=[full((D_IN, D)), full((D_IN, D))],
        out_shape=[jax.ShapeDtypeStruct((D_IN, D), F32), jax.ShapeDtypeStruct((D_IN, D), BF16)],
        compiler_params=_params(),
    )(dproj, x, g_pre, *deps)


def _bwd_in_dx(l, dproj, w_in_t, x, g_pre, dres, deps=()):
    n_steps = S // TM

    def body(dp_ref, w_ref, x_ref, g_ref, dres_ref, *rest):
        dx_ref, dg_ref, acc_ref = rest[len(deps):]
        step = pl.program_id(0)

        @pl.when(step == 0)
        def _():
            acc_ref[...] = jnp.zeros_like(acc_ref)

        dh = jnp.dot(dp_ref[...], w_ref[...], preferred_element_type=F32)
        xt = x_ref[...]
        r = lax.rsqrt(jnp.mean(xt * xt, axis=-1, keepdims=True) + EPS)
        xn = xt * r
        acc_ref[...] += _rows8(dh * xn)
        a = dh * g_ref[...]
        dx_ref[...] = dres_ref[...] + (r * a - xt * (r * r * r) * jnp.mean(a * xt, axis=-1, keepdims=True))

        @pl.when(step == n_steps - 1)
        def _():
            _store_lane_rows(dg_ref, acc_ref[...])

    row = lambda w: pl.BlockSpec((TM, w), lambda i: (i, 0))
    full = _whole
    return pl.pallas_call(
        body, name="bwd_in_dx", grid=(n_steps,),
        in_specs=[row(D_IN), full((D_IN, D)), row(D), _layer(l, 1, D), row(D)] + [ANY] * len(deps),
        out_specs=[row(D), full((8, 128))],
        out_shape=[jax.ShapeDtypeStruct((S, D), F32), jax.ShapeDtypeStruct((8, 128), F32)],
        scratch_shapes=[pltpu.VMEM((8, D), F32)],
        compiler_params=_params(),
    )(dproj, w_in_t, x, g_pre, dres, *deps)


HBM =pl.BlockSpec(memory_space=pltpu.HBM)
SEM = pl.BlockSpec(memory_space=pltpu.SEMAPHORE)
SPLIT_COPY = pltpu.CompilerParams(has_side_effects=pltpu.SideEffectType.DATAFLOW_SIDE_EFFECTING)


def _in_hbm(a):
    return pltpu.with_memory_space_constraint(a, pltpu.HBM)

def _place():
    return lax.axis_index("x"), lax.axis_index("y"), lax.axis_index("c")


def _other_chips(x, y):
    return [(1 - x, y), (x, 1 - y), (1 - x, 1 - y)]


def _peer(x, y, c, m):
    return (x ^ (m >> 2), y ^ ((m >> 1) & 1), c ^ (m & 1))


def _place_cast(name, src, chip_arr, tile):
    _, n, cols = src.shape
    steps = n // tile

    def body(chip_ref, s0_ref, s1_ref, o0_ref, o1_ref):
        o0_ref[...] = s0_ref[...].astype(BF16)
        o1_ref[...] = s1_ref[...].astype(BF16)

    return pl.pallas_call(
        body, name=name,
        grid_spec=pltpu.PrefetchScalarGridSpec(
            num_scalar_prefetch=1, grid=(steps,),
            in_specs=[pl.BlockSpec((None, tile, cols), lambda i, chip: (0, i, 0)),
                      pl.BlockSpec((None, tile, cols), lambda i, chip: (1, i, 0))],
            out_specs=[pl.BlockSpec((tile, cols), lambda i, chip: (chip[0] * steps + i, 0))] * 2),
        out_shape=[jax.ShapeDtypeStruct((N_SHARDS * n, cols), BF16)] * 2,
        compiler_params=_params(),
    )(chip_arr, src, src)


def _chip_rows(ref, chip, half=None):
    n = ref.shape[0] // N_SHARDS
    if half is None:
        return ref.at[pl.ds(pl.multiple_of(chip * n, 16), n), :]
    return ref.at[pl.ds(pl.multiple_of(chip * n + half * (n // 2), 16), n // 2), :]


def _gather_start(bufs, halved):
    n = len(bufs)

    def body(*refs):
        ins, send, recv, token = refs[:n], refs[n:2 * n], refs[2 * n:3 * n], refs[-1]
        x, y, c = _place()
        for a, buf in enumerate(ins):
            own = _chip_rows(buf, 2 * x + y, c if a in halved else None)
            for j, chip in enumerate(_other_chips(x, y)):
                pltpu.make_async_remote_copy(src_ref=own, dst_ref=own, send_sem=send[a].at[j], recv_sem=recv[a].at[j],
                                             device_id=(*chip, c), device_id_type=MESH).start()
        token[...] = jnp.zeros_like(token)

    outs = pl.pallas_call(
        body, name="gather_start", in_specs=[HBM] * n,
        out_specs=[SEM] * (2 * n) + [HBM] * n + [pl.BlockSpec(memory_space=pltpu.VMEM)],
        out_shape=[pltpu.SemaphoreType.DMA((3,))] * (2 * n) + [pltpu.HBM(b.shape, b.dtype) for b in bufs]
        + [jax.ShapeDtypeStruct((8, 128), F32)],
        input_output_aliases={a: 2 * n + a for a in range(n)},
        compiler_params=SPLIT_COPY,
    )(*[_in_hbm(b) for b in bufs])
    return outs[:n], outs[n:2 * n], outs[2 * n:3 * n], outs[-1]


def _gather_wait(name, buf, send_sem, recv_sem, after, halved=False):
    def body(buf_ref, send_ref, recv_ref, after_ref, out_ref):
        x, y, c = _place()
        half = c if halved else None
        own = _chip_rows(buf_ref, 2 * x + y, half)
        for j, chip in enumerate(_other_chips(x, y)):
            copy = pltpu.make_async_remote_copy(src_ref=own, dst_ref=_chip_rows(buf_ref, 2 * chip[0] + chip[1], half),
                                                send_sem=send_ref.at[j], recv_sem=recv_ref.at[j],
                                                device_id=(*chip, c), device_id_type=MESH)
            copy.wait_send()
            copy.wait_recv()

    return pl.pallas_call(
        body, name=name, in_specs=[HBM, SEM, SEM, ANY], out_specs=HBM, out_shape=pltpu.HBM(buf.shape, buf.dtype),
        input_output_aliases={0: 0}, compiler_params=SPLIT_COPY,
    )(buf, send_sem, recv_sem, after)


def _forward_halves(name, buf):
    def body(in_ref, out_ref, send_sems, recv_sems):
        x, y, c = _place()

        def copy(j, chip, half):
            rows = 2 * chip[0] + chip[1]
            return pltpu.make_async_remote_copy(
                src_ref=_chip_rows(in_ref, rows, half), dst_ref=_chip_rows(out_ref, rows, half), send_sem=send_sems.at[j],
                recv_sem=recv_sems.at[j], device_id=(x, y, 1 - c), device_id_type=MESH)

        chips = _other_chips(x, y)
        for j, chip in enumerate(chips):
            copy(j, chip, c).start()
        for j, chip in enumerate(chips):
            copy(j, chip, c).wait_send()
            copy(j, chip, 1 - c).wait_recv()

    return pl.pallas_call(
        body, name=name, in_specs=[ANY], out_specs=ANY, out_shape=jax.ShapeDtypeStruct(buf.shape, buf.dtype),
        input_output_aliases={0: 0},
        scratch_shapes=[pltpu.SemaphoreType.DMA((3,))] * 2,
    )(buf)


def _piece_rows(ref, k):
    p = ref.shape[0] // 8
    return ref.at[pl.ds(pl.multiple_of(k * p, 32 // jnp.dtype(ref.dtype).itemsize), p), :]


def _exchange_start(name, arrays):
    n = len(arrays)
    zones = [lax.empty((7, a.shape[0] // 8, a.shape[1]), a.dtype) for a in arrays]

    def body(*refs):
        srcs, lands = refs[:n], refs[n:2 * n]
        send, recv, token = refs[2 * n:3 * n], refs[3 * n:4 * n], refs[-1]
        x, y, c = _place()
        for a, (src, land) in enumerate(zip(srcs, lands)):
            for m in range(1, 8):
                px, py, pc = _peer(x, y, c, m)
                pltpu.make_async_remote_copy(
                    src_ref=_piece_rows(src, 4 * px + 2 * py + pc), dst_ref=land.at[m - 1], send_sem=send[a].at[m - 1],
                    recv_sem=recv[a].at[m - 1], device_id=(px, py, pc), device_id_type=MESH).start()
        token[...] = jnp.zeros_like(token)

    outs = pl.pallas_call(
        body, name=name, in_specs=[HBM] * (2 * n),
        out_specs=[SEM] * (2 * n) + [HBM] * (2 * n) + [pl.BlockSpec(memory_space=pltpu.VMEM)],
        out_shape=[pltpu.SemaphoreType.DMA((7,))] * (2 * n) + [pltpu.HBM(a.shape, a.dtype) for a in arrays + zones]
        + [jax.ShapeDtypeStruct((8, 128), F32)],
        input_output_aliases={a: 2 * n + a for a in range(2 * n)},
        compiler_params=SPLIT_COPY,
    )(*[_in_hbm(a) for a in arrays + zones])
    return outs[:n], outs[n:2 * n], outs[2 * n:3 * n], outs[3 * n:4 * n], outs[-1]


def _exchange_wait(name, started, after):
    send_sems, recv_sems, arrays, zones, _ = started
    n = len(arrays)

    def body(*refs):
        srcs, lands = refs[:n], refs[n:2 * n]
        send, recv = refs[2 * n:3 * n], refs[3 * n:4 * n]
        x, y, c = _place()
        for a, (src, land) in enumerate(zip(srcs, lands)):
            for m in range(1, 8):
                px, py, pc = _peer(x, y, c, m)
                copy = pltpu.make_async_remote_copy(
                    src_ref=_piece_rows(src, 4 * px + 2 * py + pc), dst_ref=land.at[m - 1], send_sem=send[a].at[m - 1],
                    recv_sem=recv[a].at[m - 1], device_id=(px, py, pc), device_id_type=MESH)
                copy.wait_send()
                copy.wait_recv()

    outs = pl.pallas_call(
        body, name=name, in_specs=[HBM] * (2 * n) + [SEM] * (2 * n) + [ANY], out_specs=[HBM] * (2 * n),
        out_shape=[pltpu.HBM(a.shape, a.dtype) for a in list(arrays) + list(zones)],
        input_output_aliases={a: a for a in range(2 * n)}, compiler_params=SPLIT_COPY,
    )(*arrays, *zones, *send_sems, *recv_sems, after)
    return outs[n:]


def _sum_pieces(name, partial, recv, place_arr, tile, layer, dest=None):
    p, cols = recv.shape[1:]
    steps = p // tile

    def body(place_ref, o_ref, r_ref, *rest):
        total = o_ref[...]
        for m in range(7):
            total = total + r_ref[m].astype(F32)
        rest[-1][...] = total

    return pl.pallas_call(
        body, name=name,
        grid_spec=pltpu.PrefetchScalarGridSpec(
            num_scalar_prefetch=1, grid=(steps,),
            in_specs=[pl.BlockSpec((tile, cols), lambda i, place: (place[0] * steps + i, 0)),
                      pl.BlockSpec((7, tile, cols), lambda i, place: (0, i, 0))] + ([] if dest is None else [ANY]),
            out_specs=pl.BlockSpec((None, tile, cols), lambda i, place: (layer, place[1] * steps + i, 0))),
        out_shape=jax.ShapeDtypeStruct((DEPTH, 2 * p, cols), F32),
        input_output_aliases={} if dest is None else {3: 0},
        compiler_params=_params(),
    )(place_arr, partial, recv, *(() if dest is None else (dest,)))


def _sum_small(partials, recvs, place_arr):
    n = len(partials)

    def body(place_ref, *refs):
        for o_ref, r_ref, out_ref in zip(refs[:n], refs[n:2 * n], refs[2 * n:]):
            total = o_ref[...]
            for m in range(7):
                total = total + r_ref[m]
            out_ref[...] = total

    piece = lambda a: pl.BlockSpec((a.shape[0] // 8, a.shape[1]), lambda i, place: (place[0], 0))
    return pl.pallas_call(
        body, name="sum_small",
        grid_spec=pltpu.PrefetchScalarGridSpec(
            num_scalar_prefetch=1, grid=(1,),
            in_specs=[piece(a) for a in partials] + [pl.BlockSpec(r.shape, lambda i, place: (0, 0, 0)) for r in recvs],
            out_specs=[piece(a) for a in partials]),
        out_shape=[jax.ShapeDtypeStruct(a.shape, F32) for a in partials],
        compiler_params=_params(),
    )(place_arr, *partials, *recvs)


def _share(name, bufs, parts, gathered=()):
    n, n_g = len(bufs), len(gathered)
    total = n + n_g

    def body(*refs):
        ins, outs = refs[:total], refs[total:2 * total]
        send_sems, recv_sems, send_g, recv_g = refs[2 * total:]
        x, y, c = _place()

        def half(ref, l, which):
            p = ref.shape[1] // 2
            return ref.at[l, pl.ds(pl.multiple_of(which * p, 8), p), :]

        def swap(k, which):
            a, l = parts[k]
            return pltpu.make_async_remote_copy(
                src_ref=half(ins[a], l, which), dst_ref=half(outs[a], l, which), send_sem=send_sems.at[k],
                recv_sem=recv_sems.at[k], device_id=(x, y, 1 - c), device_id_type=MESH)

        def spread(a, m, sender):
            k = 4 * sender[0] + 2 * sender[1] + sender[2]
            return pltpu.make_async_remote_copy(
                src_ref=_piece_rows(ins[n + a], k), dst_ref=_piece_rows(outs[n + a], k), send_sem=send_g.at[7 * a + m - 1],
                recv_sem=recv_g.at[7 * a + m - 1], device_id=_peer(x, y, c, m), device_id_type=MESH)

        for k in range(len(parts)):
            swap(k, c).start()
        for a in range(n_g):
            for m in range(1, 8):
                spread(a, m, (x, y, c)).start()
        for k in range(len(parts)):
            swap(k, c).wait_send()
            swap(k, 1 - c).wait_recv()
        for a in range(n_g):
            for m in range(1, 8):
                spread(a, m, (x, y, c)).wait_send()
                spread(a, m, _peer(x, y, c, m)).wait_recv()

    arrays = list(bufs) + list(gathered)
    return pl.pallas_call(
        body, name=name, in_specs=[ANY] * total, out_specs=[ANY] * total,
        out_shape=[jax.ShapeDtypeStruct(b.shape, F32) for b in arrays],
        input_output_aliases={a: a for a in range(total)},
        scratch_shapes=[pltpu.SemaphoreType.DMA((max(len(parts), 1),))] * 2
        + [pltpu.SemaphoreType.DMA((max(7 * n_g, 1),))] * 2,
    )(*arrays)


def _adamw_math(w, g, m, v):
    nm = ADAM_B1 * m + (1.0 - ADAM_B1) * g
    nv = ADAM_B2 * v + (1.0 - ADAM_B2) * (g * g)
    m_hat = nm / (1.0 - ADAM_B1 ** ADAM_STEP)
    v_hat = nv / (1.0 - ADAM_B2 ** ADAM_STEP)
    return -ADAM_LR * (m_hat / (jnp.sqrt(v_hat) + ADAM_EPS) + ADAM_WD * w), nm, nv


def _adamw(name, w, g, m, v, rows_per_step, first=0, count=None, dests=None, deps=()):
    layers, rows, cols = w.shape
    count = layers if count is None else count

    def body(w_ref, g_ref, m_ref, v_ref, *rest):
        d_ref, nm_ref, nv_ref, g_out_ref = rest[-4:]
        d_ref[...], nm_ref[...], nv_ref[...] = _adamw_math(w_ref[...], g_ref[...], m_ref[...], v_ref[...])
        g_out_ref[...] = g_ref[...]

    spec = pl.BlockSpec((1, rows_per_step, cols), lambda l, i: (first + l, i, 0))
    shape = jax.ShapeDtypeStruct(w.shape, F32)
    dests = () if dests is None else tuple(dests)
    return pl.pallas_call(
        body, name=name, grid=(count, rows // rows_per_step),
        in_specs=[spec] * 4 + [ANY] * (len(dests) + len(deps)), out_specs=[spec] * 4, out_shape=[shape] * 4,
        input_output_aliases={4 + k: k for k in range(len(dests))},
        compiler_params=_params(("arbitrary", "arbitrary")),
    )(w, g, m, v, *dests, *deps)


def _pack_misc(pool_scale, sinks, norm_pre, norm_post):
    sink_rows = jnp.zeros((DEPTH, 8, 128), F32).at[:, 0, 0:N_HEADS].set(sinks).reshape(2 * 8, 128)
    return jnp.concatenate([pool_scale.reshape(8, 128), norm_pre.reshape(16, 128), norm_post.reshape(16, 128),
                            sink_rows, jnp.zeros((8, 128), F32)], axis=0)


def _adamw_misc(w, g, m, v):
    def body(w_ref, g_ref, m_ref, v_ref, *rest):
        outs, (d_ref, nm_ref, nv_ref) = rest[:17], rest[17:]
        d_ref[...], nm_ref[...], nv_ref[...] = _adamw_math(w_ref[...], g_ref[...], m_ref[...], v_ref[...])
        for k, src in enumerate([g_ref, d_ref, nm_ref, nv_ref]):
            scale, sinks, pre, post = outs[4 * k:4 * k + 4]
            for l in range(DEPTH):
                for j in range(4):
                    scale[l:l + 1, j * 128:(j + 1) * 128] = src[MISC_SCALE + 4 * l + j:MISC_SCALE + 4 * l + j + 1, :]
                for j in range(8):
                    pre[l:l + 1, j * 128:(j + 1) * 128] = src[MISC_PRE + 8 * l + j:MISC_PRE + 8 * l + j + 1, :]
                    post[l:l + 1, j * 128:(j + 1) * 128] = src[MISC_POST + 8 * l + j:MISC_POST + 8 * l + j + 1, :]
                sinks[l:l + 1, :] = src[MISC_SINKS + 8 * l:MISC_SINKS + 8 * l + 1, 0:N_HEADS]
        outs[16][...] = g_ref[MISC_LOSS:MISC_LOSS + 1, 0:1]

    vmem = pl.BlockSpec(memory_space=pltpu.VMEM)
    shapes = [(DEPTH, D_POOL), (DEPTH, N_HEADS), (DEPTH, D), (DEPTH, D)] * 4 + [(1, 1)]
    return pl.pallas_call(
        body, name="adamw_misc", in_specs=[vmem] * 4, out_specs=[vmem] * 17,
        out_shape=[jax.ShapeDtypeStruct(s, F32) for s in shapes],
        scratch_shapes=[pltpu.VMEM((MISC_ROWS, 128), F32)] * 3,
    )(w, g, m, v)


def kernel(x, w_in, pool_w, pool_scale, attn_sinks, w_out, norm_pre, norm_post, loss_target, m_w_in, m_pool_w, m_pool_scale, m_attn_sinks, m_w_out, m_norm_pre, m_norm_post, v_w_in, v_pool_w, v_pool_scale, v_attn_sinks, v_w_out, v_norm_pre, v_norm_post):
    cx, cy, cc = _place()
    chip_arr = jnp.reshape(2 * cx + cy, (1,)).astype(jnp.int32)
    place_arr = jnp.stack([4 * cx + 2 * cy + cc, cc]).astype(jnp.int32)
    t = lambda a: jnp.transpose(a, (0, 2, 1))
    w_in_t = t(w_in)
    xs, target = x[0], loss_target[0]
    pool_w_b = pool_w.astype(BF16)
    tables = _attention_tables()
    scale3 = pool_scale.reshape(DEPTH, 1, D_POOL)
    pre3 = norm_pre.reshape(DEPTH, 1, D)
    post3 = norm_post.reshape(DEPTH, 1, D)

    wi = _place_cast("place_w_in", w_in_t, chip_arr, 288)
    wo = _place_cast("place_w_out", w_out, chip_arr, 256)
    send, recv, bufs, token = _gather_start([wi[0], wo[0], wi[1], wo[1]], halved=(0, 2))

    saved = []
    after = token
    for l in range(DEPTH):
        w_in_l = _gather_wait(f"gather_wait_in{l}", bufs[2 * l], send[2 * l], recv[2 * l], after, halved=True)
        w_in_l = _forward_halves(f"forward_w_in{l}", w_in_l)
        pu, pg, q, kv, ag = _fwd_in(l, xs, pre3, w_in_l)
        cat = _fwd_mix(l, pu, pg, q, kv, ag, pool_w_b, scale3, attn_sinks, tables)
        w_out_l = _gather_wait(f"gather_wait_out{l}", bufs[2 * l + 1], send[2 * l + 1], recv[2 * l + 1], cat)
        if l < DEPTH - 1:
            y, x_next = _fwd_out(l, cat, w_out_l, xs, post3)
        else:
            y, x_next, loss = _fwd_out(l, cat, w_out_l, xs, post3, target)
        saved.append((xs, pu, pg, q, kv, ag, cat, y, w_in_l, w_out_l))
        xs = after = x_next

    x_in, pu, pg, q, kv, ag, cat, y, w_in_l, w_out_l = saved[1]
    dcat, dw_out1, dw_out1_b, dg_post1 = _bwd_out(1, xs, y, post3, cat, w_out_l)
    ex1_out = _exchange_start("exchange_start_out1", [dw_out1_b])
    dproj, dpw, dsc1, dsink1 = _bwd_mix(1, pu, pg, q, kv, ag, dcat, pool_w_b, scale3, attn_sinks, tables,
                                        deps=(ex1_out[4],))
    dw_in1, dw_in1_b = _bwd_in_dw(1, dproj, x_in, pre3)
    ex1_in = _exchange_start("exchange_start_in1", [dw_in1_b])
    dx, dg_pre1 = _bwd_in_dx(1, dproj, w_in_l, x_in, pre3, xs, deps=(ex1_in[4],))

    x_in, pu, pg, q, kv, ag, cat, y, w_in_l, w_out_l = saved[0]
    dcat, dw_out0, dw_out0_b, dg_post0 = _bwd_out(0, dx, y, post3, cat, w_out_l)
    ex0_out = _exchange_start("exchange_start_out0", [dw_out0_b])
    dproj, dpw, dsc0, dsink0 = _bwd_mix(0, pu, pg, q, kv, ag, dcat, pool_w_b, scale3, attn_sinks, tables,
                                        deps=(ex0_out[4],), dpw_dest=dpw)
    (recv_out1,) = _exchange_wait("exchange_wait_out1", ex1_out, dproj)
    (recv_in1,) = _exchange_wait("exchange_wait_in1", ex1_in, recv_out1)
    g_in = _sum_pieces("sum_pieces_in1", dw_in1, recv_in1, place_arr, 96, 1)
    g_out = _sum_pieces("sum_pieces_out1", dw_out1, recv_out1, place_arr, 64, 1)
    dw_in0, dw_in0_b = _bwd_in_dw(0, dproj, x_in, pre3, deps=(g_in, g_out))
    ex0_in = _exchange_start("exchange_start_in0", [dw_in0_b])

    grad_x, dg_pre0 = _bwd_in_dx(0, dproj, w_in_l, x_in, pre3, dx, deps=(ex0_in[4],))
    small = [dpw.reshape(DEPTH * 4 * 128, 128),
             jnp.concatenate([dsc0, dsc1, dg_pre0, dg_pre1, dg_post0, dg_post1, dsink0, dsink1, loss], axis=0)]
    ex_small = _exchange_start("exchange_start_small", small)
    (recv_out0,) = _exchange_wait("exchange_wait_out0", ex0_out, ex_small[4])
    g_out = _sum_pieces("sum_pieces_out0", dw_out0, recv_out0, place_arr, 64, 0, dest=g_out)
    g_in, g_out = _share("share_a", [g_in, g_out], [(0, 1), (1, 0), (1, 1)])
    m_in_t, v_in_t = t(m_w_in), t(v_w_in)
    d_out, nm_out, nv_out, grad_w_out = _adamw("adamw_w_out", w_out, g_out, m_w_out, v_w_out, 256)
    upd_in = _adamw("adamw_w_in1", w_in_t, g_in, m_in_t, v_in_t, 288, first=1, count=1, deps=(d_out,))

    (recv_in0,) = _exchange_wait("exchange_wait_in0", ex0_in, upd_in[0])
    recv_small = _exchange_wait("exchange_wait_small", ex_small, recv_in0)
    g_in = _sum_pieces("sum_pieces_in0", dw_in0, recv_in0, place_arr, 96, 0, dest=g_in)
    g_in, g_pw, g_misc = _share("share_b", [g_in], [(0, 0)], _sum_small(small, recv_small, place_arr))
    d_in, nm_in, nv_in, grad_w_in_t = _adamw("adamw_w_in0", w_in_t, g_in, m_in_t, v_in_t, 288, first=0, count=1,
                                             dests=upd_in)
    flat = lambda a: a.reshape(1, DEPTH * 4 * 128, 128)
    pw = _adamw("adamw_pool_w", flat(pool_w), flat(g_pw), flat(m_pool_w), flat(v_pool_w), 1024)
    d_pw, m_pw, v_pw, g_pw = [a.reshape(pool_w.shape) for a in pw]
    misc = _adamw_misc(_pack_misc(pool_scale, attn_sinks, norm_pre, norm_post), g_misc,
                       _pack_misc(m_pool_scale, m_attn_sinks, m_norm_pre, m_norm_post),
                       _pack_misc(v_pool_scale, v_attn_sinks, v_norm_pre, v_norm_post))
    (g_sc, g_sk, g_pre, g_post, d_sc, d_sk, d_pre, d_post,
     m_sc, m_sk, m_pre, m_post, v_sc, v_sk, v_pre, v_post, loss_sum) = misc
    return (loss_sum[0, 0], grad_x[None], t(grad_w_in_t), g_pw, g_sc, g_sk, grad_w_out, g_pre, g_post,
            t(d_in), d_pw, d_sc, d_sk, d_out, d_pre, d_post,
            t(nm_in), m_pw, m_sc, m_sk, nm_out, m_pre, m_post,
            t(nv_in), v_pw, v_sc, v_sk, nv_out, v_pre, v_post)
```

```python
import jax
import jax.numpy as jnp
from jax import lax
from jax.experimental import pallas as pl
from jax.experimental.pallas import tpu as pltpu

F32 = jnp.float32
BF16 = jnp.bfloat16

S = 2048
D = 1024
DEPTH = 2
D_POOL = 512
POOL_WINDOWS = (2, 4, 8, 16)
N_HEADS = 8
D_IN = 2304
N_SHARDS = 4
W_IN_SHARD = D_IN // N_SHARDS
W_OUT_SHARD = D // N_SHARDS
BLK = 128
NB = S // BLK
HALO = 16
PAD = 8
EPS = 1e-6
NEG_INF = -1e30
C_PU, C_PG, C_Q, C_K, C_V, C_AG = 0, 512, 1024, 1536, 1664, 1792

ADAM_LR = 0.001
ADAM_B1 = 0.9
ADAM_B2 = 0.999
ADAM_EPS = 1e-08
ADAM_WD = 0.01
ADAM_STEP = 10

TM = 512
VMEM_LIMIT = 56 * 1024 * 1024

NT = (((1,), (1,)), ((), ()))
TN = (((0,), (0,)), ((), ()))

MESH = pl.DeviceIdType.MESH
ANY = pl.BlockSpec(memory_space=pl.ANY)

MISC_SCALE, MISC_PRE, MISC_POST, MISC_SINKS, MISC_LOSS = 0, 8, 24, 40, 56
MISC_ROWS = 64


def _params(sem=("arbitrary",)):
    return pltpu.CompilerParams(dimension_semantics=sem, vmem_limit_bytes=VMEM_LIMIT)


def _sigmoid(v):
    return 1.0 / (1.0 + jnp.exp(-v))


def _rows8(v):
    r, c = v.shape
    return v.reshape(r // 8, 8, c).sum(axis=0)


def _layer(l, *shape):
    zeros = (0,) * len(shape)
    return pl.BlockSpec((None,) + shape, lambda i: (l,) + zeros)


def _whole(shape):
    zeros = (0,) * len(shape)
    return pl.BlockSpec(shape, lambda i: zeros, pipeline_mode=pl.Buffered(1))


def _fwd_in(l, x, g_pre, w_in_t):
    def body(x_ref, g_ref, w_ref, pu_ref, pg_ref, q_ref, kv_ref, ag_ref):
        xt = x_ref[...]
        r = lax.rsqrt(jnp.mean(xt * xt, axis=-1, keepdims=True) + EPS)
        h = (xt * r * g_ref[...]).astype(BF16)

        def proj(lo, hi):
            return lax.dot_general(h, w_ref[lo:hi, :], NT, preferred_element_type=F32)

        pu_ref[...] = proj(C_PU, C_PG)
        pg_ref[...] = proj(C_PG, C_Q)
        q_ref[...] = proj(C_Q, C_K).astype(BF16)
        kv_ref[...] = proj(C_K, C_AG).astype(BF16)
        ag_ref[...] = proj(C_AG, D_IN)

    row = lambda w: pl.BlockSpec((TM, w), lambda i: (i, 0))
    return pl.pallas_call(
        body, name="fwd_in", grid=(S // TM,),
        in_specs=[row(D), _layer(l, 1, D), _whole((D_IN, D))],
        out_specs=[row(512), row(512), row(512), row(256), row(512)],
        out_shape=[jax.ShapeDtypeStruct((S, 512), F32), jax.ShapeDtypeStruct((S, 512), F32),
                   jax.ShapeDtypeStruct((S, 512), BF16), jax.ShapeDtypeStruct((S, 256), BF16),
                   jax.ShapeDtypeStruct((S, 512), F32)],
        compiler_params=_params(),
    )(x, g_pre, w_in_t)


LOG2E = 1.4426950408889634
SCORE_SCALE = 0.125 * LOG2E


def _attention_tables():
    qi = jnp.arange(BLK)[:, None]
    kj = jnp.arange(BLK)[None, :]
    dist = ((qi - kj) % BLK).astype(F32)
    slopes = jnp.exp2(-jnp.arange(1, N_HEADS + 1, dtype=F32))
    bias = -(slopes * LOG2E)[:, None, None] * dist[None]
    first = jnp.where(kj > qi, NEG_INF, bias)
    return jnp.stack([first, bias]), (kj <= qi).astype(BF16)


def _own_block_mask():
    return lax.broadcasted_iota(jnp.int32, (BLK, BLK), 1) <= lax.broadcasted_iota(jnp.int32, (BLK, BLK), 0)


def _merge(full, own):
    return jnp.where(own, full[:, BLK:], full[:, :BLK])


def _spread(v, tri):
    own = v * tri
    return jnp.concatenate([v - own, own], axis=1)


def _head_variants(cur, prev):
    both = jnp.concatenate([prev, cur], axis=0).astype(F32)
    swapped = pltpu.roll(both, 64, axis=1)
    low = lax.broadcasted_iota(jnp.int32, both.shape, 1) < 64
    zero = jnp.zeros_like(both)
    return ((jnp.where(low, both, zero).astype(BF16), jnp.where(low, zero, swapped).astype(BF16)),
            (jnp.where(low, swapped, zero).astype(BF16), jnp.where(low, zero, both).astype(BF16)))


def _head_of(hkv, t, half):
    return hkv * 4 + 2 * t + half


def _rows(v, t):
    return v[t * BLK:(t + 1) * BLK]


def _stack_tiles(ref, hkv, offset=0):
    lo = offset + 2 * hkv * 128
    return jnp.concatenate([ref[:, lo:lo + 128], ref[:, lo + 128:lo + 256]], axis=0)


def _scores(q2, k_var, own):
    s = {}
    for hkv in range(2):
        for half in range(2):
            full = lax.dot_general(q2[hkv], k_var[hkv][half], NT, preferred_element_type=F32)
            for t in range(2):
                s[hkv, t, half] = _merge(_rows(full, t), own)
    return s


def _softmax(s, bias, sink):
    s = s * SCORE_SCALE + bias
    sink2 = sink * LOG2E
    m = jnp.maximum(jnp.max(s, axis=-1, keepdims=True), sink2)
    p = jnp.exp2(s - m)
    e_sink = jnp.exp2(sink2 - m)
    inv = 1.0 / (jnp.sum(p, axis=-1, keepdims=True) + e_sink)
    return p * inv, e_sink * inv


def _spread_pair(v, hkv, half, tri):
    return jnp.concatenate([_spread(v[hkv, t, half].astype(BF16), tri) for t in range(2)], axis=0)


POOL_ROWS = PAD + HALO + BLK


def _window_sums(src_ref, tmp_refs, trailing):
    lo, hi = (PAD, POOL_ROWS) if trailing else (0, HALO + BLK)
    cur = src_ref
    for level in range(len(POOL_WINDOWS)):
        lanes = slice(level * 128, 512)
        shift = -(1 << level) if trailing else (1 << level)
        dst = tmp_refs[level % 2]
        dst[lo:hi, lanes] = cur[lo:hi, lanes] + cur[lo + shift:hi + shift, lanes]
        cur = dst


def _pool_block(ext_ref, tmp_refs, i, g, w):
    lanes = slice(g * 128, (g + 1) * 128)
    rows = slice(PAD + HALO, POOL_ROWS)
    t = (i * BLK + lax.broadcasted_iota(jnp.int32, (BLK, 1), 0)).astype(F32)
    inv = 1.0 / jnp.minimum(t + 1.0, float(w))
    return tmp_refs[g % 2][rows, lanes] * inv - ext_ref[rows, lanes], inv


def _fwd_mix(l, pu, pg, q, kv, ag, pool_w, pool_scale, sinks, tables):
    bias, tri = tables

    def body(pu_ref, pup_ref, pg_ref, q_ref, kv_ref, kvp_ref, ag_ref, pw_ref, sc_ref, sink_ref, bias_ref, tri_ref,
             cat_ref, ext_ref, *tmp_refs):
        i = pl.program_id(0)

        @pl.when(i == 0)
        def _():
            for ref in (ext_ref, *tmp_refs):
                ref[0:PAD, :] = jnp.zeros((PAD, 512), F32)

        ext_ref[PAD:PAD + HALO, :] = jnp.where(i > 0, pup_ref[...], 0.0)
        ext_ref[PAD + HALO:POOL_ROWS, :] = pu_ref[...]
        _window_sums(ext_ref, tmp_refs, True)
        for g, w in enumerate(POOL_WINDOWS):
            lanes = slice(g * 128, (g + 1) * 128)
            pooled, _ = _pool_block(ext_ref, tmp_refs, i, g, w)
            mixed = jnp.dot(pooled.astype(BF16), pw_ref[g], preferred_element_type=F32)
            gate = pg_ref[:, lanes]
            cat_ref[:, lanes] = (mixed * sc_ref[:, lanes] * (gate * _sigmoid(gate))).astype(BF16)

        own = _own_block_mask()
        tri = tri_ref[...]
        k_var = _head_variants(kv_ref[:, 0:128], kvp_ref[:, 0:128])
        v_var = _head_variants(kv_ref[:, 128:256], kvp_ref[:, 128:256])
        s = _scores([_stack_tiles(q_ref, hkv) for hkv in range(2)], k_var, own)
        p = {}
        for (hkv, t, half), s_head in s.items():
            head = _head_of(hkv, t, half)
            p[hkv, t, half], _ = _softmax(s_head, bias_ref[head], sink_ref[l, head])
        for hkv in range(2):
            o2 = jnp.zeros((2 * BLK, 128), F32)
            for half in range(2):
                o2 = o2 + jnp.dot(_spread_pair(p, hkv, half, tri), v_var[hkv][half], preferred_element_type=F32)
            for t in range(2):
                lo = (2 * hkv + t) * 128
                gate = ag_ref[:, lo:lo + 128]
                cat_ref[:, D_POOL + lo:D_POOL + lo + 128] = (_rows(o2, t) * (gate * _sigmoid(gate))).astype(BF16)

    blk = lambda w: pl.BlockSpec((BLK, w), lambda i: (i, 0))
    prev = lambda w: pl.BlockSpec((BLK, w), lambda i: (jnp.maximum(i - 1, 0), 0))
    halo = pl.BlockSpec((HALO, 512), lambda i: (jnp.maximum(i * (BLK // HALO) - 1, 0), 0))
    return pl.pallas_call(
        body, name="fwd_mix", grid=(NB,),
        in_specs=[blk(512), halo, blk(512), blk(512), blk(256), prev(256), blk(512),
                  _layer(l, 4, 128, 128), _layer(l, 1, 512), pl.BlockSpec(memory_space=pltpu.SMEM),
                  pl.BlockSpec((None, N_HEADS, BLK, BLK), lambda i: (jnp.minimum(i, 1), 0, 0, 0)), _whole((BLK, BLK))],
        out_specs=blk(D),
        out_shape=jax.ShapeDtypeStruct((S, D), BF16),
        scratch_shapes=[pltpu.VMEM((POOL_ROWS, 512), F32)] * 3,
        compiler_params=_params(),
    )(pu, pu, pg, q, kv, kv, ag, pool_w, pool_scale, sinks, bias, tri)


def _fwd_out(l, cat, w_out, x, g_post, target=None):
    last = target is not None
    n_steps = S // TM

    def body(*refs):
        if last:
            cat_ref, w_ref, x_ref, g_ref, t_ref, y_ref, dx_ref, loss_ref, acc_ref = refs
        else:
            cat_ref, w_ref, x_ref, g_ref, y_ref, xn_ref = refs
        y = jnp.dot(cat_ref[...], w_ref[...], preferred_element_type=F32)
        y_ref[...] = y
        r = lax.rsqrt(jnp.mean(y * y, axis=-1, keepdims=True) + EPS)
        xn = x_ref[...] + y * r * g_ref[...]
        if not last:
            xn_ref[...] = xn
            return
        step = pl.program_id(0)
        err = xn - t_ref[...]
        dx_ref[...] = err * (1.0 / D)

        @pl.when(step == 0)
        def _():
            acc_ref[...] = jnp.zeros_like(acc_ref)

        acc_ref[...] += _rows8(err * err)

        @pl.when(step == n_steps - 1)
        def _():
            loss_ref[...] = jnp.full((8, 128), (0.5 / D) * jnp.sum(acc_ref[...]), F32)

    row = lambda: pl.BlockSpec((TM, D), lambda i: (i, 0))
    act = jax.ShapeDtypeStruct((S, D), F32)
    in_specs = [row(), _whole((D, D)), row(), _layer(l, 1, D)]
    args = [cat, w_out, x, g_post]
    if last:
        return pl.pallas_call(
            body, name="fwd_out_loss", grid=(n_steps,),
            in_specs=in_specs + [row()],
            out_specs=[row(), row(), pl.BlockSpec((8, 128), lambda i: (0, 0))],
            out_shape=[act, act, jax.ShapeDtypeStruct((8, 128), F32)],
            scratch_shapes=[pltpu.VMEM((8, D), F32)],
            compiler_params=_params(),
        )(*args, target)
    return pl.pallas_call(
        body, name="fwd_out", grid=(n_steps,),
        in_specs=in_specs, out_specs=[row(), row()], out_shape=[act, act],
        compiler_params=_params(),
    )(*args)


def _store_lane_rows(ref, acc):
    total = jnp.sum(acc, axis=0, keepdims=True)
    for k in range(ref.shape[0]):
        ref[k:k + 1, :] = total[:, k * 128:(k + 1) * 128]


def _own_piece(dw_ref, place_ref):
    p = dw_ref.shape[0] // 8
    return dw_ref[pl.ds(pl.multiple_of(place_ref[0] * p, 8), p), :]


def _bwd_out(l, dxn, y, g_post, cat, w_out, place_arr, deps=()):
    n_steps = S // TM

    def body(dz_ref, y_ref, g_ref, cat_ref, w_ref, place_ref, *rest):
        dcat_ref, own_ref, dwb_ref, dg_ref, acc_ref, dw_ref = rest[len(deps):]
        step = pl.program_id(0)

        @pl.when(step == 0)
        def _():
            dw_ref[...] = jnp.zeros_like(dw_ref)
            acc_ref[...] = jnp.zeros_like(acc_ref)

        y = y_ref[...]
        dz = dz_ref[...]
        r = lax.rsqrt(jnp.mean(y * y, axis=-1, keepdims=True) + EPS)
        a = dz * g_ref[...]
        dy = r * a - y * (r * r * r) * jnp.mean(a * y, axis=-1, keepdims=True)
        acc_ref[...] += _rows8(dz * (y * r))
        dyb = dy.astype(BF16)
        dcat_ref[...] = lax.dot_general(dyb, w_ref[...], NT, preferred_element_type=F32)
        dw_ref[...] += lax.dot_general(cat_ref[...], dyb, TN, preferred_element_type=F32)

        @pl.when(step == n_steps - 1)
        def _():
            _store_lane_rows(dg_ref, acc_ref[...])
            dwb_ref[...] = dw_ref[...].astype(BF16)
            own_ref[...] = _own_piece(dw_ref, place_ref)

    row = lambda: pl.BlockSpec((TM, D), lambda i: (i, 0))
    full = _whole
    return pl.pallas_call(
        body, name="bwd_out", grid=(n_steps,),
        in_specs=[row(), row(), _layer(l, 1, D), row(), full((D, D)), pl.BlockSpec(memory_space=pltpu.SMEM)]
        + [ANY] * len(deps),
        out_specs=[row(), full((D // 8, D)), full((D, D)), full((8, 128))],
        out_shape=[jax.ShapeDtypeStruct((S, D), F32), jax.ShapeDtypeStruct((D // 8, D), F32),
                   jax.ShapeDtypeStruct((D, D), BF16), jax.ShapeDtypeStruct((8, 128), F32)],
        scratch_shapes=[pltpu.VMEM((8, D), F32), pltpu.VMEM((D, D), F32)],
        compiler_params=_params(),
    )(dxn, y, g_post, cat, w_out, place_arr, *deps)


def _bwd_mix(l, pu, pg, q, kv, ag, dcat, pool_w, pool_scale, sinks, tables, deps=(), dpw_dest=None):
    bias, tri = tables
    deps = tuple(deps) + (() if dpw_dest is None else (dpw_dest,))

    def body(pu_ref, pup_ref, pg_ref, q_ref, kv_ref, kvp_ref, ag_ref, dcat_ref, pw_ref, sc_ref, sink_ref, bias_ref,
             tri_ref, *rest):
        dproj_ref, dpw_ref, dsc_ref, dsink_ref, ext_ref, dext_ref, tmp_a, tmp_b, dkv_ref = rest[len(deps):]
        tmp_refs = (tmp_a, tmp_b)
        step = pl.program_id(0)
        i = NB - 1 - step

        @pl.when(step == 0)
        def _():
            dpw_ref[...] = jnp.zeros_like(dpw_ref)
            dsc_ref[...] = jnp.zeros_like(dsc_ref)
            dsink_ref[...] = jnp.zeros_like(dsink_ref)
            for ref in (ext_ref, tmp_a, tmp_b):
                ref[0:PAD, :] = jnp.zeros((PAD, 512), F32)
            dext_ref[BLK:POOL_ROWS, :] = jnp.zeros((HALO + PAD, 512), F32)
            dkv_ref[...] = jnp.zeros_like(dkv_ref)

        ext_ref[PAD:PAD + HALO, :] = jnp.where(i > 0, pup_ref[...], 0.0)
        ext_ref[PAD + HALO:POOL_ROWS, :] = pu_ref[...]
        _window_sums(ext_ref, tmp_refs, True)
        dpooled = []
        for g, w in enumerate(POOL_WINDOWS):
            lanes = slice(g * 128, (g + 1) * 128)
            pooled, inv = _pool_block(ext_ref, tmp_refs, i, g, w)
            pooled_b = pooled.astype(BF16)
            mixed = jnp.dot(pooled_b, pw_ref[g], preferred_element_type=F32)
            scale = sc_ref[:, lanes]
            gate = pg_ref[:, lanes]
            sg = _sigmoid(gate)
            dpo = dcat_ref[:, lanes]
            dproj_ref[:, C_PG + g * 128:C_PG + (g + 1) * 128] = (
                dpo * (mixed * scale) * (sg * (1.0 + gate * (1.0 - sg)))).astype(BF16)
            dms = dpo * (gate * sg)
            dsc_ref[g:g + 1, :] += jnp.sum(dms * mixed, axis=0, keepdims=True)
            dmixed = (dms * scale).astype(BF16)
            dpw_ref[g] += lax.dot_general(pooled_b, dmixed, TN, preferred_element_type=F32)
            dpooled.append(lax.dot_general(dmixed, pw_ref[g], NT, preferred_element_type=F32))
            dext_ref[0:BLK, lanes] = dpooled[g] * inv
        _window_sums(dext_ref, tmp_refs, False)
        for g in range(len(POOL_WINDOWS)):
            lanes = slice(g * 128, (g + 1) * 128)
            dproj_ref[:, C_PU + g * 128:C_PU + (g + 1) * 128] = (tmp_refs[g % 2][0:BLK, lanes] - dpooled[g]).astype(BF16)
        dext_ref[BLK:BLK + HALO, :] = dext_ref[0:HALO, :]

        own = _own_block_mask()
        tri = tri_ref[...]
        k_var = _head_variants(kv_ref[:, 0:128], kvp_ref[:, 0:128])
        v_var = _head_variants(kv_ref[:, 128:256], kvp_ref[:, 128:256])
        q2 = [_stack_tiles(q_ref, hkv) for hkv in range(2)]
        s = _scores(q2, k_var, own)
        p, p_sink = {}, {}
        for key, s_head in s.items():
            head = _head_of(*key)
            p[key], p_sink[key] = _softmax(s_head, bias_ref[head], sink_ref[l, head])

        do2, p_b, dp = [], {}, {}
        for hkv in range(2):
            gate = _stack_tiles(ag_ref, hkv)
            sg = _sigmoid(gate)
            dca = _stack_tiles(dcat_ref, hkv, D_POOL)
            do2.append((dca * (gate * sg)).astype(BF16))
            o2 = jnp.zeros((2 * BLK, 128), F32)
            for half in range(2):
                p_b[hkv, half] = _spread_pair(p, hkv, half, tri)
                o2 = o2 + jnp.dot(p_b[hkv, half], v_var[hkv][half], preferred_element_type=F32)
                full = lax.dot_general(do2[hkv], v_var[hkv][half], NT, preferred_element_type=F32)
                for t in range(2):
                    dp[hkv, t, half] = _merge(_rows(full, t), own)
            dag = dca * o2 * (sg * (1.0 + gate * (1.0 - sg)))
            for t in range(2):
                lo = C_AG + (2 * hkv + t) * 128
                dproj_ref[:, lo:lo + 128] = _rows(dag, t).astype(BF16)

        ds = {}
        for key in p:
            delta = jnp.sum(p[key] * dp[key], axis=-1, keepdims=True)
            ds[key] = p[key] * (dp[key] - delta)
            head = _head_of(*key)
            dsink_ref[0:1, :] += jnp.where(lax.broadcasted_iota(jnp.int32, (1, 128), 1) == head,
                                           -jnp.sum(p_sink[key] * delta, axis=0, keepdims=True), 0.0)

        dk_acc = [[None, None], [None, None]]
        dv_acc = [[None, None], [None, None]]
        for hkv in range(2):
            dq2 = jnp.zeros((2 * BLK, 128), F32)
            for half in range(2):
                ds_b = _spread_pair(ds, hkv, half, tri)
                dq2 = dq2 + jnp.dot(ds_b, k_var[hkv][half], preferred_element_type=F32)
                dk_acc[hkv][half] = lax.dot_general(ds_b, q2[hkv], TN, preferred_element_type=F32)
                dv_acc[hkv][half] = lax.dot_general(p_b[hkv, half], do2[hkv], TN, preferred_element_type=F32)
            for t in range(2):
                lo = C_Q + (2 * hkv + t) * 128
                dproj_ref[:, lo:lo + 128] = (_rows(dq2, t) * 0.125).astype(BF16)

        low = lax.broadcasted_iota(jnp.int32, (2 * BLK, 128), 1) < 64

        def gather_heads(acc):
            return jnp.where(low, acc[0][0] + pltpu.roll(acc[0][1], 64, axis=1),
                             pltpu.roll(acc[1][0], 64, axis=1) + acc[1][1])

        dk = gather_heads(dk_acc) * 0.125
        dv = gather_heads(dv_acc)
        dproj_ref[:, C_K:C_V] = (dk[BLK:, :] + dkv_ref[:, 0:128]).astype(BF16)
        dproj_ref[:, C_V:C_AG] = (dv[BLK:, :] + dkv_ref[:, 128:256]).astype(BF16)
        dkv_ref[:, 0:128] = dk[:BLK, :]
        dkv_ref[:, 128:256] = dv[:BLK, :]

    rev = lambda w: pl.BlockSpec((BLK, w), lambda s: (NB - 1 - s, 0))
    prev = lambda w: pl.BlockSpec((BLK, w), lambda s: (jnp.maximum(NB - 2 - s, 0), 0))
    halo = pl.BlockSpec((HALO, 512), lambda s: (jnp.maximum((NB - 1 - s) * (BLK // HALO) - 1, 0), 0))
    return pl.pallas_call(
        body, name="bwd_mix", grid=(NB,),
        in_specs=[rev(512), halo, rev(512), rev(512), rev(256), prev(256), rev(512), rev(D),
                  _layer(l, 4, 128, 128), _layer(l, 1, 512), pl.BlockSpec(memory_space=pltpu.SMEM),
                  pl.BlockSpec((None, N_HEADS, BLK, BLK), lambda s: (jnp.minimum(NB - 1 - s, 1), 0, 0, 0)),
                  _whole((BLK, BLK))] + [ANY] * len(deps),
        out_specs=[rev(D_IN), _layer(l, 4, 128, 128),
                   pl.BlockSpec((4, 128), lambda s: (0, 0)), pl.BlockSpec((8, 128), lambda s: (0, 0))],
        out_shape=[jax.ShapeDtypeStruct((S, D_IN), BF16), jax.ShapeDtypeStruct((DEPTH, 4, 128, 128), F32),
                   jax.ShapeDtypeStruct((4, 128), F32), jax.ShapeDtypeStruct((8, 128), F32)],
        input_output_aliases={} if dpw_dest is None else {12 + len(deps): 1},
        scratch_shapes=[pltpu.VMEM((POOL_ROWS, 512), F32)] * 4 + [pltpu.VMEM((BLK, 256), F32)],
        compiler_params=_params(),
    )(pu, pu, pg, q, kv, kv, ag, dcat, pool_w, pool_scale, sinks, bias, tri, *deps)


def _bwd_in_dw(l, dproj, x, g_pre, place_arr, deps=()):
    n_steps = S // TM

    def body(dp_ref, x_ref, g_ref, place_ref, *rest):
        own_ref, dwb_ref, dw_ref = rest[len(deps):]
        step = pl.program_id(0)

        @pl.when(step == 0)
        def _():
            dw_ref[...] = jnp.zeros_like(dw_ref)

        xt = x_ref[...]
        r = lax.rsqrt(jnp.mean(xt * xt, axis=-1, keepdims=True) + EPS)
        h = (xt * r * g_ref[...]).astype(BF16)
        dw_ref[...] += lax.dot_general(dp_ref[...], h, TN, preferred_element_type=F32)

        @pl.when(step == n_steps - 1)
        def _():
            dwb_ref[...] = dw_ref[...].astype(BF16)
            own_ref[...] = _own_piece(dw_ref, place_ref)

    row = lambda w: pl.BlockSpec((TM, w), lambda i: (i, 0))
    full = _whole
    return pl.pallas_call(
        body, name="bwd_in_dw", grid=(n_steps,),
        in_specs=[row(D_IN), row(D), _layer(l, 1, D), pl.BlockSpec(memory_space=pltpu.SMEM)] + [ANY] * len(deps),
        out_specs=[full((D_IN // 8, D)), full((D_IN, D))],
        out_shape=[jax.ShapeDtypeStruct((D_IN // 8, D), F32), jax.ShapeDtypeStruct((D_IN, D), BF16)],
        scratch_shapes=[pltpu.VMEM((D_IN, D), F32)],
        compiler_params=_params(),
    )(dproj, x, g_pre, place_arr, *deps)


def _bwd_in_dx(l, dproj, w_in_t, x, g_pre, dres, deps=()):
    n_steps = S // TM

    def body(dp_ref, w_ref, x_ref, g_ref, dres_ref, *rest):
        dx_ref, dg_ref, acc_ref = rest[len(deps):]
        step = pl.program_id(0)

        @pl.when(step == 0)
        def _():
            acc_ref[...] = jnp.zeros_like(acc_ref)

        dh = jnp.dot(dp_ref[...], w_ref[...], preferred_element_type=F32)
        xt = x_ref[...]
        r = lax.rsqrt(jnp.mean(xt * xt, axis=-1, keepdims=True) + EPS)
        xn = xt * r
        acc_ref[...] += _rows8(dh * xn)
        a = dh * g_ref[...]
        dx_ref[...] = dres_ref[...] + (r * a - xt * (r * r * r) * jnp.mean(a * xt, axis=-1, keepdims=True))

        @pl.when(step == n_steps - 1)
        def _():
            _store_lane_rows(dg_ref, acc_ref[...])

    row = lambda w: pl.BlockSpec((TM, w), lambda i: (i, 0))
    full = _whole
    return pl.pallas_call(
        body, name="bwd_in_dx", grid=(n_steps,),
        in_specs=[row(D_IN), full((D_IN, D)), row(D), _layer(l, 1, D), row(D)] + [ANY] * len(deps),
        out_specs=[row(D), full((8, 128))],
        out_shape=[jax.ShapeDtypeStruct((S, D), F32), jax.ShapeDtypeStruct((8, 128), F32)],
        scratch_shapes=[pltpu.VMEM((8, D), F32)],
        compiler_params=_params(),
    )(dproj, w_in_t, x, g_pre, dres, *deps)


HBM =pl.BlockSpec(memory_space=pltpu.HBM)
SEM = pl.BlockSpec(memory_space=pltpu.SEMAPHORE)
SPLIT_COPY = pltpu.CompilerParams(has_side_effects=pltpu.SideEffectType.DATAFLOW_SIDE_EFFECTING)


def _in_hbm(a):
    return pltpu.with_memory_space_constraint(a, pltpu.HBM)

def _place():
    return lax.axis_index("x"), lax.axis_index("y"), lax.axis_index("c")


def _other_chips(x, y):
    return [(1 - x, y), (x, 1 - y), (1 - x, 1 - y)]


def _peer(x, y, c, m):
    return (x ^ (m >> 2), y ^ ((m >> 1) & 1), c ^ (m & 1))


def _place_cast(name, src, chip_arr, tile):
    _, n, cols = src.shape
    steps = n // tile

    def body(chip_ref, s0_ref, s1_ref, o0_ref, o1_ref):
        o0_ref[...] = s0_ref[...].astype(BF16)
        o1_ref[...] = s1_ref[...].astype(BF16)

    return pl.pallas_call(
        body, name=name,
        grid_spec=pltpu.PrefetchScalarGridSpec(
            num_scalar_prefetch=1, grid=(steps,),
            in_specs=[pl.BlockSpec((None, tile, cols), lambda i, chip: (0, i, 0)),
                      pl.BlockSpec((None, tile, cols), lambda i, chip: (1, i, 0))],
            out_specs=[pl.BlockSpec((tile, cols), lambda i, chip: (chip[0] * steps + i, 0))] * 2),
        out_shape=[jax.ShapeDtypeStruct((N_SHARDS * n, cols), BF16)] * 2,
        compiler_params=_params(),
    )(chip_arr, src, src)


def _chip_rows(ref, chip, half=None):
    n = ref.shape[0] // N_SHARDS
    if half is None:
        return ref.at[pl.ds(pl.multiple_of(chip * n, 16), n), :]
    return ref.at[pl.ds(pl.multiple_of(chip * n + half * (n // 2), 16), n // 2), :]


def _gather_start(bufs, halved):
    n = len(bufs)

    def body(*refs):
        ins, send, recv, token = refs[:n], refs[n:2 * n], refs[2 * n:3 * n], refs[-1]
        x, y, c = _place()
        for a, buf in enumerate(ins):
            own = _chip_rows(buf, 2 * x + y, c if a in halved else None)
            for j, chip in enumerate(_other_chips(x, y)):
                pltpu.make_async_remote_copy(src_ref=own, dst_ref=own, send_sem=send[a].at[j], recv_sem=recv[a].at[j],
                                             device_id=(*chip, c), device_id_type=MESH).start()
        token[...] = jnp.zeros_like(token)

    outs = pl.pallas_call(
        body, name="gather_start", in_specs=[HBM] * n,
        out_specs=[SEM] * (2 * n) + [HBM] * n + [pl.BlockSpec(memory_space=pltpu.VMEM)],
        out_shape=[pltpu.SemaphoreType.DMA((3,))] * (2 * n) + [pltpu.HBM(b.shape, b.dtype) for b in bufs]
        + [jax.ShapeDtypeStruct((8, 128), F32)],
        input_output_aliases={a: 2 * n + a for a in range(n)},
        compiler_params=SPLIT_COPY,
    )(*[_in_hbm(b) for b in bufs])
    return outs[:n], outs[n:2 * n], outs[2 * n:3 * n], outs[-1]


def _gather_wait(name, buf, send_sem, recv_sem, after, halved=False):
    def body(buf_ref, send_ref, recv_ref, after_ref, out_ref):
        x, y, c = _place()
        half = c if halved else None
        own = _chip_rows(buf_ref, 2 * x + y, half)
        for j, chip in enumerate(_other_chips(x, y)):
            copy = pltpu.make_async_remote_copy(src_ref=own, dst_ref=_chip_rows(buf_ref, 2 * chip[0] + chip[1], half),
                                                send_sem=send_ref.at[j], recv_sem=recv_ref.at[j],
                                                device_id=(*chip, c), device_id_type=MESH)
            copy.wait_send()
            copy.wait_recv()

    return pl.pallas_call(
        body, name=name, in_specs=[HBM, SEM, SEM, ANY], out_specs=HBM, out_shape=pltpu.HBM(buf.shape, buf.dtype),
        input_output_aliases={0: 0}, compiler_params=SPLIT_COPY,
    )(buf, send_sem, recv_sem, after)


def _forward_halves(name, buf):
    def body(in_ref, out_ref, send_sems, recv_sems):
        x, y, c = _place()

        def copy(j, chip, half):
            rows = 2 * chip[0] + chip[1]
            return pltpu.make_async_remote_copy(
                src_ref=_chip_rows(in_ref, rows, half), dst_ref=_chip_rows(out_ref, rows, half), send_sem=send_sems.at[j],
                recv_sem=recv_sems.at[j], device_id=(x, y, 1 - c), device_id_type=MESH)

        chips = _other_chips(x, y)
        for j, chip in enumerate(chips):
            copy(j, chip, c).start()
        for j, chip in enumerate(chips):
            copy(j, chip, c).wait_send()
            copy(j, chip, 1 - c).wait_recv()

    return pl.pallas_call(
        body, name=name, in_specs=[ANY], out_specs=ANY, out_shape=jax.ShapeDtypeStruct(buf.shape, buf.dtype),
        input_output_aliases={0: 0},
        scratch_shapes=[pltpu.SemaphoreType.DMA((3,))] * 2,
    )(buf)


def _piece_rows(ref, k):
    p = ref.shape[0] // 8
    return ref.at[pl.ds(pl.multiple_of(k * p, 32 // jnp.dtype(ref.dtype).itemsize), p), :]


def _exchange_start(name, arrays):
    n = len(arrays)
    zones = [lax.empty((7, a.shape[0] // 8, a.shape[1]), a.dtype) for a in arrays]

    def body(*refs):
        srcs, lands = refs[:n], refs[n:2 * n]
        send, recv, token = refs[2 * n:3 * n], refs[3 * n:4 * n], refs[-1]
        x, y, c = _place()
        for a, (src, land) in enumerate(zip(srcs, lands)):
            for m in range(1, 8):
                px, py, pc = _peer(x, y, c, m)
                pltpu.make_async_remote_copy(
                    src_ref=_piece_rows(src, 4 * px + 2 * py + pc), dst_ref=land.at[m - 1], send_sem=send[a].at[m - 1],
                    recv_sem=recv[a].at[m - 1], device_id=(px, py, pc), device_id_type=MESH).start()
        token[...] = jnp.zeros_like(token)

    outs = pl.pallas_call(
        body, name=name, in_specs=[HBM] * (2 * n),
        out_specs=[SEM] * (2 * n) + [HBM] * (2 * n) + [pl.BlockSpec(memory_space=pltpu.VMEM)],
        out_shape=[pltpu.SemaphoreType.DMA((7,))] * (2 * n) + [pltpu.HBM(a.shape, a.dtype) for a in arrays + zones]
        + [jax.ShapeDtypeStruct((8, 128), F32)],
        input_output_aliases={a: 2 * n + a for a in range(2 * n)},
        compiler_params=SPLIT_COPY,
    )(*[_in_hbm(a) for a in arrays + zones])
    return outs[:n], outs[n:2 * n], outs[2 * n:3 * n], outs[3 * n:4 * n], outs[-1]


def _exchange_wait(name, started, after):
    send_sems, recv_sems, arrays, zones, _ = started
    n = len(arrays)

    def body(*refs):
        srcs, lands = refs[:n], refs[n:2 * n]
        send, recv = refs[2 * n:3 * n], refs[3 * n:4 * n]
        x, y, c = _place()
        for a, (src, land) in enumerate(zip(srcs, lands)):
            for m in range(1, 8):
                px, py, pc = _peer(x, y, c, m)
                copy = pltpu.make_async_remote_copy(
                    src_ref=_piece_rows(src, 4 * px + 2 * py + pc), dst_ref=land.at[m - 1], send_sem=send[a].at[m - 1],
                    recv_sem=recv[a].at[m - 1], device_id=(px, py, pc), device_id_type=MESH)
                copy.wait_send()
                copy.wait_recv()

    outs = pl.pallas_call(
        body, name=name, in_specs=[HBM] * (2 * n) + [SEM] * (2 * n) + [ANY], out_specs=[HBM] * (2 * n),
        out_shape=[pltpu.HBM(a.shape, a.dtype) for a in list(arrays) + list(zones)],
        input_output_aliases={a: a for a in range(2 * n)}, compiler_params=SPLIT_COPY,
    )(*arrays, *zones, *send_sems, *recv_sems, after)
    return outs[n:]


def _sum_pieces(name, owns, recvs, place_arr, layer, dests=None):
    n = len(owns)
    steps = 2

    def body(place_ref, *refs):
        for o_ref, r_ref, out_ref in zip(refs[:n], refs[n:2 * n], refs[-n:]):
            total = o_ref[...]
            for m in range(7):
                total = total + r_ref[m].astype(F32)
            out_ref[...] = total

    tiles = [o.shape[0] // steps for o in owns]
    return pl.pallas_call(
        body, name=name,
        grid_spec=pltpu.PrefetchScalarGridSpec(
            num_scalar_prefetch=1, grid=(steps,),
            in_specs=[pl.BlockSpec((t, o.shape[1]), lambda i, place: (i, 0)) for o, t in zip(owns, tiles)]
            + [pl.BlockSpec((7, t, o.shape[1]), lambda i, place: (0, i, 0)) for o, t in zip(owns, tiles)]
            + ([] if dests is None else [ANY] * n),
            out_specs=[pl.BlockSpec((None, t, o.shape[1]), lambda i, place: (layer, place[1] * steps + i, 0))
                       for o, t in zip(owns, tiles)]),
        out_shape=[jax.ShapeDtypeStruct((DEPTH, 2 * o.shape[0], o.shape[1]), F32) for o in owns],
        input_output_aliases={} if dests is None else {1 + 2 * n + k: k for k in range(n)},
        compiler_params=_params(),
    )(place_arr, *owns, *recvs, *(() if dests is None else dests))


def _sum_small(partials, recvs, place_arr):
    n = len(partials)

    def body(place_ref, *refs):
        for o_ref, r_ref, out_ref in zip(refs[:n], refs[n:2 * n], refs[2 * n:]):
            total = o_ref[...]
            for m in range(7):
                total = total + r_ref[m]
            out_ref[...] = total

    piece = lambda a: pl.BlockSpec((a.shape[0] // 8, a.shape[1]), lambda i, place: (place[0], 0))
    return pl.pallas_call(
        body, name="sum_small",
        grid_spec=pltpu.PrefetchScalarGridSpec(
            num_scalar_prefetch=1, grid=(1,),
            in_specs=[piece(a) for a in partials] + [pl.BlockSpec(r.shape, lambda i, place: (0, 0, 0)) for r in recvs],
            out_specs=[piece(a) for a in partials]),
        out_shape=[jax.ShapeDtypeStruct(a.shape, F32) for a in partials],
        compiler_params=_params(),
    )(place_arr, *partials, *recvs)


def _share(name, bufs, parts, gathered=()):
    n, n_g = len(bufs), len(gathered)
    total = n + n_g

    def body(*refs):
        ins, outs = refs[:total], refs[total:2 * total]
        send_sems, recv_sems, send_g, recv_g = refs[2 * total:]
        x, y, c = _place()

        def half(ref, l, which):
            p = ref.shape[1] // 2
            return ref.at[l, pl.ds(pl.multiple_of(which * p, 8), p), :]

        def swap(k, which):
            a, l = parts[k]
            return pltpu.make_async_remote_copy(
                src_ref=half(ins[a], l, which), dst_ref=half(outs[a], l, which), send_sem=send_sems.at[k],
                recv_sem=recv_sems.at[k], device_id=(x, y, 1 - c), device_id_type=MESH)

        def spread(a, m, sender):
            k = 4 * sender[0] + 2 * sender[1] + sender[2]
            return pltpu.make_async_remote_copy(
                src_ref=_piece_rows(ins[n + a], k), dst_ref=_piece_rows(outs[n + a], k), send_sem=send_g.at[7 * a + m - 1],
                recv_sem=recv_g.at[7 * a + m - 1], device_id=_peer(x, y, c, m), device_id_type=MESH)

        for k in range(len(parts)):
            swap(k, c).start()
        for a in range(n_g):
            for m in range(1, 8):
                spread(a, m, (x, y, c)).start()
        for k in range(len(parts)):
            swap(k, c).wait_send()
            swap(k, 1 - c).wait_recv()
        for a in range(n_g):
            for m in range(1, 8):
                spread(a, m, (x, y, c)).wait_send()
                spread(a, m, _peer(x, y, c, m)).wait_recv()

    arrays = list(bufs) + list(gathered)
    return pl.pallas_call(
        body, name=name, in_specs=[ANY] * total, out_specs=[ANY] * total,
        out_shape=[jax.ShapeDtypeStruct(b.shape, F32) for b in arrays],
        input_output_aliases={a: a for a in range(total)},
        scratch_shapes=[pltpu.SemaphoreType.DMA((max(len(parts), 1),))] * 2
        + [pltpu.SemaphoreType.DMA((max(7 * n_g, 1),))] * 2,
    )(*arrays)


def _adamw_math(w, g, m, v):
    nm = ADAM_B1 * m + (1.0 - ADAM_B1) * g
    nv = ADAM_B2 * v + (1.0 - ADAM_B2) * (g * g)
    m_hat = nm / (1.0 - ADAM_B1 ** ADAM_STEP)
    v_hat = nv / (1.0 - ADAM_B2 ** ADAM_STEP)
    return -ADAM_LR * (m_hat / (jnp.sqrt(v_hat) + ADAM_EPS) + ADAM_WD * w), nm, nv


def _adamw(name, w, g, m, v, rows_per_step, first=0, count=None, dests=None, deps=()):
    layers, rows, cols = w.shape
    count = layers if count is None else count

    def body(w_ref, g_ref, m_ref, v_ref, *rest):
        d_ref, nm_ref, nv_ref, g_out_ref = rest[-4:]
        d_ref[...], nm_ref[...], nv_ref[...] = _adamw_math(w_ref[...], g_ref[...], m_ref[...], v_ref[...])
        g_out_ref[...] = g_ref[...]

    spec = pl.BlockSpec((1, rows_per_step, cols), lambda l, i: (first + l, i, 0))
    shape = jax.ShapeDtypeStruct(w.shape, F32)
    dests = () if dests is None else tuple(dests)
    return pl.pallas_call(
        body, name=name, grid=(count, rows // rows_per_step),
        in_specs=[spec] * 4 + [ANY] * (len(dests) + len(deps)), out_specs=[spec] * 4, out_shape=[shape] * 4,
        input_output_aliases={4 + k: k for k in range(len(dests))},
        compiler_params=_params(("arbitrary", "arbitrary")),
    )(w, g, m, v, *dests, *deps)


def _pack_misc(pool_scale, sinks, norm_pre, norm_post):
    sink_rows = jnp.zeros((DEPTH, 8, 128), F32).at[:, 0, 0:N_HEADS].set(sinks).reshape(2 * 8, 128)
    return jnp.concatenate([pool_scale.reshape(8, 128), norm_pre.reshape(16, 128), norm_post.reshape(16, 128),
                            sink_rows, jnp.zeros((8, 128), F32)], axis=0)


def _adamw_misc(w, g, m, v):
    def body(w_ref, g_ref, m_ref, v_ref, *rest):
        outs, (d_ref, nm_ref, nv_ref) = rest[:17], rest[17:]
        d_ref[...], nm_ref[...], nv_ref[...] = _adamw_math(w_ref[...], g_ref[...], m_ref[...], v_ref[...])
        for k, src in enumerate([g_ref, d_ref, nm_ref, nv_ref]):
            scale, sinks, pre, post = outs[4 * k:4 * k + 4]
            for l in range(DEPTH):
                for j in range(4):
                    scale[l:l + 1, j * 128:(j + 1) * 128] = src[MISC_SCALE + 4 * l + j:MISC_SCALE + 4 * l + j + 1, :]
                for j in range(8):
                    pre[l:l + 1, j * 128:(j + 1) * 128] = src[MISC_PRE + 8 * l + j:MISC_PRE + 8 * l + j + 1, :]
                    post[l:l + 1, j * 128:(j + 1) * 128] = src[MISC_POST + 8 * l + j:MISC_POST + 8 * l + j + 1, :]
                sinks[l:l + 1, :] = src[MISC_SINKS + 8 * l:MISC_SINKS + 8 * l + 1, 0:N_HEADS]
        outs[16][...] = g_ref[MISC_LOSS:MISC_LOSS + 1, 0:1]

    vmem = pl.BlockSpec(memory_space=pltpu.VMEM)
    shapes = [(DEPTH, D_POOL), (DEPTH, N_HEADS), (DEPTH, D), (DEPTH, D)] * 4 + [(1, 1)]
    return pl.pallas_call(
        body, name="adamw_misc", in_specs=[vmem] * 4, out_specs=[vmem] * 17,
        out_shape=[jax.ShapeDtypeStruct(s, F32) for s in shapes],
        scratch_shapes=[pltpu.VMEM((MISC_ROWS, 128), F32)] * 3,
    )(w, g, m, v)


def kernel(x, w_in, pool_w, pool_scale, attn_sinks, w_out, norm_pre, norm_post, loss_target, m_w_in, m_pool_w, m_pool_scale, m_attn_sinks, m_w_out, m_norm_pre, m_norm_post, v_w_in, v_pool_w, v_pool_scale, v_attn_sinks, v_w_out, v_norm_pre, v_norm_post):
    cx, cy, cc = _place()
    chip_arr = jnp.reshape(2 * cx + cy, (1,)).astype(jnp.int32)
    place_arr = jnp.stack([4 * cx + 2 * cy + cc, cc]).astype(jnp.int32)
    t = lambda a: jnp.transpose(a, (0, 2, 1))
    w_in_t = t(w_in)
    xs, target = x[0], loss_target[0]
    pool_w_b = pool_w.astype(BF16)
    tables = _attention_tables()
    scale3 = pool_scale.reshape(DEPTH, 1, D_POOL)
    pre3 = norm_pre.reshape(DEPTH, 1, D)
    post3 = norm_post.reshape(DEPTH, 1, D)

    wi = _place_cast("place_w_in", w_in_t, chip_arr, 288)
    wo = _place_cast("place_w_out", w_out, chip_arr, 256)
    send, recv, bufs, token = _gather_start([wi[0], wo[0], wi[1], wo[1]], halved=(0, 2))

    saved = []
    after = token
    for l in range(DEPTH):
        w_in_l = _gather_wait(f"gather_wait_in{l}", bufs[2 * l], send[2 * l], recv[2 * l], after, halved=True)
        w_in_l = _forward_halves(f"forward_w_in{l}", w_in_l)
        pu, pg, q, kv, ag = _fwd_in(l, xs, pre3, w_in_l)
        cat = _fwd_mix(l, pu, pg, q, kv, ag, pool_w_b, scale3, attn_sinks, tables)
        w_out_l = _gather_wait(f"gather_wait_out{l}", bufs[2 * l + 1], send[2 * l + 1], recv[2 * l + 1], cat)
        if l < DEPTH - 1:
            y, x_next = _fwd_out(l, cat, w_out_l, xs, post3)
        else:
            y, x_next, loss = _fwd_out(l, cat, w_out_l, xs, post3, target)
        saved.append((xs, pu, pg, q, kv, ag, cat, y, w_in_l, w_out_l))
        xs = after = x_next

    x_in, pu, pg, q, kv, ag, cat, y, w_in_l, w_out_l = saved[1]
    dcat, dw_out1, dw_out1_b, dg_post1 = _bwd_out(1, xs, y, post3, cat, w_out_l, place_arr)
    ex1_out = _exchange_start("exchange_start_out1", [dw_out1_b])
    dproj, dpw, dsc1, dsink1 = _bwd_mix(1, pu, pg, q, kv, ag, dcat, pool_w_b, scale3, attn_sinks, tables,
                                        deps=(ex1_out[4],))
    dw_in1, dw_in1_b = _bwd_in_dw(1, dproj, x_in, pre3, place_arr)
    ex1_in = _exchange_start("exchange_start_in1", [dw_in1_b])
    dx, dg_pre1 = _bwd_in_dx(1, dproj, w_in_l, x_in, pre3, xs, deps=(ex1_in[4],))

    x_in, pu, pg, q, kv, ag, cat, y, w_in_l, w_out_l = saved[0]
    dcat, dw_out0, dw_out0_b, dg_post0 = _bwd_out(0, dx, y, post3, cat, w_out_l, place_arr)
    ex0_out = _exchange_start("exchange_start_out0", [dw_out0_b])
    dproj, dpw, dsc0, dsink0 = _bwd_mix(0, pu, pg, q, kv, ag, dcat, pool_w_b, scale3, attn_sinks, tables,
                                        deps=(ex0_out[4],), dpw_dest=dpw)
    (recv_out1,) = _exchange_wait("exchange_wait_out1", ex1_out, dproj)
    (recv_in1,) = _exchange_wait("exchange_wait_in1", ex1_in, recv_out1)
    g_in, g_out = _sum_pieces("sum_pieces_1", [dw_in1, dw_out1], [recv_in1, recv_out1], place_arr, 1)
    dw_in0, dw_in0_b = _bwd_in_dw(0, dproj, x_in, pre3, place_arr, deps=(g_in, g_out))
    ex0_in = _exchange_start("exchange_start_in0", [dw_in0_b])

    grad_x, dg_pre0 = _bwd_in_dx(0, dproj, w_in_l, x_in, pre3, dx, deps=(ex0_in[4],))
    small = [dpw.reshape(DEPTH * 4 * 128, 128),
             jnp.concatenate([dsc0, dsc1, dg_pre0, dg_pre1, dg_post0, dg_post1, dsink0, dsink1, loss], axis=0)]
    ex_small = _exchange_start("exchange_start_small", small)
    (recv_out0,) = _exchange_wait("exchange_wait_out0", ex0_out, ex_small[4])
    (g_out,) = _sum_pieces("sum_pieces_out0", [dw_out0], [recv_out0], place_arr, 0, dests=[g_out])
    g_in, g_out = _share("share_a", [g_in, g_out], [(0, 1), (1, 0), (1, 1)])
    m_in_t, v_in_t = t(m_w_in), t(v_w_in)
    d_out, nm_out, nv_out, grad_w_out = _adamw("adamw_w_out", w_out, g_out, m_w_out, v_w_out, 256)
    upd_in = _adamw("adamw_w_in1", w_in_t, g_in, m_in_t, v_in_t, 288, first=1, count=1, deps=(d_out,))

    (recv_in0,) = _exchange_wait("exchange_wait_in0", ex0_in, upd_in[0])
    recv_small = _exchange_wait("exchange_wait_small", ex_small, recv_in0)
    (g_in,) = _sum_pieces("sum_pieces_in0", [dw_in0], [recv_in0], place_arr, 0, dests=[g_in])
    g_in, g_pw, g_misc = _share("share_b", [g_in], [(0, 0)], _sum_small(small, recv_small, place_arr))
    d_in, nm_in, nv_in, grad_w_in_t = _adamw("adamw_w_in0", w_in_t, g_in, m_in_t, v_in_t, 288, first=0, count=1,
                                             dests=upd_in)
    flat = lambda a: a.reshape(1, DEPTH * 4 * 128, 128)
    pw = _adamw("adamw_pool_w", flat(pool_w), flat(g_pw), flat(m_pool_w), flat(v_pool_w), 1024)
    d_pw, m_pw, v_pw, g_pw = [a.reshape(pool_w.shape) for a in pw]
    misc = _adamw_misc(_pack_misc(pool_scale, attn_sinks, norm_pre, norm_post), g_misc,
                       _pack_misc(m_pool_scale, m_attn_sinks, m_norm_pre, m_norm_post),
                       _pack_misc(v_pool_scale, v_attn_sinks, v_norm_pre, v_norm_post))
    (g_sc, g_sk, g_pre, g_post, d_sc, d_sk, d_pre, d_post,
     m_sc, m_sk, m_pre, m_post, v_sc, v_sk, v_pre, v_post, loss_sum) = misc
    return (loss_sum[0, 0], grad_x[None], t(grad_w_in_t), g_pw, g_sc, g_sk, grad_w_out, g_pre, g_post,
            t(d_in), d_pw, d_sc, d_sk, d_out, d_pre, d_post,
            t(nm_in), m_pw, m_sc, m_sk, nm_out, m_pre, m_post,
            t(nv_in), v_pw, v_sc, v_sk, nv_out, v_pre, v_post)
```

```python
import jax
import jax.numpy as jnp
from jax import lax
from jax.experimental import pallas as pl
from jax.experimental.pallas import tpu as pltpu

F32 = jnp.float32
BF16 = jnp.bfloat16

S = 2048
D = 1024
DEPTH = 2
D_POOL = 512
POOL_WINDOWS = (2, 4, 8, 16)
N_HEADS = 8
D_IN = 2304
N_SHARDS = 4
W_IN_SHARD = D_IN // N_SHARDS
W_OUT_SHARD = D // N_SHARDS
BLK = 128
NB = S // BLK
HALO = 16
PAD = 8
EPS = 1e-6
NEG_INF = -1e30
C_PU, C_PG, C_Q, C_K, C_V, C_AG = 0, 512, 1024, 1536, 1664, 1792

ADAM_LR = 0.001
ADAM_B1 = 0.9
ADAM_B2 = 0.999
ADAM_EPS = 1e-08
ADAM_WD = 0.01
ADAM_STEP = 10

TM = 512
VMEM_LIMIT = 56 * 1024 * 1024

NT = (((1,), (1,)), ((), ()))
TN = (((0,), (0,)), ((), ()))

MESH = pl.DeviceIdType.MESH
ANY = pl.BlockSpec(memory_space=pl.ANY)

MISC_SCALE, MISC_PRE, MISC_POST, MISC_SINKS, MISC_LOSS = 0, 8, 24, 40, 56
MISC_ROWS = 64


def _params(sem=("arbitrary",)):
    return pltpu.CompilerParams(dimension_semantics=sem, vmem_limit_bytes=VMEM_LIMIT)


def _sigmoid(v):
    return 1.0 / (1.0 + jnp.exp(-v))


def _rows8(v):
    r, c = v.shape
    return v.reshape(r // 8, 8, c).sum(axis=0)


def _layer(l, *shape):
    zeros = (0,) * len(shape)
    return pl.BlockSpec((None,) + shape, lambda i: (l,) + zeros)


def _whole(shape):
    zeros = (0,) * len(shape)
    return pl.BlockSpec(shape, lambda i: zeros, pipeline_mode=pl.Buffered(1))


def _fwd_in(l, x, g_pre, w_in_t):
    def body(x_ref, g_ref, w_ref, pu_ref, pg_ref, q_ref, kv_ref, ag_ref):
        xt = x_ref[...]
        r = lax.rsqrt(jnp.mean(xt * xt, axis=-1, keepdims=True) + EPS)
        h = (xt * r * g_ref[...]).astype(BF16)

        def proj(lo, hi):
            return lax.dot_general(h, w_ref[lo:hi, :], NT, preferred_element_type=F32)

        pu_ref[...] = proj(C_PU, C_PG)
        pg_ref[...] = proj(C_PG, C_Q)
        q_ref[...] = proj(C_Q, C_K).astype(BF16)
        kv_ref[...] = proj(C_K, C_AG).astype(BF16)
        ag_ref[...] = proj(C_AG, D_IN)

    row = lambda w: pl.BlockSpec((TM, w), lambda i: (i, 0))
    return pl.pallas_call(
        body, name="fwd_in", grid=(S // TM,),
        in_specs=[row(D), _layer(l, 1, D), _whole((D_IN, D))],
        out_specs=[row(512), row(512), row(512), row(256), row(512)],
        out_shape=[jax.ShapeDtypeStruct((S, 512), F32), jax.ShapeDtypeStruct((S, 512), F32),
                   jax.ShapeDtypeStruct((S, 512), BF16), jax.ShapeDtypeStruct((S, 256), BF16),
                   jax.ShapeDtypeStruct((S, 512), F32)],
        compiler_params=_params(),
    )(x, g_pre, w_in_t)


LOG2E = 1.4426950408889634
SCORE_SCALE = 0.125 * LOG2E


def _attention_tables():
    qi = jnp.arange(BLK)[:, None]
    kj = jnp.arange(BLK)[None, :]
    dist = ((qi - kj) % BLK).astype(F32)
    slopes = jnp.exp2(-jnp.arange(1, N_HEADS + 1, dtype=F32))
    bias = -(slopes * LOG2E)[:, None, None] * dist[None]
    first = jnp.where(kj > qi, NEG_INF, bias)
    return jnp.stack([first, bias]), (kj <= qi).astype(BF16)


def _own_block_mask():
    return lax.broadcasted_iota(jnp.int32, (BLK, BLK), 1) <= lax.broadcasted_iota(jnp.int32, (BLK, BLK), 0)


def _merge(full, own):
    return jnp.where(own, full[:, BLK:], full[:, :BLK])


def _spread(v, tri):
    own = v * tri
    return jnp.concatenate([v - own, own], axis=1)


def _head_variants(cur, prev):
    both = jnp.concatenate([prev, cur], axis=0).astype(F32)
    swapped = pltpu.roll(both, 64, axis=1)
    low = lax.broadcasted_iota(jnp.int32, both.shape, 1) < 64
    zero = jnp.zeros_like(both)
    return ((jnp.where(low, both, zero).astype(BF16), jnp.where(low, zero, swapped).astype(BF16)),
            (jnp.where(low, swapped, zero).astype(BF16), jnp.where(low, zero, both).astype(BF16)))


def _head_of(hkv, t, half):
    return hkv * 4 + 2 * t + half


def _rows(v, t):
    return v[t * BLK:(t + 1) * BLK]


def _stack_tiles(ref, hkv, offset=0):
    lo = offset + 2 * hkv * 128
    return jnp.concatenate([ref[:, lo:lo + 128], ref[:, lo + 128:lo + 256]], axis=0)


def _scores(q2, k_var, own):
    s = {}
    for hkv in range(2):
        for half in range(2):
            full = lax.dot_general(q2[hkv], k_var[hkv][half], NT, preferred_element_type=F32)
            for t in range(2):
                s[hkv, t, half] = _merge(_rows(full, t), own)
    return s


def _softmax(s, bias, sink):
    s = s * SCORE_SCALE + bias
    sink2 = sink * LOG2E
    m = jnp.maximum(jnp.max(s, axis=-1, keepdims=True), sink2)
    p = jnp.exp2(s - m)
    e_sink = jnp.exp2(sink2 - m)
    inv = 1.0 / (jnp.sum(p, axis=-1, keepdims=True) + e_sink)
    return p * inv, e_sink * inv


def _spread_pair(v, hkv, half, tri):
    return jnp.concatenate([_spread(v[hkv, t, half].astype(BF16), tri) for t in range(2)], axis=0)


POOL_ROWS = PAD + HALO + BLK


def _window_sums(src_ref, tmp_refs, trailing):
    lo, hi = (PAD, POOL_ROWS) if trailing else (0, HALO + BLK)
    cur = src_ref
    for level in range(len(POOL_WINDOWS)):
        lanes = slice(level * 128, 512)
        shift = -(1 << level) if trailing else (1 << level)
        dst = tmp_refs[level % 2]
        dst[lo:hi, lanes] = cur[lo:hi, lanes] + cur[lo + shift:hi + shift, lanes]
        cur = dst


def _pool_block(ext_ref, tmp_refs, i, g, w):
    lanes = slice(g * 128, (g + 1) * 128)
    rows = slice(PAD + HALO, POOL_ROWS)
    t = (i * BLK + lax.broadcasted_iota(jnp.int32, (BLK, 1), 0)).astype(F32)
    inv = 1.0 / jnp.minimum(t + 1.0, float(w))
    return tmp_refs[g % 2][rows, lanes] * inv - ext_ref[rows, lanes], inv


def _fwd_mix(l, pu, pg, q, kv, ag, pool_w, pool_scale, sinks, tables):
    bias, tri = tables

    def body(pu_ref, pup_ref, pg_ref, q_ref, kv_ref, kvp_ref, ag_ref, pw_ref, sc_ref, sink_ref, bias_ref, tri_ref,
             cat_ref, ext_ref, *tmp_refs):
        i = pl.program_id(0)

        @pl.when(i == 0)
        def _():
            for ref in (ext_ref, *tmp_refs):
                ref[0:PAD, :] = jnp.zeros((PAD, 512), F32)

        ext_ref[PAD:PAD + HALO, :] = jnp.where(i > 0, pup_ref[...], 0.0)
        ext_ref[PAD + HALO:POOL_ROWS, :] = pu_ref[...]
        _window_sums(ext_ref, tmp_refs, True)
        for g, w in enumerate(POOL_WINDOWS):
            lanes = slice(g * 128, (g + 1) * 128)
            pooled, _ = _pool_block(ext_ref, tmp_refs, i, g, w)
            mixed = jnp.dot(pooled.astype(BF16), pw_ref[g], preferred_element_type=F32)
            gate = pg_ref[:, lanes]
            cat_ref[:, lanes] = (mixed * sc_ref[:, lanes] * (gate * _sigmoid(gate))).astype(BF16)

        own = _own_block_mask()
        tri = tri_ref[...]
        k_var = _head_variants(kv_ref[:, 0:128], kvp_ref[:, 0:128])
        v_var = _head_variants(kv_ref[:, 128:256], kvp_ref[:, 128:256])
        s = _scores([_stack_tiles(q_ref, hkv) for hkv in range(2)], k_var, own)
        p = {}
        for (hkv, t, half), s_head in s.items():
            head = _head_of(hkv, t, half)
            p[hkv, t, half], _ = _softmax(s_head, bias_ref[head], sink_ref[l, head])
        for hkv in range(2):
            o2 = jnp.zeros((2 * BLK, 128), F32)
            for half in range(2):
                o2 = o2 + jnp.dot(_spread_pair(p, hkv, half, tri), v_var[hkv][half], preferred_element_type=F32)
            for t in range(2):
                lo = (2 * hkv + t) * 128
                gate = ag_ref[:, lo:lo + 128]
                cat_ref[:, D_POOL + lo:D_POOL + lo + 128] = (_rows(o2, t) * (gate * _sigmoid(gate))).astype(BF16)

    blk = lambda w: pl.BlockSpec((BLK, w), lambda i: (i, 0))
    prev = lambda w: pl.BlockSpec((BLK, w), lambda i: (jnp.maximum(i - 1, 0), 0))
    halo = pl.BlockSpec((HALO, 512), lambda i: (jnp.maximum(i * (BLK // HALO) - 1, 0), 0))
    return pl.pallas_call(
        body, name="fwd_mix", grid=(NB,),
        in_specs=[blk(512), halo, blk(512), blk(512), blk(256), prev(256), blk(512),
                  _layer(l, 4, 128, 128), _layer(l, 1, 512), pl.BlockSpec(memory_space=pltpu.SMEM),
                  pl.BlockSpec((None, N_HEADS, BLK, BLK), lambda i: (jnp.minimum(i, 1), 0, 0, 0)), _whole((BLK, BLK))],
        out_specs=blk(D),
        out_shape=jax.ShapeDtypeStruct((S, D), BF16),
        scratch_shapes=[pltpu.VMEM((POOL_ROWS, 512), F32)] * 3,
        compiler_params=_params(),
    )(pu, pu, pg, q, kv, kv, ag, pool_w, pool_scale, sinks, bias, tri)


def _fwd_out(l, cat, w_out, x, g_post):
    def body(cat_ref, w_ref, x_ref, g_ref, y_ref, xn_ref):
        y = jnp.dot(cat_ref[...], w_ref[...], preferred_element_type=F32)
        y_ref[...] = y
        r = lax.rsqrt(jnp.mean(y * y, axis=-1, keepdims=True) + EPS)
        xn_ref[...] = x_ref[...] + y * r * g_ref[...]

    row = lambda: pl.BlockSpec((TM, D), lambda i: (i, 0))
    act = jax.ShapeDtypeStruct((S, D), F32)
    return pl.pallas_call(
        body, name="fwd_out", grid=(S // TM,),
        in_specs=[row(), _whole((D, D)), row(), _layer(l, 1, D)], out_specs=[row(), row()], out_shape=[act, act],
        compiler_params=_params(),
    )(cat, w_out, x, g_post)


def _store_lane_rows(ref, acc):
    total = jnp.sum(acc, axis=0, keepdims=True)
    for k in range(ref.shape[0]):
        ref[k:k + 1, :] = total[:, k * 128:(k + 1) * 128]


def _own_piece(dw_ref, place_ref):
    p = dw_ref.shape[0] // 8
    return dw_ref[pl.ds(pl.multiple_of(place_ref[0] * p, 8), p), :]


def _bwd_out(l, cat, w_out, g_post, place_arr, dxn=None, y=None, x=None, target=None):
    last = target is not None
    n_steps = S // TM

    def body(a_ref, b_ref, g_ref, cat_ref, w_ref, place_ref, dcat_ref, own_ref, dwb_ref, dg_ref, *rest):
        acc_ref, dw_ref = rest[-2:]
        step = pl.program_id(0)

        @pl.when(step == 0)
        def _():
            dw_ref[...] = jnp.zeros_like(dw_ref)
            acc_ref[...] = jnp.zeros_like(acc_ref)

        cat = cat_ref[...]
        g = g_ref[...]
        y = jnp.dot(cat, w_ref[...], preferred_element_type=F32) if last else b_ref[...]
        r = lax.rsqrt(jnp.mean(y * y, axis=-1, keepdims=True) + EPS)
        if last:
            loss_ref, dx_ref, loss_acc_ref = rest[:3]
            err = a_ref[...] + y * r * g - b_ref[...]

            @pl.when(step == 0)
            def _():
                loss_acc_ref[...] = jnp.zeros_like(loss_acc_ref)

            loss_acc_ref[...] += _rows8(err * err)
            dz = err * (1.0 / D)
            dx_ref[...] = dz
        else:
            dz = a_ref[...]
        a = dz * g
        dy = r * a - y * (r * r * r) * jnp.mean(a * y, axis=-1, keepdims=True)
        acc_ref[...] += _rows8(dz * (y * r))
        dyb = dy.astype(BF16)
        dcat_ref[...] = lax.dot_general(dyb, w_ref[...], NT, preferred_element_type=F32)
        dw_ref[...] += lax.dot_general(cat, dyb, TN, preferred_element_type=F32)

        @pl.when(step == n_steps - 1)
        def _():
            _store_lane_rows(dg_ref, acc_ref[...])
            dwb_ref[...] = dw_ref[...].astype(BF16)
            own_ref[...] = _own_piece(dw_ref, place_ref)
            if last:
                loss_ref[...] = jnp.full((8, 128), (0.5 / D) * jnp.sum(loss_acc_ref[...]), F32)

    row = lambda: pl.BlockSpec((TM, D), lambda i: (i, 0))
    full = _whole
    return pl.pallas_call(
        body, name="out_loss_bwd" if last else "bwd_out", grid=(n_steps,),
        in_specs=[row(), row(), _layer(l, 1, D), row(), full((D, D)), pl.BlockSpec(memory_space=pltpu.SMEM)],
        out_specs=[row(), full((D // 8, D)), full((D, D)), full((8, 128))] + ([full((8, 128)), row()] if last else []),
        out_shape=[jax.ShapeDtypeStruct((S, D), F32), jax.ShapeDtypeStruct((D // 8, D), F32),
                   jax.ShapeDtypeStruct((D, D), BF16), jax.ShapeDtypeStruct((8, 128), F32)]
        + ([jax.ShapeDtypeStruct((8, 128), F32), jax.ShapeDtypeStruct((S, D), F32)] if last else []),
        scratch_shapes=([pltpu.VMEM((8, D), F32)] if last else []) + [pltpu.VMEM((8, D), F32), pltpu.VMEM((D, D), F32)],
        compiler_params=_params(),
    )(*((x, target) if last else (dxn, y)), g_post, cat, w_out, place_arr)


def _bwd_mix(l, pu, pg, q, kv, ag, dcat, pool_w, pool_scale, sinks, tables, deps=(), dpw_dest=None):
    bias, tri = tables
    deps = tuple(deps) + (() if dpw_dest is None else (dpw_dest,))

    def body(pu_ref, pup_ref, pg_ref, q_ref, kv_ref, kvp_ref, ag_ref, dcat_ref, pw_ref, sc_ref, sink_ref, bias_ref,
             tri_ref, *rest):
        dproj_ref, dpw_ref, dsc_ref, dsink_ref, ext_ref, dext_ref, tmp_a, tmp_b, dkv_ref = rest[len(deps):]
        tmp_refs = (tmp_a, tmp_b)
        step = pl.program_id(0)
        i = NB - 1 - step

        @pl.when(step == 0)
        def _():
            dpw_ref[...] = jnp.zeros_like(dpw_ref)
            dsc_ref[...] = jnp.zeros_like(dsc_ref)
            dsink_ref[...] = jnp.zeros_like(dsink_ref)
            for ref in (ext_ref, tmp_a, tmp_b):
                ref[0:PAD, :] = jnp.zeros((PAD, 512), F32)
            dext_ref[BLK:POOL_ROWS, :] = jnp.zeros((HALO + PAD, 512), F32)
            dkv_ref[...] = jnp.zeros_like(dkv_ref)

        ext_ref[PAD:PAD + HALO, :] = jnp.where(i > 0, pup_ref[...], 0.0)
        ext_ref[PAD + HALO:POOL_ROWS, :] = pu_ref[...]
        _window_sums(ext_ref, tmp_refs, True)
        dpooled = []
        for g, w in enumerate(POOL_WINDOWS):
            lanes = slice(g * 128, (g + 1) * 128)
            pooled, inv = _pool_block(ext_ref, tmp_refs, i, g, w)
            pooled_b = pooled.astype(BF16)
            mixed = jnp.dot(pooled_b, pw_ref[g], preferred_element_type=F32)
            scale = sc_ref[:, lanes]
            gate = pg_ref[:, lanes]
            sg = _sigmoid(gate)
            dpo = dcat_ref[:, lanes]
            dproj_ref[:, C_PG + g * 128:C_PG + (g + 1) * 128] = (
                dpo * (mixed * scale) * (sg * (1.0 + gate * (1.0 - sg)))).astype(BF16)
            dms = dpo * (gate * sg)
            dsc_ref[g:g + 1, :] += jnp.sum(dms * mixed, axis=0, keepdims=True)
            dmixed = (dms * scale).astype(BF16)
            dpw_ref[g] += lax.dot_general(pooled_b, dmixed, TN, preferred_element_type=F32)
            dpooled.append(lax.dot_general(dmixed, pw_ref[g], NT, preferred_element_type=F32))
            dext_ref[0:BLK, lanes] = dpooled[g] * inv
        _window_sums(dext_ref, tmp_refs, False)
        for g in range(len(POOL_WINDOWS)):
            lanes = slice(g * 128, (g + 1) * 128)
            dproj_ref[:, C_PU + g * 128:C_PU + (g + 1) * 128] = (tmp_refs[g % 2][0:BLK, lanes] - dpooled[g]).astype(BF16)
        dext_ref[BLK:BLK + HALO, :] = dext_ref[0:HALO, :]

        own = _own_block_mask()
        tri = tri_ref[...]
        k_var = _head_variants(kv_ref[:, 0:128], kvp_ref[:, 0:128])
        v_var = _head_variants(kv_ref[:, 128:256], kvp_ref[:, 128:256])
        q2 = [_stack_tiles(q_ref, hkv) for hkv in range(2)]
        s = _scores(q2, k_var, own)
        p, p_sink = {}, {}
        for key, s_head in s.items():
            head = _head_of(*key)
            p[key], p_sink[key] = _softmax(s_head, bias_ref[head], sink_ref[l, head])

        do2, p_b, dp = [], {}, {}
        for hkv in range(2):
            gate = _stack_tiles(ag_ref, hkv)
            sg = _sigmoid(gate)
            dca = _stack_tiles(dcat_ref, hkv, D_POOL)
            do2.append((dca * (gate * sg)).astype(BF16))
            o2 = jnp.zeros((2 * BLK, 128), F32)
            for half in range(2):
                p_b[hkv, half] = _spread_pair(p, hkv, half, tri)
                o2 = o2 + jnp.dot(p_b[hkv, half], v_var[hkv][half], preferred_element_type=F32)
                full = lax.dot_general(do2[hkv], v_var[hkv][half], NT, preferred_element_type=F32)
                for t in range(2):
                    dp[hkv, t, half] = _merge(_rows(full, t), own)
            dag = dca * o2 * (sg * (1.0 + gate * (1.0 - sg)))
            for t in range(2):
                lo = C_AG + (2 * hkv + t) * 128
                dproj_ref[:, lo:lo + 128] = _rows(dag, t).astype(BF16)

        ds = {}
        for key in p:
            delta = jnp.sum(p[key] * dp[key], axis=-1, keepdims=True)
            ds[key] = p[key] * (dp[key] - delta)
            head = _head_of(*key)
            dsink_ref[0:1, :] += jnp.where(lax.broadcasted_iota(jnp.int32, (1, 128), 1) == head,
                                           -jnp.sum(p_sink[key] * delta, axis=0, keepdims=True), 0.0)

        dk_acc = [[None, None], [None, None]]
        dv_acc = [[None, None], [None, None]]
        for hkv in range(2):
            dq2 = jnp.zeros((2 * BLK, 128), F32)
            for half in range(2):
                ds_b = _spread_pair(ds, hkv, half, tri)
                dq2 = dq2 + jnp.dot(ds_b, k_var[hkv][half], preferred_element_type=F32)
                dk_acc[hkv][half] = lax.dot_general(ds_b, q2[hkv], TN, preferred_element_type=F32)
                dv_acc[hkv][half] = lax.dot_general(p_b[hkv, half], do2[hkv], TN, preferred_element_type=F32)
            for t in range(2):
                lo = C_Q + (2 * hkv + t) * 128
                dproj_ref[:, lo:lo + 128] = (_rows(dq2, t) * 0.125).astype(BF16)

        low = lax.broadcasted_iota(jnp.int32, (2 * BLK, 128), 1) < 64

        def gather_heads(acc):
            return jnp.where(low, acc[0][0] + pltpu.roll(acc[0][1], 64, axis=1),
                             pltpu.roll(acc[1][0], 64, axis=1) + acc[1][1])

        dk = gather_heads(dk_acc) * 0.125
        dv = gather_heads(dv_acc)
        dproj_ref[:, C_K:C_V] = (dk[BLK:, :] + dkv_ref[:, 0:128]).astype(BF16)
        dproj_ref[:, C_V:C_AG] = (dv[BLK:, :] + dkv_ref[:, 128:256]).astype(BF16)
        dkv_ref[:, 0:128] = dk[:BLK, :]
        dkv_ref[:, 128:256] = dv[:BLK, :]

    rev = lambda w: pl.BlockSpec((BLK, w), lambda s: (NB - 1 - s, 0))
    prev = lambda w: pl.BlockSpec((BLK, w), lambda s: (jnp.maximum(NB - 2 - s, 0), 0))
    halo = pl.BlockSpec((HALO, 512), lambda s: (jnp.maximum((NB - 1 - s) * (BLK // HALO) - 1, 0), 0))
    return pl.pallas_call(
        body, name="bwd_mix", grid=(NB,),
        in_specs=[rev(512), halo, rev(512), rev(512), rev(256), prev(256), rev(512), rev(D),
                  _layer(l, 4, 128, 128), _layer(l, 1, 512), pl.BlockSpec(memory_space=pltpu.SMEM),
                  pl.BlockSpec((None, N_HEADS, BLK, BLK), lambda s: (jnp.minimum(NB - 1 - s, 1), 0, 0, 0)),
                  _whole((BLK, BLK))] + [ANY] * len(deps),
        out_specs=[rev(D_IN), _layer(l, 4, 128, 128),
                   pl.BlockSpec((4, 128), lambda s: (0, 0)), pl.BlockSpec((8, 128), lambda s: (0, 0))],
        out_shape=[jax.ShapeDtypeStruct((S, D_IN), BF16), jax.ShapeDtypeStruct((DEPTH, 4, 128, 128), F32),
                   jax.ShapeDtypeStruct((4, 128), F32), jax.ShapeDtypeStruct((8, 128), F32)],
        input_output_aliases={} if dpw_dest is None else {12 + len(deps): 1},
        scratch_shapes=[pltpu.VMEM((POOL_ROWS, 512), F32)] * 4 + [pltpu.VMEM((BLK, 256), F32)],
        compiler_params=_params(),
    )(pu, pu, pg, q, kv, kv, ag, dcat, pool_w, pool_scale, sinks, bias, tri, *deps)


def _bwd_in_dw(l, dproj, x, g_pre, place_arr, deps=()):
    n_steps = S // TM

    def body(dp_ref, x_ref, g_ref, place_ref, *rest):
        own_ref, dwb_ref, dw_ref = rest[len(deps):]
        step = pl.program_id(0)

        @pl.when(step == 0)
        def _():
            dw_ref[...] = jnp.zeros_like(dw_ref)

        xt = x_ref[...]
        r = lax.rsqrt(jnp.mean(xt * xt, axis=-1, keepdims=True) + EPS)
        h = (xt * r * g_ref[...]).astype(BF16)
        dw_ref[...] += lax.dot_general(dp_ref[...], h, TN, preferred_element_type=F32)

        @pl.when(step == n_steps - 1)
        def _():
            dwb_ref[...] = dw_ref[...].astype(BF16)
            own_ref[...] = _own_piece(dw_ref, place_ref)

    row = lambda w: pl.BlockSpec((TM, w), lambda i: (i, 0))
    full = _whole
    return pl.pallas_call(
        body, name="bwd_in_dw", grid=(n_steps,),
        in_specs=[row(D_IN), row(D), _layer(l, 1, D), pl.BlockSpec(memory_space=pltpu.SMEM)] + [ANY] * len(deps),
        out_specs=[full((D_IN // 8, D)), full((D_IN, D))],
        out_shape=[jax.ShapeDtypeStruct((D_IN // 8, D), F32), jax.ShapeDtypeStruct((D_IN, D), BF16)],
        scratch_shapes=[pltpu.VMEM((D_IN, D), F32)],
        compiler_params=_params(),
    )(dproj, x, g_pre, place_arr, *deps)


def _bwd_in_dx(l, dproj, w_in_t, x, g_pre, dres, deps=()):
    n_steps = S // TM

    def body(dp_ref, w_ref, x_ref, g_ref, dres_ref, *rest):
        dx_ref, dg_ref, acc_ref = rest[len(deps):]
        step = pl.program_id(0)

        @pl.when(step == 0)
        def _():
            acc_ref[...] = jnp.zeros_like(acc_ref)

        dh = jnp.dot(dp_ref[...], w_ref[...], preferred_element_type=F32)
        xt = x_ref[...]
        r = lax.rsqrt(jnp.mean(xt * xt, axis=-1, keepdims=True) + EPS)
        xn = xt * r
        acc_ref[...] += _rows8(dh * xn)
        a = dh * g_ref[...]
        dx_ref[...] = dres_ref[...] + (r * a - xt * (r * r * r) * jnp.mean(a * xt, axis=-1, keepdims=True))

        @pl.when(step == n_steps - 1)
        def _():
            _store_lane_rows(dg_ref, acc_ref[...])

    row = lambda w: pl.BlockSpec((TM, w), lambda i: (i, 0))
    full = _whole
    return pl.pallas_call(
        body, name="bwd_in_dx", grid=(n_steps,),
        in_specs=[row(D_IN), full((D_IN, D)), row(D), _layer(l, 1, D), row(D)] + [ANY] * len(deps),
        out_specs=[row(D), full((8, 128))],
        out_shape=[jax.ShapeDtypeStruct((S, D), F32), jax.ShapeDtypeStruct((8, 128), F32)],
        scratch_shapes=[pltpu.VMEM((8, D), F32)],
        compiler_params=_params(),
    )(dproj, w_in_t, x, g_pre, dres, *deps)


HBM =pl.BlockSpec(memory_space=pltpu.HBM)
SEM = pl.BlockSpec(memory_space=pltpu.SEMAPHORE)
SPLIT_COPY = pltpu.CompilerParams(has_side_effects=pltpu.SideEffectType.DATAFLOW_SIDE_EFFECTING)


def _in_hbm(a):
    return pltpu.with_memory_space_constraint(a, pltpu.HBM)

def _place():
    return lax.axis_index("x"), lax.axis_index("y"), lax.axis_index("c")


def _other_chips(x, y):
    return [(1 - x, y), (x, 1 - y), (1 - x, 1 - y)]


def _peer(x, y, c, m):
    return (x ^ (m >> 2), y ^ ((m >> 1) & 1), c ^ (m & 1))


def _place_cast(name, src, chip_arr, tile):
    _, n, cols = src.shape
    steps = n // tile

    def body(chip_ref, s0_ref, s1_ref, o0_ref, o1_ref):
        o0_ref[...] = s0_ref[...].astype(BF16)
        o1_ref[...] = s1_ref[...].astype(BF16)

    return pl.pallas_call(
        body, name=name,
        grid_spec=pltpu.PrefetchScalarGridSpec(
            num_scalar_prefetch=1, grid=(steps,),
            in_specs=[pl.BlockSpec((None, tile, cols), lambda i, chip: (0, i, 0)),
                      pl.BlockSpec((None, tile, cols), lambda i, chip: (1, i, 0))],
            out_specs=[pl.BlockSpec((tile, cols), lambda i, chip: (chip[0] * steps + i, 0))] * 2),
        out_shape=[jax.ShapeDtypeStruct((N_SHARDS * n, cols), BF16)] * 2,
        compiler_params=_params(),
    )(chip_arr, src, src)


def _chip_rows(ref, chip, half=None):
    n = ref.shape[0] // N_SHARDS
    if half is None:
        return ref.at[pl.ds(pl.multiple_of(chip * n, 16), n), :]
    return ref.at[pl.ds(pl.multiple_of(chip * n + half * (n // 2), 16), n // 2), :]


def _gather_start(bufs, halved):
    n = len(bufs)

    def body(*refs):
        ins, send, recv, token = refs[:n], refs[n:2 * n], refs[2 * n:3 * n], refs[-1]
        x, y, c = _place()
        for a, buf in enumerate(ins):
            own = _chip_rows(buf, 2 * x + y, c if a in halved else None)
            for j, chip in enumerate(_other_chips(x, y)):
                pltpu.make_async_remote_copy(src_ref=own, dst_ref=own, send_sem=send[a].at[j], recv_sem=recv[a].at[j],
                                             device_id=(*chip, c), device_id_type=MESH).start()
        token[...] = jnp.zeros_like(token)

    outs = pl.pallas_call(
        body, name="gather_start", in_specs=[HBM] * n,
        out_specs=[SEM] * (2 * n) + [HBM] * n + [pl.BlockSpec(memory_space=pltpu.VMEM)],
        out_shape=[pltpu.SemaphoreType.DMA((3,))] * (2 * n) + [pltpu.HBM(b.shape, b.dtype) for b in bufs]
        + [jax.ShapeDtypeStruct((8, 128), F32)],
        input_output_aliases={a: 2 * n + a for a in range(n)},
        compiler_params=SPLIT_COPY,
    )(*[_in_hbm(b) for b in bufs])
    return outs[:n], outs[n:2 * n], outs[2 * n:3 * n], outs[-1]


def _gather_wait(name, buf, send_sem, recv_sem, after, halved=False):
    def body(buf_ref, send_ref, recv_ref, after_ref, out_ref):
        x, y, c = _place()
        half = c if halved else None
        own = _chip_rows(buf_ref, 2 * x + y, half)
        for j, chip in enumerate(_other_chips(x, y)):
            copy = pltpu.make_async_remote_copy(src_ref=own, dst_ref=_chip_rows(buf_ref, 2 * chip[0] + chip[1], half),
                                                send_sem=send_ref.at[j], recv_sem=recv_ref.at[j],
                                                device_id=(*chip, c), device_id_type=MESH)
            copy.wait_send()
            copy.wait_recv()

    return pl.pallas_call(
        body, name=name, in_specs=[HBM, SEM, SEM, ANY], out_specs=HBM, out_shape=pltpu.HBM(buf.shape, buf.dtype),
        input_output_aliases={0: 0}, compiler_params=SPLIT_COPY,
    )(buf, send_sem, recv_sem, after)


def _forward_halves(name, buf):
    def body(in_ref, out_ref, send_sems, recv_sems):
        x, y, c = _place()

        def copy(j, chip, half):
            rows = 2 * chip[0] + chip[1]
            return pltpu.make_async_remote_copy(
                src_ref=_chip_rows(in_ref, rows, half), dst_ref=_chip_rows(out_ref, rows, half), send_sem=send_sems.at[j],
                recv_sem=recv_sems.at[j], device_id=(x, y, 1 - c), device_id_type=MESH)

        chips = _other_chips(x, y)
        for j, chip in enumerate(chips):
            copy(j, chip, c).start()
        for j, chip in enumerate(chips):
            copy(j, chip, c).wait_send()
            copy(j, chip, 1 - c).wait_recv()

    return pl.pallas_call(
        body, name=name, in_specs=[ANY], out_specs=ANY, out_shape=jax.ShapeDtypeStruct(buf.shape, buf.dtype),
        input_output_aliases={0: 0},
        scratch_shapes=[pltpu.SemaphoreType.DMA((3,))] * 2,
    )(buf)


def _piece_rows(ref, k):
    p = ref.shape[0] // 8
    return ref.at[pl.ds(pl.multiple_of(k * p, 32 // jnp.dtype(ref.dtype).itemsize), p), :]


def _exchange_start(name, arrays):
    n = len(arrays)
    zones = [lax.empty((7, a.shape[0] // 8, a.shape[1]), a.dtype) for a in arrays]

    def body(*refs):
        srcs, lands = refs[:n], refs[n:2 * n]
        send, recv, token = refs[2 * n:3 * n], refs[3 * n:4 * n], refs[-1]
        x, y, c = _place()
        for a, (src, land) in enumerate(zip(srcs, lands)):
            for m in range(1, 8):
                px, py, pc = _peer(x, y, c, m)
                pltpu.make_async_remote_copy(
                    src_ref=_piece_rows(src, 4 * px + 2 * py + pc), dst_ref=land.at[m - 1], send_sem=send[a].at[m - 1],
                    recv_sem=recv[a].at[m - 1], device_id=(px, py, pc), device_id_type=MESH).start()
        token[...] = jnp.zeros_like(token)

    outs = pl.pallas_call(
        body, name=name, in_specs=[HBM] * (2 * n),
        out_specs=[SEM] * (2 * n) + [HBM] * (2 * n) + [pl.BlockSpec(memory_space=pltpu.VMEM)],
        out_shape=[pltpu.SemaphoreType.DMA((7,))] * (2 * n) + [pltpu.HBM(a.shape, a.dtype) for a in arrays + zones]
        + [jax.ShapeDtypeStruct((8, 128), F32)],
        input_output_aliases={a: 2 * n + a for a in range(2 * n)},
        compiler_params=SPLIT_COPY,
    )(*[_in_hbm(a) for a in arrays + zones])
    return outs[:n], outs[n:2 * n], outs[2 * n:3 * n], outs[3 * n:4 * n], outs[-1]


def _exchange_wait(name, started, after):
    send_sems, recv_sems, arrays, zones, _ = started
    n = len(arrays)

    def body(*refs):
        srcs, lands = refs[:n], refs[n:2 * n]
        send, recv = refs[2 * n:3 * n], refs[3 * n:4 * n]
        x, y, c = _place()
        for a, (src, land) in enumerate(zip(srcs, lands)):
            for m in range(1, 8):
                px, py, pc = _peer(x, y, c, m)
                copy = pltpu.make_async_remote_copy(
                    src_ref=_piece_rows(src, 4 * px + 2 * py + pc), dst_ref=land.at[m - 1], send_sem=send[a].at[m - 1],
                    recv_sem=recv[a].at[m - 1], device_id=(px, py, pc), device_id_type=MESH)
                copy.wait_send()
                copy.wait_recv()

    outs = pl.pallas_call(
        body, name=name, in_specs=[HBM] * (2 * n) + [SEM] * (2 * n) + [ANY], out_specs=[HBM] * (2 * n),
        out_shape=[pltpu.HBM(a.shape, a.dtype) for a in list(arrays) + list(zones)],
        input_output_aliases={a: a for a in range(2 * n)}, compiler_params=SPLIT_COPY,
    )(*arrays, *zones, *send_sems, *recv_sems, after)
    return outs[n:]


def _sum_pieces(name, owns, recvs, place_arr, layer, dests=None):
    n = len(owns)
    steps = 2

    def body(place_ref, *refs):
        for o_ref, r_ref, out_ref in zip(refs[:n], refs[n:2 * n], refs[-n:]):
            total = o_ref[...]
            for m in range(7):
                total = total + r_ref[m].astype(F32)
            out_ref[...] = total

    tiles = [o.shape[0] // steps for o in owns]
    return pl.pallas_call(
        body, name=name,
        grid_spec=pltpu.PrefetchScalarGridSpec(
            num_scalar_prefetch=1, grid=(steps,),
            in_specs=[pl.BlockSpec((t, o.shape[1]), lambda i, place: (i, 0)) for o, t in zip(owns, tiles)]
            + [pl.BlockSpec((7, t, o.shape[1]), lambda i, place: (0, i, 0)) for o, t in zip(owns, tiles)]
            + ([] if dests is None else [ANY] * n),
            out_specs=[pl.BlockSpec((None, t, o.shape[1]), lambda i, place: (layer, place[1] * steps + i, 0))
                       for o, t in zip(owns, tiles)]),
        out_shape=[jax.ShapeDtypeStruct((DEPTH, 2 * o.shape[0], o.shape[1]), F32) for o in owns],
        input_output_aliases={} if dests is None else {1 + 2 * n + k: k for k in range(n)},
        compiler_params=_params(),
    )(place_arr, *owns, *recvs, *(() if dests is None else dests))


def _sum_small(partials, recvs, place_arr):
    n = len(partials)

    def body(place_ref, *refs):
        for o_ref, r_ref, out_ref in zip(refs[:n], refs[n:2 * n], refs[2 * n:]):
            total = o_ref[...]
            for m in range(7):
                total = total + r_ref[m]
            out_ref[...] = total

    piece = lambda a: pl.BlockSpec((a.shape[0] // 8, a.shape[1]), lambda i, place: (place[0], 0))
    return pl.pallas_call(
        body, name="sum_small",
        grid_spec=pltpu.PrefetchScalarGridSpec(
            num_scalar_prefetch=1, grid=(1,),
            in_specs=[piece(a) for a in partials] + [pl.BlockSpec(r.shape, lambda i, place: (0, 0, 0)) for r in recvs],
            out_specs=[piece(a) for a in partials]),
        out_shape=[jax.ShapeDtypeStruct(a.shape, F32) for a in partials],
        compiler_params=_params(),
    )(place_arr, *partials, *recvs)


def _share(name, bufs, parts, gathered=()):
    n, n_g = len(bufs), len(gathered)
    total = n + n_g

    def body(*refs):
        ins, outs = refs[:total], refs[total:2 * total]
        send_sems, recv_sems, send_g, recv_g = refs[2 * total:]
        x, y, c = _place()

        def half(ref, l, which):
            p = ref.shape[1] // 2
            return ref.at[l, pl.ds(pl.multiple_of(which * p, 8), p), :]

        def swap(k, which):
            a, l = parts[k]
            return pltpu.make_async_remote_copy(
                src_ref=half(ins[a], l, which), dst_ref=half(outs[a], l, which), send_sem=send_sems.at[k],
                recv_sem=recv_sems.at[k], device_id=(x, y, 1 - c), device_id_type=MESH)

        def spread(a, m, sender):
            k = 4 * sender[0] + 2 * sender[1] + sender[2]
            return pltpu.make_async_remote_copy(
                src_ref=_piece_rows(ins[n + a], k), dst_ref=_piece_rows(outs[n + a], k), send_sem=send_g.at[7 * a + m - 1],
                recv_sem=recv_g.at[7 * a + m - 1], device_id=_peer(x, y, c, m), device_id_type=MESH)

        for k in range(len(parts)):
            swap(k, c).start()
        for a in range(n_g):
            for m in range(1, 8):
                spread(a, m, (x, y, c)).start()
        for k in range(len(parts)):
            swap(k, c).wait_send()
            swap(k, 1 - c).wait_recv()
        for a in range(n_g):
            for m in range(1, 8):
                spread(a, m, (x, y, c)).wait_send()
                spread(a, m, _peer(x, y, c, m)).wait_recv()

    arrays = list(bufs) + list(gathered)
    return pl.pallas_call(
        body, name=name, in_specs=[ANY] * total, out_specs=[ANY] * total,
        out_shape=[jax.ShapeDtypeStruct(b.shape, F32) for b in arrays],
        input_output_aliases={a: a for a in range(total)},
        scratch_shapes=[pltpu.SemaphoreType.DMA((max(len(parts), 1),))] * 2
        + [pltpu.SemaphoreType.DMA((max(7 * n_g, 1),))] * 2,
    )(*arrays)


def _adamw_math(w, g, m, v):
    nm = ADAM_B1 * m + (1.0 - ADAM_B1) * g
    nv = ADAM_B2 * v + (1.0 - ADAM_B2) * (g * g)
    m_hat = nm / (1.0 - ADAM_B1 ** ADAM_STEP)
    v_hat = nv / (1.0 - ADAM_B2 ** ADAM_STEP)
    return -ADAM_LR * (m_hat / (jnp.sqrt(v_hat) + ADAM_EPS) + ADAM_WD * w), nm, nv


def _adamw(name, w, g, m, v, rows_per_step, first=0, count=None, dests=None, deps=()):
    layers, rows, cols = w.shape
    count = layers if count is None else count

    def body(w_ref, g_ref, m_ref, v_ref, *rest):
        d_ref, nm_ref, nv_ref, g_out_ref = rest[-4:]
        d_ref[...], nm_ref[...], nv_ref[...] = _adamw_math(w_ref[...], g_ref[...], m_ref[...], v_ref[...])
        g_out_ref[...] = g_ref[...]

    spec = pl.BlockSpec((1, rows_per_step, cols), lambda l, i: (first + l, i, 0))
    shape = jax.ShapeDtypeStruct(w.shape, F32)
    dests = () if dests is None else tuple(dests)
    return pl.pallas_call(
        body, name=name, grid=(count, rows // rows_per_step),
        in_specs=[spec] * 4 + [ANY] * (len(dests) + len(deps)), out_specs=[spec] * 4, out_shape=[shape] * 4,
        input_output_aliases={4 + k: k for k in range(len(dests))},
        compiler_params=_params(("arbitrary", "arbitrary")),
    )(w, g, m, v, *dests, *deps)


def _pack_misc(pool_scale, sinks, norm_pre, norm_post):
    sink_rows = jnp.zeros((DEPTH, 8, 128), F32).at[:, 0, 0:N_HEADS].set(sinks).reshape(2 * 8, 128)
    return jnp.concatenate([pool_scale.reshape(8, 128), norm_pre.reshape(16, 128), norm_post.reshape(16, 128),
                            sink_rows, jnp.zeros((8, 128), F32)], axis=0)


def _adamw_misc(w, g, m, v):
    def body(w_ref, g_ref, m_ref, v_ref, *rest):
        outs, (d_ref, nm_ref, nv_ref) = rest[:17], rest[17:]
        d_ref[...], nm_ref[...], nv_ref[...] = _adamw_math(w_ref[...], g_ref[...], m_ref[...], v_ref[...])
        for k, src in enumerate([g_ref, d_ref, nm_ref, nv_ref]):
            scale, sinks, pre, post = outs[4 * k:4 * k + 4]
            for l in range(DEPTH):
                for j in range(4):
                    scale[l:l + 1, j * 128:(j + 1) * 128] = src[MISC_SCALE + 4 * l + j:MISC_SCALE + 4 * l + j + 1, :]
                for j in range(8):
                    pre[l:l + 1, j * 128:(j + 1) * 128] = src[MISC_PRE + 8 * l + j:MISC_PRE + 8 * l + j + 1, :]
                    post[l:l + 1, j * 128:(j + 1) * 128] = src[MISC_POST + 8 * l + j:MISC_POST + 8 * l + j + 1, :]
                sinks[l:l + 1, :] = src[MISC_SINKS + 8 * l:MISC_SINKS + 8 * l + 1, 0:N_HEADS]
        outs[16][...] = g_ref[MISC_LOSS:MISC_LOSS + 1, 0:1]

    vmem = pl.BlockSpec(memory_space=pltpu.VMEM)
    shapes = [(DEPTH, D_POOL), (DEPTH, N_HEADS), (DEPTH, D), (DEPTH, D)] * 4 + [(1, 1)]
    return pl.pallas_call(
        body, name="adamw_misc", in_specs=[vmem] * 4, out_specs=[vmem] * 17,
        out_shape=[jax.ShapeDtypeStruct(s, F32) for s in shapes],
        scratch_shapes=[pltpu.VMEM((MISC_ROWS, 128), F32)] * 3,
    )(w, g, m, v)


def kernel(x, w_in, pool_w, pool_scale, attn_sinks, w_out, norm_pre, norm_post, loss_target, m_w_in, m_pool_w, m_pool_scale, m_attn_sinks, m_w_out, m_norm_pre, m_norm_post, v_w_in, v_pool_w, v_pool_scale, v_attn_sinks, v_w_out, v_norm_pre, v_norm_post):
    cx, cy, cc = _place()
    chip_arr = jnp.reshape(2 * cx + cy, (1,)).astype(jnp.int32)
    place_arr = jnp.stack([4 * cx + 2 * cy + cc, cc]).astype(jnp.int32)
    t = lambda a: jnp.transpose(a, (0, 2, 1))
    w_in_t = t(w_in)
    xs, target = x[0], loss_target[0]
    pool_w_b = pool_w.astype(BF16)
    tables = _attention_tables()
    scale3 = pool_scale.reshape(DEPTH, 1, D_POOL)
    pre3 = norm_pre.reshape(DEPTH, 1, D)
    post3 = norm_post.reshape(DEPTH, 1, D)

    wi = _place_cast("place_w_in", w_in_t, chip_arr, 288)
    wo = _place_cast("place_w_out", w_out, chip_arr, 256)
    send, recv, bufs, token = _gather_start([wi[0], wo[0], wi[1], wo[1]], halved=(0, 2))

    saved = []
    after = token
    for l in range(DEPTH):
        w_in_l = _gather_wait(f"gather_wait_in{l}", bufs[2 * l], send[2 * l], recv[2 * l], after, halved=True)
        w_in_l = _forward_halves(f"forward_w_in{l}", w_in_l)
        pu, pg, q, kv, ag = _fwd_in(l, xs, pre3, w_in_l)
        cat = _fwd_mix(l, pu, pg, q, kv, ag, pool_w_b, scale3, attn_sinks, tables)
        w_out_l = _gather_wait(f"gather_wait_out{l}", bufs[2 * l + 1], send[2 * l + 1], recv[2 * l + 1], cat)
        y, x_next = _fwd_out(l, cat, w_out_l, xs, post3) if l < DEPTH - 1 else (None, None)
        saved.append((xs, pu, pg, q, kv, ag, cat, y, w_in_l, w_out_l))
        if l < DEPTH - 1:
            xs = after = x_next

    x_in, pu, pg, q, kv, ag, cat, y, w_in_l, w_out_l = saved[1]
    dcat, dw_out1, dw_out1_b, dg_post1, loss, xs = _bwd_out(1, cat, w_out_l, post3, place_arr, x=x_in, target=target)
    ex1_out = _exchange_start("exchange_start_out1", [dw_out1_b])
    dproj, dpw, dsc1, dsink1 = _bwd_mix(1, pu, pg, q, kv, ag, dcat, pool_w_b, scale3, attn_sinks, tables,
                                        deps=(ex1_out[4],))
    dw_in1, dw_in1_b = _bwd_in_dw(1, dproj, x_in, pre3, place_arr)
    ex1_in = _exchange_start("exchange_start_in1", [dw_in1_b])
    dx, dg_pre1 = _bwd_in_dx(1, dproj, w_in_l, x_in, pre3, xs, deps=(ex1_in[4],))

    x_in, pu, pg, q, kv, ag, cat, y, w_in_l, w_out_l = saved[0]
    dcat, dw_out0, dw_out0_b, dg_post0 = _bwd_out(0, cat, w_out_l, post3, place_arr, dxn=dx, y=y)
    ex0_out = _exchange_start("exchange_start_out0", [dw_out0_b])
    dproj, dpw, dsc0, dsink0 = _bwd_mix(0, pu, pg, q, kv, ag, dcat, pool_w_b, scale3, attn_sinks, tables,
                                        deps=(ex0_out[4],), dpw_dest=dpw)
    (recv_out1,) = _exchange_wait("exchange_wait_out1", ex1_out, dproj)
    (recv_in1,) = _exchange_wait("exchange_wait_in1", ex1_in, recv_out1)
    g_in, g_out = _sum_pieces("sum_pieces_1", [dw_in1, dw_out1], [recv_in1, recv_out1], place_arr, 1)
    dw_in0, dw_in0_b = _bwd_in_dw(0, dproj, x_in, pre3, place_arr, deps=(g_in, g_out))
    ex0_in = _exchange_start("exchange_start_in0", [dw_in0_b])

    grad_x, dg_pre0 = _bwd_in_dx(0, dproj, w_in_l, x_in, pre3, dx, deps=(ex0_in[4],))
    small = [dpw.reshape(DEPTH * 4 * 128, 128),
             jnp.concatenate([dsc0, dsc1, dg_pre0, dg_pre1, dg_post0, dg_post1, dsink0, dsink1, loss], axis=0)]
    ex_small = _exchange_start("exchange_start_small", small)
    (recv_out0,) = _exchange_wait("exchange_wait_out0", ex0_out, ex_small[4])
    (g_out,) = _sum_pieces("sum_pieces_out0", [dw_out0], [recv_out0], place_arr, 0, dests=[g_out])
    g_in, g_out = _share("share_a", [g_in, g_out], [(0, 1), (1, 0), (1, 1)])
    m_in_t, v_in_t = t(m_w_in), t(v_w_in)
    d_out, nm_out, nv_out, grad_w_out = _adamw("adamw_w_out", w_out, g_out, m_w_out, v_w_out, 256)
    upd_in = _adamw("adamw_w_in1", w_in_t, g_in, m_in_t, v_in_t, 288, first=1, count=1, deps=(d_out,))

    (recv_in0,) = _exchange_wait("exchange_wait_in0", ex0_in, upd_in[0])
    recv_small = _exchange_wait("exchange_wait_small", ex_small, recv_in0)
    (g_in,) = _sum_pieces("sum_pieces_in0", [dw_in0], [recv_in0], place_arr, 0, dests=[g_in])
    g_in, g_pw, g_misc = _share("share_b", [g_in], [(0, 0)], _sum_small(small, recv_small, place_arr))
    d_in, nm_in, nv_in, grad_w_in_t = _adamw("adamw_w_in0", w_in_t, g_in, m_in_t, v_in_t, 288, first=0, count=1,
                                             dests=upd_in)
    flat = lambda a: a.reshape(1, DEPTH * 4 * 128, 128)
    pw = _adamw("adamw_pool_w", flat(pool_w), flat(g_pw), flat(m_pool_w), flat(v_pool_w), 1024)
    d_pw, m_pw, v_pw, g_pw = [a.reshape(pool_w.shape) for a in pw]
    misc = _adamw_misc(_pack_misc(pool_scale, attn_sinks, norm_pre, norm_post), g_misc,
                       _pack_misc(m_pool_scale, m_attn_sinks, m_norm_pre, m_norm_post),
                       _pack_misc(v_pool_scale, v_attn_sinks, v_norm_pre, v_norm_post))
    (g_sc, g_sk, g_pre, g_post, d_sc, d_sk, d_pre, d_post,
     m_sc, m_sk, m_pre, m_post, v_sc, v_sk, v_pre, v_post, loss_sum) = misc
    return (loss_sum[0, 0], grad_x[None], t(grad_w_in_t), g_pw, g_sc, g_sk, grad_w_out, g_pre, g_post,
            t(d_in), d_pw, d_sc, d_sk, d_out, d_pre, d_post,
            t(nm_in), m_pw, m_sc, m_sk, nm_out, m_pre, m_post,
            t(nv_in), v_pw, v_sc, v_sk, nv_out, v_pre, v_post)
```

```python
import jax
import jax.numpy as jnp
from jax import lax
from jax.experimental import pallas as pl
from jax.experimental.pallas import tpu as pltpu

F32 = jnp.float32
BF16 = jnp.bfloat16

S = 2048
D = 1024
DEPTH = 2
D_POOL = 512
POOL_WINDOWS = (2, 4, 8, 16)
N_HEADS = 8
D_IN = 2304
N_SHARDS = 4
W_IN_SHARD = D_IN // N_SHARDS
W_OUT_SHARD = D // N_SHARDS
BLK = 128
NB = S // BLK
HALO = 16
PAD = 8
EPS = 1e-6
NEG_INF = -1e30
C_PU, C_PG, C_Q, C_K, C_V, C_AG = 0, 512, 1024, 1536, 1664, 1792

ADAM_LR = 0.001
ADAM_B1 = 0.9
ADAM_B2 = 0.999
ADAM_EPS = 1e-08
ADAM_WD = 0.01
ADAM_STEP = 10

TM = 512
TM_DW = 1024
VMEM_LIMIT = 56 * 1024 * 1024

NT = (((1,), (1,)), ((), ()))
TN = (((0,), (0,)), ((), ()))

MESH = pl.DeviceIdType.MESH
ANY = pl.BlockSpec(memory_space=pl.ANY)

MISC_SCALE, MISC_PRE, MISC_POST, MISC_SINKS, MISC_LOSS = 0, 8, 24, 40, 56
MISC_ROWS = 64


def _params(sem=("arbitrary",)):
    return pltpu.CompilerParams(dimension_semantics=sem, vmem_limit_bytes=VMEM_LIMIT)


def _sigmoid(v):
    return 1.0 / (1.0 + jnp.exp(-v))


def _rows8(v):
    r, c = v.shape
    return v.reshape(r // 8, 8, c).sum(axis=0)


def _layer(l, *shape):
    zeros = (0,) * len(shape)
    return pl.BlockSpec((None,) + shape, lambda i: (l,) + zeros)


def _whole(shape):
    zeros = (0,) * len(shape)
    return pl.BlockSpec(shape, lambda i: zeros, pipeline_mode=pl.Buffered(1))


def _fwd_in(l, x, g_pre, w_in_t):
    def body(x_ref, g_ref, w_ref, pu_ref, pg_ref, q_ref, kv_ref, ag_ref):
        xt = x_ref[...]
        r = lax.rsqrt(jnp.mean(xt * xt, axis=-1, keepdims=True) + EPS)
        h = (xt * r * g_ref[...]).astype(BF16)

        def proj(lo, hi):
            return lax.dot_general(h, w_ref[lo:hi, :], NT, preferred_element_type=F32)

        pu_ref[...] = proj(C_PU, C_PG)
        pg_ref[...] = proj(C_PG, C_Q)
        q_ref[...] = proj(C_Q, C_K).astype(BF16)
        kv_ref[...] = proj(C_K, C_AG).astype(BF16)
        ag_ref[...] = proj(C_AG, D_IN)

    row = lambda w: pl.BlockSpec((TM, w), lambda i: (i, 0))
    return pl.pallas_call(
        body, name="fwd_in", grid=(S // TM,),
        in_specs=[row(D), _layer(l, 1, D), _whole((D_IN, D))],
        out_specs=[row(512), row(512), row(512), row(256), row(512)],
        out_shape=[jax.ShapeDtypeStruct((S, 512), F32), jax.ShapeDtypeStruct((S, 512), F32),
                   jax.ShapeDtypeStruct((S, 512), BF16), jax.ShapeDtypeStruct((S, 256), BF16),
                   jax.ShapeDtypeStruct((S, 512), F32)],
        compiler_params=_params(),
    )(x, g_pre, w_in_t)


LOG2E = 1.4426950408889634
SCORE_SCALE = 0.125 * LOG2E


def _attention_tables():
    qi = jnp.arange(BLK)[:, None]
    kj = jnp.arange(BLK)[None, :]
    dist = ((qi - kj) % BLK).astype(F32)
    slopes = jnp.exp2(-jnp.arange(1, N_HEADS + 1, dtype=F32))
    bias = -(slopes * LOG2E)[:, None, None] * dist[None]
    first = jnp.where(kj > qi, NEG_INF, bias)
    return jnp.stack([first, bias]), (kj <= qi).astype(BF16)


def _own_block_mask():
    return lax.broadcasted_iota(jnp.int32, (BLK, BLK), 1) <= lax.broadcasted_iota(jnp.int32, (BLK, BLK), 0)


def _merge(full, own):
    return jnp.where(own, full[:, BLK:], full[:, :BLK])


def _spread(v, tri):
    own = v * tri
    return jnp.concatenate([v - own, own], axis=1)


def _head_variants(cur, prev):
    both = jnp.concatenate([prev, cur], axis=0).astype(F32)
    swapped = pltpu.roll(both, 64, axis=1)
    low = lax.broadcasted_iota(jnp.int32, both.shape, 1) < 64
    zero = jnp.zeros_like(both)
    return ((jnp.where(low, both, zero).astype(BF16), jnp.where(low, zero, swapped).astype(BF16)),
            (jnp.where(low, swapped, zero).astype(BF16), jnp.where(low, zero, both).astype(BF16)))


def _head_of(hkv, t, half):
    return hkv * 4 + 2 * t + half


def _rows(v, t):
    return v[t * BLK:(t + 1) * BLK]


def _stack_tiles(ref, hkv, offset=0):
    lo = offset + 2 * hkv * 128
    return jnp.concatenate([ref[:, lo:lo + 128], ref[:, lo + 128:lo + 256]], axis=0)


def _scores(q2, k_var, own):
    s = {}
    for hkv in range(2):
        for half in range(2):
            full = lax.dot_general(q2[hkv], k_var[hkv][half], NT, preferred_element_type=F32)
            for t in range(2):
                s[hkv, t, half] = _merge(_rows(full, t), own)
    return s


def _softmax(s, bias, sink):
    s = s * SCORE_SCALE + bias
    sink2 = sink * LOG2E
    m = jnp.maximum(jnp.max(s, axis=-1, keepdims=True), sink2)
    p = jnp.exp2(s - m)
    e_sink = jnp.exp2(sink2 - m)
    inv = 1.0 / (jnp.sum(p, axis=-1, keepdims=True) + e_sink)
    return p * inv, e_sink * inv


def _spread_pair(v, hkv, half, tri):
    return jnp.concatenate([_spread(v[hkv, t, half].astype(BF16), tri) for t in range(2)], axis=0)


POOL_ROWS = PAD + HALO + BLK


def _window_sums(src_ref, tmp_refs, trailing):
    lo, hi = (PAD, POOL_ROWS) if trailing else (0, HALO + BLK)
    cur = src_ref
    for level in range(len(POOL_WINDOWS)):
        lanes = slice(level * 128, 512)
        shift = -(1 << level) if trailing else (1 << level)
        dst = tmp_refs[level % 2]
        dst[lo:hi, lanes] = cur[lo:hi, lanes] + cur[lo + shift:hi + shift, lanes]
        cur = dst


def _pool_block(ext_ref, tmp_refs, i, g, w):
    lanes = slice(g * 128, (g + 1) * 128)
    rows = slice(PAD + HALO, POOL_ROWS)
    t = (i * BLK + lax.broadcasted_iota(jnp.int32, (BLK, 1), 0)).astype(F32)
    inv = 1.0 / jnp.minimum(t + 1.0, float(w))
    return tmp_refs[g % 2][rows, lanes] * inv - ext_ref[rows, lanes], inv


def _fwd_mix(l, pu, pg, q, kv, ag, pool_w, pool_scale, sinks, tables):
    bias, tri = tables

    def body(pu_ref, pup_ref, pg_ref, q_ref, kv_ref, kvp_ref, ag_ref, pw_ref, sc_ref, sink_ref, bias_ref, tri_ref,
             cat_ref, ext_ref, *tmp_refs):
        i = pl.program_id(0)

        @pl.when(i == 0)
        def _():
            for ref in (ext_ref, *tmp_refs):
                ref[0:PAD, :] = jnp.zeros((PAD, 512), F32)

        ext_ref[PAD:PAD + HALO, :] = jnp.where(i > 0, pup_ref[...], 0.0)
        ext_ref[PAD + HALO:POOL_ROWS, :] = pu_ref[...]
        _window_sums(ext_ref, tmp_refs, True)
        for g, w in enumerate(POOL_WINDOWS):
            lanes = slice(g * 128, (g + 1) * 128)
            pooled, _ = _pool_block(ext_ref, tmp_refs, i, g, w)
            mixed = jnp.dot(pooled.astype(BF16), pw_ref[g], preferred_element_type=F32)
            gate = pg_ref[:, lanes]
            cat_ref[:, lanes] = (mixed * sc_ref[:, lanes] * (gate * _sigmoid(gate))).astype(BF16)

        own = _own_block_mask()
        tri = tri_ref[...]
        k_var = _head_variants(kv_ref[:, 0:128], kvp_ref[:, 0:128])
        v_var = _head_variants(kv_ref[:, 128:256], kvp_ref[:, 128:256])
        s = _scores([_stack_tiles(q_ref, hkv) for hkv in range(2)], k_var, own)
        p = {}
        for (hkv, t, half), s_head in s.items():
            head = _head_of(hkv, t, half)
            p[hkv, t, half], _ = _softmax(s_head, bias_ref[head], sink_ref[l, head])
        for hkv in range(2):
            o2 = jnp.zeros((2 * BLK, 128), F32)
            for half in range(2):
                o2 = o2 + jnp.dot(_spread_pair(p, hkv, half, tri), v_var[hkv][half], preferred_element_type=F32)
            for t in range(2):
                lo = (2 * hkv + t) * 128
                gate = ag_ref[:, lo:lo + 128]
                cat_ref[:, D_POOL + lo:D_POOL + lo + 128] = (_rows(o2, t) * (gate * _sigmoid(gate))).astype(BF16)

    blk = lambda w: pl.BlockSpec((BLK, w), lambda i: (i, 0))
    prev = lambda w: pl.BlockSpec((BLK, w), lambda i: (jnp.maximum(i - 1, 0), 0))
    halo = pl.BlockSpec((HALO, 512), lambda i: (jnp.maximum(i * (BLK // HALO) - 1, 0), 0))
    return pl.pallas_call(
        body, name="fwd_mix", grid=(NB,),
        in_specs=[blk(512), halo, blk(512), blk(512), blk(256), prev(256), blk(512),
                  _layer(l, 4, 128, 128), _layer(l, 1, 512), pl.BlockSpec(memory_space=pltpu.SMEM),
                  pl.BlockSpec((None, N_HEADS, BLK, BLK), lambda i: (jnp.minimum(i, 1), 0, 0, 0)), _whole((BLK, BLK))],
        out_specs=blk(D),
        out_shape=jax.ShapeDtypeStruct((S, D), BF16),
        scratch_shapes=[pltpu.VMEM((POOL_ROWS, 512), F32)] * 3,
        compiler_params=_params(),
    )(pu, pu, pg, q, kv, kv, ag, pool_w, pool_scale, sinks, bias, tri)


def _fwd_out(l, cat, w_out, x, g_post):
    def body(cat_ref, w_ref, x_ref, g_ref, y_ref, xn_ref):
        y = jnp.dot(cat_ref[...], w_ref[...], preferred_element_type=F32)
        y_ref[...] = y
        r = lax.rsqrt(jnp.mean(y * y, axis=-1, keepdims=True) + EPS)
        xn_ref[...] = x_ref[...] + y * r * g_ref[...]

    row = lambda: pl.BlockSpec((TM, D), lambda i: (i, 0))
    act = jax.ShapeDtypeStruct((S, D), F32)
    return pl.pallas_call(
        body, name="fwd_out", grid=(S // TM,),
        in_specs=[row(), _whole((D, D)), row(), _layer(l, 1, D)], out_specs=[row(), row()], out_shape=[act, act],
        compiler_params=_params(),
    )(cat, w_out, x, g_post)


def _store_lane_rows(ref, acc):
    total = jnp.sum(acc, axis=0, keepdims=True)
    for k in range(ref.shape[0]):
        ref[k:k + 1, :] = total[:, k * 128:(k + 1) * 128]


def _own_piece(dw_ref, place_ref):
    p = dw_ref.shape[0] // 8
    return dw_ref[pl.ds(pl.multiple_of(place_ref[0] * p, 8), p), :]


def _bwd_out(l, cat, w_out, g_post, place_arr, dxn=None, y=None, x=None, target=None):
    last = target is not None
    n_steps = S // TM

    def body(a_ref, b_ref, g_ref, cat_ref, w_ref, place_ref, dcat_ref, own_ref, dwb_ref, dg_ref, *rest):
        acc_ref, dw_ref = rest[-2:]
        step = pl.program_id(0)

        @pl.when(step == 0)
        def _():
            dw_ref[...] = jnp.zeros_like(dw_ref)
            acc_ref[...] = jnp.zeros_like(acc_ref)

        cat = cat_ref[...]
        g = g_ref[...]
        y = jnp.dot(cat, w_ref[...], preferred_element_type=F32) if last else b_ref[...]
        r = lax.rsqrt(jnp.mean(y * y, axis=-1, keepdims=True) + EPS)
        if last:
            loss_ref, dx_ref, loss_acc_ref = rest[:3]
            err = a_ref[...] + y * r * g - b_ref[...]

            @pl.when(step == 0)
            def _():
                loss_acc_ref[...] = jnp.zeros_like(loss_acc_ref)

            loss_acc_ref[...] += _rows8(err * err)
            dz = err * (1.0 / D)
            dx_ref[...] = dz
        else:
            dz = a_ref[...]
        a = dz * g
        dy = r * a - y * (r * r * r) * jnp.mean(a * y, axis=-1, keepdims=True)
        acc_ref[...] += _rows8(dz * (y * r))
        dyb = dy.astype(BF16)
        dcat_ref[...] = lax.dot_general(dyb, w_ref[...], NT, preferred_element_type=F32)
        dw_ref[...] += lax.dot_general(cat, dyb, TN, preferred_element_type=F32)

        @pl.when(step == n_steps - 1)
        def _():
            _store_lane_rows(dg_ref, acc_ref[...])
            dwb_ref[...] = dw_ref[...].astype(BF16)
            own_ref[...] = _own_piece(dw_ref, place_ref)
            if last:
                loss_ref[...] = jnp.full((8, 128), (0.5 / D) * jnp.sum(loss_acc_ref[...]), F32)

    row = lambda: pl.BlockSpec((TM, D), lambda i: (i, 0))
    full = _whole
    return pl.pallas_call(
        body, name="out_loss_bwd" if last else "bwd_out", grid=(n_steps,),
        in_specs=[row(), row(), _layer(l, 1, D), row(), full((D, D)), pl.BlockSpec(memory_space=pltpu.SMEM)],
        out_specs=[row(), full((D // 8, D)), full((D, D)), full((8, 128))] + ([full((8, 128)), row()] if last else []),
        out_shape=[jax.ShapeDtypeStruct((S, D), F32), jax.ShapeDtypeStruct((D // 8, D), F32),
                   jax.ShapeDtypeStruct((D, D), BF16), jax.ShapeDtypeStruct((8, 128), F32)]
        + ([jax.ShapeDtypeStruct((8, 128), F32), jax.ShapeDtypeStruct((S, D), F32)] if last else []),
        scratch_shapes=([pltpu.VMEM((8, D), F32)] if last else []) + [pltpu.VMEM((8, D), F32), pltpu.VMEM((D, D), F32)],
        compiler_params=_params(),
    )(*((x, target) if last else (dxn, y)), g_post, cat, w_out, place_arr)


def _bwd_mix(l, pu, pg, q, kv, ag, dcat, pool_w, pool_scale, sinks, tables, deps=(), dpw_dest=None):
    bias, tri = tables
    deps = tuple(deps) + (() if dpw_dest is None else (dpw_dest,))

    def body(pu_ref, pup_ref, pg_ref, q_ref, kv_ref, kvp_ref, ag_ref, dcat_ref, pw_ref, sc_ref, sink_ref, bias_ref,
             tri_ref, *rest):
        dproj_ref, dpw_ref, dsc_ref, dsink_ref, ext_ref, dext_ref, tmp_a, tmp_b, dkv_ref = rest[len(deps):]
        tmp_refs = (tmp_a, tmp_b)
        step = pl.program_id(0)
        i = NB - 1 - step

        @pl.when(step == 0)
        def _():
            dpw_ref[...] = jnp.zeros_like(dpw_ref)
            dsc_ref[...] = jnp.zeros_like(dsc_ref)
            dsink_ref[...] = jnp.zeros_like(dsink_ref)
            for ref in (ext_ref, tmp_a, tmp_b):
                ref[0:PAD, :] = jnp.zeros((PAD, 512), F32)
            dext_ref[BLK:POOL_ROWS, :] = jnp.zeros((HALO + PAD, 512), F32)
            dkv_ref[...] = jnp.zeros_like(dkv_ref)

        ext_ref[PAD:PAD + HALO, :] = jnp.where(i > 0, pup_ref[...], 0.0)
        ext_ref[PAD + HALO:POOL_ROWS, :] = pu_ref[...]
        _window_sums(ext_ref, tmp_refs, True)
        dpooled = []
        for g, w in enumerate(POOL_WINDOWS):
            lanes = slice(g * 128, (g + 1) * 128)
            pooled, inv = _pool_block(ext_ref, tmp_refs, i, g, w)
            pooled_b = pooled.astype(BF16)
            mixed = jnp.dot(pooled_b, pw_ref[g], preferred_element_type=F32)
            scale = sc_ref[:, lanes]
            gate = pg_ref[:, lanes]
            sg = _sigmoid(gate)
            dpo = dcat_ref[:, lanes]
            dproj_ref[:, C_PG + g * 128:C_PG + (g + 1) * 128] = (
                dpo * (mixed * scale) * (sg * (1.0 + gate * (1.0 - sg)))).astype(BF16)
            dms = dpo * (gate * sg)
            dsc_ref[g:g + 1, :] += jnp.sum(dms * mixed, axis=0, keepdims=True)
            dmixed = (dms * scale).astype(BF16)
            dpw_ref[g] += lax.dot_general(pooled_b, dmixed, TN, preferred_element_type=F32)
            dpooled.append(lax.dot_general(dmixed, pw_ref[g], NT, preferred_element_type=F32))
            dext_ref[0:BLK, lanes] = dpooled[g] * inv
        _window_sums(dext_ref, tmp_refs, False)
        for g in range(len(POOL_WINDOWS)):
            lanes = slice(g * 128, (g + 1) * 128)
            dproj_ref[:, C_PU + g * 128:C_PU + (g + 1) * 128] = (tmp_refs[g % 2][0:BLK, lanes] - dpooled[g]).astype(BF16)
        dext_ref[BLK:BLK + HALO, :] = dext_ref[0:HALO, :]

        own = _own_block_mask()
        tri = tri_ref[...]
        k_var = _head_variants(kv_ref[:, 0:128], kvp_ref[:, 0:128])
        v_var = _head_variants(kv_ref[:, 128:256], kvp_ref[:, 128:256])
        q2 = [_stack_tiles(q_ref, hkv) for hkv in range(2)]
        s = _scores(q2, k_var, own)
        p, p_sink = {}, {}
        for key, s_head in s.items():
            head = _head_of(*key)
            p[key], p_sink[key] = _softmax(s_head, bias_ref[head], sink_ref[l, head])

        do2, p_b, dp = [], {}, {}
        for hkv in range(2):
            gate = _stack_tiles(ag_ref, hkv)
            sg = _sigmoid(gate)
            dca = _stack_tiles(dcat_ref, hkv, D_POOL)
            do2.append((dca * (gate * sg)).astype(BF16))
            o2 = jnp.zeros((2 * BLK, 128), F32)
            for half in range(2):
                p_b[hkv, half] = _spread_pair(p, hkv, half, tri)
                o2 = o2 + jnp.dot(p_b[hkv, half], v_var[hkv][half], preferred_element_type=F32)
                full = lax.dot_general(do2[hkv], v_var[hkv][half], NT, preferred_element_type=F32)
                for t in range(2):
                    dp[hkv, t, half] = _merge(_rows(full, t), own)
            dag = dca * o2 * (sg * (1.0 + gate * (1.0 - sg)))
            for t in range(2):
                lo = C_AG + (2 * hkv + t) * 128
                dproj_ref[:, lo:lo + 128] = _rows(dag, t).astype(BF16)

        ds = {}
        for key in p:
            delta = jnp.sum(p[key] * dp[key], axis=-1, keepdims=True)
            ds[key] = p[key] * (dp[key] - delta)
            head = _head_of(*key)
            dsink_ref[0:1, :] += jnp.where(lax.broadcasted_iota(jnp.int32, (1, 128), 1) == head,
                                           -jnp.sum(p_sink[key] * delta, axis=0, keepdims=True), 0.0)

        dk_acc = [[None, None], [None, None]]
        dv_acc = [[None, None], [None, None]]
        for hkv in range(2):
            dq2 = jnp.zeros((2 * BLK, 128), F32)
            for half in range(2):
                ds_b = _spread_pair(ds, hkv, half, tri)
                dq2 = dq2 + jnp.dot(ds_b, k_var[hkv][half], preferred_element_type=F32)
                dk_acc[hkv][half] = lax.dot_general(ds_b, q2[hkv], TN, preferred_element_type=F32)
                dv_acc[hkv][half] = lax.dot_general(p_b[hkv, half], do2[hkv], TN, preferred_element_type=F32)
            for t in range(2):
                lo = C_Q + (2 * hkv + t) * 128
                dproj_ref[:, lo:lo + 128] = (_rows(dq2, t) * 0.125).astype(BF16)

        low = lax.broadcasted_iota(jnp.int32, (2 * BLK, 128), 1) < 64

        def gather_heads(acc):
            return jnp.where(low, acc[0][0] + pltpu.roll(acc[0][1], 64, axis=1),
                             pltpu.roll(acc[1][0], 64, axis=1) + acc[1][1])

        dk = gather_heads(dk_acc) * 0.125
        dv = gather_heads(dv_acc)
        dproj_ref[:, C_K:C_V] = (dk[BLK:, :] + dkv_ref[:, 0:128]).astype(BF16)
        dproj_ref[:, C_V:C_AG] = (dv[BLK:, :] + dkv_ref[:, 128:256]).astype(BF16)
        dkv_ref[:, 0:128] = dk[:BLK, :]
        dkv_ref[:, 128:256] = dv[:BLK, :]

    rev = lambda w: pl.BlockSpec((BLK, w), lambda s: (NB - 1 - s, 0))
    prev = lambda w: pl.BlockSpec((BLK, w), lambda s: (jnp.maximum(NB - 2 - s, 0), 0))
    halo = pl.BlockSpec((HALO, 512), lambda s: (jnp.maximum((NB - 1 - s) * (BLK // HALO) - 1, 0), 0))
    return pl.pallas_call(
        body, name="bwd_mix", grid=(NB,),
        in_specs=[rev(512), halo, rev(512), rev(512), rev(256), prev(256), rev(512), rev(D),
                  _layer(l, 4, 128, 128), _layer(l, 1, 512), pl.BlockSpec(memory_space=pltpu.SMEM),
                  pl.BlockSpec((None, N_HEADS, BLK, BLK), lambda s: (jnp.minimum(NB - 1 - s, 1), 0, 0, 0)),
                  _whole((BLK, BLK))] + [ANY] * len(deps),
        out_specs=[rev(D_IN), _layer(l, 4, 128, 128),
                   pl.BlockSpec((4, 128), lambda s: (0, 0)), pl.BlockSpec((8, 128), lambda s: (0, 0))],
        out_shape=[jax.ShapeDtypeStruct((S, D_IN), BF16), jax.ShapeDtypeStruct((DEPTH, 4, 128, 128), F32),
                   jax.ShapeDtypeStruct((4, 128), F32), jax.ShapeDtypeStruct((8, 128), F32)],
        input_output_aliases={} if dpw_dest is None else {12 + len(deps): 1},
        scratch_shapes=[pltpu.VMEM((POOL_ROWS, 512), F32)] * 4 + [pltpu.VMEM((BLK, 256), F32)],
        compiler_params=_params(),
    )(pu, pu, pg, q, kv, kv, ag, dcat, pool_w, pool_scale, sinks, bias, tri, *deps)


def _bwd_in_dw(l, dproj, x, g_pre, place_arr, deps=()):
    n_steps = S // TM_DW

    def body(dp_ref, x_ref, g_ref, place_ref, *rest):
        own_ref, dwb_ref, dw_ref = rest[len(deps):]
        step = pl.program_id(0)

        @pl.when(step == 0)
        def _():
            dw_ref[...] = jnp.zeros_like(dw_ref)

        xt = x_ref[...]
        r = lax.rsqrt(jnp.mean(xt * xt, axis=-1, keepdims=True) + EPS)
        h = (xt * r * g_ref[...]).astype(BF16)
        dw_ref[...] += lax.dot_general(dp_ref[...], h, TN, preferred_element_type=F32)

        @pl.when(step == n_steps - 1)
        def _():
            dwb_ref[...] = dw_ref[...].astype(BF16)
            own_ref[...] = _own_piece(dw_ref, place_ref)

    row = lambda w: pl.BlockSpec((TM_DW, w), lambda i: (i, 0))
    full = _whole
    return pl.pallas_call(
        body, name="bwd_in_dw", grid=(n_steps,),
        in_specs=[row(D_IN), row(D), _layer(l, 1, D), pl.BlockSpec(memory_space=pltpu.SMEM)] + [ANY] * len(deps),
        out_specs=[full((D_IN // 8, D)), full((D_IN, D))],
        out_shape=[jax.ShapeDtypeStruct((D_IN // 8, D), F32), jax.ShapeDtypeStruct((D_IN, D), BF16)],
        scratch_shapes=[pltpu.VMEM((D_IN, D), F32)],
        compiler_params=_params(),
    )(dproj, x, g_pre, place_arr, *deps)


def _bwd_in_dx(l, dproj, w_in_t, x, g_pre, dres, deps=()):
    n_steps = S // TM

    def body(dp_ref, w_ref, x_ref, g_ref, dres_ref, *rest):
        dx_ref, dg_ref, acc_ref = rest[len(deps):]
        step = pl.program_id(0)

        @pl.when(step == 0)
        def _():
            acc_ref[...] = jnp.zeros_like(acc_ref)

        dh = jnp.dot(dp_ref[...], w_ref[...], preferred_element_type=F32)
        xt = x_ref[...]
        r = lax.rsqrt(jnp.mean(xt * xt, axis=-1, keepdims=True) + EPS)
        xn = xt * r
        acc_ref[...] += _rows8(dh * xn)
        a = dh * g_ref[...]
        dx_ref[...] = dres_ref[...] + (r * a - xt * (r * r * r) * jnp.mean(a * xt, axis=-1, keepdims=True))

        @pl.when(step == n_steps - 1)
        def _():
            _store_lane_rows(dg_ref, acc_ref[...])

    row = lambda w: pl.BlockSpec((TM, w), lambda i: (i, 0))
    full = _whole
    return pl.pallas_call(
        body, name="bwd_in_dx", grid=(n_steps,),
        in_specs=[row(D_IN), full((D_IN, D)), row(D), _layer(l, 1, D), row(D)] + [ANY] * len(deps),
        out_specs=[row(D), full((8, 128))],
        out_shape=[jax.ShapeDtypeStruct((S, D), F32), jax.ShapeDtypeStruct((8, 128), F32)],
        scratch_shapes=[pltpu.VMEM((8, D), F32)],
        compiler_params=_params(),
    )(dproj, w_in_t, x, g_pre, dres, *deps)


HBM =pl.BlockSpec(memory_space=pltpu.HBM)
SEM = pl.BlockSpec(memory_space=pltpu.SEMAPHORE)
SPLIT_COPY = pltpu.CompilerParams(has_side_effects=pltpu.SideEffectType.DATAFLOW_SIDE_EFFECTING)


def _in_hbm(a):
    return pltpu.with_memory_space_constraint(a, pltpu.HBM)

def _place():
    return lax.axis_index("x"), lax.axis_index("y"), lax.axis_index("c")


def _other_chips(x, y):
    return [(1 - x, y), (x, 1 - y), (1 - x, 1 - y)]


def _peer(x, y, c, m):
    return (x ^ (m >> 2), y ^ ((m >> 1) & 1), c ^ (m & 1))


def _place_cast(name, src, chip_arr, tile):
    _, n, cols = src.shape
    steps = n // tile

    def body(chip_ref, s0_ref, s1_ref, o0_ref, o1_ref):
        o0_ref[...] = s0_ref[...].astype(BF16)
        o1_ref[...] = s1_ref[...].astype(BF16)

    return pl.pallas_call(
        body, name=name,
        grid_spec=pltpu.PrefetchScalarGridSpec(
            num_scalar_prefetch=1, grid=(steps,),
            in_specs=[pl.BlockSpec((None, tile, cols), lambda i, chip: (0, i, 0)),
                      pl.BlockSpec((None, tile, cols), lambda i, chip: (1, i, 0))],
            out_specs=[pl.BlockSpec((tile, cols), lambda i, chip: (chip[0] * steps + i, 0))] * 2),
        out_shape=[jax.ShapeDtypeStruct((N_SHARDS * n, cols), BF16)] * 2,
        compiler_params=_params(),
    )(chip_arr, src, src)


def _chip_rows(ref, chip, half=None):
    n = ref.shape[0] // N_SHARDS
    if half is None:
        return ref.at[pl.ds(pl.multiple_of(chip * n, 16), n), :]
    return ref.at[pl.ds(pl.multiple_of(chip * n + half * (n // 2), 16), n // 2), :]


def _gather_start(bufs, halved):
    n = len(bufs)

    def body(*refs):
        ins, send, recv, token = refs[:n], refs[n:2 * n], refs[2 * n:3 * n], refs[-1]
        x, y, c = _place()
        for a, buf in enumerate(ins):
            own = _chip_rows(buf, 2 * x + y, c if a in halved else None)
            for j, chip in enumerate(_other_chips(x, y)):
                pltpu.make_async_remote_copy(src_ref=own, dst_ref=own, send_sem=send[a].at[j], recv_sem=recv[a].at[j],
                                             device_id=(*chip, c), device_id_type=MESH).start()
        token[...] = jnp.zeros_like(token)

    outs = pl.pallas_call(
        body, name="gather_start", in_specs=[HBM] * n,
        out_specs=[SEM] * (2 * n) + [HBM] * n + [pl.BlockSpec(memory_space=pltpu.VMEM)],
        out_shape=[pltpu.SemaphoreType.DMA((3,))] * (2 * n) + [pltpu.HBM(b.shape, b.dtype) for b in bufs]
        + [jax.ShapeDtypeStruct((8, 128), F32)],
        input_output_aliases={a: 2 * n + a for a in range(n)},
        compiler_params=SPLIT_COPY,
    )(*[_in_hbm(b) for b in bufs])
    return outs[:n], outs[n:2 * n], outs[2 * n:3 * n], outs[-1]


def _gather_wait(name, buf, send_sem, recv_sem, after, halved=False):
    def body(buf_ref, send_ref, recv_ref, after_ref, out_ref):
        x, y, c = _place()
        half = c if halved else None
        own = _chip_rows(buf_ref, 2 * x + y, half)
        for j, chip in enumerate(_other_chips(x, y)):
            copy = pltpu.make_async_remote_copy(src_ref=own, dst_ref=_chip_rows(buf_ref, 2 * chip[0] + chip[1], half),
                                                send_sem=send_ref.at[j], recv_sem=recv_ref.at[j],
                                                device_id=(*chip, c), device_id_type=MESH)
            copy.wait_send()
            copy.wait_recv()

    return pl.pallas_call(
        body, name=name, in_specs=[HBM, SEM, SEM, ANY], out_specs=HBM, out_shape=pltpu.HBM(buf.shape, buf.dtype),
        input_output_aliases={0: 0}, compiler_params=SPLIT_COPY,
    )(buf, send_sem, recv_sem, after)


def _forward_halves(name, buf):
    def body(in_ref, out_ref, send_sems, recv_sems):
        x, y, c = _place()

        def copy(j, chip, half):
            rows = 2 * chip[0] + chip[1]
            return pltpu.make_async_remote_copy(
                src_ref=_chip_rows(in_ref, rows, half), dst_ref=_chip_rows(out_ref, rows, half), send_sem=send_sems.at[j],
                recv_sem=recv_sems.at[j], device_id=(x, y, 1 - c), device_id_type=MESH)

        chips = _other_chips(x, y)
        for j, chip in enumerate(chips):
            copy(j, chip, c).start()
        for j, chip in enumerate(chips):
            copy(j, chip, c).wait_send()
            copy(j, chip, 1 - c).wait_recv()

    return pl.pallas_call(
        body, name=name, in_specs=[ANY], out_specs=ANY, out_shape=jax.ShapeDtypeStruct(buf.shape, buf.dtype),
        input_output_aliases={0: 0},
        scratch_shapes=[pltpu.SemaphoreType.DMA((3,))] * 2,
    )(buf)


def _piece_rows(ref, k):
    p = ref.shape[0] // 8
    return ref.at[pl.ds(pl.multiple_of(k * p, 32 // jnp.dtype(ref.dtype).itemsize), p), :]


def _exchange_start(name, arrays):
    n = len(arrays)
    zones = [lax.empty((7, a.shape[0] // 8, a.shape[1]), a.dtype) for a in arrays]

    def body(*refs):
        srcs, lands = refs[:n], refs[n:2 * n]
        send, recv, token = refs[2 * n:3 * n], refs[3 * n:4 * n], refs[-1]
        x, y, c = _place()
        for a, (src, land) in enumerate(zip(srcs, lands)):
            for m in range(1, 8):
                px, py, pc = _peer(x, y, c, m)
                pltpu.make_async_remote_copy(
                    src_ref=_piece_rows(src, 4 * px + 2 * py + pc), dst_ref=land.at[m - 1], send_sem=send[a].at[m - 1],
                    recv_sem=recv[a].at[m - 1], device_id=(px, py, pc), device_id_type=MESH).start()
        token[...] = jnp.zeros_like(token)

    outs = pl.pallas_call(
        body, name=name, in_specs=[HBM] * (2 * n),
        out_specs=[SEM] * (2 * n) + [HBM] * (2 * n) + [pl.BlockSpec(memory_space=pltpu.VMEM)],
        out_shape=[pltpu.SemaphoreType.DMA((7,))] * (2 * n) + [pltpu.HBM(a.shape, a.dtype) for a in arrays + zones]
        + [jax.ShapeDtypeStruct((8, 128), F32)],
        input_output_aliases={a: 2 * n + a for a in range(2 * n)},
        compiler_params=SPLIT_COPY,
    )(*[_in_hbm(a) for a in arrays + zones])
    return outs[:n], outs[n:2 * n], outs[2 * n:3 * n], outs[3 * n:4 * n], outs[-1]


def _exchange_wait(name, started, after):
    send_sems, recv_sems, arrays, zones, _ = started
    n = len(arrays)

    def body(*refs):
        srcs, lands = refs[:n], refs[n:2 * n]
        send, recv = refs[2 * n:3 * n], refs[3 * n:4 * n]
        x, y, c = _place()
        for a, (src, land) in enumerate(zip(srcs, lands)):
            for m in range(1, 8):
                px, py, pc = _peer(x, y, c, m)
                copy = pltpu.make_async_remote_copy(
                    src_ref=_piece_rows(src, 4 * px + 2 * py + pc), dst_ref=land.at[m - 1], send_sem=send[a].at[m - 1],
                    recv_sem=recv[a].at[m - 1], device_id=(px, py, pc), device_id_type=MESH)
                copy.wait_send()
                copy.wait_recv()

    outs = pl.pallas_call(
        body, name=name, in_specs=[HBM] * (2 * n) + [SEM] * (2 * n) + [ANY], out_specs=[HBM] * (2 * n),
        out_shape=[pltpu.HBM(a.shape, a.dtype) for a in list(arrays) + list(zones)],
        input_output_aliases={a: a for a in range(2 * n)}, compiler_params=SPLIT_COPY,
    )(*arrays, *zones, *send_sems, *recv_sems, after)
    return outs[n:]


def _sum_pieces(name, owns, recvs, place_arr, layer, dests=None):
    n = len(owns)
    steps = 2

    def body(place_ref, *refs):
        for o_ref, r_ref, out_ref in zip(refs[:n], refs[n:2 * n], refs[-n:]):
            total = o_ref[...]
            for m in range(7):
                total = total + r_ref[m].astype(F32)
            out_ref[...] = total

    tiles = [o.shape[0] // steps for o in owns]
    return pl.pallas_call(
        body, name=name,
        grid_spec=pltpu.PrefetchScalarGridSpec(
            num_scalar_prefetch=1, grid=(steps,),
            in_specs=[pl.BlockSpec((t, o.shape[1]), lambda i, place: (i, 0)) for o, t in zip(owns, tiles)]
            + [pl.BlockSpec((7, t, o.shape[1]), lambda i, place: (0, i, 0)) for o, t in zip(owns, tiles)]
            + ([] if dests is None else [ANY] * n),
            out_specs=[pl.BlockSpec((None, t, o.shape[1]), lambda i, place: (layer, place[1] * steps + i, 0))
                       for o, t in zip(owns, tiles)]),
        out_shape=[jax.ShapeDtypeStruct((DEPTH, 2 * o.shape[0], o.shape[1]), F32) for o in owns],
        input_output_aliases={} if dests is None else {1 + 2 * n + k: k for k in range(n)},
        compiler_params=_params(),
    )(place_arr, *owns, *recvs, *(() if dests is None else dests))


def _sum_small(partials, recvs, place_arr):
    n = len(partials)

    def body(place_ref, *refs):
        for o_ref, r_ref, out_ref in zip(refs[:n], refs[n:2 * n], refs[2 * n:]):
            total = o_ref[...]
            for m in range(7):
                total = total + r_ref[m]
            out_ref[...] = total

    piece = lambda a: pl.BlockSpec((a.shape[0] // 8, a.shape[1]), lambda i, place: (place[0], 0))
    return pl.pallas_call(
        body, name="sum_small",
        grid_spec=pltpu.PrefetchScalarGridSpec(
            num_scalar_prefetch=1, grid=(1,),
            in_specs=[piece(a) for a in partials] + [pl.BlockSpec(r.shape, lambda i, place: (0, 0, 0)) for r in recvs],
            out_specs=[piece(a) for a in partials]),
        out_shape=[jax.ShapeDtypeStruct(a.shape, F32) for a in partials],
        compiler_params=_params(),
    )(place_arr, *partials, *recvs)


def _share(name, bufs, parts, gathered=()):
    n, n_g = len(bufs), len(gathered)
    total = n + n_g

    def body(*refs):
        ins, outs = refs[:total], refs[total:2 * total]
        send_sems, recv_sems, send_g, recv_g = refs[2 * total:]
        x, y, c = _place()

        def half(ref, l, which):
            p = ref.shape[1] // 2
            return ref.at[l, pl.ds(pl.multiple_of(which * p, 8), p), :]

        def swap(k, which):
            a, l = parts[k]
            return pltpu.make_async_remote_copy(
                src_ref=half(ins[a], l, which), dst_ref=half(outs[a], l, which), send_sem=send_sems.at[k],
                recv_sem=recv_sems.at[k], device_id=(x, y, 1 - c), device_id_type=MESH)

        def spread(a, m, sender):
            k = 4 * sender[0] + 2 * sender[1] + sender[2]
            return pltpu.make_async_remote_copy(
                src_ref=_piece_rows(ins[n + a], k), dst_ref=_piece_rows(outs[n + a], k), send_sem=send_g.at[7 * a + m - 1],
                recv_sem=recv_g.at[7 * a + m - 1], device_id=_peer(x, y, c, m), device_id_type=MESH)

        for k in range(len(parts)):
            swap(k, c).start()
        for a in range(n_g):
            for m in range(1, 8):
                spread(a, m, (x, y, c)).start()
        for k in range(len(parts)):
            swap(k, c).wait_send()
            swap(k, 1 - c).wait_recv()
        for a in range(n_g):
            for m in range(1, 8):
                spread(a, m, (x, y, c)).wait_send()
                spread(a, m, _peer(x, y, c, m)).wait_recv()

    arrays = list(bufs) + list(gathered)
    return pl.pallas_call(
        body, name=name, in_specs=[ANY] * total, out_specs=[ANY] * total,
        out_shape=[jax.ShapeDtypeStruct(b.shape, F32) for b in arrays],
        input_output_aliases={a: a for a in range(total)},
        scratch_shapes=[pltpu.SemaphoreType.DMA((max(len(parts), 1),))] * 2
        + [pltpu.SemaphoreType.DMA((max(7 * n_g, 1),))] * 2,
    )(*arrays)


def _adamw_math(w, g, m, v):
    nm = ADAM_B1 * m + (1.0 - ADAM_B1) * g
    nv = ADAM_B2 * v + (1.0 - ADAM_B2) * (g * g)
    m_hat = nm / (1.0 - ADAM_B1 ** ADAM_STEP)
    v_hat = nv / (1.0 - ADAM_B2 ** ADAM_STEP)
    return -ADAM_LR * (m_hat / (jnp.sqrt(v_hat) + ADAM_EPS) + ADAM_WD * w), nm, nv


def _adamw(name, w, g, m, v, rows_per_step, first=0, count=None, dests=None, deps=()):
    layers, rows, cols = w.shape
    count = layers if count is None else count

    def body(w_ref, g_ref, m_ref, v_ref, *rest):
        d_ref, nm_ref, nv_ref, g_out_ref = rest[-4:]
        d_ref[...], nm_ref[...], nv_ref[...] = _adamw_math(w_ref[...], g_ref[...], m_ref[...], v_ref[...])
        g_out_ref[...] = g_ref[...]

    spec = pl.BlockSpec((1, rows_per_step, cols), lambda l, i: (first + l, i, 0))
    shape = jax.ShapeDtypeStruct(w.shape, F32)
    dests = () if dests is None else tuple(dests)
    return pl.pallas_call(
        body, name=name, grid=(count, rows // rows_per_step),
        in_specs=[spec] * 4 + [ANY] * (len(dests) + len(deps)), out_specs=[spec] * 4, out_shape=[shape] * 4,
        input_output_aliases={4 + k: k for k in range(len(dests))},
        compiler_params=_params(("arbitrary", "arbitrary")),
    )(w, g, m, v, *dests, *deps)


def _pack_misc(pool_scale, sinks, norm_pre, norm_post):
    sink_rows = jnp.zeros((DEPTH, 8, 128), F32).at[:, 0, 0:N_HEADS].set(sinks).reshape(2 * 8, 128)
    return jnp.concatenate([pool_scale.reshape(8, 128), norm_pre.reshape(16, 128), norm_post.reshape(16, 128),
                            sink_rows, jnp.zeros((8, 128), F32)], axis=0)


def _adamw_misc(w, g, m, v):
    def body(w_ref, g_ref, m_ref, v_ref, *rest):
        outs, (d_ref, nm_ref, nv_ref) = rest[:17], rest[17:]
        d_ref[...], nm_ref[...], nv_ref[...] = _adamw_math(w_ref[...], g_ref[...], m_ref[...], v_ref[...])
        for k, src in enumerate([g_ref, d_ref, nm_ref, nv_ref]):
            scale, sinks, pre, post = outs[4 * k:4 * k + 4]
            for l in range(DEPTH):
                for j in range(4):
                    scale[l:l + 1, j * 128:(j + 1) * 128] = src[MISC_SCALE + 4 * l + j:MISC_SCALE + 4 * l + j + 1, :]
                for j in range(8):
                    pre[l:l + 1, j * 128:(j + 1) * 128] = src[MISC_PRE + 8 * l + j:MISC_PRE + 8 * l + j + 1, :]
                    post[l:l + 1, j * 128:(j + 1) * 128] = src[MISC_POST + 8 * l + j:MISC_POST + 8 * l + j + 1, :]
                sinks[l:l + 1, :] = src[MISC_SINKS + 8 * l:MISC_SINKS + 8 * l + 1, 0:N_HEADS]
        outs[16][...] = g_ref[MISC_LOSS:MISC_LOSS + 1, 0:1]

    vmem = pl.BlockSpec(memory_space=pltpu.VMEM)
    shapes = [(DEPTH, D_POOL), (DEPTH, N_HEADS), (DEPTH, D), (DEPTH, D)] * 4 + [(1, 1)]
    return pl.pallas_call(
        body, name="adamw_misc", in_specs=[vmem] * 4, out_specs=[vmem] * 17,
        out_shape=[jax.ShapeDtypeStruct(s, F32) for s in shapes],
        scratch_shapes=[pltpu.VMEM((MISC_ROWS, 128), F32)] * 3,
    )(w, g, m, v)


def kernel(x, w_in, pool_w, pool_scale, attn_sinks, w_out, norm_pre, norm_post, loss_target, m_w_in, m_pool_w, m_pool_scale, m_attn_sinks, m_w_out, m_norm_pre, m_norm_post, v_w_in, v_pool_w, v_pool_scale, v_attn_sinks, v_w_out, v_norm_pre, v_norm_post):
    cx, cy, cc = _place()
    chip_arr = jnp.reshape(2 * cx + cy, (1,)).astype(jnp.int32)
    place_arr = jnp.stack([4 * cx + 2 * cy + cc, cc]).astype(jnp.int32)
    t = lambda a: jnp.transpose(a, (0, 2, 1))
    w_in_t = t(w_in)
    xs, target = x[0], loss_target[0]
    pool_w_b = pool_w.astype(BF16)
    tables = _attention_tables()
    scale3 = pool_scale.reshape(DEPTH, 1, D_POOL)
    pre3 = norm_pre.reshape(DEPTH, 1, D)
    post3 = norm_post.reshape(DEPTH, 1, D)

    wi = _place_cast("place_w_in", w_in_t, chip_arr, 288)
    wo = _place_cast("place_w_out", w_out, chip_arr, 256)
    send, recv, bufs, token = _gather_start([wi[0], wo[0], wi[1], wo[1]], halved=(0,))

    saved = []
    after = token
    for l in range(DEPTH):
        w_in_l = _gather_wait(f"gather_wait_in{l}", bufs[2 * l], send[2 * l], recv[2 * l], after, halved=(l == 0))
        if l == 0:
            w_in_l = _forward_halves("forward_w_in0", w_in_l)
        pu, pg, q, kv, ag = _fwd_in(l, xs, pre3, w_in_l)
        cat = _fwd_mix(l, pu, pg, q, kv, ag, pool_w_b, scale3, attn_sinks, tables)
        w_out_l = _gather_wait(f"gather_wait_out{l}", bufs[2 * l + 1], send[2 * l + 1], recv[2 * l + 1], cat)
        y, x_next = _fwd_out(l, cat, w_out_l, xs, post3) if l < DEPTH - 1 else (None, None)
        saved.append((xs, pu, pg, q, kv, ag, cat, y, w_in_l, w_out_l))
        if l < DEPTH - 1:
            xs = after = x_next

    x_in, pu, pg, q, kv, ag, cat, y, w_in_l, w_out_l = saved[1]
    dcat, dw_out1, dw_out1_b, dg_post1, loss, xs = _bwd_out(1, cat, w_out_l, post3, place_arr, x=x_in, target=target)
    dproj, dpw, dsc1, dsink1 = _bwd_mix(1, pu, pg, q, kv, ag, dcat, pool_w_b, scale3, attn_sinks, tables)
    dw_in1, dw_in1_b = _bwd_in_dw(1, dproj, x_in, pre3, place_arr)
    ex1 = _exchange_start("exchange_start_1", [dw_in1_b, dw_out1_b])
    dx, dg_pre1 = _bwd_in_dx(1, dproj, w_in_l, x_in, pre3, xs, deps=(ex1[4],))

    x_in, pu, pg, q, kv, ag, cat, y, w_in_l, w_out_l = saved[0]
    dcat, dw_out0, dw_out0_b, dg_post0 = _bwd_out(0, cat, w_out_l, post3, place_arr, dxn=dx, y=y)
    ex0_out = _exchange_start("exchange_start_out0", [dw_out0_b])
    dproj, dpw, dsc0, dsink0 = _bwd_mix(0, pu, pg, q, kv, ag, dcat, pool_w_b, scale3, attn_sinks, tables,
                                        deps=(ex0_out[4],), dpw_dest=dpw)
    recv_in1, recv_out1 = _exchange_wait("exchange_wait_1", ex1, dproj)
    g_in, g_out = _sum_pieces("sum_pieces_1", [dw_in1, dw_out1], [recv_in1, recv_out1], place_arr, 1)
    dw_in0, dw_in0_b = _bwd_in_dw(0, dproj, x_in, pre3, place_arr, deps=(g_in, g_out))
    ex0_in = _exchange_start("exchange_start_in0", [dw_in0_b])

    grad_x, dg_pre0 = _bwd_in_dx(0, dproj, w_in_l, x_in, pre3, dx, deps=(ex0_in[4],))
    small = [dpw.reshape(DEPTH * 4 * 128, 128),
             jnp.concatenate([dsc0, dsc1, dg_pre0, dg_pre1, dg_post0, dg_post1, dsink0, dsink1, loss], axis=0)]
    ex_small = _exchange_start("exchange_start_small", small)
    (recv_out0,) = _exchange_wait("exchange_wait_out0", ex0_out, ex_small[4])
    (g_out,) = _sum_pieces("sum_pieces_out0", [dw_out0], [recv_out0], place_arr, 0, dests=[g_out])
    g_in, g_out = _share("share_a", [g_in, g_out], [(0, 1), (1, 0), (1, 1)])
    m_in_t, v_in_t = t(m_w_in), t(v_w_in)
    d_out, nm_out, nv_out, grad_w_out = _adamw("adamw_w_out", w_out, g_out, m_w_out, v_w_out, 256)
    upd_in = _adamw("adamw_w_in1", w_in_t, g_in, m_in_t, v_in_t, 288, first=1, count=1, deps=(d_out,))

    (recv_in0,) = _exchange_wait("exchange_wait_in0", ex0_in, upd_in[0])
    recv_small = _exchange_wait("exchange_wait_small", ex_small, recv_in0)
    (g_in,) = _sum_pieces("sum_pieces_in0", [dw_in0], [recv_in0], place_arr, 0, dests=[g_in])
    g_in, g_pw, g_misc = _share("share_b", [g_in], [(0, 0)], _sum_small(small, recv_small, place_arr))
    d_in, nm_in, nv_in, grad_w_in_t = _adamw("adamw_w_in0", w_in_t, g_in, m_in_t, v_in_t, 288, first=0, count=1,
                                             dests=upd_in)
    flat = lambda a: a.reshape(1, DEPTH * 4 * 128, 128)
    pw = _adamw("adamw_pool_w", flat(pool_w), flat(g_pw), flat(m_pool_w), flat(v_pool_w), 1024)
    d_pw, m_pw, v_pw, g_pw = [a.reshape(pool_w.shape) for a in pw]
    misc = _adamw_misc(_pack_misc(pool_scale, attn_sinks, norm_pre, norm_post), g_misc,
                       _pack_misc(m_pool_scale, m_attn_sinks, m_norm_pre, m_norm_post),
                       _pack_misc(v_pool_scale, v_attn_sinks, v_norm_pre, v_norm_post))
    (g_sc, g_sk, g_pre, g_post, d_sc, d_sk, d_pre, d_post,
     m_sc, m_sk, m_pre, m_post, v_sc, v_sk, v_pre, v_post, loss_sum) = misc
    return (loss_sum[0, 0], grad_x[None], t(grad_w_in_t), g_pw, g_sc, g_sk, grad_w_out, g_pre, g_post,
            t(d_in), d_pw, d_sc, d_sk, d_out, d_pre, d_post,
            t(nm_in), m_pw, m_sc, m_sk, nm_out, m_pre, m_post,
            t(nv_in), v_pw, v_sc, v_sk, nv_out, v_pre, v_post)
```

```python
import jax
import jax.numpy as jnp
from jax import lax
from jax.experimental import pallas as pl
from jax.experimental.pallas import tpu as pltpu

F32 = jnp.float32
BF16 = jnp.bfloat16

S = 2048
D = 1024
DEPTH = 2
D_POOL = 512
POOL_WINDOWS = (2, 4, 8, 16)
N_HEADS = 8
D_IN = 2304
N_SHARDS = 4
W_IN_SHARD = D_IN // N_SHARDS
W_OUT_SHARD = D // N_SHARDS
BLK = 128
NB = S // BLK
HALO = 16
PAD = 8
EPS = 1e-6
NEG_INF = -1e30
C_PU, C_PG, C_Q, C_K, C_V, C_AG = 0, 512, 1024, 1536, 1664, 1792

ADAM_LR = 0.001
ADAM_B1 = 0.9
ADAM_B2 = 0.999
ADAM_EPS = 1e-08
ADAM_WD = 0.01
ADAM_STEP = 10

TM = 512
VMEM_LIMIT = 56 * 1024 * 1024

NT = (((1,), (1,)), ((), ()))
TN = (((0,), (0,)), ((), ()))

MESH = pl.DeviceIdType.MESH
ANY = pl.BlockSpec(memory_space=pl.ANY)

MISC_SCALE, MISC_PRE, MISC_POST, MISC_SINKS, MISC_LOSS = 0, 8, 24, 40, 56
MISC_ROWS = 64


def _params(sem=("arbitrary",)):
    return pltpu.CompilerParams(dimension_semantics=sem, vmem_limit_bytes=VMEM_LIMIT)


def _sigmoid(v):
    return 1.0 / (1.0 + jnp.exp(-v))


def _rows8(v):
    r, c = v.shape
    return v.reshape(r // 8, 8, c).sum(axis=0)


def _layer(l, *shape):
    zeros = (0,) * len(shape)
    return pl.BlockSpec((None,) + shape, lambda i: (l,) + zeros)


def _whole(shape):
    zeros = (0,) * len(shape)
    return pl.BlockSpec(shape, lambda i: zeros, pipeline_mode=pl.Buffered(1))


def _fwd_in(l, x, g_pre, w_in_t):
    def body(x_ref, g_ref, w_ref, pu_ref, pg_ref, q_ref, kv_ref, ag_ref):
        xt = x_ref[...]
        r = lax.rsqrt(jnp.mean(xt * xt, axis=-1, keepdims=True) + EPS)
        h = (xt * r * g_ref[...]).astype(BF16)

        def proj(lo, hi):
            return lax.dot_general(h, w_ref[lo:hi, :], NT, preferred_element_type=F32)

        pu_ref[...] = proj(C_PU, C_PG)
        pg_ref[...] = proj(C_PG, C_Q)
        q_ref[...] = proj(C_Q, C_K).astype(BF16)
        kv_ref[...] = proj(C_K, C_AG).astype(BF16)
        ag_ref[...] = proj(C_AG, D_IN)

    row = lambda w: pl.BlockSpec((TM, w), lambda i: (i, 0))
    return pl.pallas_call(
        body, name="fwd_in", grid=(S // TM,),
        in_specs=[row(D), _layer(l, 1, D), _whole((D_IN, D))],
        out_specs=[row(512), row(512), row(512), row(256), row(512)],
        out_shape=[jax.ShapeDtypeStruct((S, 512), F32), jax.ShapeDtypeStruct((S, 512), F32),
                   jax.ShapeDtypeStruct((S, 512), BF16), jax.ShapeDtypeStruct((S, 256), BF16),
                   jax.ShapeDtypeStruct((S, 512), F32)],
        compiler_params=_params(),
    )(x, g_pre, w_in_t)


LOG2E = 1.4426950408889634
SCORE_SCALE = 0.125 * LOG2E


def _attention_tables():
    qi = jnp.arange(BLK)[:, None]
    kj = jnp.arange(BLK)[None, :]
    dist = ((qi - kj) % BLK).astype(F32)
    slopes = jnp.exp2(-jnp.arange(1, N_HEADS + 1, dtype=F32))
    bias = -(slopes * LOG2E)[:, None, None] * dist[None]
    first = jnp.where(kj > qi, NEG_INF, bias)
    return jnp.stack([first, bias]), (kj <= qi).astype(BF16)


def _own_block_mask():
    return lax.broadcasted_iota(jnp.int32, (BLK, BLK), 1) <= lax.broadcasted_iota(jnp.int32, (BLK, BLK), 0)


def _merge(full, own):
    return jnp.where(own, full[:, BLK:], full[:, :BLK])


def _spread(v, tri):
    own = v * tri
    return jnp.concatenate([v - own, own], axis=1)


def _head_variants(cur, prev):
    both = jnp.concatenate([prev, cur], axis=0).astype(F32)
    swapped = pltpu.roll(both, 64, axis=1)
    low = lax.broadcasted_iota(jnp.int32, both.shape, 1) < 64
    zero = jnp.zeros_like(both)
    return ((jnp.where(low, both, zero).astype(BF16), jnp.where(low, zero, swapped).astype(BF16)),
            (jnp.where(low, swapped, zero).astype(BF16), jnp.where(low, zero, both).astype(BF16)))


def _head_of(hkv, t, half):
    return hkv * 4 + 2 * t + half


def _rows(v, t):
    return v[t * BLK:(t + 1) * BLK]


def _stack_tiles(ref, hkv, offset=0):
    lo = offset + 2 * hkv * 128
    return jnp.concatenate([ref[:, lo:lo + 128], ref[:, lo + 128:lo + 256]], axis=0)


def _scores(q2, k_var, own):
    s = {}
    for hkv in range(2):
        for half in range(2):
            full = lax.dot_general(q2[hkv], k_var[hkv][half], NT, preferred_element_type=F32)
            for t in range(2):
                s[hkv, t, half] = _merge(_rows(full, t), own)
    return s


def _softmax(s, bias, sink):
    s = s * SCORE_SCALE + bias
    sink2 = sink * LOG2E
    m = jnp.maximum(jnp.max(s, axis=-1, keepdims=True), sink2)
    p = jnp.exp2(s - m)
    e_sink = jnp.exp2(sink2 - m)
    inv = 1.0 / (jnp.sum(p, axis=-1, keepdims=True) + e_sink)
    return p * inv, e_sink * inv


def _spread_pair(v, hkv, half, tri):
    return jnp.concatenate([_spread(v[hkv, t, half].astype(BF16), tri) for t in range(2)], axis=0)


POOL_ROWS = PAD + HALO + BLK


def _window_sums(src_ref, tmp_refs, trailing):
    lo, hi = (PAD, POOL_ROWS) if trailing else (0, HALO + BLK)
    cur = src_ref
    for level in range(len(POOL_WINDOWS)):
        lanes = slice(level * 128, 512)
        shift = -(1 << level) if trailing else (1 << level)
        dst = tmp_refs[level % 2]
        dst[lo:hi, lanes] = cur[lo:hi, lanes] + cur[lo + shift:hi + shift, lanes]
        cur = dst


def _pool_block(ext_ref, tmp_refs, i, g, w):
    lanes = slice(g * 128, (g + 1) * 128)
    rows = slice(PAD + HALO, POOL_ROWS)
    t = (i * BLK + lax.broadcasted_iota(jnp.int32, (BLK, 1), 0)).astype(F32)
    inv = 1.0 / jnp.minimum(t + 1.0, float(w))
    return tmp_refs[g % 2][rows, lanes] * inv - ext_ref[rows, lanes], inv


def _fwd_mix(l, pu, pg, q, kv, ag, pool_w, pool_scale, sinks, tables):
    bias, tri = tables

    def body(pu_ref, pup_ref, pg_ref, q_ref, kv_ref, kvp_ref, ag_ref, pw_ref, sc_ref, sink_ref, bias_ref, tri_ref,
             cat_ref, ext_ref, *tmp_refs):
        i = pl.program_id(0)

        @pl.when(i == 0)
        def _():
            for ref in (ext_ref, *tmp_refs):
                ref[0:PAD, :] = jnp.zeros((PAD, 512), F32)

        ext_ref[PAD:PAD + HALO, :] = jnp.where(i > 0, pup_ref[...], 0.0)
        ext_ref[PAD + HALO:POOL_ROWS, :] = pu_ref[...]
        _window_sums(ext_ref, tmp_refs, True)
        for g, w in enumerate(POOL_WINDOWS):
            lanes = slice(g * 128, (g + 1) * 128)
            pooled, _ = _pool_block(ext_ref, tmp_refs, i, g, w)
            mixed = jnp.dot(pooled.astype(BF16), pw_ref[g], preferred_element_type=F32)
            gate = pg_ref[:, lanes]
            cat_ref[:, lanes] = (mixed * sc_ref[:, lanes] * (gate * _sigmoid(gate))).astype(BF16)

        own = _own_block_mask()
        tri = tri_ref[...]
        k_var = _head_variants(kv_ref[:, 0:128], kvp_ref[:, 0:128])
        v_var = _head_variants(kv_ref[:, 128:256], kvp_ref[:, 128:256])
        s = _scores([_stack_tiles(q_ref, hkv) for hkv in range(2)], k_var, own)
        p = {}
        for (hkv, t, half), s_head in s.items():
            head = _head_of(hkv, t, half)
            p[hkv, t, half], _ = _softmax(s_head, bias_ref[head], sink_ref[l, head])
        for hkv in range(2):
            o2 = jnp.zeros((2 * BLK, 128), F32)
            for half in range(2):
                o2 = o2 + jnp.dot(_spread_pair(p, hkv, half, tri), v_var[hkv][half], preferred_element_type=F32)
            for t in range(2):
                lo = (2 * hkv + t) * 128
                gate = ag_ref[:, lo:lo + 128]
                cat_ref[:, D_POOL + lo:D_POOL + lo + 128] = (_rows(o2, t) * (gate * _sigmoid(gate))).astype(BF16)

    blk = lambda w: pl.BlockSpec((BLK, w), lambda i: (i, 0))
    prev = lambda w: pl.BlockSpec((BLK, w), lambda i: (jnp.maximum(i - 1, 0), 0))
    halo = pl.BlockSpec((HALO, 512), lambda i: (jnp.maximum(i * (BLK // HALO) - 1, 0), 0))
    return pl.pallas_call(
        body, name="fwd_mix", grid=(NB,),
        in_specs=[blk(512), halo, blk(512), blk(512), blk(256), prev(256), blk(512),
                  _layer(l, 4, 128, 128), _layer(l, 1, 512), pl.BlockSpec(memory_space=pltpu.SMEM),
                  pl.BlockSpec((None, N_HEADS, BLK, BLK), lambda i: (jnp.minimum(i, 1), 0, 0, 0)), _whole((BLK, BLK))],
        out_specs=blk(D),
        out_shape=jax.ShapeDtypeStruct((S, D), BF16),
        scratch_shapes=[pltpu.VMEM((POOL_ROWS, 512), F32)] * 3,
        compiler_params=_params(),
    )(pu, pu, pg, q, kv, kv, ag, pool_w, pool_scale, sinks, bias, tri)


def _fwd_out(l, cat, w_out, x, g_post):
    def body(cat_ref, w_ref, x_ref, g_ref, y_ref, xn_ref):
        y = jnp.dot(cat_ref[...], w_ref[...], preferred_element_type=F32)
        y_ref[...] = y
        r = lax.rsqrt(jnp.mean(y * y, axis=-1, keepdims=True) + EPS)
        xn_ref[...] = x_ref[...] + y * r * g_ref[...]

    row = lambda: pl.BlockSpec((TM, D), lambda i: (i, 0))
    act = jax.ShapeDtypeStruct((S, D), F32)
    return pl.pallas_call(
        body, name="fwd_out", grid=(S // TM,),
        in_specs=[row(), _whole((D, D)), row(), _layer(l, 1, D)], out_specs=[row(), row()], out_shape=[act, act],
        compiler_params=_params(),
    )(cat, w_out, x, g_post)


def _store_lane_rows(ref, acc):
    total = jnp.sum(acc, axis=0, keepdims=True)
    for k in range(ref.shape[0]):
        ref[k:k + 1, :] = total[:, k * 128:(k + 1) * 128]


def _own_piece(dw_ref, place_ref):
    p = dw_ref.shape[0] // 8
    return dw_ref[pl.ds(pl.multiple_of(place_ref[0] * p, 8), p), :]


def _bwd_out(l, cat, w_out, g_post, place_arr, dxn=None, y=None, x=None, target=None):
    last = target is not None
    n_steps = S // TM

    def body(a_ref, b_ref, g_ref, cat_ref, w_ref, place_ref, dcat_ref, own_ref, dwb_ref, dg_ref, *rest):
        acc_ref, dw_ref = rest[-2:]
        step = pl.program_id(0)

        @pl.when(step == 0)
        def _():
            dw_ref[...] = jnp.zeros_like(dw_ref)
            acc_ref[...] = jnp.zeros_like(acc_ref)

        cat = cat_ref[...]
        g = g_ref[...]
        y = jnp.dot(cat, w_ref[...], preferred_element_type=F32) if last else b_ref[...]
        r = lax.rsqrt(jnp.mean(y * y, axis=-1, keepdims=True) + EPS)
        if last:
            loss_ref, dx_ref, loss_acc_ref = rest[:3]
            err = a_ref[...] + y * r * g - b_ref[...]

            @pl.when(step == 0)
            def _():
                loss_acc_ref[...] = jnp.zeros_like(loss_acc_ref)

            loss_acc_ref[...] += _rows8(err * err)
            dz = err * (1.0 / D)
            dx_ref[...] = dz
        else:
            dz = a_ref[...]
        a = dz * g
        dy = r * a - y * (r * r * r) * jnp.mean(a * y, axis=-1, keepdims=True)
        acc_ref[...] += _rows8(dz * (y * r))
        dyb = dy.astype(BF16)
        dcat_ref[...] = lax.dot_general(dyb, w_ref[...], NT, preferred_element_type=F32)
        dw_ref[...] += lax.dot_general(cat, dyb, TN, preferred_element_type=F32)

        @pl.when(step == n_steps - 1)
        def _():
            _store_lane_rows(dg_ref, acc_ref[...])
            dwb_ref[...] = dw_ref[...].astype(BF16)
            own_ref[...] = _own_piece(dw_ref, place_ref)
            if last:
                loss_ref[...] = jnp.full((8, 128), (0.5 / D) * jnp.sum(loss_acc_ref[...]), F32)

    row = lambda: pl.BlockSpec((TM, D), lambda i: (i, 0))
    full = _whole
    return pl.pallas_call(
        body, name="out_loss_bwd" if last else "bwd_out", grid=(n_steps,),
        in_specs=[row(), row(), _layer(l, 1, D), row(), full((D, D)), pl.BlockSpec(memory_space=pltpu.SMEM)],
        out_specs=[row(), full((D // 8, D)), full((D, D)), full((8, 128))] + ([full((8, 128)), row()] if last else []),
        out_shape=[jax.ShapeDtypeStruct((S, D), F32), jax.ShapeDtypeStruct((D // 8, D), F32),
                   jax.ShapeDtypeStruct((D, D), BF16), jax.ShapeDtypeStruct((8, 128), F32)]
        + ([jax.ShapeDtypeStruct((8, 128), F32), jax.ShapeDtypeStruct((S, D), F32)] if last else []),
        scratch_shapes=([pltpu.VMEM((8, D), F32)] if last else []) + [pltpu.VMEM((8, D), F32), pltpu.VMEM((D, D), F32)],
        compiler_params=_params(),
    )(*((x, target) if last else (dxn, y)), g_post, cat, w_out, place_arr)


def _bwd_mix(l, pu, pg, q, kv, ag, dcat, pool_w, pool_scale, sinks, tables, deps=(), dpw_dest=None):
    bias, tri = tables
    deps = tuple(deps) + (() if dpw_dest is None else (dpw_dest,))

    def body(pu_ref, pup_ref, pg_ref, q_ref, kv_ref, kvp_ref, ag_ref, dcat_ref, pw_ref, sc_ref, sink_ref, bias_ref,
             tri_ref, *rest):
        dproj_ref, dpw_ref, dsc_ref, dsink_ref, ext_ref, dext_ref, tmp_a, tmp_b, dkv_ref = rest[len(deps):]
        tmp_refs = (tmp_a, tmp_b)
        step = pl.program_id(0)
        i = NB - 1 - step

        @pl.when(step == 0)
        def _():
            dpw_ref[...] = jnp.zeros_like(dpw_ref)
            dsc_ref[...] = jnp.zeros_like(dsc_ref)
            dsink_ref[...] = jnp.zeros_like(dsink_ref)
            for ref in (ext_ref, tmp_a, tmp_b):
                ref[0:PAD, :] = jnp.zeros((PAD, 512), F32)
            dext_ref[BLK:POOL_ROWS, :] = jnp.zeros((HALO + PAD, 512), F32)
            dkv_ref[...] = jnp.zeros_like(dkv_ref)

        ext_ref[PAD:PAD + HALO, :] = jnp.where(i > 0, pup_ref[...], 0.0)
        ext_ref[PAD + HALO:POOL_ROWS, :] = pu_ref[...]
        _window_sums(ext_ref, tmp_refs, True)
        dpooled = []
        for g, w in enumerate(POOL_WINDOWS):
            lanes = slice(g * 128, (g + 1) * 128)
            pooled, inv = _pool_block(ext_ref, tmp_refs, i, g, w)
            pooled_b = pooled.astype(BF16)
            mixed = jnp.dot(pooled_b, pw_ref[g], preferred_element_type=F32)
            scale = sc_ref[:, lanes]
            gate = pg_ref[:, lanes]
            sg = _sigmoid(gate)
            dpo = dcat_ref[:, lanes]
            dproj_ref[:, C_PG + g * 128:C_PG + (g + 1) * 128] = (
                dpo * (mixed * scale) * (sg * (1.0 + gate * (1.0 - sg)))).astype(BF16)
            dms = dpo * (gate * sg)
            dsc_ref[g:g + 1, :] += jnp.sum(dms * mixed, axis=0, keepdims=True)
            dmixed = (dms * scale).astype(BF16)
            dpw_ref[g] += lax.dot_general(pooled_b, dmixed, TN, preferred_element_type=F32)
            dpooled.append(lax.dot_general(dmixed, pw_ref[g], NT, preferred_element_type=F32))
            dext_ref[0:BLK, lanes] = dpooled[g] * inv
        _window_sums(dext_ref, tmp_refs, False)
        for g in range(len(POOL_WINDOWS)):
            lanes = slice(g * 128, (g + 1) * 128)
            dproj_ref[:, C_PU + g * 128:C_PU + (g + 1) * 128] = (tmp_refs[g % 2][0:BLK, lanes] - dpooled[g]).astype(BF16)
        dext_ref[BLK:BLK + HALO, :] = dext_ref[0:HALO, :]

        own = _own_block_mask()
        tri = tri_ref[...]
        k_var = _head_variants(kv_ref[:, 0:128], kvp_ref[:, 0:128])
        v_var = _head_variants(kv_ref[:, 128:256], kvp_ref[:, 128:256])
        q2 = [_stack_tiles(q_ref, hkv) for hkv in range(2)]
        s = _scores(q2, k_var, own)
        p, p_sink = {}, {}
        for key, s_head in s.items():
            head = _head_of(*key)
            p[key], p_sink[key] = _softmax(s_head, bias_ref[head], sink_ref[l, head])

        do2, p_b, dp = [], {}, {}
        for hkv in range(2):
            gate = _stack_tiles(ag_ref, hkv)
            sg = _sigmoid(gate)
            dca = _stack_tiles(dcat_ref, hkv, D_POOL)
            do2.append((dca * (gate * sg)).astype(BF16))
            o2 = jnp.zeros((2 * BLK, 128), F32)
            for half in range(2):
                p_b[hkv, half] = _spread_pair(p, hkv, half, tri)
                o2 = o2 + jnp.dot(p_b[hkv, half], v_var[hkv][half], preferred_element_type=F32)
                full = lax.dot_general(do2[hkv], v_var[hkv][half], NT, preferred_element_type=F32)
                for t in range(2):
                    dp[hkv, t, half] = _merge(_rows(full, t), own)
            dag = dca * o2 * (sg * (1.0 + gate * (1.0 - sg)))
            for t in range(2):
                lo = C_AG + (2 * hkv + t) * 128
                dproj_ref[:, lo:lo + 128] = _rows(dag, t).astype(BF16)

        ds = {}
        for key in p:
            delta = jnp.sum(p[key] * dp[key], axis=-1, keepdims=True)
            ds[key] = p[key] * (dp[key] - delta)
            head = _head_of(*key)
            dsink_ref[0:1, :] += jnp.where(lax.broadcasted_iota(jnp.int32, (1, 128), 1) == head,
                                           -jnp.sum(p_sink[key] * delta, axis=0, keepdims=True), 0.0)

        dk_acc = [[None, None], [None, None]]
        dv_acc = [[None, None], [None, None]]
        for hkv in range(2):
            dq2 = jnp.zeros((2 * BLK, 128), F32)
            for half in range(2):
                ds_b = _spread_pair(ds, hkv, half, tri)
                dq2 = dq2 + jnp.dot(ds_b, k_var[hkv][half], preferred_element_type=F32)
                dk_acc[hkv][half] = lax.dot_general(ds_b, q2[hkv], TN, preferred_element_type=F32)
                dv_acc[hkv][half] = lax.dot_general(p_b[hkv, half], do2[hkv], TN, preferred_element_type=F32)
            for t in range(2):
                lo = C_Q + (2 * hkv + t) * 128
                dproj_ref[:, lo:lo + 128] = (_rows(dq2, t) * 0.125).astype(BF16)

        low = lax.broadcasted_iota(jnp.int32, (2 * BLK, 128), 1) < 64

        def gather_heads(acc):
            return jnp.where(low, acc[0][0] + pltpu.roll(acc[0][1], 64, axis=1),
                             pltpu.roll(acc[1][0], 64, axis=1) + acc[1][1])

        dk = gather_heads(dk_acc) * 0.125
        dv = gather_heads(dv_acc)
        dproj_ref[:, C_K:C_V] = (dk[BLK:, :] + dkv_ref[:, 0:128]).astype(BF16)
        dproj_ref[:, C_V:C_AG] = (dv[BLK:, :] + dkv_ref[:, 128:256]).astype(BF16)
        dkv_ref[:, 0:128] = dk[:BLK, :]
        dkv_ref[:, 128:256] = dv[:BLK, :]

    rev = lambda w: pl.BlockSpec((BLK, w), lambda s: (NB - 1 - s, 0))
    prev = lambda w: pl.BlockSpec((BLK, w), lambda s: (jnp.maximum(NB - 2 - s, 0), 0))
    halo = pl.BlockSpec((HALO, 512), lambda s: (jnp.maximum((NB - 1 - s) * (BLK // HALO) - 1, 0), 0))
    return pl.pallas_call(
        body, name="bwd_mix", grid=(NB,),
        in_specs=[rev(512), halo, rev(512), rev(512), rev(256), prev(256), rev(512), rev(D),
                  _layer(l, 4, 128, 128), _layer(l, 1, 512), pl.BlockSpec(memory_space=pltpu.SMEM),
                  pl.BlockSpec((None, N_HEADS, BLK, BLK), lambda s: (jnp.minimum(NB - 1 - s, 1), 0, 0, 0)),
                  _whole((BLK, BLK))] + [ANY] * len(deps),
        out_specs=[rev(D_IN), _layer(l, 4, 128, 128),
                   pl.BlockSpec((4, 128), lambda s: (0, 0)), pl.BlockSpec((8, 128), lambda s: (0, 0))],
        out_shape=[jax.ShapeDtypeStruct((S, D_IN), BF16), jax.ShapeDtypeStruct((DEPTH, 4, 128, 128), F32),
                   jax.ShapeDtypeStruct((4, 128), F32), jax.ShapeDtypeStruct((8, 128), F32)],
        input_output_aliases={} if dpw_dest is None else {12 + len(deps): 1},
        scratch_shapes=[pltpu.VMEM((POOL_ROWS, 512), F32)] * 4 + [pltpu.VMEM((BLK, 256), F32)],
        compiler_params=_params(),
    )(pu, pu, pg, q, kv, kv, ag, dcat, pool_w, pool_scale, sinks, bias, tri, *deps)


def _bwd_in_dw(l, dproj, x, g_pre, place_arr, deps=()):
    n_steps = S // TM

    def body(dp_ref, x_ref, g_ref, place_ref, *rest):
        own_ref, dwb_ref, dw_ref = rest[len(deps):]
        step = pl.program_id(0)

        @pl.when(step == 0)
        def _():
            dw_ref[...] = jnp.zeros_like(dw_ref)

        xt = x_ref[...]
        r = lax.rsqrt(jnp.mean(xt * xt, axis=-1, keepdims=True) + EPS)
        h = (xt * r * g_ref[...]).astype(BF16)
        dw_ref[...] += lax.dot_general(dp_ref[...], h, TN, preferred_element_type=F32)

        @pl.when(step == n_steps - 1)
        def _():
            dwb_ref[...] = dw_ref[...].astype(BF16)
            own_ref[...] = _own_piece(dw_ref, place_ref)

    row = lambda w: pl.BlockSpec((TM, w), lambda i: (i, 0))
    full = _whole
    return pl.pallas_call(
        body, name="bwd_in_dw", grid=(n_steps,),
        in_specs=[row(D_IN), row(D), _layer(l, 1, D), pl.BlockSpec(memory_space=pltpu.SMEM)] + [ANY] * len(deps),
        out_specs=[full((D_IN // 8, D)), full((D_IN, D))],
        out_shape=[jax.ShapeDtypeStruct((D_IN // 8, D), F32), jax.ShapeDtypeStruct((D_IN, D), BF16)],
        scratch_shapes=[pltpu.VMEM((D_IN, D), F32)],
        compiler_params=_params(),
    )(dproj, x, g_pre, place_arr, *deps)


def _bwd_in_dx(l, dproj, w_in_t, x, g_pre, dres, deps=()):
    n_steps = S // TM

    def body(dp_ref, w_ref, x_ref, g_ref, dres_ref, *rest):
        dx_ref, dg_ref, acc_ref = rest[len(deps):]
        step = pl.program_id(0)

        @pl.when(step == 0)
        def _():
            acc_ref[...] = jnp.zeros_like(acc_ref)

        dh = jnp.dot(dp_ref[...], w_ref[...], preferred_element_type=F32)
        xt = x_ref[...]
        r = lax.rsqrt(jnp.mean(xt * xt, axis=-1, keepdims=True) + EPS)
        xn = xt * r
        acc_ref[...] += _rows8(dh * xn)
        a = dh * g_ref[...]
        dx_ref[...] = dres_ref[...] + (r * a - xt * (r * r * r) * jnp.mean(a * xt, axis=-1, keepdims=True))

        @pl.when(step == n_steps - 1)
        def _():
            _store_lane_rows(dg_ref, acc_ref[...])

    row = lambda w: pl.BlockSpec((TM, w), lambda i: (i, 0))
    full = _whole
    return pl.pallas_call(
        body, name="bwd_in_dx", grid=(n_steps,),
        in_specs=[row(D_IN), full((D_IN, D)), row(D), _layer(l, 1, D), row(D)] + [ANY] * len(deps),
        out_specs=[row(D), full((8, 128))],
        out_shape=[jax.ShapeDtypeStruct((S, D), F32), jax.ShapeDtypeStruct((8, 128), F32)],
        scratch_shapes=[pltpu.VMEM((8, D), F32)],
        compiler_params=_params(),
    )(dproj, w_in_t, x, g_pre, dres, *deps)


HBM =pl.BlockSpec(memory_space=pltpu.HBM)
SEM = pl.BlockSpec(memory_space=pltpu.SEMAPHORE)
SPLIT_COPY = pltpu.CompilerParams(has_side_effects=pltpu.SideEffectType.DATAFLOW_SIDE_EFFECTING)


def _in_hbm(a):
    return pltpu.with_memory_space_constraint(a, pltpu.HBM)

def _place():
    return lax.axis_index("x"), lax.axis_index("y"), lax.axis_index("c")


def _other_chips(x, y):
    return [(1 - x, y), (x, 1 - y), (1 - x, 1 - y)]


def _peer(x, y, c, m):
    return (x ^ (m >> 2), y ^ ((m >> 1) & 1), c ^ (m & 1))


def _place_cast(name, src, chip_arr, tile, layers):
    _, n, cols = src.shape
    steps = n // tile
    k = len(layers)

    def body(chip_ref, *refs):
        for s_ref, o_ref in zip(refs[:k], refs[k:]):
            o_ref[...] = s_ref[...].astype(BF16)

    def layer_spec(l):
        return pl.BlockSpec((None, tile, cols), lambda i, chip: (l, i, 0))

    return pl.pallas_call(
        body, name=name,
        grid_spec=pltpu.PrefetchScalarGridSpec(
            num_scalar_prefetch=1, grid=(steps,),
            in_specs=[layer_spec(l) for l in layers],
            out_specs=[pl.BlockSpec((tile, cols), lambda i, chip: (chip[0] * steps + i, 0))] * k),
        out_shape=[jax.ShapeDtypeStruct((N_SHARDS * n, cols), BF16)] * k,
        compiler_params=_params(),
    )(chip_arr, *[src] * k)


def _chip_rows(ref, chip, half=None):
    n = ref.shape[0] // N_SHARDS
    if half is None:
        return ref.at[pl.ds(pl.multiple_of(chip * n, 16), n), :]
    return ref.at[pl.ds(pl.multiple_of(chip * n + half * (n // 2), 16), n // 2), :]


def _gather_start(name, bufs, halved):
    n = len(bufs)

    def body(*refs):
        ins, send, recv, token = refs[:n], refs[n:2 * n], refs[2 * n:3 * n], refs[-1]
        x, y, c = _place()
        for a, buf in enumerate(ins):
            own = _chip_rows(buf, 2 * x + y, c if a in halved else None)
            for j, chip in enumerate(_other_chips(x, y)):
                pltpu.make_async_remote_copy(src_ref=own, dst_ref=own, send_sem=send[a].at[j], recv_sem=recv[a].at[j],
                                             device_id=(*chip, c), device_id_type=MESH).start()
        token[...] = jnp.zeros_like(token)

    outs = pl.pallas_call(
        body, name=name, in_specs=[HBM] * n,
        out_specs=[SEM] * (2 * n) + [HBM] * n + [pl.BlockSpec(memory_space=pltpu.VMEM)],
        out_shape=[pltpu.SemaphoreType.DMA((3,))] * (2 * n) + [pltpu.HBM(b.shape, b.dtype) for b in bufs]
        + [jax.ShapeDtypeStruct((8, 128), F32)],
        input_output_aliases={a: 2 * n + a for a in range(n)},
        compiler_params=SPLIT_COPY,
    )(*[_in_hbm(b) for b in bufs])
    return outs[:n], outs[n:2 * n], outs[2 * n:3 * n], outs[-1]


def _gather_wait(name, buf, send_sem, recv_sem, after, halved=False):
    def body(buf_ref, send_ref, recv_ref, *rest):
        x, y, c = _place()
        half = c if halved else None
        own = _chip_rows(buf_ref, 2 * x + y, half)
        for j, chip in enumerate(_other_chips(x, y)):
            copy = pltpu.make_async_remote_copy(src_ref=own, dst_ref=_chip_rows(buf_ref, 2 * chip[0] + chip[1], half),
                                                send_sem=send_ref.at[j], recv_sem=recv_ref.at[j],
                                                device_id=(*chip, c), device_id_type=MESH)
            copy.wait_send()
            copy.wait_recv()

    return pl.pallas_call(
        body, name=name, in_specs=[HBM, SEM, SEM] + [ANY] * len(after), out_specs=HBM,
        out_shape=pltpu.HBM(buf.shape, buf.dtype), input_output_aliases={0: 0}, compiler_params=SPLIT_COPY,
    )(buf, send_sem, recv_sem, *after)


def _forward_halves(name, buf):
    def body(in_ref, out_ref, send_sems, recv_sems):
        x, y, c = _place()

        def copy(j, chip, half):
            rows = 2 * chip[0] + chip[1]
            return pltpu.make_async_remote_copy(
                src_ref=_chip_rows(in_ref, rows, half), dst_ref=_chip_rows(out_ref, rows, half), send_sem=send_sems.at[j],
                recv_sem=recv_sems.at[j], device_id=(x, y, 1 - c), device_id_type=MESH)

        chips = _other_chips(x, y)
        for j, chip in enumerate(chips):
            copy(j, chip, c).start()
        for j, chip in enumerate(chips):
            copy(j, chip, c).wait_send()
            copy(j, chip, 1 - c).wait_recv()

    return pl.pallas_call(
        body, name=name, in_specs=[ANY], out_specs=ANY, out_shape=jax.ShapeDtypeStruct(buf.shape, buf.dtype),
        input_output_aliases={0: 0},
        scratch_shapes=[pltpu.SemaphoreType.DMA((3,))] * 2,
    )(buf)


def _piece_rows(ref, k):
    p = ref.shape[0] // 8
    return ref.at[pl.ds(pl.multiple_of(k * p, 32 // jnp.dtype(ref.dtype).itemsize), p), :]


def _exchange_start(name, arrays):
    n = len(arrays)
    zones = [lax.empty((7, a.shape[0] // 8, a.shape[1]), a.dtype) for a in arrays]

    def body(*refs):
        srcs, lands = refs[:n], refs[n:2 * n]
        send, recv, token = refs[2 * n:3 * n], refs[3 * n:4 * n], refs[-1]
        x, y, c = _place()
        for a, (src, land) in enumerate(zip(srcs, lands)):
            for m in range(1, 8):
                px, py, pc = _peer(x, y, c, m)
                pltpu.make_async_remote_copy(
                    src_ref=_piece_rows(src, 4 * px + 2 * py + pc), dst_ref=land.at[m - 1], send_sem=send[a].at[m - 1],
                    recv_sem=recv[a].at[m - 1], device_id=(px, py, pc), device_id_type=MESH).start()
        token[...] = jnp.zeros_like(token)

    outs = pl.pallas_call(
        body, name=name, in_specs=[HBM] * (2 * n),
        out_specs=[SEM] * (2 * n) + [HBM] * (2 * n) + [pl.BlockSpec(memory_space=pltpu.VMEM)],
        out_shape=[pltpu.SemaphoreType.DMA((7,))] * (2 * n) + [pltpu.HBM(a.shape, a.dtype) for a in arrays + zones]
        + [jax.ShapeDtypeStruct((8, 128), F32)],
        input_output_aliases={a: 2 * n + a for a in range(2 * n)},
        compiler_params=SPLIT_COPY,
    )(*[_in_hbm(a) for a in arrays + zones])
    return outs[:n], outs[n:2 * n], outs[2 * n:3 * n], outs[3 * n:4 * n], outs[-1]


def _exchange_wait(name, started, after):
    send_sems, recv_sems, arrays, zones, _ = started
    n = len(arrays)

    def body(*refs):
        srcs, lands = refs[:n], refs[n:2 * n]
        send, recv = refs[2 * n:3 * n], refs[3 * n:4 * n]
        x, y, c = _place()
        for a, (src, land) in enumerate(zip(srcs, lands)):
            for m in range(1, 8):
                px, py, pc = _peer(x, y, c, m)
                copy = pltpu.make_async_remote_copy(
                    src_ref=_piece_rows(src, 4 * px + 2 * py + pc), dst_ref=land.at[m - 1], send_sem=send[a].at[m - 1],
                    recv_sem=recv[a].at[m - 1], device_id=(px, py, pc), device_id_type=MESH)
                copy.wait_send()
                copy.wait_recv()

    outs = pl.pallas_call(
        body, name=name, in_specs=[HBM] * (2 * n) + [SEM] * (2 * n) + [ANY], out_specs=[HBM] * (2 * n),
        out_shape=[pltpu.HBM(a.shape, a.dtype) for a in list(arrays) + list(zones)],
        input_output_aliases={a: a for a in range(2 * n)}, compiler_params=SPLIT_COPY,
    )(*arrays, *zones, *send_sems, *recv_sems, after)
    return outs[n:]


def _sum_pieces(name, owns, recvs, place_arr, layer, dests=None):
    n = len(owns)
    steps = 2

    def body(place_ref, *refs):
        for o_ref, r_ref, out_ref in zip(refs[:n], refs[n:2 * n], refs[-n:]):
            total = o_ref[...]
            for m in range(7):
                total = total + r_ref[m].astype(F32)
            out_ref[...] = total

    tiles = [o.shape[0] // steps for o in owns]
    return pl.pallas_call(
        body, name=name,
        grid_spec=pltpu.PrefetchScalarGridSpec(
            num_scalar_prefetch=1, grid=(steps,),
            in_specs=[pl.BlockSpec((t, o.shape[1]), lambda i, place: (i, 0)) for o, t in zip(owns, tiles)]
            + [pl.BlockSpec((7, t, o.shape[1]), lambda i, place: (0, i, 0)) for o, t in zip(owns, tiles)]
            + ([] if dests is None else [ANY] * n),
            out_specs=[pl.BlockSpec((None, t, o.shape[1]), lambda i, place: (layer, place[1] * steps + i, 0))
                       for o, t in zip(owns, tiles)]),
        out_shape=[jax.ShapeDtypeStruct((DEPTH, 2 * o.shape[0], o.shape[1]), F32) for o in owns],
        input_output_aliases={} if dests is None else {1 + 2 * n + k: k for k in range(n)},
        compiler_params=_params(),
    )(place_arr, *owns, *recvs, *(() if dests is None else dests))


def _sum_small(partials, recvs, place_arr):
    n = len(partials)

    def body(place_ref, *refs):
        for o_ref, r_ref, out_ref in zip(refs[:n], refs[n:2 * n], refs[2 * n:]):
            total = o_ref[...]
            for m in range(7):
                total = total + r_ref[m]
            out_ref[...] = total

    piece = lambda a: pl.BlockSpec((a.shape[0] // 8, a.shape[1]), lambda i, place: (place[0], 0))
    return pl.pallas_call(
        body, name="sum_small",
        grid_spec=pltpu.PrefetchScalarGridSpec(
            num_scalar_prefetch=1, grid=(1,),
            in_specs=[piece(a) for a in partials] + [pl.BlockSpec(r.shape, lambda i, place: (0, 0, 0)) for r in recvs],
            out_specs=[piece(a) for a in partials]),
        out_shape=[jax.ShapeDtypeStruct(a.shape, F32) for a in partials],
        compiler_params=_params(),
    )(place_arr, *partials, *recvs)


def _share(name, bufs, parts, gathered=()):
    n, n_g = len(bufs), len(gathered)
    total = n + n_g

    def body(*refs):
        ins, outs = refs[:total], refs[total:2 * total]
        send_sems, recv_sems, send_g, recv_g = refs[2 * total:]
        x, y, c = _place()

        def half(ref, l, which):
            p = ref.shape[1] // 2
            return ref.at[l, pl.ds(pl.multiple_of(which * p, 8), p), :]

        def swap(k, which):
            a, l = parts[k]
            return pltpu.make_async_remote_copy(
                src_ref=half(ins[a], l, which), dst_ref=half(outs[a], l, which), send_sem=send_sems.at[k],
                recv_sem=recv_sems.at[k], device_id=(x, y, 1 - c), device_id_type=MESH)

        def spread(a, m, sender):
            k = 4 * sender[0] + 2 * sender[1] + sender[2]
            return pltpu.make_async_remote_copy(
                src_ref=_piece_rows(ins[n + a], k), dst_ref=_piece_rows(outs[n + a], k), send_sem=send_g.at[7 * a + m - 1],
                recv_sem=recv_g.at[7 * a + m - 1], device_id=_peer(x, y, c, m), device_id_type=MESH)

        for k in range(len(parts)):
            swap(k, c).start()
        for a in range(n_g):
            for m in range(1, 8):
                spread(a, m, (x, y, c)).start()
        for k in range(len(parts)):
            swap(k, c).wait_send()
            swap(k, 1 - c).wait_recv()
        for a in range(n_g):
            for m in range(1, 8):
                spread(a, m, (x, y, c)).wait_send()
                spread(a, m, _peer(x, y, c, m)).wait_recv()

    arrays = list(bufs) + list(gathered)
    return pl.pallas_call(
        body, name=name, in_specs=[ANY] * total, out_specs=[ANY] * total,
        out_shape=[jax.ShapeDtypeStruct(b.shape, F32) for b in arrays],
        input_output_aliases={a: a for a in range(total)},
        scratch_shapes=[pltpu.SemaphoreType.DMA((max(len(parts), 1),))] * 2
        + [pltpu.SemaphoreType.DMA((max(7 * n_g, 1),))] * 2,
    )(*arrays)


def _adamw_math(w, g, m, v):
    nm = ADAM_B1 * m + (1.0 - ADAM_B1) * g
    nv = ADAM_B2 * v + (1.0 - ADAM_B2) * (g * g)
    m_hat = nm / (1.0 - ADAM_B1 ** ADAM_STEP)
    v_hat = nv / (1.0 - ADAM_B2 ** ADAM_STEP)
    return -ADAM_LR * (m_hat / (jnp.sqrt(v_hat) + ADAM_EPS) + ADAM_WD * w), nm, nv


def _adamw(name, w, g, m, v, rows_per_step, first=0, count=None, dests=None, deps=()):
    layers, rows, cols = w.shape
    count = layers if count is None else count

    def body(w_ref, g_ref, m_ref, v_ref, *rest):
        d_ref, nm_ref, nv_ref, g_out_ref = rest[-4:]
        d_ref[...], nm_ref[...], nv_ref[...] = _adamw_math(w_ref[...], g_ref[...], m_ref[...], v_ref[...])
        g_out_ref[...] = g_ref[...]

    spec = pl.BlockSpec((1, rows_per_step, cols), lambda l, i: (first + l, i, 0))
    shape = jax.ShapeDtypeStruct(w.shape, F32)
    dests = () if dests is None else tuple(dests)
    return pl.pallas_call(
        body, name=name, grid=(count, rows // rows_per_step),
        in_specs=[spec] * 4 + [ANY] * (len(dests) + len(deps)), out_specs=[spec] * 4, out_shape=[shape] * 4,
        input_output_aliases={4 + k: k for k in range(len(dests))},
        compiler_params=_params(("arbitrary", "arbitrary")),
    )(w, g, m, v, *dests, *deps)


def _pack_misc(pool_scale, sinks, norm_pre, norm_post):
    sink_rows = jnp.zeros((DEPTH, 8, 128), F32).at[:, 0, 0:N_HEADS].set(sinks).reshape(2 * 8, 128)
    return jnp.concatenate([pool_scale.reshape(8, 128), norm_pre.reshape(16, 128), norm_post.reshape(16, 128),
                            sink_rows, jnp.zeros((8, 128), F32)], axis=0)


def _adamw_misc(w, g, m, v):
    def body(w_ref, g_ref, m_ref, v_ref, *rest):
        outs, (d_ref, nm_ref, nv_ref) = rest[:17], rest[17:]
        d_ref[...], nm_ref[...], nv_ref[...] = _adamw_math(w_ref[...], g_ref[...], m_ref[...], v_ref[...])
        for k, src in enumerate([g_ref, d_ref, nm_ref, nv_ref]):
            scale, sinks, pre, post = outs[4 * k:4 * k + 4]
            for l in range(DEPTH):
                for j in range(4):
                    scale[l:l + 1, j * 128:(j + 1) * 128] = src[MISC_SCALE + 4 * l + j:MISC_SCALE + 4 * l + j + 1, :]
                for j in range(8):
                    pre[l:l + 1, j * 128:(j + 1) * 128] = src[MISC_PRE + 8 * l + j:MISC_PRE + 8 * l + j + 1, :]
                    post[l:l + 1, j * 128:(j + 1) * 128] = src[MISC_POST + 8 * l + j:MISC_POST + 8 * l + j + 1, :]
                sinks[l:l + 1, :] = src[MISC_SINKS + 8 * l:MISC_SINKS + 8 * l + 1, 0:N_HEADS]
        outs[16][...] = g_ref[MISC_LOSS:MISC_LOSS + 1, 0:1]

    vmem = pl.BlockSpec(memory_space=pltpu.VMEM)
    shapes = [(DEPTH, D_POOL), (DEPTH, N_HEADS), (DEPTH, D), (DEPTH, D)] * 4 + [(1, 1)]
    return pl.pallas_call(
        body, name="adamw_misc", in_specs=[vmem] * 4, out_specs=[vmem] * 17,
        out_shape=[jax.ShapeDtypeStruct(s, F32) for s in shapes],
        scratch_shapes=[pltpu.VMEM((MISC_ROWS, 128), F32)] * 3,
    )(w, g, m, v)


def kernel(x, w_in, pool_w, pool_scale, attn_sinks, w_out, norm_pre, norm_post, loss_target, m_w_in, m_pool_w, m_pool_scale, m_attn_sinks, m_w_out, m_norm_pre, m_norm_post, v_w_in, v_pool_w, v_pool_scale, v_attn_sinks, v_w_out, v_norm_pre, v_norm_post):
    cx, cy, cc = _place()
    chip_arr = jnp.reshape(2 * cx + cy, (1,)).astype(jnp.int32)
    place_arr = jnp.stack([4 * cx + 2 * cy + cc, cc]).astype(jnp.int32)
    t = lambda a: jnp.transpose(a, (0, 2, 1))
    w_in_t = t(w_in)
    xs, target = x[0], loss_target[0]
    pool_w_b = pool_w.astype(BF16)
    tables = _attention_tables()
    scale3 = pool_scale.reshape(DEPTH, 1, D_POOL)
    pre3 = norm_pre.reshape(DEPTH, 1, D)
    post3 = norm_post.reshape(DEPTH, 1, D)

    (wi0,) = _place_cast("place_w_in0", w_in_t, chip_arr, 288, [0])
    first = _gather_start("gather_start_first", [wi0], halved=(0,))
    (wi1,) = _place_cast("place_w_in1", w_in_t, chip_arr, 288, [1])
    wo = _place_cast("place_w_out", w_out, chip_arr, 256, [0, 1])
    rest = _gather_start("gather_start_rest", [wo[0], wi1, wo[1]], halved=(1,))
    send, recv, bufs = [first[k] + rest[k] for k in range(3)]
    order = {(0, "in"): 0, (0, "out"): 1, (1, "in"): 2, (1, "out"): 3}

    saved = []
    after = (first[3], rest[3], pool_w_b, *tables, scale3, pre3, post3)
    for l in range(DEPTH):
        k = order[l, "in"]
        w_in_l = _forward_halves(f"forward_w_in{l}", _gather_wait(f"gather_wait_in{l}", bufs[k], send[k], recv[k], after,
                                                                 halved=True))
        pu, pg, q, kv, ag = _fwd_in(l, xs, pre3, w_in_l)
        cat = _fwd_mix(l, pu, pg, q, kv, ag, pool_w_b, scale3, attn_sinks, tables)
        k = order[l, "out"]
        w_out_l = _gather_wait(f"gather_wait_out{l}", bufs[k], send[k], recv[k], (cat,))
        y, x_next = _fwd_out(l, cat, w_out_l, xs, post3) if l < DEPTH - 1 else (None, None)
        saved.append((xs, pu, pg, q, kv, ag, cat, y, w_in_l, w_out_l))
        if l < DEPTH - 1:
            xs, after = x_next, (x_next,)

    x_in, pu, pg, q, kv, ag, cat, y, w_in_l, w_out_l = saved[1]
    dcat, dw_out1, dw_out1_b, dg_post1, loss, xs = _bwd_out(1, cat, w_out_l, post3, place_arr, x=x_in, target=target)
    dproj, dpw, dsc1, dsink1 = _bwd_mix(1, pu, pg, q, kv, ag, dcat, pool_w_b, scale3, attn_sinks, tables)
    dw_in1, dw_in1_b = _bwd_in_dw(1, dproj, x_in, pre3, place_arr)
    ex1 = _exchange_start("exchange_start_1", [dw_in1_b, dw_out1_b])
    dx, dg_pre1 = _bwd_in_dx(1, dproj, w_in_l, x_in, pre3, xs, deps=(ex1[4],))

    x_in, pu, pg, q, kv, ag, cat, y, w_in_l, w_out_l = saved[0]
    dcat, dw_out0, dw_out0_b, dg_post0 = _bwd_out(0, cat, w_out_l, post3, place_arr, dxn=dx, y=y)
    ex0_out = _exchange_start("exchange_start_out0", [dw_out0_b])
    dproj, dpw, dsc0, dsink0 = _bwd_mix(0, pu, pg, q, kv, ag, dcat, pool_w_b, scale3, attn_sinks, tables,
                                        deps=(ex0_out[4],), dpw_dest=dpw)
    recv_in1, recv_out1 = _exchange_wait("exchange_wait_1", ex1, dproj)
    g_in, g_out = _sum_pieces("sum_pieces_1", [dw_in1, dw_out1], [recv_in1, recv_out1], place_arr, 1)
    dw_in0, dw_in0_b = _bwd_in_dw(0, dproj, x_in, pre3, place_arr, deps=(g_in, g_out))
    ex0_in = _exchange_start("exchange_start_in0", [dw_in0_b])

    grad_x, dg_pre0 = _bwd_in_dx(0, dproj, w_in_l, x_in, pre3, dx, deps=(ex0_in[4],))
    small = [dpw.reshape(DEPTH * 4 * 128, 128),
             jnp.concatenate([dsc0, dsc1, dg_pre0, dg_pre1, dg_post0, dg_post1, dsink0, dsink1, loss], axis=0)]
    ex_small = _exchange_start("exchange_start_small", small)
    (recv_out0,) = _exchange_wait("exchange_wait_out0", ex0_out, ex_small[4])
    (g_out,) = _sum_pieces("sum_pieces_out0", [dw_out0], [recv_out0], place_arr, 0, dests=[g_out])
    g_in, g_out = _share("share_a", [g_in, g_out], [(0, 1), (1, 0), (1, 1)])
    m_in_t, v_in_t = t(m_w_in), t(v_w_in)
    d_out, nm_out, nv_out, grad_w_out = _adamw("adamw_w_out", w_out, g_out, m_w_out, v_w_out, 256)
    upd_in = _adamw("adamw_w_in1", w_in_t, g_in, m_in_t, v_in_t, 288, first=1, count=1, deps=(d_out,))

    (recv_in0,) = _exchange_wait("exchange_wait_in0", ex0_in, upd_in[0])
    recv_small = _exchange_wait("exchange_wait_small", ex_small, recv_in0)
    (g_in,) = _sum_pieces("sum_pieces_in0", [dw_in0], [recv_in0], place_arr, 0, dests=[g_in])
    g_in, g_pw, g_misc = _share("share_b", [g_in], [(0, 0)], _sum_small(small, recv_small, place_arr))
    d_in, nm_in, nv_in, grad_w_in_t = _adamw("adamw_w_in0", w_in_t, g_in, m_in_t, v_in_t, 288, first=0, count=1,
                                             dests=upd_in)
    flat = lambda a: a.reshape(1, DEPTH * 4 * 128, 128)
    pw = _adamw("adamw_pool_w", flat(pool_w), flat(g_pw), flat(m_pool_w), flat(v_pool_w), 1024)
    d_pw, m_pw, v_pw, g_pw = [a.reshape(pool_w.shape) for a in pw]
    misc = _adamw_misc(_pack_misc(pool_scale, attn_sinks, norm_pre, norm_post), g_misc,
                       _pack_misc(m_pool_scale, m_attn_sinks, m_norm_pre, m_norm_post),
                       _pack_misc(v_pool_scale, v_attn_sinks, v_norm_pre, v_norm_post))
    (g_sc, g_sk, g_pre, g_post, d_sc, d_sk, d_pre, d_post,
     m_sc, m_sk, m_pre, m_post, v_sc, v_sk, v_pre, v_post, loss_sum) = misc
    return (loss_sum[0, 0], grad_x[None], t(grad_w_in_t), g_pw, g_sc, g_sk, grad_w_out, g_pre, g_post,
            t(d_in), d_pw, d_sc, d_sk, d_out, d_pre, d_post,
            t(nm_in), m_pw, m_sc, m_sk, nm_out, m_pre, m_post,
            t(nv_in), v_pw, v_sc, v_sk, nv_out, v_pre, v_post)
```

```python
import jax
import jax.numpy as jnp
from jax import lax
from jax.experimental import pallas as pl
from jax.experimental.pallas import tpu as pltpu

F32 = jnp.float32
BF16 = jnp.bfloat16

S = 2048
D = 1024
DEPTH = 2
D_POOL = 512
POOL_WINDOWS = (2, 4, 8, 16)
N_HEADS = 8
D_IN = 2304
N_SHARDS = 4
W_IN_SHARD = D_IN // N_SHARDS
W_OUT_SHARD = D // N_SHARDS
BLK = 128
NB = S // BLK
HALO = 16
PAD = 8
EPS = 1e-6
NEG_INF = -1e30
C_PU, C_PG, C_Q, C_K, C_V, C_AG = 0, 512, 1024, 1536, 1664, 1792

ADAM_LR = 0.001
ADAM_B1 = 0.9
ADAM_B2 = 0.999
ADAM_EPS = 1e-08
ADAM_WD = 0.01
ADAM_STEP = 10

TM = 512
VMEM_LIMIT = 56 * 1024 * 1024

NT = (((1,), (1,)), ((), ()))
TN = (((0,), (0,)), ((), ()))

MESH = pl.DeviceIdType.MESH
ANY = pl.BlockSpec(memory_space=pl.ANY)

MISC_SCALE, MISC_PRE, MISC_POST, MISC_SINKS, MISC_LOSS = 0, 8, 24, 40, 56
MISC_ROWS = 64


def _params(sem=("arbitrary",)):
    return pltpu.CompilerParams(dimension_semantics=sem, vmem_limit_bytes=VMEM_LIMIT)


def _sigmoid(v):
    return 1.0 / (1.0 + jnp.exp(-v))


def _rows8(v):
    r, c = v.shape
    return v.reshape(r // 8, 8, c).sum(axis=0)


def _layer(l, *shape):
    zeros = (0,) * len(shape)
    return pl.BlockSpec((None,) + shape, lambda i: (l,) + zeros)


def _whole(shape):
    zeros = (0,) * len(shape)
    return pl.BlockSpec(shape, lambda i: zeros, pipeline_mode=pl.Buffered(1))


def _fwd_in(l, x, g_pre, w_in_t):
    def body(x_ref, g_ref, w_ref, pu_ref, pg_ref, q_ref, kv_ref, ag_ref):
        xt = x_ref[...]
        r = lax.rsqrt(jnp.mean(xt * xt, axis=-1, keepdims=True) + EPS)
        h = (xt * r * g_ref[...]).astype(BF16)

        def proj(lo, hi):
            return lax.dot_general(h, w_ref[lo:hi, :], NT, preferred_element_type=F32)

        pu_ref[...] = proj(C_PU, C_PG)
        pg_ref[...] = proj(C_PG, C_Q)
        q_ref[...] = proj(C_Q, C_K).astype(BF16)
        kv_ref[...] = proj(C_K, C_AG).astype(BF16)
        ag_ref[...] = proj(C_AG, D_IN)

    row = lambda w: pl.BlockSpec((TM, w), lambda i: (i, 0))
    return pl.pallas_call(
        body, name="fwd_in", grid=(S // TM,),
        in_specs=[row(D), _layer(l, 1, D), _whole((D_IN, D))],
        out_specs=[row(512), row(512), row(512), row(256), row(512)],
        out_shape=[jax.ShapeDtypeStruct((S, 512), F32), jax.ShapeDtypeStruct((S, 512), F32),
                   jax.ShapeDtypeStruct((S, 512), BF16), jax.ShapeDtypeStruct((S, 256), BF16),
                   jax.ShapeDtypeStruct((S, 512), F32)],
        compiler_params=_params(),
    )(x, g_pre, w_in_t)


LOG2E = 1.4426950408889634
SCORE_SCALE = 0.125 * LOG2E


def _attention_tables():
    qi = jnp.arange(BLK)[:, None]
    kj = jnp.arange(BLK)[None, :]
    dist = ((qi - kj) % BLK).astype(F32)
    slopes = jnp.exp2(-jnp.arange(1, N_HEADS + 1, dtype=F32))
    bias = -(slopes * LOG2E)[:, None, None] * dist[None]
    first = jnp.where(kj > qi, NEG_INF, bias)
    return jnp.stack([first, bias]), (kj <= qi).astype(BF16)


def _own_block_mask():
    return lax.broadcasted_iota(jnp.int32, (BLK, BLK), 1) <= lax.broadcasted_iota(jnp.int32, (BLK, BLK), 0)


def _merge(full, own):
    return jnp.where(own, full[:, BLK:], full[:, :BLK])


def _spread(v, tri):
    own = v * tri
    return jnp.concatenate([v - own, own], axis=1)


def _head_variants(cur, prev):
    both = jnp.concatenate([prev, cur], axis=0).astype(F32)
    swapped = pltpu.roll(both, 64, axis=1)
    low = lax.broadcasted_iota(jnp.int32, both.shape, 1) < 64
    zero = jnp.zeros_like(both)
    return ((jnp.where(low, both, zero).astype(BF16), jnp.where(low, zero, swapped).astype(BF16)),
            (jnp.where(low, swapped, zero).astype(BF16), jnp.where(low, zero, both).astype(BF16)))


def _head_of(hkv, t, half):
    return hkv * 4 + 2 * t + half


def _rows(v, t):
    return v[t * BLK:(t + 1) * BLK]


def _stack_tiles(ref, hkv, offset=0):
    lo = offset + 2 * hkv * 128
    return jnp.concatenate([ref[:, lo:lo + 128], ref[:, lo + 128:lo + 256]], axis=0)


def _scores(q2, k_var, own):
    s = {}
    for hkv in range(2):
        for half in range(2):
            full = lax.dot_general(q2[hkv], k_var[hkv][half], NT, preferred_element_type=F32)
            for t in range(2):
                s[hkv, t, half] = _merge(_rows(full, t), own)
    return s


def _softmax(s, bias, sink):
    s = s * SCORE_SCALE + bias
    sink2 = sink * LOG2E
    m = jnp.maximum(jnp.max(s, axis=-1, keepdims=True), sink2)
    p = jnp.exp2(s - m)
    e_sink = jnp.exp2(sink2 - m)
    inv = 1.0 / (jnp.sum(p, axis=-1, keepdims=True) + e_sink)
    return p * inv, e_sink * inv


def _spread_pair(v, hkv, half, tri):
    return jnp.concatenate([_spread(v[hkv, t, half].astype(BF16), tri) for t in range(2)], axis=0)


POOL_ROWS = PAD + HALO + BLK


def _window_sums(src_ref, tmp_refs, trailing):
    lo, hi = (PAD, POOL_ROWS) if trailing else (0, HALO + BLK)
    cur = src_ref
    for level in range(len(POOL_WINDOWS)):
        lanes = slice(level * 128, 512)
        shift = -(1 << level) if trailing else (1 << level)
        dst = tmp_refs[level % 2]
        dst[lo:hi, lanes] = cur[lo:hi, lanes] + cur[lo + shift:hi + shift, lanes]
        cur = dst


def _pool_block(ext_ref, tmp_refs, i, g, w):
    lanes = slice(g * 128, (g + 1) * 128)
    rows = slice(PAD + HALO, POOL_ROWS)
    t = (i * BLK + lax.broadcasted_iota(jnp.int32, (BLK, 1), 0)).astype(F32)
    inv = 1.0 / jnp.minimum(t + 1.0, float(w))
    return tmp_refs[g % 2][rows, lanes] * inv - ext_ref[rows, lanes], inv


def _fwd_mix(l, pu, pg, q, kv, ag, pool_w, pool_scale, sinks, tables):
    bias, tri = tables

    def body(pu_ref, pup_ref, pg_ref, q_ref, kv_ref, kvp_ref, ag_ref, pw_ref, sc_ref, sink_ref, bias_ref, tri_ref,
             cat_ref, ext_ref, *tmp_refs):
        i = pl.program_id(0)

        @pl.when(i == 0)
        def _():
            for ref in (ext_ref, *tmp_refs):
                ref[0:PAD, :] = jnp.zeros((PAD, 512), F32)

        ext_ref[PAD:PAD + HALO, :] = jnp.where(i > 0, pup_ref[...], 0.0)
        ext_ref[PAD + HALO:POOL_ROWS, :] = pu_ref[...]
        _window_sums(ext_ref, tmp_refs, True)
        for g, w in enumerate(POOL_WINDOWS):
            lanes = slice(g * 128, (g + 1) * 128)
            pooled, _ = _pool_block(ext_ref, tmp_refs, i, g, w)
            mixed = jnp.dot(pooled.astype(BF16), pw_ref[g], preferred_element_type=F32)
            gate = pg_ref[:, lanes]
            cat_ref[:, lanes] = (mixed * sc_ref[:, lanes] * (gate * _sigmoid(gate))).astype(BF16)

        own = _own_block_mask()
        tri = tri_ref[...]
        k_var = _head_variants(kv_ref[:, 0:128], kvp_ref[:, 0:128])
        v_var = _head_variants(kv_ref[:, 128:256], kvp_ref[:, 128:256])
        s = _scores([_stack_tiles(q_ref, hkv) for hkv in range(2)], k_var, own)
        p = {}
        for (hkv, t, half), s_head in s.items():
            head = _head_of(hkv, t, half)
            p[hkv, t, half], _ = _softmax(s_head, bias_ref[head], sink_ref[l, head])
        for hkv in range(2):
            o2 = jnp.zeros((2 * BLK, 128), F32)
            for half in range(2):
                o2 = o2 + jnp.dot(_spread_pair(p, hkv, half, tri), v_var[hkv][half], preferred_element_type=F32)
            for t in range(2):
                lo = (2 * hkv + t) * 128
                gate = ag_ref[:, lo:lo + 128]
                cat_ref[:, D_POOL + lo:D_POOL + lo + 128] = (_rows(o2, t) * (gate * _sigmoid(gate))).astype(BF16)

    blk = lambda w: pl.BlockSpec((BLK, w), lambda i: (i, 0))
    prev = lambda w: pl.BlockSpec((BLK, w), lambda i: (jnp.maximum(i - 1, 0), 0))
    halo = pl.BlockSpec((HALO, 512), lambda i: (jnp.maximum(i * (BLK // HALO) - 1, 0), 0))
    return pl.pallas_call(
        body, name="fwd_mix", grid=(NB,),
        in_specs=[blk(512), halo, blk(512), blk(512), blk(256), prev(256), blk(512),
                  _layer(l, 4, 128, 128), _layer(l, 1, 512), pl.BlockSpec(memory_space=pltpu.SMEM),
                  pl.BlockSpec((None, N_HEADS, BLK, BLK), lambda i: (jnp.minimum(i, 1), 0, 0, 0)), _whole((BLK, BLK))],
        out_specs=blk(D),
        out_shape=jax.ShapeDtypeStruct((S, D), BF16),
        scratch_shapes=[pltpu.VMEM((POOL_ROWS, 512), F32)] * 3,
        compiler_params=_params(),
    )(pu, pu, pg, q, kv, kv, ag, pool_w, pool_scale, sinks, bias, tri)


def _fwd_out(l, cat, w_out, x, g_post):
    def body(cat_ref, w_ref, x_ref, g_ref, y_ref, xn_ref):
        y = jnp.dot(cat_ref[...], w_ref[...], preferred_element_type=F32)
        y_ref[...] = y
        r = lax.rsqrt(jnp.mean(y * y, axis=-1, keepdims=True) + EPS)
        xn_ref[...] = x_ref[...] + y * r * g_ref[...]

    row = lambda: pl.BlockSpec((TM, D), lambda i: (i, 0))
    act = jax.ShapeDtypeStruct((S, D), F32)
    return pl.pallas_call(
        body, name="fwd_out", grid=(S // TM,),
        in_specs=[row(), _whole((D, D)), row(), _layer(l, 1, D)], out_specs=[row(), row()], out_shape=[act, act],
        compiler_params=_params(),
    )(cat, w_out, x, g_post)


def _store_lane_rows(ref, acc):
    total = jnp.sum(acc, axis=0, keepdims=True)
    for k in range(ref.shape[0]):
        ref[k:k + 1, :] = total[:, k * 128:(k + 1) * 128]


def _own_piece(dw_ref, place_ref):
    p = dw_ref.shape[0] // 8
    return dw_ref[pl.ds(pl.multiple_of(place_ref[0] * p, 8), p), :]


def _bwd_out(l, cat, w_out, g_post, place_arr, dxn=None, y=None, x=None, target=None):
    last = target is not None
    n_steps = S // TM

    def body(a_ref, b_ref, g_ref, cat_ref, w_ref, place_ref, dcat_ref, own_ref, dwb_ref, dg_ref, *rest):
        acc_ref, dw_ref = rest[-2:]
        step = pl.program_id(0)

        @pl.when(step == 0)
        def _():
            dw_ref[...] = jnp.zeros_like(dw_ref)
            acc_ref[...] = jnp.zeros_like(acc_ref)

        cat = cat_ref[...]
        g = g_ref[...]
        y = jnp.dot(cat, w_ref[...], preferred_element_type=F32) if last else b_ref[...]
        r = lax.rsqrt(jnp.mean(y * y, axis=-1, keepdims=True) + EPS)
        if last:
            loss_ref, dx_ref, loss_acc_ref = rest[:3]
            err = a_ref[...] + y * r * g - b_ref[...]

            @pl.when(step == 0)
            def _():
                loss_acc_ref[...] = jnp.zeros_like(loss_acc_ref)

            loss_acc_ref[...] += _rows8(err * err)
            dz = err * (1.0 / D)
            dx_ref[...] = dz
        else:
            dz = a_ref[...]
        a = dz * g
        dy = r * a - y * (r * r * r) * jnp.mean(a * y, axis=-1, keepdims=True)
        acc_ref[...] += _rows8(dz * (y * r))
        dyb = dy.astype(BF16)
        dcat_ref[...] = lax.dot_general(dyb, w_ref[...], NT, preferred_element_type=F32)
        dw_ref[...] += lax.dot_general(cat, dyb, TN, preferred_element_type=F32)

        @pl.when(step == n_steps - 1)
        def _():
            _store_lane_rows(dg_ref, acc_ref[...])
            dwb_ref[...] = dw_ref[...].astype(BF16)
            own_ref[...] = _own_piece(dw_ref, place_ref)
            if last:
                loss_ref[...] = jnp.full((8, 128), (0.5 / D) * jnp.sum(loss_acc_ref[...]), F32)

    row = lambda: pl.BlockSpec((TM, D), lambda i: (i, 0))
    full = _whole
    return pl.pallas_call(
        body, name="out_loss_bwd" if last else "bwd_out", grid=(n_steps,),
        in_specs=[row(), row(), _layer(l, 1, D), row(), full((D, D)), pl.BlockSpec(memory_space=pltpu.SMEM)],
        out_specs=[row(), full((D // 8, D)), full((D, D)), full((8, 128))] + ([full((8, 128)), row()] if last else []),
        out_shape=[jax.ShapeDtypeStruct((S, D), F32), jax.ShapeDtypeStruct((D // 8, D), F32),
                   jax.ShapeDtypeStruct((D, D), BF16), jax.ShapeDtypeStruct((8, 128), F32)]
        + ([jax.ShapeDtypeStruct((8, 128), F32), jax.ShapeDtypeStruct((S, D), F32)] if last else []),
        scratch_shapes=([pltpu.VMEM((8, D), F32)] if last else []) + [pltpu.VMEM((8, D), F32), pltpu.VMEM((D, D), F32)],
        compiler_params=_params(),
    )(*((x, target) if last else (dxn, y)), g_post, cat, w_out, place_arr)


def _bwd_mix(l, pu, pg, q, kv, ag, dcat, pool_w, pool_scale, sinks, tables, deps=(), dpw_dest=None):
    bias, tri = tables
    deps = tuple(deps) + (() if dpw_dest is None else (dpw_dest,))

    def body(pu_ref, pup_ref, pg_ref, q_ref, kv_ref, kvp_ref, ag_ref, dcat_ref, pw_ref, sc_ref, sink_ref, bias_ref,
             tri_ref, *rest):
        dproj_ref, dpw_ref, dsc_ref, dsink_ref, ext_ref, dext_ref, tmp_a, tmp_b, dkv_ref = rest[len(deps):]
        tmp_refs = (tmp_a, tmp_b)
        step = pl.program_id(0)
        i = NB - 1 - step

        @pl.when(step == 0)
        def _():
            dpw_ref[...] = jnp.zeros_like(dpw_ref)
            dsc_ref[...] = jnp.zeros_like(dsc_ref)
            dsink_ref[...] = jnp.zeros_like(dsink_ref)
            for ref in (ext_ref, tmp_a, tmp_b):
                ref[0:PAD, :] = jnp.zeros((PAD, 512), F32)
            dext_ref[BLK:POOL_ROWS, :] = jnp.zeros((HALO + PAD, 512), F32)
            dkv_ref[...] = jnp.zeros_like(dkv_ref)

        ext_ref[PAD:PAD + HALO, :] = jnp.where(i > 0, pup_ref[...], 0.0)
        ext_ref[PAD + HALO:POOL_ROWS, :] = pu_ref[...]
        _window_sums(ext_ref, tmp_refs, True)
        dpooled = []
        for g, w in enumerate(POOL_WINDOWS):
            lanes = slice(g * 128, (g + 1) * 128)
            pooled, inv = _pool_block(ext_ref, tmp_refs, i, g, w)
            pooled_b = pooled.astype(BF16)
            mixed = jnp.dot(pooled_b, pw_ref[g], preferred_element_type=F32)
            scale = sc_ref[:, lanes]
            gate = pg_ref[:, lanes]
            sg = _sigmoid(gate)
            dpo = dcat_ref[:, lanes]
            dproj_ref[:, C_PG + g * 128:C_PG + (g + 1) * 128] = (
                dpo * (mixed * scale) * (sg * (1.0 + gate * (1.0 - sg)))).astype(BF16)
            dms = dpo * (gate * sg)
            dsc_ref[g:g + 1, :] += jnp.sum(dms * mixed, axis=0, keepdims=True)
            dmixed = (dms * scale).astype(BF16)
            dpw_ref[g] += lax.dot_general(pooled_b, dmixed, TN, preferred_element_type=F32)
            dpooled.append(lax.dot_general(dmixed, pw_ref[g], NT, preferred_element_type=F32))
            dext_ref[0:BLK, lanes] = dpooled[g] * inv
        _window_sums(dext_ref, tmp_refs, False)
        for g in range(len(POOL_WINDOWS)):
            lanes = slice(g * 128, (g + 1) * 128)
            dproj_ref[:, C_PU + g * 128:C_PU + (g + 1) * 128] = (tmp_refs[g % 2][0:BLK, lanes] - dpooled[g]).astype(BF16)
        dext_ref[BLK:BLK + HALO, :] = dext_ref[0:HALO, :]

        own = _own_block_mask()
        tri = tri_ref[...]
        k_var = _head_variants(kv_ref[:, 0:128], kvp_ref[:, 0:128])
        v_var = _head_variants(kv_ref[:, 128:256], kvp_ref[:, 128:256])
        q2 = [_stack_tiles(q_ref, hkv) for hkv in range(2)]
        s = _scores(q2, k_var, own)
        p, p_sink = {}, {}
        for key, s_head in s.items():
            head = _head_of(*key)
            p[key], p_sink[key] = _softmax(s_head, bias_ref[head], sink_ref[l, head])

        do2, p_b, dp = [], {}, {}
        for hkv in range(2):
            gate = _stack_tiles(ag_ref, hkv)
            sg = _sigmoid(gate)
            dca = _stack_tiles(dcat_ref, hkv, D_POOL)
            do2.append((dca * (gate * sg)).astype(BF16))
            o2 = jnp.zeros((2 * BLK, 128), F32)
            for half in range(2):
                p_b[hkv, half] = _spread_pair(p, hkv, half, tri)
                o2 = o2 + jnp.dot(p_b[hkv, half], v_var[hkv][half], preferred_element_type=F32)
                full = lax.dot_general(do2[hkv], v_var[hkv][half], NT, preferred_element_type=F32)
                for t in range(2):
                    dp[hkv, t, half] = _merge(_rows(full, t), own)
            dag = dca * o2 * (sg * (1.0 + gate * (1.0 - sg)))
            for t in range(2):
                lo = C_AG + (2 * hkv + t) * 128
                dproj_ref[:, lo:lo + 128] = _rows(dag, t).astype(BF16)

        ds = {}
        for key in p:
            delta = jnp.sum(p[key] * dp[key], axis=-1, keepdims=True)
            ds[key] = p[key] * (dp[key] - delta)
            head = _head_of(*key)
            dsink_ref[0:1, :] += jnp.where(lax.broadcasted_iota(jnp.int32, (1, 128), 1) == head,
                                           -jnp.sum(p_sink[key] * delta, axis=0, keepdims=True), 0.0)

        dk_acc = [[None, None], [None, None]]
        dv_acc = [[None, None], [None, None]]
        for hkv in range(2):
            dq2 = jnp.zeros((2 * BLK, 128), F32)
            for half in range(2):
                ds_b = _spread_pair(ds, hkv, half, tri)
                dq2 = dq2 + jnp.dot(ds_b, k_var[hkv][half], preferred_element_type=F32)
                dk_acc[hkv][half] = lax.dot_general(ds_b, q2[hkv], TN, preferred_element_type=F32)
                dv_acc[hkv][half] = lax.dot_general(p_b[hkv, half], do2[hkv], TN, preferred_element_type=F32)
            for t in range(2):
                lo = C_Q + (2 * hkv + t) * 128
                dproj_ref[:, lo:lo + 128] = (_rows(dq2, t) * 0.125).astype(BF16)

        low = lax.broadcasted_iota(jnp.int32, (2 * BLK, 128), 1) < 64

        def gather_heads(acc):
            return jnp.where(low, acc[0][0] + pltpu.roll(acc[0][1], 64, axis=1),
                             pltpu.roll(acc[1][0], 64, axis=1) + acc[1][1])

        dk = gather_heads(dk_acc) * 0.125
        dv = gather_heads(dv_acc)
        dproj_ref[:, C_K:C_V] = (dk[BLK:, :] + dkv_ref[:, 0:128]).astype(BF16)
        dproj_ref[:, C_V:C_AG] = (dv[BLK:, :] + dkv_ref[:, 128:256]).astype(BF16)
        dkv_ref[:, 0:128] = dk[:BLK, :]
        dkv_ref[:, 128:256] = dv[:BLK, :]

    rev = lambda w: pl.BlockSpec((BLK, w), lambda s: (NB - 1 - s, 0))
    prev = lambda w: pl.BlockSpec((BLK, w), lambda s: (jnp.maximum(NB - 2 - s, 0), 0))
    halo = pl.BlockSpec((HALO, 512), lambda s: (jnp.maximum((NB - 1 - s) * (BLK // HALO) - 1, 0), 0))
    return pl.pallas_call(
        body, name="bwd_mix", grid=(NB,),
        in_specs=[rev(512), halo, rev(512), rev(512), rev(256), prev(256), rev(512), rev(D),
                  _layer(l, 4, 128, 128), _layer(l, 1, 512), pl.BlockSpec(memory_space=pltpu.SMEM),
                  pl.BlockSpec((None, N_HEADS, BLK, BLK), lambda s: (jnp.minimum(NB - 1 - s, 1), 0, 0, 0)),
                  _whole((BLK, BLK))] + [ANY] * len(deps),
        out_specs=[rev(D_IN), _layer(l, 4, 128, 128),
                   pl.BlockSpec((4, 128), lambda s: (0, 0)), pl.BlockSpec((8, 128), lambda s: (0, 0))],
        out_shape=[jax.ShapeDtypeStruct((S, D_IN), BF16), jax.ShapeDtypeStruct((DEPTH, 4, 128, 128), F32),
                   jax.ShapeDtypeStruct((4, 128), F32), jax.ShapeDtypeStruct((8, 128), F32)],
        input_output_aliases={} if dpw_dest is None else {12 + len(deps): 1},
        scratch_shapes=[pltpu.VMEM((POOL_ROWS, 512), F32)] * 4 + [pltpu.VMEM((BLK, 256), F32)],
        compiler_params=_params(),
    )(pu, pu, pg, q, kv, kv, ag, dcat, pool_w, pool_scale, sinks, bias, tri, *deps)


def _bwd_in_dw(l, dproj, x, g_pre, place_arr, deps=()):
    n_steps = S // TM

    def body(dp_ref, x_ref, g_ref, place_ref, *rest):
        own_ref, dwb_ref, dw_ref = rest[len(deps):]
        step = pl.program_id(0)

        @pl.when(step == 0)
        def _():
            dw_ref[...] = jnp.zeros_like(dw_ref)

        xt = x_ref[...]
        r = lax.rsqrt(jnp.mean(xt * xt, axis=-1, keepdims=True) + EPS)
        h = (xt * r * g_ref[...]).astype(BF16)
        dw_ref[...] += lax.dot_general(dp_ref[...], h, TN, preferred_element_type=F32)

        @pl.when(step == n_steps - 1)
        def _():
            dwb_ref[...] = dw_ref[...].astype(BF16)
            own_ref[...] = _own_piece(dw_ref, place_ref)

    row = lambda w: pl.BlockSpec((TM, w), lambda i: (i, 0))
    full = _whole
    return pl.pallas_call(
        body, name="bwd_in_dw", grid=(n_steps,),
        in_specs=[row(D_IN), row(D), _layer(l, 1, D), pl.BlockSpec(memory_space=pltpu.SMEM)] + [ANY] * len(deps),
        out_specs=[full((D_IN // 8, D)), full((D_IN, D))],
        out_shape=[jax.ShapeDtypeStruct((D_IN // 8, D), F32), jax.ShapeDtypeStruct((D_IN, D), BF16)],
        scratch_shapes=[pltpu.VMEM((D_IN, D), F32)],
        compiler_params=_params(),
    )(dproj, x, g_pre, place_arr, *deps)


def _bwd_in_dx(l, dproj, w_in_t, x, g_pre, dres, deps=()):
    n_steps = S // TM

    def body(dp_ref, w_ref, x_ref, g_ref, dres_ref, *rest):
        dx_ref, dg_ref, acc_ref = rest[len(deps):]
        step = pl.program_id(0)

        @pl.when(step == 0)
        def _():
            acc_ref[...] = jnp.zeros_like(acc_ref)

        dh = jnp.dot(dp_ref[...], w_ref[...], preferred_element_type=F32)
        xt = x_ref[...]
        r = lax.rsqrt(jnp.mean(xt * xt, axis=-1, keepdims=True) + EPS)
        xn = xt * r
        acc_ref[...] += _rows8(dh * xn)
        a = dh * g_ref[...]
        dx_ref[...] = dres_ref[...] + (r * a - xt * (r * r * r) * jnp.mean(a * xt, axis=-1, keepdims=True))

        @pl.when(step == n_steps - 1)
        def _():
            _store_lane_rows(dg_ref, acc_ref[...])

    row = lambda w: pl.BlockSpec((TM, w), lambda i: (i, 0))
    full = _whole
    return pl.pallas_call(
        body, name="bwd_in_dx", grid=(n_steps,),
        in_specs=[row(D_IN), full((D_IN, D)), row(D), _layer(l, 1, D), row(D)] + [ANY] * len(deps),
        out_specs=[row(D), full((8, 128))],
        out_shape=[jax.ShapeDtypeStruct((S, D), F32), jax.ShapeDtypeStruct((8, 128), F32)],
        scratch_shapes=[pltpu.VMEM((8, D), F32)],
        compiler_params=_params(),
    )(dproj, w_in_t, x, g_pre, dres, *deps)


HBM =pl.BlockSpec(memory_space=pltpu.HBM)
SEM = pl.BlockSpec(memory_space=pltpu.SEMAPHORE)
SPLIT_COPY = pltpu.CompilerParams(has_side_effects=pltpu.SideEffectType.DATAFLOW_SIDE_EFFECTING)


def _in_hbm(a):
    return pltpu.with_memory_space_constraint(a, pltpu.HBM)

def _place():
    return lax.axis_index("x"), lax.axis_index("y"), lax.axis_index("c")


def _other_chips(x, y):
    return [(1 - x, y), (x, 1 - y), (1 - x, 1 - y)]


def _peer(x, y, c, m):
    return (x ^ (m >> 2), y ^ ((m >> 1) & 1), c ^ (m & 1))


def _place_cast(name, src, chip_arr, tile, layers, deps=()):
    _, n, cols = src.shape
    steps = n // tile
    k = len(layers)

    def body(chip_ref, *refs):
        for s_ref, o_ref in zip(refs[:k], refs[k + len(deps):]):
            o_ref[...] = s_ref[...].astype(BF16)

    def layer_spec(l):
        return pl.BlockSpec((None, tile, cols), lambda i, chip: (l, i, 0))

    return pl.pallas_call(
        body, name=name,
        grid_spec=pltpu.PrefetchScalarGridSpec(
            num_scalar_prefetch=1, grid=(steps,),
            in_specs=[layer_spec(l) for l in layers] + [ANY] * len(deps),
            out_specs=[pl.BlockSpec((tile, cols), lambda i, chip: (chip[0] * steps + i, 0))] * k),
        out_shape=[jax.ShapeDtypeStruct((N_SHARDS * n, cols), BF16)] * k,
        compiler_params=_params(),
    )(chip_arr, *[src] * k, *deps)


def _chip_rows(ref, chip, half=None):
    n = ref.shape[0] // N_SHARDS
    if half is None:
        return ref.at[pl.ds(pl.multiple_of(chip * n, 16), n), :]
    return ref.at[pl.ds(pl.multiple_of(chip * n + half * (n // 2), 16), n // 2), :]


def _gather_start(name, bufs, halved):
    n = len(bufs)

    def body(*refs):
        ins, send, recv, token = refs[:n], refs[n:2 * n], refs[2 * n:3 * n], refs[-1]
        x, y, c = _place()
        for a, buf in enumerate(ins):
            own = _chip_rows(buf, 2 * x + y, c if a in halved else None)
            for j, chip in enumerate(_other_chips(x, y)):
                pltpu.make_async_remote_copy(src_ref=own, dst_ref=own, send_sem=send[a].at[j], recv_sem=recv[a].at[j],
                                             device_id=(*chip, c), device_id_type=MESH).start()
        token[...] = jnp.zeros_like(token)

    outs = pl.pallas_call(
        body, name=name, in_specs=[HBM] * n,
        out_specs=[SEM] * (2 * n) + [HBM] * n + [pl.BlockSpec(memory_space=pltpu.VMEM)],
        out_shape=[pltpu.SemaphoreType.DMA((3,))] * (2 * n) + [pltpu.HBM(b.shape, b.dtype) for b in bufs]
        + [jax.ShapeDtypeStruct((8, 128), F32)],
        input_output_aliases={a: 2 * n + a for a in range(n)},
        compiler_params=SPLIT_COPY,
    )(*[_in_hbm(b) for b in bufs])
    return outs[:n], outs[n:2 * n], outs[2 * n:3 * n], outs[-1]


def _gather_wait(name, buf, send_sem, recv_sem, after, halved=False):
    def body(buf_ref, send_ref, recv_ref, *rest):
        x, y, c = _place()
        half = c if halved else None
        own = _chip_rows(buf_ref, 2 * x + y, half)
        for j, chip in enumerate(_other_chips(x, y)):
            copy = pltpu.make_async_remote_copy(src_ref=own, dst_ref=_chip_rows(buf_ref, 2 * chip[0] + chip[1], half),
                                                send_sem=send_ref.at[j], recv_sem=recv_ref.at[j],
                                                device_id=(*chip, c), device_id_type=MESH)
            copy.wait_send()
            copy.wait_recv()

    return pl.pallas_call(
        body, name=name, in_specs=[HBM, SEM, SEM] + [ANY] * len(after), out_specs=HBM,
        out_shape=pltpu.HBM(buf.shape, buf.dtype), input_output_aliases={0: 0}, compiler_params=SPLIT_COPY,
    )(buf, send_sem, recv_sem, *after)


def _forward_halves(name, buf):
    def body(in_ref, out_ref, send_sems, recv_sems):
        x, y, c = _place()

        def copy(j, chip, half):
            rows = 2 * chip[0] + chip[1]
            return pltpu.make_async_remote_copy(
                src_ref=_chip_rows(in_ref, rows, half), dst_ref=_chip_rows(out_ref, rows, half), send_sem=send_sems.at[j],
                recv_sem=recv_sems.at[j], device_id=(x, y, 1 - c), device_id_type=MESH)

        chips = _other_chips(x, y)
        for j, chip in enumerate(chips):
            copy(j, chip, c).start()
        for j, chip in enumerate(chips):
            copy(j, chip, c).wait_send()
            copy(j, chip, 1 - c).wait_recv()

    return pl.pallas_call(
        body, name=name, in_specs=[ANY], out_specs=ANY, out_shape=jax.ShapeDtypeStruct(buf.shape, buf.dtype),
        input_output_aliases={0: 0},
        scratch_shapes=[pltpu.SemaphoreType.DMA((3,))] * 2,
    )(buf)


def _piece_rows(ref, k):
    p = ref.shape[0] // 8
    return ref.at[pl.ds(pl.multiple_of(k * p, 32 // jnp.dtype(ref.dtype).itemsize), p), :]


def _exchange_start(name, arrays):
    n = len(arrays)
    zones = [lax.empty((7, a.shape[0] // 8, a.shape[1]), a.dtype) for a in arrays]

    def body(*refs):
        srcs, lands = refs[:n], refs[n:2 * n]
        send, recv, token = refs[2 * n:3 * n], refs[3 * n:4 * n], refs[-1]
        x, y, c = _place()
        for a, (src, land) in enumerate(zip(srcs, lands)):
            for m in range(1, 8):
                px, py, pc = _peer(x, y, c, m)
                pltpu.make_async_remote_copy(
                    src_ref=_piece_rows(src, 4 * px + 2 * py + pc), dst_ref=land.at[m - 1], send_sem=send[a].at[m - 1],
                    recv_sem=recv[a].at[m - 1], device_id=(px, py, pc), device_id_type=MESH).start()
        token[...] = jnp.zeros_like(token)

    outs = pl.pallas_call(
        body, name=name, in_specs=[HBM] * (2 * n),
        out_specs=[SEM] * (2 * n) + [HBM] * (2 * n) + [pl.BlockSpec(memory_space=pltpu.VMEM)],
        out_shape=[pltpu.SemaphoreType.DMA((7,))] * (2 * n) + [pltpu.HBM(a.shape, a.dtype) for a in arrays + zones]
        + [jax.ShapeDtypeStruct((8, 128), F32)],
        input_output_aliases={a: 2 * n + a for a in range(2 * n)},
        compiler_params=SPLIT_COPY,
    )(*[_in_hbm(a) for a in arrays + zones])
    return outs[:n], outs[n:2 * n], outs[2 * n:3 * n], outs[3 * n:4 * n], outs[-1]


def _exchange_wait(name, started, after):
    send_sems, recv_sems, arrays, zones, _ = started
    n = len(arrays)

    def body(*refs):
        srcs, lands = refs[:n], refs[n:2 * n]
        send, recv = refs[2 * n:3 * n], refs[3 * n:4 * n]
        x, y, c = _place()
        for a, (src, land) in enumerate(zip(srcs, lands)):
            for m in range(1, 8):
                px, py, pc = _peer(x, y, c, m)
                copy = pltpu.make_async_remote_copy(
                    src_ref=_piece_rows(src, 4 * px + 2 * py + pc), dst_ref=land.at[m - 1], send_sem=send[a].at[m - 1],
                    recv_sem=recv[a].at[m - 1], device_id=(px, py, pc), device_id_type=MESH)
                copy.wait_send()
                copy.wait_recv()

    outs = pl.pallas_call(
        body, name=name, in_specs=[HBM] * (2 * n) + [SEM] * (2 * n) + [ANY], out_specs=[HBM] * (2 * n),
        out_shape=[pltpu.HBM(a.shape, a.dtype) for a in list(arrays) + list(zones)],
        input_output_aliases={a: a for a in range(2 * n)}, compiler_params=SPLIT_COPY,
    )(*arrays, *zones, *send_sems, *recv_sems, after)
    return outs[n:]


def _sum_pieces(name, owns, recvs, place_arr, layer, dests=None):
    n = len(owns)
    steps = 2

    def body(place_ref, *refs):
        for o_ref, r_ref, out_ref in zip(refs[:n], refs[n:2 * n], refs[-n:]):
            total = o_ref[...]
            for m in range(7):
                total = total + r_ref[m].astype(F32)
            out_ref[...] = total

    tiles = [o.shape[0] // steps for o in owns]
    return pl.pallas_call(
        body, name=name,
        grid_spec=pltpu.PrefetchScalarGridSpec(
            num_scalar_prefetch=1, grid=(steps,),
            in_specs=[pl.BlockSpec((t, o.shape[1]), lambda i, place: (i, 0)) for o, t in zip(owns, tiles)]
            + [pl.BlockSpec((7, t, o.shape[1]), lambda i, place: (0, i, 0)) for o, t in zip(owns, tiles)]
            + ([] if dests is None else [ANY] * n),
            out_specs=[pl.BlockSpec((None, t, o.shape[1]), lambda i, place: (layer, place[1] * steps + i, 0))
                       for o, t in zip(owns, tiles)]),
        out_shape=[jax.ShapeDtypeStruct((DEPTH, 2 * o.shape[0], o.shape[1]), F32) for o in owns],
        input_output_aliases={} if dests is None else {1 + 2 * n + k: k for k in range(n)},
        compiler_params=_params(),
    )(place_arr, *owns, *recvs, *(() if dests is None else dests))


def _sum_small(partials, recvs, place_arr):
    n = len(partials)

    def body(place_ref, *refs):
        for o_ref, r_ref, out_ref in zip(refs[:n], refs[n:2 * n], refs[2 * n:]):
            total = o_ref[...]
            for m in range(7):
                total = total + r_ref[m]
            out_ref[...] = total

    piece = lambda a: pl.BlockSpec((a.shape[0] // 8, a.shape[1]), lambda i, place: (place[0], 0))
    return pl.pallas_call(
        body, name="sum_small",
        grid_spec=pltpu.PrefetchScalarGridSpec(
            num_scalar_prefetch=1, grid=(1,),
            in_specs=[piece(a) for a in partials] + [pl.BlockSpec(r.shape, lambda i, place: (0, 0, 0)) for r in recvs],
            out_specs=[piece(a) for a in partials]),
        out_shape=[jax.ShapeDtypeStruct(a.shape, F32) for a in partials],
        compiler_params=_params(),
    )(place_arr, *partials, *recvs)


def _share(name, bufs, parts, gathered=()):
    n, n_g = len(bufs), len(gathered)
    total = n + n_g

    def body(*refs):
        ins, outs = refs[:total], refs[total:2 * total]
        send_sems, recv_sems, send_g, recv_g = refs[2 * total:]
        x, y, c = _place()

        def half(ref, l, which):
            p = ref.shape[1] // 2
            return ref.at[l, pl.ds(pl.multiple_of(which * p, 8), p), :]

        def swap(k, which):
            a, l = parts[k]
            return pltpu.make_async_remote_copy(
                src_ref=half(ins[a], l, which), dst_ref=half(outs[a], l, which), send_sem=send_sems.at[k],
                recv_sem=recv_sems.at[k], device_id=(x, y, 1 - c), device_id_type=MESH)

        def spread(a, m, sender):
            k = 4 * sender[0] + 2 * sender[1] + sender[2]
            return pltpu.make_async_remote_copy(
                src_ref=_piece_rows(ins[n + a], k), dst_ref=_piece_rows(outs[n + a], k), send_sem=send_g.at[7 * a + m - 1],
                recv_sem=recv_g.at[7 * a + m - 1], device_id=_peer(x, y, c, m), device_id_type=MESH)

        for k in range(len(parts)):
            swap(k, c).start()
        for a in range(n_g):
            for m in range(1, 8):
                spread(a, m, (x, y, c)).start()
        for k in range(len(parts)):
            swap(k, c).wait_send()
            swap(k, 1 - c).wait_recv()
        for a in range(n_g):
            for m in range(1, 8):
                spread(a, m, (x, y, c)).wait_send()
                spread(a, m, _peer(x, y, c, m)).wait_recv()

    arrays = list(bufs) + list(gathered)
    return pl.pallas_call(
        body, name=name, in_specs=[ANY] * total, out_specs=[ANY] * total,
        out_shape=[jax.ShapeDtypeStruct(b.shape, F32) for b in arrays],
        input_output_aliases={a: a for a in range(total)},
        scratch_shapes=[pltpu.SemaphoreType.DMA((max(len(parts), 1),))] * 2
        + [pltpu.SemaphoreType.DMA((max(7 * n_g, 1),))] * 2,
    )(*arrays)


def _adamw_math(w, g, m, v):
    nm = ADAM_B1 * m + (1.0 - ADAM_B1) * g
    nv = ADAM_B2 * v + (1.0 - ADAM_B2) * (g * g)
    m_hat = nm / (1.0 - ADAM_B1 ** ADAM_STEP)
    v_hat = nv / (1.0 - ADAM_B2 ** ADAM_STEP)
    return -ADAM_LR * (m_hat / (jnp.sqrt(v_hat) + ADAM_EPS) + ADAM_WD * w), nm, nv


def _adamw(name, w, g, m, v, rows_per_step, first=0, count=None, dests=None, deps=()):
    layers, rows, cols = w.shape
    count = layers if count is None else count

    def body(w_ref, g_ref, m_ref, v_ref, *rest):
        d_ref, nm_ref, nv_ref, g_out_ref = rest[-4:]
        d_ref[...], nm_ref[...], nv_ref[...] = _adamw_math(w_ref[...], g_ref[...], m_ref[...], v_ref[...])
        g_out_ref[...] = g_ref[...]

    spec = pl.BlockSpec((1, rows_per_step, cols), lambda l, i: (first + l, i, 0))
    shape = jax.ShapeDtypeStruct(w.shape, F32)
    dests = () if dests is None else tuple(dests)
    return pl.pallas_call(
        body, name=name, grid=(count, rows // rows_per_step),
        in_specs=[spec] * 4 + [ANY] * (len(dests) + len(deps)), out_specs=[spec] * 4, out_shape=[shape] * 4,
        input_output_aliases={4 + k: k for k in range(len(dests))},
        compiler_params=_params(("arbitrary", "arbitrary")),
    )(w, g, m, v, *dests, *deps)


def _pack_misc(pool_scale, sinks, norm_pre, norm_post):
    sink_rows = jnp.zeros((DEPTH, 8, 128), F32).at[:, 0, 0:N_HEADS].set(sinks).reshape(2 * 8, 128)
    return jnp.concatenate([pool_scale.reshape(8, 128), norm_pre.reshape(16, 128), norm_post.reshape(16, 128),
                            sink_rows, jnp.zeros((8, 128), F32)], axis=0)


def _adamw_misc(w, g, m, v):
    def body(w_ref, g_ref, m_ref, v_ref, *rest):
        outs, (d_ref, nm_ref, nv_ref) = rest[:17], rest[17:]
        d_ref[...], nm_ref[...], nv_ref[...] = _adamw_math(w_ref[...], g_ref[...], m_ref[...], v_ref[...])
        for k, src in enumerate([g_ref, d_ref, nm_ref, nv_ref]):
            scale, sinks, pre, post = outs[4 * k:4 * k + 4]
            for l in range(DEPTH):
                for j in range(4):
                    scale[l:l + 1, j * 128:(j + 1) * 128] = src[MISC_SCALE + 4 * l + j:MISC_SCALE + 4 * l + j + 1, :]
                for j in range(8):
                    pre[l:l + 1, j * 128:(j + 1) * 128] = src[MISC_PRE + 8 * l + j:MISC_PRE + 8 * l + j + 1, :]
                    post[l:l + 1, j * 128:(j + 1) * 128] = src[MISC_POST + 8 * l + j:MISC_POST + 8 * l + j + 1, :]
                sinks[l:l + 1, :] = src[MISC_SINKS + 8 * l:MISC_SINKS + 8 * l + 1, 0:N_HEADS]
        outs[16][...] = g_ref[MISC_LOSS:MISC_LOSS + 1, 0:1]

    vmem = pl.BlockSpec(memory_space=pltpu.VMEM)
    shapes = [(DEPTH, D_POOL), (DEPTH, N_HEADS), (DEPTH, D), (DEPTH, D)] * 4 + [(1, 1)]
    return pl.pallas_call(
        body, name="adamw_misc", in_specs=[vmem] * 4, out_specs=[vmem] * 17,
        out_shape=[jax.ShapeDtypeStruct(s, F32) for s in shapes],
        scratch_shapes=[pltpu.VMEM((MISC_ROWS, 128), F32)] * 3,
    )(w, g, m, v)


def kernel(x, w_in, pool_w, pool_scale, attn_sinks, w_out, norm_pre, norm_post, loss_target, m_w_in, m_pool_w, m_pool_scale, m_attn_sinks, m_w_out, m_norm_pre, m_norm_post, v_w_in, v_pool_w, v_pool_scale, v_attn_sinks, v_w_out, v_norm_pre, v_norm_post):
    cx, cy, cc = _place()
    chip_arr = jnp.reshape(2 * cx + cy, (1,)).astype(jnp.int32)
    place_arr = jnp.stack([4 * cx + 2 * cy + cc, cc]).astype(jnp.int32)
    t = lambda a: jnp.transpose(a, (0, 2, 1))
    w_in_t = t(w_in)
    xs, target = x[0], loss_target[0]
    pool_w_b = pool_w.astype(BF16)
    tables = _attention_tables()
    scale3 = pool_scale.reshape(DEPTH, 1, D_POOL)
    pre3 = norm_pre.reshape(DEPTH, 1, D)
    post3 = norm_post.reshape(DEPTH, 1, D)

    (wi0,) = _place_cast("place_w_in0", w_in_t, chip_arr, 288, [0])
    first = _gather_start("gather_start_first", [wi0], halved=(0,))
    (wi1,) = _place_cast("place_w_in1", w_in_t, chip_arr, 288, [1], deps=(first[3],))
    wo = _place_cast("place_w_out", w_out, chip_arr, 256, [0, 1], deps=(first[3],))
    rest = _gather_start("gather_start_rest", [wo[0], wi1, wo[1]], halved=(1,))
    send, recv, bufs = [first[k] + rest[k] for k in range(3)]
    order = {(0, "in"): 0, (0, "out"): 1, (1, "in"): 2, (1, "out"): 3}

    saved = []
    after = (first[3], rest[3], pool_w_b, *tables, scale3, pre3, post3)
    for l in range(DEPTH):
        k = order[l, "in"]
        w_in_l = _forward_halves(f"forward_w_in{l}", _gather_wait(f"gather_wait_in{l}", bufs[k], send[k], recv[k], after,
                                                                 halved=True))
        pu, pg, q, kv, ag = _fwd_in(l, xs, pre3, w_in_l)
        cat = _fwd_mix(l, pu, pg, q, kv, ag, pool_w_b, scale3, attn_sinks, tables)
        k = order[l, "out"]
        w_out_l = _gather_wait(f"gather_wait_out{l}", bufs[k], send[k], recv[k], (cat,))
        y, x_next = _fwd_out(l, cat, w_out_l, xs, post3) if l < DEPTH - 1 else (None, None)
        saved.append((xs, pu, pg, q, kv, ag, cat, y, w_in_l, w_out_l))
        if l < DEPTH - 1:
            xs, after = x_next, (x_next,)

    x_in, pu, pg, q, kv, ag, cat, y, w_in_l, w_out_l = saved[1]
    dcat, dw_out1, dw_out1_b, dg_post1, loss, xs = _bwd_out(1, cat, w_out_l, post3, place_arr, x=x_in, target=target)
    dproj, dpw, dsc1, dsink1 = _bwd_mix(1, pu, pg, q, kv, ag, dcat, pool_w_b, scale3, attn_sinks, tables)
    dw_in1, dw_in1_b = _bwd_in_dw(1, dproj, x_in, pre3, place_arr)
    ex1 = _exchange_start("exchange_start_1", [dw_in1_b, dw_out1_b])
    dx, dg_pre1 = _bwd_in_dx(1, dproj, w_in_l, x_in, pre3, xs, deps=(ex1[4],))

    x_in, pu, pg, q, kv, ag, cat, y, w_in_l, w_out_l = saved[0]
    dcat, dw_out0, dw_out0_b, dg_post0 = _bwd_out(0, cat, w_out_l, post3, place_arr, dxn=dx, y=y)
    ex0_out = _exchange_start("exchange_start_out0", [dw_out0_b])
    dproj, dpw, dsc0, dsink0 = _bwd_mix(0, pu, pg, q, kv, ag, dcat, pool_w_b, scale3, attn_sinks, tables,
                                        deps=(ex0_out[4],), dpw_dest=dpw)
    recv_in1, recv_out1 = _exchange_wait("exchange_wait_1", ex1, dproj)
    g_in, g_out = _sum_pieces("sum_pieces_1", [dw_in1, dw_out1], [recv_in1, recv_out1], place_arr, 1)
    dw_in0, dw_in0_b = _bwd_in_dw(0, dproj, x_in, pre3, place_arr, deps=(g_in, g_out))
    ex0_in = _exchange_start("exchange_start_in0", [dw_in0_b])

    grad_x, dg_pre0 = _bwd_in_dx(0, dproj, w_in_l, x_in, pre3, dx, deps=(ex0_in[4],))
    small = [dpw.reshape(DEPTH * 4 * 128, 128),
             jnp.concatenate([dsc0, dsc1, dg_pre0, dg_pre1, dg_post0, dg_post1, dsink0, dsink1, loss], axis=0)]
    ex_small = _exchange_start("exchange_start_small", small)
    (recv_out0,) = _exchange_wait("exchange_wait_out0", ex0_out, ex_small[4])
    (g_out,) = _sum_pieces("sum_pieces_out0", [dw_out0], [recv_out0], place_arr, 0, dests=[g_out])
    g_in, g_out = _share("share_a", [g_in, g_out], [(0, 1), (1, 0), (1, 1)])
    m_in_t, v_in_t = t(m_w_in), t(v_w_in)
    d_out, nm_out, nv_out, grad_w_out = _adamw("adamw_w_out", w_out, g_out, m_w_out, v_w_out, 256)
    upd_in = _adamw("adamw_w_in1", w_in_t, g_in, m_in_t, v_in_t, 288, first=1, count=1, deps=(d_out,))

    (recv_in0,) = _exchange_wait("exchange_wait_in0", ex0_in, upd_in[0])
    recv_small = _exchange_wait("exchange_wait_small", ex_small, recv_in0)
    (g_in,) = _sum_pieces("sum_pieces_in0", [dw_in0], [recv_in0], place_arr, 0, dests=[g_in])
    g_in, g_pw, g_misc = _share("share_b", [g_in], [(0, 0)], _sum_small(small, recv_small, place_arr))
    d_in, nm_in, nv_in, grad_w_in_t = _adamw("adamw_w_in0", w_in_t, g_in, m_in_t, v_in_t, 288, first=0, count=1,
                                             dests=upd_in)
    flat = lambda a: a.reshape(1, DEPTH * 4 * 128, 128)
    pw = _adamw("adamw_pool_w", flat(pool_w), flat(g_pw), flat(m_pool_w), flat(v_pool_w), 1024)
    d_pw, m_pw, v_pw, g_pw = [a.reshape(pool_w.shape) for a in pw]
    misc = _adamw_misc(_pack_misc(pool_scale, attn_sinks, norm_pre, norm_post), g_misc,
                       _pack_misc(m_pool_scale, m_attn_sinks, m_norm_pre, m_norm_post),
                       _pack_misc(v_pool_scale, v_attn_sinks, v_norm_pre, v_norm_post))
    (g_sc, g_sk, g_pre, g_post, d_sc, d_sk, d_pre, d_post,
     m_sc, m_sk, m_pre, m_post, v_sc, v_sk, v_pre, v_post, loss_sum) = misc
    return (loss_sum[0, 0], grad_x[None], t(grad_w_in_t), g_pw, g_sc, g_sk, grad_w_out, g_pre, g_post,
            t(d_in), d_pw, d_sc, d_sk, d_out, d_pre, d_post,
            t(nm_in), m_pw, m_sc, m_sk, nm_out, m_pre, m_post,
            t(nv_in), v_pw, v_sc, v_sk, nv_out, v_pre, v_post)
```

```python
import jax
import jax.numpy as jnp
from jax import lax
from jax.experimental import pallas as pl
from jax.experimental.pallas import tpu as pltpu

F32 = jnp.float32
BF16 = jnp.bfloat16

S = 2048
D = 1024
DEPTH = 2
D_POOL = 512
POOL_WINDOWS = (2, 4, 8, 16)
N_HEADS = 8
D_IN = 2304
N_SHARDS = 4
W_IN_SHARD = D_IN // N_SHARDS
W_OUT_SHARD = D // N_SHARDS
BLK = 128
NB = S // BLK
HALO = 16
PAD = 8
EPS = 1e-6
NEG_INF = -1e30
C_PU, C_PG, C_Q, C_K, C_V, C_AG = 0, 512, 1024, 1536, 1664, 1792

ADAM_LR = 0.001
ADAM_B1 = 0.9
ADAM_B2 = 0.999
ADAM_EPS = 1e-08
ADAM_WD = 0.01
ADAM_STEP = 10

TM = 512
VMEM_LIMIT = 56 * 1024 * 1024

NT = (((1,), (1,)), ((), ()))
TN = (((0,), (0,)), ((), ()))

MESH = pl.DeviceIdType.MESH
ANY = pl.BlockSpec(memory_space=pl.ANY)

MISC_SCALE, MISC_PRE, MISC_POST, MISC_SINKS, MISC_LOSS = 0, 8, 24, 40, 56
MISC_ROWS = 64


def _params(sem=("arbitrary",)):
    return pltpu.CompilerParams(dimension_semantics=sem, vmem_limit_bytes=VMEM_LIMIT)


def _sigmoid(v):
    return 1.0 / (1.0 + jnp.exp(-v))


def _rows8(v):
    r, c = v.shape
    return v.reshape(r // 8, 8, c).sum(axis=0)


def _layer(l, *shape):
    zeros = (0,) * len(shape)
    return pl.BlockSpec((None,) + shape, lambda i: (l,) + zeros)


def _whole(shape):
    zeros = (0,) * len(shape)
    return pl.BlockSpec(shape, lambda i: zeros, pipeline_mode=pl.Buffered(1))


def _fwd_in(l, x, g_pre, w_in_t):
    def body(x_ref, g_ref, w_ref, pu_ref, pg_ref, q_ref, kv_ref, ag_ref):
        xt = x_ref[...]
        r = lax.rsqrt(jnp.mean(xt * xt, axis=-1, keepdims=True) + EPS)
        h = (xt * r * g_ref[...]).astype(BF16)

        def proj(lo, hi):
            return lax.dot_general(h, w_ref[lo:hi, :], NT, preferred_element_type=F32)

        pu_ref[...] = proj(C_PU, C_PG)
        pg_ref[...] = proj(C_PG, C_Q)
        q_ref[...] = proj(C_Q, C_K).astype(BF16)
        kv_ref[...] = proj(C_K, C_AG).astype(BF16)
        ag_ref[...] = proj(C_AG, D_IN)

    row = lambda w: pl.BlockSpec((TM, w), lambda i: (i, 0))
    return pl.pallas_call(
        body, name="fwd_in", grid=(S // TM,),
        in_specs=[row(D), _layer(l, 1, D), _whole((D_IN, D))],
        out_specs=[row(512), row(512), row(512), row(256), row(512)],
        out_shape=[jax.ShapeDtypeStruct((S, 512), F32), jax.ShapeDtypeStruct((S, 512), F32),
                   jax.ShapeDtypeStruct((S, 512), BF16), jax.ShapeDtypeStruct((S, 256), BF16),
                   jax.ShapeDtypeStruct((S, 512), F32)],
        compiler_params=_params(),
    )(x, g_pre, w_in_t)


LOG2E = 1.4426950408889634
SCORE_SCALE = 0.125 * LOG2E


def _attention_tables():
    qi = jnp.arange(BLK)[:, None]
    kj = jnp.arange(BLK)[None, :]
    dist = ((qi - kj) % BLK).astype(F32)
    slopes = jnp.exp2(-jnp.arange(1, N_HEADS + 1, dtype=F32))
    bias = -(slopes * LOG2E)[:, None, None] * dist[None]
    first = jnp.where(kj > qi, NEG_INF, bias)
    return jnp.stack([first, bias]), (kj <= qi).astype(BF16)


def _own_block_mask():
    return lax.broadcasted_iota(jnp.int32, (BLK, BLK), 1) <= lax.broadcasted_iota(jnp.int32, (BLK, BLK), 0)


def _merge(full, own):
    return jnp.where(own, full[:, BLK:], full[:, :BLK])


def _spread(v, tri):
    own = v * tri
    return jnp.concatenate([v - own, own], axis=1)


def _head_variants(cur, prev):
    both = jnp.concatenate([prev, cur], axis=0).astype(F32)
    swapped = pltpu.roll(both, 64, axis=1)
    low = lax.broadcasted_iota(jnp.int32, both.shape, 1) < 64
    zero = jnp.zeros_like(both)
    return ((jnp.where(low, both, zero).astype(BF16), jnp.where(low, zero, swapped).astype(BF16)),
            (jnp.where(low, swapped, zero).astype(BF16), jnp.where(low, zero, both).astype(BF16)))


def _head_of(hkv, t, half):
    return hkv * 4 + 2 * t + half


def _rows(v, t):
    return v[t * BLK:(t + 1) * BLK]


def _stack_tiles(ref, hkv, offset=0):
    lo = offset + 2 * hkv * 128
    return jnp.concatenate([ref[:, lo:lo + 128], ref[:, lo + 128:lo + 256]], axis=0)


def _scores(q2, k_var, own):
    s = {}
    for hkv in range(2):
        for half in range(2):
            full = lax.dot_general(q2[hkv], k_var[hkv][half], NT, preferred_element_type=F32)
            for t in range(2):
                s[hkv, t, half] = _merge(_rows(full, t), own)
    return s


def _softmax(s, bias, sink):
    s = s * SCORE_SCALE + bias
    sink2 = sink * LOG2E
    m = jnp.maximum(jnp.max(s, axis=-1, keepdims=True), sink2)
    p = jnp.exp2(s - m)
    e_sink = jnp.exp2(sink2 - m)
    inv = 1.0 / (jnp.sum(p, axis=-1, keepdims=True) + e_sink)
    return p * inv, e_sink * inv


def _spread_pair(v, hkv, half, tri):
    return jnp.concatenate([_spread(v[hkv, t, half].astype(BF16), tri) for t in range(2)], axis=0)


POOL_ROWS = PAD + HALO + BLK


def _window_sums(src_ref, tmp_refs, trailing):
    lo, hi = (PAD, POOL_ROWS) if trailing else (0, HALO + BLK)
    cur = src_ref
    for level in range(len(POOL_WINDOWS)):
        lanes = slice(level * 128, 512)
        shift = -(1 << level) if trailing else (1 << level)
        dst = tmp_refs[level % 2]
        dst[lo:hi, lanes] = cur[lo:hi, lanes] + cur[lo + shift:hi + shift, lanes]
        cur = dst


def _pool_block(ext_ref, tmp_refs, i, g, w):
    lanes = slice(g * 128, (g + 1) * 128)
    rows = slice(PAD + HALO, POOL_ROWS)
    t = (i * BLK + lax.broadcasted_iota(jnp.int32, (BLK, 1), 0)).astype(F32)
    inv = 1.0 / jnp.minimum(t + 1.0, float(w))
    return tmp_refs[g % 2][rows, lanes] * inv - ext_ref[rows, lanes], inv


def _fwd_mix(l, pu, pg, q, kv, ag, pool_w, pool_scale, sinks, tables):
    bias, tri = tables

    def body(pu_ref, pup_ref, pg_ref, q_ref, kv_ref, kvp_ref, ag_ref, pw_ref, sc_ref, sink_ref, bias_ref, tri_ref,
             cat_ref, ext_ref, *tmp_refs):
        i = pl.program_id(0)

        @pl.when(i == 0)
        def _():
            for ref in (ext_ref, *tmp_refs):
                ref[0:PAD, :] = jnp.zeros((PAD, 512), F32)

        ext_ref[PAD:PAD + HALO, :] = jnp.where(i > 0, pup_ref[...], 0.0)
        ext_ref[PAD + HALO:POOL_ROWS, :] = pu_ref[...]
        _window_sums(ext_ref, tmp_refs, True)
        for g, w in enumerate(POOL_WINDOWS):
            lanes = slice(g * 128, (g + 1) * 128)
            pooled, _ = _pool_block(ext_ref, tmp_refs, i, g, w)
            mixed = jnp.dot(pooled.astype(BF16), pw_ref[g], preferred_element_type=F32)
            gate = pg_ref[:, lanes]
            cat_ref[:, lanes] = (mixed * sc_ref[:, lanes] * (gate * _sigmoid(gate))).astype(BF16)

        own = _own_block_mask()
        tri = tri_ref[...]
        k_var = _head_variants(kv_ref[:, 0:128], kvp_ref[:, 0:128])
        v_var = _head_variants(kv_ref[:, 128:256], kvp_ref[:, 128:256])
        s = _scores([_stack_tiles(q_ref, hkv) for hkv in range(2)], k_var, own)
        p = {}
        for (hkv, t, half), s_head in s.items():
            head = _head_of(hkv, t, half)
            p[hkv, t, half], _ = _softmax(s_head, bias_ref[head], sink_ref[l, head])
        for hkv in range(2):
            o2 = jnp.zeros((2 * BLK, 128), F32)
            for half in range(2):
                o2 = o2 + jnp.dot(_spread_pair(p, hkv, half, tri), v_var[hkv][half], preferred_element_type=F32)
            for t in range(2):
                lo = (2 * hkv + t) * 128
                gate = ag_ref[:, lo:lo + 128]
                cat_ref[:, D_POOL + lo:D_POOL + lo + 128] = (_rows(o2, t) * (gate * _sigmoid(gate))).astype(BF16)

    blk = lambda w: pl.BlockSpec((BLK, w), lambda i: (i, 0))
    prev = lambda w: pl.BlockSpec((BLK, w), lambda i: (jnp.maximum(i - 1, 0), 0))
    halo = pl.BlockSpec((HALO, 512), lambda i: (jnp.maximum(i * (BLK // HALO) - 1, 0), 0))
    return pl.pallas_call(
        body, name="fwd_mix", grid=(NB,),
        in_specs=[blk(512), halo, blk(512), blk(512), blk(256), prev(256), blk(512),
                  _layer(l, 4, 128, 128), _layer(l, 1, 512), pl.BlockSpec(memory_space=pltpu.SMEM),
                  pl.BlockSpec((None, N_HEADS, BLK, BLK), lambda i: (jnp.minimum(i, 1), 0, 0, 0)), _whole((BLK, BLK))],
        out_specs=blk(D),
        out_shape=jax.ShapeDtypeStruct((S, D), BF16),
        scratch_shapes=[pltpu.VMEM((POOL_ROWS, 512), F32)] * 3,
        compiler_params=_params(),
    )(pu, pu, pg, q, kv, kv, ag, pool_w, pool_scale, sinks, bias, tri)


def _fwd_out(l, cat, w_out, x, g_post):
    def body(cat_ref, w_ref, x_ref, g_ref, y_ref, xn_ref):
        y = jnp.dot(cat_ref[...], w_ref[...], preferred_element_type=F32)
        y_ref[...] = y
        r = lax.rsqrt(jnp.mean(y * y, axis=-1, keepdims=True) + EPS)
        xn_ref[...] = x_ref[...] + y * r * g_ref[...]

    row = lambda: pl.BlockSpec((TM, D), lambda i: (i, 0))
    act = jax.ShapeDtypeStruct((S, D), F32)
    return pl.pallas_call(
        body, name="fwd_out", grid=(S // TM,),
        in_specs=[row(), _whole((D, D)), row(), _layer(l, 1, D)], out_specs=[row(), row()], out_shape=[act, act],
        compiler_params=_params(),
    )(cat, w_out, x, g_post)


def _store_lane_rows(ref, acc):
    total = jnp.sum(acc, axis=0, keepdims=True)
    for k in range(ref.shape[0]):
        ref[k:k + 1, :] = total[:, k * 128:(k + 1) * 128]


def _own_piece(dw_ref, place_ref):
    p = dw_ref.shape[0] // 8
    return dw_ref[pl.ds(pl.multiple_of(place_ref[0] * p, 8), p), :]


def _bwd_out(l, cat, w_out, g_post, place_arr, dxn=None, y=None, x=None, target=None, deps=()):
    last = target is not None
    n_steps = S // TM

    def body(a_ref, b_ref, g_ref, cat_ref, w_ref, place_ref, *rest):
        dcat_ref, own_ref, dwb_ref, dg_ref = rest[len(deps):len(deps) + 4]
        rest = rest[len(deps) + 4:]
        acc_ref, dw_ref = rest[-2:]
        step = pl.program_id(0)

        @pl.when(step == 0)
        def _():
            dw_ref[...] = jnp.zeros_like(dw_ref)
            acc_ref[...] = jnp.zeros_like(acc_ref)

        cat = cat_ref[...]
        g = g_ref[...]
        y = jnp.dot(cat, w_ref[...], preferred_element_type=F32) if last else b_ref[...]
        r = lax.rsqrt(jnp.mean(y * y, axis=-1, keepdims=True) + EPS)
        if last:
            loss_ref, dx_ref, loss_acc_ref = rest[:3]
            err = a_ref[...] + y * r * g - b_ref[...]

            @pl.when(step == 0)
            def _():
                loss_acc_ref[...] = jnp.zeros_like(loss_acc_ref)

            loss_acc_ref[...] += _rows8(err * err)
            dz = err * (1.0 / D)
            dx_ref[...] = dz
        else:
            dz = a_ref[...]
        a = dz * g
        dy = r * a - y * (r * r * r) * jnp.mean(a * y, axis=-1, keepdims=True)
        acc_ref[...] += _rows8(dz * (y * r))
        dyb = dy.astype(BF16)
        dcat_ref[...] = lax.dot_general(dyb, w_ref[...], NT, preferred_element_type=F32)
        dw_ref[...] += lax.dot_general(cat, dyb, TN, preferred_element_type=F32)

        @pl.when(step == n_steps - 1)
        def _():
            _store_lane_rows(dg_ref, acc_ref[...])
            dwb_ref[...] = dw_ref[...].astype(BF16)
            own_ref[...] = _own_piece(dw_ref, place_ref)
            if last:
                loss_ref[...] = jnp.full((8, 128), (0.5 / D) * jnp.sum(loss_acc_ref[...]), F32)

    row = lambda: pl.BlockSpec((TM, D), lambda i: (i, 0))
    full = _whole
    return pl.pallas_call(
        body, name="out_loss_bwd" if last else "bwd_out", grid=(n_steps,),
        in_specs=[row(), row(), _layer(l, 1, D), row(), full((D, D)), pl.BlockSpec(memory_space=pltpu.SMEM)]
        + [ANY] * len(deps),
        out_specs=[row(), full((D // 8, D)), full((D, D)), full((8, 128))] + ([full((8, 128)), row()] if last else []),
        out_shape=[jax.ShapeDtypeStruct((S, D), F32), jax.ShapeDtypeStruct((D // 8, D), F32),
                   jax.ShapeDtypeStruct((D, D), BF16), jax.ShapeDtypeStruct((8, 128), F32)]
        + ([jax.ShapeDtypeStruct((8, 128), F32), jax.ShapeDtypeStruct((S, D), F32)] if last else []),
        scratch_shapes=([pltpu.VMEM((8, D), F32)] if last else []) + [pltpu.VMEM((8, D), F32), pltpu.VMEM((D, D), F32)],
        compiler_params=_params(),
    )(*((x, target) if last else (dxn, y)), g_post, cat, w_out, place_arr, *deps)


def _bwd_mix(l, pu, pg, q, kv, ag, dcat, pool_w, pool_scale, sinks, tables, deps=(), dpw_dest=None):
    bias, tri = tables
    deps = tuple(deps) + (() if dpw_dest is None else (dpw_dest,))

    def body(pu_ref, pup_ref, pg_ref, q_ref, kv_ref, kvp_ref, ag_ref, dcat_ref, pw_ref, sc_ref, sink_ref, bias_ref,
             tri_ref, *rest):
        dproj_ref, dpw_ref, dsc_ref, dsink_ref, ext_ref, dext_ref, tmp_a, tmp_b, dkv_ref = rest[len(deps):]
        tmp_refs = (tmp_a, tmp_b)
        step = pl.program_id(0)
        i = NB - 1 - step

        @pl.when(step == 0)
        def _():
            dpw_ref[...] = jnp.zeros_like(dpw_ref)
            dsc_ref[...] = jnp.zeros_like(dsc_ref)
            dsink_ref[...] = jnp.zeros_like(dsink_ref)
            for ref in (ext_ref, tmp_a, tmp_b):
                ref[0:PAD, :] = jnp.zeros((PAD, 512), F32)
            dext_ref[BLK:POOL_ROWS, :] = jnp.zeros((HALO + PAD, 512), F32)
            dkv_ref[...] = jnp.zeros_like(dkv_ref)

        ext_ref[PAD:PAD + HALO, :] = jnp.where(i > 0, pup_ref[...], 0.0)
        ext_ref[PAD + HALO:POOL_ROWS, :] = pu_ref[...]
        _window_sums(ext_ref, tmp_refs, True)
        dpooled = []
        for g, w in enumerate(POOL_WINDOWS):
            lanes = slice(g * 128, (g + 1) * 128)
            pooled, inv = _pool_block(ext_ref, tmp_refs, i, g, w)
            pooled_b = pooled.astype(BF16)
            mixed = jnp.dot(pooled_b, pw_ref[g], preferred_element_type=F32)
            scale = sc_ref[:, lanes]
            gate = pg_ref[:, lanes]
            sg = _sigmoid(gate)
            dpo = dcat_ref[:, lanes]
            dproj_ref[:, C_PG + g * 128:C_PG + (g + 1) * 128] = (
                dpo * (mixed * scale) * (sg * (1.0 + gate * (1.0 - sg)))).astype(BF16)
            dms = dpo * (gate * sg)
            dsc_ref[g:g + 1, :] += jnp.sum(dms * mixed, axis=0, keepdims=True)
            dmixed = (dms * scale).astype(BF16)
            dpw_ref[g] += lax.dot_general(pooled_b, dmixed, TN, preferred_element_type=F32)
            dpooled.append(lax.dot_general(dmixed, pw_ref[g], NT, preferred_element_type=F32))
            dext_ref[0:BLK, lanes] = dpooled[g] * inv
        _window_sums(dext_ref, tmp_refs, False)
        for g in range(len(POOL_WINDOWS)):
            lanes = slice(g * 128, (g + 1) * 128)
            dproj_ref[:, C_PU + g * 128:C_PU + (g + 1) * 128] = (tmp_refs[g % 2][0:BLK, lanes] - dpooled[g]).astype(BF16)
        dext_ref[BLK:BLK + HALO, :] = dext_ref[0:HALO, :]

        own = _own_block_mask()
        tri = tri_ref[...]
        k_var = _head_variants(kv_ref[:, 0:128], kvp_ref[:, 0:128])
        v_var = _head_variants(kv_ref[:, 128:256], kvp_ref[:, 128:256])
        q2 = [_stack_tiles(q_ref, hkv) for hkv in range(2)]
        s = _scores(q2, k_var, own)
        p, p_sink = {}, {}
        for key, s_head in s.items():
            head = _head_of(*key)
            p[key], p_sink[key] = _softmax(s_head, bias_ref[head], sink_ref[l, head])

        do2, p_b, dp = [], {}, {}
        for hkv in range(2):
            gate = _stack_tiles(ag_ref, hkv)
            sg = _sigmoid(gate)
            dca = _stack_tiles(dcat_ref, hkv, D_POOL)
            do2.append((dca * (gate * sg)).astype(BF16))
            o2 = jnp.zeros((2 * BLK, 128), F32)
            for half in range(2):
                p_b[hkv, half] = _spread_pair(p, hkv, half, tri)
                o2 = o2 + jnp.dot(p_b[hkv, half], v_var[hkv][half], preferred_element_type=F32)
                full = lax.dot_general(do2[hkv], v_var[hkv][half], NT, preferred_element_type=F32)
                for t in range(2):
                    dp[hkv, t, half] = _merge(_rows(full, t), own)
            dag = dca * o2 * (sg * (1.0 + gate * (1.0 - sg)))
            for t in range(2):
                lo = C_AG + (2 * hkv + t) * 128
                dproj_ref[:, lo:lo + 128] = _rows(dag, t).astype(BF16)

        ds = {}
        for key in p:
            delta = jnp.sum(p[key] * dp[key], axis=-1, keepdims=True)
            ds[key] = p[key] * (dp[key] - delta)
            head = _head_of(*key)
            dsink_ref[0:1, :] += jnp.where(lax.broadcasted_iota(jnp.int32, (1, 128), 1) == head,
                                           -jnp.sum(p_sink[key] * delta, axis=0, keepdims=True), 0.0)

        dk_acc = [[None, None], [None, None]]
        dv_acc = [[None, None], [None, None]]
        for hkv in range(2):
            dq2 = jnp.zeros((2 * BLK, 128), F32)
            for half in range(2):
                ds_b = _spread_pair(ds, hkv, half, tri)
                dq2 = dq2 + jnp.dot(ds_b, k_var[hkv][half], preferred_element_type=F32)
                dk_acc[hkv][half] = lax.dot_general(ds_b, q2[hkv], TN, preferred_element_type=F32)
                dv_acc[hkv][half] = lax.dot_general(p_b[hkv, half], do2[hkv], TN, preferred_element_type=F32)
            for t in range(2):
                lo = C_Q + (2 * hkv + t) * 128
                dproj_ref[:, lo:lo + 128] = (_rows(dq2, t) * 0.125).astype(BF16)

        low = lax.broadcasted_iota(jnp.int32, (2 * BLK, 128), 1) < 64

        def gather_heads(acc):
            return jnp.where(low, acc[0][0] + pltpu.roll(acc[0][1], 64, axis=1),
                             pltpu.roll(acc[1][0], 64, axis=1) + acc[1][1])

        dk = gather_heads(dk_acc) * 0.125
        dv = gather_heads(dv_acc)
        dproj_ref[:, C_K:C_V] = (dk[BLK:, :] + dkv_ref[:, 0:128]).astype(BF16)
        dproj_ref[:, C_V:C_AG] = (dv[BLK:, :] + dkv_ref[:, 128:256]).astype(BF16)
        dkv_ref[:, 0:128] = dk[:BLK, :]
        dkv_ref[:, 128:256] = dv[:BLK, :]

    rev = lambda w: pl.BlockSpec((BLK, w), lambda s: (NB - 1 - s, 0))
    prev = lambda w: pl.BlockSpec((BLK, w), lambda s: (jnp.maximum(NB - 2 - s, 0), 0))
    halo = pl.BlockSpec((HALO, 512), lambda s: (jnp.maximum((NB - 1 - s) * (BLK // HALO) - 1, 0), 0))
    return pl.pallas_call(
        body, name="bwd_mix", grid=(NB,),
        in_specs=[rev(512), halo, rev(512), rev(512), rev(256), prev(256), rev(512), rev(D),
                  _layer(l, 4, 128, 128), _layer(l, 1, 512), pl.BlockSpec(memory_space=pltpu.SMEM),
                  pl.BlockSpec((None, N_HEADS, BLK, BLK), lambda s: (jnp.minimum(NB - 1 - s, 1), 0, 0, 0)),
                  _whole((BLK, BLK))] + [ANY] * len(deps),
        out_specs=[rev(D_IN), _layer(l, 4, 128, 128),
                   pl.BlockSpec((4, 128), lambda s: (0, 0)), pl.BlockSpec((8, 128), lambda s: (0, 0))],
        out_shape=[jax.ShapeDtypeStruct((S, D_IN), BF16), jax.ShapeDtypeStruct((DEPTH, 4, 128, 128), F32),
                   jax.ShapeDtypeStruct((4, 128), F32), jax.ShapeDtypeStruct((8, 128), F32)],
        input_output_aliases={} if dpw_dest is None else {12 + len(deps): 1},
        scratch_shapes=[pltpu.VMEM((POOL_ROWS, 512), F32)] * 4 + [pltpu.VMEM((BLK, 256), F32)],
        compiler_params=_params(),
    )(pu, pu, pg, q, kv, kv, ag, dcat, pool_w, pool_scale, sinks, bias, tri, *deps)


def _bwd_in_dw(l, dproj, x, g_pre, place_arr, deps=()):
    n_steps = S // TM

    def body(dp_ref, x_ref, g_ref, place_ref, *rest):
        own_ref, dwb_ref, dw_ref = rest[len(deps):]
        step = pl.program_id(0)

        @pl.when(step == 0)
        def _():
            dw_ref[...] = jnp.zeros_like(dw_ref)

        xt = x_ref[...]
        r = lax.rsqrt(jnp.mean(xt * xt, axis=-1, keepdims=True) + EPS)
        h = (xt * r * g_ref[...]).astype(BF16)
        dw_ref[...] += lax.dot_general(dp_ref[...], h, TN, preferred_element_type=F32)

        @pl.when(step == n_steps - 1)
        def _():
            dwb_ref[...] = dw_ref[...].astype(BF16)
            own_ref[...] = _own_piece(dw_ref, place_ref)

    row = lambda w: pl.BlockSpec((TM, w), lambda i: (i, 0))
    full = _whole
    return pl.pallas_call(
        body, name="bwd_in_dw", grid=(n_steps,),
        in_specs=[row(D_IN), row(D), _layer(l, 1, D), pl.BlockSpec(memory_space=pltpu.SMEM)] + [ANY] * len(deps),
        out_specs=[full((D_IN // 8, D)), full((D_IN, D))],
        out_shape=[jax.ShapeDtypeStruct((D_IN // 8, D), F32), jax.ShapeDtypeStruct((D_IN, D), BF16)],
        scratch_shapes=[pltpu.VMEM((D_IN, D), F32)],
        compiler_params=_params(),
    )(dproj, x, g_pre, place_arr, *deps)


def _bwd_in_dx(l, dproj, w_in_t, x, g_pre, dres, deps=(), dw_place=None):
    n_steps = S // TM
    with_dw = dw_place is not None

    def body(dp_ref, w_ref, x_ref, g_ref, dres_ref, *rest):
        place_ref = rest[0] if with_dw else None
        rest = rest[with_dw + len(deps):]
        if with_dw:
            dx_ref, dg_ref, own_ref, dwb_ref, acc_ref, dw_ref = rest
        else:
            dx_ref, dg_ref, acc_ref = rest
        step = pl.program_id(0)

        @pl.when(step == 0)
        def _():
            acc_ref[...] = jnp.zeros_like(acc_ref)
            if with_dw:
                dw_ref[...] = jnp.zeros_like(dw_ref)

        dp = dp_ref[...]
        dh = jnp.dot(dp, w_ref[...], preferred_element_type=F32)
        xt = x_ref[...]
        r = lax.rsqrt(jnp.mean(xt * xt, axis=-1, keepdims=True) + EPS)
        xn = xt * r
        g = g_ref[...]
        acc_ref[...] += _rows8(dh * xn)
        a = dh * g
        dx_ref[...] = dres_ref[...] + (r * a - xt * (r * r * r) * jnp.mean(a * xt, axis=-1, keepdims=True))
        if with_dw:
            dw_ref[...] += lax.dot_general(dp, (xn * g).astype(BF16), TN, preferred_element_type=F32)

        @pl.when(step == n_steps - 1)
        def _():
            _store_lane_rows(dg_ref, acc_ref[...])
            if with_dw:
                dwb_ref[...] = dw_ref[...].astype(BF16)
                own_ref[...] = _own_piece(dw_ref, place_ref)

    row = lambda w: pl.BlockSpec((TM, w), lambda i: (i, 0))
    full = _whole
    dw_specs = [full((D_IN // 8, D)), full((D_IN, D))] if with_dw else []
    dw_shapes = [jax.ShapeDtypeStruct((D_IN // 8, D), F32), jax.ShapeDtypeStruct((D_IN, D), BF16)] if with_dw else []
    return pl.pallas_call(
        body, name="bwd_in" if with_dw else "bwd_in_dx", grid=(n_steps,),
        in_specs=[row(D_IN), full((D_IN, D)), row(D), _layer(l, 1, D), row(D)]
        + [pl.BlockSpec(memory_space=pltpu.SMEM)] * with_dw + [ANY] * len(deps),
        out_specs=[row(D), full((8, 128))] + dw_specs,
        out_shape=[jax.ShapeDtypeStruct((S, D), F32), jax.ShapeDtypeStruct((8, 128), F32)] + dw_shapes,
        scratch_shapes=[pltpu.VMEM((8, D), F32)] + [pltpu.VMEM((D_IN, D), F32)] * with_dw,
        compiler_params=_params(),
    )(dproj, w_in_t, x, g_pre, dres, *((dw_place,) if with_dw else ()), *deps)


HBM =pl.BlockSpec(memory_space=pltpu.HBM)
SEM = pl.BlockSpec(memory_space=pltpu.SEMAPHORE)
SPLIT_COPY = pltpu.CompilerParams(has_side_effects=pltpu.SideEffectType.DATAFLOW_SIDE_EFFECTING)


def _in_hbm(a):
    return pltpu.with_memory_space_constraint(a, pltpu.HBM)

def _place():
    return lax.axis_index("x"), lax.axis_index("y"), lax.axis_index("c")


def _other_chips(x, y):
    return [(1 - x, y), (x, 1 - y), (1 - x, 1 - y)]


def _peer(x, y, c, m):
    return (x ^ (m >> 2), y ^ ((m >> 1) & 1), c ^ (m & 1))


def _place_cast(name, src, chip_arr, tile, layers, deps=()):
    _, n, cols = src.shape
    steps = n // tile
    k = len(layers)

    def body(chip_ref, *refs):
        for s_ref, o_ref in zip(refs[:k], refs[k + len(deps):]):
            o_ref[...] = s_ref[...].astype(BF16)

    def layer_spec(l):
        return pl.BlockSpec((None, tile, cols), lambda i, chip: (l, i, 0))

    return pl.pallas_call(
        body, name=name,
        grid_spec=pltpu.PrefetchScalarGridSpec(
            num_scalar_prefetch=1, grid=(steps,),
            in_specs=[layer_spec(l) for l in layers] + [ANY] * len(deps),
            out_specs=[pl.BlockSpec((tile, cols), lambda i, chip: (chip[0] * steps + i, 0))] * k),
        out_shape=[jax.ShapeDtypeStruct((N_SHARDS * n, cols), BF16)] * k,
        compiler_params=_params(),
    )(chip_arr, *[src] * k, *deps)


def _chip_rows(ref, chip, half=None):
    n = ref.shape[0] // N_SHARDS
    if half is None:
        return ref.at[pl.ds(pl.multiple_of(chip * n, 16), n), :]
    return ref.at[pl.ds(pl.multiple_of(chip * n + half * (n // 2), 16), n // 2), :]


def _gather_start(name, bufs, halved):
    n = len(bufs)

    def body(*refs):
        ins, send, recv, token = refs[:n], refs[n:2 * n], refs[2 * n:3 * n], refs[-1]
        x, y, c = _place()
        for a, buf in enumerate(ins):
            own = _chip_rows(buf, 2 * x + y, c if a in halved else None)
            for j, chip in enumerate(_other_chips(x, y)):
                pltpu.make_async_remote_copy(src_ref=own, dst_ref=own, send_sem=send[a].at[j], recv_sem=recv[a].at[j],
                                             device_id=(*chip, c), device_id_type=MESH).start()
        token[...] = jnp.zeros_like(token)

    outs = pl.pallas_call(
        body, name=name, in_specs=[HBM] * n,
        out_specs=[SEM] * (2 * n) + [HBM] * n + [pl.BlockSpec(memory_space=pltpu.VMEM)],
        out_shape=[pltpu.SemaphoreType.DMA((3,))] * (2 * n) + [pltpu.HBM(b.shape, b.dtype) for b in bufs]
        + [jax.ShapeDtypeStruct((8, 128), F32)],
        input_output_aliases={a: 2 * n + a for a in range(n)},
        compiler_params=SPLIT_COPY,
    )(*[_in_hbm(b) for b in bufs])
    return outs[:n], outs[n:2 * n], outs[2 * n:3 * n], outs[-1]


def _gather_wait(name, buf, send_sem, recv_sem, after, halved=False):
    def body(buf_ref, send_ref, recv_ref, *rest):
        x, y, c = _place()
        half = c if halved else None
        own = _chip_rows(buf_ref, 2 * x + y, half)
        for j, chip in enumerate(_other_chips(x, y)):
            copy = pltpu.make_async_remote_copy(src_ref=own, dst_ref=_chip_rows(buf_ref, 2 * chip[0] + chip[1], half),
                                                send_sem=send_ref.at[j], recv_sem=recv_ref.at[j],
                                                device_id=(*chip, c), device_id_type=MESH)
            copy.wait_send()
            copy.wait_recv()

    return pl.pallas_call(
        body, name=name, in_specs=[HBM, SEM, SEM] + [ANY] * len(after), out_specs=HBM,
        out_shape=pltpu.HBM(buf.shape, buf.dtype), input_output_aliases={0: 0}, compiler_params=SPLIT_COPY,
    )(buf, send_sem, recv_sem, *after)


def _forward_halves(name, buf):
    def body(in_ref, out_ref, send_sems, recv_sems):
        x, y, c = _place()

        def copy(j, chip, half):
            rows = 2 * chip[0] + chip[1]
            return pltpu.make_async_remote_copy(
                src_ref=_chip_rows(in_ref, rows, half), dst_ref=_chip_rows(out_ref, rows, half), send_sem=send_sems.at[j],
                recv_sem=recv_sems.at[j], device_id=(x, y, 1 - c), device_id_type=MESH)

        chips = _other_chips(x, y)
        for j, chip in enumerate(chips):
            copy(j, chip, c).start()
        for j, chip in enumerate(chips):
            copy(j, chip, c).wait_send()
            copy(j, chip, 1 - c).wait_recv()

    return pl.pallas_call(
        body, name=name, in_specs=[ANY], out_specs=ANY, out_shape=jax.ShapeDtypeStruct(buf.shape, buf.dtype),
        input_output_aliases={0: 0},
        scratch_shapes=[pltpu.SemaphoreType.DMA((3,))] * 2,
    )(buf)


def _piece_rows(ref, k):
    p = ref.shape[0] // 8
    return ref.at[pl.ds(pl.multiple_of(k * p, 32 // jnp.dtype(ref.dtype).itemsize), p), :]


def _exchange_start(name, arrays):
    n = len(arrays)
    zones = [lax.empty((7, a.shape[0] // 8, a.shape[1]), a.dtype) for a in arrays]

    def body(*refs):
        srcs, lands = refs[:n], refs[n:2 * n]
        send, recv, token = refs[2 * n:3 * n], refs[3 * n:4 * n], refs[-1]
        x, y, c = _place()
        for a, (src, land) in enumerate(zip(srcs, lands)):
            for m in range(1, 8):
                px, py, pc = _peer(x, y, c, m)
                pltpu.make_async_remote_copy(
                    src_ref=_piece_rows(src, 4 * px + 2 * py + pc), dst_ref=land.at[m - 1], send_sem=send[a].at[m - 1],
                    recv_sem=recv[a].at[m - 1], device_id=(px, py, pc), device_id_type=MESH).start()
        token[...] = jnp.zeros_like(token)

    outs = pl.pallas_call(
        body, name=name, in_specs=[HBM] * (2 * n),
        out_specs=[SEM] * (2 * n) + [HBM] * (2 * n) + [pl.BlockSpec(memory_space=pltpu.VMEM)],
        out_shape=[pltpu.SemaphoreType.DMA((7,))] * (2 * n) + [pltpu.HBM(a.shape, a.dtype) for a in arrays + zones]
        + [jax.ShapeDtypeStruct((8, 128), F32)],
        input_output_aliases={a: 2 * n + a for a in range(2 * n)},
        compiler_params=SPLIT_COPY,
    )(*[_in_hbm(a) for a in arrays + zones])
    return outs[:n], outs[n:2 * n], outs[2 * n:3 * n], outs[3 * n:4 * n], outs[-1]


def _exchange_wait(name, started, after):
    send_sems, recv_sems, arrays, zones, _ = started
    n = len(arrays)

    def body(*refs):
        srcs, lands = refs[:n], refs[n:2 * n]
        send, recv = refs[2 * n:3 * n], refs[3 * n:4 * n]
        x, y, c = _place()
        for a, (src, land) in enumerate(zip(srcs, lands)):
            for m in range(1, 8):
                px, py, pc = _peer(x, y, c, m)
                copy = pltpu.make_async_remote_copy(
                    src_ref=_piece_rows(src, 4 * px + 2 * py + pc), dst_ref=land.at[m - 1], send_sem=send[a].at[m - 1],
                    recv_sem=recv[a].at[m - 1], device_id=(px, py, pc), device_id_type=MESH)
                copy.wait_send()
                copy.wait_recv()

    outs = pl.pallas_call(
        body, name=name, in_specs=[HBM] * (2 * n) + [SEM] * (2 * n) + [ANY], out_specs=[HBM] * (2 * n),
        out_shape=[pltpu.HBM(a.shape, a.dtype) for a in list(arrays) + list(zones)],
        input_output_aliases={a: a for a in range(2 * n)}, compiler_params=SPLIT_COPY,
    )(*arrays, *zones, *send_sems, *recv_sems, after)
    return outs[n:]


def _sum_pieces(name, owns, recvs, place_arr, layer, dests=None):
    n = len(owns)
    steps = 2

    def body(place_ref, *refs):
        for o_ref, r_ref, out_ref in zip(refs[:n], refs[n:2 * n], refs[-n:]):
            total = o_ref[...]
            for m in range(7):
                total = total + r_ref[m].astype(F32)
            out_ref[...] = total

    tiles = [o.shape[0] // steps for o in owns]
    return pl.pallas_call(
        body, name=name,
        grid_spec=pltpu.PrefetchScalarGridSpec(
            num_scalar_prefetch=1, grid=(steps,),
            in_specs=[pl.BlockSpec((t, o.shape[1]), lambda i, place: (i, 0)) for o, t in zip(owns, tiles)]
            + [pl.BlockSpec((7, t, o.shape[1]), lambda i, place: (0, i, 0)) for o, t in zip(owns, tiles)]
            + ([] if dests is None else [ANY] * n),
            out_specs=[pl.BlockSpec((None, t, o.shape[1]), lambda i, place: (layer, place[1] * steps + i, 0))
                       for o, t in zip(owns, tiles)]),
        out_shape=[jax.ShapeDtypeStruct((DEPTH, 2 * o.shape[0], o.shape[1]), F32) for o in owns],
        input_output_aliases={} if dests is None else {1 + 2 * n + k: k for k in range(n)},
        compiler_params=_params(),
    )(place_arr, *owns, *recvs, *(() if dests is None else dests))


def _sum_small(partials, recvs, place_arr):
    n = len(partials)

    def body(place_ref, *refs):
        for o_ref, r_ref, out_ref in zip(refs[:n], refs[n:2 * n], refs[2 * n:]):
            total = o_ref[...]
            for m in range(7):
                total = total + r_ref[m]
            out_ref[...] = total

    piece = lambda a: pl.BlockSpec((a.shape[0] // 8, a.shape[1]), lambda i, place: (place[0], 0))
    return pl.pallas_call(
        body, name="sum_small",
        grid_spec=pltpu.PrefetchScalarGridSpec(
            num_scalar_prefetch=1, grid=(1,),
            in_specs=[piece(a) for a in partials] + [pl.BlockSpec(r.shape, lambda i, place: (0, 0, 0)) for r in recvs],
            out_specs=[piece(a) for a in partials]),
        out_shape=[jax.ShapeDtypeStruct(a.shape, F32) for a in partials],
        compiler_params=_params(),
    )(place_arr, *partials, *recvs)


def _share(name, bufs, parts, gathered=()):
    n, n_g = len(bufs), len(gathered)
    total = n + n_g

    def body(*refs):
        ins, outs = refs[:total], refs[total:2 * total]
        send_sems, recv_sems, send_g, recv_g = refs[2 * total:]
        x, y, c = _place()

        def half(ref, l, which):
            p = ref.shape[1] // 2
            return ref.at[l, pl.ds(pl.multiple_of(which * p, 8), p), :]

        def swap(k, which):
            a, l = parts[k]
            return pltpu.make_async_remote_copy(
                src_ref=half(ins[a], l, which), dst_ref=half(outs[a], l, which), send_sem=send_sems.at[k],
                recv_sem=recv_sems.at[k], device_id=(x, y, 1 - c), device_id_type=MESH)

        def spread(a, m, sender):
            k = 4 * sender[0] + 2 * sender[1] + sender[2]
            return pltpu.make_async_remote_copy(
                src_ref=_piece_rows(ins[n + a], k), dst_ref=_piece_rows(outs[n + a], k), send_sem=send_g.at[7 * a + m - 1],
                recv_sem=recv_g.at[7 * a + m - 1], device_id=_peer(x, y, c, m), device_id_type=MESH)

        for k in range(len(parts)):
            swap(k, c).start()
        for a in range(n_g):
            for m in range(1, 8):
                spread(a, m, (x, y, c)).start()
        for k in range(len(parts)):
            swap(k, c).wait_send()
            swap(k, 1 - c).wait_recv()
        for a in range(n_g):
            for m in range(1, 8):
                spread(a, m, (x, y, c)).wait_send()
                spread(a, m, _peer(x, y, c, m)).wait_recv()

    arrays = list(bufs) + list(gathered)
    return pl.pallas_call(
        body, name=name, in_specs=[ANY] * total, out_specs=[ANY] * total,
        out_shape=[jax.ShapeDtypeStruct(b.shape, F32) for b in arrays],
        input_output_aliases={a: a for a in range(total)},
        scratch_shapes=[pltpu.SemaphoreType.DMA((max(len(parts), 1),))] * 2
        + [pltpu.SemaphoreType.DMA((max(7 * n_g, 1),))] * 2,
    )(*arrays)


def _adamw_math(w, g, m, v):
    nm = ADAM_B1 * m + (1.0 - ADAM_B1) * g
    nv = ADAM_B2 * v + (1.0 - ADAM_B2) * (g * g)
    m_hat = nm / (1.0 - ADAM_B1 ** ADAM_STEP)
    v_hat = nv / (1.0 - ADAM_B2 ** ADAM_STEP)
    return -ADAM_LR * (m_hat / (jnp.sqrt(v_hat) + ADAM_EPS) + ADAM_WD * w), nm, nv


def _adamw(name, w, g, m, v, rows_per_step, first=0, count=None, dests=None, deps=()):
    layers, rows, cols = w.shape
    count = layers if count is None else count

    def body(w_ref, g_ref, m_ref, v_ref, *rest):
        d_ref, nm_ref, nv_ref, g_out_ref = rest[-4:]
        d_ref[...], nm_ref[...], nv_ref[...] = _adamw_math(w_ref[...], g_ref[...], m_ref[...], v_ref[...])
        g_out_ref[...] = g_ref[...]

    spec = pl.BlockSpec((1, rows_per_step, cols), lambda l, i: (first + l, i, 0))
    shape = jax.ShapeDtypeStruct(w.shape, F32)
    dests = () if dests is None else tuple(dests)
    return pl.pallas_call(
        body, name=name, grid=(count, rows // rows_per_step),
        in_specs=[spec] * 4 + [ANY] * (len(dests) + len(deps)), out_specs=[spec] * 4, out_shape=[shape] * 4,
        input_output_aliases={4 + k: k for k in range(len(dests))},
        compiler_params=_params(("arbitrary", "arbitrary")),
    )(w, g, m, v, *dests, *deps)


def _pack_misc(pool_scale, sinks, norm_pre, norm_post):
    sink_rows = jnp.zeros((DEPTH, 8, 128), F32).at[:, 0, 0:N_HEADS].set(sinks).reshape(2 * 8, 128)
    return jnp.concatenate([pool_scale.reshape(8, 128), norm_pre.reshape(16, 128), norm_post.reshape(16, 128),
                            sink_rows, jnp.zeros((8, 128), F32)], axis=0)


def _adamw_misc(w, g, m, v):
    def body(w_ref, g_ref, m_ref, v_ref, *rest):
        outs, (d_ref, nm_ref, nv_ref) = rest[:17], rest[17:]
        d_ref[...], nm_ref[...], nv_ref[...] = _adamw_math(w_ref[...], g_ref[...], m_ref[...], v_ref[...])
        for k, src in enumerate([g_ref, d_ref, nm_ref, nv_ref]):
            scale, sinks, pre, post = outs[4 * k:4 * k + 4]
            for l in range(DEPTH):
                for j in range(4):
                    scale[l:l + 1, j * 128:(j + 1) * 128] = src[MISC_SCALE + 4 * l + j:MISC_SCALE + 4 * l + j + 1, :]
                for j in range(8):
                    pre[l:l + 1, j * 128:(j + 1) * 128] = src[MISC_PRE + 8 * l + j:MISC_PRE + 8 * l + j + 1, :]
                    post[l:l + 1, j * 128:(j + 1) * 128] = src[MISC_POST + 8 * l + j:MISC_POST + 8 * l + j + 1, :]
                sinks[l:l + 1, :] = src[MISC_SINKS + 8 * l:MISC_SINKS + 8 * l + 1, 0:N_HEADS]
        outs[16][...] = g_ref[MISC_LOSS:MISC_LOSS + 1, 0:1]

    vmem = pl.BlockSpec(memory_space=pltpu.VMEM)
    shapes = [(DEPTH, D_POOL), (DEPTH, N_HEADS), (DEPTH, D), (DEPTH, D)] * 4 + [(1, 1)]
    return pl.pallas_call(
        body, name="adamw_misc", in_specs=[vmem] * 4, out_specs=[vmem] * 17,
        out_shape=[jax.ShapeDtypeStruct(s, F32) for s in shapes],
        scratch_shapes=[pltpu.VMEM((MISC_ROWS, 128), F32)] * 3,
    )(w, g, m, v)


def kernel(x, w_in, pool_w, pool_scale, attn_sinks, w_out, norm_pre, norm_post, loss_target, m_w_in, m_pool_w, m_pool_scale, m_attn_sinks, m_w_out, m_norm_pre, m_norm_post, v_w_in, v_pool_w, v_pool_scale, v_attn_sinks, v_w_out, v_norm_pre, v_norm_post):
    cx, cy, cc = _place()
    chip_arr = jnp.reshape(2 * cx + cy, (1,)).astype(jnp.int32)
    place_arr = jnp.stack([4 * cx + 2 * cy + cc, cc]).astype(jnp.int32)
    t = lambda a: jnp.transpose(a, (0, 2, 1))
    w_in_t = t(w_in)
    xs, target = x[0], loss_target[0]
    pool_w_b = pool_w.astype(BF16)
    tables = _attention_tables()
    scale3 = pool_scale.reshape(DEPTH, 1, D_POOL)
    pre3 = norm_pre.reshape(DEPTH, 1, D)
    post3 = norm_post.reshape(DEPTH, 1, D)

    (wi0,) = _place_cast("place_w_in0", w_in_t, chip_arr, 288, [0])
    first = _gather_start("gather_start_first", [wi0], halved=(0,))
    (wi1,) = _place_cast("place_w_in1", w_in_t, chip_arr, 288, [1], deps=(first[3],))
    wo = _place_cast("place_w_out", w_out, chip_arr, 256, [0, 1], deps=(first[3],))
    rest = _gather_start("gather_start_rest", [wo[0], wi1, wo[1]], halved=(1,))
    send, recv, bufs = [first[k] + rest[k] for k in range(3)]
    order = {(0, "in"): 0, (0, "out"): 1, (1, "in"): 2, (1, "out"): 3}

    saved = []
    packed = [_pack_misc(pool_scale, attn_sinks, norm_pre, norm_post),
              _pack_misc(m_pool_scale, m_attn_sinks, m_norm_pre, m_norm_post),
              _pack_misc(v_pool_scale, v_attn_sinks, v_norm_pre, v_norm_post)]
    after = (first[3], rest[3], pool_w_b, *tables, scale3, pre3, post3, *packed)
    for l in range(DEPTH):
        k = order[l, "in"]
        w_in_l = _forward_halves(f"forward_w_in{l}", _gather_wait(f"gather_wait_in{l}", bufs[k], send[k], recv[k], after,
                                                                 halved=True))
        pu, pg, q, kv, ag = _fwd_in(l, xs, pre3, w_in_l)
        cat = _fwd_mix(l, pu, pg, q, kv, ag, pool_w_b, scale3, attn_sinks, tables)
        k = order[l, "out"]
        w_out_l = _gather_wait(f"gather_wait_out{l}", bufs[k], send[k], recv[k], (cat,))
        y, x_next = _fwd_out(l, cat, w_out_l, xs, post3) if l < DEPTH - 1 else (None, None)
        saved.append((xs, pu, pg, q, kv, ag, cat, y, w_in_l, w_out_l))
        if l < DEPTH - 1:
            xs, after = x_next, (x_next,)

    x_in, pu, pg, q, kv, ag, cat, y, w_in_l, w_out_l = saved[1]
    dcat, dw_out1, dw_out1_b, dg_post1, loss, xs = _bwd_out(1, cat, w_out_l, post3, place_arr, x=x_in, target=target)
    dproj, dpw, dsc1, dsink1 = _bwd_mix(1, pu, pg, q, kv, ag, dcat, pool_w_b, scale3, attn_sinks, tables)
    dx, dg_pre1, dw_in1, dw_in1_b = _bwd_in_dx(1, dproj, w_in_l, x_in, pre3, xs, dw_place=place_arr)
    ex1 = _exchange_start("exchange_start_1", [dw_in1_b, dw_out1_b])

    x_in, pu, pg, q, kv, ag, cat, y, w_in_l, w_out_l = saved[0]
    dcat, dw_out0, dw_out0_b, dg_post0 = _bwd_out(0, cat, w_out_l, post3, place_arr, dxn=dx, y=y, deps=(ex1[4],))
    ex0_out = _exchange_start("exchange_start_out0", [dw_out0_b])
    dproj, dpw, dsc0, dsink0 = _bwd_mix(0, pu, pg, q, kv, ag, dcat, pool_w_b, scale3, attn_sinks, tables,
                                        deps=(ex0_out[4],), dpw_dest=dpw)
    recv_in1, recv_out1 = _exchange_wait("exchange_wait_1", ex1, dproj)
    g_in, g_out = _sum_pieces("sum_pieces_1", [dw_in1, dw_out1], [recv_in1, recv_out1], place_arr, 1)
    dw_in0, dw_in0_b = _bwd_in_dw(0, dproj, x_in, pre3, place_arr, deps=(g_in, g_out))
    ex0_in = _exchange_start("exchange_start_in0", [dw_in0_b])

    grad_x, dg_pre0 = _bwd_in_dx(0, dproj, w_in_l, x_in, pre3, dx, deps=(ex0_in[4],))
    small = [dpw.reshape(DEPTH * 4 * 128, 128),
             jnp.concatenate([dsc0, dsc1, dg_pre0, dg_pre1, dg_post0, dg_post1, dsink0, dsink1, loss], axis=0)]
    ex_small = _exchange_start("exchange_start_small", small)
    (recv_out0,) = _exchange_wait("exchange_wait_out0", ex0_out, ex_small[4])
    (g_out,) = _sum_pieces("sum_pieces_out0", [dw_out0], [recv_out0], place_arr, 0, dests=[g_out])
    g_in, g_out = _share("share_a", [g_in, g_out], [(0, 1), (1, 0), (1, 1)])
    m_in_t, v_in_t = t(m_w_in), t(v_w_in)
    d_out, nm_out, nv_out, grad_w_out = _adamw("adamw_w_out", w_out, g_out, m_w_out, v_w_out, 256)
    upd_in = _adamw("adamw_w_in1", w_in_t, g_in, m_in_t, v_in_t, 288, first=1, count=1, deps=(d_out,))

    (recv_in0,) = _exchange_wait("exchange_wait_in0", ex0_in, upd_in[0])
    recv_small = _exchange_wait("exchange_wait_small", ex_small, recv_in0)
    (g_in,) = _sum_pieces("sum_pieces_in0", [dw_in0], [recv_in0], place_arr, 0, dests=[g_in])
    g_in, g_pw, g_misc = _share("share_b", [g_in], [(0, 0)], _sum_small(small, recv_small, place_arr))
    d_in, nm_in, nv_in, grad_w_in_t = _adamw("adamw_w_in0", w_in_t, g_in, m_in_t, v_in_t, 288, first=0, count=1,
                                             dests=upd_in)
    flat = lambda a: a.reshape(1, DEPTH * 4 * 128, 128)
    pw = _adamw("adamw_pool_w", flat(pool_w), flat(g_pw), flat(m_pool_w), flat(v_pool_w), 1024)
    d_pw, m_pw, v_pw, g_pw = [a.reshape(pool_w.shape) for a in pw]
    misc = _adamw_misc(packed[0], g_misc, packed[1], packed[2])
    (g_sc, g_sk, g_pre, g_post, d_sc, d_sk, d_pre, d_post,
     m_sc, m_sk, m_pre, m_post, v_sc, v_sk, v_pre, v_post, loss_sum) = misc
    return (loss_sum[0, 0], grad_x[None], t(grad_w_in_t), g_pw, g_sc, g_sk, grad_w_out, g_pre, g_post,
            t(d_in), d_pw, d_sc, d_sk, d_out, d_pre, d_post,
            t(nm_in), m_pw, m_sc, m_sk, nm_out, m_pre, m_post,
            t(nv_in), v_pw, v_sc, v_sk, nv_out, v_pre, v_post)
```

```python
import jax
import jax.numpy as jnp
from jax import lax
from jax.experimental import pallas as pl
from jax.experimental.pallas import tpu as pltpu

F32 = jnp.float32
BF16 = jnp.bfloat16

S = 2048
D = 1024
DEPTH = 2
D_POOL = 512
POOL_WINDOWS = (2, 4, 8, 16)
N_HEADS = 8
D_IN = 2304
N_SHARDS = 4
W_IN_SHARD = D_IN // N_SHARDS
W_OUT_SHARD = D // N_SHARDS
BLK = 128
NB = S // BLK
HALO = 16
PAD = 8
EPS = 1e-6
NEG_INF = -1e30
C_PU, C_PG, C_Q, C_K, C_V, C_AG = 0, 512, 1024, 1536, 1664, 1792

ADAM_LR = 0.001
ADAM_B1 = 0.9
ADAM_B2 = 0.999
ADAM_EPS = 1e-08
ADAM_WD = 0.01
ADAM_STEP = 10

TM = 512
VMEM_LIMIT = 56 * 1024 * 1024

NT = (((1,), (1,)), ((), ()))
TN = (((0,), (0,)), ((), ()))

MESH = pl.DeviceIdType.MESH
ANY = pl.BlockSpec(memory_space=pl.ANY)

MISC_SCALE, MISC_PRE, MISC_POST, MISC_SINKS, MISC_LOSS = 0, 8, 24, 40, 56
MISC_ROWS = 64


def _params(sem=("arbitrary",)):
    return pltpu.CompilerParams(dimension_semantics=sem, vmem_limit_bytes=VMEM_LIMIT)


def _sigmoid(v):
    return 1.0 / (1.0 + jnp.exp(-v))


def _rows8(v):
    r, c = v.shape
    return v.reshape(r // 8, 8, c).sum(axis=0)


def _layer(l, *shape):
    zeros = (0,) * len(shape)
    return pl.BlockSpec((None,) + shape, lambda i: (l,) + zeros)


def _whole(shape):
    zeros = (0,) * len(shape)
    return pl.BlockSpec(shape, lambda i: zeros, pipeline_mode=pl.Buffered(1))


def _fwd_in(l, x, g_pre, w_in_t):
    def body(x_ref, g_ref, w_ref, pu_ref, pg_ref, q_ref, kv_ref, ag_ref):
        xt = x_ref[...]
        r = lax.rsqrt(jnp.mean(xt * xt, axis=-1, keepdims=True) + EPS)
        h = (xt * r * g_ref[...]).astype(BF16)

        def proj(lo, hi):
            return lax.dot_general(h, w_ref[lo:hi, :], NT, preferred_element_type=F32)

        pu_ref[...] = proj(C_PU, C_PG)
        pg_ref[...] = proj(C_PG, C_Q)
        q_ref[...] = proj(C_Q, C_K).astype(BF16)
        kv_ref[...] = proj(C_K, C_AG).astype(BF16)
        ag_ref[...] = proj(C_AG, D_IN)

    row = lambda w: pl.BlockSpec((TM, w), lambda i: (i, 0))
    return pl.pallas_call(
        body, name="fwd_in", grid=(S // TM,),
        in_specs=[row(D), _layer(l, 1, D), _whole((D_IN, D))],
        out_specs=[row(512), row(512), row(512), row(256), row(512)],
        out_shape=[jax.ShapeDtypeStruct((S, 512), F32), jax.ShapeDtypeStruct((S, 512), F32),
                   jax.ShapeDtypeStruct((S, 512), BF16), jax.ShapeDtypeStruct((S, 256), BF16),
                   jax.ShapeDtypeStruct((S, 512), F32)],
        compiler_params=_params(),
    )(x, g_pre, w_in_t)


LOG2E = 1.4426950408889634
SCORE_SCALE = 0.125 * LOG2E


def _attention_tables():
    qi = jnp.arange(BLK)[:, None]
    kj = jnp.arange(BLK)[None, :]
    dist = ((qi - kj) % BLK).astype(F32)
    slopes = jnp.exp2(-jnp.arange(1, N_HEADS + 1, dtype=F32))
    bias = -(slopes * LOG2E)[:, None, None] * dist[None]
    first = jnp.where(kj > qi, NEG_INF, bias)
    return jnp.stack([first, bias]), (kj <= qi).astype(BF16)


def _own_block_mask():
    return lax.broadcasted_iota(jnp.int32, (BLK, BLK), 1) <= lax.broadcasted_iota(jnp.int32, (BLK, BLK), 0)


def _merge(full, own):
    return jnp.where(own, full[:, BLK:], full[:, :BLK])


def _spread(v, tri):
    own = v * tri
    return jnp.concatenate([v - own, own], axis=1)


def _head_variants(cur, prev):
    both = jnp.concatenate([prev, cur], axis=0).astype(F32)
    swapped = pltpu.roll(both, 64, axis=1)
    low = lax.broadcasted_iota(jnp.int32, both.shape, 1) < 64
    zero = jnp.zeros_like(both)
    return ((jnp.where(low, both, zero).astype(BF16), jnp.where(low, zero, swapped).astype(BF16)),
            (jnp.where(low, swapped, zero).astype(BF16), jnp.where(low, zero, both).astype(BF16)))


def _head_of(hkv, t, half):
    return hkv * 4 + 2 * t + half


def _rows(v, t):
    return v[t * BLK:(t + 1) * BLK]


def _stack_tiles(ref, hkv, offset=0):
    lo = offset + 2 * hkv * 128
    return jnp.concatenate([ref[:, lo:lo + 128], ref[:, lo + 128:lo + 256]], axis=0)


def _scores(q2, k_var, own):
    s = {}
    for hkv in range(2):
        for half in range(2):
            full = lax.dot_general(q2[hkv], k_var[hkv][half], NT, preferred_element_type=F32)
            for t in range(2):
                s[hkv, t, half] = _merge(_rows(full, t), own)
    return s


def _softmax(s, bias, sink):
    s = s * SCORE_SCALE + bias
    sink2 = sink * LOG2E
    m = jnp.maximum(jnp.max(s, axis=-1, keepdims=True), sink2)
    p = jnp.exp2(s - m)
    e_sink = jnp.exp2(sink2 - m)
    inv = 1.0 / (jnp.sum(p, axis=-1, keepdims=True) + e_sink)
    return p * inv, e_sink * inv


def _spread_pair(v, hkv, half, tri):
    return jnp.concatenate([_spread(v[hkv, t, half].astype(BF16), tri) for t in range(2)], axis=0)


POOL_ROWS = PAD + HALO + BLK


def _window_sums(src_ref, tmp_refs, trailing):
    lo, hi = (PAD, POOL_ROWS) if trailing else (0, HALO + BLK)
    cur = src_ref
    for level in range(len(POOL_WINDOWS)):
        lanes = slice(level * 128, 512)
        shift = -(1 << level) if trailing else (1 << level)
        dst = tmp_refs[level % 2]
        dst[lo:hi, lanes] = cur[lo:hi, lanes] + cur[lo + shift:hi + shift, lanes]
        cur = dst


def _pool_block(ext_ref, tmp_refs, i, g, w):
    lanes = slice(g * 128, (g + 1) * 128)
    rows = slice(PAD + HALO, POOL_ROWS)
    t = (i * BLK + lax.broadcasted_iota(jnp.int32, (BLK, 1), 0)).astype(F32)
    inv = 1.0 / jnp.minimum(t + 1.0, float(w))
    return tmp_refs[g % 2][rows, lanes] * inv - ext_ref[rows, lanes], inv


def _fwd_mix(l, pu, pg, q, kv, ag, pool_w, pool_scale, sinks, tables):
    bias, tri = tables

    def body(pu_ref, pup_ref, pg_ref, q_ref, kv_ref, kvp_ref, ag_ref, pw_ref, sc_ref, sink_ref, bias_ref, tri_ref,
             cat_ref, ext_ref, *tmp_refs):
        i = pl.program_id(0)

        @pl.when(i == 0)
        def _():
            for ref in (ext_ref, *tmp_refs):
                ref[0:PAD, :] = jnp.zeros((PAD, 512), F32)

        ext_ref[PAD:PAD + HALO, :] = jnp.where(i > 0, pup_ref[...], 0.0)
        ext_ref[PAD + HALO:POOL_ROWS, :] = pu_ref[...]
        _window_sums(ext_ref, tmp_refs, True)
        for g, w in enumerate(POOL_WINDOWS):
            lanes = slice(g * 128, (g + 1) * 128)
            pooled, _ = _pool_block(ext_ref, tmp_refs, i, g, w)
            mixed = jnp.dot(pooled.astype(BF16), pw_ref[g], preferred_element_type=F32)
            gate = pg_ref[:, lanes]
            cat_ref[:, lanes] = (mixed * sc_ref[:, lanes] * (gate * _sigmoid(gate))).astype(BF16)

        own = _own_block_mask()
        tri = tri_ref[...]
        k_var = _head_variants(kv_ref[:, 0:128], kvp_ref[:, 0:128])
        v_var = _head_variants(kv_ref[:, 128:256], kvp_ref[:, 128:256])
        s = _scores([_stack_tiles(q_ref, hkv) for hkv in range(2)], k_var, own)
        p = {}
        for (hkv, t, half), s_head in s.items():
            head = _head_of(hkv, t, half)
            p[hkv, t, half], _ = _softmax(s_head, bias_ref[head], sink_ref[l, head])
        for hkv in range(2):
            o2 = jnp.zeros((2 * BLK, 128), F32)
            for half in range(2):
                o2 = o2 + jnp.dot(_spread_pair(p, hkv, half, tri), v_var[hkv][half], preferred_element_type=F32)
            for t in range(2):
                lo = (2 * hkv + t) * 128
                gate = ag_ref[:, lo:lo + 128]
                cat_ref[:, D_POOL + lo:D_POOL + lo + 128] = (_rows(o2, t) * (gate * _sigmoid(gate))).astype(BF16)

    blk = lambda w: pl.BlockSpec((BLK, w), lambda i: (i, 0))
    prev = lambda w: pl.BlockSpec((BLK, w), lambda i: (jnp.maximum(i - 1, 0), 0))
    halo = pl.BlockSpec((HALO, 512), lambda i: (jnp.maximum(i * (BLK // HALO) - 1, 0), 0))
    return pl.pallas_call(
        body, name="fwd_mix", grid=(NB,),
        in_specs=[blk(512), halo, blk(512), blk(512), blk(256), prev(256), blk(512),
                  _layer(l, 4, 128, 128), _layer(l, 1, 512), pl.BlockSpec(memory_space=pltpu.SMEM),
                  pl.BlockSpec((None, N_HEADS, BLK, BLK), lambda i: (jnp.minimum(i, 1), 0, 0, 0)), _whole((BLK, BLK))],
        out_specs=blk(D),
        out_shape=jax.ShapeDtypeStruct((S, D), BF16),
        scratch_shapes=[pltpu.VMEM((POOL_ROWS, 512), F32)] * 3,
        compiler_params=_params(),
    )(pu, pu, pg, q, kv, kv, ag, pool_w, pool_scale, sinks, bias, tri)


def _fwd_out(l, cat, w_out, x, g_post):
    def body(cat_ref, w_ref, x_ref, g_ref, y_ref, xn_ref):
        y = jnp.dot(cat_ref[...], w_ref[...], preferred_element_type=F32)
        y_ref[...] = y
        r = lax.rsqrt(jnp.mean(y * y, axis=-1, keepdims=True) + EPS)
        xn_ref[...] = x_ref[...] + y * r * g_ref[...]

    row = lambda: pl.BlockSpec((TM, D), lambda i: (i, 0))
    act = jax.ShapeDtypeStruct((S, D), F32)
    return pl.pallas_call(
        body, name="fwd_out", grid=(S // TM,),
        in_specs=[row(), _whole((D, D)), row(), _layer(l, 1, D)], out_specs=[row(), row()], out_shape=[act, act],
        compiler_params=_params(),
    )(cat, w_out, x, g_post)


def _store_lane_rows(ref, acc):
    total = jnp.sum(acc, axis=0, keepdims=True)
    for k in range(ref.shape[0]):
        ref[k:k + 1, :] = total[:, k * 128:(k + 1) * 128]


def _own_piece(dw_ref, place_ref):
    p = dw_ref.shape[0] // 8
    return dw_ref[pl.ds(pl.multiple_of(place_ref[0] * p, 8), p), :]


def _bwd_out(l, cat, w_out, g_post, place_arr, dxn=None, y=None, x=None, target=None, deps=()):
    last = target is not None
    n_steps = S // TM

    def body(a_ref, b_ref, g_ref, cat_ref, w_ref, place_ref, *rest):
        dcat_ref, own_ref, dwb_ref, dg_ref = rest[len(deps):len(deps) + 4]
        rest = rest[len(deps) + 4:]
        acc_ref, dw_ref = rest[-2:]
        step = pl.program_id(0)

        @pl.when(step == 0)
        def _():
            dw_ref[...] = jnp.zeros_like(dw_ref)
            acc_ref[...] = jnp.zeros_like(acc_ref)

        cat = cat_ref[...]
        g = g_ref[...]
        y = jnp.dot(cat, w_ref[...], preferred_element_type=F32) if last else b_ref[...]
        r = lax.rsqrt(jnp.mean(y * y, axis=-1, keepdims=True) + EPS)
        if last:
            loss_ref, dx_ref, loss_acc_ref = rest[:3]
            err = a_ref[...] + y * r * g - b_ref[...]

            @pl.when(step == 0)
            def _():
                loss_acc_ref[...] = jnp.zeros_like(loss_acc_ref)

            loss_acc_ref[...] += _rows8(err * err)
            dz = err * (1.0 / D)
            dx_ref[...] = dz
        else:
            dz = a_ref[...]
        a = dz * g
        dy = r * a - y * (r * r * r) * jnp.mean(a * y, axis=-1, keepdims=True)
        acc_ref[...] += _rows8(dz * (y * r))
        dyb = dy.astype(BF16)
        dcat_ref[...] = lax.dot_general(dyb, w_ref[...], NT, preferred_element_type=F32)
        dw_ref[...] += lax.dot_general(cat, dyb, TN, preferred_element_type=F32)

        @pl.when(step == n_steps - 1)
        def _():
            _store_lane_rows(dg_ref, acc_ref[...])
            dwb_ref[...] = dw_ref[...].astype(BF16)
            own_ref[...] = _own_piece(dw_ref, place_ref)
            if last:
                loss_ref[...] = jnp.full((8, 128), (0.5 / D) * jnp.sum(loss_acc_ref[...]), F32)

    row = lambda: pl.BlockSpec((TM, D), lambda i: (i, 0))
    full = _whole
    return pl.pallas_call(
        body, name="out_loss_bwd" if last else "bwd_out", grid=(n_steps,),
        in_specs=[row(), row(), _layer(l, 1, D), row(), full((D, D)), pl.BlockSpec(memory_space=pltpu.SMEM)]
        + [ANY] * len(deps),
        out_specs=[row(), full((D // 8, D)), full((D, D)), full((8, 128))] + ([full((8, 128)), row()] if last else []),
        out_shape=[jax.ShapeDtypeStruct((S, D), F32), jax.ShapeDtypeStruct((D // 8, D), F32),
                   jax.ShapeDtypeStruct((D, D), BF16), jax.ShapeDtypeStruct((8, 128), F32)]
        + ([jax.ShapeDtypeStruct((8, 128), F32), jax.ShapeDtypeStruct((S, D), F32)] if last else []),
        scratch_shapes=([pltpu.VMEM((8, D), F32)] if last else []) + [pltpu.VMEM((8, D), F32), pltpu.VMEM((D, D), F32)],
        compiler_params=_params(),
    )(*((x, target) if last else (dxn, y)), g_post, cat, w_out, place_arr, *deps)


def _bwd_mix(l, pu, pg, q, kv, ag, dcat, pool_w, pool_scale, sinks, tables, deps=(), dpw_dest=None):
    bias, tri = tables
    deps = tuple(deps) + (() if dpw_dest is None else (dpw_dest,))

    def body(pu_ref, pup_ref, pg_ref, q_ref, kv_ref, kvp_ref, ag_ref, dcat_ref, pw_ref, sc_ref, sink_ref, bias_ref,
             tri_ref, *rest):
        dproj_ref, dpw_ref, dsc_ref, dsink_ref, ext_ref, dext_ref, tmp_a, tmp_b, dkv_ref = rest[len(deps):]
        tmp_refs = (tmp_a, tmp_b)
        step = pl.program_id(0)
        i = NB - 1 - step

        @pl.when(step == 0)
        def _():
            dpw_ref[...] = jnp.zeros_like(dpw_ref)
            dsc_ref[...] = jnp.zeros_like(dsc_ref)
            dsink_ref[...] = jnp.zeros_like(dsink_ref)
            for ref in (ext_ref, tmp_a, tmp_b):
                ref[0:PAD, :] = jnp.zeros((PAD, 512), F32)
            dext_ref[BLK:POOL_ROWS, :] = jnp.zeros((HALO + PAD, 512), F32)
            dkv_ref[...] = jnp.zeros_like(dkv_ref)

        ext_ref[PAD:PAD + HALO, :] = jnp.where(i > 0, pup_ref[...], 0.0)
        ext_ref[PAD + HALO:POOL_ROWS, :] = pu_ref[...]
        _window_sums(ext_ref, tmp_refs, True)
        dpooled = []
        for g, w in enumerate(POOL_WINDOWS):
            lanes = slice(g * 128, (g + 1) * 128)
            pooled, inv = _pool_block(ext_ref, tmp_refs, i, g, w)
            pooled_b = pooled.astype(BF16)
            mixed = jnp.dot(pooled_b, pw_ref[g], preferred_element_type=F32)
            scale = sc_ref[:, lanes]
            gate = pg_ref[:, lanes]
            sg = _sigmoid(gate)
            dpo = dcat_ref[:, lanes]
            dproj_ref[:, C_PG + g * 128:C_PG + (g + 1) * 128] = (
                dpo * (mixed * scale) * (sg * (1.0 + gate * (1.0 - sg)))).astype(BF16)
            dms = dpo * (gate * sg)
            dsc_ref[g:g + 1, :] += jnp.sum(dms * mixed, axis=0, keepdims=True)
            dmixed = (dms * scale).astype(BF16)
            dpw_ref[g] += lax.dot_general(pooled_b, dmixed, TN, preferred_element_type=F32)
            dpooled.append(lax.dot_general(dmixed, pw_ref[g], NT, preferred_element_type=F32))
            dext_ref[0:BLK, lanes] = dpooled[g] * inv
        _window_sums(dext_ref, tmp_refs, False)
        for g in range(len(POOL_WINDOWS)):
            lanes = slice(g * 128, (g + 1) * 128)
            dproj_ref[:, C_PU + g * 128:C_PU + (g + 1) * 128] = (tmp_refs[g % 2][0:BLK, lanes] - dpooled[g]).astype(BF16)
        dext_ref[BLK:BLK + HALO, :] = dext_ref[0:HALO, :]

        own = _own_block_mask()
        tri = tri_ref[...]
        k_var = _head_variants(kv_ref[:, 0:128], kvp_ref[:, 0:128])
        v_var = _head_variants(kv_ref[:, 128:256], kvp_ref[:, 128:256])
        q2 = [_stack_tiles(q_ref, hkv) for hkv in range(2)]
        s = _scores(q2, k_var, own)
        p, p_sink = {}, {}
        for key, s_head in s.items():
            head = _head_of(*key)
            p[key], p_sink[key] = _softmax(s_head, bias_ref[head], sink_ref[l, head])

        do2, p_b, dp = [], {}, {}
        for hkv in range(2):
            gate = _stack_tiles(ag_ref, hkv)
            sg = _sigmoid(gate)
            dca = _stack_tiles(dcat_ref, hkv, D_POOL)
            do2.append((dca * (gate * sg)).astype(BF16))
            o2 = jnp.zeros((2 * BLK, 128), F32)
            for half in range(2):
                p_b[hkv, half] = _spread_pair(p, hkv, half, tri)
                o2 = o2 + jnp.dot(p_b[hkv, half], v_var[hkv][half], preferred_element_type=F32)
                full = lax.dot_general(do2[hkv], v_var[hkv][half], NT, preferred_element_type=F32)
                for t in range(2):
                    dp[hkv, t, half] = _merge(_rows(full, t), own)
            dag = dca * o2 * (sg * (1.0 + gate * (1.0 - sg)))
            for t in range(2):
                lo = C_AG + (2 * hkv + t) * 128
                dproj_ref[:, lo:lo + 128] = _rows(dag, t).astype(BF16)

        ds = {}
        for key in p:
            delta = jnp.sum(p[key] * dp[key], axis=-1, keepdims=True)
            ds[key] = p[key] * (dp[key] - delta)
            head = _head_of(*key)
            dsink_ref[0:1, :] += jnp.where(lax.broadcasted_iota(jnp.int32, (1, 128), 1) == head,
                                           -jnp.sum(p_sink[key] * delta, axis=0, keepdims=True), 0.0)

        dk_acc = [[None, None], [None, None]]
        dv_acc = [[None, None], [None, None]]
        for hkv in range(2):
            dq2 = jnp.zeros((2 * BLK, 128), F32)
            for half in range(2):
                ds_b = _spread_pair(ds, hkv, half, tri)
                dq2 = dq2 + jnp.dot(ds_b, k_var[hkv][half], preferred_element_type=F32)
                dk_acc[hkv][half] = lax.dot_general(ds_b, q2[hkv], TN, preferred_element_type=F32)
                dv_acc[hkv][half] = lax.dot_general(p_b[hkv, half], do2[hkv], TN, preferred_element_type=F32)
            for t in range(2):
                lo = C_Q + (2 * hkv + t) * 128
                dproj_ref[:, lo:lo + 128] = (_rows(dq2, t) * 0.125).astype(BF16)

        low = lax.broadcasted_iota(jnp.int32, (2 * BLK, 128), 1) < 64

        def gather_heads(acc):
            return jnp.where(low, acc[0][0] + pltpu.roll(acc[0][1], 64, axis=1),
                             pltpu.roll(acc[1][0], 64, axis=1) + acc[1][1])

        dk = gather_heads(dk_acc) * 0.125
        dv = gather_heads(dv_acc)
        dproj_ref[:, C_K:C_V] = (dk[BLK:, :] + dkv_ref[:, 0:128]).astype(BF16)
        dproj_ref[:, C_V:C_AG] = (dv[BLK:, :] + dkv_ref[:, 128:256]).astype(BF16)
        dkv_ref[:, 0:128] = dk[:BLK, :]
        dkv_ref[:, 128:256] = dv[:BLK, :]

    rev = lambda w: pl.BlockSpec((BLK, w), lambda s: (NB - 1 - s, 0))
    prev = lambda w: pl.BlockSpec((BLK, w), lambda s: (jnp.maximum(NB - 2 - s, 0), 0))
    halo = pl.BlockSpec((HALO, 512), lambda s: (jnp.maximum((NB - 1 - s) * (BLK // HALO) - 1, 0), 0))
    return pl.pallas_call(
        body, name="bwd_mix", grid=(NB,),
        in_specs=[rev(512), halo, rev(512), rev(512), rev(256), prev(256), rev(512), rev(D),
                  _layer(l, 4, 128, 128), _layer(l, 1, 512), pl.BlockSpec(memory_space=pltpu.SMEM),
                  pl.BlockSpec((None, N_HEADS, BLK, BLK), lambda s: (jnp.minimum(NB - 1 - s, 1), 0, 0, 0)),
                  _whole((BLK, BLK))] + [ANY] * len(deps),
        out_specs=[rev(D_IN), _layer(l, 4, 128, 128),
                   pl.BlockSpec((4, 128), lambda s: (0, 0)), pl.BlockSpec((8, 128), lambda s: (0, 0))],
        out_shape=[jax.ShapeDtypeStruct((S, D_IN), BF16), jax.ShapeDtypeStruct((DEPTH, 4, 128, 128), F32),
                   jax.ShapeDtypeStruct((4, 128), F32), jax.ShapeDtypeStruct((8, 128), F32)],
        input_output_aliases={} if dpw_dest is None else {12 + len(deps): 1},
        scratch_shapes=[pltpu.VMEM((POOL_ROWS, 512), F32)] * 4 + [pltpu.VMEM((BLK, 256), F32)],
        compiler_params=_params(),
    )(pu, pu, pg, q, kv, kv, ag, dcat, pool_w, pool_scale, sinks, bias, tri, *deps)


def _bwd_in_dw(l, dproj, x, g_pre, place_arr, deps=()):
    n_steps = S // TM

    def body(dp_ref, x_ref, g_ref, place_ref, *rest):
        own_ref, dwb_ref, dw_ref = rest[len(deps):]
        step = pl.program_id(0)

        @pl.when(step == 0)
        def _():
            dw_ref[...] = jnp.zeros_like(dw_ref)

        xt = x_ref[...]
        r = lax.rsqrt(jnp.mean(xt * xt, axis=-1, keepdims=True) + EPS)
        h = (xt * r * g_ref[...]).astype(BF16)
        dw_ref[...] += lax.dot_general(dp_ref[...], h, TN, preferred_element_type=F32)

        @pl.when(step == n_steps - 1)
        def _():
            dwb_ref[...] = dw_ref[...].astype(BF16)
            own_ref[...] = _own_piece(dw_ref, place_ref)

    row = lambda w: pl.BlockSpec((TM, w), lambda i: (i, 0))
    full = _whole
    return pl.pallas_call(
        body, name="bwd_in_dw", grid=(n_steps,),
        in_specs=[row(D_IN), row(D), _layer(l, 1, D), pl.BlockSpec(memory_space=pltpu.SMEM)] + [ANY] * len(deps),
        out_specs=[full((D_IN // 8, D)), full((D_IN, D))],
        out_shape=[jax.ShapeDtypeStruct((D_IN // 8, D), F32), jax.ShapeDtypeStruct((D_IN, D), BF16)],
        scratch_shapes=[pltpu.VMEM((D_IN, D), F32)],
        compiler_params=_params(),
    )(dproj, x, g_pre, place_arr, *deps)


def _bwd_in_dx(l, dproj, w_in_t, x, g_pre, dres, deps=(), dw_place=None):
    n_steps = S // TM
    with_dw = dw_place is not None

    def body(dp_ref, w_ref, x_ref, g_ref, dres_ref, *rest):
        place_ref = rest[0] if with_dw else None
        rest = rest[with_dw + len(deps):]
        if with_dw:
            dx_ref, dg_ref, own_ref, dwb_ref, acc_ref, dw_ref = rest
        else:
            dx_ref, dg_ref, acc_ref = rest
        step = pl.program_id(0)

        @pl.when(step == 0)
        def _():
            acc_ref[...] = jnp.zeros_like(acc_ref)
            if with_dw:
                dw_ref[...] = jnp.zeros_like(dw_ref)

        dp = dp_ref[...]
        dh = jnp.dot(dp, w_ref[...], preferred_element_type=F32)
        xt = x_ref[...]
        r = lax.rsqrt(jnp.mean(xt * xt, axis=-1, keepdims=True) + EPS)
        xn = xt * r
        g = g_ref[...]
        acc_ref[...] += _rows8(dh * xn)
        a = dh * g
        dx_ref[...] = dres_ref[...] + (r * a - xt * (r * r * r) * jnp.mean(a * xt, axis=-1, keepdims=True))
        if with_dw:
            dw_ref[...] += lax.dot_general(dp, (xn * g).astype(BF16), TN, preferred_element_type=F32)

        @pl.when(step == n_steps - 1)
        def _():
            _store_lane_rows(dg_ref, acc_ref[...])
            if with_dw:
                dwb_ref[...] = dw_ref[...].astype(BF16)
                own_ref[...] = _own_piece(dw_ref, place_ref)

    row = lambda w: pl.BlockSpec((TM, w), lambda i: (i, 0))
    full = _whole
    dw_specs = [full((D_IN // 8, D)), full((D_IN, D))] if with_dw else []
    dw_shapes = [jax.ShapeDtypeStruct((D_IN // 8, D), F32), jax.ShapeDtypeStruct((D_IN, D), BF16)] if with_dw else []
    return pl.pallas_call(
        body, name="bwd_in" if with_dw else "bwd_in_dx", grid=(n_steps,),
        in_specs=[row(D_IN), full((D_IN, D)), row(D), _layer(l, 1, D), row(D)]
        + [pl.BlockSpec(memory_space=pltpu.SMEM)] * with_dw + [ANY] * len(deps),
        out_specs=[row(D), full((8, 128))] + dw_specs,
        out_shape=[jax.ShapeDtypeStruct((S, D), F32), jax.ShapeDtypeStruct((8, 128), F32)] + dw_shapes,
        scratch_shapes=[pltpu.VMEM((8, D), F32)] + [pltpu.VMEM((D_IN, D), F32)] * with_dw,
        compiler_params=_params(),
    )(dproj, w_in_t, x, g_pre, dres, *((dw_place,) if with_dw else ()), *deps)


HBM =pl.BlockSpec(memory_space=pltpu.HBM)
SEM = pl.BlockSpec(memory_space=pltpu.SEMAPHORE)
SPLIT_COPY = pltpu.CompilerParams(has_side_effects=pltpu.SideEffectType.DATAFLOW_SIDE_EFFECTING)


def _in_hbm(a):
    return pltpu.with_memory_space_constraint(a, pltpu.HBM)

def _place():
    return lax.axis_index("x"), lax.axis_index("y"), lax.axis_index("c")


def _other_chips(x, y):
    return [(1 - x, y), (x, 1 - y), (1 - x, 1 - y)]


def _peer(x, y, c, m):
    return (x ^ (m >> 2), y ^ ((m >> 1) & 1), c ^ (m & 1))


def _place_cast(name, src, chip_arr, tile, layers, deps=()):
    _, n, cols = src.shape
    steps = n // tile
    k = len(layers)

    def body(chip_ref, *refs):
        for s_ref, o_ref in zip(refs[:k], refs[k + len(deps):]):
            o_ref[...] = s_ref[...].astype(BF16)

    def layer_spec(l):
        return pl.BlockSpec((None, tile, cols), lambda i, chip: (l, i, 0))

    return pl.pallas_call(
        body, name=name,
        grid_spec=pltpu.PrefetchScalarGridSpec(
            num_scalar_prefetch=1, grid=(steps,),
            in_specs=[layer_spec(l) for l in layers] + [ANY] * len(deps),
            out_specs=[pl.BlockSpec((tile, cols), lambda i, chip: (chip[0] * steps + i, 0))] * k),
        out_shape=[jax.ShapeDtypeStruct((N_SHARDS * n, cols), BF16)] * k,
        compiler_params=_params(),
    )(chip_arr, *[src] * k, *deps)


def _chip_rows(ref, chip, half=None):
    n = ref.shape[0] // N_SHARDS
    if half is None:
        return ref.at[pl.ds(pl.multiple_of(chip * n, 16), n), :]
    return ref.at[pl.ds(pl.multiple_of(chip * n + half * (n // 2), 16), n // 2), :]


def _gather_start(name, bufs, halved):
    n = len(bufs)

    def body(*refs):
        ins, send, recv, token = refs[:n], refs[n:2 * n], refs[2 * n:3 * n], refs[-1]
        x, y, c = _place()
        for a, buf in enumerate(ins):
            own = _chip_rows(buf, 2 * x + y, c if a in halved else None)
            for j, chip in enumerate(_other_chips(x, y)):
                pltpu.make_async_remote_copy(src_ref=own, dst_ref=own, send_sem=send[a].at[j], recv_sem=recv[a].at[j],
                                             device_id=(*chip, c), device_id_type=MESH).start()
        token[...] = jnp.zeros_like(token)

    outs = pl.pallas_call(
        body, name=name, in_specs=[HBM] * n,
        out_specs=[SEM] * (2 * n) + [HBM] * n + [pl.BlockSpec(memory_space=pltpu.VMEM)],
        out_shape=[pltpu.SemaphoreType.DMA((3,))] * (2 * n) + [pltpu.HBM(b.shape, b.dtype) for b in bufs]
        + [jax.ShapeDtypeStruct((8, 128), F32)],
        input_output_aliases={a: 2 * n + a for a in range(n)},
        compiler_params=SPLIT_COPY,
    )(*[_in_hbm(b) for b in bufs])
    return outs[:n], outs[n:2 * n], outs[2 * n:3 * n], outs[-1]


def _gather_wait(name, buf, send_sem, recv_sem, after, halved=False):
    def body(buf_ref, send_ref, recv_ref, *rest):
        x, y, c = _place()
        half = c if halved else None
        own = _chip_rows(buf_ref, 2 * x + y, half)
        for j, chip in enumerate(_other_chips(x, y)):
            copy = pltpu.make_async_remote_copy(src_ref=own, dst_ref=_chip_rows(buf_ref, 2 * chip[0] + chip[1], half),
                                                send_sem=send_ref.at[j], recv_sem=recv_ref.at[j],
                                                device_id=(*chip, c), device_id_type=MESH)
            copy.wait_send()
            copy.wait_recv()

    return pl.pallas_call(
        body, name=name, in_specs=[HBM, SEM, SEM] + [ANY] * len(after), out_specs=HBM,
        out_shape=pltpu.HBM(buf.shape, buf.dtype), input_output_aliases={0: 0}, compiler_params=SPLIT_COPY,
    )(buf, send_sem, recv_sem, *after)


def _forward_halves(name, buf):
    def body(in_ref, out_ref, send_sems, recv_sems):
        x, y, c = _place()

        def copy(j, chip, half):
            rows = 2 * chip[0] + chip[1]
            return pltpu.make_async_remote_copy(
                src_ref=_chip_rows(in_ref, rows, half), dst_ref=_chip_rows(out_ref, rows, half), send_sem=send_sems.at[j],
                recv_sem=recv_sems.at[j], device_id=(x, y, 1 - c), device_id_type=MESH)

        chips = _other_chips(x, y)
        for j, chip in enumerate(chips):
            copy(j, chip, c).start()
        for j, chip in enumerate(chips):
            copy(j, chip, c).wait_send()
            copy(j, chip, 1 - c).wait_recv()

    return pl.pallas_call(
        body, name=name, in_specs=[ANY], out_specs=ANY, out_shape=jax.ShapeDtypeStruct(buf.shape, buf.dtype),
        input_output_aliases={0: 0},
        scratch_shapes=[pltpu.SemaphoreType.DMA((3,))] * 2,
    )(buf)


def _piece_rows(ref, k):
    p = ref.shape[0] // 8
    return ref.at[pl.ds(pl.multiple_of(k * p, 32 // jnp.dtype(ref.dtype).itemsize), p), :]


def _exchange_start(name, arrays):
    n = len(arrays)
    zones = [lax.empty((7, a.shape[0] // 8, a.shape[1]), a.dtype) for a in arrays]

    def body(*refs):
        srcs, lands = refs[:n], refs[n:2 * n]
        send, recv, token = refs[2 * n:3 * n], refs[3 * n:4 * n], refs[-1]
        x, y, c = _place()
        for a, (src, land) in enumerate(zip(srcs, lands)):
            for m in range(1, 8):
                px, py, pc = _peer(x, y, c, m)
                pltpu.make_async_remote_copy(
                    src_ref=_piece_rows(src, 4 * px + 2 * py + pc), dst_ref=land.at[m - 1], send_sem=send[a].at[m - 1],
                    recv_sem=recv[a].at[m - 1], device_id=(px, py, pc), device_id_type=MESH).start()
        token[...] = jnp.zeros_like(token)

    outs = pl.pallas_call(
        body, name=name, in_specs=[HBM] * (2 * n),
        out_specs=[SEM] * (2 * n) + [HBM] * (2 * n) + [pl.BlockSpec(memory_space=pltpu.VMEM)],
        out_shape=[pltpu.SemaphoreType.DMA((7,))] * (2 * n) + [pltpu.HBM(a.shape, a.dtype) for a in arrays + zones]
        + [jax.ShapeDtypeStruct((8, 128), F32)],
        input_output_aliases={a: 2 * n + a for a in range(2 * n)},
        compiler_params=SPLIT_COPY,
    )(*[_in_hbm(a) for a in arrays + zones])
    return outs[:n], outs[n:2 * n], outs[2 * n:3 * n], outs[3 * n:4 * n], outs[-1]


def _exchange_wait(name, started, after):
    send_sems, recv_sems, arrays, zones, _ = started
    n = len(arrays)

    def body(*refs):
        srcs, lands = refs[:n], refs[n:2 * n]
        send, recv = refs[2 * n:3 * n], refs[3 * n:4 * n]
        x, y, c = _place()
        for a, (src, land) in enumerate(zip(srcs, lands)):
            for m in range(1, 8):
                px, py, pc = _peer(x, y, c, m)
                copy = pltpu.make_async_remote_copy(
                    src_ref=_piece_rows(src, 4 * px + 2 * py + pc), dst_ref=land.at[m - 1], send_sem=send[a].at[m - 1],
                    recv_sem=recv[a].at[m - 1], device_id=(px, py, pc), device_id_type=MESH)
                copy.wait_send()
                copy.wait_recv()

    outs = pl.pallas_call(
        body, name=name, in_specs=[HBM] * (2 * n) + [SEM] * (2 * n) + [ANY], out_specs=[HBM] * (2 * n),
        out_shape=[pltpu.HBM(a.shape, a.dtype) for a in list(arrays) + list(zones)],
        input_output_aliases={a: a for a in range(2 * n)}, compiler_params=SPLIT_COPY,
    )(*arrays, *zones, *send_sems, *recv_sems, after)
    return outs[n:]


def _sum_pieces(name, owns, recvs, place_arr, layer, dests=None):
    n = len(owns)
    steps = 2

    def body(place_ref, *refs):
        for o_ref, r_ref, out_ref in zip(refs[:n], refs[n:2 * n], refs[-n:]):
            total = o_ref[...]
            for m in range(7):
                total = total + r_ref[m].astype(F32)
            out_ref[...] = total

    tiles = [o.shape[0] // steps for o in owns]
    return pl.pallas_call(
        body, name=name,
        grid_spec=pltpu.PrefetchScalarGridSpec(
            num_scalar_prefetch=1, grid=(steps,),
            in_specs=[pl.BlockSpec((t, o.shape[1]), lambda i, place: (i, 0)) for o, t in zip(owns, tiles)]
            + [pl.BlockSpec((7, t, o.shape[1]), lambda i, place: (0, i, 0)) for o, t in zip(owns, tiles)]
            + ([] if dests is None else [ANY] * n),
            out_specs=[pl.BlockSpec((None, t, o.shape[1]), lambda i, place: (layer, place[1] * steps + i, 0))
                       for o, t in zip(owns, tiles)]),
        out_shape=[jax.ShapeDtypeStruct((DEPTH, 2 * o.shape[0], o.shape[1]), F32) for o in owns],
        input_output_aliases={} if dests is None else {1 + 2 * n + k: k for k in range(n)},
        compiler_params=_params(),
    )(place_arr, *owns, *recvs, *(() if dests is None else dests))


def _sum_small(partials, recvs, place_arr):
    n = len(partials)

    def body(place_ref, *refs):
        for o_ref, r_ref, out_ref in zip(refs[:n], refs[n:2 * n], refs[2 * n:]):
            total = o_ref[...]
            for m in range(7):
                total = total + r_ref[m]
            out_ref[...] = total

    piece = lambda a: pl.BlockSpec((a.shape[0] // 8, a.shape[1]), lambda i, place: (place[0], 0))
    return pl.pallas_call(
        body, name="sum_small",
        grid_spec=pltpu.PrefetchScalarGridSpec(
            num_scalar_prefetch=1, grid=(1,),
            in_specs=[piece(a) for a in partials] + [pl.BlockSpec(r.shape, lambda i, place: (0, 0, 0)) for r in recvs],
            out_specs=[piece(a) for a in partials]),
        out_shape=[jax.ShapeDtypeStruct(a.shape, F32) for a in partials],
        compiler_params=_params(),
    )(place_arr, *partials, *recvs)


def _share(name, bufs, parts, gathered=()):
    n, n_g = len(bufs), len(gathered)
    total = n + n_g

    def body(*refs):
        ins, outs = refs[:total], refs[total:2 * total]
        send_sems, recv_sems, send_g, recv_g = refs[2 * total:]
        x, y, c = _place()

        def half(ref, l, which):
            p = ref.shape[1] // 2
            return ref.at[l, pl.ds(pl.multiple_of(which * p, 8), p), :]

        def swap(k, which):
            a, l = parts[k]
            return pltpu.make_async_remote_copy(
                src_ref=half(ins[a], l, which), dst_ref=half(outs[a], l, which), send_sem=send_sems.at[k],
                recv_sem=recv_sems.at[k], device_id=(x, y, 1 - c), device_id_type=MESH)

        def spread(a, m, sender):
            k = 4 * sender[0] + 2 * sender[1] + sender[2]
            return pltpu.make_async_remote_copy(
                src_ref=_piece_rows(ins[n + a], k), dst_ref=_piece_rows(outs[n + a], k), send_sem=send_g.at[7 * a + m - 1],
                recv_sem=recv_g.at[7 * a + m - 1], device_id=_peer(x, y, c, m), device_id_type=MESH)

        for k in range(len(parts)):
            swap(k, c).start()
        for a in range(n_g):
            for m in range(1, 8):
                spread(a, m, (x, y, c)).start()
        for k in range(len(parts)):
            swap(k, c).wait_send()
            swap(k, 1 - c).wait_recv()
        for a in range(n_g):
            for m in range(1, 8):
                spread(a, m, (x, y, c)).wait_send()
                spread(a, m, _peer(x, y, c, m)).wait_recv()

    arrays = list(bufs) + list(gathered)
    return pl.pallas_call(
        body, name=name, in_specs=[ANY] * total, out_specs=[ANY] * total,
        out_shape=[jax.ShapeDtypeStruct(b.shape, F32) for b in arrays],
        input_output_aliases={a: a for a in range(total)},
        scratch_shapes=[pltpu.SemaphoreType.DMA((max(len(parts), 1),))] * 2
        + [pltpu.SemaphoreType.DMA((max(7 * n_g, 1),))] * 2,
    )(*arrays)


def _adamw_math(w, g, m, v):
    nm = ADAM_B1 * m + (1.0 - ADAM_B1) * g
    nv = ADAM_B2 * v + (1.0 - ADAM_B2) * (g * g)
    m_hat = nm / (1.0 - ADAM_B1 ** ADAM_STEP)
    v_hat = nv / (1.0 - ADAM_B2 ** ADAM_STEP)
    return -ADAM_LR * (m_hat / (jnp.sqrt(v_hat) + ADAM_EPS) + ADAM_WD * w), nm, nv


def _adamw(name, w, g, m, v, rows_per_step, first=0, count=None, dests=None, deps=()):
    layers, rows, cols = w.shape
    count = layers if count is None else count

    def body(w_ref, g_ref, m_ref, v_ref, *rest):
        d_ref, nm_ref, nv_ref, g_out_ref = rest[-4:]
        d_ref[...], nm_ref[...], nv_ref[...] = _adamw_math(w_ref[...], g_ref[...], m_ref[...], v_ref[...])
        g_out_ref[...] = g_ref[...]

    spec = pl.BlockSpec((1, rows_per_step, cols), lambda l, i: (first + l, i, 0))
    shape = jax.ShapeDtypeStruct(w.shape, F32)
    dests = () if dests is None else tuple(dests)
    return pl.pallas_call(
        body, name=name, grid=(count, rows // rows_per_step),
        in_specs=[spec] * 4 + [ANY] * (len(dests) + len(deps)), out_specs=[spec] * 4, out_shape=[shape] * 4,
        input_output_aliases={4 + k: k for k in range(len(dests))},
        compiler_params=_params(("arbitrary", "arbitrary")),
    )(w, g, m, v, *dests, *deps)


def _pack_misc(pool_scale, sinks, norm_pre, norm_post):
    sink_rows = jnp.zeros((DEPTH, 8, 128), F32).at[:, 0, 0:N_HEADS].set(sinks).reshape(2 * 8, 128)
    return jnp.concatenate([pool_scale.reshape(8, 128), norm_pre.reshape(16, 128), norm_post.reshape(16, 128),
                            sink_rows, jnp.zeros((8, 128), F32)], axis=0)


def _adamw_misc(w, g, m, v):
    def body(w_ref, g_ref, m_ref, v_ref, *rest):
        outs, (d_ref, nm_ref, nv_ref) = rest[:17], rest[17:]
        d_ref[...], nm_ref[...], nv_ref[...] = _adamw_math(w_ref[...], g_ref[...], m_ref[...], v_ref[...])
        for k, src in enumerate([g_ref, d_ref, nm_ref, nv_ref]):
            scale, sinks, pre, post = outs[4 * k:4 * k + 4]
            for l in range(DEPTH):
                for j in range(4):
                    scale[l:l + 1, j * 128:(j + 1) * 128] = src[MISC_SCALE + 4 * l + j:MISC_SCALE + 4 * l + j + 1, :]
                for j in range(8):
                    pre[l:l + 1, j * 128:(j + 1) * 128] = src[MISC_PRE + 8 * l + j:MISC_PRE + 8 * l + j + 1, :]
                    post[l:l + 1, j * 128:(j + 1) * 128] = src[MISC_POST + 8 * l + j:MISC_POST + 8 * l + j + 1, :]
                sinks[l:l + 1, :] = src[MISC_SINKS + 8 * l:MISC_SINKS + 8 * l + 1, 0:N_HEADS]
        outs[16][...] = g_ref[MISC_LOSS:MISC_LOSS + 1, 0:1]

    vmem = pl.BlockSpec(memory_space=pltpu.VMEM)
    shapes = [(DEPTH, D_POOL), (DEPTH, N_HEADS), (DEPTH, D), (DEPTH, D)] * 4 + [(1, 1)]
    return pl.pallas_call(
        body, name="adamw_misc", in_specs=[vmem] * 4, out_specs=[vmem] * 17,
        out_shape=[jax.ShapeDtypeStruct(s, F32) for s in shapes],
        scratch_shapes=[pltpu.VMEM((MISC_ROWS, 128), F32)] * 3,
    )(w, g, m, v)


def kernel(x, w_in, pool_w, pool_scale, attn_sinks, w_out, norm_pre, norm_post, loss_target, m_w_in, m_pool_w, m_pool_scale, m_attn_sinks, m_w_out, m_norm_pre, m_norm_post, v_w_in, v_pool_w, v_pool_scale, v_attn_sinks, v_w_out, v_norm_pre, v_norm_post):
    cx, cy, cc = _place()
    chip_arr = jnp.reshape(2 * cx + cy, (1,)).astype(jnp.int32)
    place_arr = jnp.stack([4 * cx + 2 * cy + cc, cc]).astype(jnp.int32)
    t = lambda a: jnp.transpose(a, (0, 2, 1))
    w_in_t = t(w_in)
    xs, target = x[0], loss_target[0]
    pool_w_b = pool_w.astype(BF16)
    tables = _attention_tables()
    scale3 = pool_scale.reshape(DEPTH, 1, D_POOL)
    pre3 = norm_pre.reshape(DEPTH, 1, D)
    post3 = norm_post.reshape(DEPTH, 1, D)

    (wi0,) = _place_cast("place_w_in0", w_in_t, chip_arr, 288, [0])
    first = _gather_start("gather_start_first", [wi0], halved=(0,))
    (wi1,) = _place_cast("place_w_in1", w_in_t, chip_arr, 288, [1], deps=(first[3],))
    wo = _place_cast("place_w_out", w_out, chip_arr, 256, [0, 1], deps=(first[3],))
    rest = _gather_start("gather_start_rest", [wo[0], wi1, wo[1]], halved=(1,))
    send, recv, bufs = [first[k] + rest[k] for k in range(3)]
    order = {(0, "in"): 0, (0, "out"): 1, (1, "in"): 2, (1, "out"): 3}

    saved = []
    packed = [_pack_misc(pool_scale, attn_sinks, norm_pre, norm_post),
              _pack_misc(m_pool_scale, m_attn_sinks, m_norm_pre, m_norm_post),
              _pack_misc(v_pool_scale, v_attn_sinks, v_norm_pre, v_norm_post)]
    after = (first[3], rest[3], pool_w_b, *tables, scale3, pre3, post3, *packed)
    for l in range(DEPTH):
        k = order[l, "in"]
        w_in_l = _forward_halves(f"forward_w_in{l}", _gather_wait(f"gather_wait_in{l}", bufs[k], send[k], recv[k], after,
                                                                 halved=True))
        pu, pg, q, kv, ag = _fwd_in(l, xs, pre3, w_in_l)
        cat = _fwd_mix(l, pu, pg, q, kv, ag, pool_w_b, scale3, attn_sinks, tables)
        k = order[l, "out"]
        w_out_l = _gather_wait(f"gather_wait_out{l}", bufs[k], send[k], recv[k], (cat,))
        y, x_next = _fwd_out(l, cat, w_out_l, xs, post3) if l < DEPTH - 1 else (None, None)
        saved.append((xs, pu, pg, q, kv, ag, cat, y, w_in_l, w_out_l))
        if l < DEPTH - 1:
            xs, after = x_next, (x_next,)

    x_in, pu, pg, q, kv, ag, cat, y, w_in_l, w_out_l = saved[1]
    dcat, dw_out1, dw_out1_b, dg_post1, loss, xs = _bwd_out(1, cat, w_out_l, post3, place_arr, x=x_in, target=target)
    ex1_out = _exchange_start("exchange_start_out1", [dw_out1_b])
    dproj, dpw, dsc1, dsink1 = _bwd_mix(1, pu, pg, q, kv, ag, dcat, pool_w_b, scale3, attn_sinks, tables,
                                        deps=(ex1_out[4],))
    dx, dg_pre1, dw_in1, dw_in1_b = _bwd_in_dx(1, dproj, w_in_l, x_in, pre3, xs, dw_place=place_arr)
    ex1_in = _exchange_start("exchange_start_in1", [dw_in1_b])

    x_in, pu, pg, q, kv, ag, cat, y, w_in_l, w_out_l = saved[0]
    dcat, dw_out0, dw_out0_b, dg_post0 = _bwd_out(0, cat, w_out_l, post3, place_arr, dxn=dx, y=y, deps=(ex1_in[4],))
    ex0_out = _exchange_start("exchange_start_out0", [dw_out0_b])
    dproj, dpw, dsc0, dsink0 = _bwd_mix(0, pu, pg, q, kv, ag, dcat, pool_w_b, scale3, attn_sinks, tables,
                                        deps=(ex0_out[4],), dpw_dest=dpw)
    (recv_out1,) = _exchange_wait("exchange_wait_out1", ex1_out, dproj)
    (recv_in1,) = _exchange_wait("exchange_wait_in1", ex1_in, recv_out1)
    g_in, g_out = _sum_pieces("sum_pieces_1", [dw_in1, dw_out1], [recv_in1, recv_out1], place_arr, 1)
    dw_in0, dw_in0_b = _bwd_in_dw(0, dproj, x_in, pre3, place_arr, deps=(g_in, g_out))
    ex0_in = _exchange_start("exchange_start_in0", [dw_in0_b])

    grad_x, dg_pre0 = _bwd_in_dx(0, dproj, w_in_l, x_in, pre3, dx, deps=(ex0_in[4],))
    small = [dpw.reshape(DEPTH * 4 * 128, 128),
             jnp.concatenate([dsc0, dsc1, dg_pre0, dg_pre1, dg_post0, dg_post1, dsink0, dsink1, loss], axis=0)]
    ex_small = _exchange_start("exchange_start_small", small)
    (recv_out0,) = _exchange_wait("exchange_wait_out0", ex0_out, ex_small[4])
    (g_out,) = _sum_pieces("sum_pieces_out0", [dw_out0], [recv_out0], place_arr, 0, dests=[g_out])
    g_in, g_out = _share("share_a", [g_in, g_out], [(0, 1), (1, 0), (1, 1)])
    m_in_t, v_in_t = t(m_w_in), t(v_w_in)
    d_out, nm_out, nv_out, grad_w_out = _adamw("adamw_w_out", w_out, g_out, m_w_out, v_w_out, 256)
    upd_in = _adamw("adamw_w_in1", w_in_t, g_in, m_in_t, v_in_t, 288, first=1, count=1, deps=(d_out,))

    (recv_in0,) = _exchange_wait("exchange_wait_in0", ex0_in, upd_in[0])
    recv_small = _exchange_wait("exchange_wait_small", ex_small, recv_in0)
    (g_in,) = _sum_pieces("sum_pieces_in0", [dw_in0], [recv_in0], place_arr, 0, dests=[g_in])
    g_in, g_pw, g_misc = _share("share_b", [g_in], [(0, 0)], _sum_small(small, recv_small, place_arr))
    d_in, nm_in, nv_in, grad_w_in_t = _adamw("adamw_w_in0", w_in_t, g_in, m_in_t, v_in_t, 288, first=0, count=1,
                                             dests=upd_in)
    flat = lambda a: a.reshape(1, DEPTH * 4 * 128, 128)
    pw = _adamw("adamw_pool_w", flat(pool_w), flat(g_pw), flat(m_pool_w), flat(v_pool_w), 1024)
    d_pw, m_pw, v_pw, g_pw = [a.reshape(pool_w.shape) for a in pw]
    misc = _adamw_misc(packed[0], g_misc, packed[1], packed[2])
    (g_sc, g_sk, g_pre, g_post, d_sc, d_sk, d_pre, d_post,
     m_sc, m_sk, m_pre, m_post, v_sc, v_sk, v_pre, v_post, loss_sum) = misc
    return (loss_sum[0, 0], grad_x[None], t(grad_w_in_t), g_pw, g_sc, g_sk, grad_w_out, g_pre, g_post,
            t(d_in), d_pw, d_sc, d_sk, d_out, d_pre, d_post,
            t(nm_in), m_pw, m_sc, m_sk, nm_out, m_pre, m_post,
            t(nv_in), v_pw, v_sc, v_sk, nv_out, v_pre, v_post)
```

```python
import jax
import jax.numpy as jnp
from jax import lax
from jax.experimental import pallas as pl
from jax.experimental.pallas import tpu as pltpu

F32 = jnp.float32
BF16 = jnp.bfloat16

S = 2048
D = 1024
DEPTH = 2
D_POOL = 512
POOL_WINDOWS = (2, 4, 8, 16)
N_HEADS = 8
D_IN = 2304
N_SHARDS = 4
W_IN_SHARD = D_IN // N_SHARDS
W_OUT_SHARD = D // N_SHARDS
BLK = 128
NB = S // BLK
HALO = 16
PAD = 8
EPS = 1e-6
NEG_INF = -1e30
C_PU, C_PG, C_Q, C_K, C_V, C_AG = 0, 512, 1024, 1536, 1664, 1792

ADAM_LR = 0.001
ADAM_B1 = 0.9
ADAM_B2 = 0.999
ADAM_EPS = 1e-08
ADAM_WD = 0.01
ADAM_STEP = 10

TM = 512
VMEM_LIMIT = 56 * 1024 * 1024

NT = (((1,), (1,)), ((), ()))
TN = (((0,), (0,)), ((), ()))

MESH = pl.DeviceIdType.MESH
ANY = pl.BlockSpec(memory_space=pl.ANY)

MISC_SCALE, MISC_PRE, MISC_POST, MISC_SINKS, MISC_LOSS = 0, 8, 24, 40, 56
MISC_ROWS = 64


def _params(sem=("arbitrary",)):
    return pltpu.CompilerParams(dimension_semantics=sem, vmem_limit_bytes=VMEM_LIMIT)


def _sigmoid(v):
    return 1.0 / (1.0 + jnp.exp(-v))


def _rows8(v):
    r, c = v.shape
    return v.reshape(r // 8, 8, c).sum(axis=0)


def _layer(l, *shape):
    zeros = (0,) * len(shape)
    return pl.BlockSpec((None,) + shape, lambda i: (l,) + zeros)


def _whole(shape):
    zeros = (0,) * len(shape)
    return pl.BlockSpec(shape, lambda i: zeros, pipeline_mode=pl.Buffered(1))


def _fwd_in(l, x, g_pre, w_in_t, below=None):
    fused = below is not None

    def body(x_ref, g_ref, w_ref, *rest):
        if fused:
            cat_ref, wo_ref, gp_ref, y_ref, xn_ref = rest[:5]
            y = jnp.dot(cat_ref[...], wo_ref[...], preferred_element_type=F32)
            y_ref[...] = y
            xt = x_ref[...] + y * lax.rsqrt(jnp.mean(y * y, axis=-1, keepdims=True) + EPS) * gp_ref[...]
            xn_ref[...] = xt
        else:
            xt = x_ref[...]
        pu_ref, pg_ref, q_ref, kv_ref, ag_ref = rest[-5:]
        r = lax.rsqrt(jnp.mean(xt * xt, axis=-1, keepdims=True) + EPS)
        h = (xt * r * g_ref[...]).astype(BF16)

        def proj(lo, hi):
            return lax.dot_general(h, w_ref[lo:hi, :], NT, preferred_element_type=F32)

        pu_ref[...] = proj(C_PU, C_PG)
        pg_ref[...] = proj(C_PG, C_Q)
        q_ref[...] = proj(C_Q, C_K).astype(BF16)
        kv_ref[...] = proj(C_K, C_AG).astype(BF16)
        ag_ref[...] = proj(C_AG, D_IN)

    row = lambda w: pl.BlockSpec((TM, w), lambda i: (i, 0))
    act = jax.ShapeDtypeStruct((S, D), F32)
    return pl.pallas_call(
        body, name="fwd_out_in" if fused else "fwd_in", grid=(S // TM,),
        in_specs=[row(D), _layer(l, 1, D), _whole((D_IN, D))]
        + ([row(D), _whole((D, D)), _layer(l - 1, 1, D)] if fused else []),
        out_specs=[row(D)] * (2 * fused) + [row(512), row(512), row(512), row(256), row(512)],
        out_shape=[act] * (2 * fused)
        + [jax.ShapeDtypeStruct((S, 512), F32), jax.ShapeDtypeStruct((S, 512), F32),
           jax.ShapeDtypeStruct((S, 512), BF16), jax.ShapeDtypeStruct((S, 256), BF16),
           jax.ShapeDtypeStruct((S, 512), F32)],
        compiler_params=_params(),
    )(x, g_pre, w_in_t, *(below if fused else ()))


LOG2E = 1.4426950408889634
SCORE_SCALE = 0.125 * LOG2E


def _attention_tables():
    qi = jnp.arange(BLK)[:, None]
    kj = jnp.arange(BLK)[None, :]
    dist = ((qi - kj) % BLK).astype(F32)
    slopes = jnp.exp2(-jnp.arange(1, N_HEADS + 1, dtype=F32))
    bias = -(slopes * LOG2E)[:, None, None] * dist[None]
    first = jnp.where(kj > qi, NEG_INF, bias)
    return jnp.stack([first, bias]), (kj <= qi).astype(BF16)


def _own_block_mask():
    return lax.broadcasted_iota(jnp.int32, (BLK, BLK), 1) <= lax.broadcasted_iota(jnp.int32, (BLK, BLK), 0)


def _merge(full, own):
    return jnp.where(own, full[:, BLK:], full[:, :BLK])


def _spread(v, tri):
    own = v * tri
    return jnp.concatenate([v - own, own], axis=1)


def _head_variants(cur, prev):
    both = jnp.concatenate([prev, cur], axis=0).astype(F32)
    swapped = pltpu.roll(both, 64, axis=1)
    low = lax.broadcasted_iota(jnp.int32, both.shape, 1) < 64
    zero = jnp.zeros_like(both)
    return ((jnp.where(low, both, zero).astype(BF16), jnp.where(low, zero, swapped).astype(BF16)),
            (jnp.where(low, swapped, zero).astype(BF16), jnp.where(low, zero, both).astype(BF16)))


def _head_of(hkv, t, half):
    return hkv * 4 + 2 * t + half


def _rows(v, t):
    return v[t * BLK:(t + 1) * BLK]


def _stack_tiles(ref, hkv, offset=0):
    lo = offset + 2 * hkv * 128
    return jnp.concatenate([ref[:, lo:lo + 128], ref[:, lo + 128:lo + 256]], axis=0)


def _scores(q2, k_var, own):
    s = {}
    for hkv in range(2):
        for half in range(2):
            full = lax.dot_general(q2[hkv], k_var[hkv][half], NT, preferred_element_type=F32)
            for t in range(2):
                s[hkv, t, half] = _merge(_rows(full, t), own)
    return s


def _softmax(s, bias, sink):
    s = s * SCORE_SCALE + bias
    sink2 = sink * LOG2E
    m = jnp.maximum(jnp.max(s, axis=-1, keepdims=True), sink2)
    p = jnp.exp2(s - m)
    e_sink = jnp.exp2(sink2 - m)
    inv = 1.0 / (jnp.sum(p, axis=-1, keepdims=True) + e_sink)
    return p * inv, e_sink * inv


def _spread_pair(v, hkv, half, tri):
    return jnp.concatenate([_spread(v[hkv, t, half].astype(BF16), tri) for t in range(2)], axis=0)


POOL_ROWS = PAD + HALO + BLK


def _window_sums(src_ref, tmp_refs, trailing):
    lo, hi = (PAD, POOL_ROWS) if trailing else (0, HALO + BLK)
    cur = src_ref
    for level in range(len(POOL_WINDOWS)):
        lanes = slice(level * 128, 512)
        shift = -(1 << level) if trailing else (1 << level)
        dst = tmp_refs[level % 2]
        dst[lo:hi, lanes] = cur[lo:hi, lanes] + cur[lo + shift:hi + shift, lanes]
        cur = dst


def _pool_block(ext_ref, tmp_refs, i, g, w):
    lanes = slice(g * 128, (g + 1) * 128)
    rows = slice(PAD + HALO, POOL_ROWS)
    t = (i * BLK + lax.broadcasted_iota(jnp.int32, (BLK, 1), 0)).astype(F32)
    inv = 1.0 / jnp.minimum(t + 1.0, float(w))
    return tmp_refs[g % 2][rows, lanes] * inv - ext_ref[rows, lanes], inv


def _fwd_mix(l, pu, pg, q, kv, ag, pool_w, pool_scale, sinks, tables):
    bias, tri = tables

    def body(pu_ref, pup_ref, pg_ref, q_ref, kv_ref, kvp_ref, ag_ref, pw_ref, sc_ref, sink_ref, bias_ref, tri_ref,
             cat_ref, ext_ref, *tmp_refs):
        i = pl.program_id(0)

        @pl.when(i == 0)
        def _():
            for ref in (ext_ref, *tmp_refs):
                ref[0:PAD, :] = jnp.zeros((PAD, 512), F32)

        ext_ref[PAD:PAD + HALO, :] = jnp.where(i > 0, pup_ref[...], 0.0)
        ext_ref[PAD + HALO:POOL_ROWS, :] = pu_ref[...]
        _window_sums(ext_ref, tmp_refs, True)
        for g, w in enumerate(POOL_WINDOWS):
            lanes = slice(g * 128, (g + 1) * 128)
            pooled, _ = _pool_block(ext_ref, tmp_refs, i, g, w)
            mixed = jnp.dot(pooled.astype(BF16), pw_ref[g], preferred_element_type=F32)
            gate = pg_ref[:, lanes]
            cat_ref[:, lanes] = (mixed * sc_ref[:, lanes] * (gate * _sigmoid(gate))).astype(BF16)

        own = _own_block_mask()
        tri = tri_ref[...]
        k_var = _head_variants(kv_ref[:, 0:128], kvp_ref[:, 0:128])
        v_var = _head_variants(kv_ref[:, 128:256], kvp_ref[:, 128:256])
        s = _scores([_stack_tiles(q_ref, hkv) for hkv in range(2)], k_var, own)
        p = {}
        for (hkv, t, half), s_head in s.items():
            head = _head_of(hkv, t, half)
            p[hkv, t, half], _ = _softmax(s_head, bias_ref[head], sink_ref[l, head])
        for hkv in range(2):
            o2 = jnp.zeros((2 * BLK, 128), F32)
            for half in range(2):
                o2 = o2 + jnp.dot(_spread_pair(p, hkv, half, tri), v_var[hkv][half], preferred_element_type=F32)
            for t in range(2):
                lo = (2 * hkv + t) * 128
                gate = ag_ref[:, lo:lo + 128]
                cat_ref[:, D_POOL + lo:D_POOL + lo + 128] = (_rows(o2, t) * (gate * _sigmoid(gate))).astype(BF16)

    blk = lambda w: pl.BlockSpec((BLK, w), lambda i: (i, 0))
    prev = lambda w: pl.BlockSpec((BLK, w), lambda i: (jnp.maximum(i - 1, 0), 0))
    halo = pl.BlockSpec((HALO, 512), lambda i: (jnp.maximum(i * (BLK // HALO) - 1, 0), 0))
    return pl.pallas_call(
        body, name="fwd_mix", grid=(NB,),
        in_specs=[blk(512), halo, blk(512), blk(512), blk(256), prev(256), blk(512),
                  _layer(l, 4, 128, 128), _layer(l, 1, 512), pl.BlockSpec(memory_space=pltpu.SMEM),
                  pl.BlockSpec((None, N_HEADS, BLK, BLK), lambda i: (jnp.minimum(i, 1), 0, 0, 0)), _whole((BLK, BLK))],
        out_specs=blk(D),
        out_shape=jax.ShapeDtypeStruct((S, D), BF16),
        scratch_shapes=[pltpu.VMEM((POOL_ROWS, 512), F32)] * 3,
        compiler_params=_params(),
    )(pu, pu, pg, q, kv, kv, ag, pool_w, pool_scale, sinks, bias, tri)


def _store_lane_rows(ref, acc):
    total = jnp.sum(acc, axis=0, keepdims=True)
    for k in range(ref.shape[0]):
        ref[k:k + 1, :] = total[:, k * 128:(k + 1) * 128]


def _own_piece(dw_ref, place_ref):
    p = dw_ref.shape[0] // 8
    return dw_ref[pl.ds(pl.multiple_of(place_ref[0] * p, 8), p), :]


def _bwd_out(l, cat, w_out, g_post, place_arr, dxn=None, y=None, x=None, target=None, deps=()):
    last = target is not None
    n_steps = S // TM

    def body(a_ref, b_ref, g_ref, cat_ref, w_ref, place_ref, *rest):
        dcat_ref, own_ref, dwb_ref, dg_ref = rest[len(deps):len(deps) + 4]
        rest = rest[len(deps) + 4:]
        acc_ref, dw_ref = rest[-2:]
        step = pl.program_id(0)

        @pl.when(step == 0)
        def _():
            dw_ref[...] = jnp.zeros_like(dw_ref)
            acc_ref[...] = jnp.zeros_like(acc_ref)

        cat = cat_ref[...]
        g = g_ref[...]
        y = jnp.dot(cat, w_ref[...], preferred_element_type=F32) if last else b_ref[...]
        r = lax.rsqrt(jnp.mean(y * y, axis=-1, keepdims=True) + EPS)
        if last:
            loss_ref, dx_ref, loss_acc_ref = rest[:3]
            err = a_ref[...] + y * r * g - b_ref[...]

            @pl.when(step == 0)
            def _():
                loss_acc_ref[...] = jnp.zeros_like(loss_acc_ref)

            loss_acc_ref[...] += _rows8(err * err)
            dz = err * (1.0 / D)
            dx_ref[...] = dz
        else:
            dz = a_ref[...]
        a = dz * g
        dy = r * a - y * (r * r * r) * jnp.mean(a * y, axis=-1, keepdims=True)
        acc_ref[...] += _rows8(dz * (y * r))
        dyb = dy.astype(BF16)
        dcat_ref[...] = lax.dot_general(dyb, w_ref[...], NT, preferred_element_type=F32)
        dw_ref[...] += lax.dot_general(cat, dyb, TN, preferred_element_type=F32)

        @pl.when(step == n_steps - 1)
        def _():
            _store_lane_rows(dg_ref, acc_ref[...])
            dwb_ref[...] = dw_ref[...].astype(BF16)
            own_ref[...] = _own_piece(dw_ref, place_ref)
            if last:
                loss_ref[...] = jnp.full((8, 128), (0.5 / D) * jnp.sum(loss_acc_ref[...]), F32)

    row = lambda: pl.BlockSpec((TM, D), lambda i: (i, 0))
    full = _whole
    return pl.pallas_call(
        body, name="out_loss_bwd" if last else "bwd_out", grid=(n_steps,),
        in_specs=[row(), row(), _layer(l, 1, D), row(), full((D, D)), pl.BlockSpec(memory_space=pltpu.SMEM)]
        + [ANY] * len(deps),
        out_specs=[row(), full((D // 8, D)), full((D, D)), full((8, 128))] + ([full((8, 128)), row()] if last else []),
        out_shape=[jax.ShapeDtypeStruct((S, D), F32), jax.ShapeDtypeStruct((D // 8, D), F32),
                   jax.ShapeDtypeStruct((D, D), BF16), jax.ShapeDtypeStruct((8, 128), F32)]
        + ([jax.ShapeDtypeStruct((8, 128), F32), jax.ShapeDtypeStruct((S, D), F32)] if last else []),
        scratch_shapes=([pltpu.VMEM((8, D), F32)] if last else []) + [pltpu.VMEM((8, D), F32), pltpu.VMEM((D, D), F32)],
        compiler_params=_params(),
    )(*((x, target) if last else (dxn, y)), g_post, cat, w_out, place_arr, *deps)


def _bwd_mix(l, pu, pg, q, kv, ag, dcat, pool_w, pool_scale, sinks, tables, deps=(), dpw_dest=None):
    bias, tri = tables
    deps = tuple(deps) + (() if dpw_dest is None else (dpw_dest,))

    def body(pu_ref, pup_ref, pg_ref, q_ref, kv_ref, kvp_ref, ag_ref, dcat_ref, pw_ref, sc_ref, sink_ref, bias_ref,
             tri_ref, *rest):
        dproj_ref, dpw_ref, dsc_ref, dsink_ref, ext_ref, dext_ref, tmp_a, tmp_b, dkv_ref = rest[len(deps):]
        tmp_refs = (tmp_a, tmp_b)
        step = pl.program_id(0)
        i = NB - 1 - step

        @pl.when(step == 0)
        def _():
            dpw_ref[...] = jnp.zeros_like(dpw_ref)
            dsc_ref[...] = jnp.zeros_like(dsc_ref)
            dsink_ref[...] = jnp.zeros_like(dsink_ref)
            for ref in (ext_ref, tmp_a, tmp_b):
                ref[0:PAD, :] = jnp.zeros((PAD, 512), F32)
            dext_ref[BLK:POOL_ROWS, :] = jnp.zeros((HALO + PAD, 512), F32)
            dkv_ref[...] = jnp.zeros_like(dkv_ref)

        ext_ref[PAD:PAD + HALO, :] = jnp.where(i > 0, pup_ref[...], 0.0)
        ext_ref[PAD + HALO:POOL_ROWS, :] = pu_ref[...]
        _window_sums(ext_ref, tmp_refs, True)
        dpooled = []
        for g, w in enumerate(POOL_WINDOWS):
            lanes = slice(g * 128, (g + 1) * 128)
            pooled, inv = _pool_block(ext_ref, tmp_refs, i, g, w)
            pooled_b = pooled.astype(BF16)
            mixed = jnp.dot(pooled_b, pw_ref[g], preferred_element_type=F32)
            scale = sc_ref[:, lanes]
            gate = pg_ref[:, lanes]
            sg = _sigmoid(gate)
            dpo = dcat_ref[:, lanes]
            dproj_ref[:, C_PG + g * 128:C_PG + (g + 1) * 128] = (
                dpo * (mixed * scale) * (sg * (1.0 + gate * (1.0 - sg)))).astype(BF16)
            dms = dpo * (gate * sg)
            dsc_ref[g:g + 1, :] += jnp.sum(dms * mixed, axis=0, keepdims=True)
            dmixed = (dms * scale).astype(BF16)
            dpw_ref[g] += lax.dot_general(pooled_b, dmixed, TN, preferred_element_type=F32)
            dpooled.append(lax.dot_general(dmixed, pw_ref[g], NT, preferred_element_type=F32))
            dext_ref[0:BLK, lanes] = dpooled[g] * inv
        _window_sums(dext_ref, tmp_refs, False)
        for g in range(len(POOL_WINDOWS)):
            lanes = slice(g * 128, (g + 1) * 128)
            dproj_ref[:, C_PU + g * 128:C_PU + (g + 1) * 128] = (tmp_refs[g % 2][0:BLK, lanes] - dpooled[g]).astype(BF16)
        dext_ref[BLK:BLK + HALO, :] = dext_ref[0:HALO, :]

        own = _own_block_mask()
        tri = tri_ref[...]
        k_var = _head_variants(kv_ref[:, 0:128], kvp_ref[:, 0:128])
        v_var = _head_variants(kv_ref[:, 128:256], kvp_ref[:, 128:256])
        q2 = [_stack_tiles(q_ref, hkv) for hkv in range(2)]
        s = _scores(q2, k_var, own)
        p, p_sink = {}, {}
        for key, s_head in s.items():
            head = _head_of(*key)
            p[key], p_sink[key] = _softmax(s_head, bias_ref[head], sink_ref[l, head])

        do2, p_b, dp = [], {}, {}
        for hkv in range(2):
            gate = _stack_tiles(ag_ref, hkv)
            sg = _sigmoid(gate)
            dca = _stack_tiles(dcat_ref, hkv, D_POOL)
            do2.append((dca * (gate * sg)).astype(BF16))
            o2 = jnp.zeros((2 * BLK, 128), F32)
            for half in range(2):
                p_b[hkv, half] = _spread_pair(p, hkv, half, tri)
                o2 = o2 + jnp.dot(p_b[hkv, half], v_var[hkv][half], preferred_element_type=F32)
                full = lax.dot_general(do2[hkv], v_var[hkv][half], NT, preferred_element_type=F32)
                for t in range(2):
                    dp[hkv, t, half] = _merge(_rows(full, t), own)
            dag = dca * o2 * (sg * (1.0 + gate * (1.0 - sg)))
            for t in range(2):
                lo = C_AG + (2 * hkv + t) * 128
                dproj_ref[:, lo:lo + 128] = _rows(dag, t).astype(BF16)

        ds = {}
        for key in p:
            delta = jnp.sum(p[key] * dp[key], axis=-1, keepdims=True)
            ds[key] = p[key] * (dp[key] - delta)
            head = _head_of(*key)
            dsink_ref[0:1, :] += jnp.where(lax.broadcasted_iota(jnp.int32, (1, 128), 1) == head,
                                           -jnp.sum(p_sink[key] * delta, axis=0, keepdims=True), 0.0)

        dk_acc = [[None, None], [None, None]]
        dv_acc = [[None, None], [None, None]]
        for hkv in range(2):
            dq2 = jnp.zeros((2 * BLK, 128), F32)
            for half in range(2):
                ds_b = _spread_pair(ds, hkv, half, tri)
                dq2 = dq2 + jnp.dot(ds_b, k_var[hkv][half], preferred_element_type=F32)
                dk_acc[hkv][half] = lax.dot_general(ds_b, q2[hkv], TN, preferred_element_type=F32)
                dv_acc[hkv][half] = lax.dot_general(p_b[hkv, half], do2[hkv], TN, preferred_element_type=F32)
            for t in range(2):
                lo = C_Q + (2 * hkv + t) * 128
                dproj_ref[:, lo:lo + 128] = (_rows(dq2, t) * 0.125).astype(BF16)

        low = lax.broadcasted_iota(jnp.int32, (2 * BLK, 128), 1) < 64

        def gather_heads(acc):
            return jnp.where(low, acc[0][0] + pltpu.roll(acc[0][1], 64, axis=1),
                             pltpu.roll(acc[1][0], 64, axis=1) + acc[1][1])

        dk = gather_heads(dk_acc) * 0.125
        dv = gather_heads(dv_acc)
        dproj_ref[:, C_K:C_V] = (dk[BLK:, :] + dkv_ref[:, 0:128]).astype(BF16)
        dproj_ref[:, C_V:C_AG] = (dv[BLK:, :] + dkv_ref[:, 128:256]).astype(BF16)
        dkv_ref[:, 0:128] = dk[:BLK, :]
        dkv_ref[:, 128:256] = dv[:BLK, :]

    rev = lambda w: pl.BlockSpec((BLK, w), lambda s: (NB - 1 - s, 0))
    prev = lambda w: pl.BlockSpec((BLK, w), lambda s: (jnp.maximum(NB - 2 - s, 0), 0))
    halo = pl.BlockSpec((HALO, 512), lambda s: (jnp.maximum((NB - 1 - s) * (BLK // HALO) - 1, 0), 0))
    return pl.pallas_call(
        body, name="bwd_mix", grid=(NB,),
        in_specs=[rev(512), halo, rev(512), rev(512), rev(256), prev(256), rev(512), rev(D),
                  _layer(l, 4, 128, 128), _layer(l, 1, 512), pl.BlockSpec(memory_space=pltpu.SMEM),
                  pl.BlockSpec((None, N_HEADS, BLK, BLK), lambda s: (jnp.minimum(NB - 1 - s, 1), 0, 0, 0)),
                  _whole((BLK, BLK))] + [ANY] * len(deps),
        out_specs=[rev(D_IN), _layer(l, 4, 128, 128),
                   pl.BlockSpec((4, 128), lambda s: (0, 0)), pl.BlockSpec((8, 128), lambda s: (0, 0))],
        out_shape=[jax.ShapeDtypeStruct((S, D_IN), BF16), jax.ShapeDtypeStruct((DEPTH, 4, 128, 128), F32),
                   jax.ShapeDtypeStruct((4, 128), F32), jax.ShapeDtypeStruct((8, 128), F32)],
        input_output_aliases={} if dpw_dest is None else {12 + len(deps): 1},
        scratch_shapes=[pltpu.VMEM((POOL_ROWS, 512), F32)] * 4 + [pltpu.VMEM((BLK, 256), F32)],
        compiler_params=_params(),
    )(pu, pu, pg, q, kv, kv, ag, dcat, pool_w, pool_scale, sinks, bias, tri, *deps)


def _bwd_in_dw(l, dproj, x, g_pre, place_arr, deps=()):
    n_steps = S // TM

    def body(dp_ref, x_ref, g_ref, place_ref, *rest):
        own_ref, dwb_ref, dw_ref = rest[len(deps):]
        step = pl.program_id(0)

        @pl.when(step == 0)
        def _():
            dw_ref[...] = jnp.zeros_like(dw_ref)

        xt = x_ref[...]
        r = lax.rsqrt(jnp.mean(xt * xt, axis=-1, keepdims=True) + EPS)
        h = (xt * r * g_ref[...]).astype(BF16)
        dw_ref[...] += lax.dot_general(dp_ref[...], h, TN, preferred_element_type=F32)

        @pl.when(step == n_steps - 1)
        def _():
            dwb_ref[...] = dw_ref[...].astype(BF16)
            own_ref[...] = _own_piece(dw_ref, place_ref)

    row = lambda w: pl.BlockSpec((TM, w), lambda i: (i, 0))
    full = _whole
    return pl.pallas_call(
        body, name="bwd_in_dw", grid=(n_steps,),
        in_specs=[row(D_IN), row(D), _layer(l, 1, D), pl.BlockSpec(memory_space=pltpu.SMEM)] + [ANY] * len(deps),
        out_specs=[full((D_IN // 8, D)), full((D_IN, D))],
        out_shape=[jax.ShapeDtypeStruct((D_IN // 8, D), F32), jax.ShapeDtypeStruct((D_IN, D), BF16)],
        scratch_shapes=[pltpu.VMEM((D_IN, D), F32)],
        compiler_params=_params(),
    )(dproj, x, g_pre, place_arr, *deps)


def _bwd_in_dx(l, dproj, w_in_t, x, g_pre, dres, deps=(), dw_place=None):
    n_steps = S // TM
    with_dw = dw_place is not None

    def body(dp_ref, w_ref, x_ref, g_ref, dres_ref, *rest):
        place_ref = rest[0] if with_dw else None
        rest = rest[with_dw + len(deps):]
        if with_dw:
            dx_ref, dg_ref, own_ref, dwb_ref, acc_ref, dw_ref = rest
        else:
            dx_ref, dg_ref, acc_ref = rest
        step = pl.program_id(0)

        @pl.when(step == 0)
        def _():
            acc_ref[...] = jnp.zeros_like(acc_ref)
            if with_dw:
                dw_ref[...] = jnp.zeros_like(dw_ref)

        dp = dp_ref[...]
        dh = jnp.dot(dp, w_ref[...], preferred_element_type=F32)
        xt = x_ref[...]
        r = lax.rsqrt(jnp.mean(xt * xt, axis=-1, keepdims=True) + EPS)
        xn = xt * r
        g = g_ref[...]
        acc_ref[...] += _rows8(dh * xn)
        a = dh * g
        dx_ref[...] = dres_ref[...] + (r * a - xt * (r * r * r) * jnp.mean(a * xt, axis=-1, keepdims=True))
        if with_dw:
            dw_ref[...] += lax.dot_general(dp, (xn * g).astype(BF16), TN, preferred_element_type=F32)

        @pl.when(step == n_steps - 1)
        def _():
            _store_lane_rows(dg_ref, acc_ref[...])
            if with_dw:
                dwb_ref[...] = dw_ref[...].astype(BF16)
                own_ref[...] = _own_piece(dw_ref, place_ref)

    row = lambda w: pl.BlockSpec((TM, w), lambda i: (i, 0))
    full = _whole
    dw_specs = [full((D_IN // 8, D)), full((D_IN, D))] if with_dw else []
    dw_shapes = [jax.ShapeDtypeStruct((D_IN // 8, D), F32), jax.ShapeDtypeStruct((D_IN, D), BF16)] if with_dw else []
    return pl.pallas_call(
        body, name="bwd_in" if with_dw else "bwd_in_dx", grid=(n_steps,),
        in_specs=[row(D_IN), full((D_IN, D)), row(D), _layer(l, 1, D), row(D)]
        + [pl.BlockSpec(memory_space=pltpu.SMEM)] * with_dw + [ANY] * len(deps),
        out_specs=[row(D), full((8, 128))] + dw_specs,
        out_shape=[jax.ShapeDtypeStruct((S, D), F32), jax.ShapeDtypeStruct((8, 128), F32)] + dw_shapes,
        scratch_shapes=[pltpu.VMEM((8, D), F32)] + [pltpu.VMEM((D_IN, D), F32)] * with_dw,
        compiler_params=_params(),
    )(dproj, w_in_t, x, g_pre, dres, *((dw_place,) if with_dw else ()), *deps)


HBM =pl.BlockSpec(memory_space=pltpu.HBM)
SEM = pl.BlockSpec(memory_space=pltpu.SEMAPHORE)
SPLIT_COPY = pltpu.CompilerParams(has_side_effects=pltpu.SideEffectType.DATAFLOW_SIDE_EFFECTING)


def _in_hbm(a):
    return pltpu.with_memory_space_constraint(a, pltpu.HBM)

def _place():
    return lax.axis_index("x"), lax.axis_index("y"), lax.axis_index("c")


def _other_chips(x, y):
    return [(1 - x, y), (x, 1 - y), (1 - x, 1 - y)]


def _peer(x, y, c, m):
    return (x ^ (m >> 2), y ^ ((m >> 1) & 1), c ^ (m & 1))


def _place_cast(name, src, chip_arr, tile, layers, deps=()):
    _, n, cols = src.shape
    steps = n // tile
    k = len(layers)

    def body(chip_ref, *refs):
        for s_ref, o_ref in zip(refs[:k], refs[k + len(deps):]):
            o_ref[...] = s_ref[...].astype(BF16)

    def layer_spec(l):
        return pl.BlockSpec((None, tile, cols), lambda i, chip: (l, i, 0))

    return pl.pallas_call(
        body, name=name,
        grid_spec=pltpu.PrefetchScalarGridSpec(
            num_scalar_prefetch=1, grid=(steps,),
            in_specs=[layer_spec(l) for l in layers] + [ANY] * len(deps),
            out_specs=[pl.BlockSpec((tile, cols), lambda i, chip: (chip[0] * steps + i, 0))] * k),
        out_shape=[jax.ShapeDtypeStruct((N_SHARDS * n, cols), BF16)] * k,
        compiler_params=_params(),
    )(chip_arr, *[src] * k, *deps)


def _chip_rows(ref, chip, half=None):
    n = ref.shape[0] // N_SHARDS
    if half is None:
        return ref.at[pl.ds(pl.multiple_of(chip * n, 16), n), :]
    return ref.at[pl.ds(pl.multiple_of(chip * n + half * (n // 2), 16), n // 2), :]


def _gather_start(name, bufs, halved):
    n = len(bufs)

    def body(*refs):
        ins, send, recv, token = refs[:n], refs[n:2 * n], refs[2 * n:3 * n], refs[-1]
        x, y, c = _place()
        for a, buf in enumerate(ins):
            own = _chip_rows(buf, 2 * x + y, c if a in halved else None)
            for j, chip in enumerate(_other_chips(x, y)):
                pltpu.make_async_remote_copy(src_ref=own, dst_ref=own, send_sem=send[a].at[j], recv_sem=recv[a].at[j],
                                             device_id=(*chip, c), device_id_type=MESH).start()
        token[...] = jnp.zeros_like(token)

    outs = pl.pallas_call(
        body, name=name, in_specs=[HBM] * n,
        out_specs=[SEM] * (2 * n) + [HBM] * n + [pl.BlockSpec(memory_space=pltpu.VMEM)],
        out_shape=[pltpu.SemaphoreType.DMA((3,))] * (2 * n) + [pltpu.HBM(b.shape, b.dtype) for b in bufs]
        + [jax.ShapeDtypeStruct((8, 128), F32)],
        input_output_aliases={a: 2 * n + a for a in range(n)},
        compiler_params=SPLIT_COPY,
    )(*[_in_hbm(b) for b in bufs])
    return outs[:n], outs[n:2 * n], outs[2 * n:3 * n], outs[-1]


def _gather_wait(name, buf, send_sem, recv_sem, after, halved=False):
    def body(buf_ref, send_ref, recv_ref, *rest):
        x, y, c = _place()
        half = c if halved else None
        own = _chip_rows(buf_ref, 2 * x + y, half)
        for j, chip in enumerate(_other_chips(x, y)):
            copy = pltpu.make_async_remote_copy(src_ref=own, dst_ref=_chip_rows(buf_ref, 2 * chip[0] + chip[1], half),
                                                send_sem=send_ref.at[j], recv_sem=recv_ref.at[j],
                                                device_id=(*chip, c), device_id_type=MESH)
            copy.wait_send()
            copy.wait_recv()

    return pl.pallas_call(
        body, name=name, in_specs=[HBM, SEM, SEM] + [ANY] * len(after), out_specs=HBM,
        out_shape=pltpu.HBM(buf.shape, buf.dtype), input_output_aliases={0: 0}, compiler_params=SPLIT_COPY,
    )(buf, send_sem, recv_sem, *after)


def _forward_halves(name, buf):
    def body(in_ref, out_ref, send_sems, recv_sems):
        x, y, c = _place()

        def copy(j, chip, half):
            rows = 2 * chip[0] + chip[1]
            return pltpu.make_async_remote_copy(
                src_ref=_chip_rows(in_ref, rows, half), dst_ref=_chip_rows(out_ref, rows, half), send_sem=send_sems.at[j],
                recv_sem=recv_sems.at[j], device_id=(x, y, 1 - c), device_id_type=MESH)

        chips = _other_chips(x, y)
        for j, chip in enumerate(chips):
            copy(j, chip, c).start()
        for j, chip in enumerate(chips):
            copy(j, chip, c).wait_send()
            copy(j, chip, 1 - c).wait_recv()

    return pl.pallas_call(
        body, name=name, in_specs=[ANY], out_specs=ANY, out_shape=jax.ShapeDtypeStruct(buf.shape, buf.dtype),
        input_output_aliases={0: 0},
        scratch_shapes=[pltpu.SemaphoreType.DMA((3,))] * 2,
    )(buf)


def _piece_rows(ref, k):
    p = ref.shape[0] // 8
    return ref.at[pl.ds(pl.multiple_of(k * p, 32 // jnp.dtype(ref.dtype).itemsize), p), :]


def _exchange_start(name, arrays):
    n = len(arrays)
    zones = [lax.empty((7, a.shape[0] // 8, a.shape[1]), a.dtype) for a in arrays]

    def body(*refs):
        srcs, lands = refs[:n], refs[n:2 * n]
        send, recv, token = refs[2 * n:3 * n], refs[3 * n:4 * n], refs[-1]
        x, y, c = _place()
        for a, (src, land) in enumerate(zip(srcs, lands)):
            for m in range(1, 8):
                px, py, pc = _peer(x, y, c, m)
                pltpu.make_async_remote_copy(
                    src_ref=_piece_rows(src, 4 * px + 2 * py + pc), dst_ref=land.at[m - 1], send_sem=send[a].at[m - 1],
                    recv_sem=recv[a].at[m - 1], device_id=(px, py, pc), device_id_type=MESH).start()
        token[...] = jnp.zeros_like(token)

    outs = pl.pallas_call(
        body, name=name, in_specs=[HBM] * (2 * n),
        out_specs=[SEM] * (2 * n) + [HBM] * (2 * n) + [pl.BlockSpec(memory_space=pltpu.VMEM)],
        out_shape=[pltpu.SemaphoreType.DMA((7,))] * (2 * n) + [pltpu.HBM(a.shape, a.dtype) for a in arrays + zones]
        + [jax.ShapeDtypeStruct((8, 128), F32)],
        input_output_aliases={a: 2 * n + a for a in range(2 * n)},
        compiler_params=SPLIT_COPY,
    )(*[_in_hbm(a) for a in arrays + zones])
    return outs[:n], outs[n:2 * n], outs[2 * n:3 * n], outs[3 * n:4 * n], outs[-1]


def _exchange_wait(name, started, after):
    send_sems, recv_sems, arrays, zones, _ = started
    n = len(arrays)

    def body(*refs):
        srcs, lands = refs[:n], refs[n:2 * n]
        send, recv = refs[2 * n:3 * n], refs[3 * n:4 * n]
        x, y, c = _place()
        for a, (src, land) in enumerate(zip(srcs, lands)):
            for m in range(1, 8):
                px, py, pc = _peer(x, y, c, m)
                copy = pltpu.make_async_remote_copy(
                    src_ref=_piece_rows(src, 4 * px + 2 * py + pc), dst_ref=land.at[m - 1], send_sem=send[a].at[m - 1],
                    recv_sem=recv[a].at[m - 1], device_id=(px, py, pc), device_id_type=MESH)
                copy.wait_send()
                copy.wait_recv()

    outs = pl.pallas_call(
        body, name=name, in_specs=[HBM] * (2 * n) + [SEM] * (2 * n) + [ANY], out_specs=[HBM] * (2 * n),
        out_shape=[pltpu.HBM(a.shape, a.dtype) for a in list(arrays) + list(zones)],
        input_output_aliases={a: a for a in range(2 * n)}, compiler_params=SPLIT_COPY,
    )(*arrays, *zones, *send_sems, *recv_sems, after)
    return outs[n:]


def _sum_pieces(name, owns, recvs, place_arr, layer, dests=None):
    n = len(owns)
    steps = 2

    def body(place_ref, *refs):
        for o_ref, r_ref, out_ref in zip(refs[:n], refs[n:2 * n], refs[-n:]):
            total = o_ref[...]
            for m in range(7):
                total = total + r_ref[m].astype(F32)
            out_ref[...] = total

    tiles = [o.shape[0] // steps for o in owns]
    return pl.pallas_call(
        body, name=name,
        grid_spec=pltpu.PrefetchScalarGridSpec(
            num_scalar_prefetch=1, grid=(steps,),
            in_specs=[pl.BlockSpec((t, o.shape[1]), lambda i, place: (i, 0)) for o, t in zip(owns, tiles)]
            + [pl.BlockSpec((7, t, o.shape[1]), lambda i, place: (0, i, 0)) for o, t in zip(owns, tiles)]
            + ([] if dests is None else [ANY] * n),
            out_specs=[pl.BlockSpec((None, t, o.shape[1]), lambda i, place: (layer, place[1] * steps + i, 0))
                       for o, t in zip(owns, tiles)]),
        out_shape=[jax.ShapeDtypeStruct((DEPTH, 2 * o.shape[0], o.shape[1]), F32) for o in owns],
        input_output_aliases={} if dests is None else {1 + 2 * n + k: k for k in range(n)},
        compiler_params=_params(),
    )(place_arr, *owns, *recvs, *(() if dests is None else dests))


def _sum_small(partials, recvs, place_arr):
    n = len(partials)

    def body(place_ref, *refs):
        for o_ref, r_ref, out_ref in zip(refs[:n], refs[n:2 * n], refs[2 * n:]):
            total = o_ref[...]
            for m in range(7):
                total = total + r_ref[m]
            out_ref[...] = total

    piece = lambda a: pl.BlockSpec((a.shape[0] // 8, a.shape[1]), lambda i, place: (place[0], 0))
    return pl.pallas_call(
        body, name="sum_small",
        grid_spec=pltpu.PrefetchScalarGridSpec(
            num_scalar_prefetch=1, grid=(1,),
            in_specs=[piece(a) for a in partials] + [pl.BlockSpec(r.shape, lambda i, place: (0, 0, 0)) for r in recvs],
            out_specs=[piece(a) for a in partials]),
        out_shape=[jax.ShapeDtypeStruct(a.shape, F32) for a in partials],
        compiler_params=_params(),
    )(place_arr, *partials, *recvs)


def _share(name, bufs, parts, gathered=()):
    n, n_g = len(bufs), len(gathered)
    total = n + n_g

    def body(*refs):
        ins, outs = refs[:total], refs[total:2 * total]
        send_sems, recv_sems, send_g, recv_g = refs[2 * total:]
        x, y, c = _place()

        def half(ref, l, which):
            p = ref.shape[1] // 2
            return ref.at[l, pl.ds(pl.multiple_of(which * p, 8), p), :]

        def swap(k, which):
            a, l = parts[k]
            return pltpu.make_async_remote_copy(
                src_ref=half(ins[a], l, which), dst_ref=half(outs[a], l, which), send_sem=send_sems.at[k],
                recv_sem=recv_sems.at[k], device_id=(x, y, 1 - c), device_id_type=MESH)

        def spread(a, m, sender):
            k = 4 * sender[0] + 2 * sender[1] + sender[2]
            return pltpu.make_async_remote_copy(
                src_ref=_piece_rows(ins[n + a], k), dst_ref=_piece_rows(outs[n + a], k), send_sem=send_g.at[7 * a + m - 1],
                recv_sem=recv_g.at[7 * a + m - 1], device_id=_peer(x, y, c, m), device_id_type=MESH)

        for k in range(len(parts)):
            swap(k, c).start()
        for a in range(n_g):
            for m in range(1, 8):
                spread(a, m, (x, y, c)).start()
        for k in range(len(parts)):
            swap(k, c).wait_send()
            swap(k, 1 - c).wait_recv()
        for a in range(n_g):
            for m in range(1, 8):
                spread(a, m, (x, y, c)).wait_send()
                spread(a, m, _peer(x, y, c, m)).wait_recv()

    arrays = list(bufs) + list(gathered)
    return pl.pallas_call(
        body, name=name, in_specs=[ANY] * total, out_specs=[ANY] * total,
        out_shape=[jax.ShapeDtypeStruct(b.shape, F32) for b in arrays],
        input_output_aliases={a: a for a in range(total)},
        scratch_shapes=[pltpu.SemaphoreType.DMA((max(len(parts), 1),))] * 2
        + [pltpu.SemaphoreType.DMA((max(7 * n_g, 1),))] * 2,
    )(*arrays)


def _adamw_math(w, g, m, v):
    nm = ADAM_B1 * m + (1.0 - ADAM_B1) * g
    nv = ADAM_B2 * v + (1.0 - ADAM_B2) * (g * g)
    m_hat = nm / (1.0 - ADAM_B1 ** ADAM_STEP)
    v_hat = nv / (1.0 - ADAM_B2 ** ADAM_STEP)
    return -ADAM_LR * (m_hat / (jnp.sqrt(v_hat) + ADAM_EPS) + ADAM_WD * w), nm, nv


def _adamw(name, w, g, m, v, rows_per_step, first=0, count=None, dests=None, deps=()):
    layers, rows, cols = w.shape
    count = layers if count is None else count

    def body(w_ref, g_ref, m_ref, v_ref, *rest):
        d_ref, nm_ref, nv_ref, g_out_ref = rest[-4:]
        d_ref[...], nm_ref[...], nv_ref[...] = _adamw_math(w_ref[...], g_ref[...], m_ref[...], v_ref[...])
        g_out_ref[...] = g_ref[...]

    spec = pl.BlockSpec((1, rows_per_step, cols), lambda l, i: (first + l, i, 0))
    shape = jax.ShapeDtypeStruct(w.shape, F32)
    dests = () if dests is None else tuple(dests)
    return pl.pallas_call(
        body, name=name, grid=(count, rows // rows_per_step),
        in_specs=[spec] * 4 + [ANY] * (len(dests) + len(deps)), out_specs=[spec] * 4, out_shape=[shape] * 4,
        input_output_aliases={4 + k: k for k in range(len(dests))},
        compiler_params=_params(("arbitrary", "arbitrary")),
    )(w, g, m, v, *dests, *deps)


def _pack_misc(pool_scale, sinks, norm_pre, norm_post):
    sink_rows = jnp.zeros((DEPTH, 8, 128), F32).at[:, 0, 0:N_HEADS].set(sinks).reshape(2 * 8, 128)
    return jnp.concatenate([pool_scale.reshape(8, 128), norm_pre.reshape(16, 128), norm_post.reshape(16, 128),
                            sink_rows, jnp.zeros((8, 128), F32)], axis=0)


def _adamw_small(w, g, m, v, pool):
    def body(w_ref, g_ref, m_ref, v_ref, pw_ref, pg_ref, pm_ref, pv_ref, *rest):
        outs, pool_outs, (d_ref, nm_ref, nv_ref) = rest[:17], rest[17:21], rest[21:]
        pool_outs[0][...] = pg_ref[...]
        pool_outs[1][...], pool_outs[2][...], pool_outs[3][...] = _adamw_math(
            pw_ref[...], pg_ref[...], pm_ref[...], pv_ref[...])
        d_ref[...], nm_ref[...], nv_ref[...] = _adamw_math(w_ref[...], g_ref[...], m_ref[...], v_ref[...])
        for k, src in enumerate([g_ref, d_ref, nm_ref, nv_ref]):
            scale, sinks, pre, post = outs[4 * k:4 * k + 4]
            for l in range(DEPTH):
                for j in range(4):
                    scale[l:l + 1, j * 128:(j + 1) * 128] = src[MISC_SCALE + 4 * l + j:MISC_SCALE + 4 * l + j + 1, :]
                for j in range(8):
                    pre[l:l + 1, j * 128:(j + 1) * 128] = src[MISC_PRE + 8 * l + j:MISC_PRE + 8 * l + j + 1, :]
                    post[l:l + 1, j * 128:(j + 1) * 128] = src[MISC_POST + 8 * l + j:MISC_POST + 8 * l + j + 1, :]
                sinks[l:l + 1, :] = src[MISC_SINKS + 8 * l:MISC_SINKS + 8 * l + 1, 0:N_HEADS]
        outs[16][...] = g_ref[MISC_LOSS:MISC_LOSS + 1, 0:1]

    vmem = pl.BlockSpec(memory_space=pltpu.VMEM)
    shapes = [(DEPTH, D_POOL), (DEPTH, N_HEADS), (DEPTH, D), (DEPTH, D)] * 4 + [(1, 1)]
    shapes += [pool[0].shape] * 4
    return pl.pallas_call(
        body, name="adamw_small", in_specs=[vmem] * 8, out_specs=[vmem] * 21,
        out_shape=[jax.ShapeDtypeStruct(s, F32) for s in shapes],
        scratch_shapes=[pltpu.VMEM((MISC_ROWS, 128), F32)] * 3,
    )(w, g, m, v, *pool)


def kernel(x, w_in, pool_w, pool_scale, attn_sinks, w_out, norm_pre, norm_post, loss_target, m_w_in, m_pool_w, m_pool_scale, m_attn_sinks, m_w_out, m_norm_pre, m_norm_post, v_w_in, v_pool_w, v_pool_scale, v_attn_sinks, v_w_out, v_norm_pre, v_norm_post):
    cx, cy, cc = _place()
    chip_arr = jnp.reshape(2 * cx + cy, (1,)).astype(jnp.int32)
    place_arr = jnp.stack([4 * cx + 2 * cy + cc, cc]).astype(jnp.int32)
    t = lambda a: jnp.transpose(a, (0, 2, 1))
    w_in_t = t(w_in)
    xs, target = x[0], loss_target[0]
    pool_w_b = pool_w.astype(BF16)
    tables = _attention_tables()
    scale3 = pool_scale.reshape(DEPTH, 1, D_POOL)
    pre3 = norm_pre.reshape(DEPTH, 1, D)
    post3 = norm_post.reshape(DEPTH, 1, D)

    (wi0,) = _place_cast("place_w_in0", w_in_t, chip_arr, 288, [0])
    first = _gather_start("gather_start_first", [wi0], halved=(0,))
    (wi1,) = _place_cast("place_w_in1", w_in_t, chip_arr, 288, [1], deps=(first[3],))
    wo = _place_cast("place_w_out", w_out, chip_arr, 256, [0, 1], deps=(first[3],))
    rest = _gather_start("gather_start_rest", [wo[0], wi1, wo[1]], halved=(1,))
    send, recv, bufs = [first[k] + rest[k] for k in range(3)]
    order = {(0, "in"): 0, (0, "out"): 1, (1, "in"): 2, (1, "out"): 3}

    saved = []
    packed = [_pack_misc(pool_scale, attn_sinks, norm_pre, norm_post),
              _pack_misc(m_pool_scale, m_attn_sinks, m_norm_pre, m_norm_post),
              _pack_misc(v_pool_scale, v_attn_sinks, v_norm_pre, v_norm_post)]
    after = (first[3], rest[3], pool_w_b, *tables, scale3, pre3, post3, *packed)
    below = None
    for l in range(DEPTH):
        k = order[l, "in"]
        w_in_l = _forward_halves(f"forward_w_in{l}", _gather_wait(f"gather_wait_in{l}", bufs[k], send[k], recv[k], after,
                                                                 halved=True))
        if below is None:
            pu, pg, q, kv, ag = _fwd_in(l, xs, pre3, w_in_l)
        else:
            y, xs, pu, pg, q, kv, ag = _fwd_in(l, xs, pre3, w_in_l, below)
            saved[l - 1][7] = y
        cat = _fwd_mix(l, pu, pg, q, kv, ag, pool_w_b, scale3, attn_sinks, tables)
        k = order[l, "out"]
        w_out_l = _gather_wait(f"gather_wait_out{l}", bufs[k], send[k], recv[k], (cat,))
        saved.append([xs, pu, pg, q, kv, ag, cat, None, w_in_l, w_out_l])
        below, after = (cat, w_out_l, post3), (w_out_l,)

    x_in, pu, pg, q, kv, ag, cat, y, w_in_l, w_out_l = saved[1]
    dcat, dw_out1, dw_out1_b, dg_post1, loss, xs = _bwd_out(1, cat, w_out_l, post3, place_arr, x=x_in, target=target)
    ex1_out = _exchange_start("exchange_start_out1", [dw_out1_b])
    dproj, dpw, dsc1, dsink1 = _bwd_mix(1, pu, pg, q, kv, ag, dcat, pool_w_b, scale3, attn_sinks, tables,
                                        deps=(ex1_out[4],))
    dx, dg_pre1, dw_in1, dw_in1_b = _bwd_in_dx(1, dproj, w_in_l, x_in, pre3, xs, dw_place=place_arr)
    ex1_in = _exchange_start("exchange_start_in1", [dw_in1_b])

    x_in, pu, pg, q, kv, ag, cat, y, w_in_l, w_out_l = saved[0]
    dcat, dw_out0, dw_out0_b, dg_post0 = _bwd_out(0, cat, w_out_l, post3, place_arr, dxn=dx, y=y, deps=(ex1_in[4],))
    ex0_out = _exchange_start("exchange_start_out0", [dw_out0_b])
    dproj, dpw, dsc0, dsink0 = _bwd_mix(0, pu, pg, q, kv, ag, dcat, pool_w_b, scale3, attn_sinks, tables,
                                        deps=(ex0_out[4],), dpw_dest=dpw)
    (recv_out1,) = _exchange_wait("exchange_wait_out1", ex1_out, dproj)
    (recv_in1,) = _exchange_wait("exchange_wait_in1", ex1_in, recv_out1)
    g_in, g_out = _sum_pieces("sum_pieces_1", [dw_in1, dw_out1], [recv_in1, recv_out1], place_arr, 1)
    dw_in0, dw_in0_b = _bwd_in_dw(0, dproj, x_in, pre3, place_arr, deps=(g_in, g_out))
    ex0_in = _exchange_start("exchange_start_in0", [dw_in0_b])

    grad_x, dg_pre0 = _bwd_in_dx(0, dproj, w_in_l, x_in, pre3, dx, deps=(ex0_in[4],))
    small = [dpw.reshape(DEPTH * 4 * 128, 128),
             jnp.concatenate([dsc0, dsc1, dg_pre0, dg_pre1, dg_post0, dg_post1, dsink0, dsink1, loss], axis=0)]
    ex_small = _exchange_start("exchange_start_small", small)
    (recv_out0,) = _exchange_wait("exchange_wait_out0", ex0_out, ex_small[4])
    (g_out,) = _sum_pieces("sum_pieces_out0", [dw_out0], [recv_out0], place_arr, 0, dests=[g_out])
    g_in, g_out = _share("share_a", [g_in, g_out], [(0, 1), (1, 0), (1, 1)])
    m_in_t, v_in_t = t(m_w_in), t(v_w_in)
    d_out, nm_out, nv_out, grad_w_out = _adamw("adamw_w_out", w_out, g_out, m_w_out, v_w_out, 256)
    upd_in = _adamw("adamw_w_in1", w_in_t, g_in, m_in_t, v_in_t, 288, first=1, count=1, deps=(d_out,))

    (recv_in0,) = _exchange_wait("exchange_wait_in0", ex0_in, upd_in[0])
    recv_small = _exchange_wait("exchange_wait_small", ex_small, recv_in0)
    (g_in,) = _sum_pieces("sum_pieces_in0", [dw_in0], [recv_in0], place_arr, 0, dests=[g_in])
    g_in, g_pw, g_misc = _share("share_b", [g_in], [(0, 0)], _sum_small(small, recv_small, place_arr))
    d_in, nm_in, nv_in, grad_w_in_t = _adamw("adamw_w_in0", w_in_t, g_in, m_in_t, v_in_t, 288, first=0, count=1,
                                             dests=upd_in)
    flat = lambda a: a.reshape(DEPTH * 4 * 128, 128)
    small_out = _adamw_small(packed[0], g_misc, packed[1], packed[2],
                             (flat(pool_w), g_pw, flat(m_pool_w), flat(v_pool_w)))
    (g_sc, g_sk, g_pre, g_post, d_sc, d_sk, d_pre, d_post,
     m_sc, m_sk, m_pre, m_post, v_sc, v_sk, v_pre, v_post, loss_sum) = small_out[:17]
    g_pw, d_pw, m_pw, v_pw = [a.reshape(pool_w.shape) for a in small_out[17:]]
    return (loss_sum[0, 0], grad_x[None], t(grad_w_in_t), g_pw, g_sc, g_sk, grad_w_out, g_pre, g_post,
            t(d_in), d_pw, d_sc, d_sk, d_out, d_pre, d_post,
            t(nm_in), m_pw, m_sc, m_sk, nm_out, m_pre, m_post,
            t(nv_in), v_pw, v_sc, v_sk, nv_out, v_pre, v_post)
```

```python
import jax
import jax.numpy as jnp
from jax import lax
from jax.experimental import pallas as pl
from jax.experimental.pallas import tpu as pltpu

F32 = jnp.float32
BF16 = jnp.bfloat16

S = 2048
D = 1024
DEPTH = 2
D_POOL = 512
POOL_WINDOWS = (2, 4, 8, 16)
N_HEADS = 8
D_IN = 2304
N_SHARDS = 4
W_IN_SHARD = D_IN // N_SHARDS
W_OUT_SHARD = D // N_SHARDS
BLK = 128
NB = S // BLK
HALO = 16
PAD = 8
EPS = 1e-6
NEG_INF = -1e30
C_PU, C_PG, C_Q, C_K, C_V, C_AG = 0, 512, 1024, 1536, 1664, 1792

ADAM_LR = 0.001
ADAM_B1 = 0.9
ADAM_B2 = 0.999
ADAM_EPS = 1e-08
ADAM_WD = 0.01
ADAM_STEP = 10

TM = 512
VMEM_LIMIT = 56 * 1024 * 1024

NT = (((1,), (1,)), ((), ()))
TN = (((0,), (0,)), ((), ()))

MESH = pl.DeviceIdType.MESH
ANY = pl.BlockSpec(memory_space=pl.ANY)

MISC_SCALE, MISC_PRE, MISC_POST, MISC_SINKS, MISC_LOSS = 0, 8, 24, 40, 56
MISC_ROWS = 64


def _params(sem=("arbitrary",)):
    return pltpu.CompilerParams(dimension_semantics=sem, vmem_limit_bytes=VMEM_LIMIT)


def _sigmoid(v):
    return 1.0 / (1.0 + jnp.exp(-v))


def _rows8(v):
    r, c = v.shape
    return v.reshape(r // 8, 8, c).sum(axis=0)


def _layer(l, *shape):
    zeros = (0,) * len(shape)
    return pl.BlockSpec((None,) + shape, lambda i: (l,) + zeros)


def _whole(shape):
    zeros = (0,) * len(shape)
    return pl.BlockSpec(shape, lambda i: zeros, pipeline_mode=pl.Buffered(1))


def _fwd_in(l, x, g_pre, w_in_t, below=None):
    fused = below is not None

    def body(x_ref, g_ref, w_ref, *rest):
        if fused:
            cat_ref, wo_ref, gp_ref, y_ref, xn_ref = rest[:5]
            y = jnp.dot(cat_ref[...], wo_ref[...], preferred_element_type=F32)
            y_ref[...] = y
            xt = x_ref[...] + y * lax.rsqrt(jnp.mean(y * y, axis=-1, keepdims=True) + EPS) * gp_ref[...]
            xn_ref[...] = xt
        else:
            xt = x_ref[...]
        pu_ref, pg_ref, q_ref, kv_ref, ag_ref = rest[-5:]
        r = lax.rsqrt(jnp.mean(xt * xt, axis=-1, keepdims=True) + EPS)
        h = (xt * r * g_ref[...]).astype(BF16)

        def proj(lo, hi):
            return lax.dot_general(h, w_ref[lo:hi, :], NT, preferred_element_type=F32)

        pu_ref[...] = proj(C_PU, C_PG)
        pg_ref[...] = proj(C_PG, C_Q)
        q_ref[...] = proj(C_Q, C_K).astype(BF16)
        kv_ref[...] = proj(C_K, C_AG).astype(BF16)
        ag_ref[...] = proj(C_AG, D_IN)

    row = lambda w: pl.BlockSpec((TM, w), lambda i: (i, 0))
    act = jax.ShapeDtypeStruct((S, D), F32)
    return pl.pallas_call(
        body, name="fwd_out_in" if fused else "fwd_in", grid=(S // TM,),
        in_specs=[row(D), _layer(l, 1, D), _whole((D_IN, D))]
        + ([row(D), _whole((D, D)), _layer(l - 1, 1, D)] if fused else []),
        out_specs=[row(D)] * (2 * fused) + [row(512), row(512), row(512), row(256), row(512)],
        out_shape=[act] * (2 * fused)
        + [jax.ShapeDtypeStruct((S, 512), F32), jax.ShapeDtypeStruct((S, 512), F32),
           jax.ShapeDtypeStruct((S, 512), BF16), jax.ShapeDtypeStruct((S, 256), BF16),
           jax.ShapeDtypeStruct((S, 512), F32)],
        compiler_params=_params(),
    )(x, g_pre, w_in_t, *(below if fused else ()))


LOG2E = 1.4426950408889634
SCORE_SCALE = 0.125 * LOG2E


def _attention_tables():
    qi = jnp.arange(BLK)[:, None]
    kj = jnp.arange(BLK)[None, :]
    dist = ((qi - kj) % BLK).astype(F32)
    slopes = jnp.exp2(-jnp.arange(1, N_HEADS + 1, dtype=F32))
    bias = -(slopes * LOG2E)[:, None, None] * dist[None]
    first = jnp.where(kj > qi, NEG_INF, bias)
    return jnp.stack([first, bias]), (kj <= qi).astype(BF16)


def _own_block_mask():
    return lax.broadcasted_iota(jnp.int32, (BLK, BLK), 1) <= lax.broadcasted_iota(jnp.int32, (BLK, BLK), 0)


def _merge(full, own):
    return jnp.where(own, full[:, BLK:], full[:, :BLK])


def _spread(v, tri):
    own = v * tri
    return jnp.concatenate([v - own, own], axis=1)


def _head_variants(cur, prev):
    both = jnp.concatenate([prev, cur], axis=0).astype(F32)
    swapped = pltpu.roll(both, 64, axis=1)
    low = lax.broadcasted_iota(jnp.int32, both.shape, 1) < 64
    zero = jnp.zeros_like(both)
    return ((jnp.where(low, both, zero).astype(BF16), jnp.where(low, zero, swapped).astype(BF16)),
            (jnp.where(low, swapped, zero).astype(BF16), jnp.where(low, zero, both).astype(BF16)))


def _head_of(hkv, t, half):
    return hkv * 4 + 2 * t + half


def _rows(v, t):
    return v[t * BLK:(t + 1) * BLK]


def _stack_tiles(ref, hkv, offset=0):
    lo = offset + 2 * hkv * 128
    return jnp.concatenate([ref[:, lo:lo + 128], ref[:, lo + 128:lo + 256]], axis=0)


def _scores(q2, k_var, own):
    s = {}
    for hkv in range(2):
        for half in range(2):
            full = lax.dot_general(q2[hkv], k_var[hkv][half], NT, preferred_element_type=F32)
            for t in range(2):
                s[hkv, t, half] = _merge(_rows(full, t), own)
    return s


def _softmax(s, bias, sink):
    s = s * SCORE_SCALE + bias
    sink2 = sink * LOG2E
    m = jnp.maximum(jnp.max(s, axis=-1, keepdims=True), sink2)
    p = jnp.exp2(s - m)
    e_sink = jnp.exp2(sink2 - m)
    inv = 1.0 / (jnp.sum(p, axis=-1, keepdims=True) + e_sink)
    return p * inv, e_sink * inv


def _spread_pair(v, hkv, half, tri):
    return jnp.concatenate([_spread(v[hkv, t, half].astype(BF16), tri) for t in range(2)], axis=0)


POOL_ROWS = PAD + HALO + BLK


def _window_sums(src_ref, tmp_refs, trailing):
    lo, hi = (PAD, POOL_ROWS) if trailing else (0, HALO + BLK)
    cur = src_ref
    for level in range(len(POOL_WINDOWS)):
        lanes = slice(level * 128, 512)
        shift = -(1 << level) if trailing else (1 << level)
        dst = tmp_refs[level % 2]
        dst[lo:hi, lanes] = cur[lo:hi, lanes] + cur[lo + shift:hi + shift, lanes]
        cur = dst


def _pool_block(ext_ref, tmp_refs, i, g, w):
    lanes = slice(g * 128, (g + 1) * 128)
    rows = slice(PAD + HALO, POOL_ROWS)
    t = (i * BLK + lax.broadcasted_iota(jnp.int32, (BLK, 1), 0)).astype(F32)
    inv = 1.0 / jnp.minimum(t + 1.0, float(w))
    return tmp_refs[g % 2][rows, lanes] * inv - ext_ref[rows, lanes], inv


def _fwd_mix(l, pu, pg, q, kv, ag, pool_w, pool_scale, sinks, tables):
    bias, tri = tables

    def body(pu_ref, pup_ref, pg_ref, q_ref, kv_ref, kvp_ref, ag_ref, pw_ref, sc_ref, sink_ref, bias_ref, tri_ref,
             cat_ref, ext_ref, *tmp_refs):
        i = pl.program_id(0)

        @pl.when(i == 0)
        def _():
            for ref in (ext_ref, *tmp_refs):
                ref[0:PAD, :] = jnp.zeros((PAD, 512), F32)

        ext_ref[PAD:PAD + HALO, :] = jnp.where(i > 0, pup_ref[...], 0.0)
        ext_ref[PAD + HALO:POOL_ROWS, :] = pu_ref[...]
        _window_sums(ext_ref, tmp_refs, True)
        for g, w in enumerate(POOL_WINDOWS):
            lanes = slice(g * 128, (g + 1) * 128)
            pooled, _ = _pool_block(ext_ref, tmp_refs, i, g, w)
            mixed = jnp.dot(pooled.astype(BF16), pw_ref[g], preferred_element_type=F32)
            gate = pg_ref[:, lanes]
            cat_ref[:, lanes] = (mixed * sc_ref[:, lanes] * (gate * _sigmoid(gate))).astype(BF16)

        own = _own_block_mask()
        tri = tri_ref[...]
        k_var = _head_variants(kv_ref[:, 0:128], kvp_ref[:, 0:128])
        v_var = _head_variants(kv_ref[:, 128:256], kvp_ref[:, 128:256])
        s = _scores([_stack_tiles(q_ref, hkv) for hkv in range(2)], k_var, own)
        p = {}
        for (hkv, t, half), s_head in s.items():
            head = _head_of(hkv, t, half)
            p[hkv, t, half], _ = _softmax(s_head, bias_ref[head], sink_ref[l, head])
        for hkv in range(2):
            o2 = jnp.zeros((2 * BLK, 128), F32)
            for half in range(2):
                o2 = o2 + jnp.dot(_spread_pair(p, hkv, half, tri), v_var[hkv][half], preferred_element_type=F32)
            for t in range(2):
                lo = (2 * hkv + t) * 128
                gate = ag_ref[:, lo:lo + 128]
                cat_ref[:, D_POOL + lo:D_POOL + lo + 128] = (_rows(o2, t) * (gate * _sigmoid(gate))).astype(BF16)

    blk = lambda w: pl.BlockSpec((BLK, w), lambda i: (i, 0))
    prev = lambda w: pl.BlockSpec((BLK, w), lambda i: (jnp.maximum(i - 1, 0), 0))
    halo = pl.BlockSpec((HALO, 512), lambda i: (jnp.maximum(i * (BLK // HALO) - 1, 0), 0))
    return pl.pallas_call(
        body, name="fwd_mix", grid=(NB,),
        in_specs=[blk(512), halo, blk(512), blk(512), blk(256), prev(256), blk(512),
                  _layer(l, 4, 128, 128), _layer(l, 1, 512), pl.BlockSpec(memory_space=pltpu.SMEM),
                  pl.BlockSpec((None, N_HEADS, BLK, BLK), lambda i: (jnp.minimum(i, 1), 0, 0, 0)), _whole((BLK, BLK))],
        out_specs=blk(D),
        out_shape=jax.ShapeDtypeStruct((S, D), BF16),
        scratch_shapes=[pltpu.VMEM((POOL_ROWS, 512), F32)] * 3,
        compiler_params=_params(),
    )(pu, pu, pg, q, kv, kv, ag, pool_w, pool_scale, sinks, bias, tri)


def _store_lane_rows(ref, acc):
    total = jnp.sum(acc, axis=0, keepdims=True)
    for k in range(ref.shape[0]):
        ref[k:k + 1, :] = total[:, k * 128:(k + 1) * 128]


def _own_piece(dw_ref, place_ref):
    p = dw_ref.shape[0] // 8
    return dw_ref[pl.ds(pl.multiple_of(place_ref[0] * p, 8), p), :]


def _bwd_out(l, cat, w_out, g_post, place_arr, dxn=None, y=None, x=None, target=None, deps=()):
    last = target is not None
    n_steps = S // TM

    def body(a_ref, b_ref, g_ref, cat_ref, w_ref, place_ref, *rest):
        dcat_ref, own_ref, dwb_ref, dg_ref = rest[len(deps):len(deps) + 4]
        rest = rest[len(deps) + 4:]
        acc_ref, dw_ref = rest[-2:]
        step = pl.program_id(0)

        @pl.when(step == 0)
        def _():
            dw_ref[...] = jnp.zeros_like(dw_ref)
            acc_ref[...] = jnp.zeros_like(acc_ref)

        cat = cat_ref[...]
        g = g_ref[...]
        y = jnp.dot(cat, w_ref[...], preferred_element_type=F32) if last else b_ref[...]
        r = lax.rsqrt(jnp.mean(y * y, axis=-1, keepdims=True) + EPS)
        if last:
            loss_ref, dx_ref, loss_acc_ref = rest[:3]
            err = a_ref[...] + y * r * g - b_ref[...]

            @pl.when(step == 0)
            def _():
                loss_acc_ref[...] = jnp.zeros_like(loss_acc_ref)

            loss_acc_ref[...] += _rows8(err * err)
            dz = err * (1.0 / D)
            dx_ref[...] = dz
        else:
            dz = a_ref[...]
        a = dz * g
        dy = r * a - y * (r * r * r) * jnp.mean(a * y, axis=-1, keepdims=True)
        acc_ref[...] += _rows8(dz * (y * r))
        dyb = dy.astype(BF16)
        dcat_ref[...] = lax.dot_general(dyb, w_ref[...], NT, preferred_element_type=F32)
        dw_ref[...] += lax.dot_general(cat, dyb, TN, preferred_element_type=F32)

        @pl.when(step == n_steps - 1)
        def _():
            _store_lane_rows(dg_ref, acc_ref[...])
            dwb_ref[...] = dw_ref[...].astype(BF16)
            own_ref[...] = _own_piece(dw_ref, place_ref)
            if last:
                loss_ref[...] = jnp.full((8, 128), (0.5 / D) * jnp.sum(loss_acc_ref[...]), F32)

    row = lambda: pl.BlockSpec((TM, D), lambda i: (i, 0))
    full = _whole
    return pl.pallas_call(
        body, name="out_loss_bwd" if last else "bwd_out", grid=(n_steps,),
        in_specs=[row(), row(), _layer(l, 1, D), row(), full((D, D)), pl.BlockSpec(memory_space=pltpu.SMEM)]
        + [ANY] * len(deps),
        out_specs=[row(), full((D // 8, D)), full((D, D)), full((8, 128))] + ([full((8, 128)), row()] if last else []),
        out_shape=[jax.ShapeDtypeStruct((S, D), F32), jax.ShapeDtypeStruct((D // 8, D), F32),
                   jax.ShapeDtypeStruct((D, D), BF16), jax.ShapeDtypeStruct((8, 128), F32)]
        + ([jax.ShapeDtypeStruct((8, 128), F32), jax.ShapeDtypeStruct((S, D), F32)] if last else []),
        scratch_shapes=([pltpu.VMEM((8, D), F32)] if last else []) + [pltpu.VMEM((8, D), F32), pltpu.VMEM((D, D), F32)],
        compiler_params=_params(),
    )(*((x, target) if last else (dxn, y)), g_post, cat, w_out, place_arr, *deps)


def _bwd_mix(l, pu, pg, q, kv, ag, dcat, pool_w, pool_scale, sinks, tables, deps=(), dpw_dest=None):
    bias, tri = tables
    deps = tuple(deps) + (() if dpw_dest is None else (dpw_dest,))

    def body(pu_ref, pup_ref, pg_ref, q_ref, kv_ref, kvp_ref, ag_ref, dcat_ref, pw_ref, sc_ref, sink_ref, bias_ref,
             tri_ref, *rest):
        dproj_ref, dpw_ref, dsc_ref, dsink_ref, ext_ref, dext_ref, tmp_a, tmp_b, dkv_ref = rest[len(deps):]
        tmp_refs = (tmp_a, tmp_b)
        step = pl.program_id(0)
        i = NB - 1 - step

        @pl.when(step == 0)
        def _():
            dpw_ref[...] = jnp.zeros_like(dpw_ref)
            dsc_ref[...] = jnp.zeros_like(dsc_ref)
            dsink_ref[...] = jnp.zeros_like(dsink_ref)
            for ref in (ext_ref, tmp_a, tmp_b):
                ref[0:PAD, :] = jnp.zeros((PAD, 512), F32)
            dext_ref[BLK:POOL_ROWS, :] = jnp.zeros((HALO + PAD, 512), F32)
            dkv_ref[...] = jnp.zeros_like(dkv_ref)

        ext_ref[PAD:PAD + HALO, :] = jnp.where(i > 0, pup_ref[...], 0.0)
        ext_ref[PAD + HALO:POOL_ROWS, :] = pu_ref[...]
        _window_sums(ext_ref, tmp_refs, True)
        dpooled = []
        for g, w in enumerate(POOL_WINDOWS):
            lanes = slice(g * 128, (g + 1) * 128)
            pooled, inv = _pool_block(ext_ref, tmp_refs, i, g, w)
            pooled_b = pooled.astype(BF16)
            mixed = jnp.dot(pooled_b, pw_ref[g], preferred_element_type=F32)
            scale = sc_ref[:, lanes]
            gate = pg_ref[:, lanes]
            sg = _sigmoid(gate)
            dpo = dcat_ref[:, lanes]
            dproj_ref[:, C_PG + g * 128:C_PG + (g + 1) * 128] = (
                dpo * (mixed * scale) * (sg * (1.0 + gate * (1.0 - sg)))).astype(BF16)
            dms = dpo * (gate * sg)
            dsc_ref[g:g + 1, :] += jnp.sum(dms * mixed, axis=0, keepdims=True)
            dmixed = (dms * scale).astype(BF16)
            dpw_ref[g] += lax.dot_general(pooled_b, dmixed, TN, preferred_element_type=F32)
            dpooled.append(lax.dot_general(dmixed, pw_ref[g], NT, preferred_element_type=F32))
            dext_ref[0:BLK, lanes] = dpooled[g] * inv
        _window_sums(dext_ref, tmp_refs, False)
        for g in range(len(POOL_WINDOWS)):
            lanes = slice(g * 128, (g + 1) * 128)
            dproj_ref[:, C_PU + g * 128:C_PU + (g + 1) * 128] = (tmp_refs[g % 2][0:BLK, lanes] - dpooled[g]).astype(BF16)
        dext_ref[BLK:BLK + HALO, :] = dext_ref[0:HALO, :]

        own = _own_block_mask()
        tri = tri_ref[...]
        k_var = _head_variants(kv_ref[:, 0:128], kvp_ref[:, 0:128])
        v_var = _head_variants(kv_ref[:, 128:256], kvp_ref[:, 128:256])
        q2 = [_stack_tiles(q_ref, hkv) for hkv in range(2)]
        s = _scores(q2, k_var, own)
        p, p_sink = {}, {}
        for key, s_head in s.items():
            head = _head_of(*key)
            p[key], p_sink[key] = _softmax(s_head, bias_ref[head], sink_ref[l, head])

        do2, p_b, dp = [], {}, {}
        for hkv in range(2):
            gate = _stack_tiles(ag_ref, hkv)
            sg = _sigmoid(gate)
            dca = _stack_tiles(dcat_ref, hkv, D_POOL)
            do2.append((dca * (gate * sg)).astype(BF16))
            o2 = jnp.zeros((2 * BLK, 128), F32)
            for half in range(2):
                p_b[hkv, half] = _spread_pair(p, hkv, half, tri)
                o2 = o2 + jnp.dot(p_b[hkv, half], v_var[hkv][half], preferred_element_type=F32)
                full = lax.dot_general(do2[hkv], v_var[hkv][half], NT, preferred_element_type=F32)
                for t in range(2):
                    dp[hkv, t, half] = _merge(_rows(full, t), own)
            dag = dca * o2 * (sg * (1.0 + gate * (1.0 - sg)))
            for t in range(2):
                lo = C_AG + (2 * hkv + t) * 128
                dproj_ref[:, lo:lo + 128] = _rows(dag, t).astype(BF16)

        ds = {}
        for key in p:
            delta = jnp.sum(p[key] * dp[key], axis=-1, keepdims=True)
            ds[key] = p[key] * (dp[key] - delta)
            head = _head_of(*key)
            dsink_ref[0:1, :] += jnp.where(lax.broadcasted_iota(jnp.int32, (1, 128), 1) == head,
                                           -jnp.sum(p_sink[key] * delta, axis=0, keepdims=True), 0.0)

        dk_acc = [[None, None], [None, None]]
        dv_acc = [[None, None], [None, None]]
        for hkv in range(2):
            dq2 = jnp.zeros((2 * BLK, 128), F32)
            for half in range(2):
                ds_b = _spread_pair(ds, hkv, half, tri)
                dq2 = dq2 + jnp.dot(ds_b, k_var[hkv][half], preferred_element_type=F32)
                dk_acc[hkv][half] = lax.dot_general(ds_b, q2[hkv], TN, preferred_element_type=F32)
                dv_acc[hkv][half] = lax.dot_general(p_b[hkv, half], do2[hkv], TN, preferred_element_type=F32)
            for t in range(2):
                lo = C_Q + (2 * hkv + t) * 128
                dproj_ref[:, lo:lo + 128] = (_rows(dq2, t) * 0.125).astype(BF16)

        low = lax.broadcasted_iota(jnp.int32, (2 * BLK, 128), 1) < 64

        def gather_heads(acc):
            return jnp.where(low, acc[0][0] + pltpu.roll(acc[0][1], 64, axis=1),
                             pltpu.roll(acc[1][0], 64, axis=1) + acc[1][1])

        dk = gather_heads(dk_acc) * 0.125
        dv = gather_heads(dv_acc)
        dproj_ref[:, C_K:C_V] = (dk[BLK:, :] + dkv_ref[:, 0:128]).astype(BF16)
        dproj_ref[:, C_V:C_AG] = (dv[BLK:, :] + dkv_ref[:, 128:256]).astype(BF16)
        dkv_ref[:, 0:128] = dk[:BLK, :]
        dkv_ref[:, 128:256] = dv[:BLK, :]

    rev = lambda w: pl.BlockSpec((BLK, w), lambda s: (NB - 1 - s, 0))
    prev = lambda w: pl.BlockSpec((BLK, w), lambda s: (jnp.maximum(NB - 2 - s, 0), 0))
    halo = pl.BlockSpec((HALO, 512), lambda s: (jnp.maximum((NB - 1 - s) * (BLK // HALO) - 1, 0), 0))
    return pl.pallas_call(
        body, name="bwd_mix", grid=(NB,),
        in_specs=[rev(512), halo, rev(512), rev(512), rev(256), prev(256), rev(512), rev(D),
                  _layer(l, 4, 128, 128), _layer(l, 1, 512), pl.BlockSpec(memory_space=pltpu.SMEM),
                  pl.BlockSpec((None, N_HEADS, BLK, BLK), lambda s: (jnp.minimum(NB - 1 - s, 1), 0, 0, 0)),
                  _whole((BLK, BLK))] + [ANY] * len(deps),
        out_specs=[rev(D_IN), _layer(l, 4, 128, 128),
                   pl.BlockSpec((4, 128), lambda s: (0, 0)), pl.BlockSpec((8, 128), lambda s: (0, 0))],
        out_shape=[jax.ShapeDtypeStruct((S, D_IN), BF16), jax.ShapeDtypeStruct((DEPTH, 4, 128, 128), F32),
                   jax.ShapeDtypeStruct((4, 128), F32), jax.ShapeDtypeStruct((8, 128), F32)],
        input_output_aliases={} if dpw_dest is None else {12 + len(deps): 1},
        scratch_shapes=[pltpu.VMEM((POOL_ROWS, 512), F32)] * 4 + [pltpu.VMEM((BLK, 256), F32)],
        compiler_params=_params(),
    )(pu, pu, pg, q, kv, kv, ag, dcat, pool_w, pool_scale, sinks, bias, tri, *deps)


def _bwd_in_dw(l, dproj, x, g_pre, place_arr, deps=()):
    n_steps = S // TM

    def body(dp_ref, x_ref, g_ref, place_ref, *rest):
        own_ref, dwb_ref, dw_ref = rest[len(deps):]
        step = pl.program_id(0)

        @pl.when(step == 0)
        def _():
            dw_ref[...] = jnp.zeros_like(dw_ref)

        xt = x_ref[...]
        r = lax.rsqrt(jnp.mean(xt * xt, axis=-1, keepdims=True) + EPS)
        h = (xt * r * g_ref[...]).astype(BF16)
        dw_ref[...] += lax.dot_general(dp_ref[...], h, TN, preferred_element_type=F32)

        @pl.when(step == n_steps - 1)
        def _():
            dwb_ref[...] = dw_ref[...].astype(BF16)
            own_ref[...] = _own_piece(dw_ref, place_ref)

    row = lambda w: pl.BlockSpec((TM, w), lambda i: (i, 0))
    full = _whole
    return pl.pallas_call(
        body, name="bwd_in_dw", grid=(n_steps,),
        in_specs=[row(D_IN), row(D), _layer(l, 1, D), pl.BlockSpec(memory_space=pltpu.SMEM)] + [ANY] * len(deps),
        out_specs=[full((D_IN // 8, D)), full((D_IN, D))],
        out_shape=[jax.ShapeDtypeStruct((D_IN // 8, D), F32), jax.ShapeDtypeStruct((D_IN, D), BF16)],
        scratch_shapes=[pltpu.VMEM((D_IN, D), F32)],
        compiler_params=_params(),
    )(dproj, x, g_pre, place_arr, *deps)


def _bwd_in_dx(l, dproj, w_in_t, x, g_pre, dres, deps=(), dw_place=None):
    n_steps = S // TM
    with_dw = dw_place is not None

    def body(dp_ref, w_ref, x_ref, g_ref, dres_ref, *rest):
        place_ref = rest[0] if with_dw else None
        rest = rest[with_dw + len(deps):]
        if with_dw:
            dx_ref, dg_ref, own_ref, dwb_ref, acc_ref, dw_ref = rest
        else:
            dx_ref, dg_ref, acc_ref = rest
        step = pl.program_id(0)

        @pl.when(step == 0)
        def _():
            acc_ref[...] = jnp.zeros_like(acc_ref)
            if with_dw:
                dw_ref[...] = jnp.zeros_like(dw_ref)

        dp = dp_ref[...]
        dh = jnp.dot(dp, w_ref[...], preferred_element_type=F32)
        xt = x_ref[...]
        r = lax.rsqrt(jnp.mean(xt * xt, axis=-1, keepdims=True) + EPS)
        xn = xt * r
        g = g_ref[...]
        acc_ref[...] += _rows8(dh * xn)
        a = dh * g
        dx_ref[...] = dres_ref[...] + (r * a - xt * (r * r * r) * jnp.mean(a * xt, axis=-1, keepdims=True))
        if with_dw:
            dw_ref[...] += lax.dot_general(dp, (xn * g).astype(BF16), TN, preferred_element_type=F32)

        @pl.when(step == n_steps - 1)
        def _():
            _store_lane_rows(dg_ref, acc_ref[...])
            if with_dw:
                dwb_ref[...] = dw_ref[...].astype(BF16)
                own_ref[...] = _own_piece(dw_ref, place_ref)

    row = lambda w: pl.BlockSpec((TM, w), lambda i: (i, 0))
    full = _whole
    dw_specs = [full((D_IN // 8, D)), full((D_IN, D))] if with_dw else []
    dw_shapes = [jax.ShapeDtypeStruct((D_IN // 8, D), F32), jax.ShapeDtypeStruct((D_IN, D), BF16)] if with_dw else []
    return pl.pallas_call(
        body, name="bwd_in" if with_dw else "bwd_in_dx", grid=(n_steps,),
        in_specs=[row(D_IN), full((D_IN, D)), row(D), _layer(l, 1, D), row(D)]
        + [pl.BlockSpec(memory_space=pltpu.SMEM)] * with_dw + [ANY] * len(deps),
        out_specs=[row(D), full((8, 128))] + dw_specs,
        out_shape=[jax.ShapeDtypeStruct((S, D), F32), jax.ShapeDtypeStruct((8, 128), F32)] + dw_shapes,
        scratch_shapes=[pltpu.VMEM((8, D), F32)] + [pltpu.VMEM((D_IN, D), F32)] * with_dw,
        compiler_params=_params(),
    )(dproj, w_in_t, x, g_pre, dres, *((dw_place,) if with_dw else ()), *deps)


HBM =pl.BlockSpec(memory_space=pltpu.HBM)
SEM = pl.BlockSpec(memory_space=pltpu.SEMAPHORE)
SPLIT_COPY = pltpu.CompilerParams(has_side_effects=pltpu.SideEffectType.DATAFLOW_SIDE_EFFECTING)


def _in_hbm(a):
    return pltpu.with_memory_space_constraint(a, pltpu.HBM)

def _place():
    return lax.axis_index("x"), lax.axis_index("y"), lax.axis_index("c")


def _other_chips(x, y):
    return [(1 - x, y), (x, 1 - y), (1 - x, 1 - y)]


def _peer(x, y, c, m):
    return (x ^ (m >> 2), y ^ ((m >> 1) & 1), c ^ (m & 1))


def _place_cast(name, src, chip_arr, tile, layers, deps=()):
    _, n, cols = src.shape
    steps = n // tile
    k = len(layers)

    def body(chip_ref, *refs):
        for s_ref, o_ref in zip(refs[:k], refs[k + len(deps):]):
            o_ref[...] = s_ref[...].astype(BF16)

    def layer_spec(l):
        return pl.BlockSpec((None, tile, cols), lambda i, chip: (l, i, 0))

    return pl.pallas_call(
        body, name=name,
        grid_spec=pltpu.PrefetchScalarGridSpec(
            num_scalar_prefetch=1, grid=(steps,),
            in_specs=[layer_spec(l) for l in layers] + [ANY] * len(deps),
            out_specs=[pl.BlockSpec((tile, cols), lambda i, chip: (chip[0] * steps + i, 0))] * k),
        out_shape=[jax.ShapeDtypeStruct((N_SHARDS * n, cols), BF16)] * k,
        compiler_params=_params(),
    )(chip_arr, *[src] * k, *deps)


def _chip_rows(ref, chip, half=None):
    n = ref.shape[0] // N_SHARDS
    if half is None:
        return ref.at[pl.ds(pl.multiple_of(chip * n, 16), n), :]
    return ref.at[pl.ds(pl.multiple_of(chip * n + half * (n // 2), 16), n // 2), :]


def _gather_start(name, bufs, halved):
    n = len(bufs)

    def body(*refs):
        ins, send, recv, token = refs[:n], refs[n:2 * n], refs[2 * n:3 * n], refs[-1]
        x, y, c = _place()
        for a, buf in enumerate(ins):
            own = _chip_rows(buf, 2 * x + y, c if a in halved else None)
            for j, chip in enumerate(_other_chips(x, y)):
                pltpu.make_async_remote_copy(src_ref=own, dst_ref=own, send_sem=send[a].at[j], recv_sem=recv[a].at[j],
                                             device_id=(*chip, c), device_id_type=MESH).start()
        token[...] = jnp.zeros_like(token)

    outs = pl.pallas_call(
        body, name=name, in_specs=[HBM] * n,
        out_specs=[SEM] * (2 * n) + [HBM] * n + [pl.BlockSpec(memory_space=pltpu.VMEM)],
        out_shape=[pltpu.SemaphoreType.DMA((3,))] * (2 * n) + [pltpu.HBM(b.shape, b.dtype) for b in bufs]
        + [jax.ShapeDtypeStruct((8, 128), F32)],
        input_output_aliases={a: 2 * n + a for a in range(n)},
        compiler_params=SPLIT_COPY,
    )(*[_in_hbm(b) for b in bufs])
    return outs[:n], outs[n:2 * n], outs[2 * n:3 * n], outs[-1]


def _gather_wait(name, buf, send_sem, recv_sem, after, halved=False):
    def body(buf_ref, send_ref, recv_ref, *rest):
        x, y, c = _place()
        half = c if halved else None
        own = _chip_rows(buf_ref, 2 * x + y, half)
        for j, chip in enumerate(_other_chips(x, y)):
            copy = pltpu.make_async_remote_copy(src_ref=own, dst_ref=_chip_rows(buf_ref, 2 * chip[0] + chip[1], half),
                                                send_sem=send_ref.at[j], recv_sem=recv_ref.at[j],
                                                device_id=(*chip, c), device_id_type=MESH)
            copy.wait_send()
            copy.wait_recv()

    return pl.pallas_call(
        body, name=name, in_specs=[HBM, SEM, SEM] + [ANY] * len(after), out_specs=HBM,
        out_shape=pltpu.HBM(buf.shape, buf.dtype), input_output_aliases={0: 0}, compiler_params=SPLIT_COPY,
    )(buf, send_sem, recv_sem, *after)


def _forward_halves(name, buf):
    def body(in_ref, out_ref, send_sems, recv_sems):
        x, y, c = _place()

        def copy(j, chip, half):
            rows = 2 * chip[0] + chip[1]
            return pltpu.make_async_remote_copy(
                src_ref=_chip_rows(in_ref, rows, half), dst_ref=_chip_rows(out_ref, rows, half), send_sem=send_sems.at[j],
                recv_sem=recv_sems.at[j], device_id=(x, y, 1 - c), device_id_type=MESH)

        chips = _other_chips(x, y)
        for j, chip in enumerate(chips):
            copy(j, chip, c).start()
        for j, chip in enumerate(chips):
            copy(j, chip, c).wait_send()
            copy(j, chip, 1 - c).wait_recv()

    return pl.pallas_call(
        body, name=name, in_specs=[ANY], out_specs=ANY, out_shape=jax.ShapeDtypeStruct(buf.shape, buf.dtype),
        input_output_aliases={0: 0},
        scratch_shapes=[pltpu.SemaphoreType.DMA((3,))] * 2,
    )(buf)


def _piece_rows(ref, k):
    p = ref.shape[0] // 8
    return ref.at[pl.ds(pl.multiple_of(k * p, 32 // jnp.dtype(ref.dtype).itemsize), p), :]


def _exchange_start(name, arrays):
    n = len(arrays)
    zones = [lax.empty((7, a.shape[0] // 8, a.shape[1]), a.dtype) for a in arrays]

    def body(*refs):
        srcs, lands = refs[:n], refs[n:2 * n]
        send, recv, token = refs[2 * n:3 * n], refs[3 * n:4 * n], refs[-1]
        x, y, c = _place()
        for a, (src, land) in enumerate(zip(srcs, lands)):
            for m in range(1, 8):
                px, py, pc = _peer(x, y, c, m)
                pltpu.make_async_remote_copy(
                    src_ref=_piece_rows(src, 4 * px + 2 * py + pc), dst_ref=land.at[m - 1], send_sem=send[a].at[m - 1],
                    recv_sem=recv[a].at[m - 1], device_id=(px, py, pc), device_id_type=MESH).start()
        token[...] = jnp.zeros_like(token)

    outs = pl.pallas_call(
        body, name=name, in_specs=[HBM] * (2 * n),
        out_specs=[SEM] * (2 * n) + [HBM] * (2 * n) + [pl.BlockSpec(memory_space=pltpu.VMEM)],
        out_shape=[pltpu.SemaphoreType.DMA((7,))] * (2 * n) + [pltpu.HBM(a.shape, a.dtype) for a in arrays + zones]
        + [jax.ShapeDtypeStruct((8, 128), F32)],
        input_output_aliases={a: 2 * n + a for a in range(2 * n)},
        compiler_params=SPLIT_COPY,
    )(*[_in_hbm(a) for a in arrays + zones])
    return outs[:n], outs[n:2 * n], outs[2 * n:3 * n], outs[3 * n:4 * n], outs[-1]


def _exchange_wait(name, started, after):
    send_sems, recv_sems, arrays, zones, _ = started
    n = len(arrays)

    def body(*refs):
        srcs, lands = refs[:n], refs[n:2 * n]
        send, recv = refs[2 * n:3 * n], refs[3 * n:4 * n]
        x, y, c = _place()
        for a, (src, land) in enumerate(zip(srcs, lands)):
            for m in range(1, 8):
                px, py, pc = _peer(x, y, c, m)
                copy = pltpu.make_async_remote_copy(
                    src_ref=_piece_rows(src, 4 * px + 2 * py + pc), dst_ref=land.at[m - 1], send_sem=send[a].at[m - 1],
                    recv_sem=recv[a].at[m - 1], device_id=(px, py, pc), device_id_type=MESH)
                copy.wait_send()
                copy.wait_recv()

    outs = pl.pallas_call(
        body, name=name, in_specs=[HBM] * (2 * n) + [SEM] * (2 * n) + [ANY], out_specs=[HBM] * (2 * n),
        out_shape=[pltpu.HBM(a.shape, a.dtype) for a in list(arrays) + list(zones)],
        input_output_aliases={a: a for a in range(2 * n)}, compiler_params=SPLIT_COPY,
    )(*arrays, *zones, *send_sems, *recv_sems, after)
    return outs[n:]


def _sum_pieces(name, owns, recvs, place_arr, layer, dests=None):
    n = len(owns)
    steps = 2

    def body(place_ref, *refs):
        for o_ref, r_ref, out_ref in zip(refs[:n], refs[n:2 * n], refs[-n:]):
            total = o_ref[...]
            for m in range(7):
                total = total + r_ref[m].astype(F32)
            out_ref[...] = total

    tiles = [o.shape[0] // steps for o in owns]
    return pl.pallas_call(
        body, name=name,
        grid_spec=pltpu.PrefetchScalarGridSpec(
            num_scalar_prefetch=1, grid=(steps,),
            in_specs=[pl.BlockSpec((t, o.shape[1]), lambda i, place: (i, 0)) for o, t in zip(owns, tiles)]
            + [pl.BlockSpec((7, t, o.shape[1]), lambda i, place: (0, i, 0)) for o, t in zip(owns, tiles)]
            + ([] if dests is None else [ANY] * n),
            out_specs=[pl.BlockSpec((None, t, o.shape[1]), lambda i, place: (layer, place[1] * steps + i, 0))
                       for o, t in zip(owns, tiles)]),
        out_shape=[jax.ShapeDtypeStruct((DEPTH, 2 * o.shape[0], o.shape[1]), F32) for o in owns],
        input_output_aliases={} if dests is None else {1 + 2 * n + k: k for k in range(n)},
        compiler_params=_params(),
    )(place_arr, *owns, *recvs, *(() if dests is None else dests))


def _sum_small(partials, recvs, place_arr):
    n = len(partials)

    def body(place_ref, *refs):
        for o_ref, r_ref, out_ref in zip(refs[:n], refs[n:2 * n], refs[2 * n:]):
            total = o_ref[...]
            for m in range(7):
                total = total + r_ref[m]
            out_ref[...] = total

    piece = lambda a: pl.BlockSpec((a.shape[0] // 8, a.shape[1]), lambda i, place: (place[0], 0))
    return pl.pallas_call(
        body, name="sum_small",
        grid_spec=pltpu.PrefetchScalarGridSpec(
            num_scalar_prefetch=1, grid=(1,),
            in_specs=[piece(a) for a in partials] + [pl.BlockSpec(r.shape, lambda i, place: (0, 0, 0)) for r in recvs],
            out_specs=[piece(a) for a in partials]),
        out_shape=[jax.ShapeDtypeStruct(a.shape, F32) for a in partials],
        compiler_params=_params(),
    )(place_arr, *partials, *recvs)


def _share(name, bufs, parts, gathered=()):
    n, n_g = len(bufs), len(gathered)
    total = n + n_g

    def body(*refs):
        ins, outs = refs[:total], refs[total:2 * total]
        send_sems, recv_sems, send_g, recv_g = refs[2 * total:]
        x, y, c = _place()

        def half(ref, l, which):
            p = ref.shape[1] // 2
            return ref.at[l, pl.ds(pl.multiple_of(which * p, 8), p), :]

        def swap(k, which):
            a, l = parts[k]
            return pltpu.make_async_remote_copy(
                src_ref=half(ins[a], l, which), dst_ref=half(outs[a], l, which), send_sem=send_sems.at[k],
                recv_sem=recv_sems.at[k], device_id=(x, y, 1 - c), device_id_type=MESH)

        def spread(a, m, sender):
            k = 4 * sender[0] + 2 * sender[1] + sender[2]
            return pltpu.make_async_remote_copy(
                src_ref=_piece_rows(ins[n + a], k), dst_ref=_piece_rows(outs[n + a], k), send_sem=send_g.at[7 * a + m - 1],
                recv_sem=recv_g.at[7 * a + m - 1], device_id=_peer(x, y, c, m), device_id_type=MESH)

        for k in range(len(parts)):
            swap(k, c).start()
        for a in range(n_g):
            for m in range(1, 8):
                spread(a, m, (x, y, c)).start()
        for k in range(len(parts)):
            swap(k, c).wait_send()
            swap(k, 1 - c).wait_recv()
        for a in range(n_g):
            for m in range(1, 8):
                spread(a, m, (x, y, c)).wait_send()
                spread(a, m, _peer(x, y, c, m)).wait_recv()

    arrays = list(bufs) + list(gathered)
    return pl.pallas_call(
        body, name=name, in_specs=[ANY] * total, out_specs=[ANY] * total,
        out_shape=[jax.ShapeDtypeStruct(b.shape, F32) for b in arrays],
        input_output_aliases={a: a for a in range(total)},
        scratch_shapes=[pltpu.SemaphoreType.DMA((max(len(parts), 1),))] * 2
        + [pltpu.SemaphoreType.DMA((max(7 * n_g, 1),))] * 2,
    )(*arrays)


def _adamw_math(w, g, m, v):
    nm = ADAM_B1 * m + (1.0 - ADAM_B1) * g
    nv = ADAM_B2 * v + (1.0 - ADAM_B2) * (g * g)
    m_hat = nm / (1.0 - ADAM_B1 ** ADAM_STEP)
    v_hat = nv / (1.0 - ADAM_B2 ** ADAM_STEP)
    return -ADAM_LR * (m_hat / (jnp.sqrt(v_hat) + ADAM_EPS) + ADAM_WD * w), nm, nv


def _adamw(name, w, g, m, v, rows_per_step, first=0, count=None, dests=None, deps=()):
    layers, rows, cols = w.shape
    count = layers if count is None else count

    def body(w_ref, g_ref, m_ref, v_ref, *rest):
        d_ref, nm_ref, nv_ref, g_out_ref = rest[-4:]
        d_ref[...], nm_ref[...], nv_ref[...] = _adamw_math(w_ref[...], g_ref[...], m_ref[...], v_ref[...])
        g_out_ref[...] = g_ref[...]

    spec = pl.BlockSpec((1, rows_per_step, cols), lambda l, i: (first + l, i, 0))
    shape = jax.ShapeDtypeStruct(w.shape, F32)
    dests = () if dests is None else tuple(dests)
    return pl.pallas_call(
        body, name=name, grid=(count, rows // rows_per_step),
        in_specs=[spec] * 4 + [ANY] * (len(dests) + len(deps)), out_specs=[spec] * 4, out_shape=[shape] * 4,
        input_output_aliases={4 + k: k for k in range(len(dests))},
        compiler_params=_params(("arbitrary", "arbitrary")),
    )(w, g, m, v, *dests, *deps)


def _pack_misc(pool_scale, sinks, norm_pre, norm_post):
    sink_rows = jnp.zeros((DEPTH, 8, 128), F32).at[:, 0, 0:N_HEADS].set(sinks).reshape(2 * 8, 128)
    return jnp.concatenate([pool_scale.reshape(8, 128), norm_pre.reshape(16, 128), norm_post.reshape(16, 128),
                            sink_rows, jnp.zeros((8, 128), F32)], axis=0)


def _adamw_small(w, g, m, v, pool):
    def body(w_ref, g_ref, m_ref, v_ref, pw_ref, pg_ref, pm_ref, pv_ref, *rest):
        outs, pool_outs, (d_ref, nm_ref, nv_ref) = rest[:17], rest[17:21], rest[21:]
        pool_outs[0][...] = pg_ref[...]
        pool_outs[1][...], pool_outs[2][...], pool_outs[3][...] = _adamw_math(
            pw_ref[...], pg_ref[...], pm_ref[...], pv_ref[...])
        d_ref[...], nm_ref[...], nv_ref[...] = _adamw_math(w_ref[...], g_ref[...], m_ref[...], v_ref[...])
        for k, src in enumerate([g_ref, d_ref, nm_ref, nv_ref]):
            scale, sinks, pre, post = outs[4 * k:4 * k + 4]
            for l in range(DEPTH):
                for j in range(4):
                    scale[l:l + 1, j * 128:(j + 1) * 128] = src[MISC_SCALE + 4 * l + j:MISC_SCALE + 4 * l + j + 1, :]
                for j in range(8):
                    pre[l:l + 1, j * 128:(j + 1) * 128] = src[MISC_PRE + 8 * l + j:MISC_PRE + 8 * l + j + 1, :]
                    post[l:l + 1, j * 128:(j + 1) * 128] = src[MISC_POST + 8 * l + j:MISC_POST + 8 * l + j + 1, :]
                sinks[l:l + 1, :] = src[MISC_SINKS + 8 * l:MISC_SINKS + 8 * l + 1, 0:N_HEADS]
        outs[16][...] = g_ref[MISC_LOSS:MISC_LOSS + 1, 0:1]

    vmem = pl.BlockSpec(memory_space=pltpu.VMEM)
    shapes = [(DEPTH, D_POOL), (DEPTH, N_HEADS), (DEPTH, D), (DEPTH, D)] * 4 + [(1, 1)]
    shapes += [pool[0].shape] * 4
    return pl.pallas_call(
        body, name="adamw_small", in_specs=[vmem] * 8, out_specs=[vmem] * 21,
        out_shape=[jax.ShapeDtypeStruct(s, F32) for s in shapes],
        scratch_shapes=[pltpu.VMEM((MISC_ROWS, 128), F32)] * 3,
    )(w, g, m, v, *pool)


def kernel(x, w_in, pool_w, pool_scale, attn_sinks, w_out, norm_pre, norm_post, loss_target, m_w_in, m_pool_w, m_pool_scale, m_attn_sinks, m_w_out, m_norm_pre, m_norm_post, v_w_in, v_pool_w, v_pool_scale, v_attn_sinks, v_w_out, v_norm_pre, v_norm_post):
    cx, cy, cc = _place()
    chip_arr = jnp.reshape(2 * cx + cy, (1,)).astype(jnp.int32)
    place_arr = jnp.stack([4 * cx + 2 * cy + cc, cc]).astype(jnp.int32)
    t = lambda a: jnp.transpose(a, (0, 2, 1))
    w_in_t = t(w_in)
    xs, target = x[0], loss_target[0]
    pool_w_b = pool_w.astype(BF16)
    tables = _attention_tables()
    scale3 = pool_scale.reshape(DEPTH, 1, D_POOL)
    pre3 = norm_pre.reshape(DEPTH, 1, D)
    post3 = norm_post.reshape(DEPTH, 1, D)

    (wi0,) = _place_cast("place_w_in0", w_in_t, chip_arr, 288, [0])
    first = _gather_start("gather_start_first", [wi0], halved=(0,))
    (wi1,) = _place_cast("place_w_in1", w_in_t, chip_arr, 288, [1], deps=(first[3],))
    wo = _place_cast("place_w_out", w_out, chip_arr, 256, [0, 1], deps=(first[3],))
    rest = _gather_start("gather_start_rest", [wi1, wo[0], wo[1]], halved=(0,))
    send, recv, bufs = [first[k] + rest[k] for k in range(3)]
    order = {(0, "in"): 0, (1, "in"): 1, (0, "out"): 2, (1, "out"): 3}

    saved = []
    packed = [_pack_misc(pool_scale, attn_sinks, norm_pre, norm_post),
              _pack_misc(m_pool_scale, m_attn_sinks, m_norm_pre, m_norm_post),
              _pack_misc(v_pool_scale, v_attn_sinks, v_norm_pre, v_norm_post)]
    after = (first[3], rest[3], pool_w_b, *tables, scale3, pre3, post3, *packed)
    below = None
    for l in range(DEPTH):
        k = order[l, "in"]
        w_in_l = _forward_halves(f"forward_w_in{l}", _gather_wait(f"gather_wait_in{l}", bufs[k], send[k], recv[k], after,
                                                                 halved=True))
        if below is None:
            pu, pg, q, kv, ag = _fwd_in(l, xs, pre3, w_in_l)
        else:
            y, xs, pu, pg, q, kv, ag = _fwd_in(l, xs, pre3, w_in_l, below)
            saved[l - 1][7] = y
        cat = _fwd_mix(l, pu, pg, q, kv, ag, pool_w_b, scale3, attn_sinks, tables)
        k = order[l, "out"]
        w_out_l = _gather_wait(f"gather_wait_out{l}", bufs[k], send[k], recv[k], (cat,))
        saved.append([xs, pu, pg, q, kv, ag, cat, None, w_in_l, w_out_l])
        below, after = (cat, w_out_l, post3), (w_out_l,)

    x_in, pu, pg, q, kv, ag, cat, y, w_in_l, w_out_l = saved[1]
    dcat, dw_out1, dw_out1_b, dg_post1, loss, xs = _bwd_out(1, cat, w_out_l, post3, place_arr, x=x_in, target=target)
    ex1_out = _exchange_start("exchange_start_out1", [dw_out1_b])
    dproj, dpw, dsc1, dsink1 = _bwd_mix(1, pu, pg, q, kv, ag, dcat, pool_w_b, scale3, attn_sinks, tables,
                                        deps=(ex1_out[4],))
    dx, dg_pre1, dw_in1, dw_in1_b = _bwd_in_dx(1, dproj, w_in_l, x_in, pre3, xs, dw_place=place_arr)
    ex1_in = _exchange_start("exchange_start_in1", [dw_in1_b])

    x_in, pu, pg, q, kv, ag, cat, y, w_in_l, w_out_l = saved[0]
    dcat, dw_out0, dw_out0_b, dg_post0 = _bwd_out(0, cat, w_out_l, post3, place_arr, dxn=dx, y=y, deps=(ex1_in[4],))
    ex0_out = _exchange_start("exchange_start_out0", [dw_out0_b])
    dproj, dpw, dsc0, dsink0 = _bwd_mix(0, pu, pg, q, kv, ag, dcat, pool_w_b, scale3, attn_sinks, tables,
                                        deps=(ex0_out[4],), dpw_dest=dpw)
    (recv_out1,) = _exchange_wait("exchange_wait_out1", ex1_out, dproj)
    (recv_in1,) = _exchange_wait("exchange_wait_in1", ex1_in, recv_out1)
    g_in, g_out = _sum_pieces("sum_pieces_1", [dw_in1, dw_out1], [recv_in1, recv_out1], place_arr, 1)
    dw_in0, dw_in0_b = _bwd_in_dw(0, dproj, x_in, pre3, place_arr, deps=(g_in, g_out))
    ex0_in = _exchange_start("exchange_start_in0", [dw_in0_b])

    grad_x, dg_pre0 = _bwd_in_dx(0, dproj, w_in_l, x_in, pre3, dx, deps=(ex0_in[4],))
    small = [dpw.reshape(DEPTH * 4 * 128, 128),
             jnp.concatenate([dsc0, dsc1, dg_pre0, dg_pre1, dg_post0, dg_post1, dsink0, dsink1, loss], axis=0)]
    ex_small = _exchange_start("exchange_start_small", small)
    (recv_out0,) = _exchange_wait("exchange_wait_out0", ex0_out, ex_small[4])
    (g_out,) = _sum_pieces("sum_pieces_out0", [dw_out0], [recv_out0], place_arr, 0, dests=[g_out])
    g_in, g_out = _share("share_a", [g_in, g_out], [(0, 1), (1, 0), (1, 1)])
    m_in_t, v_in_t = t(m_w_in), t(v_w_in)
    d_out, nm_out, nv_out, grad_w_out = _adamw("adamw_w_out", w_out, g_out, m_w_out, v_w_out, 256)
    upd_in = _adamw("adamw_w_in1", w_in_t, g_in, m_in_t, v_in_t, 288, first=1, count=1, deps=(d_out,))

    (recv_in0,) = _exchange_wait("exchange_wait_in0", ex0_in, upd_in[0])
    recv_small = _exchange_wait("exchange_wait_small", ex_small, recv_in0)
    (g_in,) = _sum_pieces("sum_pieces_in0", [dw_in0], [recv_in0], place_arr, 0, dests=[g_in])
    g_in, g_pw, g_misc = _share("share_b", [g_in], [(0, 0)], _sum_small(small, recv_small, place_arr))
    d_in, nm_in, nv_in, grad_w_in_t = _adamw("adamw_w_in0", w_in_t, g_in, m_in_t, v_in_t, 288, first=0, count=1,
                                             dests=upd_in)
    flat = lambda a: a.reshape(DEPTH * 4 * 128, 128)
    small_out = _adamw_small(packed[0], g_misc, packed[1], packed[2],
                             (flat(pool_w), g_pw, flat(m_pool_w), flat(v_pool_w)))
    (g_sc, g_sk, g_pre, g_post, d_sc, d_sk, d_pre, d_post,
     m_sc, m_sk, m_pre, m_post, v_sc, v_sk, v_pre, v_post, loss_sum) = small_out[:17]
    g_pw, d_pw, m_pw, v_pw = [a.reshape(pool_w.shape) for a in small_out[17:]]
    return (loss_sum[0, 0], grad_x[None], t(grad_w_in_t), g_pw, g_sc, g_sk, grad_w_out, g_pre, g_post,
            t(d_in), d_pw, d_sc, d_sk, d_out, d_pre, d_post,
            t(nm_in), m_pw, m_sc, m_sk, nm_out, m_pre, m_post,
            t(nv_in), v_pw, v_sc, v_sk, nv_out, v_pre, v_post)
```

```python
import jax
import jax.numpy as jnp
from jax import lax
from jax.experimental import pallas as pl
from jax.experimental.pallas import tpu as pltpu

F32 = jnp.float32
BF16 = jnp.bfloat16

S = 2048
D = 1024
DEPTH = 2
D_POOL = 512
POOL_WINDOWS = (2, 4, 8, 16)
N_HEADS = 8
D_IN = 2304
N_SHARDS = 4
W_IN_SHARD = D_IN // N_SHARDS
W_OUT_SHARD = D // N_SHARDS
BLK = 128
NB = S // BLK
HALO = 16
PAD = 8
EPS = 1e-6
NEG_INF = -1e30
C_PU, C_PG, C_Q, C_K, C_V, C_AG = 0, 512, 1024, 1536, 1664, 1792

ADAM_LR = 0.001
ADAM_B1 = 0.9
ADAM_B2 = 0.999
ADAM_EPS = 1e-08
ADAM_WD = 0.01
ADAM_STEP = 10

TM = 512
VMEM_LIMIT = 56 * 1024 * 1024

NT = (((1,), (1,)), ((), ()))
TN = (((0,), (0,)), ((), ()))

MESH = pl.DeviceIdType.MESH
ANY = pl.BlockSpec(memory_space=pl.ANY)

MISC_SCALE, MISC_PRE, MISC_POST, MISC_SINKS, MISC_LOSS = 0, 8, 24, 40, 56
MISC_ROWS = 64


def _params(sem=("arbitrary",)):
    return pltpu.CompilerParams(dimension_semantics=sem, vmem_limit_bytes=VMEM_LIMIT)


def _sigmoid(v):
    return 1.0 / (1.0 + jnp.exp(-v))


def _rows8(v):
    r, c = v.shape
    return v.reshape(r // 8, 8, c).sum(axis=0)


def _layer(l, *shape):
    zeros = (0,) * len(shape)
    return pl.BlockSpec((None,) + shape, lambda i: (l,) + zeros)


def _whole(shape):
    zeros = (0,) * len(shape)
    return pl.BlockSpec(shape, lambda i: zeros, pipeline_mode=pl.Buffered(1))


def _fwd_in(l, x, g_pre, w_in_t, below=None):
    fused = below is not None

    def body(x_ref, g_ref, w_ref, *rest):
        if fused:
            cat_ref, wo_ref, gp_ref, y_ref, xn_ref = rest[:5]
            y = jnp.dot(cat_ref[...], wo_ref[...], preferred_element_type=F32)
            y_ref[...] = y
            xt = x_ref[...] + y * lax.rsqrt(jnp.mean(y * y, axis=-1, keepdims=True) + EPS) * gp_ref[...]
            xn_ref[...] = xt
        else:
            xt = x_ref[...]
        pu_ref, pg_ref, q_ref, kv_ref, ag_ref = rest[-5:]
        r = lax.rsqrt(jnp.mean(xt * xt, axis=-1, keepdims=True) + EPS)
        h = (xt * r * g_ref[...]).astype(BF16)

        def proj(lo, hi):
            return lax.dot_general(h, w_ref[lo:hi, :], NT, preferred_element_type=F32)

        pu_ref[...] = proj(C_PU, C_PG)
        pg_ref[...] = proj(C_PG, C_Q)
        q_ref[...] = proj(C_Q, C_K).astype(BF16)
        kv_ref[...] = proj(C_K, C_AG).astype(BF16)
        ag_ref[...] = proj(C_AG, D_IN)

    row = lambda w: pl.BlockSpec((TM, w), lambda i: (i, 0))
    act = jax.ShapeDtypeStruct((S, D), F32)
    return pl.pallas_call(
        body, name="fwd_out_in" if fused else "fwd_in", grid=(S // TM,),
        in_specs=[row(D), _layer(l, 1, D), _whole((D_IN, D))]
        + ([row(D), _whole((D, D)), _layer(l - 1, 1, D)] if fused else []),
        out_specs=[row(D)] * (2 * fused) + [row(512), row(512), row(512), row(256), row(512)],
        out_shape=[act] * (2 * fused)
        + [jax.ShapeDtypeStruct((S, 512), F32), jax.ShapeDtypeStruct((S, 512), F32),
           jax.ShapeDtypeStruct((S, 512), BF16), jax.ShapeDtypeStruct((S, 256), BF16),
           jax.ShapeDtypeStruct((S, 512), F32)],
        compiler_params=_params(),
    )(x, g_pre, w_in_t, *(below if fused else ()))


LOG2E = 1.4426950408889634
SCORE_SCALE = 0.125 * LOG2E


def _attention_tables():
    qi = jnp.arange(BLK)[:, None]
    kj = jnp.arange(BLK)[None, :]
    dist = ((qi - kj) % BLK).astype(F32)
    slopes = jnp.exp2(-jnp.arange(1, N_HEADS + 1, dtype=F32))
    bias = -(slopes * LOG2E)[:, None, None] * dist[None]
    first = jnp.where(kj > qi, NEG_INF, bias)
    return jnp.stack([first, bias]), (kj <= qi).astype(BF16)


def _own_block_mask():
    return lax.broadcasted_iota(jnp.int32, (BLK, BLK), 1) <= lax.broadcasted_iota(jnp.int32, (BLK, BLK), 0)


def _merge(full, own):
    return jnp.where(own, full[:, BLK:], full[:, :BLK])


def _spread(v, tri):
    own = v * tri
    return jnp.concatenate([v - own, own], axis=1)


def _head_variants(cur, prev):
    both = jnp.concatenate([prev, cur], axis=0).astype(F32)
    swapped = pltpu.roll(both, 64, axis=1)
    low = lax.broadcasted_iota(jnp.int32, both.shape, 1) < 64
    zero = jnp.zeros_like(both)
    return ((jnp.where(low, both, zero).astype(BF16), jnp.where(low, zero, swapped).astype(BF16)),
            (jnp.where(low, swapped, zero).astype(BF16), jnp.where(low, zero, both).astype(BF16)))


def _head_of(hkv, t, half):
    return hkv * 4 + 2 * t + half


def _rows(v, t):
    return v[t * BLK:(t + 1) * BLK]


def _stack_tiles(ref, hkv, offset=0):
    lo = offset + 2 * hkv * 128
    return jnp.concatenate([ref[:, lo:lo + 128], ref[:, lo + 128:lo + 256]], axis=0)


def _scores(q2, k_var, own):
    s = {}
    for hkv in range(2):
        for half in range(2):
            full = lax.dot_general(q2[hkv], k_var[hkv][half], NT, preferred_element_type=F32)
            for t in range(2):
                s[hkv, t, half] = _merge(_rows(full, t), own)
    return s


def _softmax(s, bias, sink):
    s = s * SCORE_SCALE + bias
    sink2 = sink * LOG2E
    m = jnp.maximum(jnp.max(s, axis=-1, keepdims=True), sink2)
    p = jnp.exp2(s - m)
    e_sink = jnp.exp2(sink2 - m)
    inv = 1.0 / (jnp.sum(p, axis=-1, keepdims=True) + e_sink)
    return p * inv, e_sink * inv


def _spread_pair(v, hkv, half, tri):
    return jnp.concatenate([_spread(v[hkv, t, half].astype(BF16), tri) for t in range(2)], axis=0)


POOL_ROWS = PAD + HALO + BLK


def _window_sums(src_ref, tmp_refs, trailing):
    lo, hi = (PAD, POOL_ROWS) if trailing else (0, HALO + BLK)
    cur = src_ref
    for level in range(len(POOL_WINDOWS)):
        lanes = slice(level * 128, 512)
        shift = -(1 << level) if trailing else (1 << level)
        dst = tmp_refs[level % 2]
        dst[lo:hi, lanes] = cur[lo:hi, lanes] + cur[lo + shift:hi + shift, lanes]
        cur = dst


def _pool_block(ext_ref, tmp_refs, i, g, w):
    lanes = slice(g * 128, (g + 1) * 128)
    rows = slice(PAD + HALO, POOL_ROWS)
    t = (i * BLK + lax.broadcasted_iota(jnp.int32, (BLK, 1), 0)).astype(F32)
    inv = 1.0 / jnp.minimum(t + 1.0, float(w))
    return tmp_refs[g % 2][rows, lanes] * inv - ext_ref[rows, lanes], inv


def _fwd_mix(l, pu, pg, q, kv, ag, pool_w, pool_scale, sinks, tables):
    bias, tri = tables

    def body(pu_ref, pup_ref, pg_ref, q_ref, kv_ref, kvp_ref, ag_ref, pw_ref, sc_ref, sink_ref, bias_ref, tri_ref,
             cat_ref, ext_ref, *tmp_refs):
        i = pl.program_id(0)

        @pl.when(i == 0)
        def _():
            for ref in (ext_ref, *tmp_refs):
                ref[0:PAD, :] = jnp.zeros((PAD, 512), F32)

        ext_ref[PAD:PAD + HALO, :] = jnp.where(i > 0, pup_ref[...], 0.0)
        ext_ref[PAD + HALO:POOL_ROWS, :] = pu_ref[...]
        _window_sums(ext_ref, tmp_refs, True)
        for g, w in enumerate(POOL_WINDOWS):
            lanes = slice(g * 128, (g + 1) * 128)
            pooled, _ = _pool_block(ext_ref, tmp_refs, i, g, w)
            mixed = jnp.dot(pooled.astype(BF16), pw_ref[g], preferred_element_type=F32)
            gate = pg_ref[:, lanes]
            cat_ref[:, lanes] = (mixed * sc_ref[:, lanes] * (gate * _sigmoid(gate))).astype(BF16)

        own = _own_block_mask()
        tri = tri_ref[...]
        k_var = _head_variants(kv_ref[:, 0:128], kvp_ref[:, 0:128])
        v_var = _head_variants(kv_ref[:, 128:256], kvp_ref[:, 128:256])
        s = _scores([_stack_tiles(q_ref, hkv) for hkv in range(2)], k_var, own)
        p = {}
        for (hkv, t, half), s_head in s.items():
            head = _head_of(hkv, t, half)
            p[hkv, t, half], _ = _softmax(s_head, bias_ref[head], sink_ref[l, head])
        for hkv in range(2):
            o2 = jnp.zeros((2 * BLK, 128), F32)
            for half in range(2):
                o2 = o2 + jnp.dot(_spread_pair(p, hkv, half, tri), v_var[hkv][half], preferred_element_type=F32)
            for t in range(2):
                lo = (2 * hkv + t) * 128
                gate = ag_ref[:, lo:lo + 128]
                cat_ref[:, D_POOL + lo:D_POOL + lo + 128] = (_rows(o2, t) * (gate * _sigmoid(gate))).astype(BF16)

    blk = lambda w: pl.BlockSpec((BLK, w), lambda i: (i, 0))
    prev = lambda w: pl.BlockSpec((BLK, w), lambda i: (jnp.maximum(i - 1, 0), 0))
    halo = pl.BlockSpec((HALO, 512), lambda i: (jnp.maximum(i * (BLK // HALO) - 1, 0), 0))
    return pl.pallas_call(
        body, name="fwd_mix", grid=(NB,),
        in_specs=[blk(512), halo, blk(512), blk(512), blk(256), prev(256), blk(512),
                  _layer(l, 4, 128, 128), _layer(l, 1, 512), pl.BlockSpec(memory_space=pltpu.SMEM),
                  pl.BlockSpec((None, N_HEADS, BLK, BLK), lambda i: (jnp.minimum(i, 1), 0, 0, 0)), _whole((BLK, BLK))],
        out_specs=blk(D),
        out_shape=jax.ShapeDtypeStruct((S, D), BF16),
        scratch_shapes=[pltpu.VMEM((POOL_ROWS, 512), F32)] * 3,
        compiler_params=_params(),
    )(pu, pu, pg, q, kv, kv, ag, pool_w, pool_scale, sinks, bias, tri)


def _store_lane_rows(ref, acc):
    total = jnp.sum(acc, axis=0, keepdims=True)
    for k in range(ref.shape[0]):
        ref[k:k + 1, :] = total[:, k * 128:(k + 1) * 128]


def _own_piece(dw_ref, place_ref):
    p = dw_ref.shape[0] // 8
    return dw_ref[pl.ds(pl.multiple_of(place_ref[0] * p, 8), p), :]


def _bwd_out(l, cat, w_out, g_post, place_arr, dxn=None, y=None, x=None, target=None, deps=()):
    last = target is not None
    n_steps = S // TM

    def body(a_ref, b_ref, g_ref, cat_ref, w_ref, place_ref, *rest):
        dcat_ref, own_ref, dwb_ref, dg_ref = rest[len(deps):len(deps) + 4]
        rest = rest[len(deps) + 4:]
        acc_ref, dw_ref = rest[-2:]
        step = pl.program_id(0)

        @pl.when(step == 0)
        def _():
            dw_ref[...] = jnp.zeros_like(dw_ref)
            acc_ref[...] = jnp.zeros_like(acc_ref)

        cat = cat_ref[...]
        g = g_ref[...]
        y = jnp.dot(cat, w_ref[...], preferred_element_type=F32) if last else b_ref[...]
        r = lax.rsqrt(jnp.mean(y * y, axis=-1, keepdims=True) + EPS)
        if last:
            loss_ref, dx_ref, loss_acc_ref = rest[:3]
            err = a_ref[...] + y * r * g - b_ref[...]

            @pl.when(step == 0)
            def _():
                loss_acc_ref[...] = jnp.zeros_like(loss_acc_ref)

            loss_acc_ref[...] += _rows8(err * err)
            dz = err * (1.0 / D)
            dx_ref[...] = dz
        else:
            dz = a_ref[...]
        a = dz * g
        dy = r * a - y * (r * r * r) * jnp.mean(a * y, axis=-1, keepdims=True)
        acc_ref[...] += _rows8(dz * (y * r))
        dyb = dy.astype(BF16)
        dcat_ref[...] = lax.dot_general(dyb, w_ref[...], NT, preferred_element_type=F32)
        dw_ref[...] += lax.dot_general(cat, dyb, TN, preferred_element_type=F32)

        @pl.when(step == n_steps - 1)
        def _():
            _store_lane_rows(dg_ref, acc_ref[...])
            dwb_ref[...] = dw_ref[...].astype(BF16)
            own_ref[...] = _own_piece(dw_ref, place_ref)
            if last:
                loss_ref[...] = jnp.full((8, 128), (0.5 / D) * jnp.sum(loss_acc_ref[...]), F32)

    row = lambda: pl.BlockSpec((TM, D), lambda i: (i, 0))
    full = _whole
    return pl.pallas_call(
        body, name="out_loss_bwd" if last else "bwd_out", grid=(n_steps,),
        in_specs=[row(), row(), _layer(l, 1, D), row(), full((D, D)), pl.BlockSpec(memory_space=pltpu.SMEM)]
        + [ANY] * len(deps),
        out_specs=[row(), full((D // 8, D)), full((D, D)), full((8, 128))] + ([full((8, 128)), row()] if last else []),
        out_shape=[jax.ShapeDtypeStruct((S, D), F32), jax.ShapeDtypeStruct((D // 8, D), F32),
                   jax.ShapeDtypeStruct((D, D), BF16), jax.ShapeDtypeStruct((8, 128), F32)]
        + ([jax.ShapeDtypeStruct((8, 128), F32), jax.ShapeDtypeStruct((S, D), F32)] if last else []),
        scratch_shapes=([pltpu.VMEM((8, D), F32)] if last else []) + [pltpu.VMEM((8, D), F32), pltpu.VMEM((D, D), F32)],
        compiler_params=_params(),
    )(*((x, target) if last else (dxn, y)), g_post, cat, w_out, place_arr, *deps)


def _bwd_mix(l, pu, pg, q, kv, ag, dcat, pool_w, pool_scale, sinks, tables, deps=(), dpw_dest=None):
    bias, tri = tables
    deps = tuple(deps) + (() if dpw_dest is None else (dpw_dest,))

    def body(pu_ref, pup_ref, pg_ref, q_ref, kv_ref, kvp_ref, ag_ref, dcat_ref, pw_ref, sc_ref, sink_ref, bias_ref,
             tri_ref, *rest):
        dproj_ref, dpw_ref, dsc_ref, dsink_ref, ext_ref, dext_ref, tmp_a, tmp_b, dkv_ref = rest[len(deps):]
        tmp_refs = (tmp_a, tmp_b)
        step = pl.program_id(0)
        i = NB - 1 - step

        @pl.when(step == 0)
        def _():
            dpw_ref[...] = jnp.zeros_like(dpw_ref)
            dsc_ref[...] = jnp.zeros_like(dsc_ref)
            dsink_ref[...] = jnp.zeros_like(dsink_ref)
            for ref in (ext_ref, tmp_a, tmp_b):
                ref[0:PAD, :] = jnp.zeros((PAD, 512), F32)
            dext_ref[BLK:POOL_ROWS, :] = jnp.zeros((HALO + PAD, 512), F32)
            dkv_ref[...] = jnp.zeros_like(dkv_ref)

        ext_ref[PAD:PAD + HALO, :] = jnp.where(i > 0, pup_ref[...], 0.0)
        ext_ref[PAD + HALO:POOL_ROWS, :] = pu_ref[...]
        _window_sums(ext_ref, tmp_refs, True)
        dpooled = []
        for g, w in enumerate(POOL_WINDOWS):
            lanes = slice(g * 128, (g + 1) * 128)
            pooled, inv = _pool_block(ext_ref, tmp_refs, i, g, w)
            pooled_b = pooled.astype(BF16)
            mixed = jnp.dot(pooled_b, pw_ref[g], preferred_element_type=F32)
            scale = sc_ref[:, lanes]
            gate = pg_ref[:, lanes]
            sg = _sigmoid(gate)
            dpo = dcat_ref[:, lanes]
            dproj_ref[:, C_PG + g * 128:C_PG + (g + 1) * 128] = (
                dpo * (mixed * scale) * (sg * (1.0 + gate * (1.0 - sg)))).astype(BF16)
            dms = dpo * (gate * sg)
            dsc_ref[g:g + 1, :] += jnp.sum(dms * mixed, axis=0, keepdims=True)
            dmixed = (dms * scale).astype(BF16)
            dpw_ref[g] += lax.dot_general(pooled_b, dmixed, TN, preferred_element_type=F32)
            dpooled.append(lax.dot_general(dmixed, pw_ref[g], NT, preferred_element_type=F32))
            dext_ref[0:BLK, lanes] = dpooled[g] * inv
        _window_sums(dext_ref, tmp_refs, False)
        for g in range(len(POOL_WINDOWS)):
            lanes = slice(g * 128, (g + 1) * 128)
            dproj_ref[:, C_PU + g * 128:C_PU + (g + 1) * 128] = (tmp_refs[g % 2][0:BLK, lanes] - dpooled[g]).astype(BF16)
        dext_ref[BLK:BLK + HALO, :] = dext_ref[0:HALO, :]

        own = _own_block_mask()
        tri = tri_ref[...]
        k_var = _head_variants(kv_ref[:, 0:128], kvp_ref[:, 0:128])
        v_var = _head_variants(kv_ref[:, 128:256], kvp_ref[:, 128:256])
        q2 = [_stack_tiles(q_ref, hkv) for hkv in range(2)]
        s = _scores(q2, k_var, own)
        p, p_sink = {}, {}
        for key, s_head in s.items():
            head = _head_of(*key)
            p[key], p_sink[key] = _softmax(s_head, bias_ref[head], sink_ref[l, head])

        do2, p_b, dp = [], {}, {}
        for hkv in range(2):
            gate = _stack_tiles(ag_ref, hkv)
            sg = _sigmoid(gate)
            dca = _stack_tiles(dcat_ref, hkv, D_POOL)
            do2.append((dca * (gate * sg)).astype(BF16))
            o2 = jnp.zeros((2 * BLK, 128), F32)
            for half in range(2):
                p_b[hkv, half] = _spread_pair(p, hkv, half, tri)
                o2 = o2 + jnp.dot(p_b[hkv, half], v_var[hkv][half], preferred_element_type=F32)
                full = lax.dot_general(do2[hkv], v_var[hkv][half], NT, preferred_element_type=F32)
                for t in range(2):
                    dp[hkv, t, half] = _merge(_rows(full, t), own)
            dag = dca * o2 * (sg * (1.0 + gate * (1.0 - sg)))
            for t in range(2):
                lo = C_AG + (2 * hkv + t) * 128
                dproj_ref[:, lo:lo + 128] = _rows(dag, t).astype(BF16)

        ds = {}
        for key in p:
            delta = jnp.sum(p[key] * dp[key], axis=-1, keepdims=True)
            ds[key] = p[key] * (dp[key] - delta)
            head = _head_of(*key)
            dsink_ref[0:1, :] += jnp.where(lax.broadcasted_iota(jnp.int32, (1, 128), 1) == head,
                                           -jnp.sum(p_sink[key] * delta, axis=0, keepdims=True), 0.0)

        dk_acc = [[None, None], [None, None]]
        dv_acc = [[None, None], [None, None]]
        for hkv in range(2):
            dq2 = jnp.zeros((2 * BLK, 128), F32)
            for half in range(2):
                ds_b = _spread_pair(ds, hkv, half, tri)
                dq2 = dq2 + jnp.dot(ds_b, k_var[hkv][half], preferred_element_type=F32)
                dk_acc[hkv][half] = lax.dot_general(ds_b, q2[hkv], TN, preferred_element_type=F32)
                dv_acc[hkv][half] = lax.dot_general(p_b[hkv, half], do2[hkv], TN, preferred_element_type=F32)
            for t in range(2):
                lo = C_Q + (2 * hkv + t) * 128
                dproj_ref[:, lo:lo + 128] = (_rows(dq2, t) * 0.125).astype(BF16)

        low = lax.broadcasted_iota(jnp.int32, (2 * BLK, 128), 1) < 64

        def gather_heads(acc):
            return jnp.where(low, acc[0][0] + pltpu.roll(acc[0][1], 64, axis=1),
                             pltpu.roll(acc[1][0], 64, axis=1) + acc[1][1])

        dk = gather_heads(dk_acc) * 0.125
        dv = gather_heads(dv_acc)
        dproj_ref[:, C_K:C_V] = (dk[BLK:, :] + dkv_ref[:, 0:128]).astype(BF16)
        dproj_ref[:, C_V:C_AG] = (dv[BLK:, :] + dkv_ref[:, 128:256]).astype(BF16)
        dkv_ref[:, 0:128] = dk[:BLK, :]
        dkv_ref[:, 128:256] = dv[:BLK, :]

    rev = lambda w: pl.BlockSpec((BLK, w), lambda s: (NB - 1 - s, 0))
    prev = lambda w: pl.BlockSpec((BLK, w), lambda s: (jnp.maximum(NB - 2 - s, 0), 0))
    halo = pl.BlockSpec((HALO, 512), lambda s: (jnp.maximum((NB - 1 - s) * (BLK // HALO) - 1, 0), 0))
    return pl.pallas_call(
        body, name="bwd_mix", grid=(NB,),
        in_specs=[rev(512), halo, rev(512), rev(512), rev(256), prev(256), rev(512), rev(D),
                  _layer(l, 4, 128, 128), _layer(l, 1, 512), pl.BlockSpec(memory_space=pltpu.SMEM),
                  pl.BlockSpec((None, N_HEADS, BLK, BLK), lambda s: (jnp.minimum(NB - 1 - s, 1), 0, 0, 0)),
                  _whole((BLK, BLK))] + [ANY] * len(deps),
        out_specs=[rev(D_IN), _layer(l, 4, 128, 128),
                   pl.BlockSpec((4, 128), lambda s: (0, 0)), pl.BlockSpec((8, 128), lambda s: (0, 0))],
        out_shape=[jax.ShapeDtypeStruct((S, D_IN), BF16), jax.ShapeDtypeStruct((DEPTH, 4, 128, 128), F32),
                   jax.ShapeDtypeStruct((4, 128), F32), jax.ShapeDtypeStruct((8, 128), F32)],
        input_output_aliases={} if dpw_dest is None else {12 + len(deps): 1},
        scratch_shapes=[pltpu.VMEM((POOL_ROWS, 512), F32)] * 4 + [pltpu.VMEM((BLK, 256), F32)],
        compiler_params=_params(),
    )(pu, pu, pg, q, kv, kv, ag, dcat, pool_w, pool_scale, sinks, bias, tri, *deps)


def _bwd_in_dw(l, dproj, x, g_pre, place_arr, deps=()):
    n_steps = S // TM

    def body(dp_ref, x_ref, g_ref, place_ref, *rest):
        own_ref, dwb_ref, dw_ref = rest[len(deps):]
        step = pl.program_id(0)

        @pl.when(step == 0)
        def _():
            dw_ref[...] = jnp.zeros_like(dw_ref)

        xt = x_ref[...]
        r = lax.rsqrt(jnp.mean(xt * xt, axis=-1, keepdims=True) + EPS)
        h = (xt * r * g_ref[...]).astype(BF16)
        dw_ref[...] += lax.dot_general(dp_ref[...], h, TN, preferred_element_type=F32)

        @pl.when(step == n_steps - 1)
        def _():
            dwb_ref[...] = dw_ref[...].astype(BF16)
            own_ref[...] = _own_piece(dw_ref, place_ref)

    row = lambda w: pl.BlockSpec((TM, w), lambda i: (i, 0))
    full = _whole
    return pl.pallas_call(
        body, name="bwd_in_dw", grid=(n_steps,),
        in_specs=[row(D_IN), row(D), _layer(l, 1, D), pl.BlockSpec(memory_space=pltpu.SMEM)] + [ANY] * len(deps),
        out_specs=[full((D_IN // 8, D)), full((D_IN, D))],
        out_shape=[jax.ShapeDtypeStruct((D_IN // 8, D), F32), jax.ShapeDtypeStruct((D_IN, D), BF16)],
        scratch_shapes=[pltpu.VMEM((D_IN, D), F32)],
        compiler_params=_params(),
    )(dproj, x, g_pre, place_arr, *deps)


def _bwd_in_dx(l, dproj, w_in_t, x, g_pre, dres, deps=(), dw_place=None):
    n_steps = S // TM
    with_dw = dw_place is not None

    def body(dp_ref, w_ref, x_ref, g_ref, dres_ref, *rest):
        place_ref = rest[0] if with_dw else None
        rest = rest[with_dw + len(deps):]
        if with_dw:
            dx_ref, dg_ref, own_ref, dwb_ref, acc_ref, dw_ref = rest
        else:
            dx_ref, dg_ref, acc_ref = rest
        step = pl.program_id(0)

        @pl.when(step == 0)
        def _():
            acc_ref[...] = jnp.zeros_like(acc_ref)
            if with_dw:
                dw_ref[...] = jnp.zeros_like(dw_ref)

        dp = dp_ref[...]
        dh = jnp.dot(dp, w_ref[...], preferred_element_type=F32)
        xt = x_ref[...]
        r = lax.rsqrt(jnp.mean(xt * xt, axis=-1, keepdims=True) + EPS)
        xn = xt * r
        g = g_ref[...]
        acc_ref[...] += _rows8(dh * xn)
        a = dh * g
        dx_ref[...] = dres_ref[...] + (r * a - xt * (r * r * r) * jnp.mean(a * xt, axis=-1, keepdims=True))
        if with_dw:
            dw_ref[...] += lax.dot_general(dp, (xn * g).astype(BF16), TN, preferred_element_type=F32)

        @pl.when(step == n_steps - 1)
        def _():
            _store_lane_rows(dg_ref, acc_ref[...])
            if with_dw:
                dwb_ref[...] = dw_ref[...].astype(BF16)
                own_ref[...] = _own_piece(dw_ref, place_ref)

    row = lambda w: pl.BlockSpec((TM, w), lambda i: (i, 0))
    full = _whole
    dw_specs = [full((D_IN // 8, D)), full((D_IN, D))] if with_dw else []
    dw_shapes = [jax.ShapeDtypeStruct((D_IN // 8, D), F32), jax.ShapeDtypeStruct((D_IN, D), BF16)] if with_dw else []
    return pl.pallas_call(
        body, name="bwd_in" if with_dw else "bwd_in_dx", grid=(n_steps,),
        in_specs=[row(D_IN), full((D_IN, D)), row(D), _layer(l, 1, D), row(D)]
        + [pl.BlockSpec(memory_space=pltpu.SMEM)] * with_dw + [ANY] * len(deps),
        out_specs=[row(D), full((8, 128))] + dw_specs,
        out_shape=[jax.ShapeDtypeStruct((S, D), F32), jax.ShapeDtypeStruct((8, 128), F32)] + dw_shapes,
        scratch_shapes=[pltpu.VMEM((8, D), F32)] + [pltpu.VMEM((D_IN, D), F32)] * with_dw,
        compiler_params=_params(),
    )(dproj, w_in_t, x, g_pre, dres, *((dw_place,) if with_dw else ()), *deps)


HBM =pl.BlockSpec(memory_space=pltpu.HBM)
SEM = pl.BlockSpec(memory_space=pltpu.SEMAPHORE)
SPLIT_COPY = pltpu.CompilerParams(has_side_effects=pltpu.SideEffectType.DATAFLOW_SIDE_EFFECTING)


def _in_hbm(a):
    return pltpu.with_memory_space_constraint(a, pltpu.HBM)

def _place():
    return lax.axis_index("x"), lax.axis_index("y"), lax.axis_index("c")


def _other_chips(x, y):
    return [(1 - x, y), (x, 1 - y), (1 - x, 1 - y)]


def _peer(x, y, c, m):
    return (x ^ (m >> 2), y ^ ((m >> 1) & 1), c ^ (m & 1))


def _place_cast(name, src, chip_arr, tile, layers, deps=()):
    _, n, cols = src.shape
    steps = n // tile
    k = len(layers)

    def body(chip_ref, *refs):
        for s_ref, o_ref in zip(refs[:k], refs[k + len(deps):]):
            o_ref[...] = s_ref[...].astype(BF16)

    def layer_spec(l):
        return pl.BlockSpec((None, tile, cols), lambda i, chip: (l, i, 0))

    return pl.pallas_call(
        body, name=name,
        grid_spec=pltpu.PrefetchScalarGridSpec(
            num_scalar_prefetch=1, grid=(steps,),
            in_specs=[layer_spec(l) for l in layers] + [ANY] * len(deps),
            out_specs=[pl.BlockSpec((tile, cols), lambda i, chip: (chip[0] * steps + i, 0))] * k),
        out_shape=[jax.ShapeDtypeStruct((N_SHARDS * n, cols), BF16)] * k,
        compiler_params=_params(),
    )(chip_arr, *[src] * k, *deps)


def _chip_rows(ref, chip, half=None):
    n = ref.shape[0] // N_SHARDS
    if half is None:
        return ref.at[pl.ds(pl.multiple_of(chip * n, 16), n), :]
    return ref.at[pl.ds(pl.multiple_of(chip * n + half * (n // 2), 16), n // 2), :]


def _gather_start(name, bufs, halved):
    n = len(bufs)

    def body(*refs):
        ins, send, recv, token = refs[:n], refs[n:2 * n], refs[2 * n:3 * n], refs[-1]
        x, y, c = _place()
        for a, buf in enumerate(ins):
            own = _chip_rows(buf, 2 * x + y, c if a in halved else None)
            for j, chip in enumerate(_other_chips(x, y)):
                pltpu.make_async_remote_copy(src_ref=own, dst_ref=own, send_sem=send[a].at[j], recv_sem=recv[a].at[j],
                                             device_id=(*chip, c), device_id_type=MESH).start()
        token[...] = jnp.zeros_like(token)

    outs = pl.pallas_call(
        body, name=name, in_specs=[HBM] * n,
        out_specs=[SEM] * (2 * n) + [HBM] * n + [pl.BlockSpec(memory_space=pltpu.VMEM)],
        out_shape=[pltpu.SemaphoreType.DMA((3,))] * (2 * n) + [pltpu.HBM(b.shape, b.dtype) for b in bufs]
        + [jax.ShapeDtypeStruct((8, 128), F32)],
        input_output_aliases={a: 2 * n + a for a in range(n)},
        compiler_params=SPLIT_COPY,
    )(*[_in_hbm(b) for b in bufs])
    return outs[:n], outs[n:2 * n], outs[2 * n:3 * n], outs[-1]


def _gather_wait(name, buf, send_sem, recv_sem, after, halved=False):
    def body(buf_ref, send_ref, recv_ref, *rest):
        x, y, c = _place()
        half = c if halved else None
        own = _chip_rows(buf_ref, 2 * x + y, half)
        for j, chip in enumerate(_other_chips(x, y)):
            copy = pltpu.make_async_remote_copy(src_ref=own, dst_ref=_chip_rows(buf_ref, 2 * chip[0] + chip[1], half),
                                                send_sem=send_ref.at[j], recv_sem=recv_ref.at[j],
                                                device_id=(*chip, c), device_id_type=MESH)
            copy.wait_send()
            copy.wait_recv()

    return pl.pallas_call(
        body, name=name, in_specs=[HBM, SEM, SEM] + [ANY] * len(after), out_specs=HBM,
        out_shape=pltpu.HBM(buf.shape, buf.dtype), input_output_aliases={0: 0}, compiler_params=SPLIT_COPY,
    )(buf, send_sem, recv_sem, *after)


def _forward_halves(name, buf):
    def body(in_ref, out_ref, send_sems, recv_sems):
        x, y, c = _place()

        def copy(j, chip, half):
            rows = 2 * chip[0] + chip[1]
            return pltpu.make_async_remote_copy(
                src_ref=_chip_rows(in_ref, rows, half), dst_ref=_chip_rows(out_ref, rows, half), send_sem=send_sems.at[j],
                recv_sem=recv_sems.at[j], device_id=(x, y, 1 - c), device_id_type=MESH)

        chips = _other_chips(x, y)
        for j, chip in enumerate(chips):
            copy(j, chip, c).start()
        for j, chip in enumerate(chips):
            copy(j, chip, c).wait_send()
            copy(j, chip, 1 - c).wait_recv()

    return pl.pallas_call(
        body, name=name, in_specs=[ANY], out_specs=ANY, out_shape=jax.ShapeDtypeStruct(buf.shape, buf.dtype),
        input_output_aliases={0: 0},
        scratch_shapes=[pltpu.SemaphoreType.DMA((3,))] * 2,
    )(buf)


def _piece_rows(ref, k):
    p = ref.shape[0] // 8
    return ref.at[pl.ds(pl.multiple_of(k * p, 32 // jnp.dtype(ref.dtype).itemsize), p), :]


def _exchange_start(name, arrays):
    n = len(arrays)
    zones = [lax.empty((7, a.shape[0] // 8, a.shape[1]), a.dtype) for a in arrays]

    def body(*refs):
        srcs, lands = refs[:n], refs[n:2 * n]
        send, recv, token = refs[2 * n:3 * n], refs[3 * n:4 * n], refs[-1]
        x, y, c = _place()
        for a, (src, land) in enumerate(zip(srcs, lands)):
            for m in range(1, 8):
                px, py, pc = _peer(x, y, c, m)
                pltpu.make_async_remote_copy(
                    src_ref=_piece_rows(src, 4 * px + 2 * py + pc), dst_ref=land.at[m - 1], send_sem=send[a].at[m - 1],
                    recv_sem=recv[a].at[m - 1], device_id=(px, py, pc), device_id_type=MESH).start()
        token[...] = jnp.zeros_like(token)

    outs = pl.pallas_call(
        body, name=name, in_specs=[HBM] * (2 * n),
        out_specs=[SEM] * (2 * n) + [HBM] * (2 * n) + [pl.BlockSpec(memory_space=pltpu.VMEM)],
        out_shape=[pltpu.SemaphoreType.DMA((7,))] * (2 * n) + [pltpu.HBM(a.shape, a.dtype) for a in arrays + zones]
        + [jax.ShapeDtypeStruct((8, 128), F32)],
        input_output_aliases={a: 2 * n + a for a in range(2 * n)},
        compiler_params=SPLIT_COPY,
    )(*[_in_hbm(a) for a in arrays + zones])
    return outs[:n], outs[n:2 * n], outs[2 * n:3 * n], outs[3 * n:4 * n], outs[-1]


def _exchange_wait(name, started, after):
    send_sems, recv_sems, arrays, zones, _ = started
    n = len(arrays)

    def body(*refs):
        srcs, lands = refs[:n], refs[n:2 * n]
        send, recv = refs[2 * n:3 * n], refs[3 * n:4 * n]
        x, y, c = _place()
        for a, (src, land) in enumerate(zip(srcs, lands)):
            for m in range(1, 8):
                px, py, pc = _peer(x, y, c, m)
                copy = pltpu.make_async_remote_copy(
                    src_ref=_piece_rows(src, 4 * px + 2 * py + pc), dst_ref=land.at[m - 1], send_sem=send[a].at[m - 1],
                    recv_sem=recv[a].at[m - 1], device_id=(px, py, pc), device_id_type=MESH)
                copy.wait_send()
                copy.wait_recv()

    outs = pl.pallas_call(
        body, name=name, in_specs=[HBM] * (2 * n) + [SEM] * (2 * n) + [ANY], out_specs=[HBM] * (2 * n),
        out_shape=[pltpu.HBM(a.shape, a.dtype) for a in list(arrays) + list(zones)],
        input_output_aliases={a: a for a in range(2 * n)}, compiler_params=SPLIT_COPY,
    )(*arrays, *zones, *send_sems, *recv_sems, after)
    return outs[n:]


def _sum_pieces(name, owns, recvs, place_arr, layer, dests=None):
    n = len(owns)
    steps = 2

    def body(place_ref, *refs):
        for o_ref, r_ref, out_ref in zip(refs[:n], refs[n:2 * n], refs[-n:]):
            total = o_ref[...]
            for m in range(7):
                total = total + r_ref[m].astype(F32)
            out_ref[...] = total

    tiles = [o.shape[0] // steps for o in owns]
    return pl.pallas_call(
        body, name=name,
        grid_spec=pltpu.PrefetchScalarGridSpec(
            num_scalar_prefetch=1, grid=(steps,),
            in_specs=[pl.BlockSpec((t, o.shape[1]), lambda i, place: (i, 0)) for o, t in zip(owns, tiles)]
            + [pl.BlockSpec((7, t, o.shape[1]), lambda i, place: (0, i, 0)) for o, t in zip(owns, tiles)]
            + ([] if dests is None else [ANY] * n),
            out_specs=[pl.BlockSpec((None, t, o.shape[1]), lambda i, place: (layer, place[1] * steps + i, 0))
                       for o, t in zip(owns, tiles)]),
        out_shape=[jax.ShapeDtypeStruct((DEPTH, 2 * o.shape[0], o.shape[1]), F32) for o in owns],
        input_output_aliases={} if dests is None else {1 + 2 * n + k: k for k in range(n)},
        compiler_params=_params(),
    )(place_arr, *owns, *recvs, *(() if dests is None else dests))


def _sum_small(partials, recvs, place_arr):
    n = len(partials)

    def body(place_ref, *refs):
        for o_ref, r_ref, out_ref in zip(refs[:n], refs[n:2 * n], refs[2 * n:]):
            total = o_ref[...]
            for m in range(7):
                total = total + r_ref[m]
            out_ref[...] = total

    piece = lambda a: pl.BlockSpec((a.shape[0] // 8, a.shape[1]), lambda i, place: (place[0], 0))
    return pl.pallas_call(
        body, name="sum_small",
        grid_spec=pltpu.PrefetchScalarGridSpec(
            num_scalar_prefetch=1, grid=(1,),
            in_specs=[piece(a) for a in partials] + [pl.BlockSpec(r.shape, lambda i, place: (0, 0, 0)) for r in recvs],
            out_specs=[piece(a) for a in partials]),
        out_shape=[jax.ShapeDtypeStruct(a.shape, F32) for a in partials],
        compiler_params=_params(),
    )(place_arr, *partials, *recvs)


def _share(name, bufs, parts, gathered=()):
    n, n_g = len(bufs), len(gathered)
    total = n + n_g

    def body(*refs):
        ins, outs = refs[:total], refs[total:2 * total]
        send_sems, recv_sems, send_g, recv_g = refs[2 * total:]
        x, y, c = _place()

        def half(ref, l, which):
            p = ref.shape[1] // 2
            return ref.at[l, pl.ds(pl.multiple_of(which * p, 8), p), :]

        def swap(k, which):
            a, l = parts[k]
            return pltpu.make_async_remote_copy(
                src_ref=half(ins[a], l, which), dst_ref=half(outs[a], l, which), send_sem=send_sems.at[k],
                recv_sem=recv_sems.at[k], device_id=(x, y, 1 - c), device_id_type=MESH)

        def spread(a, m, sender):
            k = 4 * sender[0] + 2 * sender[1] + sender[2]
            return pltpu.make_async_remote_copy(
                src_ref=_piece_rows(ins[n + a], k), dst_ref=_piece_rows(outs[n + a], k), send_sem=send_g.at[7 * a + m - 1],
                recv_sem=recv_g.at[7 * a + m - 1], device_id=_peer(x, y, c, m), device_id_type=MESH)

        for k in range(len(parts)):
            swap(k, c).start()
        for a in range(n_g):
            for m in range(1, 8):
                spread(a, m, (x, y, c)).start()
        for k in range(len(parts)):
            swap(k, c).wait_send()
            swap(k, 1 - c).wait_recv()
        for a in range(n_g):
            for m in range(1, 8):
                spread(a, m, (x, y, c)).wait_send()
                spread(a, m, _peer(x, y, c, m)).wait_recv()

    arrays = list(bufs) + list(gathered)
    return pl.pallas_call(
        body, name=name, in_specs=[ANY] * total, out_specs=[ANY] * total,
        out_shape=[jax.ShapeDtypeStruct(b.shape, F32) for b in arrays],
        input_output_aliases={a: a for a in range(total)},
        scratch_shapes=[pltpu.SemaphoreType.DMA((max(len(parts), 1),))] * 2
        + [pltpu.SemaphoreType.DMA((max(7 * n_g, 1),))] * 2,
    )(*arrays)


def _adamw_math(w, g, m, v):
    nm = ADAM_B1 * m + (1.0 - ADAM_B1) * g
    nv = ADAM_B2 * v + (1.0 - ADAM_B2) * (g * g)
    m_hat = nm / (1.0 - ADAM_B1 ** ADAM_STEP)
    v_hat = nv / (1.0 - ADAM_B2 ** ADAM_STEP)
    return -ADAM_LR * (m_hat / (jnp.sqrt(v_hat) + ADAM_EPS) + ADAM_WD * w), nm, nv


def _adamw(name, w, g, m, v, rows_per_step, first=0, count=None, dests=None, deps=()):
    layers, rows, cols = w.shape
    count = layers if count is None else count

    def body(w_ref, g_ref, m_ref, v_ref, *rest):
        d_ref, nm_ref, nv_ref, g_out_ref = rest[-4:]
        d_ref[...], nm_ref[...], nv_ref[...] = _adamw_math(w_ref[...], g_ref[...], m_ref[...], v_ref[...])
        g_out_ref[...] = g_ref[...]

    spec = pl.BlockSpec((1, rows_per_step, cols), lambda l, i: (first + l, i, 0))
    shape = jax.ShapeDtypeStruct(w.shape, F32)
    dests = () if dests is None else tuple(dests)
    return pl.pallas_call(
        body, name=name, grid=(count, rows // rows_per_step),
        in_specs=[spec] * 4 + [ANY] * (len(dests) + len(deps)), out_specs=[spec] * 4, out_shape=[shape] * 4,
        input_output_aliases={4 + k: k for k in range(len(dests))},
        compiler_params=_params(("arbitrary", "arbitrary")),
    )(w, g, m, v, *dests, *deps)


def _pack_misc(pool_scale, sinks, norm_pre, norm_post):
    sink_rows = jnp.zeros((DEPTH, 8, 128), F32).at[:, 0, 0:N_HEADS].set(sinks).reshape(2 * 8, 128)
    return jnp.concatenate([pool_scale.reshape(8, 128), norm_pre.reshape(16, 128), norm_post.reshape(16, 128),
                            sink_rows, jnp.zeros((8, 128), F32)], axis=0)


def _adamw_small(w, g, m, v, pool):
    def body(w_ref, g_ref, m_ref, v_ref, pw_ref, pg_ref, pm_ref, pv_ref, *rest):
        outs, pool_outs, (d_ref, nm_ref, nv_ref) = rest[:17], rest[17:21], rest[21:]
        pool_outs[0][...] = pg_ref[...]
        pool_outs[1][...], pool_outs[2][...], pool_outs[3][...] = _adamw_math(
            pw_ref[...], pg_ref[...], pm_ref[...], pv_ref[...])
        d_ref[...], nm_ref[...], nv_ref[...] = _adamw_math(w_ref[...], g_ref[...], m_ref[...], v_ref[...])
        for k, src in enumerate([g_ref, d_ref, nm_ref, nv_ref]):
            scale, sinks, pre, post = outs[4 * k:4 * k + 4]
            for l in range(DEPTH):
                for j in range(4):
                    scale[l:l + 1, j * 128:(j + 1) * 128] = src[MISC_SCALE + 4 * l + j:MISC_SCALE + 4 * l + j + 1, :]
                for j in range(8):
                    pre[l:l + 1, j * 128:(j + 1) * 128] = src[MISC_PRE + 8 * l + j:MISC_PRE + 8 * l + j + 1, :]
                    post[l:l + 1, j * 128:(j + 1) * 128] = src[MISC_POST + 8 * l + j:MISC_POST + 8 * l + j + 1, :]
                sinks[l:l + 1, :] = src[MISC_SINKS + 8 * l:MISC_SINKS + 8 * l + 1, 0:N_HEADS]
        outs[16][...] = g_ref[MISC_LOSS:MISC_LOSS + 1, 0:1]

    vmem = pl.BlockSpec(memory_space=pltpu.VMEM)
    shapes = [(DEPTH, D_POOL), (DEPTH, N_HEADS), (DEPTH, D), (DEPTH, D)] * 4 + [(1, 1)]
    shapes += [pool[0].shape] * 4
    return pl.pallas_call(
        body, name="adamw_small", in_specs=[vmem] * 8, out_specs=[vmem] * 21,
        out_shape=[jax.ShapeDtypeStruct(s, F32) for s in shapes],
        scratch_shapes=[pltpu.VMEM((MISC_ROWS, 128), F32)] * 3,
    )(w, g, m, v, *pool)


def kernel(x, w_in, pool_w, pool_scale, attn_sinks, w_out, norm_pre, norm_post, loss_target, m_w_in, m_pool_w, m_pool_scale, m_attn_sinks, m_w_out, m_norm_pre, m_norm_post, v_w_in, v_pool_w, v_pool_scale, v_attn_sinks, v_w_out, v_norm_pre, v_norm_post):
    cx, cy, cc = _place()
    chip_arr = jnp.reshape(2 * cx + cy, (1,)).astype(jnp.int32)
    place_arr = jnp.stack([4 * cx + 2 * cy + cc, cc]).astype(jnp.int32)
    t = lambda a: jnp.transpose(a, (0, 2, 1))
    w_in_t = t(w_in)
    xs, target = x[0], loss_target[0]
    pool_w_b = pool_w.astype(BF16)
    tables = _attention_tables()
    scale3 = pool_scale.reshape(DEPTH, 1, D_POOL)
    pre3 = norm_pre.reshape(DEPTH, 1, D)
    post3 = norm_post.reshape(DEPTH, 1, D)

    (wi0,) = _place_cast("place_w_in0", w_in_t, chip_arr, 288, [0])
    first = _gather_start("gather_start_first", [wi0], halved=(0,))
    (wi1,) = _place_cast("place_w_in1", w_in_t, chip_arr, 288, [1], deps=(first[3],))
    wo = _place_cast("place_w_out", w_out, chip_arr, 256, [0, 1], deps=(first[3],))
    rest = _gather_start("gather_start_rest", [wi1, wo[0], wo[1]], halved=(0,))
    send, recv, bufs = [first[k] + rest[k] for k in range(3)]
    order = {(0, "in"): 0, (1, "in"): 1, (0, "out"): 2, (1, "out"): 3}

    saved = []
    packed = [_pack_misc(pool_scale, attn_sinks, norm_pre, norm_post),
              _pack_misc(m_pool_scale, m_attn_sinks, m_norm_pre, m_norm_post),
              _pack_misc(v_pool_scale, v_attn_sinks, v_norm_pre, v_norm_post)]
    after = (first[3], rest[3], pool_w_b, *tables, scale3, pre3, post3, *packed)
    below = None
    for l in range(DEPTH):
        k = order[l, "in"]
        w_in_l = _forward_halves(f"forward_w_in{l}", _gather_wait(f"gather_wait_in{l}", bufs[k], send[k], recv[k], after,
                                                                 halved=True))
        if below is None:
            pu, pg, q, kv, ag = _fwd_in(l, xs, pre3, w_in_l)
        else:
            y, xs, pu, pg, q, kv, ag = _fwd_in(l, xs, pre3, w_in_l, below)
            saved[l - 1][7] = y
        cat = _fwd_mix(l, pu, pg, q, kv, ag, pool_w_b, scale3, attn_sinks, tables)
        k = order[l, "out"]
        w_out_l = _gather_wait(f"gather_wait_out{l}", bufs[k], send[k], recv[k], (cat,))
        saved.append([xs, pu, pg, q, kv, ag, cat, None, w_in_l, w_out_l])
        below, after = (cat, w_out_l, post3), (w_out_l,)

    x_in, pu, pg, q, kv, ag, cat, y, w_in_l, w_out_l = saved[1]
    dcat, dw_out1, dw_out1_b, dg_post1, loss, xs = _bwd_out(1, cat, w_out_l, post3, place_arr, x=x_in, target=target)
    ex1_out = _exchange_start("exchange_start_out1", [dw_out1_b])
    dproj, dpw, dsc1, dsink1 = _bwd_mix(1, pu, pg, q, kv, ag, dcat, pool_w_b, scale3, attn_sinks, tables,
                                        deps=(ex1_out[4],))
    dx, dg_pre1, dw_in1, dw_in1_b = _bwd_in_dx(1, dproj, w_in_l, x_in, pre3, xs, dw_place=place_arr)
    ex1_in = _exchange_start("exchange_start_in1", [dw_in1_b])

    x_in, pu, pg, q, kv, ag, cat, y, w_in_l, w_out_l = saved[0]
    dcat, dw_out0, dw_out0_b, dg_post0 = _bwd_out(0, cat, w_out_l, post3, place_arr, dxn=dx, y=y, deps=(ex1_in[4],))
    ex0_out = _exchange_start("exchange_start_out0", [dw_out0_b])
    dproj, dpw, dsc0, dsink0 = _bwd_mix(0, pu, pg, q, kv, ag, dcat, pool_w_b, scale3, attn_sinks, tables,
                                        deps=(ex0_out[4],), dpw_dest=dpw)
    dw_in0, dw_in0_b = _bwd_in_dw(0, dproj, x_in, pre3, place_arr)
    ex0_in = _exchange_start("exchange_start_in0", [dw_in0_b])

    grad_x, dg_pre0 = _bwd_in_dx(0, dproj, w_in_l, x_in, pre3, dx, deps=(ex0_in[4],))
    small = [dpw.reshape(DEPTH * 4 * 128, 128),
             jnp.concatenate([dsc0, dsc1, dg_pre0, dg_pre1, dg_post0, dg_post1, dsink0, dsink1, loss], axis=0)]
    ex_small = _exchange_start("exchange_start_small", small)
    (recv_out1,) = _exchange_wait("exchange_wait_out1", ex1_out, ex_small[4])
    (recv_in1,) = _exchange_wait("exchange_wait_in1", ex1_in, recv_out1)
    g_in, g_out = _sum_pieces("sum_pieces_1", [dw_in1, dw_out1], [recv_in1, recv_out1], place_arr, 1)
    (recv_out0,) = _exchange_wait("exchange_wait_out0", ex0_out, g_out)
    (g_out,) = _sum_pieces("sum_pieces_out0", [dw_out0], [recv_out0], place_arr, 0, dests=[g_out])
    g_in, g_out = _share("share_a", [g_in, g_out], [(0, 1), (1, 0), (1, 1)])
    m_in_t, v_in_t = t(m_w_in), t(v_w_in)
    d_out, nm_out, nv_out, grad_w_out = _adamw("adamw_w_out", w_out, g_out, m_w_out, v_w_out, 256)
    upd_in = _adamw("adamw_w_in1", w_in_t, g_in, m_in_t, v_in_t, 288, first=1, count=1, deps=(d_out,))

    (recv_in0,) = _exchange_wait("exchange_wait_in0", ex0_in, upd_in[0])
    recv_small = _exchange_wait("exchange_wait_small", ex_small, recv_in0)
    (g_in,) = _sum_pieces("sum_pieces_in0", [dw_in0], [recv_in0], place_arr, 0, dests=[g_in])
    g_in, g_pw, g_misc = _share("share_b", [g_in], [(0, 0)], _sum_small(small, recv_small, place_arr))
    d_in, nm_in, nv_in, grad_w_in_t = _adamw("adamw_w_in0", w_in_t, g_in, m_in_t, v_in_t, 288, first=0, count=1,
                                             dests=upd_in)
    flat = lambda a: a.reshape(DEPTH * 4 * 128, 128)
    small_out = _adamw_small(packed[0], g_misc, packed[1], packed[2],
                             (flat(pool_w), g_pw, flat(m_pool_w), flat(v_pool_w)))
    (g_sc, g_sk, g_pre, g_post, d_sc, d_sk, d_pre, d_post,
     m_sc, m_sk, m_pre, m_post, v_sc, v_sk, v_pre, v_post, loss_sum) = small_out[:17]
    g_pw, d_pw, m_pw, v_pw = [a.reshape(pool_w.shape) for a in small_out[17:]]
    return (loss_sum[0, 0], grad_x[None], t(grad_w_in_t), g_pw, g_sc, g_sk, grad_w_out, g_pre, g_post,
            t(d_in), d_pw, d_sc, d_sk, d_out, d_pre, d_post,
            t(nm_in), m_pw, m_sc, m_sk, nm_out, m_pre, m_post,
            t(nv_in), v_pw, v_sc, v_sk, nv_out, v_pre, v_post)
```

```python
import jax
import jax.numpy as jnp
from jax import lax
from jax.experimental import pallas as pl
from jax.experimental.pallas import tpu as pltpu

F32 = jnp.float32
BF16 = jnp.bfloat16

S = 2048
D = 1024
DEPTH = 2
D_POOL = 512
POOL_WINDOWS = (2, 4, 8, 16)
N_HEADS = 8
D_IN = 2304
N_SHARDS = 4
W_IN_SHARD = D_IN // N_SHARDS
W_OUT_SHARD = D // N_SHARDS
BLK = 128
NB = S // BLK
HALO = 16
PAD = 8
EPS = 1e-6
NEG_INF = -1e30
C_PU, C_PG, C_Q, C_K, C_V, C_AG = 0, 512, 1024, 1536, 1664, 1792

ADAM_LR = 0.001
ADAM_B1 = 0.9
ADAM_B2 = 0.999
ADAM_EPS = 1e-08
ADAM_WD = 0.01
ADAM_STEP = 10

TM = 512
VMEM_LIMIT = 56 * 1024 * 1024

NT = (((1,), (1,)), ((), ()))
TN = (((0,), (0,)), ((), ()))

MESH = pl.DeviceIdType.MESH
ANY = pl.BlockSpec(memory_space=pl.ANY)

MISC_SCALE, MISC_PRE, MISC_POST, MISC_SINKS, MISC_LOSS = 0, 8, 24, 40, 56
MISC_ROWS = 64


def _params(sem=("arbitrary",)):
    return pltpu.CompilerParams(dimension_semantics=sem, vmem_limit_bytes=VMEM_LIMIT)


def _sigmoid(v):
    return 1.0 / (1.0 + jnp.exp(-v))


def _rows8(v):
    r, c = v.shape
    return v.reshape(r // 8, 8, c).sum(axis=0)


def _layer(l, *shape):
    zeros = (0,) * len(shape)
    return pl.BlockSpec((None,) + shape, lambda i: (l,) + zeros)


def _whole(shape):
    zeros = (0,) * len(shape)
    return pl.BlockSpec(shape, lambda i: zeros, pipeline_mode=pl.Buffered(1))


def _fwd_in(l, x, g_pre, w_in_t, below=None):
    fused = below is not None

    def body(x_ref, g_ref, w_ref, *rest):
        if fused:
            cat_ref, wo_ref, gp_ref, y_ref, xn_ref = rest[:5]
            y = jnp.dot(cat_ref[...], wo_ref[...], preferred_element_type=F32)
            y_ref[...] = y
            xt = x_ref[...] + y * lax.rsqrt(jnp.mean(y * y, axis=-1, keepdims=True) + EPS) * gp_ref[...]
            xn_ref[...] = xt
        else:
            xt = x_ref[...]
        pu_ref, pg_ref, q_ref, kv_ref, ag_ref = rest[-5:]
        r = lax.rsqrt(jnp.mean(xt * xt, axis=-1, keepdims=True) + EPS)
        h = (xt * r * g_ref[...]).astype(BF16)

        def proj(lo, hi):
            return lax.dot_general(h, w_ref[lo:hi, :], NT, preferred_element_type=F32)

        pu_ref[...] = proj(C_PU, C_PG)
        pg_ref[...] = proj(C_PG, C_Q)
        q_ref[...] = proj(C_Q, C_K).astype(BF16)
        kv_ref[...] = proj(C_K, C_AG).astype(BF16)
        ag_ref[...] = proj(C_AG, D_IN)

    row = lambda w: pl.BlockSpec((TM, w), lambda i: (i, 0))
    act = jax.ShapeDtypeStruct((S, D), F32)
    return pl.pallas_call(
        body, name="fwd_out_in" if fused else "fwd_in", grid=(S // TM,),
        in_specs=[row(D), _layer(l, 1, D), _whole((D_IN, D))]
        + ([row(D), _whole((D, D)), _layer(l - 1, 1, D)] if fused else []),
        out_specs=[row(D)] * (2 * fused) + [row(512), row(512), row(512), row(256), row(512)],
        out_shape=[act] * (2 * fused)
        + [jax.ShapeDtypeStruct((S, 512), F32), jax.ShapeDtypeStruct((S, 512), F32),
           jax.ShapeDtypeStruct((S, 512), BF16), jax.ShapeDtypeStruct((S, 256), BF16),
           jax.ShapeDtypeStruct((S, 512), F32)],
        compiler_params=_params(),
    )(x, g_pre, w_in_t, *(below if fused else ()))


LOG2E = 1.4426950408889634
SCORE_SCALE = 0.125 * LOG2E


def _attention_tables():
    qi = jnp.arange(BLK)[:, None]
    kj = jnp.arange(BLK)[None, :]
    dist = ((qi - kj) % BLK).astype(F32)
    slopes = jnp.exp2(-jnp.arange(1, N_HEADS + 1, dtype=F32))
    bias = -(slopes * LOG2E)[:, None, None] * dist[None]
    first = jnp.where(kj > qi, NEG_INF, bias)
    return jnp.stack([first, bias]), (kj <= qi).astype(BF16)


def _own_block_mask():
    return lax.broadcasted_iota(jnp.int32, (BLK, BLK), 1) <= lax.broadcasted_iota(jnp.int32, (BLK, BLK), 0)


def _merge(full, own):
    return jnp.where(own, full[:, BLK:], full[:, :BLK])


def _spread(v, tri):
    own = v * tri
    return jnp.concatenate([v - own, own], axis=1)


def _head_variants(cur, prev):
    both = jnp.concatenate([prev, cur], axis=0).astype(F32)
    swapped = pltpu.roll(both, 64, axis=1)
    low = lax.broadcasted_iota(jnp.int32, both.shape, 1) < 64
    zero = jnp.zeros_like(both)
    return ((jnp.where(low, both, zero).astype(BF16), jnp.where(low, zero, swapped).astype(BF16)),
            (jnp.where(low, swapped, zero).astype(BF16), jnp.where(low, zero, both).astype(BF16)))


def _head_of(hkv, t, half):
    return hkv * 4 + 2 * t + half


def _rows(v, t):
    return v[t * BLK:(t + 1) * BLK]


def _stack_tiles(ref, hkv, offset=0):
    lo = offset + 2 * hkv * 128
    return jnp.concatenate([ref[:, lo:lo + 128], ref[:, lo + 128:lo + 256]], axis=0)


def _scores(q2, k_var, own):
    s = {}
    for hkv in range(2):
        for half in range(2):
            full = lax.dot_general(q2[hkv], k_var[hkv][half], NT, preferred_element_type=F32)
            for t in range(2):
                s[hkv, t, half] = _merge(_rows(full, t), own)
    return s


def _softmax(s, bias, sink):
    s = s * SCORE_SCALE + bias
    sink2 = sink * LOG2E
    m = jnp.maximum(jnp.max(s, axis=-1, keepdims=True), sink2)
    p = jnp.exp2(s - m)
    e_sink = jnp.exp2(sink2 - m)
    inv = 1.0 / (jnp.sum(p, axis=-1, keepdims=True) + e_sink)
    return p * inv, e_sink * inv


def _spread_pair(v, hkv, half, tri):
    return jnp.concatenate([_spread(v[hkv, t, half].astype(BF16), tri) for t in range(2)], axis=0)


POOL_ROWS = PAD + HALO + BLK


def _window_sums(src_ref, tmp_refs, trailing):
    lo, hi = (PAD, POOL_ROWS) if trailing else (0, HALO + BLK)
    cur = src_ref
    for level in range(len(POOL_WINDOWS)):
        lanes = slice(level * 128, 512)
        shift = -(1 << level) if trailing else (1 << level)
        dst = tmp_refs[level % 2]
        dst[lo:hi, lanes] = cur[lo:hi, lanes] + cur[lo + shift:hi + shift, lanes]
        cur = dst


def _pool_block(ext_ref, tmp_refs, i, g, w):
    lanes = slice(g * 128, (g + 1) * 128)
    rows = slice(PAD + HALO, POOL_ROWS)
    t = (i * BLK + lax.broadcasted_iota(jnp.int32, (BLK, 1), 0)).astype(F32)
    inv = 1.0 / jnp.minimum(t + 1.0, float(w))
    return tmp_refs[g % 2][rows, lanes] * inv - ext_ref[rows, lanes], inv


def _fwd_mix(l, pu, pg, q, kv, ag, pool_w, pool_scale, sinks, tables):
    bias, tri = tables

    def body(pu_ref, pup_ref, pg_ref, q_ref, kv_ref, kvp_ref, ag_ref, pw_ref, sc_ref, sink_ref, bias_ref, tri_ref,
             cat_ref, ext_ref, *tmp_refs):
        i = pl.program_id(0)

        @pl.when(i == 0)
        def _():
            for ref in (ext_ref, *tmp_refs):
                ref[0:PAD, :] = jnp.zeros((PAD, 512), F32)

        ext_ref[PAD:PAD + HALO, :] = jnp.where(i > 0, pup_ref[...], 0.0)
        ext_ref[PAD + HALO:POOL_ROWS, :] = pu_ref[...]
        _window_sums(ext_ref, tmp_refs, True)
        for g, w in enumerate(POOL_WINDOWS):
            lanes = slice(g * 128, (g + 1) * 128)
            pooled, _ = _pool_block(ext_ref, tmp_refs, i, g, w)
            mixed = jnp.dot(pooled.astype(BF16), pw_ref[g], preferred_element_type=F32)
            gate = pg_ref[:, lanes]
            cat_ref[:, lanes] = (mixed * sc_ref[:, lanes] * (gate * _sigmoid(gate))).astype(BF16)

        own = _own_block_mask()
        tri = tri_ref[...]
        k_var = _head_variants(kv_ref[:, 0:128], kvp_ref[:, 0:128])
        v_var = _head_variants(kv_ref[:, 128:256], kvp_ref[:, 128:256])
        s = _scores([_stack_tiles(q_ref, hkv) for hkv in range(2)], k_var, own)
        p = {}
        for (hkv, t, half), s_head in s.items():
            head = _head_of(hkv, t, half)
            p[hkv, t, half], _ = _softmax(s_head, bias_ref[head], sink_ref[l, head])
        for hkv in range(2):
            o2 = jnp.zeros((2 * BLK, 128), F32)
            for half in range(2):
                o2 = o2 + jnp.dot(_spread_pair(p, hkv, half, tri), v_var[hkv][half], preferred_element_type=F32)
            for t in range(2):
                lo = (2 * hkv + t) * 128
                gate = ag_ref[:, lo:lo + 128]
                cat_ref[:, D_POOL + lo:D_POOL + lo + 128] = (_rows(o2, t) * (gate * _sigmoid(gate))).astype(BF16)

    blk = lambda w: pl.BlockSpec((BLK, w), lambda i: (i, 0))
    prev = lambda w: pl.BlockSpec((BLK, w), lambda i: (jnp.maximum(i - 1, 0), 0))
    halo = pl.BlockSpec((HALO, 512), lambda i: (jnp.maximum(i * (BLK // HALO) - 1, 0), 0))
    return pl.pallas_call(
        body, name="fwd_mix", grid=(NB,),
        in_specs=[blk(512), halo, blk(512), blk(512), blk(256), prev(256), blk(512),
                  _layer(l, 4, 128, 128), _layer(l, 1, 512), pl.BlockSpec(memory_space=pltpu.SMEM),
                  pl.BlockSpec((None, N_HEADS, BLK, BLK), lambda i: (jnp.minimum(i, 1), 0, 0, 0)), _whole((BLK, BLK))],
        out_specs=blk(D),
        out_shape=jax.ShapeDtypeStruct((S, D), BF16),
        scratch_shapes=[pltpu.VMEM((POOL_ROWS, 512), F32)] * 3,
        compiler_params=_params(),
    )(pu, pu, pg, q, kv, kv, ag, pool_w, pool_scale, sinks, bias, tri)


def _store_lane_rows(ref, acc):
    total = jnp.sum(acc, axis=0, keepdims=True)
    for k in range(ref.shape[0]):
        ref[k:k + 1, :] = total[:, k * 128:(k + 1) * 128]


def _own_piece(dw_ref, place_ref):
    p = dw_ref.shape[0] // 8
    return dw_ref[pl.ds(pl.multiple_of(place_ref[0] * p, 8), p), :]


def _bwd_out(l, cat, w_out, g_post, place_arr, dxn=None, y=None, x=None, target=None, deps=()):
    last = target is not None
    n_steps = S // TM

    def body(a_ref, b_ref, g_ref, cat_ref, w_ref, place_ref, *rest):
        dcat_ref, own_ref, dwb_ref, dg_ref = rest[len(deps):len(deps) + 4]
        rest = rest[len(deps) + 4:]
        acc_ref, dw_ref = rest[-2:]
        step = pl.program_id(0)

        @pl.when(step == 0)
        def _():
            dw_ref[...] = jnp.zeros_like(dw_ref)
            acc_ref[...] = jnp.zeros_like(acc_ref)

        cat = cat_ref[...]
        g = g_ref[...]
        y = jnp.dot(cat, w_ref[...], preferred_element_type=F32) if last else b_ref[...]
        r = lax.rsqrt(jnp.mean(y * y, axis=-1, keepdims=True) + EPS)
        if last:
            loss_ref, dx_ref, loss_acc_ref = rest[:3]
            err = a_ref[...] + y * r * g - b_ref[...]

            @pl.when(step == 0)
            def _():
                loss_acc_ref[...] = jnp.zeros_like(loss_acc_ref)

            loss_acc_ref[...] += _rows8(err * err)
            dz = err * (1.0 / D)
            dx_ref[...] = dz
        else:
            dz = a_ref[...]
        a = dz * g
        dy = r * a - y * (r * r * r) * jnp.mean(a * y, axis=-1, keepdims=True)
        acc_ref[...] += _rows8(dz * (y * r))
        dyb = dy.astype(BF16)
        dcat_ref[...] = lax.dot_general(dyb, w_ref[...], NT, preferred_element_type=F32)
        dw_ref[...] += lax.dot_general(cat, dyb, TN, preferred_element_type=F32)

        @pl.when(step == n_steps - 1)
        def _():
            _store_lane_rows(dg_ref, acc_ref[...])
            dwb_ref[...] = dw_ref[...].astype(BF16)
            own_ref[...] = _own_piece(dw_ref, place_ref)
            if last:
                loss_ref[...] = jnp.full((8, 128), (0.5 / D) * jnp.sum(loss_acc_ref[...]), F32)

    row = lambda: pl.BlockSpec((TM, D), lambda i: (i, 0))
    full = _whole
    return pl.pallas_call(
        body, name="out_loss_bwd" if last else "bwd_out", grid=(n_steps,),
        in_specs=[row(), row(), _layer(l, 1, D), row(), full((D, D)), pl.BlockSpec(memory_space=pltpu.SMEM)]
        + [ANY] * len(deps),
        out_specs=[row(), full((D // 8, D)), full((D, D)), full((8, 128))] + ([full((8, 128)), row()] if last else []),
        out_shape=[jax.ShapeDtypeStruct((S, D), F32), jax.ShapeDtypeStruct((D // 8, D), F32),
                   jax.ShapeDtypeStruct((D, D), BF16), jax.ShapeDtypeStruct((8, 128), F32)]
        + ([jax.ShapeDtypeStruct((8, 128), F32), jax.ShapeDtypeStruct((S, D), F32)] if last else []),
        scratch_shapes=([pltpu.VMEM((8, D), F32)] if last else []) + [pltpu.VMEM((8, D), F32), pltpu.VMEM((D, D), F32)],
        compiler_params=_params(),
    )(*((x, target) if last else (dxn, y)), g_post, cat, w_out, place_arr, *deps)


def _bwd_mix(l, pu, pg, q, kv, ag, dcat, pool_w, pool_scale, sinks, tables, deps=(), dpw_dest=None):
    bias, tri = tables
    deps = tuple(deps) + (() if dpw_dest is None else (dpw_dest,))

    def body(pu_ref, pup_ref, pg_ref, q_ref, kv_ref, kvp_ref, ag_ref, dcat_ref, pw_ref, sc_ref, sink_ref, bias_ref,
             tri_ref, *rest):
        dproj_ref, dpw_ref, dsc_ref, dsink_ref, ext_ref, dext_ref, tmp_a, tmp_b, dkv_ref = rest[len(deps):]
        tmp_refs = (tmp_a, tmp_b)
        step = pl.program_id(0)
        i = NB - 1 - step

        @pl.when(step == 0)
        def _():
            dpw_ref[...] = jnp.zeros_like(dpw_ref)
            dsc_ref[...] = jnp.zeros_like(dsc_ref)
            dsink_ref[...] = jnp.zeros_like(dsink_ref)
            for ref in (ext_ref, tmp_a, tmp_b):
                ref[0:PAD, :] = jnp.zeros((PAD, 512), F32)
            dext_ref[BLK:POOL_ROWS, :] = jnp.zeros((HALO + PAD, 512), F32)
            dkv_ref[...] = jnp.zeros_like(dkv_ref)

        ext_ref[PAD:PAD + HALO, :] = jnp.where(i > 0, pup_ref[...], 0.0)
        ext_ref[PAD + HALO:POOL_ROWS, :] = pu_ref[...]
        _window_sums(ext_ref, tmp_refs, True)
        dpooled = []
        for g, w in enumerate(POOL_WINDOWS):
            lanes = slice(g * 128, (g + 1) * 128)
            pooled, inv = _pool_block(ext_ref, tmp_refs, i, g, w)
            pooled_b = pooled.astype(BF16)
            mixed = jnp.dot(pooled_b, pw_ref[g], preferred_element_type=F32)
            scale = sc_ref[:, lanes]
            gate = pg_ref[:, lanes]
            sg = _sigmoid(gate)
            dpo = dcat_ref[:, lanes]
            dproj_ref[:, C_PG + g * 128:C_PG + (g + 1) * 128] = (
                dpo * (mixed * scale) * (sg * (1.0 + gate * (1.0 - sg)))).astype(BF16)
            dms = dpo * (gate * sg)
            dsc_ref[g:g + 1, :] += jnp.sum(dms * mixed, axis=0, keepdims=True)
            dmixed = (dms * scale).astype(BF16)
            dpw_ref[g] += lax.dot_general(pooled_b, dmixed, TN, preferred_element_type=F32)
            dpooled.append(lax.dot_general(dmixed, pw_ref[g], NT, preferred_element_type=F32))
            dext_ref[0:BLK, lanes] = dpooled[g] * inv
        _window_sums(dext_ref, tmp_refs, False)
        for g in range(len(POOL_WINDOWS)):
            lanes = slice(g * 128, (g + 1) * 128)
            dproj_ref[:, C_PU + g * 128:C_PU + (g + 1) * 128] = (tmp_refs[g % 2][0:BLK, lanes] - dpooled[g]).astype(BF16)
        dext_ref[BLK:BLK + HALO, :] = dext_ref[0:HALO, :]

        own = _own_block_mask()
        tri = tri_ref[...]
        k_var = _head_variants(kv_ref[:, 0:128], kvp_ref[:, 0:128])
        v_var = _head_variants(kv_ref[:, 128:256], kvp_ref[:, 128:256])
        q2 = [_stack_tiles(q_ref, hkv) for hkv in range(2)]
        s = _scores(q2, k_var, own)
        p, p_sink = {}, {}
        for key, s_head in s.items():
            head = _head_of(*key)
            p[key], p_sink[key] = _softmax(s_head, bias_ref[head], sink_ref[l, head])

        do2, p_b, dp = [], {}, {}
        for hkv in range(2):
            gate = _stack_tiles(ag_ref, hkv)
            sg = _sigmoid(gate)
            dca = _stack_tiles(dcat_ref, hkv, D_POOL)
            do2.append((dca * (gate * sg)).astype(BF16))
            o2 = jnp.zeros((2 * BLK, 128), F32)
            for half in range(2):
                p_b[hkv, half] = _spread_pair(p, hkv, half, tri)
                o2 = o2 + jnp.dot(p_b[hkv, half], v_var[hkv][half], preferred_element_type=F32)
                full = lax.dot_general(do2[hkv], v_var[hkv][half], NT, preferred_element_type=F32)
                for t in range(2):
                    dp[hkv, t, half] = _merge(_rows(full, t), own)
            dag = dca * o2 * (sg * (1.0 + gate * (1.0 - sg)))
            for t in range(2):
                lo = C_AG + (2 * hkv + t) * 128
                dproj_ref[:, lo:lo + 128] = _rows(dag, t).astype(BF16)

        ds = {}
        for key in p:
            delta = jnp.sum(p[key] * dp[key], axis=-1, keepdims=True)
            ds[key] = p[key] * (dp[key] - delta)
            head = _head_of(*key)
            dsink_ref[0:1, :] += jnp.where(lax.broadcasted_iota(jnp.int32, (1, 128), 1) == head,
                                           -jnp.sum(p_sink[key] * delta, axis=0, keepdims=True), 0.0)

        dk_acc = [[None, None], [None, None]]
        dv_acc = [[None, None], [None, None]]
        for hkv in range(2):
            dq2 = jnp.zeros((2 * BLK, 128), F32)
            for half in range(2):
                ds_b = _spread_pair(ds, hkv, half, tri)
                dq2 = dq2 + jnp.dot(ds_b, k_var[hkv][half], preferred_element_type=F32)
                dk_acc[hkv][half] = lax.dot_general(ds_b, q2[hkv], TN, preferred_element_type=F32)
                dv_acc[hkv][half] = lax.dot_general(p_b[hkv, half], do2[hkv], TN, preferred_element_type=F32)
            for t in range(2):
                lo = C_Q + (2 * hkv + t) * 128
                dproj_ref[:, lo:lo + 128] = (_rows(dq2, t) * 0.125).astype(BF16)

        low = lax.broadcasted_iota(jnp.int32, (2 * BLK, 128), 1) < 64

        def gather_heads(acc):
            return jnp.where(low, acc[0][0] + pltpu.roll(acc[0][1], 64, axis=1),
                             pltpu.roll(acc[1][0], 64, axis=1) + acc[1][1])

        dk = gather_heads(dk_acc) * 0.125
        dv = gather_heads(dv_acc)
        dproj_ref[:, C_K:C_V] = (dk[BLK:, :] + dkv_ref[:, 0:128]).astype(BF16)
        dproj_ref[:, C_V:C_AG] = (dv[BLK:, :] + dkv_ref[:, 128:256]).astype(BF16)
        dkv_ref[:, 0:128] = dk[:BLK, :]
        dkv_ref[:, 128:256] = dv[:BLK, :]

    rev = lambda w: pl.BlockSpec((BLK, w), lambda s: (NB - 1 - s, 0))
    prev = lambda w: pl.BlockSpec((BLK, w), lambda s: (jnp.maximum(NB - 2 - s, 0), 0))
    halo = pl.BlockSpec((HALO, 512), lambda s: (jnp.maximum((NB - 1 - s) * (BLK // HALO) - 1, 0), 0))
    return pl.pallas_call(
        body, name="bwd_mix", grid=(NB,),
        in_specs=[rev(512), halo, rev(512), rev(512), rev(256), prev(256), rev(512), rev(D),
                  _layer(l, 4, 128, 128), _layer(l, 1, 512), pl.BlockSpec(memory_space=pltpu.SMEM),
                  pl.BlockSpec((None, N_HEADS, BLK, BLK), lambda s: (jnp.minimum(NB - 1 - s, 1), 0, 0, 0)),
                  _whole((BLK, BLK))] + [ANY] * len(deps),
        out_specs=[rev(D_IN), _layer(l, 4, 128, 128),
                   pl.BlockSpec((4, 128), lambda s: (0, 0)), pl.BlockSpec((8, 128), lambda s: (0, 0))],
        out_shape=[jax.ShapeDtypeStruct((S, D_IN), BF16), jax.ShapeDtypeStruct((DEPTH, 4, 128, 128), F32),
                   jax.ShapeDtypeStruct((4, 128), F32), jax.ShapeDtypeStruct((8, 128), F32)],
        input_output_aliases={} if dpw_dest is None else {12 + len(deps): 1},
        scratch_shapes=[pltpu.VMEM((POOL_ROWS, 512), F32)] * 4 + [pltpu.VMEM((BLK, 256), F32)],
        compiler_params=_params(),
    )(pu, pu, pg, q, kv, kv, ag, dcat, pool_w, pool_scale, sinks, bias, tri, *deps)


def _bwd_in_dw(l, dproj, x, g_pre, place_arr, deps=()):
    n_steps = S // TM

    def body(dp_ref, x_ref, g_ref, place_ref, *rest):
        own_ref, dwb_ref, dw_ref = rest[len(deps):]
        step = pl.program_id(0)

        @pl.when(step == 0)
        def _():
            dw_ref[...] = jnp.zeros_like(dw_ref)

        xt = x_ref[...]
        r = lax.rsqrt(jnp.mean(xt * xt, axis=-1, keepdims=True) + EPS)
        h = (xt * r * g_ref[...]).astype(BF16)
        dw_ref[...] += lax.dot_general(dp_ref[...], h, TN, preferred_element_type=F32)

        @pl.when(step == n_steps - 1)
        def _():
            dwb_ref[...] = dw_ref[...].astype(BF16)
            own_ref[...] = _own_piece(dw_ref, place_ref)

    row = lambda w: pl.BlockSpec((TM, w), lambda i: (i, 0))
    full = _whole
    return pl.pallas_call(
        body, name="bwd_in_dw", grid=(n_steps,),
        in_specs=[row(D_IN), row(D), _layer(l, 1, D), pl.BlockSpec(memory_space=pltpu.SMEM)] + [ANY] * len(deps),
        out_specs=[full((D_IN // 8, D)), full((D_IN, D))],
        out_shape=[jax.ShapeDtypeStruct((D_IN // 8, D), F32), jax.ShapeDtypeStruct((D_IN, D), BF16)],
        scratch_shapes=[pltpu.VMEM((D_IN, D), F32)],
        compiler_params=_params(),
    )(dproj, x, g_pre, place_arr, *deps)


def _bwd_in_dx(l, dproj, w_in_t, x, g_pre, dres, deps=(), dw_place=None):
    n_steps = S // TM
    with_dw = dw_place is not None

    def body(dp_ref, w_ref, x_ref, g_ref, dres_ref, *rest):
        place_ref = rest[0] if with_dw else None
        rest = rest[with_dw + len(deps):]
        if with_dw:
            dx_ref, dg_ref, own_ref, dwb_ref, acc_ref, dw_ref = rest
        else:
            dx_ref, dg_ref, acc_ref = rest
        step = pl.program_id(0)

        @pl.when(step == 0)
        def _():
            acc_ref[...] = jnp.zeros_like(acc_ref)
            if with_dw:
                dw_ref[...] = jnp.zeros_like(dw_ref)

        g = g_ref[...]
        halves = [slice(k * (TM // 2), (k + 1) * (TM // 2)) for k in range(2)]
        dh = [jnp.dot(dp_ref[rows, :], w_ref[...], preferred_element_type=F32) for rows in halves]
        h = []
        for rows, dh_k in zip(halves, dh):
            xt = x_ref[rows, :]
            r = lax.rsqrt(jnp.mean(xt * xt, axis=-1, keepdims=True) + EPS)
            xn = xt * r
            acc_ref[...] += _rows8(dh_k * xn)
            a = dh_k * g
            dx_ref[rows, :] = dres_ref[rows, :] + (
                r * a - xt * (r * r * r) * jnp.mean(a * xt, axis=-1, keepdims=True))
            h.append((xn * g).astype(BF16))
        if with_dw:
            dw_ref[...] += lax.dot_general(dp_ref[...], jnp.concatenate(h, axis=0), TN, preferred_element_type=F32)

        @pl.when(step == n_steps - 1)
        def _():
            _store_lane_rows(dg_ref, acc_ref[...])
            if with_dw:
                dwb_ref[...] = dw_ref[...].astype(BF16)
                own_ref[...] = _own_piece(dw_ref, place_ref)

    row = lambda w: pl.BlockSpec((TM, w), lambda i: (i, 0))
    full = _whole
    dw_specs = [full((D_IN // 8, D)), full((D_IN, D))] if with_dw else []
    dw_shapes = [jax.ShapeDtypeStruct((D_IN // 8, D), F32), jax.ShapeDtypeStruct((D_IN, D), BF16)] if with_dw else []
    return pl.pallas_call(
        body, name="bwd_in" if with_dw else "bwd_in_dx", grid=(n_steps,),
        in_specs=[row(D_IN), full((D_IN, D)), row(D), _layer(l, 1, D), row(D)]
        + [pl.BlockSpec(memory_space=pltpu.SMEM)] * with_dw + [ANY] * len(deps),
        out_specs=[row(D), full((8, 128))] + dw_specs,
        out_shape=[jax.ShapeDtypeStruct((S, D), F32), jax.ShapeDtypeStruct((8, 128), F32)] + dw_shapes,
        scratch_shapes=[pltpu.VMEM((8, D), F32)] + [pltpu.VMEM((D_IN, D), F32)] * with_dw,
        compiler_params=_params(),
    )(dproj, w_in_t, x, g_pre, dres, *((dw_place,) if with_dw else ()), *deps)


HBM =pl.BlockSpec(memory_space=pltpu.HBM)
SEM = pl.BlockSpec(memory_space=pltpu.SEMAPHORE)
SPLIT_COPY = pltpu.CompilerParams(has_side_effects=pltpu.SideEffectType.DATAFLOW_SIDE_EFFECTING)


def _in_hbm(a):
    return pltpu.with_memory_space_constraint(a, pltpu.HBM)

def _place():
    return lax.axis_index("x"), lax.axis_index("y"), lax.axis_index("c")


def _other_chips(x, y):
    return [(1 - x, y), (x, 1 - y), (1 - x, 1 - y)]


def _peer(x, y, c, m):
    return (x ^ (m >> 2), y ^ ((m >> 1) & 1), c ^ (m & 1))


def _place_cast(name, src, chip_arr, tile, layers, deps=()):
    _, n, cols = src.shape
    steps = n // tile
    k = len(layers)

    def body(chip_ref, *refs):
        for s_ref, o_ref in zip(refs[:k], refs[k + len(deps):]):
            o_ref[...] = s_ref[...].astype(BF16)

    def layer_spec(l):
        return pl.BlockSpec((None, tile, cols), lambda i, chip: (l, i, 0))

    return pl.pallas_call(
        body, name=name,
        grid_spec=pltpu.PrefetchScalarGridSpec(
            num_scalar_prefetch=1, grid=(steps,),
            in_specs=[layer_spec(l) for l in layers] + [ANY] * len(deps),
            out_specs=[pl.BlockSpec((tile, cols), lambda i, chip: (chip[0] * steps + i, 0))] * k),
        out_shape=[jax.ShapeDtypeStruct((N_SHARDS * n, cols), BF16)] * k,
        compiler_params=_params(),
    )(chip_arr, *[src] * k, *deps)


def _chip_rows(ref, chip, half=None):
    n = ref.shape[0] // N_SHARDS
    if half is None:
        return ref.at[pl.ds(pl.multiple_of(chip * n, 16), n), :]
    return ref.at[pl.ds(pl.multiple_of(chip * n + half * (n // 2), 16), n // 2), :]


def _gather_start(name, bufs, halved):
    n = len(bufs)

    def body(*refs):
        ins, send, recv, token = refs[:n], refs[n:2 * n], refs[2 * n:3 * n], refs[-1]
        x, y, c = _place()
        for a, buf in enumerate(ins):
            own = _chip_rows(buf, 2 * x + y, c if a in halved else None)
            for j, chip in enumerate(_other_chips(x, y)):
                pltpu.make_async_remote_copy(src_ref=own, dst_ref=own, send_sem=send[a].at[j], recv_sem=recv[a].at[j],
                                             device_id=(*chip, c), device_id_type=MESH).start()
        token[...] = jnp.zeros_like(token)

    outs = pl.pallas_call(
        body, name=name, in_specs=[HBM] * n,
        out_specs=[SEM] * (2 * n) + [HBM] * n + [pl.BlockSpec(memory_space=pltpu.VMEM)],
        out_shape=[pltpu.SemaphoreType.DMA((3,))] * (2 * n) + [pltpu.HBM(b.shape, b.dtype) for b in bufs]
        + [jax.ShapeDtypeStruct((8, 128), F32)],
        input_output_aliases={a: 2 * n + a for a in range(n)},
        compiler_params=SPLIT_COPY,
    )(*[_in_hbm(b) for b in bufs])
    return outs[:n], outs[n:2 * n], outs[2 * n:3 * n], outs[-1]


def _gather_wait(name, buf, send_sem, recv_sem, after, halved=False):
    def body(buf_ref, send_ref, recv_ref, *rest):
        x, y, c = _place()
        half = c if halved else None
        own = _chip_rows(buf_ref, 2 * x + y, half)
        for j, chip in enumerate(_other_chips(x, y)):
            copy = pltpu.make_async_remote_copy(src_ref=own, dst_ref=_chip_rows(buf_ref, 2 * chip[0] + chip[1], half),
                                                send_sem=send_ref.at[j], recv_sem=recv_ref.at[j],
                                                device_id=(*chip, c), device_id_type=MESH)
            copy.wait_send()
            copy.wait_recv()

    return pl.pallas_call(
        body, name=name, in_specs=[HBM, SEM, SEM] + [ANY] * len(after), out_specs=HBM,
        out_shape=pltpu.HBM(buf.shape, buf.dtype), input_output_aliases={0: 0}, compiler_params=SPLIT_COPY,
    )(buf, send_sem, recv_sem, *after)


def _forward_halves(name, buf):
    def body(in_ref, out_ref, send_sems, recv_sems):
        x, y, c = _place()

        def copy(j, chip, half):
            rows = 2 * chip[0] + chip[1]
            return pltpu.make_async_remote_copy(
                src_ref=_chip_rows(in_ref, rows, half), dst_ref=_chip_rows(out_ref, rows, half), send_sem=send_sems.at[j],
                recv_sem=recv_sems.at[j], device_id=(x, y, 1 - c), device_id_type=MESH)

        chips = _other_chips(x, y)
        for j, chip in enumerate(chips):
            copy(j, chip, c).start()
        for j, chip in enumerate(chips):
            copy(j, chip, c).wait_send()
            copy(j, chip, 1 - c).wait_recv()

    return pl.pallas_call(
        body, name=name, in_specs=[ANY], out_specs=ANY, out_shape=jax.ShapeDtypeStruct(buf.shape, buf.dtype),
        input_output_aliases={0: 0},
        scratch_shapes=[pltpu.SemaphoreType.DMA((3,))] * 2,
    )(buf)


def _piece_rows(ref, k):
    p = ref.shape[0] // 8
    return ref.at[pl.ds(pl.multiple_of(k * p, 32 // jnp.dtype(ref.dtype).itemsize), p), :]


def _exchange_start(name, arrays):
    n = len(arrays)
    zones = [lax.empty((7, a.shape[0] // 8, a.shape[1]), a.dtype) for a in arrays]

    def body(*refs):
        srcs, lands = refs[:n], refs[n:2 * n]
        send, recv, token = refs[2 * n:3 * n], refs[3 * n:4 * n], refs[-1]
        x, y, c = _place()
        for a, (src, land) in enumerate(zip(srcs, lands)):
            for m in range(1, 8):
                px, py, pc = _peer(x, y, c, m)
                pltpu.make_async_remote_copy(
                    src_ref=_piece_rows(src, 4 * px + 2 * py + pc), dst_ref=land.at[m - 1], send_sem=send[a].at[m - 1],
                    recv_sem=recv[a].at[m - 1], device_id=(px, py, pc), device_id_type=MESH).start()
        token[...] = jnp.zeros_like(token)

    outs = pl.pallas_call(
        body, name=name, in_specs=[HBM] * (2 * n),
        out_specs=[SEM] * (2 * n) + [HBM] * (2 * n) + [pl.BlockSpec(memory_space=pltpu.VMEM)],
        out_shape=[pltpu.SemaphoreType.DMA((7,))] * (2 * n) + [pltpu.HBM(a.shape, a.dtype) for a in arrays + zones]
        + [jax.ShapeDtypeStruct((8, 128), F32)],
        input_output_aliases={a: 2 * n + a for a in range(2 * n)},
        compiler_params=SPLIT_COPY,
    )(*[_in_hbm(a) for a in arrays + zones])
    return outs[:n], outs[n:2 * n], outs[2 * n:3 * n], outs[3 * n:4 * n], outs[-1]


def _exchange_wait(name, started, after):
    send_sems, recv_sems, arrays, zones, _ = started
    n = len(arrays)

    def body(*refs):
        srcs, lands = refs[:n], refs[n:2 * n]
        send, recv = refs[2 * n:3 * n], refs[3 * n:4 * n]
        x, y, c = _place()
        for a, (src, land) in enumerate(zip(srcs, lands)):
            for m in range(1, 8):
                px, py, pc = _peer(x, y, c, m)
                copy = pltpu.make_async_remote_copy(
                    src_ref=_piece_rows(src, 4 * px + 2 * py + pc), dst_ref=land.at[m - 1], send_sem=send[a].at[m - 1],
                    recv_sem=recv[a].at[m - 1], device_id=(px, py, pc), device_id_type=MESH)
                copy.wait_send()
                copy.wait_recv()

    outs = pl.pallas_call(
        body, name=name, in_specs=[HBM] * (2 * n) + [SEM] * (2 * n) + [ANY], out_specs=[HBM] * (2 * n),
        out_shape=[pltpu.HBM(a.shape, a.dtype) for a in list(arrays) + list(zones)],
        input_output_aliases={a: a for a in range(2 * n)}, compiler_params=SPLIT_COPY,
    )(*arrays, *zones, *send_sems, *recv_sems, after)
    return outs[n:]


def _sum_pieces(name, owns, recvs, place_arr, layer, dests=None):
    n = len(owns)
    steps = 2

    def body(place_ref, *refs):
        for o_ref, r_ref, out_ref in zip(refs[:n], refs[n:2 * n], refs[-n:]):
            total = o_ref[...]
            for m in range(7):
                total = total + r_ref[m].astype(F32)
            out_ref[...] = total

    tiles = [o.shape[0] // steps for o in owns]
    return pl.pallas_call(
        body, name=name,
        grid_spec=pltpu.PrefetchScalarGridSpec(
            num_scalar_prefetch=1, grid=(steps,),
            in_specs=[pl.BlockSpec((t, o.shape[1]), lambda i, place: (i, 0)) for o, t in zip(owns, tiles)]
            + [pl.BlockSpec((7, t, o.shape[1]), lambda i, place: (0, i, 0)) for o, t in zip(owns, tiles)]
            + ([] if dests is None else [ANY] * n),
            out_specs=[pl.BlockSpec((None, t, o.shape[1]), lambda i, place: (layer, place[1] * steps + i, 0))
                       for o, t in zip(owns, tiles)]),
        out_shape=[jax.ShapeDtypeStruct((DEPTH, 2 * o.shape[0], o.shape[1]), F32) for o in owns],
        input_output_aliases={} if dests is None else {1 + 2 * n + k: k for k in range(n)},
        compiler_params=_params(),
    )(place_arr, *owns, *recvs, *(() if dests is None else dests))


def _sum_small(partials, recvs, place_arr):
    n = len(partials)

    def body(place_ref, *refs):
        for o_ref, r_ref, out_ref in zip(refs[:n], refs[n:2 * n], refs[2 * n:]):
            total = o_ref[...]
            for m in range(7):
                total = total + r_ref[m]
            out_ref[...] = total

    piece = lambda a: pl.BlockSpec((a.shape[0] // 8, a.shape[1]), lambda i, place: (place[0], 0))
    return pl.pallas_call(
        body, name="sum_small",
        grid_spec=pltpu.PrefetchScalarGridSpec(
            num_scalar_prefetch=1, grid=(1,),
            in_specs=[piece(a) for a in partials] + [pl.BlockSpec(r.shape, lambda i, place: (0, 0, 0)) for r in recvs],
            out_specs=[piece(a) for a in partials]),
        out_shape=[jax.ShapeDtypeStruct(a.shape, F32) for a in partials],
        compiler_params=_params(),
    )(place_arr, *partials, *recvs)


def _share(name, bufs, parts, gathered=()):
    n, n_g = len(bufs), len(gathered)
    total = n + n_g

    def body(*refs):
        ins, outs = refs[:total], refs[total:2 * total]
        send_sems, recv_sems, send_g, recv_g = refs[2 * total:]
        x, y, c = _place()

        def half(ref, l, which):
            p = ref.shape[1] // 2
            return ref.at[l, pl.ds(pl.multiple_of(which * p, 8), p), :]

        def swap(k, which):
            a, l = parts[k]
            return pltpu.make_async_remote_copy(
                src_ref=half(ins[a], l, which), dst_ref=half(outs[a], l, which), send_sem=send_sems.at[k],
                recv_sem=recv_sems.at[k], device_id=(x, y, 1 - c), device_id_type=MESH)

        def spread(a, m, sender):
            k = 4 * sender[0] + 2 * sender[1] + sender[2]
            return pltpu.make_async_remote_copy(
                src_ref=_piece_rows(ins[n + a], k), dst_ref=_piece_rows(outs[n + a], k), send_sem=send_g.at[7 * a + m - 1],
                recv_sem=recv_g.at[7 * a + m - 1], device_id=_peer(x, y, c, m), device_id_type=MESH)

        for k in range(len(parts)):
            swap(k, c).start()
        for a in range(n_g):
            for m in range(1, 8):
                spread(a, m, (x, y, c)).start()
        for k in range(len(parts)):
            swap(k, c).wait_send()
            swap(k, 1 - c).wait_recv()
        for a in range(n_g):
            for m in range(1, 8):
                spread(a, m, (x, y, c)).wait_send()
                spread(a, m, _peer(x, y, c, m)).wait_recv()

    arrays = list(bufs) + list(gathered)
    return pl.pallas_call(
        body, name=name, in_specs=[ANY] * total, out_specs=[ANY] * total,
        out_shape=[jax.ShapeDtypeStruct(b.shape, F32) for b in arrays],
        input_output_aliases={a: a for a in range(total)},
        scratch_shapes=[pltpu.SemaphoreType.DMA((max(len(parts), 1),))] * 2
        + [pltpu.SemaphoreType.DMA((max(7 * n_g, 1),))] * 2,
    )(*arrays)


def _adamw_math(w, g, m, v):
    nm = ADAM_B1 * m + (1.0 - ADAM_B1) * g
    nv = ADAM_B2 * v + (1.0 - ADAM_B2) * (g * g)
    m_hat = nm / (1.0 - ADAM_B1 ** ADAM_STEP)
    v_hat = nv / (1.0 - ADAM_B2 ** ADAM_STEP)
    return -ADAM_LR * (m_hat / (jnp.sqrt(v_hat) + ADAM_EPS) + ADAM_WD * w), nm, nv


def _adamw(name, w, g, m, v, rows_per_step, first=0, count=None, dests=None, deps=()):
    layers, rows, cols = w.shape
    count = layers if count is None else count

    def body(w_ref, g_ref, m_ref, v_ref, *rest):
        d_ref, nm_ref, nv_ref, g_out_ref = rest[-4:]
        d_ref[...], nm_ref[...], nv_ref[...] = _adamw_math(w_ref[...], g_ref[...], m_ref[...], v_ref[...])
        g_out_ref[...] = g_ref[...]

    spec = pl.BlockSpec((1, rows_per_step, cols), lambda l, i: (first + l, i, 0))
    shape = jax.ShapeDtypeStruct(w.shape, F32)
    dests = () if dests is None else tuple(dests)
    return pl.pallas_call(
        body, name=name, grid=(count, rows // rows_per_step),
        in_specs=[spec] * 4 + [ANY] * (len(dests) + len(deps)), out_specs=[spec] * 4, out_shape=[shape] * 4,
        input_output_aliases={4 + k: k for k in range(len(dests))},
        compiler_params=_params(("arbitrary", "arbitrary")),
    )(w, g, m, v, *dests, *deps)


def _pack_misc(pool_scale, sinks, norm_pre, norm_post):
    sink_rows = jnp.zeros((DEPTH, 8, 128), F32).at[:, 0, 0:N_HEADS].set(sinks).reshape(2 * 8, 128)
    return jnp.concatenate([pool_scale.reshape(8, 128), norm_pre.reshape(16, 128), norm_post.reshape(16, 128),
                            sink_rows, jnp.zeros((8, 128), F32)], axis=0)


def _adamw_small(w, g, m, v, pool):
    def body(w_ref, g_ref, m_ref, v_ref, pw_ref, pg_ref, pm_ref, pv_ref, *rest):
        outs, pool_outs, (d_ref, nm_ref, nv_ref) = rest[:17], rest[17:21], rest[21:]
        pool_outs[0][...] = pg_ref[...]
        pool_outs[1][...], pool_outs[2][...], pool_outs[3][...] = _adamw_math(
            pw_ref[...], pg_ref[...], pm_ref[...], pv_ref[...])
        d_ref[...], nm_ref[...], nv_ref[...] = _adamw_math(w_ref[...], g_ref[...], m_ref[...], v_ref[...])
        for k, src in enumerate([g_ref, d_ref, nm_ref, nv_ref]):
            scale, sinks, pre, post = outs[4 * k:4 * k + 4]
            for l in range(DEPTH):
                for j in range(4):
                    scale[l:l + 1, j * 128:(j + 1) * 128] = src[MISC_SCALE + 4 * l + j:MISC_SCALE + 4 * l + j + 1, :]
                for j in range(8):
                    pre[l:l + 1, j * 128:(j + 1) * 128] = src[MISC_PRE + 8 * l + j:MISC_PRE + 8 * l + j + 1, :]
                    post[l:l + 1, j * 128:(j + 1) * 128] = src[MISC_POST + 8 * l + j:MISC_POST + 8 * l + j + 1, :]
                sinks[l:l + 1, :] = src[MISC_SINKS + 8 * l:MISC_SINKS + 8 * l + 1, 0:N_HEADS]
        outs[16][...] = g_ref[MISC_LOSS:MISC_LOSS + 1, 0:1]

    vmem = pl.BlockSpec(memory_space=pltpu.VMEM)
    shapes = [(DEPTH, D_POOL), (DEPTH, N_HEADS), (DEPTH, D), (DEPTH, D)] * 4 + [(1, 1)]
    shapes += [pool[0].shape] * 4
    return pl.pallas_call(
        body, name="adamw_small", in_specs=[vmem] * 8, out_specs=[vmem] * 21,
        out_shape=[jax.ShapeDtypeStruct(s, F32) for s in shapes],
        scratch_shapes=[pltpu.VMEM((MISC_ROWS, 128), F32)] * 3,
    )(w, g, m, v, *pool)


def kernel(x, w_in, pool_w, pool_scale, attn_sinks, w_out, norm_pre, norm_post, loss_target, m_w_in, m_pool_w, m_pool_scale, m_attn_sinks, m_w_out, m_norm_pre, m_norm_post, v_w_in, v_pool_w, v_pool_scale, v_attn_sinks, v_w_out, v_norm_pre, v_norm_post):
    cx, cy, cc = _place()
    chip_arr = jnp.reshape(2 * cx + cy, (1,)).astype(jnp.int32)
    place_arr = jnp.stack([4 * cx + 2 * cy + cc, cc]).astype(jnp.int32)
    t = lambda a: jnp.transpose(a, (0, 2, 1))
    w_in_t = t(w_in)
    xs, target = x[0], loss_target[0]
    pool_w_b = pool_w.astype(BF16)
    tables = _attention_tables()
    scale3 = pool_scale.reshape(DEPTH, 1, D_POOL)
    pre3 = norm_pre.reshape(DEPTH, 1, D)
    post3 = norm_post.reshape(DEPTH, 1, D)

    (wi0,) = _place_cast("place_w_in0", w_in_t, chip_arr, 288, [0])
    first = _gather_start("gather_start_first", [wi0], halved=(0,))
    (wi1,) = _place_cast("place_w_in1", w_in_t, chip_arr, 288, [1], deps=(first[3],))
    wo = _place_cast("place_w_out", w_out, chip_arr, 256, [0, 1], deps=(first[3],))
    rest = _gather_start("gather_start_rest", [wi1, wo[0], wo[1]], halved=(0,))
    send, recv, bufs = [first[k] + rest[k] for k in range(3)]
    order = {(0, "in"): 0, (1, "in"): 1, (0, "out"): 2, (1, "out"): 3}

    saved = []
    packed = [_pack_misc(pool_scale, attn_sinks, norm_pre, norm_post),
              _pack_misc(m_pool_scale, m_attn_sinks, m_norm_pre, m_norm_post),
              _pack_misc(v_pool_scale, v_attn_sinks, v_norm_pre, v_norm_post)]
    after = (first[3], rest[3], pool_w_b, *tables, scale3, pre3, post3, *packed)
    below = None
    for l in range(DEPTH):
        k = order[l, "in"]
        w_in_l = _forward_halves(f"forward_w_in{l}", _gather_wait(f"gather_wait_in{l}", bufs[k], send[k], recv[k], after,
                                                                 halved=True))
        if below is None:
            pu, pg, q, kv, ag = _fwd_in(l, xs, pre3, w_in_l)
        else:
            y, xs, pu, pg, q, kv, ag = _fwd_in(l, xs, pre3, w_in_l, below)
            saved[l - 1][7] = y
        cat = _fwd_mix(l, pu, pg, q, kv, ag, pool_w_b, scale3, attn_sinks, tables)
        k = order[l, "out"]
        w_out_l = _gather_wait(f"gather_wait_out{l}", bufs[k], send[k], recv[k], (cat,))
        saved.append([xs, pu, pg, q, kv, ag, cat, None, w_in_l, w_out_l])
        below, after = (cat, w_out_l, post3), (w_out_l,)

    x_in, pu, pg, q, kv, ag, cat, y, w_in_l, w_out_l = saved[1]
    dcat, dw_out1, dw_out1_b, dg_post1, loss, xs = _bwd_out(1, cat, w_out_l, post3, place_arr, x=x_in, target=target)
    ex1_out = _exchange_start("exchange_start_out1", [dw_out1_b])
    dproj, dpw, dsc1, dsink1 = _bwd_mix(1, pu, pg, q, kv, ag, dcat, pool_w_b, scale3, attn_sinks, tables,
                                        deps=(ex1_out[4],))
    dx, dg_pre1, dw_in1, dw_in1_b = _bwd_in_dx(1, dproj, w_in_l, x_in, pre3, xs, dw_place=place_arr)
    ex1_in = _exchange_start("exchange_start_in1", [dw_in1_b])

    x_in, pu, pg, q, kv, ag, cat, y, w_in_l, w_out_l = saved[0]
    dcat, dw_out0, dw_out0_b, dg_post0 = _bwd_out(0, cat, w_out_l, post3, place_arr, dxn=dx, y=y, deps=(ex1_in[4],))
    ex0_out = _exchange_start("exchange_start_out0", [dw_out0_b])
    dproj, dpw, dsc0, dsink0 = _bwd_mix(0, pu, pg, q, kv, ag, dcat, pool_w_b, scale3, attn_sinks, tables,
                                        deps=(ex0_out[4],), dpw_dest=dpw)
    dw_in0, dw_in0_b = _bwd_in_dw(0, dproj, x_in, pre3, place_arr)
    ex0_in = _exchange_start("exchange_start_in0", [dw_in0_b])

    grad_x, dg_pre0 = _bwd_in_dx(0, dproj, w_in_l, x_in, pre3, dx, deps=(ex0_in[4],))
    small = [dpw.reshape(DEPTH * 4 * 128, 128),
             jnp.concatenate([dsc0, dsc1, dg_pre0, dg_pre1, dg_post0, dg_post1, dsink0, dsink1, loss], axis=0)]
    ex_small = _exchange_start("exchange_start_small", small)
    (recv_out1,) = _exchange_wait("exchange_wait_out1", ex1_out, ex_small[4])
    (recv_in1,) = _exchange_wait("exchange_wait_in1", ex1_in, recv_out1)
    g_in, g_out = _sum_pieces("sum_pieces_1", [dw_in1, dw_out1], [recv_in1, recv_out1], place_arr, 1)
    (recv_out0,) = _exchange_wait("exchange_wait_out0", ex0_out, g_out)
    (g_out,) = _sum_pieces("sum_pieces_out0", [dw_out0], [recv_out0], place_arr, 0, dests=[g_out])
    g_in, g_out = _share("share_a", [g_in, g_out], [(0, 1), (1, 0), (1, 1)])
    m_in_t, v_in_t = t(m_w_in), t(v_w_in)
    d_out, nm_out, nv_out, grad_w_out = _adamw("adamw_w_out", w_out, g_out, m_w_out, v_w_out, 256)
    upd_in = _adamw("adamw_w_in1", w_in_t, g_in, m_in_t, v_in_t, 288, first=1, count=1, deps=(d_out,))

    (recv_in0,) = _exchange_wait("exchange_wait_in0", ex0_in, upd_in[0])
    recv_small = _exchange_wait("exchange_wait_small", ex_small, recv_in0)
    (g_in,) = _sum_pieces("sum_pieces_in0", [dw_in0], [recv_in0], place_arr, 0, dests=[g_in])
    g_in, g_pw, g_misc = _share("share_b", [g_in], [(0, 0)], _sum_small(small, recv_small, place_arr))
    d_in, nm_in, nv_in, grad_w_in_t = _adamw("adamw_w_in0", w_in_t, g_in, m_in_t, v_in_t, 288, first=0, count=1,
                                             dests=upd_in)
    flat = lambda a: a.reshape(DEPTH * 4 * 128, 128)
    small_out = _adamw_small(packed[0], g_misc, packed[1], packed[2],
                             (flat(pool_w), g_pw, flat(m_pool_w), flat(v_pool_w)))
    (g_sc, g_sk, g_pre, g_post, d_sc, d_sk, d_pre, d_post,
     m_sc, m_sk, m_pre, m_post, v_sc, v_sk, v_pre, v_post, loss_sum) = small_out[:17]
    g_pw, d_pw, m_pw, v_pw = [a.reshape(pool_w.shape) for a in small_out[17:]]
    return (loss_sum[0, 0], grad_x[None], t(grad_w_in_t), g_pw, g_sc, g_sk, grad_w_out, g_pre, g_post,
            t(d_in), d_pw, d_sc, d_sk, d_out, d_pre, d_post,
            t(nm_in), m_pw, m_sc, m_sk, nm_out, m_pre, m_post,
            t(nv_in), v_pw, v_sc, v_sk, nv_out, v_pre, v_post)
```

```python
import jax
import jax.numpy as jnp
from jax import lax
from jax.experimental import pallas as pl
from jax.experimental.pallas import tpu as pltpu

F32 = jnp.float32
BF16 = jnp.bfloat16

S = 2048
D = 1024
DEPTH = 2
D_POOL = 512
POOL_WINDOWS = (2, 4, 8, 16)
N_HEADS = 8
D_IN = 2304
N_SHARDS = 4
W_IN_SHARD = D_IN // N_SHARDS
W_OUT_SHARD = D // N_SHARDS
BLK = 128
NB = S // BLK
HALO = 16
PAD = 8
EPS = 1e-6
NEG_INF = -1e30
C_PU, C_PG, C_Q, C_K, C_V, C_AG = 0, 512, 1024, 1536, 1664, 1792

ADAM_LR = 0.001
ADAM_B1 = 0.9
ADAM_B2 = 0.999
ADAM_EPS = 1e-08
ADAM_WD = 0.01
ADAM_STEP = 10

TM = 512
VMEM_LIMIT = 56 * 1024 * 1024

NT = (((1,), (1,)), ((), ()))
TN = (((0,), (0,)), ((), ()))

MESH = pl.DeviceIdType.MESH
ANY = pl.BlockSpec(memory_space=pl.ANY)

MISC_SCALE, MISC_PRE, MISC_POST, MISC_SINKS, MISC_LOSS = 0, 8, 24, 40, 56
MISC_ROWS = 64


def _params(sem=("arbitrary",)):
    return pltpu.CompilerParams(dimension_semantics=sem, vmem_limit_bytes=VMEM_LIMIT)


def _sigmoid(v):
    return 1.0 / (1.0 + jnp.exp(-v))


def _rows8(v):
    r, c = v.shape
    return v.reshape(r // 8, 8, c).sum(axis=0)


def _layer(l, *shape):
    zeros = (0,) * len(shape)
    return pl.BlockSpec((None,) + shape, lambda i: (l,) + zeros)


def _whole(shape):
    zeros = (0,) * len(shape)
    return pl.BlockSpec(shape, lambda i: zeros, pipeline_mode=pl.Buffered(1))


def _fwd_in(l, x, g_pre, w_in_t, below=None):
    fused = below is not None

    def body(x_ref, g_ref, w_ref, *rest):
        if fused:
            cat_ref, wo_ref, gp_ref, y_ref, xn_ref = rest[:5]
            y = jnp.dot(cat_ref[...], wo_ref[...], preferred_element_type=F32)
            y_ref[...] = y
            xt = x_ref[...] + y * lax.rsqrt(jnp.mean(y * y, axis=-1, keepdims=True) + EPS) * gp_ref[...]
            xn_ref[...] = xt
        else:
            xt = x_ref[...]
        pu_ref, pg_ref, q_ref, kv_ref, ag_ref = rest[-5:]
        r = lax.rsqrt(jnp.mean(xt * xt, axis=-1, keepdims=True) + EPS)
        h = (xt * r * g_ref[...]).astype(BF16)

        def proj(lo, hi):
            return lax.dot_general(h, w_ref[lo:hi, :], NT, preferred_element_type=F32)

        pu_ref[...] = proj(C_PU, C_PG)
        pg_ref[...] = proj(C_PG, C_Q)
        q_ref[...] = proj(C_Q, C_K).astype(BF16)
        kv_ref[...] = proj(C_K, C_AG).astype(BF16)
        ag_ref[...] = proj(C_AG, D_IN)

    row = lambda w: pl.BlockSpec((TM, w), lambda i: (i, 0))
    act = jax.ShapeDtypeStruct((S, D), F32)
    return pl.pallas_call(
        body, name="fwd_out_in" if fused else "fwd_in", grid=(S // TM,),
        in_specs=[row(D), _layer(l, 1, D), _whole((D_IN, D))]
        + ([row(D), _whole((D, D)), _layer(l - 1, 1, D)] if fused else []),
        out_specs=[row(D)] * (2 * fused) + [row(512), row(512), row(512), row(256), row(512)],
        out_shape=[act] * (2 * fused)
        + [jax.ShapeDtypeStruct((S, 512), F32), jax.ShapeDtypeStruct((S, 512), F32),
           jax.ShapeDtypeStruct((S, 512), BF16), jax.ShapeDtypeStruct((S, 256), BF16),
           jax.ShapeDtypeStruct((S, 512), F32)],
        compiler_params=_params(),
    )(x, g_pre, w_in_t, *(below if fused else ()))


LOG2E = 1.4426950408889634
SCORE_SCALE = 0.125 * LOG2E


def _attention_tables():
    qi = jnp.arange(BLK)[:, None]
    kj = jnp.arange(BLK)[None, :]
    dist = ((qi - kj) % BLK).astype(F32)
    slopes = jnp.exp2(-jnp.arange(1, N_HEADS + 1, dtype=F32))
    bias = -(slopes * LOG2E)[:, None, None] * dist[None]
    first = jnp.where(kj > qi, NEG_INF, bias)
    return jnp.stack([first, bias]), (kj <= qi).astype(BF16)


def _own_block_mask():
    return lax.broadcasted_iota(jnp.int32, (BLK, BLK), 1) <= lax.broadcasted_iota(jnp.int32, (BLK, BLK), 0)


def _merge(full, own):
    return jnp.where(own, full[:, BLK:], full[:, :BLK])


def _spread(v, tri):
    own = v * tri
    return jnp.concatenate([v - own, own], axis=1)


def _head_variants(cur, prev):
    both = jnp.concatenate([prev, cur], axis=0).astype(F32)
    swapped = pltpu.roll(both, 64, axis=1)
    low = lax.broadcasted_iota(jnp.int32, both.shape, 1) < 64
    zero = jnp.zeros_like(both)
    return ((jnp.where(low, both, zero).astype(BF16), jnp.where(low, zero, swapped).astype(BF16)),
            (jnp.where(low, swapped, zero).astype(BF16), jnp.where(low, zero, both).astype(BF16)))


def _head_of(hkv, t, half):
    return hkv * 4 + 2 * t + half


def _rows(v, t):
    return v[t * BLK:(t + 1) * BLK]


def _stack_tiles(ref, hkv, offset=0):
    lo = offset + 2 * hkv * 128
    return jnp.concatenate([ref[:, lo:lo + 128], ref[:, lo + 128:lo + 256]], axis=0)


def _scores(q2, k_var, own):
    s = {}
    for hkv in range(2):
        for half in range(2):
            full = lax.dot_general(q2[hkv], k_var[hkv][half], NT, preferred_element_type=F32)
            for t in range(2):
                s[hkv, t, half] = _merge(_rows(full, t), own)
    return s


def _softmax(s, bias, sink):
    s = s * SCORE_SCALE + bias
    sink2 = sink * LOG2E
    m = jnp.maximum(jnp.max(s, axis=-1, keepdims=True), sink2)
    p = jnp.exp2(s - m)
    e_sink = jnp.exp2(sink2 - m)
    inv = 1.0 / (jnp.sum(p, axis=-1, keepdims=True) + e_sink)
    return p * inv, e_sink * inv


def _spread_pair(v, hkv, half, tri):
    return jnp.concatenate([_spread(v[hkv, t, half].astype(BF16), tri) for t in range(2)], axis=0)


POOL_ROWS = PAD + HALO + BLK


def _window_sums(src_ref, tmp_refs, trailing):
    lo, hi = (PAD, POOL_ROWS) if trailing else (0, HALO + BLK)
    cur = src_ref
    for level in range(len(POOL_WINDOWS)):
        lanes = slice(level * 128, 512)
        shift = -(1 << level) if trailing else (1 << level)
        dst = tmp_refs[level % 2]
        dst[lo:hi, lanes] = cur[lo:hi, lanes] + cur[lo + shift:hi + shift, lanes]
        cur = dst


def _pool_block(ext_ref, tmp_refs, i, g, w):
    lanes = slice(g * 128, (g + 1) * 128)
    rows = slice(PAD + HALO, POOL_ROWS)
    t = (i * BLK + lax.broadcasted_iota(jnp.int32, (BLK, 1), 0)).astype(F32)
    inv = 1.0 / jnp.minimum(t + 1.0, float(w))
    return tmp_refs[g % 2][rows, lanes] * inv - ext_ref[rows, lanes], inv


def _fwd_mix(l, pu, pg, q, kv, ag, pool_w, pool_scale, sinks, tables):
    bias, tri = tables

    def body(pu_ref, pup_ref, pg_ref, q_ref, kv_ref, kvp_ref, ag_ref, pw_ref, sc_ref, sink_ref, bias_ref, tri_ref,
             cat_ref, p_ref, ps_ref, o_ref, ext_ref, *tmp_refs):
        i = pl.program_id(0)

        @pl.when(i == 0)
        def _():
            for ref in (ext_ref, *tmp_refs):
                ref[0:PAD, :] = jnp.zeros((PAD, 512), F32)

        ext_ref[PAD:PAD + HALO, :] = jnp.where(i > 0, pup_ref[...], 0.0)
        ext_ref[PAD + HALO:POOL_ROWS, :] = pu_ref[...]
        _window_sums(ext_ref, tmp_refs, True)
        for g, w in enumerate(POOL_WINDOWS):
            lanes = slice(g * 128, (g + 1) * 128)
            pooled, _ = _pool_block(ext_ref, tmp_refs, i, g, w)
            mixed = jnp.dot(pooled.astype(BF16), pw_ref[g], preferred_element_type=F32)
            gate = pg_ref[:, lanes]
            cat_ref[:, lanes] = (mixed * sc_ref[:, lanes] * (gate * _sigmoid(gate))).astype(BF16)

        own = _own_block_mask()
        tri = tri_ref[...]
        k_var = _head_variants(kv_ref[:, 0:128], kvp_ref[:, 0:128])
        v_var = _head_variants(kv_ref[:, 128:256], kvp_ref[:, 128:256])
        s = _scores([_stack_tiles(q_ref, hkv) for hkv in range(2)], k_var, own)
        p = {}
        ps_ref[...] = jnp.zeros_like(ps_ref)
        for (hkv, t, half), s_head in s.items():
            head = _head_of(hkv, t, half)
            p[hkv, t, half], p_sink = _softmax(s_head, bias_ref[head], sink_ref[l, head])
            p_ref[head] = p[hkv, t, half]
            ps_ref[:, head:head + 1] = p_sink
        for hkv in range(2):
            o2 = jnp.zeros((2 * BLK, 128), F32)
            for half in range(2):
                o2 = o2 + jnp.dot(_spread_pair(p, hkv, half, tri), v_var[hkv][half], preferred_element_type=F32)
            for t in range(2):
                lo = (2 * hkv + t) * 128
                gate = ag_ref[:, lo:lo + 128]
                o_ref[:, lo:lo + 128] = _rows(o2, t)
                cat_ref[:, D_POOL + lo:D_POOL + lo + 128] = (_rows(o2, t) * (gate * _sigmoid(gate))).astype(BF16)

    blk = lambda w: pl.BlockSpec((BLK, w), lambda i: (i, 0))
    prev = lambda w: pl.BlockSpec((BLK, w), lambda i: (jnp.maximum(i - 1, 0), 0))
    halo = pl.BlockSpec((HALO, 512), lambda i: (jnp.maximum(i * (BLK // HALO) - 1, 0), 0))
    return pl.pallas_call(
        body, name="fwd_mix", grid=(NB,),
        in_specs=[blk(512), halo, blk(512), blk(512), blk(256), prev(256), blk(512),
                  _layer(l, 4, 128, 128), _layer(l, 1, 512), pl.BlockSpec(memory_space=pltpu.SMEM),
                  pl.BlockSpec((None, N_HEADS, BLK, BLK), lambda i: (jnp.minimum(i, 1), 0, 0, 0)), _whole((BLK, BLK))],
        out_specs=[blk(D), pl.BlockSpec((N_HEADS, BLK, BLK), lambda i: (0, i, 0)), blk(128), blk(512)],
        out_shape=[jax.ShapeDtypeStruct((S, D), BF16), jax.ShapeDtypeStruct((N_HEADS, S, BLK), F32),
                   jax.ShapeDtypeStruct((S, 128), F32), jax.ShapeDtypeStruct((S, 512), F32)],
        scratch_shapes=[pltpu.VMEM((POOL_ROWS, 512), F32)] * 3,
        compiler_params=_params(),
    )(pu, pu, pg, q, kv, kv, ag, pool_w, pool_scale, sinks, bias, tri)


def _store_lane_rows(ref, acc):
    total = jnp.sum(acc, axis=0, keepdims=True)
    for k in range(ref.shape[0]):
        ref[k:k + 1, :] = total[:, k * 128:(k + 1) * 128]


def _own_piece(dw_ref, place_ref):
    p = dw_ref.shape[0] // 8
    return dw_ref[pl.ds(pl.multiple_of(place_ref[0] * p, 8), p), :]


def _bwd_out(l, cat, w_out, g_post, place_arr, dxn=None, y=None, x=None, target=None, deps=()):
    last = target is not None
    n_steps = S // TM

    def body(a_ref, b_ref, g_ref, cat_ref, w_ref, place_ref, *rest):
        dcat_ref, own_ref, dwb_ref, dg_ref = rest[len(deps):len(deps) + 4]
        rest = rest[len(deps) + 4:]
        acc_ref, dw_ref = rest[-2:]
        step = pl.program_id(0)

        @pl.when(step == 0)
        def _():
            dw_ref[...] = jnp.zeros_like(dw_ref)
            acc_ref[...] = jnp.zeros_like(acc_ref)

        cat = cat_ref[...]
        g = g_ref[...]
        y = jnp.dot(cat, w_ref[...], preferred_element_type=F32) if last else b_ref[...]
        r = lax.rsqrt(jnp.mean(y * y, axis=-1, keepdims=True) + EPS)
        if last:
            loss_ref, dx_ref, loss_acc_ref = rest[:3]
            err = a_ref[...] + y * r * g - b_ref[...]

            @pl.when(step == 0)
            def _():
                loss_acc_ref[...] = jnp.zeros_like(loss_acc_ref)

            loss_acc_ref[...] += _rows8(err * err)
            dz = err * (1.0 / D)
            dx_ref[...] = dz
        else:
            dz = a_ref[...]
        a = dz * g
        dy = r * a - y * (r * r * r) * jnp.mean(a * y, axis=-1, keepdims=True)
        acc_ref[...] += _rows8(dz * (y * r))
        dyb = dy.astype(BF16)
        dcat_ref[...] = lax.dot_general(dyb, w_ref[...], NT, preferred_element_type=F32)
        dw_ref[...] += lax.dot_general(cat, dyb, TN, preferred_element_type=F32)

        @pl.when(step == n_steps - 1)
        def _():
            _store_lane_rows(dg_ref, acc_ref[...])
            dwb_ref[...] = dw_ref[...].astype(BF16)
            own_ref[...] = _own_piece(dw_ref, place_ref)
            if last:
                loss_ref[...] = jnp.full((8, 128), (0.5 / D) * jnp.sum(loss_acc_ref[...]), F32)

    row = lambda: pl.BlockSpec((TM, D), lambda i: (i, 0))
    full = _whole
    return pl.pallas_call(
        body, name="out_loss_bwd" if last else "bwd_out", grid=(n_steps,),
        in_specs=[row(), row(), _layer(l, 1, D), row(), full((D, D)), pl.BlockSpec(memory_space=pltpu.SMEM)]
        + [ANY] * len(deps),
        out_specs=[row(), full((D // 8, D)), full((D, D)), full((8, 128))] + ([full((8, 128)), row()] if last else []),
        out_shape=[jax.ShapeDtypeStruct((S, D), F32), jax.ShapeDtypeStruct((D // 8, D), F32),
                   jax.ShapeDtypeStruct((D, D), BF16), jax.ShapeDtypeStruct((8, 128), F32)]
        + ([jax.ShapeDtypeStruct((8, 128), F32), jax.ShapeDtypeStruct((S, D), F32)] if last else []),
        scratch_shapes=([pltpu.VMEM((8, D), F32)] if last else []) + [pltpu.VMEM((8, D), F32), pltpu.VMEM((D, D), F32)],
        compiler_params=_params(),
    )(*((x, target) if last else (dxn, y)), g_post, cat, w_out, place_arr, *deps)


def _bwd_mix(l, pu, pg, q, kv, ag, dcat, pool_w, pool_scale, probs, tri, deps=(), dpw_dest=None):
    deps = tuple(deps) + (() if dpw_dest is None else (dpw_dest,))

    def body(pu_ref, pup_ref, pg_ref, q_ref, kv_ref, kvp_ref, ag_ref, dcat_ref, pw_ref, sc_ref, p_ref, ps_ref, o_ref,
             tri_ref, *rest):
        dproj_ref, dpw_ref, dsc_ref, dsink_ref, ext_ref, dext_ref, tmp_a, tmp_b, dkv_ref = rest[len(deps):]
        tmp_refs = (tmp_a, tmp_b)
        step = pl.program_id(0)
        i = NB - 1 - step

        @pl.when(step == 0)
        def _():
            dpw_ref[...] = jnp.zeros_like(dpw_ref)
            dsc_ref[...] = jnp.zeros_like(dsc_ref)
            dsink_ref[...] = jnp.zeros_like(dsink_ref)
            for ref in (ext_ref, tmp_a, tmp_b):
                ref[0:PAD, :] = jnp.zeros((PAD, 512), F32)
            dext_ref[BLK:POOL_ROWS, :] = jnp.zeros((HALO + PAD, 512), F32)
            dkv_ref[...] = jnp.zeros_like(dkv_ref)

        ext_ref[PAD:PAD + HALO, :] = jnp.where(i > 0, pup_ref[...], 0.0)
        ext_ref[PAD + HALO:POOL_ROWS, :] = pu_ref[...]
        _window_sums(ext_ref, tmp_refs, True)
        dpooled = []
        for g, w in enumerate(POOL_WINDOWS):
            lanes = slice(g * 128, (g + 1) * 128)
            pooled, inv = _pool_block(ext_ref, tmp_refs, i, g, w)
            pooled_b = pooled.astype(BF16)
            mixed = jnp.dot(pooled_b, pw_ref[g], preferred_element_type=F32)
            scale = sc_ref[:, lanes]
            gate = pg_ref[:, lanes]
            sg = _sigmoid(gate)
            dpo = dcat_ref[:, lanes]
            dproj_ref[:, C_PG + g * 128:C_PG + (g + 1) * 128] = (
                dpo * (mixed * scale) * (sg * (1.0 + gate * (1.0 - sg)))).astype(BF16)
            dms = dpo * (gate * sg)
            dsc_ref[g:g + 1, :] += jnp.sum(dms * mixed, axis=0, keepdims=True)
            dmixed = (dms * scale).astype(BF16)
            dpw_ref[g] += lax.dot_general(pooled_b, dmixed, TN, preferred_element_type=F32)
            dpooled.append(lax.dot_general(dmixed, pw_ref[g], NT, preferred_element_type=F32))
            dext_ref[0:BLK, lanes] = dpooled[g] * inv
        _window_sums(dext_ref, tmp_refs, False)
        for g in range(len(POOL_WINDOWS)):
            lanes = slice(g * 128, (g + 1) * 128)
            dproj_ref[:, C_PU + g * 128:C_PU + (g + 1) * 128] = (tmp_refs[g % 2][0:BLK, lanes] - dpooled[g]).astype(BF16)
        dext_ref[BLK:BLK + HALO, :] = dext_ref[0:HALO, :]

        own = _own_block_mask()
        tri = tri_ref[...]
        k_var = _head_variants(kv_ref[:, 0:128], kvp_ref[:, 0:128])
        v_var = _head_variants(kv_ref[:, 128:256], kvp_ref[:, 128:256])
        q2 = [_stack_tiles(q_ref, hkv) for hkv in range(2)]
        keys = [(hkv, t, half) for hkv in range(2) for half in range(2) for t in range(2)]
        p = {key: p_ref[_head_of(*key)] for key in keys}

        do2, p_b, dp = [], {}, {}
        for hkv in range(2):
            gate = _stack_tiles(ag_ref, hkv)
            sg = _sigmoid(gate)
            dca = _stack_tiles(dcat_ref, hkv, D_POOL)
            do2.append((dca * (gate * sg)).astype(BF16))
            for half in range(2):
                p_b[hkv, half] = _spread_pair(p, hkv, half, tri)
                full = lax.dot_general(do2[hkv], v_var[hkv][half], NT, preferred_element_type=F32)
                for t in range(2):
                    dp[hkv, t, half] = _merge(_rows(full, t), own)
            dag = dca * _stack_tiles(o_ref, hkv) * (sg * (1.0 + gate * (1.0 - sg)))
            for t in range(2):
                lo = C_AG + (2 * hkv + t) * 128
                dproj_ref[:, lo:lo + 128] = _rows(dag, t).astype(BF16)

        ds = {}
        for key in p:
            delta = jnp.sum(p[key] * dp[key], axis=-1, keepdims=True)
            ds[key] = p[key] * (dp[key] - delta)
            head = _head_of(*key)
            dsink_ref[0:1, :] += jnp.where(lax.broadcasted_iota(jnp.int32, (1, 128), 1) == head,
                                           -jnp.sum(ps_ref[:, head:head + 1] * delta, axis=0, keepdims=True), 0.0)

        dk_acc = [[None, None], [None, None]]
        dv_acc = [[None, None], [None, None]]
        for hkv in range(2):
            dq2 = jnp.zeros((2 * BLK, 128), F32)
            for half in range(2):
                ds_b = _spread_pair(ds, hkv, half, tri)
                dq2 = dq2 + jnp.dot(ds_b, k_var[hkv][half], preferred_element_type=F32)
                dk_acc[hkv][half] = lax.dot_general(ds_b, q2[hkv], TN, preferred_element_type=F32)
                dv_acc[hkv][half] = lax.dot_general(p_b[hkv, half], do2[hkv], TN, preferred_element_type=F32)
            for t in range(2):
                lo = C_Q + (2 * hkv + t) * 128
                dproj_ref[:, lo:lo + 128] = (_rows(dq2, t) * 0.125).astype(BF16)

        low = lax.broadcasted_iota(jnp.int32, (2 * BLK, 128), 1) < 64

        def gather_heads(acc):
            return jnp.where(low, acc[0][0] + pltpu.roll(acc[0][1], 64, axis=1),
                             pltpu.roll(acc[1][0], 64, axis=1) + acc[1][1])

        dk = gather_heads(dk_acc) * 0.125
        dv = gather_heads(dv_acc)
        dproj_ref[:, C_K:C_V] = (dk[BLK:, :] + dkv_ref[:, 0:128]).astype(BF16)
        dproj_ref[:, C_V:C_AG] = (dv[BLK:, :] + dkv_ref[:, 128:256]).astype(BF16)
        dkv_ref[:, 0:128] = dk[:BLK, :]
        dkv_ref[:, 128:256] = dv[:BLK, :]

    rev = lambda w: pl.BlockSpec((BLK, w), lambda s: (NB - 1 - s, 0))
    prev = lambda w: pl.BlockSpec((BLK, w), lambda s: (jnp.maximum(NB - 2 - s, 0), 0))
    halo = pl.BlockSpec((HALO, 512), lambda s: (jnp.maximum((NB - 1 - s) * (BLK // HALO) - 1, 0), 0))
    return pl.pallas_call(
        body, name="bwd_mix", grid=(NB,),
        in_specs=[rev(512), halo, rev(512), rev(512), rev(256), prev(256), rev(512), rev(D),
                  _layer(l, 4, 128, 128), _layer(l, 1, 512),
                  pl.BlockSpec((N_HEADS, BLK, BLK), lambda s: (0, NB - 1 - s, 0)), rev(128), rev(512),
                  _whole((BLK, BLK))] + [ANY] * len(deps),
        out_specs=[rev(D_IN), _layer(l, 4, 128, 128),
                   pl.BlockSpec((4, 128), lambda s: (0, 0)), pl.BlockSpec((8, 128), lambda s: (0, 0))],
        out_shape=[jax.ShapeDtypeStruct((S, D_IN), BF16), jax.ShapeDtypeStruct((DEPTH, 4, 128, 128), F32),
                   jax.ShapeDtypeStruct((4, 128), F32), jax.ShapeDtypeStruct((8, 128), F32)],
        input_output_aliases={} if dpw_dest is None else {13 + len(deps): 1},
        scratch_shapes=[pltpu.VMEM((POOL_ROWS, 512), F32)] * 4 + [pltpu.VMEM((BLK, 256), F32)],
        compiler_params=_params(),
    )(pu, pu, pg, q, kv, kv, ag, dcat, pool_w, pool_scale, *probs, tri, *deps)


def _bwd_in_dw(l, dproj, x, g_pre, place_arr, deps=()):
    n_steps = S // TM

    def body(dp_ref, x_ref, g_ref, place_ref, *rest):
        own_ref, dwb_ref, dw_ref = rest[len(deps):]
        step = pl.program_id(0)

        @pl.when(step == 0)
        def _():
            dw_ref[...] = jnp.zeros_like(dw_ref)

        xt = x_ref[...]
        r = lax.rsqrt(jnp.mean(xt * xt, axis=-1, keepdims=True) + EPS)
        h = (xt * r * g_ref[...]).astype(BF16)
        dw_ref[...] += lax.dot_general(dp_ref[...], h, TN, preferred_element_type=F32)

        @pl.when(step == n_steps - 1)
        def _():
            dwb_ref[...] = dw_ref[...].astype(BF16)
            own_ref[...] = _own_piece(dw_ref, place_ref)

    row = lambda w: pl.BlockSpec((TM, w), lambda i: (i, 0))
    full = _whole
    return pl.pallas_call(
        body, name="bwd_in_dw", grid=(n_steps,),
        in_specs=[row(D_IN), row(D), _layer(l, 1, D), pl.BlockSpec(memory_space=pltpu.SMEM)] + [ANY] * len(deps),
        out_specs=[full((D_IN // 8, D)), full((D_IN, D))],
        out_shape=[jax.ShapeDtypeStruct((D_IN // 8, D), F32), jax.ShapeDtypeStruct((D_IN, D), BF16)],
        scratch_shapes=[pltpu.VMEM((D_IN, D), F32)],
        compiler_params=_params(),
    )(dproj, x, g_pre, place_arr, *deps)


def _bwd_in_dx(l, dproj, w_in_t, x, g_pre, dres, deps=(), dw_place=None):
    n_steps = S // TM
    with_dw = dw_place is not None

    def body(dp_ref, w_ref, x_ref, g_ref, dres_ref, *rest):
        place_ref = rest[0] if with_dw else None
        rest = rest[with_dw + len(deps):]
        if with_dw:
            dx_ref, dg_ref, own_ref, dwb_ref, acc_ref, dw_ref = rest
        else:
            dx_ref, dg_ref, acc_ref = rest
        step = pl.program_id(0)

        @pl.when(step == 0)
        def _():
            acc_ref[...] = jnp.zeros_like(acc_ref)
            if with_dw:
                dw_ref[...] = jnp.zeros_like(dw_ref)

        g = g_ref[...]
        halves = [slice(k * (TM // 2), (k + 1) * (TM // 2)) for k in range(2)]
        dh = [jnp.dot(dp_ref[rows, :], w_ref[...], preferred_element_type=F32) for rows in halves]
        h = []
        for rows, dh_k in zip(halves, dh):
            xt = x_ref[rows, :]
            r = lax.rsqrt(jnp.mean(xt * xt, axis=-1, keepdims=True) + EPS)
            xn = xt * r
            acc_ref[...] += _rows8(dh_k * xn)
            a = dh_k * g
            dx_ref[rows, :] = dres_ref[rows, :] + (
                r * a - xt * (r * r * r) * jnp.mean(a * xt, axis=-1, keepdims=True))
            h.append((xn * g).astype(BF16))
        if with_dw:
            dw_ref[...] += lax.dot_general(dp_ref[...], jnp.concatenate(h, axis=0), TN, preferred_element_type=F32)

        @pl.when(step == n_steps - 1)
        def _():
            _store_lane_rows(dg_ref, acc_ref[...])
            if with_dw:
                dwb_ref[...] = dw_ref[...].astype(BF16)
                own_ref[...] = _own_piece(dw_ref, place_ref)

    row = lambda w: pl.BlockSpec((TM, w), lambda i: (i, 0))
    full = _whole
    dw_specs = [full((D_IN // 8, D)), full((D_IN, D))] if with_dw else []
    dw_shapes = [jax.ShapeDtypeStruct((D_IN // 8, D), F32), jax.ShapeDtypeStruct((D_IN, D), BF16)] if with_dw else []
    return pl.pallas_call(
        body, name="bwd_in" if with_dw else "bwd_in_dx", grid=(n_steps,),
        in_specs=[row(D_IN), full((D_IN, D)), row(D), _layer(l, 1, D), row(D)]
        + [pl.BlockSpec(memory_space=pltpu.SMEM)] * with_dw + [ANY] * len(deps),
        out_specs=[row(D), full((8, 128))] + dw_specs,
        out_shape=[jax.ShapeDtypeStruct((S, D), F32), jax.ShapeDtypeStruct((8, 128), F32)] + dw_shapes,
        scratch_shapes=[pltpu.VMEM((8, D), F32)] + [pltpu.VMEM((D_IN, D), F32)] * with_dw,
        compiler_params=_params(),
    )(dproj, w_in_t, x, g_pre, dres, *((dw_place,) if with_dw else ()), *deps)


HBM =pl.BlockSpec(memory_space=pltpu.HBM)
SEM = pl.BlockSpec(memory_space=pltpu.SEMAPHORE)
SPLIT_COPY = pltpu.CompilerParams(has_side_effects=pltpu.SideEffectType.DATAFLOW_SIDE_EFFECTING)


def _in_hbm(a):
    return pltpu.with_memory_space_constraint(a, pltpu.HBM)

def _place():
    return lax.axis_index("x"), lax.axis_index("y"), lax.axis_index("c")


def _other_chips(x, y):
    return [(1 - x, y), (x, 1 - y), (1 - x, 1 - y)]


def _peer(x, y, c, m):
    return (x ^ (m >> 2), y ^ ((m >> 1) & 1), c ^ (m & 1))


def _place_cast(name, src, chip_arr, tile, layers, deps=()):
    _, n, cols = src.shape
    steps = n // tile
    k = len(layers)

    def body(chip_ref, *refs):
        for s_ref, o_ref in zip(refs[:k], refs[k + len(deps):]):
            o_ref[...] = s_ref[...].astype(BF16)

    def layer_spec(l):
        return pl.BlockSpec((None, tile, cols), lambda i, chip: (l, i, 0))

    return pl.pallas_call(
        body, name=name,
        grid_spec=pltpu.PrefetchScalarGridSpec(
            num_scalar_prefetch=1, grid=(steps,),
            in_specs=[layer_spec(l) for l in layers] + [ANY] * len(deps),
            out_specs=[pl.BlockSpec((tile, cols), lambda i, chip: (chip[0] * steps + i, 0))] * k),
        out_shape=[jax.ShapeDtypeStruct((N_SHARDS * n, cols), BF16)] * k,
        compiler_params=_params(),
    )(chip_arr, *[src] * k, *deps)


def _chip_rows(ref, chip, half=None):
    n = ref.shape[0] // N_SHARDS
    if half is None:
        return ref.at[pl.ds(pl.multiple_of(chip * n, 16), n), :]
    return ref.at[pl.ds(pl.multiple_of(chip * n + half * (n // 2), 16), n // 2), :]


def _gather_start(name, bufs, halved):
    n = len(bufs)

    def body(*refs):
        ins, send, recv, token = refs[:n], refs[n:2 * n], refs[2 * n:3 * n], refs[-1]
        x, y, c = _place()
        for a, buf in enumerate(ins):
            own = _chip_rows(buf, 2 * x + y, c if a in halved else None)
            for j, chip in enumerate(_other_chips(x, y)):
                pltpu.make_async_remote_copy(src_ref=own, dst_ref=own, send_sem=send[a].at[j], recv_sem=recv[a].at[j],
                                             device_id=(*chip, c), device_id_type=MESH).start()
        token[...] = jnp.zeros_like(token)

    outs = pl.pallas_call(
        body, name=name, in_specs=[HBM] * n,
        out_specs=[SEM] * (2 * n) + [HBM] * n + [pl.BlockSpec(memory_space=pltpu.VMEM)],
        out_shape=[pltpu.SemaphoreType.DMA((3,))] * (2 * n) + [pltpu.HBM(b.shape, b.dtype) for b in bufs]
        + [jax.ShapeDtypeStruct((8, 128), F32)],
        input_output_aliases={a: 2 * n + a for a in range(n)},
        compiler_params=SPLIT_COPY,
    )(*[_in_hbm(b) for b in bufs])
    return outs[:n], outs[n:2 * n], outs[2 * n:3 * n], outs[-1]


def _gather_wait(name, buf, send_sem, recv_sem, after, halved=False):
    def body(buf_ref, send_ref, recv_ref, *rest):
        x, y, c = _place()
        half = c if halved else None
        own = _chip_rows(buf_ref, 2 * x + y, half)
        for j, chip in enumerate(_other_chips(x, y)):
            copy = pltpu.make_async_remote_copy(src_ref=own, dst_ref=_chip_rows(buf_ref, 2 * chip[0] + chip[1], half),
                                                send_sem=send_ref.at[j], recv_sem=recv_ref.at[j],
                                                device_id=(*chip, c), device_id_type=MESH)
            copy.wait_send()
            copy.wait_recv()

    return pl.pallas_call(
        body, name=name, in_specs=[HBM, SEM, SEM] + [ANY] * len(after), out_specs=HBM,
        out_shape=pltpu.HBM(buf.shape, buf.dtype), input_output_aliases={0: 0}, compiler_params=SPLIT_COPY,
    )(buf, send_sem, recv_sem, *after)


def _forward_halves(name, buf):
    def body(in_ref, out_ref, send_sems, recv_sems):
        x, y, c = _place()

        def copy(j, chip, half):
            rows = 2 * chip[0] + chip[1]
            return pltpu.make_async_remote_copy(
                src_ref=_chip_rows(in_ref, rows, half), dst_ref=_chip_rows(out_ref, rows, half), send_sem=send_sems.at[j],
                recv_sem=recv_sems.at[j], device_id=(x, y, 1 - c), device_id_type=MESH)

        chips = _other_chips(x, y)
        for j, chip in enumerate(chips):
            copy(j, chip, c).start()
        for j, chip in enumerate(chips):
            copy(j, chip, c).wait_send()
            copy(j, chip, 1 - c).wait_recv()

    return pl.pallas_call(
        body, name=name, in_specs=[ANY], out_specs=ANY, out_shape=jax.ShapeDtypeStruct(buf.shape, buf.dtype),
        input_output_aliases={0: 0},
        scratch_shapes=[pltpu.SemaphoreType.DMA((3,))] * 2,
    )(buf)


def _piece_rows(ref, k):
    p = ref.shape[0] // 8
    return ref.at[pl.ds(pl.multiple_of(k * p, 32 // jnp.dtype(ref.dtype).itemsize), p), :]


def _exchange_start(name, arrays):
    n = len(arrays)
    zones = [lax.empty((7, a.shape[0] // 8, a.shape[1]), a.dtype) for a in arrays]

    def body(*refs):
        srcs, lands = refs[:n], refs[n:2 * n]
        send, recv, token = refs[2 * n:3 * n], refs[3 * n:4 * n], refs[-1]
        x, y, c = _place()
        for a, (src, land) in enumerate(zip(srcs, lands)):
            for m in range(1, 8):
                px, py, pc = _peer(x, y, c, m)
                pltpu.make_async_remote_copy(
                    src_ref=_piece_rows(src, 4 * px + 2 * py + pc), dst_ref=land.at[m - 1], send_sem=send[a].at[m - 1],
                    recv_sem=recv[a].at[m - 1], device_id=(px, py, pc), device_id_type=MESH).start()
        token[...] = jnp.zeros_like(token)

    outs = pl.pallas_call(
        body, name=name, in_specs=[HBM] * (2 * n),
        out_specs=[SEM] * (2 * n) + [HBM] * (2 * n) + [pl.BlockSpec(memory_space=pltpu.VMEM)],
        out_shape=[pltpu.SemaphoreType.DMA((7,))] * (2 * n) + [pltpu.HBM(a.shape, a.dtype) for a in arrays + zones]
        + [jax.ShapeDtypeStruct((8, 128), F32)],
        input_output_aliases={a: 2 * n + a for a in range(2 * n)},
        compiler_params=SPLIT_COPY,
    )(*[_in_hbm(a) for a in arrays + zones])
    return outs[:n], outs[n:2 * n], outs[2 * n:3 * n], outs[3 * n:4 * n], outs[-1]


def _exchange_wait(name, started, after):
    send_sems, recv_sems, arrays, zones, _ = started
    n = len(arrays)

    def body(*refs):
        srcs, lands = refs[:n], refs[n:2 * n]
        send, recv = refs[2 * n:3 * n], refs[3 * n:4 * n]
        x, y, c = _place()
        for a, (src, land) in enumerate(zip(srcs, lands)):
            for m in range(1, 8):
                px, py, pc = _peer(x, y, c, m)
                copy = pltpu.make_async_remote_copy(
                    src_ref=_piece_rows(src, 4 * px + 2 * py + pc), dst_ref=land.at[m - 1], send_sem=send[a].at[m - 1],
                    recv_sem=recv[a].at[m - 1], device_id=(px, py, pc), device_id_type=MESH)
                copy.wait_send()
                copy.wait_recv()

    outs = pl.pallas_call(
        body, name=name, in_specs=[HBM] * (2 * n) + [SEM] * (2 * n) + [ANY], out_specs=[HBM] * (2 * n),
        out_shape=[pltpu.HBM(a.shape, a.dtype) for a in list(arrays) + list(zones)],
        input_output_aliases={a: a for a in range(2 * n)}, compiler_params=SPLIT_COPY,
    )(*arrays, *zones, *send_sems, *recv_sems, after)
    return outs[n:]


def _sum_pieces(name, owns, recvs, place_arr, layer, dests=None):
    n = len(owns)
    steps = 2

    def body(place_ref, *refs):
        for o_ref, r_ref, out_ref in zip(refs[:n], refs[n:2 * n], refs[-n:]):
            total = o_ref[...]
            for m in range(7):
                total = total + r_ref[m].astype(F32)
            out_ref[...] = total

    tiles = [o.shape[0] // steps for o in owns]
    return pl.pallas_call(
        body, name=name,
        grid_spec=pltpu.PrefetchScalarGridSpec(
            num_scalar_prefetch=1, grid=(steps,),
            in_specs=[pl.BlockSpec((t, o.shape[1]), lambda i, place: (i, 0)) for o, t in zip(owns, tiles)]
            + [pl.BlockSpec((7, t, o.shape[1]), lambda i, place: (0, i, 0)) for o, t in zip(owns, tiles)]
            + ([] if dests is None else [ANY] * n),
            out_specs=[pl.BlockSpec((None, t, o.shape[1]), lambda i, place: (layer, place[1] * steps + i, 0))
                       for o, t in zip(owns, tiles)]),
        out_shape=[jax.ShapeDtypeStruct((DEPTH, 2 * o.shape[0], o.shape[1]), F32) for o in owns],
        input_output_aliases={} if dests is None else {1 + 2 * n + k: k for k in range(n)},
        compiler_params=_params(),
    )(place_arr, *owns, *recvs, *(() if dests is None else dests))


def _sum_small(partials, recvs, place_arr):
    n = len(partials)

    def body(place_ref, *refs):
        for o_ref, r_ref, out_ref in zip(refs[:n], refs[n:2 * n], refs[2 * n:]):
            total = o_ref[...]
            for m in range(7):
                total = total + r_ref[m]
            out_ref[...] = total

    piece = lambda a: pl.BlockSpec((a.shape[0] // 8, a.shape[1]), lambda i, place: (place[0], 0))
    return pl.pallas_call(
        body, name="sum_small",
        grid_spec=pltpu.PrefetchScalarGridSpec(
            num_scalar_prefetch=1, grid=(1,),
            in_specs=[piece(a) for a in partials] + [pl.BlockSpec(r.shape, lambda i, place: (0, 0, 0)) for r in recvs],
            out_specs=[piece(a) for a in partials]),
        out_shape=[jax.ShapeDtypeStruct(a.shape, F32) for a in partials],
        compiler_params=_params(),
    )(place_arr, *partials, *recvs)


def _share(name, bufs, parts, gathered=()):
    n, n_g = len(bufs), len(gathered)
    total = n + n_g

    def body(*refs):
        ins, outs = refs[:total], refs[total:2 * total]
        send_sems, recv_sems, send_g, recv_g = refs[2 * total:]
        x, y, c = _place()

        def half(ref, l, which):
            p = ref.shape[1] // 2
            return ref.at[l, pl.ds(pl.multiple_of(which * p, 8), p), :]

        def swap(k, which):
            a, l = parts[k]
            return pltpu.make_async_remote_copy(
                src_ref=half(ins[a], l, which), dst_ref=half(outs[a], l, which), send_sem=send_sems.at[k],
                recv_sem=recv_sems.at[k], device_id=(x, y, 1 - c), device_id_type=MESH)

        def spread(a, m, sender):
            k = 4 * sender[0] + 2 * sender[1] + sender[2]
            return pltpu.make_async_remote_copy(
                src_ref=_piece_rows(ins[n + a], k), dst_ref=_piece_rows(outs[n + a], k), send_sem=send_g.at[7 * a + m - 1],
                recv_sem=recv_g.at[7 * a + m - 1], device_id=_peer(x, y, c, m), device_id_type=MESH)

        for k in range(len(parts)):
            swap(k, c).start()
        for a in range(n_g):
            for m in range(1, 8):
                spread(a, m, (x, y, c)).start()
        for k in range(len(parts)):
            swap(k, c).wait_send()
            swap(k, 1 - c).wait_recv()
        for a in range(n_g):
            for m in range(1, 8):
                spread(a, m, (x, y, c)).wait_send()
                spread(a, m, _peer(x, y, c, m)).wait_recv()

    arrays = list(bufs) + list(gathered)
    return pl.pallas_call(
        body, name=name, in_specs=[ANY] * total, out_specs=[ANY] * total,
        out_shape=[jax.ShapeDtypeStruct(b.shape, F32) for b in arrays],
        input_output_aliases={a: a for a in range(total)},
        scratch_shapes=[pltpu.SemaphoreType.DMA((max(len(parts), 1),))] * 2
        + [pltpu.SemaphoreType.DMA((max(7 * n_g, 1),))] * 2,
    )(*arrays)


def _adamw_math(w, g, m, v):
    nm = ADAM_B1 * m + (1.0 - ADAM_B1) * g
    nv = ADAM_B2 * v + (1.0 - ADAM_B2) * (g * g)
    m_hat = nm / (1.0 - ADAM_B1 ** ADAM_STEP)
    v_hat = nv / (1.0 - ADAM_B2 ** ADAM_STEP)
    return -ADAM_LR * (m_hat / (jnp.sqrt(v_hat) + ADAM_EPS) + ADAM_WD * w), nm, nv


def _adamw(name, w, g, m, v, rows_per_step, first=0, count=None, dests=None, deps=()):
    layers, rows, cols = w.shape
    count = layers if count is None else count

    def body(w_ref, g_ref, m_ref, v_ref, *rest):
        d_ref, nm_ref, nv_ref, g_out_ref = rest[-4:]
        d_ref[...], nm_ref[...], nv_ref[...] = _adamw_math(w_ref[...], g_ref[...], m_ref[...], v_ref[...])
        g_out_ref[...] = g_ref[...]

    spec = pl.BlockSpec((1, rows_per_step, cols), lambda l, i: (first + l, i, 0))
    shape = jax.ShapeDtypeStruct(w.shape, F32)
    dests = () if dests is None else tuple(dests)
    return pl.pallas_call(
        body, name=name, grid=(count, rows // rows_per_step),
        in_specs=[spec] * 4 + [ANY] * (len(dests) + len(deps)), out_specs=[spec] * 4, out_shape=[shape] * 4,
        input_output_aliases={4 + k: k for k in range(len(dests))},
        compiler_params=_params(("arbitrary", "arbitrary")),
    )(w, g, m, v, *dests, *deps)


def _pack_misc(pool_scale, sinks, norm_pre, norm_post):
    sink_rows = jnp.zeros((DEPTH, 8, 128), F32).at[:, 0, 0:N_HEADS].set(sinks).reshape(2 * 8, 128)
    return jnp.concatenate([pool_scale.reshape(8, 128), norm_pre.reshape(16, 128), norm_post.reshape(16, 128),
                            sink_rows, jnp.zeros((8, 128), F32)], axis=0)


def _adamw_small(w, g, m, v, pool):
    def body(w_ref, g_ref, m_ref, v_ref, pw_ref, pg_ref, pm_ref, pv_ref, *rest):
        outs, pool_outs, (d_ref, nm_ref, nv_ref) = rest[:17], rest[17:21], rest[21:]
        pool_outs[0][...] = pg_ref[...]
        pool_outs[1][...], pool_outs[2][...], pool_outs[3][...] = _adamw_math(
            pw_ref[...], pg_ref[...], pm_ref[...], pv_ref[...])
        d_ref[...], nm_ref[...], nv_ref[...] = _adamw_math(w_ref[...], g_ref[...], m_ref[...], v_ref[...])
        for k, src in enumerate([g_ref, d_ref, nm_ref, nv_ref]):
            scale, sinks, pre, post = outs[4 * k:4 * k + 4]
            for l in range(DEPTH):
                for j in range(4):
                    scale[l:l + 1, j * 128:(j + 1) * 128] = src[MISC_SCALE + 4 * l + j:MISC_SCALE + 4 * l + j + 1, :]
                for j in range(8):
                    pre[l:l + 1, j * 128:(j + 1) * 128] = src[MISC_PRE + 8 * l + j:MISC_PRE + 8 * l + j + 1, :]
                    post[l:l + 1, j * 128:(j + 1) * 128] = src[MISC_POST + 8 * l + j:MISC_POST + 8 * l + j + 1, :]
                sinks[l:l + 1, :] = src[MISC_SINKS + 8 * l:MISC_SINKS + 8 * l + 1, 0:N_HEADS]
        outs[16][...] = g_ref[MISC_LOSS:MISC_LOSS + 1, 0:1]

    vmem = pl.BlockSpec(memory_space=pltpu.VMEM)
    shapes = [(DEPTH, D_POOL), (DEPTH, N_HEADS), (DEPTH, D), (DEPTH, D)] * 4 + [(1, 1)]
    shapes += [pool[0].shape] * 4
    return pl.pallas_call(
        body, name="adamw_small", in_specs=[vmem] * 8, out_specs=[vmem] * 21,
        out_shape=[jax.ShapeDtypeStruct(s, F32) for s in shapes],
        scratch_shapes=[pltpu.VMEM((MISC_ROWS, 128), F32)] * 3,
    )(w, g, m, v, *pool)


def kernel(x, w_in, pool_w, pool_scale, attn_sinks, w_out, norm_pre, norm_post, loss_target, m_w_in, m_pool_w, m_pool_scale, m_attn_sinks, m_w_out, m_norm_pre, m_norm_post, v_w_in, v_pool_w, v_pool_scale, v_attn_sinks, v_w_out, v_norm_pre, v_norm_post):
    cx, cy, cc = _place()
    chip_arr = jnp.reshape(2 * cx + cy, (1,)).astype(jnp.int32)
    place_arr = jnp.stack([4 * cx + 2 * cy + cc, cc]).astype(jnp.int32)
    t = lambda a: jnp.transpose(a, (0, 2, 1))
    w_in_t = t(w_in)
    xs, target = x[0], loss_target[0]
    pool_w_b = pool_w.astype(BF16)
    tables = _attention_tables()
    scale3 = pool_scale.reshape(DEPTH, 1, D_POOL)
    pre3 = norm_pre.reshape(DEPTH, 1, D)
    post3 = norm_post.reshape(DEPTH, 1, D)

    (wi0,) = _place_cast("place_w_in0", w_in_t, chip_arr, 288, [0])
    first = _gather_start("gather_start_first", [wi0], halved=(0,))
    (wi1,) = _place_cast("place_w_in1", w_in_t, chip_arr, 288, [1], deps=(first[3],))
    wo = _place_cast("place_w_out", w_out, chip_arr, 256, [0, 1], deps=(first[3],))
    rest = _gather_start("gather_start_rest", [wi1, wo[0], wo[1]], halved=(0,))
    send, recv, bufs = [first[k] + rest[k] for k in range(3)]
    order = {(0, "in"): 0, (1, "in"): 1, (0, "out"): 2, (1, "out"): 3}

    saved = []
    packed = [_pack_misc(pool_scale, attn_sinks, norm_pre, norm_post),
              _pack_misc(m_pool_scale, m_attn_sinks, m_norm_pre, m_norm_post),
              _pack_misc(v_pool_scale, v_attn_sinks, v_norm_pre, v_norm_post)]
    after = (first[3], rest[3], pool_w_b, *tables, scale3, pre3, post3, *packed)
    below = None
    for l in range(DEPTH):
        k = order[l, "in"]
        w_in_l = _forward_halves(f"forward_w_in{l}", _gather_wait(f"gather_wait_in{l}", bufs[k], send[k], recv[k], after,
                                                                 halved=True))
        if below is None:
            pu, pg, q, kv, ag = _fwd_in(l, xs, pre3, w_in_l)
        else:
            y, xs, pu, pg, q, kv, ag = _fwd_in(l, xs, pre3, w_in_l, below)
            saved[l - 1][7] = y
        cat, *probs = _fwd_mix(l, pu, pg, q, kv, ag, pool_w_b, scale3, attn_sinks, tables)
        k = order[l, "out"]
        w_out_l = _gather_wait(f"gather_wait_out{l}", bufs[k], send[k], recv[k], (cat,))
        saved.append([xs, pu, pg, q, kv, ag, cat, None, w_in_l, w_out_l, probs])
        below, after = (cat, w_out_l, post3), (w_out_l,)

    tri = tables[1]
    x_in, pu, pg, q, kv, ag, cat, y, w_in_l, w_out_l, probs = saved[1]
    dcat, dw_out1, dw_out1_b, dg_post1, loss, xs = _bwd_out(1, cat, w_out_l, post3, place_arr, x=x_in, target=target)
    ex1_out = _exchange_start("exchange_start_out1", [dw_out1_b])
    dproj, dpw, dsc1, dsink1 = _bwd_mix(1, pu, pg, q, kv, ag, dcat, pool_w_b, scale3, probs, tri, deps=(ex1_out[4],))
    dx, dg_pre1, dw_in1, dw_in1_b = _bwd_in_dx(1, dproj, w_in_l, x_in, pre3, xs, dw_place=place_arr)
    ex1_in = _exchange_start("exchange_start_in1", [dw_in1_b])

    x_in, pu, pg, q, kv, ag, cat, y, w_in_l, w_out_l, probs = saved[0]
    dcat, dw_out0, dw_out0_b, dg_post0 = _bwd_out(0, cat, w_out_l, post3, place_arr, dxn=dx, y=y, deps=(ex1_in[4],))
    ex0_out = _exchange_start("exchange_start_out0", [dw_out0_b])
    dproj, dpw, dsc0, dsink0 = _bwd_mix(0, pu, pg, q, kv, ag, dcat, pool_w_b, scale3, probs, tri,
                                        deps=(ex0_out[4],), dpw_dest=dpw)
    dw_in0, dw_in0_b = _bwd_in_dw(0, dproj, x_in, pre3, place_arr)
    ex0_in = _exchange_start("exchange_start_in0", [dw_in0_b])

    grad_x, dg_pre0 = _bwd_in_dx(0, dproj, w_in_l, x_in, pre3, dx, deps=(ex0_in[4],))
    small = [dpw.reshape(DEPTH * 4 * 128, 128),
             jnp.concatenate([dsc0, dsc1, dg_pre0, dg_pre1, dg_post0, dg_post1, dsink0, dsink1, loss], axis=0)]
    ex_small = _exchange_start("exchange_start_small", small)
    (recv_out1,) = _exchange_wait("exchange_wait_out1", ex1_out, ex_small[4])
    (recv_in1,) = _exchange_wait("exchange_wait_in1", ex1_in, recv_out1)
    g_in, g_out = _sum_pieces("sum_pieces_1", [dw_in1, dw_out1], [recv_in1, recv_out1], place_arr, 1)
    (recv_out0,) = _exchange_wait("exchange_wait_out0", ex0_out, g_out)
    (g_out,) = _sum_pieces("sum_pieces_out0", [dw_out0], [recv_out0], place_arr, 0, dests=[g_out])
    g_in, g_out = _share("share_a", [g_in, g_out], [(0, 1), (1, 0), (1, 1)])
    m_in_t, v_in_t = t(m_w_in), t(v_w_in)
    d_out, nm_out, nv_out, grad_w_out = _adamw("adamw_w_out", w_out, g_out, m_w_out, v_w_out, 256)
    upd_in = _adamw("adamw_w_in1", w_in_t, g_in, m_in_t, v_in_t, 288, first=1, count=1, deps=(d_out,))

    (recv_in0,) = _exchange_wait("exchange_wait_in0", ex0_in, upd_in[0])
    recv_small = _exchange_wait("exchange_wait_small", ex_small, recv_in0)
    (g_in,) = _sum_pieces("sum_pieces_in0", [dw_in0], [recv_in0], place_arr, 0, dests=[g_in])
    g_in, g_pw, g_misc = _share("share_b", [g_in], [(0, 0)], _sum_small(small, recv_small, place_arr))
    d_in, nm_in, nv_in, grad_w_in_t = _adamw("adamw_w_in0", w_in_t, g_in, m_in_t, v_in_t, 288, first=0, count=1,
                                             dests=upd_in)
    flat = lambda a: a.reshape(DEPTH * 4 * 128, 128)
    small_out = _adamw_small(packed[0], g_misc, packed[1], packed[2],
                             (flat(pool_w), g_pw, flat(m_pool_w), flat(v_pool_w)))
    (g_sc, g_sk, g_pre, g_post, d_sc, d_sk, d_pre, d_post,
     m_sc, m_sk, m_pre, m_post, v_sc, v_sk, v_pre, v_post, loss_sum) = small_out[:17]
    g_pw, d_pw, m_pw, v_pw = [a.reshape(pool_w.shape) for a in small_out[17:]]
    return (loss_sum[0, 0], grad_x[None], t(grad_w_in_t), g_pw, g_sc, g_sk, grad_w_out, g_pre, g_post,
            t(d_in), d_pw, d_sc, d_sk, d_out, d_pre, d_post,
            t(nm_in), m_pw, m_sc, m_sk, nm_out, m_pre, m_post,
            t(nv_in), v_pw, v_sc, v_sk, nv_out, v_pre, v_post)
```

```python
import jax
import jax.numpy as jnp
from jax import lax
from jax.experimental import pallas as pl
from jax.experimental.pallas import tpu as pltpu

F32 = jnp.float32
BF16 = jnp.bfloat16

S = 2048
D = 1024
DEPTH = 2
D_POOL = 512
POOL_WINDOWS = (2, 4, 8, 16)
N_HEADS = 8
D_IN = 2304
N_SHARDS = 4
W_IN_SHARD = D_IN // N_SHARDS
W_OUT_SHARD = D // N_SHARDS
BLK = 128
NB = S // BLK
HALO = 16
PAD = 8
EPS = 1e-6
NEG_INF = -1e30
C_PU, C_PG, C_Q, C_K, C_V, C_AG = 0, 512, 1024, 1536, 1664, 1792

ADAM_LR = 0.001
ADAM_B1 = 0.9
ADAM_B2 = 0.999
ADAM_EPS = 1e-08
ADAM_WD = 0.01
ADAM_STEP = 10

TM = 512
VMEM_LIMIT = 56 * 1024 * 1024

NT = (((1,), (1,)), ((), ()))
TN = (((0,), (0,)), ((), ()))

MESH = pl.DeviceIdType.MESH
ANY = pl.BlockSpec(memory_space=pl.ANY)

MISC_SCALE, MISC_PRE, MISC_POST, MISC_SINKS, MISC_LOSS = 0, 8, 24, 40, 56
MISC_ROWS = 64


def _params(sem=("arbitrary",)):
    return pltpu.CompilerParams(dimension_semantics=sem, vmem_limit_bytes=VMEM_LIMIT)


def _sigmoid(v):
    return 1.0 / (1.0 + jnp.exp(-v))


def _rows8(v):
    r, c = v.shape
    return v.reshape(r // 8, 8, c).sum(axis=0)


def _layer(l, *shape):
    zeros = (0,) * len(shape)
    return pl.BlockSpec((None,) + shape, lambda i: (l,) + zeros)


def _whole(shape):
    zeros = (0,) * len(shape)
    return pl.BlockSpec(shape, lambda i: zeros, pipeline_mode=pl.Buffered(1))


def _fwd_in(l, x, g_pre, w_in_t, below=None):
    fused = below is not None

    def body(x_ref, g_ref, w_ref, *rest):
        if fused:
            cat_ref, wo_ref, gp_ref, y_ref, xn_ref = rest[:5]
            y = jnp.dot(cat_ref[...], wo_ref[...], preferred_element_type=F32)
            y_ref[...] = y
            xt = x_ref[...] + y * lax.rsqrt(jnp.mean(y * y, axis=-1, keepdims=True) + EPS) * gp_ref[...]
            xn_ref[...] = xt
        else:
            xt = x_ref[...]
        pu_ref, pg_ref, q_ref, kv_ref, ag_ref = rest[-5:]
        r = lax.rsqrt(jnp.mean(xt * xt, axis=-1, keepdims=True) + EPS)
        h = (xt * r * g_ref[...]).astype(BF16)

        def proj(lo, hi):
            return lax.dot_general(h, w_ref[lo:hi, :], NT, preferred_element_type=F32)

        pu_ref[...] = proj(C_PU, C_PG)
        pg_ref[...] = proj(C_PG, C_Q)
        q_ref[...] = proj(C_Q, C_K).astype(BF16)
        kv_ref[...] = proj(C_K, C_AG).astype(BF16)
        ag_ref[...] = proj(C_AG, D_IN)

    row = lambda w: pl.BlockSpec((TM, w), lambda i: (i, 0))
    act = jax.ShapeDtypeStruct((S, D), F32)
    return pl.pallas_call(
        body, name="fwd_out_in" if fused else "fwd_in", grid=(S // TM,),
        in_specs=[row(D), _layer(l, 1, D), _whole((D_IN, D))]
        + ([row(D), _whole((D, D)), _layer(l - 1, 1, D)] if fused else []),
        out_specs=[row(D)] * (2 * fused) + [row(512), row(512), row(512), row(256), row(512)],
        out_shape=[act] * (2 * fused)
        + [jax.ShapeDtypeStruct((S, 512), F32), jax.ShapeDtypeStruct((S, 512), F32),
           jax.ShapeDtypeStruct((S, 512), BF16), jax.ShapeDtypeStruct((S, 256), BF16),
           jax.ShapeDtypeStruct((S, 512), F32)],
        compiler_params=_params(),
    )(x, g_pre, w_in_t, *(below if fused else ()))


LOG2E = 1.4426950408889634
SCORE_SCALE = 0.125 * LOG2E


def _attention_tables():
    qi = jnp.arange(BLK)[:, None]
    kj = jnp.arange(BLK)[None, :]
    dist = ((qi - kj) % BLK).astype(F32)
    slopes = jnp.exp2(-jnp.arange(1, N_HEADS + 1, dtype=F32))
    bias = -(slopes * LOG2E)[:, None, None] * dist[None]
    first = jnp.where(kj > qi, NEG_INF, bias)
    return jnp.stack([first, bias]), (kj <= qi).astype(BF16)


def _own_block_mask():
    return lax.broadcasted_iota(jnp.int32, (BLK, BLK), 1) <= lax.broadcasted_iota(jnp.int32, (BLK, BLK), 0)


def _merge(full, own):
    return jnp.where(own, full[:, BLK:], full[:, :BLK])


def _spread(v, tri):
    own = v * tri
    return jnp.concatenate([v - own, own], axis=1)


def _head_variants(cur, prev):
    both = jnp.concatenate([prev, cur], axis=0).astype(F32)
    swapped = pltpu.roll(both, 64, axis=1)
    low = lax.broadcasted_iota(jnp.int32, both.shape, 1) < 64
    zero = jnp.zeros_like(both)
    return ((jnp.where(low, both, zero).astype(BF16), jnp.where(low, zero, swapped).astype(BF16)),
            (jnp.where(low, swapped, zero).astype(BF16), jnp.where(low, zero, both).astype(BF16)))


def _head_of(hkv, t, half):
    return hkv * 4 + 2 * t + half


def _rows(v, t):
    return v[t * BLK:(t + 1) * BLK]


def _stack_tiles(ref, hkv, offset=0):
    lo = offset + 2 * hkv * 128
    return jnp.concatenate([ref[:, lo:lo + 128], ref[:, lo + 128:lo + 256]], axis=0)


def _scores(q2, k_var, own):
    s = {}
    for hkv in range(2):
        for half in range(2):
            full = lax.dot_general(q2[hkv], k_var[hkv][half], NT, preferred_element_type=F32)
            for t in range(2):
                s[hkv, t, half] = _merge(_rows(full, t), own)
    return s


def _softmax(s, bias, sink):
    s = s * SCORE_SCALE + bias
    sink2 = sink * LOG2E
    m = jnp.maximum(jnp.max(s, axis=-1, keepdims=True), sink2)
    p = jnp.exp2(s - m)
    e_sink = jnp.exp2(sink2 - m)
    inv = 1.0 / (jnp.sum(p, axis=-1, keepdims=True) + e_sink)
    return p * inv, e_sink * inv


def _spread_pair(v, hkv, half, tri):
    return jnp.concatenate([_spread(v[hkv, t, half].astype(BF16), tri) for t in range(2)], axis=0)


POOL_ROWS = PAD + HALO + BLK


def _window_sums(src_ref, tmp_refs, trailing):
    lo, hi = (PAD, POOL_ROWS) if trailing else (0, HALO + BLK)
    cur = src_ref
    for level in range(len(POOL_WINDOWS)):
        lanes = slice(level * 128, 512)
        shift = -(1 << level) if trailing else (1 << level)
        dst = tmp_refs[level % 2]
        dst[lo:hi, lanes] = cur[lo:hi, lanes] + cur[lo + shift:hi + shift, lanes]
        cur = dst


def _pool_block(ext_ref, tmp_refs, i, g, w):
    lanes = slice(g * 128, (g + 1) * 128)
    rows = slice(PAD + HALO, POOL_ROWS)
    t = (i * BLK + lax.broadcasted_iota(jnp.int32, (BLK, 1), 0)).astype(F32)
    inv = 1.0 / jnp.minimum(t + 1.0, float(w))
    return tmp_refs[g % 2][rows, lanes] * inv - ext_ref[rows, lanes], inv


def _fwd_mix(l, pu, pg, q, kv, ag, pool_w, pool_scale, sinks, tables):
    bias, tri = tables

    def body(pu_ref, pup_ref, pg_ref, q_ref, kv_ref, kvp_ref, ag_ref, pw_ref, sc_ref, sink_ref, bias_ref, tri_ref,
             cat_ref, ext_ref, *tmp_refs):
        i = pl.program_id(0)

        @pl.when(i == 0)
        def _():
            for ref in (ext_ref, *tmp_refs):
                ref[0:PAD, :] = jnp.zeros((PAD, 512), F32)

        ext_ref[PAD:PAD + HALO, :] = jnp.where(i > 0, pup_ref[...], 0.0)
        ext_ref[PAD + HALO:POOL_ROWS, :] = pu_ref[...]
        _window_sums(ext_ref, tmp_refs, True)
        for g, w in enumerate(POOL_WINDOWS):
            lanes = slice(g * 128, (g + 1) * 128)
            pooled, _ = _pool_block(ext_ref, tmp_refs, i, g, w)
            mixed = jnp.dot(pooled.astype(BF16), pw_ref[g], preferred_element_type=F32)
            gate = pg_ref[:, lanes]
            cat_ref[:, lanes] = (mixed * sc_ref[:, lanes] * (gate * _sigmoid(gate))).astype(BF16)

        own = _own_block_mask()
        tri = tri_ref[...]
        k_var = _head_variants(kv_ref[:, 0:128], kvp_ref[:, 0:128])
        v_var = _head_variants(kv_ref[:, 128:256], kvp_ref[:, 128:256])
        s = _scores([_stack_tiles(q_ref, hkv) for hkv in range(2)], k_var, own)
        p = {}
        for (hkv, t, half), s_head in s.items():
            head = _head_of(hkv, t, half)
            p[hkv, t, half], _ = _softmax(s_head, bias_ref[head], sink_ref[l, head])
        for hkv in range(2):
            o2 = jnp.zeros((2 * BLK, 128), F32)
            for half in range(2):
                o2 = o2 + jnp.dot(_spread_pair(p, hkv, half, tri), v_var[hkv][half], preferred_element_type=F32)
            for t in range(2):
                lo = (2 * hkv + t) * 128
                gate = ag_ref[:, lo:lo + 128]
                cat_ref[:, D_POOL + lo:D_POOL + lo + 128] = (_rows(o2, t) * (gate * _sigmoid(gate))).astype(BF16)

    blk = lambda w: pl.BlockSpec((BLK, w), lambda i: (i, 0))
    prev = lambda w: pl.BlockSpec((BLK, w), lambda i: (jnp.maximum(i - 1, 0), 0))
    halo = pl.BlockSpec((HALO, 512), lambda i: (jnp.maximum(i * (BLK // HALO) - 1, 0), 0))
    return pl.pallas_call(
        body, name="fwd_mix", grid=(NB,),
        in_specs=[blk(512), halo, blk(512), blk(512), blk(256), prev(256), blk(512),
                  _layer(l, 4, 128, 128), _layer(l, 1, 512), pl.BlockSpec(memory_space=pltpu.SMEM),
                  pl.BlockSpec((None, N_HEADS, BLK, BLK), lambda i: (jnp.minimum(i, 1), 0, 0, 0)), _whole((BLK, BLK))],
        out_specs=blk(D),
        out_shape=jax.ShapeDtypeStruct((S, D), BF16),
        scratch_shapes=[pltpu.VMEM((POOL_ROWS, 512), F32)] * 3,
        compiler_params=_params(),
    )(pu, pu, pg, q, kv, kv, ag, pool_w, pool_scale, sinks, bias, tri)


def _store_lane_rows(ref, acc):
    total = jnp.sum(acc, axis=0, keepdims=True)
    for k in range(ref.shape[0]):
        ref[k:k + 1, :] = total[:, k * 128:(k + 1) * 128]


def _own_piece(dw_ref, place_ref):
    p = dw_ref.shape[0] // 8
    return dw_ref[pl.ds(pl.multiple_of(place_ref[0] * p, 8), p), :]


def _bwd_out(l, cat, w_out, g_post, place_arr, dxn=None, y=None, x=None, target=None, deps=()):
    last = target is not None
    n_steps = S // TM

    def body(a_ref, b_ref, g_ref, cat_ref, w_ref, place_ref, *rest):
        dcat_ref, own_ref, dwb_ref, dg_ref = rest[len(deps):len(deps) + 4]
        rest = rest[len(deps) + 4:]
        acc_ref, dw_ref = rest[-2:]
        step = pl.program_id(0)

        @pl.when(step == 0)
        def _():
            dw_ref[...] = jnp.zeros_like(dw_ref)
            acc_ref[...] = jnp.zeros_like(acc_ref)

        cat = cat_ref[...]
        g = g_ref[...]
        y = jnp.dot(cat, w_ref[...], preferred_element_type=F32) if last else b_ref[...]
        r = lax.rsqrt(jnp.mean(y * y, axis=-1, keepdims=True) + EPS)
        if last:
            loss_ref, dx_ref, loss_acc_ref = rest[:3]
            err = a_ref[...] + y * r * g - b_ref[...]

            @pl.when(step == 0)
            def _():
                loss_acc_ref[...] = jnp.zeros_like(loss_acc_ref)

            loss_acc_ref[...] += _rows8(err * err)
            dz = err * (1.0 / D)
            dx_ref[...] = dz
        else:
            dz = a_ref[...]
        a = dz * g
        dy = r * a - y * (r * r * r) * jnp.mean(a * y, axis=-1, keepdims=True)
        acc_ref[...] += _rows8(dz * (y * r))
        dyb = dy.astype(BF16)
        dcat_ref[...] = lax.dot_general(dyb, w_ref[...], NT, preferred_element_type=F32)
        dw_ref[...] += lax.dot_general(cat, dyb, TN, preferred_element_type=F32)

        @pl.when(step == n_steps - 1)
        def _():
            _store_lane_rows(dg_ref, acc_ref[...])
            dwb_ref[...] = dw_ref[...].astype(BF16)
            own_ref[...] = _own_piece(dw_ref, place_ref)
            if last:
                loss_ref[...] = jnp.full((8, 128), (0.5 / D) * jnp.sum(loss_acc_ref[...]), F32)

    row = lambda: pl.BlockSpec((TM, D), lambda i: (i, 0))
    full = _whole
    return pl.pallas_call(
        body, name="out_loss_bwd" if last else "bwd_out", grid=(n_steps,),
        in_specs=[row(), row(), _layer(l, 1, D), row(), full((D, D)), pl.BlockSpec(memory_space=pltpu.SMEM)]
        + [ANY] * len(deps),
        out_specs=[row(), full((D // 8, D)), full((D, D)), full((8, 128))] + ([full((8, 128)), row()] if last else []),
        out_shape=[jax.ShapeDtypeStruct((S, D), F32), jax.ShapeDtypeStruct((D // 8, D), F32),
                   jax.ShapeDtypeStruct((D, D), BF16), jax.ShapeDtypeStruct((8, 128), F32)]
        + ([jax.ShapeDtypeStruct((8, 128), F32), jax.ShapeDtypeStruct((S, D), F32)] if last else []),
        scratch_shapes=([pltpu.VMEM((8, D), F32)] if last else []) + [pltpu.VMEM((8, D), F32), pltpu.VMEM((D, D), F32)],
        compiler_params=_params(),
    )(*((x, target) if last else (dxn, y)), g_post, cat, w_out, place_arr, *deps)


def _bwd_mix(l, pu, pg, q, kv, ag, dcat, pool_w, pool_scale, sinks, tables, deps=(), dpw_dest=None):
    bias, tri = tables
    deps = tuple(deps) + (() if dpw_dest is None else (dpw_dest,))

    def body(pu_ref, pup_ref, pg_ref, q_ref, kv_ref, kvp_ref, ag_ref, dcat_ref, pw_ref, sc_ref, sink_ref, bias_ref,
             tri_ref, *rest):
        dproj_ref, dpw_ref, dsc_ref, dsink_ref, ext_ref, dext_ref, tmp_a, tmp_b, dkv_ref = rest[len(deps):]
        tmp_refs = (tmp_a, tmp_b)
        step = pl.program_id(0)
        i = NB - 1 - step

        @pl.when(step == 0)
        def _():
            dpw_ref[...] = jnp.zeros_like(dpw_ref)
            dsc_ref[...] = jnp.zeros_like(dsc_ref)
            dsink_ref[...] = jnp.zeros_like(dsink_ref)
            for ref in (ext_ref, tmp_a, tmp_b):
                ref[0:PAD, :] = jnp.zeros((PAD, 512), F32)
            dext_ref[BLK:POOL_ROWS, :] = jnp.zeros((HALO + PAD, 512), F32)
            dkv_ref[...] = jnp.zeros_like(dkv_ref)

        ext_ref[PAD:PAD + HALO, :] = jnp.where(i > 0, pup_ref[...], 0.0)
        ext_ref[PAD + HALO:POOL_ROWS, :] = pu_ref[...]
        _window_sums(ext_ref, tmp_refs, True)
        dpooled = []
        for g, w in enumerate(POOL_WINDOWS):
            lanes = slice(g * 128, (g + 1) * 128)
            pooled, inv = _pool_block(ext_ref, tmp_refs, i, g, w)
            pooled_b = pooled.astype(BF16)
            mixed = jnp.dot(pooled_b, pw_ref[g], preferred_element_type=F32)
            scale = sc_ref[:, lanes]
            gate = pg_ref[:, lanes]
            sg = _sigmoid(gate)
            dpo = dcat_ref[:, lanes]
            dproj_ref[:, C_PG + g * 128:C_PG + (g + 1) * 128] = (
                dpo * (mixed * scale) * (sg * (1.0 + gate * (1.0 - sg)))).astype(BF16)
            dms = dpo * (gate * sg)
            dsc_ref[g:g + 1, :] += jnp.sum(dms * mixed, axis=0, keepdims=True)
            dmixed = (dms * scale).astype(BF16)
            dpw_ref[g] += lax.dot_general(pooled_b, dmixed, TN, preferred_element_type=F32)
            dpooled.append(lax.dot_general(dmixed, pw_ref[g], NT, preferred_element_type=F32))
            dext_ref[0:BLK, lanes] = dpooled[g] * inv
        _window_sums(dext_ref, tmp_refs, False)
        for g in range(len(POOL_WINDOWS)):
            lanes = slice(g * 128, (g + 1) * 128)
            dproj_ref[:, C_PU + g * 128:C_PU + (g + 1) * 128] = (tmp_refs[g % 2][0:BLK, lanes] - dpooled[g]).astype(BF16)
        dext_ref[BLK:BLK + HALO, :] = dext_ref[0:HALO, :]

        own = _own_block_mask()
        tri = tri_ref[...]
        k_var = _head_variants(kv_ref[:, 0:128], kvp_ref[:, 0:128])
        v_var = _head_variants(kv_ref[:, 128:256], kvp_ref[:, 128:256])
        q2 = [_stack_tiles(q_ref, hkv) for hkv in range(2)]
        s = _scores(q2, k_var, own)
        p, p_sink = {}, {}
        for key, s_head in s.items():
            head = _head_of(*key)
            p[key], p_sink[key] = _softmax(s_head, bias_ref[head], sink_ref[l, head])

        do2, p_b, dp = [], {}, {}
        for hkv in range(2):
            gate = _stack_tiles(ag_ref, hkv)
            sg = _sigmoid(gate)
            dca = _stack_tiles(dcat_ref, hkv, D_POOL)
            do2.append((dca * (gate * sg)).astype(BF16))
            o2 = jnp.zeros((2 * BLK, 128), F32)
            for half in range(2):
                p_b[hkv, half] = _spread_pair(p, hkv, half, tri)
                o2 = o2 + jnp.dot(p_b[hkv, half], v_var[hkv][half], preferred_element_type=F32)
                full = lax.dot_general(do2[hkv], v_var[hkv][half], NT, preferred_element_type=F32)
                for t in range(2):
                    dp[hkv, t, half] = _merge(_rows(full, t), own)
            dag = dca * o2 * (sg * (1.0 + gate * (1.0 - sg)))
            for t in range(2):
                lo = C_AG + (2 * hkv + t) * 128
                dproj_ref[:, lo:lo + 128] = _rows(dag, t).astype(BF16)

        ds = {}
        for key in p:
            delta = jnp.sum(p[key] * dp[key], axis=-1, keepdims=True)
            ds[key] = p[key] * (dp[key] - delta)
            head = _head_of(*key)
            dsink_ref[0:1, :] += jnp.where(lax.broadcasted_iota(jnp.int32, (1, 128), 1) == head,
                                           -jnp.sum(p_sink[key] * delta, axis=0, keepdims=True), 0.0)

        dk_acc = [[None, None], [None, None]]
        dv_acc = [[None, None], [None, None]]
        for hkv in range(2):
            dq2 = jnp.zeros((2 * BLK, 128), F32)
            for half in range(2):
                ds_b = _spread_pair(ds, hkv, half, tri)
                dq2 = dq2 + jnp.dot(ds_b, k_var[hkv][half], preferred_element_type=F32)
                dk_acc[hkv][half] = lax.dot_general(ds_b, q2[hkv], TN, preferred_element_type=F32)
                dv_acc[hkv][half] = lax.dot_general(p_b[hkv, half], do2[hkv], TN, preferred_element_type=F32)
            for t in range(2):
                lo = C_Q + (2 * hkv + t) * 128
                dproj_ref[:, lo:lo + 128] = (_rows(dq2, t) * 0.125).astype(BF16)

        low = lax.broadcasted_iota(jnp.int32, (2 * BLK, 128), 1) < 64

        def gather_heads(acc):
            return jnp.where(low, acc[0][0] + pltpu.roll(acc[0][1], 64, axis=1),
                             pltpu.roll(acc[1][0], 64, axis=1) + acc[1][1])

        dk = gather_heads(dk_acc) * 0.125
        dv = gather_heads(dv_acc)
        dproj_ref[:, C_K:C_V] = (dk[BLK:, :] + dkv_ref[:, 0:128]).astype(BF16)
        dproj_ref[:, C_V:C_AG] = (dv[BLK:, :] + dkv_ref[:, 128:256]).astype(BF16)
        dkv_ref[:, 0:128] = dk[:BLK, :]
        dkv_ref[:, 128:256] = dv[:BLK, :]

    rev = lambda w: pl.BlockSpec((BLK, w), lambda s: (NB - 1 - s, 0))
    prev = lambda w: pl.BlockSpec((BLK, w), lambda s: (jnp.maximum(NB - 2 - s, 0), 0))
    halo = pl.BlockSpec((HALO, 512), lambda s: (jnp.maximum((NB - 1 - s) * (BLK // HALO) - 1, 0), 0))
    return pl.pallas_call(
        body, name="bwd_mix", grid=(NB,),
        in_specs=[rev(512), halo, rev(512), rev(512), rev(256), prev(256), rev(512), rev(D),
                  _layer(l, 4, 128, 128), _layer(l, 1, 512), pl.BlockSpec(memory_space=pltpu.SMEM),
                  pl.BlockSpec((None, N_HEADS, BLK, BLK), lambda s: (jnp.minimum(NB - 1 - s, 1), 0, 0, 0)),
                  _whole((BLK, BLK))] + [ANY] * len(deps),
        out_specs=[rev(D_IN), _layer(l, 4, 128, 128),
                   pl.BlockSpec((4, 128), lambda s: (0, 0)), pl.BlockSpec((8, 128), lambda s: (0, 0))],
        out_shape=[jax.ShapeDtypeStruct((S, D_IN), BF16), jax.ShapeDtypeStruct((DEPTH, 4, 128, 128), F32),
                   jax.ShapeDtypeStruct((4, 128), F32), jax.ShapeDtypeStruct((8, 128), F32)],
        input_output_aliases={} if dpw_dest is None else {12 + len(deps): 1},
        scratch_shapes=[pltpu.VMEM((POOL_ROWS, 512), F32)] * 4 + [pltpu.VMEM((BLK, 256), F32)],
        compiler_params=_params(),
    )(pu, pu, pg, q, kv, kv, ag, dcat, pool_w, pool_scale, sinks, bias, tri, *deps)


def _bwd_in_dw(l, dproj, x, g_pre, place_arr, deps=()):
    n_steps = S // TM

    def body(dp_ref, x_ref, g_ref, place_ref, *rest):
        own_ref, dwb_ref, dw_ref = rest[len(deps):]
        step = pl.program_id(0)

        @pl.when(step == 0)
        def _():
            dw_ref[...] = jnp.zeros_like(dw_ref)

        xt = x_ref[...]
        r = lax.rsqrt(jnp.mean(xt * xt, axis=-1, keepdims=True) + EPS)
        h = (xt * r * g_ref[...]).astype(BF16)
        dw_ref[...] += lax.dot_general(dp_ref[...], h, TN, preferred_element_type=F32)

        @pl.when(step == n_steps - 1)
        def _():
            dwb_ref[...] = dw_ref[...].astype(BF16)
            own_ref[...] = _own_piece(dw_ref, place_ref)

    row = lambda w: pl.BlockSpec((TM, w), lambda i: (i, 0))
    full = _whole
    return pl.pallas_call(
        body, name="bwd_in_dw", grid=(n_steps,),
        in_specs=[row(D_IN), row(D), _layer(l, 1, D), pl.BlockSpec(memory_space=pltpu.SMEM)] + [ANY] * len(deps),
        out_specs=[full((D_IN // 8, D)), full((D_IN, D))],
        out_shape=[jax.ShapeDtypeStruct((D_IN // 8, D), F32), jax.ShapeDtypeStruct((D_IN, D), BF16)],
        scratch_shapes=[pltpu.VMEM((D_IN, D), F32)],
        compiler_params=_params(),
    )(dproj, x, g_pre, place_arr, *deps)


def _bwd_in_dx(l, dproj, w_in_t, x, g_pre, dres, deps=(), dw_place=None):
    n_steps = S // TM
    with_dw = dw_place is not None

    def body(dp_ref, w_ref, x_ref, g_ref, dres_ref, *rest):
        place_ref = rest[0] if with_dw else None
        rest = rest[with_dw + len(deps):]
        if with_dw:
            dx_ref, dg_ref, own_ref, dwb_ref, acc_ref, dw_ref = rest
        else:
            dx_ref, dg_ref, acc_ref = rest
        step = pl.program_id(0)

        @pl.when(step == 0)
        def _():
            acc_ref[...] = jnp.zeros_like(acc_ref)
            if with_dw:
                dw_ref[...] = jnp.zeros_like(dw_ref)

        g = g_ref[...]
        halves = [slice(k * (TM // 2), (k + 1) * (TM // 2)) for k in range(2)]
        dh = [jnp.dot(dp_ref[rows, :], w_ref[...], preferred_element_type=F32) for rows in halves]
        h = []
        for rows, dh_k in zip(halves, dh):
            xt = x_ref[rows, :]
            r = lax.rsqrt(jnp.mean(xt * xt, axis=-1, keepdims=True) + EPS)
            xn = xt * r
            acc_ref[...] += _rows8(dh_k * xn)
            a = dh_k * g
            dx_ref[rows, :] = dres_ref[rows, :] + (
                r * a - xt * (r * r * r) * jnp.mean(a * xt, axis=-1, keepdims=True))
            h.append((xn * g).astype(BF16))
        if with_dw:
            dw_ref[...] += lax.dot_general(dp_ref[...], jnp.concatenate(h, axis=0), TN, preferred_element_type=F32)

        @pl.when(step == n_steps - 1)
        def _():
            _store_lane_rows(dg_ref, acc_ref[...])
            if with_dw:
                dwb_ref[...] = dw_ref[...].astype(BF16)
                own_ref[...] = _own_piece(dw_ref, place_ref)

    row = lambda w: pl.BlockSpec((TM, w), lambda i: (i, 0))
    full = _whole
    dw_specs = [full((D_IN // 8, D)), full((D_IN, D))] if with_dw else []
    dw_shapes = [jax.ShapeDtypeStruct((D_IN // 8, D), F32), jax.ShapeDtypeStruct((D_IN, D), BF16)] if with_dw else []
    return pl.pallas_call(
        body, name="bwd_in" if with_dw else "bwd_in_dx", grid=(n_steps,),
        in_specs=[row(D_IN), full((D_IN, D)), row(D), _layer(l, 1, D), row(D)]
        + [pl.BlockSpec(memory_space=pltpu.SMEM)] * with_dw + [ANY] * len(deps),
        out_specs=[row(D), full((8, 128))] + dw_specs,
        out_shape=[jax.ShapeDtypeStruct((S, D), F32), jax.ShapeDtypeStruct((8, 128), F32)] + dw_shapes,
        scratch_shapes=[pltpu.VMEM((8, D), F32)] + [pltpu.VMEM((D_IN, D), F32)] * with_dw,
        compiler_params=_params(),
    )(dproj, w_in_t, x, g_pre, dres, *((dw_place,) if with_dw else ()), *deps)


HBM =pl.BlockSpec(memory_space=pltpu.HBM)
SEM = pl.BlockSpec(memory_space=pltpu.SEMAPHORE)
SPLIT_COPY = pltpu.CompilerParams(has_side_effects=pltpu.SideEffectType.DATAFLOW_SIDE_EFFECTING)


def _in_hbm(a):
    return pltpu.with_memory_space_constraint(a, pltpu.HBM)

def _place():
    return lax.axis_index("x"), lax.axis_index("y"), lax.axis_index("c")


def _other_chips(x, y):
    return [(1 - x, y), (x, 1 - y), (1 - x, 1 - y)]


def _peer(x, y, c, m):
    return (x ^ (m >> 2), y ^ ((m >> 1) & 1), c ^ (m & 1))


def _place_cast(name, src, chip_arr, tile, layers, deps=()):
    _, n, cols = src.shape
    steps = n // tile
    k = len(layers)

    def body(chip_ref, *refs):
        for s_ref, o_ref in zip(refs[:k], refs[k + len(deps):]):
            o_ref[...] = s_ref[...].astype(BF16)

    def layer_spec(l):
        return pl.BlockSpec((None, tile, cols), lambda i, chip: (l, i, 0))

    return pl.pallas_call(
        body, name=name,
        grid_spec=pltpu.PrefetchScalarGridSpec(
            num_scalar_prefetch=1, grid=(steps,),
            in_specs=[layer_spec(l) for l in layers] + [ANY] * len(deps),
            out_specs=[pl.BlockSpec((tile, cols), lambda i, chip: (chip[0] * steps + i, 0))] * k),
        out_shape=[jax.ShapeDtypeStruct((N_SHARDS * n, cols), BF16)] * k,
        compiler_params=_params(),
    )(chip_arr, *[src] * k, *deps)


def _chip_rows(ref, chip, half=None):
    n = ref.shape[0] // N_SHARDS
    if half is None:
        return ref.at[pl.ds(pl.multiple_of(chip * n, 16), n), :]
    return ref.at[pl.ds(pl.multiple_of(chip * n + half * (n // 2), 16), n // 2), :]


def _gather_start(name, bufs, halved):
    n = len(bufs)

    def body(*refs):
        ins, send, recv, token = refs[:n], refs[n:2 * n], refs[2 * n:3 * n], refs[-1]
        x, y, c = _place()
        for a, buf in enumerate(ins):
            own = _chip_rows(buf, 2 * x + y, c if a in halved else None)
            for j, chip in enumerate(_other_chips(x, y)):
                pltpu.make_async_remote_copy(src_ref=own, dst_ref=own, send_sem=send[a].at[j], recv_sem=recv[a].at[j],
                                             device_id=(*chip, c), device_id_type=MESH).start()
        token[...] = jnp.zeros_like(token)

    outs = pl.pallas_call(
        body, name=name, in_specs=[HBM] * n,
        out_specs=[SEM] * (2 * n) + [HBM] * n + [pl.BlockSpec(memory_space=pltpu.VMEM)],
        out_shape=[pltpu.SemaphoreType.DMA((3,))] * (2 * n) + [pltpu.HBM(b.shape, b.dtype) for b in bufs]
        + [jax.ShapeDtypeStruct((8, 128), F32)],
        input_output_aliases={a: 2 * n + a for a in range(n)},
        compiler_params=SPLIT_COPY,
    )(*[_in_hbm(b) for b in bufs])
    return outs[:n], outs[n:2 * n], outs[2 * n:3 * n], outs[-1]


def _gather_wait(name, buf, send_sem, recv_sem, after, halved=False):
    def body(buf_ref, send_ref, recv_ref, *rest):
        x, y, c = _place()
        half = c if halved else None
        own = _chip_rows(buf_ref, 2 * x + y, half)
        for j, chip in enumerate(_other_chips(x, y)):
            copy = pltpu.make_async_remote_copy(src_ref=own, dst_ref=_chip_rows(buf_ref, 2 * chip[0] + chip[1], half),
                                                send_sem=send_ref.at[j], recv_sem=recv_ref.at[j],
                                                device_id=(*chip, c), device_id_type=MESH)
            copy.wait_send()
            copy.wait_recv()

    return pl.pallas_call(
        body, name=name, in_specs=[HBM, SEM, SEM] + [ANY] * len(after), out_specs=HBM,
        out_shape=pltpu.HBM(buf.shape, buf.dtype), input_output_aliases={0: 0}, compiler_params=SPLIT_COPY,
    )(buf, send_sem, recv_sem, *after)


def _forward_halves(name, buf):
    def body(in_ref, out_ref, send_sems, recv_sems):
        x, y, c = _place()

        def copy(j, chip, half):
            rows = 2 * chip[0] + chip[1]
            return pltpu.make_async_remote_copy(
                src_ref=_chip_rows(in_ref, rows, half), dst_ref=_chip_rows(out_ref, rows, half), send_sem=send_sems.at[j],
                recv_sem=recv_sems.at[j], device_id=(x, y, 1 - c), device_id_type=MESH)

        chips = _other_chips(x, y)
        for j, chip in enumerate(chips):
            copy(j, chip, c).start()
        for j, chip in enumerate(chips):
            copy(j, chip, c).wait_send()
            copy(j, chip, 1 - c).wait_recv()

    return pl.pallas_call(
        body, name=name, in_specs=[ANY], out_specs=ANY, out_shape=jax.ShapeDtypeStruct(buf.shape, buf.dtype),
        input_output_aliases={0: 0},
        scratch_shapes=[pltpu.SemaphoreType.DMA((3,))] * 2,
    )(buf)


def _piece_rows(ref, k):
    p = ref.shape[0] // 8
    return ref.at[pl.ds(pl.multiple_of(k * p, 32 // jnp.dtype(ref.dtype).itemsize), p), :]


def _exchange_start(name, arrays):
    n = len(arrays)
    zones = [lax.empty((7, a.shape[0] // 8, a.shape[1]), a.dtype) for a in arrays]

    def body(*refs):
        srcs, lands = refs[:n], refs[n:2 * n]
        send, recv, token = refs[2 * n:3 * n], refs[3 * n:4 * n], refs[-1]
        x, y, c = _place()
        for a, (src, land) in enumerate(zip(srcs, lands)):
            for m in range(1, 8):
                px, py, pc = _peer(x, y, c, m)
                pltpu.make_async_remote_copy(
                    src_ref=_piece_rows(src, 4 * px + 2 * py + pc), dst_ref=land.at[m - 1], send_sem=send[a].at[m - 1],
                    recv_sem=recv[a].at[m - 1], device_id=(px, py, pc), device_id_type=MESH).start()
        token[...] = jnp.zeros_like(token)

    outs = pl.pallas_call(
        body, name=name, in_specs=[HBM] * (2 * n),
        out_specs=[SEM] * (2 * n) + [HBM] * (2 * n) + [pl.BlockSpec(memory_space=pltpu.VMEM)],
        out_shape=[pltpu.SemaphoreType.DMA((7,))] * (2 * n) + [pltpu.HBM(a.shape, a.dtype) for a in arrays + zones]
        + [jax.ShapeDtypeStruct((8, 128), F32)],
        input_output_aliases={a: 2 * n + a for a in range(2 * n)},
        compiler_params=SPLIT_COPY,
    )(*[_in_hbm(a) for a in arrays + zones])
    return outs[:n], outs[n:2 * n], outs[2 * n:3 * n], outs[3 * n:4 * n], outs[-1]


def _exchange_wait(name, started, after):
    send_sems, recv_sems, arrays, zones, _ = started
    n = len(arrays)

    def body(*refs):
        srcs, lands = refs[:n], refs[n:2 * n]
        send, recv = refs[2 * n:3 * n], refs[3 * n:4 * n]
        x, y, c = _place()
        for a, (src, land) in enumerate(zip(srcs, lands)):
            for m in range(1, 8):
                px, py, pc = _peer(x, y, c, m)
                copy = pltpu.make_async_remote_copy(
                    src_ref=_piece_rows(src, 4 * px + 2 * py + pc), dst_ref=land.at[m - 1], send_sem=send[a].at[m - 1],
                    recv_sem=recv[a].at[m - 1], device_id=(px, py, pc), device_id_type=MESH)
                copy.wait_send()
                copy.wait_recv()

    outs = pl.pallas_call(
        body, name=name, in_specs=[HBM] * (2 * n) + [SEM] * (2 * n) + [ANY], out_specs=[HBM] * (2 * n),
        out_shape=[pltpu.HBM(a.shape, a.dtype) for a in list(arrays) + list(zones)],
        input_output_aliases={a: a for a in range(2 * n)}, compiler_params=SPLIT_COPY,
    )(*arrays, *zones, *send_sems, *recv_sems, after)
    return outs[n:]


def _sum_pieces(name, weights, place_arr, dests=None):
    steps = 2
    flat = [item for items in weights for item in items]
    n = len(flat)

    def body(place_ref, *refs):
        outs = iter(refs[len(refs) - len(weights):])
        k = 0
        for items in weights:
            out_ref = next(outs)
            for layer, _, _ in items:
                total = refs[k][...]
                for m in range(7):
                    total = total + refs[n + k][m].astype(F32)
                if len(items) == DEPTH:
                    out_ref[layer] = total
                else:
                    out_ref[...] = total
                k += 1

    def out_spec(items):
        _, own, _ = items[0]
        t, cols = own.shape[0] // steps, own.shape[1]
        if len(items) == DEPTH:
            return pl.BlockSpec((DEPTH, t, cols), lambda i, place: (0, place[1] * steps + i, 0))
        layer = items[0][0]
        return pl.BlockSpec((None, t, cols), lambda i, place: (layer, place[1] * steps + i, 0))

    owns = [own for _, own, _ in flat]
    dests = [] if dests is None else list(dests)
    return pl.pallas_call(
        body, name=name,
        grid_spec=pltpu.PrefetchScalarGridSpec(
            num_scalar_prefetch=1, grid=(steps,),
            in_specs=[pl.BlockSpec((o.shape[0] // steps, o.shape[1]), lambda i, place: (i, 0)) for o in owns]
            + [pl.BlockSpec((7, o.shape[0] // steps, o.shape[1]), lambda i, place: (0, i, 0)) for o in owns]
            + [ANY] * len(dests),
            out_specs=[out_spec(items) for items in weights]),
        out_shape=[jax.ShapeDtypeStruct((DEPTH, 2 * items[0][1].shape[0], items[0][1].shape[1]), F32)
                   for items in weights],
        input_output_aliases={1 + 2 * n + k: k for k in range(len(dests))},
        compiler_params=_params(),
    )(place_arr, *owns, *[recv for _, _, recv in flat], *dests)


def _sum_small(partials, recvs, place_arr):
    n = len(partials)

    def body(place_ref, *refs):
        for o_ref, r_ref, out_ref in zip(refs[:n], refs[n:2 * n], refs[2 * n:]):
            total = o_ref[...]
            for m in range(7):
                total = total + r_ref[m]
            out_ref[...] = total

    piece = lambda a: pl.BlockSpec((a.shape[0] // 8, a.shape[1]), lambda i, place: (place[0], 0))
    return pl.pallas_call(
        body, name="sum_small",
        grid_spec=pltpu.PrefetchScalarGridSpec(
            num_scalar_prefetch=1, grid=(1,),
            in_specs=[piece(a) for a in partials] + [pl.BlockSpec(r.shape, lambda i, place: (0, 0, 0)) for r in recvs],
            out_specs=[piece(a) for a in partials]),
        out_shape=[jax.ShapeDtypeStruct(a.shape, F32) for a in partials],
        compiler_params=_params(),
    )(place_arr, *partials, *recvs)


def _share(name, bufs, parts, gathered=()):
    n, n_g = len(bufs), len(gathered)
    total = n + n_g

    def body(*refs):
        ins, outs = refs[:total], refs[total:2 * total]
        send_sems, recv_sems, send_g, recv_g = refs[2 * total:]
        x, y, c = _place()

        def half(ref, l, which):
            p = ref.shape[1] // 2
            return ref.at[l, pl.ds(pl.multiple_of(which * p, 8), p), :]

        def swap(k, which):
            a, l = parts[k]
            return pltpu.make_async_remote_copy(
                src_ref=half(ins[a], l, which), dst_ref=half(outs[a], l, which), send_sem=send_sems.at[k],
                recv_sem=recv_sems.at[k], device_id=(x, y, 1 - c), device_id_type=MESH)

        def spread(a, m, sender):
            k = 4 * sender[0] + 2 * sender[1] + sender[2]
            return pltpu.make_async_remote_copy(
                src_ref=_piece_rows(ins[n + a], k), dst_ref=_piece_rows(outs[n + a], k), send_sem=send_g.at[7 * a + m - 1],
                recv_sem=recv_g.at[7 * a + m - 1], device_id=_peer(x, y, c, m), device_id_type=MESH)

        for k in range(len(parts)):
            swap(k, c).start()
        for a in range(n_g):
            for m in range(1, 8):
                spread(a, m, (x, y, c)).start()
        for k in range(len(parts)):
            swap(k, c).wait_send()
            swap(k, 1 - c).wait_recv()
        for a in range(n_g):
            for m in range(1, 8):
                spread(a, m, (x, y, c)).wait_send()
                spread(a, m, _peer(x, y, c, m)).wait_recv()

    arrays = list(bufs) + list(gathered)
    return pl.pallas_call(
        body, name=name, in_specs=[ANY] * total, out_specs=[ANY] * total,
        out_shape=[jax.ShapeDtypeStruct(b.shape, F32) for b in arrays],
        input_output_aliases={a: a for a in range(total)},
        scratch_shapes=[pltpu.SemaphoreType.DMA((max(len(parts), 1),))] * 2
        + [pltpu.SemaphoreType.DMA((max(7 * n_g, 1),))] * 2,
    )(*arrays)


def _adamw_math(w, g, m, v):
    nm = ADAM_B1 * m + (1.0 - ADAM_B1) * g
    nv = ADAM_B2 * v + (1.0 - ADAM_B2) * (g * g)
    m_hat = nm / (1.0 - ADAM_B1 ** ADAM_STEP)
    v_hat = nv / (1.0 - ADAM_B2 ** ADAM_STEP)
    return -ADAM_LR * (m_hat / (jnp.sqrt(v_hat) + ADAM_EPS) + ADAM_WD * w), nm, nv


def _adamw(name, w, g, m, v, rows_per_step, first=0, count=None, dests=None, deps=()):
    layers, rows, cols = w.shape
    count = layers if count is None else count

    def body(w_ref, g_ref, m_ref, v_ref, *rest):
        d_ref, nm_ref, nv_ref, g_out_ref = rest[-4:]
        d_ref[...], nm_ref[...], nv_ref[...] = _adamw_math(w_ref[...], g_ref[...], m_ref[...], v_ref[...])
        g_out_ref[...] = g_ref[...]

    spec = pl.BlockSpec((1, rows_per_step, cols), lambda l, i: (first + l, i, 0))
    shape = jax.ShapeDtypeStruct(w.shape, F32)
    dests = () if dests is None else tuple(dests)
    return pl.pallas_call(
        body, name=name, grid=(count, rows // rows_per_step),
        in_specs=[spec] * 4 + [ANY] * (len(dests) + len(deps)), out_specs=[spec] * 4, out_shape=[shape] * 4,
        input_output_aliases={4 + k: k for k in range(len(dests))},
        compiler_params=_params(("arbitrary", "arbitrary")),
    )(w, g, m, v, *dests, *deps)


def _pack_misc(pool_scale, sinks, norm_pre, norm_post):
    sink_rows = jnp.zeros((DEPTH, 8, 128), F32).at[:, 0, 0:N_HEADS].set(sinks).reshape(2 * 8, 128)
    return jnp.concatenate([pool_scale.reshape(8, 128), norm_pre.reshape(16, 128), norm_post.reshape(16, 128),
                            sink_rows, jnp.zeros((8, 128), F32)], axis=0)


def _adamw_small(w, g, m, v, pool):
    def body(w_ref, g_ref, m_ref, v_ref, pw_ref, pg_ref, pm_ref, pv_ref, *rest):
        outs, pool_outs, (d_ref, nm_ref, nv_ref) = rest[:17], rest[17:21], rest[21:]
        pool_outs[0][...] = pg_ref[...]
        pool_outs[1][...], pool_outs[2][...], pool_outs[3][...] = _adamw_math(
            pw_ref[...], pg_ref[...], pm_ref[...], pv_ref[...])
        d_ref[...], nm_ref[...], nv_ref[...] = _adamw_math(w_ref[...], g_ref[...], m_ref[...], v_ref[...])
        for k, src in enumerate([g_ref, d_ref, nm_ref, nv_ref]):
            scale, sinks, pre, post = outs[4 * k:4 * k + 4]
            for l in range(DEPTH):
                for j in range(4):
                    scale[l:l + 1, j * 128:(j + 1) * 128] = src[MISC_SCALE + 4 * l + j:MISC_SCALE + 4 * l + j + 1, :]
                for j in range(8):
                    pre[l:l + 1, j * 128:(j + 1) * 128] = src[MISC_PRE + 8 * l + j:MISC_PRE + 8 * l + j + 1, :]
                    post[l:l + 1, j * 128:(j + 1) * 128] = src[MISC_POST + 8 * l + j:MISC_POST + 8 * l + j + 1, :]
                sinks[l:l + 1, :] = src[MISC_SINKS + 8 * l:MISC_SINKS + 8 * l + 1, 0:N_HEADS]
        outs[16][...] = g_ref[MISC_LOSS:MISC_LOSS + 1, 0:1]

    vmem = pl.BlockSpec(memory_space=pltpu.VMEM)
    shapes = [(DEPTH, D_POOL), (DEPTH, N_HEADS), (DEPTH, D), (DEPTH, D)] * 4 + [(1, 1)]
    shapes += [pool[0].shape] * 4
    return pl.pallas_call(
        body, name="adamw_small", in_specs=[vmem] * 8, out_specs=[vmem] * 21,
        out_shape=[jax.ShapeDtypeStruct(s, F32) for s in shapes],
        scratch_shapes=[pltpu.VMEM((MISC_ROWS, 128), F32)] * 3,
    )(w, g, m, v, *pool)


def kernel(x, w_in, pool_w, pool_scale, attn_sinks, w_out, norm_pre, norm_post, loss_target, m_w_in, m_pool_w, m_pool_scale, m_attn_sinks, m_w_out, m_norm_pre, m_norm_post, v_w_in, v_pool_w, v_pool_scale, v_attn_sinks, v_w_out, v_norm_pre, v_norm_post):
    cx, cy, cc = _place()
    chip_arr = jnp.reshape(2 * cx + cy, (1,)).astype(jnp.int32)
    place_arr = jnp.stack([4 * cx + 2 * cy + cc, cc]).astype(jnp.int32)
    t = lambda a: jnp.transpose(a, (0, 2, 1))
    w_in_t = t(w_in)
    xs, target = x[0], loss_target[0]
    pool_w_b = pool_w.astype(BF16)
    tables = _attention_tables()
    scale3 = pool_scale.reshape(DEPTH, 1, D_POOL)
    pre3 = norm_pre.reshape(DEPTH, 1, D)
    post3 = norm_post.reshape(DEPTH, 1, D)

    (wi0,) = _place_cast("place_w_in0", w_in_t, chip_arr, 288, [0])
    first = _gather_start("gather_start_first", [wi0], halved=(0,))
    (wi1,) = _place_cast("place_w_in1", w_in_t, chip_arr, 288, [1], deps=(first[3],))
    wo = _place_cast("place_w_out", w_out, chip_arr, 256, [0, 1], deps=(first[3],))
    rest = _gather_start("gather_start_rest", [wi1, wo[0], wo[1]], halved=(0,))
    send, recv, bufs = [first[k] + rest[k] for k in range(3)]
    order = {(0, "in"): 0, (1, "in"): 1, (0, "out"): 2, (1, "out"): 3}

    saved = []
    packed = [_pack_misc(pool_scale, attn_sinks, norm_pre, norm_post),
              _pack_misc(m_pool_scale, m_attn_sinks, m_norm_pre, m_norm_post),
              _pack_misc(v_pool_scale, v_attn_sinks, v_norm_pre, v_norm_post)]
    after = (first[3], rest[3], pool_w_b, *tables, scale3, pre3, post3, *packed)
    below = None
    for l in range(DEPTH):
        k = order[l, "in"]
        w_in_l = _forward_halves(f"forward_w_in{l}", _gather_wait(f"gather_wait_in{l}", bufs[k], send[k], recv[k], after,
                                                                 halved=True))
        if below is None:
            pu, pg, q, kv, ag = _fwd_in(l, xs, pre3, w_in_l)
        else:
            y, xs, pu, pg, q, kv, ag = _fwd_in(l, xs, pre3, w_in_l, below)
            saved[l - 1][7] = y
        cat = _fwd_mix(l, pu, pg, q, kv, ag, pool_w_b, scale3, attn_sinks, tables)
        k = order[l, "out"]
        w_out_l = _gather_wait(f"gather_wait_out{l}", bufs[k], send[k], recv[k], (cat,))
        saved.append([xs, pu, pg, q, kv, ag, cat, None, w_in_l, w_out_l])
        below, after = (cat, w_out_l, post3), (w_out_l,)

    x_in, pu, pg, q, kv, ag, cat, y, w_in_l, w_out_l = saved[1]
    dcat, dw_out1, dw_out1_b, dg_post1, loss, xs = _bwd_out(1, cat, w_out_l, post3, place_arr, x=x_in, target=target)
    ex1_out = _exchange_start("exchange_start_out1", [dw_out1_b])
    dproj, dpw, dsc1, dsink1 = _bwd_mix(1, pu, pg, q, kv, ag, dcat, pool_w_b, scale3, attn_sinks, tables,
                                        deps=(ex1_out[4],))
    dx, dg_pre1, dw_in1, dw_in1_b = _bwd_in_dx(1, dproj, w_in_l, x_in, pre3, xs, dw_place=place_arr)
    ex1_in = _exchange_start("exchange_start_in1", [dw_in1_b])

    x_in, pu, pg, q, kv, ag, cat, y, w_in_l, w_out_l = saved[0]
    dcat, dw_out0, dw_out0_b, dg_post0 = _bwd_out(0, cat, w_out_l, post3, place_arr, dxn=dx, y=y, deps=(ex1_in[4],))
    ex0_out = _exchange_start("exchange_start_out0", [dw_out0_b])
    dproj, dpw, dsc0, dsink0 = _bwd_mix(0, pu, pg, q, kv, ag, dcat, pool_w_b, scale3, attn_sinks, tables,
                                        deps=(ex0_out[4],), dpw_dest=dpw)
    dw_in0, dw_in0_b = _bwd_in_dw(0, dproj, x_in, pre3, place_arr)
    ex0_in = _exchange_start("exchange_start_in0", [dw_in0_b])

    grad_x, dg_pre0 = _bwd_in_dx(0, dproj, w_in_l, x_in, pre3, dx, deps=(ex0_in[4],))
    small = [dpw.reshape(DEPTH * 4 * 128, 128),
             jnp.concatenate([dsc0, dsc1, dg_pre0, dg_pre1, dg_post0, dg_post1, dsink0, dsink1, loss], axis=0)]
    ex_small = _exchange_start("exchange_start_small", small)
    (recv_out1,) = _exchange_wait("exchange_wait_out1", ex1_out, ex_small[4])
    (recv_in1,) = _exchange_wait("exchange_wait_in1", ex1_in, recv_out1)
    (recv_out0,) = _exchange_wait("exchange_wait_out0", ex0_out, recv_in1)
    g_in, g_out = _sum_pieces("sum_pieces_a", [[(1, dw_in1, recv_in1)],
                                               [(0, dw_out0, recv_out0), (1, dw_out1, recv_out1)]], place_arr)
    g_in, g_out = _share("share_a", [g_in, g_out], [(0, 1), (1, 0), (1, 1)])
    m_in_t, v_in_t = t(m_w_in), t(v_w_in)
    d_out, nm_out, nv_out, grad_w_out = _adamw("adamw_w_out", w_out, g_out, m_w_out, v_w_out, 256)
    upd_in = _adamw("adamw_w_in1", w_in_t, g_in, m_in_t, v_in_t, 288, first=1, count=1, deps=(d_out,))

    (recv_in0,) = _exchange_wait("exchange_wait_in0", ex0_in, upd_in[0])
    recv_small = _exchange_wait("exchange_wait_small", ex_small, recv_in0)
    (g_in,) = _sum_pieces("sum_pieces_in0", [[(0, dw_in0, recv_in0)]], place_arr, dests=[g_in])
    g_in, g_pw, g_misc = _share("share_b", [g_in], [(0, 0)], _sum_small(small, recv_small, place_arr))
    d_in, nm_in, nv_in, grad_w_in_t = _adamw("adamw_w_in0", w_in_t, g_in, m_in_t, v_in_t, 288, first=0, count=1,
                                             dests=upd_in)
    flat = lambda a: a.reshape(DEPTH * 4 * 128, 128)
    small_out = _adamw_small(packed[0], g_misc, packed[1], packed[2],
                             (flat(pool_w), g_pw, flat(m_pool_w), flat(v_pool_w)))
    (g_sc, g_sk, g_pre, g_post, d_sc, d_sk, d_pre, d_post,
     m_sc, m_sk, m_pre, m_post, v_sc, v_sk, v_pre, v_post, loss_sum) = small_out[:17]
    g_pw, d_pw, m_pw, v_pw = [a.reshape(pool_w.shape) for a in small_out[17:]]
    return (loss_sum[0, 0], grad_x[None], t(grad_w_in_t), g_pw, g_sc, g_sk, grad_w_out, g_pre, g_post,
            t(d_in), d_pw, d_sc, d_sk, d_out, d_pre, d_post,
            t(nm_in), m_pw, m_sc, m_sk, nm_out, m_pre, m_post,
            t(nv_in), v_pw, v_sc, v_sk, nv_out, v_pre, v_post)
```

```python
import jax
import jax.numpy as jnp
from jax import lax
from jax.experimental import pallas as pl
from jax.experimental.pallas import tpu as pltpu

F32 = jnp.float32
BF16 = jnp.bfloat16

S = 2048
D = 1024
DEPTH = 2
D_POOL = 512
POOL_WINDOWS = (2, 4, 8, 16)
N_HEADS = 8
D_IN = 2304
N_SHARDS = 4
W_IN_SHARD = D_IN // N_SHARDS
W_OUT_SHARD = D // N_SHARDS
BLK = 128
NB = S // BLK
HALO = 16
PAD = 8
EPS = 1e-6
NEG_INF = -1e30
C_PU, C_PG, C_Q, C_K, C_V, C_AG = 0, 512, 1024, 1536, 1664, 1792

ADAM_LR = 0.001
ADAM_B1 = 0.9
ADAM_B2 = 0.999
ADAM_EPS = 1e-08
ADAM_WD = 0.01
ADAM_STEP = 10

TM = 512
VMEM_LIMIT = 56 * 1024 * 1024

NT = (((1,), (1,)), ((), ()))
TN = (((0,), (0,)), ((), ()))

MESH = pl.DeviceIdType.MESH
ANY = pl.BlockSpec(memory_space=pl.ANY)

MISC_SCALE, MISC_PRE, MISC_POST, MISC_SINKS, MISC_LOSS = 0, 8, 24, 40, 56
MISC_ROWS = 64


def _params(sem=("arbitrary",)):
    return pltpu.CompilerParams(dimension_semantics=sem, vmem_limit_bytes=VMEM_LIMIT)


def _sigmoid(v):
    return 1.0 / (1.0 + jnp.exp(-v))


def _rows8(v):
    r, c = v.shape
    return v.reshape(r // 8, 8, c).sum(axis=0)


def _layer(l, *shape):
    zeros = (0,) * len(shape)
    return pl.BlockSpec((None,) + shape, lambda i: (l,) + zeros)


def _whole(shape):
    zeros = (0,) * len(shape)
    return pl.BlockSpec(shape, lambda i: zeros, pipeline_mode=pl.Buffered(1))


def _fwd_in(l, x, g_pre, w_in_t, below=None):
    fused = below is not None

    def body(x_ref, g_ref, w_ref, *rest):
        if fused:
            cat_ref, wo_ref, gp_ref, y_ref, xn_ref = rest[:5]
            y = jnp.dot(cat_ref[...], wo_ref[...], preferred_element_type=F32)
            y_ref[...] = y
            xt = x_ref[...] + y * lax.rsqrt(jnp.mean(y * y, axis=-1, keepdims=True) + EPS) * gp_ref[...]
            xn_ref[...] = xt
        else:
            xt = x_ref[...]
        pu_ref, pg_ref, q_ref, kv_ref, ag_ref = rest[-5:]
        r = lax.rsqrt(jnp.mean(xt * xt, axis=-1, keepdims=True) + EPS)
        h = (xt * r * g_ref[...]).astype(BF16)

        def proj(lo, hi):
            return lax.dot_general(h, w_ref[lo:hi, :], NT, preferred_element_type=F32)

        pu_ref[...] = proj(C_PU, C_PG)
        pg_ref[...] = proj(C_PG, C_Q)
        q_ref[...] = proj(C_Q, C_K).astype(BF16)
        kv_ref[...] = proj(C_K, C_AG).astype(BF16)
        ag_ref[...] = proj(C_AG, D_IN)

    row = lambda w: pl.BlockSpec((TM, w), lambda i: (i, 0))
    act = jax.ShapeDtypeStruct((S, D), F32)
    return pl.pallas_call(
        body, name="fwd_out_in" if fused else "fwd_in", grid=(S // TM,),
        in_specs=[row(D), _layer(l, 1, D), _whole((D_IN, D))]
        + ([row(D), _whole((D, D)), _layer(l - 1, 1, D)] if fused else []),
        out_specs=[row(D)] * (2 * fused) + [row(512), row(512), row(512), row(256), row(512)],
        out_shape=[act] * (2 * fused)
        + [jax.ShapeDtypeStruct((S, 512), F32), jax.ShapeDtypeStruct((S, 512), F32),
           jax.ShapeDtypeStruct((S, 512), BF16), jax.ShapeDtypeStruct((S, 256), BF16),
           jax.ShapeDtypeStruct((S, 512), F32)],
        compiler_params=_params(),
    )(x, g_pre, w_in_t, *(below if fused else ()))


LOG2E = 1.4426950408889634
SCORE_SCALE = 0.125 * LOG2E


def _attention_tables():
    qi = jnp.arange(BLK)[:, None]
    kj = jnp.arange(BLK)[None, :]
    dist = ((qi - kj) % BLK).astype(F32)
    slopes = jnp.exp2(-jnp.arange(1, N_HEADS + 1, dtype=F32))
    bias = -(slopes * LOG2E)[:, None, None] * dist[None]
    first = jnp.where(kj > qi, NEG_INF, bias)
    return jnp.stack([first, bias]), (kj <= qi).astype(BF16)


def _own_block_mask():
    return lax.broadcasted_iota(jnp.int32, (BLK, BLK), 1) <= lax.broadcasted_iota(jnp.int32, (BLK, BLK), 0)


def _merge(full, own):
    return jnp.where(own, full[:, BLK:], full[:, :BLK])


def _spread(v, tri):
    own = v * tri
    return jnp.concatenate([v - own, own], axis=1)


def _head_variants(cur, prev):
    both = jnp.concatenate([prev, cur], axis=0).astype(F32)
    swapped = pltpu.roll(both, 64, axis=1)
    low = lax.broadcasted_iota(jnp.int32, both.shape, 1) < 64
    zero = jnp.zeros_like(both)
    return ((jnp.where(low, both, zero).astype(BF16), jnp.where(low, zero, swapped).astype(BF16)),
            (jnp.where(low, swapped, zero).astype(BF16), jnp.where(low, zero, both).astype(BF16)))


def _head_of(hkv, t, half):
    return hkv * 4 + 2 * t + half


def _rows(v, t):
    return v[t * BLK:(t + 1) * BLK]


def _stack_tiles(ref, hkv, offset=0):
    lo = offset + 2 * hkv * 128
    return jnp.concatenate([ref[:, lo:lo + 128], ref[:, lo + 128:lo + 256]], axis=0)


def _scores(q2, k_var, own):
    s = {}
    for hkv in range(2):
        for half in range(2):
            full = lax.dot_general(q2[hkv], k_var[hkv][half], NT, preferred_element_type=F32)
            for t in range(2):
                s[hkv, t, half] = _merge(_rows(full, t), own)
    return s


def _softmax(s, bias, sink):
    s = s * SCORE_SCALE + bias
    sink2 = sink * LOG2E
    m = jnp.maximum(jnp.max(s, axis=-1, keepdims=True), sink2)
    p = jnp.exp2(s - m)
    e_sink = jnp.exp2(sink2 - m)
    inv = 1.0 / (jnp.sum(p, axis=-1, keepdims=True) + e_sink)
    return p * inv, e_sink * inv


def _spread_pair(v, hkv, half, tri):
    return jnp.concatenate([_spread(v[hkv, t, half].astype(BF16), tri) for t in range(2)], axis=0)


POOL_ROWS = PAD + HALO + BLK


def _window_sums(src_ref, tmp_refs, trailing):
    lo, hi = (PAD, POOL_ROWS) if trailing else (0, HALO + BLK)
    cur = src_ref
    for level in range(len(POOL_WINDOWS)):
        lanes = slice(level * 128, 512)
        shift = -(1 << level) if trailing else (1 << level)
        dst = tmp_refs[level % 2]
        dst[lo:hi, lanes] = cur[lo:hi, lanes] + cur[lo + shift:hi + shift, lanes]
        cur = dst


def _pool_block(ext_ref, tmp_refs, i, g, w):
    lanes = slice(g * 128, (g + 1) * 128)
    rows = slice(PAD + HALO, POOL_ROWS)
    t = (i * BLK + lax.broadcasted_iota(jnp.int32, (BLK, 1), 0)).astype(F32)
    inv = 1.0 / jnp.minimum(t + 1.0, float(w))
    return tmp_refs[g % 2][rows, lanes] * inv - ext_ref[rows, lanes], inv


def _fwd_mix(l, pu, pg, q, kv, ag, pool_w, pool_scale, sinks, tables):
    bias, tri = tables

    def body(pu_ref, pup_ref, pg_ref, q_ref, kv_ref, kvp_ref, ag_ref, pw_ref, sc_ref, sink_ref, bias_ref, tri_ref,
             cat_ref, ext_ref, *tmp_refs):
        i = pl.program_id(0)

        @pl.when(i == 0)
        def _():
            for ref in (ext_ref, *tmp_refs):
                ref[0:PAD, :] = jnp.zeros((PAD, 512), F32)

        ext_ref[PAD:PAD + HALO, :] = jnp.where(i > 0, pup_ref[...], 0.0)
        ext_ref[PAD + HALO:POOL_ROWS, :] = pu_ref[...]
        _window_sums(ext_ref, tmp_refs, True)
        for g, w in enumerate(POOL_WINDOWS):
            lanes = slice(g * 128, (g + 1) * 128)
            pooled, _ = _pool_block(ext_ref, tmp_refs, i, g, w)
            mixed = jnp.dot(pooled.astype(BF16), pw_ref[g], preferred_element_type=F32)
            gate = pg_ref[:, lanes]
            cat_ref[:, lanes] = (mixed * sc_ref[:, lanes] * (gate * _sigmoid(gate))).astype(BF16)

        own = _own_block_mask()
        tri = tri_ref[...]
        k_var = _head_variants(kv_ref[:, 0:128], kvp_ref[:, 0:128])
        v_var = _head_variants(kv_ref[:, 128:256], kvp_ref[:, 128:256])
        s = _scores([_stack_tiles(q_ref, hkv) for hkv in range(2)], k_var, own)
        p = {}
        for (hkv, t, half), s_head in s.items():
            head = _head_of(hkv, t, half)
            p[hkv, t, half], _ = _softmax(s_head, bias_ref[head], sink_ref[l, head])
        for hkv in range(2):
            o2 = jnp.zeros((2 * BLK, 128), F32)
            for half in range(2):
                o2 = o2 + jnp.dot(_spread_pair(p, hkv, half, tri), v_var[hkv][half], preferred_element_type=F32)
            for t in range(2):
                lo = (2 * hkv + t) * 128
                gate = ag_ref[:, lo:lo + 128]
                cat_ref[:, D_POOL + lo:D_POOL + lo + 128] = (_rows(o2, t) * (gate * _sigmoid(gate))).astype(BF16)

    blk = lambda w: pl.BlockSpec((BLK, w), lambda i: (i, 0))
    prev = lambda w: pl.BlockSpec((BLK, w), lambda i: (jnp.maximum(i - 1, 0), 0))
    halo = pl.BlockSpec((HALO, 512), lambda i: (jnp.maximum(i * (BLK // HALO) - 1, 0), 0))
    return pl.pallas_call(
        body, name="fwd_mix", grid=(NB,),
        in_specs=[blk(512), halo, blk(512), blk(512), blk(256), prev(256), blk(512),
                  _layer(l, 4, 128, 128), _layer(l, 1, 512), pl.BlockSpec(memory_space=pltpu.SMEM),
                  pl.BlockSpec((None, N_HEADS, BLK, BLK), lambda i: (jnp.minimum(i, 1), 0, 0, 0)), _whole((BLK, BLK))],
        out_specs=blk(D),
        out_shape=jax.ShapeDtypeStruct((S, D), BF16),
        scratch_shapes=[pltpu.VMEM((POOL_ROWS, 512), F32)] * 3,
        compiler_params=_params(),
    )(pu, pu, pg, q, kv, kv, ag, pool_w, pool_scale, sinks, bias, tri)


def _store_lane_rows(ref, acc):
    total = jnp.sum(acc, axis=0, keepdims=True)
    for k in range(ref.shape[0]):
        ref[k:k + 1, :] = total[:, k * 128:(k + 1) * 128]


def _own_piece(dw_ref, place_ref):
    p = dw_ref.shape[0] // 8
    return dw_ref[pl.ds(pl.multiple_of(place_ref[0] * p, 8), p), :]


def _bwd_out(l, cat, w_out, g_post, place_arr, dxn=None, y=None, x=None, target=None, deps=()):
    last = target is not None
    n_steps = S // TM

    def body(a_ref, b_ref, g_ref, cat_ref, w_ref, place_ref, *rest):
        dcat_ref, own_ref, dwb_ref, dg_ref = rest[len(deps):len(deps) + 4]
        rest = rest[len(deps) + 4:]
        acc_ref, dw_ref = rest[-2:]
        step = pl.program_id(0)

        @pl.when(step == 0)
        def _():
            dw_ref[...] = jnp.zeros_like(dw_ref)
            acc_ref[...] = jnp.zeros_like(acc_ref)

        cat = cat_ref[...]
        g = g_ref[...]
        y = jnp.dot(cat, w_ref[...], preferred_element_type=F32) if last else b_ref[...]
        r = lax.rsqrt(jnp.mean(y * y, axis=-1, keepdims=True) + EPS)
        if last:
            loss_ref, dx_ref, loss_acc_ref = rest[:3]
            err = a_ref[...] + y * r * g - b_ref[...]

            @pl.when(step == 0)
            def _():
                loss_acc_ref[...] = jnp.zeros_like(loss_acc_ref)

            loss_acc_ref[...] += _rows8(err * err)
            dz = err * (1.0 / D)
            dx_ref[...] = dz
        else:
            dz = a_ref[...]
        a = dz * g
        dy = r * a - y * (r * r * r) * jnp.mean(a * y, axis=-1, keepdims=True)
        acc_ref[...] += _rows8(dz * (y * r))
        dyb = dy.astype(BF16)
        dcat_ref[...] = lax.dot_general(dyb, w_ref[...], NT, preferred_element_type=F32)
        dw_ref[...] += lax.dot_general(cat, dyb, TN, preferred_element_type=F32)

        @pl.when(step == n_steps - 1)
        def _():
            _store_lane_rows(dg_ref, acc_ref[...])
            dwb_ref[...] = dw_ref[...].astype(BF16)
            own_ref[...] = _own_piece(dw_ref, place_ref)
            if last:
                loss_ref[...] = jnp.full((8, 128), (0.5 / D) * jnp.sum(loss_acc_ref[...]), F32)

    row = lambda: pl.BlockSpec((TM, D), lambda i: (i, 0))
    full = _whole
    return pl.pallas_call(
        body, name="out_loss_bwd" if last else "bwd_out", grid=(n_steps,),
        in_specs=[row(), row(), _layer(l, 1, D), row(), full((D, D)), pl.BlockSpec(memory_space=pltpu.SMEM)]
        + [ANY] * len(deps),
        out_specs=[row(), full((D // 8, D)), full((D, D)), full((8, 128))] + ([full((8, 128)), row()] if last else []),
        out_shape=[jax.ShapeDtypeStruct((S, D), F32), jax.ShapeDtypeStruct((D // 8, D), F32),
                   jax.ShapeDtypeStruct((D, D), BF16), jax.ShapeDtypeStruct((8, 128), F32)]
        + ([jax.ShapeDtypeStruct((8, 128), F32), jax.ShapeDtypeStruct((S, D), F32)] if last else []),
        scratch_shapes=([pltpu.VMEM((8, D), F32)] if last else []) + [pltpu.VMEM((8, D), F32), pltpu.VMEM((D, D), F32)],
        compiler_params=_params(),
    )(*((x, target) if last else (dxn, y)), g_post, cat, w_out, place_arr, *deps)


def _bwd_mix(l, pu, pg, q, kv, ag, dcat, pool_w, pool_scale, sinks, tables, deps=(), dpw_dest=None):
    bias, tri = tables
    deps = tuple(deps) + (() if dpw_dest is None else (dpw_dest,))

    def body(pu_ref, pup_ref, pg_ref, q_ref, kv_ref, kvp_ref, ag_ref, dcat_ref, pw_ref, sc_ref, sink_ref, bias_ref,
             tri_ref, *rest):
        dproj_ref, dpw_ref, dsc_ref, dsink_ref, ext_ref, dext_ref, tmp_a, tmp_b, dkv_ref = rest[len(deps):]
        tmp_refs = (tmp_a, tmp_b)
        step = pl.program_id(0)
        i = NB - 1 - step

        @pl.when(step == 0)
        def _():
            dpw_ref[...] = jnp.zeros_like(dpw_ref)
            dsc_ref[...] = jnp.zeros_like(dsc_ref)
            dsink_ref[...] = jnp.zeros_like(dsink_ref)
            for ref in (ext_ref, tmp_a, tmp_b):
                ref[0:PAD, :] = jnp.zeros((PAD, 512), F32)
            dext_ref[BLK:POOL_ROWS, :] = jnp.zeros((HALO + PAD, 512), F32)
            dkv_ref[...] = jnp.zeros_like(dkv_ref)

        ext_ref[PAD:PAD + HALO, :] = jnp.where(i > 0, pup_ref[...], 0.0)
        ext_ref[PAD + HALO:POOL_ROWS, :] = pu_ref[...]
        _window_sums(ext_ref, tmp_refs, True)
        dpooled = []
        for g, w in enumerate(POOL_WINDOWS):
            lanes = slice(g * 128, (g + 1) * 128)
            pooled, inv = _pool_block(ext_ref, tmp_refs, i, g, w)
            pooled_b = pooled.astype(BF16)
            mixed = jnp.dot(pooled_b, pw_ref[g], preferred_element_type=F32)
            scale = sc_ref[:, lanes]
            gate = pg_ref[:, lanes]
            sg = _sigmoid(gate)
            dpo = dcat_ref[:, lanes]
            dproj_ref[:, C_PG + g * 128:C_PG + (g + 1) * 128] = (
                dpo * (mixed * scale) * (sg * (1.0 + gate * (1.0 - sg)))).astype(BF16)
            dms = dpo * (gate * sg)
            dsc_ref[g:g + 1, :] += jnp.sum(dms * mixed, axis=0, keepdims=True)
            dmixed = (dms * scale).astype(BF16)
            dpw_ref[g] += lax.dot_general(pooled_b, dmixed, TN, preferred_element_type=F32)
            dpooled.append(lax.dot_general(dmixed, pw_ref[g], NT, preferred_element_type=F32))
            dext_ref[0:BLK, lanes] = dpooled[g] * inv
        _window_sums(dext_ref, tmp_refs, False)
        for g in range(len(POOL_WINDOWS)):
            lanes = slice(g * 128, (g + 1) * 128)
            dproj_ref[:, C_PU + g * 128:C_PU + (g + 1) * 128] = (tmp_refs[g % 2][0:BLK, lanes] - dpooled[g]).astype(BF16)
        dext_ref[BLK:BLK + HALO, :] = dext_ref[0:HALO, :]

        own = _own_block_mask()
        tri = tri_ref[...]
        k_var = _head_variants(kv_ref[:, 0:128], kvp_ref[:, 0:128])
        v_var = _head_variants(kv_ref[:, 128:256], kvp_ref[:, 128:256])
        q2 = [_stack_tiles(q_ref, hkv) for hkv in range(2)]
        s = _scores(q2, k_var, own)
        p, p_sink = {}, {}
        for key, s_head in s.items():
            head = _head_of(*key)
            p[key], p_sink[key] = _softmax(s_head, bias_ref[head], sink_ref[l, head])

        do2, p_b, dp = [], {}, {}
        for hkv in range(2):
            gate = _stack_tiles(ag_ref, hkv)
            sg = _sigmoid(gate)
            dca = _stack_tiles(dcat_ref, hkv, D_POOL)
            do2.append((dca * (gate * sg)).astype(BF16))
            o2 = jnp.zeros((2 * BLK, 128), F32)
            for half in range(2):
                p_b[hkv, half] = _spread_pair(p, hkv, half, tri)
                o2 = o2 + jnp.dot(p_b[hkv, half], v_var[hkv][half], preferred_element_type=F32)
                full = lax.dot_general(do2[hkv], v_var[hkv][half], NT, preferred_element_type=F32)
                for t in range(2):
                    dp[hkv, t, half] = _merge(_rows(full, t), own)
            dag = dca * o2 * (sg * (1.0 + gate * (1.0 - sg)))
            for t in range(2):
                lo = C_AG + (2 * hkv + t) * 128
                dproj_ref[:, lo:lo + 128] = _rows(dag, t).astype(BF16)

        ds = {}
        for key in p:
            delta = jnp.sum(p[key] * dp[key], axis=-1, keepdims=True)
            ds[key] = p[key] * (dp[key] - delta)
            head = _head_of(*key)
            dsink_ref[0:1, :] += jnp.where(lax.broadcasted_iota(jnp.int32, (1, 128), 1) == head,
                                           -jnp.sum(p_sink[key] * delta, axis=0, keepdims=True), 0.0)

        dk_acc = [[None, None], [None, None]]
        dv_acc = [[None, None], [None, None]]
        for hkv in range(2):
            dq2 = jnp.zeros((2 * BLK, 128), F32)
            for half in range(2):
                ds_b = _spread_pair(ds, hkv, half, tri)
                dq2 = dq2 + jnp.dot(ds_b, k_var[hkv][half], preferred_element_type=F32)
                dk_acc[hkv][half] = lax.dot_general(ds_b, q2[hkv], TN, preferred_element_type=F32)
                dv_acc[hkv][half] = lax.dot_general(p_b[hkv, half], do2[hkv], TN, preferred_element_type=F32)
            for t in range(2):
                lo = C_Q + (2 * hkv + t) * 128
                dproj_ref[:, lo:lo + 128] = (_rows(dq2, t) * 0.125).astype(BF16)

        low = lax.broadcasted_iota(jnp.int32, (2 * BLK, 128), 1) < 64

        def gather_heads(acc):
            return jnp.where(low, acc[0][0] + pltpu.roll(acc[0][1], 64, axis=1),
                             pltpu.roll(acc[1][0], 64, axis=1) + acc[1][1])

        dk = gather_heads(dk_acc) * 0.125
        dv = gather_heads(dv_acc)
        dproj_ref[:, C_K:C_V] = (dk[BLK:, :] + dkv_ref[:, 0:128]).astype(BF16)
        dproj_ref[:, C_V:C_AG] = (dv[BLK:, :] + dkv_ref[:, 128:256]).astype(BF16)
        dkv_ref[:, 0:128] = dk[:BLK, :]
        dkv_ref[:, 128:256] = dv[:BLK, :]

    rev = lambda w: pl.BlockSpec((BLK, w), lambda s: (NB - 1 - s, 0))
    prev = lambda w: pl.BlockSpec((BLK, w), lambda s: (jnp.maximum(NB - 2 - s, 0), 0))
    halo = pl.BlockSpec((HALO, 512), lambda s: (jnp.maximum((NB - 1 - s) * (BLK // HALO) - 1, 0), 0))
    return pl.pallas_call(
        body, name="bwd_mix", grid=(NB,),
        in_specs=[rev(512), halo, rev(512), rev(512), rev(256), prev(256), rev(512), rev(D),
                  _layer(l, 4, 128, 128), _layer(l, 1, 512), pl.BlockSpec(memory_space=pltpu.SMEM),
                  pl.BlockSpec((None, N_HEADS, BLK, BLK), lambda s: (jnp.minimum(NB - 1 - s, 1), 0, 0, 0)),
                  _whole((BLK, BLK))] + [ANY] * len(deps),
        out_specs=[rev(D_IN), _layer(l, 4, 128, 128),
                   pl.BlockSpec((4, 128), lambda s: (0, 0)), pl.BlockSpec((8, 128), lambda s: (0, 0))],
        out_shape=[jax.ShapeDtypeStruct((S, D_IN), BF16), jax.ShapeDtypeStruct((DEPTH, 4, 128, 128), F32),
                   jax.ShapeDtypeStruct((4, 128), F32), jax.ShapeDtypeStruct((8, 128), F32)],
        input_output_aliases={} if dpw_dest is None else {12 + len(deps): 1},
        scratch_shapes=[pltpu.VMEM((POOL_ROWS, 512), F32)] * 4 + [pltpu.VMEM((BLK, 256), F32)],
        compiler_params=_params(),
    )(pu, pu, pg, q, kv, kv, ag, dcat, pool_w, pool_scale, sinks, bias, tri, *deps)


def _bwd_in_dw(l, dproj, x, g_pre, place_arr, deps=()):
    n_steps = S // TM

    def body(dp_ref, x_ref, g_ref, place_ref, *rest):
        own_ref, dwb_ref, dw_ref = rest[len(deps):]
        step = pl.program_id(0)

        @pl.when(step == 0)
        def _():
            dw_ref[...] = jnp.zeros_like(dw_ref)

        xt = x_ref[...]
        r = lax.rsqrt(jnp.mean(xt * xt, axis=-1, keepdims=True) + EPS)
        h = (xt * r * g_ref[...]).astype(BF16)
        dw_ref[...] += lax.dot_general(dp_ref[...], h, TN, preferred_element_type=F32)

        @pl.when(step == n_steps - 1)
        def _():
            dwb_ref[...] = dw_ref[...].astype(BF16)
            own_ref[...] = _own_piece(dw_ref, place_ref)

    row = lambda w: pl.BlockSpec((TM, w), lambda i: (i, 0))
    full = _whole
    return pl.pallas_call(
        body, name="bwd_in_dw", grid=(n_steps,),
        in_specs=[row(D_IN), row(D), _layer(l, 1, D), pl.BlockSpec(memory_space=pltpu.SMEM)] + [ANY] * len(deps),
        out_specs=[full((D_IN // 8, D)), full((D_IN, D))],
        out_shape=[jax.ShapeDtypeStruct((D_IN // 8, D), F32), jax.ShapeDtypeStruct((D_IN, D), BF16)],
        scratch_shapes=[pltpu.VMEM((D_IN, D), F32)],
        compiler_params=_params(),
    )(dproj, x, g_pre, place_arr, *deps)


def _bwd_in_dx(l, dproj, w_in_t, x, g_pre, dres, deps=(), dw_place=None):
    n_steps = S // TM
    with_dw = dw_place is not None

    def body(dp_ref, w_ref, x_ref, g_ref, dres_ref, *rest):
        place_ref = rest[0] if with_dw else None
        rest = rest[with_dw + len(deps):]
        if with_dw:
            dx_ref, dg_ref, own_ref, dwb_ref, acc_ref, dw_ref = rest
        else:
            dx_ref, dg_ref, acc_ref = rest
        step = pl.program_id(0)

        @pl.when(step == 0)
        def _():
            acc_ref[...] = jnp.zeros_like(acc_ref)
            if with_dw:
                dw_ref[...] = jnp.zeros_like(dw_ref)

        g = g_ref[...]
        halves = [slice(k * (TM // 2), (k + 1) * (TM // 2)) for k in range(2)]
        dh = [jnp.dot(dp_ref[rows, :], w_ref[...], preferred_element_type=F32) for rows in halves]
        h = []
        for rows, dh_k in zip(halves, dh):
            xt = x_ref[rows, :]
            r = lax.rsqrt(jnp.mean(xt * xt, axis=-1, keepdims=True) + EPS)
            xn = xt * r
            acc_ref[...] += _rows8(dh_k * xn)
            a = dh_k * g
            dx_ref[rows, :] = dres_ref[rows, :] + (
                r * a - xt * (r * r * r) * jnp.mean(a * xt, axis=-1, keepdims=True))
            h.append((xn * g).astype(BF16))
        if with_dw:
            dw_ref[...] += lax.dot_general(dp_ref[...], jnp.concatenate(h, axis=0), TN, preferred_element_type=F32)

        @pl.when(step == n_steps - 1)
        def _():
            _store_lane_rows(dg_ref, acc_ref[...])
            if with_dw:
                dwb_ref[...] = dw_ref[...].astype(BF16)
                own_ref[...] = _own_piece(dw_ref, place_ref)

    row = lambda w: pl.BlockSpec((TM, w), lambda i: (i, 0))
    full = _whole
    dw_specs = [full((D_IN // 8, D)), full((D_IN, D))] if with_dw else []
    dw_shapes = [jax.ShapeDtypeStruct((D_IN // 8, D), F32), jax.ShapeDtypeStruct((D_IN, D), BF16)] if with_dw else []
    return pl.pallas_call(
        body, name="bwd_in" if with_dw else "bwd_in_dx", grid=(n_steps,),
        in_specs=[row(D_IN), full((D_IN, D)), row(D), _layer(l, 1, D), row(D)]
        + [pl.BlockSpec(memory_space=pltpu.SMEM)] * with_dw + [ANY] * len(deps),
        out_specs=[row(D), full((8, 128))] + dw_specs,
        out_shape=[jax.ShapeDtypeStruct((S, D), F32), jax.ShapeDtypeStruct((8, 128), F32)] + dw_shapes,
        scratch_shapes=[pltpu.VMEM((8, D), F32)] + [pltpu.VMEM((D_IN, D), F32)] * with_dw,
        compiler_params=_params(),
    )(dproj, w_in_t, x, g_pre, dres, *((dw_place,) if with_dw else ()), *deps)


HBM =pl.BlockSpec(memory_space=pltpu.HBM)
SEM = pl.BlockSpec(memory_space=pltpu.SEMAPHORE)
SPLIT_COPY = pltpu.CompilerParams(has_side_effects=pltpu.SideEffectType.DATAFLOW_SIDE_EFFECTING)


def _in_hbm(a):
    return pltpu.with_memory_space_constraint(a, pltpu.HBM)

def _place():
    return lax.axis_index("x"), lax.axis_index("y"), lax.axis_index("c")


def _other_chips(x, y):
    return [(1 - x, y), (x, 1 - y), (1 - x, 1 - y)]


def _peer(x, y, c, m):
    return (x ^ (m >> 2), y ^ ((m >> 1) & 1), c ^ (m & 1))


def _place_cast(name, src, chip_arr, tile, layers, deps=()):
    _, n, cols = src.shape
    steps = n // tile
    k = len(layers)

    def body(chip_ref, *refs):
        for s_ref, o_ref in zip(refs[:k], refs[k + len(deps):]):
            o_ref[...] = s_ref[...].astype(BF16)

    def layer_spec(l):
        return pl.BlockSpec((None, tile, cols), lambda i, chip: (l, i, 0))

    return pl.pallas_call(
        body, name=name,
        grid_spec=pltpu.PrefetchScalarGridSpec(
            num_scalar_prefetch=1, grid=(steps,),
            in_specs=[layer_spec(l) for l in layers] + [ANY] * len(deps),
            out_specs=[pl.BlockSpec((tile, cols), lambda i, chip: (chip[0] * steps + i, 0))] * k),
        out_shape=[jax.ShapeDtypeStruct((N_SHARDS * n, cols), BF16)] * k,
        compiler_params=_params(),
    )(chip_arr, *[src] * k, *deps)


def _chip_rows(ref, chip, half=None):
    n = ref.shape[0] // N_SHARDS
    if half is None:
        return ref.at[pl.ds(pl.multiple_of(chip * n, 16), n), :]
    return ref.at[pl.ds(pl.multiple_of(chip * n + half * (n // 2), 16), n // 2), :]


def _gather_start(name, bufs, halved):
    n = len(bufs)

    def body(*refs):
        ins, send, recv, token = refs[:n], refs[n:2 * n], refs[2 * n:3 * n], refs[-1]
        x, y, c = _place()
        for a, buf in enumerate(ins):
            own = _chip_rows(buf, 2 * x + y, c if a in halved else None)
            for j, chip in enumerate(_other_chips(x, y)):
                pltpu.make_async_remote_copy(src_ref=own, dst_ref=own, send_sem=send[a].at[j], recv_sem=recv[a].at[j],
                                             device_id=(*chip, c), device_id_type=MESH).start()
        token[...] = jnp.zeros_like(token)

    outs = pl.pallas_call(
        body, name=name, in_specs=[HBM] * n,
        out_specs=[SEM] * (2 * n) + [HBM] * n + [pl.BlockSpec(memory_space=pltpu.VMEM)],
        out_shape=[pltpu.SemaphoreType.DMA((3,))] * (2 * n) + [pltpu.HBM(b.shape, b.dtype) for b in bufs]
        + [jax.ShapeDtypeStruct((8, 128), F32)],
        input_output_aliases={a: 2 * n + a for a in range(n)},
        compiler_params=SPLIT_COPY,
    )(*[_in_hbm(b) for b in bufs])
    return outs[:n], outs[n:2 * n], outs[2 * n:3 * n], outs[-1]


def _gather_wait(name, buf, send_sem, recv_sem, after, halved=False):
    def body(buf_ref, send_ref, recv_ref, *rest):
        x, y, c = _place()
        half = c if halved else None
        own = _chip_rows(buf_ref, 2 * x + y, half)
        for j, chip in enumerate(_other_chips(x, y)):
            copy = pltpu.make_async_remote_copy(src_ref=own, dst_ref=_chip_rows(buf_ref, 2 * chip[0] + chip[1], half),
                                                send_sem=send_ref.at[j], recv_sem=recv_ref.at[j],
                                                device_id=(*chip, c), device_id_type=MESH)
            copy.wait_send()
            copy.wait_recv()

    return pl.pallas_call(
        body, name=name, in_specs=[HBM, SEM, SEM] + [ANY] * len(after), out_specs=HBM,
        out_shape=pltpu.HBM(buf.shape, buf.dtype), input_output_aliases={0: 0}, compiler_params=SPLIT_COPY,
    )(buf, send_sem, recv_sem, *after)


def _sibling_handshake(x, y, c):
    barrier = pltpu.get_barrier_semaphore()
    pl.semaphore_signal(barrier, inc=1, device_id=(x, y, 1 - c), device_id_type=MESH)
    pl.semaphore_wait(barrier, 1)


def _forward_halves(name, buf, collective_id):
    def body(in_ref, out_ref, send_sems, recv_sems):
        x, y, c = _place()
        _sibling_handshake(x, y, c)

        def copy(j, chip, half):
            rows = 2 * chip[0] + chip[1]
            return pltpu.make_async_remote_copy(
                src_ref=_chip_rows(in_ref, rows, half), dst_ref=_chip_rows(out_ref, rows, half), send_sem=send_sems.at[j],
                recv_sem=recv_sems.at[j], device_id=(x, y, 1 - c), device_id_type=MESH)

        chips = _other_chips(x, y)
        for j, chip in enumerate(chips):
            copy(j, chip, c).start()
        for j, chip in enumerate(chips):
            copy(j, chip, c).wait_send()
            copy(j, chip, 1 - c).wait_recv()

    return pl.pallas_call(
        body, name=name, in_specs=[ANY], out_specs=ANY, out_shape=jax.ShapeDtypeStruct(buf.shape, buf.dtype),
        input_output_aliases={0: 0},
        scratch_shapes=[pltpu.SemaphoreType.DMA((3,))] * 2,
        compiler_params=pltpu.CompilerParams(collective_id=collective_id),
    )(buf)


def _piece_rows(ref, k):
    p = ref.shape[0] // 8
    return ref.at[pl.ds(pl.multiple_of(k * p, 32 // jnp.dtype(ref.dtype).itemsize), p), :]


def _exchange_start(name, arrays):
    n = len(arrays)
    zones = [lax.empty((7, a.shape[0] // 8, a.shape[1]), a.dtype) for a in arrays]

    def body(*refs):
        srcs, lands = refs[:n], refs[n:2 * n]
        send, recv, token = refs[2 * n:3 * n], refs[3 * n:4 * n], refs[-1]
        x, y, c = _place()
        for a, (src, land) in enumerate(zip(srcs, lands)):
            for m in range(1, 8):
                px, py, pc = _peer(x, y, c, m)
                pltpu.make_async_remote_copy(
                    src_ref=_piece_rows(src, 4 * px + 2 * py + pc), dst_ref=land.at[m - 1], send_sem=send[a].at[m - 1],
                    recv_sem=recv[a].at[m - 1], device_id=(px, py, pc), device_id_type=MESH).start()
        token[...] = jnp.zeros_like(token)

    outs = pl.pallas_call(
        body, name=name, in_specs=[HBM] * (2 * n),
        out_specs=[SEM] * (2 * n) + [HBM] * (2 * n) + [pl.BlockSpec(memory_space=pltpu.VMEM)],
        out_shape=[pltpu.SemaphoreType.DMA((7,))] * (2 * n) + [pltpu.HBM(a.shape, a.dtype) for a in arrays + zones]
        + [jax.ShapeDtypeStruct((8, 128), F32)],
        input_output_aliases={a: 2 * n + a for a in range(2 * n)},
        compiler_params=SPLIT_COPY,
    )(*[_in_hbm(a) for a in arrays + zones])
    return outs[:n], outs[n:2 * n], outs[2 * n:3 * n], outs[3 * n:4 * n], outs[-1]


def _exchange_wait(name, started, after):
    send_sems, recv_sems, arrays, zones, _ = started
    n = len(arrays)

    def body(*refs):
        srcs, lands = refs[:n], refs[n:2 * n]
        send, recv = refs[2 * n:3 * n], refs[3 * n:4 * n]
        x, y, c = _place()
        for a, (src, land) in enumerate(zip(srcs, lands)):
            for m in range(1, 8):
                px, py, pc = _peer(x, y, c, m)
                copy = pltpu.make_async_remote_copy(
                    src_ref=_piece_rows(src, 4 * px + 2 * py + pc), dst_ref=land.at[m - 1], send_sem=send[a].at[m - 1],
                    recv_sem=recv[a].at[m - 1], device_id=(px, py, pc), device_id_type=MESH)
                copy.wait_send()
                copy.wait_recv()

    outs = pl.pallas_call(
        body, name=name, in_specs=[HBM] * (2 * n) + [SEM] * (2 * n) + [ANY], out_specs=[HBM] * (2 * n),
        out_shape=[pltpu.HBM(a.shape, a.dtype) for a in list(arrays) + list(zones)],
        input_output_aliases={a: a for a in range(2 * n)}, compiler_params=SPLIT_COPY,
    )(*arrays, *zones, *send_sems, *recv_sems, after)
    return outs[n:]


def _sum_pieces(name, weights, place_arr, dests=None):
    steps = 2
    flat = [item for items in weights for item in items]
    n = len(flat)

    def body(place_ref, *refs):
        outs = iter(refs[len(refs) - len(weights):])
        k = 0
        for items in weights:
            out_ref = next(outs)
            for layer, _, _ in items:
                total = refs[k][...]
                for m in range(7):
                    total = total + refs[n + k][m].astype(F32)
                if len(items) == DEPTH:
                    out_ref[layer] = total
                else:
                    out_ref[...] = total
                k += 1

    def out_spec(items):
        _, own, _ = items[0]
        t, cols = own.shape[0] // steps, own.shape[1]
        if len(items) == DEPTH:
            return pl.BlockSpec((DEPTH, t, cols), lambda i, place: (0, place[1] * steps + i, 0))
        layer = items[0][0]
        return pl.BlockSpec((None, t, cols), lambda i, place: (layer, place[1] * steps + i, 0))

    owns = [own for _, own, _ in flat]
    dests = [] if dests is None else list(dests)
    return pl.pallas_call(
        body, name=name,
        grid_spec=pltpu.PrefetchScalarGridSpec(
            num_scalar_prefetch=1, grid=(steps,),
            in_specs=[pl.BlockSpec((o.shape[0] // steps, o.shape[1]), lambda i, place: (i, 0)) for o in owns]
            + [pl.BlockSpec((7, o.shape[0] // steps, o.shape[1]), lambda i, place: (0, i, 0)) for o in owns]
            + [ANY] * len(dests),
            out_specs=[out_spec(items) for items in weights]),
        out_shape=[jax.ShapeDtypeStruct((DEPTH, 2 * items[0][1].shape[0], items[0][1].shape[1]), F32)
                   for items in weights],
        input_output_aliases={1 + 2 * n + k: k for k in range(len(dests))},
        compiler_params=_params(),
    )(place_arr, *owns, *[recv for _, _, recv in flat], *dests)


def _sum_small(partials, recvs, place_arr):
    n = len(partials)

    def body(place_ref, *refs):
        for o_ref, r_ref, out_ref in zip(refs[:n], refs[n:2 * n], refs[2 * n:]):
            total = o_ref[...]
            for m in range(7):
                total = total + r_ref[m]
            out_ref[...] = total

    piece = lambda a: pl.BlockSpec((a.shape[0] // 8, a.shape[1]), lambda i, place: (place[0], 0))
    return pl.pallas_call(
        body, name="sum_small",
        grid_spec=pltpu.PrefetchScalarGridSpec(
            num_scalar_prefetch=1, grid=(1,),
            in_specs=[piece(a) for a in partials] + [pl.BlockSpec(r.shape, lambda i, place: (0, 0, 0)) for r in recvs],
            out_specs=[piece(a) for a in partials]),
        out_shape=[jax.ShapeDtypeStruct(a.shape, F32) for a in partials],
        compiler_params=_params(),
    )(place_arr, *partials, *recvs)


def _share(name, bufs, parts, gathered=(), collective_id=None):
    n, n_g = len(bufs), len(gathered)
    total = n + n_g
    assert collective_id is None or not gathered

    def body(*refs):
        ins, outs = refs[:total], refs[total:2 * total]
        send_sems, recv_sems, send_g, recv_g = refs[2 * total:]
        x, y, c = _place()
        if collective_id is not None:
            _sibling_handshake(x, y, c)

        def half(ref, l, which):
            p = ref.shape[1] // 2
            return ref.at[l, pl.ds(pl.multiple_of(which * p, 8), p), :]

        def swap(k, which):
            a, l = parts[k]
            return pltpu.make_async_remote_copy(
                src_ref=half(ins[a], l, which), dst_ref=half(outs[a], l, which), send_sem=send_sems.at[k],
                recv_sem=recv_sems.at[k], device_id=(x, y, 1 - c), device_id_type=MESH)

        def spread(a, m, sender):
            k = 4 * sender[0] + 2 * sender[1] + sender[2]
            return pltpu.make_async_remote_copy(
                src_ref=_piece_rows(ins[n + a], k), dst_ref=_piece_rows(outs[n + a], k), send_sem=send_g.at[7 * a + m - 1],
                recv_sem=recv_g.at[7 * a + m - 1], device_id=_peer(x, y, c, m), device_id_type=MESH)

        for k in range(len(parts)):
            swap(k, c).start()
        for a in range(n_g):
            for m in range(1, 8):
                spread(a, m, (x, y, c)).start()
        for k in range(len(parts)):
            swap(k, c).wait_send()
            swap(k, 1 - c).wait_recv()
        for a in range(n_g):
            for m in range(1, 8):
                spread(a, m, (x, y, c)).wait_send()
                spread(a, m, _peer(x, y, c, m)).wait_recv()

    arrays = list(bufs) + list(gathered)
    return pl.pallas_call(
        body, name=name, in_specs=[ANY] * total, out_specs=[ANY] * total,
        out_shape=[jax.ShapeDtypeStruct(b.shape, F32) for b in arrays],
        input_output_aliases={a: a for a in range(total)},
        scratch_shapes=[pltpu.SemaphoreType.DMA((max(len(parts), 1),))] * 2
        + [pltpu.SemaphoreType.DMA((max(7 * n_g, 1),))] * 2,
        compiler_params=pltpu.CompilerParams(collective_id=collective_id),
    )(*arrays)


def _adamw_math(w, g, m, v):
    nm = ADAM_B1 * m + (1.0 - ADAM_B1) * g
    nv = ADAM_B2 * v + (1.0 - ADAM_B2) * (g * g)
    m_hat = nm / (1.0 - ADAM_B1 ** ADAM_STEP)
    v_hat = nv / (1.0 - ADAM_B2 ** ADAM_STEP)
    return -ADAM_LR * (m_hat / (jnp.sqrt(v_hat) + ADAM_EPS) + ADAM_WD * w), nm, nv


def _adamw(name, w, g, m, v, rows_per_step, first=0, count=None, dests=None, deps=()):
    layers, rows, cols = w.shape
    count = layers if count is None else count

    def body(w_ref, g_ref, m_ref, v_ref, *rest):
        d_ref, nm_ref, nv_ref, g_out_ref = rest[-4:]
        d_ref[...], nm_ref[...], nv_ref[...] = _adamw_math(w_ref[...], g_ref[...], m_ref[...], v_ref[...])
        g_out_ref[...] = g_ref[...]

    spec = pl.BlockSpec((1, rows_per_step, cols), lambda l, i: (first + l, i, 0))
    shape = jax.ShapeDtypeStruct(w.shape, F32)
    dests = () if dests is None else tuple(dests)
    return pl.pallas_call(
        body, name=name, grid=(count, rows // rows_per_step),
        in_specs=[spec] * 4 + [ANY] * (len(dests) + len(deps)), out_specs=[spec] * 4, out_shape=[shape] * 4,
        input_output_aliases={4 + k: k for k in range(len(dests))},
        compiler_params=_params(("arbitrary", "arbitrary")),
    )(w, g, m, v, *dests, *deps)


def _pack_misc(pool_scale, sinks, norm_pre, norm_post):
    sink_rows = jnp.zeros((DEPTH, 8, 128), F32).at[:, 0, 0:N_HEADS].set(sinks).reshape(2 * 8, 128)
    return jnp.concatenate([pool_scale.reshape(8, 128), norm_pre.reshape(16, 128), norm_post.reshape(16, 128),
                            sink_rows, jnp.zeros((8, 128), F32)], axis=0)


def _adamw_small(w, g, m, v, pool):
    def body(w_ref, g_ref, m_ref, v_ref, pw_ref, pg_ref, pm_ref, pv_ref, *rest):
        outs, pool_outs, (d_ref, nm_ref, nv_ref) = rest[:17], rest[17:21], rest[21:]
        pool_outs[0][...] = pg_ref[...]
        pool_outs[1][...], pool_outs[2][...], pool_outs[3][...] = _adamw_math(
            pw_ref[...], pg_ref[...], pm_ref[...], pv_ref[...])
        d_ref[...], nm_ref[...], nv_ref[...] = _adamw_math(w_ref[...], g_ref[...], m_ref[...], v_ref[...])
        for k, src in enumerate([g_ref, d_ref, nm_ref, nv_ref]):
            scale, sinks, pre, post = outs[4 * k:4 * k + 4]
            for l in range(DEPTH):
                for j in range(4):
                    scale[l:l + 1, j * 128:(j + 1) * 128] = src[MISC_SCALE + 4 * l + j:MISC_SCALE + 4 * l + j + 1, :]
                for j in range(8):
                    pre[l:l + 1, j * 128:(j + 1) * 128] = src[MISC_PRE + 8 * l + j:MISC_PRE + 8 * l + j + 1, :]
                    post[l:l + 1, j * 128:(j + 1) * 128] = src[MISC_POST + 8 * l + j:MISC_POST + 8 * l + j + 1, :]
                sinks[l:l + 1, :] = src[MISC_SINKS + 8 * l:MISC_SINKS + 8 * l + 1, 0:N_HEADS]
        outs[16][...] = g_ref[MISC_LOSS:MISC_LOSS + 1, 0:1]

    vmem = pl.BlockSpec(memory_space=pltpu.VMEM)
    shapes = [(DEPTH, D_POOL), (DEPTH, N_HEADS), (DEPTH, D), (DEPTH, D)] * 4 + [(1, 1)]
    shapes += [pool[0].shape] * 4
    return pl.pallas_call(
        body, name="adamw_small", in_specs=[vmem] * 8, out_specs=[vmem] * 21,
        out_shape=[jax.ShapeDtypeStruct(s, F32) for s in shapes],
        scratch_shapes=[pltpu.VMEM((MISC_ROWS, 128), F32)] * 3,
    )(w, g, m, v, *pool)


def kernel(x, w_in, pool_w, pool_scale, attn_sinks, w_out, norm_pre, norm_post, loss_target, m_w_in, m_pool_w, m_pool_scale, m_attn_sinks, m_w_out, m_norm_pre, m_norm_post, v_w_in, v_pool_w, v_pool_scale, v_attn_sinks, v_w_out, v_norm_pre, v_norm_post):
    cx, cy, cc = _place()
    chip_arr = jnp.reshape(2 * cx + cy, (1,)).astype(jnp.int32)
    place_arr = jnp.stack([4 * cx + 2 * cy + cc, cc]).astype(jnp.int32)
    t = lambda a: jnp.transpose(a, (0, 2, 1))
    w_in_t = t(w_in)
    xs, target = x[0], loss_target[0]
    pool_w_b = pool_w.astype(BF16)
    tables = _attention_tables()
    scale3 = pool_scale.reshape(DEPTH, 1, D_POOL)
    pre3 = norm_pre.reshape(DEPTH, 1, D)
    post3 = norm_post.reshape(DEPTH, 1, D)

    (wi0,) = _place_cast("place_w_in0", w_in_t, chip_arr, 288, [0])
    first = _gather_start("gather_start_first", [wi0], halved=(0,))
    (wi1,) = _place_cast("place_w_in1", w_in_t, chip_arr, 288, [1], deps=(first[3],))
    wo = _place_cast("place_w_out", w_out, chip_arr, 256, [0, 1], deps=(first[3],))
    rest = _gather_start("gather_start_rest", [wi1, wo[0], wo[1]], halved=(0,))
    send, recv, bufs = [first[k] + rest[k] for k in range(3)]
    order = {(0, "in"): 0, (1, "in"): 1, (0, "out"): 2, (1, "out"): 3}

    saved = []
    packed = [_pack_misc(pool_scale, attn_sinks, norm_pre, norm_post),
              _pack_misc(m_pool_scale, m_attn_sinks, m_norm_pre, m_norm_post),
              _pack_misc(v_pool_scale, v_attn_sinks, v_norm_pre, v_norm_post)]
    after = (first[3], rest[3], pool_w_b, *tables, scale3, pre3, post3, *packed)
    below = None
    for l in range(DEPTH):
        k = order[l, "in"]
        w_in_l = _forward_halves(f"forward_w_in{l}", _gather_wait(f"gather_wait_in{l}", bufs[k], send[k], recv[k], after,
                                                                 halved=True), collective_id=l)
        if below is None:
            pu, pg, q, kv, ag = _fwd_in(l, xs, pre3, w_in_l)
        else:
            y, xs, pu, pg, q, kv, ag = _fwd_in(l, xs, pre3, w_in_l, below)
            saved[l - 1][7] = y
        cat = _fwd_mix(l, pu, pg, q, kv, ag, pool_w_b, scale3, attn_sinks, tables)
        k = order[l, "out"]
        w_out_l = _gather_wait(f"gather_wait_out{l}", bufs[k], send[k], recv[k], (cat,))
        saved.append([xs, pu, pg, q, kv, ag, cat, None, w_in_l, w_out_l])
        below, after = (cat, w_out_l, post3), (w_out_l,)

    x_in, pu, pg, q, kv, ag, cat, y, w_in_l, w_out_l = saved[1]
    dcat, dw_out1, dw_out1_b, dg_post1, loss, xs = _bwd_out(1, cat, w_out_l, post3, place_arr, x=x_in, target=target)
    ex1_out = _exchange_start("exchange_start_out1", [dw_out1_b])
    dproj, dpw, dsc1, dsink1 = _bwd_mix(1, pu, pg, q, kv, ag, dcat, pool_w_b, scale3, attn_sinks, tables,
                                        deps=(ex1_out[4],))
    dx, dg_pre1, dw_in1, dw_in1_b = _bwd_in_dx(1, dproj, w_in_l, x_in, pre3, xs, dw_place=place_arr)
    ex1_in = _exchange_start("exchange_start_in1", [dw_in1_b])

    x_in, pu, pg, q, kv, ag, cat, y, w_in_l, w_out_l = saved[0]
    dcat, dw_out0, dw_out0_b, dg_post0 = _bwd_out(0, cat, w_out_l, post3, place_arr, dxn=dx, y=y, deps=(ex1_in[4],))
    ex0_out = _exchange_start("exchange_start_out0", [dw_out0_b])
    dproj, dpw, dsc0, dsink0 = _bwd_mix(0, pu, pg, q, kv, ag, dcat, pool_w_b, scale3, attn_sinks, tables,
                                        deps=(ex0_out[4],), dpw_dest=dpw)
    dw_in0, dw_in0_b = _bwd_in_dw(0, dproj, x_in, pre3, place_arr)
    ex0_in = _exchange_start("exchange_start_in0", [dw_in0_b])

    grad_x, dg_pre0 = _bwd_in_dx(0, dproj, w_in_l, x_in, pre3, dx, deps=(ex0_in[4],))
    small = [dpw.reshape(DEPTH * 4 * 128, 128),
             jnp.concatenate([dsc0, dsc1, dg_pre0, dg_pre1, dg_post0, dg_post1, dsink0, dsink1, loss], axis=0)]
    ex_small = _exchange_start("exchange_start_small", small)
    (recv_out1,) = _exchange_wait("exchange_wait_out1", ex1_out, ex_small[4])
    (recv_in1,) = _exchange_wait("exchange_wait_in1", ex1_in, recv_out1)
    g_in, g_out = _sum_pieces("sum_pieces_1", [[(1, dw_in1, recv_in1)], [(1, dw_out1, recv_out1)]], place_arr)
    (recv_out0,) = _exchange_wait("exchange_wait_out0", ex0_out, g_out)
    (g_out,) = _sum_pieces("sum_pieces_out0", [[(0, dw_out0, recv_out0)]], place_arr, dests=[g_out])
    g_in, g_out = _share("share_a", [g_in, g_out], [(0, 1), (1, 0), (1, 1)], collective_id=DEPTH)
    m_in_t, v_in_t = t(m_w_in), t(v_w_in)
    d_out, nm_out, nv_out, grad_w_out = _adamw("adamw_w_out", w_out, g_out, m_w_out, v_w_out, 256)
    upd_in = _adamw("adamw_w_in1", w_in_t, g_in, m_in_t, v_in_t, 288, first=1, count=1, deps=(d_out,))

    (recv_in0,) = _exchange_wait("exchange_wait_in0", ex0_in, upd_in[0])
    recv_small = _exchange_wait("exchange_wait_small", ex_small, recv_in0)
    (g_in,) = _sum_pieces("sum_pieces_in0", [[(0, dw_in0, recv_in0)]], place_arr, dests=[g_in])
    g_in, g_pw, g_misc = _share("share_b", [g_in], [(0, 0)], _sum_small(small, recv_small, place_arr))
    d_in, nm_in, nv_in, grad_w_in_t = _adamw("adamw_w_in0", w_in_t, g_in, m_in_t, v_in_t, 288, first=0, count=1,
                                             dests=upd_in)
    flat = lambda a: a.reshape(DEPTH * 4 * 128, 128)
    small_out = _adamw_small(packed[0], g_misc, packed[1], packed[2],
                             (flat(pool_w), g_pw, flat(m_pool_w), flat(v_pool_w)))
    (g_sc, g_sk, g_pre, g_post, d_sc, d_sk, d_pre, d_post,
     m_sc, m_sk, m_pre, m_post, v_sc, v_sk, v_pre, v_post, loss_sum) = small_out[:17]
    g_pw, d_pw, m_pw, v_pw = [a.reshape(pool_w.shape) for a in small_out[17:]]
    return (loss_sum[0, 0], grad_x[None], t(grad_w_in_t), g_pw, g_sc, g_sk, grad_w_out, g_pre, g_post,
            t(d_in), d_pw, d_sc, d_sk, d_out, d_pre, d_post,
            t(nm_in), m_pw, m_sc, m_sk, nm_out, m_pre, m_post,
            t(nv_in), v_pw, v_sc, v_sk, nv_out, v_pre, v_post)
```

```python
import jax
import jax.numpy as jnp
from jax import lax
from jax.experimental import pallas as pl
from jax.experimental.pallas import tpu as pltpu

F32 = jnp.float32
BF16 = jnp.bfloat16

S = 2048
D = 1024
DEPTH = 2
D_POOL = 512
POOL_WINDOWS = (2, 4, 8, 16)
N_HEADS = 8
D_IN = 2304
N_SHARDS = 4
W_IN_SHARD = D_IN // N_SHARDS
W_OUT_SHARD = D // N_SHARDS
BLK = 128
NB = S // BLK
HALO = 16
PAD = 8
EPS = 1e-6
NEG_INF = -1e30
C_PU, C_PG, C_Q, C_K, C_V, C_AG = 0, 512, 1024, 1536, 1664, 1792

ADAM_LR = 0.001
ADAM_B1 = 0.9
ADAM_B2 = 0.999
ADAM_EPS = 1e-08
ADAM_WD = 0.01
ADAM_STEP = 10

TM = 512
VMEM_LIMIT = 56 * 1024 * 1024

NT = (((1,), (1,)), ((), ()))
TN = (((0,), (0,)), ((), ()))

MESH = pl.DeviceIdType.MESH
ANY = pl.BlockSpec(memory_space=pl.ANY)

ID_FORWARD = (0, 1)
(ID_SHARE_A, ID_SHARE_B, ID_GATHER_FIRST, ID_GATHER_REST, ID_OUT1, ID_IN1, ID_OUT0, ID_IN0, ID_SMALL) = range(2, 11)

MISC_SCALE, MISC_PRE, MISC_POST, MISC_SINKS, MISC_LOSS = 0, 8, 24, 40, 56
MISC_ROWS = 64


def _params(sem=("arbitrary",)):
    return pltpu.CompilerParams(dimension_semantics=sem, vmem_limit_bytes=VMEM_LIMIT)


def _sigmoid(v):
    return 1.0 / (1.0 + jnp.exp(-v))


def _rows8(v):
    r, c = v.shape
    return v.reshape(r // 8, 8, c).sum(axis=0)


def _layer(l, *shape):
    zeros = (0,) * len(shape)
    return pl.BlockSpec((None,) + shape, lambda i: (l,) + zeros)


def _whole(shape):
    zeros = (0,) * len(shape)
    return pl.BlockSpec(shape, lambda i: zeros, pipeline_mode=pl.Buffered(1))


def _fwd_in(l, x, g_pre, w_in_t, below=None):
    fused = below is not None

    def body(x_ref, g_ref, w_ref, *rest):
        if fused:
            cat_ref, wo_ref, gp_ref, y_ref, xn_ref = rest[:5]
            y = jnp.dot(cat_ref[...], wo_ref[...], preferred_element_type=F32)
            y_ref[...] = y
            xt = x_ref[...] + y * lax.rsqrt(jnp.mean(y * y, axis=-1, keepdims=True) + EPS) * gp_ref[...]
            xn_ref[...] = xt
        else:
            xt = x_ref[...]
        pu_ref, pg_ref, q_ref, kv_ref, ag_ref = rest[-5:]
        r = lax.rsqrt(jnp.mean(xt * xt, axis=-1, keepdims=True) + EPS)
        h = (xt * r * g_ref[...]).astype(BF16)

        def proj(lo, hi):
            return lax.dot_general(h, w_ref[lo:hi, :], NT, preferred_element_type=F32)

        pu_ref[...] = proj(C_PU, C_PG)
        pg_ref[...] = proj(C_PG, C_Q)
        q_ref[...] = proj(C_Q, C_K).astype(BF16)
        kv_ref[...] = proj(C_K, C_AG).astype(BF16)
        ag_ref[...] = proj(C_AG, D_IN)

    row = lambda w: pl.BlockSpec((TM, w), lambda i: (i, 0))
    act = jax.ShapeDtypeStruct((S, D), F32)
    return pl.pallas_call(
        body, name="fwd_out_in" if fused else "fwd_in", grid=(S // TM,),
        in_specs=[row(D), _layer(l, 1, D), _whole((D_IN, D))]
        + ([row(D), _whole((D, D)), _layer(l - 1, 1, D)] if fused else []),
        out_specs=[row(D)] * (2 * fused) + [row(512), row(512), row(512), row(256), row(512)],
        out_shape=[act] * (2 * fused)
        + [jax.ShapeDtypeStruct((S, 512), F32), jax.ShapeDtypeStruct((S, 512), F32),
           jax.ShapeDtypeStruct((S, 512), BF16), jax.ShapeDtypeStruct((S, 256), BF16),
           jax.ShapeDtypeStruct((S, 512), F32)],
        compiler_params=_params(),
    )(x, g_pre, w_in_t, *(below if fused else ()))


LOG2E = 1.4426950408889634
SCORE_SCALE = 0.125 * LOG2E


def _attention_tables():
    qi = jnp.arange(BLK)[:, None]
    kj = jnp.arange(BLK)[None, :]
    dist = ((qi - kj) % BLK).astype(F32)
    slopes = jnp.exp2(-jnp.arange(1, N_HEADS + 1, dtype=F32))
    bias = -(slopes * LOG2E)[:, None, None] * dist[None]
    first = jnp.where(kj > qi, NEG_INF, bias)
    return jnp.stack([first, bias]), (kj <= qi).astype(BF16)


def _own_block_mask():
    return lax.broadcasted_iota(jnp.int32, (BLK, BLK), 1) <= lax.broadcasted_iota(jnp.int32, (BLK, BLK), 0)


def _merge(full, own):
    return jnp.where(own, full[:, BLK:], full[:, :BLK])


def _spread(v, tri):
    own = v * tri
    return jnp.concatenate([v - own, own], axis=1)


def _head_variants(cur, prev):
    both = jnp.concatenate([prev, cur], axis=0).astype(F32)
    swapped = pltpu.roll(both, 64, axis=1)
    low = lax.broadcasted_iota(jnp.int32, both.shape, 1) < 64
    zero = jnp.zeros_like(both)
    return ((jnp.where(low, both, zero).astype(BF16), jnp.where(low, zero, swapped).astype(BF16)),
            (jnp.where(low, swapped, zero).astype(BF16), jnp.where(low, zero, both).astype(BF16)))


def _head_of(hkv, t, half):
    return hkv * 4 + 2 * t + half


def _rows(v, t):
    return v[t * BLK:(t + 1) * BLK]


def _stack_tiles(ref, hkv, offset=0):
    lo = offset + 2 * hkv * 128
    return jnp.concatenate([ref[:, lo:lo + 128], ref[:, lo + 128:lo + 256]], axis=0)


def _scores(q2, k_var, own):
    s = {}
    for hkv in range(2):
        for half in range(2):
            full = lax.dot_general(q2[hkv], k_var[hkv][half], NT, preferred_element_type=F32)
            for t in range(2):
                s[hkv, t, half] = _merge(_rows(full, t), own)
    return s


def _softmax(s, bias, sink):
    s = s * SCORE_SCALE + bias
    sink2 = sink * LOG2E
    m = jnp.maximum(jnp.max(s, axis=-1, keepdims=True), sink2)
    p = jnp.exp2(s - m)
    e_sink = jnp.exp2(sink2 - m)
    inv = 1.0 / (jnp.sum(p, axis=-1, keepdims=True) + e_sink)
    return p * inv, e_sink * inv


def _spread_pair(v, hkv, half, tri):
    return jnp.concatenate([_spread(v[hkv, t, half].astype(BF16), tri) for t in range(2)], axis=0)


POOL_ROWS = PAD + HALO + BLK


def _window_sums(src_ref, tmp_refs, trailing):
    lo, hi = (PAD, POOL_ROWS) if trailing else (0, HALO + BLK)
    cur = src_ref
    for level in range(len(POOL_WINDOWS)):
        lanes = slice(level * 128, 512)
        shift = -(1 << level) if trailing else (1 << level)
        dst = tmp_refs[level % 2]
        dst[lo:hi, lanes] = cur[lo:hi, lanes] + cur[lo + shift:hi + shift, lanes]
        cur = dst


def _pool_block(ext_ref, tmp_refs, i, g, w):
    lanes = slice(g * 128, (g + 1) * 128)
    rows = slice(PAD + HALO, POOL_ROWS)
    t = (i * BLK + lax.broadcasted_iota(jnp.int32, (BLK, 1), 0)).astype(F32)
    inv = 1.0 / jnp.minimum(t + 1.0, float(w))
    return tmp_refs[g % 2][rows, lanes] * inv - ext_ref[rows, lanes], inv


def _fwd_mix(l, pu, pg, q, kv, ag, pool_w, pool_scale, sinks, tables):
    bias, tri = tables

    def body(pu_ref, pup_ref, pg_ref, q_ref, kv_ref, kvp_ref, ag_ref, pw_ref, sc_ref, sink_ref, bias_ref, tri_ref,
             cat_ref, ext_ref, *tmp_refs):
        i = pl.program_id(0)

        @pl.when(i == 0)
        def _():
            for ref in (ext_ref, *tmp_refs):
                ref[0:PAD, :] = jnp.zeros((PAD, 512), F32)

        ext_ref[PAD:PAD + HALO, :] = jnp.where(i > 0, pup_ref[...], 0.0)
        ext_ref[PAD + HALO:POOL_ROWS, :] = pu_ref[...]
        _window_sums(ext_ref, tmp_refs, True)
        for g, w in enumerate(POOL_WINDOWS):
            lanes = slice(g * 128, (g + 1) * 128)
            pooled, _ = _pool_block(ext_ref, tmp_refs, i, g, w)
            mixed = jnp.dot(pooled.astype(BF16), pw_ref[g], preferred_element_type=F32)
            gate = pg_ref[:, lanes]
            cat_ref[:, lanes] = (mixed * sc_ref[:, lanes] * (gate * _sigmoid(gate))).astype(BF16)

        own = _own_block_mask()
        tri = tri_ref[...]
        k_var = _head_variants(kv_ref[:, 0:128], kvp_ref[:, 0:128])
        v_var = _head_variants(kv_ref[:, 128:256], kvp_ref[:, 128:256])
        s = _scores([_stack_tiles(q_ref, hkv) for hkv in range(2)], k_var, own)
        p = {}
        for (hkv, t, half), s_head in s.items():
            head = _head_of(hkv, t, half)
            p[hkv, t, half], _ = _softmax(s_head, bias_ref[head], sink_ref[l, head])
        for hkv in range(2):
            o2 = jnp.zeros((2 * BLK, 128), F32)
            for half in range(2):
                o2 = o2 + jnp.dot(_spread_pair(p, hkv, half, tri), v_var[hkv][half], preferred_element_type=F32)
            for t in range(2):
                lo = (2 * hkv + t) * 128
                gate = ag_ref[:, lo:lo + 128]
                cat_ref[:, D_POOL + lo:D_POOL + lo + 128] = (_rows(o2, t) * (gate * _sigmoid(gate))).astype(BF16)

    blk = lambda w: pl.BlockSpec((BLK, w), lambda i: (i, 0))
    prev = lambda w: pl.BlockSpec((BLK, w), lambda i: (jnp.maximum(i - 1, 0), 0))
    halo = pl.BlockSpec((HALO, 512), lambda i: (jnp.maximum(i * (BLK // HALO) - 1, 0), 0))
    return pl.pallas_call(
        body, name="fwd_mix", grid=(NB,),
        in_specs=[blk(512), halo, blk(512), blk(512), blk(256), prev(256), blk(512),
                  _layer(l, 4, 128, 128), _layer(l, 1, 512), pl.BlockSpec(memory_space=pltpu.SMEM),
                  pl.BlockSpec((None, N_HEADS, BLK, BLK), lambda i: (jnp.minimum(i, 1), 0, 0, 0)), _whole((BLK, BLK))],
        out_specs=blk(D),
        out_shape=jax.ShapeDtypeStruct((S, D), BF16),
        scratch_shapes=[pltpu.VMEM((POOL_ROWS, 512), F32)] * 3,
        compiler_params=_params(),
    )(pu, pu, pg, q, kv, kv, ag, pool_w, pool_scale, sinks, bias, tri)


def _store_lane_rows(ref, acc):
    total = jnp.sum(acc, axis=0, keepdims=True)
    for k in range(ref.shape[0]):
        ref[k:k + 1, :] = total[:, k * 128:(k + 1) * 128]


def _own_piece(dw_ref, place_ref):
    p = dw_ref.shape[0] // 8
    return dw_ref[pl.ds(pl.multiple_of(place_ref[0] * p, 8), p), :]


def _bwd_out(l, cat, w_out, g_post, place_arr, dxn=None, y=None, x=None, target=None, deps=()):
    last = target is not None
    n_steps = S // TM

    def body(a_ref, b_ref, g_ref, cat_ref, w_ref, place_ref, *rest):
        dcat_ref, own_ref, dwb_ref, dg_ref = rest[len(deps):len(deps) + 4]
        rest = rest[len(deps) + 4:]
        acc_ref, dw_ref = rest[-2:]
        step = pl.program_id(0)

        @pl.when(step == 0)
        def _():
            dw_ref[...] = jnp.zeros_like(dw_ref)
            acc_ref[...] = jnp.zeros_like(acc_ref)

        cat = cat_ref[...]
        g = g_ref[...]
        y = jnp.dot(cat, w_ref[...], preferred_element_type=F32) if last else b_ref[...]
        r = lax.rsqrt(jnp.mean(y * y, axis=-1, keepdims=True) + EPS)
        if last:
            loss_ref, dx_ref, loss_acc_ref = rest[:3]
            err = a_ref[...] + y * r * g - b_ref[...]

            @pl.when(step == 0)
            def _():
                loss_acc_ref[...] = jnp.zeros_like(loss_acc_ref)

            loss_acc_ref[...] += _rows8(err * err)
            dz = err * (1.0 / D)
            dx_ref[...] = dz
        else:
            dz = a_ref[...]
        a = dz * g
        dy = r * a - y * (r * r * r) * jnp.mean(a * y, axis=-1, keepdims=True)
        acc_ref[...] += _rows8(dz * (y * r))
        dyb = dy.astype(BF16)
        dcat_ref[...] = lax.dot_general(dyb, w_ref[...], NT, preferred_element_type=F32)
        dw_ref[...] += lax.dot_general(cat, dyb, TN, preferred_element_type=F32)

        @pl.when(step == n_steps - 1)
        def _():
            _store_lane_rows(dg_ref, acc_ref[...])
            dwb_ref[...] = dw_ref[...].astype(BF16)
            own_ref[...] = _own_piece(dw_ref, place_ref)
            if last:
                loss_ref[...] = jnp.full((8, 128), (0.5 / D) * jnp.sum(loss_acc_ref[...]), F32)

    row = lambda: pl.BlockSpec((TM, D), lambda i: (i, 0))
    full = _whole
    return pl.pallas_call(
        body, name="out_loss_bwd" if last else "bwd_out", grid=(n_steps,),
        in_specs=[row(), row(), _layer(l, 1, D), row(), full((D, D)), pl.BlockSpec(memory_space=pltpu.SMEM)]
        + [ANY] * len(deps),
        out_specs=[row(), full((D // 8, D)), full((D, D)), full((8, 128))] + ([full((8, 128)), row()] if last else []),
        out_shape=[jax.ShapeDtypeStruct((S, D), F32), jax.ShapeDtypeStruct((D // 8, D), F32),
                   jax.ShapeDtypeStruct((D, D), BF16), jax.ShapeDtypeStruct((8, 128), F32)]
        + ([jax.ShapeDtypeStruct((8, 128), F32), jax.ShapeDtypeStruct((S, D), F32)] if last else []),
        scratch_shapes=([pltpu.VMEM((8, D), F32)] if last else []) + [pltpu.VMEM((8, D), F32), pltpu.VMEM((D, D), F32)],
        compiler_params=_params(),
    )(*((x, target) if last else (dxn, y)), g_post, cat, w_out, place_arr, *deps)


def _bwd_mix(l, pu, pg, q, kv, ag, dcat, pool_w, pool_scale, sinks, tables, deps=(), dpw_dest=None):
    bias, tri = tables
    deps = tuple(deps) + (() if dpw_dest is None else (dpw_dest,))

    def body(pu_ref, pup_ref, pg_ref, q_ref, kv_ref, kvp_ref, ag_ref, dcat_ref, pw_ref, sc_ref, sink_ref, bias_ref,
             tri_ref, *rest):
        dproj_ref, dpw_ref, dsc_ref, dsink_ref, ext_ref, dext_ref, tmp_a, tmp_b, dkv_ref = rest[len(deps):]
        tmp_refs = (tmp_a, tmp_b)
        step = pl.program_id(0)
        i = NB - 1 - step

        @pl.when(step == 0)
        def _():
            dpw_ref[...] = jnp.zeros_like(dpw_ref)
            dsc_ref[...] = jnp.zeros_like(dsc_ref)
            dsink_ref[...] = jnp.zeros_like(dsink_ref)
            for ref in (ext_ref, tmp_a, tmp_b):
                ref[0:PAD, :] = jnp.zeros((PAD, 512), F32)
            dext_ref[BLK:POOL_ROWS, :] = jnp.zeros((HALO + PAD, 512), F32)
            dkv_ref[...] = jnp.zeros_like(dkv_ref)

        ext_ref[PAD:PAD + HALO, :] = jnp.where(i > 0, pup_ref[...], 0.0)
        ext_ref[PAD + HALO:POOL_ROWS, :] = pu_ref[...]
        _window_sums(ext_ref, tmp_refs, True)
        dpooled = []
        for g, w in enumerate(POOL_WINDOWS):
            lanes = slice(g * 128, (g + 1) * 128)
            pooled, inv = _pool_block(ext_ref, tmp_refs, i, g, w)
            pooled_b = pooled.astype(BF16)
            mixed = jnp.dot(pooled_b, pw_ref[g], preferred_element_type=F32)
            scale = sc_ref[:, lanes]
            gate = pg_ref[:, lanes]
            sg = _sigmoid(gate)
            dpo = dcat_ref[:, lanes]
            dproj_ref[:, C_PG + g * 128:C_PG + (g + 1) * 128] = (
                dpo * (mixed * scale) * (sg * (1.0 + gate * (1.0 - sg)))).astype(BF16)
            dms = dpo * (gate * sg)
            dsc_ref[g:g + 1, :] += jnp.sum(dms * mixed, axis=0, keepdims=True)
            dmixed = (dms * scale).astype(BF16)
            dpw_ref[g] += lax.dot_general(pooled_b, dmixed, TN, preferred_element_type=F32)
            dpooled.append(lax.dot_general(dmixed, pw_ref[g], NT, preferred_element_type=F32))
            dext_ref[0:BLK, lanes] = dpooled[g] * inv
        _window_sums(dext_ref, tmp_refs, False)
        for g in range(len(POOL_WINDOWS)):
            lanes = slice(g * 128, (g + 1) * 128)
            dproj_ref[:, C_PU + g * 128:C_PU + (g + 1) * 128] = (tmp_refs[g % 2][0:BLK, lanes] - dpooled[g]).astype(BF16)
        dext_ref[BLK:BLK + HALO, :] = dext_ref[0:HALO, :]

        own = _own_block_mask()
        tri = tri_ref[...]
        k_var = _head_variants(kv_ref[:, 0:128], kvp_ref[:, 0:128])
        v_var = _head_variants(kv_ref[:, 128:256], kvp_ref[:, 128:256])
        q2 = [_stack_tiles(q_ref, hkv) for hkv in range(2)]
        s = _scores(q2, k_var, own)
        p, p_sink = {}, {}
        for key, s_head in s.items():
            head = _head_of(*key)
            p[key], p_sink[key] = _softmax(s_head, bias_ref[head], sink_ref[l, head])

        do2, p_b, dp = [], {}, {}
        for hkv in range(2):
            gate = _stack_tiles(ag_ref, hkv)
            sg = _sigmoid(gate)
            dca = _stack_tiles(dcat_ref, hkv, D_POOL)
            do2.append((dca * (gate * sg)).astype(BF16))
            o2 = jnp.zeros((2 * BLK, 128), F32)
            for half in range(2):
                p_b[hkv, half] = _spread_pair(p, hkv, half, tri)
                o2 = o2 + jnp.dot(p_b[hkv, half], v_var[hkv][half], preferred_element_type=F32)
                full = lax.dot_general(do2[hkv], v_var[hkv][half], NT, preferred_element_type=F32)
                for t in range(2):
                    dp[hkv, t, half] = _merge(_rows(full, t), own)
            dag = dca * o2 * (sg * (1.0 + gate * (1.0 - sg)))
            for t in range(2):
                lo = C_AG + (2 * hkv + t) * 128
                dproj_ref[:, lo:lo + 128] = _rows(dag, t).astype(BF16)

        ds = {}
        for key in p:
            delta = jnp.sum(p[key] * dp[key], axis=-1, keepdims=True)
            ds[key] = p[key] * (dp[key] - delta)
            head = _head_of(*key)
            dsink_ref[0:1, :] += jnp.where(lax.broadcasted_iota(jnp.int32, (1, 128), 1) == head,
                                           -jnp.sum(p_sink[key] * delta, axis=0, keepdims=True), 0.0)

        dk_acc = [[None, None], [None, None]]
        dv_acc = [[None, None], [None, None]]
        for hkv in range(2):
            dq2 = jnp.zeros((2 * BLK, 128), F32)
            for half in range(2):
                ds_b = _spread_pair(ds, hkv, half, tri)
                dq2 = dq2 + jnp.dot(ds_b, k_var[hkv][half], preferred_element_type=F32)
                dk_acc[hkv][half] = lax.dot_general(ds_b, q2[hkv], TN, preferred_element_type=F32)
                dv_acc[hkv][half] = lax.dot_general(p_b[hkv, half], do2[hkv], TN, preferred_element_type=F32)
            for t in range(2):
                lo = C_Q + (2 * hkv + t) * 128
                dproj_ref[:, lo:lo + 128] = (_rows(dq2, t) * 0.125).astype(BF16)

        low = lax.broadcasted_iota(jnp.int32, (2 * BLK, 128), 1) < 64

        def gather_heads(acc):
            return jnp.where(low, acc[0][0] + pltpu.roll(acc[0][1], 64, axis=1),
                             pltpu.roll(acc[1][0], 64, axis=1) + acc[1][1])

        dk = gather_heads(dk_acc) * 0.125
        dv = gather_heads(dv_acc)
        dproj_ref[:, C_K:C_V] = (dk[BLK:, :] + dkv_ref[:, 0:128]).astype(BF16)
        dproj_ref[:, C_V:C_AG] = (dv[BLK:, :] + dkv_ref[:, 128:256]).astype(BF16)
        dkv_ref[:, 0:128] = dk[:BLK, :]
        dkv_ref[:, 128:256] = dv[:BLK, :]

    rev = lambda w: pl.BlockSpec((BLK, w), lambda s: (NB - 1 - s, 0))
    prev = lambda w: pl.BlockSpec((BLK, w), lambda s: (jnp.maximum(NB - 2 - s, 0), 0))
    halo = pl.BlockSpec((HALO, 512), lambda s: (jnp.maximum((NB - 1 - s) * (BLK // HALO) - 1, 0), 0))
    return pl.pallas_call(
        body, name="bwd_mix", grid=(NB,),
        in_specs=[rev(512), halo, rev(512), rev(512), rev(256), prev(256), rev(512), rev(D),
                  _layer(l, 4, 128, 128), _layer(l, 1, 512), pl.BlockSpec(memory_space=pltpu.SMEM),
                  pl.BlockSpec((None, N_HEADS, BLK, BLK), lambda s: (jnp.minimum(NB - 1 - s, 1), 0, 0, 0)),
                  _whole((BLK, BLK))] + [ANY] * len(deps),
        out_specs=[rev(D_IN), _layer(l, 4, 128, 128),
                   pl.BlockSpec((4, 128), lambda s: (0, 0)), pl.BlockSpec((8, 128), lambda s: (0, 0))],
        out_shape=[jax.ShapeDtypeStruct((S, D_IN), BF16), jax.ShapeDtypeStruct((DEPTH, 4, 128, 128), F32),
                   jax.ShapeDtypeStruct((4, 128), F32), jax.ShapeDtypeStruct((8, 128), F32)],
        input_output_aliases={} if dpw_dest is None else {12 + len(deps): 1},
        scratch_shapes=[pltpu.VMEM((POOL_ROWS, 512), F32)] * 4 + [pltpu.VMEM((BLK, 256), F32)],
        compiler_params=_params(),
    )(pu, pu, pg, q, kv, kv, ag, dcat, pool_w, pool_scale, sinks, bias, tri, *deps)


def _bwd_in_dw(l, dproj, x, g_pre, place_arr, deps=()):
    n_steps = S // TM

    def body(dp_ref, x_ref, g_ref, place_ref, *rest):
        own_ref, dwb_ref, dw_ref = rest[len(deps):]
        step = pl.program_id(0)

        @pl.when(step == 0)
        def _():
            dw_ref[...] = jnp.zeros_like(dw_ref)

        xt = x_ref[...]
        r = lax.rsqrt(jnp.mean(xt * xt, axis=-1, keepdims=True) + EPS)
        h = (xt * r * g_ref[...]).astype(BF16)
        dw_ref[...] += lax.dot_general(dp_ref[...], h, TN, preferred_element_type=F32)

        @pl.when(step == n_steps - 1)
        def _():
            dwb_ref[...] = dw_ref[...].astype(BF16)
            own_ref[...] = _own_piece(dw_ref, place_ref)

    row = lambda w: pl.BlockSpec((TM, w), lambda i: (i, 0))
    full = _whole
    return pl.pallas_call(
        body, name="bwd_in_dw", grid=(n_steps,),
        in_specs=[row(D_IN), row(D), _layer(l, 1, D), pl.BlockSpec(memory_space=pltpu.SMEM)] + [ANY] * len(deps),
        out_specs=[full((D_IN // 8, D)), full((D_IN, D))],
        out_shape=[jax.ShapeDtypeStruct((D_IN // 8, D), F32), jax.ShapeDtypeStruct((D_IN, D), BF16)],
        scratch_shapes=[pltpu.VMEM((D_IN, D), F32)],
        compiler_params=_params(),
    )(dproj, x, g_pre, place_arr, *deps)


def _bwd_in_dx(l, dproj, w_in_t, x, g_pre, dres, deps=(), dw_place=None):
    n_steps = S // TM
    with_dw = dw_place is not None

    def body(dp_ref, w_ref, x_ref, g_ref, dres_ref, *rest):
        place_ref = rest[0] if with_dw else None
        rest = rest[with_dw + len(deps):]
        if with_dw:
            dx_ref, dg_ref, own_ref, dwb_ref, acc_ref, dw_ref = rest
        else:
            dx_ref, dg_ref, acc_ref = rest
        step = pl.program_id(0)

        @pl.when(step == 0)
        def _():
            acc_ref[...] = jnp.zeros_like(acc_ref)
            if with_dw:
                dw_ref[...] = jnp.zeros_like(dw_ref)

        g = g_ref[...]
        halves = [slice(k * (TM // 2), (k + 1) * (TM // 2)) for k in range(2)]
        dh = [jnp.dot(dp_ref[rows, :], w_ref[...], preferred_element_type=F32) for rows in halves]
        h = []
        for rows, dh_k in zip(halves, dh):
            xt = x_ref[rows, :]
            r = lax.rsqrt(jnp.mean(xt * xt, axis=-1, keepdims=True) + EPS)
            xn = xt * r
            acc_ref[...] += _rows8(dh_k * xn)
            a = dh_k * g
            dx_ref[rows, :] = dres_ref[rows, :] + (
                r * a - xt * (r * r * r) * jnp.mean(a * xt, axis=-1, keepdims=True))
            h.append((xn * g).astype(BF16))
        if with_dw:
            dw_ref[...] += lax.dot_general(dp_ref[...], jnp.concatenate(h, axis=0), TN, preferred_element_type=F32)

        @pl.when(step == n_steps - 1)
        def _():
            _store_lane_rows(dg_ref, acc_ref[...])
            if with_dw:
                dwb_ref[...] = dw_ref[...].astype(BF16)
                own_ref[...] = _own_piece(dw_ref, place_ref)

    row = lambda w: pl.BlockSpec((TM, w), lambda i: (i, 0))
    full = _whole
    dw_specs = [full((D_IN // 8, D)), full((D_IN, D))] if with_dw else []
    dw_shapes = [jax.ShapeDtypeStruct((D_IN // 8, D), F32), jax.ShapeDtypeStruct((D_IN, D), BF16)] if with_dw else []
    return pl.pallas_call(
        body, name="bwd_in" if with_dw else "bwd_in_dx", grid=(n_steps,),
        in_specs=[row(D_IN), full((D_IN, D)), row(D), _layer(l, 1, D), row(D)]
        + [pl.BlockSpec(memory_space=pltpu.SMEM)] * with_dw + [ANY] * len(deps),
        out_specs=[row(D), full((8, 128))] + dw_specs,
        out_shape=[jax.ShapeDtypeStruct((S, D), F32), jax.ShapeDtypeStruct((8, 128), F32)] + dw_shapes,
        scratch_shapes=[pltpu.VMEM((8, D), F32)] + [pltpu.VMEM((D_IN, D), F32)] * with_dw,
        compiler_params=_params(),
    )(dproj, w_in_t, x, g_pre, dres, *((dw_place,) if with_dw else ()), *deps)


HBM =pl.BlockSpec(memory_space=pltpu.HBM)
SEM = pl.BlockSpec(memory_space=pltpu.SEMAPHORE)
def _split_copy(collective_id=None):
    return pltpu.CompilerParams(has_side_effects=pltpu.SideEffectType.DATAFLOW_SIDE_EFFECTING,
                                collective_id=collective_id)


SPLIT_COPY = _split_copy()


def _in_hbm(a):
    return pltpu.with_memory_space_constraint(a, pltpu.HBM)

def _place():
    return lax.axis_index("x"), lax.axis_index("y"), lax.axis_index("c")


def _other_chips(x, y):
    return [(1 - x, y), (x, 1 - y), (1 - x, 1 - y)]


def _peer(x, y, c, m):
    return (x ^ (m >> 2), y ^ ((m >> 1) & 1), c ^ (m & 1))


def _place_cast(name, src, chip_arr, tile, layers, deps=()):
    _, n, cols = src.shape
    steps = n // tile
    k = len(layers)

    def body(chip_ref, *refs):
        for s_ref, o_ref in zip(refs[:k], refs[k + len(deps):]):
            o_ref[...] = s_ref[...].astype(BF16)

    def layer_spec(l):
        return pl.BlockSpec((None, tile, cols), lambda i, chip: (l, i, 0))

    return pl.pallas_call(
        body, name=name,
        grid_spec=pltpu.PrefetchScalarGridSpec(
            num_scalar_prefetch=1, grid=(steps,),
            in_specs=[layer_spec(l) for l in layers] + [ANY] * len(deps),
            out_specs=[pl.BlockSpec((tile, cols), lambda i, chip: (chip[0] * steps + i, 0))] * k),
        out_shape=[jax.ShapeDtypeStruct((N_SHARDS * n, cols), BF16)] * k,
        compiler_params=_params(),
    )(chip_arr, *[src] * k, *deps)


def _chip_rows(ref, chip, half=None):
    n = ref.shape[0] // N_SHARDS
    if half is None:
        return ref.at[pl.ds(pl.multiple_of(chip * n, 16), n), :]
    return ref.at[pl.ds(pl.multiple_of(chip * n + half * (n // 2), 16), n // 2), :]


def _gather_start(name, bufs, halved, collective_id):
    n = len(bufs)

    def body(*refs):
        ins, send, recv, token = refs[:n], refs[n:2 * n], refs[2 * n:3 * n], refs[-1]
        x, y, c = _place()
        _handshake([(*chip, c) for chip in _other_chips(x, y)])
        for a, buf in enumerate(ins):
            own = _chip_rows(buf, 2 * x + y, c if a in halved else None)
            for j, chip in enumerate(_other_chips(x, y)):
                pltpu.make_async_remote_copy(src_ref=own, dst_ref=own, send_sem=send[a].at[j], recv_sem=recv[a].at[j],
                                             device_id=(*chip, c), device_id_type=MESH).start()
        token[...] = jnp.zeros_like(token)

    outs = pl.pallas_call(
        body, name=name, in_specs=[HBM] * n,
        out_specs=[SEM] * (2 * n) + [HBM] * n + [pl.BlockSpec(memory_space=pltpu.VMEM)],
        out_shape=[pltpu.SemaphoreType.DMA((3,))] * (2 * n) + [pltpu.HBM(b.shape, b.dtype) for b in bufs]
        + [jax.ShapeDtypeStruct((8, 128), F32)],
        input_output_aliases={a: 2 * n + a for a in range(n)},
        compiler_params=_split_copy(collective_id),
    )(*[_in_hbm(b) for b in bufs])
    return outs[:n], outs[n:2 * n], outs[2 * n:3 * n], outs[-1]


def _gather_wait(name, buf, send_sem, recv_sem, after, halved=False):
    def body(buf_ref, send_ref, recv_ref, *rest):
        x, y, c = _place()
        half = c if halved else None
        own = _chip_rows(buf_ref, 2 * x + y, half)
        for j, chip in enumerate(_other_chips(x, y)):
            copy = pltpu.make_async_remote_copy(src_ref=own, dst_ref=_chip_rows(buf_ref, 2 * chip[0] + chip[1], half),
                                                send_sem=send_ref.at[j], recv_sem=recv_ref.at[j],
                                                device_id=(*chip, c), device_id_type=MESH)
            copy.wait_send()
            copy.wait_recv()

    return pl.pallas_call(
        body, name=name, in_specs=[HBM, SEM, SEM] + [ANY] * len(after), out_specs=HBM,
        out_shape=pltpu.HBM(buf.shape, buf.dtype), input_output_aliases={0: 0}, compiler_params=SPLIT_COPY,
    )(buf, send_sem, recv_sem, *after)


def _handshake(peers):
    barrier = pltpu.get_barrier_semaphore()
    for peer in peers:
        pl.semaphore_signal(barrier, inc=1, device_id=peer, device_id_type=MESH)
    pl.semaphore_wait(barrier, len(peers))


def _sibling_handshake(x, y, c):
    _handshake([(x, y, 1 - c)])


def _forward_halves(name, buf, collective_id):
    def body(in_ref, out_ref, send_sems, recv_sems):
        x, y, c = _place()
        _sibling_handshake(x, y, c)

        def copy(j, chip, half):
            rows = 2 * chip[0] + chip[1]
            return pltpu.make_async_remote_copy(
                src_ref=_chip_rows(in_ref, rows, half), dst_ref=_chip_rows(out_ref, rows, half), send_sem=send_sems.at[j],
                recv_sem=recv_sems.at[j], device_id=(x, y, 1 - c), device_id_type=MESH)

        chips = _other_chips(x, y)
        for j, chip in enumerate(chips):
            copy(j, chip, c).start()
        for j, chip in enumerate(chips):
            copy(j, chip, c).wait_send()
            copy(j, chip, 1 - c).wait_recv()

    return pl.pallas_call(
        body, name=name, in_specs=[ANY], out_specs=ANY, out_shape=jax.ShapeDtypeStruct(buf.shape, buf.dtype),
        input_output_aliases={0: 0},
        scratch_shapes=[pltpu.SemaphoreType.DMA((3,))] * 2,
        compiler_params=pltpu.CompilerParams(collective_id=collective_id),
    )(buf)


def _piece_rows(ref, k):
    p = ref.shape[0] // 8
    return ref.at[pl.ds(pl.multiple_of(k * p, 32 // jnp.dtype(ref.dtype).itemsize), p), :]


def _exchange_start(name, arrays, collective_id):
    n = len(arrays)
    zones = [lax.empty((7, a.shape[0] // 8, a.shape[1]), a.dtype) for a in arrays]

    def body(*refs):
        srcs, lands = refs[:n], refs[n:2 * n]
        send, recv, token = refs[2 * n:3 * n], refs[3 * n:4 * n], refs[-1]
        x, y, c = _place()
        _handshake([_peer(x, y, c, m) for m in range(1, 8)])
        for a, (src, land) in enumerate(zip(srcs, lands)):
            for m in range(1, 8):
                px, py, pc = _peer(x, y, c, m)
                pltpu.make_async_remote_copy(
                    src_ref=_piece_rows(src, 4 * px + 2 * py + pc), dst_ref=land.at[m - 1], send_sem=send[a].at[m - 1],
                    recv_sem=recv[a].at[m - 1], device_id=(px, py, pc), device_id_type=MESH).start()
        token[...] = jnp.zeros_like(token)

    outs = pl.pallas_call(
        body, name=name, in_specs=[HBM] * (2 * n),
        out_specs=[SEM] * (2 * n) + [HBM] * (2 * n) + [pl.BlockSpec(memory_space=pltpu.VMEM)],
        out_shape=[pltpu.SemaphoreType.DMA((7,))] * (2 * n) + [pltpu.HBM(a.shape, a.dtype) for a in arrays + zones]
        + [jax.ShapeDtypeStruct((8, 128), F32)],
        input_output_aliases={a: 2 * n + a for a in range(2 * n)},
        compiler_params=_split_copy(collective_id),
    )(*[_in_hbm(a) for a in arrays + zones])
    return outs[:n], outs[n:2 * n], outs[2 * n:3 * n], outs[3 * n:4 * n], outs[-1]


def _exchange_wait(name, started, after):
    send_sems, recv_sems, arrays, zones, _ = started
    n = len(arrays)

    def body(*refs):
        srcs, lands = refs[:n], refs[n:2 * n]
        send, recv = refs[2 * n:3 * n], refs[3 * n:4 * n]
        x, y, c = _place()
        for a, (src, land) in enumerate(zip(srcs, lands)):
            for m in range(1, 8):
                px, py, pc = _peer(x, y, c, m)
                copy = pltpu.make_async_remote_copy(
                    src_ref=_piece_rows(src, 4 * px + 2 * py + pc), dst_ref=land.at[m - 1], send_sem=send[a].at[m - 1],
                    recv_sem=recv[a].at[m - 1], device_id=(px, py, pc), device_id_type=MESH)
                copy.wait_send()
                copy.wait_recv()

    outs = pl.pallas_call(
        body, name=name, in_specs=[HBM] * (2 * n) + [SEM] * (2 * n) + [ANY], out_specs=[HBM] * (2 * n),
        out_shape=[pltpu.HBM(a.shape, a.dtype) for a in list(arrays) + list(zones)],
        input_output_aliases={a: a for a in range(2 * n)}, compiler_params=SPLIT_COPY,
    )(*arrays, *zones, *send_sems, *recv_sems, after)
    return outs[n:]


def _sum_pieces(name, weights, place_arr, dests=None):
    steps = 2
    flat = [item for items in weights for item in items]
    n = len(flat)

    def body(place_ref, *refs):
        outs = iter(refs[len(refs) - len(weights):])
        k = 0
        for items in weights:
            out_ref = next(outs)
            for layer, _, _ in items:
                total = refs[k][...]
                for m in range(7):
                    total = total + refs[n + k][m].astype(F32)
                if len(items) == DEPTH:
                    out_ref[layer] = total
                else:
                    out_ref[...] = total
                k += 1

    def out_spec(items):
        _, own, _ = items[0]
        t, cols = own.shape[0] // steps, own.shape[1]
        if len(items) == DEPTH:
            return pl.BlockSpec((DEPTH, t, cols), lambda i, place: (0, place[1] * steps + i, 0))
        layer = items[0][0]
        return pl.BlockSpec((None, t, cols), lambda i, place: (layer, place[1] * steps + i, 0))

    owns = [own for _, own, _ in flat]
    dests = [] if dests is None else list(dests)
    return pl.pallas_call(
        body, name=name,
        grid_spec=pltpu.PrefetchScalarGridSpec(
            num_scalar_prefetch=1, grid=(steps,),
            in_specs=[pl.BlockSpec((o.shape[0] // steps, o.shape[1]), lambda i, place: (i, 0)) for o in owns]
            + [pl.BlockSpec((7, o.shape[0] // steps, o.shape[1]), lambda i, place: (0, i, 0)) for o in owns]
            + [ANY] * len(dests),
            out_specs=[out_spec(items) for items in weights]),
        out_shape=[jax.ShapeDtypeStruct((DEPTH, 2 * items[0][1].shape[0], items[0][1].shape[1]), F32)
                   for items in weights],
        input_output_aliases={1 + 2 * n + k: k for k in range(len(dests))},
        compiler_params=_params(),
    )(place_arr, *owns, *[recv for _, _, recv in flat], *dests)


def _sum_small(partials, recvs, place_arr):
    n = len(partials)

    def body(place_ref, *refs):
        for o_ref, r_ref, out_ref in zip(refs[:n], refs[n:2 * n], refs[2 * n:]):
            total = o_ref[...]
            for m in range(7):
                total = total + r_ref[m]
            out_ref[...] = total

    piece = lambda a: pl.BlockSpec((a.shape[0] // 8, a.shape[1]), lambda i, place: (place[0], 0))
    return pl.pallas_call(
        body, name="sum_small",
        grid_spec=pltpu.PrefetchScalarGridSpec(
            num_scalar_prefetch=1, grid=(1,),
            in_specs=[piece(a) for a in partials] + [pl.BlockSpec(r.shape, lambda i, place: (0, 0, 0)) for r in recvs],
            out_specs=[piece(a) for a in partials]),
        out_shape=[jax.ShapeDtypeStruct(a.shape, F32) for a in partials],
        compiler_params=_params(),
    )(place_arr, *partials, *recvs)


def _share(name, bufs, parts, gathered=(), collective_id=None):
    n, n_g = len(bufs), len(gathered)
    total = n + n_g

    def body(*refs):
        ins, outs = refs[:total], refs[total:2 * total]
        send_sems, recv_sems, send_g, recv_g = refs[2 * total:]
        x, y, c = _place()
        _handshake([_peer(x, y, c, m) for m in range(1, 8)] if gathered else [(x, y, 1 - c)])

        def half(ref, l, which):
            p = ref.shape[1] // 2
            return ref.at[l, pl.ds(pl.multiple_of(which * p, 8), p), :]

        def swap(k, which):
            a, l = parts[k]
            return pltpu.make_async_remote_copy(
                src_ref=half(ins[a], l, which), dst_ref=half(outs[a], l, which), send_sem=send_sems.at[k],
                recv_sem=recv_sems.at[k], device_id=(x, y, 1 - c), device_id_type=MESH)

        def spread(a, m, sender):
            k = 4 * sender[0] + 2 * sender[1] + sender[2]
            return pltpu.make_async_remote_copy(
                src_ref=_piece_rows(ins[n + a], k), dst_ref=_piece_rows(outs[n + a], k), send_sem=send_g.at[7 * a + m - 1],
                recv_sem=recv_g.at[7 * a + m - 1], device_id=_peer(x, y, c, m), device_id_type=MESH)

        for k in range(len(parts)):
            swap(k, c).start()
        for a in range(n_g):
            for m in range(1, 8):
                spread(a, m, (x, y, c)).start()
        for k in range(len(parts)):
            swap(k, c).wait_send()
            swap(k, 1 - c).wait_recv()
        for a in range(n_g):
            for m in range(1, 8):
                spread(a, m, (x, y, c)).wait_send()
                spread(a, m, _peer(x, y, c, m)).wait_recv()

    arrays = list(bufs) + list(gathered)
    return pl.pallas_call(
        body, name=name, in_specs=[ANY] * total, out_specs=[ANY] * total,
        out_shape=[jax.ShapeDtypeStruct(b.shape, F32) for b in arrays],
        input_output_aliases={a: a for a in range(total)},
        scratch_shapes=[pltpu.SemaphoreType.DMA((max(len(parts), 1),))] * 2
        + [pltpu.SemaphoreType.DMA((max(7 * n_g, 1),))] * 2,
        compiler_params=pltpu.CompilerParams(collective_id=collective_id),
    )(*arrays)


def _adamw_math(w, g, m, v):
    nm = ADAM_B1 * m + (1.0 - ADAM_B1) * g
    nv = ADAM_B2 * v + (1.0 - ADAM_B2) * (g * g)
    m_hat = nm / (1.0 - ADAM_B1 ** ADAM_STEP)
    v_hat = nv / (1.0 - ADAM_B2 ** ADAM_STEP)
    return -ADAM_LR * (m_hat / (jnp.sqrt(v_hat) + ADAM_EPS) + ADAM_WD * w), nm, nv


def _adamw(name, w, g, m, v, rows_per_step, first=0, count=None, dests=None, deps=()):
    layers, rows, cols = w.shape
    count = layers if count is None else count

    def body(w_ref, g_ref, m_ref, v_ref, *rest):
        d_ref, nm_ref, nv_ref, g_out_ref = rest[-4:]
        d_ref[...], nm_ref[...], nv_ref[...] = _adamw_math(w_ref[...], g_ref[...], m_ref[...], v_ref[...])
        g_out_ref[...] = g_ref[...]

    spec = pl.BlockSpec((1, rows_per_step, cols), lambda l, i: (first + l, i, 0))
    shape = jax.ShapeDtypeStruct(w.shape, F32)
    dests = () if dests is None else tuple(dests)
    return pl.pallas_call(
        body, name=name, grid=(count, rows // rows_per_step),
        in_specs=[spec] * 4 + [ANY] * (len(dests) + len(deps)), out_specs=[spec] * 4, out_shape=[shape] * 4,
        input_output_aliases={4 + k: k for k in range(len(dests))},
        compiler_params=_params(("arbitrary", "arbitrary")),
    )(w, g, m, v, *dests, *deps)


def _pack_misc(pool_scale, sinks, norm_pre, norm_post):
    sink_rows = jnp.zeros((DEPTH, 8, 128), F32).at[:, 0, 0:N_HEADS].set(sinks).reshape(2 * 8, 128)
    return jnp.concatenate([pool_scale.reshape(8, 128), norm_pre.reshape(16, 128), norm_post.reshape(16, 128),
                            sink_rows, jnp.zeros((8, 128), F32)], axis=0)


def _adamw_small(w, g, m, v, pool):
    def body(w_ref, g_ref, m_ref, v_ref, pw_ref, pg_ref, pm_ref, pv_ref, *rest):
        outs, pool_outs, (d_ref, nm_ref, nv_ref) = rest[:17], rest[17:21], rest[21:]
        pool_outs[0][...] = pg_ref[...]
        pool_outs[1][...], pool_outs[2][...], pool_outs[3][...] = _adamw_math(
            pw_ref[...], pg_ref[...], pm_ref[...], pv_ref[...])
        d_ref[...], nm_ref[...], nv_ref[...] = _adamw_math(w_ref[...], g_ref[...], m_ref[...], v_ref[...])
        for k, src in enumerate([g_ref, d_ref, nm_ref, nv_ref]):
            scale, sinks, pre, post = outs[4 * k:4 * k + 4]
            for l in range(DEPTH):
                for j in range(4):
                    scale[l:l + 1, j * 128:(j + 1) * 128] = src[MISC_SCALE + 4 * l + j:MISC_SCALE + 4 * l + j + 1, :]
                for j in range(8):
                    pre[l:l + 1, j * 128:(j + 1) * 128] = src[MISC_PRE + 8 * l + j:MISC_PRE + 8 * l + j + 1, :]
                    post[l:l + 1, j * 128:(j + 1) * 128] = src[MISC_POST + 8 * l + j:MISC_POST + 8 * l + j + 1, :]
                sinks[l:l + 1, :] = src[MISC_SINKS + 8 * l:MISC_SINKS + 8 * l + 1, 0:N_HEADS]
        outs[16][...] = g_ref[MISC_LOSS:MISC_LOSS + 1, 0:1]

    vmem = pl.BlockSpec(memory_space=pltpu.VMEM)
    shapes = [(DEPTH, D_POOL), (DEPTH, N_HEADS), (DEPTH, D), (DEPTH, D)] * 4 + [(1, 1)]
    shapes += [pool[0].shape] * 4
    return pl.pallas_call(
        body, name="adamw_small", in_specs=[vmem] * 8, out_specs=[vmem] * 21,
        out_shape=[jax.ShapeDtypeStruct(s, F32) for s in shapes],
        scratch_shapes=[pltpu.VMEM((MISC_ROWS, 128), F32)] * 3,
    )(w, g, m, v, *pool)


def kernel(x, w_in, pool_w, pool_scale, attn_sinks, w_out, norm_pre, norm_post, loss_target, m_w_in, m_pool_w, m_pool_scale, m_attn_sinks, m_w_out, m_norm_pre, m_norm_post, v_w_in, v_pool_w, v_pool_scale, v_attn_sinks, v_w_out, v_norm_pre, v_norm_post):
    cx, cy, cc = _place()
    chip_arr = jnp.reshape(2 * cx + cy, (1,)).astype(jnp.int32)
    place_arr = jnp.stack([4 * cx + 2 * cy + cc, cc]).astype(jnp.int32)
    t = lambda a: jnp.transpose(a, (0, 2, 1))
    w_in_t = t(w_in)
    xs, target = x[0], loss_target[0]
    pool_w_b = pool_w.astype(BF16)
    tables = _attention_tables()
    scale3 = pool_scale.reshape(DEPTH, 1, D_POOL)
    pre3 = norm_pre.reshape(DEPTH, 1, D)
    post3 = norm_post.reshape(DEPTH, 1, D)

    (wi0,) = _place_cast("place_w_in0", w_in_t, chip_arr, 288, [0])
    first = _gather_start("gather_start_first", [wi0], halved=(0,), collective_id=ID_GATHER_FIRST)
    (wi1,) = _place_cast("place_w_in1", w_in_t, chip_arr, 288, [1], deps=(first[3],))
    wo = _place_cast("place_w_out", w_out, chip_arr, 256, [0, 1], deps=(first[3],))
    rest = _gather_start("gather_start_rest", [wi1, wo[0], wo[1]], halved=(0,), collective_id=ID_GATHER_REST)
    send, recv, bufs = [first[k] + rest[k] for k in range(3)]
    order = {(0, "in"): 0, (1, "in"): 1, (0, "out"): 2, (1, "out"): 3}

    saved = []
    packed = [_pack_misc(pool_scale, attn_sinks, norm_pre, norm_post),
              _pack_misc(m_pool_scale, m_attn_sinks, m_norm_pre, m_norm_post),
              _pack_misc(v_pool_scale, v_attn_sinks, v_norm_pre, v_norm_post)]
    after = (first[3], rest[3], pool_w_b, *tables, scale3, pre3, post3, *packed)
    below = None
    for l in range(DEPTH):
        k = order[l, "in"]
        w_in_l = _forward_halves(f"forward_w_in{l}", _gather_wait(f"gather_wait_in{l}", bufs[k], send[k], recv[k], after,
                                                                 halved=True), collective_id=ID_FORWARD[l])
        if below is None:
            pu, pg, q, kv, ag = _fwd_in(l, xs, pre3, w_in_l)
        else:
            y, xs, pu, pg, q, kv, ag = _fwd_in(l, xs, pre3, w_in_l, below)
            saved[l - 1][7] = y
        cat = _fwd_mix(l, pu, pg, q, kv, ag, pool_w_b, scale3, attn_sinks, tables)
        k = order[l, "out"]
        w_out_l = _gather_wait(f"gather_wait_out{l}", bufs[k], send[k], recv[k], (cat,))
        saved.append([xs, pu, pg, q, kv, ag, cat, None, w_in_l, w_out_l])
        below, after = (cat, w_out_l, post3), (w_out_l,)

    x_in, pu, pg, q, kv, ag, cat, y, w_in_l, w_out_l = saved[1]
    dcat, dw_out1, dw_out1_b, dg_post1, loss, xs = _bwd_out(1, cat, w_out_l, post3, place_arr, x=x_in, target=target)
    ex1_out = _exchange_start("exchange_start_out1", [dw_out1_b], ID_OUT1)
    dproj, dpw, dsc1, dsink1 = _bwd_mix(1, pu, pg, q, kv, ag, dcat, pool_w_b, scale3, attn_sinks, tables,
                                        deps=(ex1_out[4],))
    dx, dg_pre1, dw_in1, dw_in1_b = _bwd_in_dx(1, dproj, w_in_l, x_in, pre3, xs, dw_place=place_arr)
    ex1_in = _exchange_start("exchange_start_in1", [dw_in1_b], ID_IN1)

    x_in, pu, pg, q, kv, ag, cat, y, w_in_l, w_out_l = saved[0]
    dcat, dw_out0, dw_out0_b, dg_post0 = _bwd_out(0, cat, w_out_l, post3, place_arr, dxn=dx, y=y, deps=(ex1_in[4],))
    ex0_out = _exchange_start("exchange_start_out0", [dw_out0_b], ID_OUT0)
    dproj, dpw, dsc0, dsink0 = _bwd_mix(0, pu, pg, q, kv, ag, dcat, pool_w_b, scale3, attn_sinks, tables,
                                        deps=(ex0_out[4],), dpw_dest=dpw)
    dw_in0, dw_in0_b = _bwd_in_dw(0, dproj, x_in, pre3, place_arr)
    ex0_in = _exchange_start("exchange_start_in0", [dw_in0_b], ID_IN0)

    grad_x, dg_pre0 = _bwd_in_dx(0, dproj, w_in_l, x_in, pre3, dx, deps=(ex0_in[4],))
    small = [dpw.reshape(DEPTH * 4 * 128, 128),
             jnp.concatenate([dsc0, dsc1, dg_pre0, dg_pre1, dg_post0, dg_post1, dsink0, dsink1, loss], axis=0)]
    ex_small = _exchange_start("exchange_start_small", small, ID_SMALL)
    (recv_out1,) = _exchange_wait("exchange_wait_out1", ex1_out, ex_small[4])
    (recv_in1,) = _exchange_wait("exchange_wait_in1", ex1_in, recv_out1)
    g_in, g_out = _sum_pieces("sum_pieces_1", [[(1, dw_in1, recv_in1)], [(1, dw_out1, recv_out1)]], place_arr)
    (recv_out0,) = _exchange_wait("exchange_wait_out0", ex0_out, g_out)
    (g_out,) = _sum_pieces("sum_pieces_out0", [[(0, dw_out0, recv_out0)]], place_arr, dests=[g_out])
    g_in, g_out = _share("share_a", [g_in, g_out], [(0, 1), (1, 0), (1, 1)], collective_id=ID_SHARE_A)
    m_in_t, v_in_t = t(m_w_in), t(v_w_in)
    d_out, nm_out, nv_out, grad_w_out = _adamw("adamw_w_out", w_out, g_out, m_w_out, v_w_out, 256)
    upd_in = _adamw("adamw_w_in1", w_in_t, g_in, m_in_t, v_in_t, 288, first=1, count=1, deps=(d_out,))

    (recv_in0,) = _exchange_wait("exchange_wait_in0", ex0_in, upd_in[0])
    recv_small = _exchange_wait("exchange_wait_small", ex_small, recv_in0)
    (g_in,) = _sum_pieces("sum_pieces_in0", [[(0, dw_in0, recv_in0)]], place_arr, dests=[g_in])
    g_in, g_pw, g_misc = _share("share_b", [g_in], [(0, 0)], _sum_small(small, recv_small, place_arr),
                                collective_id=ID_SHARE_B)
    d_in, nm_in, nv_in, grad_w_in_t = _adamw("adamw_w_in0", w_in_t, g_in, m_in_t, v_in_t, 288, first=0, count=1,
                                             dests=upd_in)
    flat = lambda a: a.reshape(DEPTH * 4 * 128, 128)
    small_out = _adamw_small(packed[0], g_misc, packed[1], packed[2],
                             (flat(pool_w), g_pw, flat(m_pool_w), flat(v_pool_w)))
    (g_sc, g_sk, g_pre, g_post, d_sc, d_sk, d_pre, d_post,
     m_sc, m_sk, m_pre, m_post, v_sc, v_sk, v_pre, v_post, loss_sum) = small_out[:17]
    g_pw, d_pw, m_pw, v_pw = [a.reshape(pool_w.shape) for a in small_out[17:]]
    return (loss_sum[0, 0], grad_x[None], t(grad_w_in_t), g_pw, g_sc, g_sk, grad_w_out, g_pre, g_post,
            t(d_in), d_pw, d_sc, d_sk, d_out, d_pre, d_post,
            t(nm_in), m_pw, m_sc, m_sk, nm_out, m_pre, m_post,
            t(nv_in), v_pw, v_sc, v_sk, nv_out, v_pre, v_post)
```

```python
import jax
import jax.numpy as jnp
from jax import lax
from jax.experimental import pallas as pl
from jax.experimental.pallas import tpu as pltpu

F32 = jnp.float32
BF16 = jnp.bfloat16

S = 2048
D = 1024
DEPTH = 2
D_POOL = 512
POOL_WINDOWS = (2, 4, 8, 16)
N_HEADS = 8
D_IN = 2304
N_SHARDS = 4
W_IN_SHARD = D_IN // N_SHARDS
W_OUT_SHARD = D // N_SHARDS
BLK = 128
NB = S // BLK
HALO = 16
PAD = 8
EPS = 1e-6
NEG_INF = -1e30
C_PU, C_PG, C_Q, C_K, C_V, C_AG = 0, 512, 1024, 1536, 1664, 1792

ADAM_LR = 0.001
ADAM_B1 = 0.9
ADAM_B2 = 0.999
ADAM_EPS = 1e-08
ADAM_WD = 0.01
ADAM_STEP = 10

TM = 512
VMEM_LIMIT = 56 * 1024 * 1024

NT = (((1,), (1,)), ((), ()))
TN = (((0,), (0,)), ((), ()))

MESH = pl.DeviceIdType.MESH
ANY = pl.BlockSpec(memory_space=pl.ANY)

ID_FORWARD = (0, 1)
(ID_SHARE_A, ID_SHARE_B, ID_GATHER_FIRST, ID_GATHER_REST, ID_OUT1, ID_IN1, ID_OUT0, ID_IN0, ID_SMALL) = range(2, 11)

MISC_SCALE, MISC_PRE, MISC_POST, MISC_SINKS, MISC_LOSS = 0, 8, 24, 40, 56
MISC_ROWS = 64


def _params(sem=("arbitrary",)):
    return pltpu.CompilerParams(dimension_semantics=sem, vmem_limit_bytes=VMEM_LIMIT)


def _sigmoid(v):
    return 1.0 / (1.0 + jnp.exp(-v))


def _rows8(v):
    r, c = v.shape
    return v.reshape(r // 8, 8, c).sum(axis=0)


def _layer(l, *shape):
    zeros = (0,) * len(shape)
    return pl.BlockSpec((None,) + shape, lambda i: (l,) + zeros)


def _whole(shape):
    zeros = (0,) * len(shape)
    return pl.BlockSpec(shape, lambda i: zeros, pipeline_mode=pl.Buffered(1))


def _fwd_in(l, x, g_pre, w_in_t, below=None):
    fused = below is not None

    def body(x_ref, g_ref, w_ref, *rest):
        if fused:
            cat_ref, wo_ref, gp_ref, y_ref, xn_ref = rest[:5]
            y = jnp.dot(cat_ref[...], wo_ref[...], preferred_element_type=F32)
            y_ref[...] = y
            xt = x_ref[...] + y * lax.rsqrt(jnp.mean(y * y, axis=-1, keepdims=True) + EPS) * gp_ref[...]
            xn_ref[...] = xt
        else:
            xt = x_ref[...]
        pu_ref, pg_ref, q_ref, kv_ref, ag_ref = rest[-5:]
        r = lax.rsqrt(jnp.mean(xt * xt, axis=-1, keepdims=True) + EPS)
        h = (xt * r * g_ref[...]).astype(BF16)

        def proj(lo, hi):
            return lax.dot_general(h, w_ref[lo:hi, :], NT, preferred_element_type=F32)

        pu_ref[...] = proj(C_PU, C_PG)
        pg_ref[...] = proj(C_PG, C_Q)
        q_ref[...] = proj(C_Q, C_K).astype(BF16)
        kv_ref[...] = proj(C_K, C_AG).astype(BF16)
        ag_ref[...] = proj(C_AG, D_IN)

    row = lambda w: pl.BlockSpec((TM, w), lambda i: (i, 0))
    act = jax.ShapeDtypeStruct((S, D), F32)
    return pl.pallas_call(
        body, name="fwd_out_in" if fused else "fwd_in", grid=(S // TM,),
        in_specs=[row(D), _layer(l, 1, D), _whole((D_IN, D))]
        + ([row(D), _whole((D, D)), _layer(l - 1, 1, D)] if fused else []),
        out_specs=[row(D)] * (2 * fused) + [row(512), row(512), row(512), row(256), row(512)],
        out_shape=[act] * (2 * fused)
        + [jax.ShapeDtypeStruct((S, 512), F32), jax.ShapeDtypeStruct((S, 512), F32),
           jax.ShapeDtypeStruct((S, 512), BF16), jax.ShapeDtypeStruct((S, 256), BF16),
           jax.ShapeDtypeStruct((S, 512), F32)],
        compiler_params=_params(),
    )(x, g_pre, w_in_t, *(below if fused else ()))


LOG2E = 1.4426950408889634
SCORE_SCALE = 0.125 * LOG2E


def _attention_tables():
    qi = jnp.arange(BLK)[:, None]
    kj = jnp.arange(BLK)[None, :]
    dist = ((qi - kj) % BLK).astype(F32)
    slopes = jnp.exp2(-jnp.arange(1, N_HEADS + 1, dtype=F32))
    bias = -(slopes * LOG2E)[:, None, None] * dist[None]
    first = jnp.where(kj > qi, NEG_INF, bias)
    return jnp.stack([first, bias]), (kj <= qi).astype(BF16)


def _own_block_mask():
    return lax.broadcasted_iota(jnp.int32, (BLK, BLK), 1) <= lax.broadcasted_iota(jnp.int32, (BLK, BLK), 0)


def _merge(full, own):
    return jnp.where(own, full[:, BLK:], full[:, :BLK])


def _spread(v, tri):
    own = v * tri
    return jnp.concatenate([v - own, own], axis=1)


def _head_variants(cur, prev):
    both = jnp.concatenate([prev, cur], axis=0).astype(F32)
    swapped = pltpu.roll(both, 64, axis=1)
    low = lax.broadcasted_iota(jnp.int32, both.shape, 1) < 64
    zero = jnp.zeros_like(both)
    return ((jnp.where(low, both, zero).astype(BF16), jnp.where(low, zero, swapped).astype(BF16)),
            (jnp.where(low, swapped, zero).astype(BF16), jnp.where(low, zero, both).astype(BF16)))


def _head_of(hkv, t, half):
    return hkv * 4 + 2 * t + half


def _rows(v, t):
    return v[t * BLK:(t + 1) * BLK]


def _stack_tiles(ref, hkv, offset=0):
    lo = offset + 2 * hkv * 128
    return jnp.concatenate([ref[:, lo:lo + 128], ref[:, lo + 128:lo + 256]], axis=0)


def _scores(q2, k_var, own):
    s = {}
    for hkv in range(2):
        for half in range(2):
            full = lax.dot_general(q2[hkv], k_var[hkv][half], NT, preferred_element_type=F32)
            for t in range(2):
                s[hkv, t, half] = _merge(_rows(full, t), own)
    return s


def _softmax(s, bias, sink):
    s = s * SCORE_SCALE + bias
    sink2 = sink * LOG2E
    m = jnp.maximum(jnp.max(s, axis=-1, keepdims=True), sink2)
    p = jnp.exp2(s - m)
    e_sink = jnp.exp2(sink2 - m)
    inv = 1.0 / (jnp.sum(p, axis=-1, keepdims=True) + e_sink)
    return p * inv, e_sink * inv


def _spread_pair(v, hkv, half, tri):
    return jnp.concatenate([_spread(v[hkv, t, half].astype(BF16), tri) for t in range(2)], axis=0)


POOL_ROWS = PAD + HALO + BLK


def _window_sums(src_ref, tmp_refs, trailing):
    lo, hi = (PAD, POOL_ROWS) if trailing else (0, HALO + BLK)
    cur = src_ref
    for level in range(len(POOL_WINDOWS)):
        lanes = slice(level * 128, 512)
        shift = -(1 << level) if trailing else (1 << level)
        dst = tmp_refs[level % 2]
        dst[lo:hi, lanes] = cur[lo:hi, lanes] + cur[lo + shift:hi + shift, lanes]
        cur = dst


def _pool_block(ext_ref, tmp_refs, i, g, w):
    lanes = slice(g * 128, (g + 1) * 128)
    rows = slice(PAD + HALO, POOL_ROWS)
    t = (i * BLK + lax.broadcasted_iota(jnp.int32, (BLK, 1), 0)).astype(F32)
    inv = 1.0 / jnp.minimum(t + 1.0, float(w))
    return tmp_refs[g % 2][rows, lanes] * inv - ext_ref[rows, lanes], inv


def _fwd_mix(l, pu, pg, q, kv, ag, pool_w, pool_scale, sinks, tables):
    bias, tri = tables

    def body(pu_ref, pup_ref, pg_ref, q_ref, kv_ref, kvp_ref, ag_ref, pw_ref, sc_ref, sink_ref, bias_ref, tri_ref,
             cat_ref, ext_ref, *tmp_refs):
        i = pl.program_id(0)

        @pl.when(i == 0)
        def _():
            for ref in (ext_ref, *tmp_refs):
                ref[0:PAD, :] = jnp.zeros((PAD, 512), F32)

        ext_ref[PAD:PAD + HALO, :] = jnp.where(i > 0, pup_ref[...], 0.0)
        ext_ref[PAD + HALO:POOL_ROWS, :] = pu_ref[...]
        _window_sums(ext_ref, tmp_refs, True)
        for g, w in enumerate(POOL_WINDOWS):
            lanes = slice(g * 128, (g + 1) * 128)
            pooled, _ = _pool_block(ext_ref, tmp_refs, i, g, w)
            mixed = jnp.dot(pooled.astype(BF16), pw_ref[g], preferred_element_type=F32)
            gate = pg_ref[:, lanes]
            cat_ref[:, lanes] = (mixed * sc_ref[:, lanes] * (gate * _sigmoid(gate))).astype(BF16)

        own = _own_block_mask()
        tri = tri_ref[...]
        k_var = _head_variants(kv_ref[:, 0:128], kvp_ref[:, 0:128])
        v_var = _head_variants(kv_ref[:, 128:256], kvp_ref[:, 128:256])
        s = _scores([_stack_tiles(q_ref, hkv) for hkv in range(2)], k_var, own)
        p = {}
        for (hkv, t, half), s_head in s.items():
            head = _head_of(hkv, t, half)
            p[hkv, t, half], _ = _softmax(s_head, bias_ref[head], sink_ref[l, head])
        for hkv in range(2):
            o2 = jnp.zeros((2 * BLK, 128), F32)
            for half in range(2):
                o2 = o2 + jnp.dot(_spread_pair(p, hkv, half, tri), v_var[hkv][half], preferred_element_type=F32)
            for t in range(2):
                lo = (2 * hkv + t) * 128
                gate = ag_ref[:, lo:lo + 128]
                cat_ref[:, D_POOL + lo:D_POOL + lo + 128] = (_rows(o2, t) * (gate * _sigmoid(gate))).astype(BF16)

    blk = lambda w: pl.BlockSpec((BLK, w), lambda i: (i, 0))
    prev = lambda w: pl.BlockSpec((BLK, w), lambda i: (jnp.maximum(i - 1, 0), 0))
    halo = pl.BlockSpec((HALO, 512), lambda i: (jnp.maximum(i * (BLK // HALO) - 1, 0), 0))
    return pl.pallas_call(
        body, name="fwd_mix", grid=(NB,),
        in_specs=[blk(512), halo, blk(512), blk(512), blk(256), prev(256), blk(512),
                  _layer(l, 4, 128, 128), _layer(l, 1, 512), pl.BlockSpec(memory_space=pltpu.SMEM),
                  pl.BlockSpec((None, N_HEADS, BLK, BLK), lambda i: (jnp.minimum(i, 1), 0, 0, 0)), _whole((BLK, BLK))],
        out_specs=blk(D),
        out_shape=jax.ShapeDtypeStruct((S, D), BF16),
        scratch_shapes=[pltpu.VMEM((POOL_ROWS, 512), F32)] * 3,
        compiler_params=_params(),
    )(pu, pu, pg, q, kv, kv, ag, pool_w, pool_scale, sinks, bias, tri)


def _store_lane_rows(ref, acc):
    total = jnp.sum(acc, axis=0, keepdims=True)
    for k in range(ref.shape[0]):
        ref[k:k + 1, :] = total[:, k * 128:(k + 1) * 128]


def _own_piece(dw_ref, place_ref):
    p = dw_ref.shape[0] // 8
    return dw_ref[pl.ds(pl.multiple_of(place_ref[0] * p, 8), p), :]


def _bwd_out(l, cat, w_out, g_post, place_arr, dxn=None, y=None, x=None, target=None, deps=()):
    last = target is not None
    n_steps = S // TM

    def body(a_ref, b_ref, g_ref, cat_ref, w_ref, place_ref, *rest):
        dcat_ref, own_ref, dwb_ref, dg_ref = rest[len(deps):len(deps) + 4]
        rest = rest[len(deps) + 4:]
        acc_ref, dw_ref = rest[-2:]
        step = pl.program_id(0)

        @pl.when(step == 0)
        def _():
            dw_ref[...] = jnp.zeros_like(dw_ref)
            acc_ref[...] = jnp.zeros_like(acc_ref)

        cat = cat_ref[...]
        g = g_ref[...]
        y = jnp.dot(cat, w_ref[...], preferred_element_type=F32) if last else b_ref[...]
        r = lax.rsqrt(jnp.mean(y * y, axis=-1, keepdims=True) + EPS)
        if last:
            loss_ref, dx_ref, loss_acc_ref = rest[:3]
            err = a_ref[...] + y * r * g - b_ref[...]

            @pl.when(step == 0)
            def _():
                loss_acc_ref[...] = jnp.zeros_like(loss_acc_ref)

            loss_acc_ref[...] += _rows8(err * err)
            dz = err * (1.0 / D)
            dx_ref[...] = dz
        else:
            dz = a_ref[...]
        a = dz * g
        dy = r * a - y * (r * r * r) * jnp.mean(a * y, axis=-1, keepdims=True)
        acc_ref[...] += _rows8(dz * (y * r))
        dyb = dy.astype(BF16)
        dcat_ref[...] = lax.dot_general(dyb, w_ref[...], NT, preferred_element_type=F32)
        dw_ref[...] += lax.dot_general(cat, dyb, TN, preferred_element_type=F32)

        @pl.when(step == n_steps - 1)
        def _():
            _store_lane_rows(dg_ref, acc_ref[...])
            dwb_ref[...] = dw_ref[...].astype(BF16)
            own_ref[...] = _own_piece(dw_ref, place_ref)
            if last:
                loss_ref[...] = jnp.full((8, 128), (0.5 / D) * jnp.sum(loss_acc_ref[...]), F32)

    row = lambda: pl.BlockSpec((TM, D), lambda i: (i, 0))
    full = _whole
    return pl.pallas_call(
        body, name="out_loss_bwd" if last else "bwd_out", grid=(n_steps,),
        in_specs=[row(), row(), _layer(l, 1, D), row(), full((D, D)), pl.BlockSpec(memory_space=pltpu.SMEM)]
        + [ANY] * len(deps),
        out_specs=[row(), full((D // 8, D)), full((D, D)), full((8, 128))] + ([full((8, 128)), row()] if last else []),
        out_shape=[jax.ShapeDtypeStruct((S, D), F32), jax.ShapeDtypeStruct((D // 8, D), F32),
                   jax.ShapeDtypeStruct((D, D), BF16), jax.ShapeDtypeStruct((8, 128), F32)]
        + ([jax.ShapeDtypeStruct((8, 128), F32), jax.ShapeDtypeStruct((S, D), F32)] if last else []),
        scratch_shapes=([pltpu.VMEM((8, D), F32)] if last else []) + [pltpu.VMEM((8, D), F32), pltpu.VMEM((D, D), F32)],
        compiler_params=_params(),
    )(*((x, target) if last else (dxn, y)), g_post, cat, w_out, place_arr, *deps)


def _bwd_mix(l, pu, pg, q, kv, ag, dcat, pool_w, pool_scale, sinks, tables, deps=(), dpw_dest=None):
    bias, tri = tables
    deps = tuple(deps) + (() if dpw_dest is None else (dpw_dest,))

    def body(pu_ref, pup_ref, pg_ref, q_ref, kv_ref, kvp_ref, ag_ref, dcat_ref, pw_ref, sc_ref, sink_ref, bias_ref,
             tri_ref, *rest):
        dproj_ref, dpw_ref, dsc_ref, dsink_ref, ext_ref, dext_ref, tmp_a, tmp_b, dkv_ref = rest[len(deps):]
        tmp_refs = (tmp_a, tmp_b)
        step = pl.program_id(0)
        i = NB - 1 - step

        @pl.when(step == 0)
        def _():
            dpw_ref[...] = jnp.zeros_like(dpw_ref)
            dsc_ref[...] = jnp.zeros_like(dsc_ref)
            dsink_ref[...] = jnp.zeros_like(dsink_ref)
            for ref in (ext_ref, tmp_a, tmp_b):
                ref[0:PAD, :] = jnp.zeros((PAD, 512), F32)
            dext_ref[BLK:POOL_ROWS, :] = jnp.zeros((HALO + PAD, 512), F32)
            dkv_ref[...] = jnp.zeros_like(dkv_ref)

        ext_ref[PAD:PAD + HALO, :] = jnp.where(i > 0, pup_ref[...], 0.0)
        ext_ref[PAD + HALO:POOL_ROWS, :] = pu_ref[...]
        _window_sums(ext_ref, tmp_refs, True)
        dpooled = []
        for g, w in enumerate(POOL_WINDOWS):
            lanes = slice(g * 128, (g + 1) * 128)
            pooled, inv = _pool_block(ext_ref, tmp_refs, i, g, w)
            pooled_b = pooled.astype(BF16)
            mixed = jnp.dot(pooled_b, pw_ref[g], preferred_element_type=F32)
            scale = sc_ref[:, lanes]
            gate = pg_ref[:, lanes]
            sg = _sigmoid(gate)
            dpo = dcat_ref[:, lanes]
            dproj_ref[:, C_PG + g * 128:C_PG + (g + 1) * 128] = (
                dpo * (mixed * scale) * (sg * (1.0 + gate * (1.0 - sg)))).astype(BF16)
            dms = dpo * (gate * sg)
            dsc_ref[g:g + 1, :] += jnp.sum(dms * mixed, axis=0, keepdims=True)
            dmixed = (dms * scale).astype(BF16)
            dpw_ref[g] += lax.dot_general(pooled_b, dmixed, TN, preferred_element_type=F32)
            dpooled.append(lax.dot_general(dmixed, pw_ref[g], NT, preferred_element_type=F32))
            dext_ref[0:BLK, lanes] = dpooled[g] * inv
        _window_sums(dext_ref, tmp_refs, False)
        for g in range(len(POOL_WINDOWS)):
            lanes = slice(g * 128, (g + 1) * 128)
            dproj_ref[:, C_PU + g * 128:C_PU + (g + 1) * 128] = (tmp_refs[g % 2][0:BLK, lanes] - dpooled[g]).astype(BF16)
        dext_ref[BLK:BLK + HALO, :] = dext_ref[0:HALO, :]

        own = _own_block_mask()
        tri = tri_ref[...]
        k_var = _head_variants(kv_ref[:, 0:128], kvp_ref[:, 0:128])
        v_var = _head_variants(kv_ref[:, 128:256], kvp_ref[:, 128:256])
        q2 = [_stack_tiles(q_ref, hkv) for hkv in range(2)]
        s = _scores(q2, k_var, own)
        p, p_sink = {}, {}
        for key, s_head in s.items():
            head = _head_of(*key)
            p[key], p_sink[key] = _softmax(s_head, bias_ref[head], sink_ref[l, head])

        do2, p_b, dp = [], {}, {}
        for hkv in range(2):
            gate = _stack_tiles(ag_ref, hkv)
            sg = _sigmoid(gate)
            dca = _stack_tiles(dcat_ref, hkv, D_POOL)
            do2.append((dca * (gate * sg)).astype(BF16))
            o2 = jnp.zeros((2 * BLK, 128), F32)
            for half in range(2):
                p_b[hkv, half] = _spread_pair(p, hkv, half, tri)
                o2 = o2 + jnp.dot(p_b[hkv, half], v_var[hkv][half], preferred_element_type=F32)
                full = lax.dot_general(do2[hkv], v_var[hkv][half], NT, preferred_element_type=F32)
                for t in range(2):
                    dp[hkv, t, half] = _merge(_rows(full, t), own)
            dag = dca * o2 * (sg * (1.0 + gate * (1.0 - sg)))
            for t in range(2):
                lo = C_AG + (2 * hkv + t) * 128
                dproj_ref[:, lo:lo + 128] = _rows(dag, t).astype(BF16)

        ds = {}
        for key in p:
            delta = jnp.sum(p[key] * dp[key], axis=-1, keepdims=True)
            ds[key] = p[key] * (dp[key] - delta)
            head = _head_of(*key)
            dsink_ref[0:1, :] += jnp.where(lax.broadcasted_iota(jnp.int32, (1, 128), 1) == head,
                                           -jnp.sum(p_sink[key] * delta, axis=0, keepdims=True), 0.0)

        dk_acc = [[None, None], [None, None]]
        dv_acc = [[None, None], [None, None]]
        for hkv in range(2):
            dq2 = jnp.zeros((2 * BLK, 128), F32)
            for half in range(2):
                ds_b = _spread_pair(ds, hkv, half, tri)
                dq2 = dq2 + jnp.dot(ds_b, k_var[hkv][half], preferred_element_type=F32)
                dk_acc[hkv][half] = lax.dot_general(ds_b, q2[hkv], TN, preferred_element_type=F32)
                dv_acc[hkv][half] = lax.dot_general(p_b[hkv, half], do2[hkv], TN, preferred_element_type=F32)
            for t in range(2):
                lo = C_Q + (2 * hkv + t) * 128
                dproj_ref[:, lo:lo + 128] = (_rows(dq2, t) * 0.125).astype(BF16)

        low = lax.broadcasted_iota(jnp.int32, (2 * BLK, 128), 1) < 64

        def gather_heads(acc):
            return jnp.where(low, acc[0][0] + pltpu.roll(acc[0][1], 64, axis=1),
                             pltpu.roll(acc[1][0], 64, axis=1) + acc[1][1])

        dk = gather_heads(dk_acc) * 0.125
        dv = gather_heads(dv_acc)
        dproj_ref[:, C_K:C_V] = (dk[BLK:, :] + dkv_ref[:, 0:128]).astype(BF16)
        dproj_ref[:, C_V:C_AG] = (dv[BLK:, :] + dkv_ref[:, 128:256]).astype(BF16)
        dkv_ref[:, 0:128] = dk[:BLK, :]
        dkv_ref[:, 128:256] = dv[:BLK, :]

    rev = lambda w: pl.BlockSpec((BLK, w), lambda s: (NB - 1 - s, 0))
    prev = lambda w: pl.BlockSpec((BLK, w), lambda s: (jnp.maximum(NB - 2 - s, 0), 0))
    halo = pl.BlockSpec((HALO, 512), lambda s: (jnp.maximum((NB - 1 - s) * (BLK // HALO) - 1, 0), 0))
    return pl.pallas_call(
        body, name="bwd_mix", grid=(NB,),
        in_specs=[rev(512), halo, rev(512), rev(512), rev(256), prev(256), rev(512), rev(D),
                  _layer(l, 4, 128, 128), _layer(l, 1, 512), pl.BlockSpec(memory_space=pltpu.SMEM),
                  pl.BlockSpec((None, N_HEADS, BLK, BLK), lambda s: (jnp.minimum(NB - 1 - s, 1), 0, 0, 0)),
                  _whole((BLK, BLK))] + [ANY] * len(deps),
        out_specs=[rev(D_IN), _layer(l, 4, 128, 128),
                   pl.BlockSpec((4, 128), lambda s: (0, 0)), pl.BlockSpec((8, 128), lambda s: (0, 0))],
        out_shape=[jax.ShapeDtypeStruct((S, D_IN), BF16), jax.ShapeDtypeStruct((DEPTH, 4, 128, 128), F32),
                   jax.ShapeDtypeStruct((4, 128), F32), jax.ShapeDtypeStruct((8, 128), F32)],
        input_output_aliases={} if dpw_dest is None else {12 + len(deps): 1},
        scratch_shapes=[pltpu.VMEM((POOL_ROWS, 512), F32)] * 4 + [pltpu.VMEM((BLK, 256), F32)],
        compiler_params=_params(),
    )(pu, pu, pg, q, kv, kv, ag, dcat, pool_w, pool_scale, sinks, bias, tri, *deps)


def _bwd_in_dw(l, dproj, x, g_pre, place_arr, deps=()):
    n_steps = S // TM

    def body(dp_ref, x_ref, g_ref, place_ref, *rest):
        own_ref, dwb_ref, dw_ref = rest[len(deps):]
        step = pl.program_id(0)

        @pl.when(step == 0)
        def _():
            dw_ref[...] = jnp.zeros_like(dw_ref)

        xt = x_ref[...]
        r = lax.rsqrt(jnp.mean(xt * xt, axis=-1, keepdims=True) + EPS)
        h = (xt * r * g_ref[...]).astype(BF16)
        dw_ref[...] += lax.dot_general(dp_ref[...], h, TN, preferred_element_type=F32)

        @pl.when(step == n_steps - 1)
        def _():
            dwb_ref[...] = dw_ref[...].astype(BF16)
            own_ref[...] = _own_piece(dw_ref, place_ref)

    row = lambda w: pl.BlockSpec((TM, w), lambda i: (i, 0))
    full = _whole
    return pl.pallas_call(
        body, name="bwd_in_dw", grid=(n_steps,),
        in_specs=[row(D_IN), row(D), _layer(l, 1, D), pl.BlockSpec(memory_space=pltpu.SMEM)] + [ANY] * len(deps),
        out_specs=[full((D_IN // 8, D)), full((D_IN, D))],
        out_shape=[jax.ShapeDtypeStruct((D_IN // 8, D), F32), jax.ShapeDtypeStruct((D_IN, D), BF16)],
        scratch_shapes=[pltpu.VMEM((D_IN, D), F32)],
        compiler_params=_params(),
    )(dproj, x, g_pre, place_arr, *deps)


def _bwd_in_dx(l, dproj, w_in_t, x, g_pre, dres, deps=(), dw_place=None):
    n_steps = S // TM
    with_dw = dw_place is not None

    def body(dp_ref, w_ref, x_ref, g_ref, dres_ref, *rest):
        place_ref = rest[0] if with_dw else None
        rest = rest[with_dw + len(deps):]
        if with_dw:
            dx_ref, dg_ref, own_ref, dwb_ref, acc_ref, dw_ref = rest
        else:
            dx_ref, dg_ref, acc_ref = rest
        step = pl.program_id(0)

        @pl.when(step == 0)
        def _():
            acc_ref[...] = jnp.zeros_like(acc_ref)
            if with_dw:
                dw_ref[...] = jnp.zeros_like(dw_ref)

        g = g_ref[...]
        halves = [slice(k * (TM // 2), (k + 1) * (TM // 2)) for k in range(2)]
        dh = [jnp.dot(dp_ref[rows, :], w_ref[...], preferred_element_type=F32) for rows in halves]
        h = []
        for rows, dh_k in zip(halves, dh):
            xt = x_ref[rows, :]
            r = lax.rsqrt(jnp.mean(xt * xt, axis=-1, keepdims=True) + EPS)
            xn = xt * r
            acc_ref[...] += _rows8(dh_k * xn)
            a = dh_k * g
            dx_ref[rows, :] = dres_ref[rows, :] + (
                r * a - xt * (r * r * r) * jnp.mean(a * xt, axis=-1, keepdims=True))
            h.append((xn * g).astype(BF16))
        if with_dw:
            dw_ref[...] += lax.dot_general(dp_ref[...], jnp.concatenate(h, axis=0), TN, preferred_element_type=F32)

        @pl.when(step == n_steps - 1)
        def _():
            _store_lane_rows(dg_ref, acc_ref[...])
            if with_dw:
                dwb_ref[...] = dw_ref[...].astype(BF16)
                own_ref[...] = _own_piece(dw_ref, place_ref)

    row = lambda w: pl.BlockSpec((TM, w), lambda i: (i, 0))
    full = _whole
    dw_specs = [full((D_IN // 8, D)), full((D_IN, D))] if with_dw else []
    dw_shapes = [jax.ShapeDtypeStruct((D_IN // 8, D), F32), jax.ShapeDtypeStruct((D_IN, D), BF16)] if with_dw else []
    return pl.pallas_call(
        body, name="bwd_in" if with_dw else "bwd_in_dx", grid=(n_steps,),
        in_specs=[row(D_IN), full((D_IN, D)), row(D), _layer(l, 1, D), row(D)]
        + [pl.BlockSpec(memory_space=pltpu.SMEM)] * with_dw + [ANY] * len(deps),
        out_specs=[row(D), full((8, 128))] + dw_specs,
        out_shape=[jax.ShapeDtypeStruct((S, D), F32), jax.ShapeDtypeStruct((8, 128), F32)] + dw_shapes,
        scratch_shapes=[pltpu.VMEM((8, D), F32)] + [pltpu.VMEM((D_IN, D), F32)] * with_dw,
        compiler_params=_params(),
    )(dproj, w_in_t, x, g_pre, dres, *((dw_place,) if with_dw else ()), *deps)


HBM =pl.BlockSpec(memory_space=pltpu.HBM)
SEM = pl.BlockSpec(memory_space=pltpu.SEMAPHORE)
def _split_copy(collective_id=None):
    return pltpu.CompilerParams(has_side_effects=pltpu.SideEffectType.DATAFLOW_SIDE_EFFECTING,
                                collective_id=collective_id)


SPLIT_COPY = _split_copy()


def _in_hbm(a):
    return pltpu.with_memory_space_constraint(a, pltpu.HBM)

def _place():
    return lax.axis_index("x"), lax.axis_index("y"), lax.axis_index("c")


def _other_chips(x, y):
    return [(1 - x, y), (x, 1 - y), (1 - x, 1 - y)]


def _peer(x, y, c, m):
    return (x ^ (m >> 2), y ^ ((m >> 1) & 1), c ^ (m & 1))


SAME_CORE = (2, 4, 6)


def _place_cast(name, src, chip_arr, tile, layers, deps=()):
    _, n, cols = src.shape
    steps = n // tile
    k = len(layers)

    def body(chip_ref, *refs):
        for s_ref, o_ref in zip(refs[:k], refs[k + len(deps):]):
            o_ref[...] = s_ref[...].astype(BF16)

    def layer_spec(l):
        return pl.BlockSpec((None, tile, cols), lambda i, chip: (l, i, 0))

    return pl.pallas_call(
        body, name=name,
        grid_spec=pltpu.PrefetchScalarGridSpec(
            num_scalar_prefetch=1, grid=(steps,),
            in_specs=[layer_spec(l) for l in layers] + [ANY] * len(deps),
            out_specs=[pl.BlockSpec((tile, cols), lambda i, chip: (chip[0] * steps + i, 0))] * k),
        out_shape=[jax.ShapeDtypeStruct((N_SHARDS * n, cols), BF16)] * k,
        compiler_params=_params(),
    )(chip_arr, *[src] * k, *deps)


def _chip_rows(ref, chip, half=None):
    n = ref.shape[0] // N_SHARDS
    if half is None:
        return ref.at[pl.ds(pl.multiple_of(chip * n, 16), n), :]
    return ref.at[pl.ds(pl.multiple_of(chip * n + half * (n // 2), 16), n // 2), :]


def _gather_start(name, bufs, halved, collective_id):
    n = len(bufs)

    def body(*refs):
        ins, send, recv, token = refs[:n], refs[n:2 * n], refs[2 * n:3 * n], refs[-1]
        x, y, c = _place()
        _handshake([(*chip, c) for chip in _other_chips(x, y)])
        for a, buf in enumerate(ins):
            own = _chip_rows(buf, 2 * x + y, c if a in halved else None)
            for j, chip in enumerate(_other_chips(x, y)):
                pltpu.make_async_remote_copy(src_ref=own, dst_ref=own, send_sem=send[a].at[j], recv_sem=recv[a].at[j],
                                             device_id=(*chip, c), device_id_type=MESH).start()
        token[...] = jnp.zeros_like(token)

    outs = pl.pallas_call(
        body, name=name, in_specs=[HBM] * n,
        out_specs=[SEM] * (2 * n) + [HBM] * n + [pl.BlockSpec(memory_space=pltpu.VMEM)],
        out_shape=[pltpu.SemaphoreType.DMA((3,))] * (2 * n) + [pltpu.HBM(b.shape, b.dtype) for b in bufs]
        + [jax.ShapeDtypeStruct((8, 128), F32)],
        input_output_aliases={a: 2 * n + a for a in range(n)},
        compiler_params=_split_copy(collective_id),
    )(*[_in_hbm(b) for b in bufs])
    return outs[:n], outs[n:2 * n], outs[2 * n:3 * n], outs[-1]


def _gather_wait(name, buf, send_sem, recv_sem, after, halved=False):
    def body(buf_ref, send_ref, recv_ref, *rest):
        x, y, c = _place()
        half = c if halved else None
        own = _chip_rows(buf_ref, 2 * x + y, half)
        for j, chip in enumerate(_other_chips(x, y)):
            copy = pltpu.make_async_remote_copy(src_ref=own, dst_ref=_chip_rows(buf_ref, 2 * chip[0] + chip[1], half),
                                                send_sem=send_ref.at[j], recv_sem=recv_ref.at[j],
                                                device_id=(*chip, c), device_id_type=MESH)
            copy.wait_send()
            copy.wait_recv()

    return pl.pallas_call(
        body, name=name, in_specs=[HBM, SEM, SEM] + [ANY] * len(after), out_specs=HBM,
        out_shape=pltpu.HBM(buf.shape, buf.dtype), input_output_aliases={0: 0}, compiler_params=SPLIT_COPY,
    )(buf, send_sem, recv_sem, *after)


def _handshake(peers):
    barrier = pltpu.get_barrier_semaphore()
    for peer in peers:
        pl.semaphore_signal(barrier, inc=1, device_id=peer, device_id_type=MESH)
    pl.semaphore_wait(barrier, len(peers))


def _sibling_handshake(x, y, c):
    _handshake([(x, y, 1 - c)])


def _forward_halves(name, buf, collective_id):
    def body(in_ref, out_ref, send_sems, recv_sems):
        x, y, c = _place()
        _sibling_handshake(x, y, c)

        def copy(j, chip, half):
            rows = 2 * chip[0] + chip[1]
            return pltpu.make_async_remote_copy(
                src_ref=_chip_rows(in_ref, rows, half), dst_ref=_chip_rows(out_ref, rows, half), send_sem=send_sems.at[j],
                recv_sem=recv_sems.at[j], device_id=(x, y, 1 - c), device_id_type=MESH)

        chips = _other_chips(x, y)
        for j, chip in enumerate(chips):
            copy(j, chip, c).start()
        for j, chip in enumerate(chips):
            copy(j, chip, c).wait_send()
            copy(j, chip, 1 - c).wait_recv()

    return pl.pallas_call(
        body, name=name, in_specs=[ANY], out_specs=ANY, out_shape=jax.ShapeDtypeStruct(buf.shape, buf.dtype),
        input_output_aliases={0: 0},
        scratch_shapes=[pltpu.SemaphoreType.DMA((3,))] * 2,
        compiler_params=pltpu.CompilerParams(collective_id=collective_id),
    )(buf)


def _piece_rows(ref, k):
    p = ref.shape[0] // 8
    return ref.at[pl.ds(pl.multiple_of(k * p, 32 // jnp.dtype(ref.dtype).itemsize), p), :]


def _exchange_start(name, arrays, collective_id):
    n = len(arrays)
    zones = [lax.empty((7, a.shape[0] // 8, a.shape[1]), a.dtype) for a in arrays]

    def body(*refs):
        srcs, lands = refs[:n], refs[n:2 * n]
        send, recv, token = refs[2 * n:3 * n], refs[3 * n:4 * n], refs[-1]
        x, y, c = _place()
        _handshake([_peer(x, y, c, m) for m in range(1, 8)])
        for a, (src, land) in enumerate(zip(srcs, lands)):
            for m in range(1, 8):
                px, py, pc = _peer(x, y, c, m)
                pltpu.make_async_remote_copy(
                    src_ref=_piece_rows(src, 4 * px + 2 * py + pc), dst_ref=land.at[m - 1], send_sem=send[a].at[m - 1],
                    recv_sem=recv[a].at[m - 1], device_id=(px, py, pc), device_id_type=MESH).start()
        token[...] = jnp.zeros_like(token)

    outs = pl.pallas_call(
        body, name=name, in_specs=[HBM] * (2 * n),
        out_specs=[SEM] * (2 * n) + [HBM] * (2 * n) + [pl.BlockSpec(memory_space=pltpu.VMEM)],
        out_shape=[pltpu.SemaphoreType.DMA((7,))] * (2 * n) + [pltpu.HBM(a.shape, a.dtype) for a in arrays + zones]
        + [jax.ShapeDtypeStruct((8, 128), F32)],
        input_output_aliases={a: 2 * n + a for a in range(2 * n)},
        compiler_params=_split_copy(collective_id),
    )(*[_in_hbm(a) for a in arrays + zones])
    return outs[:n], outs[n:2 * n], outs[2 * n:3 * n], outs[3 * n:4 * n], outs[-1]


def _exchange_wait(name, started, after):
    send_sems, recv_sems, arrays, zones, _ = started
    n = len(arrays)

    def body(*refs):
        srcs, lands = refs[:n], refs[n:2 * n]
        send, recv = refs[2 * n:3 * n], refs[3 * n:4 * n]
        x, y, c = _place()
        for a, (src, land) in enumerate(zip(srcs, lands)):
            for m in range(1, 8):
                px, py, pc = _peer(x, y, c, m)
                copy = pltpu.make_async_remote_copy(
                    src_ref=_piece_rows(src, 4 * px + 2 * py + pc), dst_ref=land.at[m - 1], send_sem=send[a].at[m - 1],
                    recv_sem=recv[a].at[m - 1], device_id=(px, py, pc), device_id_type=MESH)
                copy.wait_send()
                copy.wait_recv()

    outs = pl.pallas_call(
        body, name=name, in_specs=[HBM] * (2 * n) + [SEM] * (2 * n) + [ANY], out_specs=[HBM] * (2 * n),
        out_shape=[pltpu.HBM(a.shape, a.dtype) for a in list(arrays) + list(zones)],
        input_output_aliases={a: a for a in range(2 * n)}, compiler_params=SPLIT_COPY,
    )(*arrays, *zones, *send_sems, *recv_sems, after)
    return outs[n:]


def _sum_pieces(name, weights, place_arr, dests=None):
    steps = 2
    flat = [item for items in weights for item in items]
    n = len(flat)

    def body(place_ref, *refs):
        outs = iter(refs[len(refs) - len(weights):])
        k = 0
        for items in weights:
            out_ref = next(outs)
            for layer, _, _ in items:
                total = refs[k][...]
                for m in range(7):
                    total = total + refs[n + k][m].astype(F32)
                if len(items) == DEPTH:
                    out_ref[layer] = total
                else:
                    out_ref[...] = total
                k += 1

    def out_spec(items):
        _, own, _ = items[0]
        t, cols = own.shape[0] // steps, own.shape[1]
        if len(items) == DEPTH:
            return pl.BlockSpec((DEPTH, t, cols), lambda i, place: (0, place[1] * steps + i, 0))
        layer = items[0][0]
        return pl.BlockSpec((None, t, cols), lambda i, place: (layer, place[1] * steps + i, 0))

    owns = [own for _, own, _ in flat]
    dests = [] if dests is None else list(dests)
    return pl.pallas_call(
        body, name=name,
        grid_spec=pltpu.PrefetchScalarGridSpec(
            num_scalar_prefetch=1, grid=(steps,),
            in_specs=[pl.BlockSpec((o.shape[0] // steps, o.shape[1]), lambda i, place: (i, 0)) for o in owns]
            + [pl.BlockSpec((7, o.shape[0] // steps, o.shape[1]), lambda i, place: (0, i, 0)) for o in owns]
            + [ANY] * len(dests),
            out_specs=[out_spec(items) for items in weights]),
        out_shape=[jax.ShapeDtypeStruct((DEPTH, 2 * items[0][1].shape[0], items[0][1].shape[1]), F32)
                   for items in weights],
        input_output_aliases={1 + 2 * n + k: k for k in range(len(dests))},
        compiler_params=_params(),
    )(place_arr, *owns, *[recv for _, _, recv in flat], *dests)


def _sum_small(partials, recvs, place_arr):
    n = len(partials)

    def body(place_ref, *refs):
        for o_ref, r_ref, out_ref in zip(refs[:n], refs[n:2 * n], refs[2 * n:]):
            total = o_ref[...]
            for m in range(7):
                total = total + r_ref[m]
            out_ref[...] = total

    piece = lambda a: pl.BlockSpec((a.shape[0] // 8, a.shape[1]), lambda i, place: (place[0], 0))
    return pl.pallas_call(
        body, name="sum_small",
        grid_spec=pltpu.PrefetchScalarGridSpec(
            num_scalar_prefetch=1, grid=(1,),
            in_specs=[piece(a) for a in partials] + [pl.BlockSpec(r.shape, lambda i, place: (0, 0, 0)) for r in recvs],
            out_specs=[piece(a) for a in partials]),
        out_shape=[jax.ShapeDtypeStruct(a.shape, F32) for a in partials],
        compiler_params=_params(),
    )(place_arr, *partials, *recvs)


def _share(name, bufs, parts, gathered=(), collective_id=None):
    n, n_g = len(bufs), len(gathered)
    total = n + n_g

    def body(*refs):
        ins, outs = refs[:total], refs[total:2 * total]
        send_sems, recv_sems, send_g, recv_g = refs[2 * total:]
        x, y, c = _place()
        _handshake([_peer(x, y, c, m) for m in (SAME_CORE if gathered else ()) + (1,)])

        def half(ref, l, which):
            p = ref.shape[1] // 2
            return ref.at[l, pl.ds(pl.multiple_of(which * p, 8), p), :]

        def swap(k, which):
            a, l = parts[k]
            return pltpu.make_async_remote_copy(
                src_ref=half(ins[a], l, which), dst_ref=half(outs[a], l, which), send_sem=send_sems.at[k],
                recv_sem=recv_sems.at[k], device_id=(x, y, 1 - c), device_id_type=MESH)

        def spread(a, m, sender, held, to):
            k = 4 * sender[0] + 2 * sender[1] + sender[2]
            return pltpu.make_async_remote_copy(
                src_ref=_piece_rows(held[n + a], k), dst_ref=_piece_rows(outs[n + a], k),
                send_sem=send_g.at[7 * a + m - 1], recv_sem=recv_g.at[7 * a + m - 1], device_id=to, device_id_type=MESH)

        me, sibling = (x, y, c), (x, y, 1 - c)
        for k in range(len(parts)):
            swap(k, c).start()
        def own(a, m):
            return spread(a, m, me, ins, _peer(x, y, c, m))

        def handed_on(a, m):
            return spread(a, m + 1, _peer(x, y, c, m), outs, sibling)

        for a in range(n_g):
            for m in SAME_CORE + (1,):
                own(a, m).start()
        for a in range(n_g):
            for m in SAME_CORE:
                spread(a, m, _peer(x, y, c, m), ins, _peer(x, y, c, m)).wait_recv()
                handed_on(a, m).start()
        for k in range(len(parts)):
            swap(k, c).wait_send()
            swap(k, 1 - c).wait_recv()
        for a in range(n_g):
            for m in SAME_CORE + (1,):
                own(a, m).wait_send()
            for m in SAME_CORE:
                handed_on(a, m).wait_send()
                spread(a, m + 1, _peer(x, y, c, m + 1), ins, sibling).wait_recv()
            spread(a, 1, sibling, ins, sibling).wait_recv()

    arrays = list(bufs) + list(gathered)
    return pl.pallas_call(
        body, name=name, in_specs=[ANY] * total, out_specs=[ANY] * total,
        out_shape=[jax.ShapeDtypeStruct(b.shape, F32) for b in arrays],
        input_output_aliases={a: a for a in range(total)},
        scratch_shapes=[pltpu.SemaphoreType.DMA((max(len(parts), 1),))] * 2
        + [pltpu.SemaphoreType.DMA((max(7 * n_g, 1),))] * 2,
        compiler_params=pltpu.CompilerParams(collective_id=collective_id),
    )(*arrays)


def _adamw_math(w, g, m, v):
    nm = ADAM_B1 * m + (1.0 - ADAM_B1) * g
    nv = ADAM_B2 * v + (1.0 - ADAM_B2) * (g * g)
    m_hat = nm / (1.0 - ADAM_B1 ** ADAM_STEP)
    v_hat = nv / (1.0 - ADAM_B2 ** ADAM_STEP)
    return -ADAM_LR * (m_hat / (jnp.sqrt(v_hat) + ADAM_EPS) + ADAM_WD * w), nm, nv


def _adamw(name, w, g, m, v, rows_per_step, first=0, count=None, dests=None, deps=()):
    layers, rows, cols = w.shape
    count = layers if count is None else count

    def body(w_ref, g_ref, m_ref, v_ref, *rest):
        d_ref, nm_ref, nv_ref, g_out_ref = rest[-4:]
        d_ref[...], nm_ref[...], nv_ref[...] = _adamw_math(w_ref[...], g_ref[...], m_ref[...], v_ref[...])
        g_out_ref[...] = g_ref[...]

    spec = pl.BlockSpec((1, rows_per_step, cols), lambda l, i: (first + l, i, 0))
    shape = jax.ShapeDtypeStruct(w.shape, F32)
    dests = () if dests is None else tuple(dests)
    return pl.pallas_call(
        body, name=name, grid=(count, rows // rows_per_step),
        in_specs=[spec] * 4 + [ANY] * (len(dests) + len(deps)), out_specs=[spec] * 4, out_shape=[shape] * 4,
        input_output_aliases={4 + k: k for k in range(len(dests))},
        compiler_params=_params(("arbitrary", "arbitrary")),
    )(w, g, m, v, *dests, *deps)


def _pack_misc(pool_scale, sinks, norm_pre, norm_post):
    sink_rows = jnp.zeros((DEPTH, 8, 128), F32).at[:, 0, 0:N_HEADS].set(sinks).reshape(2 * 8, 128)
    return jnp.concatenate([pool_scale.reshape(8, 128), norm_pre.reshape(16, 128), norm_post.reshape(16, 128),
                            sink_rows, jnp.zeros((8, 128), F32)], axis=0)


def _adamw_small(w, g, m, v, pool):
    def body(w_ref, g_ref, m_ref, v_ref, pw_ref, pg_ref, pm_ref, pv_ref, *rest):
        outs, pool_outs, (d_ref, nm_ref, nv_ref) = rest[:17], rest[17:21], rest[21:]
        pool_outs[0][...] = pg_ref[...]
        pool_outs[1][...], pool_outs[2][...], pool_outs[3][...] = _adamw_math(
            pw_ref[...], pg_ref[...], pm_ref[...], pv_ref[...])
        d_ref[...], nm_ref[...], nv_ref[...] = _adamw_math(w_ref[...], g_ref[...], m_ref[...], v_ref[...])
        for k, src in enumerate([g_ref, d_ref, nm_ref, nv_ref]):
            scale, sinks, pre, post = outs[4 * k:4 * k + 4]
            for l in range(DEPTH):
                for j in range(4):
                    scale[l:l + 1, j * 128:(j + 1) * 128] = src[MISC_SCALE + 4 * l + j:MISC_SCALE + 4 * l + j + 1, :]
                for j in range(8):
                    pre[l:l + 1, j * 128:(j + 1) * 128] = src[MISC_PRE + 8 * l + j:MISC_PRE + 8 * l + j + 1, :]
                    post[l:l + 1, j * 128:(j + 1) * 128] = src[MISC_POST + 8 * l + j:MISC_POST + 8 * l + j + 1, :]
                sinks[l:l + 1, :] = src[MISC_SINKS + 8 * l:MISC_SINKS + 8 * l + 1, 0:N_HEADS]
        outs[16][...] = g_ref[MISC_LOSS:MISC_LOSS + 1, 0:1]

    vmem = pl.BlockSpec(memory_space=pltpu.VMEM)
    shapes = [(DEPTH, D_POOL), (DEPTH, N_HEADS), (DEPTH, D), (DEPTH, D)] * 4 + [(1, 1)]
    shapes += [pool[0].shape] * 4
    return pl.pallas_call(
        body, name="adamw_small", in_specs=[vmem] * 8, out_specs=[vmem] * 21,
        out_shape=[jax.ShapeDtypeStruct(s, F32) for s in shapes],
        scratch_shapes=[pltpu.VMEM((MISC_ROWS, 128), F32)] * 3,
    )(w, g, m, v, *pool)


def kernel(x, w_in, pool_w, pool_scale, attn_sinks, w_out, norm_pre, norm_post, loss_target, m_w_in, m_pool_w, m_pool_scale, m_attn_sinks, m_w_out, m_norm_pre, m_norm_post, v_w_in, v_pool_w, v_pool_scale, v_attn_sinks, v_w_out, v_norm_pre, v_norm_post):
    cx, cy, cc = _place()
    chip_arr = jnp.reshape(2 * cx + cy, (1,)).astype(jnp.int32)
    place_arr = jnp.stack([4 * cx + 2 * cy + cc, cc]).astype(jnp.int32)
    t = lambda a: jnp.transpose(a, (0, 2, 1))
    w_in_t = t(w_in)
    xs, target = x[0], loss_target[0]
    pool_w_b = pool_w.astype(BF16)
    tables = _attention_tables()
    scale3 = pool_scale.reshape(DEPTH, 1, D_POOL)
    pre3 = norm_pre.reshape(DEPTH, 1, D)
    post3 = norm_post.reshape(DEPTH, 1, D)

    (wi0,) = _place_cast("place_w_in0", w_in_t, chip_arr, 288, [0])
    first = _gather_start("gather_start_first", [wi0], halved=(0,), collective_id=ID_GATHER_FIRST)
    (wi1,) = _place_cast("place_w_in1", w_in_t, chip_arr, 288, [1], deps=(first[3],))
    wo = _place_cast("place_w_out", w_out, chip_arr, 256, [0, 1], deps=(first[3],))
    rest = _gather_start("gather_start_rest", [wi1, wo[0], wo[1]], halved=(0,), collective_id=ID_GATHER_REST)
    send, recv, bufs = [first[k] + rest[k] for k in range(3)]
    order = {(0, "in"): 0, (1, "in"): 1, (0, "out"): 2, (1, "out"): 3}

    saved = []
    packed = [_pack_misc(pool_scale, attn_sinks, norm_pre, norm_post),
              _pack_misc(m_pool_scale, m_attn_sinks, m_norm_pre, m_norm_post),
              _pack_misc(v_pool_scale, v_attn_sinks, v_norm_pre, v_norm_post)]
    after = (first[3], rest[3], pool_w_b, *tables, scale3, pre3, post3, *packed)
    below = None
    for l in range(DEPTH):
        k = order[l, "in"]
        w_in_l = _forward_halves(f"forward_w_in{l}", _gather_wait(f"gather_wait_in{l}", bufs[k], send[k], recv[k], after,
                                                                 halved=True), collective_id=ID_FORWARD[l])
        if below is None:
            pu, pg, q, kv, ag = _fwd_in(l, xs, pre3, w_in_l)
        else:
            y, xs, pu, pg, q, kv, ag = _fwd_in(l, xs, pre3, w_in_l, below)
            saved[l - 1][7] = y
        cat = _fwd_mix(l, pu, pg, q, kv, ag, pool_w_b, scale3, attn_sinks, tables)
        k = order[l, "out"]
        w_out_l = _gather_wait(f"gather_wait_out{l}", bufs[k], send[k], recv[k], (cat,))
        saved.append([xs, pu, pg, q, kv, ag, cat, None, w_in_l, w_out_l])
        below, after = (cat, w_out_l, post3), (w_out_l,)

    x_in, pu, pg, q, kv, ag, cat, y, w_in_l, w_out_l = saved[1]
    dcat, dw_out1, dw_out1_b, dg_post1, loss, xs = _bwd_out(1, cat, w_out_l, post3, place_arr, x=x_in, target=target)
    ex1_out = _exchange_start("exchange_start_out1", [dw_out1_b], ID_OUT1)
    dproj, dpw, dsc1, dsink1 = _bwd_mix(1, pu, pg, q, kv, ag, dcat, pool_w_b, scale3, attn_sinks, tables,
                                        deps=(ex1_out[4],))
    dx, dg_pre1, dw_in1, dw_in1_b = _bwd_in_dx(1, dproj, w_in_l, x_in, pre3, xs, dw_place=place_arr)
    ex1_in = _exchange_start("exchange_start_in1", [dw_in1_b], ID_IN1)

    x_in, pu, pg, q, kv, ag, cat, y, w_in_l, w_out_l = saved[0]
    dcat, dw_out0, dw_out0_b, dg_post0 = _bwd_out(0, cat, w_out_l, post3, place_arr, dxn=dx, y=y, deps=(ex1_in[4],))
    ex0_out = _exchange_start("exchange_start_out0", [dw_out0_b], ID_OUT0)
    dproj, dpw, dsc0, dsink0 = _bwd_mix(0, pu, pg, q, kv, ag, dcat, pool_w_b, scale3, attn_sinks, tables,
                                        deps=(ex0_out[4],), dpw_dest=dpw)
    dw_in0, dw_in0_b = _bwd_in_dw(0, dproj, x_in, pre3, place_arr)
    ex0_in = _exchange_start("exchange_start_in0", [dw_in0_b], ID_IN0)

    grad_x, dg_pre0 = _bwd_in_dx(0, dproj, w_in_l, x_in, pre3, dx, deps=(ex0_in[4],))
    small = [dpw.reshape(DEPTH * 4 * 128, 128),
             jnp.concatenate([dsc0, dsc1, dg_pre0, dg_pre1, dg_post0, dg_post1, dsink0, dsink1, loss], axis=0)]
    ex_small = _exchange_start("exchange_start_small", small, ID_SMALL)
    (recv_out1,) = _exchange_wait("exchange_wait_out1", ex1_out, ex_small[4])
    (recv_in1,) = _exchange_wait("exchange_wait_in1", ex1_in, recv_out1)
    g_in, g_out = _sum_pieces("sum_pieces_1", [[(1, dw_in1, recv_in1)], [(1, dw_out1, recv_out1)]], place_arr)
    (recv_out0,) = _exchange_wait("exchange_wait_out0", ex0_out, g_out)
    (g_out,) = _sum_pieces("sum_pieces_out0", [[(0, dw_out0, recv_out0)]], place_arr, dests=[g_out])
    g_in, g_out = _share("share_a", [g_in, g_out], [(0, 1), (1, 0), (1, 1)], collective_id=ID_SHARE_A)
    m_in_t, v_in_t = t(m_w_in), t(v_w_in)
    d_out, nm_out, nv_out, grad_w_out = _adamw("adamw_w_out", w_out, g_out, m_w_out, v_w_out, 256)
    upd_in = _adamw("adamw_w_in1", w_in_t, g_in, m_in_t, v_in_t, 288, first=1, count=1, deps=(d_out,))

    (recv_in0,) = _exchange_wait("exchange_wait_in0", ex0_in, upd_in[0])
    recv_small = _exchange_wait("exchange_wait_small", ex_small, recv_in0)
    (g_in,) = _sum_pieces("sum_pieces_in0", [[(0, dw_in0, recv_in0)]], place_arr, dests=[g_in])
    g_in, g_pw, g_misc = _share("share_b", [g_in], [(0, 0)], _sum_small(small, recv_small, place_arr),
                                collective_id=ID_SHARE_B)
    d_in, nm_in, nv_in, grad_w_in_t = _adamw("adamw_w_in0", w_in_t, g_in, m_in_t, v_in_t, 288, first=0, count=1,
                                             dests=upd_in)
    flat = lambda a: a.reshape(DEPTH * 4 * 128, 128)
    small_out = _adamw_small(packed[0], g_misc, packed[1], packed[2],
                             (flat(pool_w), g_pw, flat(m_pool_w), flat(v_pool_w)))
    (g_sc, g_sk, g_pre, g_post, d_sc, d_sk, d_pre, d_post,
     m_sc, m_sk, m_pre, m_post, v_sc, v_sk, v_pre, v_post, loss_sum) = small_out[:17]
    g_pw, d_pw, m_pw, v_pw = [a.reshape(pool_w.shape) for a in small_out[17:]]
    return (loss_sum[0, 0], grad_x[None], t(grad_w_in_t), g_pw, g_sc, g_sk, grad_w_out, g_pre, g_post,
            t(d_in), d_pw, d_sc, d_sk, d_out, d_pre, d_post,
            t(nm_in), m_pw, m_sc, m_sk, nm_out, m_pre, m_post,
            t(nv_in), v_pw, v_sc, v_sk, nv_out, v_pre, v_post)
```

```python
import jax
import jax.numpy as jnp
from jax import lax
from jax.experimental import pallas as pl
from jax.experimental.pallas import tpu as pltpu

F32 = jnp.float32
BF16 = jnp.bfloat16

S = 2048
D = 1024
DEPTH = 2
D_POOL = 512
POOL_WINDOWS = (2, 4, 8, 16)
N_HEADS = 8
D_IN = 2304
N_SHARDS = 4
W_IN_SHARD = D_IN // N_SHARDS
W_OUT_SHARD = D // N_SHARDS
BLK = 128
NB = S // BLK
HALO = 16
PAD = 8
EPS = 1e-6
NEG_INF = -1e30
C_PU, C_PG, C_Q, C_K, C_V, C_AG = 0, 512, 1024, 1536, 1664, 1792

ADAM_LR = 0.001
ADAM_B1 = 0.9
ADAM_B2 = 0.999
ADAM_EPS = 1e-08
ADAM_WD = 0.01
ADAM_STEP = 10

TM = 512
VMEM_LIMIT = 56 * 1024 * 1024

NT = (((1,), (1,)), ((), ()))
TN = (((0,), (0,)), ((), ()))

MESH = pl.DeviceIdType.MESH
ANY = pl.BlockSpec(memory_space=pl.ANY)

ID_FORWARD = (0, 1)
(ID_SHARE_A, ID_SHARE_B, ID_GATHER_FIRST, ID_GATHER_REST, ID_OUT1, ID_IN1, ID_OUT0, ID_IN0, ID_SMALL) = range(2, 11)

MISC_SCALE, MISC_PRE, MISC_POST, MISC_SINKS, MISC_LOSS = 0, 8, 24, 40, 56
MISC_ROWS = 64


def _params(sem=("arbitrary",)):
    return pltpu.CompilerParams(dimension_semantics=sem, vmem_limit_bytes=VMEM_LIMIT)


def _sigmoid(v):
    return 1.0 / (1.0 + jnp.exp(-v))


def _rows8(v):
    r, c = v.shape
    return v.reshape(r // 8, 8, c).sum(axis=0)


def _layer(l, *shape):
    zeros = (0,) * len(shape)
    return pl.BlockSpec((None,) + shape, lambda i: (l,) + zeros)


def _whole(shape):
    zeros = (0,) * len(shape)
    return pl.BlockSpec(shape, lambda i: zeros, pipeline_mode=pl.Buffered(1))


def _fwd_in(l, x, g_pre, w_in_t, below=None):
    fused = below is not None

    def body(x_ref, g_ref, w_ref, *rest):
        if fused:
            cat_ref, wo_ref, gp_ref, y_ref, xn_ref = rest[:5]
            y = jnp.dot(cat_ref[...], wo_ref[...], preferred_element_type=F32)
            y_ref[...] = y
            xt = x_ref[...] + y * lax.rsqrt(jnp.mean(y * y, axis=-1, keepdims=True) + EPS) * gp_ref[...]
            xn_ref[...] = xt
        else:
            xt = x_ref[...]
        pu_ref, pg_ref, q_ref, kv_ref, ag_ref = rest[-5:]
        r = lax.rsqrt(jnp.mean(xt * xt, axis=-1, keepdims=True) + EPS)
        h = (xt * r * g_ref[...]).astype(BF16)

        def proj(lo, hi):
            return lax.dot_general(h, w_ref[lo:hi, :], NT, preferred_element_type=F32)

        pu_ref[...] = proj(C_PU, C_PG)
        pg_ref[...] = proj(C_PG, C_Q)
        q_ref[...] = proj(C_Q, C_K).astype(BF16)
        kv_ref[...] = proj(C_K, C_AG).astype(BF16)
        ag_ref[...] = proj(C_AG, D_IN)

    row = lambda w: pl.BlockSpec((TM, w), lambda i: (i, 0))
    act = jax.ShapeDtypeStruct((S, D), F32)
    return pl.pallas_call(
        body, name="fwd_out_in" if fused else "fwd_in", grid=(S // TM,),
        in_specs=[row(D), _layer(l, 1, D), _whole((D_IN, D))]
        + ([row(D), _whole((D, D)), _layer(l - 1, 1, D)] if fused else []),
        out_specs=[row(D)] * (2 * fused) + [row(512), row(512), row(512), row(256), row(512)],
        out_shape=[act] * (2 * fused)
        + [jax.ShapeDtypeStruct((S, 512), F32), jax.ShapeDtypeStruct((S, 512), F32),
           jax.ShapeDtypeStruct((S, 512), BF16), jax.ShapeDtypeStruct((S, 256), BF16),
           jax.ShapeDtypeStruct((S, 512), F32)],
        compiler_params=_params(),
    )(x, g_pre, w_in_t, *(below if fused else ()))


LOG2E = 1.4426950408889634
SCORE_SCALE = 0.125 * LOG2E


def _attention_tables():
    qi = jnp.arange(BLK)[:, None]
    kj = jnp.arange(BLK)[None, :]
    dist = ((qi - kj) % BLK).astype(F32)
    slopes = jnp.exp2(-jnp.arange(1, N_HEADS + 1, dtype=F32))
    bias = -(slopes * LOG2E)[:, None, None] * dist[None]
    first = jnp.where(kj > qi, NEG_INF, bias)
    return jnp.stack([first, bias]), (kj <= qi).astype(BF16)


def _own_block_mask():
    return lax.broadcasted_iota(jnp.int32, (BLK, BLK), 1) <= lax.broadcasted_iota(jnp.int32, (BLK, BLK), 0)


def _merge(full, own):
    return jnp.where(own, full[:, BLK:], full[:, :BLK])


def _spread(v, tri):
    own = v * tri
    return jnp.concatenate([v - own, own], axis=1)


def _head_variants(cur, prev):
    both = jnp.concatenate([prev, cur], axis=0).astype(F32)
    swapped = pltpu.roll(both, 64, axis=1)
    low = lax.broadcasted_iota(jnp.int32, both.shape, 1) < 64
    zero = jnp.zeros_like(both)
    return ((jnp.where(low, both, zero).astype(BF16), jnp.where(low, zero, swapped).astype(BF16)),
            (jnp.where(low, swapped, zero).astype(BF16), jnp.where(low, zero, both).astype(BF16)))


def _head_of(hkv, t, half):
    return hkv * 4 + 2 * t + half


def _rows(v, t):
    return v[t * BLK:(t + 1) * BLK]


def _stack_tiles(ref, hkv, offset=0):
    lo = offset + 2 * hkv * 128
    return jnp.concatenate([ref[:, lo:lo + 128], ref[:, lo + 128:lo + 256]], axis=0)


def _scores(q2, k_var, own):
    s = {}
    for hkv in range(2):
        for half in range(2):
            full = lax.dot_general(q2[hkv], k_var[hkv][half], NT, preferred_element_type=F32)
            for t in range(2):
                s[hkv, t, half] = _merge(_rows(full, t), own)
    return s


def _softmax(s, bias, sink):
    s = s * SCORE_SCALE + bias
    sink2 = sink * LOG2E
    m = jnp.maximum(jnp.max(s, axis=-1, keepdims=True), sink2)
    p = jnp.exp2(s - m)
    e_sink = jnp.exp2(sink2 - m)
    inv = 1.0 / (jnp.sum(p, axis=-1, keepdims=True) + e_sink)
    return p * inv, e_sink * inv


def _spread_pair(v, hkv, half, tri):
    return jnp.concatenate([_spread(v[hkv, t, half].astype(BF16), tri) for t in range(2)], axis=0)


POOL_ROWS = PAD + HALO + BLK


def _window_sums(src_ref, tmp_refs, trailing):
    lo, hi = (PAD, POOL_ROWS) if trailing else (0, HALO + BLK)
    cur = src_ref
    for level in range(len(POOL_WINDOWS)):
        lanes = slice(level * 128, 512)
        shift = -(1 << level) if trailing else (1 << level)
        dst = tmp_refs[level % 2]
        dst[lo:hi, lanes] = cur[lo:hi, lanes] + cur[lo + shift:hi + shift, lanes]
        cur = dst


def _pool_block(ext_ref, tmp_refs, i, g, w):
    lanes = slice(g * 128, (g + 1) * 128)
    rows = slice(PAD + HALO, POOL_ROWS)
    t = (i * BLK + lax.broadcasted_iota(jnp.int32, (BLK, 1), 0)).astype(F32)
    inv = 1.0 / jnp.minimum(t + 1.0, float(w))
    return tmp_refs[g % 2][rows, lanes] * inv - ext_ref[rows, lanes], inv


def _fwd_mix(l, pu, pg, q, kv, ag, pool_w, pool_scale, sinks, tables):
    bias, tri = tables

    def body(pu_ref, pup_ref, pg_ref, q_ref, kv_ref, kvp_ref, ag_ref, pw_ref, sc_ref, sink_ref, bias_ref, tri_ref,
             cat_ref, ext_ref, *tmp_refs):
        i = pl.program_id(0)

        @pl.when(i == 0)
        def _():
            for ref in (ext_ref, *tmp_refs):
                ref[0:PAD, :] = jnp.zeros((PAD, 512), F32)

        ext_ref[PAD:PAD + HALO, :] = jnp.where(i > 0, pup_ref[...], 0.0)
        ext_ref[PAD + HALO:POOL_ROWS, :] = pu_ref[...]
        _window_sums(ext_ref, tmp_refs, True)
        for g, w in enumerate(POOL_WINDOWS):
            lanes = slice(g * 128, (g + 1) * 128)
            pooled, _ = _pool_block(ext_ref, tmp_refs, i, g, w)
            mixed = jnp.dot(pooled.astype(BF16), pw_ref[g], preferred_element_type=F32)
            gate = pg_ref[:, lanes]
            cat_ref[:, lanes] = (mixed * sc_ref[:, lanes] * (gate * _sigmoid(gate))).astype(BF16)

        own = _own_block_mask()
        tri = tri_ref[...]
        k_var = _head_variants(kv_ref[:, 0:128], kvp_ref[:, 0:128])
        v_var = _head_variants(kv_ref[:, 128:256], kvp_ref[:, 128:256])
        s = _scores([_stack_tiles(q_ref, hkv) for hkv in range(2)], k_var, own)
        p = {}
        for (hkv, t, half), s_head in s.items():
            head = _head_of(hkv, t, half)
            p[hkv, t, half], _ = _softmax(s_head, bias_ref[head], sink_ref[l, head])
        for hkv in range(2):
            o2 = jnp.zeros((2 * BLK, 128), F32)
            for half in range(2):
                o2 = o2 + jnp.dot(_spread_pair(p, hkv, half, tri), v_var[hkv][half], preferred_element_type=F32)
            for t in range(2):
                lo = (2 * hkv + t) * 128
                gate = ag_ref[:, lo:lo + 128]
                cat_ref[:, D_POOL + lo:D_POOL + lo + 128] = (_rows(o2, t) * (gate * _sigmoid(gate))).astype(BF16)

    blk = lambda w: pl.BlockSpec((BLK, w), lambda i: (i, 0))
    prev = lambda w: pl.BlockSpec((BLK, w), lambda i: (jnp.maximum(i - 1, 0), 0))
    halo = pl.BlockSpec((HALO, 512), lambda i: (jnp.maximum(i * (BLK // HALO) - 1, 0), 0))
    return pl.pallas_call(
        body, name="fwd_mix", grid=(NB,),
        in_specs=[blk(512), halo, blk(512), blk(512), blk(256), prev(256), blk(512),
                  _layer(l, 4, 128, 128), _layer(l, 1, 512), pl.BlockSpec(memory_space=pltpu.SMEM),
                  pl.BlockSpec((None, N_HEADS, BLK, BLK), lambda i: (jnp.minimum(i, 1), 0, 0, 0)), _whole((BLK, BLK))],
        out_specs=blk(D),
        out_shape=jax.ShapeDtypeStruct((S, D), BF16),
        scratch_shapes=[pltpu.VMEM((POOL_ROWS, 512), F32)] * 3,
        compiler_params=_params(),
    )(pu, pu, pg, q, kv, kv, ag, pool_w, pool_scale, sinks, bias, tri)


def _store_lane_rows(ref, acc):
    total = jnp.sum(acc, axis=0, keepdims=True)
    for k in range(ref.shape[0]):
        ref[k:k + 1, :] = total[:, k * 128:(k + 1) * 128]


def _own_piece(dw_ref, place_ref):
    p = dw_ref.shape[0] // 8
    return dw_ref[pl.ds(pl.multiple_of(place_ref[0] * p, 8), p), :]


def _bwd_out(l, cat, w_out, g_post, place_arr, dxn=None, y=None, x=None, target=None, deps=()):
    last = target is not None
    n_steps = S // TM

    def body(a_ref, b_ref, g_ref, cat_ref, w_ref, place_ref, *rest):
        dcat_ref, own_ref, dwb_ref, dg_ref = rest[len(deps):len(deps) + 4]
        rest = rest[len(deps) + 4:]
        acc_ref, dw_ref = rest[-2:]
        step = pl.program_id(0)

        @pl.when(step == 0)
        def _():
            dw_ref[...] = jnp.zeros_like(dw_ref)
            acc_ref[...] = jnp.zeros_like(acc_ref)

        cat = cat_ref[...]
        g = g_ref[...]
        y = jnp.dot(cat, w_ref[...], preferred_element_type=F32) if last else b_ref[...]
        r = lax.rsqrt(jnp.mean(y * y, axis=-1, keepdims=True) + EPS)
        if last:
            loss_ref, dx_ref, loss_acc_ref = rest[:3]
            err = a_ref[...] + y * r * g - b_ref[...]

            @pl.when(step == 0)
            def _():
                loss_acc_ref[...] = jnp.zeros_like(loss_acc_ref)

            loss_acc_ref[...] += _rows8(err * err)
            dz = err * (1.0 / D)
            dx_ref[...] = dz
        else:
            dz = a_ref[...]
        a = dz * g
        dy = r * a - y * (r * r * r) * jnp.mean(a * y, axis=-1, keepdims=True)
        acc_ref[...] += _rows8(dz * (y * r))
        dyb = dy.astype(BF16)
        dcat_ref[...] = lax.dot_general(dyb, w_ref[...], NT, preferred_element_type=F32)
        dw_ref[...] += lax.dot_general(cat, dyb, TN, preferred_element_type=F32)

        @pl.when(step == n_steps - 1)
        def _():
            _store_lane_rows(dg_ref, acc_ref[...])
            dwb_ref[...] = dw_ref[...].astype(BF16)
            own_ref[...] = _own_piece(dw_ref, place_ref)
            if last:
                loss_ref[...] = jnp.full((8, 128), (0.5 / D) * jnp.sum(loss_acc_ref[...]), F32)

    row = lambda: pl.BlockSpec((TM, D), lambda i: (i, 0))
    full = _whole
    return pl.pallas_call(
        body, name="out_loss_bwd" if last else "bwd_out", grid=(n_steps,),
        in_specs=[row(), row(), _layer(l, 1, D), row(), full((D, D)), pl.BlockSpec(memory_space=pltpu.SMEM)]
        + [ANY] * len(deps),
        out_specs=[row(), full((D // 8, D)), full((D, D)), full((8, 128))] + ([full((8, 128)), row()] if last else []),
        out_shape=[jax.ShapeDtypeStruct((S, D), F32), jax.ShapeDtypeStruct((D // 8, D), F32),
                   jax.ShapeDtypeStruct((D, D), BF16), jax.ShapeDtypeStruct((8, 128), F32)]
        + ([jax.ShapeDtypeStruct((8, 128), F32), jax.ShapeDtypeStruct((S, D), F32)] if last else []),
        scratch_shapes=([pltpu.VMEM((8, D), F32)] if last else []) + [pltpu.VMEM((8, D), F32), pltpu.VMEM((D, D), F32)],
        compiler_params=_params(),
    )(*((x, target) if last else (dxn, y)), g_post, cat, w_out, place_arr, *deps)


def _bwd_mix(l, pu, pg, q, kv, ag, dcat, pool_w, pool_scale, sinks, tables, deps=(), dpw_dest=None):
    bias, tri = tables
    deps = tuple(deps) + (() if dpw_dest is None else (dpw_dest,))

    def body(pu_ref, pup_ref, pg_ref, q_ref, kv_ref, kvp_ref, ag_ref, dcat_ref, pw_ref, sc_ref, sink_ref, bias_ref,
             tri_ref, *rest):
        dproj_ref, dpw_ref, dsc_ref, dsink_ref, ext_ref, dext_ref, tmp_a, tmp_b, dkv_ref = rest[len(deps):]
        tmp_refs = (tmp_a, tmp_b)
        step = pl.program_id(0)
        i = NB - 1 - step

        @pl.when(step == 0)
        def _():
            dpw_ref[...] = jnp.zeros_like(dpw_ref)
            dsc_ref[...] = jnp.zeros_like(dsc_ref)
            dsink_ref[...] = jnp.zeros_like(dsink_ref)
            for ref in (ext_ref, tmp_a, tmp_b):
                ref[0:PAD, :] = jnp.zeros((PAD, 512), F32)
            dext_ref[BLK:POOL_ROWS, :] = jnp.zeros((HALO + PAD, 512), F32)
            dkv_ref[...] = jnp.zeros_like(dkv_ref)

        ext_ref[PAD:PAD + HALO, :] = jnp.where(i > 0, pup_ref[...], 0.0)
        ext_ref[PAD + HALO:POOL_ROWS, :] = pu_ref[...]
        _window_sums(ext_ref, tmp_refs, True)
        dpooled = []
        for g, w in enumerate(POOL_WINDOWS):
            lanes = slice(g * 128, (g + 1) * 128)
            pooled, inv = _pool_block(ext_ref, tmp_refs, i, g, w)
            pooled_b = pooled.astype(BF16)
            mixed = jnp.dot(pooled_b, pw_ref[g], preferred_element_type=F32)
            scale = sc_ref[:, lanes]
            gate = pg_ref[:, lanes]
            sg = _sigmoid(gate)
            dpo = dcat_ref[:, lanes]
            dproj_ref[:, C_PG + g * 128:C_PG + (g + 1) * 128] = (
                dpo * (mixed * scale) * (sg * (1.0 + gate * (1.0 - sg)))).astype(BF16)
            dms = dpo * (gate * sg)
            dsc_ref[g:g + 1, :] += jnp.sum(dms * mixed, axis=0, keepdims=True)
            dmixed = (dms * scale).astype(BF16)
            dpw_ref[g] += lax.dot_general(pooled_b, dmixed, TN, preferred_element_type=F32)
            dpooled.append(lax.dot_general(dmixed, pw_ref[g], NT, preferred_element_type=F32))
            dext_ref[0:BLK, lanes] = dpooled[g] * inv
        _window_sums(dext_ref, tmp_refs, False)
        for g in range(len(POOL_WINDOWS)):
            lanes = slice(g * 128, (g + 1) * 128)
            dproj_ref[:, C_PU + g * 128:C_PU + (g + 1) * 128] = (tmp_refs[g % 2][0:BLK, lanes] - dpooled[g]).astype(BF16)
        dext_ref[BLK:BLK + HALO, :] = dext_ref[0:HALO, :]

        own = _own_block_mask()
        tri = tri_ref[...]
        k_var = _head_variants(kv_ref[:, 0:128], kvp_ref[:, 0:128])
        v_var = _head_variants(kv_ref[:, 128:256], kvp_ref[:, 128:256])
        q2 = [_stack_tiles(q_ref, hkv) for hkv in range(2)]
        s = _scores(q2, k_var, own)
        p, p_sink = {}, {}
        for key, s_head in s.items():
            head = _head_of(*key)
            p[key], p_sink[key] = _softmax(s_head, bias_ref[head], sink_ref[l, head])

        do2, p_b, dp = [], {}, {}
        for hkv in range(2):
            gate = _stack_tiles(ag_ref, hkv)
            sg = _sigmoid(gate)
            dca = _stack_tiles(dcat_ref, hkv, D_POOL)
            do2.append((dca * (gate * sg)).astype(BF16))
            o2 = jnp.zeros((2 * BLK, 128), F32)
            for half in range(2):
                p_b[hkv, half] = _spread_pair(p, hkv, half, tri)
                o2 = o2 + jnp.dot(p_b[hkv, half], v_var[hkv][half], preferred_element_type=F32)
                full = lax.dot_general(do2[hkv], v_var[hkv][half], NT, preferred_element_type=F32)
                for t in range(2):
                    dp[hkv, t, half] = _merge(_rows(full, t), own)
            dag = dca * o2 * (sg * (1.0 + gate * (1.0 - sg)))
            for t in range(2):
                lo = C_AG + (2 * hkv + t) * 128
                dproj_ref[:, lo:lo + 128] = _rows(dag, t).astype(BF16)

        ds = {}
        for key in p:
            delta = jnp.sum(p[key] * dp[key], axis=-1, keepdims=True)
            ds[key] = p[key] * (dp[key] - delta)
            head = _head_of(*key)
            dsink_ref[0:1, :] += jnp.where(lax.broadcasted_iota(jnp.int32, (1, 128), 1) == head,
                                           -jnp.sum(p_sink[key] * delta, axis=0, keepdims=True), 0.0)

        dk_acc = [[None, None], [None, None]]
        dv_acc = [[None, None], [None, None]]
        for hkv in range(2):
            dq2 = jnp.zeros((2 * BLK, 128), F32)
            for half in range(2):
                ds_b = _spread_pair(ds, hkv, half, tri)
                dq2 = dq2 + jnp.dot(ds_b, k_var[hkv][half], preferred_element_type=F32)
                dk_acc[hkv][half] = lax.dot_general(ds_b, q2[hkv], TN, preferred_element_type=F32)
                dv_acc[hkv][half] = lax.dot_general(p_b[hkv, half], do2[hkv], TN, preferred_element_type=F32)
            for t in range(2):
                lo = C_Q + (2 * hkv + t) * 128
                dproj_ref[:, lo:lo + 128] = (_rows(dq2, t) * 0.125).astype(BF16)

        low = lax.broadcasted_iota(jnp.int32, (2 * BLK, 128), 1) < 64

        def gather_heads(acc):
            return jnp.where(low, acc[0][0] + pltpu.roll(acc[0][1], 64, axis=1),
                             pltpu.roll(acc[1][0], 64, axis=1) + acc[1][1])

        dk = gather_heads(dk_acc) * 0.125
        dv = gather_heads(dv_acc)
        dproj_ref[:, C_K:C_V] = (dk[BLK:, :] + dkv_ref[:, 0:128]).astype(BF16)
        dproj_ref[:, C_V:C_AG] = (dv[BLK:, :] + dkv_ref[:, 128:256]).astype(BF16)
        dkv_ref[:, 0:128] = dk[:BLK, :]
        dkv_ref[:, 128:256] = dv[:BLK, :]

    rev = lambda w: pl.BlockSpec((BLK, w), lambda s: (NB - 1 - s, 0))
    prev = lambda w: pl.BlockSpec((BLK, w), lambda s: (jnp.maximum(NB - 2 - s, 0), 0))
    halo = pl.BlockSpec((HALO, 512), lambda s: (jnp.maximum((NB - 1 - s) * (BLK // HALO) - 1, 0), 0))
    return pl.pallas_call(
        body, name="bwd_mix", grid=(NB,),
        in_specs=[rev(512), halo, rev(512), rev(512), rev(256), prev(256), rev(512), rev(D),
                  _layer(l, 4, 128, 128), _layer(l, 1, 512), pl.BlockSpec(memory_space=pltpu.SMEM),
                  pl.BlockSpec((None, N_HEADS, BLK, BLK), lambda s: (jnp.minimum(NB - 1 - s, 1), 0, 0, 0)),
                  _whole((BLK, BLK))] + [ANY] * len(deps),
        out_specs=[rev(D_IN), _layer(l, 4, 128, 128),
                   pl.BlockSpec((4, 128), lambda s: (0, 0)), pl.BlockSpec((8, 128), lambda s: (0, 0))],
        out_shape=[jax.ShapeDtypeStruct((S, D_IN), BF16), jax.ShapeDtypeStruct((DEPTH, 4, 128, 128), F32),
                   jax.ShapeDtypeStruct((4, 128), F32), jax.ShapeDtypeStruct((8, 128), F32)],
        input_output_aliases={} if dpw_dest is None else {12 + len(deps): 1},
        scratch_shapes=[pltpu.VMEM((POOL_ROWS, 512), F32)] * 4 + [pltpu.VMEM((BLK, 256), F32)],
        compiler_params=_params(),
    )(pu, pu, pg, q, kv, kv, ag, dcat, pool_w, pool_scale, sinks, bias, tri, *deps)


def _bwd_in_dw(l, dproj, x, g_pre, place_arr, deps=()):
    n_steps = S // TM

    def body(dp_ref, x_ref, g_ref, place_ref, *rest):
        own_ref, dwb_ref, dw_ref = rest[len(deps):]
        step = pl.program_id(0)

        @pl.when(step == 0)
        def _():
            dw_ref[...] = jnp.zeros_like(dw_ref)

        xt = x_ref[...]
        r = lax.rsqrt(jnp.mean(xt * xt, axis=-1, keepdims=True) + EPS)
        h = (xt * r * g_ref[...]).astype(BF16)
        dw_ref[...] += lax.dot_general(dp_ref[...], h, TN, preferred_element_type=F32)

        @pl.when(step == n_steps - 1)
        def _():
            dwb_ref[...] = dw_ref[...].astype(BF16)
            own_ref[...] = _own_piece(dw_ref, place_ref)

    row = lambda w: pl.BlockSpec((TM, w), lambda i: (i, 0))
    full = _whole
    return pl.pallas_call(
        body, name="bwd_in_dw", grid=(n_steps,),
        in_specs=[row(D_IN), row(D), _layer(l, 1, D), pl.BlockSpec(memory_space=pltpu.SMEM)] + [ANY] * len(deps),
        out_specs=[full((D_IN // 8, D)), full((D_IN, D))],
        out_shape=[jax.ShapeDtypeStruct((D_IN // 8, D), F32), jax.ShapeDtypeStruct((D_IN, D), BF16)],
        scratch_shapes=[pltpu.VMEM((D_IN, D), F32)],
        compiler_params=_params(),
    )(dproj, x, g_pre, place_arr, *deps)


def _bwd_in_dx(l, dproj, w_in_t, x, g_pre, dres, deps=(), dw_place=None):
    n_steps = S // TM
    with_dw = dw_place is not None

    def body(dp_ref, w_ref, x_ref, g_ref, dres_ref, *rest):
        place_ref = rest[0] if with_dw else None
        rest = rest[with_dw + len(deps):]
        if with_dw:
            dx_ref, dg_ref, own_ref, dwb_ref, acc_ref, dw_ref = rest
        else:
            dx_ref, dg_ref, acc_ref = rest
        step = pl.program_id(0)

        @pl.when(step == 0)
        def _():
            acc_ref[...] = jnp.zeros_like(acc_ref)
            if with_dw:
                dw_ref[...] = jnp.zeros_like(dw_ref)

        g = g_ref[...]
        halves = [slice(k * (TM // 2), (k + 1) * (TM // 2)) for k in range(2)]
        dh = [jnp.dot(dp_ref[rows, :], w_ref[...], preferred_element_type=F32) for rows in halves]
        h = []
        for rows, dh_k in zip(halves, dh):
            xt = x_ref[rows, :]
            r = lax.rsqrt(jnp.mean(xt * xt, axis=-1, keepdims=True) + EPS)
            xn = xt * r
            acc_ref[...] += _rows8(dh_k * xn)
            a = dh_k * g
            dx_ref[rows, :] = dres_ref[rows, :] + (
                r * a - xt * (r * r * r) * jnp.mean(a * xt, axis=-1, keepdims=True))
            h.append((xn * g).astype(BF16))
        if with_dw:
            dw_ref[...] += lax.dot_general(dp_ref[...], jnp.concatenate(h, axis=0), TN, preferred_element_type=F32)

        @pl.when(step == n_steps - 1)
        def _():
            _store_lane_rows(dg_ref, acc_ref[...])
            if with_dw:
                dwb_ref[...] = dw_ref[...].astype(BF16)
                own_ref[...] = _own_piece(dw_ref, place_ref)

    row = lambda w: pl.BlockSpec((TM, w), lambda i: (i, 0))
    full = _whole
    dw_specs = [full((D_IN // 8, D)), full((D_IN, D))] if with_dw else []
    dw_shapes = [jax.ShapeDtypeStruct((D_IN // 8, D), F32), jax.ShapeDtypeStruct((D_IN, D), BF16)] if with_dw else []
    return pl.pallas_call(
        body, name="bwd_in" if with_dw else "bwd_in_dx", grid=(n_steps,),
        in_specs=[row(D_IN), full((D_IN, D)), row(D), _layer(l, 1, D), row(D)]
        + [pl.BlockSpec(memory_space=pltpu.SMEM)] * with_dw + [ANY] * len(deps),
        out_specs=[row(D), full((8, 128))] + dw_specs,
        out_shape=[jax.ShapeDtypeStruct((S, D), F32), jax.ShapeDtypeStruct((8, 128), F32)] + dw_shapes,
        scratch_shapes=[pltpu.VMEM((8, D), F32)] + [pltpu.VMEM((D_IN, D), F32)] * with_dw,
        compiler_params=_params(),
    )(dproj, w_in_t, x, g_pre, dres, *((dw_place,) if with_dw else ()), *deps)


HBM =pl.BlockSpec(memory_space=pltpu.HBM)
SEM = pl.BlockSpec(memory_space=pltpu.SEMAPHORE)
def _split_copy(collective_id=None):
    return pltpu.CompilerParams(has_side_effects=pltpu.SideEffectType.DATAFLOW_SIDE_EFFECTING,
                                collective_id=collective_id)


SPLIT_COPY = _split_copy()


def _in_hbm(a):
    return pltpu.with_memory_space_constraint(a, pltpu.HBM)

def _place():
    return lax.axis_index("x"), lax.axis_index("y"), lax.axis_index("c")


def _other_chips(x, y):
    return [(1 - x, y), (x, 1 - y), (1 - x, 1 - y)]


def _peer(x, y, c, m):
    return (x ^ (m >> 2), y ^ ((m >> 1) & 1), c ^ (m & 1))


SAME_CORE = (2, 4, 6)


def _place_cast(name, src, chip_arr, tile, layers, deps=()):
    _, n, cols = src.shape
    steps = n // tile
    k = len(layers)

    def body(chip_ref, *refs):
        for s_ref, o_ref in zip(refs[:k], refs[k + len(deps):]):
            o_ref[...] = s_ref[...].astype(BF16)

    def layer_spec(l):
        return pl.BlockSpec((None, tile, cols), lambda i, chip: (l, i, 0))

    return pl.pallas_call(
        body, name=name,
        grid_spec=pltpu.PrefetchScalarGridSpec(
            num_scalar_prefetch=1, grid=(steps,),
            in_specs=[layer_spec(l) for l in layers] + [ANY] * len(deps),
            out_specs=[pl.BlockSpec((tile, cols), lambda i, chip: (chip[0] * steps + i, 0))] * k),
        out_shape=[jax.ShapeDtypeStruct((N_SHARDS * n, cols), BF16)] * k,
        compiler_params=_params(),
    )(chip_arr, *[src] * k, *deps)


def _chip_rows(ref, chip, half=None):
    n = ref.shape[0] // N_SHARDS
    if half is None:
        return ref.at[pl.ds(pl.multiple_of(chip * n, 16), n), :]
    return ref.at[pl.ds(pl.multiple_of(chip * n + half * (n // 2), 16), n // 2), :]


def _gather_start(name, bufs, halved, collective_id):
    n = len(bufs)

    def body(*refs):
        ins, send, recv, token = refs[:n], refs[n:2 * n], refs[2 * n:3 * n], refs[-1]
        x, y, c = _place()
        _handshake([(*chip, c) for chip in _other_chips(x, y)])
        for a, buf in enumerate(ins):
            own = _chip_rows(buf, 2 * x + y, c if a in halved else None)
            for j, chip in enumerate(_other_chips(x, y)):
                pltpu.make_async_remote_copy(src_ref=own, dst_ref=own, send_sem=send[a].at[j], recv_sem=recv[a].at[j],
                                             device_id=(*chip, c), device_id_type=MESH).start()
        token[...] = jnp.zeros_like(token)

    outs = pl.pallas_call(
        body, name=name, in_specs=[HBM] * n,
        out_specs=[SEM] * (2 * n) + [HBM] * n + [pl.BlockSpec(memory_space=pltpu.VMEM)],
        out_shape=[pltpu.SemaphoreType.DMA((3,))] * (2 * n) + [pltpu.HBM(b.shape, b.dtype) for b in bufs]
        + [jax.ShapeDtypeStruct((8, 128), F32)],
        input_output_aliases={a: 2 * n + a for a in range(n)},
        compiler_params=_split_copy(collective_id),
    )(*[_in_hbm(b) for b in bufs])
    return outs[:n], outs[n:2 * n], outs[2 * n:3 * n], outs[-1]


def _gather_wait(name, buf, send_sem, recv_sem, after, halved=False):
    def body(buf_ref, send_ref, recv_ref, *rest):
        x, y, c = _place()
        half = c if halved else None
        own = _chip_rows(buf_ref, 2 * x + y, half)
        for j, chip in enumerate(_other_chips(x, y)):
            copy = pltpu.make_async_remote_copy(src_ref=own, dst_ref=_chip_rows(buf_ref, 2 * chip[0] + chip[1], half),
                                                send_sem=send_ref.at[j], recv_sem=recv_ref.at[j],
                                                device_id=(*chip, c), device_id_type=MESH)
            copy.wait_send()
            copy.wait_recv()

    return pl.pallas_call(
        body, name=name, in_specs=[HBM, SEM, SEM] + [ANY] * len(after), out_specs=HBM,
        out_shape=pltpu.HBM(buf.shape, buf.dtype), input_output_aliases={0: 0}, compiler_params=SPLIT_COPY,
    )(buf, send_sem, recv_sem, *after)


def _handshake(peers):
    barrier = pltpu.get_barrier_semaphore()
    for peer in peers:
        pl.semaphore_signal(barrier, inc=1, device_id=peer, device_id_type=MESH)
    pl.semaphore_wait(barrier, len(peers))


def _sibling_handshake(x, y, c):
    _handshake([(x, y, 1 - c)])


def _forward_halves(name, buf, collective_id):
    def body(in_ref, out_ref, send_sems, recv_sems):
        x, y, c = _place()
        _sibling_handshake(x, y, c)

        def copy(j, chip, half):
            rows = 2 * chip[0] + chip[1]
            return pltpu.make_async_remote_copy(
                src_ref=_chip_rows(in_ref, rows, half), dst_ref=_chip_rows(out_ref, rows, half), send_sem=send_sems.at[j],
                recv_sem=recv_sems.at[j], device_id=(x, y, 1 - c), device_id_type=MESH)

        chips = _other_chips(x, y)
        for j, chip in enumerate(chips):
            copy(j, chip, c).start()
        for j, chip in enumerate(chips):
            copy(j, chip, c).wait_send()
            copy(j, chip, 1 - c).wait_recv()

    return pl.pallas_call(
        body, name=name, in_specs=[ANY], out_specs=ANY, out_shape=jax.ShapeDtypeStruct(buf.shape, buf.dtype),
        input_output_aliases={0: 0},
        scratch_shapes=[pltpu.SemaphoreType.DMA((3,))] * 2,
        compiler_params=pltpu.CompilerParams(collective_id=collective_id),
    )(buf)


def _piece_rows(ref, k):
    p = ref.shape[0] // 8
    return ref.at[pl.ds(pl.multiple_of(k * p, 32 // jnp.dtype(ref.dtype).itemsize), p), :]


def _exchange_start(name, arrays, collective_id):
    n = len(arrays)
    zones = [lax.empty((7, a.shape[0] // 8, a.shape[1]), a.dtype) for a in arrays]

    def body(*refs):
        srcs, lands = refs[:n], refs[n:2 * n]
        send, recv, token = refs[2 * n:3 * n], refs[3 * n:4 * n], refs[-1]
        x, y, c = _place()
        _handshake([_peer(x, y, c, m) for m in range(1, 8)])
        for a, (src, land) in enumerate(zip(srcs, lands)):
            for m in range(1, 8):
                px, py, pc = _peer(x, y, c, m)
                pltpu.make_async_remote_copy(
                    src_ref=_piece_rows(src, 4 * px + 2 * py + pc), dst_ref=land.at[m - 1], send_sem=send[a].at[m - 1],
                    recv_sem=recv[a].at[m - 1], device_id=(px, py, pc), device_id_type=MESH).start()
        token[...] = jnp.zeros_like(token)

    outs = pl.pallas_call(
        body, name=name, in_specs=[HBM] * (2 * n),
        out_specs=[SEM] * (2 * n) + [HBM] * (2 * n) + [pl.BlockSpec(memory_space=pltpu.VMEM)],
        out_shape=[pltpu.SemaphoreType.DMA((7,))] * (2 * n) + [pltpu.HBM(a.shape, a.dtype) for a in arrays + zones]
        + [jax.ShapeDtypeStruct((8, 128), F32)],
        input_output_aliases={a: 2 * n + a for a in range(2 * n)},
        compiler_params=_split_copy(collective_id),
    )(*[_in_hbm(a) for a in arrays + zones])
    return outs[:n], outs[n:2 * n], outs[2 * n:3 * n], outs[3 * n:4 * n], outs[-1]


def _exchange_wait(name, started, after, which=None, with_sent=False):
    which = range(len(started[2])) if which is None else which
    send_sems, recv_sems, arrays, zones = [[group[k] for k in which] for group in started[:4]]
    n = len(arrays)

    def body(*refs):
        srcs, lands = refs[:n], refs[n:2 * n]
        send, recv = refs[2 * n:3 * n], refs[3 * n:4 * n]
        x, y, c = _place()
        for a, (src, land) in enumerate(zip(srcs, lands)):
            for m in range(1, 8):
                px, py, pc = _peer(x, y, c, m)
                copy = pltpu.make_async_remote_copy(
                    src_ref=_piece_rows(src, 4 * px + 2 * py + pc), dst_ref=land.at[m - 1], send_sem=send[a].at[m - 1],
                    recv_sem=recv[a].at[m - 1], device_id=(px, py, pc), device_id_type=MESH)
                copy.wait_send()
                copy.wait_recv()

    outs = pl.pallas_call(
        body, name=name, in_specs=[HBM] * (2 * n) + [SEM] * (2 * n) + [ANY], out_specs=[HBM] * (2 * n),
        out_shape=[pltpu.HBM(a.shape, a.dtype) for a in list(arrays) + list(zones)],
        input_output_aliases={a: a for a in range(2 * n)}, compiler_params=SPLIT_COPY,
    )(*arrays, *zones, *send_sems, *recv_sems, after)
    return outs if with_sent else outs[n:]


def _sum_pieces(name, weights, place_arr, dests=None):
    steps = 2
    flat = [item for items in weights for item in items]
    n = len(flat)

    def body(place_ref, *refs):
        outs = iter(refs[len(refs) - len(weights):])
        k = 0
        for items in weights:
            out_ref = next(outs)
            for layer, _, _ in items:
                total = refs[k][...]
                for m in range(7):
                    total = total + refs[n + k][m].astype(F32)
                if len(items) == DEPTH:
                    out_ref[layer] = total
                else:
                    out_ref[...] = total
                k += 1

    def out_spec(items):
        _, own, _ = items[0]
        t, cols = own.shape[0] // steps, own.shape[1]
        if len(items) == DEPTH:
            return pl.BlockSpec((DEPTH, t, cols), lambda i, place: (0, place[1] * steps + i, 0))
        layer = items[0][0]
        return pl.BlockSpec((None, t, cols), lambda i, place: (layer, place[1] * steps + i, 0))

    owns = [own for _, own, _ in flat]
    dests = [] if dests is None else list(dests)
    return pl.pallas_call(
        body, name=name,
        grid_spec=pltpu.PrefetchScalarGridSpec(
            num_scalar_prefetch=1, grid=(steps,),
            in_specs=[pl.BlockSpec((o.shape[0] // steps, o.shape[1]), lambda i, place: (i, 0)) for o in owns]
            + [pl.BlockSpec((7, o.shape[0] // steps, o.shape[1]), lambda i, place: (0, i, 0)) for o in owns]
            + [ANY] * len(dests),
            out_specs=[out_spec(items) for items in weights]),
        out_shape=[jax.ShapeDtypeStruct((DEPTH, 2 * items[0][1].shape[0], items[0][1].shape[1]), F32)
                   for items in weights],
        input_output_aliases={1 + 2 * n + k: k for k in range(len(dests))},
        compiler_params=_params(),
    )(place_arr, *owns, *[recv for _, _, recv in flat], *dests)


def _sum_small(name, partials, recvs, place_arr):
    n = len(partials)

    def body(place_ref, *refs):
        for o_ref, r_ref, out_ref in zip(refs[:n], refs[n:2 * n], refs[2 * n:]):
            total = o_ref[...]
            for m in range(7):
                total = total + r_ref[m]
            out_ref[...] = total

    piece = lambda a: pl.BlockSpec((a.shape[0] // 8, a.shape[1]), lambda i, place: (place[0], 0))
    return pl.pallas_call(
        body, name=name,
        grid_spec=pltpu.PrefetchScalarGridSpec(
            num_scalar_prefetch=1, grid=(1,),
            in_specs=[piece(a) for a in partials] + [pl.BlockSpec(r.shape, lambda i, place: (0, 0, 0)) for r in recvs],
            out_specs=[piece(a) for a in partials]),
        out_shape=[jax.ShapeDtypeStruct(a.shape, F32) for a in partials],
        compiler_params=_params(),
    )(place_arr, *partials, *recvs)


def _share(name, bufs, parts, gathered=(), collective_id=None):
    n, n_g = len(bufs), len(gathered)
    total = n + n_g

    def body(*refs):
        ins, outs = refs[:total], refs[total:2 * total]
        send_sems, recv_sems, send_g, recv_g = refs[2 * total:]
        x, y, c = _place()
        _handshake([_peer(x, y, c, m) for m in (SAME_CORE if gathered else ()) + (1,)])

        def half(ref, l, which):
            p = ref.shape[1] // 2
            return ref.at[l, pl.ds(pl.multiple_of(which * p, 8), p), :]

        def swap(k, which):
            a, l = parts[k]
            return pltpu.make_async_remote_copy(
                src_ref=half(ins[a], l, which), dst_ref=half(outs[a], l, which), send_sem=send_sems.at[k],
                recv_sem=recv_sems.at[k], device_id=(x, y, 1 - c), device_id_type=MESH)

        def spread(a, m, sender, held, to):
            k = 4 * sender[0] + 2 * sender[1] + sender[2]
            return pltpu.make_async_remote_copy(
                src_ref=_piece_rows(held[n + a], k), dst_ref=_piece_rows(outs[n + a], k),
                send_sem=send_g.at[7 * a + m - 1], recv_sem=recv_g.at[7 * a + m - 1], device_id=to, device_id_type=MESH)

        me, sibling = (x, y, c), (x, y, 1 - c)
        for k in range(len(parts)):
            swap(k, c).start()
        def own(a, m):
            return spread(a, m, me, ins, _peer(x, y, c, m))

        def handed_on(a, m):
            return spread(a, m + 1, _peer(x, y, c, m), outs, sibling)

        for a in range(n_g):
            for m in SAME_CORE + (1,):
                own(a, m).start()
        for a in range(n_g):
            for m in SAME_CORE:
                spread(a, m, _peer(x, y, c, m), ins, _peer(x, y, c, m)).wait_recv()
                handed_on(a, m).start()
        for k in range(len(parts)):
            swap(k, c).wait_send()
            swap(k, 1 - c).wait_recv()
        for a in range(n_g):
            for m in SAME_CORE + (1,):
                own(a, m).wait_send()
            for m in SAME_CORE:
                handed_on(a, m).wait_send()
                spread(a, m + 1, _peer(x, y, c, m + 1), ins, sibling).wait_recv()
            spread(a, 1, sibling, ins, sibling).wait_recv()

    arrays = list(bufs) + list(gathered)
    return pl.pallas_call(
        body, name=name, in_specs=[ANY] * total, out_specs=[ANY] * total,
        out_shape=[jax.ShapeDtypeStruct(b.shape, F32) for b in arrays],
        input_output_aliases={a: a for a in range(total)},
        scratch_shapes=[pltpu.SemaphoreType.DMA((max(len(parts), 1),))] * 2
        + [pltpu.SemaphoreType.DMA((max(7 * n_g, 1),))] * 2,
        compiler_params=pltpu.CompilerParams(collective_id=collective_id),
    )(*arrays)


def _adamw_math(w, g, m, v):
    nm = ADAM_B1 * m + (1.0 - ADAM_B1) * g
    nv = ADAM_B2 * v + (1.0 - ADAM_B2) * (g * g)
    m_hat = nm / (1.0 - ADAM_B1 ** ADAM_STEP)
    v_hat = nv / (1.0 - ADAM_B2 ** ADAM_STEP)
    return -ADAM_LR * (m_hat / (jnp.sqrt(v_hat) + ADAM_EPS) + ADAM_WD * w), nm, nv


def _adamw(name, w, g, m, v, rows_per_step, first=0, count=None, dests=None, deps=()):
    layers, rows, cols = w.shape
    count = layers if count is None else count

    def body(w_ref, g_ref, m_ref, v_ref, *rest):
        d_ref, nm_ref, nv_ref, g_out_ref = rest[-4:]
        d_ref[...], nm_ref[...], nv_ref[...] = _adamw_math(w_ref[...], g_ref[...], m_ref[...], v_ref[...])
        g_out_ref[...] = g_ref[...]

    spec = pl.BlockSpec((1, rows_per_step, cols), lambda l, i: (first + l, i, 0))
    shape = jax.ShapeDtypeStruct(w.shape, F32)
    dests = () if dests is None else tuple(dests)
    return pl.pallas_call(
        body, name=name, grid=(count, rows // rows_per_step),
        in_specs=[spec] * 4 + [ANY] * (len(dests) + len(deps)), out_specs=[spec] * 4, out_shape=[shape] * 4,
        input_output_aliases={4 + k: k for k in range(len(dests))},
        compiler_params=_params(("arbitrary", "arbitrary")),
    )(w, g, m, v, *dests, *deps)


def _pack_misc(pool_scale, sinks, norm_pre, norm_post):
    sink_rows = jnp.zeros((DEPTH, 8, 128), F32).at[:, 0, 0:N_HEADS].set(sinks).reshape(2 * 8, 128)
    return jnp.concatenate([pool_scale.reshape(8, 128), norm_pre.reshape(16, 128), norm_post.reshape(16, 128),
                            sink_rows, jnp.zeros((8, 128), F32)], axis=0)


def _adamw_pool(w, g, m, v, deps=()):
    def body(w_ref, g_ref, m_ref, v_ref, *rest):
        d_ref, nm_ref, nv_ref = rest[len(deps):]
        d_ref[...], nm_ref[...], nv_ref[...] = _adamw_math(w_ref[...], g_ref[...], m_ref[...], v_ref[...])

    vmem = pl.BlockSpec(memory_space=pltpu.VMEM)
    return pl.pallas_call(
        body, name="adamw_pool_w", in_specs=[vmem] * 4 + [ANY] * len(deps), out_specs=[vmem] * 3,
        out_shape=[jax.ShapeDtypeStruct(w.shape, F32)] * 3,
    )(w, g, m, v, *deps)


def _adamw_small(w, g, m, v):
    def body(w_ref, g_ref, m_ref, v_ref, *rest):
        outs, (d_ref, nm_ref, nv_ref) = rest[:17], rest[17:]
        d_ref[...], nm_ref[...], nv_ref[...] = _adamw_math(w_ref[...], g_ref[...], m_ref[...], v_ref[...])
        for k, src in enumerate([g_ref, d_ref, nm_ref, nv_ref]):
            scale, sinks, pre, post = outs[4 * k:4 * k + 4]
            for l in range(DEPTH):
                for j in range(4):
                    scale[l:l + 1, j * 128:(j + 1) * 128] = src[MISC_SCALE + 4 * l + j:MISC_SCALE + 4 * l + j + 1, :]
                for j in range(8):
                    pre[l:l + 1, j * 128:(j + 1) * 128] = src[MISC_PRE + 8 * l + j:MISC_PRE + 8 * l + j + 1, :]
                    post[l:l + 1, j * 128:(j + 1) * 128] = src[MISC_POST + 8 * l + j:MISC_POST + 8 * l + j + 1, :]
                sinks[l:l + 1, :] = src[MISC_SINKS + 8 * l:MISC_SINKS + 8 * l + 1, 0:N_HEADS]
        outs[16][...] = g_ref[MISC_LOSS:MISC_LOSS + 1, 0:1]

    vmem = pl.BlockSpec(memory_space=pltpu.VMEM)
    shapes = [(DEPTH, D_POOL), (DEPTH, N_HEADS), (DEPTH, D), (DEPTH, D)] * 4 + [(1, 1)]
    return pl.pallas_call(
        body, name="adamw_small", in_specs=[vmem] * 4, out_specs=[vmem] * 17,
        out_shape=[jax.ShapeDtypeStruct(s, F32) for s in shapes],
        scratch_shapes=[pltpu.VMEM((MISC_ROWS, 128), F32)] * 3,
    )(w, g, m, v)


def kernel(x, w_in, pool_w, pool_scale, attn_sinks, w_out, norm_pre, norm_post, loss_target, m_w_in, m_pool_w, m_pool_scale, m_attn_sinks, m_w_out, m_norm_pre, m_norm_post, v_w_in, v_pool_w, v_pool_scale, v_attn_sinks, v_w_out, v_norm_pre, v_norm_post):
    cx, cy, cc = _place()
    chip_arr = jnp.reshape(2 * cx + cy, (1,)).astype(jnp.int32)
    place_arr = jnp.stack([4 * cx + 2 * cy + cc, cc]).astype(jnp.int32)
    t = lambda a: jnp.transpose(a, (0, 2, 1))
    w_in_t = t(w_in)
    xs, target = x[0], loss_target[0]
    pool_w_b = pool_w.astype(BF16)
    tables = _attention_tables()
    scale3 = pool_scale.reshape(DEPTH, 1, D_POOL)
    pre3 = norm_pre.reshape(DEPTH, 1, D)
    post3 = norm_post.reshape(DEPTH, 1, D)

    (wi0,) = _place_cast("place_w_in0", w_in_t, chip_arr, 288, [0])
    first = _gather_start("gather_start_first", [wi0], halved=(0,), collective_id=ID_GATHER_FIRST)
    (wi1,) = _place_cast("place_w_in1", w_in_t, chip_arr, 288, [1], deps=(first[3],))
    wo = _place_cast("place_w_out", w_out, chip_arr, 256, [0, 1], deps=(first[3],))
    rest = _gather_start("gather_start_rest", [wi1, wo[0], wo[1]], halved=(0,), collective_id=ID_GATHER_REST)
    send, recv, bufs = [first[k] + rest[k] for k in range(3)]
    order = {(0, "in"): 0, (1, "in"): 1, (0, "out"): 2, (1, "out"): 3}

    saved = []
    packed = [_pack_misc(pool_scale, attn_sinks, norm_pre, norm_post),
              _pack_misc(m_pool_scale, m_attn_sinks, m_norm_pre, m_norm_post),
              _pack_misc(v_pool_scale, v_attn_sinks, v_norm_pre, v_norm_post)]
    after = (first[3], rest[3], pool_w_b, *tables, scale3, pre3, post3, *packed)
    below = None
    for l in range(DEPTH):
        k = order[l, "in"]
        w_in_l = _forward_halves(f"forward_w_in{l}", _gather_wait(f"gather_wait_in{l}", bufs[k], send[k], recv[k], after,
                                                                 halved=True), collective_id=ID_FORWARD[l])
        if below is None:
            pu, pg, q, kv, ag = _fwd_in(l, xs, pre3, w_in_l)
        else:
            y, xs, pu, pg, q, kv, ag = _fwd_in(l, xs, pre3, w_in_l, below)
            saved[l - 1][7] = y
        cat = _fwd_mix(l, pu, pg, q, kv, ag, pool_w_b, scale3, attn_sinks, tables)
        k = order[l, "out"]
        w_out_l = _gather_wait(f"gather_wait_out{l}", bufs[k], send[k], recv[k], (cat,))
        saved.append([xs, pu, pg, q, kv, ag, cat, None, w_in_l, w_out_l])
        below, after = (cat, w_out_l, post3), (w_out_l,)

    x_in, pu, pg, q, kv, ag, cat, y, w_in_l, w_out_l = saved[1]
    dcat, dw_out1, dw_out1_b, dg_post1, loss, xs = _bwd_out(1, cat, w_out_l, post3, place_arr, x=x_in, target=target)
    ex1_out = _exchange_start("exchange_start_out1", [dw_out1_b], ID_OUT1)
    dproj, dpw, dsc1, dsink1 = _bwd_mix(1, pu, pg, q, kv, ag, dcat, pool_w_b, scale3, attn_sinks, tables,
                                        deps=(ex1_out[4],))
    dx, dg_pre1, dw_in1, dw_in1_b = _bwd_in_dx(1, dproj, w_in_l, x_in, pre3, xs, dw_place=place_arr)
    ex1_in = _exchange_start("exchange_start_in1", [dw_in1_b], ID_IN1)

    x_in, pu, pg, q, kv, ag, cat, y, w_in_l, w_out_l = saved[0]
    dcat, dw_out0, dw_out0_b, dg_post0 = _bwd_out(0, cat, w_out_l, post3, place_arr, dxn=dx, y=y, deps=(ex1_in[4],))
    ex0_out = _exchange_start("exchange_start_out0", [dw_out0_b], ID_OUT0)
    dproj, dpw, dsc0, dsink0 = _bwd_mix(0, pu, pg, q, kv, ag, dcat, pool_w_b, scale3, attn_sinks, tables,
                                        deps=(ex0_out[4],), dpw_dest=dpw)
    dw_in0, dw_in0_b = _bwd_in_dw(0, dproj, x_in, pre3, place_arr)
    flat = lambda a: a.reshape(DEPTH * 4 * 128, 128)
    ex0_in = _exchange_start("exchange_start_in0", [flat(dpw), dw_in0_b], ID_IN0)

    grad_x, dg_pre0 = _bwd_in_dx(0, dproj, w_in_l, x_in, pre3, dx, deps=(ex0_in[4],))
    small = [jnp.concatenate([dsc0, dsc1, dg_pre0, dg_pre1, dg_post0, dg_post1, dsink0, dsink1, loss], axis=0)]
    ex_small = _exchange_start("exchange_start_small", small, ID_SMALL)
    (recv_out1,) = _exchange_wait("exchange_wait_out1", ex1_out, ex_small[4])
    (recv_in1,) = _exchange_wait("exchange_wait_in1", ex1_in, recv_out1)
    g_in, g_out = _sum_pieces("sum_pieces_1", [[(1, dw_in1, recv_in1)], [(1, dw_out1, recv_out1)]], place_arr)
    (recv_out0,) = _exchange_wait("exchange_wait_out0", ex0_out, g_out)
    (g_out,) = _sum_pieces("sum_pieces_out0", [[(0, dw_out0, recv_out0)]], place_arr, dests=[g_out])
    dpw_own, recv_pw = _exchange_wait("exchange_wait_pool", ex0_in, g_out, which=[0], with_sent=True)
    g_in, g_out, g_pw = _share("share_a", [g_in, g_out], [(0, 1), (1, 0), (1, 1)],
                               _sum_small("sum_pool", [dpw_own], [recv_pw], place_arr), collective_id=ID_SHARE_A)
    m_in_t, v_in_t = t(m_w_in), t(v_w_in)
    d_out, nm_out, nv_out, grad_w_out = _adamw("adamw_w_out", w_out, g_out, m_w_out, v_w_out, 256)
    upd_in = _adamw("adamw_w_in1", w_in_t, g_in, m_in_t, v_in_t, 288, first=1, count=1, deps=(d_out,))
    d_pw, m_pw, v_pw = [a.reshape(pool_w.shape) for a in
                        _adamw_pool(flat(pool_w), g_pw, flat(m_pool_w), flat(v_pool_w), deps=(upd_in[0],))]

    (recv_in0,) = _exchange_wait("exchange_wait_in0", ex0_in, d_pw, which=[1])
    misc_own, recv_misc = _exchange_wait("exchange_wait_small", ex_small, recv_in0, with_sent=True)
    (g_in,) = _sum_pieces("sum_pieces_in0", [[(0, dw_in0, recv_in0)]], place_arr, dests=[g_in])
    g_in, g_misc = _share("share_b", [g_in], [(0, 0)], _sum_small("sum_small", [misc_own], [recv_misc], place_arr),
                          collective_id=ID_SHARE_B)
    d_in, nm_in, nv_in, grad_w_in_t = _adamw("adamw_w_in0", w_in_t, g_in, m_in_t, v_in_t, 288, first=0, count=1,
                                             dests=upd_in)
    (g_sc, g_sk, g_pre, g_post, d_sc, d_sk, d_pre, d_post, m_sc, m_sk, m_pre, m_post,
     v_sc, v_sk, v_pre, v_post, loss_sum) = _adamw_small(packed[0], g_misc, packed[1], packed[2])
    g_pw = g_pw.reshape(pool_w.shape)
    return (loss_sum[0, 0], grad_x[None], t(grad_w_in_t), g_pw, g_sc, g_sk, grad_w_out, g_pre, g_post,
            t(d_in), d_pw, d_sc, d_sk, d_out, d_pre, d_post,
            t(nm_in), m_pw, m_sc, m_sk, nm_out, m_pre, m_post,
            t(nv_in), v_pw, v_sc, v_sk, nv_out, v_pre, v_post)
```

```python
import jax
import jax.numpy as jnp
from jax import lax
from jax.experimental import pallas as pl
from jax.experimental.pallas import tpu as pltpu

F32 = jnp.float32
BF16 = jnp.bfloat16

S = 2048
D = 1024
DEPTH = 2
D_POOL = 512
POOL_WINDOWS = (2, 4, 8, 16)
N_HEADS = 8
D_IN = 2304
N_SHARDS = 4
W_IN_SHARD = D_IN // N_SHARDS
W_OUT_SHARD = D // N_SHARDS
BLK = 128
NB = S // BLK
HALO = 16
PAD = 8
EPS = 1e-6
NEG_INF = -1e30
C_PU, C_PG, C_Q, C_K, C_V, C_AG = 0, 512, 1024, 1536, 1664, 1792

ADAM_LR = 0.001
ADAM_B1 = 0.9
ADAM_B2 = 0.999
ADAM_EPS = 1e-08
ADAM_WD = 0.01
ADAM_STEP = 10

TM = 512
VMEM_LIMIT = 56 * 1024 * 1024

NT = (((1,), (1,)), ((), ()))
TN = (((0,), (0,)), ((), ()))

MESH = pl.DeviceIdType.MESH
ANY = pl.BlockSpec(memory_space=pl.ANY)

ID_FORWARD = (0, 1)
(ID_SHARE_A, ID_SHARE_B, ID_GATHER_FIRST, ID_GATHER_REST, ID_OUT1, ID_IN1, ID_OUT0, ID_IN0, ID_SMALL) = range(2, 11)

MISC_SCALE, MISC_PRE, MISC_POST, MISC_SINKS, MISC_LOSS = 0, 8, 24, 40, 56
MISC_ROWS = 64


def _params(sem=("arbitrary",)):
    return pltpu.CompilerParams(dimension_semantics=sem, vmem_limit_bytes=VMEM_LIMIT)


def _sigmoid(v):
    return 1.0 / (1.0 + jnp.exp(-v))


def _rows8(v):
    r, c = v.shape
    return v.reshape(r // 8, 8, c).sum(axis=0)


def _layer(l, *shape):
    zeros = (0,) * len(shape)
    return pl.BlockSpec((None,) + shape, lambda i: (l,) + zeros)


def _whole(shape):
    zeros = (0,) * len(shape)
    return pl.BlockSpec(shape, lambda i: zeros, pipeline_mode=pl.Buffered(1))


def _fwd_in(l, x, g_pre, w_in_t, below=None):
    fused = below is not None

    def body(x_ref, g_ref, w_ref, *rest):
        if fused:
            cat_ref, wo_ref, gp_ref, y_ref, xn_ref = rest[:5]
            y = jnp.dot(cat_ref[...], wo_ref[...], preferred_element_type=F32)
            y_ref[...] = y
            xt = x_ref[...] + y * lax.rsqrt(jnp.mean(y * y, axis=-1, keepdims=True) + EPS) * gp_ref[...]
            xn_ref[...] = xt
        else:
            xt = x_ref[...]
        pu_ref, pg_ref, q_ref, kv_ref, ag_ref = rest[-5:]
        r = lax.rsqrt(jnp.mean(xt * xt, axis=-1, keepdims=True) + EPS)
        h = (xt * r * g_ref[...]).astype(BF16)

        def proj(lo, hi):
            return lax.dot_general(h, w_ref[lo:hi, :], NT, preferred_element_type=F32)

        pu_ref[...] = proj(C_PU, C_PG)
        pg_ref[...] = proj(C_PG, C_Q)
        q_ref[...] = proj(C_Q, C_K).astype(BF16)
        kv_ref[...] = proj(C_K, C_AG).astype(BF16)
        ag_ref[...] = proj(C_AG, D_IN)

    row = lambda w: pl.BlockSpec((TM, w), lambda i: (i, 0))
    act = jax.ShapeDtypeStruct((S, D), F32)
    return pl.pallas_call(
        body, name="fwd_out_in" if fused else "fwd_in", grid=(S // TM,),
        in_specs=[row(D), _layer(l, 1, D), _whole((D_IN, D))]
        + ([row(D), _whole((D, D)), _layer(l - 1, 1, D)] if fused else []),
        out_specs=[row(D)] * (2 * fused) + [row(512), row(512), row(512), row(256), row(512)],
        out_shape=[act] * (2 * fused)
        + [jax.ShapeDtypeStruct((S, 512), F32), jax.ShapeDtypeStruct((S, 512), F32),
           jax.ShapeDtypeStruct((S, 512), BF16), jax.ShapeDtypeStruct((S, 256), BF16),
           jax.ShapeDtypeStruct((S, 512), F32)],
        compiler_params=_params(),
    )(x, g_pre, w_in_t, *(below if fused else ()))


LOG2E = 1.4426950408889634
SCORE_SCALE = 0.125 * LOG2E


def _attention_tables():
    qi = jnp.arange(BLK)[:, None]
    kj = jnp.arange(BLK)[None, :]
    dist = ((qi - kj) % BLK).astype(F32)
    slopes = jnp.exp2(-jnp.arange(1, N_HEADS + 1, dtype=F32))
    bias = -(slopes * LOG2E)[:, None, None] * dist[None]
    first = jnp.where(kj > qi, NEG_INF, bias)
    return jnp.stack([first, bias]), (kj <= qi).astype(BF16)


def _own_block_mask():
    return lax.broadcasted_iota(jnp.int32, (BLK, BLK), 1) <= lax.broadcasted_iota(jnp.int32, (BLK, BLK), 0)


def _merge(full, own):
    return jnp.where(own, full[:, BLK:], full[:, :BLK])


def _spread(v, tri):
    own = v * tri
    return jnp.concatenate([v - own, own], axis=1)


def _head_variants(cur, prev):
    both = jnp.concatenate([prev, cur], axis=0).astype(F32)
    swapped = pltpu.roll(both, 64, axis=1)
    low = lax.broadcasted_iota(jnp.int32, both.shape, 1) < 64
    zero = jnp.zeros_like(both)
    return ((jnp.where(low, both, zero).astype(BF16), jnp.where(low, zero, swapped).astype(BF16)),
            (jnp.where(low, swapped, zero).astype(BF16), jnp.where(low, zero, both).astype(BF16)))


def _head_of(hkv, t, half):
    return hkv * 4 + 2 * t + half


def _rows(v, t):
    return v[t * BLK:(t + 1) * BLK]


def _stack_tiles(ref, hkv, offset=0):
    lo = offset + 2 * hkv * 128
    return jnp.concatenate([ref[:, lo:lo + 128], ref[:, lo + 128:lo + 256]], axis=0)


def _scores(q2, k_var, own):
    s = {}
    for hkv in range(2):
        for half in range(2):
            full = lax.dot_general(q2[hkv], k_var[hkv][half], NT, preferred_element_type=F32)
            for t in range(2):
                s[hkv, t, half] = _merge(_rows(full, t), own)
    return s


def _softmax(s, bias, sink):
    s = s * SCORE_SCALE + bias
    sink2 = sink * LOG2E
    m = jnp.maximum(jnp.max(s, axis=-1, keepdims=True), sink2)
    p = jnp.exp2(s - m)
    e_sink = jnp.exp2(sink2 - m)
    inv = 1.0 / (jnp.sum(p, axis=-1, keepdims=True) + e_sink)
    return p * inv, e_sink * inv


def _spread_pair(v, hkv, half, tri):
    return jnp.concatenate([_spread(v[hkv, t, half].astype(BF16), tri) for t in range(2)], axis=0)


POOL_ROWS = PAD + HALO + BLK


def _window_sums(src_ref, tmp_refs, trailing):
    lo, hi = (PAD, POOL_ROWS) if trailing else (0, HALO + BLK)
    cur = src_ref
    for level in range(len(POOL_WINDOWS)):
        lanes = slice(level * 128, 512)
        shift = -(1 << level) if trailing else (1 << level)
        dst = tmp_refs[level % 2]
        dst[lo:hi, lanes] = cur[lo:hi, lanes] + cur[lo + shift:hi + shift, lanes]
        cur = dst


def _pool_block(ext_ref, tmp_refs, i, g, w):
    lanes = slice(g * 128, (g + 1) * 128)
    rows = slice(PAD + HALO, POOL_ROWS)
    t = (i * BLK + lax.broadcasted_iota(jnp.int32, (BLK, 1), 0)).astype(F32)
    inv = 1.0 / jnp.minimum(t + 1.0, float(w))
    return tmp_refs[g % 2][rows, lanes] * inv - ext_ref[rows, lanes], inv


def _fwd_mix(l, pu, pg, q, kv, ag, pool_w, pool_scale, sinks, tables):
    bias, tri = tables

    def body(pu_ref, pup_ref, pg_ref, q_ref, kv_ref, kvp_ref, ag_ref, pw_ref, sc_ref, sink_ref, bias_ref, tri_ref,
             cat_ref, ext_ref, *tmp_refs):
        i = pl.program_id(0)

        @pl.when(i == 0)
        def _():
            for ref in (ext_ref, *tmp_refs):
                ref[0:PAD, :] = jnp.zeros((PAD, 512), F32)

        ext_ref[PAD:PAD + HALO, :] = jnp.where(i > 0, pup_ref[...], 0.0)
        ext_ref[PAD + HALO:POOL_ROWS, :] = pu_ref[...]
        _window_sums(ext_ref, tmp_refs, True)
        for g, w in enumerate(POOL_WINDOWS):
            lanes = slice(g * 128, (g + 1) * 128)
            pooled, _ = _pool_block(ext_ref, tmp_refs, i, g, w)
            mixed = jnp.dot(pooled.astype(BF16), pw_ref[g], preferred_element_type=F32)
            gate = pg_ref[:, lanes]
            cat_ref[:, lanes] = (mixed * sc_ref[:, lanes] * (gate * _sigmoid(gate))).astype(BF16)

        own = _own_block_mask()
        tri = tri_ref[...]
        k_var = _head_variants(kv_ref[:, 0:128], kvp_ref[:, 0:128])
        v_var = _head_variants(kv_ref[:, 128:256], kvp_ref[:, 128:256])
        s = _scores([_stack_tiles(q_ref, hkv) for hkv in range(2)], k_var, own)
        p = {}
        for (hkv, t, half), s_head in s.items():
            head = _head_of(hkv, t, half)
            p[hkv, t, half], _ = _softmax(s_head, bias_ref[head], sink_ref[l, head])
        for hkv in range(2):
            o2 = jnp.zeros((2 * BLK, 128), F32)
            for half in range(2):
                o2 = o2 + jnp.dot(_spread_pair(p, hkv, half, tri), v_var[hkv][half], preferred_element_type=F32)
            for t in range(2):
                lo = (2 * hkv + t) * 128
                gate = ag_ref[:, lo:lo + 128]
                cat_ref[:, D_POOL + lo:D_POOL + lo + 128] = (_rows(o2, t) * (gate * _sigmoid(gate))).astype(BF16)

    blk = lambda w: pl.BlockSpec((BLK, w), lambda i: (i, 0))
    prev = lambda w: pl.BlockSpec((BLK, w), lambda i: (jnp.maximum(i - 1, 0), 0))
    halo = pl.BlockSpec((HALO, 512), lambda i: (jnp.maximum(i * (BLK // HALO) - 1, 0), 0))
    return pl.pallas_call(
        body, name="fwd_mix", grid=(NB,),
        in_specs=[blk(512), halo, blk(512), blk(512), blk(256), prev(256), blk(512),
                  _layer(l, 4, 128, 128), _layer(l, 1, 512), pl.BlockSpec(memory_space=pltpu.SMEM),
                  pl.BlockSpec((None, N_HEADS, BLK, BLK), lambda i: (jnp.minimum(i, 1), 0, 0, 0)), _whole((BLK, BLK))],
        out_specs=blk(D),
        out_shape=jax.ShapeDtypeStruct((S, D), BF16),
        scratch_shapes=[pltpu.VMEM((POOL_ROWS, 512), F32)] * 3,
        compiler_params=_params(),
    )(pu, pu, pg, q, kv, kv, ag, pool_w, pool_scale, sinks, bias, tri)


def _store_lane_rows(ref, acc):
    total = jnp.sum(acc, axis=0, keepdims=True)
    for k in range(ref.shape[0]):
        ref[k:k + 1, :] = total[:, k * 128:(k + 1) * 128]


def _own_piece(dw_ref, place_ref):
    p = dw_ref.shape[0] // 8
    return dw_ref[pl.ds(pl.multiple_of(place_ref[0] * p, 8), p), :]


def _bwd_out(l, cat, w_out, g_post, place_arr, dxn=None, y=None, x=None, target=None, deps=()):
    last = target is not None
    n_steps = S // TM

    def body(a_ref, b_ref, g_ref, cat_ref, w_ref, place_ref, *rest):
        dcat_ref, own_ref, dwb_ref, dg_ref = rest[len(deps):len(deps) + 4]
        rest = rest[len(deps) + 4:]
        acc_ref, dw_ref = rest[-2:]
        step = pl.program_id(0)

        @pl.when(step == 0)
        def _():
            dw_ref[...] = jnp.zeros_like(dw_ref)
            acc_ref[...] = jnp.zeros_like(acc_ref)

        cat = cat_ref[...]
        g = g_ref[...]
        y = jnp.dot(cat, w_ref[...], preferred_element_type=F32) if last else b_ref[...]
        r = lax.rsqrt(jnp.mean(y * y, axis=-1, keepdims=True) + EPS)
        if last:
            loss_ref, dx_ref, loss_acc_ref = rest[:3]
            err = a_ref[...] + y * r * g - b_ref[...]

            @pl.when(step == 0)
            def _():
                loss_acc_ref[...] = jnp.zeros_like(loss_acc_ref)

            loss_acc_ref[...] += _rows8(err * err)
            dz = err * (1.0 / D)
            dx_ref[...] = dz
        else:
            dz = a_ref[...]
        a = dz * g
        dy = r * a - y * (r * r * r) * jnp.mean(a * y, axis=-1, keepdims=True)
        acc_ref[...] += _rows8(dz * (y * r))
        dyb = dy.astype(BF16)
        dcat_ref[...] = lax.dot_general(dyb, w_ref[...], NT, preferred_element_type=F32)
        dw_ref[...] += lax.dot_general(cat, dyb, TN, preferred_element_type=F32)

        @pl.when(step == n_steps - 1)
        def _():
            _store_lane_rows(dg_ref, acc_ref[...])
            dwb_ref[...] = dw_ref[...].astype(BF16)
            own_ref[...] = _own_piece(dw_ref, place_ref)
            if last:
                loss_ref[...] = jnp.full((8, 128), (0.5 / D) * jnp.sum(loss_acc_ref[...]), F32)

    row = lambda: pl.BlockSpec((TM, D), lambda i: (i, 0))
    full = _whole
    return pl.pallas_call(
        body, name="out_loss_bwd" if last else "bwd_out", grid=(n_steps,),
        in_specs=[row(), row(), _layer(l, 1, D), row(), full((D, D)), pl.BlockSpec(memory_space=pltpu.SMEM)]
        + [ANY] * len(deps),
        out_specs=[row(), full((D // 8, D)), full((D, D)), full((8, 128))] + ([full((8, 128)), row()] if last else []),
        out_shape=[jax.ShapeDtypeStruct((S, D), F32), jax.ShapeDtypeStruct((D // 8, D), F32),
                   jax.ShapeDtypeStruct((D, D), BF16), jax.ShapeDtypeStruct((8, 128), F32)]
        + ([jax.ShapeDtypeStruct((8, 128), F32), jax.ShapeDtypeStruct((S, D), F32)] if last else []),
        scratch_shapes=([pltpu.VMEM((8, D), F32)] if last else []) + [pltpu.VMEM((8, D), F32), pltpu.VMEM((D, D), F32)],
        compiler_params=_params(),
    )(*((x, target) if last else (dxn, y)), g_post, cat, w_out, place_arr, *deps)


def _bwd_mix(l, pu, pg, q, kv, ag, dcat, pool_w, pool_scale, sinks, tables, deps=(), dpw_dest=None):
    bias, tri = tables
    deps = tuple(deps) + (() if dpw_dest is None else (dpw_dest,))

    def body(pu_ref, pup_ref, pg_ref, q_ref, kv_ref, kvp_ref, ag_ref, dcat_ref, pw_ref, sc_ref, sink_ref, bias_ref,
             tri_ref, *rest):
        dproj_ref, dpw_ref, dsc_ref, dsink_ref, ext_ref, dext_ref, tmp_a, tmp_b, dkv_ref = rest[len(deps):]
        tmp_refs = (tmp_a, tmp_b)
        step = pl.program_id(0)
        i = NB - 1 - step

        @pl.when(step == 0)
        def _():
            dpw_ref[...] = jnp.zeros_like(dpw_ref)
            dsc_ref[...] = jnp.zeros_like(dsc_ref)
            dsink_ref[...] = jnp.zeros_like(dsink_ref)
            for ref in (ext_ref, tmp_a, tmp_b):
                ref[0:PAD, :] = jnp.zeros((PAD, 512), F32)
            dext_ref[BLK:POOL_ROWS, :] = jnp.zeros((HALO + PAD, 512), F32)
            dkv_ref[...] = jnp.zeros_like(dkv_ref)

        ext_ref[PAD:PAD + HALO, :] = jnp.where(i > 0, pup_ref[...], 0.0)
        ext_ref[PAD + HALO:POOL_ROWS, :] = pu_ref[...]
        _window_sums(ext_ref, tmp_refs, True)
        dpooled = []
        for g, w in enumerate(POOL_WINDOWS):
            lanes = slice(g * 128, (g + 1) * 128)
            pooled, inv = _pool_block(ext_ref, tmp_refs, i, g, w)
            pooled_b = pooled.astype(BF16)
            mixed = jnp.dot(pooled_b, pw_ref[g], preferred_element_type=F32)
            scale = sc_ref[:, lanes]
            gate = pg_ref[:, lanes]
            sg = _sigmoid(gate)
            dpo = dcat_ref[:, lanes]
            dproj_ref[:, C_PG + g * 128:C_PG + (g + 1) * 128] = (
                dpo * (mixed * scale) * (sg * (1.0 + gate * (1.0 - sg)))).astype(BF16)
            dms = dpo * (gate * sg)
            dsc_ref[g:g + 1, :] += jnp.sum(dms * mixed, axis=0, keepdims=True)
            dmixed = (dms * scale).astype(BF16)
            dpw_ref[g] += lax.dot_general(pooled_b, dmixed, TN, preferred_element_type=F32)
            dpooled.append(lax.dot_general(dmixed, pw_ref[g], NT, preferred_element_type=F32))
            dext_ref[0:BLK, lanes] = dpooled[g] * inv
        _window_sums(dext_ref, tmp_refs, False)
        for g in range(len(POOL_WINDOWS)):
            lanes = slice(g * 128, (g + 1) * 128)
            dproj_ref[:, C_PU + g * 128:C_PU + (g + 1) * 128] = (tmp_refs[g % 2][0:BLK, lanes] - dpooled[g]).astype(BF16)
        dext_ref[BLK:BLK + HALO, :] = dext_ref[0:HALO, :]

        own = _own_block_mask()
        tri = tri_ref[...]
        k_var = _head_variants(kv_ref[:, 0:128], kvp_ref[:, 0:128])
        v_var = _head_variants(kv_ref[:, 128:256], kvp_ref[:, 128:256])
        q2 = [_stack_tiles(q_ref, hkv) for hkv in range(2)]
        s = _scores(q2, k_var, own)
        p, p_sink = {}, {}
        for key, s_head in s.items():
            head = _head_of(*key)
            p[key], p_sink[key] = _softmax(s_head, bias_ref[head], sink_ref[l, head])

        do2, p_b, dp = [], {}, {}
        for hkv in range(2):
            gate = _stack_tiles(ag_ref, hkv)
            sg = _sigmoid(gate)
            dca = _stack_tiles(dcat_ref, hkv, D_POOL)
            do2.append((dca * (gate * sg)).astype(BF16))
            o2 = jnp.zeros((2 * BLK, 128), F32)
            for half in range(2):
                p_b[hkv, half] = _spread_pair(p, hkv, half, tri)
                o2 = o2 + jnp.dot(p_b[hkv, half], v_var[hkv][half], preferred_element_type=F32)
                full = lax.dot_general(do2[hkv], v_var[hkv][half], NT, preferred_element_type=F32)
                for t in range(2):
                    dp[hkv, t, half] = _merge(_rows(full, t), own)
            dag = dca * o2 * (sg * (1.0 + gate * (1.0 - sg)))
            for t in range(2):
                lo = C_AG + (2 * hkv + t) * 128
                dproj_ref[:, lo:lo + 128] = _rows(dag, t).astype(BF16)

        ds = {}
        for key in p:
            delta = jnp.sum(p[key] * dp[key], axis=-1, keepdims=True)
            ds[key] = p[key] * (dp[key] - delta)
            head = _head_of(*key)
            dsink_ref[0:1, :] += jnp.where(lax.broadcasted_iota(jnp.int32, (1, 128), 1) == head,
                                           -jnp.sum(p_sink[key] * delta, axis=0, keepdims=True), 0.0)

        dk_acc = [[None, None], [None, None]]
        dv_acc = [[None, None], [None, None]]
        for hkv in range(2):
            dq2 = jnp.zeros((2 * BLK, 128), F32)
            for half in range(2):
                ds_b = _spread_pair(ds, hkv, half, tri)
                dq2 = dq2 + jnp.dot(ds_b, k_var[hkv][half], preferred_element_type=F32)
                dk_acc[hkv][half] = lax.dot_general(ds_b, q2[hkv], TN, preferred_element_type=F32)
                dv_acc[hkv][half] = lax.dot_general(p_b[hkv, half], do2[hkv], TN, preferred_element_type=F32)
            for t in range(2):
                lo = C_Q + (2 * hkv + t) * 128
                dproj_ref[:, lo:lo + 128] = (_rows(dq2, t) * 0.125).astype(BF16)

        low = lax.broadcasted_iota(jnp.int32, (2 * BLK, 128), 1) < 64

        def gather_heads(acc):
            return jnp.where(low, acc[0][0] + pltpu.roll(acc[0][1], 64, axis=1),
                             pltpu.roll(acc[1][0], 64, axis=1) + acc[1][1])

        dk = gather_heads(dk_acc) * 0.125
        dv = gather_heads(dv_acc)
        dproj_ref[:, C_K:C_V] = (dk[BLK:, :] + dkv_ref[:, 0:128]).astype(BF16)
        dproj_ref[:, C_V:C_AG] = (dv[BLK:, :] + dkv_ref[:, 128:256]).astype(BF16)
        dkv_ref[:, 0:128] = dk[:BLK, :]
        dkv_ref[:, 128:256] = dv[:BLK, :]

    rev = lambda w: pl.BlockSpec((BLK, w), lambda s: (NB - 1 - s, 0))
    prev = lambda w: pl.BlockSpec((BLK, w), lambda s: (jnp.maximum(NB - 2 - s, 0), 0))
    halo = pl.BlockSpec((HALO, 512), lambda s: (jnp.maximum((NB - 1 - s) * (BLK // HALO) - 1, 0), 0))
    return pl.pallas_call(
        body, name="bwd_mix", grid=(NB,),
        in_specs=[rev(512), halo, rev(512), rev(512), rev(256), prev(256), rev(512), rev(D),
                  _layer(l, 4, 128, 128), _layer(l, 1, 512), pl.BlockSpec(memory_space=pltpu.SMEM),
                  pl.BlockSpec((None, N_HEADS, BLK, BLK), lambda s: (jnp.minimum(NB - 1 - s, 1), 0, 0, 0)),
                  _whole((BLK, BLK))] + [ANY] * len(deps),
        out_specs=[rev(D_IN), _layer(l, 4, 128, 128),
                   pl.BlockSpec((4, 128), lambda s: (0, 0)), pl.BlockSpec((8, 128), lambda s: (0, 0))],
        out_shape=[jax.ShapeDtypeStruct((S, D_IN), BF16), jax.ShapeDtypeStruct((DEPTH, 4, 128, 128), F32),
                   jax.ShapeDtypeStruct((4, 128), F32), jax.ShapeDtypeStruct((8, 128), F32)],
        input_output_aliases={} if dpw_dest is None else {12 + len(deps): 1},
        scratch_shapes=[pltpu.VMEM((POOL_ROWS, 512), F32)] * 4 + [pltpu.VMEM((BLK, 256), F32)],
        compiler_params=_params(),
    )(pu, pu, pg, q, kv, kv, ag, dcat, pool_w, pool_scale, sinks, bias, tri, *deps)


def _bwd_in_dw(l, dproj, x, g_pre, place_arr, deps=()):
    n_steps = S // TM

    def body(dp_ref, x_ref, g_ref, place_ref, *rest):
        own_ref, dwb_ref, dw_ref = rest[len(deps):]
        step = pl.program_id(0)

        @pl.when(step == 0)
        def _():
            dw_ref[...] = jnp.zeros_like(dw_ref)

        xt = x_ref[...]
        r = lax.rsqrt(jnp.mean(xt * xt, axis=-1, keepdims=True) + EPS)
        h = (xt * r * g_ref[...]).astype(BF16)
        dw_ref[...] += lax.dot_general(dp_ref[...], h, TN, preferred_element_type=F32)

        @pl.when(step == n_steps - 1)
        def _():
            dwb_ref[...] = dw_ref[...].astype(BF16)
            own_ref[...] = _own_piece(dw_ref, place_ref)

    row = lambda w: pl.BlockSpec((TM, w), lambda i: (i, 0))
    full = _whole
    return pl.pallas_call(
        body, name="bwd_in_dw", grid=(n_steps,),
        in_specs=[row(D_IN), row(D), _layer(l, 1, D), pl.BlockSpec(memory_space=pltpu.SMEM)] + [ANY] * len(deps),
        out_specs=[full((D_IN // 8, D)), full((D_IN, D))],
        out_shape=[jax.ShapeDtypeStruct((D_IN // 8, D), F32), jax.ShapeDtypeStruct((D_IN, D), BF16)],
        scratch_shapes=[pltpu.VMEM((D_IN, D), F32)],
        compiler_params=_params(),
    )(dproj, x, g_pre, place_arr, *deps)


def _bwd_in_dx(l, dproj, w_in_t, x, g_pre, dres, deps=(), dw_place=None):
    n_steps = S // TM
    with_dw = dw_place is not None

    def body(dp_ref, w_ref, x_ref, g_ref, dres_ref, *rest):
        place_ref = rest[0] if with_dw else None
        rest = rest[with_dw + len(deps):]
        if with_dw:
            dx_ref, dg_ref, own_ref, dwb_ref, acc_ref, dw_ref = rest
        else:
            dx_ref, dg_ref, acc_ref = rest
        step = pl.program_id(0)

        @pl.when(step == 0)
        def _():
            acc_ref[...] = jnp.zeros_like(acc_ref)
            if with_dw:
                dw_ref[...] = jnp.zeros_like(dw_ref)

        g = g_ref[...]
        halves = [slice(k * (TM // 2), (k + 1) * (TM // 2)) for k in range(2)]
        dh = [jnp.dot(dp_ref[rows, :], w_ref[...], preferred_element_type=F32) for rows in halves]
        h = []
        for rows, dh_k in zip(halves, dh):
            xt = x_ref[rows, :]
            r = lax.rsqrt(jnp.mean(xt * xt, axis=-1, keepdims=True) + EPS)
            xn = xt * r
            acc_ref[...] += _rows8(dh_k * xn)
            a = dh_k * g
            dx_ref[rows, :] = dres_ref[rows, :] + (
                r * a - xt * (r * r * r) * jnp.mean(a * xt, axis=-1, keepdims=True))
            h.append((xn * g).astype(BF16))
        if with_dw:
            dw_ref[...] += lax.dot_general(dp_ref[...], jnp.concatenate(h, axis=0), TN, preferred_element_type=F32)

        @pl.when(step == n_steps - 1)
        def _():
            _store_lane_rows(dg_ref, acc_ref[...])
            if with_dw:
                dwb_ref[...] = dw_ref[...].astype(BF16)
                own_ref[...] = _own_piece(dw_ref, place_ref)

    row = lambda w: pl.BlockSpec((TM, w), lambda i: (i, 0))
    full = _whole
    dw_specs = [full((D_IN // 8, D)), full((D_IN, D))] if with_dw else []
    dw_shapes = [jax.ShapeDtypeStruct((D_IN // 8, D), F32), jax.ShapeDtypeStruct((D_IN, D), BF16)] if with_dw else []
    return pl.pallas_call(
        body, name="bwd_in" if with_dw else "bwd_in_dx", grid=(n_steps,),
        in_specs=[row(D_IN), full((D_IN, D)), row(D), _layer(l, 1, D), row(D)]
        + [pl.BlockSpec(memory_space=pltpu.SMEM)] * with_dw + [ANY] * len(deps),
        out_specs=[row(D), full((8, 128))] + dw_specs,
        out_shape=[jax.ShapeDtypeStruct((S, D), F32), jax.ShapeDtypeStruct((8, 128), F32)] + dw_shapes,
        scratch_shapes=[pltpu.VMEM((8, D), F32)] + [pltpu.VMEM((D_IN, D), F32)] * with_dw,
        compiler_params=_params(),
    )(dproj, w_in_t, x, g_pre, dres, *((dw_place,) if with_dw else ()), *deps)


HBM =pl.BlockSpec(memory_space=pltpu.HBM)
SEM = pl.BlockSpec(memory_space=pltpu.SEMAPHORE)
def _split_copy(collective_id=None):
    return pltpu.CompilerParams(has_side_effects=pltpu.SideEffectType.DATAFLOW_SIDE_EFFECTING,
                                collective_id=collective_id)


SPLIT_COPY = _split_copy()


def _in_hbm(a):
    return pltpu.with_memory_space_constraint(a, pltpu.HBM)

def _place():
    return lax.axis_index("x"), lax.axis_index("y"), lax.axis_index("c")


def _other_chips(x, y):
    return [(1 - x, y), (x, 1 - y), (1 - x, 1 - y)]


def _peer(x, y, c, m):
    return (x ^ (m >> 2), y ^ ((m >> 1) & 1), c ^ (m & 1))


SAME_CORE = (2, 4, 6)


def _place_cast(name, src, chip_arr, tile, layers, deps=()):
    _, n, cols = src.shape
    steps = n // tile
    k = len(layers)

    def body(chip_ref, *refs):
        for s_ref, o_ref in zip(refs[:k], refs[k + len(deps):]):
            o_ref[...] = s_ref[...].astype(BF16)

    def layer_spec(l):
        return pl.BlockSpec((None, tile, cols), lambda i, chip: (l, i, 0))

    return pl.pallas_call(
        body, name=name,
        grid_spec=pltpu.PrefetchScalarGridSpec(
            num_scalar_prefetch=1, grid=(steps,),
            in_specs=[layer_spec(l) for l in layers] + [ANY] * len(deps),
            out_specs=[pl.BlockSpec((tile, cols), lambda i, chip: (chip[0] * steps + i, 0))] * k),
        out_shape=[jax.ShapeDtypeStruct((N_SHARDS * n, cols), BF16)] * k,
        compiler_params=_params(),
    )(chip_arr, *[src] * k, *deps)


def _chip_rows(ref, chip, half=None):
    n = ref.shape[0] // N_SHARDS
    if half is None:
        return ref.at[pl.ds(pl.multiple_of(chip * n, 16), n), :]
    return ref.at[pl.ds(pl.multiple_of(chip * n + half * (n // 2), 16), n // 2), :]


def _gather_start(name, bufs, halved, collective_id):
    n = len(bufs)

    def body(*refs):
        ins, send, recv, token = refs[:n], refs[n:2 * n], refs[2 * n:3 * n], refs[-1]
        x, y, c = _place()
        _handshake([(*chip, c) for chip in _other_chips(x, y)])
        for a, buf in enumerate(ins):
            own = _chip_rows(buf, 2 * x + y, c if a in halved else None)
            for j, chip in enumerate(_other_chips(x, y)):
                pltpu.make_async_remote_copy(src_ref=own, dst_ref=own, send_sem=send[a].at[j], recv_sem=recv[a].at[j],
                                             device_id=(*chip, c), device_id_type=MESH).start()
        token[...] = jnp.zeros_like(token)

    outs = pl.pallas_call(
        body, name=name, in_specs=[HBM] * n,
        out_specs=[SEM] * (2 * n) + [HBM] * n + [pl.BlockSpec(memory_space=pltpu.VMEM)],
        out_shape=[pltpu.SemaphoreType.DMA((3,))] * (2 * n) + [pltpu.HBM(b.shape, b.dtype) for b in bufs]
        + [jax.ShapeDtypeStruct((8, 128), F32)],
        input_output_aliases={a: 2 * n + a for a in range(n)},
        compiler_params=_split_copy(collective_id),
    )(*[_in_hbm(b) for b in bufs])
    return outs[:n], outs[n:2 * n], outs[2 * n:3 * n], outs[-1]


def _gather_wait(name, buf, send_sem, recv_sem, after, halved=False):
    def body(buf_ref, send_ref, recv_ref, *rest):
        x, y, c = _place()
        half = c if halved else None
        own = _chip_rows(buf_ref, 2 * x + y, half)
        for j, chip in enumerate(_other_chips(x, y)):
            copy = pltpu.make_async_remote_copy(src_ref=own, dst_ref=_chip_rows(buf_ref, 2 * chip[0] + chip[1], half),
                                                send_sem=send_ref.at[j], recv_sem=recv_ref.at[j],
                                                device_id=(*chip, c), device_id_type=MESH)
            copy.wait_send()
            copy.wait_recv()

    return pl.pallas_call(
        body, name=name, in_specs=[HBM, SEM, SEM] + [ANY] * len(after), out_specs=HBM,
        out_shape=pltpu.HBM(buf.shape, buf.dtype), input_output_aliases={0: 0}, compiler_params=SPLIT_COPY,
    )(buf, send_sem, recv_sem, *after)


def _handshake(peers):
    barrier = pltpu.get_barrier_semaphore()
    for peer in peers:
        pl.semaphore_signal(barrier, inc=1, device_id=peer, device_id_type=MESH)
    pl.semaphore_wait(barrier, len(peers))


def _sibling_handshake(x, y, c):
    _handshake([(x, y, 1 - c)])


def _forward_halves(name, buf, collective_id):
    def body(in_ref, out_ref, send_sems, recv_sems):
        x, y, c = _place()
        _sibling_handshake(x, y, c)

        def copy(j, chip, half):
            rows = 2 * chip[0] + chip[1]
            return pltpu.make_async_remote_copy(
                src_ref=_chip_rows(in_ref, rows, half), dst_ref=_chip_rows(out_ref, rows, half), send_sem=send_sems.at[j],
                recv_sem=recv_sems.at[j], device_id=(x, y, 1 - c), device_id_type=MESH)

        chips = _other_chips(x, y)
        for j, chip in enumerate(chips):
            copy(j, chip, c).start()
        for j, chip in enumerate(chips):
            copy(j, chip, c).wait_send()
            copy(j, chip, 1 - c).wait_recv()

    return pl.pallas_call(
        body, name=name, in_specs=[ANY], out_specs=ANY, out_shape=jax.ShapeDtypeStruct(buf.shape, buf.dtype),
        input_output_aliases={0: 0},
        scratch_shapes=[pltpu.SemaphoreType.DMA((3,))] * 2,
        compiler_params=pltpu.CompilerParams(collective_id=collective_id),
    )(buf)


def _piece_rows(ref, k):
    p = ref.shape[0] // 8
    return ref.at[pl.ds(pl.multiple_of(k * p, 32 // jnp.dtype(ref.dtype).itemsize), p), :]


def _exchange_start(name, arrays, collective_id):
    n = len(arrays)
    zones = [lax.empty((7, a.shape[0] // 8, a.shape[1]), a.dtype) for a in arrays]

    def body(*refs):
        srcs, lands = refs[:n], refs[n:2 * n]
        send, recv, token = refs[2 * n:3 * n], refs[3 * n:4 * n], refs[-1]
        x, y, c = _place()
        _handshake([_peer(x, y, c, m) for m in range(1, 8)])
        for a, (src, land) in enumerate(zip(srcs, lands)):
            for m in range(1, 8):
                px, py, pc = _peer(x, y, c, m)
                pltpu.make_async_remote_copy(
                    src_ref=_piece_rows(src, 4 * px + 2 * py + pc), dst_ref=land.at[m - 1], send_sem=send[a].at[m - 1],
                    recv_sem=recv[a].at[m - 1], device_id=(px, py, pc), device_id_type=MESH).start()
        token[...] = jnp.zeros_like(token)

    outs = pl.pallas_call(
        body, name=name, in_specs=[HBM] * (2 * n),
        out_specs=[SEM] * (2 * n) + [HBM] * (2 * n) + [pl.BlockSpec(memory_space=pltpu.VMEM)],
        out_shape=[pltpu.SemaphoreType.DMA((7,))] * (2 * n) + [pltpu.HBM(a.shape, a.dtype) for a in arrays + zones]
        + [jax.ShapeDtypeStruct((8, 128), F32)],
        input_output_aliases={a: 2 * n + a for a in range(2 * n)},
        compiler_params=_split_copy(collective_id),
    )(*[_in_hbm(a) for a in arrays + zones])
    return outs[:n], outs[n:2 * n], outs[2 * n:3 * n], outs[3 * n:4 * n], outs[-1]


def _exchange_wait(name, started, after, which=None, with_sent=False):
    which = range(len(started[2])) if which is None else which
    send_sems, recv_sems, arrays, zones = [[group[k] for k in which] for group in started[:4]]
    n = len(arrays)

    def body(*refs):
        srcs, lands = refs[:n], refs[n:2 * n]
        send, recv = refs[2 * n:3 * n], refs[3 * n:4 * n]
        x, y, c = _place()
        for a, (src, land) in enumerate(zip(srcs, lands)):
            for m in range(1, 8):
                px, py, pc = _peer(x, y, c, m)
                copy = pltpu.make_async_remote_copy(
                    src_ref=_piece_rows(src, 4 * px + 2 * py + pc), dst_ref=land.at[m - 1], send_sem=send[a].at[m - 1],
                    recv_sem=recv[a].at[m - 1], device_id=(px, py, pc), device_id_type=MESH)
                copy.wait_send()
                copy.wait_recv()

    outs = pl.pallas_call(
        body, name=name, in_specs=[HBM] * (2 * n) + [SEM] * (2 * n) + [ANY], out_specs=[HBM] * (2 * n),
        out_shape=[pltpu.HBM(a.shape, a.dtype) for a in list(arrays) + list(zones)],
        input_output_aliases={a: a for a in range(2 * n)}, compiler_params=SPLIT_COPY,
    )(*arrays, *zones, *send_sems, *recv_sems, after)
    return outs if with_sent else outs[n:]


def _sum_pieces(name, weights, place_arr, dests=None):
    steps = 2
    flat = [item for items in weights for item in items]
    n = len(flat)

    def body(place_ref, *refs):
        outs = iter(refs[len(refs) - len(weights):])
        k = 0
        for items in weights:
            out_ref = next(outs)
            for layer, _, _ in items:
                total = refs[k][...]
                for m in range(7):
                    total = total + refs[n + k][m].astype(F32)
                if len(items) == DEPTH:
                    out_ref[layer] = total
                else:
                    out_ref[...] = total
                k += 1

    def out_spec(items):
        _, own, _ = items[0]
        t, cols = own.shape[0] // steps, own.shape[1]
        if len(items) == DEPTH:
            return pl.BlockSpec((DEPTH, t, cols), lambda i, place: (0, place[1] * steps + i, 0))
        layer = items[0][0]
        return pl.BlockSpec((None, t, cols), lambda i, place: (layer, place[1] * steps + i, 0))

    owns = [own for _, own, _ in flat]
    dests = [] if dests is None else list(dests)
    return pl.pallas_call(
        body, name=name,
        grid_spec=pltpu.PrefetchScalarGridSpec(
            num_scalar_prefetch=1, grid=(steps,),
            in_specs=[pl.BlockSpec((o.shape[0] // steps, o.shape[1]), lambda i, place: (i, 0)) for o in owns]
            + [pl.BlockSpec((7, o.shape[0] // steps, o.shape[1]), lambda i, place: (0, i, 0)) for o in owns]
            + [ANY] * len(dests),
            out_specs=[out_spec(items) for items in weights]),
        out_shape=[jax.ShapeDtypeStruct((DEPTH, 2 * items[0][1].shape[0], items[0][1].shape[1]), F32)
                   for items in weights],
        input_output_aliases={1 + 2 * n + k: k for k in range(len(dests))},
        compiler_params=_params(),
    )(place_arr, *owns, *[recv for _, _, recv in flat], *dests)


def _sum_small(name, partials, recvs, place_arr):
    n = len(partials)

    def body(place_ref, *refs):
        for o_ref, r_ref, out_ref in zip(refs[:n], refs[n:2 * n], refs[2 * n:]):
            total = o_ref[...]
            for m in range(7):
                total = total + r_ref[m]
            out_ref[...] = total

    piece = lambda a: pl.BlockSpec((a.shape[0] // 8, a.shape[1]), lambda i, place: (place[0], 0))
    return pl.pallas_call(
        body, name=name,
        grid_spec=pltpu.PrefetchScalarGridSpec(
            num_scalar_prefetch=1, grid=(1,),
            in_specs=[piece(a) for a in partials] + [pl.BlockSpec(r.shape, lambda i, place: (0, 0, 0)) for r in recvs],
            out_specs=[piece(a) for a in partials]),
        out_shape=[jax.ShapeDtypeStruct(a.shape, F32) for a in partials],
        compiler_params=_params(),
    )(place_arr, *partials, *recvs)


def _share(name, bufs, parts, gathered=(), collective_id=None):
    n, n_g = len(bufs), len(gathered)
    total = n + n_g

    def body(*refs):
        ins, outs = refs[:total], refs[total:2 * total]
        send_sems, recv_sems, send_g, recv_g = refs[2 * total:]
        x, y, c = _place()
        _handshake([_peer(x, y, c, m) for m in (SAME_CORE if gathered else ()) + (1,)])

        def half(ref, l, which):
            p = ref.shape[1] // 2
            return ref.at[l, pl.ds(pl.multiple_of(which * p, 8), p), :]

        def swap(k, which):
            a, l = parts[k]
            return pltpu.make_async_remote_copy(
                src_ref=half(ins[a], l, which), dst_ref=half(outs[a], l, which), send_sem=send_sems.at[k],
                recv_sem=recv_sems.at[k], device_id=(x, y, 1 - c), device_id_type=MESH)

        def spread(a, m, sender, held, to):
            k = 4 * sender[0] + 2 * sender[1] + sender[2]
            return pltpu.make_async_remote_copy(
                src_ref=_piece_rows(held[n + a], k), dst_ref=_piece_rows(outs[n + a], k),
                send_sem=send_g.at[7 * a + m - 1], recv_sem=recv_g.at[7 * a + m - 1], device_id=to, device_id_type=MESH)

        me, sibling = (x, y, c), (x, y, 1 - c)
        for k in range(len(parts)):
            swap(k, c).start()
        def own(a, m):
            return spread(a, m, me, ins, _peer(x, y, c, m))

        def handed_on(a, m):
            return spread(a, m + 1, _peer(x, y, c, m), outs, sibling)

        for a in range(n_g):
            for m in SAME_CORE + (1,):
                own(a, m).start()
        for a in range(n_g):
            for m in SAME_CORE:
                spread(a, m, _peer(x, y, c, m), ins, _peer(x, y, c, m)).wait_recv()
                handed_on(a, m).start()
        for k in range(len(parts)):
            swap(k, c).wait_send()
            swap(k, 1 - c).wait_recv()
        for a in range(n_g):
            for m in SAME_CORE + (1,):
                own(a, m).wait_send()
            for m in SAME_CORE:
                handed_on(a, m).wait_send()
                spread(a, m + 1, _peer(x, y, c, m + 1), ins, sibling).wait_recv()
            spread(a, 1, sibling, ins, sibling).wait_recv()

    arrays = list(bufs) + list(gathered)
    return pl.pallas_call(
        body, name=name, in_specs=[ANY] * total, out_specs=[ANY] * total,
        out_shape=[jax.ShapeDtypeStruct(b.shape, F32) for b in arrays],
        input_output_aliases={a: a for a in range(total)},
        scratch_shapes=[pltpu.SemaphoreType.DMA((max(len(parts), 1),))] * 2
        + [pltpu.SemaphoreType.DMA((max(7 * n_g, 1),))] * 2,
        compiler_params=pltpu.CompilerParams(collective_id=collective_id),
    )(*arrays)


def _adamw_math(w, g, m, v):
    nm = ADAM_B1 * m + (1.0 - ADAM_B1) * g
    nv = ADAM_B2 * v + (1.0 - ADAM_B2) * (g * g)
    m_hat = nm / (1.0 - ADAM_B1 ** ADAM_STEP)
    v_hat = nv / (1.0 - ADAM_B2 ** ADAM_STEP)
    return -ADAM_LR * (m_hat / (jnp.sqrt(v_hat) + ADAM_EPS) + ADAM_WD * w), nm, nv


def _adamw(name, w, g, m, v, rows_per_step, first=0, count=None, dests=None, deps=()):
    layers, rows, cols = w.shape
    count = layers if count is None else count

    def body(w_ref, g_ref, m_ref, v_ref, *rest):
        d_ref, nm_ref, nv_ref, g_out_ref = rest[-4:]
        d_ref[...], nm_ref[...], nv_ref[...] = _adamw_math(w_ref[...], g_ref[...], m_ref[...], v_ref[...])
        g_out_ref[...] = g_ref[...]

    spec = pl.BlockSpec((1, rows_per_step, cols), lambda l, i: (first + l, i, 0))
    shape = jax.ShapeDtypeStruct(w.shape, F32)
    dests = () if dests is None else tuple(dests)
    return pl.pallas_call(
        body, name=name, grid=(count, rows // rows_per_step),
        in_specs=[spec] * 4 + [ANY] * (len(dests) + len(deps)), out_specs=[spec] * 4, out_shape=[shape] * 4,
        input_output_aliases={4 + k: k for k in range(len(dests))},
        compiler_params=_params(("arbitrary", "arbitrary")),
    )(w, g, m, v, *dests, *deps)


def _pack_misc(pool_scale, sinks, norm_pre, norm_post):
    sink_rows = jnp.zeros((DEPTH, 8, 128), F32).at[:, 0, 0:N_HEADS].set(sinks).reshape(2 * 8, 128)
    return jnp.concatenate([pool_scale.reshape(8, 128), norm_pre.reshape(16, 128), norm_post.reshape(16, 128),
                            sink_rows, jnp.zeros((8, 128), F32)], axis=0)


def _adamw_small(w, g, m, v, pool):
    def body(w_ref, g_ref, m_ref, v_ref, pw_ref, pg_ref, pm_ref, pv_ref, *rest):
        outs, pool_outs, (d_ref, nm_ref, nv_ref) = rest[:17], rest[17:20], rest[20:]
        pool_outs[0][...], pool_outs[1][...], pool_outs[2][...] = _adamw_math(
            pw_ref[...], pg_ref[...], pm_ref[...], pv_ref[...])
        d_ref[...], nm_ref[...], nv_ref[...] = _adamw_math(w_ref[...], g_ref[...], m_ref[...], v_ref[...])
        for k, src in enumerate([g_ref, d_ref, nm_ref, nv_ref]):
            scale, sinks, pre, post = outs[4 * k:4 * k + 4]
            for l in range(DEPTH):
                for j in range(4):
                    scale[l:l + 1, j * 128:(j + 1) * 128] = src[MISC_SCALE + 4 * l + j:MISC_SCALE + 4 * l + j + 1, :]
                for j in range(8):
                    pre[l:l + 1, j * 128:(j + 1) * 128] = src[MISC_PRE + 8 * l + j:MISC_PRE + 8 * l + j + 1, :]
                    post[l:l + 1, j * 128:(j + 1) * 128] = src[MISC_POST + 8 * l + j:MISC_POST + 8 * l + j + 1, :]
                sinks[l:l + 1, :] = src[MISC_SINKS + 8 * l:MISC_SINKS + 8 * l + 1, 0:N_HEADS]
        outs[16][...] = g_ref[MISC_LOSS:MISC_LOSS + 1, 0:1]

    vmem = pl.BlockSpec(memory_space=pltpu.VMEM)
    shapes = [(DEPTH, D_POOL), (DEPTH, N_HEADS), (DEPTH, D), (DEPTH, D)] * 4 + [(1, 1)]
    shapes += [pool[0].shape] * 3
    return pl.pallas_call(
        body, name="adamw_small", in_specs=[vmem] * 8, out_specs=[vmem] * 20,
        out_shape=[jax.ShapeDtypeStruct(s, F32) for s in shapes],
        scratch_shapes=[pltpu.VMEM((MISC_ROWS, 128), F32)] * 3,
    )(w, g, m, v, *pool)


def kernel(x, w_in, pool_w, pool_scale, attn_sinks, w_out, norm_pre, norm_post, loss_target, m_w_in, m_pool_w, m_pool_scale, m_attn_sinks, m_w_out, m_norm_pre, m_norm_post, v_w_in, v_pool_w, v_pool_scale, v_attn_sinks, v_w_out, v_norm_pre, v_norm_post):
    cx, cy, cc = _place()
    chip_arr = jnp.reshape(2 * cx + cy, (1,)).astype(jnp.int32)
    place_arr = jnp.stack([4 * cx + 2 * cy + cc, cc]).astype(jnp.int32)
    t = lambda a: jnp.transpose(a, (0, 2, 1))
    w_in_t = t(w_in)
    xs, target = x[0], loss_target[0]
    pool_w_b = pool_w.astype(BF16)
    tables = _attention_tables()
    scale3 = pool_scale.reshape(DEPTH, 1, D_POOL)
    pre3 = norm_pre.reshape(DEPTH, 1, D)
    post3 = norm_post.reshape(DEPTH, 1, D)

    (wi0,) = _place_cast("place_w_in0", w_in_t, chip_arr, 288, [0])
    first = _gather_start("gather_start_first", [wi0], halved=(0,), collective_id=ID_GATHER_FIRST)
    (wi1,) = _place_cast("place_w_in1", w_in_t, chip_arr, 288, [1], deps=(first[3],))
    wo = _place_cast("place_w_out", w_out, chip_arr, 256, [0, 1], deps=(first[3],))
    rest = _gather_start("gather_start_rest", [wi1, wo[0], wo[1]], halved=(0,), collective_id=ID_GATHER_REST)
    send, recv, bufs = [first[k] + rest[k] for k in range(3)]
    order = {(0, "in"): 0, (1, "in"): 1, (0, "out"): 2, (1, "out"): 3}

    saved = []
    packed = [_pack_misc(pool_scale, attn_sinks, norm_pre, norm_post),
              _pack_misc(m_pool_scale, m_attn_sinks, m_norm_pre, m_norm_post),
              _pack_misc(v_pool_scale, v_attn_sinks, v_norm_pre, v_norm_post)]
    after = (first[3], rest[3], pool_w_b, *tables, scale3, pre3, post3, *packed)
    below = None
    for l in range(DEPTH):
        k = order[l, "in"]
        w_in_l = _forward_halves(f"forward_w_in{l}", _gather_wait(f"gather_wait_in{l}", bufs[k], send[k], recv[k], after,
                                                                 halved=True), collective_id=ID_FORWARD[l])
        if below is None:
            pu, pg, q, kv, ag = _fwd_in(l, xs, pre3, w_in_l)
        else:
            y, xs, pu, pg, q, kv, ag = _fwd_in(l, xs, pre3, w_in_l, below)
            saved[l - 1][7] = y
        cat = _fwd_mix(l, pu, pg, q, kv, ag, pool_w_b, scale3, attn_sinks, tables)
        k = order[l, "out"]
        w_out_l = _gather_wait(f"gather_wait_out{l}", bufs[k], send[k], recv[k], (cat,))
        saved.append([xs, pu, pg, q, kv, ag, cat, None, w_in_l, w_out_l])
        below, after = (cat, w_out_l, post3), (w_out_l,)

    x_in, pu, pg, q, kv, ag, cat, y, w_in_l, w_out_l = saved[1]
    dcat, dw_out1, dw_out1_b, dg_post1, loss, xs = _bwd_out(1, cat, w_out_l, post3, place_arr, x=x_in, target=target)
    ex1_out = _exchange_start("exchange_start_out1", [dw_out1_b], ID_OUT1)
    dproj, dpw, dsc1, dsink1 = _bwd_mix(1, pu, pg, q, kv, ag, dcat, pool_w_b, scale3, attn_sinks, tables,
                                        deps=(ex1_out[4],))
    dx, dg_pre1, dw_in1, dw_in1_b = _bwd_in_dx(1, dproj, w_in_l, x_in, pre3, xs, dw_place=place_arr)
    ex1_in = _exchange_start("exchange_start_in1", [dw_in1_b], ID_IN1)

    x_in, pu, pg, q, kv, ag, cat, y, w_in_l, w_out_l = saved[0]
    dcat, dw_out0, dw_out0_b, dg_post0 = _bwd_out(0, cat, w_out_l, post3, place_arr, dxn=dx, y=y, deps=(ex1_in[4],))
    ex0_out = _exchange_start("exchange_start_out0", [dw_out0_b], ID_OUT0)
    dproj, dpw, dsc0, dsink0 = _bwd_mix(0, pu, pg, q, kv, ag, dcat, pool_w_b, scale3, attn_sinks, tables,
                                        deps=(ex0_out[4],), dpw_dest=dpw)
    dw_in0, dw_in0_b = _bwd_in_dw(0, dproj, x_in, pre3, place_arr)
    flat = lambda a: a.reshape(DEPTH * 4 * 128, 128)
    ex0_in = _exchange_start("exchange_start_in0", [flat(dpw), dw_in0_b], ID_IN0)

    grad_x, dg_pre0 = _bwd_in_dx(0, dproj, w_in_l, x_in, pre3, dx, deps=(ex0_in[4],))
    small = [jnp.concatenate([dsc0, dsc1, dg_pre0, dg_pre1, dg_post0, dg_post1, dsink0, dsink1, loss], axis=0)]
    ex_small = _exchange_start("exchange_start_small", small, ID_SMALL)
    (recv_out1,) = _exchange_wait("exchange_wait_out1", ex1_out, ex_small[4])
    (recv_in1,) = _exchange_wait("exchange_wait_in1", ex1_in, recv_out1)
    g_in, g_out = _sum_pieces("sum_pieces_1", [[(1, dw_in1, recv_in1)], [(1, dw_out1, recv_out1)]], place_arr)
    (recv_out0,) = _exchange_wait("exchange_wait_out0", ex0_out, g_out)
    (g_out,) = _sum_pieces("sum_pieces_out0", [[(0, dw_out0, recv_out0)]], place_arr, dests=[g_out])
    g_in, g_out = _share("share_a", [g_in, g_out], [(0, 1), (1, 0), (1, 1)], collective_id=ID_SHARE_A)
    m_in_t, v_in_t = t(m_w_in), t(v_w_in)
    d_out, nm_out, nv_out, grad_w_out = _adamw("adamw_w_out", w_out, g_out, m_w_out, v_w_out, 256)
    upd_in = _adamw("adamw_w_in1", w_in_t, g_in, m_in_t, v_in_t, 288, first=1, count=1, deps=(d_out,))
    dpw_own, recv_pw = _exchange_wait("exchange_wait_pool", ex0_in, upd_in[0], which=[0], with_sent=True)
    (g_pw,) = _sum_small("sum_pool", [dpw_own], [recv_pw], place_arr)

    (recv_in0,) = _exchange_wait("exchange_wait_in0", ex0_in, g_pw, which=[1])
    misc_own, recv_misc = _exchange_wait("exchange_wait_small", ex_small, recv_in0, with_sent=True)
    (g_in,) = _sum_pieces("sum_pieces_in0", [[(0, dw_in0, recv_in0)]], place_arr, dests=[g_in])
    (g_misc,) = _sum_small("sum_small", [misc_own], [recv_misc], place_arr)
    g_in, g_pw, g_misc = _share("share_b", [g_in], [(0, 0)], [g_pw, g_misc], collective_id=ID_SHARE_B)
    d_in, nm_in, nv_in, grad_w_in_t = _adamw("adamw_w_in0", w_in_t, g_in, m_in_t, v_in_t, 288, first=0, count=1,
                                             dests=upd_in)
    small_out = _adamw_small(packed[0], g_misc, packed[1], packed[2],
                             (flat(pool_w), g_pw, flat(m_pool_w), flat(v_pool_w)))
    (g_sc, g_sk, g_pre, g_post, d_sc, d_sk, d_pre, d_post,
     m_sc, m_sk, m_pre, m_post, v_sc, v_sk, v_pre, v_post, loss_sum) = small_out[:17]
    g_pw, d_pw, m_pw, v_pw = [a.reshape(pool_w.shape) for a in [g_pw] + list(small_out[17:])]
    return (loss_sum[0, 0], grad_x[None], t(grad_w_in_t), g_pw, g_sc, g_sk, grad_w_out, g_pre, g_post,
            t(d_in), d_pw, d_sc, d_sk, d_out, d_pre, d_post,
            t(nm_in), m_pw, m_sc, m_sk, nm_out, m_pre, m_post,
            t(nv_in), v_pw, v_sc, v_sk, nv_out, v_pre, v_post)
```

```python
import jax
import jax.numpy as jnp
from jax import lax
from jax.experimental import pallas as pl
from jax.experimental.pallas import tpu as pltpu

F32 = jnp.float32
BF16 = jnp.bfloat16

S = 2048
D = 1024
DEPTH = 2
D_POOL = 512
POOL_WINDOWS = (2, 4, 8, 16)
N_HEADS = 8
D_IN = 2304
N_SHARDS = 4
W_IN_SHARD = D_IN // N_SHARDS
W_OUT_SHARD = D // N_SHARDS
BLK = 128
NB = S // BLK
HALO = 16
PAD = 8
EPS = 1e-6
NEG_INF = -1e30
C_PU, C_PG, C_Q, C_K, C_V, C_AG = 0, 512, 1024, 1536, 1664, 1792

ADAM_LR = 0.001
ADAM_B1 = 0.9
ADAM_B2 = 0.999
ADAM_EPS = 1e-08
ADAM_WD = 0.01
ADAM_STEP = 10

TM = 512
VMEM_LIMIT = 56 * 1024 * 1024

NT = (((1,), (1,)), ((), ()))
TN = (((0,), (0,)), ((), ()))

MESH = pl.DeviceIdType.MESH
ANY = pl.BlockSpec(memory_space=pl.ANY)

ID_FORWARD = (0, 1)
(ID_SHARE_A, ID_SHARE_B, ID_GATHER_FIRST, ID_GATHER_REST, ID_OUT1, ID_IN1, ID_OUT0, ID_IN0, ID_SMALL) = range(2, 11)

MISC_SCALE, MISC_PRE, MISC_POST, MISC_SINKS, MISC_LOSS = 0, 8, 24, 40, 56
MISC_ROWS = 64


def _params(sem=("arbitrary",)):
    return pltpu.CompilerParams(dimension_semantics=sem, vmem_limit_bytes=VMEM_LIMIT)


def _sigmoid(v):
    return 1.0 / (1.0 + jnp.exp(-v))


def _rows8(v):
    r, c = v.shape
    return v.reshape(r // 8, 8, c).sum(axis=0)


def _layer(l, *shape):
    zeros = (0,) * len(shape)
    return pl.BlockSpec((None,) + shape, lambda i: (l,) + zeros)


def _whole(shape):
    zeros = (0,) * len(shape)
    return pl.BlockSpec(shape, lambda i: zeros, pipeline_mode=pl.Buffered(1))


def _fwd_in(l, x, g_pre, w_in_t, below=None):
    fused = below is not None

    def body(x_ref, g_ref, w_ref, *rest):
        if fused:
            cat_ref, wo_ref, gp_ref, y_ref, xn_ref = rest[:5]
            y = jnp.dot(cat_ref[...], wo_ref[...], preferred_element_type=F32)
            y_ref[...] = y
            xt = x_ref[...] + y * lax.rsqrt(jnp.mean(y * y, axis=-1, keepdims=True) + EPS) * gp_ref[...]
            xn_ref[...] = xt
        else:
            xt = x_ref[...]
        pu_ref, pg_ref, q_ref, kv_ref, ag_ref = rest[-5:]
        r = lax.rsqrt(jnp.mean(xt * xt, axis=-1, keepdims=True) + EPS)
        h = (xt * r * g_ref[...]).astype(BF16)

        def proj(lo, hi):
            return lax.dot_general(h, w_ref[lo:hi, :], NT, preferred_element_type=F32)

        pu_ref[...] = proj(C_PU, C_PG)
        pg_ref[...] = proj(C_PG, C_Q)
        q_ref[...] = proj(C_Q, C_K).astype(BF16)
        kv_ref[...] = proj(C_K, C_AG).astype(BF16)
        ag_ref[...] = proj(C_AG, D_IN)

    row = lambda w: pl.BlockSpec((TM, w), lambda i: (i, 0))
    act = jax.ShapeDtypeStruct((S, D), F32)
    return pl.pallas_call(
        body, name="fwd_out_in" if fused else "fwd_in", grid=(S // TM,),
        in_specs=[row(D), _layer(l, 1, D), _whole((D_IN, D))]
        + ([row(D), _whole((D, D)), _layer(l - 1, 1, D)] if fused else []),
        out_specs=[row(D)] * (2 * fused) + [row(512), row(512), row(512), row(256), row(512)],
        out_shape=[act] * (2 * fused)
        + [jax.ShapeDtypeStruct((S, 512), F32), jax.ShapeDtypeStruct((S, 512), F32),
           jax.ShapeDtypeStruct((S, 512), BF16), jax.ShapeDtypeStruct((S, 256), BF16),
           jax.ShapeDtypeStruct((S, 512), F32)],
        compiler_params=_params(),
    )(x, g_pre, w_in_t, *(below if fused else ()))


LOG2E = 1.4426950408889634
SCORE_SCALE = 0.125 * LOG2E


def _attention_tables():
    qi = jnp.arange(BLK)[:, None]
    kj = jnp.arange(BLK)[None, :]
    dist = ((qi - kj) % BLK).astype(F32)
    slopes = jnp.exp2(-jnp.arange(1, N_HEADS + 1, dtype=F32))
    bias = -(slopes * LOG2E)[:, None, None] * dist[None]
    first = jnp.where(kj > qi, NEG_INF, bias)
    return jnp.stack([first, bias]), (kj <= qi).astype(BF16)


def _own_block_mask():
    return lax.broadcasted_iota(jnp.int32, (BLK, BLK), 1) <= lax.broadcasted_iota(jnp.int32, (BLK, BLK), 0)


def _merge(full, own):
    return jnp.where(own, full[:, BLK:], full[:, :BLK])


def _spread(v, tri):
    own = v * tri
    return jnp.concatenate([v - own, own], axis=1)


def _head_variants(cur, prev):
    both = jnp.concatenate([prev, cur], axis=0).astype(F32)
    swapped = pltpu.roll(both, 64, axis=1)
    low = lax.broadcasted_iota(jnp.int32, both.shape, 1) < 64
    zero = jnp.zeros_like(both)
    return ((jnp.where(low, both, zero).astype(BF16), jnp.where(low, zero, swapped).astype(BF16)),
            (jnp.where(low, swapped, zero).astype(BF16), jnp.where(low, zero, both).astype(BF16)))


def _head_of(hkv, t, half):
    return hkv * 4 + 2 * t + half


def _rows(v, t):
    return v[t * BLK:(t + 1) * BLK]


def _stack_tiles(ref, hkv, offset=0):
    lo = offset + 2 * hkv * 128
    return jnp.concatenate([ref[:, lo:lo + 128], ref[:, lo + 128:lo + 256]], axis=0)


def _scores(q2, k_var, own):
    s = {}
    for hkv in range(2):
        for half in range(2):
            full = lax.dot_general(q2[hkv], k_var[hkv][half], NT, preferred_element_type=F32)
            for t in range(2):
                s[hkv, t, half] = _merge(_rows(full, t), own)
    return s


def _softmax(s, bias, sink):
    s = s * SCORE_SCALE + bias
    sink2 = sink * LOG2E
    m = jnp.maximum(jnp.max(s, axis=-1, keepdims=True), sink2)
    p = jnp.exp2(s - m)
    e_sink = jnp.exp2(sink2 - m)
    inv = 1.0 / (jnp.sum(p, axis=-1, keepdims=True) + e_sink)
    return p * inv, e_sink * inv


def _spread_pair(v, hkv, half, tri):
    return jnp.concatenate([_spread(v[hkv, t, half].astype(BF16), tri) for t in range(2)], axis=0)


POOL_ROWS = PAD + HALO + BLK


def _window_sums(src_ref, tmp_refs, trailing):
    lo, hi = (PAD, POOL_ROWS) if trailing else (0, HALO + BLK)
    cur = src_ref
    for level in range(len(POOL_WINDOWS)):
        lanes = slice(level * 128, 512)
        shift = -(1 << level) if trailing else (1 << level)
        dst = tmp_refs[level % 2]
        dst[lo:hi, lanes] = cur[lo:hi, lanes] + cur[lo + shift:hi + shift, lanes]
        cur = dst


def _pool_block(ext_ref, tmp_refs, i, g, w):
    lanes = slice(g * 128, (g + 1) * 128)
    rows = slice(PAD + HALO, POOL_ROWS)
    t = (i * BLK + lax.broadcasted_iota(jnp.int32, (BLK, 1), 0)).astype(F32)
    inv = 1.0 / jnp.minimum(t + 1.0, float(w))
    return tmp_refs[g % 2][rows, lanes] * inv - ext_ref[rows, lanes], inv


def _fwd_mix(l, pu, pg, q, kv, ag, pool_w, pool_scale, sinks, tables):
    bias, tri = tables

    def body(pu_ref, pup_ref, pg_ref, q_ref, kv_ref, kvp_ref, ag_ref, pw_ref, sc_ref, sink_ref, bias_ref, tri_ref,
             cat_ref, ext_ref, *tmp_refs):
        i = pl.program_id(0)

        @pl.when(i == 0)
        def _():
            for ref in (ext_ref, *tmp_refs):
                ref[0:PAD, :] = jnp.zeros((PAD, 512), F32)

        ext_ref[PAD:PAD + HALO, :] = jnp.where(i > 0, pup_ref[...], 0.0)
        ext_ref[PAD + HALO:POOL_ROWS, :] = pu_ref[...]
        _window_sums(ext_ref, tmp_refs, True)
        for g, w in enumerate(POOL_WINDOWS):
            lanes = slice(g * 128, (g + 1) * 128)
            pooled, _ = _pool_block(ext_ref, tmp_refs, i, g, w)
            mixed = jnp.dot(pooled.astype(BF16), pw_ref[g], preferred_element_type=F32)
            gate = pg_ref[:, lanes]
            cat_ref[:, lanes] = (mixed * sc_ref[:, lanes] * (gate * _sigmoid(gate))).astype(BF16)

        own = _own_block_mask()
        tri = tri_ref[...]
        k_var = _head_variants(kv_ref[:, 0:128], kvp_ref[:, 0:128])
        v_var = _head_variants(kv_ref[:, 128:256], kvp_ref[:, 128:256])
        s = _scores([_stack_tiles(q_ref, hkv) for hkv in range(2)], k_var, own)
        p = {}
        for (hkv, t, half), s_head in s.items():
            head = _head_of(hkv, t, half)
            p[hkv, t, half], _ = _softmax(s_head, bias_ref[head], sink_ref[l, head])
        for hkv in range(2):
            o2 = jnp.zeros((2 * BLK, 128), F32)
            for half in range(2):
                o2 = o2 + jnp.dot(_spread_pair(p, hkv, half, tri), v_var[hkv][half], preferred_element_type=F32)
            for t in range(2):
                lo = (2 * hkv + t) * 128
                gate = ag_ref[:, lo:lo + 128]
                cat_ref[:, D_POOL + lo:D_POOL + lo + 128] = (_rows(o2, t) * (gate * _sigmoid(gate))).astype(BF16)

    blk = lambda w: pl.BlockSpec((BLK, w), lambda i: (i, 0))
    prev = lambda w: pl.BlockSpec((BLK, w), lambda i: (jnp.maximum(i - 1, 0), 0))
    halo = pl.BlockSpec((HALO, 512), lambda i: (jnp.maximum(i * (BLK // HALO) - 1, 0), 0))
    return pl.pallas_call(
        body, name="fwd_mix", grid=(NB,),
        in_specs=[blk(512), halo, blk(512), blk(512), blk(256), prev(256), blk(512),
                  _layer(l, 4, 128, 128), _layer(l, 1, 512), pl.BlockSpec(memory_space=pltpu.SMEM),
                  pl.BlockSpec((None, N_HEADS, BLK, BLK), lambda i: (jnp.minimum(i, 1), 0, 0, 0)), _whole((BLK, BLK))],
        out_specs=blk(D),
        out_shape=jax.ShapeDtypeStruct((S, D), BF16),
        scratch_shapes=[pltpu.VMEM((POOL_ROWS, 512), F32)] * 3,
        compiler_params=_params(),
    )(pu, pu, pg, q, kv, kv, ag, pool_w, pool_scale, sinks, bias, tri)


def _store_lane_rows(ref, acc):
    total = jnp.sum(acc, axis=0, keepdims=True)
    for k in range(ref.shape[0]):
        ref[k:k + 1, :] = total[:, k * 128:(k + 1) * 128]


def _own_piece(dw_ref, place_ref):
    p = dw_ref.shape[0] // 8
    return dw_ref[pl.ds(pl.multiple_of(place_ref[0] * p, 8), p), :]


def _bwd_out(l, cat, w_out, g_post, place_arr, dxn=None, y=None, x=None, target=None, deps=()):
    last = target is not None
    n_steps = S // TM

    def body(a_ref, b_ref, g_ref, cat_ref, w_ref, place_ref, *rest):
        dcat_ref, own_ref, dwb_ref, dg_ref = rest[len(deps):len(deps) + 4]
        rest = rest[len(deps) + 4:]
        acc_ref, dw_ref = rest[-2:]
        step = pl.program_id(0)

        @pl.when(step == 0)
        def _():
            dw_ref[...] = jnp.zeros_like(dw_ref)
            acc_ref[...] = jnp.zeros_like(acc_ref)

        cat = cat_ref[...]
        g = g_ref[...]
        y = jnp.dot(cat, w_ref[...], preferred_element_type=F32) if last else b_ref[...]
        r = lax.rsqrt(jnp.mean(y * y, axis=-1, keepdims=True) + EPS)
        if last:
            loss_ref, dx_ref, loss_acc_ref = rest[:3]
            err = a_ref[...] + y * r * g - b_ref[...]

            @pl.when(step == 0)
            def _():
                loss_acc_ref[...] = jnp.zeros_like(loss_acc_ref)

            loss_acc_ref[...] += _rows8(err * err)
            dz = err * (1.0 / D)
            dx_ref[...] = dz
        else:
            dz = a_ref[...]
        a = dz * g
        dy = r * a - y * (r * r * r) * jnp.mean(a * y, axis=-1, keepdims=True)
        acc_ref[...] += _rows8(dz * (y * r))
        dyb = dy.astype(BF16)
        dcat_ref[...] = lax.dot_general(dyb, w_ref[...], NT, preferred_element_type=F32)
        dw_ref[...] += lax.dot_general(cat, dyb, TN, preferred_element_type=F32)

        @pl.when(step == n_steps - 1)
        def _():
            _store_lane_rows(dg_ref, acc_ref[...])
            dwb_ref[...] = dw_ref[...].astype(BF16)
            own_ref[...] = _own_piece(dw_ref, place_ref)
            if last:
                loss_ref[...] = jnp.full((8, 128), (0.5 / D) * jnp.sum(loss_acc_ref[...]), F32)

    row = lambda: pl.BlockSpec((TM, D), lambda i: (i, 0))
    full = _whole
    return pl.pallas_call(
        body, name="out_loss_bwd" if last else "bwd_out", grid=(n_steps,),
        in_specs=[row(), row(), _layer(l, 1, D), row(), full((D, D)), pl.BlockSpec(memory_space=pltpu.SMEM)]
        + [ANY] * len(deps),
        out_specs=[row(), full((D // 8, D)), full((D, D)), full((8, 128))] + ([full((8, 128)), row()] if last else []),
        out_shape=[jax.ShapeDtypeStruct((S, D), F32), jax.ShapeDtypeStruct((D // 8, D), F32),
                   jax.ShapeDtypeStruct((D, D), BF16), jax.ShapeDtypeStruct((8, 128), F32)]
        + ([jax.ShapeDtypeStruct((8, 128), F32), jax.ShapeDtypeStruct((S, D), F32)] if last else []),
        scratch_shapes=([pltpu.VMEM((8, D), F32)] if last else []) + [pltpu.VMEM((8, D), F32), pltpu.VMEM((D, D), F32)],
        compiler_params=_params(),
    )(*((x, target) if last else (dxn, y)), g_post, cat, w_out, place_arr, *deps)


def _bwd_mix(l, pu, pg, q, kv, ag, dcat, pool_w, pool_scale, sinks, tables, deps=(), dpw_dest=None):
    bias, tri = tables
    deps = tuple(deps) + (() if dpw_dest is None else (dpw_dest,))

    def body(pu_ref, pup_ref, pg_ref, q_ref, kv_ref, kvp_ref, ag_ref, dcat_ref, pw_ref, sc_ref, sink_ref, bias_ref,
             tri_ref, *rest):
        dproj_ref, dpw_ref, dsc_ref, dsink_ref, ext_ref, dext_ref, tmp_a, tmp_b, dkv_ref = rest[len(deps):]
        tmp_refs = (tmp_a, tmp_b)
        step = pl.program_id(0)
        i = NB - 1 - step

        @pl.when(step == 0)
        def _():
            dpw_ref[...] = jnp.zeros_like(dpw_ref)
            dsc_ref[...] = jnp.zeros_like(dsc_ref)
            dsink_ref[...] = jnp.zeros_like(dsink_ref)
            for ref in (ext_ref, tmp_a, tmp_b):
                ref[0:PAD, :] = jnp.zeros((PAD, 512), F32)
            dext_ref[BLK:POOL_ROWS, :] = jnp.zeros((HALO + PAD, 512), F32)
            dkv_ref[...] = jnp.zeros_like(dkv_ref)

        ext_ref[PAD:PAD + HALO, :] = jnp.where(i > 0, pup_ref[...], 0.0)
        ext_ref[PAD + HALO:POOL_ROWS, :] = pu_ref[...]
        _window_sums(ext_ref, tmp_refs, True)
        dpooled = []
        for g, w in enumerate(POOL_WINDOWS):
            lanes = slice(g * 128, (g + 1) * 128)
            pooled, inv = _pool_block(ext_ref, tmp_refs, i, g, w)
            pooled_b = pooled.astype(BF16)
            mixed = jnp.dot(pooled_b, pw_ref[g], preferred_element_type=F32)
            scale = sc_ref[:, lanes]
            gate = pg_ref[:, lanes]
            sg = _sigmoid(gate)
            dpo = dcat_ref[:, lanes]
            dproj_ref[:, C_PG + g * 128:C_PG + (g + 1) * 128] = (
                dpo * (mixed * scale) * (sg * (1.0 + gate * (1.0 - sg)))).astype(BF16)
            dms = dpo * (gate * sg)
            dsc_ref[g:g + 1, :] += jnp.sum(dms * mixed, axis=0, keepdims=True)
            dmixed = (dms * scale).astype(BF16)
            dpw_ref[g] += lax.dot_general(pooled_b, dmixed, TN, preferred_element_type=F32)
            dpooled.append(lax.dot_general(dmixed, pw_ref[g], NT, preferred_element_type=F32))
            dext_ref[0:BLK, lanes] = dpooled[g] * inv
        _window_sums(dext_ref, tmp_refs, False)
        for g in range(len(POOL_WINDOWS)):
            lanes = slice(g * 128, (g + 1) * 128)
            dproj_ref[:, C_PU + g * 128:C_PU + (g + 1) * 128] = (tmp_refs[g % 2][0:BLK, lanes] - dpooled[g]).astype(BF16)
        dext_ref[BLK:BLK + HALO, :] = dext_ref[0:HALO, :]

        own = _own_block_mask()
        tri = tri_ref[...]
        k_var = _head_variants(kv_ref[:, 0:128], kvp_ref[:, 0:128])
        v_var = _head_variants(kv_ref[:, 128:256], kvp_ref[:, 128:256])
        q2 = [_stack_tiles(q_ref, hkv) for hkv in range(2)]
        s = _scores(q2, k_var, own)
        p, p_sink = {}, {}
        for key, s_head in s.items():
            head = _head_of(*key)
            p[key], p_sink[key] = _softmax(s_head, bias_ref[head], sink_ref[l, head])

        do2, p_b, dp = [], {}, {}
        for hkv in range(2):
            gate = _stack_tiles(ag_ref, hkv)
            sg = _sigmoid(gate)
            dca = _stack_tiles(dcat_ref, hkv, D_POOL)
            do2.append((dca * (gate * sg)).astype(BF16))
            o2 = jnp.zeros((2 * BLK, 128), F32)
            for half in range(2):
                p_b[hkv, half] = _spread_pair(p, hkv, half, tri)
                o2 = o2 + jnp.dot(p_b[hkv, half], v_var[hkv][half], preferred_element_type=F32)
                full = lax.dot_general(do2[hkv], v_var[hkv][half], NT, preferred_element_type=F32)
                for t in range(2):
                    dp[hkv, t, half] = _merge(_rows(full, t), own)
            dag = dca * o2 * (sg * (1.0 + gate * (1.0 - sg)))
            for t in range(2):
                lo = C_AG + (2 * hkv + t) * 128
                dproj_ref[:, lo:lo + 128] = _rows(dag, t).astype(BF16)

        ds = {}
        for key in p:
            delta = jnp.sum(p[key] * dp[key], axis=-1, keepdims=True)
            ds[key] = p[key] * (dp[key] - delta)
            head = _head_of(*key)
            dsink_ref[0:1, :] += jnp.where(lax.broadcasted_iota(jnp.int32, (1, 128), 1) == head,
                                           -jnp.sum(p_sink[key] * delta, axis=0, keepdims=True), 0.0)

        dk_acc = [[None, None], [None, None]]
        dv_acc = [[None, None], [None, None]]
        for hkv in range(2):
            dq2 = jnp.zeros((2 * BLK, 128), F32)
            for half in range(2):
                ds_b = _spread_pair(ds, hkv, half, tri)
                dq2 = dq2 + jnp.dot(ds_b, k_var[hkv][half], preferred_element_type=F32)
                dk_acc[hkv][half] = lax.dot_general(ds_b, q2[hkv], TN, preferred_element_type=F32)
                dv_acc[hkv][half] = lax.dot_general(p_b[hkv, half], do2[hkv], TN, preferred_element_type=F32)
            for t in range(2):
                lo = C_Q + (2 * hkv + t) * 128
                dproj_ref[:, lo:lo + 128] = (_rows(dq2, t) * 0.125).astype(BF16)

        low = lax.broadcasted_iota(jnp.int32, (2 * BLK, 128), 1) < 64

        def gather_heads(acc):
            return jnp.where(low, acc[0][0] + pltpu.roll(acc[0][1], 64, axis=1),
                             pltpu.roll(acc[1][0], 64, axis=1) + acc[1][1])

        dk = gather_heads(dk_acc) * 0.125
        dv = gather_heads(dv_acc)
        dproj_ref[:, C_K:C_V] = (dk[BLK:, :] + dkv_ref[:, 0:128]).astype(BF16)
        dproj_ref[:, C_V:C_AG] = (dv[BLK:, :] + dkv_ref[:, 128:256]).astype(BF16)
        dkv_ref[:, 0:128] = dk[:BLK, :]
        dkv_ref[:, 128:256] = dv[:BLK, :]

    rev = lambda w: pl.BlockSpec((BLK, w), lambda s: (NB - 1 - s, 0))
    prev = lambda w: pl.BlockSpec((BLK, w), lambda s: (jnp.maximum(NB - 2 - s, 0), 0))
    halo = pl.BlockSpec((HALO, 512), lambda s: (jnp.maximum((NB - 1 - s) * (BLK // HALO) - 1, 0), 0))
    return pl.pallas_call(
        body, name="bwd_mix", grid=(NB,),
        in_specs=[rev(512), halo, rev(512), rev(512), rev(256), prev(256), rev(512), rev(D),
                  _layer(l, 4, 128, 128), _layer(l, 1, 512), pl.BlockSpec(memory_space=pltpu.SMEM),
                  pl.BlockSpec((None, N_HEADS, BLK, BLK), lambda s: (jnp.minimum(NB - 1 - s, 1), 0, 0, 0)),
                  _whole((BLK, BLK))] + [ANY] * len(deps),
        out_specs=[rev(D_IN), _layer(l, 4, 128, 128),
                   pl.BlockSpec((4, 128), lambda s: (0, 0)), pl.BlockSpec((8, 128), lambda s: (0, 0))],
        out_shape=[jax.ShapeDtypeStruct((S, D_IN), BF16), jax.ShapeDtypeStruct((DEPTH, 4, 128, 128), F32),
                   jax.ShapeDtypeStruct((4, 128), F32), jax.ShapeDtypeStruct((8, 128), F32)],
        input_output_aliases={} if dpw_dest is None else {12 + len(deps): 1},
        scratch_shapes=[pltpu.VMEM((POOL_ROWS, 512), F32)] * 4 + [pltpu.VMEM((BLK, 256), F32)],
        compiler_params=_params(),
    )(pu, pu, pg, q, kv, kv, ag, dcat, pool_w, pool_scale, sinks, bias, tri, *deps)


def _bwd_in_dw(l, dproj, x, g_pre, place_arr, deps=()):
    n_steps = S // TM

    def body(dp_ref, x_ref, g_ref, place_ref, *rest):
        own_ref, dwb_ref, dw_ref = rest[len(deps):]
        step = pl.program_id(0)

        @pl.when(step == 0)
        def _():
            dw_ref[...] = jnp.zeros_like(dw_ref)

        xt = x_ref[...]
        r = lax.rsqrt(jnp.mean(xt * xt, axis=-1, keepdims=True) + EPS)
        h = (xt * r * g_ref[...]).astype(BF16)
        dw_ref[...] += lax.dot_general(dp_ref[...], h, TN, preferred_element_type=F32)

        @pl.when(step == n_steps - 1)
        def _():
            dwb_ref[...] = dw_ref[...].astype(BF16)
            own_ref[...] = _own_piece(dw_ref, place_ref)

    row = lambda w: pl.BlockSpec((TM, w), lambda i: (i, 0))
    full = _whole
    return pl.pallas_call(
        body, name="bwd_in_dw", grid=(n_steps,),
        in_specs=[row(D_IN), row(D), _layer(l, 1, D), pl.BlockSpec(memory_space=pltpu.SMEM)] + [ANY] * len(deps),
        out_specs=[full((D_IN // 8, D)), full((D_IN, D))],
        out_shape=[jax.ShapeDtypeStruct((D_IN // 8, D), F32), jax.ShapeDtypeStruct((D_IN, D), BF16)],
        scratch_shapes=[pltpu.VMEM((D_IN, D), F32)],
        compiler_params=_params(),
    )(dproj, x, g_pre, place_arr, *deps)


def _bwd_in_dx(l, dproj, w_in_t, x, g_pre, dres, deps=(), dw_place=None):
    n_steps = S // TM
    with_dw = dw_place is not None

    def body(dp_ref, w_ref, x_ref, g_ref, dres_ref, *rest):
        place_ref = rest[0] if with_dw else None
        rest = rest[with_dw + len(deps):]
        if with_dw:
            dx_ref, dg_ref, own_ref, dwb_ref, acc_ref, dw_ref = rest
        else:
            dx_ref, dg_ref, acc_ref = rest
        step = pl.program_id(0)

        @pl.when(step == 0)
        def _():
            acc_ref[...] = jnp.zeros_like(acc_ref)
            if with_dw:
                dw_ref[...] = jnp.zeros_like(dw_ref)

        g = g_ref[...]
        halves = [slice(k * (TM // 2), (k + 1) * (TM // 2)) for k in range(2)]
        dh = [jnp.dot(dp_ref[rows, :], w_ref[...], preferred_element_type=F32) for rows in halves]
        h = []
        for rows, dh_k in zip(halves, dh):
            xt = x_ref[rows, :]
            r = lax.rsqrt(jnp.mean(xt * xt, axis=-1, keepdims=True) + EPS)
            xn = xt * r
            acc_ref[...] += _rows8(dh_k * xn)
            a = dh_k * g
            dx_ref[rows, :] = dres_ref[rows, :] + (
                r * a - xt * (r * r * r) * jnp.mean(a * xt, axis=-1, keepdims=True))
            h.append((xn * g).astype(BF16))
        if with_dw:
            dw_ref[...] += lax.dot_general(dp_ref[...], jnp.concatenate(h, axis=0), TN, preferred_element_type=F32)

        @pl.when(step == n_steps - 1)
        def _():
            _store_lane_rows(dg_ref, acc_ref[...])
            if with_dw:
                dwb_ref[...] = dw_ref[...].astype(BF16)
                own_ref[...] = _own_piece(dw_ref, place_ref)

    row = lambda w: pl.BlockSpec((TM, w), lambda i: (i, 0))
    full = _whole
    dw_specs = [full((D_IN // 8, D)), full((D_IN, D))] if with_dw else []
    dw_shapes = [jax.ShapeDtypeStruct((D_IN // 8, D), F32), jax.ShapeDtypeStruct((D_IN, D), BF16)] if with_dw else []
    return pl.pallas_call(
        body, name="bwd_in" if with_dw else "bwd_in_dx", grid=(n_steps,),
        in_specs=[row(D_IN), full((D_IN, D)), row(D), _layer(l, 1, D), row(D)]
        + [pl.BlockSpec(memory_space=pltpu.SMEM)] * with_dw + [ANY] * len(deps),
        out_specs=[row(D), full((8, 128))] + dw_specs,
        out_shape=[jax.ShapeDtypeStruct((S, D), F32), jax.ShapeDtypeStruct((8, 128), F32)] + dw_shapes,
        scratch_shapes=[pltpu.VMEM((8, D), F32)] + [pltpu.VMEM((D_IN, D), F32)] * with_dw,
        compiler_params=_params(),
    )(dproj, w_in_t, x, g_pre, dres, *((dw_place,) if with_dw else ()), *deps)


HBM =pl.BlockSpec(memory_space=pltpu.HBM)
SEM = pl.BlockSpec(memory_space=pltpu.SEMAPHORE)
def _split_copy(collective_id=None):
    return pltpu.CompilerParams(has_side_effects=pltpu.SideEffectType.DATAFLOW_SIDE_EFFECTING,
                                collective_id=collective_id)


SPLIT_COPY = _split_copy()


def _in_hbm(a):
    return pltpu.with_memory_space_constraint(a, pltpu.HBM)

def _place():
    return lax.axis_index("x"), lax.axis_index("y"), lax.axis_index("c")


def _other_chips(x, y):
    return [(1 - x, y), (x, 1 - y), (1 - x, 1 - y)]


def _peer(x, y, c, m):
    return (x ^ (m >> 2), y ^ ((m >> 1) & 1), c ^ (m & 1))


SAME_CORE = (2, 4, 6)


def _place_cast(name, src, chip_arr, tile, layers, deps=()):
    _, n, cols = src.shape
    steps = n // tile
    k = len(layers)

    def body(chip_ref, *refs):
        for s_ref, o_ref in zip(refs[:k], refs[k + len(deps):]):
            o_ref[...] = s_ref[...].astype(BF16)

    def layer_spec(l):
        return pl.BlockSpec((None, tile, cols), lambda i, chip: (l, i, 0))

    return pl.pallas_call(
        body, name=name,
        grid_spec=pltpu.PrefetchScalarGridSpec(
            num_scalar_prefetch=1, grid=(steps,),
            in_specs=[layer_spec(l) for l in layers] + [ANY] * len(deps),
            out_specs=[pl.BlockSpec((tile, cols), lambda i, chip: (chip[0] * steps + i, 0))] * k),
        out_shape=[jax.ShapeDtypeStruct((N_SHARDS * n, cols), BF16)] * k,
        compiler_params=_params(),
    )(chip_arr, *[src] * k, *deps)


def _chip_rows(ref, chip, half=None):
    n = ref.shape[0] // N_SHARDS
    if half is None:
        return ref.at[pl.ds(pl.multiple_of(chip * n, 16), n), :]
    return ref.at[pl.ds(pl.multiple_of(chip * n + half * (n // 2), 16), n // 2), :]


def _gather_start(name, bufs, halved, collective_id):
    n = len(bufs)

    def body(*refs):
        ins, send, recv, token = refs[:n], refs[n:2 * n], refs[2 * n:3 * n], refs[-1]
        x, y, c = _place()
        _handshake([(*chip, c) for chip in _other_chips(x, y)])
        for a, buf in enumerate(ins):
            own = _chip_rows(buf, 2 * x + y, c if a in halved else None)
            for j, chip in enumerate(_other_chips(x, y)):
                pltpu.make_async_remote_copy(src_ref=own, dst_ref=own, send_sem=send[a].at[j], recv_sem=recv[a].at[j],
                                             device_id=(*chip, c), device_id_type=MESH).start()
        token[...] = jnp.zeros_like(token)

    outs = pl.pallas_call(
        body, name=name, in_specs=[HBM] * n,
        out_specs=[SEM] * (2 * n) + [HBM] * n + [pl.BlockSpec(memory_space=pltpu.VMEM)],
        out_shape=[pltpu.SemaphoreType.DMA((3,))] * (2 * n) + [pltpu.HBM(b.shape, b.dtype) for b in bufs]
        + [jax.ShapeDtypeStruct((8, 128), F32)],
        input_output_aliases={a: 2 * n + a for a in range(n)},
        compiler_params=_split_copy(collective_id),
    )(*[_in_hbm(b) for b in bufs])
    return outs[:n], outs[n:2 * n], outs[2 * n:3 * n], outs[-1]


def _gather_wait(name, buf, send_sem, recv_sem, after, halved=False):
    def body(buf_ref, send_ref, recv_ref, *rest):
        x, y, c = _place()
        half = c if halved else None
        own = _chip_rows(buf_ref, 2 * x + y, half)
        for j, chip in enumerate(_other_chips(x, y)):
            copy = pltpu.make_async_remote_copy(src_ref=own, dst_ref=_chip_rows(buf_ref, 2 * chip[0] + chip[1], half),
                                                send_sem=send_ref.at[j], recv_sem=recv_ref.at[j],
                                                device_id=(*chip, c), device_id_type=MESH)
            copy.wait_send()
            copy.wait_recv()

    return pl.pallas_call(
        body, name=name, in_specs=[HBM, SEM, SEM] + [ANY] * len(after), out_specs=HBM,
        out_shape=pltpu.HBM(buf.shape, buf.dtype), input_output_aliases={0: 0}, compiler_params=SPLIT_COPY,
    )(buf, send_sem, recv_sem, *after)


def _handshake(peers):
    barrier = pltpu.get_barrier_semaphore()
    for peer in peers:
        pl.semaphore_signal(barrier, inc=1, device_id=peer, device_id_type=MESH)
    pl.semaphore_wait(barrier, len(peers))


def _sibling_handshake(x, y, c):
    _handshake([(x, y, 1 - c)])


def _forward_halves(name, buf, collective_id):
    def body(in_ref, out_ref, send_sems, recv_sems):
        x, y, c = _place()
        _sibling_handshake(x, y, c)

        def copy(j, chip, half):
            rows = 2 * chip[0] + chip[1]
            return pltpu.make_async_remote_copy(
                src_ref=_chip_rows(in_ref, rows, half), dst_ref=_chip_rows(out_ref, rows, half), send_sem=send_sems.at[j],
                recv_sem=recv_sems.at[j], device_id=(x, y, 1 - c), device_id_type=MESH)

        chips = _other_chips(x, y)
        for j, chip in enumerate(chips):
            copy(j, chip, c).start()
        for j, chip in enumerate(chips):
            copy(j, chip, c).wait_send()
            copy(j, chip, 1 - c).wait_recv()

    return pl.pallas_call(
        body, name=name, in_specs=[ANY], out_specs=ANY, out_shape=jax.ShapeDtypeStruct(buf.shape, buf.dtype),
        input_output_aliases={0: 0},
        scratch_shapes=[pltpu.SemaphoreType.DMA((3,))] * 2,
        compiler_params=pltpu.CompilerParams(collective_id=collective_id),
    )(buf)


def _piece_rows(ref, k):
    p = ref.shape[0] // 8
    return ref.at[pl.ds(pl.multiple_of(k * p, 32 // jnp.dtype(ref.dtype).itemsize), p), :]


def _exchange_start(name, arrays, collective_id):
    n = len(arrays)
    zones = [lax.empty((7, a.shape[0] // 8, a.shape[1]), a.dtype) for a in arrays]

    def body(*refs):
        srcs, lands = refs[:n], refs[n:2 * n]
        send, recv, token = refs[2 * n:3 * n], refs[3 * n:4 * n], refs[-1]
        x, y, c = _place()
        _handshake([_peer(x, y, c, m) for m in range(1, 8)])
        for a, (src, land) in enumerate(zip(srcs, lands)):
            for m in range(1, 8):
                px, py, pc = _peer(x, y, c, m)
                pltpu.make_async_remote_copy(
                    src_ref=_piece_rows(src, 4 * px + 2 * py + pc), dst_ref=land.at[m - 1], send_sem=send[a].at[m - 1],
                    recv_sem=recv[a].at[m - 1], device_id=(px, py, pc), device_id_type=MESH).start()
        token[...] = jnp.zeros_like(token)

    outs = pl.pallas_call(
        body, name=name, in_specs=[HBM] * (2 * n),
        out_specs=[SEM] * (2 * n) + [HBM] * (2 * n) + [pl.BlockSpec(memory_space=pltpu.VMEM)],
        out_shape=[pltpu.SemaphoreType.DMA((7,))] * (2 * n) + [pltpu.HBM(a.shape, a.dtype) for a in arrays + zones]
        + [jax.ShapeDtypeStruct((8, 128), F32)],
        input_output_aliases={a: 2 * n + a for a in range(2 * n)},
        compiler_params=_split_copy(collective_id),
    )(*[_in_hbm(a) for a in arrays + zones])
    return outs[:n], outs[n:2 * n], outs[2 * n:3 * n], outs[3 * n:4 * n], outs[-1]


def _exchange_wait(name, started, after, which=None, with_sent=False):
    which = range(len(started[2])) if which is None else which
    send_sems, recv_sems, arrays, zones = [[group[k] for k in which] for group in started[:4]]
    n = len(arrays)

    def body(*refs):
        srcs, lands = refs[:n], refs[n:2 * n]
        send, recv = refs[2 * n:3 * n], refs[3 * n:4 * n]
        x, y, c = _place()
        for a, (src, land) in enumerate(zip(srcs, lands)):
            for m in range(1, 8):
                px, py, pc = _peer(x, y, c, m)
                copy = pltpu.make_async_remote_copy(
                    src_ref=_piece_rows(src, 4 * px + 2 * py + pc), dst_ref=land.at[m - 1], send_sem=send[a].at[m - 1],
                    recv_sem=recv[a].at[m - 1], device_id=(px, py, pc), device_id_type=MESH)
                copy.wait_send()
                copy.wait_recv()

    outs = pl.pallas_call(
        body, name=name, in_specs=[HBM] * (2 * n) + [SEM] * (2 * n) + [ANY], out_specs=[HBM] * (2 * n),
        out_shape=[pltpu.HBM(a.shape, a.dtype) for a in list(arrays) + list(zones)],
        input_output_aliases={a: a for a in range(2 * n)}, compiler_params=SPLIT_COPY,
    )(*arrays, *zones, *send_sems, *recv_sems, after)
    return outs if with_sent else outs[n:]


def _sum_pieces(name, weights, place_arr, dests=None):
    steps = 2
    flat = [item for items in weights for item in items]
    n = len(flat)

    def body(place_ref, *refs):
        outs = iter(refs[len(refs) - len(weights):])
        k = 0
        for items in weights:
            out_ref = next(outs)
            for layer, _, _ in items:
                total = refs[k][...]
                for m in range(7):
                    total = total + refs[n + k][m].astype(F32)
                if len(items) == DEPTH:
                    out_ref[layer] = total
                else:
                    out_ref[...] = total
                k += 1

    def out_spec(items):
        _, own, _ = items[0]
        t, cols = own.shape[0] // steps, own.shape[1]
        if len(items) == DEPTH:
            return pl.BlockSpec((DEPTH, t, cols), lambda i, place: (0, place[1] * steps + i, 0))
        layer = items[0][0]
        return pl.BlockSpec((None, t, cols), lambda i, place: (layer, place[1] * steps + i, 0))

    owns = [own for _, own, _ in flat]
    dests = [] if dests is None else list(dests)
    return pl.pallas_call(
        body, name=name,
        grid_spec=pltpu.PrefetchScalarGridSpec(
            num_scalar_prefetch=1, grid=(steps,),
            in_specs=[pl.BlockSpec((o.shape[0] // steps, o.shape[1]), lambda i, place: (i, 0)) for o in owns]
            + [pl.BlockSpec((7, o.shape[0] // steps, o.shape[1]), lambda i, place: (0, i, 0)) for o in owns]
            + [ANY] * len(dests),
            out_specs=[out_spec(items) for items in weights]),
        out_shape=[jax.ShapeDtypeStruct((DEPTH, 2 * items[0][1].shape[0], items[0][1].shape[1]), F32)
                   for items in weights],
        input_output_aliases={1 + 2 * n + k: k for k in range(len(dests))},
        compiler_params=_params(),
    )(place_arr, *owns, *[recv for _, _, recv in flat], *dests)


def _sum_small(name, partials, recvs, place_arr):
    n = len(partials)

    def body(place_ref, *refs):
        for o_ref, r_ref, out_ref in zip(refs[:n], refs[n:2 * n], refs[2 * n:]):
            total = o_ref[...]
            for m in range(7):
                total = total + r_ref[m]
            out_ref[...] = total

    piece = lambda a: pl.BlockSpec((a.shape[0] // 8, a.shape[1]), lambda i, place: (place[0], 0))
    return pl.pallas_call(
        body, name=name,
        grid_spec=pltpu.PrefetchScalarGridSpec(
            num_scalar_prefetch=1, grid=(1,),
            in_specs=[piece(a) for a in partials] + [pl.BlockSpec(r.shape, lambda i, place: (0, 0, 0)) for r in recvs],
            out_specs=[piece(a) for a in partials]),
        out_shape=[jax.ShapeDtypeStruct(a.shape, F32) for a in partials],
        compiler_params=_params(),
    )(place_arr, *partials, *recvs)


def _share(name, bufs, parts, gathered=(), collective_id=None):
    n, n_g = len(bufs), len(gathered)
    total = n + n_g

    def body(*refs):
        ins, outs = refs[:total], refs[total:2 * total]
        send_sems, recv_sems, send_g, recv_g = refs[2 * total:]
        x, y, c = _place()
        _handshake([_peer(x, y, c, m) for m in (SAME_CORE if gathered else ()) + (1,)])

        def half(ref, l, which):
            p = ref.shape[1] // 2
            return ref.at[l, pl.ds(pl.multiple_of(which * p, 8), p), :]

        def swap(k, which):
            a, l = parts[k]
            return pltpu.make_async_remote_copy(
                src_ref=half(ins[a], l, which), dst_ref=half(outs[a], l, which), send_sem=send_sems.at[k],
                recv_sem=recv_sems.at[k], device_id=(x, y, 1 - c), device_id_type=MESH)

        def spread(a, m, sender, held, to):
            k = 4 * sender[0] + 2 * sender[1] + sender[2]
            return pltpu.make_async_remote_copy(
                src_ref=_piece_rows(held[n + a], k), dst_ref=_piece_rows(outs[n + a], k),
                send_sem=send_g.at[7 * a + m - 1], recv_sem=recv_g.at[7 * a + m - 1], device_id=to, device_id_type=MESH)

        me, sibling = (x, y, c), (x, y, 1 - c)
        for k in range(len(parts)):
            swap(k, c).start()
        def own(a, m):
            return spread(a, m, me, ins, _peer(x, y, c, m))

        def handed_on(a, m):
            return spread(a, m + 1, _peer(x, y, c, m), outs, sibling)

        for a in range(n_g):
            for m in SAME_CORE + (1,):
                own(a, m).start()
        for a in range(n_g):
            for m in SAME_CORE:
                spread(a, m, _peer(x, y, c, m), ins, _peer(x, y, c, m)).wait_recv()
                handed_on(a, m).start()
        for k in range(len(parts)):
            swap(k, c).wait_send()
            swap(k, 1 - c).wait_recv()
        for a in range(n_g):
            for m in SAME_CORE + (1,):
                own(a, m).wait_send()
            for m in SAME_CORE:
                handed_on(a, m).wait_send()
                spread(a, m + 1, _peer(x, y, c, m + 1), ins, sibling).wait_recv()
            spread(a, 1, sibling, ins, sibling).wait_recv()

    arrays = list(bufs) + list(gathered)
    return pl.pallas_call(
        body, name=name, in_specs=[ANY] * total, out_specs=[ANY] * total,
        out_shape=[jax.ShapeDtypeStruct(b.shape, F32) for b in arrays],
        input_output_aliases={a: a for a in range(total)},
        scratch_shapes=[pltpu.SemaphoreType.DMA((max(len(parts), 1),))] * 2
        + [pltpu.SemaphoreType.DMA((max(7 * n_g, 1),))] * 2,
        compiler_params=pltpu.CompilerParams(collective_id=collective_id),
    )(*arrays)


def _adamw_math(w, g, m, v):
    nm = ADAM_B1 * m + (1.0 - ADAM_B1) * g
    nv = ADAM_B2 * v + (1.0 - ADAM_B2) * (g * g)
    m_hat = nm / (1.0 - ADAM_B1 ** ADAM_STEP)
    v_hat = nv / (1.0 - ADAM_B2 ** ADAM_STEP)
    return -ADAM_LR * (m_hat / (jnp.sqrt(v_hat) + ADAM_EPS) + ADAM_WD * w), nm, nv


def _adamw(name, w, g, m, v, rows_per_step, first=0, count=None, dests=None, deps=()):
    layers, rows, cols = w.shape
    count = layers if count is None else count

    def body(w_ref, g_ref, m_ref, v_ref, *rest):
        d_ref, nm_ref, nv_ref, g_out_ref = rest[-4:]
        d_ref[...], nm_ref[...], nv_ref[...] = _adamw_math(w_ref[...], g_ref[...], m_ref[...], v_ref[...])
        g_out_ref[...] = g_ref[...]

    spec = pl.BlockSpec((1, rows_per_step, cols), lambda l, i: (first + l, i, 0))
    shape = jax.ShapeDtypeStruct(w.shape, F32)
    dests = () if dests is None else tuple(dests)
    return pl.pallas_call(
        body, name=name, grid=(count, rows // rows_per_step),
        in_specs=[spec] * 4 + [ANY] * (len(dests) + len(deps)), out_specs=[spec] * 4, out_shape=[shape] * 4,
        input_output_aliases={4 + k: k for k in range(len(dests))},
        compiler_params=_params(("arbitrary", "arbitrary")),
    )(w, g, m, v, *dests, *deps)


def _pack_misc(pool_scale, sinks, norm_pre, norm_post):
    sink_rows = jnp.zeros((DEPTH, 8, 128), F32).at[:, 0, 0:N_HEADS].set(sinks).reshape(2 * 8, 128)
    return jnp.concatenate([pool_scale.reshape(8, 128), norm_pre.reshape(16, 128), norm_post.reshape(16, 128),
                            sink_rows, jnp.zeros((8, 128), F32)], axis=0)


def _adamw_small(w, g, m, v, pool):
    def body(w_ref, g_ref, m_ref, v_ref, pw_ref, pg_ref, pm_ref, pv_ref, *rest):
        outs, pool_outs, (d_ref, nm_ref, nv_ref) = rest[:17], rest[17:21], rest[21:]
        pool_outs[0][...] = pg_ref[...]
        pool_outs[1][...], pool_outs[2][...], pool_outs[3][...] = _adamw_math(
            pw_ref[...], pg_ref[...], pm_ref[...], pv_ref[...])
        d_ref[...], nm_ref[...], nv_ref[...] = _adamw_math(w_ref[...], g_ref[...], m_ref[...], v_ref[...])
        for k, src in enumerate([g_ref, d_ref, nm_ref, nv_ref]):
            scale, sinks, pre, post = outs[4 * k:4 * k + 4]
            for l in range(DEPTH):
                for j in range(4):
                    scale[l:l + 1, j * 128:(j + 1) * 128] = src[MISC_SCALE + 4 * l + j:MISC_SCALE + 4 * l + j + 1, :]
                for j in range(8):
                    pre[l:l + 1, j * 128:(j + 1) * 128] = src[MISC_PRE + 8 * l + j:MISC_PRE + 8 * l + j + 1, :]
                    post[l:l + 1, j * 128:(j + 1) * 128] = src[MISC_POST + 8 * l + j:MISC_POST + 8 * l + j + 1, :]
                sinks[l:l + 1, :] = src[MISC_SINKS + 8 * l:MISC_SINKS + 8 * l + 1, 0:N_HEADS]
        outs[16][...] = g_ref[MISC_LOSS:MISC_LOSS + 1, 0:1]

    vmem = pl.BlockSpec(memory_space=pltpu.VMEM)
    shapes = [(DEPTH, D_POOL), (DEPTH, N_HEADS), (DEPTH, D), (DEPTH, D)] * 4 + [(1, 1)]
    shapes += [pool[0].shape] * 4
    return pl.pallas_call(
        body, name="adamw_small", in_specs=[vmem] * 8, out_specs=[vmem] * 21,
        out_shape=[jax.ShapeDtypeStruct(s, F32) for s in shapes],
        scratch_shapes=[pltpu.VMEM((MISC_ROWS, 128), F32)] * 3,
    )(w, g, m, v, *pool)


def kernel(x, w_in, pool_w, pool_scale, attn_sinks, w_out, norm_pre, norm_post, loss_target, m_w_in, m_pool_w, m_pool_scale, m_attn_sinks, m_w_out, m_norm_pre, m_norm_post, v_w_in, v_pool_w, v_pool_scale, v_attn_sinks, v_w_out, v_norm_pre, v_norm_post):
    cx, cy, cc = _place()
    chip_arr = jnp.reshape(2 * cx + cy, (1,)).astype(jnp.int32)
    place_arr = jnp.stack([4 * cx + 2 * cy + cc, cc]).astype(jnp.int32)
    t = lambda a: jnp.transpose(a, (0, 2, 1))
    w_in_t = t(w_in)
    xs, target = x[0], loss_target[0]
    pool_w_b = pool_w.astype(BF16)
    tables = _attention_tables()
    scale3 = pool_scale.reshape(DEPTH, 1, D_POOL)
    pre3 = norm_pre.reshape(DEPTH, 1, D)
    post3 = norm_post.reshape(DEPTH, 1, D)

    (wi0,) = _place_cast("place_w_in0", w_in_t, chip_arr, 288, [0])
    first = _gather_start("gather_start_first", [wi0], halved=(0,), collective_id=ID_GATHER_FIRST)
    (wi1,) = _place_cast("place_w_in1", w_in_t, chip_arr, 288, [1], deps=(first[3],))
    wo = _place_cast("place_w_out", w_out, chip_arr, 256, [0, 1], deps=(first[3],))
    rest = _gather_start("gather_start_rest", [wi1, wo[0], wo[1]], halved=(0,), collective_id=ID_GATHER_REST)
    send, recv, bufs = [first[k] + rest[k] for k in range(3)]
    order = {(0, "in"): 0, (1, "in"): 1, (0, "out"): 2, (1, "out"): 3}

    saved = []
    packed = [_pack_misc(pool_scale, attn_sinks, norm_pre, norm_post),
              _pack_misc(m_pool_scale, m_attn_sinks, m_norm_pre, m_norm_post),
              _pack_misc(v_pool_scale, v_attn_sinks, v_norm_pre, v_norm_post)]
    after = (first[3], rest[3], pool_w_b, *tables, scale3, pre3, post3, *packed)
    below = None
    for l in range(DEPTH):
        k = order[l, "in"]
        w_in_l = _forward_halves(f"forward_w_in{l}", _gather_wait(f"gather_wait_in{l}", bufs[k], send[k], recv[k], after,
                                                                 halved=True), collective_id=ID_FORWARD[l])
        if below is None:
            pu, pg, q, kv, ag = _fwd_in(l, xs, pre3, w_in_l)
        else:
            y, xs, pu, pg, q, kv, ag = _fwd_in(l, xs, pre3, w_in_l, below)
            saved[l - 1][7] = y
        cat = _fwd_mix(l, pu, pg, q, kv, ag, pool_w_b, scale3, attn_sinks, tables)
        k = order[l, "out"]
        w_out_l = _gather_wait(f"gather_wait_out{l}", bufs[k], send[k], recv[k], (cat,))
        saved.append([xs, pu, pg, q, kv, ag, cat, None, w_in_l, w_out_l])
        below, after = (cat, w_out_l, post3), (w_out_l,)

    x_in, pu, pg, q, kv, ag, cat, y, w_in_l, w_out_l = saved[1]
    dcat, dw_out1, dw_out1_b, dg_post1, loss, xs = _bwd_out(1, cat, w_out_l, post3, place_arr, x=x_in, target=target)
    ex1_out = _exchange_start("exchange_start_out1", [dw_out1_b], ID_OUT1)
    dproj, dpw, dsc1, dsink1 = _bwd_mix(1, pu, pg, q, kv, ag, dcat, pool_w_b, scale3, attn_sinks, tables,
                                        deps=(ex1_out[4],))
    dx, dg_pre1, dw_in1, dw_in1_b = _bwd_in_dx(1, dproj, w_in_l, x_in, pre3, xs, dw_place=place_arr)
    ex1_in = _exchange_start("exchange_start_in1", [dw_in1_b], ID_IN1)

    x_in, pu, pg, q, kv, ag, cat, y, w_in_l, w_out_l = saved[0]
    dcat, dw_out0, dw_out0_b, dg_post0 = _bwd_out(0, cat, w_out_l, post3, place_arr, dxn=dx, y=y, deps=(ex1_in[4],))
    ex0_out = _exchange_start("exchange_start_out0", [dw_out0_b], ID_OUT0)
    dproj, dpw, dsc0, dsink0 = _bwd_mix(0, pu, pg, q, kv, ag, dcat, pool_w_b, scale3, attn_sinks, tables,
                                        deps=(ex0_out[4],), dpw_dest=dpw)
    dw_in0, dw_in0_b = _bwd_in_dw(0, dproj, x_in, pre3, place_arr)
    flat = lambda a: a.reshape(DEPTH * 4 * 128, 128)
    ex0_in = _exchange_start("exchange_start_in0", [flat(dpw), dw_in0_b], ID_IN0)

    grad_x, dg_pre0 = _bwd_in_dx(0, dproj, w_in_l, x_in, pre3, dx, deps=(ex0_in[4],))
    small = [jnp.concatenate([dsc0, dsc1, dg_pre0, dg_pre1, dg_post0, dg_post1, dsink0, dsink1, loss], axis=0)]
    ex_small = _exchange_start("exchange_start_small", small, ID_SMALL)
    (recv_out1,) = _exchange_wait("exchange_wait_out1", ex1_out, ex_small[4])
    (recv_in1,) = _exchange_wait("exchange_wait_in1", ex1_in, recv_out1)
    g_in, g_out = _sum_pieces("sum_pieces_1", [[(1, dw_in1, recv_in1)], [(1, dw_out1, recv_out1)]], place_arr)
    (recv_out0,) = _exchange_wait("exchange_wait_out0", ex0_out, g_out)
    (g_out,) = _sum_pieces("sum_pieces_out0", [[(0, dw_out0, recv_out0)]], place_arr, dests=[g_out])
    g_in, g_out = _share("share_a", [g_in, g_out], [(0, 1), (1, 0), (1, 1)], collective_id=ID_SHARE_A)
    m_in_t, v_in_t = t(m_w_in), t(v_w_in)
    d_out, nm_out, nv_out, grad_w_out = _adamw("adamw_w_out", w_out, g_out, m_w_out, v_w_out, 256)
    upd_in = _adamw("adamw_w_in1", w_in_t, g_in, m_in_t, v_in_t, 288, first=1, count=1, deps=(d_out,))
    dpw_own, recv_pw = _exchange_wait("exchange_wait_pool", ex0_in, upd_in[0], which=[0], with_sent=True)
    (g_pw,) = _sum_small("sum_pool", [dpw_own], [recv_pw], place_arr)

    (recv_in0,) = _exchange_wait("exchange_wait_in0", ex0_in, g_pw, which=[1])
    recv_misc = _exchange_wait("exchange_wait_small", ex_small, recv_in0)
    (g_in,) = _sum_pieces("sum_pieces_in0", [[(0, dw_in0, recv_in0)]], place_arr, dests=[g_in])
    (g_misc,) = _sum_small("sum_small", small, recv_misc, place_arr)
    g_in, g_pw, g_misc = _share("share_b", [g_in], [(0, 0)], [g_pw, g_misc], collective_id=ID_SHARE_B)
    d_in, nm_in, nv_in, grad_w_in_t = _adamw("adamw_w_in0", w_in_t, g_in, m_in_t, v_in_t, 288, first=0, count=1,
                                             dests=upd_in)
    small_out = _adamw_small(packed[0], g_misc, packed[1], packed[2],
                             (flat(pool_w), g_pw, flat(m_pool_w), flat(v_pool_w)))
    (g_sc, g_sk, g_pre, g_post, d_sc, d_sk, d_pre, d_post,
     m_sc, m_sk, m_pre, m_post, v_sc, v_sk, v_pre, v_post, loss_sum) = small_out[:17]
    g_pw, d_pw, m_pw, v_pw = [a.reshape(pool_w.shape) for a in small_out[17:]]
    return (loss_sum[0, 0], grad_x[None], t(grad_w_in_t), g_pw, g_sc, g_sk, grad_w_out, g_pre, g_post,
            t(d_in), d_pw, d_sc, d_sk, d_out, d_pre, d_post,
            t(nm_in), m_pw, m_sc, m_sk, nm_out, m_pre, m_post,
            t(nv_in), v_pw, v_sc, v_sk, nv_out, v_pre, v_post)
```

```python
import jax
import jax.numpy as jnp
from jax import lax
from jax.experimental import pallas as pl
from jax.experimental.pallas import tpu as pltpu

F32 = jnp.float32
BF16 = jnp.bfloat16

S = 2048
D = 1024
DEPTH = 2
D_POOL = 512
POOL_WINDOWS = (2, 4, 8, 16)
N_HEADS = 8
D_IN = 2304
N_SHARDS = 4
W_IN_SHARD = D_IN // N_SHARDS
W_OUT_SHARD = D // N_SHARDS
BLK = 128
NB = S // BLK
HALO = 16
PAD = 8
EPS = 1e-6
NEG_INF = -1e30
C_PU, C_PG, C_Q, C_K, C_V, C_AG = 0, 512, 1024, 1536, 1664, 1792

ADAM_LR = 0.001
ADAM_B1 = 0.9
ADAM_B2 = 0.999
ADAM_EPS = 1e-08
ADAM_WD = 0.01
ADAM_STEP = 10

TM = 512
VMEM_LIMIT = 56 * 1024 * 1024

NT = (((1,), (1,)), ((), ()))
TN = (((0,), (0,)), ((), ()))

MESH = pl.DeviceIdType.MESH
ANY = pl.BlockSpec(memory_space=pl.ANY)

ID_FORWARD = (0, 1)
(ID_SHARE_A, ID_SHARE_B, ID_GATHER_FIRST, ID_GATHER_REST, ID_OUT1, ID_IN1, ID_OUT0, ID_IN0, ID_SMALL) = range(2, 11)

MISC_SCALE, MISC_PRE, MISC_POST, MISC_SINKS, MISC_LOSS = 0, 8, 24, 40, 56
MISC_ROWS = 64


def _params(sem=("arbitrary",)):
    return pltpu.CompilerParams(dimension_semantics=sem, vmem_limit_bytes=VMEM_LIMIT)


def _sigmoid(v):
    return 1.0 / (1.0 + jnp.exp(-v))


def _rows8(v):
    r, c = v.shape
    return v.reshape(r // 8, 8, c).sum(axis=0)


def _layer(l, *shape):
    zeros = (0,) * len(shape)
    return pl.BlockSpec((None,) + shape, lambda i: (l,) + zeros)


def _whole(shape):
    zeros = (0,) * len(shape)
    return pl.BlockSpec(shape, lambda i: zeros, pipeline_mode=pl.Buffered(1))


def _fwd_in(l, x, g_pre, w_in_t, below=None):
    fused = below is not None

    def body(x_ref, g_ref, w_ref, *rest):
        if fused:
            cat_ref, wo_ref, gp_ref, y_ref, xn_ref = rest[:5]
            y = jnp.dot(cat_ref[...], wo_ref[...], preferred_element_type=F32)
            y_ref[...] = y
            xt = x_ref[...] + y * lax.rsqrt(jnp.mean(y * y, axis=-1, keepdims=True) + EPS) * gp_ref[...]
            xn_ref[...] = xt
        else:
            xt = x_ref[...]
        pu_ref, pg_ref, q_ref, kv_ref, ag_ref = rest[-5:]
        r = lax.rsqrt(jnp.mean(xt * xt, axis=-1, keepdims=True) + EPS)
        h = (xt * r * g_ref[...]).astype(BF16)

        def proj(lo, hi):
            return lax.dot_general(h, w_ref[lo:hi, :], NT, preferred_element_type=F32)

        pu_ref[...] = proj(C_PU, C_PG)
        pg_ref[...] = proj(C_PG, C_Q)
        q_ref[...] = proj(C_Q, C_K).astype(BF16)
        kv_ref[...] = proj(C_K, C_AG).astype(BF16)
        ag_ref[...] = proj(C_AG, D_IN)

    row = lambda w: pl.BlockSpec((TM, w), lambda i: (i, 0))
    act = jax.ShapeDtypeStruct((S, D), F32)
    return pl.pallas_call(
        body, name="fwd_out_in" if fused else "fwd_in", grid=(S // TM,),
        in_specs=[row(D), _layer(l, 1, D), _whole((D_IN, D))]
        + ([row(D), _whole((D, D)), _layer(l - 1, 1, D)] if fused else []),
        out_specs=[row(D)] * (2 * fused) + [row(512), row(512), row(512), row(256), row(512)],
        out_shape=[act] * (2 * fused)
        + [jax.ShapeDtypeStruct((S, 512), F32), jax.ShapeDtypeStruct((S, 512), F32),
           jax.ShapeDtypeStruct((S, 512), BF16), jax.ShapeDtypeStruct((S, 256), BF16),
           jax.ShapeDtypeStruct((S, 512), F32)],
        compiler_params=_params(),
    )(x, g_pre, w_in_t, *(below if fused else ()))


LOG2E = 1.4426950408889634
SCORE_SCALE = 0.125 * LOG2E


def _attention_tables():
    qi = jnp.arange(BLK)[:, None]
    kj = jnp.arange(BLK)[None, :]
    dist = ((qi - kj) % BLK).astype(F32)
    slopes = jnp.exp2(-jnp.arange(1, N_HEADS + 1, dtype=F32))
    bias = -(slopes * LOG2E)[:, None, None] * dist[None]
    first = jnp.where(kj > qi, NEG_INF, bias)
    return jnp.stack([first, bias]), (kj <= qi).astype(BF16)


def _own_block_mask():
    return lax.broadcasted_iota(jnp.int32, (BLK, BLK), 1) <= lax.broadcasted_iota(jnp.int32, (BLK, BLK), 0)


def _merge(full, own):
    return jnp.where(own, full[:, BLK:], full[:, :BLK])


def _spread(v, tri):
    own = v * tri
    return jnp.concatenate([v - own, own], axis=1)


def _head_variants(cur, prev):
    both = jnp.concatenate([prev, cur], axis=0).astype(F32)
    swapped = pltpu.roll(both, 64, axis=1)
    low = lax.broadcasted_iota(jnp.int32, both.shape, 1) < 64
    zero = jnp.zeros_like(both)
    return ((jnp.where(low, both, zero).astype(BF16), jnp.where(low, zero, swapped).astype(BF16)),
            (jnp.where(low, swapped, zero).astype(BF16), jnp.where(low, zero, both).astype(BF16)))


def _head_of(hkv, t, half):
    return hkv * 4 + 2 * t + half


def _rows(v, t):
    return v[t * BLK:(t + 1) * BLK]


def _stack_tiles(ref, hkv, offset=0):
    lo = offset + 2 * hkv * 128
    return jnp.concatenate([ref[:, lo:lo + 128], ref[:, lo + 128:lo + 256]], axis=0)


def _scores(q2, k_var, own):
    s = {}
    for hkv in range(2):
        for half in range(2):
            full = lax.dot_general(q2[hkv], k_var[hkv][half], NT, preferred_element_type=F32)
            for t in range(2):
                s[hkv, t, half] = _merge(_rows(full, t), own)
    return s


def _softmax(s, bias, sink):
    s = s * SCORE_SCALE + bias
    sink2 = sink * LOG2E
    m = jnp.maximum(jnp.max(s, axis=-1, keepdims=True), sink2)
    p = jnp.exp2(s - m)
    e_sink = jnp.exp2(sink2 - m)
    inv = 1.0 / (jnp.sum(p, axis=-1, keepdims=True) + e_sink)
    return p * inv, e_sink * inv


def _spread_pair(v, hkv, half, tri):
    return jnp.concatenate([_spread(v[hkv, t, half].astype(BF16), tri) for t in range(2)], axis=0)


POOL_ROWS = PAD + HALO + BLK


def _window_sums(src_ref, tmp_refs, trailing):
    lo, hi = (PAD, POOL_ROWS) if trailing else (0, HALO + BLK)
    cur = src_ref
    for level in range(len(POOL_WINDOWS)):
        lanes = slice(level * 128, 512)
        shift = -(1 << level) if trailing else (1 << level)
        dst = tmp_refs[level % 2]
        dst[lo:hi, lanes] = cur[lo:hi, lanes] + cur[lo + shift:hi + shift, lanes]
        cur = dst


def _pool_block(ext_ref, tmp_refs, i, g, w):
    lanes = slice(g * 128, (g + 1) * 128)
    rows = slice(PAD + HALO, POOL_ROWS)
    t = (i * BLK + lax.broadcasted_iota(jnp.int32, (BLK, 1), 0)).astype(F32)
    inv = 1.0 / jnp.minimum(t + 1.0, float(w))
    return tmp_refs[g % 2][rows, lanes] * inv - ext_ref[rows, lanes], inv


def _fwd_mix(l, pu, pg, q, kv, ag, pool_w, pool_scale, sinks, tables):
    bias, tri = tables

    def body(pu_ref, pup_ref, pg_ref, q_ref, kv_ref, kvp_ref, ag_ref, pw_ref, sc_ref, sink_ref, bias_ref, tri_ref,
             cat_ref, ext_ref, *tmp_refs):
        i = pl.program_id(0)

        @pl.when(i == 0)
        def _():
            for ref in (ext_ref, *tmp_refs):
                ref[0:PAD, :] = jnp.zeros((PAD, 512), F32)

        ext_ref[PAD:PAD + HALO, :] = jnp.where(i > 0, pup_ref[...], 0.0)
        ext_ref[PAD + HALO:POOL_ROWS, :] = pu_ref[...]
        _window_sums(ext_ref, tmp_refs, True)
        for g, w in enumerate(POOL_WINDOWS):
            lanes = slice(g * 128, (g + 1) * 128)
            pooled, _ = _pool_block(ext_ref, tmp_refs, i, g, w)
            mixed = jnp.dot(pooled.astype(BF16), pw_ref[g], preferred_element_type=F32)
            gate = pg_ref[:, lanes]
            cat_ref[:, lanes] = (mixed * sc_ref[:, lanes] * (gate * _sigmoid(gate))).astype(BF16)

        own = _own_block_mask()
        tri = tri_ref[...]
        k_var = _head_variants(kv_ref[:, 0:128], kvp_ref[:, 0:128])
        v_var = _head_variants(kv_ref[:, 128:256], kvp_ref[:, 128:256])
        s = _scores([_stack_tiles(q_ref, hkv) for hkv in range(2)], k_var, own)
        p = {}
        for (hkv, t, half), s_head in s.items():
            head = _head_of(hkv, t, half)
            p[hkv, t, half], _ = _softmax(s_head, bias_ref[head], sink_ref[l, head])
        for hkv in range(2):
            o2 = jnp.zeros((2 * BLK, 128), F32)
            for half in range(2):
                o2 = o2 + jnp.dot(_spread_pair(p, hkv, half, tri), v_var[hkv][half], preferred_element_type=F32)
            for t in range(2):
                lo = (2 * hkv + t) * 128
                gate = ag_ref[:, lo:lo + 128]
                cat_ref[:, D_POOL + lo:D_POOL + lo + 128] = (_rows(o2, t) * (gate * _sigmoid(gate))).astype(BF16)

    blk = lambda w: pl.BlockSpec((BLK, w), lambda i: (i, 0))
    prev = lambda w: pl.BlockSpec((BLK, w), lambda i: (jnp.maximum(i - 1, 0), 0))
    halo = pl.BlockSpec((HALO, 512), lambda i: (jnp.maximum(i * (BLK // HALO) - 1, 0), 0))
    return pl.pallas_call(
        body, name="fwd_mix", grid=(NB,),
        in_specs=[blk(512), halo, blk(512), blk(512), blk(256), prev(256), blk(512),
                  _layer(l, 4, 128, 128), _layer(l, 1, 512), pl.BlockSpec(memory_space=pltpu.SMEM),
                  pl.BlockSpec((None, N_HEADS, BLK, BLK), lambda i: (jnp.minimum(i, 1), 0, 0, 0)), _whole((BLK, BLK))],
        out_specs=blk(D),
        out_shape=jax.ShapeDtypeStruct((S, D), BF16),
        scratch_shapes=[pltpu.VMEM((POOL_ROWS, 512), F32)] * 3,
        compiler_params=_params(),
    )(pu, pu, pg, q, kv, kv, ag, pool_w, pool_scale, sinks, bias, tri)


def _store_lane_rows(ref, acc):
    total = jnp.sum(acc, axis=0, keepdims=True)
    for k in range(ref.shape[0]):
        ref[k:k + 1, :] = total[:, k * 128:(k + 1) * 128]


def _own_piece(dw_ref, place_ref):
    p = dw_ref.shape[0] // 8
    return dw_ref[pl.ds(pl.multiple_of(place_ref[0] * p, 8), p), :]


def _bwd_out(l, cat, w_out, g_post, place_arr, dxn=None, y=None, x=None, target=None, deps=()):
    last = target is not None
    n_steps = S // TM

    def body(a_ref, b_ref, g_ref, cat_ref, w_ref, place_ref, *rest):
        dcat_ref, own_ref, dwb_ref, dg_ref = rest[len(deps):len(deps) + 4]
        rest = rest[len(deps) + 4:]
        acc_ref, dw_ref = rest[-2:]
        step = pl.program_id(0)

        @pl.when(step == 0)
        def _():
            dw_ref[...] = jnp.zeros_like(dw_ref)
            acc_ref[...] = jnp.zeros_like(acc_ref)

        cat = cat_ref[...]
        g = g_ref[...]
        y = jnp.dot(cat, w_ref[...], preferred_element_type=F32) if last else b_ref[...]
        r = lax.rsqrt(jnp.mean(y * y, axis=-1, keepdims=True) + EPS)
        if last:
            loss_ref, dx_ref, loss_acc_ref = rest[:3]
            err = a_ref[...] + y * r * g - b_ref[...]

            @pl.when(step == 0)
            def _():
                loss_acc_ref[...] = jnp.zeros_like(loss_acc_ref)

            loss_acc_ref[...] += _rows8(err * err)
            dz = err * (1.0 / D)
            dx_ref[...] = dz
        else:
            dz = a_ref[...]
        a = dz * g
        dy = r * a - y * (r * r * r) * jnp.mean(a * y, axis=-1, keepdims=True)
        acc_ref[...] += _rows8(dz * (y * r))
        dyb = dy.astype(BF16)
        dcat_ref[...] = lax.dot_general(dyb, w_ref[...], NT, preferred_element_type=F32)
        dw_ref[...] += lax.dot_general(cat, dyb, TN, preferred_element_type=F32)

        @pl.when(step == n_steps - 1)
        def _():
            _store_lane_rows(dg_ref, acc_ref[...])
            dwb_ref[...] = dw_ref[...].astype(BF16)
            own_ref[...] = _own_piece(dw_ref, place_ref)
            if last:
                loss_ref[...] = jnp.full((8, 128), (0.5 / D) * jnp.sum(loss_acc_ref[...]), F32)

    row = lambda: pl.BlockSpec((TM, D), lambda i: (i, 0))
    full = _whole
    return pl.pallas_call(
        body, name="out_loss_bwd" if last else "bwd_out", grid=(n_steps,),
        in_specs=[row(), row(), _layer(l, 1, D), row(), full((D, D)), pl.BlockSpec(memory_space=pltpu.SMEM)]
        + [ANY] * len(deps),
        out_specs=[row(), full((D // 8, D)), full((D, D)), full((8, 128))] + ([full((8, 128)), row()] if last else []),
        out_shape=[jax.ShapeDtypeStruct((S, D), F32), jax.ShapeDtypeStruct((D // 8, D), F32),
                   jax.ShapeDtypeStruct((D, D), BF16), jax.ShapeDtypeStruct((8, 128), F32)]
        + ([jax.ShapeDtypeStruct((8, 128), F32), jax.ShapeDtypeStruct((S, D), F32)] if last else []),
        scratch_shapes=([pltpu.VMEM((8, D), F32)] if last else []) + [pltpu.VMEM((8, D), F32), pltpu.VMEM((D, D), F32)],
        compiler_params=_params(),
    )(*((x, target) if last else (dxn, y)), g_post, cat, w_out, place_arr, *deps)


def _bwd_mix(l, pu, pg, q, kv, ag, dcat, pool_w, pool_scale, sinks, tables, deps=(), dpw_dest=None):
    bias, tri = tables
    deps = tuple(deps) + (() if dpw_dest is None else (dpw_dest,))

    def body(pu_ref, pup_ref, pg_ref, q_ref, kv_ref, kvp_ref, ag_ref, dcat_ref, pw_ref, sc_ref, sink_ref, bias_ref,
             tri_ref, *rest):
        dproj_ref, dpw_ref, dsc_ref, dsink_ref, ext_ref, dext_ref, tmp_a, tmp_b, dkv_ref = rest[len(deps):]
        tmp_refs = (tmp_a, tmp_b)
        step = pl.program_id(0)
        i = NB - 1 - step

        @pl.when(step == 0)
        def _():
            dpw_ref[...] = jnp.zeros_like(dpw_ref)
            dsc_ref[...] = jnp.zeros_like(dsc_ref)
            dsink_ref[...] = jnp.zeros_like(dsink_ref)
            for ref in (ext_ref, tmp_a, tmp_b):
                ref[0:PAD, :] = jnp.zeros((PAD, 512), F32)
            dext_ref[BLK:POOL_ROWS, :] = jnp.zeros((HALO + PAD, 512), F32)
            dkv_ref[...] = jnp.zeros_like(dkv_ref)

        ext_ref[PAD:PAD + HALO, :] = jnp.where(i > 0, pup_ref[...], 0.0)
        ext_ref[PAD + HALO:POOL_ROWS, :] = pu_ref[...]
        _window_sums(ext_ref, tmp_refs, True)
        dpooled = []
        for g, w in enumerate(POOL_WINDOWS):
            lanes = slice(g * 128, (g + 1) * 128)
            pooled, inv = _pool_block(ext_ref, tmp_refs, i, g, w)
            pooled_b = pooled.astype(BF16)
            mixed = jnp.dot(pooled_b, pw_ref[g], preferred_element_type=F32)
            scale = sc_ref[:, lanes]
            gate = pg_ref[:, lanes]
            sg = _sigmoid(gate)
            dpo = dcat_ref[:, lanes]
            dproj_ref[:, C_PG + g * 128:C_PG + (g + 1) * 128] = (
                dpo * (mixed * scale) * (sg * (1.0 + gate * (1.0 - sg)))).astype(BF16)
            dms = dpo * (gate * sg)
            dsc_ref[g:g + 1, :] += jnp.sum(dms * mixed, axis=0, keepdims=True)
            dmixed = (dms * scale).astype(BF16)
            dpw_ref[g] += lax.dot_general(pooled_b, dmixed, TN, preferred_element_type=F32)
            dpooled.append(lax.dot_general(dmixed, pw_ref[g], NT, preferred_element_type=F32))
            dext_ref[0:BLK, lanes] = dpooled[g] * inv
        _window_sums(dext_ref, tmp_refs, False)
        for g in range(len(POOL_WINDOWS)):
            lanes = slice(g * 128, (g + 1) * 128)
            dproj_ref[:, C_PU + g * 128:C_PU + (g + 1) * 128] = (tmp_refs[g % 2][0:BLK, lanes] - dpooled[g]).astype(BF16)
        dext_ref[BLK:BLK + HALO, :] = dext_ref[0:HALO, :]

        own = _own_block_mask()
        tri = tri_ref[...]
        k_var = _head_variants(kv_ref[:, 0:128], kvp_ref[:, 0:128])
        v_var = _head_variants(kv_ref[:, 128:256], kvp_ref[:, 128:256])
        q2 = [_stack_tiles(q_ref, hkv) for hkv in range(2)]
        s = _scores(q2, k_var, own)
        p, p_sink = {}, {}
        for key, s_head in s.items():
            head = _head_of(*key)
            p[key], p_sink[key] = _softmax(s_head, bias_ref[head], sink_ref[l, head])

        do2, p_b, dp = [], {}, {}
        for hkv in range(2):
            gate = _stack_tiles(ag_ref, hkv)
            sg = _sigmoid(gate)
            dca = _stack_tiles(dcat_ref, hkv, D_POOL)
            do2.append((dca * (gate * sg)).astype(BF16))
            o2 = jnp.zeros((2 * BLK, 128), F32)
            for half in range(2):
                p_b[hkv, half] = _spread_pair(p, hkv, half, tri)
                o2 = o2 + jnp.dot(p_b[hkv, half], v_var[hkv][half], preferred_element_type=F32)
                full = lax.dot_general(do2[hkv], v_var[hkv][half], NT, preferred_element_type=F32)
                for t in range(2):
                    dp[hkv, t, half] = _merge(_rows(full, t), own)
            dag = dca * o2 * (sg * (1.0 + gate * (1.0 - sg)))
            for t in range(2):
                lo = C_AG + (2 * hkv + t) * 128
                dproj_ref[:, lo:lo + 128] = _rows(dag, t).astype(BF16)

        ds = {}
        for key in p:
            delta = jnp.sum(p[key] * dp[key], axis=-1, keepdims=True)
            ds[key] = p[key] * (dp[key] - delta)
            head = _head_of(*key)
            dsink_ref[0:1, :] += jnp.where(lax.broadcasted_iota(jnp.int32, (1, 128), 1) == head,
                                           -jnp.sum(p_sink[key] * delta, axis=0, keepdims=True), 0.0)

        dk_acc = [[None, None], [None, None]]
        dv_acc = [[None, None], [None, None]]
        for hkv in range(2):
            dq2 = jnp.zeros((2 * BLK, 128), F32)
            for half in range(2):
                ds_b = _spread_pair(ds, hkv, half, tri)
                dq2 = dq2 + jnp.dot(ds_b, k_var[hkv][half], preferred_element_type=F32)
                dk_acc[hkv][half] = lax.dot_general(ds_b, q2[hkv], TN, preferred_element_type=F32)
                dv_acc[hkv][half] = lax.dot_general(p_b[hkv, half], do2[hkv], TN, preferred_element_type=F32)
            for t in range(2):
                lo = C_Q + (2 * hkv + t) * 128
                dproj_ref[:, lo:lo + 128] = (_rows(dq2, t) * 0.125).astype(BF16)

        low = lax.broadcasted_iota(jnp.int32, (2 * BLK, 128), 1) < 64

        def gather_heads(acc):
            return jnp.where(low, acc[0][0] + pltpu.roll(acc[0][1], 64, axis=1),
                             pltpu.roll(acc[1][0], 64, axis=1) + acc[1][1])

        dk = gather_heads(dk_acc) * 0.125
        dv = gather_heads(dv_acc)
        dproj_ref[:, C_K:C_V] = (dk[BLK:, :] + dkv_ref[:, 0:128]).astype(BF16)
        dproj_ref[:, C_V:C_AG] = (dv[BLK:, :] + dkv_ref[:, 128:256]).astype(BF16)
        dkv_ref[:, 0:128] = dk[:BLK, :]
        dkv_ref[:, 128:256] = dv[:BLK, :]

    rev = lambda w: pl.BlockSpec((BLK, w), lambda s: (NB - 1 - s, 0))
    prev = lambda w: pl.BlockSpec((BLK, w), lambda s: (jnp.maximum(NB - 2 - s, 0), 0))
    halo = pl.BlockSpec((HALO, 512), lambda s: (jnp.maximum((NB - 1 - s) * (BLK // HALO) - 1, 0), 0))
    return pl.pallas_call(
        body, name="bwd_mix", grid=(NB,),
        in_specs=[rev(512), halo, rev(512), rev(512), rev(256), prev(256), rev(512), rev(D),
                  _layer(l, 4, 128, 128), _layer(l, 1, 512), pl.BlockSpec(memory_space=pltpu.SMEM),
                  pl.BlockSpec((None, N_HEADS, BLK, BLK), lambda s: (jnp.minimum(NB - 1 - s, 1), 0, 0, 0)),
                  _whole((BLK, BLK))] + [ANY] * len(deps),
        out_specs=[rev(D_IN), _layer(l, 4, 128, 128),
                   pl.BlockSpec((4, 128), lambda s: (0, 0)), pl.BlockSpec((8, 128), lambda s: (0, 0))],
        out_shape=[jax.ShapeDtypeStruct((S, D_IN), BF16), jax.ShapeDtypeStruct((DEPTH, 4, 128, 128), F32),
                   jax.ShapeDtypeStruct((4, 128), F32), jax.ShapeDtypeStruct((8, 128), F32)],
        input_output_aliases={} if dpw_dest is None else {12 + len(deps): 1},
        scratch_shapes=[pltpu.VMEM((POOL_ROWS, 512), F32)] * 4 + [pltpu.VMEM((BLK, 256), F32)],
        compiler_params=_params(),
    )(pu, pu, pg, q, kv, kv, ag, dcat, pool_w, pool_scale, sinks, bias, tri, *deps)


def _bwd_in_dw(l, dproj, x, g_pre, place_arr, deps=()):
    n_steps = S // TM

    def body(dp_ref, x_ref, g_ref, place_ref, *rest):
        own_ref, dwb_ref, dw_ref = rest[len(deps):]
        step = pl.program_id(0)

        @pl.when(step == 0)
        def _():
            dw_ref[...] = jnp.zeros_like(dw_ref)

        xt = x_ref[...]
        r = lax.rsqrt(jnp.mean(xt * xt, axis=-1, keepdims=True) + EPS)
        h = (xt * r * g_ref[...]).astype(BF16)
        dw_ref[...] += lax.dot_general(dp_ref[...], h, TN, preferred_element_type=F32)

        @pl.when(step == n_steps - 1)
        def _():
            dwb_ref[...] = dw_ref[...].astype(BF16)
            own_ref[...] = _own_piece(dw_ref, place_ref)

    row = lambda w: pl.BlockSpec((TM, w), lambda i: (i, 0))
    full = _whole
    return pl.pallas_call(
        body, name="bwd_in_dw", grid=(n_steps,),
        in_specs=[row(D_IN), row(D), _layer(l, 1, D), pl.BlockSpec(memory_space=pltpu.SMEM)] + [ANY] * len(deps),
        out_specs=[full((D_IN // 8, D)), full((D_IN, D))],
        out_shape=[jax.ShapeDtypeStruct((D_IN // 8, D), F32), jax.ShapeDtypeStruct((D_IN, D), BF16)],
        scratch_shapes=[pltpu.VMEM((D_IN, D), F32)],
        compiler_params=_params(),
    )(dproj, x, g_pre, place_arr, *deps)


def _bwd_in_dx(l, dproj, w_in_t, x, g_pre, dres, deps=(), dw_place=None):
    n_steps = S // TM
    with_dw = dw_place is not None

    def body(dp_ref, w_ref, x_ref, g_ref, dres_ref, *rest):
        place_ref = rest[0] if with_dw else None
        rest = rest[with_dw + len(deps):]
        if with_dw:
            dx_ref, dg_ref, own_ref, dwb_ref, acc_ref, dw_ref = rest
        else:
            dx_ref, dg_ref, acc_ref = rest
        step = pl.program_id(0)

        @pl.when(step == 0)
        def _():
            acc_ref[...] = jnp.zeros_like(acc_ref)
            if with_dw:
                dw_ref[...] = jnp.zeros_like(dw_ref)

        g = g_ref[...]
        halves = [slice(k * (TM // 2), (k + 1) * (TM // 2)) for k in range(2)]
        dh = [jnp.dot(dp_ref[rows, :], w_ref[...], preferred_element_type=F32) for rows in halves]
        h = []
        for rows, dh_k in zip(halves, dh):
            xt = x_ref[rows, :]
            r = lax.rsqrt(jnp.mean(xt * xt, axis=-1, keepdims=True) + EPS)
            xn = xt * r
            acc_ref[...] += _rows8(dh_k * xn)
            a = dh_k * g
            dx_ref[rows, :] = dres_ref[rows, :] + (
                r * a - xt * (r * r * r) * jnp.mean(a * xt, axis=-1, keepdims=True))
            h.append((xn * g).astype(BF16))
        if with_dw:
            dw_ref[...] += lax.dot_general(dp_ref[...], jnp.concatenate(h, axis=0), TN, preferred_element_type=F32)

        @pl.when(step == n_steps - 1)
        def _():
            _store_lane_rows(dg_ref, acc_ref[...])
            if with_dw:
                dwb_ref[...] = dw_ref[...].astype(BF16)
                own_ref[...] = _own_piece(dw_ref, place_ref)

    row = lambda w: pl.BlockSpec((TM, w), lambda i: (i, 0))
    full = _whole
    dw_specs = [full((D_IN // 8, D)), full((D_IN, D))] if with_dw else []
    dw_shapes = [jax.ShapeDtypeStruct((D_IN // 8, D), F32), jax.ShapeDtypeStruct((D_IN, D), BF16)] if with_dw else []
    return pl.pallas_call(
        body, name="bwd_in" if with_dw else "bwd_in_dx", grid=(n_steps,),
        in_specs=[row(D_IN), full((D_IN, D)), row(D), _layer(l, 1, D), row(D)]
        + [pl.BlockSpec(memory_space=pltpu.SMEM)] * with_dw + [ANY] * len(deps),
        out_specs=[row(D), full((8, 128))] + dw_specs,
        out_shape=[jax.ShapeDtypeStruct((S, D), F32), jax.ShapeDtypeStruct((8, 128), F32)] + dw_shapes,
        scratch_shapes=[pltpu.VMEM((8, D), F32)] + [pltpu.VMEM((D_IN, D), F32)] * with_dw,
        compiler_params=_params(),
    )(dproj, w_in_t, x, g_pre, dres, *((dw_place,) if with_dw else ()), *deps)


HBM =pl.BlockSpec(memory_space=pltpu.HBM)
SEM = pl.BlockSpec(memory_space=pltpu.SEMAPHORE)
def _split_copy(collective_id=None):
    return pltpu.CompilerParams(has_side_effects=pltpu.SideEffectType.DATAFLOW_SIDE_EFFECTING,
                                collective_id=collective_id)


SPLIT_COPY = _split_copy()


def _in_hbm(a):
    return pltpu.with_memory_space_constraint(a, pltpu.HBM)

def _place():
    return lax.axis_index("x"), lax.axis_index("y"), lax.axis_index("c")


def _other_chips(x, y):
    return [(1 - x, y), (x, 1 - y), (1 - x, 1 - y)]


def _peer(x, y, c, m):
    return (x ^ (m >> 2), y ^ ((m >> 1) & 1), c ^ (m & 1))


SAME_CORE = (2, 4, 6)


def _place_cast(name, src, chip_arr, tile, layers, deps=()):
    _, n, cols = src.shape
    steps = n // tile
    k = len(layers)

    def body(chip_ref, *refs):
        for s_ref, o_ref in zip(refs[:k], refs[k + len(deps):]):
            o_ref[...] = s_ref[...].astype(BF16)

    def layer_spec(l):
        return pl.BlockSpec((None, tile, cols), lambda i, chip: (l, i, 0))

    return pl.pallas_call(
        body, name=name,
        grid_spec=pltpu.PrefetchScalarGridSpec(
            num_scalar_prefetch=1, grid=(steps,),
            in_specs=[layer_spec(l) for l in layers] + [ANY] * len(deps),
            out_specs=[pl.BlockSpec((tile, cols), lambda i, chip: (chip[0] * steps + i, 0))] * k),
        out_shape=[jax.ShapeDtypeStruct((N_SHARDS * n, cols), BF16)] * k,
        compiler_params=_params(),
    )(chip_arr, *[src] * k, *deps)


def _chip_rows(ref, chip, half=None):
    n = ref.shape[0] // N_SHARDS
    if half is None:
        return ref.at[pl.ds(pl.multiple_of(chip * n, 16), n), :]
    return ref.at[pl.ds(pl.multiple_of(chip * n + half * (n // 2), 16), n // 2), :]


def _gather_start(name, bufs, halved, collective_id):
    n = len(bufs)

    def body(*refs):
        ins, send, recv, token = refs[:n], refs[n:2 * n], refs[2 * n:3 * n], refs[-1]
        x, y, c = _place()
        _handshake([(*chip, c) for chip in _other_chips(x, y)])
        for a, buf in enumerate(ins):
            own = _chip_rows(buf, 2 * x + y, c if a in halved else None)
            for j, chip in enumerate(_other_chips(x, y)):
                pltpu.make_async_remote_copy(src_ref=own, dst_ref=own, send_sem=send[a].at[j], recv_sem=recv[a].at[j],
                                             device_id=(*chip, c), device_id_type=MESH).start()
        token[...] = jnp.zeros_like(token)

    outs = pl.pallas_call(
        body, name=name, in_specs=[HBM] * n,
        out_specs=[SEM] * (2 * n) + [HBM] * n + [pl.BlockSpec(memory_space=pltpu.VMEM)],
        out_shape=[pltpu.SemaphoreType.DMA((3,))] * (2 * n) + [pltpu.HBM(b.shape, b.dtype) for b in bufs]
        + [jax.ShapeDtypeStruct((8, 128), F32)],
        input_output_aliases={a: 2 * n + a for a in range(n)},
        compiler_params=_split_copy(collective_id),
    )(*[_in_hbm(b) for b in bufs])
    return outs[:n], outs[n:2 * n], outs[2 * n:3 * n], outs[-1]


def _gather_wait(name, buf, send_sem, recv_sem, after, halved=False):
    def body(buf_ref, send_ref, recv_ref, *rest):
        x, y, c = _place()
        half = c if halved else None
        own = _chip_rows(buf_ref, 2 * x + y, half)
        for j, chip in enumerate(_other_chips(x, y)):
            copy = pltpu.make_async_remote_copy(src_ref=own, dst_ref=_chip_rows(buf_ref, 2 * chip[0] + chip[1], half),
                                                send_sem=send_ref.at[j], recv_sem=recv_ref.at[j],
                                                device_id=(*chip, c), device_id_type=MESH)
            copy.wait_send()
            copy.wait_recv()

    return pl.pallas_call(
        body, name=name, in_specs=[HBM, SEM, SEM] + [ANY] * len(after), out_specs=HBM,
        out_shape=pltpu.HBM(buf.shape, buf.dtype), input_output_aliases={0: 0}, compiler_params=SPLIT_COPY,
    )(buf, send_sem, recv_sem, *after)


def _handshake(peers):
    barrier = pltpu.get_barrier_semaphore()
    for peer in peers:
        pl.semaphore_signal(barrier, inc=1, device_id=peer, device_id_type=MESH)
    pl.semaphore_wait(barrier, len(peers))


def _sibling_handshake(x, y, c):
    _handshake([(x, y, 1 - c)])


def _forward_halves(name, buf, collective_id):
    def body(in_ref, out_ref, send_sems, recv_sems):
        x, y, c = _place()
        _sibling_handshake(x, y, c)

        def copy(j, chip, half):
            rows = 2 * chip[0] + chip[1]
            return pltpu.make_async_remote_copy(
                src_ref=_chip_rows(in_ref, rows, half), dst_ref=_chip_rows(out_ref, rows, half), send_sem=send_sems.at[j],
                recv_sem=recv_sems.at[j], device_id=(x, y, 1 - c), device_id_type=MESH)

        chips = _other_chips(x, y)
        for j, chip in enumerate(chips):
            copy(j, chip, c).start()
        for j, chip in enumerate(chips):
            copy(j, chip, c).wait_send()
            copy(j, chip, 1 - c).wait_recv()

    return pl.pallas_call(
        body, name=name, in_specs=[ANY], out_specs=ANY, out_shape=jax.ShapeDtypeStruct(buf.shape, buf.dtype),
        input_output_aliases={0: 0},
        scratch_shapes=[pltpu.SemaphoreType.DMA((3,))] * 2,
        compiler_params=pltpu.CompilerParams(collective_id=collective_id),
    )(buf)


def _piece_rows(ref, k):
    p = ref.shape[0] // 8
    return ref.at[pl.ds(pl.multiple_of(k * p, 32 // jnp.dtype(ref.dtype).itemsize), p), :]


def _exchange_start(name, arrays, collective_id):
    n = len(arrays)
    zones = [lax.empty((7, a.shape[0] // 8, a.shape[1]), a.dtype) for a in arrays]

    def body(*refs):
        srcs, lands = refs[:n], refs[n:2 * n]
        send, recv, token = refs[2 * n:3 * n], refs[3 * n:4 * n], refs[-1]
        x, y, c = _place()
        _handshake([_peer(x, y, c, m) for m in range(1, 8)])
        for a, (src, land) in enumerate(zip(srcs, lands)):
            for m in range(1, 8):
                px, py, pc = _peer(x, y, c, m)
                pltpu.make_async_remote_copy(
                    src_ref=_piece_rows(src, 4 * px + 2 * py + pc), dst_ref=land.at[m - 1], send_sem=send[a].at[m - 1],
                    recv_sem=recv[a].at[m - 1], device_id=(px, py, pc), device_id_type=MESH).start()
        token[...] = jnp.zeros_like(token)

    outs = pl.pallas_call(
        body, name=name, in_specs=[HBM] * (2 * n),
        out_specs=[SEM] * (2 * n) + [HBM] * (2 * n) + [pl.BlockSpec(memory_space=pltpu.VMEM)],
        out_shape=[pltpu.SemaphoreType.DMA((7,))] * (2 * n) + [pltpu.HBM(a.shape, a.dtype) for a in arrays + zones]
        + [jax.ShapeDtypeStruct((8, 128), F32)],
        input_output_aliases={a: 2 * n + a for a in range(2 * n)},
        compiler_params=_split_copy(collective_id),
    )(*[_in_hbm(a) for a in arrays + zones])
    return outs[:n], outs[n:2 * n], outs[2 * n:3 * n], outs[3 * n:4 * n], outs[-1]


def _exchange_wait(name, started, after, which=None, with_sent=False):
    which = range(len(started[2])) if which is None else which
    send_sems, recv_sems, arrays, zones = [[group[k] for k in which] for group in started[:4]]
    n = len(arrays)

    def body(*refs):
        srcs, lands = refs[:n], refs[n:2 * n]
        send, recv = refs[2 * n:3 * n], refs[3 * n:4 * n]
        x, y, c = _place()
        for a, (src, land) in enumerate(zip(srcs, lands)):
            for m in range(1, 8):
                px, py, pc = _peer(x, y, c, m)
                copy = pltpu.make_async_remote_copy(
                    src_ref=_piece_rows(src, 4 * px + 2 * py + pc), dst_ref=land.at[m - 1], send_sem=send[a].at[m - 1],
                    recv_sem=recv[a].at[m - 1], device_id=(px, py, pc), device_id_type=MESH)
                copy.wait_send()
                copy.wait_recv()

    outs = pl.pallas_call(
        body, name=name, in_specs=[HBM] * (2 * n) + [SEM] * (2 * n) + [ANY], out_specs=[HBM] * (2 * n),
        out_shape=[pltpu.HBM(a.shape, a.dtype) for a in list(arrays) + list(zones)],
        input_output_aliases={a: a for a in range(2 * n)}, compiler_params=SPLIT_COPY,
    )(*arrays, *zones, *send_sems, *recv_sems, after)
    return outs if with_sent else outs[n:]


def _sum_pieces(name, weights, place_arr, dests=None):
    steps = 2
    flat = [item for items in weights for item in items]
    n = len(flat)

    def body(place_ref, *refs):
        outs = iter(refs[len(refs) - len(weights):])
        k = 0
        for items in weights:
            out_ref = next(outs)
            for layer, _, _ in items:
                total = refs[k][...]
                for m in range(7):
                    total = total + refs[n + k][m].astype(F32)
                if len(items) == DEPTH:
                    out_ref[layer] = total
                else:
                    out_ref[...] = total
                k += 1

    def out_spec(items):
        _, own, _ = items[0]
        t, cols = own.shape[0] // steps, own.shape[1]
        if len(items) == DEPTH:
            return pl.BlockSpec((DEPTH, t, cols), lambda i, place: (0, place[1] * steps + i, 0))
        layer = items[0][0]
        return pl.BlockSpec((None, t, cols), lambda i, place: (layer, place[1] * steps + i, 0))

    owns = [own for _, own, _ in flat]
    dests = [] if dests is None else list(dests)
    return pl.pallas_call(
        body, name=name,
        grid_spec=pltpu.PrefetchScalarGridSpec(
            num_scalar_prefetch=1, grid=(steps,),
            in_specs=[pl.BlockSpec((o.shape[0] // steps, o.shape[1]), lambda i, place: (i, 0)) for o in owns]
            + [pl.BlockSpec((7, o.shape[0] // steps, o.shape[1]), lambda i, place: (0, i, 0)) for o in owns]
            + [ANY] * len(dests),
            out_specs=[out_spec(items) for items in weights]),
        out_shape=[jax.ShapeDtypeStruct((DEPTH, 2 * items[0][1].shape[0], items[0][1].shape[1]), F32)
                   for items in weights],
        input_output_aliases={1 + 2 * n + k: k for k in range(len(dests))},
        compiler_params=_params(),
    )(place_arr, *owns, *[recv for _, _, recv in flat], *dests)


def _sum_small(name, partials, recvs, place_arr):
    n = len(partials)

    def body(place_ref, *refs):
        for o_ref, r_ref, out_ref in zip(refs[:n], refs[n:2 * n], refs[2 * n:]):
            total = o_ref[...]
            for m in range(7):
                total = total + r_ref[m]
            out_ref[...] = total

    piece = lambda a: pl.BlockSpec((a.shape[0] // 8, a.shape[1]), lambda i, place: (place[0], 0))
    return pl.pallas_call(
        body, name=name,
        grid_spec=pltpu.PrefetchScalarGridSpec(
            num_scalar_prefetch=1, grid=(1,),
            in_specs=[piece(a) for a in partials] + [pl.BlockSpec(r.shape, lambda i, place: (0, 0, 0)) for r in recvs],
            out_specs=[piece(a) for a in partials]),
        out_shape=[jax.ShapeDtypeStruct(a.shape, F32) for a in partials],
        compiler_params=_params(),
    )(place_arr, *partials, *recvs)


def _share(name, bufs, parts, gathered=(), collective_id=None):
    n, n_g = len(bufs), len(gathered)
    total = n + n_g

    def body(*refs):
        ins, outs = refs[:total], refs[total:2 * total]
        send_sems, recv_sems, send_g, recv_g = refs[2 * total:]
        x, y, c = _place()
        _handshake([_peer(x, y, c, m) for m in (SAME_CORE if gathered else ()) + (1,)])

        def half(ref, l, which):
            p = ref.shape[1] // 2
            return ref.at[l, pl.ds(pl.multiple_of(which * p, 8), p), :]

        def swap(k, which):
            a, l = parts[k]
            return pltpu.make_async_remote_copy(
                src_ref=half(ins[a], l, which), dst_ref=half(outs[a], l, which), send_sem=send_sems.at[k],
                recv_sem=recv_sems.at[k], device_id=(x, y, 1 - c), device_id_type=MESH)

        def spread(a, m, sender, held, to):
            k = 4 * sender[0] + 2 * sender[1] + sender[2]
            return pltpu.make_async_remote_copy(
                src_ref=_piece_rows(held[n + a], k), dst_ref=_piece_rows(outs[n + a], k),
                send_sem=send_g.at[7 * a + m - 1], recv_sem=recv_g.at[7 * a + m - 1], device_id=to, device_id_type=MESH)

        me, sibling = (x, y, c), (x, y, 1 - c)
        for k in range(len(parts)):
            swap(k, c).start()
        def own(a, m):
            return spread(a, m, me, ins, _peer(x, y, c, m))

        def handed_on(a, m):
            return spread(a, m + 1, _peer(x, y, c, m), outs, sibling)

        for a in range(n_g):
            for m in SAME_CORE + (1,):
                own(a, m).start()
        for a in range(n_g):
            for m in SAME_CORE:
                spread(a, m, _peer(x, y, c, m), ins, _peer(x, y, c, m)).wait_recv()
                handed_on(a, m).start()
        for k in range(len(parts)):
            swap(k, c).wait_send()
            swap(k, 1 - c).wait_recv()
        for a in range(n_g):
            for m in SAME_CORE + (1,):
                own(a, m).wait_send()
            for m in SAME_CORE:
                handed_on(a, m).wait_send()
                spread(a, m + 1, _peer(x, y, c, m + 1), ins, sibling).wait_recv()
            spread(a, 1, sibling, ins, sibling).wait_recv()

    arrays = list(bufs) + list(gathered)
    return pl.pallas_call(
        body, name=name, in_specs=[ANY] * total, out_specs=[ANY] * total,
        out_shape=[jax.ShapeDtypeStruct(b.shape, F32) for b in arrays],
        input_output_aliases={a: a for a in range(total)},
        scratch_shapes=[pltpu.SemaphoreType.DMA((max(len(parts), 1),))] * 2
        + [pltpu.SemaphoreType.DMA((max(7 * n_g, 1),))] * 2,
        compiler_params=pltpu.CompilerParams(collective_id=collective_id),
    )(*arrays)


def _adamw_math(w, g, m, v):
    nm = ADAM_B1 * m + (1.0 - ADAM_B1) * g
    nv = ADAM_B2 * v + (1.0 - ADAM_B2) * (g * g)
    m_hat = nm / (1.0 - ADAM_B1 ** ADAM_STEP)
    v_hat = nv / (1.0 - ADAM_B2 ** ADAM_STEP)
    return -ADAM_LR * (m_hat / (jnp.sqrt(v_hat) + ADAM_EPS) + ADAM_WD * w), nm, nv


def _adamw(name, w, g, m, v, rows_per_step, first=0, count=None, dests=None, deps=()):
    layers, rows, cols = w.shape
    count = layers if count is None else count

    def body(w_ref, g_ref, m_ref, v_ref, *rest):
        d_ref, nm_ref, nv_ref, g_out_ref = rest[-4:]
        d_ref[...], nm_ref[...], nv_ref[...] = _adamw_math(w_ref[...], g_ref[...], m_ref[...], v_ref[...])
        g_out_ref[...] = g_ref[...]

    spec = pl.BlockSpec((1, rows_per_step, cols), lambda l, i: (first + l, i, 0))
    shape = jax.ShapeDtypeStruct(w.shape, F32)
    dests = () if dests is None else tuple(dests)
    return pl.pallas_call(
        body, name=name, grid=(count, rows // rows_per_step),
        in_specs=[spec] * 4 + [ANY] * (len(dests) + len(deps)), out_specs=[spec] * 4, out_shape=[shape] * 4,
        input_output_aliases={4 + k: k for k in range(len(dests))},
        compiler_params=_params(("arbitrary", "arbitrary")),
    )(w, g, m, v, *dests, *deps)


def _sum_adamw(name, layer, w, own, recv, m, v, place_arr, dests):
    steps = 2
    p, cols = own.shape
    tile = p // steps

    def body(place_ref, w_ref, own_ref, recv_ref, m_ref, v_ref, *rest):
        d_ref, nm_ref, nv_ref, g_ref = rest[-4:]
        total = own_ref[...]
        for k in range(7):
            total = total + recv_ref[k].astype(F32)
        g_ref[...] = total
        d_ref[...], nm_ref[...], nv_ref[...] = _adamw_math(w_ref[...], total, m_ref[...], v_ref[...])

    rows = pl.BlockSpec((None, tile, cols), lambda i, place: (layer, place[1] * steps + i, 0))
    return pl.pallas_call(
        body, name=name,
        grid_spec=pltpu.PrefetchScalarGridSpec(
            num_scalar_prefetch=1, grid=(steps,),
            in_specs=[rows, pl.BlockSpec((tile, cols), lambda i, place: (i, 0)),
                      pl.BlockSpec((7, tile, cols), lambda i, place: (0, i, 0)), rows, rows] + [ANY] * 4,
            out_specs=[rows] * 4),
        out_shape=[jax.ShapeDtypeStruct(w.shape, F32)] * 4,
        input_output_aliases={6 + k: k for k in range(4)},
        compiler_params=_params(),
    )(place_arr, w, own, recv, m, v, *dests)


def _pack_misc(pool_scale, sinks, norm_pre, norm_post):
    sink_rows = jnp.zeros((DEPTH, 8, 128), F32).at[:, 0, 0:N_HEADS].set(sinks).reshape(2 * 8, 128)
    return jnp.concatenate([pool_scale.reshape(8, 128), norm_pre.reshape(16, 128), norm_post.reshape(16, 128),
                            sink_rows, jnp.zeros((8, 128), F32)], axis=0)


def _adamw_small(w, g, m, v, pool):
    def body(w_ref, g_ref, m_ref, v_ref, pw_ref, pg_ref, pm_ref, pv_ref, *rest):
        outs, pool_outs, (d_ref, nm_ref, nv_ref) = rest[:17], rest[17:21], rest[21:]
        pool_outs[0][...] = pg_ref[...]
        pool_outs[1][...], pool_outs[2][...], pool_outs[3][...] = _adamw_math(
            pw_ref[...], pg_ref[...], pm_ref[...], pv_ref[...])
        d_ref[...], nm_ref[...], nv_ref[...] = _adamw_math(w_ref[...], g_ref[...], m_ref[...], v_ref[...])
        for k, src in enumerate([g_ref, d_ref, nm_ref, nv_ref]):
            scale, sinks, pre, post = outs[4 * k:4 * k + 4]
            for l in range(DEPTH):
                for j in range(4):
                    scale[l:l + 1, j * 128:(j + 1) * 128] = src[MISC_SCALE + 4 * l + j:MISC_SCALE + 4 * l + j + 1, :]
                for j in range(8):
                    pre[l:l + 1, j * 128:(j + 1) * 128] = src[MISC_PRE + 8 * l + j:MISC_PRE + 8 * l + j + 1, :]
                    post[l:l + 1, j * 128:(j + 1) * 128] = src[MISC_POST + 8 * l + j:MISC_POST + 8 * l + j + 1, :]
                sinks[l:l + 1, :] = src[MISC_SINKS + 8 * l:MISC_SINKS + 8 * l + 1, 0:N_HEADS]
        outs[16][...] = g_ref[MISC_LOSS:MISC_LOSS + 1, 0:1]

    vmem = pl.BlockSpec(memory_space=pltpu.VMEM)
    shapes = [(DEPTH, D_POOL), (DEPTH, N_HEADS), (DEPTH, D), (DEPTH, D)] * 4 + [(1, 1)]
    shapes += [pool[0].shape] * 4
    return pl.pallas_call(
        body, name="adamw_small", in_specs=[vmem] * 8, out_specs=[vmem] * 21,
        out_shape=[jax.ShapeDtypeStruct(s, F32) for s in shapes],
        scratch_shapes=[pltpu.VMEM((MISC_ROWS, 128), F32)] * 3,
    )(w, g, m, v, *pool)


def kernel(x, w_in, pool_w, pool_scale, attn_sinks, w_out, norm_pre, norm_post, loss_target, m_w_in, m_pool_w, m_pool_scale, m_attn_sinks, m_w_out, m_norm_pre, m_norm_post, v_w_in, v_pool_w, v_pool_scale, v_attn_sinks, v_w_out, v_norm_pre, v_norm_post):
    cx, cy, cc = _place()
    chip_arr = jnp.reshape(2 * cx + cy, (1,)).astype(jnp.int32)
    place_arr = jnp.stack([4 * cx + 2 * cy + cc, cc]).astype(jnp.int32)
    t = lambda a: jnp.transpose(a, (0, 2, 1))
    w_in_t = t(w_in)
    xs, target = x[0], loss_target[0]
    pool_w_b = pool_w.astype(BF16)
    tables = _attention_tables()
    scale3 = pool_scale.reshape(DEPTH, 1, D_POOL)
    pre3 = norm_pre.reshape(DEPTH, 1, D)
    post3 = norm_post.reshape(DEPTH, 1, D)

    (wi0,) = _place_cast("place_w_in0", w_in_t, chip_arr, 288, [0])
    first = _gather_start("gather_start_first", [wi0], halved=(0,), collective_id=ID_GATHER_FIRST)
    (wi1,) = _place_cast("place_w_in1", w_in_t, chip_arr, 288, [1], deps=(first[3],))
    wo = _place_cast("place_w_out", w_out, chip_arr, 256, [0, 1], deps=(first[3],))
    rest = _gather_start("gather_start_rest", [wi1, wo[0], wo[1]], halved=(0,), collective_id=ID_GATHER_REST)
    send, recv, bufs = [first[k] + rest[k] for k in range(3)]
    order = {(0, "in"): 0, (1, "in"): 1, (0, "out"): 2, (1, "out"): 3}

    saved = []
    packed = [_pack_misc(pool_scale, attn_sinks, norm_pre, norm_post),
              _pack_misc(m_pool_scale, m_attn_sinks, m_norm_pre, m_norm_post),
              _pack_misc(v_pool_scale, v_attn_sinks, v_norm_pre, v_norm_post)]
    after = (first[3], rest[3], pool_w_b, *tables, scale3, pre3, post3, *packed)
    below = None
    for l in range(DEPTH):
        k = order[l, "in"]
        w_in_l = _forward_halves(f"forward_w_in{l}", _gather_wait(f"gather_wait_in{l}", bufs[k], send[k], recv[k], after,
                                                                 halved=True), collective_id=ID_FORWARD[l])
        if below is None:
            pu, pg, q, kv, ag = _fwd_in(l, xs, pre3, w_in_l)
        else:
            y, xs, pu, pg, q, kv, ag = _fwd_in(l, xs, pre3, w_in_l, below)
            saved[l - 1][7] = y
        cat = _fwd_mix(l, pu, pg, q, kv, ag, pool_w_b, scale3, attn_sinks, tables)
        k = order[l, "out"]
        w_out_l = _gather_wait(f"gather_wait_out{l}", bufs[k], send[k], recv[k], (cat,))
        saved.append([xs, pu, pg, q, kv, ag, cat, None, w_in_l, w_out_l])
        below, after = (cat, w_out_l, post3), (w_out_l,)

    x_in, pu, pg, q, kv, ag, cat, y, w_in_l, w_out_l = saved[1]
    dcat, dw_out1, dw_out1_b, dg_post1, loss, xs = _bwd_out(1, cat, w_out_l, post3, place_arr, x=x_in, target=target)
    ex1_out = _exchange_start("exchange_start_out1", [dw_out1_b], ID_OUT1)
    dproj, dpw, dsc1, dsink1 = _bwd_mix(1, pu, pg, q, kv, ag, dcat, pool_w_b, scale3, attn_sinks, tables,
                                        deps=(ex1_out[4],))
    dx, dg_pre1, dw_in1, dw_in1_b = _bwd_in_dx(1, dproj, w_in_l, x_in, pre3, xs, dw_place=place_arr)
    ex1_in = _exchange_start("exchange_start_in1", [dw_in1_b], ID_IN1)

    x_in, pu, pg, q, kv, ag, cat, y, w_in_l, w_out_l = saved[0]
    dcat, dw_out0, dw_out0_b, dg_post0 = _bwd_out(0, cat, w_out_l, post3, place_arr, dxn=dx, y=y, deps=(ex1_in[4],))
    ex0_out = _exchange_start("exchange_start_out0", [dw_out0_b], ID_OUT0)
    dproj, dpw, dsc0, dsink0 = _bwd_mix(0, pu, pg, q, kv, ag, dcat, pool_w_b, scale3, attn_sinks, tables,
                                        deps=(ex0_out[4],), dpw_dest=dpw)
    dw_in0, dw_in0_b = _bwd_in_dw(0, dproj, x_in, pre3, place_arr)
    flat = lambda a: a.reshape(DEPTH * 4 * 128, 128)
    ex0_in = _exchange_start("exchange_start_in0", [flat(dpw), dw_in0_b], ID_IN0)

    grad_x, dg_pre0 = _bwd_in_dx(0, dproj, w_in_l, x_in, pre3, dx, deps=(ex0_in[4],))
    small = [jnp.concatenate([dsc0, dsc1, dg_pre0, dg_pre1, dg_post0, dg_post1, dsink0, dsink1, loss], axis=0)]
    ex_small = _exchange_start("exchange_start_small", small, ID_SMALL)
    (recv_out1,) = _exchange_wait("exchange_wait_out1", ex1_out, ex_small[4])
    (recv_in1,) = _exchange_wait("exchange_wait_in1", ex1_in, recv_out1)
    g_in, g_out = _sum_pieces("sum_pieces_1", [[(1, dw_in1, recv_in1)], [(1, dw_out1, recv_out1)]], place_arr)
    (recv_out0,) = _exchange_wait("exchange_wait_out0", ex0_out, g_out)
    (g_out,) = _sum_pieces("sum_pieces_out0", [[(0, dw_out0, recv_out0)]], place_arr, dests=[g_out])
    g_in, g_out = _share("share_a", [g_in, g_out], [(0, 1), (1, 0), (1, 1)], collective_id=ID_SHARE_A)
    m_in_t, v_in_t = t(m_w_in), t(v_w_in)
    d_out, nm_out, nv_out, grad_w_out = _adamw("adamw_w_out", w_out, g_out, m_w_out, v_w_out, 256)
    upd_in = _adamw("adamw_w_in1", w_in_t, g_in, m_in_t, v_in_t, 288, first=1, count=1, deps=(d_out,))
    dpw_own, recv_pw = _exchange_wait("exchange_wait_pool", ex0_in, upd_in[0], which=[0], with_sent=True)
    (g_pw,) = _sum_small("sum_pool", [dpw_own], [recv_pw], place_arr)

    (recv_in0,) = _exchange_wait("exchange_wait_in0", ex0_in, g_pw, which=[1])
    recv_misc = _exchange_wait("exchange_wait_small", ex_small, recv_in0)
    (g_misc,) = _sum_small("sum_small", small, recv_misc, place_arr)
    upd_in = _sum_adamw("sum_adamw_w_in0", 0, w_in_t, dw_in0, recv_in0, m_in_t, v_in_t, place_arr, upd_in)
    d_in, nm_in, nv_in, grad_w_in_t, g_pw, g_misc = _share(
        "share_b", upd_in, [(k, 0) for k in range(4)], [g_pw, g_misc], collective_id=ID_SHARE_B)
    small_out = _adamw_small(packed[0], g_misc, packed[1], packed[2],
                             (flat(pool_w), g_pw, flat(m_pool_w), flat(v_pool_w)))
    (g_sc, g_sk, g_pre, g_post, d_sc, d_sk, d_pre, d_post,
     m_sc, m_sk, m_pre, m_post, v_sc, v_sk, v_pre, v_post, loss_sum) = small_out[:17]
    g_pw, d_pw, m_pw, v_pw = [a.reshape(pool_w.shape) for a in small_out[17:]]
    return (loss_sum[0, 0], grad_x[None], t(grad_w_in_t), g_pw, g_sc, g_sk, grad_w_out, g_pre, g_post,
            t(d_in), d_pw, d_sc, d_sk, d_out, d_pre, d_post,
            t(nm_in), m_pw, m_sc, m_sk, nm_out, m_pre, m_post,
            t(nv_in), v_pw, v_sc, v_sk, nv_out, v_pre, v_post)
```

```python
import jax
import jax.numpy as jnp
from jax import lax
from jax.experimental import pallas as pl
from jax.experimental.pallas import tpu as pltpu

F32 = jnp.float32
BF16 = jnp.bfloat16

S = 2048
D = 1024
DEPTH = 2
D_POOL = 512
POOL_WINDOWS = (2, 4, 8, 16)
N_HEADS = 8
D_IN = 2304
N_SHARDS = 4
W_IN_SHARD = D_IN // N_SHARDS
W_OUT_SHARD = D // N_SHARDS
BLK = 128
NB = S // BLK
HALO = 16
PAD = 8
EPS = 1e-6
NEG_INF = -1e30
C_PU, C_PG, C_Q, C_K, C_V, C_AG = 0, 512, 1024, 1536, 1664, 1792

ADAM_LR = 0.001
ADAM_B1 = 0.9
ADAM_B2 = 0.999
ADAM_EPS = 1e-08
ADAM_WD = 0.01
ADAM_STEP = 10

TM = 512
VMEM_LIMIT = 56 * 1024 * 1024

NT = (((1,), (1,)), ((), ()))
TN = (((0,), (0,)), ((), ()))

MESH = pl.DeviceIdType.MESH
ANY = pl.BlockSpec(memory_space=pl.ANY)

ID_FORWARD = (0, 1)
(ID_SHARE_A, ID_SHARE_B, ID_GATHER_FIRST, ID_GATHER_REST, ID_OUT1, ID_IN1, ID_OUT0, ID_IN0, ID_SMALL) = range(2, 11)

MISC_SCALE, MISC_PRE, MISC_POST, MISC_SINKS, MISC_LOSS = 0, 8, 24, 40, 56
MISC_ROWS = 64


def _params(sem=("arbitrary",)):
    return pltpu.CompilerParams(dimension_semantics=sem, vmem_limit_bytes=VMEM_LIMIT)


def _sigmoid(v):
    return 1.0 / (1.0 + jnp.exp(-v))


def _rows8(v):
    r, c = v.shape
    return v.reshape(r // 8, 8, c).sum(axis=0)


def _layer(l, *shape):
    zeros = (0,) * len(shape)
    return pl.BlockSpec((None,) + shape, lambda i: (l,) + zeros)


def _whole(shape):
    zeros = (0,) * len(shape)
    return pl.BlockSpec(shape, lambda i: zeros, pipeline_mode=pl.Buffered(1))


def _fwd_in(l, x, g_pre, w_in_t, below=None):
    fused = below is not None

    def body(x_ref, g_ref, w_ref, *rest):
        if fused:
            cat_ref, wo_ref, gp_ref, y_ref, xn_ref = rest[:5]
            y = jnp.dot(cat_ref[...], wo_ref[...], preferred_element_type=F32)
            y_ref[...] = y
            xt = x_ref[...] + y * lax.rsqrt(jnp.mean(y * y, axis=-1, keepdims=True) + EPS) * gp_ref[...]
            xn_ref[...] = xt
        else:
            xt = x_ref[...]
        pu_ref, pg_ref, q_ref, kv_ref, ag_ref = rest[-5:]
        r = lax.rsqrt(jnp.mean(xt * xt, axis=-1, keepdims=True) + EPS)
        h = (xt * r * g_ref[...]).astype(BF16)

        def proj(lo, hi):
            return lax.dot_general(h, w_ref[lo:hi, :], NT, preferred_element_type=F32)

        pu_ref[...] = proj(C_PU, C_PG)
        pg_ref[...] = proj(C_PG, C_Q)
        q_ref[...] = proj(C_Q, C_K).astype(BF16)
        kv_ref[...] = proj(C_K, C_AG).astype(BF16)
        ag_ref[...] = proj(C_AG, D_IN)

    row = lambda w: pl.BlockSpec((TM, w), lambda i: (i, 0))
    act = jax.ShapeDtypeStruct((S, D), F32)
    return pl.pallas_call(
        body, name="fwd_out_in" if fused else "fwd_in", grid=(S // TM,),
        in_specs=[row(D), _layer(l, 1, D), _whole((D_IN, D))]
        + ([row(D), _whole((D, D)), _layer(l - 1, 1, D)] if fused else []),
        out_specs=[row(D)] * (2 * fused) + [row(512), row(512), row(512), row(256), row(512)],
        out_shape=[act] * (2 * fused)
        + [jax.ShapeDtypeStruct((S, 512), F32), jax.ShapeDtypeStruct((S, 512), F32),
           jax.ShapeDtypeStruct((S, 512), BF16), jax.ShapeDtypeStruct((S, 256), BF16),
           jax.ShapeDtypeStruct((S, 512), F32)],
        compiler_params=_params(),
    )(x, g_pre, w_in_t, *(below if fused else ()))


LOG2E = 1.4426950408889634
SCORE_SCALE = 0.125 * LOG2E


def _attention_tables():
    qi = jnp.arange(BLK)[:, None]
    kj = jnp.arange(BLK)[None, :]
    dist = ((qi - kj) % BLK).astype(F32)
    slopes = jnp.exp2(-jnp.arange(1, N_HEADS + 1, dtype=F32))
    bias = -(slopes * LOG2E)[:, None, None] * dist[None]
    first = jnp.where(kj > qi, NEG_INF, bias)
    return jnp.stack([first, bias]), (kj <= qi).astype(BF16)


def _own_block_mask():
    return lax.broadcasted_iota(jnp.int32, (BLK, BLK), 1) <= lax.broadcasted_iota(jnp.int32, (BLK, BLK), 0)


def _merge(full, own):
    return jnp.where(own, full[:, BLK:], full[:, :BLK])


def _spread(v, tri):
    own = v * tri
    return jnp.concatenate([v - own, own], axis=1)


def _head_variants(cur, prev):
    both = jnp.concatenate([prev, cur], axis=0).astype(F32)
    swapped = pltpu.roll(both, 64, axis=1)
    low = lax.broadcasted_iota(jnp.int32, both.shape, 1) < 64
    zero = jnp.zeros_like(both)
    return ((jnp.where(low, both, zero).astype(BF16), jnp.where(low, zero, swapped).astype(BF16)),
            (jnp.where(low, swapped, zero).astype(BF16), jnp.where(low, zero, both).astype(BF16)))


def _head_of(hkv, t, half):
    return hkv * 4 + 2 * t + half


def _rows(v, t):
    return v[t * BLK:(t + 1) * BLK]


def _stack_tiles(ref, hkv, offset=0):
    lo = offset + 2 * hkv * 128
    return jnp.concatenate([ref[:, lo:lo + 128], ref[:, lo + 128:lo + 256]], axis=0)


def _scores(q2, k_var, own):
    s = {}
    for hkv in range(2):
        for half in range(2):
            full = lax.dot_general(q2[hkv], k_var[hkv][half], NT, preferred_element_type=F32)
            for t in range(2):
                s[hkv, t, half] = _merge(_rows(full, t), own)
    return s


def _softmax(s, bias, sink):
    s = s * SCORE_SCALE + bias
    sink2 = sink * LOG2E
    m = jnp.maximum(jnp.max(s, axis=-1, keepdims=True), sink2)
    p = jnp.exp2(s - m)
    e_sink = jnp.exp2(sink2 - m)
    inv = 1.0 / (jnp.sum(p, axis=-1, keepdims=True) + e_sink)
    return p * inv, e_sink * inv


def _spread_pair(v, hkv, half, tri):
    return jnp.concatenate([_spread(v[hkv, t, half].astype(BF16), tri) for t in range(2)], axis=0)


POOL_ROWS = PAD + HALO + BLK


def _window_sums(src_ref, tmp_refs, trailing):
    lo, hi = (PAD, POOL_ROWS) if trailing else (0, HALO + BLK)
    cur = src_ref
    for level in range(len(POOL_WINDOWS)):
        lanes = slice(level * 128, 512)
        shift = -(1 << level) if trailing else (1 << level)
        dst = tmp_refs[level % 2]
        dst[lo:hi, lanes] = cur[lo:hi, lanes] + cur[lo + shift:hi + shift, lanes]
        cur = dst


def _pool_block(ext_ref, tmp_refs, i, g, w):
    lanes = slice(g * 128, (g + 1) * 128)
    rows = slice(PAD + HALO, POOL_ROWS)
    t = (i * BLK + lax.broadcasted_iota(jnp.int32, (BLK, 1), 0)).astype(F32)
    inv = 1.0 / jnp.minimum(t + 1.0, float(w))
    return tmp_refs[g % 2][rows, lanes] * inv - ext_ref[rows, lanes], inv


def _fwd_mix(l, pu, pg, q, kv, ag, pool_w, pool_scale, sinks, tables):
    bias, tri = tables

    def body(pu_ref, pup_ref, pg_ref, q_ref, kv_ref, kvp_ref, ag_ref, pw_ref, sc_ref, sink_ref, bias_ref, tri_ref,
             cat_ref, ext_ref, *tmp_refs):
        i = pl.program_id(0)

        @pl.when(i == 0)
        def _():
            for ref in (ext_ref, *tmp_refs):
                ref[0:PAD, :] = jnp.zeros((PAD, 512), F32)

        ext_ref[PAD:PAD + HALO, :] = jnp.where(i > 0, pup_ref[...], 0.0)
        ext_ref[PAD + HALO:POOL_ROWS, :] = pu_ref[...]
        _window_sums(ext_ref, tmp_refs, True)
        for g, w in enumerate(POOL_WINDOWS):
            lanes = slice(g * 128, (g + 1) * 128)
            pooled, _ = _pool_block(ext_ref, tmp_refs, i, g, w)
            mixed = jnp.dot(pooled.astype(BF16), pw_ref[g], preferred_element_type=F32)
            gate = pg_ref[:, lanes]
            cat_ref[:, lanes] = (mixed * sc_ref[:, lanes] * (gate * _sigmoid(gate))).astype(BF16)

        own = _own_block_mask()
        tri = tri_ref[...]
        k_var = _head_variants(kv_ref[:, 0:128], kvp_ref[:, 0:128])
        v_var = _head_variants(kv_ref[:, 128:256], kvp_ref[:, 128:256])
        s = _scores([_stack_tiles(q_ref, hkv) for hkv in range(2)], k_var, own)
        p = {}
        for (hkv, t, half), s_head in s.items():
            head = _head_of(hkv, t, half)
            p[hkv, t, half], _ = _softmax(s_head, bias_ref[head], sink_ref[l, head])
        for hkv in range(2):
            o2 = jnp.zeros((2 * BLK, 128), F32)
            for half in range(2):
                o2 = o2 + jnp.dot(_spread_pair(p, hkv, half, tri), v_var[hkv][half], preferred_element_type=F32)
            for t in range(2):
                lo = (2 * hkv + t) * 128
                gate = ag_ref[:, lo:lo + 128]
                cat_ref[:, D_POOL + lo:D_POOL + lo + 128] = (_rows(o2, t) * (gate * _sigmoid(gate))).astype(BF16)

    blk = lambda w: pl.BlockSpec((BLK, w), lambda i: (i, 0))
    prev = lambda w: pl.BlockSpec((BLK, w), lambda i: (jnp.maximum(i - 1, 0), 0))
    halo = pl.BlockSpec((HALO, 512), lambda i: (jnp.maximum(i * (BLK // HALO) - 1, 0), 0))
    return pl.pallas_call(
        body, name="fwd_mix", grid=(NB,),
        in_specs=[blk(512), halo, blk(512), blk(512), blk(256), prev(256), blk(512),
                  _layer(l, 4, 128, 128), _layer(l, 1, 512), pl.BlockSpec(memory_space=pltpu.SMEM),
                  pl.BlockSpec((None, N_HEADS, BLK, BLK), lambda i: (jnp.minimum(i, 1), 0, 0, 0)), _whole((BLK, BLK))],
        out_specs=blk(D),
        out_shape=jax.ShapeDtypeStruct((S, D), BF16),
        scratch_shapes=[pltpu.VMEM((POOL_ROWS, 512), F32)] * 3,
        compiler_params=_params(),
    )(pu, pu, pg, q, kv, kv, ag, pool_w, pool_scale, sinks, bias, tri)


def _store_lane_rows(ref, acc):
    total = jnp.sum(acc, axis=0, keepdims=True)
    for k in range(ref.shape[0]):
        ref[k:k + 1, :] = total[:, k * 128:(k + 1) * 128]


def _own_piece(dw_ref, place_ref):
    p = dw_ref.shape[0] // 8
    return dw_ref[pl.ds(pl.multiple_of(place_ref[0] * p, 8), p), :]


def _bwd_out(l, cat, w_out, g_post, place_arr, dxn=None, y=None, x=None, target=None, deps=()):
    last = target is not None
    n_steps = S // TM

    def body(a_ref, b_ref, g_ref, cat_ref, w_ref, place_ref, *rest):
        dcat_ref, own_ref, dwb_ref, dg_ref = rest[len(deps):len(deps) + 4]
        rest = rest[len(deps) + 4:]
        acc_ref, dw_ref = rest[-2:]
        step = pl.program_id(0)

        @pl.when(step == 0)
        def _():
            dw_ref[...] = jnp.zeros_like(dw_ref)
            acc_ref[...] = jnp.zeros_like(acc_ref)

        cat = cat_ref[...]
        g = g_ref[...]
        y = jnp.dot(cat, w_ref[...], preferred_element_type=F32) if last else b_ref[...]
        r = lax.rsqrt(jnp.mean(y * y, axis=-1, keepdims=True) + EPS)
        if last:
            loss_ref, dx_ref, loss_acc_ref = rest[:3]
            err = a_ref[...] + y * r * g - b_ref[...]

            @pl.when(step == 0)
            def _():
                loss_acc_ref[...] = jnp.zeros_like(loss_acc_ref)

            loss_acc_ref[...] += _rows8(err * err)
            dz = err * (1.0 / D)
            dx_ref[...] = dz
        else:
            dz = a_ref[...]
        a = dz * g
        dy = r * a - y * (r * r * r) * jnp.mean(a * y, axis=-1, keepdims=True)
        acc_ref[...] += _rows8(dz * (y * r))
        dyb = dy.astype(BF16)
        dcat_ref[...] = lax.dot_general(dyb, w_ref[...], NT, preferred_element_type=F32)
        dw_ref[...] += lax.dot_general(cat, dyb, TN, preferred_element_type=F32)

        @pl.when(step == n_steps - 1)
        def _():
            _store_lane_rows(dg_ref, acc_ref[...])
            dwb_ref[...] = dw_ref[...].astype(BF16)
            own_ref[...] = _own_piece(dw_ref, place_ref)
            if last:
                loss_ref[...] = jnp.full((8, 128), (0.5 / D) * jnp.sum(loss_acc_ref[...]), F32)

    row = lambda: pl.BlockSpec((TM, D), lambda i: (i, 0))
    full = _whole
    return pl.pallas_call(
        body, name="out_loss_bwd" if last else "bwd_out", grid=(n_steps,),
        in_specs=[row(), row(), _layer(l, 1, D), row(), full((D, D)), pl.BlockSpec(memory_space=pltpu.SMEM)]
        + [ANY] * len(deps),
        out_specs=[row(), full((D // 8, D)), full((D, D)), full((8, 128))] + ([full((8, 128)), row()] if last else []),
        out_shape=[jax.ShapeDtypeStruct((S, D), F32), jax.ShapeDtypeStruct((D // 8, D), F32),
                   jax.ShapeDtypeStruct((D, D), BF16), jax.ShapeDtypeStruct((8, 128), F32)]
        + ([jax.ShapeDtypeStruct((8, 128), F32), jax.ShapeDtypeStruct((S, D), F32)] if last else []),
        scratch_shapes=([pltpu.VMEM((8, D), F32)] if last else []) + [pltpu.VMEM((8, D), F32), pltpu.VMEM((D, D), F32)],
        compiler_params=_params(),
    )(*((x, target) if last else (dxn, y)), g_post, cat, w_out, place_arr, *deps)


def _bwd_mix(l, pu, pg, q, kv, ag, dcat, pool_w, pool_scale, sinks, tables, deps=(), dpw_dest=None):
    bias, tri = tables
    deps = tuple(deps) + (() if dpw_dest is None else (dpw_dest,))

    def body(pu_ref, pup_ref, pg_ref, q_ref, kv_ref, kvp_ref, ag_ref, dcat_ref, pw_ref, sc_ref, sink_ref, bias_ref,
             tri_ref, *rest):
        dproj_ref, dpw_ref, dsc_ref, dsink_ref, ext_ref, dext_ref, tmp_a, tmp_b, dkv_ref = rest[len(deps):]
        tmp_refs = (tmp_a, tmp_b)
        step = pl.program_id(0)
        i = NB - 1 - step

        @pl.when(step == 0)
        def _():
            dpw_ref[...] = jnp.zeros_like(dpw_ref)
            dsc_ref[...] = jnp.zeros_like(dsc_ref)
            dsink_ref[...] = jnp.zeros_like(dsink_ref)
            for ref in (ext_ref, tmp_a, tmp_b):
                ref[0:PAD, :] = jnp.zeros((PAD, 512), F32)
            dext_ref[BLK:POOL_ROWS, :] = jnp.zeros((HALO + PAD, 512), F32)
            dkv_ref[...] = jnp.zeros_like(dkv_ref)

        ext_ref[PAD:PAD + HALO, :] = jnp.where(i > 0, pup_ref[...], 0.0)
        ext_ref[PAD + HALO:POOL_ROWS, :] = pu_ref[...]
        _window_sums(ext_ref, tmp_refs, True)
        dpooled = []
        for g, w in enumerate(POOL_WINDOWS):
            lanes = slice(g * 128, (g + 1) * 128)
            pooled, inv = _pool_block(ext_ref, tmp_refs, i, g, w)
            pooled_b = pooled.astype(BF16)
            mixed = jnp.dot(pooled_b, pw_ref[g], preferred_element_type=F32)
            scale = sc_ref[:, lanes]
            gate = pg_ref[:, lanes]
            sg = _sigmoid(gate)
            dpo = dcat_ref[:, lanes]
            dproj_ref[:, C_PG + g * 128:C_PG + (g + 1) * 128] = (
                dpo * (mixed * scale) * (sg * (1.0 + gate * (1.0 - sg)))).astype(BF16)
            dms = dpo * (gate * sg)
            dsc_ref[g:g + 1, :] += jnp.sum(dms * mixed, axis=0, keepdims=True)
            dmixed = (dms * scale).astype(BF16)
            dpw_ref[g] += lax.dot_general(pooled_b, dmixed, TN, preferred_element_type=F32)
            dpooled.append(lax.dot_general(dmixed, pw_ref[g], NT, preferred_element_type=F32))
            dext_ref[0:BLK, lanes] = dpooled[g] * inv
        _window_sums(dext_ref, tmp_refs, False)
        for g in range(len(POOL_WINDOWS)):
            lanes = slice(g * 128, (g + 1) * 128)
            dproj_ref[:, C_PU + g * 128:C_PU + (g + 1) * 128] = (tmp_refs[g % 2][0:BLK, lanes] - dpooled[g]).astype(BF16)
        dext_ref[BLK:BLK + HALO, :] = dext_ref[0:HALO, :]

        own = _own_block_mask()
        tri = tri_ref[...]
        k_var = _head_variants(kv_ref[:, 0:128], kvp_ref[:, 0:128])
        v_var = _head_variants(kv_ref[:, 128:256], kvp_ref[:, 128:256])
        q2 = [_stack_tiles(q_ref, hkv) for hkv in range(2)]
        s = _scores(q2, k_var, own)
        p, p_sink = {}, {}
        for key, s_head in s.items():
            head = _head_of(*key)
            p[key], p_sink[key] = _softmax(s_head, bias_ref[head], sink_ref[l, head])

        do2, p_b, dp = [], {}, {}
        for hkv in range(2):
            gate = _stack_tiles(ag_ref, hkv)
            sg = _sigmoid(gate)
            dca = _stack_tiles(dcat_ref, hkv, D_POOL)
            do2.append((dca * (gate * sg)).astype(BF16))
            o2 = jnp.zeros((2 * BLK, 128), F32)
            for half in range(2):
                p_b[hkv, half] = _spread_pair(p, hkv, half, tri)
                o2 = o2 + jnp.dot(p_b[hkv, half], v_var[hkv][half], preferred_element_type=F32)
                full = lax.dot_general(do2[hkv], v_var[hkv][half], NT, preferred_element_type=F32)
                for t in range(2):
                    dp[hkv, t, half] = _merge(_rows(full, t), own)
            dag = dca * o2 * (sg * (1.0 + gate * (1.0 - sg)))
            for t in range(2):
                lo = C_AG + (2 * hkv + t) * 128
                dproj_ref[:, lo:lo + 128] = _rows(dag, t).astype(BF16)

        ds = {}
        for key in p:
            delta = jnp.sum(p[key] * dp[key], axis=-1, keepdims=True)
            ds[key] = p[key] * (dp[key] - delta)
            head = _head_of(*key)
            dsink_ref[0:1, :] += jnp.where(lax.broadcasted_iota(jnp.int32, (1, 128), 1) == head,
                                           -jnp.sum(p_sink[key] * delta, axis=0, keepdims=True), 0.0)

        dk_acc = [[None, None], [None, None]]
        dv_acc = [[None, None], [None, None]]
        for hkv in range(2):
            dq2 = jnp.zeros((2 * BLK, 128), F32)
            for half in range(2):
                ds_b = _spread_pair(ds, hkv, half, tri)
                dq2 = dq2 + jnp.dot(ds_b, k_var[hkv][half], preferred_element_type=F32)
                dk_acc[hkv][half] = lax.dot_general(ds_b, q2[hkv], TN, preferred_element_type=F32)
                dv_acc[hkv][half] = lax.dot_general(p_b[hkv, half], do2[hkv], TN, preferred_element_type=F32)
            for t in range(2):
                lo = C_Q + (2 * hkv + t) * 128
                dproj_ref[:, lo:lo + 128] = (_rows(dq2, t) * 0.125).astype(BF16)

        low = lax.broadcasted_iota(jnp.int32, (2 * BLK, 128), 1) < 64

        def gather_heads(acc):
            return jnp.where(low, acc[0][0] + pltpu.roll(acc[0][1], 64, axis=1),
                             pltpu.roll(acc[1][0], 64, axis=1) + acc[1][1])

        dk = gather_heads(dk_acc) * 0.125
        dv = gather_heads(dv_acc)
        dproj_ref[:, C_K:C_V] = (dk[BLK:, :] + dkv_ref[:, 0:128]).astype(BF16)
        dproj_ref[:, C_V:C_AG] = (dv[BLK:, :] + dkv_ref[:, 128:256]).astype(BF16)
        dkv_ref[:, 0:128] = dk[:BLK, :]
        dkv_ref[:, 128:256] = dv[:BLK, :]

    rev = lambda w: pl.BlockSpec((BLK, w), lambda s: (NB - 1 - s, 0))
    prev = lambda w: pl.BlockSpec((BLK, w), lambda s: (jnp.maximum(NB - 2 - s, 0), 0))
    halo = pl.BlockSpec((HALO, 512), lambda s: (jnp.maximum((NB - 1 - s) * (BLK // HALO) - 1, 0), 0))
    return pl.pallas_call(
        body, name="bwd_mix", grid=(NB,),
        in_specs=[rev(512), halo, rev(512), rev(512), rev(256), prev(256), rev(512), rev(D),
                  _layer(l, 4, 128, 128), _layer(l, 1, 512), pl.BlockSpec(memory_space=pltpu.SMEM),
                  pl.BlockSpec((None, N_HEADS, BLK, BLK), lambda s: (jnp.minimum(NB - 1 - s, 1), 0, 0, 0)),
                  _whole((BLK, BLK))] + [ANY] * len(deps),
        out_specs=[rev(D_IN), _layer(l, 4, 128, 128),
                   pl.BlockSpec((4, 128), lambda s: (0, 0)), pl.BlockSpec((8, 128), lambda s: (0, 0))],
        out_shape=[jax.ShapeDtypeStruct((S, D_IN), BF16), jax.ShapeDtypeStruct((DEPTH, 4, 128, 128), F32),
                   jax.ShapeDtypeStruct((4, 128), F32), jax.ShapeDtypeStruct((8, 128), F32)],
        input_output_aliases={} if dpw_dest is None else {12 + len(deps): 1},
        scratch_shapes=[pltpu.VMEM((POOL_ROWS, 512), F32)] * 4 + [pltpu.VMEM((BLK, 256), F32)],
        compiler_params=_params(),
    )(pu, pu, pg, q, kv, kv, ag, dcat, pool_w, pool_scale, sinks, bias, tri, *deps)


def _bwd_in_dw(l, dproj, x, g_pre, place_arr, deps=()):
    n_steps = S // TM

    def body(dp_ref, x_ref, g_ref, place_ref, *rest):
        own_ref, dwb_ref, dw_ref = rest[len(deps):]
        step = pl.program_id(0)

        @pl.when(step == 0)
        def _():
            dw_ref[...] = jnp.zeros_like(dw_ref)

        xt = x_ref[...]
        r = lax.rsqrt(jnp.mean(xt * xt, axis=-1, keepdims=True) + EPS)
        h = (xt * r * g_ref[...]).astype(BF16)
        dw_ref[...] += lax.dot_general(dp_ref[...], h, TN, preferred_element_type=F32)

        @pl.when(step == n_steps - 1)
        def _():
            dwb_ref[...] = dw_ref[...].astype(BF16)
            own_ref[...] = _own_piece(dw_ref, place_ref)

    row = lambda w: pl.BlockSpec((TM, w), lambda i: (i, 0))
    full = _whole
    return pl.pallas_call(
        body, name="bwd_in_dw", grid=(n_steps,),
        in_specs=[row(D_IN), row(D), _layer(l, 1, D), pl.BlockSpec(memory_space=pltpu.SMEM)] + [ANY] * len(deps),
        out_specs=[full((D_IN // 8, D)), full((D_IN, D))],
        out_shape=[jax.ShapeDtypeStruct((D_IN // 8, D), F32), jax.ShapeDtypeStruct((D_IN, D), BF16)],
        scratch_shapes=[pltpu.VMEM((D_IN, D), F32)],
        compiler_params=_params(),
    )(dproj, x, g_pre, place_arr, *deps)


def _bwd_in_dx(l, dproj, w_in_t, x, g_pre, dres, deps=(), dw_place=None):
    n_steps = S // TM
    with_dw = dw_place is not None

    def body(dp_ref, w_ref, x_ref, g_ref, dres_ref, *rest):
        place_ref = rest[0] if with_dw else None
        rest = rest[with_dw + len(deps):]
        if with_dw:
            dx_ref, dg_ref, own_ref, dwb_ref, acc_ref, dw_ref = rest
        else:
            dx_ref, dg_ref, acc_ref = rest
        step = pl.program_id(0)

        @pl.when(step == 0)
        def _():
            acc_ref[...] = jnp.zeros_like(acc_ref)
            if with_dw:
                dw_ref[...] = jnp.zeros_like(dw_ref)

        g = g_ref[...]
        halves = [slice(k * (TM // 2), (k + 1) * (TM // 2)) for k in range(2)]
        dh = [jnp.dot(dp_ref[rows, :], w_ref[...], preferred_element_type=F32) for rows in halves]
        h = []
        for rows, dh_k in zip(halves, dh):
            xt = x_ref[rows, :]
            r = lax.rsqrt(jnp.mean(xt * xt, axis=-1, keepdims=True) + EPS)
            xn = xt * r
            acc_ref[...] += _rows8(dh_k * xn)
            a = dh_k * g
            dx_ref[rows, :] = dres_ref[rows, :] + (
                r * a - xt * (r * r * r) * jnp.mean(a * xt, axis=-1, keepdims=True))
            h.append((xn * g).astype(BF16))
        if with_dw:
            dw_ref[...] += lax.dot_general(dp_ref[...], jnp.concatenate(h, axis=0), TN, preferred_element_type=F32)

        @pl.when(step == n_steps - 1)
        def _():
            _store_lane_rows(dg_ref, acc_ref[...])
            if with_dw:
                dwb_ref[...] = dw_ref[...].astype(BF16)
                own_ref[...] = _own_piece(dw_ref, place_ref)

    row = lambda w: pl.BlockSpec((TM, w), lambda i: (i, 0))
    full = _whole
    dw_specs = [full((D_IN // 8, D)), full((D_IN, D))] if with_dw else []
    dw_shapes = [jax.ShapeDtypeStruct((D_IN // 8, D), F32), jax.ShapeDtypeStruct((D_IN, D), BF16)] if with_dw else []
    return pl.pallas_call(
        body, name="bwd_in" if with_dw else "bwd_in_dx", grid=(n_steps,),
        in_specs=[row(D_IN), full((D_IN, D)), row(D), _layer(l, 1, D), row(D)]
        + [pl.BlockSpec(memory_space=pltpu.SMEM)] * with_dw + [ANY] * len(deps),
        out_specs=[row(D), full((8, 128))] + dw_specs,
        out_shape=[jax.ShapeDtypeStruct((S, D), F32), jax.ShapeDtypeStruct((8, 128), F32)] + dw_shapes,
        scratch_shapes=[pltpu.VMEM((8, D), F32)] + [pltpu.VMEM((D_IN, D), F32)] * with_dw,
        compiler_params=_params(),
    )(dproj, w_in_t, x, g_pre, dres, *((dw_place,) if with_dw else ()), *deps)


HBM =pl.BlockSpec(memory_space=pltpu.HBM)
SEM = pl.BlockSpec(memory_space=pltpu.SEMAPHORE)
def _split_copy(collective_id=None):
    return pltpu.CompilerParams(has_side_effects=pltpu.SideEffectType.DATAFLOW_SIDE_EFFECTING,
                                collective_id=collective_id)


SPLIT_COPY = _split_copy()


def _in_hbm(a):
    return pltpu.with_memory_space_constraint(a, pltpu.HBM)

def _place():
    return lax.axis_index("x"), lax.axis_index("y"), lax.axis_index("c")


def _other_chips(x, y):
    return [(1 - x, y), (x, 1 - y), (1 - x, 1 - y)]


def _peer(x, y, c, m):
    return (x ^ (m >> 2), y ^ ((m >> 1) & 1), c ^ (m & 1))


SAME_CORE = (2, 4, 6)


def _place_cast(name, src, chip_arr, tile, layers, deps=()):
    _, n, cols = src.shape
    steps = n // tile
    k = len(layers)

    def body(chip_ref, *refs):
        for s_ref, o_ref in zip(refs[:k], refs[k + len(deps):]):
            o_ref[...] = s_ref[...].astype(BF16)

    def layer_spec(l):
        return pl.BlockSpec((None, tile, cols), lambda i, chip: (l, i, 0))

    return pl.pallas_call(
        body, name=name,
        grid_spec=pltpu.PrefetchScalarGridSpec(
            num_scalar_prefetch=1, grid=(steps,),
            in_specs=[layer_spec(l) for l in layers] + [ANY] * len(deps),
            out_specs=[pl.BlockSpec((tile, cols), lambda i, chip: (chip[0] * steps + i, 0))] * k),
        out_shape=[jax.ShapeDtypeStruct((N_SHARDS * n, cols), BF16)] * k,
        compiler_params=_params(),
    )(chip_arr, *[src] * k, *deps)


def _chip_rows(ref, chip, half=None):
    n = ref.shape[0] // N_SHARDS
    if half is None:
        return ref.at[pl.ds(pl.multiple_of(chip * n, 16), n), :]
    return ref.at[pl.ds(pl.multiple_of(chip * n + half * (n // 2), 16), n // 2), :]


def _gather_start(name, bufs, halved, collective_id):
    n = len(bufs)

    def body(*refs):
        ins, send, recv, token = refs[:n], refs[n:2 * n], refs[2 * n:3 * n], refs[-1]
        x, y, c = _place()
        _handshake([(*chip, c) for chip in _other_chips(x, y)])
        for a, buf in enumerate(ins):
            own = _chip_rows(buf, 2 * x + y, c if a in halved else None)
            for j, chip in enumerate(_other_chips(x, y)):
                pltpu.make_async_remote_copy(src_ref=own, dst_ref=own, send_sem=send[a].at[j], recv_sem=recv[a].at[j],
                                             device_id=(*chip, c), device_id_type=MESH).start()
        token[...] = jnp.zeros_like(token)

    outs = pl.pallas_call(
        body, name=name, in_specs=[HBM] * n,
        out_specs=[SEM] * (2 * n) + [HBM] * n + [pl.BlockSpec(memory_space=pltpu.VMEM)],
        out_shape=[pltpu.SemaphoreType.DMA((3,))] * (2 * n) + [pltpu.HBM(b.shape, b.dtype) for b in bufs]
        + [jax.ShapeDtypeStruct((8, 128), F32)],
        input_output_aliases={a: 2 * n + a for a in range(n)},
        compiler_params=_split_copy(collective_id),
    )(*[_in_hbm(b) for b in bufs])
    return outs[:n], outs[n:2 * n], outs[2 * n:3 * n], outs[-1]


def _gather_wait(name, buf, send_sem, recv_sem, after, halved=False):
    def body(buf_ref, send_ref, recv_ref, *rest):
        x, y, c = _place()
        half = c if halved else None
        own = _chip_rows(buf_ref, 2 * x + y, half)
        for j, chip in enumerate(_other_chips(x, y)):
            copy = pltpu.make_async_remote_copy(src_ref=own, dst_ref=_chip_rows(buf_ref, 2 * chip[0] + chip[1], half),
                                                send_sem=send_ref.at[j], recv_sem=recv_ref.at[j],
                                                device_id=(*chip, c), device_id_type=MESH)
            copy.wait_send()
            copy.wait_recv()

    return pl.pallas_call(
        body, name=name, in_specs=[HBM, SEM, SEM] + [ANY] * len(after), out_specs=HBM,
        out_shape=pltpu.HBM(buf.shape, buf.dtype), input_output_aliases={0: 0}, compiler_params=SPLIT_COPY,
    )(buf, send_sem, recv_sem, *after)


def _handshake(peers):
    barrier = pltpu.get_barrier_semaphore()
    for peer in peers:
        pl.semaphore_signal(barrier, inc=1, device_id=peer, device_id_type=MESH)
    pl.semaphore_wait(barrier, len(peers))


def _sibling_handshake(x, y, c):
    _handshake([(x, y, 1 - c)])


def _forward_halves(name, buf, collective_id):
    def body(in_ref, out_ref, send_sems, recv_sems):
        x, y, c = _place()
        _sibling_handshake(x, y, c)

        def copy(j, chip, half):
            rows = 2 * chip[0] + chip[1]
            return pltpu.make_async_remote_copy(
                src_ref=_chip_rows(in_ref, rows, half), dst_ref=_chip_rows(out_ref, rows, half), send_sem=send_sems.at[j],
                recv_sem=recv_sems.at[j], device_id=(x, y, 1 - c), device_id_type=MESH)

        chips = _other_chips(x, y)
        for j, chip in enumerate(chips):
            copy(j, chip, c).start()
        for j, chip in enumerate(chips):
            copy(j, chip, c).wait_send()
            copy(j, chip, 1 - c).wait_recv()

    return pl.pallas_call(
        body, name=name, in_specs=[ANY], out_specs=ANY, out_shape=jax.ShapeDtypeStruct(buf.shape, buf.dtype),
        input_output_aliases={0: 0},
        scratch_shapes=[pltpu.SemaphoreType.DMA((3,))] * 2,
        compiler_params=pltpu.CompilerParams(collective_id=collective_id),
    )(buf)


def _piece_rows(ref, k):
    p = ref.shape[0] // 8
    return ref.at[pl.ds(pl.multiple_of(k * p, 32 // jnp.dtype(ref.dtype).itemsize), p), :]


def _exchange_start(name, arrays, collective_id):
    n = len(arrays)
    zones = [lax.empty((7, a.shape[0] // 8, a.shape[1]), a.dtype) for a in arrays]

    def body(*refs):
        srcs, lands = refs[:n], refs[n:2 * n]
        send, recv, token = refs[2 * n:3 * n], refs[3 * n:4 * n], refs[-1]
        x, y, c = _place()
        _handshake([_peer(x, y, c, m) for m in range(1, 8)])
        for a, (src, land) in enumerate(zip(srcs, lands)):
            for m in range(1, 8):
                px, py, pc = _peer(x, y, c, m)
                pltpu.make_async_remote_copy(
                    src_ref=_piece_rows(src, 4 * px + 2 * py + pc), dst_ref=land.at[m - 1], send_sem=send[a].at[m - 1],
                    recv_sem=recv[a].at[m - 1], device_id=(px, py, pc), device_id_type=MESH).start()
        token[...] = jnp.zeros_like(token)

    outs = pl.pallas_call(
        body, name=name, in_specs=[HBM] * (2 * n),
        out_specs=[SEM] * (2 * n) + [HBM] * (2 * n) + [pl.BlockSpec(memory_space=pltpu.VMEM)],
        out_shape=[pltpu.SemaphoreType.DMA((7,))] * (2 * n) + [pltpu.HBM(a.shape, a.dtype) for a in arrays + zones]
        + [jax.ShapeDtypeStruct((8, 128), F32)],
        input_output_aliases={a: 2 * n + a for a in range(2 * n)},
        compiler_params=_split_copy(collective_id),
    )(*[_in_hbm(a) for a in arrays + zones])
    return outs[:n], outs[n:2 * n], outs[2 * n:3 * n], outs[3 * n:4 * n], outs[-1]


def _exchange_wait(name, started, after, which=None, with_sent=False):
    which = range(len(started[2])) if which is None else which
    send_sems, recv_sems, arrays, zones = [[group[k] for k in which] for group in started[:4]]
    n = len(arrays)

    def body(*refs):
        srcs, lands = refs[:n], refs[n:2 * n]
        send, recv = refs[2 * n:3 * n], refs[3 * n:4 * n]
        x, y, c = _place()
        for a, (src, land) in enumerate(zip(srcs, lands)):
            for m in range(1, 8):
                px, py, pc = _peer(x, y, c, m)
                copy = pltpu.make_async_remote_copy(
                    src_ref=_piece_rows(src, 4 * px + 2 * py + pc), dst_ref=land.at[m - 1], send_sem=send[a].at[m - 1],
                    recv_sem=recv[a].at[m - 1], device_id=(px, py, pc), device_id_type=MESH)
                copy.wait_send()
                copy.wait_recv()

    outs = pl.pallas_call(
        body, name=name, in_specs=[HBM] * (2 * n) + [SEM] * (2 * n) + [ANY], out_specs=[HBM] * (2 * n),
        out_shape=[pltpu.HBM(a.shape, a.dtype) for a in list(arrays) + list(zones)],
        input_output_aliases={a: a for a in range(2 * n)}, compiler_params=SPLIT_COPY,
    )(*arrays, *zones, *send_sems, *recv_sems, after)
    return outs if with_sent else outs[n:]


def _sum_pieces(name, weights, place_arr, dests=None, small=()):
    steps = 2
    flat = [item for items in weights for item in items]
    n = len(flat)

    def body(place_ref, *refs):
        first_out = len(refs) - len(weights) - len(small)
        outs = iter(refs[first_out:])
        small_refs = refs[first_out - 2 * len(small):first_out]

        @pl.when(pl.program_id(0) == 0)
        def _():
            for j in range(len(small)):
                total = small_refs[2 * j][...]
                for m in range(7):
                    total = total + small_refs[2 * j + 1][m]
                refs[first_out + len(weights) + j][...] = total

        k = 0
        for items in weights:
            out_ref = next(outs)
            for layer, _, _ in items:
                total = refs[k][...]
                for m in range(7):
                    total = total + refs[n + k][m].astype(F32)
                if len(items) == DEPTH:
                    out_ref[layer] = total
                else:
                    out_ref[...] = total
                k += 1

    def out_spec(items):
        _, own, _ = items[0]
        t, cols = own.shape[0] // steps, own.shape[1]
        if len(items) == DEPTH:
            return pl.BlockSpec((DEPTH, t, cols), lambda i, place: (0, place[1] * steps + i, 0))
        layer = items[0][0]
        return pl.BlockSpec((None, t, cols), lambda i, place: (layer, place[1] * steps + i, 0))

    owns = [own for _, own, _ in flat]
    dests = [] if dests is None else list(dests)
    piece = lambda a: pl.BlockSpec((a.shape[0] // 8, a.shape[1]), lambda i, place: (place[0], 0))
    small_specs = [spec for a, r in small for spec in (piece(a), pl.BlockSpec(r.shape, lambda i, place: (0, 0, 0)))]
    return pl.pallas_call(
        body, name=name,
        grid_spec=pltpu.PrefetchScalarGridSpec(
            num_scalar_prefetch=1, grid=(steps,),
            in_specs=[pl.BlockSpec((o.shape[0] // steps, o.shape[1]), lambda i, place: (i, 0)) for o in owns]
            + [pl.BlockSpec((7, o.shape[0] // steps, o.shape[1]), lambda i, place: (0, i, 0)) for o in owns]
            + [ANY] * len(dests) + small_specs,
            out_specs=[out_spec(items) for items in weights] + [piece(a) for a, _ in small]),
        out_shape=[jax.ShapeDtypeStruct((DEPTH, 2 * items[0][1].shape[0], items[0][1].shape[1]), F32)
                   for items in weights] + [jax.ShapeDtypeStruct(a.shape, F32) for a, _ in small],
        input_output_aliases={1 + 2 * n + k: k for k in range(len(dests))},
        compiler_params=_params(),
    )(place_arr, *owns, *[recv for _, _, recv in flat], *dests, *[a for pair in small for a in pair])


def _sum_small(name, partials, recvs, place_arr):
    n = len(partials)

    def body(place_ref, *refs):
        for o_ref, r_ref, out_ref in zip(refs[:n], refs[n:2 * n], refs[2 * n:]):
            total = o_ref[...]
            for m in range(7):
                total = total + r_ref[m]
            out_ref[...] = total

    piece = lambda a: pl.BlockSpec((a.shape[0] // 8, a.shape[1]), lambda i, place: (place[0], 0))
    return pl.pallas_call(
        body, name=name,
        grid_spec=pltpu.PrefetchScalarGridSpec(
            num_scalar_prefetch=1, grid=(1,),
            in_specs=[piece(a) for a in partials] + [pl.BlockSpec(r.shape, lambda i, place: (0, 0, 0)) for r in recvs],
            out_specs=[piece(a) for a in partials]),
        out_shape=[jax.ShapeDtypeStruct(a.shape, F32) for a in partials],
        compiler_params=_params(),
    )(place_arr, *partials, *recvs)


def _share(name, bufs, parts, gathered=(), collective_id=None):
    n, n_g = len(bufs), len(gathered)
    total = n + n_g

    def body(*refs):
        ins, outs = refs[:total], refs[total:2 * total]
        send_sems, recv_sems, send_g, recv_g = refs[2 * total:]
        x, y, c = _place()
        _handshake([_peer(x, y, c, m) for m in (SAME_CORE if gathered else ()) + (1,)])

        def half(ref, l, which):
            p = ref.shape[1] // 2
            return ref.at[l, pl.ds(pl.multiple_of(which * p, 8), p), :]

        def swap(k, which):
            a, l = parts[k]
            return pltpu.make_async_remote_copy(
                src_ref=half(ins[a], l, which), dst_ref=half(outs[a], l, which), send_sem=send_sems.at[k],
                recv_sem=recv_sems.at[k], device_id=(x, y, 1 - c), device_id_type=MESH)

        def spread(a, m, sender, held, to):
            k = 4 * sender[0] + 2 * sender[1] + sender[2]
            return pltpu.make_async_remote_copy(
                src_ref=_piece_rows(held[n + a], k), dst_ref=_piece_rows(outs[n + a], k),
                send_sem=send_g.at[7 * a + m - 1], recv_sem=recv_g.at[7 * a + m - 1], device_id=to, device_id_type=MESH)

        me, sibling = (x, y, c), (x, y, 1 - c)
        for k in range(len(parts)):
            swap(k, c).start()
        def own(a, m):
            return spread(a, m, me, ins, _peer(x, y, c, m))

        def handed_on(a, m):
            return spread(a, m + 1, _peer(x, y, c, m), outs, sibling)

        for a in range(n_g):
            for m in SAME_CORE + (1,):
                own(a, m).start()
        for a in range(n_g):
            for m in SAME_CORE:
                spread(a, m, _peer(x, y, c, m), ins, _peer(x, y, c, m)).wait_recv()
                handed_on(a, m).start()
        for k in range(len(parts)):
            swap(k, c).wait_send()
            swap(k, 1 - c).wait_recv()
        for a in range(n_g):
            for m in SAME_CORE + (1,):
                own(a, m).wait_send()
            for m in SAME_CORE:
                handed_on(a, m).wait_send()
                spread(a, m + 1, _peer(x, y, c, m + 1), ins, sibling).wait_recv()
            spread(a, 1, sibling, ins, sibling).wait_recv()

    arrays = list(bufs) + list(gathered)
    return pl.pallas_call(
        body, name=name, in_specs=[ANY] * total, out_specs=[ANY] * total,
        out_shape=[jax.ShapeDtypeStruct(b.shape, F32) for b in arrays],
        input_output_aliases={a: a for a in range(total)},
        scratch_shapes=[pltpu.SemaphoreType.DMA((max(len(parts), 1),))] * 2
        + [pltpu.SemaphoreType.DMA((max(7 * n_g, 1),))] * 2,
        compiler_params=pltpu.CompilerParams(collective_id=collective_id),
    )(*arrays)


def _adamw_math(w, g, m, v):
    nm = ADAM_B1 * m + (1.0 - ADAM_B1) * g
    nv = ADAM_B2 * v + (1.0 - ADAM_B2) * (g * g)
    m_hat = nm / (1.0 - ADAM_B1 ** ADAM_STEP)
    v_hat = nv / (1.0 - ADAM_B2 ** ADAM_STEP)
    return -ADAM_LR * (m_hat / (jnp.sqrt(v_hat) + ADAM_EPS) + ADAM_WD * w), nm, nv


def _adamw(name, w, g, m, v, rows_per_step, first=0, count=None, dests=None, deps=()):
    layers, rows, cols = w.shape
    count = layers if count is None else count

    def body(w_ref, g_ref, m_ref, v_ref, *rest):
        d_ref, nm_ref, nv_ref, g_out_ref = rest[-4:]
        d_ref[...], nm_ref[...], nv_ref[...] = _adamw_math(w_ref[...], g_ref[...], m_ref[...], v_ref[...])
        g_out_ref[...] = g_ref[...]

    spec = pl.BlockSpec((1, rows_per_step, cols), lambda l, i: (first + l, i, 0))
    shape = jax.ShapeDtypeStruct(w.shape, F32)
    dests = () if dests is None else tuple(dests)
    return pl.pallas_call(
        body, name=name, grid=(count, rows // rows_per_step),
        in_specs=[spec] * 4 + [ANY] * (len(dests) + len(deps)), out_specs=[spec] * 4, out_shape=[shape] * 4,
        input_output_aliases={4 + k: k for k in range(len(dests))},
        compiler_params=_params(("arbitrary", "arbitrary")),
    )(w, g, m, v, *dests, *deps)


def _pack_misc(pool_scale, sinks, norm_pre, norm_post):
    sink_rows = jnp.zeros((DEPTH, 8, 128), F32).at[:, 0, 0:N_HEADS].set(sinks).reshape(2 * 8, 128)
    return jnp.concatenate([pool_scale.reshape(8, 128), norm_pre.reshape(16, 128), norm_post.reshape(16, 128),
                            sink_rows, jnp.zeros((8, 128), F32)], axis=0)


def _adamw_small(w, g, m, v, pool):
    def body(w_ref, g_ref, m_ref, v_ref, pw_ref, pg_ref, pm_ref, pv_ref, *rest):
        outs, pool_outs, (d_ref, nm_ref, nv_ref) = rest[:17], rest[17:21], rest[21:]
        pool_outs[0][...] = pg_ref[...]
        pool_outs[1][...], pool_outs[2][...], pool_outs[3][...] = _adamw_math(
            pw_ref[...], pg_ref[...], pm_ref[...], pv_ref[...])
        d_ref[...], nm_ref[...], nv_ref[...] = _adamw_math(w_ref[...], g_ref[...], m_ref[...], v_ref[...])
        for k, src in enumerate([g_ref, d_ref, nm_ref, nv_ref]):
            scale, sinks, pre, post = outs[4 * k:4 * k + 4]
            for l in range(DEPTH):
                for j in range(4):
                    scale[l:l + 1, j * 128:(j + 1) * 128] = src[MISC_SCALE + 4 * l + j:MISC_SCALE + 4 * l + j + 1, :]
                for j in range(8):
                    pre[l:l + 1, j * 128:(j + 1) * 128] = src[MISC_PRE + 8 * l + j:MISC_PRE + 8 * l + j + 1, :]
                    post[l:l + 1, j * 128:(j + 1) * 128] = src[MISC_POST + 8 * l + j:MISC_POST + 8 * l + j + 1, :]
                sinks[l:l + 1, :] = src[MISC_SINKS + 8 * l:MISC_SINKS + 8 * l + 1, 0:N_HEADS]
        outs[16][...] = g_ref[MISC_LOSS:MISC_LOSS + 1, 0:1]

    vmem = pl.BlockSpec(memory_space=pltpu.VMEM)
    shapes = [(DEPTH, D_POOL), (DEPTH, N_HEADS), (DEPTH, D), (DEPTH, D)] * 4 + [(1, 1)]
    shapes += [pool[0].shape] * 4
    return pl.pallas_call(
        body, name="adamw_small", in_specs=[vmem] * 8, out_specs=[vmem] * 21,
        out_shape=[jax.ShapeDtypeStruct(s, F32) for s in shapes],
        scratch_shapes=[pltpu.VMEM((MISC_ROWS, 128), F32)] * 3,
    )(w, g, m, v, *pool)


def kernel(x, w_in, pool_w, pool_scale, attn_sinks, w_out, norm_pre, norm_post, loss_target, m_w_in, m_pool_w, m_pool_scale, m_attn_sinks, m_w_out, m_norm_pre, m_norm_post, v_w_in, v_pool_w, v_pool_scale, v_attn_sinks, v_w_out, v_norm_pre, v_norm_post):
    cx, cy, cc = _place()
    chip_arr = jnp.reshape(2 * cx + cy, (1,)).astype(jnp.int32)
    place_arr = jnp.stack([4 * cx + 2 * cy + cc, cc]).astype(jnp.int32)
    t = lambda a: jnp.transpose(a, (0, 2, 1))
    w_in_t = t(w_in)
    xs, target = x[0], loss_target[0]
    pool_w_b = pool_w.astype(BF16)
    tables = _attention_tables()
    scale3 = pool_scale.reshape(DEPTH, 1, D_POOL)
    pre3 = norm_pre.reshape(DEPTH, 1, D)
    post3 = norm_post.reshape(DEPTH, 1, D)

    (wi0,) = _place_cast("place_w_in0", w_in_t, chip_arr, 288, [0])
    first = _gather_start("gather_start_first", [wi0], halved=(0,), collective_id=ID_GATHER_FIRST)
    (wi1,) = _place_cast("place_w_in1", w_in_t, chip_arr, 288, [1], deps=(first[3],))
    wo = _place_cast("place_w_out", w_out, chip_arr, 256, [0, 1], deps=(first[3],))
    rest = _gather_start("gather_start_rest", [wi1, wo[0], wo[1]], halved=(0,), collective_id=ID_GATHER_REST)
    send, recv, bufs = [first[k] + rest[k] for k in range(3)]
    order = {(0, "in"): 0, (1, "in"): 1, (0, "out"): 2, (1, "out"): 3}

    saved = []
    packed = [_pack_misc(pool_scale, attn_sinks, norm_pre, norm_post),
              _pack_misc(m_pool_scale, m_attn_sinks, m_norm_pre, m_norm_post),
              _pack_misc(v_pool_scale, v_attn_sinks, v_norm_pre, v_norm_post)]
    after = (first[3], rest[3], pool_w_b, *tables, scale3, pre3, post3, *packed)
    below = None
    for l in range(DEPTH):
        k = order[l, "in"]
        w_in_l = _forward_halves(f"forward_w_in{l}", _gather_wait(f"gather_wait_in{l}", bufs[k], send[k], recv[k], after,
                                                                 halved=True), collective_id=ID_FORWARD[l])
        if below is None:
            pu, pg, q, kv, ag = _fwd_in(l, xs, pre3, w_in_l)
        else:
            y, xs, pu, pg, q, kv, ag = _fwd_in(l, xs, pre3, w_in_l, below)
            saved[l - 1][7] = y
        cat = _fwd_mix(l, pu, pg, q, kv, ag, pool_w_b, scale3, attn_sinks, tables)
        k = order[l, "out"]
        w_out_l = _gather_wait(f"gather_wait_out{l}", bufs[k], send[k], recv[k], (cat,))
        saved.append([xs, pu, pg, q, kv, ag, cat, None, w_in_l, w_out_l])
        below, after = (cat, w_out_l, post3), (w_out_l,)

    x_in, pu, pg, q, kv, ag, cat, y, w_in_l, w_out_l = saved[1]
    dcat, dw_out1, dw_out1_b, dg_post1, loss, xs = _bwd_out(1, cat, w_out_l, post3, place_arr, x=x_in, target=target)
    ex1_out = _exchange_start("exchange_start_out1", [dw_out1_b], ID_OUT1)
    dproj, dpw, dsc1, dsink1 = _bwd_mix(1, pu, pg, q, kv, ag, dcat, pool_w_b, scale3, attn_sinks, tables,
                                        deps=(ex1_out[4],))
    dx, dg_pre1, dw_in1, dw_in1_b = _bwd_in_dx(1, dproj, w_in_l, x_in, pre3, xs, dw_place=place_arr)
    ex1_in = _exchange_start("exchange_start_in1", [dw_in1_b], ID_IN1)

    x_in, pu, pg, q, kv, ag, cat, y, w_in_l, w_out_l = saved[0]
    dcat, dw_out0, dw_out0_b, dg_post0 = _bwd_out(0, cat, w_out_l, post3, place_arr, dxn=dx, y=y, deps=(ex1_in[4],))
    ex0_out = _exchange_start("exchange_start_out0", [dw_out0_b], ID_OUT0)
    dproj, dpw, dsc0, dsink0 = _bwd_mix(0, pu, pg, q, kv, ag, dcat, pool_w_b, scale3, attn_sinks, tables,
                                        deps=(ex0_out[4],), dpw_dest=dpw)
    dw_in0, dw_in0_b = _bwd_in_dw(0, dproj, x_in, pre3, place_arr)
    flat = lambda a: a.reshape(DEPTH * 4 * 128, 128)
    ex0_in = _exchange_start("exchange_start_in0", [flat(dpw), dw_in0_b], ID_IN0)

    grad_x, dg_pre0 = _bwd_in_dx(0, dproj, w_in_l, x_in, pre3, dx, deps=(ex0_in[4],))
    small = [jnp.concatenate([dsc0, dsc1, dg_pre0, dg_pre1, dg_post0, dg_post1, dsink0, dsink1, loss], axis=0)]
    ex_small = _exchange_start("exchange_start_small", small, ID_SMALL)
    (recv_out1,) = _exchange_wait("exchange_wait_out1", ex1_out, ex_small[4])
    (recv_in1,) = _exchange_wait("exchange_wait_in1", ex1_in, recv_out1)
    g_in, g_out = _sum_pieces("sum_pieces_1", [[(1, dw_in1, recv_in1)], [(1, dw_out1, recv_out1)]], place_arr)
    (recv_out0,) = _exchange_wait("exchange_wait_out0", ex0_out, g_out)
    (g_out,) = _sum_pieces("sum_pieces_out0", [[(0, dw_out0, recv_out0)]], place_arr, dests=[g_out])
    g_in, g_out = _share("share_a", [g_in, g_out], [(0, 1), (1, 0), (1, 1)], collective_id=ID_SHARE_A)
    m_in_t, v_in_t = t(m_w_in), t(v_w_in)
    d_out, nm_out, nv_out, grad_w_out = _adamw("adamw_w_out", w_out, g_out, m_w_out, v_w_out, 256)
    upd_in = _adamw("adamw_w_in1", w_in_t, g_in, m_in_t, v_in_t, 288, first=1, count=1, deps=(d_out,))
    dpw_own, recv_pw = _exchange_wait("exchange_wait_pool", ex0_in, upd_in[0], which=[0], with_sent=True)
    (g_pw,) = _sum_small("sum_pool", [dpw_own], [recv_pw], place_arr)

    (recv_in0,) = _exchange_wait("exchange_wait_in0", ex0_in, g_pw, which=[1])
    (recv_misc,) = _exchange_wait("exchange_wait_small", ex_small, recv_in0)
    g_in, g_misc = _sum_pieces("sum_pieces_in0", [[(0, dw_in0, recv_in0)]], place_arr, dests=[g_in],
                               small=[(small[0], recv_misc)])
    g_in, g_pw, g_misc = _share("share_b", [g_in], [(0, 0)], [g_pw, g_misc], collective_id=ID_SHARE_B)
    d_in, nm_in, nv_in, grad_w_in_t = _adamw("adamw_w_in0", w_in_t, g_in, m_in_t, v_in_t, 288, first=0, count=1,
                                             dests=upd_in)
    small_out = _adamw_small(packed[0], g_misc, packed[1], packed[2],
                             (flat(pool_w), g_pw, flat(m_pool_w), flat(v_pool_w)))
    (g_sc, g_sk, g_pre, g_post, d_sc, d_sk, d_pre, d_post,
     m_sc, m_sk, m_pre, m_post, v_sc, v_sk, v_pre, v_post, loss_sum) = small_out[:17]
    g_pw, d_pw, m_pw, v_pw = [a.reshape(pool_w.shape) for a in small_out[17:]]
    return (loss_sum[0, 0], grad_x[None], t(grad_w_in_t), g_pw, g_sc, g_sk, grad_w_out, g_pre, g_post,
            t(d_in), d_pw, d_sc, d_sk, d_out, d_pre, d_post,
            t(nm_in), m_pw, m_sc, m_sk, nm_out, m_pre, m_post,
            t(nv_in), v_pw, v_sc, v_sk, nv_out, v_pre, v_post)
```

```python
import jax
import jax.numpy as jnp
from jax import lax
from jax.experimental import pallas as pl
from jax.experimental.pallas import tpu as pltpu

F32 = jnp.float32
BF16 = jnp.bfloat16

S = 2048
D = 1024
DEPTH = 2
D_POOL = 512
POOL_WINDOWS = (2, 4, 8, 16)
N_HEADS = 8
D_IN = 2304
N_SHARDS = 4
W_IN_SHARD = D_IN // N_SHARDS
W_OUT_SHARD = D // N_SHARDS
BLK = 128
NB = S // BLK
HALO = 16
PAD = 8
EPS = 1e-6
NEG_INF = -1e30
C_PU, C_PG, C_Q, C_K, C_V, C_AG = 0, 512, 1024, 1536, 1664, 1792

ADAM_LR = 0.001
ADAM_B1 = 0.9
ADAM_B2 = 0.999
ADAM_EPS = 1e-08
ADAM_WD = 0.01
ADAM_STEP = 10

TM = 512
VMEM_LIMIT = 56 * 1024 * 1024

NT = (((1,), (1,)), ((), ()))
TN = (((0,), (0,)), ((), ()))

MESH = pl.DeviceIdType.MESH
ANY = pl.BlockSpec(memory_space=pl.ANY)

ID_FORWARD = (0, 1)
(ID_SHARE_A, ID_SHARE_B, ID_GATHER_FIRST, ID_GATHER_REST, ID_OUT1, ID_IN1, ID_OUT0, ID_IN0, ID_SMALL) = range(2, 11)

MISC_SCALE, MISC_PRE, MISC_POST, MISC_SINKS, MISC_LOSS = 0, 8, 24, 40, 56
MISC_ROWS = 64


def _params(sem=("arbitrary",)):
    return pltpu.CompilerParams(dimension_semantics=sem, vmem_limit_bytes=VMEM_LIMIT)


def _sigmoid(v):
    return 1.0 / (1.0 + jnp.exp(-v))


def _rows8(v):
    r, c = v.shape
    return v.reshape(r // 8, 8, c).sum(axis=0)


def _layer(l, *shape):
    zeros = (0,) * len(shape)
    return pl.BlockSpec((None,) + shape, lambda i: (l,) + zeros)


def _whole(shape):
    zeros = (0,) * len(shape)
    return pl.BlockSpec(shape, lambda i: zeros, pipeline_mode=pl.Buffered(1))


def _fwd_in(l, x, g_pre, w_in_t, below=None):
    fused = below is not None

    def body(x_ref, g_ref, w_ref, *rest):
        if fused:
            cat_ref, wo_ref, gp_ref, y_ref, xn_ref = rest[:5]
            y = jnp.dot(cat_ref[...], wo_ref[...], preferred_element_type=F32)
            y_ref[...] = y
            xt = x_ref[...] + y * lax.rsqrt(jnp.mean(y * y, axis=-1, keepdims=True) + EPS) * gp_ref[...]
            xn_ref[...] = xt
        else:
            xt = x_ref[...]
        pu_ref, pg_ref, q_ref, kv_ref, ag_ref = rest[-5:]
        r = lax.rsqrt(jnp.mean(xt * xt, axis=-1, keepdims=True) + EPS)
        h = (xt * r * g_ref[...]).astype(BF16)

        def proj(lo, hi):
            return lax.dot_general(h, w_ref[lo:hi, :], NT, preferred_element_type=F32)

        pu_ref[...] = proj(C_PU, C_PG)
        pg_ref[...] = proj(C_PG, C_Q)
        q_ref[...] = proj(C_Q, C_K).astype(BF16)
        kv_ref[...] = proj(C_K, C_AG).astype(BF16)
        ag_ref[...] = proj(C_AG, D_IN)

    row = lambda w: pl.BlockSpec((TM, w), lambda i: (i, 0))
    act = jax.ShapeDtypeStruct((S, D), F32)
    return pl.pallas_call(
        body, name="fwd_out_in" if fused else "fwd_in", grid=(S // TM,),
        in_specs=[row(D), _layer(l, 1, D), _whole((D_IN, D))]
        + ([row(D), _whole((D, D)), _layer(l - 1, 1, D)] if fused else []),
        out_specs=[row(D)] * (2 * fused) + [row(512), row(512), row(512), row(256), row(512)],
        out_shape=[act] * (2 * fused)
        + [jax.ShapeDtypeStruct((S, 512), F32), jax.ShapeDtypeStruct((S, 512), F32),
           jax.ShapeDtypeStruct((S, 512), BF16), jax.ShapeDtypeStruct((S, 256), BF16),
           jax.ShapeDtypeStruct((S, 512), F32)],
        compiler_params=_params(),
    )(x, g_pre, w_in_t, *(below if fused else ()))


LOG2E = 1.4426950408889634
SCORE_SCALE = 0.125 * LOG2E


def _attention_tables():
    qi = jnp.arange(BLK)[:, None]
    kj = jnp.arange(BLK)[None, :]
    dist = ((qi - kj) % BLK).astype(F32)
    slopes = jnp.exp2(-jnp.arange(1, N_HEADS + 1, dtype=F32))
    bias = -(slopes * LOG2E)[:, None, None] * dist[None]
    first = jnp.where(kj > qi, NEG_INF, bias)
    return jnp.stack([first, bias]), (kj <= qi).astype(BF16)


def _own_block_mask():
    return lax.broadcasted_iota(jnp.int32, (BLK, BLK), 1) <= lax.broadcasted_iota(jnp.int32, (BLK, BLK), 0)


def _merge(full, own):
    return jnp.where(own, full[:, BLK:], full[:, :BLK])


def _spread(v, tri):
    own = v * tri
    return jnp.concatenate([v - own, own], axis=1)


def _head_variants(cur, prev):
    both = jnp.concatenate([prev, cur], axis=0).astype(F32)
    swapped = pltpu.roll(both, 64, axis=1)
    low = lax.broadcasted_iota(jnp.int32, both.shape, 1) < 64
    zero = jnp.zeros_like(both)
    return ((jnp.where(low, both, zero).astype(BF16), jnp.where(low, zero, swapped).astype(BF16)),
            (jnp.where(low, swapped, zero).astype(BF16), jnp.where(low, zero, both).astype(BF16)))


def _head_of(hkv, t, half):
    return hkv * 4 + 2 * t + half


def _rows(v, t):
    return v[t * BLK:(t + 1) * BLK]


def _stack_tiles(ref, hkv, offset=0):
    lo = offset + 2 * hkv * 128
    return jnp.concatenate([ref[:, lo:lo + 128], ref[:, lo + 128:lo + 256]], axis=0)


def _scores(q2, k_var, own):
    s = {}
    for hkv in range(2):
        for half in range(2):
            full = lax.dot_general(q2[hkv], k_var[hkv][half], NT, preferred_element_type=F32)
            for t in range(2):
                s[hkv, t, half] = _merge(_rows(full, t), own)
    return s


def _softmax(s, bias, sink):
    s = s * SCORE_SCALE + bias
    sink2 = sink * LOG2E
    m = jnp.maximum(jnp.max(s, axis=-1, keepdims=True), sink2)
    p = jnp.exp2(s - m)
    e_sink = jnp.exp2(sink2 - m)
    inv = 1.0 / (jnp.sum(p, axis=-1, keepdims=True) + e_sink)
    return p * inv, e_sink * inv


def _spread_pair(v, hkv, half, tri):
    return jnp.concatenate([_spread(v[hkv, t, half].astype(BF16), tri) for t in range(2)], axis=0)


POOL_ROWS = PAD + HALO + BLK


def _window_sums(src_ref, tmp_refs, trailing):
    lo, hi = (PAD, POOL_ROWS) if trailing else (0, HALO + BLK)
    cur = src_ref
    for level in range(len(POOL_WINDOWS)):
        lanes = slice(level * 128, 512)
        shift = -(1 << level) if trailing else (1 << level)
        dst = tmp_refs[level % 2]
        dst[lo:hi, lanes] = cur[lo:hi, lanes] + cur[lo + shift:hi + shift, lanes]
        cur = dst


def _pool_block(ext_ref, tmp_refs, i, g, w):
    lanes = slice(g * 128, (g + 1) * 128)
    rows = slice(PAD + HALO, POOL_ROWS)
    t = (i * BLK + lax.broadcasted_iota(jnp.int32, (BLK, 1), 0)).astype(F32)
    inv = 1.0 / jnp.minimum(t + 1.0, float(w))
    return tmp_refs[g % 2][rows, lanes] * inv - ext_ref[rows, lanes], inv


def _fwd_mix(l, pu, pg, q, kv, ag, pool_w, pool_scale, sinks, tables):
    bias, tri = tables

    def body(pu_ref, pup_ref, pg_ref, q_ref, kv_ref, kvp_ref, ag_ref, pw_ref, sc_ref, sink_ref, bias_ref, tri_ref,
             cat_ref, ext_ref, *tmp_refs):
        i = pl.program_id(0)

        @pl.when(i == 0)
        def _():
            for ref in (ext_ref, *tmp_refs):
                ref[0:PAD, :] = jnp.zeros((PAD, 512), F32)

        ext_ref[PAD:PAD + HALO, :] = jnp.where(i > 0, pup_ref[...], 0.0)
        ext_ref[PAD + HALO:POOL_ROWS, :] = pu_ref[...]
        _window_sums(ext_ref, tmp_refs, True)
        for g, w in enumerate(POOL_WINDOWS):
            lanes = slice(g * 128, (g + 1) * 128)
            pooled, _ = _pool_block(ext_ref, tmp_refs, i, g, w)
            mixed = jnp.dot(pooled.astype(BF16), pw_ref[g], preferred_element_type=F32)
            gate = pg_ref[:, lanes]
            cat_ref[:, lanes] = (mixed * sc_ref[:, lanes] * (gate * _sigmoid(gate))).astype(BF16)

        own = _own_block_mask()
        tri = tri_ref[...]
        k_var = _head_variants(kv_ref[:, 0:128], kvp_ref[:, 0:128])
        v_var = _head_variants(kv_ref[:, 128:256], kvp_ref[:, 128:256])
        s = _scores([_stack_tiles(q_ref, hkv) for hkv in range(2)], k_var, own)
        p = {}
        for (hkv, t, half), s_head in s.items():
            head = _head_of(hkv, t, half)
            p[hkv, t, half], _ = _softmax(s_head, bias_ref[head], sink_ref[l, head])
        for hkv in range(2):
            o2 = jnp.zeros((2 * BLK, 128), F32)
            for half in range(2):
                o2 = o2 + jnp.dot(_spread_pair(p, hkv, half, tri), v_var[hkv][half], preferred_element_type=F32)
            for t in range(2):
                lo = (2 * hkv + t) * 128
                gate = ag_ref[:, lo:lo + 128]
                cat_ref[:, D_POOL + lo:D_POOL + lo + 128] = (_rows(o2, t) * (gate * _sigmoid(gate))).astype(BF16)

    blk = lambda w: pl.BlockSpec((BLK, w), lambda i: (i, 0))
    prev = lambda w: pl.BlockSpec((BLK, w), lambda i: (jnp.maximum(i - 1, 0), 0))
    halo = pl.BlockSpec((HALO, 512), lambda i: (jnp.maximum(i * (BLK // HALO) - 1, 0), 0))
    return pl.pallas_call(
        body, name="fwd_mix", grid=(NB,),
        in_specs=[blk(512), halo, blk(512), blk(512), blk(256), prev(256), blk(512),
                  _layer(l, 4, 128, 128), _layer(l, 1, 512), pl.BlockSpec(memory_space=pltpu.SMEM),
                  pl.BlockSpec((None, N_HEADS, BLK, BLK), lambda i: (jnp.minimum(i, 1), 0, 0, 0)), _whole((BLK, BLK))],
        out_specs=blk(D),
        out_shape=jax.ShapeDtypeStruct((S, D), BF16),
        scratch_shapes=[pltpu.VMEM((POOL_ROWS, 512), F32)] * 3,
        compiler_params=_params(),
    )(pu, pu, pg, q, kv, kv, ag, pool_w, pool_scale, sinks, bias, tri)


def _store_lane_rows(ref, acc):
    total = jnp.sum(acc, axis=0, keepdims=True)
    for k in range(ref.shape[0]):
        ref[k:k + 1, :] = total[:, k * 128:(k + 1) * 128]


def _own_piece(dw_ref, place_ref):
    p = dw_ref.shape[0] // 8
    return dw_ref[pl.ds(pl.multiple_of(place_ref[0] * p, 8), p), :]


def _bwd_out(l, cat, w_out, g_post, place_arr, dxn=None, y=None, x=None, target=None, deps=()):
    last = target is not None
    n_steps = S // TM

    def body(a_ref, b_ref, g_ref, cat_ref, w_ref, place_ref, *rest):
        dcat_ref, own_ref, dwb_ref, dg_ref = rest[len(deps):len(deps) + 4]
        rest = rest[len(deps) + 4:]
        acc_ref, dw_ref = rest[-2:]
        step = pl.program_id(0)

        @pl.when(step == 0)
        def _():
            dw_ref[...] = jnp.zeros_like(dw_ref)
            acc_ref[...] = jnp.zeros_like(acc_ref)

        cat = cat_ref[...]
        g = g_ref[...]
        y = jnp.dot(cat, w_ref[...], preferred_element_type=F32) if last else b_ref[...]
        r = lax.rsqrt(jnp.mean(y * y, axis=-1, keepdims=True) + EPS)
        if last:
            loss_ref, dx_ref, loss_acc_ref = rest[:3]
            err = a_ref[...] + y * r * g - b_ref[...]

            @pl.when(step == 0)
            def _():
                loss_acc_ref[...] = jnp.zeros_like(loss_acc_ref)

            loss_acc_ref[...] += _rows8(err * err)
            dz = err * (1.0 / D)
            dx_ref[...] = dz
        else:
            dz = a_ref[...]
        a = dz * g
        dy = r * a - y * (r * r * r) * jnp.mean(a * y, axis=-1, keepdims=True)
        acc_ref[...] += _rows8(dz * (y * r))
        dyb = dy.astype(BF16)
        dcat_ref[...] = lax.dot_general(dyb, w_ref[...], NT, preferred_element_type=F32)
        dw_ref[...] += lax.dot_general(cat, dyb, TN, preferred_element_type=F32)

        @pl.when(step == n_steps - 1)
        def _():
            _store_lane_rows(dg_ref, acc_ref[...])
            dwb_ref[...] = dw_ref[...].astype(BF16)
            own_ref[...] = _own_piece(dw_ref, place_ref)
            if last:
                loss_ref[...] = jnp.full((8, 128), (0.5 / D) * jnp.sum(loss_acc_ref[...]), F32)

    row = lambda: pl.BlockSpec((TM, D), lambda i: (i, 0))
    full = _whole
    return pl.pallas_call(
        body, name="out_loss_bwd" if last else "bwd_out", grid=(n_steps,),
        in_specs=[row(), row(), _layer(l, 1, D), row(), full((D, D)), pl.BlockSpec(memory_space=pltpu.SMEM)]
        + [ANY] * len(deps),
        out_specs=[row(), full((D // 8, D)), full((D, D)), full((8, 128))] + ([full((8, 128)), row()] if last else []),
        out_shape=[jax.ShapeDtypeStruct((S, D), F32), jax.ShapeDtypeStruct((D // 8, D), F32),
                   jax.ShapeDtypeStruct((D, D), BF16), jax.ShapeDtypeStruct((8, 128), F32)]
        + ([jax.ShapeDtypeStruct((8, 128), F32), jax.ShapeDtypeStruct((S, D), F32)] if last else []),
        scratch_shapes=([pltpu.VMEM((8, D), F32)] if last else []) + [pltpu.VMEM((8, D), F32), pltpu.VMEM((D, D), F32)],
        compiler_params=_params(),
    )(*((x, target) if last else (dxn, y)), g_post, cat, w_out, place_arr, *deps)


def _bwd_mix(l, pu, pg, q, kv, ag, dcat, pool_w, pool_scale, sinks, tables, deps=(), dpw_dest=None):
    bias, tri = tables
    deps = tuple(deps) + (() if dpw_dest is None else (dpw_dest,))

    def body(pu_ref, pup_ref, pg_ref, q_ref, kv_ref, kvp_ref, ag_ref, dcat_ref, pw_ref, sc_ref, sink_ref, bias_ref,
             tri_ref, *rest):
        dproj_ref, dpw_ref, dsc_ref, dsink_ref, ext_ref, dext_ref, tmp_a, tmp_b, dkv_ref = rest[len(deps):]
        tmp_refs = (tmp_a, tmp_b)
        step = pl.program_id(0)
        i = NB - 1 - step

        @pl.when(step == 0)
        def _():
            dpw_ref[...] = jnp.zeros_like(dpw_ref)
            dsc_ref[...] = jnp.zeros_like(dsc_ref)
            dsink_ref[...] = jnp.zeros_like(dsink_ref)
            for ref in (ext_ref, tmp_a, tmp_b):
                ref[0:PAD, :] = jnp.zeros((PAD, 512), F32)
            dext_ref[BLK:POOL_ROWS, :] = jnp.zeros((HALO + PAD, 512), F32)
            dkv_ref[...] = jnp.zeros_like(dkv_ref)

        ext_ref[PAD:PAD + HALO, :] = jnp.where(i > 0, pup_ref[...], 0.0)
        ext_ref[PAD + HALO:POOL_ROWS, :] = pu_ref[...]
        _window_sums(ext_ref, tmp_refs, True)
        dpooled = []
        for g, w in enumerate(POOL_WINDOWS):
            lanes = slice(g * 128, (g + 1) * 128)
            pooled, inv = _pool_block(ext_ref, tmp_refs, i, g, w)
            pooled_b = pooled.astype(BF16)
            mixed = jnp.dot(pooled_b, pw_ref[g], preferred_element_type=F32)
            scale = sc_ref[:, lanes]
            gate = pg_ref[:, lanes]
            sg = _sigmoid(gate)
            dpo = dcat_ref[:, lanes]
            dproj_ref[:, C_PG + g * 128:C_PG + (g + 1) * 128] = (
                dpo * (mixed * scale) * (sg * (1.0 + gate * (1.0 - sg)))).astype(BF16)
            dms = dpo * (gate * sg)
            dsc_ref[g:g + 1, :] += jnp.sum(dms * mixed, axis=0, keepdims=True)
            dmixed = (dms * scale).astype(BF16)
            dpw_ref[g] += lax.dot_general(pooled_b, dmixed, TN, preferred_element_type=F32)
            dpooled.append(lax.dot_general(dmixed, pw_ref[g], NT, preferred_element_type=F32))
            dext_ref[0:BLK, lanes] = dpooled[g] * inv
        _window_sums(dext_ref, tmp_refs, False)
        for g in range(len(POOL_WINDOWS)):
            lanes = slice(g * 128, (g + 1) * 128)
            dproj_ref[:, C_PU + g * 128:C_PU + (g + 1) * 128] = (tmp_refs[g % 2][0:BLK, lanes] - dpooled[g]).astype(BF16)
        dext_ref[BLK:BLK + HALO, :] = dext_ref[0:HALO, :]

        own = _own_block_mask()
        tri = tri_ref[...]
        k_var = _head_variants(kv_ref[:, 0:128], kvp_ref[:, 0:128])
        v_var = _head_variants(kv_ref[:, 128:256], kvp_ref[:, 128:256])
        q2 = [_stack_tiles(q_ref, hkv) for hkv in range(2)]
        s = _scores(q2, k_var, own)
        p, p_sink = {}, {}
        for key, s_head in s.items():
            head = _head_of(*key)
            p[key], p_sink[key] = _softmax(s_head, bias_ref[head], sink_ref[l, head])

        do2, p_b, dp = [], {}, {}
        for hkv in range(2):
            gate = _stack_tiles(ag_ref, hkv)
            sg = _sigmoid(gate)
            dca = _stack_tiles(dcat_ref, hkv, D_POOL)
            do2.append((dca * (gate * sg)).astype(BF16))
            o2 = jnp.zeros((2 * BLK, 128), F32)
            for half in range(2):
                p_b[hkv, half] = _spread_pair(p, hkv, half, tri)
                o2 = o2 + jnp.dot(p_b[hkv, half], v_var[hkv][half], preferred_element_type=F32)
                full = lax.dot_general(do2[hkv], v_var[hkv][half], NT, preferred_element_type=F32)
                for t in range(2):
                    dp[hkv, t, half] = _merge(_rows(full, t), own)
            dag = dca * o2 * (sg * (1.0 + gate * (1.0 - sg)))
            for t in range(2):
                lo = C_AG + (2 * hkv + t) * 128
                dproj_ref[:, lo:lo + 128] = _rows(dag, t).astype(BF16)

        ds = {}
        for key in p:
            delta = jnp.sum(p[key] * dp[key], axis=-1, keepdims=True)
            ds[key] = p[key] * (dp[key] - delta)
            head = _head_of(*key)
            dsink_ref[0:1, :] += jnp.where(lax.broadcasted_iota(jnp.int32, (1, 128), 1) == head,
                                           -jnp.sum(p_sink[key] * delta, axis=0, keepdims=True), 0.0)

        dk_acc = [[None, None], [None, None]]
        dv_acc = [[None, None], [None, None]]
        for hkv in range(2):
            dq2 = jnp.zeros((2 * BLK, 128), F32)
            for half in range(2):
                ds_b = _spread_pair(ds, hkv, half, tri)
                dq2 = dq2 + jnp.dot(ds_b, k_var[hkv][half], preferred_element_type=F32)
                dk_acc[hkv][half] = lax.dot_general(ds_b, q2[hkv], TN, preferred_element_type=F32)
                dv_acc[hkv][half] = lax.dot_general(p_b[hkv, half], do2[hkv], TN, preferred_element_type=F32)
            for t in range(2):
                lo = C_Q + (2 * hkv + t) * 128
                dproj_ref[:, lo:lo + 128] = (_rows(dq2, t) * 0.125).astype(BF16)

        low = lax.broadcasted_iota(jnp.int32, (2 * BLK, 128), 1) < 64

        def gather_heads(acc):
            return jnp.where(low, acc[0][0] + pltpu.roll(acc[0][1], 64, axis=1),
                             pltpu.roll(acc[1][0], 64, axis=1) + acc[1][1])

        dk = gather_heads(dk_acc) * 0.125
        dv = gather_heads(dv_acc)
        dproj_ref[:, C_K:C_V] = (dk[BLK:, :] + dkv_ref[:, 0:128]).astype(BF16)
        dproj_ref[:, C_V:C_AG] = (dv[BLK:, :] + dkv_ref[:, 128:256]).astype(BF16)
        dkv_ref[:, 0:128] = dk[:BLK, :]
        dkv_ref[:, 128:256] = dv[:BLK, :]

    rev = lambda w: pl.BlockSpec((BLK, w), lambda s: (NB - 1 - s, 0))
    prev = lambda w: pl.BlockSpec((BLK, w), lambda s: (jnp.maximum(NB - 2 - s, 0), 0))
    halo = pl.BlockSpec((HALO, 512), lambda s: (jnp.maximum((NB - 1 - s) * (BLK // HALO) - 1, 0), 0))
    return pl.pallas_call(
        body, name="bwd_mix", grid=(NB,),
        in_specs=[rev(512), halo, rev(512), rev(512), rev(256), prev(256), rev(512), rev(D),
                  _layer(l, 4, 128, 128), _layer(l, 1, 512), pl.BlockSpec(memory_space=pltpu.SMEM),
                  pl.BlockSpec((None, N_HEADS, BLK, BLK), lambda s: (jnp.minimum(NB - 1 - s, 1), 0, 0, 0)),
                  _whole((BLK, BLK))] + [ANY] * len(deps),
        out_specs=[rev(D_IN), _layer(l, 4, 128, 128),
                   pl.BlockSpec((4, 128), lambda s: (0, 0)), pl.BlockSpec((8, 128), lambda s: (0, 0))],
        out_shape=[jax.ShapeDtypeStruct((S, D_IN), BF16), jax.ShapeDtypeStruct((DEPTH, 4, 128, 128), F32),
                   jax.ShapeDtypeStruct((4, 128), F32), jax.ShapeDtypeStruct((8, 128), F32)],
        input_output_aliases={} if dpw_dest is None else {12 + len(deps): 1},
        scratch_shapes=[pltpu.VMEM((POOL_ROWS, 512), F32)] * 4 + [pltpu.VMEM((BLK, 256), F32)],
        compiler_params=_params(),
    )(pu, pu, pg, q, kv, kv, ag, dcat, pool_w, pool_scale, sinks, bias, tri, *deps)


def _bwd_in_dw(l, dproj, x, g_pre, place_arr, deps=()):
    n_steps = S // TM

    def body(dp_ref, x_ref, g_ref, place_ref, *rest):
        own_ref, dwb_ref, dw_ref = rest[len(deps):]
        step = pl.program_id(0)

        @pl.when(step == 0)
        def _():
            dw_ref[...] = jnp.zeros_like(dw_ref)

        xt = x_ref[...]
        r = lax.rsqrt(jnp.mean(xt * xt, axis=-1, keepdims=True) + EPS)
        h = (xt * r * g_ref[...]).astype(BF16)
        dw_ref[...] += lax.dot_general(dp_ref[...], h, TN, preferred_element_type=F32)

        @pl.when(step == n_steps - 1)
        def _():
            dwb_ref[...] = dw_ref[...].astype(BF16)
            own_ref[...] = _own_piece(dw_ref, place_ref)

    row = lambda w: pl.BlockSpec((TM, w), lambda i: (i, 0))
    full = _whole
    return pl.pallas_call(
        body, name="bwd_in_dw", grid=(n_steps,),
        in_specs=[row(D_IN), row(D), _layer(l, 1, D), pl.BlockSpec(memory_space=pltpu.SMEM)] + [ANY] * len(deps),
        out_specs=[full((D_IN // 8, D)), full((D_IN, D))],
        out_shape=[jax.ShapeDtypeStruct((D_IN // 8, D), F32), jax.ShapeDtypeStruct((D_IN, D), BF16)],
        scratch_shapes=[pltpu.VMEM((D_IN, D), F32)],
        compiler_params=_params(),
    )(dproj, x, g_pre, place_arr, *deps)


def _bwd_in_dx(l, dproj, w_in_t, x, g_pre, dres, deps=(), dw_place=None):
    n_steps = S // TM
    with_dw = dw_place is not None

    def body(dp_ref, w_ref, x_ref, g_ref, dres_ref, *rest):
        place_ref = rest[0] if with_dw else None
        rest = rest[with_dw + len(deps):]
        if with_dw:
            dx_ref, dg_ref, own_ref, dwb_ref, acc_ref, dw_ref = rest
        else:
            dx_ref, dg_ref, acc_ref = rest
        step = pl.program_id(0)

        @pl.when(step == 0)
        def _():
            acc_ref[...] = jnp.zeros_like(acc_ref)
            if with_dw:
                dw_ref[...] = jnp.zeros_like(dw_ref)

        g = g_ref[...]
        halves = [slice(k * (TM // 2), (k + 1) * (TM // 2)) for k in range(2)]
        dh = [jnp.dot(dp_ref[rows, :], w_ref[...], preferred_element_type=F32) for rows in halves]
        h = []
        for rows, dh_k in zip(halves, dh):
            xt = x_ref[rows, :]
            r = lax.rsqrt(jnp.mean(xt * xt, axis=-1, keepdims=True) + EPS)
            xn = xt * r
            acc_ref[...] += _rows8(dh_k * xn)
            a = dh_k * g
            dx_ref[rows, :] = dres_ref[rows, :] + (
                r * a - xt * (r * r * r) * jnp.mean(a * xt, axis=-1, keepdims=True))
            h.append((xn * g).astype(BF16))
        if with_dw:
            dw_ref[...] += lax.dot_general(dp_ref[...], jnp.concatenate(h, axis=0), TN, preferred_element_type=F32)

        @pl.when(step == n_steps - 1)
        def _():
            _store_lane_rows(dg_ref, acc_ref[...])
            if with_dw:
                dwb_ref[...] = dw_ref[...].astype(BF16)
                own_ref[...] = _own_piece(dw_ref, place_ref)

    row = lambda w: pl.BlockSpec((TM, w), lambda i: (i, 0))
    full = _whole
    dw_specs = [full((D_IN // 8, D)), full((D_IN, D))] if with_dw else []
    dw_shapes = [jax.ShapeDtypeStruct((D_IN // 8, D), F32), jax.ShapeDtypeStruct((D_IN, D), BF16)] if with_dw else []
    return pl.pallas_call(
        body, name="bwd_in" if with_dw else "bwd_in_dx", grid=(n_steps,),
        in_specs=[row(D_IN), full((D_IN, D)), row(D), _layer(l, 1, D), row(D)]
        + [pl.BlockSpec(memory_space=pltpu.SMEM)] * with_dw + [ANY] * len(deps),
        out_specs=[row(D), full((8, 128))] + dw_specs,
        out_shape=[jax.ShapeDtypeStruct((S, D), F32), jax.ShapeDtypeStruct((8, 128), F32)] + dw_shapes,
        scratch_shapes=[pltpu.VMEM((8, D), F32)] + [pltpu.VMEM((D_IN, D), F32)] * with_dw,
        compiler_params=_params(),
    )(dproj, w_in_t, x, g_pre, dres, *((dw_place,) if with_dw else ()), *deps)


HBM =pl.BlockSpec(memory_space=pltpu.HBM)
SEM = pl.BlockSpec(memory_space=pltpu.SEMAPHORE)
def _split_copy(collective_id=None):
    return pltpu.CompilerParams(has_side_effects=pltpu.SideEffectType.DATAFLOW_SIDE_EFFECTING,
                                collective_id=collective_id)


SPLIT_COPY = _split_copy()


def _in_hbm(a):
    return pltpu.with_memory_space_constraint(a, pltpu.HBM)

def _place():
    return lax.axis_index("x"), lax.axis_index("y"), lax.axis_index("c")


def _other_chips(x, y):
    return [(1 - x, y), (x, 1 - y), (1 - x, 1 - y)]


def _peer(x, y, c, m):
    return (x ^ (m >> 2), y ^ ((m >> 1) & 1), c ^ (m & 1))


SAME_CORE = (2, 4, 6)


def _place_cast(name, src, chip_arr, tile, layers, deps=()):
    _, n, cols = src.shape
    steps = n // tile
    k = len(layers)

    def body(chip_ref, *refs):
        for s_ref, o_ref in zip(refs[:k], refs[k + len(deps):]):
            o_ref[...] = s_ref[...].astype(BF16)

    def layer_spec(l):
        return pl.BlockSpec((None, tile, cols), lambda i, chip: (l, i, 0))

    return pl.pallas_call(
        body, name=name,
        grid_spec=pltpu.PrefetchScalarGridSpec(
            num_scalar_prefetch=1, grid=(steps,),
            in_specs=[layer_spec(l) for l in layers] + [ANY] * len(deps),
            out_specs=[pl.BlockSpec((tile, cols), lambda i, chip: (chip[0] * steps + i, 0))] * k),
        out_shape=[jax.ShapeDtypeStruct((N_SHARDS * n, cols), BF16)] * k,
        compiler_params=_params(),
    )(chip_arr, *[src] * k, *deps)


def _chip_rows(ref, chip, half=None):
    n = ref.shape[0] // N_SHARDS
    if half is None:
        return ref.at[pl.ds(pl.multiple_of(chip * n, 16), n), :]
    return ref.at[pl.ds(pl.multiple_of(chip * n + half * (n // 2), 16), n // 2), :]


def _gather_start(name, bufs, halved, collective_id):
    n = len(bufs)

    def body(*refs):
        ins, send, recv, token = refs[:n], refs[n:2 * n], refs[2 * n:3 * n], refs[-1]
        x, y, c = _place()
        _handshake([(*chip, c) for chip in _other_chips(x, y)])
        for a, buf in enumerate(ins):
            own = _chip_rows(buf, 2 * x + y, c if a in halved else None)
            for j, chip in enumerate(_other_chips(x, y)):
                pltpu.make_async_remote_copy(src_ref=own, dst_ref=own, send_sem=send[a].at[j], recv_sem=recv[a].at[j],
                                             device_id=(*chip, c), device_id_type=MESH).start()
        token[...] = jnp.zeros_like(token)

    outs = pl.pallas_call(
        body, name=name, in_specs=[HBM] * n,
        out_specs=[SEM] * (2 * n) + [HBM] * n + [pl.BlockSpec(memory_space=pltpu.VMEM)],
        out_shape=[pltpu.SemaphoreType.DMA((3,))] * (2 * n) + [pltpu.HBM(b.shape, b.dtype) for b in bufs]
        + [jax.ShapeDtypeStruct((8, 128), F32)],
        input_output_aliases={a: 2 * n + a for a in range(n)},
        compiler_params=_split_copy(collective_id),
    )(*[_in_hbm(b) for b in bufs])
    return outs[:n], outs[n:2 * n], outs[2 * n:3 * n], outs[-1]


def _gather_wait(name, buf, send_sem, recv_sem, after, halved=False):
    def body(buf_ref, send_ref, recv_ref, *rest):
        x, y, c = _place()
        half = c if halved else None
        own = _chip_rows(buf_ref, 2 * x + y, half)
        for j, chip in enumerate(_other_chips(x, y)):
            copy = pltpu.make_async_remote_copy(src_ref=own, dst_ref=_chip_rows(buf_ref, 2 * chip[0] + chip[1], half),
                                                send_sem=send_ref.at[j], recv_sem=recv_ref.at[j],
                                                device_id=(*chip, c), device_id_type=MESH)
            copy.wait_send()
            copy.wait_recv()

    return pl.pallas_call(
        body, name=name, in_specs=[HBM, SEM, SEM] + [ANY] * len(after), out_specs=HBM,
        out_shape=pltpu.HBM(buf.shape, buf.dtype), input_output_aliases={0: 0}, compiler_params=SPLIT_COPY,
    )(buf, send_sem, recv_sem, *after)


def _handshake(peers):
    barrier = pltpu.get_barrier_semaphore()
    for peer in peers:
        pl.semaphore_signal(barrier, inc=1, device_id=peer, device_id_type=MESH)
    pl.semaphore_wait(barrier, len(peers))


def _sibling_handshake(x, y, c):
    _handshake([(x, y, 1 - c)])


def _forward_halves(name, bufs, collective_id):
    n = len(bufs)

    def body(*refs):
        ins, outs, (send_sems, recv_sems) = refs[:n], refs[n:2 * n], refs[2 * n:]
        x, y, c = _place()
        _sibling_handshake(x, y, c)

        def copy(a, j, chip, half):
            rows = 2 * chip[0] + chip[1]
            return pltpu.make_async_remote_copy(
                src_ref=_chip_rows(ins[a], rows, half), dst_ref=_chip_rows(outs[a], rows, half),
                send_sem=send_sems.at[3 * a + j], recv_sem=recv_sems.at[3 * a + j], device_id=(x, y, 1 - c),
                device_id_type=MESH)

        copies = [(a, j, chip) for a in range(n) for j, chip in enumerate(_other_chips(x, y))]
        for a, j, chip in copies:
            copy(a, j, chip, c).start()
        for a, j, chip in copies:
            copy(a, j, chip, c).wait_send()
            copy(a, j, chip, 1 - c).wait_recv()

    return pl.pallas_call(
        body, name=name, in_specs=[ANY] * n, out_specs=[ANY] * n,
        out_shape=[jax.ShapeDtypeStruct(b.shape, b.dtype) for b in bufs],
        input_output_aliases={a: a for a in range(n)},
        scratch_shapes=[pltpu.SemaphoreType.DMA((3 * n,))] * 2,
        compiler_params=pltpu.CompilerParams(collective_id=collective_id),
    )(*bufs)


def _piece_rows(ref, k):
    p = ref.shape[0] // 8
    return ref.at[pl.ds(pl.multiple_of(k * p, 32 // jnp.dtype(ref.dtype).itemsize), p), :]


def _exchange_start(name, arrays, collective_id):
    n = len(arrays)
    zones = [lax.empty((7, a.shape[0] // 8, a.shape[1]), a.dtype) for a in arrays]

    def body(*refs):
        srcs, lands = refs[:n], refs[n:2 * n]
        send, recv, token = refs[2 * n:3 * n], refs[3 * n:4 * n], refs[-1]
        x, y, c = _place()
        _handshake([_peer(x, y, c, m) for m in range(1, 8)])
        for a, (src, land) in enumerate(zip(srcs, lands)):
            for m in range(1, 8):
                px, py, pc = _peer(x, y, c, m)
                pltpu.make_async_remote_copy(
                    src_ref=_piece_rows(src, 4 * px + 2 * py + pc), dst_ref=land.at[m - 1], send_sem=send[a].at[m - 1],
                    recv_sem=recv[a].at[m - 1], device_id=(px, py, pc), device_id_type=MESH).start()
        token[...] = jnp.zeros_like(token)

    outs = pl.pallas_call(
        body, name=name, in_specs=[HBM] * (2 * n),
        out_specs=[SEM] * (2 * n) + [HBM] * (2 * n) + [pl.BlockSpec(memory_space=pltpu.VMEM)],
        out_shape=[pltpu.SemaphoreType.DMA((7,))] * (2 * n) + [pltpu.HBM(a.shape, a.dtype) for a in arrays + zones]
        + [jax.ShapeDtypeStruct((8, 128), F32)],
        input_output_aliases={a: 2 * n + a for a in range(2 * n)},
        compiler_params=_split_copy(collective_id),
    )(*[_in_hbm(a) for a in arrays + zones])
    return outs[:n], outs[n:2 * n], outs[2 * n:3 * n], outs[3 * n:4 * n], outs[-1]


def _exchange_wait(name, started, after, which=None, with_sent=False):
    which = range(len(started[2])) if which is None else which
    send_sems, recv_sems, arrays, zones = [[group[k] for k in which] for group in started[:4]]
    n = len(arrays)

    def body(*refs):
        srcs, lands = refs[:n], refs[n:2 * n]
        send, recv = refs[2 * n:3 * n], refs[3 * n:4 * n]
        x, y, c = _place()
        for a, (src, land) in enumerate(zip(srcs, lands)):
            for m in range(1, 8):
                px, py, pc = _peer(x, y, c, m)
                copy = pltpu.make_async_remote_copy(
                    src_ref=_piece_rows(src, 4 * px + 2 * py + pc), dst_ref=land.at[m - 1], send_sem=send[a].at[m - 1],
                    recv_sem=recv[a].at[m - 1], device_id=(px, py, pc), device_id_type=MESH)
                copy.wait_send()
                copy.wait_recv()

    outs = pl.pallas_call(
        body, name=name, in_specs=[HBM] * (2 * n) + [SEM] * (2 * n) + [ANY], out_specs=[HBM] * (2 * n),
        out_shape=[pltpu.HBM(a.shape, a.dtype) for a in list(arrays) + list(zones)],
        input_output_aliases={a: a for a in range(2 * n)}, compiler_params=SPLIT_COPY,
    )(*arrays, *zones, *send_sems, *recv_sems, after)
    return outs if with_sent else outs[n:]


def _sum_pieces(name, weights, place_arr, dests=None, small=()):
    steps = 2
    flat = [item for items in weights for item in items]
    n = len(flat)

    def body(place_ref, *refs):
        first_out = len(refs) - len(weights) - len(small)
        outs = iter(refs[first_out:])
        small_refs = refs[first_out - 2 * len(small):first_out]

        @pl.when(pl.program_id(0) == 0)
        def _():
            for j in range(len(small)):
                total = small_refs[2 * j][...]
                for m in range(7):
                    total = total + small_refs[2 * j + 1][m]
                refs[first_out + len(weights) + j][...] = total

        k = 0
        for items in weights:
            out_ref = next(outs)
            for layer, _, _ in items:
                total = refs[k][...]
                for m in range(7):
                    total = total + refs[n + k][m].astype(F32)
                if len(items) == DEPTH:
                    out_ref[layer] = total
                else:
                    out_ref[...] = total
                k += 1

    def out_spec(items):
        _, own, _ = items[0]
        t, cols = own.shape[0] // steps, own.shape[1]
        if len(items) == DEPTH:
            return pl.BlockSpec((DEPTH, t, cols), lambda i, place: (0, place[1] * steps + i, 0))
        layer = items[0][0]
        return pl.BlockSpec((None, t, cols), lambda i, place: (layer, place[1] * steps + i, 0))

    owns = [own for _, own, _ in flat]
    dests = [] if dests is None else list(dests)
    piece = lambda a: pl.BlockSpec((a.shape[0] // 8, a.shape[1]), lambda i, place: (place[0], 0))
    small_specs = [spec for a, r in small for spec in (piece(a), pl.BlockSpec(r.shape, lambda i, place: (0, 0, 0)))]
    return pl.pallas_call(
        body, name=name,
        grid_spec=pltpu.PrefetchScalarGridSpec(
            num_scalar_prefetch=1, grid=(steps,),
            in_specs=[pl.BlockSpec((o.shape[0] // steps, o.shape[1]), lambda i, place: (i, 0)) for o in owns]
            + [pl.BlockSpec((7, o.shape[0] // steps, o.shape[1]), lambda i, place: (0, i, 0)) for o in owns]
            + [ANY] * len(dests) + small_specs,
            out_specs=[out_spec(items) for items in weights] + [piece(a) for a, _ in small]),
        out_shape=[jax.ShapeDtypeStruct((DEPTH, 2 * items[0][1].shape[0], items[0][1].shape[1]), F32)
                   for items in weights] + [jax.ShapeDtypeStruct(a.shape, F32) for a, _ in small],
        input_output_aliases={1 + 2 * n + k: k for k in range(len(dests))},
        compiler_params=_params(),
    )(place_arr, *owns, *[recv for _, _, recv in flat], *dests, *[a for pair in small for a in pair])


def _sum_small(name, partials, recvs, place_arr):
    n = len(partials)

    def body(place_ref, *refs):
        for o_ref, r_ref, out_ref in zip(refs[:n], refs[n:2 * n], refs[2 * n:]):
            total = o_ref[...]
            for m in range(7):
                total = total + r_ref[m]
            out_ref[...] = total

    piece = lambda a: pl.BlockSpec((a.shape[0] // 8, a.shape[1]), lambda i, place: (place[0], 0))
    return pl.pallas_call(
        body, name=name,
        grid_spec=pltpu.PrefetchScalarGridSpec(
            num_scalar_prefetch=1, grid=(1,),
            in_specs=[piece(a) for a in partials] + [pl.BlockSpec(r.shape, lambda i, place: (0, 0, 0)) for r in recvs],
            out_specs=[piece(a) for a in partials]),
        out_shape=[jax.ShapeDtypeStruct(a.shape, F32) for a in partials],
        compiler_params=_params(),
    )(place_arr, *partials, *recvs)


def _share(name, bufs, parts, gathered=(), collective_id=None):
    n, n_g = len(bufs), len(gathered)
    total = n + n_g

    def body(*refs):
        ins, outs = refs[:total], refs[total:2 * total]
        send_sems, recv_sems, send_g, recv_g = refs[2 * total:]
        x, y, c = _place()
        _handshake([_peer(x, y, c, m) for m in (SAME_CORE if gathered else ()) + (1,)])

        def half(ref, l, which):
            p = ref.shape[1] // 2
            return ref.at[l, pl.ds(pl.multiple_of(which * p, 8), p), :]

        def swap(k, which):
            a, l = parts[k]
            return pltpu.make_async_remote_copy(
                src_ref=half(ins[a], l, which), dst_ref=half(outs[a], l, which), send_sem=send_sems.at[k],
                recv_sem=recv_sems.at[k], device_id=(x, y, 1 - c), device_id_type=MESH)

        def spread(a, m, sender, held, to):
            k = 4 * sender[0] + 2 * sender[1] + sender[2]
            return pltpu.make_async_remote_copy(
                src_ref=_piece_rows(held[n + a], k), dst_ref=_piece_rows(outs[n + a], k),
                send_sem=send_g.at[7 * a + m - 1], recv_sem=recv_g.at[7 * a + m - 1], device_id=to, device_id_type=MESH)

        me, sibling = (x, y, c), (x, y, 1 - c)
        for k in range(len(parts)):
            swap(k, c).start()
        def own(a, m):
            return spread(a, m, me, ins, _peer(x, y, c, m))

        def handed_on(a, m):
            return spread(a, m + 1, _peer(x, y, c, m), outs, sibling)

        for a in range(n_g):
            for m in SAME_CORE + (1,):
                own(a, m).start()
        for a in range(n_g):
            for m in SAME_CORE:
                spread(a, m, _peer(x, y, c, m), ins, _peer(x, y, c, m)).wait_recv()
                handed_on(a, m).start()
        for k in range(len(parts)):
            swap(k, c).wait_send()
            swap(k, 1 - c).wait_recv()
        for a in range(n_g):
            for m in SAME_CORE + (1,):
                own(a, m).wait_send()
            for m in SAME_CORE:
                handed_on(a, m).wait_send()
                spread(a, m + 1, _peer(x, y, c, m + 1), ins, sibling).wait_recv()
            spread(a, 1, sibling, ins, sibling).wait_recv()

    arrays = list(bufs) + list(gathered)
    return pl.pallas_call(
        body, name=name, in_specs=[ANY] * total, out_specs=[ANY] * total,
        out_shape=[jax.ShapeDtypeStruct(b.shape, F32) for b in arrays],
        input_output_aliases={a: a for a in range(total)},
        scratch_shapes=[pltpu.SemaphoreType.DMA((max(len(parts), 1),))] * 2
        + [pltpu.SemaphoreType.DMA((max(7 * n_g, 1),))] * 2,
        compiler_params=pltpu.CompilerParams(collective_id=collective_id),
    )(*arrays)


def _adamw_math(w, g, m, v):
    nm = ADAM_B1 * m + (1.0 - ADAM_B1) * g
    nv = ADAM_B2 * v + (1.0 - ADAM_B2) * (g * g)
    m_hat = nm / (1.0 - ADAM_B1 ** ADAM_STEP)
    v_hat = nv / (1.0 - ADAM_B2 ** ADAM_STEP)
    return -ADAM_LR * (m_hat / (jnp.sqrt(v_hat) + ADAM_EPS) + ADAM_WD * w), nm, nv


def _adamw(name, w, g, m, v, rows_per_step, first=0, count=None, dests=None, deps=()):
    layers, rows, cols = w.shape
    count = layers if count is None else count

    def body(w_ref, g_ref, m_ref, v_ref, *rest):
        d_ref, nm_ref, nv_ref, g_out_ref = rest[-4:]
        d_ref[...], nm_ref[...], nv_ref[...] = _adamw_math(w_ref[...], g_ref[...], m_ref[...], v_ref[...])
        g_out_ref[...] = g_ref[...]

    spec = pl.BlockSpec((1, rows_per_step, cols), lambda l, i: (first + l, i, 0))
    shape = jax.ShapeDtypeStruct(w.shape, F32)
    dests = () if dests is None else tuple(dests)
    return pl.pallas_call(
        body, name=name, grid=(count, rows // rows_per_step),
        in_specs=[spec] * 4 + [ANY] * (len(dests) + len(deps)), out_specs=[spec] * 4, out_shape=[shape] * 4,
        input_output_aliases={4 + k: k for k in range(len(dests))},
        compiler_params=_params(("arbitrary", "arbitrary")),
    )(w, g, m, v, *dests, *deps)


def _pack_misc(pool_scale, sinks, norm_pre, norm_post):
    sink_rows = jnp.zeros((DEPTH, 8, 128), F32).at[:, 0, 0:N_HEADS].set(sinks).reshape(2 * 8, 128)
    return jnp.concatenate([pool_scale.reshape(8, 128), norm_pre.reshape(16, 128), norm_post.reshape(16, 128),
                            sink_rows, jnp.zeros((8, 128), F32)], axis=0)


def _adamw_small(w, g, m, v, pool):
    def body(w_ref, g_ref, m_ref, v_ref, pw_ref, pg_ref, pm_ref, pv_ref, *rest):
        outs, pool_outs, (d_ref, nm_ref, nv_ref) = rest[:17], rest[17:21], rest[21:]
        pool_outs[0][...] = pg_ref[...]
        pool_outs[1][...], pool_outs[2][...], pool_outs[3][...] = _adamw_math(
            pw_ref[...], pg_ref[...], pm_ref[...], pv_ref[...])
        d_ref[...], nm_ref[...], nv_ref[...] = _adamw_math(w_ref[...], g_ref[...], m_ref[...], v_ref[...])
        for k, src in enumerate([g_ref, d_ref, nm_ref, nv_ref]):
            scale, sinks, pre, post = outs[4 * k:4 * k + 4]
            for l in range(DEPTH):
                for j in range(4):
                    scale[l:l + 1, j * 128:(j + 1) * 128] = src[MISC_SCALE + 4 * l + j:MISC_SCALE + 4 * l + j + 1, :]
                for j in range(8):
                    pre[l:l + 1, j * 128:(j + 1) * 128] = src[MISC_PRE + 8 * l + j:MISC_PRE + 8 * l + j + 1, :]
                    post[l:l + 1, j * 128:(j + 1) * 128] = src[MISC_POST + 8 * l + j:MISC_POST + 8 * l + j + 1, :]
                sinks[l:l + 1, :] = src[MISC_SINKS + 8 * l:MISC_SINKS + 8 * l + 1, 0:N_HEADS]
        outs[16][...] = g_ref[MISC_LOSS:MISC_LOSS + 1, 0:1]

    vmem = pl.BlockSpec(memory_space=pltpu.VMEM)
    shapes = [(DEPTH, D_POOL), (DEPTH, N_HEADS), (DEPTH, D), (DEPTH, D)] * 4 + [(1, 1)]
    shapes += [pool[0].shape] * 4
    return pl.pallas_call(
        body, name="adamw_small", in_specs=[vmem] * 8, out_specs=[vmem] * 21,
        out_shape=[jax.ShapeDtypeStruct(s, F32) for s in shapes],
        scratch_shapes=[pltpu.VMEM((MISC_ROWS, 128), F32)] * 3,
    )(w, g, m, v, *pool)


def kernel(x, w_in, pool_w, pool_scale, attn_sinks, w_out, norm_pre, norm_post, loss_target, m_w_in, m_pool_w, m_pool_scale, m_attn_sinks, m_w_out, m_norm_pre, m_norm_post, v_w_in, v_pool_w, v_pool_scale, v_attn_sinks, v_w_out, v_norm_pre, v_norm_post):
    cx, cy, cc = _place()
    chip_arr = jnp.reshape(2 * cx + cy, (1,)).astype(jnp.int32)
    place_arr = jnp.stack([4 * cx + 2 * cy + cc, cc]).astype(jnp.int32)
    t = lambda a: jnp.transpose(a, (0, 2, 1))
    w_in_t = t(w_in)
    xs, target = x[0], loss_target[0]
    pool_w_b = pool_w.astype(BF16)
    tables = _attention_tables()
    scale3 = pool_scale.reshape(DEPTH, 1, D_POOL)
    pre3 = norm_pre.reshape(DEPTH, 1, D)
    post3 = norm_post.reshape(DEPTH, 1, D)

    (wi0,) = _place_cast("place_w_in0", w_in_t, chip_arr, 288, [0])
    first = _gather_start("gather_start_first", [wi0], halved=(0,), collective_id=ID_GATHER_FIRST)
    (wi1,) = _place_cast("place_w_in1", w_in_t, chip_arr, 288, [1], deps=(first[3],))
    wo = _place_cast("place_w_out", w_out, chip_arr, 256, [0, 1], deps=(first[3],))
    rest = _gather_start("gather_start_rest", [wi1, wo[0], wo[1]], halved=(0, 1), collective_id=ID_GATHER_REST)
    send, recv, bufs = [first[k] + rest[k] for k in range(3)]
    order = {(0, "in"): 0, (1, "in"): 1, (0, "out"): 2, (1, "out"): 3}

    saved = []
    packed = [_pack_misc(pool_scale, attn_sinks, norm_pre, norm_post),
              _pack_misc(m_pool_scale, m_attn_sinks, m_norm_pre, m_norm_post),
              _pack_misc(v_pool_scale, v_attn_sinks, v_norm_pre, v_norm_post)]
    after = (first[3], rest[3], pool_w_b, *tables, scale3, pre3, post3, *packed)
    below = None
    for l in range(DEPTH):
        k = order[l, "in"]
        halves = [_gather_wait(f"gather_wait_in{l}", bufs[k], send[k], recv[k], after, halved=True)]
        if below is not None:
            halves.append(below[1])
        w_in_l, *w_out_below = _forward_halves(f"forward_w{l}", halves, collective_id=ID_FORWARD[l])
        if below is None:
            pu, pg, q, kv, ag = _fwd_in(l, xs, pre3, w_in_l)
        else:
            saved[l - 1][9] = w_out_below[0]
            y, xs, pu, pg, q, kv, ag = _fwd_in(l, xs, pre3, w_in_l, (below[0], w_out_below[0], below[2]))
            saved[l - 1][7] = y
        cat = _fwd_mix(l, pu, pg, q, kv, ag, pool_w_b, scale3, attn_sinks, tables)
        k = order[l, "out"]
        w_out_l = _gather_wait(f"gather_wait_out{l}", bufs[k], send[k], recv[k], (cat,), halved=l + 1 < DEPTH)
        saved.append([xs, pu, pg, q, kv, ag, cat, None, w_in_l, w_out_l])
        below, after = (cat, w_out_l, post3), (w_out_l,)

    x_in, pu, pg, q, kv, ag, cat, y, w_in_l, w_out_l = saved[1]
    dcat, dw_out1, dw_out1_b, dg_post1, loss, xs = _bwd_out(1, cat, w_out_l, post3, place_arr, x=x_in, target=target)
    ex1_out = _exchange_start("exchange_start_out1", [dw_out1_b], ID_OUT1)
    dproj, dpw, dsc1, dsink1 = _bwd_mix(1, pu, pg, q, kv, ag, dcat, pool_w_b, scale3, attn_sinks, tables,
                                        deps=(ex1_out[4],))
    dx, dg_pre1, dw_in1, dw_in1_b = _bwd_in_dx(1, dproj, w_in_l, x_in, pre3, xs, dw_place=place_arr)
    ex1_in = _exchange_start("exchange_start_in1", [dw_in1_b], ID_IN1)

    x_in, pu, pg, q, kv, ag, cat, y, w_in_l, w_out_l = saved[0]
    dcat, dw_out0, dw_out0_b, dg_post0 = _bwd_out(0, cat, w_out_l, post3, place_arr, dxn=dx, y=y, deps=(ex1_in[4],))
    ex0_out = _exchange_start("exchange_start_out0", [dw_out0_b], ID_OUT0)
    dproj, dpw, dsc0, dsink0 = _bwd_mix(0, pu, pg, q, kv, ag, dcat, pool_w_b, scale3, attn_sinks, tables,
                                        deps=(ex0_out[4],), dpw_dest=dpw)
    dw_in0, dw_in0_b = _bwd_in_dw(0, dproj, x_in, pre3, place_arr)
    flat = lambda a: a.reshape(DEPTH * 4 * 128, 128)
    ex0_in = _exchange_start("exchange_start_in0", [flat(dpw), dw_in0_b], ID_IN0)

    grad_x, dg_pre0 = _bwd_in_dx(0, dproj, w_in_l, x_in, pre3, dx, deps=(ex0_in[4],))
    small = [jnp.concatenate([dsc0, dsc1, dg_pre0, dg_pre1, dg_post0, dg_post1, dsink0, dsink1, loss], axis=0)]
    ex_small = _exchange_start("exchange_start_small", small, ID_SMALL)
    (recv_out1,) = _exchange_wait("exchange_wait_out1", ex1_out, ex_small[4])
    (recv_in1,) = _exchange_wait("exchange_wait_in1", ex1_in, recv_out1)
    g_in, g_out = _sum_pieces("sum_pieces_1", [[(1, dw_in1, recv_in1)], [(1, dw_out1, recv_out1)]], place_arr)
    (recv_out0,) = _exchange_wait("exchange_wait_out0", ex0_out, g_out)
    (g_out,) = _sum_pieces("sum_pieces_out0", [[(0, dw_out0, recv_out0)]], place_arr, dests=[g_out])
    g_in, g_out = _share("share_a", [g_in, g_out], [(0, 1), (1, 0), (1, 1)], collective_id=ID_SHARE_A)
    m_in_t, v_in_t = t(m_w_in), t(v_w_in)
    d_out, nm_out, nv_out, grad_w_out = _adamw("adamw_w_out", w_out, g_out, m_w_out, v_w_out, 256)
    upd_in = _adamw("adamw_w_in1", w_in_t, g_in, m_in_t, v_in_t, 288, first=1, count=1, deps=(d_out,))
    dpw_own, recv_pw = _exchange_wait("exchange_wait_pool", ex0_in, upd_in[0], which=[0], with_sent=True)
    (g_pw,) = _sum_small("sum_pool", [dpw_own], [recv_pw], place_arr)

    (recv_in0,) = _exchange_wait("exchange_wait_in0", ex0_in, g_pw, which=[1])
    (recv_misc,) = _exchange_wait("exchange_wait_small", ex_small, recv_in0)
    g_in, g_misc = _sum_pieces("sum_pieces_in0", [[(0, dw_in0, recv_in0)]], place_arr, dests=[g_in],
                               small=[(small[0], recv_misc)])
    g_in, g_pw, g_misc = _share("share_b", [g_in], [(0, 0)], [g_pw, g_misc], collective_id=ID_SHARE_B)
    d_in, nm_in, nv_in, grad_w_in_t = _adamw("adamw_w_in0", w_in_t, g_in, m_in_t, v_in_t, 288, first=0, count=1,
                                             dests=upd_in)
    small_out = _adamw_small(packed[0], g_misc, packed[1], packed[2],
                             (flat(pool_w), g_pw, flat(m_pool_w), flat(v_pool_w)))
    (g_sc, g_sk, g_pre, g_post, d_sc, d_sk, d_pre, d_post,
     m_sc, m_sk, m_pre, m_post, v_sc, v_sk, v_pre, v_post, loss_sum) = small_out[:17]
    g_pw, d_pw, m_pw, v_pw = [a.reshape(pool_w.shape) for a in small_out[17:]]
    return (loss_sum[0, 0], grad_x[None], t(grad_w_in_t), g_pw, g_sc, g_sk, grad_w_out, g_pre, g_post,
            t(d_in), d_pw, d_sc, d_sk, d_out, d_pre, d_post,
            t(nm_in), m_pw, m_sc, m_sk, nm_out, m_pre, m_post,
            t(nv_in), v_pw, v_sc, v_sk, nv_out, v_pre, v_post)
```

```python
import jax
import jax.numpy as jnp
from jax import lax
from jax.experimental import pallas as pl
from jax.experimental.pallas import tpu as pltpu

F32 = jnp.float32
BF16 = jnp.bfloat16

S = 2048
D = 1024
DEPTH = 2
D_POOL = 512
POOL_WINDOWS = (2, 4, 8, 16)
N_HEADS = 8
D_IN = 2304
N_SHARDS = 4
W_IN_SHARD = D_IN // N_SHARDS
W_OUT_SHARD = D // N_SHARDS
BLK = 128
NB = S // BLK
HALO = 16
PAD = 8
EPS = 1e-6
NEG_INF = -1e30
C_PU, C_PG, C_Q, C_K, C_V, C_AG = 0, 512, 1024, 1536, 1664, 1792

ADAM_LR = 0.001
ADAM_B1 = 0.9
ADAM_B2 = 0.999
ADAM_EPS = 1e-08
ADAM_WD = 0.01
ADAM_STEP = 10

TM = 512
VMEM_LIMIT = 56 * 1024 * 1024

NT = (((1,), (1,)), ((), ()))
TN = (((0,), (0,)), ((), ()))

MESH = pl.DeviceIdType.MESH
ANY = pl.BlockSpec(memory_space=pl.ANY)

ID_FORWARD = (0, 1)
(ID_SHARE_A, ID_SHARE_B, ID_GATHER_FIRST, ID_GATHER_REST, ID_OUT1, ID_IN1, ID_OUT0, ID_IN0, ID_SMALL) = range(2, 11)

MISC_SCALE, MISC_PRE, MISC_POST, MISC_SINKS, MISC_LOSS = 0, 8, 24, 40, 56
MISC_ROWS = 64


def _params(sem=("arbitrary",)):
    return pltpu.CompilerParams(dimension_semantics=sem, vmem_limit_bytes=VMEM_LIMIT)


def _sigmoid(v):
    return 1.0 / (1.0 + jnp.exp(-v))


def _rows8(v):
    r, c = v.shape
    return v.reshape(r // 8, 8, c).sum(axis=0)


def _layer(l, *shape):
    zeros = (0,) * len(shape)
    return pl.BlockSpec((None,) + shape, lambda i: (l,) + zeros)


def _whole(shape):
    zeros = (0,) * len(shape)
    return pl.BlockSpec(shape, lambda i: zeros, pipeline_mode=pl.Buffered(1))


def _fwd_in(l, x, g_pre, w_in_t, below=None):
    fused = below is not None

    def body(x_ref, g_ref, w_ref, *rest):
        if fused:
            cat_ref, wo_ref, gp_ref, y_ref, xn_ref = rest[:5]
            y = jnp.dot(cat_ref[...], wo_ref[...], preferred_element_type=F32)
            y_ref[...] = y
            xt = x_ref[...] + y * lax.rsqrt(jnp.mean(y * y, axis=-1, keepdims=True) + EPS) * gp_ref[...]
            xn_ref[...] = xt
        else:
            xt = x_ref[...]
        pu_ref, pg_ref, q_ref, kv_ref, ag_ref = rest[-5:]
        r = lax.rsqrt(jnp.mean(xt * xt, axis=-1, keepdims=True) + EPS)
        h = (xt * r * g_ref[...]).astype(BF16)

        def proj(lo, hi):
            return lax.dot_general(h, w_ref[lo:hi, :], NT, preferred_element_type=F32)

        pu_ref[...] = proj(C_PU, C_PG)
        pg_ref[...] = proj(C_PG, C_Q)
        q_ref[...] = proj(C_Q, C_K).astype(BF16)
        kv_ref[...] = proj(C_K, C_AG).astype(BF16)
        ag_ref[...] = proj(C_AG, D_IN)

    row = lambda w: pl.BlockSpec((TM, w), lambda i: (i, 0))
    act = jax.ShapeDtypeStruct((S, D), F32)
    return pl.pallas_call(
        body, name="fwd_out_in" if fused else "fwd_in", grid=(S // TM,),
        in_specs=[row(D), _layer(l, 1, D), _whole((D_IN, D))]
        + ([row(D), _whole((D, D)), _layer(l - 1, 1, D)] if fused else []),
        out_specs=[row(D)] * (2 * fused) + [row(512), row(512), row(512), row(256), row(512)],
        out_shape=[act] * (2 * fused)
        + [jax.ShapeDtypeStruct((S, 512), F32), jax.ShapeDtypeStruct((S, 512), F32),
           jax.ShapeDtypeStruct((S, 512), BF16), jax.ShapeDtypeStruct((S, 256), BF16),
           jax.ShapeDtypeStruct((S, 512), F32)],
        compiler_params=_params(),
    )(x, g_pre, w_in_t, *(below if fused else ()))


LOG2E = 1.4426950408889634
SCORE_SCALE = 0.125 * LOG2E


def _attention_tables():
    qi = jnp.arange(BLK)[:, None]
    kj = jnp.arange(BLK)[None, :]
    dist = ((qi - kj) % BLK).astype(F32)
    slopes = jnp.exp2(-jnp.arange(1, N_HEADS + 1, dtype=F32))
    bias = -(slopes * LOG2E)[:, None, None] * dist[None]
    first = jnp.where(kj > qi, NEG_INF, bias)
    return jnp.stack([first, bias]), (kj <= qi).astype(BF16)


def _own_block_mask():
    return lax.broadcasted_iota(jnp.int32, (BLK, BLK), 1) <= lax.broadcasted_iota(jnp.int32, (BLK, BLK), 0)


def _merge(full, own):
    return jnp.where(own, full[:, BLK:], full[:, :BLK])


def _spread(v, tri):
    own = v * tri
    return jnp.concatenate([v - own, own], axis=1)


def _head_variants(cur, prev):
    both = jnp.concatenate([prev, cur], axis=0).astype(F32)
    swapped = pltpu.roll(both, 64, axis=1)
    low = lax.broadcasted_iota(jnp.int32, both.shape, 1) < 64
    zero = jnp.zeros_like(both)
    return ((jnp.where(low, both, zero).astype(BF16), jnp.where(low, zero, swapped).astype(BF16)),
            (jnp.where(low, swapped, zero).astype(BF16), jnp.where(low, zero, both).astype(BF16)))


def _head_of(hkv, t, half):
    return hkv * 4 + 2 * t + half


def _rows(v, t):
    return v[t * BLK:(t + 1) * BLK]


def _stack_tiles(ref, hkv, offset=0):
    lo = offset + 2 * hkv * 128
    return jnp.concatenate([ref[:, lo:lo + 128], ref[:, lo + 128:lo + 256]], axis=0)


def _scores(q2, k_var, own):
    s = {}
    for hkv in range(2):
        for half in range(2):
            full = lax.dot_general(q2[hkv], k_var[hkv][half], NT, preferred_element_type=F32)
            for t in range(2):
                s[hkv, t, half] = _merge(_rows(full, t), own)
    return s


def _softmax(s, bias, sink):
    s = s * SCORE_SCALE + bias
    sink2 = sink * LOG2E
    m = jnp.maximum(jnp.max(s, axis=-1, keepdims=True), sink2)
    p = jnp.exp2(s - m)
    e_sink = jnp.exp2(sink2 - m)
    inv = 1.0 / (jnp.sum(p, axis=-1, keepdims=True) + e_sink)
    return p * inv, e_sink * inv


def _spread_pair(v, hkv, half, tri):
    return jnp.concatenate([_spread(v[hkv, t, half].astype(BF16), tri) for t in range(2)], axis=0)


POOL_ROWS = PAD + HALO + BLK


def _window_sums(src_ref, tmp_refs, trailing):
    lo, hi = (PAD, POOL_ROWS) if trailing else (0, HALO + BLK)
    cur = src_ref
    for level in range(len(POOL_WINDOWS)):
        lanes = slice(level * 128, 512)
        shift = -(1 << level) if trailing else (1 << level)
        dst = tmp_refs[level % 2]
        dst[lo:hi, lanes] = cur[lo:hi, lanes] + cur[lo + shift:hi + shift, lanes]
        cur = dst


def _pool_block(ext_ref, tmp_refs, i, g, w):
    lanes = slice(g * 128, (g + 1) * 128)
    rows = slice(PAD + HALO, POOL_ROWS)
    t = (i * BLK + lax.broadcasted_iota(jnp.int32, (BLK, 1), 0)).astype(F32)
    inv = 1.0 / jnp.minimum(t + 1.0, float(w))
    return tmp_refs[g % 2][rows, lanes] * inv - ext_ref[rows, lanes], inv


def _fwd_mix(l, pu, pg, q, kv, ag, pool_w, pool_scale, sinks, tables):
    bias, tri = tables

    def body(pu_ref, pup_ref, pg_ref, q_ref, kv_ref, kvp_ref, ag_ref, pw_ref, sc_ref, sink_ref, bias_ref, tri_ref,
             cat_ref, ext_ref, *tmp_refs):
        i = pl.program_id(0)

        @pl.when(i == 0)
        def _():
            for ref in (ext_ref, *tmp_refs):
                ref[0:PAD, :] = jnp.zeros((PAD, 512), F32)

        ext_ref[PAD:PAD + HALO, :] = jnp.where(i > 0, pup_ref[...], 0.0)
        ext_ref[PAD + HALO:POOL_ROWS, :] = pu_ref[...]
        _window_sums(ext_ref, tmp_refs, True)
        for g, w in enumerate(POOL_WINDOWS):
            lanes = slice(g * 128, (g + 1) * 128)
            pooled, _ = _pool_block(ext_ref, tmp_refs, i, g, w)
            mixed = jnp.dot(pooled.astype(BF16), pw_ref[g], preferred_element_type=F32)
            gate = pg_ref[:, lanes]
            cat_ref[:, lanes] = (mixed * sc_ref[:, lanes] * (gate * _sigmoid(gate))).astype(BF16)

        own = _own_block_mask()
        tri = tri_ref[...]
        k_var = _head_variants(kv_ref[:, 0:128], kvp_ref[:, 0:128])
        v_var = _head_variants(kv_ref[:, 128:256], kvp_ref[:, 128:256])
        s = _scores([_stack_tiles(q_ref, hkv) for hkv in range(2)], k_var, own)
        p = {}
        for (hkv, t, half), s_head in s.items():
            head = _head_of(hkv, t, half)
            p[hkv, t, half], _ = _softmax(s_head, bias_ref[head], sink_ref[l, head])
        for hkv in range(2):
            o2 = jnp.zeros((2 * BLK, 128), F32)
            for half in range(2):
                o2 = o2 + jnp.dot(_spread_pair(p, hkv, half, tri), v_var[hkv][half], preferred_element_type=F32)
            for t in range(2):
                lo = (2 * hkv + t) * 128
                gate = ag_ref[:, lo:lo + 128]
                cat_ref[:, D_POOL + lo:D_POOL + lo + 128] = (_rows(o2, t) * (gate * _sigmoid(gate))).astype(BF16)

    blk = lambda w: pl.BlockSpec((BLK, w), lambda i: (i, 0))
    prev = lambda w: pl.BlockSpec((BLK, w), lambda i: (jnp.maximum(i - 1, 0), 0))
    halo = pl.BlockSpec((HALO, 512), lambda i: (jnp.maximum(i * (BLK // HALO) - 1, 0), 0))
    return pl.pallas_call(
        body, name="fwd_mix", grid=(NB,),
        in_specs=[blk(512), halo, blk(512), blk(512), blk(256), prev(256), blk(512),
                  _layer(l, 4, 128, 128), _layer(l, 1, 512), pl.BlockSpec(memory_space=pltpu.SMEM),
                  pl.BlockSpec((None, N_HEADS, BLK, BLK), lambda i: (jnp.minimum(i, 1), 0, 0, 0)), _whole((BLK, BLK))],
        out_specs=blk(D),
        out_shape=jax.ShapeDtypeStruct((S, D), BF16),
        scratch_shapes=[pltpu.VMEM((POOL_ROWS, 512), F32)] * 3,
        compiler_params=_params(),
    )(pu, pu, pg, q, kv, kv, ag, pool_w, pool_scale, sinks, bias, tri)


def _store_lane_rows(ref, acc):
    total = jnp.sum(acc, axis=0, keepdims=True)
    for k in range(ref.shape[0]):
        ref[k:k + 1, :] = total[:, k * 128:(k + 1) * 128]


def _own_piece(dw_ref, place_ref):
    p = dw_ref.shape[0] // 8
    return dw_ref[pl.ds(pl.multiple_of(place_ref[0] * p, 8), p), :]


def _bwd_out(l, cat, w_out, g_post, place_arr, dxn=None, y=None, x=None, target=None, deps=()):
    last = target is not None
    n_steps = S // TM

    def body(a_ref, b_ref, g_ref, cat_ref, w_ref, place_ref, *rest):
        dcat_ref, own_ref, dwb_ref, dg_ref = rest[len(deps):len(deps) + 4]
        rest = rest[len(deps) + 4:]
        acc_ref, dw_ref = rest[-2:]
        step = pl.program_id(0)

        @pl.when(step == 0)
        def _():
            dw_ref[...] = jnp.zeros_like(dw_ref)
            acc_ref[...] = jnp.zeros_like(acc_ref)

        cat = cat_ref[...]
        g = g_ref[...]
        y = jnp.dot(cat, w_ref[...], preferred_element_type=F32) if last else b_ref[...]
        r = lax.rsqrt(jnp.mean(y * y, axis=-1, keepdims=True) + EPS)
        if last:
            loss_ref, dx_ref, loss_acc_ref = rest[:3]
            err = a_ref[...] + y * r * g - b_ref[...]

            @pl.when(step == 0)
            def _():
                loss_acc_ref[...] = jnp.zeros_like(loss_acc_ref)

            loss_acc_ref[...] += _rows8(err * err)
            dz = err * (1.0 / D)
            dx_ref[...] = dz
        else:
            dz = a_ref[...]
        a = dz * g
        dy = r * a - y * (r * r * r) * jnp.mean(a * y, axis=-1, keepdims=True)
        acc_ref[...] += _rows8(dz * (y * r))
        dyb = dy.astype(BF16)
        dcat_ref[...] = lax.dot_general(dyb, w_ref[...], NT, preferred_element_type=F32)
        dw_ref[...] += lax.dot_general(cat, dyb, TN, preferred_element_type=F32)

        @pl.when(step == n_steps - 1)
        def _():
            _store_lane_rows(dg_ref, acc_ref[...])
            dwb_ref[...] = dw_ref[...].astype(BF16)
            own_ref[...] = _own_piece(dw_ref, place_ref)
            if last:
                loss_ref[...] = jnp.full((8, 128), (0.5 / D) * jnp.sum(loss_acc_ref[...]), F32)

    row = lambda: pl.BlockSpec((TM, D), lambda i: (i, 0))
    full = _whole
    return pl.pallas_call(
        body, name="out_loss_bwd" if last else "bwd_out", grid=(n_steps,),
        in_specs=[row(), row(), _layer(l, 1, D), row(), full((D, D)), pl.BlockSpec(memory_space=pltpu.SMEM)]
        + [ANY] * len(deps),
        out_specs=[row(), full((D // 8, D)), full((D, D)), full((8, 128))] + ([full((8, 128)), row()] if last else []),
        out_shape=[jax.ShapeDtypeStruct((S, D), F32), jax.ShapeDtypeStruct((D // 8, D), F32),
                   jax.ShapeDtypeStruct((D, D), BF16), jax.ShapeDtypeStruct((8, 128), F32)]
        + ([jax.ShapeDtypeStruct((8, 128), F32), jax.ShapeDtypeStruct((S, D), F32)] if last else []),
        scratch_shapes=([pltpu.VMEM((8, D), F32)] if last else []) + [pltpu.VMEM((8, D), F32), pltpu.VMEM((D, D), F32)],
        compiler_params=_params(),
    )(*((x, target) if last else (dxn, y)), g_post, cat, w_out, place_arr, *deps)


def _bwd_mix(l, pu, pg, q, kv, ag, dcat, pool_w, pool_scale, sinks, tables, deps=(), dpw_dest=None):
    bias, tri = tables
    deps = tuple(deps) + (() if dpw_dest is None else (dpw_dest,))

    def body(pu_ref, pup_ref, pg_ref, q_ref, kv_ref, kvp_ref, ag_ref, dcat_ref, pw_ref, sc_ref, sink_ref, bias_ref,
             tri_ref, *rest):
        dproj_ref, dpw_ref, dsc_ref, dsink_ref, ext_ref, dext_ref, tmp_a, tmp_b, dkv_ref = rest[len(deps):]
        tmp_refs = (tmp_a, tmp_b)
        step = pl.program_id(0)
        i = NB - 1 - step

        @pl.when(step == 0)
        def _():
            dpw_ref[...] = jnp.zeros_like(dpw_ref)
            dsc_ref[...] = jnp.zeros_like(dsc_ref)
            dsink_ref[...] = jnp.zeros_like(dsink_ref)
            for ref in (ext_ref, tmp_a, tmp_b):
                ref[0:PAD, :] = jnp.zeros((PAD, 512), F32)
            dext_ref[BLK:POOL_ROWS, :] = jnp.zeros((HALO + PAD, 512), F32)
            dkv_ref[...] = jnp.zeros_like(dkv_ref)

        ext_ref[PAD:PAD + HALO, :] = jnp.where(i > 0, pup_ref[...], 0.0)
        ext_ref[PAD + HALO:POOL_ROWS, :] = pu_ref[...]
        _window_sums(ext_ref, tmp_refs, True)
        dpooled = []
        for g, w in enumerate(POOL_WINDOWS):
            lanes = slice(g * 128, (g + 1) * 128)
            pooled, inv = _pool_block(ext_ref, tmp_refs, i, g, w)
            pooled_b = pooled.astype(BF16)
            mixed = jnp.dot(pooled_b, pw_ref[g], preferred_element_type=F32)
            scale = sc_ref[:, lanes]
            gate = pg_ref[:, lanes]
            sg = _sigmoid(gate)
            dpo = dcat_ref[:, lanes]
            dproj_ref[:, C_PG + g * 128:C_PG + (g + 1) * 128] = (
                dpo * (mixed * scale) * (sg * (1.0 + gate * (1.0 - sg)))).astype(BF16)
            dms = dpo * (gate * sg)
            dsc_ref[g:g + 1, :] += jnp.sum(dms * mixed, axis=0, keepdims=True)
            dmixed = (dms * scale).astype(BF16)
            dpw_ref[g] += lax.dot_general(pooled_b, dmixed, TN, preferred_element_type=F32)
            dpooled.append(lax.dot_general(dmixed, pw_ref[g], NT, preferred_element_type=F32))
            dext_ref[0:BLK, lanes] = dpooled[g] * inv
        _window_sums(dext_ref, tmp_refs, False)
        for g in range(len(POOL_WINDOWS)):
            lanes = slice(g * 128, (g + 1) * 128)
            dproj_ref[:, C_PU + g * 128:C_PU + (g + 1) * 128] = (tmp_refs[g % 2][0:BLK, lanes] - dpooled[g]).astype(BF16)
        dext_ref[BLK:BLK + HALO, :] = dext_ref[0:HALO, :]

        own = _own_block_mask()
        tri = tri_ref[...]
        k_var = _head_variants(kv_ref[:, 0:128], kvp_ref[:, 0:128])
        v_var = _head_variants(kv_ref[:, 128:256], kvp_ref[:, 128:256])
        q2 = [_stack_tiles(q_ref, hkv) for hkv in range(2)]
        s = _scores(q2, k_var, own)
        p, p_sink = {}, {}
        for key, s_head in s.items():
            head = _head_of(*key)
            p[key], p_sink[key] = _softmax(s_head, bias_ref[head], sink_ref[l, head])

        do2, p_b, dp = [], {}, {}
        for hkv in range(2):
            gate = _stack_tiles(ag_ref, hkv)
            sg = _sigmoid(gate)
            dca = _stack_tiles(dcat_ref, hkv, D_POOL)
            do2.append((dca * (gate * sg)).astype(BF16))
            o2 = jnp.zeros((2 * BLK, 128), F32)
            for half in range(2):
                p_b[hkv, half] = _spread_pair(p, hkv, half, tri)
                o2 = o2 + jnp.dot(p_b[hkv, half], v_var[hkv][half], preferred_element_type=F32)
                full = lax.dot_general(do2[hkv], v_var[hkv][half], NT, preferred_element_type=F32)
                for t in range(2):
                    dp[hkv, t, half] = _merge(_rows(full, t), own)
            dag = dca * o2 * (sg * (1.0 + gate * (1.0 - sg)))
            for t in range(2):
                lo = C_AG + (2 * hkv + t) * 128
                dproj_ref[:, lo:lo + 128] = _rows(dag, t).astype(BF16)

        ds = {}
        for key in p:
            delta = jnp.sum(p[key] * dp[key], axis=-1, keepdims=True)
            ds[key] = p[key] * (dp[key] - delta)
            head = _head_of(*key)
            dsink_ref[0:1, :] += jnp.where(lax.broadcasted_iota(jnp.int32, (1, 128), 1) == head,
                                           -jnp.sum(p_sink[key] * delta, axis=0, keepdims=True), 0.0)

        dk_acc = [[None, None], [None, None]]
        dv_acc = [[None, None], [None, None]]
        for hkv in range(2):
            dq2 = jnp.zeros((2 * BLK, 128), F32)
            for half in range(2):
                ds_b = _spread_pair(ds, hkv, half, tri)
                dq2 = dq2 + jnp.dot(ds_b, k_var[hkv][half], preferred_element_type=F32)
                dk_acc[hkv][half] = lax.dot_general(ds_b, q2[hkv], TN, preferred_element_type=F32)
                dv_acc[hkv][half] = lax.dot_general(p_b[hkv, half], do2[hkv], TN, preferred_element_type=F32)
            for t in range(2):
                lo = C_Q + (2 * hkv + t) * 128
                dproj_ref[:, lo:lo + 128] = (_rows(dq2, t) * 0.125).astype(BF16)

        low = lax.broadcasted_iota(jnp.int32, (2 * BLK, 128), 1) < 64

        def gather_heads(acc):
            return jnp.where(low, acc[0][0] + pltpu.roll(acc[0][1], 64, axis=1),
                             pltpu.roll(acc[1][0], 64, axis=1) + acc[1][1])

        dk = gather_heads(dk_acc) * 0.125
        dv = gather_heads(dv_acc)
        dproj_ref[:, C_K:C_V] = (dk[BLK:, :] + dkv_ref[:, 0:128]).astype(BF16)
        dproj_ref[:, C_V:C_AG] = (dv[BLK:, :] + dkv_ref[:, 128:256]).astype(BF16)
        dkv_ref[:, 0:128] = dk[:BLK, :]
        dkv_ref[:, 128:256] = dv[:BLK, :]

    rev = lambda w: pl.BlockSpec((BLK, w), lambda s: (NB - 1 - s, 0))
    prev = lambda w: pl.BlockSpec((BLK, w), lambda s: (jnp.maximum(NB - 2 - s, 0), 0))
    halo = pl.BlockSpec((HALO, 512), lambda s: (jnp.maximum((NB - 1 - s) * (BLK // HALO) - 1, 0), 0))
    return pl.pallas_call(
        body, name="bwd_mix", grid=(NB,),
        in_specs=[rev(512), halo, rev(512), rev(512), rev(256), prev(256), rev(512), rev(D),
                  _layer(l, 4, 128, 128), _layer(l, 1, 512), pl.BlockSpec(memory_space=pltpu.SMEM),
                  pl.BlockSpec((None, N_HEADS, BLK, BLK), lambda s: (jnp.minimum(NB - 1 - s, 1), 0, 0, 0)),
                  _whole((BLK, BLK))] + [ANY] * len(deps),
        out_specs=[rev(D_IN), _layer(l, 4, 128, 128),
                   pl.BlockSpec((4, 128), lambda s: (0, 0)), pl.BlockSpec((8, 128), lambda s: (0, 0))],
        out_shape=[jax.ShapeDtypeStruct((S, D_IN), BF16), jax.ShapeDtypeStruct((DEPTH, 4, 128, 128), F32),
                   jax.ShapeDtypeStruct((4, 128), F32), jax.ShapeDtypeStruct((8, 128), F32)],
        input_output_aliases={} if dpw_dest is None else {12 + len(deps): 1},
        scratch_shapes=[pltpu.VMEM((POOL_ROWS, 512), F32)] * 4 + [pltpu.VMEM((BLK, 256), F32)],
        compiler_params=_params(),
    )(pu, pu, pg, q, kv, kv, ag, dcat, pool_w, pool_scale, sinks, bias, tri, *deps)


def _bwd_in_dw(l, dproj, x, g_pre, place_arr, deps=()):
    n_steps = S // TM

    def body(dp_ref, x_ref, g_ref, place_ref, *rest):
        own_ref, dwb_ref, dw_ref = rest[len(deps):]
        step = pl.program_id(0)

        @pl.when(step == 0)
        def _():
            dw_ref[...] = jnp.zeros_like(dw_ref)

        xt = x_ref[...]
        r = lax.rsqrt(jnp.mean(xt * xt, axis=-1, keepdims=True) + EPS)
        h = (xt * r * g_ref[...]).astype(BF16)
        dw_ref[...] += lax.dot_general(dp_ref[...], h, TN, preferred_element_type=F32)

        @pl.when(step == n_steps - 1)
        def _():
            dwb_ref[...] = dw_ref[...].astype(BF16)
            own_ref[...] = _own_piece(dw_ref, place_ref)

    row = lambda w: pl.BlockSpec((TM, w), lambda i: (i, 0))
    full = _whole
    return pl.pallas_call(
        body, name="bwd_in_dw", grid=(n_steps,),
        in_specs=[row(D_IN), row(D), _layer(l, 1, D), pl.BlockSpec(memory_space=pltpu.SMEM)] + [ANY] * len(deps),
        out_specs=[full((D_IN // 8, D)), full((D_IN, D))],
        out_shape=[jax.ShapeDtypeStruct((D_IN // 8, D), F32), jax.ShapeDtypeStruct((D_IN, D), BF16)],
        scratch_shapes=[pltpu.VMEM((D_IN, D), F32)],
        compiler_params=_params(),
    )(dproj, x, g_pre, place_arr, *deps)


def _bwd_in_dx(l, dproj, w_in_t, x, g_pre, dres, deps=(), dw_place=None):
    n_steps = S // TM
    with_dw = dw_place is not None

    def body(dp_ref, w_ref, x_ref, g_ref, dres_ref, *rest):
        place_ref = rest[0] if with_dw else None
        rest = rest[with_dw + len(deps):]
        if with_dw:
            dx_ref, dg_ref, own_ref, dwb_ref, acc_ref, dw_ref = rest
        else:
            dx_ref, dg_ref, acc_ref = rest
        step = pl.program_id(0)

        @pl.when(step == 0)
        def _():
            acc_ref[...] = jnp.zeros_like(acc_ref)
            if with_dw:
                dw_ref[...] = jnp.zeros_like(dw_ref)

        g = g_ref[...]
        halves = [slice(k * (TM // 2), (k + 1) * (TM // 2)) for k in range(2)]
        dh = [jnp.dot(dp_ref[rows, :], w_ref[...], preferred_element_type=F32) for rows in halves]
        h = []
        for rows, dh_k in zip(halves, dh):
            xt = x_ref[rows, :]
            r = lax.rsqrt(jnp.mean(xt * xt, axis=-1, keepdims=True) + EPS)
            xn = xt * r
            acc_ref[...] += _rows8(dh_k * xn)
            a = dh_k * g
            dx_ref[rows, :] = dres_ref[rows, :] + (
                r * a - xt * (r * r * r) * jnp.mean(a * xt, axis=-1, keepdims=True))
            h.append((xn * g).astype(BF16))
        if with_dw:
            dw_ref[...] += lax.dot_general(dp_ref[...], jnp.concatenate(h, axis=0), TN, preferred_element_type=F32)

        @pl.when(step == n_steps - 1)
        def _():
            _store_lane_rows(dg_ref, acc_ref[...])
            if with_dw:
                dwb_ref[...] = dw_ref[...].astype(BF16)
                own_ref[...] = _own_piece(dw_ref, place_ref)

    row = lambda w: pl.BlockSpec((TM, w), lambda i: (i, 0))
    full = _whole
    dw_specs = [full((D_IN // 8, D)), full((D_IN, D))] if with_dw else []
    dw_shapes = [jax.ShapeDtypeStruct((D_IN // 8, D), F32), jax.ShapeDtypeStruct((D_IN, D), BF16)] if with_dw else []
    return pl.pallas_call(
        body, name="bwd_in" if with_dw else "bwd_in_dx", grid=(n_steps,),
        in_specs=[row(D_IN), full((D_IN, D)), row(D), _layer(l, 1, D), row(D)]
        + [pl.BlockSpec(memory_space=pltpu.SMEM)] * with_dw + [ANY] * len(deps),
        out_specs=[row(D), full((8, 128))] + dw_specs,
        out_shape=[jax.ShapeDtypeStruct((S, D), F32), jax.ShapeDtypeStruct((8, 128), F32)] + dw_shapes,
        scratch_shapes=[pltpu.VMEM((8, D), F32)] + [pltpu.VMEM((D_IN, D), F32)] * with_dw,
        compiler_params=_params(),
    )(dproj, w_in_t, x, g_pre, dres, *((dw_place,) if with_dw else ()), *deps)


HBM =pl.BlockSpec(memory_space=pltpu.HBM)
SEM = pl.BlockSpec(memory_space=pltpu.SEMAPHORE)
def _split_copy(collective_id=None):
    return pltpu.CompilerParams(has_side_effects=pltpu.SideEffectType.DATAFLOW_SIDE_EFFECTING,
                                collective_id=collective_id)


SPLIT_COPY = _split_copy()


def _in_hbm(a):
    return pltpu.with_memory_space_constraint(a, pltpu.HBM)

def _place():
    return lax.axis_index("x"), lax.axis_index("y"), lax.axis_index("c")


def _other_chips(x, y):
    return [(1 - x, y), (x, 1 - y), (1 - x, 1 - y)]


def _peer(x, y, c, m):
    return (x ^ (m >> 2), y ^ ((m >> 1) & 1), c ^ (m & 1))


SAME_CORE = (2, 4, 6)


def _place_cast(name, src, chip_arr, tile, layers, deps=()):
    _, n, cols = src.shape
    steps = n // tile
    k = len(layers)

    def body(chip_ref, *refs):
        for s_ref, o_ref in zip(refs[:k], refs[k + len(deps):]):
            o_ref[...] = s_ref[...].astype(BF16)

    def layer_spec(l):
        return pl.BlockSpec((None, tile, cols), lambda i, chip: (l, i, 0))

    return pl.pallas_call(
        body, name=name,
        grid_spec=pltpu.PrefetchScalarGridSpec(
            num_scalar_prefetch=1, grid=(steps,),
            in_specs=[layer_spec(l) for l in layers] + [ANY] * len(deps),
            out_specs=[pl.BlockSpec((tile, cols), lambda i, chip: (chip[0] * steps + i, 0))] * k),
        out_shape=[jax.ShapeDtypeStruct((N_SHARDS * n, cols), BF16)] * k,
        compiler_params=_params(),
    )(chip_arr, *[src] * k, *deps)


def _chip_rows(ref, chip, half=None):
    n = ref.shape[0] // N_SHARDS
    if half is None:
        return ref.at[pl.ds(pl.multiple_of(chip * n, 16), n), :]
    return ref.at[pl.ds(pl.multiple_of(chip * n + half * (n // 2), 16), n // 2), :]


def _gather_start(name, bufs, halved, collective_id):
    n = len(bufs)

    def body(*refs):
        ins, send, recv, token = refs[:n], refs[n:2 * n], refs[2 * n:3 * n], refs[-1]
        x, y, c = _place()
        _handshake([(*chip, c) for chip in _other_chips(x, y)])
        for a, buf in enumerate(ins):
            own = _chip_rows(buf, 2 * x + y, c if a in halved else None)
            for j, chip in enumerate(_other_chips(x, y)):
                pltpu.make_async_remote_copy(src_ref=own, dst_ref=own, send_sem=send[a].at[j], recv_sem=recv[a].at[j],
                                             device_id=(*chip, c), device_id_type=MESH).start()
        token[...] = jnp.zeros_like(token)

    outs = pl.pallas_call(
        body, name=name, in_specs=[HBM] * n,
        out_specs=[SEM] * (2 * n) + [HBM] * n + [pl.BlockSpec(memory_space=pltpu.VMEM)],
        out_shape=[pltpu.SemaphoreType.DMA((3,))] * (2 * n) + [pltpu.HBM(b.shape, b.dtype) for b in bufs]
        + [jax.ShapeDtypeStruct((8, 128), F32)],
        input_output_aliases={a: 2 * n + a for a in range(n)},
        compiler_params=_split_copy(collective_id),
    )(*[_in_hbm(b) for b in bufs])
    return outs[:n], outs[n:2 * n], outs[2 * n:3 * n], outs[-1]


def _gather_wait(name, buf, send_sem, recv_sem, after, halved=False):
    def body(buf_ref, send_ref, recv_ref, *rest):
        x, y, c = _place()
        half = c if halved else None
        own = _chip_rows(buf_ref, 2 * x + y, half)
        for j, chip in enumerate(_other_chips(x, y)):
            copy = pltpu.make_async_remote_copy(src_ref=own, dst_ref=_chip_rows(buf_ref, 2 * chip[0] + chip[1], half),
                                                send_sem=send_ref.at[j], recv_sem=recv_ref.at[j],
                                                device_id=(*chip, c), device_id_type=MESH)
            copy.wait_send()
            copy.wait_recv()

    return pl.pallas_call(
        body, name=name, in_specs=[HBM, SEM, SEM] + [ANY] * len(after), out_specs=HBM,
        out_shape=pltpu.HBM(buf.shape, buf.dtype), input_output_aliases={0: 0}, compiler_params=SPLIT_COPY,
    )(buf, send_sem, recv_sem, *after)


def _handshake(peers):
    barrier = pltpu.get_barrier_semaphore()
    for peer in peers:
        pl.semaphore_signal(barrier, inc=1, device_id=peer, device_id_type=MESH)
    pl.semaphore_wait(barrier, len(peers))


def _sibling_handshake(x, y, c):
    _handshake([(x, y, 1 - c)])


def _forward_halves(name, bufs, collective_id):
    n = len(bufs)

    def body(*refs):
        ins, outs, (send_sems, recv_sems) = refs[:n], refs[n:2 * n], refs[2 * n:]
        x, y, c = _place()
        _sibling_handshake(x, y, c)

        def copy(a, j, chip, half):
            rows = 2 * chip[0] + chip[1]
            return pltpu.make_async_remote_copy(
                src_ref=_chip_rows(ins[a], rows, half), dst_ref=_chip_rows(outs[a], rows, half),
                send_sem=send_sems.at[3 * a + j], recv_sem=recv_sems.at[3 * a + j], device_id=(x, y, 1 - c),
                device_id_type=MESH)

        copies = [(a, j, chip) for a in range(n) for j, chip in enumerate(_other_chips(x, y))]
        for a, j, chip in copies:
            copy(a, j, chip, c).start()
        for a, j, chip in copies:
            copy(a, j, chip, c).wait_send()
            copy(a, j, chip, 1 - c).wait_recv()

    return pl.pallas_call(
        body, name=name, in_specs=[ANY] * n, out_specs=[ANY] * n,
        out_shape=[jax.ShapeDtypeStruct(b.shape, b.dtype) for b in bufs],
        input_output_aliases={a: a for a in range(n)},
        scratch_shapes=[pltpu.SemaphoreType.DMA((3 * n,))] * 2,
        compiler_params=pltpu.CompilerParams(collective_id=collective_id),
    )(*bufs)


def _piece_rows(ref, k):
    p = ref.shape[0] // 8
    return ref.at[pl.ds(pl.multiple_of(k * p, 32 // jnp.dtype(ref.dtype).itemsize), p), :]


def _exchange_start(name, arrays, collective_id):
    n = len(arrays)
    zones = [lax.empty((7, a.shape[0] // 8, a.shape[1]), a.dtype) for a in arrays]

    def body(*refs):
        srcs, lands = refs[:n], refs[n:2 * n]
        send, recv, token = refs[2 * n:3 * n], refs[3 * n:4 * n], refs[-1]
        x, y, c = _place()
        _handshake([_peer(x, y, c, m) for m in range(1, 8)])
        for a, (src, land) in enumerate(zip(srcs, lands)):
            for m in range(1, 8):
                px, py, pc = _peer(x, y, c, m)
                pltpu.make_async_remote_copy(
                    src_ref=_piece_rows(src, 4 * px + 2 * py + pc), dst_ref=land.at[m - 1], send_sem=send[a].at[m - 1],
                    recv_sem=recv[a].at[m - 1], device_id=(px, py, pc), device_id_type=MESH).start()
        token[...] = jnp.zeros_like(token)

    outs = pl.pallas_call(
        body, name=name, in_specs=[HBM] * (2 * n),
        out_specs=[SEM] * (2 * n) + [HBM] * (2 * n) + [pl.BlockSpec(memory_space=pltpu.VMEM)],
        out_shape=[pltpu.SemaphoreType.DMA((7,))] * (2 * n) + [pltpu.HBM(a.shape, a.dtype) for a in arrays + zones]
        + [jax.ShapeDtypeStruct((8, 128), F32)],
        input_output_aliases={a: 2 * n + a for a in range(2 * n)},
        compiler_params=_split_copy(collective_id),
    )(*[_in_hbm(a) for a in arrays + zones])
    return outs[:n], outs[n:2 * n], outs[2 * n:3 * n], outs[3 * n:4 * n], outs[-1]


def _exchange_wait(name, started, after, which=None, with_sent=False):
    which = range(len(started[2])) if which is None else which
    send_sems, recv_sems, arrays, zones = [[group[k] for k in which] for group in started[:4]]
    n = len(arrays)

    def body(*refs):
        srcs, lands = refs[:n], refs[n:2 * n]
        send, recv = refs[2 * n:3 * n], refs[3 * n:4 * n]
        x, y, c = _place()
        for a, (src, land) in enumerate(zip(srcs, lands)):
            for m in range(1, 8):
                px, py, pc = _peer(x, y, c, m)
                copy = pltpu.make_async_remote_copy(
                    src_ref=_piece_rows(src, 4 * px + 2 * py + pc), dst_ref=land.at[m - 1], send_sem=send[a].at[m - 1],
                    recv_sem=recv[a].at[m - 1], device_id=(px, py, pc), device_id_type=MESH)
                copy.wait_send()
                copy.wait_recv()

    outs = pl.pallas_call(
        body, name=name, in_specs=[HBM] * (2 * n) + [SEM] * (2 * n) + [ANY], out_specs=[HBM] * (2 * n),
        out_shape=[pltpu.HBM(a.shape, a.dtype) for a in list(arrays) + list(zones)],
        input_output_aliases={a: a for a in range(2 * n)}, compiler_params=SPLIT_COPY,
    )(*arrays, *zones, *send_sems, *recv_sems, after)
    return outs if with_sent else outs[n:]


def _sum_pieces(name, weights, place_arr, dests=None, small=(), steps=2):
    flat = [item for items in weights for item in items]
    n = len(flat)

    def body(place_ref, *refs):
        first_out = len(refs) - len(weights) - len(small)
        outs = iter(refs[first_out:])
        small_refs = refs[first_out - 2 * len(small):first_out]

        @pl.when(pl.program_id(0) == 0)
        def _():
            for j in range(len(small)):
                total = small_refs[2 * j][...]
                for m in range(7):
                    total = total + small_refs[2 * j + 1][m]
                refs[first_out + len(weights) + j][...] = total

        k = 0
        for items in weights:
            out_ref = next(outs)
            for layer, _, _ in items:
                total = refs[k][...]
                for m in range(7):
                    total = total + refs[n + k][m].astype(F32)
                if len(items) == DEPTH:
                    out_ref[layer] = total
                else:
                    out_ref[...] = total
                k += 1

    def out_spec(items):
        _, own, _ = items[0]
        t, cols = own.shape[0] // steps, own.shape[1]
        if len(items) == DEPTH:
            return pl.BlockSpec((DEPTH, t, cols), lambda i, place: (0, place[1] * steps + i, 0))
        layer = items[0][0]
        return pl.BlockSpec((None, t, cols), lambda i, place: (layer, place[1] * steps + i, 0))

    owns = [own for _, own, _ in flat]
    dests = [] if dests is None else list(dests)
    piece = lambda a: pl.BlockSpec((a.shape[0] // 8, a.shape[1]), lambda i, place: (place[0], 0))
    small_specs = [spec for a, r in small for spec in (piece(a), pl.BlockSpec(r.shape, lambda i, place: (0, 0, 0)))]
    return pl.pallas_call(
        body, name=name,
        grid_spec=pltpu.PrefetchScalarGridSpec(
            num_scalar_prefetch=1, grid=(steps,),
            in_specs=[pl.BlockSpec((o.shape[0] // steps, o.shape[1]), lambda i, place: (i, 0)) for o in owns]
            + [pl.BlockSpec((7, o.shape[0] // steps, o.shape[1]), lambda i, place: (0, i, 0)) for o in owns]
            + [ANY] * len(dests) + small_specs,
            out_specs=[out_spec(items) for items in weights] + [piece(a) for a, _ in small]),
        out_shape=[jax.ShapeDtypeStruct((DEPTH, 2 * items[0][1].shape[0], items[0][1].shape[1]), F32)
                   for items in weights] + [jax.ShapeDtypeStruct(a.shape, F32) for a, _ in small],
        input_output_aliases={1 + 2 * n + k: k for k in range(len(dests))},
        compiler_params=_params(),
    )(place_arr, *owns, *[recv for _, _, recv in flat], *dests, *[a for pair in small for a in pair])


def _sum_small(name, partials, recvs, place_arr):
    n = len(partials)

    def body(place_ref, *refs):
        for o_ref, r_ref, out_ref in zip(refs[:n], refs[n:2 * n], refs[2 * n:]):
            total = o_ref[...]
            for m in range(7):
                total = total + r_ref[m]
            out_ref[...] = total

    piece = lambda a: pl.BlockSpec((a.shape[0] // 8, a.shape[1]), lambda i, place: (place[0], 0))
    return pl.pallas_call(
        body, name=name,
        grid_spec=pltpu.PrefetchScalarGridSpec(
            num_scalar_prefetch=1, grid=(1,),
            in_specs=[piece(a) for a in partials] + [pl.BlockSpec(r.shape, lambda i, place: (0, 0, 0)) for r in recvs],
            out_specs=[piece(a) for a in partials]),
        out_shape=[jax.ShapeDtypeStruct(a.shape, F32) for a in partials],
        compiler_params=_params(),
    )(place_arr, *partials, *recvs)


def _share(name, bufs, parts, gathered=(), collective_id=None):
    n, n_g = len(bufs), len(gathered)
    total = n + n_g

    def body(*refs):
        ins, outs = refs[:total], refs[total:2 * total]
        send_sems, recv_sems, send_g, recv_g = refs[2 * total:]
        x, y, c = _place()
        _handshake([_peer(x, y, c, m) for m in (SAME_CORE if gathered else ()) + (1,)])

        def half(ref, l, which):
            p = ref.shape[1] // 2
            return ref.at[l, pl.ds(pl.multiple_of(which * p, 8), p), :]

        def swap(k, which):
            a, l = parts[k]
            return pltpu.make_async_remote_copy(
                src_ref=half(ins[a], l, which), dst_ref=half(outs[a], l, which), send_sem=send_sems.at[k],
                recv_sem=recv_sems.at[k], device_id=(x, y, 1 - c), device_id_type=MESH)

        def spread(a, m, sender, held, to):
            k = 4 * sender[0] + 2 * sender[1] + sender[2]
            return pltpu.make_async_remote_copy(
                src_ref=_piece_rows(held[n + a], k), dst_ref=_piece_rows(outs[n + a], k),
                send_sem=send_g.at[7 * a + m - 1], recv_sem=recv_g.at[7 * a + m - 1], device_id=to, device_id_type=MESH)

        me, sibling = (x, y, c), (x, y, 1 - c)
        for k in range(len(parts)):
            swap(k, c).start()
        def own(a, m):
            return spread(a, m, me, ins, _peer(x, y, c, m))

        def handed_on(a, m):
            return spread(a, m + 1, _peer(x, y, c, m), outs, sibling)

        for a in range(n_g):
            for m in SAME_CORE + (1,):
                own(a, m).start()
        for a in range(n_g):
            for m in SAME_CORE:
                spread(a, m, _peer(x, y, c, m), ins, _peer(x, y, c, m)).wait_recv()
                handed_on(a, m).start()
        for k in range(len(parts)):
            swap(k, c).wait_send()
            swap(k, 1 - c).wait_recv()
        for a in range(n_g):
            for m in SAME_CORE + (1,):
                own(a, m).wait_send()
            for m in SAME_CORE:
                handed_on(a, m).wait_send()
                spread(a, m + 1, _peer(x, y, c, m + 1), ins, sibling).wait_recv()
            spread(a, 1, sibling, ins, sibling).wait_recv()

    arrays = list(bufs) + list(gathered)
    return pl.pallas_call(
        body, name=name, in_specs=[ANY] * total, out_specs=[ANY] * total,
        out_shape=[jax.ShapeDtypeStruct(b.shape, F32) for b in arrays],
        input_output_aliases={a: a for a in range(total)},
        scratch_shapes=[pltpu.SemaphoreType.DMA((max(len(parts), 1),))] * 2
        + [pltpu.SemaphoreType.DMA((max(7 * n_g, 1),))] * 2,
        compiler_params=pltpu.CompilerParams(collective_id=collective_id),
    )(*arrays)


def _adamw_math(w, g, m, v):
    nm = ADAM_B1 * m + (1.0 - ADAM_B1) * g
    nv = ADAM_B2 * v + (1.0 - ADAM_B2) * (g * g)
    m_hat = nm / (1.0 - ADAM_B1 ** ADAM_STEP)
    v_hat = nv / (1.0 - ADAM_B2 ** ADAM_STEP)
    return -ADAM_LR * (m_hat / (jnp.sqrt(v_hat) + ADAM_EPS) + ADAM_WD * w), nm, nv


def _adamw(name, w, g, m, v, rows_per_step, first=0, count=None, dests=None, deps=()):
    layers, rows, cols = w.shape
    count = layers if count is None else count

    def body(w_ref, g_ref, m_ref, v_ref, *rest):
        d_ref, nm_ref, nv_ref, g_out_ref = rest[-4:]
        d_ref[...], nm_ref[...], nv_ref[...] = _adamw_math(w_ref[...], g_ref[...], m_ref[...], v_ref[...])
        g_out_ref[...] = g_ref[...]

    spec = pl.BlockSpec((1, rows_per_step, cols), lambda l, i: (first + l, i, 0))
    shape = jax.ShapeDtypeStruct(w.shape, F32)
    dests = () if dests is None else tuple(dests)
    return pl.pallas_call(
        body, name=name, grid=(count, rows // rows_per_step),
        in_specs=[spec] * 4 + [ANY] * (len(dests) + len(deps)), out_specs=[spec] * 4, out_shape=[shape] * 4,
        input_output_aliases={4 + k: k for k in range(len(dests))},
        compiler_params=_params(("arbitrary", "arbitrary")),
    )(w, g, m, v, *dests, *deps)


def _pack_misc(pool_scale, sinks, norm_pre, norm_post):
    sink_rows = jnp.zeros((DEPTH, 8, 128), F32).at[:, 0, 0:N_HEADS].set(sinks).reshape(2 * 8, 128)
    return jnp.concatenate([pool_scale.reshape(8, 128), norm_pre.reshape(16, 128), norm_post.reshape(16, 128),
                            sink_rows, jnp.zeros((8, 128), F32)], axis=0)


def _adamw_small(w, g, m, v, pool):
    def body(w_ref, g_ref, m_ref, v_ref, pw_ref, pg_ref, pm_ref, pv_ref, *rest):
        outs, pool_outs, (d_ref, nm_ref, nv_ref) = rest[:17], rest[17:21], rest[21:]
        pool_outs[0][...] = pg_ref[...]
        pool_outs[1][...], pool_outs[2][...], pool_outs[3][...] = _adamw_math(
            pw_ref[...], pg_ref[...], pm_ref[...], pv_ref[...])
        d_ref[...], nm_ref[...], nv_ref[...] = _adamw_math(w_ref[...], g_ref[...], m_ref[...], v_ref[...])
        for k, src in enumerate([g_ref, d_ref, nm_ref, nv_ref]):
            scale, sinks, pre, post = outs[4 * k:4 * k + 4]
            for l in range(DEPTH):
                for j in range(4):
                    scale[l:l + 1, j * 128:(j + 1) * 128] = src[MISC_SCALE + 4 * l + j:MISC_SCALE + 4 * l + j + 1, :]
                for j in range(8):
                    pre[l:l + 1, j * 128:(j + 1) * 128] = src[MISC_PRE + 8 * l + j:MISC_PRE + 8 * l + j + 1, :]
                    post[l:l + 1, j * 128:(j + 1) * 128] = src[MISC_POST + 8 * l + j:MISC_POST + 8 * l + j + 1, :]
                sinks[l:l + 1, :] = src[MISC_SINKS + 8 * l:MISC_SINKS + 8 * l + 1, 0:N_HEADS]
        outs[16][...] = g_ref[MISC_LOSS:MISC_LOSS + 1, 0:1]

    vmem = pl.BlockSpec(memory_space=pltpu.VMEM)
    shapes = [(DEPTH, D_POOL), (DEPTH, N_HEADS), (DEPTH, D), (DEPTH, D)] * 4 + [(1, 1)]
    shapes += [pool[0].shape] * 4
    return pl.pallas_call(
        body, name="adamw_small", in_specs=[vmem] * 8, out_specs=[vmem] * 21,
        out_shape=[jax.ShapeDtypeStruct(s, F32) for s in shapes],
        scratch_shapes=[pltpu.VMEM((MISC_ROWS, 128), F32)] * 3,
    )(w, g, m, v, *pool)


def kernel(x, w_in, pool_w, pool_scale, attn_sinks, w_out, norm_pre, norm_post, loss_target, m_w_in, m_pool_w, m_pool_scale, m_attn_sinks, m_w_out, m_norm_pre, m_norm_post, v_w_in, v_pool_w, v_pool_scale, v_attn_sinks, v_w_out, v_norm_pre, v_norm_post):
    cx, cy, cc = _place()
    chip_arr = jnp.reshape(2 * cx + cy, (1,)).astype(jnp.int32)
    place_arr = jnp.stack([4 * cx + 2 * cy + cc, cc]).astype(jnp.int32)
    t = lambda a: jnp.transpose(a, (0, 2, 1))
    w_in_t = t(w_in)
    xs, target = x[0], loss_target[0]
    pool_w_b = pool_w.astype(BF16)
    tables = _attention_tables()
    scale3 = pool_scale.reshape(DEPTH, 1, D_POOL)
    pre3 = norm_pre.reshape(DEPTH, 1, D)
    post3 = norm_post.reshape(DEPTH, 1, D)

    (wi0,) = _place_cast("place_w_in0", w_in_t, chip_arr, 288, [0])
    first = _gather_start("gather_start_first", [wi0], halved=(0,), collective_id=ID_GATHER_FIRST)
    (wi1,) = _place_cast("place_w_in1", w_in_t, chip_arr, 288, [1], deps=(first[3],))
    wo = _place_cast("place_w_out", w_out, chip_arr, 256, [0, 1], deps=(first[3],))
    rest = _gather_start("gather_start_rest", [wi1, wo[0], wo[1]], halved=(0, 1), collective_id=ID_GATHER_REST)
    send, recv, bufs = [first[k] + rest[k] for k in range(3)]
    order = {(0, "in"): 0, (1, "in"): 1, (0, "out"): 2, (1, "out"): 3}

    saved = []
    packed = [_pack_misc(pool_scale, attn_sinks, norm_pre, norm_post),
              _pack_misc(m_pool_scale, m_attn_sinks, m_norm_pre, m_norm_post),
              _pack_misc(v_pool_scale, v_attn_sinks, v_norm_pre, v_norm_post)]
    after = (first[3], rest[3], pool_w_b, *tables, scale3, pre3, post3, *packed)
    below = None
    for l in range(DEPTH):
        k = order[l, "in"]
        halves = [_gather_wait(f"gather_wait_in{l}", bufs[k], send[k], recv[k], after, halved=True)]
        if below is not None:
            halves.append(below[1])
        w_in_l, *w_out_below = _forward_halves(f"forward_w{l}", halves, collective_id=ID_FORWARD[l])
        if below is None:
            pu, pg, q, kv, ag = _fwd_in(l, xs, pre3, w_in_l)
        else:
            saved[l - 1][9] = w_out_below[0]
            y, xs, pu, pg, q, kv, ag = _fwd_in(l, xs, pre3, w_in_l, (below[0], w_out_below[0], below[2]))
            saved[l - 1][7] = y
        cat = _fwd_mix(l, pu, pg, q, kv, ag, pool_w_b, scale3, attn_sinks, tables)
        k = order[l, "out"]
        w_out_l = _gather_wait(f"gather_wait_out{l}", bufs[k], send[k], recv[k], (cat,), halved=l + 1 < DEPTH)
        saved.append([xs, pu, pg, q, kv, ag, cat, None, w_in_l, w_out_l])
        below, after = (cat, w_out_l, post3), (w_out_l,)

    x_in, pu, pg, q, kv, ag, cat, y, w_in_l, w_out_l = saved[1]
    dcat, dw_out1, dw_out1_b, dg_post1, loss, xs = _bwd_out(1, cat, w_out_l, post3, place_arr, x=x_in, target=target)
    ex1_out = _exchange_start("exchange_start_out1", [dw_out1_b], ID_OUT1)
    dproj, dpw, dsc1, dsink1 = _bwd_mix(1, pu, pg, q, kv, ag, dcat, pool_w_b, scale3, attn_sinks, tables,
                                        deps=(ex1_out[4],))
    dx, dg_pre1, dw_in1, dw_in1_b = _bwd_in_dx(1, dproj, w_in_l, x_in, pre3, xs, dw_place=place_arr)
    ex1_in = _exchange_start("exchange_start_in1", [dw_in1_b], ID_IN1)

    x_in, pu, pg, q, kv, ag, cat, y, w_in_l, w_out_l = saved[0]
    dcat, dw_out0, dw_out0_b, dg_post0 = _bwd_out(0, cat, w_out_l, post3, place_arr, dxn=dx, y=y, deps=(ex1_in[4],))
    ex0_out = _exchange_start("exchange_start_out0", [dw_out0_b], ID_OUT0)
    dproj, dpw, dsc0, dsink0 = _bwd_mix(0, pu, pg, q, kv, ag, dcat, pool_w_b, scale3, attn_sinks, tables,
                                        deps=(ex0_out[4],), dpw_dest=dpw)
    dw_in0, dw_in0_b = _bwd_in_dw(0, dproj, x_in, pre3, place_arr)
    flat = lambda a: a.reshape(DEPTH * 4 * 128, 128)
    ex0_in = _exchange_start("exchange_start_in0", [flat(dpw), dw_in0_b], ID_IN0)

    grad_x, dg_pre0 = _bwd_in_dx(0, dproj, w_in_l, x_in, pre3, dx, deps=(ex0_in[4],))
    small = [jnp.concatenate([dsc0, dsc1, dg_pre0, dg_pre1, dg_post0, dg_post1, dsink0, dsink1, loss], axis=0)]
    ex_small = _exchange_start("exchange_start_small", small, ID_SMALL)
    (recv_out1,) = _exchange_wait("exchange_wait_out1", ex1_out, ex_small[4])
    (recv_in1,) = _exchange_wait("exchange_wait_in1", ex1_in, recv_out1)
    g_in, g_out = _sum_pieces("sum_pieces_1", [[(1, dw_in1, recv_in1)], [(1, dw_out1, recv_out1)]], place_arr)
    (recv_out0,) = _exchange_wait("exchange_wait_out0", ex0_out, g_out)
    (g_out,) = _sum_pieces("sum_pieces_out0", [[(0, dw_out0, recv_out0)]], place_arr, dests=[g_out])
    g_in, g_out = _share("share_a", [g_in, g_out], [(0, 1), (1, 0), (1, 1)], collective_id=ID_SHARE_A)
    m_in_t, v_in_t = t(m_w_in), t(v_w_in)
    d_out, nm_out, nv_out, grad_w_out = _adamw("adamw_w_out", w_out, g_out, m_w_out, v_w_out, 256)
    upd_in = _adamw("adamw_w_in1", w_in_t, g_in, m_in_t, v_in_t, 288, first=1, count=1, deps=(d_out,))
    dpw_own, recv_pw = _exchange_wait("exchange_wait_pool", ex0_in, upd_in[0], which=[0], with_sent=True)
    (g_pw,) = _sum_small("sum_pool", [dpw_own], [recv_pw], place_arr)

    (recv_in0,) = _exchange_wait("exchange_wait_in0", ex0_in, g_pw, which=[1])
    (recv_misc,) = _exchange_wait("exchange_wait_small", ex_small, recv_in0)
    g_in, g_misc = _sum_pieces("sum_pieces_in0", [[(0, dw_in0, recv_in0)]], place_arr, dests=[g_in],
                               small=[(small[0], recv_misc)], steps=3)
    g_in, g_pw, g_misc = _share("share_b", [g_in], [(0, 0)], [g_pw, g_misc], collective_id=ID_SHARE_B)
    d_in, nm_in, nv_in, grad_w_in_t = _adamw("adamw_w_in0", w_in_t, g_in, m_in_t, v_in_t, 96, first=0, count=1,
                                             dests=upd_in)
    small_out = _adamw_small(packed[0], g_misc, packed[1], packed[2],
                             (flat(pool_w), g_pw, flat(m_pool_w), flat(v_pool_w)))
    (g_sc, g_sk, g_pre, g_post, d_sc, d_sk, d_pre, d_post,
     m_sc, m_sk, m_pre, m_post, v_sc, v_sk, v_pre, v_post, loss_sum) = small_out[:17]
    g_pw, d_pw, m_pw, v_pw = [a.reshape(pool_w.shape) for a in small_out[17:]]
    return (loss_sum[0, 0], grad_x[None], t(grad_w_in_t), g_pw, g_sc, g_sk, grad_w_out, g_pre, g_post,
            t(d_in), d_pw, d_sc, d_sk, d_out, d_pre, d_post,
            t(nm_in), m_pw, m_sc, m_sk, nm_out, m_pre, m_post,
            t(nv_in), v_pw, v_sc, v_sk, nv_out, v_pre, v_post)
```

```python
import jax
import jax.numpy as jnp
from jax import lax
from jax.experimental import pallas as pl
from jax.experimental.pallas import tpu as pltpu

F32 = jnp.float32
BF16 = jnp.bfloat16

S = 2048
D = 1024
DEPTH = 2
D_POOL = 512
POOL_WINDOWS = (2, 4, 8, 16)
N_HEADS = 8
D_IN = 2304
N_SHARDS = 4
W_IN_SHARD = D_IN // N_SHARDS
W_OUT_SHARD = D // N_SHARDS
BLK = 128
NB = S // BLK
HALO = 16
PAD = 8
EPS = 1e-6
NEG_INF = -1e30
C_PU, C_PG, C_Q, C_K, C_V, C_AG = 0, 512, 1024, 1536, 1664, 1792

ADAM_LR = 0.001
ADAM_B1 = 0.9
ADAM_B2 = 0.999
ADAM_EPS = 1e-08
ADAM_WD = 0.01
ADAM_STEP = 10

TM = 512
VMEM_LIMIT = 56 * 1024 * 1024

NT = (((1,), (1,)), ((), ()))
TN = (((0,), (0,)), ((), ()))

MESH = pl.DeviceIdType.MESH
ANY = pl.BlockSpec(memory_space=pl.ANY)

ID_FORWARD = (0, 1)
(ID_SHARE_A, ID_SHARE_B, ID_GATHER_FIRST, ID_GATHER_REST, ID_OUT1, ID_IN1, ID_OUT0, ID_IN0, ID_SMALL) = range(2, 11)

MISC_SCALE, MISC_PRE, MISC_POST, MISC_SINKS, MISC_LOSS = 0, 8, 24, 40, 56
MISC_ROWS = 64


def _params(sem=("arbitrary",)):
    return pltpu.CompilerParams(dimension_semantics=sem, vmem_limit_bytes=VMEM_LIMIT)


def _sigmoid(v):
    return 1.0 / (1.0 + jnp.exp(-v))


def _rows8(v):
    r, c = v.shape
    return v.reshape(r // 8, 8, c).sum(axis=0)


def _layer(l, *shape):
    zeros = (0,) * len(shape)
    return pl.BlockSpec((None,) + shape, lambda i: (l,) + zeros)


def _whole(shape):
    zeros = (0,) * len(shape)
    return pl.BlockSpec(shape, lambda i: zeros, pipeline_mode=pl.Buffered(1))


def _fwd_in(l, x, g_pre, w_in_t, below=None):
    fused = below is not None

    def body(x_ref, g_ref, w_ref, *rest):
        if fused:
            cat_ref, wo_ref, gp_ref, y_ref, xn_ref = rest[:5]
            y = jnp.dot(cat_ref[...], wo_ref[...], preferred_element_type=F32)
            y_ref[...] = y
            xt = x_ref[...] + y * lax.rsqrt(jnp.mean(y * y, axis=-1, keepdims=True) + EPS) * gp_ref[...]
            xn_ref[...] = xt
        else:
            xt = x_ref[...]
        pu_ref, pg_ref, q_ref, kv_ref, ag_ref = rest[-5:]
        r = lax.rsqrt(jnp.mean(xt * xt, axis=-1, keepdims=True) + EPS)
        h = (xt * r * g_ref[...]).astype(BF16)

        def proj(lo, hi):
            return lax.dot_general(h, w_ref[lo:hi, :], NT, preferred_element_type=F32)

        pu_ref[...] = proj(C_PU, C_PG)
        pg_ref[...] = proj(C_PG, C_Q)
        q_ref[...] = proj(C_Q, C_K).astype(BF16)
        kv_ref[...] = proj(C_K, C_AG).astype(BF16)
        ag_ref[...] = proj(C_AG, D_IN)

    row = lambda w: pl.BlockSpec((TM, w), lambda i: (i, 0))
    act = jax.ShapeDtypeStruct((S, D), F32)
    return pl.pallas_call(
        body, name="fwd_out_in" if fused else "fwd_in", grid=(S // TM,),
        in_specs=[row(D), _layer(l, 1, D), _whole((D_IN, D))]
        + ([row(D), _whole((D, D)), _layer(l - 1, 1, D)] if fused else []),
        out_specs=[row(D)] * (2 * fused) + [row(512), row(512), row(512), row(256), row(512)],
        out_shape=[act] * (2 * fused)
        + [jax.ShapeDtypeStruct((S, 512), F32), jax.ShapeDtypeStruct((S, 512), F32),
           jax.ShapeDtypeStruct((S, 512), BF16), jax.ShapeDtypeStruct((S, 256), BF16),
           jax.ShapeDtypeStruct((S, 512), F32)],
        compiler_params=_params(),
    )(x, g_pre, w_in_t, *(below if fused else ()))


LOG2E = 1.4426950408889634
SCORE_SCALE = 0.125 * LOG2E


def _attention_tables():
    qi = jnp.arange(BLK)[:, None]
    kj = jnp.arange(BLK)[None, :]
    dist = ((qi - kj) % BLK).astype(F32)
    slopes = jnp.exp2(-jnp.arange(1, N_HEADS + 1, dtype=F32))
    bias = -(slopes * LOG2E)[:, None, None] * dist[None]
    first = jnp.where(kj > qi, NEG_INF, bias)
    return jnp.stack([first, bias]), (kj <= qi).astype(BF16)


def _own_block_mask():
    return lax.broadcasted_iota(jnp.int32, (BLK, BLK), 1) <= lax.broadcasted_iota(jnp.int32, (BLK, BLK), 0)


def _merge(full, own):
    return jnp.where(own, full[:, BLK:], full[:, :BLK])


def _spread(v, tri):
    own = v * tri
    return jnp.concatenate([v - own, own], axis=1)


def _head_variants(cur, prev):
    both = jnp.concatenate([prev, cur], axis=0).astype(F32)
    swapped = pltpu.roll(both, 64, axis=1)
    low = lax.broadcasted_iota(jnp.int32, both.shape, 1) < 64
    zero = jnp.zeros_like(both)
    return ((jnp.where(low, both, zero).astype(BF16), jnp.where(low, zero, swapped).astype(BF16)),
            (jnp.where(low, swapped, zero).astype(BF16), jnp.where(low, zero, both).astype(BF16)))


def _head_of(hkv, t, half):
    return hkv * 4 + 2 * t + half


def _rows(v, t):
    return v[t * BLK:(t + 1) * BLK]


def _stack_tiles(ref, hkv, offset=0):
    lo = offset + 2 * hkv * 128
    return jnp.concatenate([ref[:, lo:lo + 128], ref[:, lo + 128:lo + 256]], axis=0)


def _scores(q2, k_var, own):
    s = {}
    for hkv in range(2):
        for half in range(2):
            full = lax.dot_general(q2[hkv], k_var[hkv][half], NT, preferred_element_type=F32)
            for t in range(2):
                s[hkv, t, half] = _merge(_rows(full, t), own)
    return s


def _softmax(s, bias, sink):
    s = s * SCORE_SCALE + bias
    sink2 = sink * LOG2E
    m = jnp.maximum(jnp.max(s, axis=-1, keepdims=True), sink2)
    p = jnp.exp2(s - m)
    e_sink = jnp.exp2(sink2 - m)
    inv = 1.0 / (jnp.sum(p, axis=-1, keepdims=True) + e_sink)
    return p * inv, e_sink * inv


def _spread_pair(v, hkv, half, tri):
    return jnp.concatenate([_spread(v[hkv, t, half].astype(BF16), tri) for t in range(2)], axis=0)


POOL_ROWS = PAD + HALO + BLK


def _window_sums(src_ref, tmp_refs, trailing):
    lo, hi = (PAD, POOL_ROWS) if trailing else (0, HALO + BLK)
    cur = src_ref
    for level in range(len(POOL_WINDOWS)):
        lanes = slice(level * 128, 512)
        shift = -(1 << level) if trailing else (1 << level)
        dst = tmp_refs[level % 2]
        dst[lo:hi, lanes] = cur[lo:hi, lanes] + cur[lo + shift:hi + shift, lanes]
        cur = dst


def _pool_block(ext_ref, tmp_refs, i, g, w):
    lanes = slice(g * 128, (g + 1) * 128)
    rows = slice(PAD + HALO, POOL_ROWS)
    t = (i * BLK + lax.broadcasted_iota(jnp.int32, (BLK, 1), 0)).astype(F32)
    inv = 1.0 / jnp.minimum(t + 1.0, float(w))
    return tmp_refs[g % 2][rows, lanes] * inv - ext_ref[rows, lanes], inv


def _fwd_mix(l, pu, pg, q, kv, ag, pool_w, pool_scale, sinks, tables):
    bias, tri = tables

    def body(pu_ref, pup_ref, pg_ref, q_ref, kv_ref, kvp_ref, ag_ref, pw_ref, sc_ref, sink_ref, bias_ref, tri_ref,
             cat_ref, ext_ref, *tmp_refs):
        i = pl.program_id(0)

        @pl.when(i == 0)
        def _():
            for ref in (ext_ref, *tmp_refs):
                ref[0:PAD, :] = jnp.zeros((PAD, 512), F32)

        ext_ref[PAD:PAD + HALO, :] = jnp.where(i > 0, pup_ref[...], 0.0)
        ext_ref[PAD + HALO:POOL_ROWS, :] = pu_ref[...]
        _window_sums(ext_ref, tmp_refs, True)
        for g, w in enumerate(POOL_WINDOWS):
            lanes = slice(g * 128, (g + 1) * 128)
            pooled, _ = _pool_block(ext_ref, tmp_refs, i, g, w)
            mixed = jnp.dot(pooled.astype(BF16), pw_ref[g], preferred_element_type=F32)
            gate = pg_ref[:, lanes]
            cat_ref[:, lanes] = (mixed * sc_ref[:, lanes] * (gate * _sigmoid(gate))).astype(BF16)

        own = _own_block_mask()
        tri = tri_ref[...]
        k_var = _head_variants(kv_ref[:, 0:128], kvp_ref[:, 0:128])
        v_var = _head_variants(kv_ref[:, 128:256], kvp_ref[:, 128:256])
        s = _scores([_stack_tiles(q_ref, hkv) for hkv in range(2)], k_var, own)
        p = {}
        for (hkv, t, half), s_head in s.items():
            head = _head_of(hkv, t, half)
            p[hkv, t, half], _ = _softmax(s_head, bias_ref[head], sink_ref[l, head])
        for hkv in range(2):
            o2 = jnp.zeros((2 * BLK, 128), F32)
            for half in range(2):
                o2 = o2 + jnp.dot(_spread_pair(p, hkv, half, tri), v_var[hkv][half], preferred_element_type=F32)
            for t in range(2):
                lo = (2 * hkv + t) * 128
                gate = ag_ref[:, lo:lo + 128]
                cat_ref[:, D_POOL + lo:D_POOL + lo + 128] = (_rows(o2, t) * (gate * _sigmoid(gate))).astype(BF16)

    blk = lambda w: pl.BlockSpec((BLK, w), lambda i: (i, 0))
    prev = lambda w: pl.BlockSpec((BLK, w), lambda i: (jnp.maximum(i - 1, 0), 0))
    halo = pl.BlockSpec((HALO, 512), lambda i: (jnp.maximum(i * (BLK // HALO) - 1, 0), 0))
    return pl.pallas_call(
        body, name="fwd_mix", grid=(NB,),
        in_specs=[blk(512), halo, blk(512), blk(512), blk(256), prev(256), blk(512),
                  _layer(l, 4, 128, 128), _layer(l, 1, 512), pl.BlockSpec(memory_space=pltpu.SMEM),
                  pl.BlockSpec((None, N_HEADS, BLK, BLK), lambda i: (jnp.minimum(i, 1), 0, 0, 0)), _whole((BLK, BLK))],
        out_specs=blk(D),
        out_shape=jax.ShapeDtypeStruct((S, D), BF16),
        scratch_shapes=[pltpu.VMEM((POOL_ROWS, 512), F32)] * 3,
        compiler_params=_params(),
    )(pu, pu, pg, q, kv, kv, ag, pool_w, pool_scale, sinks, bias, tri)


def _store_lane_rows(ref, acc):
    total = jnp.sum(acc, axis=0, keepdims=True)
    for k in range(ref.shape[0]):
        ref[k:k + 1, :] = total[:, k * 128:(k + 1) * 128]


def _own_piece(dw_ref, place_ref):
    p = dw_ref.shape[0] // 8
    return dw_ref[pl.ds(pl.multiple_of(place_ref[0] * p, 8), p), :]


def _bwd_out(l, cat, w_out, g_post, place_arr, dxn=None, y=None, x=None, target=None, deps=()):
    last = target is not None
    n_steps = S // TM

    def body(a_ref, b_ref, g_ref, cat_ref, w_ref, place_ref, *rest):
        dcat_ref, own_ref, dwb_ref, dg_ref = rest[len(deps):len(deps) + 4]
        rest = rest[len(deps) + 4:]
        acc_ref, dw_ref = rest[-2:]
        step = pl.program_id(0)

        @pl.when(step == 0)
        def _():
            dw_ref[...] = jnp.zeros_like(dw_ref)
            acc_ref[...] = jnp.zeros_like(acc_ref)

        cat = cat_ref[...]
        g = g_ref[...]
        y = jnp.dot(cat, w_ref[...], preferred_element_type=F32) if last else b_ref[...]
        r = lax.rsqrt(jnp.mean(y * y, axis=-1, keepdims=True) + EPS)
        if last:
            loss_ref, dx_ref, loss_acc_ref = rest[:3]
            err = a_ref[...] + y * r * g - b_ref[...]

            @pl.when(step == 0)
            def _():
                loss_acc_ref[...] = jnp.zeros_like(loss_acc_ref)

            loss_acc_ref[...] += _rows8(err * err)
            dz = err * (1.0 / D)
            dx_ref[...] = dz
        else:
            dz = a_ref[...]
        a = dz * g
        dy = r * a - y * (r * r * r) * jnp.mean(a * y, axis=-1, keepdims=True)
        acc_ref[...] += _rows8(dz * (y * r))
        dyb = dy.astype(BF16)
        dcat_ref[...] = lax.dot_general(dyb, w_ref[...], NT, preferred_element_type=F32)
        dw_ref[...] += lax.dot_general(cat, dyb, TN, preferred_element_type=F32)

        @pl.when(step == n_steps - 1)
        def _():
            _store_lane_rows(dg_ref, acc_ref[...])
            dwb_ref[...] = dw_ref[...].astype(BF16)
            own_ref[...] = _own_piece(dw_ref, place_ref)
            if last:
                loss_ref[...] = jnp.full((8, 128), (0.5 / D) * jnp.sum(loss_acc_ref[...]), F32)

    row = lambda: pl.BlockSpec((TM, D), lambda i: (i, 0))
    full = _whole
    return pl.pallas_call(
        body, name="out_loss_bwd" if last else "bwd_out", grid=(n_steps,),
        in_specs=[row(), row(), _layer(l, 1, D), row(), full((D, D)), pl.BlockSpec(memory_space=pltpu.SMEM)]
        + [ANY] * len(deps),
        out_specs=[row(), full((D // 8, D)), full((D, D)), full((8, 128))] + ([full((8, 128)), row()] if last else []),
        out_shape=[jax.ShapeDtypeStruct((S, D), F32), jax.ShapeDtypeStruct((D // 8, D), F32),
                   jax.ShapeDtypeStruct((D, D), BF16), jax.ShapeDtypeStruct((8, 128), F32)]
        + ([jax.ShapeDtypeStruct((8, 128), F32), jax.ShapeDtypeStruct((S, D), F32)] if last else []),
        scratch_shapes=([pltpu.VMEM((8, D), F32)] if last else []) + [pltpu.VMEM((8, D), F32), pltpu.VMEM((D, D), F32)],
        compiler_params=_params(),
    )(*((x, target) if last else (dxn, y)), g_post, cat, w_out, place_arr, *deps)


def _bwd_mix(l, pu, pg, q, kv, ag, dcat, pool_w, pool_scale, sinks, tables, deps=(), dpw_dest=None):
    bias, tri = tables
    deps = tuple(deps) + (() if dpw_dest is None else (dpw_dest,))

    def body(pu_ref, pup_ref, pg_ref, q_ref, kv_ref, kvp_ref, ag_ref, dcat_ref, pw_ref, sc_ref, sink_ref, bias_ref,
             tri_ref, *rest):
        dproj_ref, dpw_ref, dsc_ref, dsink_ref, ext_ref, dext_ref, tmp_a, tmp_b, dkv_ref = rest[len(deps):]
        tmp_refs = (tmp_a, tmp_b)
        step = pl.program_id(0)
        i = NB - 1 - step

        @pl.when(step == 0)
        def _():
            dpw_ref[...] = jnp.zeros_like(dpw_ref)
            dsc_ref[...] = jnp.zeros_like(dsc_ref)
            dsink_ref[...] = jnp.zeros_like(dsink_ref)
            for ref in (ext_ref, tmp_a, tmp_b):
                ref[0:PAD, :] = jnp.zeros((PAD, 512), F32)
            dext_ref[BLK:POOL_ROWS, :] = jnp.zeros((HALO + PAD, 512), F32)
            dkv_ref[...] = jnp.zeros_like(dkv_ref)

        ext_ref[PAD:PAD + HALO, :] = jnp.where(i > 0, pup_ref[...], 0.0)
        ext_ref[PAD + HALO:POOL_ROWS, :] = pu_ref[...]
        _window_sums(ext_ref, tmp_refs, True)
        dpooled = []
        for g, w in enumerate(POOL_WINDOWS):
            lanes = slice(g * 128, (g + 1) * 128)
            pooled, inv = _pool_block(ext_ref, tmp_refs, i, g, w)
            pooled_b = pooled.astype(BF16)
            mixed = jnp.dot(pooled_b, pw_ref[g], preferred_element_type=F32)
            scale = sc_ref[:, lanes]
            gate = pg_ref[:, lanes]
            sg = _sigmoid(gate)
            dpo = dcat_ref[:, lanes]
            dproj_ref[:, C_PG + g * 128:C_PG + (g + 1) * 128] = (
                dpo * (mixed * scale) * (sg * (1.0 + gate * (1.0 - sg)))).astype(BF16)
            dms = dpo * (gate * sg)
            dsc_ref[g:g + 1, :] += jnp.sum(dms * mixed, axis=0, keepdims=True)
            dmixed = (dms * scale).astype(BF16)
            dpw_ref[g] += lax.dot_general(pooled_b, dmixed, TN, preferred_element_type=F32)
            dpooled.append(lax.dot_general(dmixed, pw_ref[g], NT, preferred_element_type=F32))
            dext_ref[0:BLK, lanes] = dpooled[g] * inv
        _window_sums(dext_ref, tmp_refs, False)
        for g in range(len(POOL_WINDOWS)):
            lanes = slice(g * 128, (g + 1) * 128)
            dproj_ref[:, C_PU + g * 128:C_PU + (g + 1) * 128] = (tmp_refs[g % 2][0:BLK, lanes] - dpooled[g]).astype(BF16)
        dext_ref[BLK:BLK + HALO, :] = dext_ref[0:HALO, :]

        own = _own_block_mask()
        tri = tri_ref[...]
        k_var = _head_variants(kv_ref[:, 0:128], kvp_ref[:, 0:128])
        v_var = _head_variants(kv_ref[:, 128:256], kvp_ref[:, 128:256])
        q2 = [_stack_tiles(q_ref, hkv) for hkv in range(2)]
        s = _scores(q2, k_var, own)
        p, p_sink = {}, {}
        for key, s_head in s.items():
            head = _head_of(*key)
            p[key], p_sink[key] = _softmax(s_head, bias_ref[head], sink_ref[l, head])

        do2, p_b, dp = [], {}, {}
        for hkv in range(2):
            gate = _stack_tiles(ag_ref, hkv)
            sg = _sigmoid(gate)
            dca = _stack_tiles(dcat_ref, hkv, D_POOL)
            do2.append((dca * (gate * sg)).astype(BF16))
            o2 = jnp.zeros((2 * BLK, 128), F32)
            for half in range(2):
                p_b[hkv, half] = _spread_pair(p, hkv, half, tri)
                o2 = o2 + jnp.dot(p_b[hkv, half], v_var[hkv][half], preferred_element_type=F32)
                full = lax.dot_general(do2[hkv], v_var[hkv][half], NT, preferred_element_type=F32)
                for t in range(2):
                    dp[hkv, t, half] = _merge(_rows(full, t), own)
            dag = dca * o2 * (sg * (1.0 + gate * (1.0 - sg)))
            for t in range(2):
                lo = C_AG + (2 * hkv + t) * 128
                dproj_ref[:, lo:lo + 128] = _rows(dag, t).astype(BF16)

        ds = {}
        for key in p:
            delta = jnp.sum(p[key] * dp[key], axis=-1, keepdims=True)
            ds[key] = p[key] * (dp[key] - delta)
            head = _head_of(*key)
            dsink_ref[0:1, :] += jnp.where(lax.broadcasted_iota(jnp.int32, (1, 128), 1) == head,
                                           -jnp.sum(p_sink[key] * delta, axis=0, keepdims=True), 0.0)

        dk_acc = [[None, None], [None, None]]
        dv_acc = [[None, None], [None, None]]
        for hkv in range(2):
            dq2 = jnp.zeros((2 * BLK, 128), F32)
            for half in range(2):
                ds_b = _spread_pair(ds, hkv, half, tri)
                dq2 = dq2 + jnp.dot(ds_b, k_var[hkv][half], preferred_element_type=F32)
                dk_acc[hkv][half] = lax.dot_general(ds_b, q2[hkv], TN, preferred_element_type=F32)
                dv_acc[hkv][half] = lax.dot_general(p_b[hkv, half], do2[hkv], TN, preferred_element_type=F32)
            for t in range(2):
                lo = C_Q + (2 * hkv + t) * 128
                dproj_ref[:, lo:lo + 128] = (_rows(dq2, t) * 0.125).astype(BF16)

        low = lax.broadcasted_iota(jnp.int32, (2 * BLK, 128), 1) < 64

        def gather_heads(acc):
            return jnp.where(low, acc[0][0] + pltpu.roll(acc[0][1], 64, axis=1),
                             pltpu.roll(acc[1][0], 64, axis=1) + acc[1][1])

        dk = gather_heads(dk_acc) * 0.125
        dv = gather_heads(dv_acc)
        dproj_ref[:, C_K:C_V] = (dk[BLK:, :] + dkv_ref[:, 0:128]).astype(BF16)
        dproj_ref[:, C_V:C_AG] = (dv[BLK:, :] + dkv_ref[:, 128:256]).astype(BF16)
        dkv_ref[:, 0:128] = dk[:BLK, :]
        dkv_ref[:, 128:256] = dv[:BLK, :]

    rev = lambda w: pl.BlockSpec((BLK, w), lambda s: (NB - 1 - s, 0))
    prev = lambda w: pl.BlockSpec((BLK, w), lambda s: (jnp.maximum(NB - 2 - s, 0), 0))
    halo = pl.BlockSpec((HALO, 512), lambda s: (jnp.maximum((NB - 1 - s) * (BLK // HALO) - 1, 0), 0))
    return pl.pallas_call(
        body, name="bwd_mix", grid=(NB,),
        in_specs=[rev(512), halo, rev(512), rev(512), rev(256), prev(256), rev(512), rev(D),
                  _layer(l, 4, 128, 128), _layer(l, 1, 512), pl.BlockSpec(memory_space=pltpu.SMEM),
                  pl.BlockSpec((None, N_HEADS, BLK, BLK), lambda s: (jnp.minimum(NB - 1 - s, 1), 0, 0, 0)),
                  _whole((BLK, BLK))] + [ANY] * len(deps),
        out_specs=[rev(D_IN), _layer(l, 4, 128, 128),
                   pl.BlockSpec((4, 128), lambda s: (0, 0)), pl.BlockSpec((8, 128), lambda s: (0, 0))],
        out_shape=[jax.ShapeDtypeStruct((S, D_IN), BF16), jax.ShapeDtypeStruct((DEPTH, 4, 128, 128), F32),
                   jax.ShapeDtypeStruct((4, 128), F32), jax.ShapeDtypeStruct((8, 128), F32)],
        input_output_aliases={} if dpw_dest is None else {12 + len(deps): 1},
        scratch_shapes=[pltpu.VMEM((POOL_ROWS, 512), F32)] * 4 + [pltpu.VMEM((BLK, 256), F32)],
        compiler_params=_params(),
    )(pu, pu, pg, q, kv, kv, ag, dcat, pool_w, pool_scale, sinks, bias, tri, *deps)


def _bwd_in_dw(l, dproj, x, g_pre, place_arr, deps=()):
    n_steps = S // TM

    def body(dp_ref, x_ref, g_ref, place_ref, *rest):
        own_ref, dwb_ref, dw_ref = rest[len(deps):]
        step = pl.program_id(0)

        @pl.when(step == 0)
        def _():
            dw_ref[...] = jnp.zeros_like(dw_ref)

        xt = x_ref[...]
        r = lax.rsqrt(jnp.mean(xt * xt, axis=-1, keepdims=True) + EPS)
        h = (xt * r * g_ref[...]).astype(BF16)
        dw_ref[...] += lax.dot_general(dp_ref[...], h, TN, preferred_element_type=F32)

        @pl.when(step == n_steps - 1)
        def _():
            dwb_ref[...] = dw_ref[...].astype(BF16)
            own_ref[...] = _own_piece(dw_ref, place_ref)

    row = lambda w: pl.BlockSpec((TM, w), lambda i: (i, 0))
    full = _whole
    return pl.pallas_call(
        body, name="bwd_in_dw", grid=(n_steps,),
        in_specs=[row(D_IN), row(D), _layer(l, 1, D), pl.BlockSpec(memory_space=pltpu.SMEM)] + [ANY] * len(deps),
        out_specs=[full((D_IN // 8, D)), full((D_IN, D))],
        out_shape=[jax.ShapeDtypeStruct((D_IN // 8, D), F32), jax.ShapeDtypeStruct((D_IN, D), BF16)],
        scratch_shapes=[pltpu.VMEM((D_IN, D), F32)],
        compiler_params=_params(),
    )(dproj, x, g_pre, place_arr, *deps)


def _bwd_in_dx(l, dproj, w_in_t, x, g_pre, dres, deps=(), dw_place=None):
    n_steps = S // TM
    with_dw = dw_place is not None

    def body(dp_ref, w_ref, x_ref, g_ref, dres_ref, *rest):
        place_ref = rest[0] if with_dw else None
        rest = rest[with_dw + len(deps):]
        if with_dw:
            dx_ref, dg_ref, own_ref, dwb_ref, acc_ref, dw_ref = rest
        else:
            dx_ref, dg_ref, acc_ref = rest
        step = pl.program_id(0)

        @pl.when(step == 0)
        def _():
            acc_ref[...] = jnp.zeros_like(acc_ref)
            if with_dw:
                dw_ref[...] = jnp.zeros_like(dw_ref)

        g = g_ref[...]
        halves = [slice(k * (TM // 2), (k + 1) * (TM // 2)) for k in range(2)]
        dh = [jnp.dot(dp_ref[rows, :], w_ref[...], preferred_element_type=F32) for rows in halves]
        h = []
        for rows, dh_k in zip(halves, dh):
            xt = x_ref[rows, :]
            r = lax.rsqrt(jnp.mean(xt * xt, axis=-1, keepdims=True) + EPS)
            xn = xt * r
            acc_ref[...] += _rows8(dh_k * xn)
            a = dh_k * g
            dx_ref[rows, :] = dres_ref[rows, :] + (
                r * a - xt * (r * r * r) * jnp.mean(a * xt, axis=-1, keepdims=True))
            h.append((xn * g).astype(BF16))
        if with_dw:
            dw_ref[...] += lax.dot_general(dp_ref[...], jnp.concatenate(h, axis=0), TN, preferred_element_type=F32)

        @pl.when(step == n_steps - 1)
        def _():
            _store_lane_rows(dg_ref, acc_ref[...])
            if with_dw:
                dwb_ref[...] = dw_ref[...].astype(BF16)
                own_ref[...] = _own_piece(dw_ref, place_ref)

    row = lambda w: pl.BlockSpec((TM, w), lambda i: (i, 0))
    full = _whole
    dw_specs = [full((D_IN // 8, D)), full((D_IN, D))] if with_dw else []
    dw_shapes = [jax.ShapeDtypeStruct((D_IN // 8, D), F32), jax.ShapeDtypeStruct((D_IN, D), BF16)] if with_dw else []
    return pl.pallas_call(
        body, name="bwd_in" if with_dw else "bwd_in_dx", grid=(n_steps,),
        in_specs=[row(D_IN), full((D_IN, D)), row(D), _layer(l, 1, D), row(D)]
        + [pl.BlockSpec(memory_space=pltpu.SMEM)] * with_dw + [ANY] * len(deps),
        out_specs=[row(D), full((8, 128))] + dw_specs,
        out_shape=[jax.ShapeDtypeStruct((S, D), F32), jax.ShapeDtypeStruct((8, 128), F32)] + dw_shapes,
        scratch_shapes=[pltpu.VMEM((8, D), F32)] + [pltpu.VMEM((D_IN, D), F32)] * with_dw,
        compiler_params=_params(),
    )(dproj, w_in_t, x, g_pre, dres, *((dw_place,) if with_dw else ()), *deps)


HBM =pl.BlockSpec(memory_space=pltpu.HBM)
SEM = pl.BlockSpec(memory_space=pltpu.SEMAPHORE)
def _split_copy(collective_id=None):
    return pltpu.CompilerParams(has_side_effects=pltpu.SideEffectType.DATAFLOW_SIDE_EFFECTING,
                                collective_id=collective_id)


SPLIT_COPY = _split_copy()


def _in_hbm(a):
    return pltpu.with_memory_space_constraint(a, pltpu.HBM)

def _place():
    return lax.axis_index("x"), lax.axis_index("y"), lax.axis_index("c")


def _other_chips(x, y):
    return [(1 - x, y), (x, 1 - y), (1 - x, 1 - y)]


def _peer(x, y, c, m):
    return (x ^ (m >> 2), y ^ ((m >> 1) & 1), c ^ (m & 1))


SAME_CORE = (2, 4, 6)


def _place_cast(name, src, chip_arr, tile, layers, deps=()):
    _, n, cols = src.shape
    steps = n // tile
    k = len(layers)

    def body(chip_ref, *refs):
        for s_ref, o_ref in zip(refs[:k], refs[k + len(deps):]):
            o_ref[...] = s_ref[...].astype(BF16)

    def layer_spec(l):
        return pl.BlockSpec((None, tile, cols), lambda i, chip: (l, i, 0))

    return pl.pallas_call(
        body, name=name,
        grid_spec=pltpu.PrefetchScalarGridSpec(
            num_scalar_prefetch=1, grid=(steps,),
            in_specs=[layer_spec(l) for l in layers] + [ANY] * len(deps),
            out_specs=[pl.BlockSpec((tile, cols), lambda i, chip: (chip[0] * steps + i, 0))] * k),
        out_shape=[jax.ShapeDtypeStruct((N_SHARDS * n, cols), BF16)] * k,
        compiler_params=_params(),
    )(chip_arr, *[src] * k, *deps)


def _place_half(name, src, place_arr, layer, other, dest=None):
    _, n, cols = src.shape
    which = (lambda place: 1 - place[1]) if other else (lambda place: place[1])
    extra = [] if dest is None else [dest]

    def body(place_ref, s_ref, *rest):
        rest[-1][...] = s_ref[...].astype(BF16)

    return pl.pallas_call(
        body, name=name,
        grid_spec=pltpu.PrefetchScalarGridSpec(
            num_scalar_prefetch=1, grid=(1,),
            in_specs=[pl.BlockSpec((None, n // 2, cols), lambda i, place: (layer, which(place), 0))] + [ANY] * len(extra),
            out_specs=pl.BlockSpec((n // 2, cols), lambda i, place: (place[0] - place[1] + which(place), 0))),
        out_shape=jax.ShapeDtypeStruct((N_SHARDS * n, cols), BF16),
        input_output_aliases={2: 0} if extra else {},
        compiler_params=_params(),
    )(place_arr, src, *extra)


def _chip_rows(ref, chip, half=None):
    n = ref.shape[0] // N_SHARDS
    if half is None:
        return ref.at[pl.ds(pl.multiple_of(chip * n, 16), n), :]
    return ref.at[pl.ds(pl.multiple_of(chip * n + half * (n // 2), 16), n // 2), :]


def _gather_start(name, bufs, halved, collective_id):
    n = len(bufs)

    def body(*refs):
        ins, send, recv, token = refs[:n], refs[n:2 * n], refs[2 * n:3 * n], refs[-1]
        x, y, c = _place()
        _handshake([(*chip, c) for chip in _other_chips(x, y)])
        for a, buf in enumerate(ins):
            own = _chip_rows(buf, 2 * x + y, c if a in halved else None)
            for j, chip in enumerate(_other_chips(x, y)):
                pltpu.make_async_remote_copy(src_ref=own, dst_ref=own, send_sem=send[a].at[j], recv_sem=recv[a].at[j],
                                             device_id=(*chip, c), device_id_type=MESH).start()
        token[...] = jnp.zeros_like(token)

    outs = pl.pallas_call(
        body, name=name, in_specs=[HBM] * n,
        out_specs=[SEM] * (2 * n) + [HBM] * n + [pl.BlockSpec(memory_space=pltpu.VMEM)],
        out_shape=[pltpu.SemaphoreType.DMA((3,))] * (2 * n) + [pltpu.HBM(b.shape, b.dtype) for b in bufs]
        + [jax.ShapeDtypeStruct((8, 128), F32)],
        input_output_aliases={a: 2 * n + a for a in range(n)},
        compiler_params=_split_copy(collective_id),
    )(*[_in_hbm(b) for b in bufs])
    return outs[:n], outs[n:2 * n], outs[2 * n:3 * n], outs[-1]


def _gather_wait(name, buf, send_sem, recv_sem, after, halved=False):
    def body(buf_ref, send_ref, recv_ref, *rest):
        x, y, c = _place()
        half = c if halved else None
        own = _chip_rows(buf_ref, 2 * x + y, half)
        for j, chip in enumerate(_other_chips(x, y)):
            copy = pltpu.make_async_remote_copy(src_ref=own, dst_ref=_chip_rows(buf_ref, 2 * chip[0] + chip[1], half),
                                                send_sem=send_ref.at[j], recv_sem=recv_ref.at[j],
                                                device_id=(*chip, c), device_id_type=MESH)
            copy.wait_send()
            copy.wait_recv()

    return pl.pallas_call(
        body, name=name, in_specs=[HBM, SEM, SEM] + [ANY] * len(after), out_specs=HBM,
        out_shape=pltpu.HBM(buf.shape, buf.dtype), input_output_aliases={0: 0}, compiler_params=SPLIT_COPY,
    )(buf, send_sem, recv_sem, *after)


def _handshake(peers):
    barrier = pltpu.get_barrier_semaphore()
    for peer in peers:
        pl.semaphore_signal(barrier, inc=1, device_id=peer, device_id_type=MESH)
    pl.semaphore_wait(barrier, len(peers))


def _sibling_handshake(x, y, c):
    _handshake([(x, y, 1 - c)])


def _forward_halves(name, bufs, collective_id):
    n = len(bufs)

    def body(*refs):
        ins, outs, (send_sems, recv_sems) = refs[:n], refs[n:2 * n], refs[2 * n:]
        x, y, c = _place()
        _sibling_handshake(x, y, c)

        def copy(a, j, chip, half):
            rows = 2 * chip[0] + chip[1]
            return pltpu.make_async_remote_copy(
                src_ref=_chip_rows(ins[a], rows, half), dst_ref=_chip_rows(outs[a], rows, half),
                send_sem=send_sems.at[3 * a + j], recv_sem=recv_sems.at[3 * a + j], device_id=(x, y, 1 - c),
                device_id_type=MESH)

        copies = [(a, j, chip) for a in range(n) for j, chip in enumerate(_other_chips(x, y))]
        for a, j, chip in copies:
            copy(a, j, chip, c).start()
        for a, j, chip in copies:
            copy(a, j, chip, c).wait_send()
            copy(a, j, chip, 1 - c).wait_recv()

    return pl.pallas_call(
        body, name=name, in_specs=[ANY] * n, out_specs=[ANY] * n,
        out_shape=[jax.ShapeDtypeStruct(b.shape, b.dtype) for b in bufs],
        input_output_aliases={a: a for a in range(n)},
        scratch_shapes=[pltpu.SemaphoreType.DMA((3 * n,))] * 2,
        compiler_params=pltpu.CompilerParams(collective_id=collective_id),
    )(*bufs)


def _piece_rows(ref, k):
    p = ref.shape[0] // 8
    return ref.at[pl.ds(pl.multiple_of(k * p, 32 // jnp.dtype(ref.dtype).itemsize), p), :]


def _exchange_start(name, arrays, collective_id):
    n = len(arrays)
    zones = [lax.empty((7, a.shape[0] // 8, a.shape[1]), a.dtype) for a in arrays]

    def body(*refs):
        srcs, lands = refs[:n], refs[n:2 * n]
        send, recv, token = refs[2 * n:3 * n], refs[3 * n:4 * n], refs[-1]
        x, y, c = _place()
        _handshake([_peer(x, y, c, m) for m in range(1, 8)])
        for a, (src, land) in enumerate(zip(srcs, lands)):
            for m in range(1, 8):
                px, py, pc = _peer(x, y, c, m)
                pltpu.make_async_remote_copy(
                    src_ref=_piece_rows(src, 4 * px + 2 * py + pc), dst_ref=land.at[m - 1], send_sem=send[a].at[m - 1],
                    recv_sem=recv[a].at[m - 1], device_id=(px, py, pc), device_id_type=MESH).start()
        token[...] = jnp.zeros_like(token)

    outs = pl.pallas_call(
        body, name=name, in_specs=[HBM] * (2 * n),
        out_specs=[SEM] * (2 * n) + [HBM] * (2 * n) + [pl.BlockSpec(memory_space=pltpu.VMEM)],
        out_shape=[pltpu.SemaphoreType.DMA((7,))] * (2 * n) + [pltpu.HBM(a.shape, a.dtype) for a in arrays + zones]
        + [jax.ShapeDtypeStruct((8, 128), F32)],
        input_output_aliases={a: 2 * n + a for a in range(2 * n)},
        compiler_params=_split_copy(collective_id),
    )(*[_in_hbm(a) for a in arrays + zones])
    return outs[:n], outs[n:2 * n], outs[2 * n:3 * n], outs[3 * n:4 * n], outs[-1]


def _exchange_wait(name, started, after, which=None, with_sent=False):
    which = range(len(started[2])) if which is None else which
    send_sems, recv_sems, arrays, zones = [[group[k] for k in which] for group in started[:4]]
    n = len(arrays)

    def body(*refs):
        srcs, lands = refs[:n], refs[n:2 * n]
        send, recv = refs[2 * n:3 * n], refs[3 * n:4 * n]
        x, y, c = _place()
        for a, (src, land) in enumerate(zip(srcs, lands)):
            for m in range(1, 8):
                px, py, pc = _peer(x, y, c, m)
                copy = pltpu.make_async_remote_copy(
                    src_ref=_piece_rows(src, 4 * px + 2 * py + pc), dst_ref=land.at[m - 1], send_sem=send[a].at[m - 1],
                    recv_sem=recv[a].at[m - 1], device_id=(px, py, pc), device_id_type=MESH)
                copy.wait_send()
                copy.wait_recv()

    outs = pl.pallas_call(
        body, name=name, in_specs=[HBM] * (2 * n) + [SEM] * (2 * n) + [ANY], out_specs=[HBM] * (2 * n),
        out_shape=[pltpu.HBM(a.shape, a.dtype) for a in list(arrays) + list(zones)],
        input_output_aliases={a: a for a in range(2 * n)}, compiler_params=SPLIT_COPY,
    )(*arrays, *zones, *send_sems, *recv_sems, after)
    return outs if with_sent else outs[n:]


def _sum_pieces(name, weights, place_arr, dests=None, small=()):
    steps = 2
    flat = [item for items in weights for item in items]
    n = len(flat)

    def body(place_ref, *refs):
        first_out = len(refs) - len(weights) - len(small)
        outs = iter(refs[first_out:])
        small_refs = refs[first_out - 2 * len(small):first_out]

        @pl.when(pl.program_id(0) == 0)
        def _():
            for j in range(len(small)):
                total = small_refs[2 * j][...]
                for m in range(7):
                    total = total + small_refs[2 * j + 1][m]
                refs[first_out + len(weights) + j][...] = total

        k = 0
        for items in weights:
            out_ref = next(outs)
            for layer, _, _ in items:
                total = refs[k][...]
                for m in range(7):
                    total = total + refs[n + k][m].astype(F32)
                if len(items) == DEPTH:
                    out_ref[layer] = total
                else:
                    out_ref[...] = total
                k += 1

    def out_spec(items):
        _, own, _ = items[0]
        t, cols = own.shape[0] // steps, own.shape[1]
        if len(items) == DEPTH:
            return pl.BlockSpec((DEPTH, t, cols), lambda i, place: (0, place[1] * steps + i, 0))
        layer = items[0][0]
        return pl.BlockSpec((None, t, cols), lambda i, place: (layer, place[1] * steps + i, 0))

    owns = [own for _, own, _ in flat]
    dests = [] if dests is None else list(dests)
    piece = lambda a: pl.BlockSpec((a.shape[0] // 8, a.shape[1]), lambda i, place: (place[0], 0))
    small_specs = [spec for a, r in small for spec in (piece(a), pl.BlockSpec(r.shape, lambda i, place: (0, 0, 0)))]
    return pl.pallas_call(
        body, name=name,
        grid_spec=pltpu.PrefetchScalarGridSpec(
            num_scalar_prefetch=1, grid=(steps,),
            in_specs=[pl.BlockSpec((o.shape[0] // steps, o.shape[1]), lambda i, place: (i, 0)) for o in owns]
            + [pl.BlockSpec((7, o.shape[0] // steps, o.shape[1]), lambda i, place: (0, i, 0)) for o in owns]
            + [ANY] * len(dests) + small_specs,
            out_specs=[out_spec(items) for items in weights] + [piece(a) for a, _ in small]),
        out_shape=[jax.ShapeDtypeStruct((DEPTH, 2 * items[0][1].shape[0], items[0][1].shape[1]), F32)
                   for items in weights] + [jax.ShapeDtypeStruct(a.shape, F32) for a, _ in small],
        input_output_aliases={1 + 2 * n + k: k for k in range(len(dests))},
        compiler_params=_params(),
    )(place_arr, *owns, *[recv for _, _, recv in flat], *dests, *[a for pair in small for a in pair])


def _sum_small(name, partials, recvs, place_arr):
    n = len(partials)

    def body(place_ref, *refs):
        for o_ref, r_ref, out_ref in zip(refs[:n], refs[n:2 * n], refs[2 * n:]):
            total = o_ref[...]
            for m in range(7):
                total = total + r_ref[m]
            out_ref[...] = total

    piece = lambda a: pl.BlockSpec((a.shape[0] // 8, a.shape[1]), lambda i, place: (place[0], 0))
    return pl.pallas_call(
        body, name=name,
        grid_spec=pltpu.PrefetchScalarGridSpec(
            num_scalar_prefetch=1, grid=(1,),
            in_specs=[piece(a) for a in partials] + [pl.BlockSpec(r.shape, lambda i, place: (0, 0, 0)) for r in recvs],
            out_specs=[piece(a) for a in partials]),
        out_shape=[jax.ShapeDtypeStruct(a.shape, F32) for a in partials],
        compiler_params=_params(),
    )(place_arr, *partials, *recvs)


def _share(name, bufs, parts, gathered=(), collective_id=None):
    n, n_g = len(bufs), len(gathered)
    total = n + n_g

    def body(*refs):
        ins, outs = refs[:total], refs[total:2 * total]
        send_sems, recv_sems, send_g, recv_g = refs[2 * total:]
        x, y, c = _place()
        _handshake([_peer(x, y, c, m) for m in (SAME_CORE if gathered else ()) + (1,)])

        def half(ref, l, which):
            p = ref.shape[1] // 2
            return ref.at[l, pl.ds(pl.multiple_of(which * p, 8), p), :]

        def swap(k, which):
            a, l = parts[k]
            return pltpu.make_async_remote_copy(
                src_ref=half(ins[a], l, which), dst_ref=half(outs[a], l, which), send_sem=send_sems.at[k],
                recv_sem=recv_sems.at[k], device_id=(x, y, 1 - c), device_id_type=MESH)

        def spread(a, m, sender, held, to):
            k = 4 * sender[0] + 2 * sender[1] + sender[2]
            return pltpu.make_async_remote_copy(
                src_ref=_piece_rows(held[n + a], k), dst_ref=_piece_rows(outs[n + a], k),
                send_sem=send_g.at[7 * a + m - 1], recv_sem=recv_g.at[7 * a + m - 1], device_id=to, device_id_type=MESH)

        me, sibling = (x, y, c), (x, y, 1 - c)
        for k in range(len(parts)):
            swap(k, c).start()
        def own(a, m):
            return spread(a, m, me, ins, _peer(x, y, c, m))

        def handed_on(a, m):
            return spread(a, m + 1, _peer(x, y, c, m), outs, sibling)

        for a in range(n_g):
            for m in SAME_CORE + (1,):
                own(a, m).start()
        for a in range(n_g):
            for m in SAME_CORE:
                spread(a, m, _peer(x, y, c, m), ins, _peer(x, y, c, m)).wait_recv()
                handed_on(a, m).start()
        for k in range(len(parts)):
            swap(k, c).wait_send()
            swap(k, 1 - c).wait_recv()
        for a in range(n_g):
            for m in SAME_CORE + (1,):
                own(a, m).wait_send()
            for m in SAME_CORE:
                handed_on(a, m).wait_send()
                spread(a, m + 1, _peer(x, y, c, m + 1), ins, sibling).wait_recv()
            spread(a, 1, sibling, ins, sibling).wait_recv()

    arrays = list(bufs) + list(gathered)
    return pl.pallas_call(
        body, name=name, in_specs=[ANY] * total, out_specs=[ANY] * total,
        out_shape=[jax.ShapeDtypeStruct(b.shape, F32) for b in arrays],
        input_output_aliases={a: a for a in range(total)},
        scratch_shapes=[pltpu.SemaphoreType.DMA((max(len(parts), 1),))] * 2
        + [pltpu.SemaphoreType.DMA((max(7 * n_g, 1),))] * 2,
        compiler_params=pltpu.CompilerParams(collective_id=collective_id),
    )(*arrays)


def _adamw_math(w, g, m, v):
    nm = ADAM_B1 * m + (1.0 - ADAM_B1) * g
    nv = ADAM_B2 * v + (1.0 - ADAM_B2) * (g * g)
    m_hat = nm / (1.0 - ADAM_B1 ** ADAM_STEP)
    v_hat = nv / (1.0 - ADAM_B2 ** ADAM_STEP)
    return -ADAM_LR * (m_hat / (jnp.sqrt(v_hat) + ADAM_EPS) + ADAM_WD * w), nm, nv


def _adamw(name, w, g, m, v, rows_per_step, first=0, count=None, dests=None, deps=()):
    layers, rows, cols = w.shape
    count = layers if count is None else count

    def body(w_ref, g_ref, m_ref, v_ref, *rest):
        d_ref, nm_ref, nv_ref, g_out_ref = rest[-4:]
        d_ref[...], nm_ref[...], nv_ref[...] = _adamw_math(w_ref[...], g_ref[...], m_ref[...], v_ref[...])
        g_out_ref[...] = g_ref[...]

    spec = pl.BlockSpec((1, rows_per_step, cols), lambda l, i: (first + l, i, 0))
    shape = jax.ShapeDtypeStruct(w.shape, F32)
    dests = () if dests is None else tuple(dests)
    return pl.pallas_call(
        body, name=name, grid=(count, rows // rows_per_step),
        in_specs=[spec] * 4 + [ANY] * (len(dests) + len(deps)), out_specs=[spec] * 4, out_shape=[shape] * 4,
        input_output_aliases={4 + k: k for k in range(len(dests))},
        compiler_params=_params(("arbitrary", "arbitrary")),
    )(w, g, m, v, *dests, *deps)


def _pack_misc(pool_scale, sinks, norm_pre, norm_post):
    sink_rows = jnp.zeros((DEPTH, 8, 128), F32).at[:, 0, 0:N_HEADS].set(sinks).reshape(2 * 8, 128)
    return jnp.concatenate([pool_scale.reshape(8, 128), norm_pre.reshape(16, 128), norm_post.reshape(16, 128),
                            sink_rows, jnp.zeros((8, 128), F32)], axis=0)


def _adamw_small(w, g, m, v, pool):
    def body(w_ref, g_ref, m_ref, v_ref, pw_ref, pg_ref, pm_ref, pv_ref, *rest):
        outs, pool_outs, (d_ref, nm_ref, nv_ref) = rest[:17], rest[17:21], rest[21:]
        pool_outs[0][...] = pg_ref[...]
        pool_outs[1][...], pool_outs[2][...], pool_outs[3][...] = _adamw_math(
            pw_ref[...], pg_ref[...], pm_ref[...], pv_ref[...])
        d_ref[...], nm_ref[...], nv_ref[...] = _adamw_math(w_ref[...], g_ref[...], m_ref[...], v_ref[...])
        for k, src in enumerate([g_ref, d_ref, nm_ref, nv_ref]):
            scale, sinks, pre, post = outs[4 * k:4 * k + 4]
            for l in range(DEPTH):
                for j in range(4):
                    scale[l:l + 1, j * 128:(j + 1) * 128] = src[MISC_SCALE + 4 * l + j:MISC_SCALE + 4 * l + j + 1, :]
                for j in range(8):
                    pre[l:l + 1, j * 128:(j + 1) * 128] = src[MISC_PRE + 8 * l + j:MISC_PRE + 8 * l + j + 1, :]
                    post[l:l + 1, j * 128:(j + 1) * 128] = src[MISC_POST + 8 * l + j:MISC_POST + 8 * l + j + 1, :]
                sinks[l:l + 1, :] = src[MISC_SINKS + 8 * l:MISC_SINKS + 8 * l + 1, 0:N_HEADS]
        outs[16][...] = g_ref[MISC_LOSS:MISC_LOSS + 1, 0:1]

    vmem = pl.BlockSpec(memory_space=pltpu.VMEM)
    shapes = [(DEPTH, D_POOL), (DEPTH, N_HEADS), (DEPTH, D), (DEPTH, D)] * 4 + [(1, 1)]
    shapes += [pool[0].shape] * 4
    return pl.pallas_call(
        body, name="adamw_small", in_specs=[vmem] * 8, out_specs=[vmem] * 21,
        out_shape=[jax.ShapeDtypeStruct(s, F32) for s in shapes],
        scratch_shapes=[pltpu.VMEM((MISC_ROWS, 128), F32)] * 3,
    )(w, g, m, v, *pool)


def kernel(x, w_in, pool_w, pool_scale, attn_sinks, w_out, norm_pre, norm_post, loss_target, m_w_in, m_pool_w, m_pool_scale, m_attn_sinks, m_w_out, m_norm_pre, m_norm_post, v_w_in, v_pool_w, v_pool_scale, v_attn_sinks, v_w_out, v_norm_pre, v_norm_post):
    cx, cy, cc = _place()
    chip_arr = jnp.reshape(2 * cx + cy, (1,)).astype(jnp.int32)
    place_arr = jnp.stack([4 * cx + 2 * cy + cc, cc]).astype(jnp.int32)
    t = lambda a: jnp.transpose(a, (0, 2, 1))
    w_in_t = t(w_in)
    xs, target = x[0], loss_target[0]
    pool_w_b = pool_w.astype(BF16)
    tables = _attention_tables()
    scale3 = pool_scale.reshape(DEPTH, 1, D_POOL)
    pre3 = norm_pre.reshape(DEPTH, 1, D)
    post3 = norm_post.reshape(DEPTH, 1, D)

    wi0 = _place_half("place_w_in0", w_in_t, place_arr, 0, False)
    first = _gather_start("gather_start_first", [wi0], halved=(0,), collective_id=ID_GATHER_FIRST)
    wi0 = _place_half("place_w_in0_rest", w_in_t, place_arr, 0, True, dest=first[2][0])
    (wi1,) = _place_cast("place_w_in1", w_in_t, chip_arr, 288, [1], deps=(first[3],))
    wo = _place_cast("place_w_out", w_out, chip_arr, 256, [0, 1], deps=(first[3],))
    rest = _gather_start("gather_start_rest", [wi1, wo[0], wo[1]], halved=(0, 1), collective_id=ID_GATHER_REST)
    send, recv, bufs = [first[k] + rest[k] for k in range(3)]
    bufs = [wi0, *bufs[1:]]
    order = {(0, "in"): 0, (1, "in"): 1, (0, "out"): 2, (1, "out"): 3}

    saved = []
    packed = [_pack_misc(pool_scale, attn_sinks, norm_pre, norm_post),
              _pack_misc(m_pool_scale, m_attn_sinks, m_norm_pre, m_norm_post),
              _pack_misc(v_pool_scale, v_attn_sinks, v_norm_pre, v_norm_post)]
    after = (first[3], rest[3], pool_w_b, *tables, scale3, pre3, post3, *packed)
    below = None
    for l in range(DEPTH):
        k = order[l, "in"]
        halves = [_gather_wait(f"gather_wait_in{l}", bufs[k], send[k], recv[k], after, halved=True)]
        if below is not None:
            halves.append(below[1])
        w_in_l, *w_out_below = _forward_halves(f"forward_w{l}", halves, collective_id=ID_FORWARD[l])
        if below is None:
            pu, pg, q, kv, ag = _fwd_in(l, xs, pre3, w_in_l)
        else:
            saved[l - 1][9] = w_out_below[0]
            y, xs, pu, pg, q, kv, ag = _fwd_in(l, xs, pre3, w_in_l, (below[0], w_out_below[0], below[2]))
            saved[l - 1][7] = y
        cat = _fwd_mix(l, pu, pg, q, kv, ag, pool_w_b, scale3, attn_sinks, tables)
        k = order[l, "out"]
        w_out_l = _gather_wait(f"gather_wait_out{l}", bufs[k], send[k], recv[k], (cat,), halved=l + 1 < DEPTH)
        saved.append([xs, pu, pg, q, kv, ag, cat, None, w_in_l, w_out_l])
        below, after = (cat, w_out_l, post3), (w_out_l,)

    x_in, pu, pg, q, kv, ag, cat, y, w_in_l, w_out_l = saved[1]
    dcat, dw_out1, dw_out1_b, dg_post1, loss, xs = _bwd_out(1, cat, w_out_l, post3, place_arr, x=x_in, target=target)
    ex1_out = _exchange_start("exchange_start_out1", [dw_out1_b], ID_OUT1)
    dproj, dpw, dsc1, dsink1 = _bwd_mix(1, pu, pg, q, kv, ag, dcat, pool_w_b, scale3, attn_sinks, tables,
                                        deps=(ex1_out[4],))
    dx, dg_pre1, dw_in1, dw_in1_b = _bwd_in_dx(1, dproj, w_in_l, x_in, pre3, xs, dw_place=place_arr)
    ex1_in = _exchange_start("exchange_start_in1", [dw_in1_b], ID_IN1)

    x_in, pu, pg, q, kv, ag, cat, y, w_in_l, w_out_l = saved[0]
    dcat, dw_out0, dw_out0_b, dg_post0 = _bwd_out(0, cat, w_out_l, post3, place_arr, dxn=dx, y=y, deps=(ex1_in[4],))
    ex0_out = _exchange_start("exchange_start_out0", [dw_out0_b], ID_OUT0)
    dproj, dpw, dsc0, dsink0 = _bwd_mix(0, pu, pg, q, kv, ag, dcat, pool_w_b, scale3, attn_sinks, tables,
                                        deps=(ex0_out[4],), dpw_dest=dpw)
    dw_in0, dw_in0_b = _bwd_in_dw(0, dproj, x_in, pre3, place_arr)
    flat = lambda a: a.reshape(DEPTH * 4 * 128, 128)
    ex0_in = _exchange_start("exchange_start_in0", [flat(dpw), dw_in0_b], ID_IN0)

    grad_x, dg_pre0 = _bwd_in_dx(0, dproj, w_in_l, x_in, pre3, dx, deps=(ex0_in[4],))
    small = [jnp.concatenate([dsc0, dsc1, dg_pre0, dg_pre1, dg_post0, dg_post1, dsink0, dsink1, loss], axis=0)]
    ex_small = _exchange_start("exchange_start_small", small, ID_SMALL)
    (recv_out1,) = _exchange_wait("exchange_wait_out1", ex1_out, ex_small[4])
    (recv_in1,) = _exchange_wait("exchange_wait_in1", ex1_in, recv_out1)
    g_in, g_out = _sum_pieces("sum_pieces_1", [[(1, dw_in1, recv_in1)], [(1, dw_out1, recv_out1)]], place_arr)
    (recv_out0,) = _exchange_wait("exchange_wait_out0", ex0_out, g_out)
    (g_out,) = _sum_pieces("sum_pieces_out0", [[(0, dw_out0, recv_out0)]], place_arr, dests=[g_out])
    g_in, g_out = _share("share_a", [g_in, g_out], [(0, 1), (1, 0), (1, 1)], collective_id=ID_SHARE_A)
    m_in_t, v_in_t = t(m_w_in), t(v_w_in)
    d_out, nm_out, nv_out, grad_w_out = _adamw("adamw_w_out", w_out, g_out, m_w_out, v_w_out, 256)
    upd_in = _adamw("adamw_w_in1", w_in_t, g_in, m_in_t, v_in_t, 288, first=1, count=1, deps=(d_out,))
    dpw_own, recv_pw = _exchange_wait("exchange_wait_pool", ex0_in, upd_in[0], which=[0], with_sent=True)
    (g_pw,) = _sum_small("sum_pool", [dpw_own], [recv_pw], place_arr)

    (recv_in0,) = _exchange_wait("exchange_wait_in0", ex0_in, g_pw, which=[1])
    (recv_misc,) = _exchange_wait("exchange_wait_small", ex_small, recv_in0)
    g_in, g_misc = _sum_pieces("sum_pieces_in0", [[(0, dw_in0, recv_in0)]], place_arr, dests=[g_in],
                               small=[(small[0], recv_misc)])
    g_in, g_pw, g_misc = _share("share_b", [g_in], [(0, 0)], [g_pw, g_misc], collective_id=ID_SHARE_B)
    d_in, nm_in, nv_in, grad_w_in_t = _adamw("adamw_w_in0", w_in_t, g_in, m_in_t, v_in_t, 288, first=0, count=1,
                                             dests=upd_in)
    small_out = _adamw_small(packed[0], g_misc, packed[1], packed[2],
                             (flat(pool_w), g_pw, flat(m_pool_w), flat(v_pool_w)))
    (g_sc, g_sk, g_pre, g_post, d_sc, d_sk, d_pre, d_post,
     m_sc, m_sk, m_pre, m_post, v_sc, v_sk, v_pre, v_post, loss_sum) = small_out[:17]
    g_pw, d_pw, m_pw, v_pw = [a.reshape(pool_w.shape) for a in small_out[17:]]
    return (loss_sum[0, 0], grad_x[None], t(grad_w_in_t), g_pw, g_sc, g_sk, grad_w_out, g_pre, g_post,
            t(d_in), d_pw, d_sc, d_sk, d_out, d_pre, d_post,
            t(nm_in), m_pw, m_sc, m_sk, nm_out, m_pre, m_post,
            t(nv_in), v_pw, v_sc, v_sk, nv_out, v_pre, v_post)
```

```python
import jax
import jax.numpy as jnp
from jax import lax
from jax.experimental import pallas as pl
from jax.experimental.pallas import tpu as pltpu

F32 = jnp.float32
BF16 = jnp.bfloat16

S = 2048
D = 1024
DEPTH = 2
D_POOL = 512
POOL_WINDOWS = (2, 4, 8, 16)
N_HEADS = 8
D_IN = 2304
N_SHARDS = 4
W_IN_SHARD = D_IN // N_SHARDS
W_OUT_SHARD = D // N_SHARDS
BLK = 128
NB = S // BLK
HALO = 16
PAD = 8
EPS = 1e-6
NEG_INF = -1e30
C_PU, C_PG, C_Q, C_K, C_V, C_AG = 0, 512, 1024, 1536, 1664, 1792

ADAM_LR = 0.001
ADAM_B1 = 0.9
ADAM_B2 = 0.999
ADAM_EPS = 1e-08
ADAM_WD = 0.01
ADAM_STEP = 10

TM = 512
VMEM_LIMIT = 56 * 1024 * 1024

NT = (((1,), (1,)), ((), ()))
TN = (((0,), (0,)), ((), ()))

MESH = pl.DeviceIdType.MESH
ANY = pl.BlockSpec(memory_space=pl.ANY)

ID_FORWARD = (0, 1)
(ID_SHARE_A, ID_SHARE_B, ID_GATHER_FIRST, ID_GATHER_REST, ID_OUT1, ID_IN1, ID_OUT0, ID_IN0, ID_SMALL) = range(2, 11)

MISC_SCALE, MISC_PRE, MISC_POST, MISC_SINKS, MISC_LOSS = 0, 8, 24, 40, 56
MISC_ROWS = 64


def _params(sem=("arbitrary",)):
    return pltpu.CompilerParams(dimension_semantics=sem, vmem_limit_bytes=VMEM_LIMIT)


def _sigmoid(v):
    return 1.0 / (1.0 + jnp.exp(-v))


def _rows8(v):
    r, c = v.shape
    return v.reshape(r // 8, 8, c).sum(axis=0)


def _layer(l, *shape):
    zeros = (0,) * len(shape)
    return pl.BlockSpec((None,) + shape, lambda i: (l,) + zeros)


def _whole(shape):
    zeros = (0,) * len(shape)
    return pl.BlockSpec(shape, lambda i: zeros, pipeline_mode=pl.Buffered(1))


def _fwd_in(l, x, g_pre, w_in_t, below=None):
    fused = below is not None

    def body(x_ref, g_ref, w_ref, *rest):
        if fused:
            cat_ref, wo_ref, gp_ref, y_ref, xn_ref = rest[:5]
            y = jnp.dot(cat_ref[...], wo_ref[...], preferred_element_type=F32)
            y_ref[...] = y
            xt = x_ref[...] + y * lax.rsqrt(jnp.mean(y * y, axis=-1, keepdims=True) + EPS) * gp_ref[...]
            xn_ref[...] = xt
        else:
            xt = x_ref[...]
        pu_ref, pg_ref, q_ref, kv_ref, ag_ref = rest[-5:]
        r = lax.rsqrt(jnp.mean(xt * xt, axis=-1, keepdims=True) + EPS)
        h = (xt * r * g_ref[...]).astype(BF16)

        def proj(lo, hi):
            return lax.dot_general(h, w_ref[lo:hi, :], NT, preferred_element_type=F32)

        pu_ref[...] = proj(C_PU, C_PG)
        pg_ref[...] = proj(C_PG, C_Q)
        q_ref[...] = proj(C_Q, C_K).astype(BF16)
        kv_ref[...] = proj(C_K, C_AG).astype(BF16)
        ag_ref[...] = proj(C_AG, D_IN)

    row = lambda w: pl.BlockSpec((TM, w), lambda i: (i, 0))
    act = jax.ShapeDtypeStruct((S, D), F32)
    return pl.pallas_call(
        body, name="fwd_out_in" if fused else "fwd_in", grid=(S // TM,),
        in_specs=[row(D), _layer(l, 1, D), _whole((D_IN, D))]
        + ([row(D), _whole((D, D)), _layer(l - 1, 1, D)] if fused else []),
        out_specs=[row(D)] * (2 * fused) + [row(512), row(512), row(512), row(256), row(512)],
        out_shape=[act] * (2 * fused)
        + [jax.ShapeDtypeStruct((S, 512), F32), jax.ShapeDtypeStruct((S, 512), F32),
           jax.ShapeDtypeStruct((S, 512), BF16), jax.ShapeDtypeStruct((S, 256), BF16),
           jax.ShapeDtypeStruct((S, 512), F32)],
        compiler_params=_params(),
    )(x, g_pre, w_in_t, *(below if fused else ()))


LOG2E = 1.4426950408889634
SCORE_SCALE = 0.125 * LOG2E


def _attention_tables():
    qi = jnp.arange(BLK)[:, None]
    kj = jnp.arange(BLK)[None, :]
    dist = ((qi - kj) % BLK).astype(F32)
    slopes = jnp.exp2(-jnp.arange(1, N_HEADS + 1, dtype=F32))
    bias = -(slopes * LOG2E)[:, None, None] * dist[None]
    first = jnp.where(kj > qi, NEG_INF, bias)
    return jnp.stack([first, bias]), (kj <= qi).astype(BF16)


def _own_block_mask():
    return lax.broadcasted_iota(jnp.int32, (BLK, BLK), 1) <= lax.broadcasted_iota(jnp.int32, (BLK, BLK), 0)


def _merge(full, own):
    return jnp.where(own, full[:, BLK:], full[:, :BLK])


def _spread(v, tri):
    own = v * tri
    return jnp.concatenate([v - own, own], axis=1)


def _head_variants(cur, prev):
    both = jnp.concatenate([prev, cur], axis=0).astype(F32)
    swapped = pltpu.roll(both, 64, axis=1)
    low = lax.broadcasted_iota(jnp.int32, both.shape, 1) < 64
    zero = jnp.zeros_like(both)
    return ((jnp.where(low, both, zero).astype(BF16), jnp.where(low, zero, swapped).astype(BF16)),
            (jnp.where(low, swapped, zero).astype(BF16), jnp.where(low, zero, both).astype(BF16)))


def _head_of(hkv, t, half):
    return hkv * 4 + 2 * t + half


def _rows(v, t):
    return v[t * BLK:(t + 1) * BLK]


def _stack_tiles(ref, hkv, offset=0):
    lo = offset + 2 * hkv * 128
    return jnp.concatenate([ref[:, lo:lo + 128], ref[:, lo + 128:lo + 256]], axis=0)


def _scores(q2, k_var, own):
    s = {}
    for hkv in range(2):
        for half in range(2):
            full = lax.dot_general(q2[hkv], k_var[hkv][half], NT, preferred_element_type=F32)
            for t in range(2):
                s[hkv, t, half] = _merge(_rows(full, t), own)
    return s


def _softmax(s, bias, sink):
    s = s * SCORE_SCALE + bias
    sink2 = sink * LOG2E
    m = jnp.maximum(jnp.max(s, axis=-1, keepdims=True), sink2)
    p = jnp.exp2(s - m)
    e_sink = jnp.exp2(sink2 - m)
    inv = 1.0 / (jnp.sum(p, axis=-1, keepdims=True) + e_sink)
    return p * inv, e_sink * inv


def _spread_pair(v, hkv, half, tri):
    return jnp.concatenate([_spread(v[hkv, t, half].astype(BF16), tri) for t in range(2)], axis=0)


POOL_ROWS = PAD + HALO + BLK


def _window_sums(src_ref, tmp_refs, trailing):
    lo, hi = (PAD, POOL_ROWS) if trailing else (0, HALO + BLK)
    cur = src_ref
    for level in range(len(POOL_WINDOWS)):
        lanes = slice(level * 128, 512)
        shift = -(1 << level) if trailing else (1 << level)
        dst = tmp_refs[level % 2]
        dst[lo:hi, lanes] = cur[lo:hi, lanes] + cur[lo + shift:hi + shift, lanes]
        cur = dst


def _pool_block(ext_ref, tmp_refs, i, g, w):
    lanes = slice(g * 128, (g + 1) * 128)
    rows = slice(PAD + HALO, POOL_ROWS)
    t = (i * BLK + lax.broadcasted_iota(jnp.int32, (BLK, 1), 0)).astype(F32)
    inv = 1.0 / jnp.minimum(t + 1.0, float(w))
    return tmp_refs[g % 2][rows, lanes] * inv - ext_ref[rows, lanes], inv


def _fwd_mix(l, pu, pg, q, kv, ag, pool_w, pool_scale, sinks, tables):
    bias, tri = tables

    def body(pu_ref, pup_ref, pg_ref, q_ref, kv_ref, kvp_ref, ag_ref, pw_ref, sc_ref, sink_ref, bias_ref, tri_ref,
             cat_ref, ext_ref, *tmp_refs):
        i = pl.program_id(0)

        @pl.when(i == 0)
        def _():
            for ref in (ext_ref, *tmp_refs):
                ref[0:PAD, :] = jnp.zeros((PAD, 512), F32)

        ext_ref[PAD:PAD + HALO, :] = jnp.where(i > 0, pup_ref[...], 0.0)
        ext_ref[PAD + HALO:POOL_ROWS, :] = pu_ref[...]
        _window_sums(ext_ref, tmp_refs, True)
        for g, w in enumerate(POOL_WINDOWS):
            lanes = slice(g * 128, (g + 1) * 128)
            pooled, _ = _pool_block(ext_ref, tmp_refs, i, g, w)
            mixed = jnp.dot(pooled.astype(BF16), pw_ref[g], preferred_element_type=F32)
            gate = pg_ref[:, lanes]
            cat_ref[:, lanes] = (mixed * sc_ref[:, lanes] * (gate * _sigmoid(gate))).astype(BF16)

        own = _own_block_mask()
        tri = tri_ref[...]
        k_var = _head_variants(kv_ref[:, 0:128], kvp_ref[:, 0:128])
        v_var = _head_variants(kv_ref[:, 128:256], kvp_ref[:, 128:256])
        s = _scores([_stack_tiles(q_ref, hkv) for hkv in range(2)], k_var, own)
        p = {}
        for (hkv, t, half), s_head in s.items():
            head = _head_of(hkv, t, half)
            p[hkv, t, half], _ = _softmax(s_head, bias_ref[head], sink_ref[l, head])
        for hkv in range(2):
            o2 = jnp.zeros((2 * BLK, 128), F32)
            for half in range(2):
                o2 = o2 + jnp.dot(_spread_pair(p, hkv, half, tri), v_var[hkv][half], preferred_element_type=F32)
            for t in range(2):
                lo = (2 * hkv + t) * 128
                gate = ag_ref[:, lo:lo + 128]
                cat_ref[:, D_POOL + lo:D_POOL + lo + 128] = (_rows(o2, t) * (gate * _sigmoid(gate))).astype(BF16)

    blk = lambda w: pl.BlockSpec((BLK, w), lambda i: (i, 0))
    prev = lambda w: pl.BlockSpec((BLK, w), lambda i: (jnp.maximum(i - 1, 0), 0))
    halo = pl.BlockSpec((HALO, 512), lambda i: (jnp.maximum(i * (BLK // HALO) - 1, 0), 0))
    return pl.pallas_call(
        body, name="fwd_mix", grid=(NB,),
        in_specs=[blk(512), halo, blk(512), blk(512), blk(256), prev(256), blk(512),
                  _layer(l, 4, 128, 128), _layer(l, 1, 512), pl.BlockSpec(memory_space=pltpu.SMEM),
                  pl.BlockSpec((None, N_HEADS, BLK, BLK), lambda i: (jnp.minimum(i, 1), 0, 0, 0)), _whole((BLK, BLK))],
        out_specs=blk(D),
        out_shape=jax.ShapeDtypeStruct((S, D), BF16),
        scratch_shapes=[pltpu.VMEM((POOL_ROWS, 512), F32)] * 3,
        compiler_params=_params(),
    )(pu, pu, pg, q, kv, kv, ag, pool_w, pool_scale, sinks, bias, tri)


def _store_lane_rows(ref, acc):
    total = jnp.sum(acc, axis=0, keepdims=True)
    for k in range(ref.shape[0]):
        ref[k:k + 1, :] = total[:, k * 128:(k + 1) * 128]


def _own_piece(dw_ref, place_ref):
    p = dw_ref.shape[0] // 8
    return dw_ref[pl.ds(pl.multiple_of(place_ref[0] * p, 8), p), :]


def _bwd_out(l, cat, w_out, g_post, place_arr, dxn=None, y=None, x=None, target=None, deps=()):
    last = target is not None
    n_steps = S // TM

    def body(a_ref, b_ref, g_ref, cat_ref, w_ref, place_ref, *rest):
        dcat_ref, own_ref, dwb_ref, dg_ref = rest[len(deps):len(deps) + 4]
        rest = rest[len(deps) + 4:]
        acc_ref, dw_ref = rest[-2:]
        step = pl.program_id(0)

        @pl.when(step == 0)
        def _():
            dw_ref[...] = jnp.zeros_like(dw_ref)
            acc_ref[...] = jnp.zeros_like(acc_ref)

        cat = cat_ref[...]
        g = g_ref[...]
        y = jnp.dot(cat, w_ref[...], preferred_element_type=F32) if last else b_ref[...]
        r = lax.rsqrt(jnp.mean(y * y, axis=-1, keepdims=True) + EPS)
        if last:
            loss_ref, dx_ref, loss_acc_ref = rest[:3]
            err = a_ref[...] + y * r * g - b_ref[...]

            @pl.when(step == 0)
            def _():
                loss_acc_ref[...] = jnp.zeros_like(loss_acc_ref)

            loss_acc_ref[...] += _rows8(err * err)
            dz = err * (1.0 / D)
            dx_ref[...] = dz
        else:
            dz = a_ref[...]
        a = dz * g
        dy = r * a - y * (r * r * r) * jnp.mean(a * y, axis=-1, keepdims=True)
        acc_ref[...] += _rows8(dz * (y * r))
        dyb = dy.astype(BF16)
        dcat_ref[...] = lax.dot_general(dyb, w_ref[...], NT, preferred_element_type=F32)
        dw_ref[...] += lax.dot_general(cat, dyb, TN, preferred_element_type=F32)

        @pl.when(step == n_steps - 1)
        def _():
            _store_lane_rows(dg_ref, acc_ref[...])
            dwb_ref[...] = dw_ref[...].astype(BF16)
            own_ref[...] = _own_piece(dw_ref, place_ref)
            if last:
                loss_ref[...] = jnp.full((8, 128), (0.5 / D) * jnp.sum(loss_acc_ref[...]), F32)

    row = lambda: pl.BlockSpec((TM, D), lambda i: (i, 0))
    full = _whole
    return pl.pallas_call(
        body, name="out_loss_bwd" if last else "bwd_out", grid=(n_steps,),
        in_specs=[row(), row(), _layer(l, 1, D), row(), full((D, D)), pl.BlockSpec(memory_space=pltpu.SMEM)]
        + [ANY] * len(deps),
        out_specs=[row(), full((D // 8, D)), full((D, D)), full((8, 128))] + ([full((8, 128)), row()] if last else []),
        out_shape=[jax.ShapeDtypeStruct((S, D), F32), jax.ShapeDtypeStruct((D // 8, D), F32),
                   jax.ShapeDtypeStruct((D, D), BF16), jax.ShapeDtypeStruct((8, 128), F32)]
        + ([jax.ShapeDtypeStruct((8, 128), F32), jax.ShapeDtypeStruct((S, D), F32)] if last else []),
        scratch_shapes=([pltpu.VMEM((8, D), F32)] if last else []) + [pltpu.VMEM((8, D), F32), pltpu.VMEM((D, D), F32)],
        compiler_params=_params(),
    )(*((x, target) if last else (dxn, y)), g_post, cat, w_out, place_arr, *deps)


def _bwd_mix(l, pu, pg, q, kv, ag, dcat, pool_w, pool_scale, sinks, tables, deps=(), dpw_dest=None):
    bias, tri = tables
    deps = tuple(deps) + (() if dpw_dest is None else (dpw_dest,))

    def body(pu_ref, pup_ref, pg_ref, q_ref, kv_ref, kvp_ref, ag_ref, dcat_ref, pw_ref, sc_ref, sink_ref, bias_ref,
             tri_ref, *rest):
        dproj_ref, dpw_ref, dsc_ref, dsink_ref, ext_ref, dext_ref, tmp_a, tmp_b, dkv_ref = rest[len(deps):]
        tmp_refs = (tmp_a, tmp_b)
        step = pl.program_id(0)
        i = NB - 1 - step

        @pl.when(step == 0)
        def _():
            dpw_ref[...] = jnp.zeros_like(dpw_ref)
            dsc_ref[...] = jnp.zeros_like(dsc_ref)
            dsink_ref[...] = jnp.zeros_like(dsink_ref)
            for ref in (ext_ref, tmp_a, tmp_b):
                ref[0:PAD, :] = jnp.zeros((PAD, 512), F32)
            dext_ref[BLK:POOL_ROWS, :] = jnp.zeros((HALO + PAD, 512), F32)
            dkv_ref[...] = jnp.zeros_like(dkv_ref)

        ext_ref[PAD:PAD + HALO, :] = jnp.where(i > 0, pup_ref[...], 0.0)
        ext_ref[PAD + HALO:POOL_ROWS, :] = pu_ref[...]
        _window_sums(ext_ref, tmp_refs, True)
        dpooled = []
        for g, w in enumerate(POOL_WINDOWS):
            lanes = slice(g * 128, (g + 1) * 128)
            pooled, inv = _pool_block(ext_ref, tmp_refs, i, g, w)
            pooled_b = pooled.astype(BF16)
            mixed = jnp.dot(pooled_b, pw_ref[g], preferred_element_type=F32)
            scale = sc_ref[:, lanes]
            gate = pg_ref[:, lanes]
            sg = _sigmoid(gate)
            dpo = dcat_ref[:, lanes]
            dproj_ref[:, C_PG + g * 128:C_PG + (g + 1) * 128] = (
                dpo * (mixed * scale) * (sg * (1.0 + gate * (1.0 - sg)))).astype(BF16)
            dms = dpo * (gate * sg)
            dsc_ref[g:g + 1, :] += jnp.sum(dms * mixed, axis=0, keepdims=True)
            dmixed = (dms * scale).astype(BF16)
            dpw_ref[g] += lax.dot_general(pooled_b, dmixed, TN, preferred_element_type=F32)
            dpooled.append(lax.dot_general(dmixed, pw_ref[g], NT, preferred_element_type=F32))
            dext_ref[0:BLK, lanes] = dpooled[g] * inv
        _window_sums(dext_ref, tmp_refs, False)
        for g in range(len(POOL_WINDOWS)):
            lanes = slice(g * 128, (g + 1) * 128)
            dproj_ref[:, C_PU + g * 128:C_PU + (g + 1) * 128] = (tmp_refs[g % 2][0:BLK, lanes] - dpooled[g]).astype(BF16)
        dext_ref[BLK:BLK + HALO, :] = dext_ref[0:HALO, :]

        own = _own_block_mask()
        tri = tri_ref[...]
        k_var = _head_variants(kv_ref[:, 0:128], kvp_ref[:, 0:128])
        v_var = _head_variants(kv_ref[:, 128:256], kvp_ref[:, 128:256])
        q2 = [_stack_tiles(q_ref, hkv) for hkv in range(2)]
        s = _scores(q2, k_var, own)
        p, p_sink = {}, {}
        for key, s_head in s.items():
            head = _head_of(*key)
            p[key], p_sink[key] = _softmax(s_head, bias_ref[head], sink_ref[l, head])

        do2, p_b, dp = [], {}, {}
        for hkv in range(2):
            gate = _stack_tiles(ag_ref, hkv)
            sg = _sigmoid(gate)
            dca = _stack_tiles(dcat_ref, hkv, D_POOL)
            do2.append((dca * (gate * sg)).astype(BF16))
            o2 = jnp.zeros((2 * BLK, 128), F32)
            for half in range(2):
                p_b[hkv, half] = _spread_pair(p, hkv, half, tri)
                o2 = o2 + jnp.dot(p_b[hkv, half], v_var[hkv][half], preferred_element_type=F32)
                full = lax.dot_general(do2[hkv], v_var[hkv][half], NT, preferred_element_type=F32)
                for t in range(2):
                    dp[hkv, t, half] = _merge(_rows(full, t), own)
            dag = dca * o2 * (sg * (1.0 + gate * (1.0 - sg)))
            for t in range(2):
                lo = C_AG + (2 * hkv + t) * 128
                dproj_ref[:, lo:lo + 128] = _rows(dag, t).astype(BF16)

        ds = {}
        for key in p:
            delta = jnp.sum(p[key] * dp[key], axis=-1, keepdims=True)
            ds[key] = p[key] * (dp[key] - delta)
            head = _head_of(*key)
            dsink_ref[0:1, :] += jnp.where(lax.broadcasted_iota(jnp.int32, (1, 128), 1) == head,
                                           -jnp.sum(p_sink[key] * delta, axis=0, keepdims=True), 0.0)

        dk_acc = [[None, None], [None, None]]
        dv_acc = [[None, None], [None, None]]
        for hkv in range(2):
            dq2 = jnp.zeros((2 * BLK, 128), F32)
            for half in range(2):
                ds_b = _spread_pair(ds, hkv, half, tri)
                dq2 = dq2 + jnp.dot(ds_b, k_var[hkv][half], preferred_element_type=F32)
                dk_acc[hkv][half] = lax.dot_general(ds_b, q2[hkv], TN, preferred_element_type=F32)
                dv_acc[hkv][half] = lax.dot_general(p_b[hkv, half], do2[hkv], TN, preferred_element_type=F32)
            for t in range(2):
                lo = C_Q + (2 * hkv + t) * 128
                dproj_ref[:, lo:lo + 128] = (_rows(dq2, t) * 0.125).astype(BF16)

        low = lax.broadcasted_iota(jnp.int32, (2 * BLK, 128), 1) < 64

        def gather_heads(acc):
            return jnp.where(low, acc[0][0] + pltpu.roll(acc[0][1], 64, axis=1),
                             pltpu.roll(acc[1][0], 64, axis=1) + acc[1][1])

        dk = gather_heads(dk_acc) * 0.125
        dv = gather_heads(dv_acc)
        dproj_ref[:, C_K:C_V] = (dk[BLK:, :] + dkv_ref[:, 0:128]).astype(BF16)
        dproj_ref[:, C_V:C_AG] = (dv[BLK:, :] + dkv_ref[:, 128:256]).astype(BF16)
        dkv_ref[:, 0:128] = dk[:BLK, :]
        dkv_ref[:, 128:256] = dv[:BLK, :]

    rev = lambda w: pl.BlockSpec((BLK, w), lambda s: (NB - 1 - s, 0))
    prev = lambda w: pl.BlockSpec((BLK, w), lambda s: (jnp.maximum(NB - 2 - s, 0), 0))
    halo = pl.BlockSpec((HALO, 512), lambda s: (jnp.maximum((NB - 1 - s) * (BLK // HALO) - 1, 0), 0))
    return pl.pallas_call(
        body, name="bwd_mix", grid=(NB,),
        in_specs=[rev(512), halo, rev(512), rev(512), rev(256), prev(256), rev(512), rev(D),
                  _layer(l, 4, 128, 128), _layer(l, 1, 512), pl.BlockSpec(memory_space=pltpu.SMEM),
                  pl.BlockSpec((None, N_HEADS, BLK, BLK), lambda s: (jnp.minimum(NB - 1 - s, 1), 0, 0, 0)),
                  _whole((BLK, BLK))] + [ANY] * len(deps),
        out_specs=[rev(D_IN), _layer(l, 4, 128, 128),
                   pl.BlockSpec((4, 128), lambda s: (0, 0)), pl.BlockSpec((8, 128), lambda s: (0, 0))],
        out_shape=[jax.ShapeDtypeStruct((S, D_IN), BF16), jax.ShapeDtypeStruct((DEPTH, 4, 128, 128), F32),
                   jax.ShapeDtypeStruct((4, 128), F32), jax.ShapeDtypeStruct((8, 128), F32)],
        input_output_aliases={} if dpw_dest is None else {12 + len(deps): 1},
        scratch_shapes=[pltpu.VMEM((POOL_ROWS, 512), F32)] * 4 + [pltpu.VMEM((BLK, 256), F32)],
        compiler_params=_params(),
    )(pu, pu, pg, q, kv, kv, ag, dcat, pool_w, pool_scale, sinks, bias, tri, *deps)


def _bwd_in_dw(l, dproj, x, g_pre, place_arr, deps=()):
    n_steps = S // TM

    def body(dp_ref, x_ref, g_ref, place_ref, *rest):
        own_ref, dwb_ref, dw_ref = rest[len(deps):]
        step = pl.program_id(0)

        @pl.when(step == 0)
        def _():
            dw_ref[...] = jnp.zeros_like(dw_ref)

        xt = x_ref[...]
        r = lax.rsqrt(jnp.mean(xt * xt, axis=-1, keepdims=True) + EPS)
        h = (xt * r * g_ref[...]).astype(BF16)
        dw_ref[...] += lax.dot_general(dp_ref[...], h, TN, preferred_element_type=F32)

        @pl.when(step == n_steps - 1)
        def _():
            dwb_ref[...] = dw_ref[...].astype(BF16)
            own_ref[...] = _own_piece(dw_ref, place_ref)

    row = lambda w: pl.BlockSpec((TM, w), lambda i: (i, 0))
    full = _whole
    return pl.pallas_call(
        body, name="bwd_in_dw", grid=(n_steps,),
        in_specs=[row(D_IN), row(D), _layer(l, 1, D), pl.BlockSpec(memory_space=pltpu.SMEM)] + [ANY] * len(deps),
        out_specs=[full((D_IN // 8, D)), full((D_IN, D))],
        out_shape=[jax.ShapeDtypeStruct((D_IN // 8, D), F32), jax.ShapeDtypeStruct((D_IN, D), BF16)],
        scratch_shapes=[pltpu.VMEM((D_IN, D), F32)],
        compiler_params=_params(),
    )(dproj, x, g_pre, place_arr, *deps)


def _bwd_in_dx(l, dproj, w_in_t, x, g_pre, dres, deps=(), dw_place=None):
    n_steps = S // TM
    with_dw = dw_place is not None

    def body(dp_ref, w_ref, x_ref, g_ref, dres_ref, *rest):
        place_ref = rest[0] if with_dw else None
        rest = rest[with_dw + len(deps):]
        if with_dw:
            dx_ref, dg_ref, own_ref, dwb_ref, acc_ref, dw_ref = rest
        else:
            dx_ref, dg_ref, acc_ref = rest
        step = pl.program_id(0)

        @pl.when(step == 0)
        def _():
            acc_ref[...] = jnp.zeros_like(acc_ref)
            if with_dw:
                dw_ref[...] = jnp.zeros_like(dw_ref)

        g = g_ref[...]
        halves = [slice(k * (TM // 2), (k + 1) * (TM // 2)) for k in range(2)]
        dh = [jnp.dot(dp_ref[rows, :], w_ref[...], preferred_element_type=F32) for rows in halves]
        h = []
        for rows, dh_k in zip(halves, dh):
            xt = x_ref[rows, :]
            r = lax.rsqrt(jnp.mean(xt * xt, axis=-1, keepdims=True) + EPS)
            xn = xt * r
            acc_ref[...] += _rows8(dh_k * xn)
            a = dh_k * g
            dx_ref[rows, :] = dres_ref[rows, :] + (
                r * a - xt * (r * r * r) * jnp.mean(a * xt, axis=-1, keepdims=True))
            h.append((xn * g).astype(BF16))
        if with_dw:
            dw_ref[...] += lax.dot_general(dp_ref[...], jnp.concatenate(h, axis=0), TN, preferred_element_type=F32)

        @pl.when(step == n_steps - 1)
        def _():
            _store_lane_rows(dg_ref, acc_ref[...])
            if with_dw:
                dwb_ref[...] = dw_ref[...].astype(BF16)
                own_ref[...] = _own_piece(dw_ref, place_ref)

    row = lambda w: pl.BlockSpec((TM, w), lambda i: (i, 0))
    full = _whole
    dw_specs = [full((D_IN // 8, D)), full((D_IN, D))] if with_dw else []
    dw_shapes = [jax.ShapeDtypeStruct((D_IN // 8, D), F32), jax.ShapeDtypeStruct((D_IN, D), BF16)] if with_dw else []
    return pl.pallas_call(
        body, name="bwd_in" if with_dw else "bwd_in_dx", grid=(n_steps,),
        in_specs=[row(D_IN), full((D_IN, D)), row(D), _layer(l, 1, D), row(D)]
        + [pl.BlockSpec(memory_space=pltpu.SMEM)] * with_dw + [ANY] * len(deps),
        out_specs=[row(D), full((8, 128))] + dw_specs,
        out_shape=[jax.ShapeDtypeStruct((S, D), F32), jax.ShapeDtypeStruct((8, 128), F32)] + dw_shapes,
        scratch_shapes=[pltpu.VMEM((8, D), F32)] + [pltpu.VMEM((D_IN, D), F32)] * with_dw,
        compiler_params=_params(),
    )(dproj, w_in_t, x, g_pre, dres, *((dw_place,) if with_dw else ()), *deps)


HBM =pl.BlockSpec(memory_space=pltpu.HBM)
SEM = pl.BlockSpec(memory_space=pltpu.SEMAPHORE)
def _split_copy(collective_id=None):
    return pltpu.CompilerParams(has_side_effects=pltpu.SideEffectType.DATAFLOW_SIDE_EFFECTING,
                                collective_id=collective_id)


SPLIT_COPY = _split_copy()


def _in_hbm(a):
    return pltpu.with_memory_space_constraint(a, pltpu.HBM)

def _place():
    return lax.axis_index("x"), lax.axis_index("y"), lax.axis_index("c")


def _other_chips(x, y):
    return [(1 - x, y), (x, 1 - y), (1 - x, 1 - y)]


def _peer(x, y, c, m):
    return (x ^ (m >> 2), y ^ ((m >> 1) & 1), c ^ (m & 1))


SAME_CORE = (2, 4, 6)


def _place_cast(name, src, chip_arr, tile, layers, deps=()):
    _, n, cols = src.shape
    steps = n // tile
    k = len(layers)

    def body(chip_ref, *refs):
        for s_ref, o_ref in zip(refs[:k], refs[k + len(deps):]):
            o_ref[...] = s_ref[...].astype(BF16)

    def layer_spec(l):
        return pl.BlockSpec((None, tile, cols), lambda i, chip: (l, i, 0))

    return pl.pallas_call(
        body, name=name,
        grid_spec=pltpu.PrefetchScalarGridSpec(
            num_scalar_prefetch=1, grid=(steps,),
            in_specs=[layer_spec(l) for l in layers] + [ANY] * len(deps),
            out_specs=[pl.BlockSpec((tile, cols), lambda i, chip: (chip[0] * steps + i, 0))] * k),
        out_shape=[jax.ShapeDtypeStruct((N_SHARDS * n, cols), BF16)] * k,
        compiler_params=_params(),
    )(chip_arr, *[src] * k, *deps)


def _place_other_half(name, src, place_arr, layer, dest):
    _, n, cols = src.shape

    def body(place_ref, s_ref, dest_ref, o_ref):
        o_ref[...] = s_ref[...].astype(BF16)

    return pl.pallas_call(
        body, name=name,
        grid_spec=pltpu.PrefetchScalarGridSpec(
            num_scalar_prefetch=1, grid=(1,),
            in_specs=[pl.BlockSpec((None, n // 2, cols), lambda i, place: (layer, 1 - place[1], 0)), ANY],
            out_specs=pl.BlockSpec((n // 2, cols), lambda i, place: (place[0] + 1 - 2 * place[1], 0))),
        out_shape=jax.ShapeDtypeStruct((N_SHARDS * n, cols), BF16),
        input_output_aliases={2: 0},
        compiler_params=_params(),
    )(place_arr, src, dest)


def _chip_rows(ref, chip, half=None):
    n = ref.shape[0] // N_SHARDS
    if half is None:
        return ref.at[pl.ds(pl.multiple_of(chip * n, 16), n), :]
    return ref.at[pl.ds(pl.multiple_of(chip * n + half * (n // 2), 16), n // 2), :]


def _gather_start(name, bufs, halved, collective_id):
    n = len(bufs)

    def body(*refs):
        ins, send, recv, token = refs[:n], refs[n:2 * n], refs[2 * n:3 * n], refs[-1]
        x, y, c = _place()
        _handshake([(*chip, c) for chip in _other_chips(x, y)])
        for a, buf in enumerate(ins):
            own = _chip_rows(buf, 2 * x + y, c if a in halved else None)
            for j, chip in enumerate(_other_chips(x, y)):
                pltpu.make_async_remote_copy(src_ref=own, dst_ref=own, send_sem=send[a].at[j], recv_sem=recv[a].at[j],
                                             device_id=(*chip, c), device_id_type=MESH).start()
        token[...] = jnp.zeros_like(token)

    outs = pl.pallas_call(
        body, name=name, in_specs=[HBM] * n,
        out_specs=[SEM] * (2 * n) + [HBM] * n + [pl.BlockSpec(memory_space=pltpu.VMEM)],
        out_shape=[pltpu.SemaphoreType.DMA((3,))] * (2 * n) + [pltpu.HBM(b.shape, b.dtype) for b in bufs]
        + [jax.ShapeDtypeStruct((8, 128), F32)],
        input_output_aliases={a: 2 * n + a for a in range(n)},
        compiler_params=_split_copy(collective_id),
    )(*[_in_hbm(b) for b in bufs])
    return outs[:n], outs[n:2 * n], outs[2 * n:3 * n], outs[-1]


def _gather_start_cast(name, src, layer, collective_id):
    _, n, cols = src.shape
    half = n // 2

    def body(src_ref, send, recv, buf_ref, token, f32_ref, bf16_ref, local):
        x, y, c = _place()
        own = _chip_rows(buf_ref, 2 * x + y, c)

        def cast():
            load = pltpu.make_async_copy(src_ref.at[layer, pl.ds(pl.multiple_of(c * half, 16), half), :], f32_ref,
                                         local.at[0])
            load.start()
            load.wait()
            bf16_ref[...] = f32_ref[...].astype(BF16)
            store = pltpu.make_async_copy(bf16_ref, own, local.at[1])
            store.start()
            store.wait()

        _handshake([(*chip, c) for chip in _other_chips(x, y)], cast)
        for j, chip in enumerate(_other_chips(x, y)):
            pltpu.make_async_remote_copy(src_ref=own, dst_ref=own, send_sem=send.at[j], recv_sem=recv.at[j],
                                         device_id=(*chip, c), device_id_type=MESH).start()
        token[...] = jnp.zeros_like(token)

    outs = pl.pallas_call(
        body, name=name, in_specs=[HBM],
        out_specs=[SEM, SEM, HBM, pl.BlockSpec(memory_space=pltpu.VMEM)],
        out_shape=[pltpu.SemaphoreType.DMA((3,))] * 2 + [pltpu.HBM((N_SHARDS * n, cols), BF16),
                                                         jax.ShapeDtypeStruct((8, 128), F32)],
        scratch_shapes=[pltpu.VMEM((half, cols), F32), pltpu.VMEM((half, cols), BF16), pltpu.SemaphoreType.DMA((2,))],
        compiler_params=_split_copy(collective_id),
    )(_in_hbm(src))
    return outs[:1], outs[1:2], outs[2:3], outs[3]


def _gather_wait(name, buf, send_sem, recv_sem, after, halved=False):
    def body(buf_ref, send_ref, recv_ref, *rest):
        x, y, c = _place()
        half = c if halved else None
        own = _chip_rows(buf_ref, 2 * x + y, half)
        for j, chip in enumerate(_other_chips(x, y)):
            copy = pltpu.make_async_remote_copy(src_ref=own, dst_ref=_chip_rows(buf_ref, 2 * chip[0] + chip[1], half),
                                                send_sem=send_ref.at[j], recv_sem=recv_ref.at[j],
                                                device_id=(*chip, c), device_id_type=MESH)
            copy.wait_send()
            copy.wait_recv()

    return pl.pallas_call(
        body, name=name, in_specs=[HBM, SEM, SEM] + [ANY] * len(after), out_specs=HBM,
        out_shape=pltpu.HBM(buf.shape, buf.dtype), input_output_aliases={0: 0}, compiler_params=SPLIT_COPY,
    )(buf, send_sem, recv_sem, *after)


def _handshake(peers, meanwhile=None):
    barrier = pltpu.get_barrier_semaphore()
    for peer in peers:
        pl.semaphore_signal(barrier, inc=1, device_id=peer, device_id_type=MESH)
    if meanwhile is not None:
        meanwhile()
    pl.semaphore_wait(barrier, len(peers))


def _sibling_handshake(x, y, c):
    _handshake([(x, y, 1 - c)])


def _forward_halves(name, bufs, collective_id):
    n = len(bufs)

    def body(*refs):
        ins, outs, (send_sems, recv_sems) = refs[:n], refs[n:2 * n], refs[2 * n:]
        x, y, c = _place()
        _sibling_handshake(x, y, c)

        def copy(a, j, chip, half):
            rows = 2 * chip[0] + chip[1]
            return pltpu.make_async_remote_copy(
                src_ref=_chip_rows(ins[a], rows, half), dst_ref=_chip_rows(outs[a], rows, half),
                send_sem=send_sems.at[3 * a + j], recv_sem=recv_sems.at[3 * a + j], device_id=(x, y, 1 - c),
                device_id_type=MESH)

        copies = [(a, j, chip) for a in range(n) for j, chip in enumerate(_other_chips(x, y))]
        for a, j, chip in copies:
            copy(a, j, chip, c).start()
        for a, j, chip in copies:
            copy(a, j, chip, c).wait_send()
            copy(a, j, chip, 1 - c).wait_recv()

    return pl.pallas_call(
        body, name=name, in_specs=[ANY] * n, out_specs=[ANY] * n,
        out_shape=[jax.ShapeDtypeStruct(b.shape, b.dtype) for b in bufs],
        input_output_aliases={a: a for a in range(n)},
        scratch_shapes=[pltpu.SemaphoreType.DMA((3 * n,))] * 2,
        compiler_params=pltpu.CompilerParams(collective_id=collective_id),
    )(*bufs)


def _piece_rows(ref, k):
    p = ref.shape[0] // 8
    return ref.at[pl.ds(pl.multiple_of(k * p, 32 // jnp.dtype(ref.dtype).itemsize), p), :]


def _exchange_start(name, arrays, collective_id):
    n = len(arrays)
    zones = [lax.empty((7, a.shape[0] // 8, a.shape[1]), a.dtype) for a in arrays]

    def body(*refs):
        srcs, lands = refs[:n], refs[n:2 * n]
        send, recv, token = refs[2 * n:3 * n], refs[3 * n:4 * n], refs[-1]
        x, y, c = _place()
        _handshake([_peer(x, y, c, m) for m in range(1, 8)])
        for a, (src, land) in enumerate(zip(srcs, lands)):
            for m in range(1, 8):
                px, py, pc = _peer(x, y, c, m)
                pltpu.make_async_remote_copy(
                    src_ref=_piece_rows(src, 4 * px + 2 * py + pc), dst_ref=land.at[m - 1], send_sem=send[a].at[m - 1],
                    recv_sem=recv[a].at[m - 1], device_id=(px, py, pc), device_id_type=MESH).start()
        token[...] = jnp.zeros_like(token)

    outs = pl.pallas_call(
        body, name=name, in_specs=[HBM] * (2 * n),
        out_specs=[SEM] * (2 * n) + [HBM] * (2 * n) + [pl.BlockSpec(memory_space=pltpu.VMEM)],
        out_shape=[pltpu.SemaphoreType.DMA((7,))] * (2 * n) + [pltpu.HBM(a.shape, a.dtype) for a in arrays + zones]
        + [jax.ShapeDtypeStruct((8, 128), F32)],
        input_output_aliases={a: 2 * n + a for a in range(2 * n)},
        compiler_params=_split_copy(collective_id),
    )(*[_in_hbm(a) for a in arrays + zones])
    return outs[:n], outs[n:2 * n], outs[2 * n:3 * n], outs[3 * n:4 * n], outs[-1]


def _exchange_wait(name, started, after, which=None, with_sent=False):
    which = range(len(started[2])) if which is None else which
    send_sems, recv_sems, arrays, zones = [[group[k] for k in which] for group in started[:4]]
    n = len(arrays)

    def body(*refs):
        srcs, lands = refs[:n], refs[n:2 * n]
        send, recv = refs[2 * n:3 * n], refs[3 * n:4 * n]
        x, y, c = _place()
        for a, (src, land) in enumerate(zip(srcs, lands)):
            for m in range(1, 8):
                px, py, pc = _peer(x, y, c, m)
                copy = pltpu.make_async_remote_copy(
                    src_ref=_piece_rows(src, 4 * px + 2 * py + pc), dst_ref=land.at[m - 1], send_sem=send[a].at[m - 1],
                    recv_sem=recv[a].at[m - 1], device_id=(px, py, pc), device_id_type=MESH)
                copy.wait_send()
                copy.wait_recv()

    outs = pl.pallas_call(
        body, name=name, in_specs=[HBM] * (2 * n) + [SEM] * (2 * n) + [ANY], out_specs=[HBM] * (2 * n),
        out_shape=[pltpu.HBM(a.shape, a.dtype) for a in list(arrays) + list(zones)],
        input_output_aliases={a: a for a in range(2 * n)}, compiler_params=SPLIT_COPY,
    )(*arrays, *zones, *send_sems, *recv_sems, after)
    return outs if with_sent else outs[n:]


def _sum_pieces(name, weights, place_arr, dests=None, small=()):
    steps = 2
    flat = [item for items in weights for item in items]
    n = len(flat)

    def body(place_ref, *refs):
        first_out = len(refs) - len(weights) - len(small)
        outs = iter(refs[first_out:])
        small_refs = refs[first_out - 2 * len(small):first_out]

        @pl.when(pl.program_id(0) == 0)
        def _():
            for j in range(len(small)):
                total = small_refs[2 * j][...]
                for m in range(7):
                    total = total + small_refs[2 * j + 1][m]
                refs[first_out + len(weights) + j][...] = total

        k = 0
        for items in weights:
            out_ref = next(outs)
            for layer, _, _ in items:
                total = refs[k][...]
                for m in range(7):
                    total = total + refs[n + k][m].astype(F32)
                if len(items) == DEPTH:
                    out_ref[layer] = total
                else:
                    out_ref[...] = total
                k += 1

    def out_spec(items):
        _, own, _ = items[0]
        t, cols = own.shape[0] // steps, own.shape[1]
        if len(items) == DEPTH:
            return pl.BlockSpec((DEPTH, t, cols), lambda i, place: (0, place[1] * steps + i, 0))
        layer = items[0][0]
        return pl.BlockSpec((None, t, cols), lambda i, place: (layer, place[1] * steps + i, 0))

    owns = [own for _, own, _ in flat]
    dests = [] if dests is None else list(dests)
    piece = lambda a: pl.BlockSpec((a.shape[0] // 8, a.shape[1]), lambda i, place: (place[0], 0))
    small_specs = [spec for a, r in small for spec in (piece(a), pl.BlockSpec(r.shape, lambda i, place: (0, 0, 0)))]
    return pl.pallas_call(
        body, name=name,
        grid_spec=pltpu.PrefetchScalarGridSpec(
            num_scalar_prefetch=1, grid=(steps,),
            in_specs=[pl.BlockSpec((o.shape[0] // steps, o.shape[1]), lambda i, place: (i, 0)) for o in owns]
            + [pl.BlockSpec((7, o.shape[0] // steps, o.shape[1]), lambda i, place: (0, i, 0)) for o in owns]
            + [ANY] * len(dests) + small_specs,
            out_specs=[out_spec(items) for items in weights] + [piece(a) for a, _ in small]),
        out_shape=[jax.ShapeDtypeStruct((DEPTH, 2 * items[0][1].shape[0], items[0][1].shape[1]), F32)
                   for items in weights] + [jax.ShapeDtypeStruct(a.shape, F32) for a, _ in small],
        input_output_aliases={1 + 2 * n + k: k for k in range(len(dests))},
        compiler_params=_params(),
    )(place_arr, *owns, *[recv for _, _, recv in flat], *dests, *[a for pair in small for a in pair])


def _sum_small(name, partials, recvs, place_arr):
    n = len(partials)

    def body(place_ref, *refs):
        for o_ref, r_ref, out_ref in zip(refs[:n], refs[n:2 * n], refs[2 * n:]):
            total = o_ref[...]
            for m in range(7):
                total = total + r_ref[m]
            out_ref[...] = total

    piece = lambda a: pl.BlockSpec((a.shape[0] // 8, a.shape[1]), lambda i, place: (place[0], 0))
    return pl.pallas_call(
        body, name=name,
        grid_spec=pltpu.PrefetchScalarGridSpec(
            num_scalar_prefetch=1, grid=(1,),
            in_specs=[piece(a) for a in partials] + [pl.BlockSpec(r.shape, lambda i, place: (0, 0, 0)) for r in recvs],
            out_specs=[piece(a) for a in partials]),
        out_shape=[jax.ShapeDtypeStruct(a.shape, F32) for a in partials],
        compiler_params=_params(),
    )(place_arr, *partials, *recvs)


def _share(name, bufs, parts, gathered=(), collective_id=None):
    n, n_g = len(bufs), len(gathered)
    total = n + n_g

    def body(*refs):
        ins, outs = refs[:total], refs[total:2 * total]
        send_sems, recv_sems, send_g, recv_g = refs[2 * total:]
        x, y, c = _place()
        _handshake([_peer(x, y, c, m) for m in (SAME_CORE if gathered else ()) + (1,)])

        def half(ref, l, which):
            p = ref.shape[1] // 2
            return ref.at[l, pl.ds(pl.multiple_of(which * p, 8), p), :]

        def swap(k, which):
            a, l = parts[k]
            return pltpu.make_async_remote_copy(
                src_ref=half(ins[a], l, which), dst_ref=half(outs[a], l, which), send_sem=send_sems.at[k],
                recv_sem=recv_sems.at[k], device_id=(x, y, 1 - c), device_id_type=MESH)

        def spread(a, m, sender, held, to):
            k = 4 * sender[0] + 2 * sender[1] + sender[2]
            return pltpu.make_async_remote_copy(
                src_ref=_piece_rows(held[n + a], k), dst_ref=_piece_rows(outs[n + a], k),
                send_sem=send_g.at[7 * a + m - 1], recv_sem=recv_g.at[7 * a + m - 1], device_id=to, device_id_type=MESH)

        me, sibling = (x, y, c), (x, y, 1 - c)
        for k in range(len(parts)):
            swap(k, c).start()
        def own(a, m):
            return spread(a, m, me, ins, _peer(x, y, c, m))

        def handed_on(a, m):
            return spread(a, m + 1, _peer(x, y, c, m), outs, sibling)

        for a in range(n_g):
            for m in SAME_CORE + (1,):
                own(a, m).start()
        for a in range(n_g):
            for m in SAME_CORE:
                spread(a, m, _peer(x, y, c, m), ins, _peer(x, y, c, m)).wait_recv()
                handed_on(a, m).start()
        for k in range(len(parts)):
            swap(k, c).wait_send()
            swap(k, 1 - c).wait_recv()
        for a in range(n_g):
            for m in SAME_CORE + (1,):
                own(a, m).wait_send()
            for m in SAME_CORE:
                handed_on(a, m).wait_send()
                spread(a, m + 1, _peer(x, y, c, m + 1), ins, sibling).wait_recv()
            spread(a, 1, sibling, ins, sibling).wait_recv()

    arrays = list(bufs) + list(gathered)
    return pl.pallas_call(
        body, name=name, in_specs=[ANY] * total, out_specs=[ANY] * total,
        out_shape=[jax.ShapeDtypeStruct(b.shape, F32) for b in arrays],
        input_output_aliases={a: a for a in range(total)},
        scratch_shapes=[pltpu.SemaphoreType.DMA((max(len(parts), 1),))] * 2
        + [pltpu.SemaphoreType.DMA((max(7 * n_g, 1),))] * 2,
        compiler_params=pltpu.CompilerParams(collective_id=collective_id),
    )(*arrays)


def _adamw_math(w, g, m, v):
    nm = ADAM_B1 * m + (1.0 - ADAM_B1) * g
    nv = ADAM_B2 * v + (1.0 - ADAM_B2) * (g * g)
    m_hat = nm / (1.0 - ADAM_B1 ** ADAM_STEP)
    v_hat = nv / (1.0 - ADAM_B2 ** ADAM_STEP)
    return -ADAM_LR * (m_hat / (jnp.sqrt(v_hat) + ADAM_EPS) + ADAM_WD * w), nm, nv


def _adamw(name, w, g, m, v, rows_per_step, first=0, count=None, dests=None, deps=()):
    layers, rows, cols = w.shape
    count = layers if count is None else count

    def body(w_ref, g_ref, m_ref, v_ref, *rest):
        d_ref, nm_ref, nv_ref, g_out_ref = rest[-4:]
        d_ref[...], nm_ref[...], nv_ref[...] = _adamw_math(w_ref[...], g_ref[...], m_ref[...], v_ref[...])
        g_out_ref[...] = g_ref[...]

    spec = pl.BlockSpec((1, rows_per_step, cols), lambda l, i: (first + l, i, 0))
    shape = jax.ShapeDtypeStruct(w.shape, F32)
    dests = () if dests is None else tuple(dests)
    return pl.pallas_call(
        body, name=name, grid=(count, rows // rows_per_step),
        in_specs=[spec] * 4 + [ANY] * (len(dests) + len(deps)), out_specs=[spec] * 4, out_shape=[shape] * 4,
        input_output_aliases={4 + k: k for k in range(len(dests))},
        compiler_params=_params(("arbitrary", "arbitrary")),
    )(w, g, m, v, *dests, *deps)


def _pack_misc(pool_scale, sinks, norm_pre, norm_post):
    sink_rows = jnp.zeros((DEPTH, 8, 128), F32).at[:, 0, 0:N_HEADS].set(sinks).reshape(2 * 8, 128)
    return jnp.concatenate([pool_scale.reshape(8, 128), norm_pre.reshape(16, 128), norm_post.reshape(16, 128),
                            sink_rows, jnp.zeros((8, 128), F32)], axis=0)


def _adamw_small(w, g, m, v, pool):
    def body(w_ref, g_ref, m_ref, v_ref, pw_ref, pg_ref, pm_ref, pv_ref, *rest):
        outs, pool_outs, (d_ref, nm_ref, nv_ref) = rest[:17], rest[17:21], rest[21:]
        pool_outs[0][...] = pg_ref[...]
        pool_outs[1][...], pool_outs[2][...], pool_outs[3][...] = _adamw_math(
            pw_ref[...], pg_ref[...], pm_ref[...], pv_ref[...])
        d_ref[...], nm_ref[...], nv_ref[...] = _adamw_math(w_ref[...], g_ref[...], m_ref[...], v_ref[...])
        for k, src in enumerate([g_ref, d_ref, nm_ref, nv_ref]):
            scale, sinks, pre, post = outs[4 * k:4 * k + 4]
            for l in range(DEPTH):
                for j in range(4):
                    scale[l:l + 1, j * 128:(j + 1) * 128] = src[MISC_SCALE + 4 * l + j:MISC_SCALE + 4 * l + j + 1, :]
                for j in range(8):
                    pre[l:l + 1, j * 128:(j + 1) * 128] = src[MISC_PRE + 8 * l + j:MISC_PRE + 8 * l + j + 1, :]
                    post[l:l + 1, j * 128:(j + 1) * 128] = src[MISC_POST + 8 * l + j:MISC_POST + 8 * l + j + 1, :]
                sinks[l:l + 1, :] = src[MISC_SINKS + 8 * l:MISC_SINKS + 8 * l + 1, 0:N_HEADS]
        outs[16][...] = g_ref[MISC_LOSS:MISC_LOSS + 1, 0:1]

    vmem = pl.BlockSpec(memory_space=pltpu.VMEM)
    shapes = [(DEPTH, D_POOL), (DEPTH, N_HEADS), (DEPTH, D), (DEPTH, D)] * 4 + [(1, 1)]
    shapes += [pool[0].shape] * 4
    return pl.pallas_call(
        body, name="adamw_small", in_specs=[vmem] * 8, out_specs=[vmem] * 21,
        out_shape=[jax.ShapeDtypeStruct(s, F32) for s in shapes],
        scratch_shapes=[pltpu.VMEM((MISC_ROWS, 128), F32)] * 3,
    )(w, g, m, v, *pool)


def kernel(x, w_in, pool_w, pool_scale, attn_sinks, w_out, norm_pre, norm_post, loss_target, m_w_in, m_pool_w, m_pool_scale, m_attn_sinks, m_w_out, m_norm_pre, m_norm_post, v_w_in, v_pool_w, v_pool_scale, v_attn_sinks, v_w_out, v_norm_pre, v_norm_post):
    cx, cy, cc = _place()
    chip_arr = jnp.reshape(2 * cx + cy, (1,)).astype(jnp.int32)
    place_arr = jnp.stack([4 * cx + 2 * cy + cc, cc]).astype(jnp.int32)
    t = lambda a: jnp.transpose(a, (0, 2, 1))
    w_in_t = t(w_in)
    xs, target = x[0], loss_target[0]
    pool_w_b = pool_w.astype(BF16)
    tables = _attention_tables()
    scale3 = pool_scale.reshape(DEPTH, 1, D_POOL)
    pre3 = norm_pre.reshape(DEPTH, 1, D)
    post3 = norm_post.reshape(DEPTH, 1, D)

    first = _gather_start_cast("gather_start_first", w_in_t, 0, ID_GATHER_FIRST)
    wi0 = _place_other_half("place_w_in0_rest", w_in_t, place_arr, 0, first[2][0])
    (wi1,) = _place_cast("place_w_in1", w_in_t, chip_arr, 288, [1], deps=(first[3],))
    wo = _place_cast("place_w_out", w_out, chip_arr, 256, [0, 1], deps=(first[3],))
    rest = _gather_start("gather_start_rest", [wi1, wo[0], wo[1]], halved=(0, 1), collective_id=ID_GATHER_REST)
    send, recv, bufs = [first[k] + rest[k] for k in range(3)]
    bufs = [wi0, *bufs[1:]]
    order = {(0, "in"): 0, (1, "in"): 1, (0, "out"): 2, (1, "out"): 3}

    saved = []
    packed = [_pack_misc(pool_scale, attn_sinks, norm_pre, norm_post),
              _pack_misc(m_pool_scale, m_attn_sinks, m_norm_pre, m_norm_post),
              _pack_misc(v_pool_scale, v_attn_sinks, v_norm_pre, v_norm_post)]
    after = (first[3], rest[3], pool_w_b, *tables, scale3, pre3, post3, *packed)
    below = None
    for l in range(DEPTH):
        k = order[l, "in"]
        halves = [_gather_wait(f"gather_wait_in{l}", bufs[k], send[k], recv[k], after, halved=True)]
        if below is not None:
            halves.append(below[1])
        w_in_l, *w_out_below = _forward_halves(f"forward_w{l}", halves, collective_id=ID_FORWARD[l])
        if below is None:
            pu, pg, q, kv, ag = _fwd_in(l, xs, pre3, w_in_l)
        else:
            saved[l - 1][9] = w_out_below[0]
            y, xs, pu, pg, q, kv, ag = _fwd_in(l, xs, pre3, w_in_l, (below[0], w_out_below[0], below[2]))
            saved[l - 1][7] = y
        cat = _fwd_mix(l, pu, pg, q, kv, ag, pool_w_b, scale3, attn_sinks, tables)
        k = order[l, "out"]
        w_out_l = _gather_wait(f"gather_wait_out{l}", bufs[k], send[k], recv[k], (cat,), halved=l + 1 < DEPTH)
        saved.append([xs, pu, pg, q, kv, ag, cat, None, w_in_l, w_out_l])
        below, after = (cat, w_out_l, post3), (w_out_l,)

    x_in, pu, pg, q, kv, ag, cat, y, w_in_l, w_out_l = saved[1]
    dcat, dw_out1, dw_out1_b, dg_post1, loss, xs = _bwd_out(1, cat, w_out_l, post3, place_arr, x=x_in, target=target)
    ex1_out = _exchange_start("exchange_start_out1", [dw_out1_b], ID_OUT1)
    dproj, dpw, dsc1, dsink1 = _bwd_mix(1, pu, pg, q, kv, ag, dcat, pool_w_b, scale3, attn_sinks, tables,
                                        deps=(ex1_out[4],))
    dx, dg_pre1, dw_in1, dw_in1_b = _bwd_in_dx(1, dproj, w_in_l, x_in, pre3, xs, dw_place=place_arr)
    ex1_in = _exchange_start("exchange_start_in1", [dw_in1_b], ID_IN1)

    x_in, pu, pg, q, kv, ag, cat, y, w_in_l, w_out_l = saved[0]
    dcat, dw_out0, dw_out0_b, dg_post0 = _bwd_out(0, cat, w_out_l, post3, place_arr, dxn=dx, y=y, deps=(ex1_in[4],))
    ex0_out = _exchange_start("exchange_start_out0", [dw_out0_b], ID_OUT0)
    dproj, dpw, dsc0, dsink0 = _bwd_mix(0, pu, pg, q, kv, ag, dcat, pool_w_b, scale3, attn_sinks, tables,
                                        deps=(ex0_out[4],), dpw_dest=dpw)
    dw_in0, dw_in0_b = _bwd_in_dw(0, dproj, x_in, pre3, place_arr)
    flat = lambda a: a.reshape(DEPTH * 4 * 128, 128)
    ex0_in = _exchange_start("exchange_start_in0", [flat(dpw), dw_in0_b], ID_IN0)

    grad_x, dg_pre0 = _bwd_in_dx(0, dproj, w_in_l, x_in, pre3, dx, deps=(ex0_in[4],))
    small = [jnp.concatenate([dsc0, dsc1, dg_pre0, dg_pre1, dg_post0, dg_post1, dsink0, dsink1, loss], axis=0)]
    ex_small = _exchange_start("exchange_start_small", small, ID_SMALL)
    (recv_out1,) = _exchange_wait("exchange_wait_out1", ex1_out, ex_small[4])
    (recv_in1,) = _exchange_wait("exchange_wait_in1", ex1_in, recv_out1)
    g_in, g_out = _sum_pieces("sum_pieces_1", [[(1, dw_in1, recv_in1)], [(1, dw_out1, recv_out1)]], place_arr)
    (recv_out0,) = _exchange_wait("exchange_wait_out0", ex0_out, g_out)
    (g_out,) = _sum_pieces("sum_pieces_out0", [[(0, dw_out0, recv_out0)]], place_arr, dests=[g_out])
    g_in, g_out = _share("share_a", [g_in, g_out], [(0, 1), (1, 0), (1, 1)], collective_id=ID_SHARE_A)
    m_in_t, v_in_t = t(m_w_in), t(v_w_in)
    d_out, nm_out, nv_out, grad_w_out = _adamw("adamw_w_out", w_out, g_out, m_w_out, v_w_out, 256)
    upd_in = _adamw("adamw_w_in1", w_in_t, g_in, m_in_t, v_in_t, 288, first=1, count=1, deps=(d_out,))
    dpw_own, recv_pw = _exchange_wait("exchange_wait_pool", ex0_in, upd_in[0], which=[0], with_sent=True)
    (g_pw,) = _sum_small("sum_pool", [dpw_own], [recv_pw], place_arr)

    (recv_in0,) = _exchange_wait("exchange_wait_in0", ex0_in, g_pw, which=[1])
    (recv_misc,) = _exchange_wait("exchange_wait_small", ex_small, recv_in0)
    g_in, g_misc = _sum_pieces("sum_pieces_in0", [[(0, dw_in0, recv_in0)]], place_arr, dests=[g_in],
                               small=[(small[0], recv_misc)])
    g_in, g_pw, g_misc = _share("share_b", [g_in], [(0, 0)], [g_pw, g_misc], collective_id=ID_SHARE_B)
    d_in, nm_in, nv_in, grad_w_in_t = _adamw("adamw_w_in0", w_in_t, g_in, m_in_t, v_in_t, 288, first=0, count=1,
                                             dests=upd_in)
    small_out = _adamw_small(packed[0], g_misc, packed[1], packed[2],
                             (flat(pool_w), g_pw, flat(m_pool_w), flat(v_pool_w)))
    (g_sc, g_sk, g_pre, g_post, d_sc, d_sk, d_pre, d_post,
     m_sc, m_sk, m_pre, m_post, v_sc, v_sk, v_pre, v_post, loss_sum) = small_out[:17]
    g_pw, d_pw, m_pw, v_pw = [a.reshape(pool_w.shape) for a in small_out[17:]]
    return (loss_sum[0, 0], grad_x[None], t(grad_w_in_t), g_pw, g_sc, g_sk, grad_w_out, g_pre, g_post,
            t(d_in), d_pw, d_sc, d_sk, d_out, d_pre, d_post,
            t(nm_in), m_pw, m_sc, m_sk, nm_out, m_pre, m_post,
            t(nv_in), v_pw, v_sc, v_sk, nv_out, v_pre, v_post)
```

```python
import jax
import jax.numpy as jnp
from jax import lax
from jax.experimental import pallas as pl
from jax.experimental.pallas import tpu as pltpu

F32 = jnp.float32
BF16 = jnp.bfloat16

S = 2048
D = 1024
DEPTH = 2
D_POOL = 512
POOL_WINDOWS = (2, 4, 8, 16)
N_HEADS = 8
D_IN = 2304
N_SHARDS = 4
W_IN_SHARD = D_IN // N_SHARDS
W_OUT_SHARD = D // N_SHARDS
BLK = 128
NB = S // BLK
HALO = 16
PAD = 8
EPS = 1e-6
NEG_INF = -1e30
C_PU, C_PG, C_Q, C_K, C_V, C_AG = 0, 512, 1024, 1536, 1664, 1792

ADAM_LR = 0.001
ADAM_B1 = 0.9
ADAM_B2 = 0.999
ADAM_EPS = 1e-08
ADAM_WD = 0.01
ADAM_STEP = 10

TM = 512
VMEM_LIMIT = 56 * 1024 * 1024

NT = (((1,), (1,)), ((), ()))
TN = (((0,), (0,)), ((), ()))

MESH = pl.DeviceIdType.MESH
ANY = pl.BlockSpec(memory_space=pl.ANY)

ID_FORWARD = (0, 1)
(ID_SHARE_A, ID_SHARE_B, ID_GATHER_FIRST, ID_GATHER_REST, ID_OUT1, ID_IN1, ID_OUT0, ID_IN0, ID_SMALL) = range(2, 11)

MISC_SCALE, MISC_PRE, MISC_POST, MISC_SINKS, MISC_LOSS = 0, 8, 24, 40, 56
MISC_ROWS = 64


def _params(sem=("arbitrary",)):
    return pltpu.CompilerParams(dimension_semantics=sem, vmem_limit_bytes=VMEM_LIMIT)


def _sigmoid(v):
    return 1.0 / (1.0 + jnp.exp(-v))


def _rows8(v):
    r, c = v.shape
    return v.reshape(r // 8, 8, c).sum(axis=0)


def _layer(l, *shape):
    zeros = (0,) * len(shape)
    return pl.BlockSpec((None,) + shape, lambda i: (l,) + zeros)


def _whole(shape):
    zeros = (0,) * len(shape)
    return pl.BlockSpec(shape, lambda i: zeros, pipeline_mode=pl.Buffered(1))


def _fwd_in(l, x, g_pre, w_in_t, below=None):
    fused = below is not None

    def body(x_ref, g_ref, w_ref, *rest):
        if fused:
            cat_ref, wo_ref, gp_ref, y_ref, xn_ref = rest[:5]
            y = jnp.dot(cat_ref[...], wo_ref[...], preferred_element_type=F32)
            y_ref[...] = y
            xt = x_ref[...] + y * lax.rsqrt(jnp.mean(y * y, axis=-1, keepdims=True) + EPS) * gp_ref[...]
            xn_ref[...] = xt
        else:
            xt = x_ref[...]
        pu_ref, pg_ref, q_ref, kv_ref, ag_ref = rest[-5:]
        r = lax.rsqrt(jnp.mean(xt * xt, axis=-1, keepdims=True) + EPS)
        h = (xt * r * g_ref[...]).astype(BF16)

        def proj(lo, hi):
            return lax.dot_general(h, w_ref[lo:hi, :], NT, preferred_element_type=F32)

        pu_ref[...] = proj(C_PU, C_PG)
        pg_ref[...] = proj(C_PG, C_Q)
        q_ref[...] = proj(C_Q, C_K).astype(BF16)
        kv_ref[...] = proj(C_K, C_AG).astype(BF16)
        ag_ref[...] = proj(C_AG, D_IN)

    row = lambda w: pl.BlockSpec((TM, w), lambda i: (i, 0))
    act = jax.ShapeDtypeStruct((S, D), F32)
    return pl.pallas_call(
        body, name="fwd_out_in" if fused else "fwd_in", grid=(S // TM,),
        in_specs=[row(D), _layer(l, 1, D), _whole((D_IN, D))]
        + ([row(D), _whole((D, D)), _layer(l - 1, 1, D)] if fused else []),
        out_specs=[row(D)] * (2 * fused) + [row(512), row(512), row(512), row(256), row(512)],
        out_shape=[act] * (2 * fused)
        + [jax.ShapeDtypeStruct((S, 512), F32), jax.ShapeDtypeStruct((S, 512), F32),
           jax.ShapeDtypeStruct((S, 512), BF16), jax.ShapeDtypeStruct((S, 256), BF16),
           jax.ShapeDtypeStruct((S, 512), F32)],
        compiler_params=_params(),
    )(x, g_pre, w_in_t, *(below if fused else ()))


LOG2E = 1.4426950408889634
SCORE_SCALE = 0.125 * LOG2E


def _attention_tables():
    qi = jnp.arange(BLK)[:, None]
    kj = jnp.arange(BLK)[None, :]
    dist = ((qi - kj) % BLK).astype(F32)
    slopes = jnp.exp2(-jnp.arange(1, N_HEADS + 1, dtype=F32))
    bias = -(slopes * LOG2E)[:, None, None] * dist[None]
    first = jnp.where(kj > qi, NEG_INF, bias)
    return jnp.stack([first, bias]), (kj <= qi).astype(BF16)


def _own_block_mask():
    return lax.broadcasted_iota(jnp.int32, (BLK, BLK), 1) <= lax.broadcasted_iota(jnp.int32, (BLK, BLK), 0)


def _merge(full, own):
    return jnp.where(own, full[:, BLK:], full[:, :BLK])


def _spread(v, tri):
    own = v * tri
    return jnp.concatenate([v - own, own], axis=1)


def _head_variants(cur, prev):
    both = jnp.concatenate([prev, cur], axis=0).astype(F32)
    swapped = pltpu.roll(both, 64, axis=1)
    low = lax.broadcasted_iota(jnp.int32, both.shape, 1) < 64
    zero = jnp.zeros_like(both)
    return ((jnp.where(low, both, zero).astype(BF16), jnp.where(low, zero, swapped).astype(BF16)),
            (jnp.where(low, swapped, zero).astype(BF16), jnp.where(low, zero, both).astype(BF16)))


def _head_of(hkv, t, half):
    return hkv * 4 + 2 * t + half


def _rows(v, t):
    return v[t * BLK:(t + 1) * BLK]


def _stack_tiles(ref, hkv, offset=0):
    lo = offset + 2 * hkv * 128
    return jnp.concatenate([ref[:, lo:lo + 128], ref[:, lo + 128:lo + 256]], axis=0)


def _scores(q2, k_var, own):
    s = {}
    for hkv in range(2):
        for half in range(2):
            full = lax.dot_general(q2[hkv], k_var[hkv][half], NT, preferred_element_type=F32)
            for t in range(2):
                s[hkv, t, half] = _merge(_rows(full, t), own)
    return s


def _softmax(s, bias, sink):
    s = s * SCORE_SCALE + bias
    sink2 = sink * LOG2E
    m = jnp.maximum(jnp.max(s, axis=-1, keepdims=True), sink2)
    p = jnp.exp2(s - m)
    e_sink = jnp.exp2(sink2 - m)
    inv = 1.0 / (jnp.sum(p, axis=-1, keepdims=True) + e_sink)
    return p * inv, e_sink * inv


def _spread_pair(v, hkv, half, tri):
    return jnp.concatenate([_spread(v[hkv, t, half].astype(BF16), tri) for t in range(2)], axis=0)


POOL_ROWS = PAD + HALO + BLK


def _window_sums(src_ref, tmp_refs, trailing):
    lo, hi = (PAD, POOL_ROWS) if trailing else (0, HALO + BLK)
    cur = src_ref
    for level in range(len(POOL_WINDOWS)):
        lanes = slice(level * 128, 512)
        shift = -(1 << level) if trailing else (1 << level)
        dst = tmp_refs[level % 2]
        dst[lo:hi, lanes] = cur[lo:hi, lanes] + cur[lo + shift:hi + shift, lanes]
        cur = dst


def _pool_block(ext_ref, tmp_refs, i, g, w):
    lanes = slice(g * 128, (g + 1) * 128)
    rows = slice(PAD + HALO, POOL_ROWS)
    t = (i * BLK + lax.broadcasted_iota(jnp.int32, (BLK, 1), 0)).astype(F32)
    inv = 1.0 / jnp.minimum(t + 1.0, float(w))
    return tmp_refs[g % 2][rows, lanes] * inv - ext_ref[rows, lanes], inv


def _fwd_mix(l, pu, pg, q, kv, ag, pool_w, pool_scale, sinks, tables):
    bias, tri = tables

    def body(pu_ref, pup_ref, pg_ref, q_ref, kv_ref, kvp_ref, ag_ref, pw_ref, sc_ref, sink_ref, bias_ref, tri_ref,
             cat_ref, ext_ref, *tmp_refs):
        i = pl.program_id(0)

        @pl.when(i == 0)
        def _():
            for ref in (ext_ref, *tmp_refs):
                ref[0:PAD, :] = jnp.zeros((PAD, 512), F32)

        ext_ref[PAD:PAD + HALO, :] = jnp.where(i > 0, pup_ref[...], 0.0)
        ext_ref[PAD + HALO:POOL_ROWS, :] = pu_ref[...]
        _window_sums(ext_ref, tmp_refs, True)
        for g, w in enumerate(POOL_WINDOWS):
            lanes = slice(g * 128, (g + 1) * 128)
            pooled, _ = _pool_block(ext_ref, tmp_refs, i, g, w)
            mixed = jnp.dot(pooled.astype(BF16), pw_ref[g], preferred_element_type=F32)
            gate = pg_ref[:, lanes]
            cat_ref[:, lanes] = (mixed * sc_ref[:, lanes] * (gate * _sigmoid(gate))).astype(BF16)

        own = _own_block_mask()
        tri = tri_ref[...]
        k_var = _head_variants(kv_ref[:, 0:128], kvp_ref[:, 0:128])
        v_var = _head_variants(kv_ref[:, 128:256], kvp_ref[:, 128:256])
        s = _scores([_stack_tiles(q_ref, hkv) for hkv in range(2)], k_var, own)
        p = {}
        for (hkv, t, half), s_head in s.items():
            head = _head_of(hkv, t, half)
            p[hkv, t, half], _ = _softmax(s_head, bias_ref[head], sink_ref[l, head])
        for hkv in range(2):
            o2 = jnp.zeros((2 * BLK, 128), F32)
            for half in range(2):
                o2 = o2 + jnp.dot(_spread_pair(p, hkv, half, tri), v_var[hkv][half], preferred_element_type=F32)
            for t in range(2):
                lo = (2 * hkv + t) * 128
                gate = ag_ref[:, lo:lo + 128]
                cat_ref[:, D_POOL + lo:D_POOL + lo + 128] = (_rows(o2, t) * (gate * _sigmoid(gate))).astype(BF16)

    blk = lambda w: pl.BlockSpec((BLK, w), lambda i: (i, 0))
    prev = lambda w: pl.BlockSpec((BLK, w), lambda i: (jnp.maximum(i - 1, 0), 0))
    halo = pl.BlockSpec((HALO, 512), lambda i: (jnp.maximum(i * (BLK // HALO) - 1, 0), 0))
    return pl.pallas_call(
        body, name="fwd_mix", grid=(NB,),
        in_specs=[blk(512), halo, blk(512), blk(512), blk(256), prev(256), blk(512),
                  _layer(l, 4, 128, 128), _layer(l, 1, 512), pl.BlockSpec(memory_space=pltpu.SMEM),
                  pl.BlockSpec((None, N_HEADS, BLK, BLK), lambda i: (jnp.minimum(i, 1), 0, 0, 0)), _whole((BLK, BLK))],
        out_specs=blk(D),
        out_shape=jax.ShapeDtypeStruct((S, D), BF16),
        scratch_shapes=[pltpu.VMEM((POOL_ROWS, 512), F32)] * 3,
        compiler_params=_params(),
    )(pu, pu, pg, q, kv, kv, ag, pool_w, pool_scale, sinks, bias, tri)


def _store_lane_rows(ref, acc):
    total = jnp.sum(acc, axis=0, keepdims=True)
    for k in range(ref.shape[0]):
        ref[k:k + 1, :] = total[:, k * 128:(k + 1) * 128]


def _own_piece(dw_ref, place_ref):
    p = dw_ref.shape[0] // 8
    return dw_ref[pl.ds(pl.multiple_of(place_ref[0] * p, 8), p), :]


def _bwd_out(l, cat, w_out, g_post, place_arr, dxn=None, y=None, x=None, target=None, deps=()):
    last = target is not None
    n_steps = S // TM

    def body(a_ref, b_ref, g_ref, cat_ref, w_ref, place_ref, *rest):
        dcat_ref, own_ref, dwb_ref, dg_ref = rest[len(deps):len(deps) + 4]
        rest = rest[len(deps) + 4:]
        acc_ref, dw_ref = rest[-2:]
        step = pl.program_id(0)

        @pl.when(step == 0)
        def _():
            dw_ref[...] = jnp.zeros_like(dw_ref)
            acc_ref[...] = jnp.zeros_like(acc_ref)

        cat = cat_ref[...]
        g = g_ref[...]
        y = jnp.dot(cat, w_ref[...], preferred_element_type=F32) if last else b_ref[...]
        r = lax.rsqrt(jnp.mean(y * y, axis=-1, keepdims=True) + EPS)
        if last:
            loss_ref, dx_ref, loss_acc_ref = rest[:3]
            err = a_ref[...] + y * r * g - b_ref[...]

            @pl.when(step == 0)
            def _():
                loss_acc_ref[...] = jnp.zeros_like(loss_acc_ref)

            loss_acc_ref[...] += _rows8(err * err)
            dz = err * (1.0 / D)
            dx_ref[...] = dz
        else:
            dz = a_ref[...]
        a = dz * g
        dy = r * a - y * (r * r * r) * jnp.mean(a * y, axis=-1, keepdims=True)
        acc_ref[...] += _rows8(dz * (y * r))
        dyb = dy.astype(BF16)
        dcat_ref[...] = lax.dot_general(dyb, w_ref[...], NT, preferred_element_type=F32)
        dw_ref[...] += lax.dot_general(cat, dyb, TN, preferred_element_type=F32)

        @pl.when(step == n_steps - 1)
        def _():
            _store_lane_rows(dg_ref, acc_ref[...])
            dwb_ref[...] = dw_ref[...].astype(BF16)
            own_ref[...] = _own_piece(dw_ref, place_ref)
            if last:
                loss_ref[...] = jnp.full((8, 128), (0.5 / D) * jnp.sum(loss_acc_ref[...]), F32)

    row = lambda: pl.BlockSpec((TM, D), lambda i: (i, 0))
    full = _whole
    return pl.pallas_call(
        body, name="out_loss_bwd" if last else "bwd_out", grid=(n_steps,),
        in_specs=[row(), row(), _layer(l, 1, D), row(), full((D, D)), pl.BlockSpec(memory_space=pltpu.SMEM)]
        + [ANY] * len(deps),
        out_specs=[row(), full((D // 8, D)), full((D, D)), full((8, 128))] + ([full((8, 128)), row()] if last else []),
        out_shape=[jax.ShapeDtypeStruct((S, D), F32), jax.ShapeDtypeStruct((D // 8, D), F32),
                   jax.ShapeDtypeStruct((D, D), BF16), jax.ShapeDtypeStruct((8, 128), F32)]
        + ([jax.ShapeDtypeStruct((8, 128), F32), jax.ShapeDtypeStruct((S, D), F32)] if last else []),
        scratch_shapes=([pltpu.VMEM((8, D), F32)] if last else []) + [pltpu.VMEM((8, D), F32), pltpu.VMEM((D, D), F32)],
        compiler_params=_params(),
    )(*((x, target) if last else (dxn, y)), g_post, cat, w_out, place_arr, *deps)


def _bwd_mix(l, pu, pg, q, kv, ag, dcat, pool_w, pool_scale, sinks, tables, deps=(), dpw_dest=None):
    bias, tri = tables
    deps = tuple(deps) + (() if dpw_dest is None else (dpw_dest,))

    def body(pu_ref, pup_ref, pg_ref, q_ref, kv_ref, kvp_ref, ag_ref, dcat_ref, pw_ref, sc_ref, sink_ref, bias_ref,
             tri_ref, *rest):
        dproj_ref, dpw_ref, dsc_ref, dsink_ref, ext_ref, dext_ref, tmp_a, tmp_b, dkv_ref = rest[len(deps):]
        tmp_refs = (tmp_a, tmp_b)
        step = pl.program_id(0)
        i = NB - 1 - step

        @pl.when(step == 0)
        def _():
            dpw_ref[...] = jnp.zeros_like(dpw_ref)
            dsc_ref[...] = jnp.zeros_like(dsc_ref)
            dsink_ref[...] = jnp.zeros_like(dsink_ref)
            for ref in (ext_ref, tmp_a, tmp_b):
                ref[0:PAD, :] = jnp.zeros((PAD, 512), F32)
            dext_ref[BLK:POOL_ROWS, :] = jnp.zeros((HALO + PAD, 512), F32)
            dkv_ref[...] = jnp.zeros_like(dkv_ref)

        ext_ref[PAD:PAD + HALO, :] = jnp.where(i > 0, pup_ref[...], 0.0)
        ext_ref[PAD + HALO:POOL_ROWS, :] = pu_ref[...]
        _window_sums(ext_ref, tmp_refs, True)
        dpooled = []
        for g, w in enumerate(POOL_WINDOWS):
            lanes = slice(g * 128, (g + 1) * 128)
            pooled, inv = _pool_block(ext_ref, tmp_refs, i, g, w)
            pooled_b = pooled.astype(BF16)
            mixed = jnp.dot(pooled_b, pw_ref[g], preferred_element_type=F32)
            scale = sc_ref[:, lanes]
            gate = pg_ref[:, lanes]
            sg = _sigmoid(gate)
            dpo = dcat_ref[:, lanes]
            dproj_ref[:, C_PG + g * 128:C_PG + (g + 1) * 128] = (
                dpo * (mixed * scale) * (sg * (1.0 + gate * (1.0 - sg)))).astype(BF16)
            dms = dpo * (gate * sg)
            dsc_ref[g:g + 1, :] += jnp.sum(dms * mixed, axis=0, keepdims=True)
            dmixed = (dms * scale).astype(BF16)
            dpw_ref[g] += lax.dot_general(pooled_b, dmixed, TN, preferred_element_type=F32)
            dpooled.append(lax.dot_general(dmixed, pw_ref[g], NT, preferred_element_type=F32))
            dext_ref[0:BLK, lanes] = dpooled[g] * inv
        _window_sums(dext_ref, tmp_refs, False)
        for g in range(len(POOL_WINDOWS)):
            lanes = slice(g * 128, (g + 1) * 128)
            dproj_ref[:, C_PU + g * 128:C_PU + (g + 1) * 128] = (tmp_refs[g % 2][0:BLK, lanes] - dpooled[g]).astype(BF16)
        dext_ref[BLK:BLK + HALO, :] = dext_ref[0:HALO, :]

        own = _own_block_mask()
        tri = tri_ref[...]
        k_var = _head_variants(kv_ref[:, 0:128], kvp_ref[:, 0:128])
        v_var = _head_variants(kv_ref[:, 128:256], kvp_ref[:, 128:256])
        q2 = [_stack_tiles(q_ref, hkv) for hkv in range(2)]
        s = _scores(q2, k_var, own)
        p, p_sink = {}, {}
        for key, s_head in s.items():
            head = _head_of(*key)
            p[key], p_sink[key] = _softmax(s_head, bias_ref[head], sink_ref[l, head])

        do2, p_b, dp = [], {}, {}
        for hkv in range(2):
            gate = _stack_tiles(ag_ref, hkv)
            sg = _sigmoid(gate)
            dca = _stack_tiles(dcat_ref, hkv, D_POOL)
            do2.append((dca * (gate * sg)).astype(BF16))
            o2 = jnp.zeros((2 * BLK, 128), F32)
            for half in range(2):
                p_b[hkv, half] = _spread_pair(p, hkv, half, tri)
                o2 = o2 + jnp.dot(p_b[hkv, half], v_var[hkv][half], preferred_element_type=F32)
                full = lax.dot_general(do2[hkv], v_var[hkv][half], NT, preferred_element_type=F32)
                for t in range(2):
                    dp[hkv, t, half] = _merge(_rows(full, t), own)
            dag = dca * o2 * (sg * (1.0 + gate * (1.0 - sg)))
            for t in range(2):
                lo = C_AG + (2 * hkv + t) * 128
                dproj_ref[:, lo:lo + 128] = _rows(dag, t).astype(BF16)

        ds = {}
        for key in p:
            delta = jnp.sum(p[key] * dp[key], axis=-1, keepdims=True)
            ds[key] = p[key] * (dp[key] - delta)
            head = _head_of(*key)
            dsink_ref[0:1, :] += jnp.where(lax.broadcasted_iota(jnp.int32, (1, 128), 1) == head,
                                           -jnp.sum(p_sink[key] * delta, axis=0, keepdims=True), 0.0)

        dk_acc = [[None, None], [None, None]]
        dv_acc = [[None, None], [None, None]]
        for hkv in range(2):
            dq2 = jnp.zeros((2 * BLK, 128), F32)
            for half in range(2):
                ds_b = _spread_pair(ds, hkv, half, tri)
                dq2 = dq2 + jnp.dot(ds_b, k_var[hkv][half], preferred_element_type=F32)
                dk_acc[hkv][half] = lax.dot_general(ds_b, q2[hkv], TN, preferred_element_type=F32)
                dv_acc[hkv][half] = lax.dot_general(p_b[hkv, half], do2[hkv], TN, preferred_element_type=F32)
            for t in range(2):
                lo = C_Q + (2 * hkv + t) * 128
                dproj_ref[:, lo:lo + 128] = (_rows(dq2, t) * 0.125).astype(BF16)

        low = lax.broadcasted_iota(jnp.int32, (2 * BLK, 128), 1) < 64

        def gather_heads(acc):
            return jnp.where(low, acc[0][0] + pltpu.roll(acc[0][1], 64, axis=1),
                             pltpu.roll(acc[1][0], 64, axis=1) + acc[1][1])

        dk = gather_heads(dk_acc) * 0.125
        dv = gather_heads(dv_acc)
        dproj_ref[:, C_K:C_V] = (dk[BLK:, :] + dkv_ref[:, 0:128]).astype(BF16)
        dproj_ref[:, C_V:C_AG] = (dv[BLK:, :] + dkv_ref[:, 128:256]).astype(BF16)
        dkv_ref[:, 0:128] = dk[:BLK, :]
        dkv_ref[:, 128:256] = dv[:BLK, :]

    rev = lambda w: pl.BlockSpec((BLK, w), lambda s: (NB - 1 - s, 0))
    prev = lambda w: pl.BlockSpec((BLK, w), lambda s: (jnp.maximum(NB - 2 - s, 0), 0))
    halo = pl.BlockSpec((HALO, 512), lambda s: (jnp.maximum((NB - 1 - s) * (BLK // HALO) - 1, 0), 0))
    return pl.pallas_call(
        body, name="bwd_mix", grid=(NB,),
        in_specs=[rev(512), halo, rev(512), rev(512), rev(256), prev(256), rev(512), rev(D),
                  _layer(l, 4, 128, 128), _layer(l, 1, 512), pl.BlockSpec(memory_space=pltpu.SMEM),
                  pl.BlockSpec((None, N_HEADS, BLK, BLK), lambda s: (jnp.minimum(NB - 1 - s, 1), 0, 0, 0)),
                  _whole((BLK, BLK))] + [ANY] * len(deps),
        out_specs=[rev(D_IN), _layer(l, 4, 128, 128),
                   pl.BlockSpec((4, 128), lambda s: (0, 0)), pl.BlockSpec((8, 128), lambda s: (0, 0))],
        out_shape=[jax.ShapeDtypeStruct((S, D_IN), BF16), jax.ShapeDtypeStruct((DEPTH, 4, 128, 128), F32),
                   jax.ShapeDtypeStruct((4, 128), F32), jax.ShapeDtypeStruct((8, 128), F32)],
        input_output_aliases={} if dpw_dest is None else {12 + len(deps): 1},
        scratch_shapes=[pltpu.VMEM((POOL_ROWS, 512), F32)] * 4 + [pltpu.VMEM((BLK, 256), F32)],
        compiler_params=_params(),
    )(pu, pu, pg, q, kv, kv, ag, dcat, pool_w, pool_scale, sinks, bias, tri, *deps)


def _bwd_in_dw(l, dproj, x, g_pre, place_arr, deps=()):
    n_steps = S // TM

    def body(dp_ref, x_ref, g_ref, place_ref, *rest):
        own_ref, dwb_ref, dw_ref = rest[len(deps):]
        step = pl.program_id(0)

        @pl.when(step == 0)
        def _():
            dw_ref[...] = jnp.zeros_like(dw_ref)

        xt = x_ref[...]
        r = lax.rsqrt(jnp.mean(xt * xt, axis=-1, keepdims=True) + EPS)
        h = (xt * r * g_ref[...]).astype(BF16)
        dw_ref[...] += lax.dot_general(dp_ref[...], h, TN, preferred_element_type=F32)

        @pl.when(step == n_steps - 1)
        def _():
            dwb_ref[...] = dw_ref[...].astype(BF16)
            own_ref[...] = _own_piece(dw_ref, place_ref)

    row = lambda w: pl.BlockSpec((TM, w), lambda i: (i, 0))
    full = _whole
    return pl.pallas_call(
        body, name="bwd_in_dw", grid=(n_steps,),
        in_specs=[row(D_IN), row(D), _layer(l, 1, D), pl.BlockSpec(memory_space=pltpu.SMEM)] + [ANY] * len(deps),
        out_specs=[full((D_IN // 8, D)), full((D_IN, D))],
        out_shape=[jax.ShapeDtypeStruct((D_IN // 8, D), F32), jax.ShapeDtypeStruct((D_IN, D), BF16)],
        scratch_shapes=[pltpu.VMEM((D_IN, D), F32)],
        compiler_params=_params(),
    )(dproj, x, g_pre, place_arr, *deps)


def _bwd_in_dx(l, dproj, w_in_t, x, g_pre, dres, deps=(), dw_place=None):
    n_steps = S // TM
    with_dw = dw_place is not None

    def body(dp_ref, w_ref, x_ref, g_ref, dres_ref, *rest):
        place_ref = rest[0] if with_dw else None
        rest = rest[with_dw + len(deps):]
        if with_dw:
            dx_ref, dg_ref, own_ref, dwb_ref, acc_ref, dw_ref = rest
        else:
            dx_ref, dg_ref, acc_ref = rest
        step = pl.program_id(0)

        @pl.when(step == 0)
        def _():
            acc_ref[...] = jnp.zeros_like(acc_ref)
            if with_dw:
                dw_ref[...] = jnp.zeros_like(dw_ref)

        g = g_ref[...]
        halves = [slice(k * (TM // 2), (k + 1) * (TM // 2)) for k in range(2)]
        dh = [jnp.dot(dp_ref[rows, :], w_ref[...], preferred_element_type=F32) for rows in halves]
        h = []
        for rows, dh_k in zip(halves, dh):
            xt = x_ref[rows, :]
            r = lax.rsqrt(jnp.mean(xt * xt, axis=-1, keepdims=True) + EPS)
            xn = xt * r
            acc_ref[...] += _rows8(dh_k * xn)
            a = dh_k * g
            dx_ref[rows, :] = dres_ref[rows, :] + (
                r * a - xt * (r * r * r) * jnp.mean(a * xt, axis=-1, keepdims=True))
            h.append((xn * g).astype(BF16))
        if with_dw:
            dw_ref[...] += lax.dot_general(dp_ref[...], jnp.concatenate(h, axis=0), TN, preferred_element_type=F32)

        @pl.when(step == n_steps - 1)
        def _():
            _store_lane_rows(dg_ref, acc_ref[...])
            if with_dw:
                dwb_ref[...] = dw_ref[...].astype(BF16)
                own_ref[...] = _own_piece(dw_ref, place_ref)

    row = lambda w: pl.BlockSpec((TM, w), lambda i: (i, 0))
    full = _whole
    dw_specs = [full((D_IN // 8, D)), full((D_IN, D))] if with_dw else []
    dw_shapes = [jax.ShapeDtypeStruct((D_IN // 8, D), F32), jax.ShapeDtypeStruct((D_IN, D), BF16)] if with_dw else []
    return pl.pallas_call(
        body, name="bwd_in" if with_dw else "bwd_in_dx", grid=(n_steps,),
        in_specs=[row(D_IN), full((D_IN, D)), row(D), _layer(l, 1, D), row(D)]
        + [pl.BlockSpec(memory_space=pltpu.SMEM)] * with_dw + [ANY] * len(deps),
        out_specs=[row(D), full((8, 128))] + dw_specs,
        out_shape=[jax.ShapeDtypeStruct((S, D), F32), jax.ShapeDtypeStruct((8, 128), F32)] + dw_shapes,
        scratch_shapes=[pltpu.VMEM((8, D), F32)] + [pltpu.VMEM((D_IN, D), F32)] * with_dw,
        compiler_params=_params(),
    )(dproj, w_in_t, x, g_pre, dres, *((dw_place,) if with_dw else ()), *deps)


HBM =pl.BlockSpec(memory_space=pltpu.HBM)
SEM = pl.BlockSpec(memory_space=pltpu.SEMAPHORE)
def _split_copy(collective_id=None):
    return pltpu.CompilerParams(has_side_effects=pltpu.SideEffectType.DATAFLOW_SIDE_EFFECTING,
                                collective_id=collective_id)


SPLIT_COPY = _split_copy()


def _in_hbm(a):
    return pltpu.with_memory_space_constraint(a, pltpu.HBM)

def _place():
    return lax.axis_index("x"), lax.axis_index("y"), lax.axis_index("c")


def _other_chips(x, y):
    return [(1 - x, y), (x, 1 - y), (1 - x, 1 - y)]


def _peer(x, y, c, m):
    return (x ^ (m >> 2), y ^ ((m >> 1) & 1), c ^ (m & 1))


SAME_CORE = (2, 4, 6)


def _place_cast(name, src, chip_arr, tile, layers, deps=()):
    _, n, cols = src.shape
    steps = n // tile
    k = len(layers)

    def body(chip_ref, *refs):
        for s_ref, o_ref in zip(refs[:k], refs[k + len(deps):]):
            o_ref[...] = s_ref[...].astype(BF16)

    def layer_spec(l):
        return pl.BlockSpec((None, tile, cols), lambda i, chip: (l, i, 0))

    return pl.pallas_call(
        body, name=name,
        grid_spec=pltpu.PrefetchScalarGridSpec(
            num_scalar_prefetch=1, grid=(steps,),
            in_specs=[layer_spec(l) for l in layers] + [ANY] * len(deps),
            out_specs=[pl.BlockSpec((tile, cols), lambda i, chip: (chip[0] * steps + i, 0))] * k),
        out_shape=[jax.ShapeDtypeStruct((N_SHARDS * n, cols), BF16)] * k,
        compiler_params=_params(),
    )(chip_arr, *[src] * k, *deps)


def _place_other_half(name, src, place_arr, layer, dest):
    _, n, cols = src.shape

    def body(place_ref, s_ref, dest_ref, o_ref):
        o_ref[...] = s_ref[...].astype(BF16)

    return pl.pallas_call(
        body, name=name,
        grid_spec=pltpu.PrefetchScalarGridSpec(
            num_scalar_prefetch=1, grid=(1,),
            in_specs=[pl.BlockSpec((None, n // 2, cols), lambda i, place: (layer, 1 - place[1], 0)), ANY],
            out_specs=pl.BlockSpec((n // 2, cols), lambda i, place: (place[0] + 1 - 2 * place[1], 0))),
        out_shape=jax.ShapeDtypeStruct((N_SHARDS * n, cols), BF16),
        input_output_aliases={2: 0},
        compiler_params=_params(),
    )(place_arr, src, dest)


def _chip_rows(ref, chip, half=None):
    n = ref.shape[0] // N_SHARDS
    if half is None:
        return ref.at[pl.ds(pl.multiple_of(chip * n, 16), n), :]
    return ref.at[pl.ds(pl.multiple_of(chip * n + half * (n // 2), 16), n // 2), :]


def _gather_start(name, bufs, halved, collective_id):
    n = len(bufs)

    def body(*refs):
        ins, send, recv, token = refs[:n], refs[n:2 * n], refs[2 * n:3 * n], refs[-1]
        x, y, c = _place()
        _handshake([(*chip, c) for chip in _other_chips(x, y)])
        for a, buf in enumerate(ins):
            own = _chip_rows(buf, 2 * x + y, c if a in halved else None)
            for j, chip in enumerate(_other_chips(x, y)):
                pltpu.make_async_remote_copy(src_ref=own, dst_ref=own, send_sem=send[a].at[j], recv_sem=recv[a].at[j],
                                             device_id=(*chip, c), device_id_type=MESH).start()
        token[...] = jnp.zeros_like(token)

    outs = pl.pallas_call(
        body, name=name, in_specs=[HBM] * n,
        out_specs=[SEM] * (2 * n) + [HBM] * n + [pl.BlockSpec(memory_space=pltpu.VMEM)],
        out_shape=[pltpu.SemaphoreType.DMA((3,))] * (2 * n) + [pltpu.HBM(b.shape, b.dtype) for b in bufs]
        + [jax.ShapeDtypeStruct((8, 128), F32)],
        input_output_aliases={a: 2 * n + a for a in range(n)},
        compiler_params=_split_copy(collective_id),
    )(*[_in_hbm(b) for b in bufs])
    return outs[:n], outs[n:2 * n], outs[2 * n:3 * n], outs[-1]


def _gather_start_cast(name, src, layer, collective_id):
    _, n, cols = src.shape
    half = n // 2

    def body(src_ref, send, recv, buf_ref, token, f32_ref, bf16_ref, local):
        x, y, c = _place()
        own = _chip_rows(buf_ref, 2 * x + y, c)

        def cast():
            load = pltpu.make_async_copy(src_ref.at[layer, pl.ds(pl.multiple_of(c * half, 16), half), :], f32_ref,
                                         local.at[0])
            load.start()
            load.wait()
            bf16_ref[...] = f32_ref[...].astype(BF16)
            store = pltpu.make_async_copy(bf16_ref, own, local.at[1])
            store.start()
            store.wait()

        _handshake([(*chip, c) for chip in _other_chips(x, y)], cast)
        for j, chip in enumerate(_other_chips(x, y)):
            pltpu.make_async_remote_copy(src_ref=own, dst_ref=own, send_sem=send.at[j], recv_sem=recv.at[j],
                                         device_id=(*chip, c), device_id_type=MESH).start()
        token[...] = jnp.zeros_like(token)

    outs = pl.pallas_call(
        body, name=name, in_specs=[HBM],
        out_specs=[SEM, SEM, HBM, pl.BlockSpec(memory_space=pltpu.VMEM)],
        out_shape=[pltpu.SemaphoreType.DMA((3,))] * 2 + [pltpu.HBM((N_SHARDS * n, cols), BF16),
                                                         jax.ShapeDtypeStruct((8, 128), F32)],
        scratch_shapes=[pltpu.VMEM((half, cols), F32), pltpu.VMEM((half, cols), BF16), pltpu.SemaphoreType.DMA((2,))],
        compiler_params=_split_copy(collective_id),
    )(_in_hbm(src))
    return outs[:1], outs[1:2], outs[2:3], outs[3]


def _gather_wait(name, buf, send_sem, recv_sem, after, halved=False):
    def body(buf_ref, send_ref, recv_ref, *rest):
        x, y, c = _place()
        half = c if halved else None
        own = _chip_rows(buf_ref, 2 * x + y, half)
        for j, chip in enumerate(_other_chips(x, y)):
            copy = pltpu.make_async_remote_copy(src_ref=own, dst_ref=_chip_rows(buf_ref, 2 * chip[0] + chip[1], half),
                                                send_sem=send_ref.at[j], recv_sem=recv_ref.at[j],
                                                device_id=(*chip, c), device_id_type=MESH)
            copy.wait_send()
            copy.wait_recv()

    return pl.pallas_call(
        body, name=name, in_specs=[HBM, SEM, SEM] + [ANY] * len(after), out_specs=HBM,
        out_shape=pltpu.HBM(buf.shape, buf.dtype), input_output_aliases={0: 0}, compiler_params=SPLIT_COPY,
    )(buf, send_sem, recv_sem, *after)


def _handshake(peers, meanwhile=None):
    barrier = pltpu.get_barrier_semaphore()
    for peer in peers:
        pl.semaphore_signal(barrier, inc=1, device_id=peer, device_id_type=MESH)
    if meanwhile is not None:
        meanwhile()
    pl.semaphore_wait(barrier, len(peers))


def _sibling_handshake(x, y, c):
    _handshake([(x, y, 1 - c)])


def _forward_halves(name, bufs, collective_id):
    n = len(bufs)

    def body(*refs):
        ins, outs, (send_sems, recv_sems) = refs[:n], refs[n:2 * n], refs[2 * n:]
        x, y, c = _place()
        _sibling_handshake(x, y, c)

        def copy(a, j, chip, half):
            rows = 2 * chip[0] + chip[1]
            return pltpu.make_async_remote_copy(
                src_ref=_chip_rows(ins[a], rows, half), dst_ref=_chip_rows(outs[a], rows, half),
                send_sem=send_sems.at[3 * a + j], recv_sem=recv_sems.at[3 * a + j], device_id=(x, y, 1 - c),
                device_id_type=MESH)

        copies = [(a, j, chip) for a in range(n) for j, chip in enumerate(_other_chips(x, y))]
        for a, j, chip in copies:
            copy(a, j, chip, c).start()
        for a, j, chip in copies:
            copy(a, j, chip, c).wait_send()
            copy(a, j, chip, 1 - c).wait_recv()

    return pl.pallas_call(
        body, name=name, in_specs=[ANY] * n, out_specs=[ANY] * n,
        out_shape=[jax.ShapeDtypeStruct(b.shape, b.dtype) for b in bufs],
        input_output_aliases={a: a for a in range(n)},
        scratch_shapes=[pltpu.SemaphoreType.DMA((3 * n,))] * 2,
        compiler_params=pltpu.CompilerParams(collective_id=collective_id),
    )(*bufs)


def _piece_rows(ref, k):
    p = ref.shape[0] // 8
    return ref.at[pl.ds(pl.multiple_of(k * p, 32 // jnp.dtype(ref.dtype).itemsize), p), :]


def _exchange_start(name, arrays, collective_id):
    n = len(arrays)
    zones = [lax.empty((7, a.shape[0] // 8, a.shape[1]), a.dtype) for a in arrays]

    def body(*refs):
        srcs, lands = refs[:n], refs[n:2 * n]
        send, recv, token = refs[2 * n:3 * n], refs[3 * n:4 * n], refs[-1]
        x, y, c = _place()
        _handshake([_peer(x, y, c, m) for m in range(1, 8)])
        for a, (src, land) in enumerate(zip(srcs, lands)):
            for m in range(1, 8):
                px, py, pc = _peer(x, y, c, m)
                pltpu.make_async_remote_copy(
                    src_ref=_piece_rows(src, 4 * px + 2 * py + pc), dst_ref=land.at[m - 1], send_sem=send[a].at[m - 1],
                    recv_sem=recv[a].at[m - 1], device_id=(px, py, pc), device_id_type=MESH).start()
        token[...] = jnp.zeros_like(token)

    outs = pl.pallas_call(
        body, name=name, in_specs=[HBM] * (2 * n),
        out_specs=[SEM] * (2 * n) + [HBM] * (2 * n) + [pl.BlockSpec(memory_space=pltpu.VMEM)],
        out_shape=[pltpu.SemaphoreType.DMA((7,))] * (2 * n) + [pltpu.HBM(a.shape, a.dtype) for a in arrays + zones]
        + [jax.ShapeDtypeStruct((8, 128), F32)],
        input_output_aliases={a: 2 * n + a for a in range(2 * n)},
        compiler_params=_split_copy(collective_id),
    )(*[_in_hbm(a) for a in arrays + zones])
    return outs[:n], outs[n:2 * n], outs[2 * n:3 * n], outs[3 * n:4 * n], outs[-1]


def _exchange_wait(name, started, after, which=None, with_sent=False):
    which = range(len(started[2])) if which is None else which
    send_sems, recv_sems, arrays, zones = [[group[k] for k in which] for group in started[:4]]
    n = len(arrays)

    def body(*refs):
        srcs, lands = refs[:n], refs[n:2 * n]
        send, recv = refs[2 * n:3 * n], refs[3 * n:4 * n]
        x, y, c = _place()
        for a, (src, land) in enumerate(zip(srcs, lands)):
            for m in range(1, 8):
                px, py, pc = _peer(x, y, c, m)
                copy = pltpu.make_async_remote_copy(
                    src_ref=_piece_rows(src, 4 * px + 2 * py + pc), dst_ref=land.at[m - 1], send_sem=send[a].at[m - 1],
                    recv_sem=recv[a].at[m - 1], device_id=(px, py, pc), device_id_type=MESH)
                copy.wait_send()
                copy.wait_recv()

    outs = pl.pallas_call(
        body, name=name, in_specs=[HBM] * (2 * n) + [SEM] * (2 * n) + [ANY], out_specs=[HBM] * (2 * n),
        out_shape=[pltpu.HBM(a.shape, a.dtype) for a in list(arrays) + list(zones)],
        input_output_aliases={a: a for a in range(2 * n)}, compiler_params=SPLIT_COPY,
    )(*arrays, *zones, *send_sems, *recv_sems, after)
    return outs if with_sent else outs[n:]


def _sum_pieces(name, weights, place_arr, dests=None, small=()):
    steps = 2
    flat = [item for items in weights for item in items]
    n = len(flat)

    def body(place_ref, *refs):
        first_out = len(refs) - len(weights) - len(small)
        outs = iter(refs[first_out:])
        small_refs = refs[first_out - 2 * len(small):first_out]

        @pl.when(pl.program_id(0) == 0)
        def _():
            for j in range(len(small)):
                total = small_refs[2 * j][...]
                for m in range(7):
                    total = total + small_refs[2 * j + 1][m]
                refs[first_out + len(weights) + j][...] = total

        k = 0
        for items in weights:
            out_ref = next(outs)
            for layer, _, _ in items:
                total = refs[k][...]
                for m in range(7):
                    total = total + refs[n + k][m].astype(F32)
                if len(items) == DEPTH:
                    out_ref[layer] = total
                else:
                    out_ref[...] = total
                k += 1

    def out_spec(items):
        _, own, _ = items[0]
        t, cols = own.shape[0] // steps, own.shape[1]
        if len(items) == DEPTH:
            return pl.BlockSpec((DEPTH, t, cols), lambda i, place: (0, place[1] * steps + i, 0))
        layer = items[0][0]
        return pl.BlockSpec((None, t, cols), lambda i, place: (layer, place[1] * steps + i, 0))

    owns = [own for _, own, _ in flat]
    dests = [] if dests is None else list(dests)
    piece = lambda a: pl.BlockSpec((a.shape[0] // 8, a.shape[1]), lambda i, place: (place[0], 0))
    small_specs = [spec for a, r in small for spec in (piece(a), pl.BlockSpec(r.shape, lambda i, place: (0, 0, 0)))]
    return pl.pallas_call(
        body, name=name,
        grid_spec=pltpu.PrefetchScalarGridSpec(
            num_scalar_prefetch=1, grid=(steps,),
            in_specs=[pl.BlockSpec((o.shape[0] // steps, o.shape[1]), lambda i, place: (i, 0)) for o in owns]
            + [pl.BlockSpec((7, o.shape[0] // steps, o.shape[1]), lambda i, place: (0, i, 0)) for o in owns]
            + [ANY] * len(dests) + small_specs,
            out_specs=[out_spec(items) for items in weights] + [piece(a) for a, _ in small]),
        out_shape=[jax.ShapeDtypeStruct((DEPTH, 2 * items[0][1].shape[0], items[0][1].shape[1]), F32)
                   for items in weights] + [jax.ShapeDtypeStruct(a.shape, F32) for a, _ in small],
        input_output_aliases={1 + 2 * n + k: k for k in range(len(dests))},
        compiler_params=_params(),
    )(place_arr, *owns, *[recv for _, _, recv in flat], *dests, *[a for pair in small for a in pair])


def _sum_small(name, partials, recvs, place_arr):
    n = len(partials)

    def body(place_ref, *refs):
        for o_ref, r_ref, out_ref in zip(refs[:n], refs[n:2 * n], refs[2 * n:]):
            total = o_ref[...]
            for m in range(7):
                total = total + r_ref[m]
            out_ref[...] = total

    piece = lambda a: pl.BlockSpec((a.shape[0] // 8, a.shape[1]), lambda i, place: (place[0], 0))
    return pl.pallas_call(
        body, name=name,
        grid_spec=pltpu.PrefetchScalarGridSpec(
            num_scalar_prefetch=1, grid=(1,),
            in_specs=[piece(a) for a in partials] + [pl.BlockSpec(r.shape, lambda i, place: (0, 0, 0)) for r in recvs],
            out_specs=[piece(a) for a in partials]),
        out_shape=[jax.ShapeDtypeStruct(a.shape, F32) for a in partials],
        compiler_params=_params(),
    )(place_arr, *partials, *recvs)


def _share(name, bufs, parts, gathered=(), collective_id=None, summed=()):
    n, n_g, n_s = len(bufs), len(gathered), len(summed)
    total = n + n_g
    made = [target for target, _, _ in summed]

    def body(*refs):
        ins, extra, outs = refs[:total], refs[total:total + 2 * n_s], refs[total + 2 * n_s:2 * total + 2 * n_s]
        send_sems, recv_sems, send_g, recv_g = refs[2 * total + 2 * n_s:2 * total + 2 * n_s + 4]
        scratch = refs[2 * total + 2 * n_s + 4:]
        x, y, c = _place()

        def half(ref, l, which):
            p = ref.shape[1] // 2
            return ref.at[l, pl.ds(pl.multiple_of(which * p, 8), p), :]

        def sums():
            local, acc, got = scratch[0], scratch[1:1 + n_s], scratch[1 + n_s:]
            loads, stores = [], []
            for j, (target, _, _) in enumerate(summed):
                own_rows = extra[2 * j] if target[0] == "half" else _piece_rows(extra[2 * j], 4 * x + 2 * y + c)
                loads += [pltpu.make_async_copy(own_rows, acc[j], local.at[2 * j]),
                          pltpu.make_async_copy(extra[2 * j + 1], got[j], local.at[2 * j + 1])]
            for load in loads:
                load.start()
            for j, (target, _, _) in enumerate(summed):
                loads[2 * j].wait()
                loads[2 * j + 1].wait()
                rows = acc[j].shape[0]
                step = min(rows, 96)
                for r in range(0, rows, step):
                    part = acc[j][r:r + step, :]
                    for m in range(7):
                        part = part + got[j][m, r:r + step, :].astype(F32)
                    acc[j][r:r + step, :] = part
                if target[0] == "half":
                    dest = half(outs[target[1]], target[2], c)
                else:
                    dest = _piece_rows(outs[n + target[1]], 4 * x + 2 * y + c)
                stores.append(pltpu.make_async_copy(acc[j], dest, local.at[2 * j]))
                stores[-1].start()
            for store in stores:
                store.wait()

        _handshake([_peer(x, y, c, m) for m in (SAME_CORE if gathered else ()) + (1,)], sums if summed else None)

        def swap(k, which):
            a, l = parts[k]
            held = outs if ("half", a, l) in made else ins
            return pltpu.make_async_remote_copy(
                src_ref=half(held[a], l, which), dst_ref=half(outs[a], l, which), send_sem=send_sems.at[k],
                recv_sem=recv_sems.at[k], device_id=(x, y, 1 - c), device_id_type=MESH)

        def spread(a, m, sender, held, to):
            k = 4 * sender[0] + 2 * sender[1] + sender[2]
            return pltpu.make_async_remote_copy(
                src_ref=_piece_rows(held[n + a], k), dst_ref=_piece_rows(outs[n + a], k),
                send_sem=send_g.at[7 * a + m - 1], recv_sem=recv_g.at[7 * a + m - 1], device_id=to, device_id_type=MESH)

        me, sibling = (x, y, c), (x, y, 1 - c)
        for k in range(len(parts)):
            swap(k, c).start()
        def own(a, m):
            return spread(a, m, me, outs if ("piece", a) in made else ins, _peer(x, y, c, m))

        def handed_on(a, m):
            return spread(a, m + 1, _peer(x, y, c, m), outs, sibling)

        for a in range(n_g):
            for m in SAME_CORE + (1,):
                own(a, m).start()
        for a in range(n_g):
            for m in SAME_CORE:
                spread(a, m, _peer(x, y, c, m), ins, _peer(x, y, c, m)).wait_recv()
                handed_on(a, m).start()
        for k in range(len(parts)):
            swap(k, c).wait_send()
            swap(k, 1 - c).wait_recv()
        for a in range(n_g):
            for m in SAME_CORE + (1,):
                own(a, m).wait_send()
            for m in SAME_CORE:
                handed_on(a, m).wait_send()
                spread(a, m + 1, _peer(x, y, c, m + 1), ins, sibling).wait_recv()
            spread(a, 1, sibling, ins, sibling).wait_recv()

    arrays = list(bufs) + list(gathered)
    sum_scratch = []
    if summed:
        sum_scratch = [pltpu.SemaphoreType.DMA((2 * n_s,))] + [pltpu.VMEM(recv.shape[1:], F32) for _, _, recv in summed]
        sum_scratch += [pltpu.VMEM(recv.shape, recv.dtype) for _, _, recv in summed]
    return pl.pallas_call(
        body, name=name, in_specs=[ANY] * (total + 2 * n_s), out_specs=[ANY] * total,
        out_shape=[jax.ShapeDtypeStruct(b.shape, F32) for b in arrays],
        input_output_aliases={a: a for a in range(total)},
        scratch_shapes=[pltpu.SemaphoreType.DMA((max(len(parts), 1),))] * 2
        + [pltpu.SemaphoreType.DMA((max(7 * n_g, 1),))] * 2 + sum_scratch,
        compiler_params=pltpu.CompilerParams(collective_id=collective_id, vmem_limit_bytes=VMEM_LIMIT),
    )(*arrays, *[array for _, own, recv in summed for array in (own, recv)])


def _adamw_math(w, g, m, v):
    nm = ADAM_B1 * m + (1.0 - ADAM_B1) * g
    nv = ADAM_B2 * v + (1.0 - ADAM_B2) * (g * g)
    m_hat = nm / (1.0 - ADAM_B1 ** ADAM_STEP)
    v_hat = nv / (1.0 - ADAM_B2 ** ADAM_STEP)
    return -ADAM_LR * (m_hat / (jnp.sqrt(v_hat) + ADAM_EPS) + ADAM_WD * w), nm, nv


def _adamw(name, w, g, m, v, rows_per_step, first=0, count=None, dests=None, deps=()):
    layers, rows, cols = w.shape
    count = layers if count is None else count

    def body(w_ref, g_ref, m_ref, v_ref, *rest):
        d_ref, nm_ref, nv_ref, g_out_ref = rest[-4:]
        d_ref[...], nm_ref[...], nv_ref[...] = _adamw_math(w_ref[...], g_ref[...], m_ref[...], v_ref[...])
        g_out_ref[...] = g_ref[...]

    spec = pl.BlockSpec((1, rows_per_step, cols), lambda l, i: (first + l, i, 0))
    shape = jax.ShapeDtypeStruct(w.shape, F32)
    dests = () if dests is None else tuple(dests)
    return pl.pallas_call(
        body, name=name, grid=(count, rows // rows_per_step),
        in_specs=[spec] * 4 + [ANY] * (len(dests) + len(deps)), out_specs=[spec] * 4, out_shape=[shape] * 4,
        input_output_aliases={4 + k: k for k in range(len(dests))},
        compiler_params=_params(("arbitrary", "arbitrary")),
    )(w, g, m, v, *dests, *deps)


def _pack_misc(pool_scale, sinks, norm_pre, norm_post):
    sink_rows = jnp.zeros((DEPTH, 8, 128), F32).at[:, 0, 0:N_HEADS].set(sinks).reshape(2 * 8, 128)
    return jnp.concatenate([pool_scale.reshape(8, 128), norm_pre.reshape(16, 128), norm_post.reshape(16, 128),
                            sink_rows, jnp.zeros((8, 128), F32)], axis=0)


def _adamw_small(w, g, m, v, pool):
    def body(w_ref, g_ref, m_ref, v_ref, pw_ref, pg_ref, pm_ref, pv_ref, *rest):
        outs, pool_outs, (d_ref, nm_ref, nv_ref) = rest[:17], rest[17:21], rest[21:]
        pool_outs[0][...] = pg_ref[...]
        pool_outs[1][...], pool_outs[2][...], pool_outs[3][...] = _adamw_math(
            pw_ref[...], pg_ref[...], pm_ref[...], pv_ref[...])
        d_ref[...], nm_ref[...], nv_ref[...] = _adamw_math(w_ref[...], g_ref[...], m_ref[...], v_ref[...])
        for k, src in enumerate([g_ref, d_ref, nm_ref, nv_ref]):
            scale, sinks, pre, post = outs[4 * k:4 * k + 4]
            for l in range(DEPTH):
                for j in range(4):
                    scale[l:l + 1, j * 128:(j + 1) * 128] = src[MISC_SCALE + 4 * l + j:MISC_SCALE + 4 * l + j + 1, :]
                for j in range(8):
                    pre[l:l + 1, j * 128:(j + 1) * 128] = src[MISC_PRE + 8 * l + j:MISC_PRE + 8 * l + j + 1, :]
                    post[l:l + 1, j * 128:(j + 1) * 128] = src[MISC_POST + 8 * l + j:MISC_POST + 8 * l + j + 1, :]
                sinks[l:l + 1, :] = src[MISC_SINKS + 8 * l:MISC_SINKS + 8 * l + 1, 0:N_HEADS]
        outs[16][...] = g_ref[MISC_LOSS:MISC_LOSS + 1, 0:1]

    vmem = pl.BlockSpec(memory_space=pltpu.VMEM)
    shapes = [(DEPTH, D_POOL), (DEPTH, N_HEADS), (DEPTH, D), (DEPTH, D)] * 4 + [(1, 1)]
    shapes += [pool[0].shape] * 4
    return pl.pallas_call(
        body, name="adamw_small", in_specs=[vmem] * 8, out_specs=[vmem] * 21,
        out_shape=[jax.ShapeDtypeStruct(s, F32) for s in shapes],
        scratch_shapes=[pltpu.VMEM((MISC_ROWS, 128), F32)] * 3,
    )(w, g, m, v, *pool)


def kernel(x, w_in, pool_w, pool_scale, attn_sinks, w_out, norm_pre, norm_post, loss_target, m_w_in, m_pool_w, m_pool_scale, m_attn_sinks, m_w_out, m_norm_pre, m_norm_post, v_w_in, v_pool_w, v_pool_scale, v_attn_sinks, v_w_out, v_norm_pre, v_norm_post):
    cx, cy, cc = _place()
    chip_arr = jnp.reshape(2 * cx + cy, (1,)).astype(jnp.int32)
    place_arr = jnp.stack([4 * cx + 2 * cy + cc, cc]).astype(jnp.int32)
    t = lambda a: jnp.transpose(a, (0, 2, 1))
    w_in_t = t(w_in)
    xs, target = x[0], loss_target[0]
    pool_w_b = pool_w.astype(BF16)
    tables = _attention_tables()
    scale3 = pool_scale.reshape(DEPTH, 1, D_POOL)
    pre3 = norm_pre.reshape(DEPTH, 1, D)
    post3 = norm_post.reshape(DEPTH, 1, D)

    first = _gather_start_cast("gather_start_first", w_in_t, 0, ID_GATHER_FIRST)
    wi0 = _place_other_half("place_w_in0_rest", w_in_t, place_arr, 0, first[2][0])
    (wi1,) = _place_cast("place_w_in1", w_in_t, chip_arr, 288, [1], deps=(first[3],))
    wo = _place_cast("place_w_out", w_out, chip_arr, 256, [0, 1], deps=(first[3],))
    rest = _gather_start("gather_start_rest", [wi1, wo[0], wo[1]], halved=(0, 1), collective_id=ID_GATHER_REST)
    send, recv, bufs = [first[k] + rest[k] for k in range(3)]
    bufs = [wi0, *bufs[1:]]
    order = {(0, "in"): 0, (1, "in"): 1, (0, "out"): 2, (1, "out"): 3}

    saved = []
    packed = [_pack_misc(pool_scale, attn_sinks, norm_pre, norm_post),
              _pack_misc(m_pool_scale, m_attn_sinks, m_norm_pre, m_norm_post),
              _pack_misc(v_pool_scale, v_attn_sinks, v_norm_pre, v_norm_post)]
    after = (first[3], rest[3], pool_w_b, *tables, scale3, pre3, post3, *packed)
    below = None
    for l in range(DEPTH):
        k = order[l, "in"]
        halves = [_gather_wait(f"gather_wait_in{l}", bufs[k], send[k], recv[k], after, halved=True)]
        if below is not None:
            halves.append(below[1])
        w_in_l, *w_out_below = _forward_halves(f"forward_w{l}", halves, collective_id=ID_FORWARD[l])
        if below is None:
            pu, pg, q, kv, ag = _fwd_in(l, xs, pre3, w_in_l)
        else:
            saved[l - 1][9] = w_out_below[0]
            y, xs, pu, pg, q, kv, ag = _fwd_in(l, xs, pre3, w_in_l, (below[0], w_out_below[0], below[2]))
            saved[l - 1][7] = y
        cat = _fwd_mix(l, pu, pg, q, kv, ag, pool_w_b, scale3, attn_sinks, tables)
        k = order[l, "out"]
        w_out_l = _gather_wait(f"gather_wait_out{l}", bufs[k], send[k], recv[k], (cat,), halved=l + 1 < DEPTH)
        saved.append([xs, pu, pg, q, kv, ag, cat, None, w_in_l, w_out_l])
        below, after = (cat, w_out_l, post3), (w_out_l,)

    x_in, pu, pg, q, kv, ag, cat, y, w_in_l, w_out_l = saved[1]
    dcat, dw_out1, dw_out1_b, dg_post1, loss, xs = _bwd_out(1, cat, w_out_l, post3, place_arr, x=x_in, target=target)
    ex1_out = _exchange_start("exchange_start_out1", [dw_out1_b], ID_OUT1)
    dproj, dpw, dsc1, dsink1 = _bwd_mix(1, pu, pg, q, kv, ag, dcat, pool_w_b, scale3, attn_sinks, tables,
                                        deps=(ex1_out[4],))
    dx, dg_pre1, dw_in1, dw_in1_b = _bwd_in_dx(1, dproj, w_in_l, x_in, pre3, xs, dw_place=place_arr)
    ex1_in = _exchange_start("exchange_start_in1", [dw_in1_b], ID_IN1)

    x_in, pu, pg, q, kv, ag, cat, y, w_in_l, w_out_l = saved[0]
    dcat, dw_out0, dw_out0_b, dg_post0 = _bwd_out(0, cat, w_out_l, post3, place_arr, dxn=dx, y=y, deps=(ex1_in[4],))
    ex0_out = _exchange_start("exchange_start_out0", [dw_out0_b], ID_OUT0)
    dproj, dpw, dsc0, dsink0 = _bwd_mix(0, pu, pg, q, kv, ag, dcat, pool_w_b, scale3, attn_sinks, tables,
                                        deps=(ex0_out[4],), dpw_dest=dpw)
    dw_in0, dw_in0_b = _bwd_in_dw(0, dproj, x_in, pre3, place_arr)
    flat = lambda a: a.reshape(DEPTH * 4 * 128, 128)
    ex0_in = _exchange_start("exchange_start_in0", [flat(dpw), dw_in0_b], ID_IN0)

    grad_x, dg_pre0 = _bwd_in_dx(0, dproj, w_in_l, x_in, pre3, dx, deps=(ex0_in[4],))
    small = [jnp.concatenate([dsc0, dsc1, dg_pre0, dg_pre1, dg_post0, dg_post1, dsink0, dsink1, loss], axis=0)]
    ex_small = _exchange_start("exchange_start_small", small, ID_SMALL)
    (recv_out1,) = _exchange_wait("exchange_wait_out1", ex1_out, ex_small[4])
    (recv_in1,) = _exchange_wait("exchange_wait_in1", ex1_in, recv_out1)
    g_in, g_out = _sum_pieces("sum_pieces_1", [[(1, dw_in1, recv_in1)], [(1, dw_out1, recv_out1)]], place_arr)
    (recv_out0,) = _exchange_wait("exchange_wait_out0", ex0_out, g_out)
    (g_out,) = _sum_pieces("sum_pieces_out0", [[(0, dw_out0, recv_out0)]], place_arr, dests=[g_out])
    g_in, g_out = _share("share_a", [g_in, g_out], [(0, 1), (1, 0), (1, 1)], collective_id=ID_SHARE_A)
    m_in_t, v_in_t = t(m_w_in), t(v_w_in)
    d_out, nm_out, nv_out, grad_w_out = _adamw("adamw_w_out", w_out, g_out, m_w_out, v_w_out, 256)
    upd_in = _adamw("adamw_w_in1", w_in_t, g_in, m_in_t, v_in_t, 288, first=1, count=1, deps=(d_out,))
    dpw_own, recv_pw = _exchange_wait("exchange_wait_pool", ex0_in, upd_in[0], which=[0], with_sent=True)
    (g_pw,) = _sum_small("sum_pool", [dpw_own], [recv_pw], place_arr)

    (recv_in0,) = _exchange_wait("exchange_wait_in0", ex0_in, g_pw, which=[1])
    (recv_misc,) = _exchange_wait("exchange_wait_small", ex_small, recv_in0)
    g_in, g_pw, g_misc = _share(
        "share_b", [g_in], [(0, 0)], [g_pw, lax.empty(small[0].shape, F32)], collective_id=ID_SHARE_B,
        summed=[(("half", 0, 0), dw_in0, recv_in0), (("piece", 1), small[0], recv_misc)])
    d_in, nm_in, nv_in, grad_w_in_t = _adamw("adamw_w_in0", w_in_t, g_in, m_in_t, v_in_t, 288, first=0, count=1,
                                             dests=upd_in)
    small_out = _adamw_small(packed[0], g_misc, packed[1], packed[2],
                             (flat(pool_w), g_pw, flat(m_pool_w), flat(v_pool_w)))
    (g_sc, g_sk, g_pre, g_post, d_sc, d_sk, d_pre, d_post,
     m_sc, m_sk, m_pre, m_post, v_sc, v_sk, v_pre, v_post, loss_sum) = small_out[:17]
    g_pw, d_pw, m_pw, v_pw = [a.reshape(pool_w.shape) for a in small_out[17:]]
    return (loss_sum[0, 0], grad_x[None], t(grad_w_in_t), g_pw, g_sc, g_sk, grad_w_out, g_pre, g_post,
            t(d_in), d_pw, d_sc, d_sk, d_out, d_pre, d_post,
            t(nm_in), m_pw, m_sc, m_sk, nm_out, m_pre, m_post,
            t(nv_in), v_pw, v_sc, v_sk, nv_out, v_pre, v_post)
```

```python
import jax
import jax.numpy as jnp
from jax import lax
from jax.experimental import pallas as pl
from jax.experimental.pallas import tpu as pltpu

F32 = jnp.float32
BF16 = jnp.bfloat16

S = 2048
D = 1024
DEPTH = 2
D_POOL = 512
POOL_WINDOWS = (2, 4, 8, 16)
N_HEADS = 8
D_IN = 2304
N_SHARDS = 4
W_IN_SHARD = D_IN // N_SHARDS
W_OUT_SHARD = D // N_SHARDS
BLK = 128
NB = S // BLK
HALO = 16
PAD = 8
EPS = 1e-6
NEG_INF = -1e30
C_PU, C_PG, C_Q, C_K, C_V, C_AG = 0, 512, 1024, 1536, 1664, 1792

ADAM_LR = 0.001
ADAM_B1 = 0.9
ADAM_B2 = 0.999
ADAM_EPS = 1e-08
ADAM_WD = 0.01
ADAM_STEP = 10

TM = 512
VMEM_LIMIT = 56 * 1024 * 1024

NT = (((1,), (1,)), ((), ()))
TN = (((0,), (0,)), ((), ()))

MESH = pl.DeviceIdType.MESH
ANY = pl.BlockSpec(memory_space=pl.ANY)

ID_FORWARD = (0, 1)
(ID_SHARE_A, ID_SHARE_B, ID_GATHER_FIRST, ID_GATHER_REST, ID_OUT1, ID_IN1, ID_OUT0, ID_IN0, ID_SMALL) = range(2, 11)

MISC_SCALE, MISC_PRE, MISC_POST, MISC_SINKS, MISC_LOSS = 0, 8, 24, 40, 56
MISC_ROWS = 64


def _params(sem=("arbitrary",)):
    return pltpu.CompilerParams(dimension_semantics=sem, vmem_limit_bytes=VMEM_LIMIT)


def _sigmoid(v):
    return 1.0 / (1.0 + jnp.exp(-v))


def _rows8(v):
    r, c = v.shape
    return v.reshape(r // 8, 8, c).sum(axis=0)


def _layer(l, *shape):
    zeros = (0,) * len(shape)
    return pl.BlockSpec((None,) + shape, lambda i: (l,) + zeros)


def _whole(shape):
    zeros = (0,) * len(shape)
    return pl.BlockSpec(shape, lambda i: zeros, pipeline_mode=pl.Buffered(1))


def _fwd_in(l, x, g_pre, w_in_t, below=None):
    fused = below is not None

    def body(x_ref, g_ref, w_ref, *rest):
        if fused:
            cat_ref, wo_ref, gp_ref, y_ref, xn_ref = rest[:5]
            y = jnp.dot(cat_ref[...], wo_ref[...], preferred_element_type=F32)
            y_ref[...] = y
            xt = x_ref[...] + y * lax.rsqrt(jnp.mean(y * y, axis=-1, keepdims=True) + EPS) * gp_ref[...]
            xn_ref[...] = xt
        else:
            xt = x_ref[...]
        pu_ref, pg_ref, q_ref, kv_ref, ag_ref = rest[-5:]
        r = lax.rsqrt(jnp.mean(xt * xt, axis=-1, keepdims=True) + EPS)
        h = (xt * r * g_ref[...]).astype(BF16)

        def proj(lo, hi):
            return lax.dot_general(h, w_ref[lo:hi, :], NT, preferred_element_type=F32)

        pu_ref[...] = proj(C_PU, C_PG)
        pg_ref[...] = proj(C_PG, C_Q)
        q_ref[...] = proj(C_Q, C_K).astype(BF16)
        kv_ref[...] = proj(C_K, C_AG).astype(BF16)
        ag_ref[...] = proj(C_AG, D_IN)

    row = lambda w: pl.BlockSpec((TM, w), lambda i: (i, 0))
    act = jax.ShapeDtypeStruct((S, D), F32)
    return pl.pallas_call(
        body, name="fwd_out_in" if fused else "fwd_in", grid=(S // TM,),
        in_specs=[row(D), _layer(l, 1, D), _whole((D_IN, D))]
        + ([row(D), _whole((D, D)), _layer(l - 1, 1, D)] if fused else []),
        out_specs=[row(D)] * (2 * fused) + [row(512), row(512), row(512), row(256), row(512)],
        out_shape=[act] * (2 * fused)
        + [jax.ShapeDtypeStruct((S, 512), F32), jax.ShapeDtypeStruct((S, 512), F32),
           jax.ShapeDtypeStruct((S, 512), BF16), jax.ShapeDtypeStruct((S, 256), BF16),
           jax.ShapeDtypeStruct((S, 512), F32)],
        compiler_params=_params(),
    )(x, g_pre, w_in_t, *(below if fused else ()))


LOG2E = 1.4426950408889634
SCORE_SCALE = 0.125 * LOG2E


def _attention_tables():
    qi = jnp.arange(BLK)[:, None]
    kj = jnp.arange(BLK)[None, :]
    dist = ((qi - kj) % BLK).astype(F32)
    slopes = jnp.exp2(-jnp.arange(1, N_HEADS + 1, dtype=F32))
    bias = -(slopes * LOG2E)[:, None, None] * dist[None]
    first = jnp.where(kj > qi, NEG_INF, bias)
    return jnp.stack([first, bias]), (kj <= qi).astype(BF16)


def _own_block_mask():
    return lax.broadcasted_iota(jnp.int32, (BLK, BLK), 1) <= lax.broadcasted_iota(jnp.int32, (BLK, BLK), 0)


def _merge(full, own):
    return jnp.where(own, full[:, BLK:], full[:, :BLK])


def _spread(v, tri):
    own = v * tri
    return jnp.concatenate([v - own, own], axis=1)


def _head_variants(cur, prev):
    both = jnp.concatenate([prev, cur], axis=0).astype(F32)
    swapped = pltpu.roll(both, 64, axis=1)
    low = lax.broadcasted_iota(jnp.int32, both.shape, 1) < 64
    zero = jnp.zeros_like(both)
    return ((jnp.where(low, both, zero).astype(BF16), jnp.where(low, zero, swapped).astype(BF16)),
            (jnp.where(low, swapped, zero).astype(BF16), jnp.where(low, zero, both).astype(BF16)))


def _head_of(hkv, t, half):
    return hkv * 4 + 2 * t + half


def _rows(v, t):
    return v[t * BLK:(t + 1) * BLK]


def _stack_tiles(ref, hkv, offset=0):
    lo = offset + 2 * hkv * 128
    return jnp.concatenate([ref[:, lo:lo + 128], ref[:, lo + 128:lo + 256]], axis=0)


def _scores(q2, k_var, own):
    s = {}
    for hkv in range(2):
        for half in range(2):
            full = lax.dot_general(q2[hkv], k_var[hkv][half], NT, preferred_element_type=F32)
            for t in range(2):
                s[hkv, t, half] = _merge(_rows(full, t), own)
    return s


def _softmax(s, bias, sink):
    s = s * SCORE_SCALE + bias
    sink2 = sink * LOG2E
    m = jnp.maximum(jnp.max(s, axis=-1, keepdims=True), sink2)
    p = jnp.exp2(s - m)
    e_sink = jnp.exp2(sink2 - m)
    inv = 1.0 / (jnp.sum(p, axis=-1, keepdims=True) + e_sink)
    return p * inv, e_sink * inv


def _spread_pair(v, hkv, half, tri):
    return jnp.concatenate([_spread(v[hkv, t, half].astype(BF16), tri) for t in range(2)], axis=0)


POOL_ROWS = PAD + HALO + BLK


def _window_sums(src_ref, tmp_refs, trailing):
    lo, hi = (PAD, POOL_ROWS) if trailing else (0, HALO + BLK)
    cur = src_ref
    for level in range(len(POOL_WINDOWS)):
        lanes = slice(level * 128, 512)
        shift = -(1 << level) if trailing else (1 << level)
        dst = tmp_refs[level % 2]
        dst[lo:hi, lanes] = cur[lo:hi, lanes] + cur[lo + shift:hi + shift, lanes]
        cur = dst


def _pool_block(ext_ref, tmp_refs, i, g, w):
    lanes = slice(g * 128, (g + 1) * 128)
    rows = slice(PAD + HALO, POOL_ROWS)
    t = (i * BLK + lax.broadcasted_iota(jnp.int32, (BLK, 1), 0)).astype(F32)
    inv = 1.0 / jnp.minimum(t + 1.0, float(w))
    return tmp_refs[g % 2][rows, lanes] * inv - ext_ref[rows, lanes], inv


def _fwd_mix(l, pu, pg, q, kv, ag, pool_w, pool_scale, sinks, tables):
    bias, tri = tables

    def body(pu_ref, pup_ref, pg_ref, q_ref, kv_ref, kvp_ref, ag_ref, pw_ref, sc_ref, sink_ref, bias_ref, tri_ref,
             cat_ref, ext_ref, *tmp_refs):
        i = pl.program_id(0)

        @pl.when(i == 0)
        def _():
            for ref in (ext_ref, *tmp_refs):
                ref[0:PAD, :] = jnp.zeros((PAD, 512), F32)

        ext_ref[PAD:PAD + HALO, :] = jnp.where(i > 0, pup_ref[...], 0.0)
        ext_ref[PAD + HALO:POOL_ROWS, :] = pu_ref[...]
        _window_sums(ext_ref, tmp_refs, True)
        for g, w in enumerate(POOL_WINDOWS):
            lanes = slice(g * 128, (g + 1) * 128)
            pooled, _ = _pool_block(ext_ref, tmp_refs, i, g, w)
            mixed = jnp.dot(pooled.astype(BF16), pw_ref[g], preferred_element_type=F32)
            gate = pg_ref[:, lanes]
            cat_ref[:, lanes] = (mixed * sc_ref[:, lanes] * (gate * _sigmoid(gate))).astype(BF16)

        own = _own_block_mask()
        tri = tri_ref[...]
        k_var = _head_variants(kv_ref[:, 0:128], kvp_ref[:, 0:128])
        v_var = _head_variants(kv_ref[:, 128:256], kvp_ref[:, 128:256])
        s = _scores([_stack_tiles(q_ref, hkv) for hkv in range(2)], k_var, own)
        p = {}
        for (hkv, t, half), s_head in s.items():
            head = _head_of(hkv, t, half)
            p[hkv, t, half], _ = _softmax(s_head, bias_ref[head], sink_ref[l, head])
        for hkv in range(2):
            o2 = jnp.zeros((2 * BLK, 128), F32)
            for half in range(2):
                o2 = o2 + jnp.dot(_spread_pair(p, hkv, half, tri), v_var[hkv][half], preferred_element_type=F32)
            for t in range(2):
                lo = (2 * hkv + t) * 128
                gate = ag_ref[:, lo:lo + 128]
                cat_ref[:, D_POOL + lo:D_POOL + lo + 128] = (_rows(o2, t) * (gate * _sigmoid(gate))).astype(BF16)

    blk = lambda w: pl.BlockSpec((BLK, w), lambda i: (i, 0))
    prev = lambda w: pl.BlockSpec((BLK, w), lambda i: (jnp.maximum(i - 1, 0), 0))
    halo = pl.BlockSpec((HALO, 512), lambda i: (jnp.maximum(i * (BLK // HALO) - 1, 0), 0))
    return pl.pallas_call(
        body, name="fwd_mix", grid=(NB,),
        in_specs=[blk(512), halo, blk(512), blk(512), blk(256), prev(256), blk(512),
                  _layer(l, 4, 128, 128), _layer(l, 1, 512), pl.BlockSpec(memory_space=pltpu.SMEM),
                  pl.BlockSpec((None, N_HEADS, BLK, BLK), lambda i: (jnp.minimum(i, 1), 0, 0, 0)), _whole((BLK, BLK))],
        out_specs=blk(D),
        out_shape=jax.ShapeDtypeStruct((S, D), BF16),
        scratch_shapes=[pltpu.VMEM((POOL_ROWS, 512), F32)] * 3,
        compiler_params=_params(),
    )(pu, pu, pg, q, kv, kv, ag, pool_w, pool_scale, sinks, bias, tri)


def _store_lane_rows(ref, acc):
    total = jnp.sum(acc, axis=0, keepdims=True)
    for k in range(ref.shape[0]):
        ref[k:k + 1, :] = total[:, k * 128:(k + 1) * 128]


def _own_piece(dw_ref, place_ref):
    p = dw_ref.shape[0] // 8
    return dw_ref[pl.ds(pl.multiple_of(place_ref[0] * p, 8), p), :]


def _bwd_out(l, cat, w_out, g_post, place_arr, dxn=None, y=None, x=None, target=None, deps=()):
    last = target is not None
    n_steps = S // TM

    def body(a_ref, b_ref, g_ref, cat_ref, w_ref, place_ref, *rest):
        dcat_ref, own_ref, dwb_ref, dg_ref = rest[len(deps):len(deps) + 4]
        rest = rest[len(deps) + 4:]
        acc_ref, dw_ref = rest[-2:]
        step = pl.program_id(0)

        @pl.when(step == 0)
        def _():
            dw_ref[...] = jnp.zeros_like(dw_ref)
            acc_ref[...] = jnp.zeros_like(acc_ref)

        cat = cat_ref[...]
        g = g_ref[...]
        y = jnp.dot(cat, w_ref[...], preferred_element_type=F32) if last else b_ref[...]
        r = lax.rsqrt(jnp.mean(y * y, axis=-1, keepdims=True) + EPS)
        if last:
            loss_ref, dx_ref, loss_acc_ref = rest[:3]
            err = a_ref[...] + y * r * g - b_ref[...]

            @pl.when(step == 0)
            def _():
                loss_acc_ref[...] = jnp.zeros_like(loss_acc_ref)

            loss_acc_ref[...] += _rows8(err * err)
            dz = err * (1.0 / D)
            dx_ref[...] = dz
        else:
            dz = a_ref[...]
        a = dz * g
        dy = r * a - y * (r * r * r) * jnp.mean(a * y, axis=-1, keepdims=True)
        acc_ref[...] += _rows8(dz * (y * r))
        dyb = dy.astype(BF16)
        dcat_ref[...] = lax.dot_general(dyb, w_ref[...], NT, preferred_element_type=F32)
        dw_ref[...] += lax.dot_general(cat, dyb, TN, preferred_element_type=F32)

        @pl.when(step == n_steps - 1)
        def _():
            _store_lane_rows(dg_ref, acc_ref[...])
            dwb_ref[...] = dw_ref[...].astype(BF16)
            own_ref[...] = _own_piece(dw_ref, place_ref)
            if last:
                loss_ref[...] = jnp.full((8, 128), (0.5 / D) * jnp.sum(loss_acc_ref[...]), F32)

    row = lambda: pl.BlockSpec((TM, D), lambda i: (i, 0))
    full = _whole
    return pl.pallas_call(
        body, name="out_loss_bwd" if last else "bwd_out", grid=(n_steps,),
        in_specs=[row(), row(), _layer(l, 1, D), row(), full((D, D)), pl.BlockSpec(memory_space=pltpu.SMEM)]
        + [ANY] * len(deps),
        out_specs=[row(), full((D // 8, D)), full((D, D)), full((8, 128))] + ([full((8, 128)), row()] if last else []),
        out_shape=[jax.ShapeDtypeStruct((S, D), F32), jax.ShapeDtypeStruct((D // 8, D), F32),
                   jax.ShapeDtypeStruct((D, D), BF16), jax.ShapeDtypeStruct((8, 128), F32)]
        + ([jax.ShapeDtypeStruct((8, 128), F32), jax.ShapeDtypeStruct((S, D), F32)] if last else []),
        scratch_shapes=([pltpu.VMEM((8, D), F32)] if last else []) + [pltpu.VMEM((8, D), F32), pltpu.VMEM((D, D), F32)],
        compiler_params=_params(),
    )(*((x, target) if last else (dxn, y)), g_post, cat, w_out, place_arr, *deps)


def _bwd_mix(l, pu, pg, q, kv, ag, dcat, pool_w, pool_scale, sinks, tables, deps=(), dpw_dest=None):
    bias, tri = tables
    deps = tuple(deps) + (() if dpw_dest is None else (dpw_dest,))

    def body(pu_ref, pup_ref, pg_ref, q_ref, kv_ref, kvp_ref, ag_ref, dcat_ref, pw_ref, sc_ref, sink_ref, bias_ref,
             tri_ref, *rest):
        dproj_ref, dpw_ref, dsc_ref, dsink_ref, ext_ref, dext_ref, tmp_a, tmp_b, dkv_ref = rest[len(deps):]
        tmp_refs = (tmp_a, tmp_b)
        step = pl.program_id(0)
        i = NB - 1 - step

        @pl.when(step == 0)
        def _():
            dpw_ref[...] = jnp.zeros_like(dpw_ref)
            dsc_ref[...] = jnp.zeros_like(dsc_ref)
            dsink_ref[...] = jnp.zeros_like(dsink_ref)
            for ref in (ext_ref, tmp_a, tmp_b):
                ref[0:PAD, :] = jnp.zeros((PAD, 512), F32)
            dext_ref[BLK:POOL_ROWS, :] = jnp.zeros((HALO + PAD, 512), F32)
            dkv_ref[...] = jnp.zeros_like(dkv_ref)

        ext_ref[PAD:PAD + HALO, :] = jnp.where(i > 0, pup_ref[...], 0.0)
        ext_ref[PAD + HALO:POOL_ROWS, :] = pu_ref[...]
        _window_sums(ext_ref, tmp_refs, True)
        dpooled = []
        for g, w in enumerate(POOL_WINDOWS):
            lanes = slice(g * 128, (g + 1) * 128)
            pooled, inv = _pool_block(ext_ref, tmp_refs, i, g, w)
            pooled_b = pooled.astype(BF16)
            mixed = jnp.dot(pooled_b, pw_ref[g], preferred_element_type=F32)
            scale = sc_ref[:, lanes]
            gate = pg_ref[:, lanes]
            sg = _sigmoid(gate)
            dpo = dcat_ref[:, lanes]
            dproj_ref[:, C_PG + g * 128:C_PG + (g + 1) * 128] = (
                dpo * (mixed * scale) * (sg * (1.0 + gate * (1.0 - sg)))).astype(BF16)
            dms = dpo * (gate * sg)
            dsc_ref[g:g + 1, :] += jnp.sum(dms * mixed, axis=0, keepdims=True)
            dmixed = (dms * scale).astype(BF16)
            dpw_ref[g] += lax.dot_general(pooled_b, dmixed, TN, preferred_element_type=F32)
            dpooled.append(lax.dot_general(dmixed, pw_ref[g], NT, preferred_element_type=F32))
            dext_ref[0:BLK, lanes] = dpooled[g] * inv
        _window_sums(dext_ref, tmp_refs, False)
        for g in range(len(POOL_WINDOWS)):
            lanes = slice(g * 128, (g + 1) * 128)
            dproj_ref[:, C_PU + g * 128:C_PU + (g + 1) * 128] = (tmp_refs[g % 2][0:BLK, lanes] - dpooled[g]).astype(BF16)
        dext_ref[BLK:BLK + HALO, :] = dext_ref[0:HALO, :]

        own = _own_block_mask()
        tri = tri_ref[...]
        k_var = _head_variants(kv_ref[:, 0:128], kvp_ref[:, 0:128])
        v_var = _head_variants(kv_ref[:, 128:256], kvp_ref[:, 128:256])
        q2 = [_stack_tiles(q_ref, hkv) for hkv in range(2)]
        s = _scores(q2, k_var, own)
        p, p_sink = {}, {}
        for key, s_head in s.items():
            head = _head_of(*key)
            p[key], p_sink[key] = _softmax(s_head, bias_ref[head], sink_ref[l, head])

        do2, p_b, dp = [], {}, {}
        for hkv in range(2):
            gate = _stack_tiles(ag_ref, hkv)
            sg = _sigmoid(gate)
            dca = _stack_tiles(dcat_ref, hkv, D_POOL)
            do2.append((dca * (gate * sg)).astype(BF16))
            o2 = jnp.zeros((2 * BLK, 128), F32)
            for half in range(2):
                p_b[hkv, half] = _spread_pair(p, hkv, half, tri)
                o2 = o2 + jnp.dot(p_b[hkv, half], v_var[hkv][half], preferred_element_type=F32)
                full = lax.dot_general(do2[hkv], v_var[hkv][half], NT, preferred_element_type=F32)
                for t in range(2):
                    dp[hkv, t, half] = _merge(_rows(full, t), own)
            dag = dca * o2 * (sg * (1.0 + gate * (1.0 - sg)))
            for t in range(2):
                lo = C_AG + (2 * hkv + t) * 128
                dproj_ref[:, lo:lo + 128] = _rows(dag, t).astype(BF16)

        ds = {}
        for key in p:
            delta = jnp.sum(p[key] * dp[key], axis=-1, keepdims=True)
            ds[key] = p[key] * (dp[key] - delta)
            head = _head_of(*key)
            dsink_ref[0:1, :] += jnp.where(lax.broadcasted_iota(jnp.int32, (1, 128), 1) == head,
                                           -jnp.sum(p_sink[key] * delta, axis=0, keepdims=True), 0.0)

        dk_acc = [[None, None], [None, None]]
        dv_acc = [[None, None], [None, None]]
        for hkv in range(2):
            dq2 = jnp.zeros((2 * BLK, 128), F32)
            for half in range(2):
                ds_b = _spread_pair(ds, hkv, half, tri)
                dq2 = dq2 + jnp.dot(ds_b, k_var[hkv][half], preferred_element_type=F32)
                dk_acc[hkv][half] = lax.dot_general(ds_b, q2[hkv], TN, preferred_element_type=F32)
                dv_acc[hkv][half] = lax.dot_general(p_b[hkv, half], do2[hkv], TN, preferred_element_type=F32)
            for t in range(2):
                lo = C_Q + (2 * hkv + t) * 128
                dproj_ref[:, lo:lo + 128] = (_rows(dq2, t) * 0.125).astype(BF16)

        low = lax.broadcasted_iota(jnp.int32, (2 * BLK, 128), 1) < 64

        def gather_heads(acc):
            return jnp.where(low, acc[0][0] + pltpu.roll(acc[0][1], 64, axis=1),
                             pltpu.roll(acc[1][0], 64, axis=1) + acc[1][1])

        dk = gather_heads(dk_acc) * 0.125
        dv = gather_heads(dv_acc)
        dproj_ref[:, C_K:C_V] = (dk[BLK:, :] + dkv_ref[:, 0:128]).astype(BF16)
        dproj_ref[:, C_V:C_AG] = (dv[BLK:, :] + dkv_ref[:, 128:256]).astype(BF16)
        dkv_ref[:, 0:128] = dk[:BLK, :]
        dkv_ref[:, 128:256] = dv[:BLK, :]

    rev = lambda w: pl.BlockSpec((BLK, w), lambda s: (NB - 1 - s, 0))
    prev = lambda w: pl.BlockSpec((BLK, w), lambda s: (jnp.maximum(NB - 2 - s, 0), 0))
    halo = pl.BlockSpec((HALO, 512), lambda s: (jnp.maximum((NB - 1 - s) * (BLK // HALO) - 1, 0), 0))
    return pl.pallas_call(
        body, name="bwd_mix", grid=(NB,),
        in_specs=[rev(512), halo, rev(512), rev(512), rev(256), prev(256), rev(512), rev(D),
                  _layer(l, 4, 128, 128), _layer(l, 1, 512), pl.BlockSpec(memory_space=pltpu.SMEM),
                  pl.BlockSpec((None, N_HEADS, BLK, BLK), lambda s: (jnp.minimum(NB - 1 - s, 1), 0, 0, 0)),
                  _whole((BLK, BLK))] + [ANY] * len(deps),
        out_specs=[rev(D_IN), _layer(l, 4, 128, 128),
                   pl.BlockSpec((4, 128), lambda s: (0, 0)), pl.BlockSpec((8, 128), lambda s: (0, 0))],
        out_shape=[jax.ShapeDtypeStruct((S, D_IN), BF16), jax.ShapeDtypeStruct((DEPTH, 4, 128, 128), F32),
                   jax.ShapeDtypeStruct((4, 128), F32), jax.ShapeDtypeStruct((8, 128), F32)],
        input_output_aliases={} if dpw_dest is None else {12 + len(deps): 1},
        scratch_shapes=[pltpu.VMEM((POOL_ROWS, 512), F32)] * 4 + [pltpu.VMEM((BLK, 256), F32)],
        compiler_params=_params(),
    )(pu, pu, pg, q, kv, kv, ag, dcat, pool_w, pool_scale, sinks, bias, tri, *deps)


def _bwd_in_dw(l, dproj, x, g_pre, place_arr, deps=()):
    n_steps = S // TM

    def body(dp_ref, x_ref, g_ref, place_ref, *rest):
        own_ref, dwb_ref, dw_ref = rest[len(deps):]
        step = pl.program_id(0)

        @pl.when(step == 0)
        def _():
            dw_ref[...] = jnp.zeros_like(dw_ref)

        xt = x_ref[...]
        r = lax.rsqrt(jnp.mean(xt * xt, axis=-1, keepdims=True) + EPS)
        h = (xt * r * g_ref[...]).astype(BF16)
        dw_ref[...] += lax.dot_general(dp_ref[...], h, TN, preferred_element_type=F32)

        @pl.when(step == n_steps - 1)
        def _():
            dwb_ref[...] = dw_ref[...].astype(BF16)
            own_ref[...] = _own_piece(dw_ref, place_ref)

    row = lambda w: pl.BlockSpec((TM, w), lambda i: (i, 0))
    full = _whole
    return pl.pallas_call(
        body, name="bwd_in_dw", grid=(n_steps,),
        in_specs=[row(D_IN), row(D), _layer(l, 1, D), pl.BlockSpec(memory_space=pltpu.SMEM)] + [ANY] * len(deps),
        out_specs=[full((D_IN // 8, D)), full((D_IN, D))],
        out_shape=[jax.ShapeDtypeStruct((D_IN // 8, D), F32), jax.ShapeDtypeStruct((D_IN, D), BF16)],
        scratch_shapes=[pltpu.VMEM((D_IN, D), F32)],
        compiler_params=_params(),
    )(dproj, x, g_pre, place_arr, *deps)


def _bwd_in_dx(l, dproj, w_in_t, x, g_pre, dres, deps=(), dw_place=None):
    n_steps = S // TM
    with_dw = dw_place is not None

    def body(dp_ref, w_ref, x_ref, g_ref, dres_ref, *rest):
        place_ref = rest[0] if with_dw else None
        rest = rest[with_dw + len(deps):]
        if with_dw:
            dx_ref, dg_ref, own_ref, dwb_ref, acc_ref, dw_ref = rest
        else:
            dx_ref, dg_ref, acc_ref = rest
        step = pl.program_id(0)

        @pl.when(step == 0)
        def _():
            acc_ref[...] = jnp.zeros_like(acc_ref)
            if with_dw:
                dw_ref[...] = jnp.zeros_like(dw_ref)

        g = g_ref[...]
        halves = [slice(k * (TM // 2), (k + 1) * (TM // 2)) for k in range(2)]
        dh = [jnp.dot(dp_ref[rows, :], w_ref[...], preferred_element_type=F32) for rows in halves]
        h = []
        for rows, dh_k in zip(halves, dh):
            xt = x_ref[rows, :]
            r = lax.rsqrt(jnp.mean(xt * xt, axis=-1, keepdims=True) + EPS)
            xn = xt * r
            acc_ref[...] += _rows8(dh_k * xn)
            a = dh_k * g
            dx_ref[rows, :] = dres_ref[rows, :] + (
                r * a - xt * (r * r * r) * jnp.mean(a * xt, axis=-1, keepdims=True))
            h.append((xn * g).astype(BF16))
        if with_dw:
            dw_ref[...] += lax.dot_general(dp_ref[...], jnp.concatenate(h, axis=0), TN, preferred_element_type=F32)

        @pl.when(step == n_steps - 1)
        def _():
            _store_lane_rows(dg_ref, acc_ref[...])
            if with_dw:
                dwb_ref[...] = dw_ref[...].astype(BF16)
                own_ref[...] = _own_piece(dw_ref, place_ref)

    row = lambda w: pl.BlockSpec((TM, w), lambda i: (i, 0))
    full = _whole
    dw_specs = [full((D_IN // 8, D)), full((D_IN, D))] if with_dw else []
    dw_shapes = [jax.ShapeDtypeStruct((D_IN // 8, D), F32), jax.ShapeDtypeStruct((D_IN, D), BF16)] if with_dw else []
    return pl.pallas_call(
        body, name="bwd_in" if with_dw else "bwd_in_dx", grid=(n_steps,),
        in_specs=[row(D_IN), full((D_IN, D)), row(D), _layer(l, 1, D), row(D)]
        + [pl.BlockSpec(memory_space=pltpu.SMEM)] * with_dw + [ANY] * len(deps),
        out_specs=[row(D), full((8, 128))] + dw_specs,
        out_shape=[jax.ShapeDtypeStruct((S, D), F32), jax.ShapeDtypeStruct((8, 128), F32)] + dw_shapes,
        scratch_shapes=[pltpu.VMEM((8, D), F32)] + [pltpu.VMEM((D_IN, D), F32)] * with_dw,
        compiler_params=_params(),
    )(dproj, w_in_t, x, g_pre, dres, *((dw_place,) if with_dw else ()), *deps)


HBM =pl.BlockSpec(memory_space=pltpu.HBM)
SEM = pl.BlockSpec(memory_space=pltpu.SEMAPHORE)
def _split_copy(collective_id=None):
    return pltpu.CompilerParams(has_side_effects=pltpu.SideEffectType.DATAFLOW_SIDE_EFFECTING,
                                collective_id=collective_id)


SPLIT_COPY = _split_copy()


def _in_hbm(a):
    return pltpu.with_memory_space_constraint(a, pltpu.HBM)

def _place():
    return lax.axis_index("x"), lax.axis_index("y"), lax.axis_index("c")


def _other_chips(x, y):
    return [(1 - x, y), (x, 1 - y), (1 - x, 1 - y)]


def _peer(x, y, c, m):
    return (x ^ (m >> 2), y ^ ((m >> 1) & 1), c ^ (m & 1))


SAME_CORE = (2, 4, 6)


def _place_cast(name, src, chip_arr, tile, layers, deps=()):
    _, n, cols = src.shape
    steps = n // tile
    k = len(layers)

    def body(chip_ref, *refs):
        for s_ref, o_ref in zip(refs[:k], refs[k + len(deps):]):
            o_ref[...] = s_ref[...].astype(BF16)

    def layer_spec(l):
        return pl.BlockSpec((None, tile, cols), lambda i, chip: (l, i, 0))

    return pl.pallas_call(
        body, name=name,
        grid_spec=pltpu.PrefetchScalarGridSpec(
            num_scalar_prefetch=1, grid=(steps,),
            in_specs=[layer_spec(l) for l in layers] + [ANY] * len(deps),
            out_specs=[pl.BlockSpec((tile, cols), lambda i, chip: (chip[0] * steps + i, 0))] * k),
        out_shape=[jax.ShapeDtypeStruct((N_SHARDS * n, cols), BF16)] * k,
        compiler_params=_params(),
    )(chip_arr, *[src] * k, *deps)


def _place_other_half(name, src, place_arr, layer, dest):
    _, n, cols = src.shape

    def body(place_ref, s_ref, dest_ref, o_ref):
        o_ref[...] = s_ref[...].astype(BF16)

    return pl.pallas_call(
        body, name=name,
        grid_spec=pltpu.PrefetchScalarGridSpec(
            num_scalar_prefetch=1, grid=(1,),
            in_specs=[pl.BlockSpec((None, n // 2, cols), lambda i, place: (layer, 1 - place[1], 0)), ANY],
            out_specs=pl.BlockSpec((n // 2, cols), lambda i, place: (place[0] + 1 - 2 * place[1], 0))),
        out_shape=jax.ShapeDtypeStruct((N_SHARDS * n, cols), BF16),
        input_output_aliases={2: 0},
        compiler_params=_params(),
    )(place_arr, src, dest)


def _chip_rows(ref, chip, half=None):
    n = ref.shape[0] // N_SHARDS
    if half is None:
        return ref.at[pl.ds(pl.multiple_of(chip * n, 16), n), :]
    return ref.at[pl.ds(pl.multiple_of(chip * n + half * (n // 2), 16), n // 2), :]


def _gather_start(name, bufs, halved, collective_id):
    n = len(bufs)

    def body(*refs):
        ins, send, recv, token = refs[:n], refs[n:2 * n], refs[2 * n:3 * n], refs[-1]
        x, y, c = _place()
        _handshake([(*chip, c) for chip in _other_chips(x, y)])
        for a, buf in enumerate(ins):
            own = _chip_rows(buf, 2 * x + y, c if a in halved else None)
            for j, chip in enumerate(_other_chips(x, y)):
                pltpu.make_async_remote_copy(src_ref=own, dst_ref=own, send_sem=send[a].at[j], recv_sem=recv[a].at[j],
                                             device_id=(*chip, c), device_id_type=MESH).start()
        token[...] = jnp.zeros_like(token)

    outs = pl.pallas_call(
        body, name=name, in_specs=[HBM] * n,
        out_specs=[SEM] * (2 * n) + [HBM] * n + [pl.BlockSpec(memory_space=pltpu.VMEM)],
        out_shape=[pltpu.SemaphoreType.DMA((3,))] * (2 * n) + [pltpu.HBM(b.shape, b.dtype) for b in bufs]
        + [jax.ShapeDtypeStruct((8, 128), F32)],
        input_output_aliases={a: 2 * n + a for a in range(n)},
        compiler_params=_split_copy(collective_id),
    )(*[_in_hbm(b) for b in bufs])
    return outs[:n], outs[n:2 * n], outs[2 * n:3 * n], outs[-1]


def _gather_start_cast(name, src, layer, collective_id):
    _, n, cols = src.shape
    half = n // 2

    def body(src_ref, send, recv, buf_ref, token, f32_ref, bf16_ref, local):
        x, y, c = _place()
        own = _chip_rows(buf_ref, 2 * x + y, c)

        def cast():
            load = pltpu.make_async_copy(src_ref.at[layer, pl.ds(pl.multiple_of(c * half, 16), half), :], f32_ref,
                                         local.at[0])
            load.start()
            load.wait()
            bf16_ref[...] = f32_ref[...].astype(BF16)
            store = pltpu.make_async_copy(bf16_ref, own, local.at[1])
            store.start()
            store.wait()

        _handshake([(*chip, c) for chip in _other_chips(x, y)], cast)
        for j, chip in enumerate(_other_chips(x, y)):
            pltpu.make_async_remote_copy(src_ref=own, dst_ref=own, send_sem=send.at[j], recv_sem=recv.at[j],
                                         device_id=(*chip, c), device_id_type=MESH).start()
        token[...] = jnp.zeros_like(token)

    outs = pl.pallas_call(
        body, name=name, in_specs=[HBM],
        out_specs=[SEM, SEM, HBM, pl.BlockSpec(memory_space=pltpu.VMEM)],
        out_shape=[pltpu.SemaphoreType.DMA((3,))] * 2 + [pltpu.HBM((N_SHARDS * n, cols), BF16),
                                                         jax.ShapeDtypeStruct((8, 128), F32)],
        scratch_shapes=[pltpu.VMEM((half, cols), F32), pltpu.VMEM((half, cols), BF16), pltpu.SemaphoreType.DMA((2,))],
        compiler_params=_split_copy(collective_id),
    )(_in_hbm(src))
    return outs[:1], outs[1:2], outs[2:3], outs[3]


def _gather_wait(name, buf, send_sem, recv_sem, after, halved=False):
    def body(buf_ref, send_ref, recv_ref, *rest):
        x, y, c = _place()
        half = c if halved else None
        own = _chip_rows(buf_ref, 2 * x + y, half)
        for j, chip in enumerate(_other_chips(x, y)):
            copy = pltpu.make_async_remote_copy(src_ref=own, dst_ref=_chip_rows(buf_ref, 2 * chip[0] + chip[1], half),
                                                send_sem=send_ref.at[j], recv_sem=recv_ref.at[j],
                                                device_id=(*chip, c), device_id_type=MESH)
            copy.wait_send()
            copy.wait_recv()

    return pl.pallas_call(
        body, name=name, in_specs=[HBM, SEM, SEM] + [ANY] * len(after), out_specs=HBM,
        out_shape=pltpu.HBM(buf.shape, buf.dtype), input_output_aliases={0: 0}, compiler_params=SPLIT_COPY,
    )(buf, send_sem, recv_sem, *after)


def _handshake(peers, meanwhile=None):
    barrier = pltpu.get_barrier_semaphore()
    for peer in peers:
        pl.semaphore_signal(barrier, inc=1, device_id=peer, device_id_type=MESH)
    if meanwhile is not None:
        meanwhile()
    pl.semaphore_wait(barrier, len(peers))


def _sibling_handshake(x, y, c):
    _handshake([(x, y, 1 - c)])


def _forward_halves(name, bufs, collective_id):
    n = len(bufs)

    def body(*refs):
        ins, outs, (send_sems, recv_sems) = refs[:n], refs[n:2 * n], refs[2 * n:]
        x, y, c = _place()
        _sibling_handshake(x, y, c)

        def copy(a, j, chip, half):
            rows = 2 * chip[0] + chip[1]
            return pltpu.make_async_remote_copy(
                src_ref=_chip_rows(ins[a], rows, half), dst_ref=_chip_rows(outs[a], rows, half),
                send_sem=send_sems.at[3 * a + j], recv_sem=recv_sems.at[3 * a + j], device_id=(x, y, 1 - c),
                device_id_type=MESH)

        copies = [(a, j, chip) for a in range(n) for j, chip in enumerate(_other_chips(x, y))]
        for a, j, chip in copies:
            copy(a, j, chip, c).start()
        for a, j, chip in copies:
            copy(a, j, chip, c).wait_send()
            copy(a, j, chip, 1 - c).wait_recv()

    return pl.pallas_call(
        body, name=name, in_specs=[ANY] * n, out_specs=[ANY] * n,
        out_shape=[jax.ShapeDtypeStruct(b.shape, b.dtype) for b in bufs],
        input_output_aliases={a: a for a in range(n)},
        scratch_shapes=[pltpu.SemaphoreType.DMA((3 * n,))] * 2,
        compiler_params=pltpu.CompilerParams(collective_id=collective_id),
    )(*bufs)


def _piece_rows(ref, k):
    p = ref.shape[0] // 8
    return ref.at[pl.ds(pl.multiple_of(k * p, 32 // jnp.dtype(ref.dtype).itemsize), p), :]


def _exchange_start(name, arrays, collective_id):
    n = len(arrays)
    zones = [lax.empty((7, a.shape[0] // 8, a.shape[1]), a.dtype) for a in arrays]

    def body(*refs):
        srcs, lands = refs[:n], refs[n:2 * n]
        send, recv, token = refs[2 * n:3 * n], refs[3 * n:4 * n], refs[-1]
        x, y, c = _place()
        _handshake([_peer(x, y, c, m) for m in range(1, 8)])
        for a, (src, land) in enumerate(zip(srcs, lands)):
            for m in range(1, 8):
                px, py, pc = _peer(x, y, c, m)
                pltpu.make_async_remote_copy(
                    src_ref=_piece_rows(src, 4 * px + 2 * py + pc), dst_ref=land.at[m - 1], send_sem=send[a].at[m - 1],
                    recv_sem=recv[a].at[m - 1], device_id=(px, py, pc), device_id_type=MESH).start()
        token[...] = jnp.zeros_like(token)

    outs = pl.pallas_call(
        body, name=name, in_specs=[HBM] * (2 * n),
        out_specs=[SEM] * (2 * n) + [HBM] * (2 * n) + [pl.BlockSpec(memory_space=pltpu.VMEM)],
        out_shape=[pltpu.SemaphoreType.DMA((7,))] * (2 * n) + [pltpu.HBM(a.shape, a.dtype) for a in arrays + zones]
        + [jax.ShapeDtypeStruct((8, 128), F32)],
        input_output_aliases={a: 2 * n + a for a in range(2 * n)},
        compiler_params=_split_copy(collective_id),
    )(*[_in_hbm(a) for a in arrays + zones])
    return outs[:n], outs[n:2 * n], outs[2 * n:3 * n], outs[3 * n:4 * n], outs[-1]


def _exchange_wait(name, started, after, which=None, with_sent=False):
    which = range(len(started[2])) if which is None else which
    send_sems, recv_sems, arrays, zones = [[group[k] for k in which] for group in started[:4]]
    n = len(arrays)

    def body(*refs):
        srcs, lands = refs[:n], refs[n:2 * n]
        send, recv = refs[2 * n:3 * n], refs[3 * n:4 * n]
        x, y, c = _place()
        for a, (src, land) in enumerate(zip(srcs, lands)):
            for m in range(1, 8):
                px, py, pc = _peer(x, y, c, m)
                copy = pltpu.make_async_remote_copy(
                    src_ref=_piece_rows(src, 4 * px + 2 * py + pc), dst_ref=land.at[m - 1], send_sem=send[a].at[m - 1],
                    recv_sem=recv[a].at[m - 1], device_id=(px, py, pc), device_id_type=MESH)
                copy.wait_send()
                copy.wait_recv()

    outs = pl.pallas_call(
        body, name=name, in_specs=[HBM] * (2 * n) + [SEM] * (2 * n) + [ANY], out_specs=[HBM] * (2 * n),
        out_shape=[pltpu.HBM(a.shape, a.dtype) for a in list(arrays) + list(zones)],
        input_output_aliases={a: a for a in range(2 * n)}, compiler_params=SPLIT_COPY,
    )(*arrays, *zones, *send_sems, *recv_sems, after)
    return outs if with_sent else outs[n:]


def _sum_pieces(name, weights, place_arr, dests=None):
    steps = 2
    flat = [item for items in weights for item in items]
    n = len(flat)

    def body(place_ref, *refs):
        outs = iter(refs[len(refs) - len(weights):])
        k = 0
        for items in weights:
            out_ref = next(outs)
            for layer, _, _ in items:
                total = refs[k][...]
                for m in range(7):
                    total = total + refs[n + k][m].astype(F32)
                if len(items) == DEPTH:
                    out_ref[layer] = total
                else:
                    out_ref[...] = total
                k += 1

    def out_spec(items):
        _, own, _ = items[0]
        t, cols = own.shape[0] // steps, own.shape[1]
        if len(items) == DEPTH:
            return pl.BlockSpec((DEPTH, t, cols), lambda i, place: (0, place[1] * steps + i, 0))
        layer = items[0][0]
        return pl.BlockSpec((None, t, cols), lambda i, place: (layer, place[1] * steps + i, 0))

    owns = [own for _, own, _ in flat]
    dests = [] if dests is None else list(dests)
    return pl.pallas_call(
        body, name=name,
        grid_spec=pltpu.PrefetchScalarGridSpec(
            num_scalar_prefetch=1, grid=(steps,),
            in_specs=[pl.BlockSpec((o.shape[0] // steps, o.shape[1]), lambda i, place: (i, 0)) for o in owns]
            + [pl.BlockSpec((7, o.shape[0] // steps, o.shape[1]), lambda i, place: (0, i, 0)) for o in owns]
            + [ANY] * len(dests),
            out_specs=[out_spec(items) for items in weights]),
        out_shape=[jax.ShapeDtypeStruct((DEPTH, 2 * items[0][1].shape[0], items[0][1].shape[1]), F32)
                   for items in weights],
        input_output_aliases={1 + 2 * n + k: k for k in range(len(dests))},
        compiler_params=_params(),
    )(place_arr, *owns, *[recv for _, _, recv in flat], *dests)


def _sum_small(name, partials, recvs, place_arr):
    n = len(partials)

    def body(place_ref, *refs):
        for o_ref, r_ref, out_ref in zip(refs[:n], refs[n:2 * n], refs[2 * n:]):
            total = o_ref[...]
            for m in range(7):
                total = total + r_ref[m]
            out_ref[...] = total

    piece = lambda a: pl.BlockSpec((a.shape[0] // 8, a.shape[1]), lambda i, place: (place[0], 0))
    return pl.pallas_call(
        body, name=name,
        grid_spec=pltpu.PrefetchScalarGridSpec(
            num_scalar_prefetch=1, grid=(1,),
            in_specs=[piece(a) for a in partials] + [pl.BlockSpec(r.shape, lambda i, place: (0, 0, 0)) for r in recvs],
            out_specs=[piece(a) for a in partials]),
        out_shape=[jax.ShapeDtypeStruct(a.shape, F32) for a in partials],
        compiler_params=_params(),
    )(place_arr, *partials, *recvs)


def _share(name, bufs, parts, gathered=(), collective_id=None, summed=()):
    n, n_g, n_s = len(bufs), len(gathered), len(summed)
    total = n + n_g
    made = [target for target, _, _ in summed]

    def body(*refs):
        ins, extra, outs = refs[:total], refs[total:total + 2 * n_s], refs[total + 2 * n_s:2 * total + 2 * n_s]
        send_sems, recv_sems, send_g, recv_g = refs[2 * total + 2 * n_s:2 * total + 2 * n_s + 4]
        scratch = refs[2 * total + 2 * n_s + 4:]
        x, y, c = _place()

        def half(ref, l, which):
            p = ref.shape[1] // 2
            return ref.at[l, pl.ds(pl.multiple_of(which * p, 8), p), :]

        def sums():
            local, acc, got = scratch[0], scratch[1:1 + n_s], scratch[1 + n_s:]
            loads, stores = [], []
            for j, (target, _, _) in enumerate(summed):
                own_rows = extra[2 * j] if target[0] == "half" else _piece_rows(extra[2 * j], 4 * x + 2 * y + c)
                loads += [pltpu.make_async_copy(own_rows, acc[j], local.at[2 * j]),
                          pltpu.make_async_copy(extra[2 * j + 1], got[j], local.at[2 * j + 1])]
            for load in loads:
                load.start()
            for j, (target, _, _) in enumerate(summed):
                loads[2 * j].wait()
                loads[2 * j + 1].wait()
                rows = acc[j].shape[0]
                step = min(rows, 96)
                for r in range(0, rows, step):
                    part = acc[j][r:r + step, :]
                    for m in range(7):
                        part = part + got[j][m, r:r + step, :].astype(F32)
                    acc[j][r:r + step, :] = part
                if target[0] == "half":
                    dest = half(outs[target[1]], target[2], c)
                else:
                    dest = _piece_rows(outs[n + target[1]], 4 * x + 2 * y + c)
                stores.append(pltpu.make_async_copy(acc[j], dest, local.at[2 * j]))
                stores[-1].start()
            for store in stores:
                store.wait()

        _handshake([_peer(x, y, c, m) for m in (SAME_CORE if gathered else ()) + (1,)], sums if summed else None)

        def swap(k, which):
            a, l = parts[k]
            held = outs if ("half", a, l) in made else ins
            return pltpu.make_async_remote_copy(
                src_ref=half(held[a], l, which), dst_ref=half(outs[a], l, which), send_sem=send_sems.at[k],
                recv_sem=recv_sems.at[k], device_id=(x, y, 1 - c), device_id_type=MESH)

        def spread(a, m, sender, held, to):
            k = 4 * sender[0] + 2 * sender[1] + sender[2]
            return pltpu.make_async_remote_copy(
                src_ref=_piece_rows(held[n + a], k), dst_ref=_piece_rows(outs[n + a], k),
                send_sem=send_g.at[7 * a + m - 1], recv_sem=recv_g.at[7 * a + m - 1], device_id=to, device_id_type=MESH)

        me, sibling = (x, y, c), (x, y, 1 - c)
        for k in range(len(parts)):
            swap(k, c).start()
        def own(a, m):
            return spread(a, m, me, outs if ("piece", a) in made else ins, _peer(x, y, c, m))

        def handed_on(a, m):
            return spread(a, m + 1, _peer(x, y, c, m), outs, sibling)

        for a in range(n_g):
            for m in SAME_CORE + (1,):
                own(a, m).start()
        for a in range(n_g):
            for m in SAME_CORE:
                spread(a, m, _peer(x, y, c, m), ins, _peer(x, y, c, m)).wait_recv()
                handed_on(a, m).start()
        for k in range(len(parts)):
            swap(k, c).wait_send()
            swap(k, 1 - c).wait_recv()
        for a in range(n_g):
            for m in SAME_CORE + (1,):
                own(a, m).wait_send()
            for m in SAME_CORE:
                handed_on(a, m).wait_send()
                spread(a, m + 1, _peer(x, y, c, m + 1), ins, sibling).wait_recv()
            spread(a, 1, sibling, ins, sibling).wait_recv()

    arrays = list(bufs) + list(gathered)
    sum_scratch = []
    if summed:
        sum_scratch = [pltpu.SemaphoreType.DMA((2 * n_s,))] + [pltpu.VMEM(recv.shape[1:], F32) for _, _, recv in summed]
        sum_scratch += [pltpu.VMEM(recv.shape, recv.dtype) for _, _, recv in summed]
    return pl.pallas_call(
        body, name=name, in_specs=[ANY] * (total + 2 * n_s), out_specs=[ANY] * total,
        out_shape=[jax.ShapeDtypeStruct(b.shape, F32) for b in arrays],
        input_output_aliases={a: a for a in range(total)},
        scratch_shapes=[pltpu.SemaphoreType.DMA((max(len(parts), 1),))] * 2
        + [pltpu.SemaphoreType.DMA((max(7 * n_g, 1),))] * 2 + sum_scratch,
        compiler_params=pltpu.CompilerParams(collective_id=collective_id, vmem_limit_bytes=VMEM_LIMIT),
    )(*arrays, *[array for _, own, recv in summed for array in (own, recv)])


def _adamw_math(w, g, m, v):
    nm = ADAM_B1 * m + (1.0 - ADAM_B1) * g
    nv = ADAM_B2 * v + (1.0 - ADAM_B2) * (g * g)
    m_hat = nm / (1.0 - ADAM_B1 ** ADAM_STEP)
    v_hat = nv / (1.0 - ADAM_B2 ** ADAM_STEP)
    return -ADAM_LR * (m_hat / (jnp.sqrt(v_hat) + ADAM_EPS) + ADAM_WD * w), nm, nv


def _adamw(name, w, g, m, v, rows_per_step, first=0, count=None, dests=None, deps=(), small=()):
    layers, rows, cols = w.shape
    count = layers if count is None else count
    dests = () if dests is None else tuple(dests)
    n_in = 4 + len(dests) + len(deps)
    small_shapes = _small_shapes(small[4].shape) if small else []

    def body(*refs):
        w_ref, g_ref, m_ref, v_ref = refs[:4]
        d_ref, nm_ref, nv_ref, g_out_ref = refs[n_in + len(small):n_in + len(small) + 4]
        d_ref[...], nm_ref[...], nv_ref[...] = _adamw_math(w_ref[...], g_ref[...], m_ref[...], v_ref[...])
        g_out_ref[...] = g_ref[...]
        if small:
            @pl.when((pl.program_id(0) == 0) & (pl.program_id(1) == 0))
            def _():
                _adamw_small(*refs[n_in:n_in + len(small)], *refs[n_in + len(small) + 4:])

    spec = pl.BlockSpec((1, rows_per_step, cols), lambda l, i: (first + l, i, 0))
    whole = lambda shape: pl.BlockSpec(shape, lambda l, i: (0,) * len(shape))
    shape = jax.ShapeDtypeStruct(w.shape, F32)
    return pl.pallas_call(
        body, name=name, grid=(count, rows // rows_per_step),
        in_specs=[spec] * 4 + [ANY] * (len(dests) + len(deps)) + [whole(a.shape) for a in small],
        out_specs=[spec] * 4 + [whole(s) for s in small_shapes],
        out_shape=[shape] * 4 + [jax.ShapeDtypeStruct(s, F32) for s in small_shapes],
        input_output_aliases={4 + k: k for k in range(len(dests))},
        scratch_shapes=[pltpu.VMEM((MISC_ROWS, 128), F32)] * 3 if small else [],
        compiler_params=_params(("arbitrary", "arbitrary")),
    )(w, g, m, v, *dests, *deps, *small)


def _pack_misc(pool_scale, sinks, norm_pre, norm_post):
    sink_rows = jnp.zeros((DEPTH, 8, 128), F32).at[:, 0, 0:N_HEADS].set(sinks).reshape(2 * 8, 128)
    return jnp.concatenate([pool_scale.reshape(8, 128), norm_pre.reshape(16, 128), norm_post.reshape(16, 128),
                            sink_rows, jnp.zeros((8, 128), F32)], axis=0)


def _adamw_small(w_ref, g_ref, m_ref, v_ref, pw_ref, pg_ref, pm_ref, pv_ref, *rest):
    outs, pool_outs, (d_ref, nm_ref, nv_ref) = rest[:17], rest[17:21], rest[21:]
    pool_outs[0][...] = pg_ref[...]
    pool_outs[1][...], pool_outs[2][...], pool_outs[3][...] = _adamw_math(
        pw_ref[...], pg_ref[...], pm_ref[...], pv_ref[...])
    d_ref[...], nm_ref[...], nv_ref[...] = _adamw_math(w_ref[...], g_ref[...], m_ref[...], v_ref[...])
    for k, src in enumerate([g_ref, d_ref, nm_ref, nv_ref]):
        scale, sinks, pre, post = outs[4 * k:4 * k + 4]
        for l in range(DEPTH):
            for j in range(4):
                scale[l:l + 1, j * 128:(j + 1) * 128] = src[MISC_SCALE + 4 * l + j:MISC_SCALE + 4 * l + j + 1, :]
            for j in range(8):
                pre[l:l + 1, j * 128:(j + 1) * 128] = src[MISC_PRE + 8 * l + j:MISC_PRE + 8 * l + j + 1, :]
                post[l:l + 1, j * 128:(j + 1) * 128] = src[MISC_POST + 8 * l + j:MISC_POST + 8 * l + j + 1, :]
            sinks[l:l + 1, :] = src[MISC_SINKS + 8 * l:MISC_SINKS + 8 * l + 1, 0:N_HEADS]
    outs[16][...] = g_ref[MISC_LOSS:MISC_LOSS + 1, 0:1]


def _small_shapes(pool_shape):
    return [(DEPTH, D_POOL), (DEPTH, N_HEADS), (DEPTH, D), (DEPTH, D)] * 4 + [(1, 1)] + [pool_shape] * 4


def kernel(x, w_in, pool_w, pool_scale, attn_sinks, w_out, norm_pre, norm_post, loss_target, m_w_in, m_pool_w, m_pool_scale, m_attn_sinks, m_w_out, m_norm_pre, m_norm_post, v_w_in, v_pool_w, v_pool_scale, v_attn_sinks, v_w_out, v_norm_pre, v_norm_post):
    cx, cy, cc = _place()
    chip_arr = jnp.reshape(2 * cx + cy, (1,)).astype(jnp.int32)
    place_arr = jnp.stack([4 * cx + 2 * cy + cc, cc]).astype(jnp.int32)
    t = lambda a: jnp.transpose(a, (0, 2, 1))
    w_in_t = t(w_in)
    xs, target = x[0], loss_target[0]
    pool_w_b = pool_w.astype(BF16)
    tables = _attention_tables()
    scale3 = pool_scale.reshape(DEPTH, 1, D_POOL)
    pre3 = norm_pre.reshape(DEPTH, 1, D)
    post3 = norm_post.reshape(DEPTH, 1, D)

    first = _gather_start_cast("gather_start_first", w_in_t, 0, ID_GATHER_FIRST)
    wi0 = _place_other_half("place_w_in0_rest", w_in_t, place_arr, 0, first[2][0])
    (wi1,) = _place_cast("place_w_in1", w_in_t, chip_arr, 288, [1], deps=(first[3],))
    wo = _place_cast("place_w_out", w_out, chip_arr, 256, [0, 1], deps=(first[3],))
    rest = _gather_start("gather_start_rest", [wi1, wo[0], wo[1]], halved=(0, 1), collective_id=ID_GATHER_REST)
    send, recv, bufs = [first[k] + rest[k] for k in range(3)]
    bufs = [wi0, *bufs[1:]]
    order = {(0, "in"): 0, (1, "in"): 1, (0, "out"): 2, (1, "out"): 3}

    saved = []
    packed = [_pack_misc(pool_scale, attn_sinks, norm_pre, norm_post),
              _pack_misc(m_pool_scale, m_attn_sinks, m_norm_pre, m_norm_post),
              _pack_misc(v_pool_scale, v_attn_sinks, v_norm_pre, v_norm_post)]
    after = (first[3], rest[3], pool_w_b, *tables, scale3, pre3, post3, *packed)
    below = None
    for l in range(DEPTH):
        k = order[l, "in"]
        halves = [_gather_wait(f"gather_wait_in{l}", bufs[k], send[k], recv[k], after, halved=True)]
        if below is not None:
            halves.append(below[1])
        w_in_l, *w_out_below = _forward_halves(f"forward_w{l}", halves, collective_id=ID_FORWARD[l])
        if below is None:
            pu, pg, q, kv, ag = _fwd_in(l, xs, pre3, w_in_l)
        else:
            saved[l - 1][9] = w_out_below[0]
            y, xs, pu, pg, q, kv, ag = _fwd_in(l, xs, pre3, w_in_l, (below[0], w_out_below[0], below[2]))
            saved[l - 1][7] = y
        cat = _fwd_mix(l, pu, pg, q, kv, ag, pool_w_b, scale3, attn_sinks, tables)
        k = order[l, "out"]
        w_out_l = _gather_wait(f"gather_wait_out{l}", bufs[k], send[k], recv[k], (cat,), halved=l + 1 < DEPTH)
        saved.append([xs, pu, pg, q, kv, ag, cat, None, w_in_l, w_out_l])
        below, after = (cat, w_out_l, post3), (w_out_l,)

    x_in, pu, pg, q, kv, ag, cat, y, w_in_l, w_out_l = saved[1]
    dcat, dw_out1, dw_out1_b, dg_post1, loss, xs = _bwd_out(1, cat, w_out_l, post3, place_arr, x=x_in, target=target)
    ex1_out = _exchange_start("exchange_start_out1", [dw_out1_b], ID_OUT1)
    dproj, dpw, dsc1, dsink1 = _bwd_mix(1, pu, pg, q, kv, ag, dcat, pool_w_b, scale3, attn_sinks, tables,
                                        deps=(ex1_out[4],))
    dx, dg_pre1, dw_in1, dw_in1_b = _bwd_in_dx(1, dproj, w_in_l, x_in, pre3, xs, dw_place=place_arr)
    ex1_in = _exchange_start("exchange_start_in1", [dw_in1_b], ID_IN1)

    x_in, pu, pg, q, kv, ag, cat, y, w_in_l, w_out_l = saved[0]
    dcat, dw_out0, dw_out0_b, dg_post0 = _bwd_out(0, cat, w_out_l, post3, place_arr, dxn=dx, y=y, deps=(ex1_in[4],))
    ex0_out = _exchange_start("exchange_start_out0", [dw_out0_b], ID_OUT0)
    dproj, dpw, dsc0, dsink0 = _bwd_mix(0, pu, pg, q, kv, ag, dcat, pool_w_b, scale3, attn_sinks, tables,
                                        deps=(ex0_out[4],), dpw_dest=dpw)
    dw_in0, dw_in0_b = _bwd_in_dw(0, dproj, x_in, pre3, place_arr)
    flat = lambda a: a.reshape(DEPTH * 4 * 128, 128)
    ex0_in = _exchange_start("exchange_start_in0", [flat(dpw), dw_in0_b], ID_IN0)

    grad_x, dg_pre0 = _bwd_in_dx(0, dproj, w_in_l, x_in, pre3, dx, deps=(ex0_in[4],))
    small = [jnp.concatenate([dsc0, dsc1, dg_pre0, dg_pre1, dg_post0, dg_post1, dsink0, dsink1, loss], axis=0)]
    ex_small = _exchange_start("exchange_start_small", small, ID_SMALL)
    (recv_out1,) = _exchange_wait("exchange_wait_out1", ex1_out, ex_small[4])
    (recv_in1,) = _exchange_wait("exchange_wait_in1", ex1_in, recv_out1)
    g_in, g_out = _sum_pieces("sum_pieces_1", [[(1, dw_in1, recv_in1)], [(1, dw_out1, recv_out1)]], place_arr)
    (recv_out0,) = _exchange_wait("exchange_wait_out0", ex0_out, g_out)
    (g_out,) = _sum_pieces("sum_pieces_out0", [[(0, dw_out0, recv_out0)]], place_arr, dests=[g_out])
    g_in, g_out = _share("share_a", [g_in, g_out], [(0, 1), (1, 0), (1, 1)], collective_id=ID_SHARE_A)
    m_in_t, v_in_t = t(m_w_in), t(v_w_in)
    d_out, nm_out, nv_out, grad_w_out = _adamw("adamw_w_out", w_out, g_out, m_w_out, v_w_out, 256)
    upd_in = _adamw("adamw_w_in1", w_in_t, g_in, m_in_t, v_in_t, 288, first=1, count=1, deps=(d_out,))
    dpw_own, recv_pw = _exchange_wait("exchange_wait_pool", ex0_in, upd_in[0], which=[0], with_sent=True)
    (g_pw,) = _sum_small("sum_pool", [dpw_own], [recv_pw], place_arr)

    (recv_in0,) = _exchange_wait("exchange_wait_in0", ex0_in, g_pw, which=[1])
    (recv_misc,) = _exchange_wait("exchange_wait_small", ex_small, recv_in0)
    g_in, g_pw, g_misc = _share(
        "share_b", [g_in], [(0, 0)], [g_pw, lax.empty(small[0].shape, F32)], collective_id=ID_SHARE_B,
        summed=[(("half", 0, 0), dw_in0, recv_in0), (("piece", 1), small[0], recv_misc)])
    d_in, nm_in, nv_in, grad_w_in_t, *small_out = _adamw(
        "adamw_w_in0", w_in_t, g_in, m_in_t, v_in_t, 288, first=0, count=1, dests=upd_in,
        small=(packed[0], g_misc, packed[1], packed[2], flat(pool_w), g_pw, flat(m_pool_w), flat(v_pool_w)))
    (g_sc, g_sk, g_pre, g_post, d_sc, d_sk, d_pre, d_post,
     m_sc, m_sk, m_pre, m_post, v_sc, v_sk, v_pre, v_post, loss_sum) = small_out[:17]
    g_pw, d_pw, m_pw, v_pw = [a.reshape(pool_w.shape) for a in small_out[17:]]
    return (loss_sum[0, 0], grad_x[None], t(grad_w_in_t), g_pw, g_sc, g_sk, grad_w_out, g_pre, g_post,
            t(d_in), d_pw, d_sc, d_sk, d_out, d_pre, d_post,
            t(nm_in), m_pw, m_sc, m_sk, nm_out, m_pre, m_post,
            t(nv_in), v_pw, v_sc, v_sk, nv_out, v_pre, v_post)
```

```python
import jax
import jax.numpy as jnp
from jax import lax
from jax.experimental import pallas as pl
from jax.experimental.pallas import tpu as pltpu

F32 = jnp.float32
BF16 = jnp.bfloat16

S = 2048
D = 1024
DEPTH = 2
D_POOL = 512
POOL_WINDOWS = (2, 4, 8, 16)
N_HEADS = 8
D_IN = 2304
N_SHARDS = 4
W_IN_SHARD = D_IN // N_SHARDS
W_OUT_SHARD = D // N_SHARDS
BLK = 128
NB = S // BLK
HALO = 16
PAD = 8
EPS = 1e-6
NEG_INF = -1e30
C_PU, C_PG, C_Q, C_K, C_V, C_AG = 0, 512, 1024, 1536, 1664, 1792

ADAM_LR = 0.001
ADAM_B1 = 0.9
ADAM_B2 = 0.999
ADAM_EPS = 1e-08
ADAM_WD = 0.01
ADAM_STEP = 10

TM = 512
VMEM_LIMIT = 56 * 1024 * 1024

NT = (((1,), (1,)), ((), ()))
TN = (((0,), (0,)), ((), ()))

MESH = pl.DeviceIdType.MESH
ANY = pl.BlockSpec(memory_space=pl.ANY)

ID_FORWARD = (0, 1)
(ID_SHARE_A, ID_SHARE_B, ID_GATHER_FIRST, ID_GATHER_REST, ID_OUT1, ID_IN1, ID_OUT0, ID_IN0, ID_SMALL) = range(2, 11)

MISC_SCALE, MISC_PRE, MISC_POST, MISC_SINKS, MISC_LOSS = 0, 8, 24, 40, 56
MISC_ROWS = 64


def _params(sem=("arbitrary",)):
    return pltpu.CompilerParams(dimension_semantics=sem, vmem_limit_bytes=VMEM_LIMIT)


def _sigmoid(v):
    return 1.0 / (1.0 + jnp.exp(-v))


def _rows8(v):
    r, c = v.shape
    return v.reshape(r // 8, 8, c).sum(axis=0)


def _layer(l, *shape):
    zeros = (0,) * len(shape)
    return pl.BlockSpec((None,) + shape, lambda i: (l,) + zeros)


def _whole(shape):
    zeros = (0,) * len(shape)
    return pl.BlockSpec(shape, lambda i: zeros, pipeline_mode=pl.Buffered(1))


def _fwd_in(l, x, g_pre, w_in_t, below=None):
    fused = below is not None

    def body(x_ref, g_ref, w_ref, *rest):
        if fused:
            cat_ref, wo_ref, gp_ref, y_ref, xn_ref = rest[:5]
            y = jnp.dot(cat_ref[...], wo_ref[...], preferred_element_type=F32)
            y_ref[...] = y
            xt = x_ref[...] + y * lax.rsqrt(jnp.mean(y * y, axis=-1, keepdims=True) + EPS) * gp_ref[...]
            xn_ref[...] = xt
        else:
            xt = x_ref[...]
        pu_ref, pg_ref, q_ref, kv_ref, ag_ref = rest[-5:]
        r = lax.rsqrt(jnp.mean(xt * xt, axis=-1, keepdims=True) + EPS)
        h = (xt * r * g_ref[...]).astype(BF16)

        def proj(lo, hi):
            return lax.dot_general(h, w_ref[lo:hi, :], NT, preferred_element_type=F32)

        pu_ref[...] = proj(C_PU, C_PG)
        pg_ref[...] = proj(C_PG, C_Q)
        q_ref[...] = proj(C_Q, C_K).astype(BF16)
        kv_ref[...] = proj(C_K, C_AG).astype(BF16)
        ag_ref[...] = proj(C_AG, D_IN)

    row = lambda w: pl.BlockSpec((TM, w), lambda i: (i, 0))
    act = jax.ShapeDtypeStruct((S, D), F32)
    return pl.pallas_call(
        body, name="fwd_out_in" if fused else "fwd_in", grid=(S // TM,),
        in_specs=[row(D), _layer(l, 1, D), _whole((D_IN, D))]
        + ([row(D), _whole((D, D)), _layer(l - 1, 1, D)] if fused else []),
        out_specs=[row(D)] * (2 * fused) + [row(512), row(512), row(512), row(256), row(512)],
        out_shape=[act] * (2 * fused)
        + [jax.ShapeDtypeStruct((S, 512), F32), jax.ShapeDtypeStruct((S, 512), F32),
           jax.ShapeDtypeStruct((S, 512), BF16), jax.ShapeDtypeStruct((S, 256), BF16),
           jax.ShapeDtypeStruct((S, 512), F32)],
        compiler_params=_params(),
    )(x, g_pre, w_in_t, *(below if fused else ()))


LOG2E = 1.4426950408889634
SCORE_SCALE = 0.125 * LOG2E


def _attention_tables():
    qi = jnp.arange(BLK)[:, None]
    kj = jnp.arange(BLK)[None, :]
    dist = ((qi - kj) % BLK).astype(F32)
    slopes = jnp.exp2(-jnp.arange(1, N_HEADS + 1, dtype=F32))
    bias = -(slopes * LOG2E)[:, None, None] * dist[None]
    first = jnp.where(kj > qi, NEG_INF, bias)
    return jnp.stack([first, bias]), (kj <= qi).astype(BF16)


def _own_block_mask():
    return lax.broadcasted_iota(jnp.int32, (BLK, BLK), 1) <= lax.broadcasted_iota(jnp.int32, (BLK, BLK), 0)


def _merge(full, own):
    return jnp.where(own, full[:, BLK:], full[:, :BLK])


def _spread(v, tri):
    own = v * tri
    return jnp.concatenate([v - own, own], axis=1)


def _head_variants(cur, prev):
    both = jnp.concatenate([prev, cur], axis=0).astype(F32)
    swapped = pltpu.roll(both, 64, axis=1)
    low = lax.broadcasted_iota(jnp.int32, both.shape, 1) < 64
    zero = jnp.zeros_like(both)
    return ((jnp.where(low, both, zero).astype(BF16), jnp.where(low, zero, swapped).astype(BF16)),
            (jnp.where(low, swapped, zero).astype(BF16), jnp.where(low, zero, both).astype(BF16)))


def _head_of(hkv, t, half):
    return hkv * 4 + 2 * t + half


def _rows(v, t):
    return v[t * BLK:(t + 1) * BLK]


def _stack_tiles(ref, hkv, offset=0):
    lo = offset + 2 * hkv * 128
    return jnp.concatenate([ref[:, lo:lo + 128], ref[:, lo + 128:lo + 256]], axis=0)


def _scores(q2, k_var, own):
    s = {}
    for hkv in range(2):
        for half in range(2):
            full = lax.dot_general(q2[hkv], k_var[hkv][half], NT, preferred_element_type=F32)
            for t in range(2):
                s[hkv, t, half] = _merge(_rows(full, t), own)
    return s


def _softmax(s, bias, sink):
    s = s * SCORE_SCALE + bias
    sink2 = sink * LOG2E
    m = jnp.maximum(jnp.max(s, axis=-1, keepdims=True), sink2)
    p = jnp.exp2(s - m)
    e_sink = jnp.exp2(sink2 - m)
    inv = 1.0 / (jnp.sum(p, axis=-1, keepdims=True) + e_sink)
    return p * inv, e_sink * inv


def _spread_pair(v, hkv, half, tri):
    return jnp.concatenate([_spread(v[hkv, t, half].astype(BF16), tri) for t in range(2)], axis=0)


POOL_ROWS = PAD + HALO + BLK


def _window_sums(src_ref, tmp_refs, trailing):
    lo, hi = (PAD, POOL_ROWS) if trailing else (0, HALO + BLK)
    cur = src_ref
    for level in range(len(POOL_WINDOWS)):
        lanes = slice(level * 128, 512)
        shift = -(1 << level) if trailing else (1 << level)
        dst = tmp_refs[level % 2]
        dst[lo:hi, lanes] = cur[lo:hi, lanes] + cur[lo + shift:hi + shift, lanes]
        cur = dst


def _pool_block(ext_ref, tmp_refs, i, g, w):
    lanes = slice(g * 128, (g + 1) * 128)
    rows = slice(PAD + HALO, POOL_ROWS)
    t = (i * BLK + lax.broadcasted_iota(jnp.int32, (BLK, 1), 0)).astype(F32)
    inv = 1.0 / jnp.minimum(t + 1.0, float(w))
    return tmp_refs[g % 2][rows, lanes] * inv - ext_ref[rows, lanes], inv


def _fwd_mix(l, pu, pg, q, kv, ag, pool_w, pool_scale, sinks, tables):
    bias, tri = tables

    def body(pu_ref, pup_ref, pg_ref, q_ref, kv_ref, kvp_ref, ag_ref, pw_ref, sc_ref, sink_ref, bias_ref, tri_ref,
             cat_ref, ext_ref, *tmp_refs):
        i = pl.program_id(0)

        @pl.when(i == 0)
        def _():
            for ref in (ext_ref, *tmp_refs):
                ref[0:PAD, :] = jnp.zeros((PAD, 512), F32)

        ext_ref[PAD:PAD + HALO, :] = jnp.where(i > 0, pup_ref[...], 0.0)
        ext_ref[PAD + HALO:POOL_ROWS, :] = pu_ref[...]
        _window_sums(ext_ref, tmp_refs, True)
        for g, w in enumerate(POOL_WINDOWS):
            lanes = slice(g * 128, (g + 1) * 128)
            pooled, _ = _pool_block(ext_ref, tmp_refs, i, g, w)
            mixed = jnp.dot(pooled.astype(BF16), pw_ref[g], preferred_element_type=F32)
            gate = pg_ref[:, lanes]
            cat_ref[:, lanes] = (mixed * sc_ref[:, lanes] * (gate * _sigmoid(gate))).astype(BF16)

        own = _own_block_mask()
        tri = tri_ref[...]
        k_var = _head_variants(kv_ref[:, 0:128], kvp_ref[:, 0:128])
        v_var = _head_variants(kv_ref[:, 128:256], kvp_ref[:, 128:256])
        s = _scores([_stack_tiles(q_ref, hkv) for hkv in range(2)], k_var, own)
        p = {}
        for (hkv, t, half), s_head in s.items():
            head = _head_of(hkv, t, half)
            p[hkv, t, half], _ = _softmax(s_head, bias_ref[head], sink_ref[l, head])
        for hkv in range(2):
            o2 = jnp.zeros((2 * BLK, 128), F32)
            for half in range(2):
                o2 = o2 + jnp.dot(_spread_pair(p, hkv, half, tri), v_var[hkv][half], preferred_element_type=F32)
            for t in range(2):
                lo = (2 * hkv + t) * 128
                gate = ag_ref[:, lo:lo + 128]
                cat_ref[:, D_POOL + lo:D_POOL + lo + 128] = (_rows(o2, t) * (gate * _sigmoid(gate))).astype(BF16)

    blk = lambda w: pl.BlockSpec((BLK, w), lambda i: (i, 0))
    prev = lambda w: pl.BlockSpec((BLK, w), lambda i: (jnp.maximum(i - 1, 0), 0))
    halo = pl.BlockSpec((HALO, 512), lambda i: (jnp.maximum(i * (BLK // HALO) - 1, 0), 0))
    return pl.pallas_call(
        body, name="fwd_mix", grid=(NB,),
        in_specs=[blk(512), halo, blk(512), blk(512), blk(256), prev(256), blk(512),
                  _layer(l, 4, 128, 128), _layer(l, 1, 512), pl.BlockSpec(memory_space=pltpu.SMEM),
                  pl.BlockSpec((None, N_HEADS, BLK, BLK), lambda i: (jnp.minimum(i, 1), 0, 0, 0)), _whole((BLK, BLK))],
        out_specs=blk(D),
        out_shape=jax.ShapeDtypeStruct((S, D), BF16),
        scratch_shapes=[pltpu.VMEM((POOL_ROWS, 512), F32)] * 3,
        compiler_params=_params(),
    )(pu, pu, pg, q, kv, kv, ag, pool_w, pool_scale, sinks, bias, tri)


def _store_lane_rows(ref, acc):
    total = jnp.sum(acc, axis=0, keepdims=True)
    for k in range(ref.shape[0]):
        ref[k:k + 1, :] = total[:, k * 128:(k + 1) * 128]


def _own_piece(dw_ref, place_ref):
    p = dw_ref.shape[0] // 8
    return dw_ref[pl.ds(pl.multiple_of(place_ref[0] * p, 8), p), :]


def _bwd_out(l, cat, w_out, g_post, place_arr, dxn=None, y=None, x=None, target=None, deps=()):
    last = target is not None
    n_steps = S // TM

    def body(a_ref, b_ref, g_ref, cat_ref, w_ref, place_ref, *rest):
        dcat_ref, own_ref, dwb_ref, dg_ref = rest[len(deps):len(deps) + 4]
        rest = rest[len(deps) + 4:]
        acc_ref, dw_ref = rest[-2:]
        step = pl.program_id(0)

        @pl.when(step == 0)
        def _():
            dw_ref[...] = jnp.zeros_like(dw_ref)
            acc_ref[...] = jnp.zeros_like(acc_ref)

        cat = cat_ref[...]
        g = g_ref[...]
        y = jnp.dot(cat, w_ref[...], preferred_element_type=F32) if last else b_ref[...]
        r = lax.rsqrt(jnp.mean(y * y, axis=-1, keepdims=True) + EPS)
        if last:
            loss_ref, dx_ref, loss_acc_ref = rest[:3]
            err = a_ref[...] + y * r * g - b_ref[...]

            @pl.when(step == 0)
            def _():
                loss_acc_ref[...] = jnp.zeros_like(loss_acc_ref)

            loss_acc_ref[...] += _rows8(err * err)
            dz = err * (1.0 / D)
            dx_ref[...] = dz
        else:
            dz = a_ref[...]
        a = dz * g
        dy = r * a - y * (r * r * r) * jnp.mean(a * y, axis=-1, keepdims=True)
        acc_ref[...] += _rows8(dz * (y * r))
        dyb = dy.astype(BF16)
        dcat_ref[...] = lax.dot_general(dyb, w_ref[...], NT, preferred_element_type=F32)
        dw_ref[...] += lax.dot_general(cat, dyb, TN, preferred_element_type=F32)

        @pl.when(step == n_steps - 1)
        def _():
            _store_lane_rows(dg_ref, acc_ref[...])
            dwb_ref[...] = dw_ref[...].astype(BF16)
            own_ref[...] = _own_piece(dw_ref, place_ref)
            if last:
                loss_ref[...] = jnp.full((8, 128), (0.5 / D) * jnp.sum(loss_acc_ref[...]), F32)

    row = lambda: pl.BlockSpec((TM, D), lambda i: (i, 0))
    full = _whole
    return pl.pallas_call(
        body, name="out_loss_bwd" if last else "bwd_out", grid=(n_steps,),
        in_specs=[row(), row(), _layer(l, 1, D), row(), full((D, D)), pl.BlockSpec(memory_space=pltpu.SMEM)]
        + [ANY] * len(deps),
        out_specs=[row(), full((D // 8, D)), full((D, D)), full((8, 128))] + ([full((8, 128)), row()] if last else []),
        out_shape=[jax.ShapeDtypeStruct((S, D), F32), jax.ShapeDtypeStruct((D // 8, D), F32),
                   jax.ShapeDtypeStruct((D, D), BF16), jax.ShapeDtypeStruct((8, 128), F32)]
        + ([jax.ShapeDtypeStruct((8, 128), F32), jax.ShapeDtypeStruct((S, D), F32)] if last else []),
        scratch_shapes=([pltpu.VMEM((8, D), F32)] if last else []) + [pltpu.VMEM((8, D), F32), pltpu.VMEM((D, D), F32)],
        compiler_params=_params(),
    )(*((x, target) if last else (dxn, y)), g_post, cat, w_out, place_arr, *deps)


def _bwd_mix(l, pu, pg, q, kv, ag, dcat, pool_w, pool_scale, sinks, tables, deps=(), dpw_dest=None):
    bias, tri = tables
    deps = tuple(deps) + (() if dpw_dest is None else (dpw_dest,))

    def body(pu_ref, pup_ref, pg_ref, q_ref, kv_ref, kvp_ref, ag_ref, dcat_ref, pw_ref, sc_ref, sink_ref, bias_ref,
             tri_ref, *rest):
        dproj_ref, dpw_ref, dsc_ref, dsink_ref, ext_ref, dext_ref, tmp_a, tmp_b, dkv_ref = rest[len(deps):]
        tmp_refs = (tmp_a, tmp_b)
        step = pl.program_id(0)
        i = NB - 1 - step

        @pl.when(step == 0)
        def _():
            dpw_ref[...] = jnp.zeros_like(dpw_ref)
            dsc_ref[...] = jnp.zeros_like(dsc_ref)
            dsink_ref[...] = jnp.zeros_like(dsink_ref)
            for ref in (ext_ref, tmp_a, tmp_b):
                ref[0:PAD, :] = jnp.zeros((PAD, 512), F32)
            dext_ref[BLK:POOL_ROWS, :] = jnp.zeros((HALO + PAD, 512), F32)
            dkv_ref[...] = jnp.zeros_like(dkv_ref)

        ext_ref[PAD:PAD + HALO, :] = jnp.where(i > 0, pup_ref[...], 0.0)
        ext_ref[PAD + HALO:POOL_ROWS, :] = pu_ref[...]
        _window_sums(ext_ref, tmp_refs, True)
        dpooled = []
        for g, w in enumerate(POOL_WINDOWS):
            lanes = slice(g * 128, (g + 1) * 128)
            pooled, inv = _pool_block(ext_ref, tmp_refs, i, g, w)
            pooled_b = pooled.astype(BF16)
            mixed = jnp.dot(pooled_b, pw_ref[g], preferred_element_type=F32)
            scale = sc_ref[:, lanes]
            gate = pg_ref[:, lanes]
            sg = _sigmoid(gate)
            dpo = dcat_ref[:, lanes]
            dproj_ref[:, C_PG + g * 128:C_PG + (g + 1) * 128] = (
                dpo * (mixed * scale) * (sg * (1.0 + gate * (1.0 - sg)))).astype(BF16)
            dms = dpo * (gate * sg)
            dsc_ref[g:g + 1, :] += jnp.sum(dms * mixed, axis=0, keepdims=True)
            dmixed = (dms * scale).astype(BF16)
            dpw_ref[g] += lax.dot_general(pooled_b, dmixed, TN, preferred_element_type=F32)
            dpooled.append(lax.dot_general(dmixed, pw_ref[g], NT, preferred_element_type=F32))
            dext_ref[0:BLK, lanes] = dpooled[g] * inv
        _window_sums(dext_ref, tmp_refs, False)
        for g in range(len(POOL_WINDOWS)):
            lanes = slice(g * 128, (g + 1) * 128)
            dproj_ref[:, C_PU + g * 128:C_PU + (g + 1) * 128] = (tmp_refs[g % 2][0:BLK, lanes] - dpooled[g]).astype(BF16)
        dext_ref[BLK:BLK + HALO, :] = dext_ref[0:HALO, :]

        own = _own_block_mask()
        tri = tri_ref[...]
        k_var = _head_variants(kv_ref[:, 0:128], kvp_ref[:, 0:128])
        v_var = _head_variants(kv_ref[:, 128:256], kvp_ref[:, 128:256])
        q2 = [_stack_tiles(q_ref, hkv) for hkv in range(2)]
        s = _scores(q2, k_var, own)
        p, p_sink = {}, {}
        for key, s_head in s.items():
            head = _head_of(*key)
            p[key], p_sink[key] = _softmax(s_head, bias_ref[head], sink_ref[l, head])

        do2, p_b, dp = [], {}, {}
        for hkv in range(2):
            gate = _stack_tiles(ag_ref, hkv)
            sg = _sigmoid(gate)
            dca = _stack_tiles(dcat_ref, hkv, D_POOL)
            do2.append((dca * (gate * sg)).astype(BF16))
            o2 = jnp.zeros((2 * BLK, 128), F32)
            for half in range(2):
                p_b[hkv, half] = _spread_pair(p, hkv, half, tri)
                o2 = o2 + jnp.dot(p_b[hkv, half], v_var[hkv][half], preferred_element_type=F32)
                full = lax.dot_general(do2[hkv], v_var[hkv][half], NT, preferred_element_type=F32)
                for t in range(2):
                    dp[hkv, t, half] = _merge(_rows(full, t), own)
            dag = dca * o2 * (sg * (1.0 + gate * (1.0 - sg)))
            for t in range(2):
                lo = C_AG + (2 * hkv + t) * 128
                dproj_ref[:, lo:lo + 128] = _rows(dag, t).astype(BF16)

        ds = {}
        for key in p:
            delta = jnp.sum(p[key] * dp[key], axis=-1, keepdims=True)
            ds[key] = p[key] * (dp[key] - delta)
            head = _head_of(*key)
            dsink_ref[0:1, :] += jnp.where(lax.broadcasted_iota(jnp.int32, (1, 128), 1) == head,
                                           -jnp.sum(p_sink[key] * delta, axis=0, keepdims=True), 0.0)

        dk_acc = [[None, None], [None, None]]
        dv_acc = [[None, None], [None, None]]
        for hkv in range(2):
            dq2 = jnp.zeros((2 * BLK, 128), F32)
            for half in range(2):
                ds_b = _spread_pair(ds, hkv, half, tri)
                dq2 = dq2 + jnp.dot(ds_b, k_var[hkv][half], preferred_element_type=F32)
                dk_acc[hkv][half] = lax.dot_general(ds_b, q2[hkv], TN, preferred_element_type=F32)
                dv_acc[hkv][half] = lax.dot_general(p_b[hkv, half], do2[hkv], TN, preferred_element_type=F32)
            for t in range(2):
                lo = C_Q + (2 * hkv + t) * 128
                dproj_ref[:, lo:lo + 128] = (_rows(dq2, t) * 0.125).astype(BF16)

        low = lax.broadcasted_iota(jnp.int32, (2 * BLK, 128), 1) < 64

        def gather_heads(acc):
            return jnp.where(low, acc[0][0] + pltpu.roll(acc[0][1], 64, axis=1),
                             pltpu.roll(acc[1][0], 64, axis=1) + acc[1][1])

        dk = gather_heads(dk_acc) * 0.125
        dv = gather_heads(dv_acc)
        dproj_ref[:, C_K:C_V] = (dk[BLK:, :] + dkv_ref[:, 0:128]).astype(BF16)
        dproj_ref[:, C_V:C_AG] = (dv[BLK:, :] + dkv_ref[:, 128:256]).astype(BF16)
        dkv_ref[:, 0:128] = dk[:BLK, :]
        dkv_ref[:, 128:256] = dv[:BLK, :]

    rev = lambda w: pl.BlockSpec((BLK, w), lambda s: (NB - 1 - s, 0))
    prev = lambda w: pl.BlockSpec((BLK, w), lambda s: (jnp.maximum(NB - 2 - s, 0), 0))
    halo = pl.BlockSpec((HALO, 512), lambda s: (jnp.maximum((NB - 1 - s) * (BLK // HALO) - 1, 0), 0))
    return pl.pallas_call(
        body, name="bwd_mix", grid=(NB,),
        in_specs=[rev(512), halo, rev(512), rev(512), rev(256), prev(256), rev(512), rev(D),
                  _layer(l, 4, 128, 128), _layer(l, 1, 512), pl.BlockSpec(memory_space=pltpu.SMEM),
                  pl.BlockSpec((None, N_HEADS, BLK, BLK), lambda s: (jnp.minimum(NB - 1 - s, 1), 0, 0, 0)),
                  _whole((BLK, BLK))] + [ANY] * len(deps),
        out_specs=[rev(D_IN), _layer(l, 4, 128, 128),
                   pl.BlockSpec((4, 128), lambda s: (0, 0)), pl.BlockSpec((8, 128), lambda s: (0, 0))],
        out_shape=[jax.ShapeDtypeStruct((S, D_IN), BF16), jax.ShapeDtypeStruct((DEPTH, 4, 128, 128), F32),
                   jax.ShapeDtypeStruct((4, 128), F32), jax.ShapeDtypeStruct((8, 128), F32)],
        input_output_aliases={} if dpw_dest is None else {12 + len(deps): 1},
        scratch_shapes=[pltpu.VMEM((POOL_ROWS, 512), F32)] * 4 + [pltpu.VMEM((BLK, 256), F32)],
        compiler_params=_params(),
    )(pu, pu, pg, q, kv, kv, ag, dcat, pool_w, pool_scale, sinks, bias, tri, *deps)


def _bwd_in_dw(l, dproj, x, g_pre, place_arr, deps=()):
    n_steps = S // TM

    def body(dp_ref, x_ref, g_ref, place_ref, *rest):
        own_ref, dwb_ref, dw_ref = rest[len(deps):]
        step = pl.program_id(0)

        @pl.when(step == 0)
        def _():
            dw_ref[...] = jnp.zeros_like(dw_ref)

        xt = x_ref[...]
        r = lax.rsqrt(jnp.mean(xt * xt, axis=-1, keepdims=True) + EPS)
        h = (xt * r * g_ref[...]).astype(BF16)
        dw_ref[...] += lax.dot_general(dp_ref[...], h, TN, preferred_element_type=F32)

        @pl.when(step == n_steps - 1)
        def _():
            dwb_ref[...] = dw_ref[...].astype(BF16)
            own_ref[...] = _own_piece(dw_ref, place_ref)

    row = lambda w: pl.BlockSpec((TM, w), lambda i: (i, 0))
    full = _whole
    return pl.pallas_call(
        body, name="bwd_in_dw", grid=(n_steps,),
        in_specs=[row(D_IN), row(D), _layer(l, 1, D), pl.BlockSpec(memory_space=pltpu.SMEM)] + [ANY] * len(deps),
        out_specs=[full((D_IN // 8, D)), full((D_IN, D))],
        out_shape=[jax.ShapeDtypeStruct((D_IN // 8, D), F32), jax.ShapeDtypeStruct((D_IN, D), BF16)],
        scratch_shapes=[pltpu.VMEM((D_IN, D), F32)],
        compiler_params=_params(),
    )(dproj, x, g_pre, place_arr, *deps)


def _bwd_in_dx(l, dproj, w_in_t, x, g_pre, dres, deps=(), dw_place=None):
    n_steps = S // TM
    with_dw = dw_place is not None

    def body(dp_ref, w_ref, x_ref, g_ref, dres_ref, *rest):
        place_ref = rest[0] if with_dw else None
        rest = rest[with_dw + len(deps):]
        if with_dw:
            dx_ref, dg_ref, own_ref, dwb_ref, acc_ref, dw_ref = rest
        else:
            dx_ref, dg_ref, acc_ref = rest
        step = pl.program_id(0)

        @pl.when(step == 0)
        def _():
            acc_ref[...] = jnp.zeros_like(acc_ref)
            if with_dw:
                dw_ref[...] = jnp.zeros_like(dw_ref)

        g = g_ref[...]
        halves = [slice(k * (TM // 2), (k + 1) * (TM // 2)) for k in range(2)]
        dh = [jnp.dot(dp_ref[rows, :], w_ref[...], preferred_element_type=F32) for rows in halves]
        h = []
        for rows, dh_k in zip(halves, dh):
            xt = x_ref[rows, :]
            r = lax.rsqrt(jnp.mean(xt * xt, axis=-1, keepdims=True) + EPS)
            xn = xt * r
            acc_ref[...] += _rows8(dh_k * xn)
            a = dh_k * g
            dx_ref[rows, :] = dres_ref[rows, :] + (
                r * a - xt * (r * r * r) * jnp.mean(a * xt, axis=-1, keepdims=True))
            h.append((xn * g).astype(BF16))
        if with_dw:
            dw_ref[...] += lax.dot_general(dp_ref[...], jnp.concatenate(h, axis=0), TN, preferred_element_type=F32)

        @pl.when(step == n_steps - 1)
        def _():
            _store_lane_rows(dg_ref, acc_ref[...])
            if with_dw:
                dwb_ref[...] = dw_ref[...].astype(BF16)
                own_ref[...] = _own_piece(dw_ref, place_ref)

    row = lambda w: pl.BlockSpec((TM, w), lambda i: (i, 0))
    full = _whole
    dw_specs = [full((D_IN // 8, D)), full((D_IN, D))] if with_dw else []
    dw_shapes = [jax.ShapeDtypeStruct((D_IN // 8, D), F32), jax.ShapeDtypeStruct((D_IN, D), BF16)] if with_dw else []
    return pl.pallas_call(
        body, name="bwd_in" if with_dw else "bwd_in_dx", grid=(n_steps,),
        in_specs=[row(D_IN), full((D_IN, D)), row(D), _layer(l, 1, D), row(D)]
        + [pl.BlockSpec(memory_space=pltpu.SMEM)] * with_dw + [ANY] * len(deps),
        out_specs=[row(D), full((8, 128))] + dw_specs,
        out_shape=[jax.ShapeDtypeStruct((S, D), F32), jax.ShapeDtypeStruct((8, 128), F32)] + dw_shapes,
        scratch_shapes=[pltpu.VMEM((8, D), F32)] + [pltpu.VMEM((D_IN, D), F32)] * with_dw,
        compiler_params=_params(),
    )(dproj, w_in_t, x, g_pre, dres, *((dw_place,) if with_dw else ()), *deps)


HBM =pl.BlockSpec(memory_space=pltpu.HBM)
SEM = pl.BlockSpec(memory_space=pltpu.SEMAPHORE)
def _split_copy(collective_id=None):
    return pltpu.CompilerParams(has_side_effects=pltpu.SideEffectType.DATAFLOW_SIDE_EFFECTING,
                                collective_id=collective_id)


SPLIT_COPY = _split_copy()


def _in_hbm(a):
    return pltpu.with_memory_space_constraint(a, pltpu.HBM)

def _place():
    return lax.axis_index("x"), lax.axis_index("y"), lax.axis_index("c")


def _other_chips(x, y):
    return [(1 - x, y), (x, 1 - y), (1 - x, 1 - y)]


def _peer(x, y, c, m):
    return (x ^ (m >> 2), y ^ ((m >> 1) & 1), c ^ (m & 1))


SAME_CORE = (2, 4, 6)


def _place_cast(name, src, chip_arr, tile, layers, deps=()):
    _, n, cols = src.shape
    steps = n // tile
    k = len(layers)

    def body(chip_ref, *refs):
        for s_ref, o_ref in zip(refs[:k], refs[k + len(deps):]):
            o_ref[...] = s_ref[...].astype(BF16)

    def layer_spec(l):
        return pl.BlockSpec((None, tile, cols), lambda i, chip: (l, i, 0))

    return pl.pallas_call(
        body, name=name,
        grid_spec=pltpu.PrefetchScalarGridSpec(
            num_scalar_prefetch=1, grid=(steps,),
            in_specs=[layer_spec(l) for l in layers] + [ANY] * len(deps),
            out_specs=[pl.BlockSpec((tile, cols), lambda i, chip: (chip[0] * steps + i, 0))] * k),
        out_shape=[jax.ShapeDtypeStruct((N_SHARDS * n, cols), BF16)] * k,
        compiler_params=_params(),
    )(chip_arr, *[src] * k, *deps)


def _place_other_half(name, src, place_arr, layer, dest):
    _, n, cols = src.shape

    def body(place_ref, s_ref, dest_ref, o_ref):
        o_ref[...] = s_ref[...].astype(BF16)

    return pl.pallas_call(
        body, name=name,
        grid_spec=pltpu.PrefetchScalarGridSpec(
            num_scalar_prefetch=1, grid=(1,),
            in_specs=[pl.BlockSpec((None, n // 2, cols), lambda i, place: (layer, 1 - place[1], 0)), ANY],
            out_specs=pl.BlockSpec((n // 2, cols), lambda i, place: (place[0] + 1 - 2 * place[1], 0))),
        out_shape=jax.ShapeDtypeStruct((N_SHARDS * n, cols), BF16),
        input_output_aliases={2: 0},
        compiler_params=_params(),
    )(place_arr, src, dest)


def _chip_rows(ref, chip, half=None):
    n = ref.shape[0] // N_SHARDS
    if half is None:
        return ref.at[pl.ds(pl.multiple_of(chip * n, 16), n), :]
    return ref.at[pl.ds(pl.multiple_of(chip * n + half * (n // 2), 16), n // 2), :]


def _gather_start(name, bufs, halved, collective_id):
    n = len(bufs)

    def body(*refs):
        ins, send, recv, token = refs[:n], refs[n:2 * n], refs[2 * n:3 * n], refs[-1]
        x, y, c = _place()
        _handshake([(*chip, c) for chip in _other_chips(x, y)])
        for a, buf in enumerate(ins):
            own = _chip_rows(buf, 2 * x + y, c if a in halved else None)
            for j, chip in enumerate(_other_chips(x, y)):
                pltpu.make_async_remote_copy(src_ref=own, dst_ref=own, send_sem=send[a].at[j], recv_sem=recv[a].at[j],
                                             device_id=(*chip, c), device_id_type=MESH).start()
        token[...] = jnp.zeros_like(token)

    outs = pl.pallas_call(
        body, name=name, in_specs=[HBM] * n,
        out_specs=[SEM] * (2 * n) + [HBM] * n + [pl.BlockSpec(memory_space=pltpu.VMEM)],
        out_shape=[pltpu.SemaphoreType.DMA((3,))] * (2 * n) + [pltpu.HBM(b.shape, b.dtype) for b in bufs]
        + [jax.ShapeDtypeStruct((8, 128), F32)],
        input_output_aliases={a: 2 * n + a for a in range(n)},
        compiler_params=_split_copy(collective_id),
    )(*[_in_hbm(b) for b in bufs])
    return outs[:n], outs[n:2 * n], outs[2 * n:3 * n], outs[-1]


def _gather_start_cast(name, src, layer, collective_id):
    _, n, cols = src.shape
    half = n // 2

    def body(src_ref, send, recv, buf_ref, token, f32_ref, bf16_ref, local):
        x, y, c = _place()
        own = _chip_rows(buf_ref, 2 * x + y, c)

        def cast():
            load = pltpu.make_async_copy(src_ref.at[layer, pl.ds(pl.multiple_of(c * half, 16), half), :], f32_ref,
                                         local.at[0])
            load.start()
            load.wait()
            bf16_ref[...] = f32_ref[...].astype(BF16)
            store = pltpu.make_async_copy(bf16_ref, own, local.at[1])
            store.start()
            store.wait()

        _handshake([(*chip, c) for chip in _other_chips(x, y)], cast)
        for j, chip in enumerate(_other_chips(x, y)):
            pltpu.make_async_remote_copy(src_ref=own, dst_ref=own, send_sem=send.at[j], recv_sem=recv.at[j],
                                         device_id=(*chip, c), device_id_type=MESH).start()
        token[...] = jnp.zeros_like(token)

    outs = pl.pallas_call(
        body, name=name, in_specs=[HBM],
        out_specs=[SEM, SEM, HBM, pl.BlockSpec(memory_space=pltpu.VMEM)],
        out_shape=[pltpu.SemaphoreType.DMA((3,))] * 2 + [pltpu.HBM((N_SHARDS * n, cols), BF16),
                                                         jax.ShapeDtypeStruct((8, 128), F32)],
        scratch_shapes=[pltpu.VMEM((half, cols), F32), pltpu.VMEM((half, cols), BF16), pltpu.SemaphoreType.DMA((2,))],
        compiler_params=_split_copy(collective_id),
    )(_in_hbm(src))
    return outs[:1], outs[1:2], outs[2:3], outs[3]


def _gather_wait(name, buf, send_sem, recv_sem, after, halved=False):
    def body(buf_ref, send_ref, recv_ref, *rest):
        x, y, c = _place()
        half = c if halved else None
        own = _chip_rows(buf_ref, 2 * x + y, half)
        for j, chip in enumerate(_other_chips(x, y)):
            copy = pltpu.make_async_remote_copy(src_ref=own, dst_ref=_chip_rows(buf_ref, 2 * chip[0] + chip[1], half),
                                                send_sem=send_ref.at[j], recv_sem=recv_ref.at[j],
                                                device_id=(*chip, c), device_id_type=MESH)
            copy.wait_send()
            copy.wait_recv()

    return pl.pallas_call(
        body, name=name, in_specs=[HBM, SEM, SEM] + [ANY] * len(after), out_specs=HBM,
        out_shape=pltpu.HBM(buf.shape, buf.dtype), input_output_aliases={0: 0}, compiler_params=SPLIT_COPY,
    )(buf, send_sem, recv_sem, *after)


def _handshake(peers, meanwhile=None):
    barrier = pltpu.get_barrier_semaphore()
    for peer in peers:
        pl.semaphore_signal(barrier, inc=1, device_id=peer, device_id_type=MESH)
    if meanwhile is not None:
        meanwhile()
    pl.semaphore_wait(barrier, len(peers))


def _sibling_handshake(x, y, c):
    _handshake([(x, y, 1 - c)])


def _forward_halves(name, bufs, collective_id):
    n = len(bufs)

    def body(*refs):
        ins, outs, (send_sems, recv_sems) = refs[:n], refs[n:2 * n], refs[2 * n:]
        x, y, c = _place()
        _sibling_handshake(x, y, c)

        def copy(a, j, chip, half):
            rows = 2 * chip[0] + chip[1]
            return pltpu.make_async_remote_copy(
                src_ref=_chip_rows(ins[a], rows, half), dst_ref=_chip_rows(outs[a], rows, half),
                send_sem=send_sems.at[3 * a + j], recv_sem=recv_sems.at[3 * a + j], device_id=(x, y, 1 - c),
                device_id_type=MESH)

        copies = [(a, j, chip) for a in range(n) for j, chip in enumerate(_other_chips(x, y))]
        for a, j, chip in copies:
            copy(a, j, chip, c).start()
        for a, j, chip in copies:
            copy(a, j, chip, c).wait_send()
            copy(a, j, chip, 1 - c).wait_recv()

    return pl.pallas_call(
        body, name=name, in_specs=[ANY] * n, out_specs=[ANY] * n,
        out_shape=[jax.ShapeDtypeStruct(b.shape, b.dtype) for b in bufs],
        input_output_aliases={a: a for a in range(n)},
        scratch_shapes=[pltpu.SemaphoreType.DMA((3 * n,))] * 2,
        compiler_params=pltpu.CompilerParams(collective_id=collective_id),
    )(*bufs)


def _piece_rows(ref, k):
    p = ref.shape[0] // 8
    return ref.at[pl.ds(pl.multiple_of(k * p, 32 // jnp.dtype(ref.dtype).itemsize), p), :]


def _exchange_start(name, arrays, collective_id):
    n = len(arrays)
    zones = [lax.empty((7, a.shape[0] // 8, a.shape[1]), a.dtype) for a in arrays]

    def body(*refs):
        srcs, lands = refs[:n], refs[n:2 * n]
        send, recv = refs[2 * n:3 * n], refs[3 * n:4 * n]
        x, y, c = _place()
        _handshake([_peer(x, y, c, m) for m in range(1, 8)])
        for a, (src, land) in enumerate(zip(srcs, lands)):
            for m in range(1, 8):
                px, py, pc = _peer(x, y, c, m)
                pltpu.make_async_remote_copy(
                    src_ref=_piece_rows(src, 4 * px + 2 * py + pc), dst_ref=land.at[m - 1], send_sem=send[a].at[m - 1],
                    recv_sem=recv[a].at[m - 1], device_id=(px, py, pc), device_id_type=MESH).start()

    outs = pl.pallas_call(
        body, name=name, in_specs=[HBM] * (2 * n), out_specs=[SEM] * (2 * n) + [HBM] * (2 * n),
        out_shape=[pltpu.SemaphoreType.DMA((7,))] * (2 * n) + [pltpu.HBM(a.shape, a.dtype) for a in arrays + zones],
        input_output_aliases={a: 2 * n + a for a in range(2 * n)},
        compiler_params=_split_copy(collective_id),
    )(*[_in_hbm(a) for a in arrays + zones])
    return outs[:n], outs[n:2 * n], outs[2 * n:3 * n], outs[3 * n:4 * n]


def _exchange_wait(name, started, after, which=None, with_sent=False):
    which = range(len(started[2])) if which is None else which
    send_sems, recv_sems, arrays, zones = [[group[k] for k in which] for group in started[:4]]
    n = len(arrays)

    def body(*refs):
        srcs, lands = refs[:n], refs[n:2 * n]
        send, recv = refs[2 * n:3 * n], refs[3 * n:4 * n]
        x, y, c = _place()
        for a, (src, land) in enumerate(zip(srcs, lands)):
            for m in range(1, 8):
                px, py, pc = _peer(x, y, c, m)
                copy = pltpu.make_async_remote_copy(
                    src_ref=_piece_rows(src, 4 * px + 2 * py + pc), dst_ref=land.at[m - 1], send_sem=send[a].at[m - 1],
                    recv_sem=recv[a].at[m - 1], device_id=(px, py, pc), device_id_type=MESH)
                copy.wait_send()
                copy.wait_recv()

    outs = pl.pallas_call(
        body, name=name, in_specs=[HBM] * (2 * n) + [SEM] * (2 * n) + [ANY], out_specs=[HBM] * (2 * n),
        out_shape=[pltpu.HBM(a.shape, a.dtype) for a in list(arrays) + list(zones)],
        input_output_aliases={a: a for a in range(2 * n)}, compiler_params=SPLIT_COPY,
    )(*arrays, *zones, *send_sems, *recv_sems, after)
    return outs if with_sent else outs[n:]


def _sum_pieces(name, weights, place_arr, dests=None):
    steps = 2
    flat = [item for items in weights for item in items]
    n = len(flat)

    def body(place_ref, *refs):
        outs = iter(refs[len(refs) - len(weights):])
        k = 0
        for items in weights:
            out_ref = next(outs)
            for layer, _, _ in items:
                total = refs[k][...]
                for m in range(7):
                    total = total + refs[n + k][m].astype(F32)
                if len(items) == DEPTH:
                    out_ref[layer] = total
                else:
                    out_ref[...] = total
                k += 1

    def out_spec(items):
        _, own, _ = items[0]
        t, cols = own.shape[0] // steps, own.shape[1]
        if len(items) == DEPTH:
            return pl.BlockSpec((DEPTH, t, cols), lambda i, place: (0, place[1] * steps + i, 0))
        layer = items[0][0]
        return pl.BlockSpec((None, t, cols), lambda i, place: (layer, place[1] * steps + i, 0))

    owns = [own for _, own, _ in flat]
    dests = [] if dests is None else list(dests)
    return pl.pallas_call(
        body, name=name,
        grid_spec=pltpu.PrefetchScalarGridSpec(
            num_scalar_prefetch=1, grid=(steps,),
            in_specs=[pl.BlockSpec((o.shape[0] // steps, o.shape[1]), lambda i, place: (i, 0)) for o in owns]
            + [pl.BlockSpec((7, o.shape[0] // steps, o.shape[1]), lambda i, place: (0, i, 0)) for o in owns]
            + [ANY] * len(dests),
            out_specs=[out_spec(items) for items in weights]),
        out_shape=[jax.ShapeDtypeStruct((DEPTH, 2 * items[0][1].shape[0], items[0][1].shape[1]), F32)
                   for items in weights],
        input_output_aliases={1 + 2 * n + k: k for k in range(len(dests))},
        compiler_params=_params(),
    )(place_arr, *owns, *[recv for _, _, recv in flat], *dests)


def _sum_small(name, partials, recvs, place_arr):
    n = len(partials)

    def body(place_ref, *refs):
        for o_ref, r_ref, out_ref in zip(refs[:n], refs[n:2 * n], refs[2 * n:]):
            total = o_ref[...]
            for m in range(7):
                total = total + r_ref[m]
            out_ref[...] = total

    piece = lambda a: pl.BlockSpec((a.shape[0] // 8, a.shape[1]), lambda i, place: (place[0], 0))
    return pl.pallas_call(
        body, name=name,
        grid_spec=pltpu.PrefetchScalarGridSpec(
            num_scalar_prefetch=1, grid=(1,),
            in_specs=[piece(a) for a in partials] + [pl.BlockSpec(r.shape, lambda i, place: (0, 0, 0)) for r in recvs],
            out_specs=[piece(a) for a in partials]),
        out_shape=[jax.ShapeDtypeStruct(a.shape, F32) for a in partials],
        compiler_params=_params(),
    )(place_arr, *partials, *recvs)


def _share(name, bufs, parts, gathered=(), collective_id=None, summed=()):
    n, n_g, n_s = len(bufs), len(gathered), len(summed)
    total = n + n_g
    made = [target for target, _, _ in summed]

    def body(*refs):
        ins, extra, outs = refs[:total], refs[total:total + 2 * n_s], refs[total + 2 * n_s:2 * total + 2 * n_s]
        send_sems, recv_sems, send_g, recv_g = refs[2 * total + 2 * n_s:2 * total + 2 * n_s + 4]
        scratch = refs[2 * total + 2 * n_s + 4:]
        x, y, c = _place()

        def half(ref, l, which):
            p = ref.shape[1] // 2
            return ref.at[l, pl.ds(pl.multiple_of(which * p, 8), p), :]

        def sums():
            local, acc, got = scratch[0], scratch[1:1 + n_s], scratch[1 + n_s:]
            loads, stores = [], []
            for j, (target, _, _) in enumerate(summed):
                own_rows = extra[2 * j] if target[0] == "half" else _piece_rows(extra[2 * j], 4 * x + 2 * y + c)
                loads += [pltpu.make_async_copy(own_rows, acc[j], local.at[2 * j]),
                          pltpu.make_async_copy(extra[2 * j + 1], got[j], local.at[2 * j + 1])]
            for load in loads:
                load.start()
            for j, (target, _, _) in enumerate(summed):
                loads[2 * j].wait()
                loads[2 * j + 1].wait()
                rows = acc[j].shape[0]
                step = min(rows, 96)
                for r in range(0, rows, step):
                    part = acc[j][r:r + step, :]
                    for m in range(7):
                        part = part + got[j][m, r:r + step, :].astype(F32)
                    acc[j][r:r + step, :] = part
                if target[0] == "half":
                    dest = half(outs[target[1]], target[2], c)
                else:
                    dest = _piece_rows(outs[n + target[1]], 4 * x + 2 * y + c)
                stores.append(pltpu.make_async_copy(acc[j], dest, local.at[2 * j]))
                stores[-1].start()
            for store in stores:
                store.wait()

        _handshake([_peer(x, y, c, m) for m in (SAME_CORE if gathered else ()) + (1,)], sums if summed else None)

        def swap(k, which):
            a, l = parts[k]
            held = outs if ("half", a, l) in made else ins
            return pltpu.make_async_remote_copy(
                src_ref=half(held[a], l, which), dst_ref=half(outs[a], l, which), send_sem=send_sems.at[k],
                recv_sem=recv_sems.at[k], device_id=(x, y, 1 - c), device_id_type=MESH)

        def spread(a, m, sender, held, to):
            k = 4 * sender[0] + 2 * sender[1] + sender[2]
            return pltpu.make_async_remote_copy(
                src_ref=_piece_rows(held[n + a], k), dst_ref=_piece_rows(outs[n + a], k),
                send_sem=send_g.at[7 * a + m - 1], recv_sem=recv_g.at[7 * a + m - 1], device_id=to, device_id_type=MESH)

        me, sibling = (x, y, c), (x, y, 1 - c)
        for k in range(len(parts)):
            swap(k, c).start()
        def own(a, m):
            return spread(a, m, me, outs if ("piece", a) in made else ins, _peer(x, y, c, m))

        def handed_on(a, m):
            return spread(a, m + 1, _peer(x, y, c, m), outs, sibling)

        for a in range(n_g):
            for m in SAME_CORE + (1,):
                own(a, m).start()
        for a in range(n_g):
            for m in SAME_CORE:
                spread(a, m, _peer(x, y, c, m), ins, _peer(x, y, c, m)).wait_recv()
                handed_on(a, m).start()
        for k in range(len(parts)):
            swap(k, c).wait_send()
            swap(k, 1 - c).wait_recv()
        for a in range(n_g):
            for m in SAME_CORE + (1,):
                own(a, m).wait_send()
            for m in SAME_CORE:
                handed_on(a, m).wait_send()
                spread(a, m + 1, _peer(x, y, c, m + 1), ins, sibling).wait_recv()
            spread(a, 1, sibling, ins, sibling).wait_recv()

    arrays = list(bufs) + list(gathered)
    sum_scratch = []
    if summed:
        sum_scratch = [pltpu.SemaphoreType.DMA((2 * n_s,))] + [pltpu.VMEM(recv.shape[1:], F32) for _, _, recv in summed]
        sum_scratch += [pltpu.VMEM(recv.shape, recv.dtype) for _, _, recv in summed]
    return pl.pallas_call(
        body, name=name, in_specs=[ANY] * (total + 2 * n_s), out_specs=[ANY] * total,
        out_shape=[jax.ShapeDtypeStruct(b.shape, F32) for b in arrays],
        input_output_aliases={a: a for a in range(total)},
        scratch_shapes=[pltpu.SemaphoreType.DMA((max(len(parts), 1),))] * 2
        + [pltpu.SemaphoreType.DMA((max(7 * n_g, 1),))] * 2 + sum_scratch,
        compiler_params=pltpu.CompilerParams(collective_id=collective_id, vmem_limit_bytes=VMEM_LIMIT),
    )(*arrays, *[array for _, own, recv in summed for array in (own, recv)])


def _adamw_math(w, g, m, v):
    nm = ADAM_B1 * m + (1.0 - ADAM_B1) * g
    nv = ADAM_B2 * v + (1.0 - ADAM_B2) * (g * g)
    m_hat = nm / (1.0 - ADAM_B1 ** ADAM_STEP)
    v_hat = nv / (1.0 - ADAM_B2 ** ADAM_STEP)
    return -ADAM_LR * (m_hat / (jnp.sqrt(v_hat) + ADAM_EPS) + ADAM_WD * w), nm, nv


def _adamw(name, w, g, m, v, rows_per_step, first=0, count=None, dests=None, deps=(), small=()):
    layers, rows, cols = w.shape
    count = layers if count is None else count
    dests = () if dests is None else tuple(dests)
    n_in = 4 + len(dests) + len(deps)
    small_shapes = _small_shapes(small[4].shape) if small else []

    def body(*refs):
        w_ref, g_ref, m_ref, v_ref = refs[:4]
        d_ref, nm_ref, nv_ref, g_out_ref = refs[n_in + len(small):n_in + len(small) + 4]
        d_ref[...], nm_ref[...], nv_ref[...] = _adamw_math(w_ref[...], g_ref[...], m_ref[...], v_ref[...])
        g_out_ref[...] = g_ref[...]
        if small:
            @pl.when((pl.program_id(0) == 0) & (pl.program_id(1) == 0))
            def _():
                _adamw_small(*refs[n_in:n_in + len(small)], *refs[n_in + len(small) + 4:])

    spec = pl.BlockSpec((1, rows_per_step, cols), lambda l, i: (first + l, i, 0))
    whole = lambda shape: pl.BlockSpec(shape, lambda l, i: (0,) * len(shape))
    shape = jax.ShapeDtypeStruct(w.shape, F32)
    return pl.pallas_call(
        body, name=name, grid=(count, rows // rows_per_step),
        in_specs=[spec] * 4 + [ANY] * (len(dests) + len(deps)) + [whole(a.shape) for a in small],
        out_specs=[spec] * 4 + [whole(s) for s in small_shapes],
        out_shape=[shape] * 4 + [jax.ShapeDtypeStruct(s, F32) for s in small_shapes],
        input_output_aliases={4 + k: k for k in range(len(dests))},
        scratch_shapes=[pltpu.VMEM((MISC_ROWS, 128), F32)] * 3 if small else [],
        compiler_params=_params(("arbitrary", "arbitrary")),
    )(w, g, m, v, *dests, *deps, *small)


def _pack_misc(pool_scale, sinks, norm_pre, norm_post):
    sink_rows = jnp.zeros((DEPTH, 8, 128), F32).at[:, 0, 0:N_HEADS].set(sinks).reshape(2 * 8, 128)
    return jnp.concatenate([pool_scale.reshape(8, 128), norm_pre.reshape(16, 128), norm_post.reshape(16, 128),
                            sink_rows, jnp.zeros((8, 128), F32)], axis=0)


def _adamw_small(w_ref, g_ref, m_ref, v_ref, pw_ref, pg_ref, pm_ref, pv_ref, *rest):
    outs, pool_outs, (d_ref, nm_ref, nv_ref) = rest[:17], rest[17:21], rest[21:]
    pool_outs[0][...] = pg_ref[...]
    pool_outs[1][...], pool_outs[2][...], pool_outs[3][...] = _adamw_math(
        pw_ref[...], pg_ref[...], pm_ref[...], pv_ref[...])
    d_ref[...], nm_ref[...], nv_ref[...] = _adamw_math(w_ref[...], g_ref[...], m_ref[...], v_ref[...])
    for k, src in enumerate([g_ref, d_ref, nm_ref, nv_ref]):
        scale, sinks, pre, post = outs[4 * k:4 * k + 4]
        for l in range(DEPTH):
            for j in range(4):
                scale[l:l + 1, j * 128:(j + 1) * 128] = src[MISC_SCALE + 4 * l + j:MISC_SCALE + 4 * l + j + 1, :]
            for j in range(8):
                pre[l:l + 1, j * 128:(j + 1) * 128] = src[MISC_PRE + 8 * l + j:MISC_PRE + 8 * l + j + 1, :]
                post[l:l + 1, j * 128:(j + 1) * 128] = src[MISC_POST + 8 * l + j:MISC_POST + 8 * l + j + 1, :]
            sinks[l:l + 1, :] = src[MISC_SINKS + 8 * l:MISC_SINKS + 8 * l + 1, 0:N_HEADS]
    outs[16][...] = g_ref[MISC_LOSS:MISC_LOSS + 1, 0:1]


def _small_shapes(pool_shape):
    return [(DEPTH, D_POOL), (DEPTH, N_HEADS), (DEPTH, D), (DEPTH, D)] * 4 + [(1, 1)] + [pool_shape] * 4


def kernel(x, w_in, pool_w, pool_scale, attn_sinks, w_out, norm_pre, norm_post, loss_target, m_w_in, m_pool_w, m_pool_scale, m_attn_sinks, m_w_out, m_norm_pre, m_norm_post, v_w_in, v_pool_w, v_pool_scale, v_attn_sinks, v_w_out, v_norm_pre, v_norm_post):
    cx, cy, cc = _place()
    chip_arr = jnp.reshape(2 * cx + cy, (1,)).astype(jnp.int32)
    place_arr = jnp.stack([4 * cx + 2 * cy + cc, cc]).astype(jnp.int32)
    t = lambda a: jnp.transpose(a, (0, 2, 1))
    w_in_t = t(w_in)
    xs, target = x[0], loss_target[0]
    pool_w_b = pool_w.astype(BF16)
    tables = _attention_tables()
    scale3 = pool_scale.reshape(DEPTH, 1, D_POOL)
    pre3 = norm_pre.reshape(DEPTH, 1, D)
    post3 = norm_post.reshape(DEPTH, 1, D)

    first = _gather_start_cast("gather_start_first", w_in_t, 0, ID_GATHER_FIRST)
    wi0 = _place_other_half("place_w_in0_rest", w_in_t, place_arr, 0, first[2][0])
    (wi1,) = _place_cast("place_w_in1", w_in_t, chip_arr, 288, [1], deps=(first[3],))
    wo = _place_cast("place_w_out", w_out, chip_arr, 256, [0, 1], deps=(first[3],))
    rest = _gather_start("gather_start_rest", [wi1, wo[0], wo[1]], halved=(0, 1), collective_id=ID_GATHER_REST)
    send, recv, bufs = [first[k] + rest[k] for k in range(3)]
    bufs = [wi0, *bufs[1:]]
    order = {(0, "in"): 0, (1, "in"): 1, (0, "out"): 2, (1, "out"): 3}

    saved = []
    packed = [_pack_misc(pool_scale, attn_sinks, norm_pre, norm_post),
              _pack_misc(m_pool_scale, m_attn_sinks, m_norm_pre, m_norm_post),
              _pack_misc(v_pool_scale, v_attn_sinks, v_norm_pre, v_norm_post)]
    after = (first[3], rest[3], pool_w_b, *tables, scale3, pre3, post3, *packed)
    below = None
    for l in range(DEPTH):
        k = order[l, "in"]
        halves = [_gather_wait(f"gather_wait_in{l}", bufs[k], send[k], recv[k], after, halved=True)]
        if below is not None:
            halves.append(below[1])
        w_in_l, *w_out_below = _forward_halves(f"forward_w{l}", halves, collective_id=ID_FORWARD[l])
        if below is None:
            pu, pg, q, kv, ag = _fwd_in(l, xs, pre3, w_in_l)
        else:
            saved[l - 1][9] = w_out_below[0]
            y, xs, pu, pg, q, kv, ag = _fwd_in(l, xs, pre3, w_in_l, (below[0], w_out_below[0], below[2]))
            saved[l - 1][7] = y
        cat = _fwd_mix(l, pu, pg, q, kv, ag, pool_w_b, scale3, attn_sinks, tables)
        k = order[l, "out"]
        w_out_l = _gather_wait(f"gather_wait_out{l}", bufs[k], send[k], recv[k], (cat,), halved=l + 1 < DEPTH)
        saved.append([xs, pu, pg, q, kv, ag, cat, None, w_in_l, w_out_l])
        below, after = (cat, w_out_l, post3), (w_out_l,)

    x_in, pu, pg, q, kv, ag, cat, y, w_in_l, w_out_l = saved[1]
    dcat, dw_out1, dw_out1_b, dg_post1, loss, xs = _bwd_out(1, cat, w_out_l, post3, place_arr, x=x_in, target=target)
    ex1_out = _exchange_start("exchange_start_out1", [dw_out1_b], ID_OUT1)
    dproj, dpw, dsc1, dsink1 = _bwd_mix(1, pu, pg, q, kv, ag, dcat, pool_w_b, scale3, attn_sinks, tables,
                                        deps=(ex1_out[2][0],))
    dx, dg_pre1, dw_in1, dw_in1_b = _bwd_in_dx(1, dproj, w_in_l, x_in, pre3, xs, dw_place=place_arr)
    ex1_in = _exchange_start("exchange_start_in1", [dw_in1_b], ID_IN1)

    x_in, pu, pg, q, kv, ag, cat, y, w_in_l, w_out_l = saved[0]
    dcat, dw_out0, dw_out0_b, dg_post0 = _bwd_out(0, cat, w_out_l, post3, place_arr, dxn=dx, y=y, deps=(ex1_in[2][0],))
    ex0_out = _exchange_start("exchange_start_out0", [dw_out0_b], ID_OUT0)
    dproj, dpw, dsc0, dsink0 = _bwd_mix(0, pu, pg, q, kv, ag, dcat, pool_w_b, scale3, attn_sinks, tables,
                                        deps=(ex0_out[2][0],), dpw_dest=dpw)
    dw_in0, dw_in0_b = _bwd_in_dw(0, dproj, x_in, pre3, place_arr)
    flat = lambda a: a.reshape(DEPTH * 4 * 128, 128)
    ex0_in = _exchange_start("exchange_start_in0", [flat(dpw), dw_in0_b], ID_IN0)

    grad_x, dg_pre0 = _bwd_in_dx(0, dproj, w_in_l, x_in, pre3, dx, deps=(ex0_in[2][1],))
    small = [jnp.concatenate([dsc0, dsc1, dg_pre0, dg_pre1, dg_post0, dg_post1, dsink0, dsink1, loss], axis=0)]
    ex_small = _exchange_start("exchange_start_small", small, ID_SMALL)
    (recv_out1,) = _exchange_wait("exchange_wait_out1", ex1_out, ex_small[2][0])
    (recv_in1,) = _exchange_wait("exchange_wait_in1", ex1_in, recv_out1)
    g_in, g_out = _sum_pieces("sum_pieces_1", [[(1, dw_in1, recv_in1)], [(1, dw_out1, recv_out1)]], place_arr)
    (recv_out0,) = _exchange_wait("exchange_wait_out0", ex0_out, g_out)
    (g_out,) = _sum_pieces("sum_pieces_out0", [[(0, dw_out0, recv_out0)]], place_arr, dests=[g_out])
    g_in, g_out = _share("share_a", [g_in, g_out], [(0, 1), (1, 0), (1, 1)], collective_id=ID_SHARE_A)
    m_in_t, v_in_t = t(m_w_in), t(v_w_in)
    d_out, nm_out, nv_out, grad_w_out = _adamw("adamw_w_out", w_out, g_out, m_w_out, v_w_out, 256)
    upd_in = _adamw("adamw_w_in1", w_in_t, g_in, m_in_t, v_in_t, 288, first=1, count=1, deps=(d_out,))
    dpw_own, recv_pw = _exchange_wait("exchange_wait_pool", ex0_in, upd_in[0], which=[0], with_sent=True)
    (g_pw,) = _sum_small("sum_pool", [dpw_own], [recv_pw], place_arr)

    (recv_in0,) = _exchange_wait("exchange_wait_in0", ex0_in, g_pw, which=[1])
    (recv_misc,) = _exchange_wait("exchange_wait_small", ex_small, recv_in0)
    g_in, g_pw, g_misc = _share(
        "share_b", [g_in], [(0, 0)], [g_pw, lax.empty(small[0].shape, F32)], collective_id=ID_SHARE_B,
        summed=[(("half", 0, 0), dw_in0, recv_in0), (("piece", 1), small[0], recv_misc)])
    d_in, nm_in, nv_in, grad_w_in_t, *small_out = _adamw(
        "adamw_w_in0", w_in_t, g_in, m_in_t, v_in_t, 288, first=0, count=1, dests=upd_in,
        small=(packed[0], g_misc, packed[1], packed[2], flat(pool_w), g_pw, flat(m_pool_w), flat(v_pool_w)))
    (g_sc, g_sk, g_pre, g_post, d_sc, d_sk, d_pre, d_post,
     m_sc, m_sk, m_pre, m_post, v_sc, v_sk, v_pre, v_post, loss_sum) = small_out[:17]
    g_pw, d_pw, m_pw, v_pw = [a.reshape(pool_w.shape) for a in small_out[17:]]
    return (loss_sum[0, 0], grad_x[None], t(grad_w_in_t), g_pw, g_sc, g_sk, grad_w_out, g_pre, g_post,
            t(d_in), d_pw, d_sc, d_sk, d_out, d_pre, d_post,
            t(nm_in), m_pw, m_sc, m_sk, nm_out, m_pre, m_post,
            t(nv_in), v_pw, v_sc, v_sk, nv_out, v_pre, v_post)
```

```python
import jax
import jax.numpy as jnp
from jax import lax
from jax.experimental import pallas as pl
from jax.experimental.pallas import tpu as pltpu

F32 = jnp.float32
BF16 = jnp.bfloat16

S = 2048
D = 1024
DEPTH = 2
D_POOL = 512
POOL_WINDOWS = (2, 4, 8, 16)
N_HEADS = 8
D_IN = 2304
N_SHARDS = 4
W_IN_SHARD = D_IN // N_SHARDS
W_OUT_SHARD = D // N_SHARDS
BLK = 128
NB = S // BLK
HALO = 16
PAD = 8
EPS = 1e-6
NEG_INF = -1e30
C_PU, C_PG, C_Q, C_K, C_V, C_AG = 0, 512, 1024, 1536, 1664, 1792

ADAM_LR = 0.001
ADAM_B1 = 0.9
ADAM_B2 = 0.999
ADAM_EPS = 1e-08
ADAM_WD = 0.01
ADAM_STEP = 10

TM = 512
VMEM_LIMIT = 56 * 1024 * 1024

NT = (((1,), (1,)), ((), ()))
TN = (((0,), (0,)), ((), ()))

MESH = pl.DeviceIdType.MESH
ANY = pl.BlockSpec(memory_space=pl.ANY)

ID_FORWARD = (0, 1)
(ID_SHARE_A, ID_SHARE_B, ID_GATHER_FIRST, ID_GATHER_REST, ID_OUT1, ID_IN1, ID_OUT0, ID_IN0, ID_SMALL) = range(2, 11)

MISC_SCALE, MISC_PRE, MISC_POST, MISC_SINKS, MISC_LOSS = 0, 8, 24, 40, 56
MISC_ROWS = 64


def _params(sem=("arbitrary",)):
    return pltpu.CompilerParams(dimension_semantics=sem, vmem_limit_bytes=VMEM_LIMIT)


def _sigmoid(v):
    return 1.0 / (1.0 + jnp.exp(-v))


def _rows8(v):
    r, c = v.shape
    return v.reshape(r // 8, 8, c).sum(axis=0)


def _layer(l, *shape):
    zeros = (0,) * len(shape)
    return pl.BlockSpec((None,) + shape, lambda i: (l,) + zeros)


def _whole(shape):
    zeros = (0,) * len(shape)
    return pl.BlockSpec(shape, lambda i: zeros, pipeline_mode=pl.Buffered(1))


def _fwd_in(l, x, g_pre, w_in_t, below=None):
    fused = below is not None

    def body(x_ref, g_ref, w_ref, *rest):
        if fused:
            cat_ref, wo_ref, gp_ref, y_ref, xn_ref = rest[:5]
            y = jnp.dot(cat_ref[...], wo_ref[...], preferred_element_type=F32)
            y_ref[...] = y
            xt = x_ref[...] + y * lax.rsqrt(jnp.mean(y * y, axis=-1, keepdims=True) + EPS) * gp_ref[...]
            xn_ref[...] = xt
        else:
            xt = x_ref[...]
        pu_ref, pg_ref, q_ref, kv_ref, ag_ref = rest[-5:]
        r = lax.rsqrt(jnp.mean(xt * xt, axis=-1, keepdims=True) + EPS)
        h = (xt * r * g_ref[...]).astype(BF16)

        def proj(lo, hi):
            return lax.dot_general(h, w_ref[lo:hi, :], NT, preferred_element_type=F32)

        pu_ref[...] = proj(C_PU, C_PG)
        pg_ref[...] = proj(C_PG, C_Q)
        q_ref[...] = proj(C_Q, C_K).astype(BF16)
        kv_ref[...] = proj(C_K, C_AG).astype(BF16)
        ag_ref[...] = proj(C_AG, D_IN)

    row = lambda w: pl.BlockSpec((TM, w), lambda i: (i, 0))
    act = jax.ShapeDtypeStruct((S, D), F32)
    return pl.pallas_call(
        body, name="fwd_out_in" if fused else "fwd_in", grid=(S // TM,),
        in_specs=[row(D), _layer(l, 1, D), _whole((D_IN, D))]
        + ([row(D), _whole((D, D)), _layer(l - 1, 1, D)] if fused else []),
        out_specs=[row(D)] * (2 * fused) + [row(512), row(512), row(512), row(256), row(512)],
        out_shape=[act] * (2 * fused)
        + [jax.ShapeDtypeStruct((S, 512), F32), jax.ShapeDtypeStruct((S, 512), F32),
           jax.ShapeDtypeStruct((S, 512), BF16), jax.ShapeDtypeStruct((S, 256), BF16),
           jax.ShapeDtypeStruct((S, 512), F32)],
        compiler_params=_params(),
    )(x, g_pre, w_in_t, *(below if fused else ()))


LOG2E = 1.4426950408889634
SCORE_SCALE = 0.125 * LOG2E


def _attention_tables():
    qi = jnp.arange(BLK)[:, None]
    kj = jnp.arange(BLK)[None, :]
    dist = ((qi - kj) % BLK).astype(F32)
    slopes = jnp.exp2(-jnp.arange(1, N_HEADS + 1, dtype=F32))
    bias = -(slopes * LOG2E)[:, None, None] * dist[None]
    first = jnp.where(kj > qi, NEG_INF, bias)
    return jnp.stack([first, bias]), (kj <= qi).astype(BF16)


def _own_block_mask():
    return lax.broadcasted_iota(jnp.int32, (BLK, BLK), 1) <= lax.broadcasted_iota(jnp.int32, (BLK, BLK), 0)


def _merge(full, own):
    return jnp.where(own, full[:, BLK:], full[:, :BLK])


def _spread(v, tri):
    own = v * tri
    return jnp.concatenate([v - own, own], axis=1)


def _head_variants(cur, prev):
    both = jnp.concatenate([prev, cur], axis=0).astype(F32)
    swapped = pltpu.roll(both, 64, axis=1)
    low = lax.broadcasted_iota(jnp.int32, both.shape, 1) < 64
    zero = jnp.zeros_like(both)
    return ((jnp.where(low, both, zero).astype(BF16), jnp.where(low, zero, swapped).astype(BF16)),
            (jnp.where(low, swapped, zero).astype(BF16), jnp.where(low, zero, both).astype(BF16)))


def _head_of(hkv, t, half):
    return hkv * 4 + 2 * t + half


def _rows(v, t):
    return v[t * BLK:(t + 1) * BLK]


def _stack_tiles(ref, hkv, offset=0):
    lo = offset + 2 * hkv * 128
    return jnp.concatenate([ref[:, lo:lo + 128], ref[:, lo + 128:lo + 256]], axis=0)


def _scores(q2, k_var, own):
    s = {}
    for hkv in range(2):
        for half in range(2):
            full = lax.dot_general(q2[hkv], k_var[hkv][half], NT, preferred_element_type=F32)
            for t in range(2):
                s[hkv, t, half] = _merge(_rows(full, t), own)
    return s


def _softmax(s, bias, sink):
    s = s * SCORE_SCALE + bias
    sink2 = sink * LOG2E
    m = jnp.maximum(jnp.max(s, axis=-1, keepdims=True), sink2)
    p = jnp.exp2(s - m)
    e_sink = jnp.exp2(sink2 - m)
    inv = 1.0 / (jnp.sum(p, axis=-1, keepdims=True) + e_sink)
    return p * inv, e_sink * inv


def _spread_pair(v, hkv, half, tri):
    return jnp.concatenate([_spread(v[hkv, t, half].astype(BF16), tri) for t in range(2)], axis=0)


POOL_ROWS = PAD + HALO + BLK


def _window_sums(src_ref, tmp_refs, trailing):
    lo, hi = (PAD, POOL_ROWS) if trailing else (0, HALO + BLK)
    cur = src_ref
    for level in range(len(POOL_WINDOWS)):
        lanes = slice(level * 128, 512)
        shift = -(1 << level) if trailing else (1 << level)
        dst = tmp_refs[level % 2]
        dst[lo:hi, lanes] = cur[lo:hi, lanes] + cur[lo + shift:hi + shift, lanes]
        cur = dst


def _pool_block(ext_ref, tmp_refs, i, g, w):
    lanes = slice(g * 128, (g + 1) * 128)
    rows = slice(PAD + HALO, POOL_ROWS)
    t = (i * BLK + lax.broadcasted_iota(jnp.int32, (BLK, 1), 0)).astype(F32)
    inv = 1.0 / jnp.minimum(t + 1.0, float(w))
    return tmp_refs[g % 2][rows, lanes] * inv - ext_ref[rows, lanes], inv


def _fwd_mix(l, pu, pg, q, kv, ag, pool_w, pool_scale, sinks, tables):
    bias, tri = tables

    def body(pu_ref, pup_ref, pg_ref, q_ref, kv_ref, kvp_ref, ag_ref, pw_ref, sc_ref, sink_ref, bias_ref, tri_ref,
             cat_ref, ext_ref, *tmp_refs):
        i = pl.program_id(0)

        @pl.when(i == 0)
        def _():
            for ref in (ext_ref, *tmp_refs):
                ref[0:PAD, :] = jnp.zeros((PAD, 512), F32)

        ext_ref[PAD:PAD + HALO, :] = jnp.where(i > 0, pup_ref[...], 0.0)
        ext_ref[PAD + HALO:POOL_ROWS, :] = pu_ref[...]
        _window_sums(ext_ref, tmp_refs, True)
        for g, w in enumerate(POOL_WINDOWS):
            lanes = slice(g * 128, (g + 1) * 128)
            pooled, _ = _pool_block(ext_ref, tmp_refs, i, g, w)
            mixed = jnp.dot(pooled.astype(BF16), pw_ref[g], preferred_element_type=F32)
            gate = pg_ref[:, lanes]
            cat_ref[:, lanes] = (mixed * sc_ref[:, lanes] * (gate * _sigmoid(gate))).astype(BF16)

        own = _own_block_mask()
        tri = tri_ref[...]
        k_var = _head_variants(kv_ref[:, 0:128], kvp_ref[:, 0:128])
        v_var = _head_variants(kv_ref[:, 128:256], kvp_ref[:, 128:256])
        s = _scores([_stack_tiles(q_ref, hkv) for hkv in range(2)], k_var, own)
        p = {}
        for (hkv, t, half), s_head in s.items():
            head = _head_of(hkv, t, half)
            p[hkv, t, half], _ = _softmax(s_head, bias_ref[head], sink_ref[l, head])
        for hkv in range(2):
            o2 = jnp.zeros((2 * BLK, 128), F32)
            for half in range(2):
                o2 = o2 + jnp.dot(_spread_pair(p, hkv, half, tri), v_var[hkv][half], preferred_element_type=F32)
            for t in range(2):
                lo = (2 * hkv + t) * 128
                gate = ag_ref[:, lo:lo + 128]
                cat_ref[:, D_POOL + lo:D_POOL + lo + 128] = (_rows(o2, t) * (gate * _sigmoid(gate))).astype(BF16)

    blk = lambda w: pl.BlockSpec((BLK, w), lambda i: (i, 0))
    prev = lambda w: pl.BlockSpec((BLK, w), lambda i: (jnp.maximum(i - 1, 0), 0))
    halo = pl.BlockSpec((HALO, 512), lambda i: (jnp.maximum(i * (BLK // HALO) - 1, 0), 0))
    return pl.pallas_call(
        body, name="fwd_mix", grid=(NB,),
        in_specs=[blk(512), halo, blk(512), blk(512), blk(256), prev(256), blk(512),
                  _layer(l, 4, 128, 128), _layer(l, 1, 512), pl.BlockSpec(memory_space=pltpu.SMEM),
                  pl.BlockSpec((None, N_HEADS, BLK, BLK), lambda i: (jnp.minimum(i, 1), 0, 0, 0)), _whole((BLK, BLK))],
        out_specs=blk(D),
        out_shape=jax.ShapeDtypeStruct((S, D), BF16),
        scratch_shapes=[pltpu.VMEM((POOL_ROWS, 512), F32)] * 3,
        compiler_params=_params(),
    )(pu, pu, pg, q, kv, kv, ag, pool_w, pool_scale, sinks, bias, tri)


def _store_lane_rows(ref, acc):
    total = jnp.sum(acc, axis=0, keepdims=True)
    for k in range(ref.shape[0]):
        ref[k:k + 1, :] = total[:, k * 128:(k + 1) * 128]


def _own_piece(dw_ref, place_ref):
    p = dw_ref.shape[0] // 8
    return dw_ref[pl.ds(pl.multiple_of(place_ref[0] * p, 8), p), :]


def _bwd_out(l, cat, w_out, g_post, place_arr, dxn=None, y=None, x=None, target=None, deps=()):
    last = target is not None
    n_steps = S // TM

    def body(a_ref, b_ref, g_ref, cat_ref, w_ref, place_ref, *rest):
        dcat_ref, own_ref, dwb_ref, dg_ref = rest[len(deps):len(deps) + 4]
        rest = rest[len(deps) + 4:]
        acc_ref, dw_ref = rest[-2:]
        step = pl.program_id(0)

        @pl.when(step == 0)
        def _():
            dw_ref[...] = jnp.zeros_like(dw_ref)
            acc_ref[...] = jnp.zeros_like(acc_ref)

        cat = cat_ref[...]
        g = g_ref[...]
        y = jnp.dot(cat, w_ref[...], preferred_element_type=F32) if last else b_ref[...]
        r = lax.rsqrt(jnp.mean(y * y, axis=-1, keepdims=True) + EPS)
        if last:
            loss_ref, dx_ref, loss_acc_ref = rest[:3]
            err = a_ref[...] + y * r * g - b_ref[...]

            @pl.when(step == 0)
            def _():
                loss_acc_ref[...] = jnp.zeros_like(loss_acc_ref)

            loss_acc_ref[...] += _rows8(err * err)
            dz = err * (1.0 / D)
            dx_ref[...] = dz
        else:
            dz = a_ref[...]
        a = dz * g
        dy = r * a - y * (r * r * r) * jnp.mean(a * y, axis=-1, keepdims=True)
        acc_ref[...] += _rows8(dz * (y * r))
        dyb = dy.astype(BF16)
        dcat_ref[...] = lax.dot_general(dyb, w_ref[...], NT, preferred_element_type=F32)
        dw_ref[...] += lax.dot_general(cat, dyb, TN, preferred_element_type=F32)

        @pl.when(step == n_steps - 1)
        def _():
            _store_lane_rows(dg_ref, acc_ref[...])
            dwb_ref[...] = dw_ref[...].astype(BF16)
            own_ref[...] = _own_piece(dw_ref, place_ref)
            if last:
                loss_ref[...] = jnp.full((8, 128), (0.5 / D) * jnp.sum(loss_acc_ref[...]), F32)

    row = lambda: pl.BlockSpec((TM, D), lambda i: (i, 0))
    full = _whole
    return pl.pallas_call(
        body, name="out_loss_bwd" if last else "bwd_out", grid=(n_steps,),
        in_specs=[row(), row(), _layer(l, 1, D), row(), full((D, D)), pl.BlockSpec(memory_space=pltpu.SMEM)]
        + [ANY] * len(deps),
        out_specs=[row(), full((D // 8, D)), full((D, D)), full((8, 128))] + ([full((8, 128)), row()] if last else []),
        out_shape=[jax.ShapeDtypeStruct((S, D), F32), jax.ShapeDtypeStruct((D // 8, D), F32),
                   jax.ShapeDtypeStruct((D, D), BF16), jax.ShapeDtypeStruct((8, 128), F32)]
        + ([jax.ShapeDtypeStruct((8, 128), F32), jax.ShapeDtypeStruct((S, D), F32)] if last else []),
        scratch_shapes=([pltpu.VMEM((8, D), F32)] if last else []) + [pltpu.VMEM((8, D), F32), pltpu.VMEM((D, D), F32)],
        compiler_params=_params(),
    )(*((x, target) if last else (dxn, y)), g_post, cat, w_out, place_arr, *deps)


def _bwd_mix(l, pu, pg, q, kv, ag, dcat, pool_w, pool_scale, sinks, tables, deps=(), dpw_dest=None):
    bias, tri = tables
    deps = tuple(deps) + (() if dpw_dest is None else (dpw_dest,))

    def body(pu_ref, pup_ref, pg_ref, q_ref, kv_ref, kvp_ref, ag_ref, dcat_ref, pw_ref, sc_ref, sink_ref, bias_ref,
             tri_ref, *rest):
        dproj_ref, dpw_ref, dsc_ref, dsink_ref, ext_ref, dext_ref, tmp_a, tmp_b, dkv_ref = rest[len(deps):]
        tmp_refs = (tmp_a, tmp_b)
        step = pl.program_id(0)
        i = NB - 1 - step

        @pl.when(step == 0)
        def _():
            dpw_ref[...] = jnp.zeros_like(dpw_ref)
            dsc_ref[...] = jnp.zeros_like(dsc_ref)
            dsink_ref[...] = jnp.zeros_like(dsink_ref)
            for ref in (ext_ref, tmp_a, tmp_b):
                ref[0:PAD, :] = jnp.zeros((PAD, 512), F32)
            dext_ref[BLK:POOL_ROWS, :] = jnp.zeros((HALO + PAD, 512), F32)
            dkv_ref[...] = jnp.zeros_like(dkv_ref)

        ext_ref[PAD:PAD + HALO, :] = jnp.where(i > 0, pup_ref[...], 0.0)
        ext_ref[PAD + HALO:POOL_ROWS, :] = pu_ref[...]
        _window_sums(ext_ref, tmp_refs, True)
        dpooled = []
        for g, w in enumerate(POOL_WINDOWS):
            lanes = slice(g * 128, (g + 1) * 128)
            pooled, inv = _pool_block(ext_ref, tmp_refs, i, g, w)
            pooled_b = pooled.astype(BF16)
            mixed = jnp.dot(pooled_b, pw_ref[g], preferred_element_type=F32)
            scale = sc_ref[:, lanes]
            gate = pg_ref[:, lanes]
            sg = _sigmoid(gate)
            dpo = dcat_ref[:, lanes]
            dproj_ref[:, C_PG + g * 128:C_PG + (g + 1) * 128] = (
                dpo * (mixed * scale) * (sg * (1.0 + gate * (1.0 - sg)))).astype(BF16)
            dms = dpo * (gate * sg)
            dsc_ref[g:g + 1, :] += jnp.sum(dms * mixed, axis=0, keepdims=True)
            dmixed = (dms * scale).astype(BF16)
            dpw_ref[g] += lax.dot_general(pooled_b, dmixed, TN, preferred_element_type=F32)
            dpooled.append(lax.dot_general(dmixed, pw_ref[g], NT, preferred_element_type=F32))
            dext_ref[0:BLK, lanes] = dpooled[g] * inv
        _window_sums(dext_ref, tmp_refs, False)
        for g in range(len(POOL_WINDOWS)):
            lanes = slice(g * 128, (g + 1) * 128)
            dproj_ref[:, C_PU + g * 128:C_PU + (g + 1) * 128] = (tmp_refs[g % 2][0:BLK, lanes] - dpooled[g]).astype(BF16)
        dext_ref[BLK:BLK + HALO, :] = dext_ref[0:HALO, :]

        own = _own_block_mask()
        tri = tri_ref[...]
        k_var = _head_variants(kv_ref[:, 0:128], kvp_ref[:, 0:128])
        v_var = _head_variants(kv_ref[:, 128:256], kvp_ref[:, 128:256])
        q2 = [_stack_tiles(q_ref, hkv) for hkv in range(2)]
        s = _scores(q2, k_var, own)
        p, p_sink = {}, {}
        for key, s_head in s.items():
            head = _head_of(*key)
            p[key], p_sink[key] = _softmax(s_head, bias_ref[head], sink_ref[l, head])

        do2, p_b, dp = [], {}, {}
        for hkv in range(2):
            gate = _stack_tiles(ag_ref, hkv)
            sg = _sigmoid(gate)
            dca = _stack_tiles(dcat_ref, hkv, D_POOL)
            do2.append((dca * (gate * sg)).astype(BF16))
            o2 = jnp.zeros((2 * BLK, 128), F32)
            for half in range(2):
                p_b[hkv, half] = _spread_pair(p, hkv, half, tri)
                o2 = o2 + jnp.dot(p_b[hkv, half], v_var[hkv][half], preferred_element_type=F32)
                full = lax.dot_general(do2[hkv], v_var[hkv][half], NT, preferred_element_type=F32)
                for t in range(2):
                    dp[hkv, t, half] = _merge(_rows(full, t), own)
            dag = dca * o2 * (sg * (1.0 + gate * (1.0 - sg)))
            for t in range(2):
                lo = C_AG + (2 * hkv + t) * 128
                dproj_ref[:, lo:lo + 128] = _rows(dag, t).astype(BF16)

        ds = {}
        for key in p:
            delta = jnp.sum(p[key] * dp[key], axis=-1, keepdims=True)
            ds[key] = p[key] * (dp[key] - delta)
            head = _head_of(*key)
            dsink_ref[0:1, :] += jnp.where(lax.broadcasted_iota(jnp.int32, (1, 128), 1) == head,
                                           -jnp.sum(p_sink[key] * delta, axis=0, keepdims=True), 0.0)

        dk_acc = [[None, None], [None, None]]
        dv_acc = [[None, None], [None, None]]
        for hkv in range(2):
            dq2 = jnp.zeros((2 * BLK, 128), F32)
            for half in range(2):
                ds_b = _spread_pair(ds, hkv, half, tri)
                dq2 = dq2 + jnp.dot(ds_b, k_var[hkv][half], preferred_element_type=F32)
                dk_acc[hkv][half] = lax.dot_general(ds_b, q2[hkv], TN, preferred_element_type=F32)
                dv_acc[hkv][half] = lax.dot_general(p_b[hkv, half], do2[hkv], TN, preferred_element_type=F32)
            for t in range(2):
                lo = C_Q + (2 * hkv + t) * 128
                dproj_ref[:, lo:lo + 128] = (_rows(dq2, t) * 0.125).astype(BF16)

        low = lax.broadcasted_iota(jnp.int32, (2 * BLK, 128), 1) < 64

        def gather_heads(acc):
            return jnp.where(low, acc[0][0] + pltpu.roll(acc[0][1], 64, axis=1),
                             pltpu.roll(acc[1][0], 64, axis=1) + acc[1][1])

        dk = gather_heads(dk_acc) * 0.125
        dv = gather_heads(dv_acc)
        dproj_ref[:, C_K:C_V] = (dk[BLK:, :] + dkv_ref[:, 0:128]).astype(BF16)
        dproj_ref[:, C_V:C_AG] = (dv[BLK:, :] + dkv_ref[:, 128:256]).astype(BF16)
        dkv_ref[:, 0:128] = dk[:BLK, :]
        dkv_ref[:, 128:256] = dv[:BLK, :]

    rev = lambda w: pl.BlockSpec((BLK, w), lambda s: (NB - 1 - s, 0))
    prev = lambda w: pl.BlockSpec((BLK, w), lambda s: (jnp.maximum(NB - 2 - s, 0), 0))
    halo = pl.BlockSpec((HALO, 512), lambda s: (jnp.maximum((NB - 1 - s) * (BLK // HALO) - 1, 0), 0))
    return pl.pallas_call(
        body, name="bwd_mix", grid=(NB,),
        in_specs=[rev(512), halo, rev(512), rev(512), rev(256), prev(256), rev(512), rev(D),
                  _layer(l, 4, 128, 128), _layer(l, 1, 512), pl.BlockSpec(memory_space=pltpu.SMEM),
                  pl.BlockSpec((None, N_HEADS, BLK, BLK), lambda s: (jnp.minimum(NB - 1 - s, 1), 0, 0, 0)),
                  _whole((BLK, BLK))] + [ANY] * len(deps),
        out_specs=[rev(D_IN), _layer(l, 4, 128, 128),
                   pl.BlockSpec((4, 128), lambda s: (0, 0)), pl.BlockSpec((8, 128), lambda s: (0, 0))],
        out_shape=[jax.ShapeDtypeStruct((S, D_IN), BF16), jax.ShapeDtypeStruct((DEPTH, 4, 128, 128), F32),
                   jax.ShapeDtypeStruct((4, 128), F32), jax.ShapeDtypeStruct((8, 128), F32)],
        input_output_aliases={} if dpw_dest is None else {12 + len(deps): 1},
        scratch_shapes=[pltpu.VMEM((POOL_ROWS, 512), F32)] * 4 + [pltpu.VMEM((BLK, 256), F32)],
        compiler_params=_params(),
    )(pu, pu, pg, q, kv, kv, ag, dcat, pool_w, pool_scale, sinks, bias, tri, *deps)


def _bwd_in_dw(l, dproj, x, g_pre, place_arr, deps=()):
    n_steps = S // TM

    def body(dp_ref, x_ref, g_ref, place_ref, *rest):
        own_ref, dwb_ref, dw_ref = rest[len(deps):]
        step = pl.program_id(0)

        @pl.when(step == 0)
        def _():
            dw_ref[...] = jnp.zeros_like(dw_ref)

        xt = x_ref[...]
        r = lax.rsqrt(jnp.mean(xt * xt, axis=-1, keepdims=True) + EPS)
        h = (xt * r * g_ref[...]).astype(BF16)
        dw_ref[...] += lax.dot_general(dp_ref[...], h, TN, preferred_element_type=F32)

        @pl.when(step == n_steps - 1)
        def _():
            dwb_ref[...] = dw_ref[...].astype(BF16)
            own_ref[...] = _own_piece(dw_ref, place_ref)

    row = lambda w: pl.BlockSpec((TM, w), lambda i: (i, 0))
    full = _whole
    return pl.pallas_call(
        body, name="bwd_in_dw", grid=(n_steps,),
        in_specs=[row(D_IN), row(D), _layer(l, 1, D), pl.BlockSpec(memory_space=pltpu.SMEM)] + [ANY] * len(deps),
        out_specs=[full((D_IN // 8, D)), full((D_IN, D))],
        out_shape=[jax.ShapeDtypeStruct((D_IN // 8, D), F32), jax.ShapeDtypeStruct((D_IN, D), BF16)],
        scratch_shapes=[pltpu.VMEM((D_IN, D), F32)],
        compiler_params=_params(),
    )(dproj, x, g_pre, place_arr, *deps)


def _bwd_in_dx(l, dproj, w_in_t, x, g_pre, dres, deps=(), dw_place=None):
    n_steps = S // TM
    with_dw = dw_place is not None

    def body(dp_ref, w_ref, x_ref, g_ref, dres_ref, *rest):
        place_ref = rest[0] if with_dw else None
        rest = rest[with_dw + len(deps):]
        if with_dw:
            dx_ref, dg_ref, own_ref, dwb_ref, acc_ref, dw_ref = rest
        else:
            dx_ref, dg_ref, acc_ref = rest
        step = pl.program_id(0)

        @pl.when(step == 0)
        def _():
            acc_ref[...] = jnp.zeros_like(acc_ref)
            if with_dw:
                dw_ref[...] = jnp.zeros_like(dw_ref)

        g = g_ref[...]
        halves = [slice(k * (TM // 2), (k + 1) * (TM // 2)) for k in range(2)]
        dh = [jnp.dot(dp_ref[rows, :], w_ref[...], preferred_element_type=F32) for rows in halves]
        h = []
        for rows, dh_k in zip(halves, dh):
            xt = x_ref[rows, :]
            r = lax.rsqrt(jnp.mean(xt * xt, axis=-1, keepdims=True) + EPS)
            xn = xt * r
            acc_ref[...] += _rows8(dh_k * xn)
            a = dh_k * g
            dx_ref[rows, :] = dres_ref[rows, :] + (
                r * a - xt * (r * r * r) * jnp.mean(a * xt, axis=-1, keepdims=True))
            h.append((xn * g).astype(BF16))
        if with_dw:
            dw_ref[...] += lax.dot_general(dp_ref[...], jnp.concatenate(h, axis=0), TN, preferred_element_type=F32)

        @pl.when(step == n_steps - 1)
        def _():
            _store_lane_rows(dg_ref, acc_ref[...])
            if with_dw:
                dwb_ref[...] = dw_ref[...].astype(BF16)
                own_ref[...] = _own_piece(dw_ref, place_ref)

    row = lambda w: pl.BlockSpec((TM, w), lambda i: (i, 0))
    full = _whole
    dw_specs = [full((D_IN // 8, D)), full((D_IN, D))] if with_dw else []
    dw_shapes = [jax.ShapeDtypeStruct((D_IN // 8, D), F32), jax.ShapeDtypeStruct((D_IN, D), BF16)] if with_dw else []
    return pl.pallas_call(
        body, name="bwd_in" if with_dw else "bwd_in_dx", grid=(n_steps,),
        in_specs=[row(D_IN), full((D_IN, D)), row(D), _layer(l, 1, D), row(D)]
        + [pl.BlockSpec(memory_space=pltpu.SMEM)] * with_dw + [ANY] * len(deps),
        out_specs=[row(D), full((8, 128))] + dw_specs,
        out_shape=[jax.ShapeDtypeStruct((S, D), F32), jax.ShapeDtypeStruct((8, 128), F32)] + dw_shapes,
        scratch_shapes=[pltpu.VMEM((8, D), F32)] + [pltpu.VMEM((D_IN, D), F32)] * with_dw,
        compiler_params=_params(),
    )(dproj, w_in_t, x, g_pre, dres, *((dw_place,) if with_dw else ()), *deps)


HBM =pl.BlockSpec(memory_space=pltpu.HBM)
SEM = pl.BlockSpec(memory_space=pltpu.SEMAPHORE)
def _split_copy(collective_id=None):
    return pltpu.CompilerParams(has_side_effects=pltpu.SideEffectType.DATAFLOW_SIDE_EFFECTING,
                                collective_id=collective_id)


SPLIT_COPY = _split_copy()


def _in_hbm(a):
    return pltpu.with_memory_space_constraint(a, pltpu.HBM)

def _place():
    return lax.axis_index("x"), lax.axis_index("y"), lax.axis_index("c")


def _other_chips(x, y):
    return [(1 - x, y), (x, 1 - y), (1 - x, 1 - y)]


def _peer(x, y, c, m):
    return (x ^ (m >> 2), y ^ ((m >> 1) & 1), c ^ (m & 1))


SAME_CORE = (2, 4, 6)


def _place_cast(name, src, chip_arr, tile, layers, deps=()):
    _, n, cols = src.shape
    steps = n // tile
    k = len(layers)

    def body(chip_ref, *refs):
        for s_ref, o_ref in zip(refs[:k], refs[k + len(deps):]):
            o_ref[...] = s_ref[...].astype(BF16)

    def layer_spec(l):
        return pl.BlockSpec((None, tile, cols), lambda i, chip: (l, i, 0))

    return pl.pallas_call(
        body, name=name,
        grid_spec=pltpu.PrefetchScalarGridSpec(
            num_scalar_prefetch=1, grid=(steps,),
            in_specs=[layer_spec(l) for l in layers] + [ANY] * len(deps),
            out_specs=[pl.BlockSpec((tile, cols), lambda i, chip: (chip[0] * steps + i, 0))] * k),
        out_shape=[jax.ShapeDtypeStruct((N_SHARDS * n, cols), BF16)] * k,
        compiler_params=_params(),
    )(chip_arr, *[src] * k, *deps)


def _place_other_half(name, src, place_arr, layer, dest):
    _, n, cols = src.shape

    def body(place_ref, s_ref, dest_ref, o_ref):
        o_ref[...] = s_ref[...].astype(BF16)

    return pl.pallas_call(
        body, name=name,
        grid_spec=pltpu.PrefetchScalarGridSpec(
            num_scalar_prefetch=1, grid=(1,),
            in_specs=[pl.BlockSpec((None, n // 2, cols), lambda i, place: (layer, 1 - place[1], 0)), ANY],
            out_specs=pl.BlockSpec((n // 2, cols), lambda i, place: (place[0] + 1 - 2 * place[1], 0))),
        out_shape=jax.ShapeDtypeStruct((N_SHARDS * n, cols), BF16),
        input_output_aliases={2: 0},
        compiler_params=_params(),
    )(place_arr, src, dest)


def _chip_rows(ref, chip, half=None):
    n = ref.shape[0] // N_SHARDS
    if half is None:
        return ref.at[pl.ds(pl.multiple_of(chip * n, 16), n), :]
    return ref.at[pl.ds(pl.multiple_of(chip * n + half * (n // 2), 16), n // 2), :]


def _gather_start(name, bufs, halved, collective_id):
    n = len(bufs)

    def body(*refs):
        ins, send, recv, token = refs[:n], refs[n:2 * n], refs[2 * n:3 * n], refs[-1]
        x, y, c = _place()
        _handshake([(*chip, c) for chip in _other_chips(x, y)])
        for a, buf in enumerate(ins):
            own = _chip_rows(buf, 2 * x + y, c if a in halved else None)
            for j, chip in enumerate(_other_chips(x, y)):
                pltpu.make_async_remote_copy(src_ref=own, dst_ref=own, send_sem=send[a].at[j], recv_sem=recv[a].at[j],
                                             device_id=(*chip, c), device_id_type=MESH).start()
        token[...] = jnp.zeros_like(token)

    outs = pl.pallas_call(
        body, name=name, in_specs=[HBM] * n,
        out_specs=[SEM] * (2 * n) + [HBM] * n + [pl.BlockSpec(memory_space=pltpu.VMEM)],
        out_shape=[pltpu.SemaphoreType.DMA((3,))] * (2 * n) + [pltpu.HBM(b.shape, b.dtype) for b in bufs]
        + [jax.ShapeDtypeStruct((8, 128), F32)],
        input_output_aliases={a: 2 * n + a for a in range(n)},
        compiler_params=_split_copy(collective_id),
    )(*[_in_hbm(b) for b in bufs])
    return outs[:n], outs[n:2 * n], outs[2 * n:3 * n], outs[-1]


def _gather_start_cast(name, src, layer, collective_id):
    _, n, cols = src.shape
    half = n // 2

    def body(src_ref, send, recv, buf_ref, token, f32_ref, bf16_ref, local):
        x, y, c = _place()
        own = _chip_rows(buf_ref, 2 * x + y, c)

        def cast():
            load = pltpu.make_async_copy(src_ref.at[layer, pl.ds(pl.multiple_of(c * half, 16), half), :], f32_ref,
                                         local.at[0])
            load.start()
            load.wait()
            bf16_ref[...] = f32_ref[...].astype(BF16)
            store = pltpu.make_async_copy(bf16_ref, own, local.at[1])
            store.start()
            store.wait()

        _handshake([(*chip, c) for chip in _other_chips(x, y)], cast)
        for j, chip in enumerate(_other_chips(x, y)):
            pltpu.make_async_remote_copy(src_ref=own, dst_ref=own, send_sem=send.at[j], recv_sem=recv.at[j],
                                         device_id=(*chip, c), device_id_type=MESH).start()
        token[...] = jnp.zeros_like(token)

    outs = pl.pallas_call(
        body, name=name, in_specs=[HBM],
        out_specs=[SEM, SEM, HBM, pl.BlockSpec(memory_space=pltpu.VMEM)],
        out_shape=[pltpu.SemaphoreType.DMA((3,))] * 2 + [pltpu.HBM((N_SHARDS * n, cols), BF16),
                                                         jax.ShapeDtypeStruct((8, 128), F32)],
        scratch_shapes=[pltpu.VMEM((half, cols), F32), pltpu.VMEM((half, cols), BF16), pltpu.SemaphoreType.DMA((2,))],
        compiler_params=_split_copy(collective_id),
    )(_in_hbm(src))
    return outs[:1], outs[1:2], outs[2:3], outs[3]


def _gather_wait(name, buf, send_sem, recv_sem, after, halved=False):
    def body(buf_ref, send_ref, recv_ref, *rest):
        x, y, c = _place()
        half = c if halved else None
        own = _chip_rows(buf_ref, 2 * x + y, half)
        for j, chip in enumerate(_other_chips(x, y)):
            copy = pltpu.make_async_remote_copy(src_ref=own, dst_ref=_chip_rows(buf_ref, 2 * chip[0] + chip[1], half),
                                                send_sem=send_ref.at[j], recv_sem=recv_ref.at[j],
                                                device_id=(*chip, c), device_id_type=MESH)
            copy.wait_send()
            copy.wait_recv()

    return pl.pallas_call(
        body, name=name, in_specs=[HBM, SEM, SEM] + [ANY] * len(after), out_specs=HBM,
        out_shape=pltpu.HBM(buf.shape, buf.dtype), input_output_aliases={0: 0}, compiler_params=SPLIT_COPY,
    )(buf, send_sem, recv_sem, *after)


def _handshake(peers, meanwhile=None):
    barrier = pltpu.get_barrier_semaphore()
    for peer in peers:
        pl.semaphore_signal(barrier, inc=1, device_id=peer, device_id_type=MESH)
    if meanwhile is not None:
        meanwhile()
    pl.semaphore_wait(barrier, len(peers))


def _sibling_handshake(x, y, c):
    _handshake([(x, y, 1 - c)])


def _forward_halves(name, bufs, collective_id):
    n = len(bufs)

    def body(*refs):
        ins, outs, (send_sems, recv_sems) = refs[:n], refs[n:2 * n], refs[2 * n:]
        x, y, c = _place()
        _sibling_handshake(x, y, c)

        def copy(a, j, chip, half):
            rows = 2 * chip[0] + chip[1]
            return pltpu.make_async_remote_copy(
                src_ref=_chip_rows(ins[a], rows, half), dst_ref=_chip_rows(outs[a], rows, half),
                send_sem=send_sems.at[3 * a + j], recv_sem=recv_sems.at[3 * a + j], device_id=(x, y, 1 - c),
                device_id_type=MESH)

        copies = [(a, j, chip) for a in range(n) for j, chip in enumerate(_other_chips(x, y))]
        for a, j, chip in copies:
            copy(a, j, chip, c).start()
        for a, j, chip in copies:
            copy(a, j, chip, c).wait_send()
            copy(a, j, chip, 1 - c).wait_recv()

    return pl.pallas_call(
        body, name=name, in_specs=[ANY] * n, out_specs=[ANY] * n,
        out_shape=[jax.ShapeDtypeStruct(b.shape, b.dtype) for b in bufs],
        input_output_aliases={a: a for a in range(n)},
        scratch_shapes=[pltpu.SemaphoreType.DMA((3 * n,))] * 2,
        compiler_params=pltpu.CompilerParams(collective_id=collective_id),
    )(*bufs)


def _piece_rows(ref, k):
    p = ref.shape[0] // 8
    return ref.at[pl.ds(pl.multiple_of(k * p, 32 // jnp.dtype(ref.dtype).itemsize), p), :]


def _exchange_start(name, arrays, collective_id):
    n = len(arrays)
    zones = [lax.empty((7, a.shape[0] // 8, a.shape[1]), a.dtype) for a in arrays]

    def body(*refs):
        srcs, lands = refs[:n], refs[n:2 * n]
        send, recv = refs[2 * n:3 * n], refs[3 * n:4 * n]
        x, y, c = _place()
        _handshake([_peer(x, y, c, m) for m in range(1, 8)])
        for a, (src, land) in enumerate(zip(srcs, lands)):
            for m in range(1, 8):
                px, py, pc = _peer(x, y, c, m)
                pltpu.make_async_remote_copy(
                    src_ref=_piece_rows(src, 4 * px + 2 * py + pc), dst_ref=land.at[m - 1], send_sem=send[a].at[m - 1],
                    recv_sem=recv[a].at[m - 1], device_id=(px, py, pc), device_id_type=MESH).start()

    outs = pl.pallas_call(
        body, name=name, in_specs=[HBM] * (2 * n), out_specs=[SEM] * (2 * n) + [HBM] * (2 * n),
        out_shape=[pltpu.SemaphoreType.DMA((7,))] * (2 * n) + [pltpu.HBM(a.shape, a.dtype) for a in arrays + zones],
        input_output_aliases={a: 2 * n + a for a in range(2 * n)},
        compiler_params=_split_copy(collective_id),
    )(*[_in_hbm(a) for a in arrays + zones])
    return outs[:n], outs[n:2 * n], outs[2 * n:3 * n], outs[3 * n:4 * n]


def _exchange_wait(name, started, after, which=None, with_sent=False):
    which = range(len(started[2])) if which is None else which
    send_sems, recv_sems, arrays, zones = [[group[k] for k in which] for group in started[:4]]
    n = len(arrays)

    def body(*refs):
        srcs, lands = refs[:n], refs[n:2 * n]
        send, recv = refs[2 * n:3 * n], refs[3 * n:4 * n]
        x, y, c = _place()
        for a, (src, land) in enumerate(zip(srcs, lands)):
            for m in range(1, 8):
                px, py, pc = _peer(x, y, c, m)
                copy = pltpu.make_async_remote_copy(
                    src_ref=_piece_rows(src, 4 * px + 2 * py + pc), dst_ref=land.at[m - 1], send_sem=send[a].at[m - 1],
                    recv_sem=recv[a].at[m - 1], device_id=(px, py, pc), device_id_type=MESH)
                copy.wait_send()
                copy.wait_recv()

    outs = pl.pallas_call(
        body, name=name, in_specs=[HBM] * (2 * n) + [SEM] * (2 * n) + [ANY], out_specs=[HBM] * (2 * n),
        out_shape=[pltpu.HBM(a.shape, a.dtype) for a in list(arrays) + list(zones)],
        input_output_aliases={a: a for a in range(2 * n)}, compiler_params=SPLIT_COPY,
    )(*arrays, *zones, *send_sems, *recv_sems, after)
    return outs if with_sent else outs[n:]


def _sum_pieces(name, weights, place_arr, dests=None):
    steps = 2
    flat = [item for items in weights for item in items]
    n = len(flat)

    def body(place_ref, *refs):
        outs = iter(refs[len(refs) - len(weights):])
        k = 0
        for items in weights:
            out_ref = next(outs)
            for layer, _, _ in items:
                total = refs[k][...]
                for m in range(7):
                    total = total + refs[n + k][m].astype(F32)
                if len(items) == DEPTH:
                    out_ref[layer] = total
                else:
                    out_ref[...] = total
                k += 1

    def out_spec(items):
        _, own, _ = items[0]
        t, cols = own.shape[0] // steps, own.shape[1]
        if len(items) == DEPTH:
            return pl.BlockSpec((DEPTH, t, cols), lambda i, place: (0, place[1] * steps + i, 0))
        layer = items[0][0]
        return pl.BlockSpec((None, t, cols), lambda i, place: (layer, place[1] * steps + i, 0))

    owns = [own for _, own, _ in flat]
    dests = [] if dests is None else list(dests)
    return pl.pallas_call(
        body, name=name,
        grid_spec=pltpu.PrefetchScalarGridSpec(
            num_scalar_prefetch=1, grid=(steps,),
            in_specs=[pl.BlockSpec((o.shape[0] // steps, o.shape[1]), lambda i, place: (i, 0)) for o in owns]
            + [pl.BlockSpec((7, o.shape[0] // steps, o.shape[1]), lambda i, place: (0, i, 0)) for o in owns]
            + [ANY] * len(dests),
            out_specs=[out_spec(items) for items in weights]),
        out_shape=[jax.ShapeDtypeStruct((DEPTH, 2 * items[0][1].shape[0], items[0][1].shape[1]), F32)
                   for items in weights],
        input_output_aliases={1 + 2 * n + k: k for k in range(len(dests))},
        compiler_params=_params(),
    )(place_arr, *owns, *[recv for _, _, recv in flat], *dests)


def _sum_small(name, partials, recvs, place_arr):
    n = len(partials)

    def body(place_ref, *refs):
        for o_ref, r_ref, out_ref in zip(refs[:n], refs[n:2 * n], refs[2 * n:]):
            total = o_ref[...]
            for m in range(7):
                total = total + r_ref[m]
            out_ref[...] = total

    piece = lambda a: pl.BlockSpec((a.shape[0] // 8, a.shape[1]), lambda i, place: (place[0], 0))
    return pl.pallas_call(
        body, name=name,
        grid_spec=pltpu.PrefetchScalarGridSpec(
            num_scalar_prefetch=1, grid=(1,),
            in_specs=[piece(a) for a in partials] + [pl.BlockSpec(r.shape, lambda i, place: (0, 0, 0)) for r in recvs],
            out_specs=[piece(a) for a in partials]),
        out_shape=[jax.ShapeDtypeStruct(a.shape, F32) for a in partials],
        compiler_params=_params(),
    )(place_arr, *partials, *recvs)


def _share(name, bufs, parts, gathered=(), collective_id=None, summed=()):
    n, n_g, n_s = len(bufs), len(gathered), len(summed)
    total = n + n_g
    made = [target for target, _, _ in summed]

    def body(*refs):
        ins, extra, outs = refs[:total], refs[total:total + 2 * n_s], refs[total + 2 * n_s:2 * total + 2 * n_s]
        send_sems, recv_sems, send_g, recv_g = refs[2 * total + 2 * n_s:2 * total + 2 * n_s + 4]
        scratch = refs[2 * total + 2 * n_s + 4:]
        x, y, c = _place()

        def half(ref, l, which):
            p = ref.shape[1] // 2
            return ref.at[l, pl.ds(pl.multiple_of(which * p, 8), p), :]

        def sums():
            local, acc, got = scratch[0], scratch[1:1 + n_s], scratch[1 + n_s:]
            loads, stores = [], []
            for j, (target, _, _) in enumerate(summed):
                own_rows = extra[2 * j] if target[0] == "half" else _piece_rows(extra[2 * j], 4 * x + 2 * y + c)
                loads += [pltpu.make_async_copy(own_rows, acc[j], local.at[2 * j]),
                          pltpu.make_async_copy(extra[2 * j + 1], got[j], local.at[2 * j + 1])]
            for load in loads:
                load.start()
            for j, (target, _, _) in enumerate(summed):
                loads[2 * j].wait()
                loads[2 * j + 1].wait()
                rows = acc[j].shape[0]
                step = min(rows, 96)
                for r in range(0, rows, step):
                    part = acc[j][r:r + step, :]
                    for m in range(7):
                        part = part + got[j][m, r:r + step, :].astype(F32)
                    acc[j][r:r + step, :] = part
                if target[0] == "half":
                    dest = half(outs[target[1]], target[2], c)
                else:
                    dest = _piece_rows(outs[n + target[1]], 4 * x + 2 * y + c)
                stores.append(pltpu.make_async_copy(acc[j], dest, local.at[2 * j]))
                stores[-1].start()
            for store in stores:
                store.wait()

        _handshake([_peer(x, y, c, m) for m in (SAME_CORE if gathered else ()) + (1,)], sums if summed else None)

        def swap(k, which):
            a, l = parts[k]
            held = outs if ("half", a, l) in made else ins
            return pltpu.make_async_remote_copy(
                src_ref=half(held[a], l, which), dst_ref=half(outs[a], l, which), send_sem=send_sems.at[k],
                recv_sem=recv_sems.at[k], device_id=(x, y, 1 - c), device_id_type=MESH)

        def spread(a, m, sender, held, to):
            k = 4 * sender[0] + 2 * sender[1] + sender[2]
            return pltpu.make_async_remote_copy(
                src_ref=_piece_rows(held[n + a], k), dst_ref=_piece_rows(outs[n + a], k),
                send_sem=send_g.at[7 * a + m - 1], recv_sem=recv_g.at[7 * a + m - 1], device_id=to, device_id_type=MESH)

        me, sibling = (x, y, c), (x, y, 1 - c)
        for k in range(len(parts)):
            swap(k, c).start()
        def own(a, m):
            return spread(a, m, me, outs if ("piece", a) in made else ins, _peer(x, y, c, m))

        def handed_on(a, m):
            return spread(a, m + 1, _peer(x, y, c, m), outs, sibling)

        for a in range(n_g):
            for m in SAME_CORE + (1,):
                own(a, m).start()
        for a in range(n_g):
            for m in SAME_CORE:
                spread(a, m, _peer(x, y, c, m), ins, _peer(x, y, c, m)).wait_recv()
                handed_on(a, m).start()
        for k in range(len(parts)):
            swap(k, c).wait_send()
            swap(k, 1 - c).wait_recv()
        for a in range(n_g):
            for m in SAME_CORE + (1,):
                own(a, m).wait_send()
            for m in SAME_CORE:
                handed_on(a, m).wait_send()
                spread(a, m + 1, _peer(x, y, c, m + 1), ins, sibling).wait_recv()
            spread(a, 1, sibling, ins, sibling).wait_recv()

    arrays = list(bufs) + list(gathered)
    sum_scratch = []
    if summed:
        sum_scratch = [pltpu.SemaphoreType.DMA((2 * n_s,))] + [pltpu.VMEM(recv.shape[1:], F32) for _, _, recv in summed]
        sum_scratch += [pltpu.VMEM(recv.shape, recv.dtype) for _, _, recv in summed]
    return pl.pallas_call(
        body, name=name, in_specs=[ANY] * (total + 2 * n_s), out_specs=[ANY] * total,
        out_shape=[jax.ShapeDtypeStruct(b.shape, F32) for b in arrays],
        input_output_aliases={a: a for a in range(total)},
        scratch_shapes=[pltpu.SemaphoreType.DMA((max(len(parts), 1),))] * 2
        + [pltpu.SemaphoreType.DMA((max(7 * n_g, 1),))] * 2 + sum_scratch,
        compiler_params=pltpu.CompilerParams(collective_id=collective_id, vmem_limit_bytes=VMEM_LIMIT),
    )(*arrays, *[array for _, own, recv in summed for array in (own, recv)])


def _adamw_math(w, g, m, v):
    nm = ADAM_B1 * m + (1.0 - ADAM_B1) * g
    nv = ADAM_B2 * v + (1.0 - ADAM_B2) * (g * g)
    m_hat = nm / (1.0 - ADAM_B1 ** ADAM_STEP)
    v_hat = nv / (1.0 - ADAM_B2 ** ADAM_STEP)
    return -ADAM_LR * (m_hat / (jnp.sqrt(v_hat) + ADAM_EPS) + ADAM_WD * w), nm, nv


def _adamw(name, w, g, m, v, rows_per_step, first=0, count=None, dests=None, deps=(), small=()):
    layers, rows, cols = w.shape
    count = layers if count is None else count
    dests = () if dests is None else tuple(dests)
    n_in = 4 + len(dests) + len(deps)
    small_shapes = _small_shapes(small[4].shape) if small else []

    def body(*refs):
        w_ref, g_ref, m_ref, v_ref = refs[:4]
        d_ref, nm_ref, nv_ref, g_out_ref = refs[n_in + len(small):n_in + len(small) + 4]
        d_ref[...], nm_ref[...], nv_ref[...] = _adamw_math(w_ref[...], g_ref[...], m_ref[...], v_ref[...])
        g_out_ref[...] = g_ref[...]
        if small:
            @pl.when((pl.program_id(0) == 0) & (pl.program_id(1) == 0))
            def _():
                _adamw_small(*refs[n_in:n_in + len(small)], *refs[n_in + len(small) + 4:])

    spec = pl.BlockSpec((1, rows_per_step, cols), lambda l, i: (first + l, i, 0))
    whole = lambda shape: pl.BlockSpec(shape, lambda l, i: (0,) * len(shape))
    shape = jax.ShapeDtypeStruct(w.shape, F32)
    return pl.pallas_call(
        body, name=name, grid=(count, rows // rows_per_step),
        in_specs=[spec] * 4 + [ANY] * (len(dests) + len(deps)) + [whole(a.shape) for a in small],
        out_specs=[spec] * 4 + [whole(s) for s in small_shapes],
        out_shape=[shape] * 4 + [jax.ShapeDtypeStruct(s, F32) for s in small_shapes],
        input_output_aliases={4 + k: k for k in range(len(dests))},
        scratch_shapes=[pltpu.VMEM((MISC_ROWS, 128), F32)] * 3 if small else [],
        compiler_params=_params(("arbitrary", "arbitrary")),
    )(w, g, m, v, *dests, *deps, *small)


def _pack_misc(pool_scale, sinks, norm_pre, norm_post):
    sink_rows = jnp.zeros((DEPTH, 8, 128), F32).at[:, 0, 0:N_HEADS].set(sinks).reshape(2 * 8, 128)
    return jnp.concatenate([pool_scale.reshape(8, 128), norm_pre.reshape(16, 128), norm_post.reshape(16, 128),
                            sink_rows, jnp.zeros((8, 128), F32)], axis=0)


def _adamw_small(w_ref, g_ref, m_ref, v_ref, pw_ref, pg_ref, pm_ref, pv_ref, *rest):
    outs, pool_outs, (d_ref, nm_ref, nv_ref) = rest[:17], rest[17:21], rest[21:]
    pool_outs[0][...] = pg_ref[...]
    pool_outs[1][...], pool_outs[2][...], pool_outs[3][...] = _adamw_math(
        pw_ref[...], pg_ref[...], pm_ref[...], pv_ref[...])
    d_ref[...], nm_ref[...], nv_ref[...] = _adamw_math(w_ref[...], g_ref[...], m_ref[...], v_ref[...])
    for k, src in enumerate([g_ref, d_ref, nm_ref, nv_ref]):
        scale, sinks, pre, post = outs[4 * k:4 * k + 4]
        for l in range(DEPTH):
            for j in range(4):
                scale[l:l + 1, j * 128:(j + 1) * 128] = src[MISC_SCALE + 4 * l + j:MISC_SCALE + 4 * l + j + 1, :]
            for j in range(8):
                pre[l:l + 1, j * 128:(j + 1) * 128] = src[MISC_PRE + 8 * l + j:MISC_PRE + 8 * l + j + 1, :]
                post[l:l + 1, j * 128:(j + 1) * 128] = src[MISC_POST + 8 * l + j:MISC_POST + 8 * l + j + 1, :]
            sinks[l:l + 1, :] = src[MISC_SINKS + 8 * l:MISC_SINKS + 8 * l + 1, 0:N_HEADS]
    outs[16][...] = g_ref[MISC_LOSS:MISC_LOSS + 1, 0:1]


def _small_shapes(pool_shape):
    return [(DEPTH, D_POOL), (DEPTH, N_HEADS), (DEPTH, D), (DEPTH, D)] * 4 + [(1, 1)] + [pool_shape] * 4


def kernel(x, w_in, pool_w, pool_scale, attn_sinks, w_out, norm_pre, norm_post, loss_target, m_w_in, m_pool_w, m_pool_scale, m_attn_sinks, m_w_out, m_norm_pre, m_norm_post, v_w_in, v_pool_w, v_pool_scale, v_attn_sinks, v_w_out, v_norm_pre, v_norm_post):
    cx, cy, cc = _place()
    chip_arr = jnp.reshape(2 * cx + cy, (1,)).astype(jnp.int32)
    place_arr = jnp.stack([4 * cx + 2 * cy + cc, cc]).astype(jnp.int32)
    t = lambda a: jnp.transpose(a, (0, 2, 1))
    w_in_t = t(w_in)
    xs, target = x[0], loss_target[0]
    pool_w_b = pool_w.astype(BF16)
    tables = _attention_tables()
    scale3 = pool_scale.reshape(DEPTH, 1, D_POOL)
    pre3 = norm_pre.reshape(DEPTH, 1, D)
    post3 = norm_post.reshape(DEPTH, 1, D)

    first = _gather_start_cast("gather_start_first", w_in_t, 0, ID_GATHER_FIRST)
    wi0 = _place_other_half("place_w_in0_rest", w_in_t, place_arr, 0, first[2][0])
    (wi1,) = _place_cast("place_w_in1", w_in_t, chip_arr, 288, [1], deps=(first[3],))
    wo = _place_cast("place_w_out", w_out, chip_arr, 256, [0, 1], deps=(first[3],))
    rest = _gather_start("gather_start_rest", [wi1, wo[0], wo[1]], halved=(0, 1), collective_id=ID_GATHER_REST)
    send, recv, bufs = [first[k] + rest[k] for k in range(3)]
    bufs = [wi0, *bufs[1:]]
    order = {(0, "in"): 0, (1, "in"): 1, (0, "out"): 2, (1, "out"): 3}

    saved = []
    packed = [_pack_misc(pool_scale, attn_sinks, norm_pre, norm_post),
              _pack_misc(m_pool_scale, m_attn_sinks, m_norm_pre, m_norm_post),
              _pack_misc(v_pool_scale, v_attn_sinks, v_norm_pre, v_norm_post)]
    after = (first[3], rest[3], pool_w_b, *tables, scale3, pre3, post3, *packed)
    below = None
    for l in range(DEPTH):
        k = order[l, "in"]
        halves = [_gather_wait(f"gather_wait_in{l}", bufs[k], send[k], recv[k], after, halved=True)]
        if below is not None:
            halves.append(below[1])
        w_in_l, *w_out_below = _forward_halves(f"forward_w{l}", halves, collective_id=ID_FORWARD[l])
        if below is None:
            pu, pg, q, kv, ag = _fwd_in(l, xs, pre3, w_in_l)
        else:
            saved[l - 1][9] = w_out_below[0]
            y, xs, pu, pg, q, kv, ag = _fwd_in(l, xs, pre3, w_in_l, (below[0], w_out_below[0], below[2]))
            saved[l - 1][7] = y
        cat = _fwd_mix(l, pu, pg, q, kv, ag, pool_w_b, scale3, attn_sinks, tables)
        k = order[l, "out"]
        w_out_l = _gather_wait(f"gather_wait_out{l}", bufs[k], send[k], recv[k], (cat,), halved=l + 1 < DEPTH)
        saved.append([xs, pu, pg, q, kv, ag, cat, None, w_in_l, w_out_l])
        below, after = (cat, w_out_l, post3), (w_out_l,)

    x_in, pu, pg, q, kv, ag, cat, y, w_in_l, w_out_l = saved[1]
    dcat, dw_out1, dw_out1_b, dg_post1, loss, xs = _bwd_out(1, cat, w_out_l, post3, place_arr, x=x_in, target=target)
    ex1_out = _exchange_start("exchange_start_out1", [dw_out1_b], ID_OUT1)
    dproj, dpw, dsc1, dsink1 = _bwd_mix(1, pu, pg, q, kv, ag, dcat, pool_w_b, scale3, attn_sinks, tables,
                                        deps=(ex1_out[2][0],))
    dx, dg_pre1, dw_in1, dw_in1_b = _bwd_in_dx(1, dproj, w_in_l, x_in, pre3, xs, dw_place=place_arr)
    ex1_in = _exchange_start("exchange_start_in1", [dw_in1_b], ID_IN1)

    x_in, pu, pg, q, kv, ag, cat, y, w_in_l, w_out_l = saved[0]
    dcat, dw_out0, dw_out0_b, dg_post0 = _bwd_out(0, cat, w_out_l, post3, place_arr, dxn=dx, y=y, deps=(ex1_in[2][0],))
    ex0_out = _exchange_start("exchange_start_out0", [dw_out0_b], ID_OUT0)
    dproj, dpw, dsc0, dsink0 = _bwd_mix(0, pu, pg, q, kv, ag, dcat, pool_w_b, scale3, attn_sinks, tables,
                                        deps=(ex0_out[2][0],), dpw_dest=dpw)
    dw_in0, dw_in0_b = _bwd_in_dw(0, dproj, x_in, pre3, place_arr)
    flat = lambda a: a.reshape(DEPTH * 4 * 128, 128)
    ex0_in = _exchange_start("exchange_start_in0", [flat(dpw), dw_in0_b], ID_IN0)

    grad_x, dg_pre0 = _bwd_in_dx(0, dproj, w_in_l, x_in, pre3, dx, deps=(ex0_in[2][1],))
    small = [jnp.concatenate([dsc0, dsc1, dg_pre0, dg_pre1, dg_post0, dg_post1, dsink0, dsink1, loss], axis=0)]
    ex_small = _exchange_start("exchange_start_small", small, ID_SMALL)
    (recv_out1,) = _exchange_wait("exchange_wait_out1", ex1_out, ex_small[2][0])
    (recv_in1,) = _exchange_wait("exchange_wait_in1", ex1_in, recv_out1)
    g_in, g_out = _sum_pieces("sum_pieces_1", [[(1, dw_in1, recv_in1)], [(1, dw_out1, recv_out1)]], place_arr)
    (recv_out0,) = _exchange_wait("exchange_wait_out0", ex0_out, g_out)
    (g_out,) = _sum_pieces("sum_pieces_out0", [[(0, dw_out0, recv_out0)]], place_arr, dests=[g_out])
    g_in, g_out = _share("share_a", [g_in, g_out], [(0, 1), (1, 0), (1, 1)], collective_id=ID_SHARE_A)
    m_in_t, v_in_t = t(m_w_in), t(v_w_in)
    d_out, nm_out, nv_out, grad_w_out = _adamw("adamw_w_out", w_out, g_out, m_w_out, v_w_out, 256)
    upd_in = _adamw("adamw_w_in1", w_in_t, g_in, m_in_t, v_in_t, 288, first=1, count=1, deps=(d_out,))
    dpw_own, recv_pw = _exchange_wait("exchange_wait_pool", ex0_in, upd_in[0], which=[0], with_sent=True)
    (g_pw,) = _sum_small("sum_pool", [dpw_own], [recv_pw], place_arr)

    last = [list(ex0_in[k][1:]) + list(ex_small[k]) for k in range(4)]
    recv_in0, recv_misc = _exchange_wait("exchange_wait_last", last, g_pw)
    g_in, g_pw, g_misc = _share(
        "share_b", [g_in], [(0, 0)], [g_pw, lax.empty(small[0].shape, F32)], collective_id=ID_SHARE_B,
        summed=[(("half", 0, 0), dw_in0, recv_in0), (("piece", 1), small[0], recv_misc)])
    d_in, nm_in, nv_in, grad_w_in_t, *small_out = _adamw(
        "adamw_w_in0", w_in_t, g_in, m_in_t, v_in_t, 288, first=0, count=1, dests=upd_in,
        small=(packed[0], g_misc, packed[1], packed[2], flat(pool_w), g_pw, flat(m_pool_w), flat(v_pool_w)))
    (g_sc, g_sk, g_pre, g_post, d_sc, d_sk, d_pre, d_post,
     m_sc, m_sk, m_pre, m_post, v_sc, v_sk, v_pre, v_post, loss_sum) = small_out[:17]
    g_pw, d_pw, m_pw, v_pw = [a.reshape(pool_w.shape) for a in small_out[17:]]
    return (loss_sum[0, 0], grad_x[None], t(grad_w_in_t), g_pw, g_sc, g_sk, grad_w_out, g_pre, g_post,
            t(d_in), d_pw, d_sc, d_sk, d_out, d_pre, d_post,
            t(nm_in), m_pw, m_sc, m_sk, nm_out, m_pre, m_post,
            t(nv_in), v_pw, v_sc, v_sk, nv_out, v_pre, v_post)
```

```python
import jax
import jax.numpy as jnp
from jax import lax
from jax.experimental import pallas as pl
from jax.experimental.pallas import tpu as pltpu

F32 = jnp.float32
BF16 = jnp.bfloat16

S = 2048
D = 1024
DEPTH = 2
D_POOL = 512
POOL_WINDOWS = (2, 4, 8, 16)
N_HEADS = 8
D_IN = 2304
N_SHARDS = 4
W_IN_SHARD = D_IN // N_SHARDS
W_OUT_SHARD = D // N_SHARDS
BLK = 128
NB = S // BLK
HALO = 16
PAD = 8
EPS = 1e-6
NEG_INF = -1e30
C_PU, C_PG, C_Q, C_K, C_V, C_AG = 0, 512, 1024, 1536, 1664, 1792

ADAM_LR = 0.001
ADAM_B1 = 0.9
ADAM_B2 = 0.999
ADAM_EPS = 1e-08
ADAM_WD = 0.01
ADAM_STEP = 10

TM = 512
VMEM_LIMIT = 56 * 1024 * 1024

NT = (((1,), (1,)), ((), ()))
TN = (((0,), (0,)), ((), ()))

MESH = pl.DeviceIdType.MESH
ANY = pl.BlockSpec(memory_space=pl.ANY)

ID_FORWARD = (0, 1)
(ID_SHARE_A, ID_SHARE_B, ID_GATHER_FIRST, ID_GATHER_REST, ID_OUT1, ID_IN1, ID_OUT0, ID_IN0, ID_SMALL) = range(2, 11)

MISC_SCALE, MISC_PRE, MISC_POST, MISC_SINKS, MISC_LOSS = 0, 8, 24, 40, 56
MISC_ROWS = 64


def _params(sem=("arbitrary",)):
    return pltpu.CompilerParams(dimension_semantics=sem, vmem_limit_bytes=VMEM_LIMIT)


def _sigmoid(v):
    return 1.0 / (1.0 + jnp.exp(-v))


def _rows8(v):
    r, c = v.shape
    return v.reshape(r // 8, 8, c).sum(axis=0)


def _layer(l, *shape):
    zeros = (0,) * len(shape)
    return pl.BlockSpec((None,) + shape, lambda i: (l,) + zeros)


def _whole(shape):
    zeros = (0,) * len(shape)
    return pl.BlockSpec(shape, lambda i: zeros, pipeline_mode=pl.Buffered(1))


def _fwd_in(l, x, g_pre, w_in_t, below=None):
    fused = below is not None

    def body(x_ref, g_ref, w_ref, *rest):
        if fused:
            cat_ref, wo_ref, gp_ref, y_ref, xn_ref = rest[:5]
            y = jnp.dot(cat_ref[...], wo_ref[...], preferred_element_type=F32)
            y_ref[...] = y
            xt = x_ref[...] + y * lax.rsqrt(jnp.mean(y * y, axis=-1, keepdims=True) + EPS) * gp_ref[...]
            xn_ref[...] = xt
        else:
            xt = x_ref[...]
        pu_ref, pg_ref, q_ref, kv_ref, ag_ref = rest[-5:]
        r = lax.rsqrt(jnp.mean(xt * xt, axis=-1, keepdims=True) + EPS)
        h = (xt * r * g_ref[...]).astype(BF16)

        def proj(lo, hi):
            return lax.dot_general(h, w_ref[lo:hi, :], NT, preferred_element_type=F32)

        pu_ref[...] = proj(C_PU, C_PG)
        pg_ref[...] = proj(C_PG, C_Q)
        q_ref[...] = proj(C_Q, C_K).astype(BF16)
        kv_ref[...] = proj(C_K, C_AG).astype(BF16)
        ag_ref[...] = proj(C_AG, D_IN)

    row = lambda w: pl.BlockSpec((TM, w), lambda i: (i, 0))
    act = jax.ShapeDtypeStruct((S, D), F32)
    return pl.pallas_call(
        body, name="fwd_out_in" if fused else "fwd_in", grid=(S // TM,),
        in_specs=[row(D), _layer(l, 1, D), _whole((D_IN, D))]
        + ([row(D), _whole((D, D)), _layer(l - 1, 1, D)] if fused else []),
        out_specs=[row(D)] * (2 * fused) + [row(512), row(512), row(512), row(256), row(512)],
        out_shape=[act] * (2 * fused)
        + [jax.ShapeDtypeStruct((S, 512), F32), jax.ShapeDtypeStruct((S, 512), F32),
           jax.ShapeDtypeStruct((S, 512), BF16), jax.ShapeDtypeStruct((S, 256), BF16),
           jax.ShapeDtypeStruct((S, 512), F32)],
        compiler_params=_params(),
    )(x, g_pre, w_in_t, *(below if fused else ()))


LOG2E = 1.4426950408889634
SCORE_SCALE = 0.125 * LOG2E


def _attention_tables():
    qi = jnp.arange(BLK)[:, None]
    kj = jnp.arange(BLK)[None, :]
    dist = ((qi - kj) % BLK).astype(F32)
    slopes = jnp.exp2(-jnp.arange(1, N_HEADS + 1, dtype=F32))
    bias = -(slopes * LOG2E)[:, None, None] * dist[None]
    first = jnp.where(kj > qi, NEG_INF, bias)
    return jnp.stack([first, bias]), (kj <= qi).astype(BF16)


def _own_block_mask():
    return lax.broadcasted_iota(jnp.int32, (BLK, BLK), 1) <= lax.broadcasted_iota(jnp.int32, (BLK, BLK), 0)


def _merge(full, own):
    return jnp.where(own, full[:, BLK:], full[:, :BLK])


def _spread(v, tri):
    own = v * tri
    return jnp.concatenate([v - own, own], axis=1)


def _head_variants(cur, prev):
    both = jnp.concatenate([prev, cur], axis=0).astype(F32)
    swapped = pltpu.roll(both, 64, axis=1)
    low = lax.broadcasted_iota(jnp.int32, both.shape, 1) < 64
    zero = jnp.zeros_like(both)
    return ((jnp.where(low, both, zero).astype(BF16), jnp.where(low, zero, swapped).astype(BF16)),
            (jnp.where(low, swapped, zero).astype(BF16), jnp.where(low, zero, both).astype(BF16)))


def _head_of(hkv, t, half):
    return hkv * 4 + 2 * t + half


def _rows(v, t):
    return v[t * BLK:(t + 1) * BLK]


def _stack_tiles(ref, hkv, offset=0):
    lo = offset + 2 * hkv * 128
    return jnp.concatenate([ref[:, lo:lo + 128], ref[:, lo + 128:lo + 256]], axis=0)


def _scores(q2, k_var, own):
    s = {}
    for hkv in range(2):
        for half in range(2):
            full = lax.dot_general(q2[hkv], k_var[hkv][half], NT, preferred_element_type=F32)
            for t in range(2):
                s[hkv, t, half] = _merge(_rows(full, t), own)
    return s


def _softmax(s, bias, sink):
    s = s * SCORE_SCALE + bias
    sink2 = sink * LOG2E
    m = jnp.maximum(jnp.max(s, axis=-1, keepdims=True), sink2)
    p = jnp.exp2(s - m)
    e_sink = jnp.exp2(sink2 - m)
    inv = 1.0 / (jnp.sum(p, axis=-1, keepdims=True) + e_sink)
    return p * inv, e_sink * inv


def _spread_pair(v, hkv, half, tri):
    return jnp.concatenate([_spread(v[hkv, t, half].astype(BF16), tri) for t in range(2)], axis=0)


POOL_ROWS = PAD + HALO + BLK


def _window_sums(src_ref, tmp_refs, trailing):
    lo, hi = (PAD, POOL_ROWS) if trailing else (0, HALO + BLK)
    cur = src_ref
    for level in range(len(POOL_WINDOWS)):
        lanes = slice(level * 128, 512)
        shift = -(1 << level) if trailing else (1 << level)
        dst = tmp_refs[level % 2]
        dst[lo:hi, lanes] = cur[lo:hi, lanes] + cur[lo + shift:hi + shift, lanes]
        cur = dst


def _pool_block(ext_ref, tmp_refs, i, g, w):
    lanes = slice(g * 128, (g + 1) * 128)
    rows = slice(PAD + HALO, POOL_ROWS)
    t = (i * BLK + lax.broadcasted_iota(jnp.int32, (BLK, 1), 0)).astype(F32)
    inv = 1.0 / jnp.minimum(t + 1.0, float(w))
    return tmp_refs[g % 2][rows, lanes] * inv - ext_ref[rows, lanes], inv


def _fwd_mix(l, pu, pg, q, kv, ag, pool_w, pool_scale, sinks, tables):
    bias, tri = tables

    def body(pu_ref, pup_ref, pg_ref, q_ref, kv_ref, kvp_ref, ag_ref, pw_ref, sc_ref, sink_ref, bias_ref, tri_ref,
             cat_ref, ext_ref, *tmp_refs):
        i = pl.program_id(0)

        @pl.when(i == 0)
        def _():
            for ref in (ext_ref, *tmp_refs):
                ref[0:PAD, :] = jnp.zeros((PAD, 512), F32)

        ext_ref[PAD:PAD + HALO, :] = jnp.where(i > 0, pup_ref[...], 0.0)
        ext_ref[PAD + HALO:POOL_ROWS, :] = pu_ref[...]
        _window_sums(ext_ref, tmp_refs, True)
        for g, w in enumerate(POOL_WINDOWS):
            lanes = slice(g * 128, (g + 1) * 128)
            pooled, _ = _pool_block(ext_ref, tmp_refs, i, g, w)
            mixed = jnp.dot(pooled.astype(BF16), pw_ref[g], preferred_element_type=F32)
            gate = pg_ref[:, lanes]
            cat_ref[:, lanes] = (mixed * sc_ref[:, lanes] * (gate * _sigmoid(gate))).astype(BF16)

        own = _own_block_mask()
        tri = tri_ref[...]
        k_var = _head_variants(kv_ref[:, 0:128], kvp_ref[:, 0:128])
        v_var = _head_variants(kv_ref[:, 128:256], kvp_ref[:, 128:256])
        s = _scores([_stack_tiles(q_ref, hkv) for hkv in range(2)], k_var, own)
        p = {}
        for (hkv, t, half), s_head in s.items():
            head = _head_of(hkv, t, half)
            p[hkv, t, half], _ = _softmax(s_head, bias_ref[head], sink_ref[l, head])
        for hkv in range(2):
            o2 = jnp.zeros((2 * BLK, 128), F32)
            for half in range(2):
                o2 = o2 + jnp.dot(_spread_pair(p, hkv, half, tri), v_var[hkv][half], preferred_element_type=F32)
            for t in range(2):
                lo = (2 * hkv + t) * 128
                gate = ag_ref[:, lo:lo + 128]
                cat_ref[:, D_POOL + lo:D_POOL + lo + 128] = (_rows(o2, t) * (gate * _sigmoid(gate))).astype(BF16)

    blk = lambda w: pl.BlockSpec((BLK, w), lambda i: (i, 0))
    prev = lambda w: pl.BlockSpec((BLK, w), lambda i: (jnp.maximum(i - 1, 0), 0))
    halo = pl.BlockSpec((HALO, 512), lambda i: (jnp.maximum(i * (BLK // HALO) - 1, 0), 0))
    return pl.pallas_call(
        body, name="fwd_mix", grid=(NB,),
        in_specs=[blk(512), halo, blk(512), blk(512), blk(256), prev(256), blk(512),
                  _layer(l, 4, 128, 128), _layer(l, 1, 512), pl.BlockSpec(memory_space=pltpu.SMEM),
                  pl.BlockSpec((None, N_HEADS, BLK, BLK), lambda i: (jnp.minimum(i, 1), 0, 0, 0)), _whole((BLK, BLK))],
        out_specs=blk(D),
        out_shape=jax.ShapeDtypeStruct((S, D), BF16),
        scratch_shapes=[pltpu.VMEM((POOL_ROWS, 512), F32)] * 3,
        compiler_params=_params(),
    )(pu, pu, pg, q, kv, kv, ag, pool_w, pool_scale, sinks, bias, tri)


def _store_lane_rows(ref, acc):
    total = jnp.sum(acc, axis=0, keepdims=True)
    for k in range(ref.shape[0]):
        ref[k:k + 1, :] = total[:, k * 128:(k + 1) * 128]


def _own_piece(dw_ref, place_ref):
    p = dw_ref.shape[0] // 8
    return dw_ref[pl.ds(pl.multiple_of(place_ref[0] * p, 8), p), :]


def _bwd_out(l, cat, w_out, g_post, place_arr, dxn=None, y=None, x=None, target=None, deps=()):
    last = target is not None
    n_steps = S // TM

    def body(a_ref, b_ref, g_ref, cat_ref, w_ref, place_ref, *rest):
        dcat_ref, own_ref, dwb_ref, dg_ref = rest[len(deps):len(deps) + 4]
        rest = rest[len(deps) + 4:]
        acc_ref, dw_ref = rest[-2:]
        step = pl.program_id(0)

        @pl.when(step == 0)
        def _():
            dw_ref[...] = jnp.zeros_like(dw_ref)
            acc_ref[...] = jnp.zeros_like(acc_ref)

        cat = cat_ref[...]
        g = g_ref[...]
        y = jnp.dot(cat, w_ref[...], preferred_element_type=F32) if last else b_ref[...]
        r = lax.rsqrt(jnp.mean(y * y, axis=-1, keepdims=True) + EPS)
        if last:
            loss_ref, dx_ref, loss_acc_ref = rest[:3]
            err = a_ref[...] + y * r * g - b_ref[...]

            @pl.when(step == 0)
            def _():
                loss_acc_ref[...] = jnp.zeros_like(loss_acc_ref)

            loss_acc_ref[...] += _rows8(err * err)
            dz = err * (1.0 / D)
            dx_ref[...] = dz
        else:
            dz = a_ref[...]
        a = dz * g
        dy = r * a - y * (r * r * r) * jnp.mean(a * y, axis=-1, keepdims=True)
        acc_ref[...] += _rows8(dz * (y * r))
        dyb = dy.astype(BF16)
        dcat_ref[...] = lax.dot_general(dyb, w_ref[...], NT, preferred_element_type=F32)
        dw_ref[...] += lax.dot_general(cat, dyb, TN, preferred_element_type=F32)

        @pl.when(step == n_steps - 1)
        def _():
            _store_lane_rows(dg_ref, acc_ref[...])
            dwb_ref[...] = dw_ref[...].astype(BF16)
            own_ref[...] = _own_piece(dw_ref, place_ref)
            if last:
                loss_ref[...] = jnp.full((8, 128), (0.5 / D) * jnp.sum(loss_acc_ref[...]), F32)

    row = lambda: pl.BlockSpec((TM, D), lambda i: (i, 0))
    full = _whole
    return pl.pallas_call(
        body, name="out_loss_bwd" if last else "bwd_out", grid=(n_steps,),
        in_specs=[row(), row(), _layer(l, 1, D), row(), full((D, D)), pl.BlockSpec(memory_space=pltpu.SMEM)]
        + [ANY] * len(deps),
        out_specs=[row(), full((D // 8, D)), full((D, D)), full((8, 128))] + ([full((8, 128)), row()] if last else []),
        out_shape=[jax.ShapeDtypeStruct((S, D), F32), jax.ShapeDtypeStruct((D // 8, D), F32),
                   jax.ShapeDtypeStruct((D, D), BF16), jax.ShapeDtypeStruct((8, 128), F32)]
        + ([jax.ShapeDtypeStruct((8, 128), F32), jax.ShapeDtypeStruct((S, D), F32)] if last else []),
        scratch_shapes=([pltpu.VMEM((8, D), F32)] if last else []) + [pltpu.VMEM((8, D), F32), pltpu.VMEM((D, D), F32)],
        compiler_params=_params(),
    )(*((x, target) if last else (dxn, y)), g_post, cat, w_out, place_arr, *deps)


def _bwd_mix(l, pu, pg, q, kv, ag, dcat, pool_w, pool_scale, sinks, tables, deps=(), dpw_dest=None):
    bias, tri = tables
    deps = tuple(deps) + (() if dpw_dest is None else (dpw_dest,))

    def body(pu_ref, pup_ref, pg_ref, q_ref, kv_ref, kvp_ref, ag_ref, dcat_ref, pw_ref, sc_ref, sink_ref, bias_ref,
             tri_ref, *rest):
        dproj_ref, dpw_ref, dsc_ref, dsink_ref, ext_ref, dext_ref, tmp_a, tmp_b, dkv_ref = rest[len(deps):]
        tmp_refs = (tmp_a, tmp_b)
        step = pl.program_id(0)
        i = NB - 1 - step

        @pl.when(step == 0)
        def _():
            dpw_ref[...] = jnp.zeros_like(dpw_ref)
            dsc_ref[...] = jnp.zeros_like(dsc_ref)
            dsink_ref[...] = jnp.zeros_like(dsink_ref)
            for ref in (ext_ref, tmp_a, tmp_b):
                ref[0:PAD, :] = jnp.zeros((PAD, 512), F32)
            dext_ref[BLK:POOL_ROWS, :] = jnp.zeros((HALO + PAD, 512), F32)
            dkv_ref[...] = jnp.zeros_like(dkv_ref)

        ext_ref[PAD:PAD + HALO, :] = jnp.where(i > 0, pup_ref[...], 0.0)
        ext_ref[PAD + HALO:POOL_ROWS, :] = pu_ref[...]
        _window_sums(ext_ref, tmp_refs, True)
        dpooled = []
        for g, w in enumerate(POOL_WINDOWS):
            lanes = slice(g * 128, (g + 1) * 128)
            pooled, inv = _pool_block(ext_ref, tmp_refs, i, g, w)
            pooled_b = pooled.astype(BF16)
            mixed = jnp.dot(pooled_b, pw_ref[g], preferred_element_type=F32)
            scale = sc_ref[:, lanes]
            gate = pg_ref[:, lanes]
            sg = _sigmoid(gate)
            dpo = dcat_ref[:, lanes]
            dproj_ref[:, C_PG + g * 128:C_PG + (g + 1) * 128] = (
                dpo * (mixed * scale) * (sg * (1.0 + gate * (1.0 - sg)))).astype(BF16)
            dms = dpo * (gate * sg)
            dsc_ref[g:g + 1, :] += jnp.sum(dms * mixed, axis=0, keepdims=True)
            dmixed = (dms * scale).astype(BF16)
            dpw_ref[g] += lax.dot_general(pooled_b, dmixed, TN, preferred_element_type=F32)
            dpooled.append(lax.dot_general(dmixed, pw_ref[g], NT, preferred_element_type=F32))
            dext_ref[0:BLK, lanes] = dpooled[g] * inv
        _window_sums(dext_ref, tmp_refs, False)
        for g in range(len(POOL_WINDOWS)):
            lanes = slice(g * 128, (g + 1) * 128)
            dproj_ref[:, C_PU + g * 128:C_PU + (g + 1) * 128] = (tmp_refs[g % 2][0:BLK, lanes] - dpooled[g]).astype(BF16)
        dext_ref[BLK:BLK + HALO, :] = dext_ref[0:HALO, :]

        own = _own_block_mask()
        tri = tri_ref[...]
        k_var = _head_variants(kv_ref[:, 0:128], kvp_ref[:, 0:128])
        v_var = _head_variants(kv_ref[:, 128:256], kvp_ref[:, 128:256])
        q2 = [_stack_tiles(q_ref, hkv) for hkv in range(2)]
        s = _scores(q2, k_var, own)
        p, p_sink = {}, {}
        for key, s_head in s.items():
            head = _head_of(*key)
            p[key], p_sink[key] = _softmax(s_head, bias_ref[head], sink_ref[l, head])

        do2, p_b, dp = [], {}, {}
        for hkv in range(2):
            gate = _stack_tiles(ag_ref, hkv)
            sg = _sigmoid(gate)
            dca = _stack_tiles(dcat_ref, hkv, D_POOL)
            do2.append((dca * (gate * sg)).astype(BF16))
            o2 = jnp.zeros((2 * BLK, 128), F32)
            for half in range(2):
                p_b[hkv, half] = _spread_pair(p, hkv, half, tri)
                o2 = o2 + jnp.dot(p_b[hkv, half], v_var[hkv][half], preferred_element_type=F32)
                full = lax.dot_general(do2[hkv], v_var[hkv][half], NT, preferred_element_type=F32)
                for t in range(2):
                    dp[hkv, t, half] = _merge(_rows(full, t), own)
            dag = dca * o2 * (sg * (1.0 + gate * (1.0 - sg)))
            for t in range(2):
                lo = C_AG + (2 * hkv + t) * 128
                dproj_ref[:, lo:lo + 128] = _rows(dag, t).astype(BF16)

        ds = {}
        for key in p:
            delta = jnp.sum(p[key] * dp[key], axis=-1, keepdims=True)
            ds[key] = p[key] * (dp[key] - delta)
            head = _head_of(*key)
            dsink_ref[0:1, :] += jnp.where(lax.broadcasted_iota(jnp.int32, (1, 128), 1) == head,
                                           -jnp.sum(p_sink[key] * delta, axis=0, keepdims=True), 0.0)

        dk_acc = [[None, None], [None, None]]
        dv_acc = [[None, None], [None, None]]
        for hkv in range(2):
            dq2 = jnp.zeros((2 * BLK, 128), F32)
            for half in range(2):
                ds_b = _spread_pair(ds, hkv, half, tri)
                dq2 = dq2 + jnp.dot(ds_b, k_var[hkv][half], preferred_element_type=F32)
                dk_acc[hkv][half] = lax.dot_general(ds_b, q2[hkv], TN, preferred_element_type=F32)
                dv_acc[hkv][half] = lax.dot_general(p_b[hkv, half], do2[hkv], TN, preferred_element_type=F32)
            for t in range(2):
                lo = C_Q + (2 * hkv + t) * 128
                dproj_ref[:, lo:lo + 128] = (_rows(dq2, t) * 0.125).astype(BF16)

        low = lax.broadcasted_iota(jnp.int32, (2 * BLK, 128), 1) < 64

        def gather_heads(acc):
            return jnp.where(low, acc[0][0] + pltpu.roll(acc[0][1], 64, axis=1),
                             pltpu.roll(acc[1][0], 64, axis=1) + acc[1][1])

        dk = gather_heads(dk_acc) * 0.125
        dv = gather_heads(dv_acc)
        dproj_ref[:, C_K:C_V] = (dk[BLK:, :] + dkv_ref[:, 0:128]).astype(BF16)
        dproj_ref[:, C_V:C_AG] = (dv[BLK:, :] + dkv_ref[:, 128:256]).astype(BF16)
        dkv_ref[:, 0:128] = dk[:BLK, :]
        dkv_ref[:, 128:256] = dv[:BLK, :]

    rev = lambda w: pl.BlockSpec((BLK, w), lambda s: (NB - 1 - s, 0))
    prev = lambda w: pl.BlockSpec((BLK, w), lambda s: (jnp.maximum(NB - 2 - s, 0), 0))
    halo = pl.BlockSpec((HALO, 512), lambda s: (jnp.maximum((NB - 1 - s) * (BLK // HALO) - 1, 0), 0))
    return pl.pallas_call(
        body, name="bwd_mix", grid=(NB,),
        in_specs=[rev(512), halo, rev(512), rev(512), rev(256), prev(256), rev(512), rev(D),
                  _layer(l, 4, 128, 128), _layer(l, 1, 512), pl.BlockSpec(memory_space=pltpu.SMEM),
                  pl.BlockSpec((None, N_HEADS, BLK, BLK), lambda s: (jnp.minimum(NB - 1 - s, 1), 0, 0, 0)),
                  _whole((BLK, BLK))] + [ANY] * len(deps),
        out_specs=[rev(D_IN), _layer(l, 4, 128, 128),
                   pl.BlockSpec((4, 128), lambda s: (0, 0)), pl.BlockSpec((8, 128), lambda s: (0, 0))],
        out_shape=[jax.ShapeDtypeStruct((S, D_IN), BF16), jax.ShapeDtypeStruct((DEPTH, 4, 128, 128), F32),
                   jax.ShapeDtypeStruct((4, 128), F32), jax.ShapeDtypeStruct((8, 128), F32)],
        input_output_aliases={} if dpw_dest is None else {12 + len(deps): 1},
        scratch_shapes=[pltpu.VMEM((POOL_ROWS, 512), F32)] * 4 + [pltpu.VMEM((BLK, 256), F32)],
        compiler_params=_params(),
    )(pu, pu, pg, q, kv, kv, ag, dcat, pool_w, pool_scale, sinks, bias, tri, *deps)


def _bwd_in_dw(l, dproj, x, g_pre, place_arr, to_bf16, deps=()):
    n_steps = S // TM

    def body(dp_ref, x_ref, g_ref, place_ref, *rest):
        f32_ref, own_ref, dwb_ref, bf16_ref, dw_ref = rest[len(deps):]
        step = pl.program_id(0)

        @pl.when(step == 0)
        def _():
            dw_ref[...] = jnp.zeros_like(dw_ref)
            bf16_ref[...] = f32_ref[...].astype(BF16)

        xt = x_ref[...]
        r = lax.rsqrt(jnp.mean(xt * xt, axis=-1, keepdims=True) + EPS)
        h = (xt * r * g_ref[...]).astype(BF16)
        dw_ref[...] += lax.dot_general(dp_ref[...], h, TN, preferred_element_type=F32)

        @pl.when(step == n_steps - 1)
        def _():
            dwb_ref[...] = dw_ref[...].astype(BF16)
            own_ref[...] = _own_piece(dw_ref, place_ref)

    row = lambda w: pl.BlockSpec((TM, w), lambda i: (i, 0))
    full = _whole
    return pl.pallas_call(
        body, name="bwd_in_dw", grid=(n_steps,),
        in_specs=[row(D_IN), row(D), _layer(l, 1, D), pl.BlockSpec(memory_space=pltpu.SMEM)] + [ANY] * len(deps)
        + [full(to_bf16.shape)],
        out_specs=[full((D_IN // 8, D)), full((D_IN, D)), full(to_bf16.shape)],
        out_shape=[jax.ShapeDtypeStruct((D_IN // 8, D), F32), jax.ShapeDtypeStruct((D_IN, D), BF16),
                   jax.ShapeDtypeStruct(to_bf16.shape, BF16)],
        scratch_shapes=[pltpu.VMEM((D_IN, D), F32)],
        compiler_params=_params(),
    )(dproj, x, g_pre, place_arr, *deps, to_bf16)


def _bwd_in_dx(l, dproj, w_in_t, x, g_pre, dres, deps=(), dw_place=None):
    n_steps = S // TM
    with_dw = dw_place is not None

    def body(dp_ref, w_ref, x_ref, g_ref, dres_ref, *rest):
        place_ref = rest[0] if with_dw else None
        rest = rest[with_dw + len(deps):]
        if with_dw:
            dx_ref, dg_ref, own_ref, dwb_ref, acc_ref, dw_ref = rest
        else:
            dx_ref, dg_ref, acc_ref = rest
        step = pl.program_id(0)

        @pl.when(step == 0)
        def _():
            acc_ref[...] = jnp.zeros_like(acc_ref)
            if with_dw:
                dw_ref[...] = jnp.zeros_like(dw_ref)

        g = g_ref[...]
        halves = [slice(k * (TM // 2), (k + 1) * (TM // 2)) for k in range(2)]
        dh = [jnp.dot(dp_ref[rows, :], w_ref[...], preferred_element_type=F32) for rows in halves]
        h = []
        for rows, dh_k in zip(halves, dh):
            xt = x_ref[rows, :]
            r = lax.rsqrt(jnp.mean(xt * xt, axis=-1, keepdims=True) + EPS)
            xn = xt * r
            acc_ref[...] += _rows8(dh_k * xn)
            a = dh_k * g
            dx_ref[rows, :] = dres_ref[rows, :] + (
                r * a - xt * (r * r * r) * jnp.mean(a * xt, axis=-1, keepdims=True))
            h.append((xn * g).astype(BF16))
        if with_dw:
            dw_ref[...] += lax.dot_general(dp_ref[...], jnp.concatenate(h, axis=0), TN, preferred_element_type=F32)

        @pl.when(step == n_steps - 1)
        def _():
            _store_lane_rows(dg_ref, acc_ref[...])
            if with_dw:
                dwb_ref[...] = dw_ref[...].astype(BF16)
                own_ref[...] = _own_piece(dw_ref, place_ref)

    row = lambda w: pl.BlockSpec((TM, w), lambda i: (i, 0))
    full = _whole
    dw_specs = [full((D_IN // 8, D)), full((D_IN, D))] if with_dw else []
    dw_shapes = [jax.ShapeDtypeStruct((D_IN // 8, D), F32), jax.ShapeDtypeStruct((D_IN, D), BF16)] if with_dw else []
    return pl.pallas_call(
        body, name="bwd_in" if with_dw else "bwd_in_dx", grid=(n_steps,),
        in_specs=[row(D_IN), full((D_IN, D)), row(D), _layer(l, 1, D), row(D)]
        + [pl.BlockSpec(memory_space=pltpu.SMEM)] * with_dw + [ANY] * len(deps),
        out_specs=[row(D), full((8, 128))] + dw_specs,
        out_shape=[jax.ShapeDtypeStruct((S, D), F32), jax.ShapeDtypeStruct((8, 128), F32)] + dw_shapes,
        scratch_shapes=[pltpu.VMEM((8, D), F32)] + [pltpu.VMEM((D_IN, D), F32)] * with_dw,
        compiler_params=_params(),
    )(dproj, w_in_t, x, g_pre, dres, *((dw_place,) if with_dw else ()), *deps)


HBM =pl.BlockSpec(memory_space=pltpu.HBM)
SEM = pl.BlockSpec(memory_space=pltpu.SEMAPHORE)
def _split_copy(collective_id=None):
    return pltpu.CompilerParams(has_side_effects=pltpu.SideEffectType.DATAFLOW_SIDE_EFFECTING,
                                collective_id=collective_id)


SPLIT_COPY = _split_copy()


def _in_hbm(a):
    return pltpu.with_memory_space_constraint(a, pltpu.HBM)

def _place():
    return lax.axis_index("x"), lax.axis_index("y"), lax.axis_index("c")


def _other_chips(x, y):
    return [(1 - x, y), (x, 1 - y), (1 - x, 1 - y)]


def _peer(x, y, c, m):
    return (x ^ (m >> 2), y ^ ((m >> 1) & 1), c ^ (m & 1))


SAME_CORE = (2, 4, 6)


def _place_cast(name, src, chip_arr, tile, layers, deps=()):
    _, n, cols = src.shape
    steps = n // tile
    k = len(layers)

    def body(chip_ref, *refs):
        for s_ref, o_ref in zip(refs[:k], refs[k + len(deps):]):
            o_ref[...] = s_ref[...].astype(BF16)

    def layer_spec(l):
        return pl.BlockSpec((None, tile, cols), lambda i, chip: (l, i, 0))

    return pl.pallas_call(
        body, name=name,
        grid_spec=pltpu.PrefetchScalarGridSpec(
            num_scalar_prefetch=1, grid=(steps,),
            in_specs=[layer_spec(l) for l in layers] + [ANY] * len(deps),
            out_specs=[pl.BlockSpec((tile, cols), lambda i, chip: (chip[0] * steps + i, 0))] * k),
        out_shape=[jax.ShapeDtypeStruct((N_SHARDS * n, cols), BF16)] * k,
        compiler_params=_params(),
    )(chip_arr, *[src] * k, *deps)


def _place_other_half(name, src, place_arr, layer, dest):
    _, n, cols = src.shape

    def body(place_ref, s_ref, dest_ref, o_ref):
        o_ref[...] = s_ref[...].astype(BF16)

    return pl.pallas_call(
        body, name=name,
        grid_spec=pltpu.PrefetchScalarGridSpec(
            num_scalar_prefetch=1, grid=(1,),
            in_specs=[pl.BlockSpec((None, n // 2, cols), lambda i, place: (layer, 1 - place[1], 0)), ANY],
            out_specs=pl.BlockSpec((n // 2, cols), lambda i, place: (place[0] + 1 - 2 * place[1], 0))),
        out_shape=jax.ShapeDtypeStruct((N_SHARDS * n, cols), BF16),
        input_output_aliases={2: 0},
        compiler_params=_params(),
    )(place_arr, src, dest)


def _chip_rows(ref, chip, half=None):
    n = ref.shape[0] // N_SHARDS
    if half is None:
        return ref.at[pl.ds(pl.multiple_of(chip * n, 16), n), :]
    return ref.at[pl.ds(pl.multiple_of(chip * n + half * (n // 2), 16), n // 2), :]


def _gather_start(name, bufs, halved, collective_id):
    n = len(bufs)

    def body(*refs):
        ins, send, recv, token = refs[:n], refs[n:2 * n], refs[2 * n:3 * n], refs[-1]
        x, y, c = _place()
        _handshake([(*chip, c) for chip in _other_chips(x, y)])
        for a, buf in enumerate(ins):
            own = _chip_rows(buf, 2 * x + y, c if a in halved else None)
            for j, chip in enumerate(_other_chips(x, y)):
                pltpu.make_async_remote_copy(src_ref=own, dst_ref=own, send_sem=send[a].at[j], recv_sem=recv[a].at[j],
                                             device_id=(*chip, c), device_id_type=MESH).start()
        token[...] = jnp.zeros_like(token)

    outs = pl.pallas_call(
        body, name=name, in_specs=[HBM] * n,
        out_specs=[SEM] * (2 * n) + [HBM] * n + [pl.BlockSpec(memory_space=pltpu.VMEM)],
        out_shape=[pltpu.SemaphoreType.DMA((3,))] * (2 * n) + [pltpu.HBM(b.shape, b.dtype) for b in bufs]
        + [jax.ShapeDtypeStruct((8, 128), F32)],
        input_output_aliases={a: 2 * n + a for a in range(n)},
        compiler_params=_split_copy(collective_id),
    )(*[_in_hbm(b) for b in bufs])
    return outs[:n], outs[n:2 * n], outs[2 * n:3 * n], outs[-1]


def _gather_start_cast(name, src, layer, collective_id):
    _, n, cols = src.shape
    half = n // 2

    def body(src_ref, send, recv, buf_ref, token, f32_ref, bf16_ref, local):
        x, y, c = _place()
        own = _chip_rows(buf_ref, 2 * x + y, c)

        def cast():
            load = pltpu.make_async_copy(src_ref.at[layer, pl.ds(pl.multiple_of(c * half, 16), half), :], f32_ref,
                                         local.at[0])
            load.start()
            load.wait()
            bf16_ref[...] = f32_ref[...].astype(BF16)
            store = pltpu.make_async_copy(bf16_ref, own, local.at[1])
            store.start()
            store.wait()

        _handshake([(*chip, c) for chip in _other_chips(x, y)], cast)
        for j, chip in enumerate(_other_chips(x, y)):
            pltpu.make_async_remote_copy(src_ref=own, dst_ref=own, send_sem=send.at[j], recv_sem=recv.at[j],
                                         device_id=(*chip, c), device_id_type=MESH).start()
        token[...] = jnp.zeros_like(token)

    outs = pl.pallas_call(
        body, name=name, in_specs=[HBM],
        out_specs=[SEM, SEM, HBM, pl.BlockSpec(memory_space=pltpu.VMEM)],
        out_shape=[pltpu.SemaphoreType.DMA((3,))] * 2 + [pltpu.HBM((N_SHARDS * n, cols), BF16),
                                                         jax.ShapeDtypeStruct((8, 128), F32)],
        scratch_shapes=[pltpu.VMEM((half, cols), F32), pltpu.VMEM((half, cols), BF16), pltpu.SemaphoreType.DMA((2,))],
        compiler_params=_split_copy(collective_id),
    )(_in_hbm(src))
    return outs[:1], outs[1:2], outs[2:3], outs[3]


def _gather_wait(name, buf, send_sem, recv_sem, after, halved=False):
    def body(buf_ref, send_ref, recv_ref, *rest):
        x, y, c = _place()
        half = c if halved else None
        own = _chip_rows(buf_ref, 2 * x + y, half)
        for j, chip in enumerate(_other_chips(x, y)):
            copy = pltpu.make_async_remote_copy(src_ref=own, dst_ref=_chip_rows(buf_ref, 2 * chip[0] + chip[1], half),
                                                send_sem=send_ref.at[j], recv_sem=recv_ref.at[j],
                                                device_id=(*chip, c), device_id_type=MESH)
            copy.wait_send()
            copy.wait_recv()

    return pl.pallas_call(
        body, name=name, in_specs=[HBM, SEM, SEM] + [ANY] * len(after), out_specs=HBM,
        out_shape=pltpu.HBM(buf.shape, buf.dtype), input_output_aliases={0: 0}, compiler_params=SPLIT_COPY,
    )(buf, send_sem, recv_sem, *after)


def _handshake(peers, meanwhile=None):
    barrier = pltpu.get_barrier_semaphore()
    for peer in peers:
        pl.semaphore_signal(barrier, inc=1, device_id=peer, device_id_type=MESH)
    if meanwhile is not None:
        meanwhile()
    pl.semaphore_wait(barrier, len(peers))


def _sibling_handshake(x, y, c):
    _handshake([(x, y, 1 - c)])


def _forward_halves(name, bufs, collective_id):
    n = len(bufs)

    def body(*refs):
        ins, outs, (send_sems, recv_sems) = refs[:n], refs[n:2 * n], refs[2 * n:]
        x, y, c = _place()
        _sibling_handshake(x, y, c)

        def copy(a, j, chip, half):
            rows = 2 * chip[0] + chip[1]
            return pltpu.make_async_remote_copy(
                src_ref=_chip_rows(ins[a], rows, half), dst_ref=_chip_rows(outs[a], rows, half),
                send_sem=send_sems.at[3 * a + j], recv_sem=recv_sems.at[3 * a + j], device_id=(x, y, 1 - c),
                device_id_type=MESH)

        copies = [(a, j, chip) for a in range(n) for j, chip in enumerate(_other_chips(x, y))]
        for a, j, chip in copies:
            copy(a, j, chip, c).start()
        for a, j, chip in copies:
            copy(a, j, chip, c).wait_send()
            copy(a, j, chip, 1 - c).wait_recv()

    return pl.pallas_call(
        body, name=name, in_specs=[ANY] * n, out_specs=[ANY] * n,
        out_shape=[jax.ShapeDtypeStruct(b.shape, b.dtype) for b in bufs],
        input_output_aliases={a: a for a in range(n)},
        scratch_shapes=[pltpu.SemaphoreType.DMA((3 * n,))] * 2,
        compiler_params=pltpu.CompilerParams(collective_id=collective_id),
    )(*bufs)


def _piece_rows(ref, k):
    p = ref.shape[0] // 8
    return ref.at[pl.ds(pl.multiple_of(k * p, 32 // jnp.dtype(ref.dtype).itemsize), p), :]


def _exchange_start(name, arrays, collective_id):
    n = len(arrays)
    zones = [lax.empty((7, a.shape[0] // 8, a.shape[1]), a.dtype) for a in arrays]

    def body(*refs):
        srcs, lands = refs[:n], refs[n:2 * n]
        send, recv = refs[2 * n:3 * n], refs[3 * n:4 * n]
        x, y, c = _place()
        _handshake([_peer(x, y, c, m) for m in range(1, 8)])
        for a, (src, land) in enumerate(zip(srcs, lands)):
            for m in range(1, 8):
                px, py, pc = _peer(x, y, c, m)
                pltpu.make_async_remote_copy(
                    src_ref=_piece_rows(src, 4 * px + 2 * py + pc), dst_ref=land.at[m - 1], send_sem=send[a].at[m - 1],
                    recv_sem=recv[a].at[m - 1], device_id=(px, py, pc), device_id_type=MESH).start()

    outs = pl.pallas_call(
        body, name=name, in_specs=[HBM] * (2 * n), out_specs=[SEM] * (2 * n) + [HBM] * (2 * n),
        out_shape=[pltpu.SemaphoreType.DMA((7,))] * (2 * n) + [pltpu.HBM(a.shape, a.dtype) for a in arrays + zones],
        input_output_aliases={a: 2 * n + a for a in range(2 * n)},
        compiler_params=_split_copy(collective_id),
    )(*[_in_hbm(a) for a in arrays + zones])
    return outs[:n], outs[n:2 * n], outs[2 * n:3 * n], outs[3 * n:4 * n]


def _exchange_wait(name, started, after, which=None):
    which = range(len(started[2])) if which is None else which
    send_sems, recv_sems, arrays, zones = [[group[k] for k in which] for group in started[:4]]
    n = len(arrays)

    def body(*refs):
        srcs, lands = refs[:n], refs[n:2 * n]
        send, recv = refs[2 * n:3 * n], refs[3 * n:4 * n]
        x, y, c = _place()
        for a, (src, land) in enumerate(zip(srcs, lands)):
            for m in range(1, 8):
                px, py, pc = _peer(x, y, c, m)
                copy = pltpu.make_async_remote_copy(
                    src_ref=_piece_rows(src, 4 * px + 2 * py + pc), dst_ref=land.at[m - 1], send_sem=send[a].at[m - 1],
                    recv_sem=recv[a].at[m - 1], device_id=(px, py, pc), device_id_type=MESH)
                copy.wait_send()
                copy.wait_recv()

    outs = pl.pallas_call(
        body, name=name, in_specs=[HBM] * (2 * n) + [SEM] * (2 * n) + [ANY], out_specs=[HBM] * (2 * n),
        out_shape=[pltpu.HBM(a.shape, a.dtype) for a in list(arrays) + list(zones)],
        input_output_aliases={a: a for a in range(2 * n)}, compiler_params=SPLIT_COPY,
    )(*arrays, *zones, *send_sems, *recv_sems, after)
    return outs[n:]


def _sum_pieces(name, weights, place_arr, dests=None):
    steps = 2
    flat = [item for items in weights for item in items]
    n = len(flat)

    def body(place_ref, *refs):
        outs = iter(refs[len(refs) - len(weights):])
        k = 0
        for items in weights:
            out_ref = next(outs)
            for layer, _, _ in items:
                total = refs[k][...]
                for m in range(7):
                    total = total + refs[n + k][m].astype(F32)
                if len(items) == DEPTH:
                    out_ref[layer] = total
                else:
                    out_ref[...] = total
                k += 1

    def out_spec(items):
        _, own, _ = items[0]
        t, cols = own.shape[0] // steps, own.shape[1]
        if len(items) == DEPTH:
            return pl.BlockSpec((DEPTH, t, cols), lambda i, place: (0, place[1] * steps + i, 0))
        layer = items[0][0]
        return pl.BlockSpec((None, t, cols), lambda i, place: (layer, place[1] * steps + i, 0))

    owns = [own for _, own, _ in flat]
    dests = [] if dests is None else list(dests)
    return pl.pallas_call(
        body, name=name,
        grid_spec=pltpu.PrefetchScalarGridSpec(
            num_scalar_prefetch=1, grid=(steps,),
            in_specs=[pl.BlockSpec((o.shape[0] // steps, o.shape[1]), lambda i, place: (i, 0)) for o in owns]
            + [pl.BlockSpec((7, o.shape[0] // steps, o.shape[1]), lambda i, place: (0, i, 0)) for o in owns]
            + [ANY] * len(dests),
            out_specs=[out_spec(items) for items in weights]),
        out_shape=[jax.ShapeDtypeStruct((DEPTH, 2 * items[0][1].shape[0], items[0][1].shape[1]), F32)
                   for items in weights],
        input_output_aliases={1 + 2 * n + k: k for k in range(len(dests))},
        compiler_params=_params(),
    )(place_arr, *owns, *[recv for _, _, recv in flat], *dests)


def _sum_small(name, partials, recvs, place_arr):
    n = len(partials)

    def body(place_ref, *refs):
        for o_ref, r_ref, out_ref in zip(refs[:n], refs[n:2 * n], refs[2 * n:]):
            total = o_ref[...]
            for m in range(7):
                total = total + r_ref[m].astype(F32)
            out_ref[...] = total

    piece = lambda a: pl.BlockSpec((a.shape[0] // 8, a.shape[1]), lambda i, place: (place[0], 0))
    return pl.pallas_call(
        body, name=name,
        grid_spec=pltpu.PrefetchScalarGridSpec(
            num_scalar_prefetch=1, grid=(1,),
            in_specs=[piece(a) for a in partials] + [pl.BlockSpec(r.shape, lambda i, place: (0, 0, 0)) for r in recvs],
            out_specs=[piece(a) for a in partials]),
        out_shape=[jax.ShapeDtypeStruct(a.shape, F32) for a in partials],
        compiler_params=_params(),
    )(place_arr, *partials, *recvs)


def _share(name, bufs, parts, gathered=(), collective_id=None, summed=()):
    n, n_g, n_s = len(bufs), len(gathered), len(summed)
    total = n + n_g
    made = [target for target, _, _ in summed]

    def body(*refs):
        ins, extra, outs = refs[:total], refs[total:total + 2 * n_s], refs[total + 2 * n_s:2 * total + 2 * n_s]
        send_sems, recv_sems, send_g, recv_g = refs[2 * total + 2 * n_s:2 * total + 2 * n_s + 4]
        scratch = refs[2 * total + 2 * n_s + 4:]
        x, y, c = _place()

        def half(ref, l, which):
            p = ref.shape[1] // 2
            return ref.at[l, pl.ds(pl.multiple_of(which * p, 8), p), :]

        def sums():
            local, acc, got = scratch[0], scratch[1:1 + n_s], scratch[1 + n_s:]
            loads, stores = [], []
            for j, (target, _, _) in enumerate(summed):
                own_rows = extra[2 * j] if target[0] == "half" else _piece_rows(extra[2 * j], 4 * x + 2 * y + c)
                loads += [pltpu.make_async_copy(own_rows, acc[j], local.at[2 * j]),
                          pltpu.make_async_copy(extra[2 * j + 1], got[j], local.at[2 * j + 1])]
            for load in loads:
                load.start()
            for j, (target, _, _) in enumerate(summed):
                loads[2 * j].wait()
                loads[2 * j + 1].wait()
                rows = acc[j].shape[0]
                step = min(rows, 96)
                for r in range(0, rows, step):
                    part = acc[j][r:r + step, :]
                    for m in range(7):
                        part = part + got[j][m, r:r + step, :].astype(F32)
                    acc[j][r:r + step, :] = part
                if target[0] == "half":
                    dest = half(outs[target[1]], target[2], c)
                else:
                    dest = _piece_rows(outs[n + target[1]], 4 * x + 2 * y + c)
                stores.append(pltpu.make_async_copy(acc[j], dest, local.at[2 * j]))
                stores[-1].start()
            for store in stores:
                store.wait()

        _handshake([_peer(x, y, c, m) for m in (SAME_CORE if gathered else ()) + (1,)], sums if summed else None)

        def swap(k, which):
            a, l = parts[k]
            held = outs if ("half", a, l) in made else ins
            return pltpu.make_async_remote_copy(
                src_ref=half(held[a], l, which), dst_ref=half(outs[a], l, which), send_sem=send_sems.at[k],
                recv_sem=recv_sems.at[k], device_id=(x, y, 1 - c), device_id_type=MESH)

        def spread(a, m, sender, held, to):
            k = 4 * sender[0] + 2 * sender[1] + sender[2]
            return pltpu.make_async_remote_copy(
                src_ref=_piece_rows(held[n + a], k), dst_ref=_piece_rows(outs[n + a], k),
                send_sem=send_g.at[7 * a + m - 1], recv_sem=recv_g.at[7 * a + m - 1], device_id=to, device_id_type=MESH)

        me, sibling = (x, y, c), (x, y, 1 - c)
        for k in range(len(parts)):
            swap(k, c).start()
        def own(a, m):
            return spread(a, m, me, outs if ("piece", a) in made else ins, _peer(x, y, c, m))

        def handed_on(a, m):
            return spread(a, m + 1, _peer(x, y, c, m), outs, sibling)

        for a in range(n_g):
            for m in SAME_CORE + (1,):
                own(a, m).start()
        for a in range(n_g):
            for m in SAME_CORE:
                spread(a, m, _peer(x, y, c, m), ins, _peer(x, y, c, m)).wait_recv()
                handed_on(a, m).start()
        for k in range(len(parts)):
            swap(k, c).wait_send()
            swap(k, 1 - c).wait_recv()
        for a in range(n_g):
            for m in SAME_CORE + (1,):
                own(a, m).wait_send()
            for m in SAME_CORE:
                handed_on(a, m).wait_send()
                spread(a, m + 1, _peer(x, y, c, m + 1), ins, sibling).wait_recv()
            spread(a, 1, sibling, ins, sibling).wait_recv()

    arrays = list(bufs) + list(gathered)
    sum_scratch = []
    if summed:
        sum_scratch = [pltpu.SemaphoreType.DMA((2 * n_s,))] + [pltpu.VMEM(recv.shape[1:], F32) for _, _, recv in summed]
        sum_scratch += [pltpu.VMEM(recv.shape, recv.dtype) for _, _, recv in summed]
    return pl.pallas_call(
        body, name=name, in_specs=[ANY] * (total + 2 * n_s), out_specs=[ANY] * total,
        out_shape=[jax.ShapeDtypeStruct(b.shape, F32) for b in arrays],
        input_output_aliases={a: a for a in range(total)},
        scratch_shapes=[pltpu.SemaphoreType.DMA((max(len(parts), 1),))] * 2
        + [pltpu.SemaphoreType.DMA((max(7 * n_g, 1),))] * 2 + sum_scratch,
        compiler_params=pltpu.CompilerParams(collective_id=collective_id, vmem_limit_bytes=VMEM_LIMIT),
    )(*arrays, *[array for _, own, recv in summed for array in (own, recv)])


def _adamw_math(w, g, m, v):
    nm = ADAM_B1 * m + (1.0 - ADAM_B1) * g
    nv = ADAM_B2 * v + (1.0 - ADAM_B2) * (g * g)
    m_hat = nm / (1.0 - ADAM_B1 ** ADAM_STEP)
    v_hat = nv / (1.0 - ADAM_B2 ** ADAM_STEP)
    return -ADAM_LR * (m_hat / (jnp.sqrt(v_hat) + ADAM_EPS) + ADAM_WD * w), nm, nv


def _adamw(name, w, g, m, v, rows_per_step, first=0, count=None, dests=None, deps=(), small=()):
    layers, rows, cols = w.shape
    count = layers if count is None else count
    dests = () if dests is None else tuple(dests)
    n_in = 4 + len(dests) + len(deps)
    small_shapes = _small_shapes(small[4].shape) if small else []

    def body(*refs):
        w_ref, g_ref, m_ref, v_ref = refs[:4]
        d_ref, nm_ref, nv_ref, g_out_ref = refs[n_in + len(small):n_in + len(small) + 4]
        d_ref[...], nm_ref[...], nv_ref[...] = _adamw_math(w_ref[...], g_ref[...], m_ref[...], v_ref[...])
        g_out_ref[...] = g_ref[...]
        if small:
            @pl.when((pl.program_id(0) == 0) & (pl.program_id(1) == 0))
            def _():
                _adamw_small(*refs[n_in:n_in + len(small)], *refs[n_in + len(small) + 4:])

    spec = pl.BlockSpec((1, rows_per_step, cols), lambda l, i: (first + l, i, 0))
    whole = lambda shape: pl.BlockSpec(shape, lambda l, i: (0,) * len(shape))
    shape = jax.ShapeDtypeStruct(w.shape, F32)
    return pl.pallas_call(
        body, name=name, grid=(count, rows // rows_per_step),
        in_specs=[spec] * 4 + [ANY] * (len(dests) + len(deps)) + [whole(a.shape) for a in small],
        out_specs=[spec] * 4 + [whole(s) for s in small_shapes],
        out_shape=[shape] * 4 + [jax.ShapeDtypeStruct(s, F32) for s in small_shapes],
        input_output_aliases={4 + k: k for k in range(len(dests))},
        scratch_shapes=[pltpu.VMEM((MISC_ROWS, 128), F32)] * 3 if small else [],
        compiler_params=_params(("arbitrary", "arbitrary")),
    )(w, g, m, v, *dests, *deps, *small)


def _pack_misc(pool_scale, sinks, norm_pre, norm_post):
    sink_rows = jnp.zeros((DEPTH, 8, 128), F32).at[:, 0, 0:N_HEADS].set(sinks).reshape(2 * 8, 128)
    return jnp.concatenate([pool_scale.reshape(8, 128), norm_pre.reshape(16, 128), norm_post.reshape(16, 128),
                            sink_rows, jnp.zeros((8, 128), F32)], axis=0)


def _adamw_small(w_ref, g_ref, m_ref, v_ref, pw_ref, pg_ref, pm_ref, pv_ref, *rest):
    outs, pool_outs, (d_ref, nm_ref, nv_ref) = rest[:17], rest[17:21], rest[21:]
    pool_outs[0][...] = pg_ref[...]
    pool_outs[1][...], pool_outs[2][...], pool_outs[3][...] = _adamw_math(
        pw_ref[...], pg_ref[...], pm_ref[...], pv_ref[...])
    d_ref[...], nm_ref[...], nv_ref[...] = _adamw_math(w_ref[...], g_ref[...], m_ref[...], v_ref[...])
    for k, src in enumerate([g_ref, d_ref, nm_ref, nv_ref]):
        scale, sinks, pre, post = outs[4 * k:4 * k + 4]
        for l in range(DEPTH):
            for j in range(4):
                scale[l:l + 1, j * 128:(j + 1) * 128] = src[MISC_SCALE + 4 * l + j:MISC_SCALE + 4 * l + j + 1, :]
            for j in range(8):
                pre[l:l + 1, j * 128:(j + 1) * 128] = src[MISC_PRE + 8 * l + j:MISC_PRE + 8 * l + j + 1, :]
                post[l:l + 1, j * 128:(j + 1) * 128] = src[MISC_POST + 8 * l + j:MISC_POST + 8 * l + j + 1, :]
            sinks[l:l + 1, :] = src[MISC_SINKS + 8 * l:MISC_SINKS + 8 * l + 1, 0:N_HEADS]
    outs[16][...] = g_ref[MISC_LOSS:MISC_LOSS + 1, 0:1]


def _small_shapes(pool_shape):
    return [(DEPTH, D_POOL), (DEPTH, N_HEADS), (DEPTH, D), (DEPTH, D)] * 4 + [(1, 1)] + [pool_shape] * 4


def kernel(x, w_in, pool_w, pool_scale, attn_sinks, w_out, norm_pre, norm_post, loss_target, m_w_in, m_pool_w, m_pool_scale, m_attn_sinks, m_w_out, m_norm_pre, m_norm_post, v_w_in, v_pool_w, v_pool_scale, v_attn_sinks, v_w_out, v_norm_pre, v_norm_post):
    cx, cy, cc = _place()
    chip_arr = jnp.reshape(2 * cx + cy, (1,)).astype(jnp.int32)
    place_arr = jnp.stack([4 * cx + 2 * cy + cc, cc]).astype(jnp.int32)
    t = lambda a: jnp.transpose(a, (0, 2, 1))
    w_in_t = t(w_in)
    xs, target = x[0], loss_target[0]
    pool_w_b = pool_w.astype(BF16)
    tables = _attention_tables()
    scale3 = pool_scale.reshape(DEPTH, 1, D_POOL)
    pre3 = norm_pre.reshape(DEPTH, 1, D)
    post3 = norm_post.reshape(DEPTH, 1, D)

    first = _gather_start_cast("gather_start_first", w_in_t, 0, ID_GATHER_FIRST)
    wi0 = _place_other_half("place_w_in0_rest", w_in_t, place_arr, 0, first[2][0])
    (wi1,) = _place_cast("place_w_in1", w_in_t, chip_arr, 288, [1], deps=(first[3],))
    wo = _place_cast("place_w_out", w_out, chip_arr, 256, [0, 1], deps=(first[3],))
    rest = _gather_start("gather_start_rest", [wi1, wo[0], wo[1]], halved=(0, 1), collective_id=ID_GATHER_REST)
    send, recv, bufs = [first[k] + rest[k] for k in range(3)]
    bufs = [wi0, *bufs[1:]]
    order = {(0, "in"): 0, (1, "in"): 1, (0, "out"): 2, (1, "out"): 3}

    saved = []
    packed = [_pack_misc(pool_scale, attn_sinks, norm_pre, norm_post),
              _pack_misc(m_pool_scale, m_attn_sinks, m_norm_pre, m_norm_post),
              _pack_misc(v_pool_scale, v_attn_sinks, v_norm_pre, v_norm_post)]
    after = (first[3], rest[3], pool_w_b, *tables, scale3, pre3, post3, *packed)
    below = None
    for l in range(DEPTH):
        k = order[l, "in"]
        halves = [_gather_wait(f"gather_wait_in{l}", bufs[k], send[k], recv[k], after, halved=True)]
        if below is not None:
            halves.append(below[1])
        w_in_l, *w_out_below = _forward_halves(f"forward_w{l}", halves, collective_id=ID_FORWARD[l])
        if below is None:
            pu, pg, q, kv, ag = _fwd_in(l, xs, pre3, w_in_l)
        else:
            saved[l - 1][9] = w_out_below[0]
            y, xs, pu, pg, q, kv, ag = _fwd_in(l, xs, pre3, w_in_l, (below[0], w_out_below[0], below[2]))
            saved[l - 1][7] = y
        cat = _fwd_mix(l, pu, pg, q, kv, ag, pool_w_b, scale3, attn_sinks, tables)
        k = order[l, "out"]
        w_out_l = _gather_wait(f"gather_wait_out{l}", bufs[k], send[k], recv[k], (cat,), halved=l + 1 < DEPTH)
        saved.append([xs, pu, pg, q, kv, ag, cat, None, w_in_l, w_out_l])
        below, after = (cat, w_out_l, post3), (w_out_l,)

    x_in, pu, pg, q, kv, ag, cat, y, w_in_l, w_out_l = saved[1]
    dcat, dw_out1, dw_out1_b, dg_post1, loss, xs = _bwd_out(1, cat, w_out_l, post3, place_arr, x=x_in, target=target)
    ex1_out = _exchange_start("exchange_start_out1", [dw_out1_b], ID_OUT1)
    dproj, dpw, dsc1, dsink1 = _bwd_mix(1, pu, pg, q, kv, ag, dcat, pool_w_b, scale3, attn_sinks, tables,
                                        deps=(ex1_out[2][0],))
    dx, dg_pre1, dw_in1, dw_in1_b = _bwd_in_dx(1, dproj, w_in_l, x_in, pre3, xs, dw_place=place_arr)
    ex1_in = _exchange_start("exchange_start_in1", [dw_in1_b], ID_IN1)

    x_in, pu, pg, q, kv, ag, cat, y, w_in_l, w_out_l = saved[0]
    dcat, dw_out0, dw_out0_b, dg_post0 = _bwd_out(0, cat, w_out_l, post3, place_arr, dxn=dx, y=y, deps=(ex1_in[2][0],))
    ex0_out = _exchange_start("exchange_start_out0", [dw_out0_b], ID_OUT0)
    dproj, dpw, dsc0, dsink0 = _bwd_mix(0, pu, pg, q, kv, ag, dcat, pool_w_b, scale3, attn_sinks, tables,
                                        deps=(ex0_out[2][0],), dpw_dest=dpw)
    flat = lambda a: a.reshape(DEPTH * 4 * 128, 128)
    dw_in0, dw_in0_b, dpw_b = _bwd_in_dw(0, dproj, x_in, pre3, place_arr, flat(dpw))
    ex0_in = _exchange_start("exchange_start_in0", [dpw_b, dw_in0_b], ID_IN0)

    grad_x, dg_pre0 = _bwd_in_dx(0, dproj, w_in_l, x_in, pre3, dx, deps=(ex0_in[2][1],))
    small = [jnp.concatenate([dsc0, dsc1, dg_pre0, dg_pre1, dg_post0, dg_post1, dsink0, dsink1, loss], axis=0)]
    ex_small = _exchange_start("exchange_start_small", small, ID_SMALL)
    (recv_out1,) = _exchange_wait("exchange_wait_out1", ex1_out, ex_small[2][0])
    (recv_in1,) = _exchange_wait("exchange_wait_in1", ex1_in, recv_out1)
    g_in, g_out = _sum_pieces("sum_pieces_1", [[(1, dw_in1, recv_in1)], [(1, dw_out1, recv_out1)]], place_arr)
    (recv_out0,) = _exchange_wait("exchange_wait_out0", ex0_out, g_out)
    (g_out,) = _sum_pieces("sum_pieces_out0", [[(0, dw_out0, recv_out0)]], place_arr, dests=[g_out])
    g_in, g_out = _share("share_a", [g_in, g_out], [(0, 1), (1, 0), (1, 1)], collective_id=ID_SHARE_A)
    m_in_t, v_in_t = t(m_w_in), t(v_w_in)
    d_out, nm_out, nv_out, grad_w_out = _adamw("adamw_w_out", w_out, g_out, m_w_out, v_w_out, 256)
    upd_in = _adamw("adamw_w_in1", w_in_t, g_in, m_in_t, v_in_t, 288, first=1, count=1, deps=(d_out,))
    (recv_pw,) = _exchange_wait("exchange_wait_pool", ex0_in, upd_in[0], which=[0])
    (g_pw,) = _sum_small("sum_pool", [flat(dpw)], [recv_pw], place_arr)

    (recv_in0,) = _exchange_wait("exchange_wait_in0", ex0_in, g_pw, which=[1])
    (recv_misc,) = _exchange_wait("exchange_wait_small", ex_small, recv_in0)
    g_in, g_pw, g_misc = _share(
        "share_b", [g_in], [(0, 0)], [g_pw, lax.empty(small[0].shape, F32)], collective_id=ID_SHARE_B,
        summed=[(("half", 0, 0), dw_in0, recv_in0), (("piece", 1), small[0], recv_misc)])
    d_in, nm_in, nv_in, grad_w_in_t, *small_out = _adamw(
        "adamw_w_in0", w_in_t, g_in, m_in_t, v_in_t, 288, first=0, count=1, dests=upd_in,
        small=(packed[0], g_misc, packed[1], packed[2], flat(pool_w), g_pw, flat(m_pool_w), flat(v_pool_w)))
    (g_sc, g_sk, g_pre, g_post, d_sc, d_sk, d_pre, d_post,
     m_sc, m_sk, m_pre, m_post, v_sc, v_sk, v_pre, v_post, loss_sum) = small_out[:17]
    g_pw, d_pw, m_pw, v_pw = [a.reshape(pool_w.shape) for a in small_out[17:]]
    return (loss_sum[0, 0], grad_x[None], t(grad_w_in_t), g_pw, g_sc, g_sk, grad_w_out, g_pre, g_post,
            t(d_in), d_pw, d_sc, d_sk, d_out, d_pre, d_post,
            t(nm_in), m_pw, m_sc, m_sk, nm_out, m_pre, m_post,
            t(nv_in), v_pw, v_sc, v_sk, nv_out, v_pre, v_post)
```

```python
import jax
import jax.numpy as jnp
from jax import lax
from jax.experimental import pallas as pl
from jax.experimental.pallas import tpu as pltpu

F32 = jnp.float32
BF16 = jnp.bfloat16

S = 2048
D = 1024
DEPTH = 2
D_POOL = 512
POOL_WINDOWS = (2, 4, 8, 16)
N_HEADS = 8
D_IN = 2304
N_SHARDS = 4
W_IN_SHARD = D_IN // N_SHARDS
W_OUT_SHARD = D // N_SHARDS
BLK = 128
NB = S // BLK
HALO = 16
PAD = 8
EPS = 1e-6
NEG_INF = -1e30
C_PU, C_PG, C_Q, C_K, C_V, C_AG = 0, 512, 1024, 1536, 1664, 1792

ADAM_LR = 0.001
ADAM_B1 = 0.9
ADAM_B2 = 0.999
ADAM_EPS = 1e-08
ADAM_WD = 0.01
ADAM_STEP = 10

TM = 512
VMEM_LIMIT = 56 * 1024 * 1024

NT = (((1,), (1,)), ((), ()))
TN = (((0,), (0,)), ((), ()))

MESH = pl.DeviceIdType.MESH
ANY = pl.BlockSpec(memory_space=pl.ANY)

ID_FORWARD = (0, 1)
(ID_SHARE_A, ID_SHARE_B, ID_GATHER_FIRST, ID_GATHER_REST, ID_OUT1, ID_IN1, ID_OUT0, ID_IN0, ID_SMALL) = range(2, 11)

MISC_SCALE, MISC_PRE, MISC_POST, MISC_SINKS, MISC_LOSS = 0, 8, 24, 40, 56
MISC_ROWS = 64


def _params(sem=("arbitrary",)):
    return pltpu.CompilerParams(dimension_semantics=sem, vmem_limit_bytes=VMEM_LIMIT)


def _sigmoid(v):
    return 1.0 / (1.0 + jnp.exp(-v))


def _rows8(v):
    r, c = v.shape
    return v.reshape(r // 8, 8, c).sum(axis=0)


def _layer(l, *shape):
    zeros = (0,) * len(shape)
    return pl.BlockSpec((None,) + shape, lambda i: (l,) + zeros)


def _whole(shape):
    zeros = (0,) * len(shape)
    return pl.BlockSpec(shape, lambda i: zeros, pipeline_mode=pl.Buffered(1))


def _fwd_in(l, x, g_pre, w_in_t, below=None):
    fused = below is not None

    def body(x_ref, g_ref, w_ref, *rest):
        if fused:
            cat_ref, wo_ref, gp_ref, y_ref, xn_ref = rest[:5]
            y = jnp.dot(cat_ref[...], wo_ref[...], preferred_element_type=F32)
            y_ref[...] = y
            xt = x_ref[...] + y * lax.rsqrt(jnp.mean(y * y, axis=-1, keepdims=True) + EPS) * gp_ref[...]
            xn_ref[...] = xt
        else:
            xt = x_ref[...]
        pu_ref, pg_ref, q_ref, kv_ref, ag_ref = rest[-5:]
        r = lax.rsqrt(jnp.mean(xt * xt, axis=-1, keepdims=True) + EPS)
        h = (xt * r * g_ref[...]).astype(BF16)

        def proj(lo, hi):
            return lax.dot_general(h, w_ref[lo:hi, :], NT, preferred_element_type=F32)

        pu_ref[...] = proj(C_PU, C_PG)
        pg_ref[...] = proj(C_PG, C_Q)
        q_ref[...] = proj(C_Q, C_K).astype(BF16)
        kv_ref[...] = proj(C_K, C_AG).astype(BF16)
        ag_ref[...] = proj(C_AG, D_IN)

    row = lambda w: pl.BlockSpec((TM, w), lambda i: (i, 0))
    act = jax.ShapeDtypeStruct((S, D), F32)
    return pl.pallas_call(
        body, name="fwd_out_in" if fused else "fwd_in", grid=(S // TM,),
        in_specs=[row(D), _layer(l, 1, D), _whole((D_IN, D))]
        + ([row(D), _whole((D, D)), _layer(l - 1, 1, D)] if fused else []),
        out_specs=[row(D)] * (2 * fused) + [row(512), row(512), row(512), row(256), row(512)],
        out_shape=[act] * (2 * fused)
        + [jax.ShapeDtypeStruct((S, 512), F32), jax.ShapeDtypeStruct((S, 512), F32),
           jax.ShapeDtypeStruct((S, 512), BF16), jax.ShapeDtypeStruct((S, 256), BF16),
           jax.ShapeDtypeStruct((S, 512), F32)],
        compiler_params=_params(),
    )(x, g_pre, w_in_t, *(below if fused else ()))


LOG2E = 1.4426950408889634
SCORE_SCALE = 0.125 * LOG2E


def _attention_tables():
    qi = jnp.arange(BLK)[:, None]
    kj = jnp.arange(BLK)[None, :]
    dist = ((qi - kj) % BLK).astype(F32)
    slopes = jnp.exp2(-jnp.arange(1, N_HEADS + 1, dtype=F32))
    bias = -(slopes * LOG2E)[:, None, None] * dist[None]
    first = jnp.where(kj > qi, NEG_INF, bias)
    return jnp.stack([first, bias]), (kj <= qi).astype(BF16)


def _own_block_mask():
    return lax.broadcasted_iota(jnp.int32, (BLK, BLK), 1) <= lax.broadcasted_iota(jnp.int32, (BLK, BLK), 0)


def _merge(full, own):
    return jnp.where(own, full[:, BLK:], full[:, :BLK])


def _spread(v, tri):
    own = v * tri
    return jnp.concatenate([v - own, own], axis=1)


def _head_variants(cur, prev):
    both = jnp.concatenate([prev, cur], axis=0).astype(F32)
    swapped = pltpu.roll(both, 64, axis=1)
    low = lax.broadcasted_iota(jnp.int32, both.shape, 1) < 64
    zero = jnp.zeros_like(both)
    return ((jnp.where(low, both, zero).astype(BF16), jnp.where(low, zero, swapped).astype(BF16)),
            (jnp.where(low, swapped, zero).astype(BF16), jnp.where(low, zero, both).astype(BF16)))


def _head_of(hkv, t, half):
    return hkv * 4 + 2 * t + half


def _rows(v, t):
    return v[t * BLK:(t + 1) * BLK]


def _stack_tiles(ref, hkv, offset=0):
    lo = offset + 2 * hkv * 128
    return jnp.concatenate([ref[:, lo:lo + 128], ref[:, lo + 128:lo + 256]], axis=0)


def _scores(q2, k_var, own):
    s = {}
    for hkv in range(2):
        for half in range(2):
            full = lax.dot_general(q2[hkv], k_var[hkv][half], NT, preferred_element_type=F32)
            for t in range(2):
                s[hkv, t, half] = _merge(_rows(full, t), own)
    return s


def _softmax(s, bias, sink):
    s = s * SCORE_SCALE + bias
    sink2 = sink * LOG2E
    m = jnp.maximum(jnp.max(s, axis=-1, keepdims=True), sink2)
    p = jnp.exp2(s - m)
    e_sink = jnp.exp2(sink2 - m)
    inv = 1.0 / (jnp.sum(p, axis=-1, keepdims=True) + e_sink)
    return p * inv, e_sink * inv


def _spread_pair(v, hkv, half, tri):
    return jnp.concatenate([_spread(v[hkv, t, half].astype(BF16), tri) for t in range(2)], axis=0)


POOL_ROWS = PAD + HALO + BLK


def _window_sums(src_ref, tmp_refs, trailing):
    lo, hi = (PAD, POOL_ROWS) if trailing else (0, HALO + BLK)
    cur = src_ref
    for level in range(len(POOL_WINDOWS)):
        lanes = slice(level * 128, 512)
        shift = -(1 << level) if trailing else (1 << level)
        dst = tmp_refs[level % 2]
        dst[lo:hi, lanes] = cur[lo:hi, lanes] + cur[lo + shift:hi + shift, lanes]
        cur = dst


def _pool_block(ext_ref, tmp_refs, i, g, w):
    lanes = slice(g * 128, (g + 1) * 128)
    rows = slice(PAD + HALO, POOL_ROWS)
    t = (i * BLK + lax.broadcasted_iota(jnp.int32, (BLK, 1), 0)).astype(F32)
    inv = 1.0 / jnp.minimum(t + 1.0, float(w))
    return tmp_refs[g % 2][rows, lanes] * inv - ext_ref[rows, lanes], inv


def _fwd_mix(l, pu, pg, q, kv, ag, pool_w, pool_scale, sinks, tables):
    bias, tri = tables

    def body(pu_ref, pup_ref, pg_ref, q_ref, kv_ref, kvp_ref, ag_ref, pw_ref, sc_ref, sink_ref, bias_ref, tri_ref,
             cat_ref, ext_ref, *tmp_refs):
        i = pl.program_id(0)

        @pl.when(i == 0)
        def _():
            for ref in (ext_ref, *tmp_refs):
                ref[0:PAD, :] = jnp.zeros((PAD, 512), F32)

        ext_ref[PAD:PAD + HALO, :] = jnp.where(i > 0, pup_ref[...], 0.0)
        ext_ref[PAD + HALO:POOL_ROWS, :] = pu_ref[...]
        _window_sums(ext_ref, tmp_refs, True)
        for g, w in enumerate(POOL_WINDOWS):
            lanes = slice(g * 128, (g + 1) * 128)
            pooled, _ = _pool_block(ext_ref, tmp_refs, i, g, w)
            mixed = jnp.dot(pooled.astype(BF16), pw_ref[g], preferred_element_type=F32)
            gate = pg_ref[:, lanes]
            cat_ref[:, lanes] = (mixed * sc_ref[:, lanes] * (gate * _sigmoid(gate))).astype(BF16)

        own = _own_block_mask()
        tri = tri_ref[...]
        k_var = _head_variants(kv_ref[:, 0:128], kvp_ref[:, 0:128])
        v_var = _head_variants(kv_ref[:, 128:256], kvp_ref[:, 128:256])
        s = _scores([_stack_tiles(q_ref, hkv) for hkv in range(2)], k_var, own)
        p = {}
        for (hkv, t, half), s_head in s.items():
            head = _head_of(hkv, t, half)
            p[hkv, t, half], _ = _softmax(s_head, bias_ref[head], sink_ref[l, head])
        for hkv in range(2):
            o2 = jnp.zeros((2 * BLK, 128), F32)
            for half in range(2):
                o2 = o2 + jnp.dot(_spread_pair(p, hkv, half, tri), v_var[hkv][half], preferred_element_type=F32)
            for t in range(2):
                lo = (2 * hkv + t) * 128
                gate = ag_ref[:, lo:lo + 128]
                cat_ref[:, D_POOL + lo:D_POOL + lo + 128] = (_rows(o2, t) * (gate * _sigmoid(gate))).astype(BF16)

    blk = lambda w: pl.BlockSpec((BLK, w), lambda i: (i, 0))
    prev = lambda w: pl.BlockSpec((BLK, w), lambda i: (jnp.maximum(i - 1, 0), 0))
    halo = pl.BlockSpec((HALO, 512), lambda i: (jnp.maximum(i * (BLK // HALO) - 1, 0), 0))
    return pl.pallas_call(
        body, name="fwd_mix", grid=(NB,),
        in_specs=[blk(512), halo, blk(512), blk(512), blk(256), prev(256), blk(512),
                  _layer(l, 4, 128, 128), _layer(l, 1, 512), pl.BlockSpec(memory_space=pltpu.SMEM),
                  pl.BlockSpec((None, N_HEADS, BLK, BLK), lambda i: (jnp.minimum(i, 1), 0, 0, 0)), _whole((BLK, BLK))],
        out_specs=blk(D),
        out_shape=jax.ShapeDtypeStruct((S, D), BF16),
        scratch_shapes=[pltpu.VMEM((POOL_ROWS, 512), F32)] * 3,
        compiler_params=_params(),
    )(pu, pu, pg, q, kv, kv, ag, pool_w, pool_scale, sinks, bias, tri)


def _store_lane_rows(ref, acc):
    total = jnp.sum(acc, axis=0, keepdims=True)
    for k in range(ref.shape[0]):
        ref[k:k + 1, :] = total[:, k * 128:(k + 1) * 128]


def _own_piece(dw_ref, place_ref):
    p = dw_ref.shape[0] // 8
    return dw_ref[pl.ds(pl.multiple_of(place_ref[0] * p, 8), p), :]


def _bwd_out(l, cat, w_out, g_post, place_arr, dxn=None, y=None, x=None, target=None, deps=()):
    last = target is not None
    n_steps = S // TM

    def body(a_ref, b_ref, g_ref, cat_ref, w_ref, place_ref, *rest):
        dcat_ref, own_ref, dwb_ref, dg_ref = rest[len(deps):len(deps) + 4]
        rest = rest[len(deps) + 4:]
        acc_ref, dw_ref = rest[-2:]
        step = pl.program_id(0)

        @pl.when(step == 0)
        def _():
            dw_ref[...] = jnp.zeros_like(dw_ref)
            acc_ref[...] = jnp.zeros_like(acc_ref)

        cat = cat_ref[...]
        g = g_ref[...]
        y = jnp.dot(cat, w_ref[...], preferred_element_type=F32) if last else b_ref[...]
        r = lax.rsqrt(jnp.mean(y * y, axis=-1, keepdims=True) + EPS)
        if last:
            loss_ref, dx_ref, loss_acc_ref = rest[:3]
            err = a_ref[...] + y * r * g - b_ref[...]

            @pl.when(step == 0)
            def _():
                loss_acc_ref[...] = jnp.zeros_like(loss_acc_ref)

            loss_acc_ref[...] += _rows8(err * err)
            dz = err * (1.0 / D)
            dx_ref[...] = dz
        else:
            dz = a_ref[...]
        a = dz * g
        dy = r * a - y * (r * r * r) * jnp.mean(a * y, axis=-1, keepdims=True)
        acc_ref[...] += _rows8(dz * (y * r))
        dyb = dy.astype(BF16)
        dcat_ref[...] = lax.dot_general(dyb, w_ref[...], NT, preferred_element_type=F32)
        dw_ref[...] += lax.dot_general(cat, dyb, TN, preferred_element_type=F32)

        @pl.when(step == n_steps - 1)
        def _():
            _store_lane_rows(dg_ref, acc_ref[...])
            dwb_ref[...] = dw_ref[...].astype(BF16)
            own_ref[...] = _own_piece(dw_ref, place_ref)
            if last:
                loss_ref[...] = jnp.full((8, 128), (0.5 / D) * jnp.sum(loss_acc_ref[...]), F32)

    row = lambda: pl.BlockSpec((TM, D), lambda i: (i, 0))
    full = _whole
    return pl.pallas_call(
        body, name="out_loss_bwd" if last else "bwd_out", grid=(n_steps,),
        in_specs=[row(), row(), _layer(l, 1, D), row(), full((D, D)), pl.BlockSpec(memory_space=pltpu.SMEM)]
        + [ANY] * len(deps),
        out_specs=[row(), full((D // 8, D)), full((D, D)), full((8, 128))] + ([full((8, 128)), row()] if last else []),
        out_shape=[jax.ShapeDtypeStruct((S, D), F32), jax.ShapeDtypeStruct((D // 8, D), F32),
                   jax.ShapeDtypeStruct((D, D), BF16), jax.ShapeDtypeStruct((8, 128), F32)]
        + ([jax.ShapeDtypeStruct((8, 128), F32), jax.ShapeDtypeStruct((S, D), F32)] if last else []),
        scratch_shapes=([pltpu.VMEM((8, D), F32)] if last else []) + [pltpu.VMEM((8, D), F32), pltpu.VMEM((D, D), F32)],
        compiler_params=_params(),
    )(*((x, target) if last else (dxn, y)), g_post, cat, w_out, place_arr, *deps)


def _bwd_mix(l, pu, pg, q, kv, ag, dcat, pool_w, pool_scale, sinks, tables, deps=(), dpw_dest=None):
    bias, tri = tables
    deps = tuple(deps) + (() if dpw_dest is None else (dpw_dest,))

    def body(pu_ref, pup_ref, pg_ref, q_ref, kv_ref, kvp_ref, ag_ref, dcat_ref, pw_ref, sc_ref, sink_ref, bias_ref,
             tri_ref, *rest):
        dproj_ref, dpw_ref, dsc_ref, dsink_ref, ext_ref, dext_ref, tmp_a, tmp_b, dkv_ref = rest[len(deps):]
        tmp_refs = (tmp_a, tmp_b)
        step = pl.program_id(0)
        i = NB - 1 - step

        @pl.when(step == 0)
        def _():
            dpw_ref[...] = jnp.zeros_like(dpw_ref)
            dsc_ref[...] = jnp.zeros_like(dsc_ref)
            dsink_ref[...] = jnp.zeros_like(dsink_ref)
            for ref in (ext_ref, tmp_a, tmp_b):
                ref[0:PAD, :] = jnp.zeros((PAD, 512), F32)
            dext_ref[BLK:POOL_ROWS, :] = jnp.zeros((HALO + PAD, 512), F32)
            dkv_ref[...] = jnp.zeros_like(dkv_ref)

        ext_ref[PAD:PAD + HALO, :] = jnp.where(i > 0, pup_ref[...], 0.0)
        ext_ref[PAD + HALO:POOL_ROWS, :] = pu_ref[...]
        _window_sums(ext_ref, tmp_refs, True)
        dpooled = []
        for g, w in enumerate(POOL_WINDOWS):
            lanes = slice(g * 128, (g + 1) * 128)
            pooled, inv = _pool_block(ext_ref, tmp_refs, i, g, w)
            pooled_b = pooled.astype(BF16)
            mixed = jnp.dot(pooled_b, pw_ref[g], preferred_element_type=F32)
            scale = sc_ref[:, lanes]
            gate = pg_ref[:, lanes]
            sg = _sigmoid(gate)
            dpo = dcat_ref[:, lanes]
            dproj_ref[:, C_PG + g * 128:C_PG + (g + 1) * 128] = (
                dpo * (mixed * scale) * (sg * (1.0 + gate * (1.0 - sg)))).astype(BF16)
            dms = dpo * (gate * sg)
            dsc_ref[g:g + 1, :] += jnp.sum(dms * mixed, axis=0, keepdims=True)
            dmixed = (dms * scale).astype(BF16)
            dpw_ref[g] += lax.dot_general(pooled_b, dmixed, TN, preferred_element_type=F32)
            dpooled.append(lax.dot_general(dmixed, pw_ref[g], NT, preferred_element_type=F32))
            dext_ref[0:BLK, lanes] = dpooled[g] * inv
        _window_sums(dext_ref, tmp_refs, False)
        for g in range(len(POOL_WINDOWS)):
            lanes = slice(g * 128, (g + 1) * 128)
            dproj_ref[:, C_PU + g * 128:C_PU + (g + 1) * 128] = (tmp_refs[g % 2][0:BLK, lanes] - dpooled[g]).astype(BF16)
        dext_ref[BLK:BLK + HALO, :] = dext_ref[0:HALO, :]

        own = _own_block_mask()
        tri = tri_ref[...]
        k_var = _head_variants(kv_ref[:, 0:128], kvp_ref[:, 0:128])
        v_var = _head_variants(kv_ref[:, 128:256], kvp_ref[:, 128:256])
        q2 = [_stack_tiles(q_ref, hkv) for hkv in range(2)]
        s = _scores(q2, k_var, own)
        p, p_sink = {}, {}
        for key, s_head in s.items():
            head = _head_of(*key)
            p[key], p_sink[key] = _softmax(s_head, bias_ref[head], sink_ref[l, head])

        do2, p_b, dp = [], {}, {}
        for hkv in range(2):
            gate = _stack_tiles(ag_ref, hkv)
            sg = _sigmoid(gate)
            dca = _stack_tiles(dcat_ref, hkv, D_POOL)
            do2.append((dca * (gate * sg)).astype(BF16))
            o2 = jnp.zeros((2 * BLK, 128), F32)
            for half in range(2):
                p_b[hkv, half] = _spread_pair(p, hkv, half, tri)
                o2 = o2 + jnp.dot(p_b[hkv, half], v_var[hkv][half], preferred_element_type=F32)
                full = lax.dot_general(do2[hkv], v_var[hkv][half], NT, preferred_element_type=F32)
                for t in range(2):
                    dp[hkv, t, half] = _merge(_rows(full, t), own)
            dag = dca * o2 * (sg * (1.0 + gate * (1.0 - sg)))
            for t in range(2):
                lo = C_AG + (2 * hkv + t) * 128
                dproj_ref[:, lo:lo + 128] = _rows(dag, t).astype(BF16)

        ds = {}
        for key in p:
            delta = jnp.sum(p[key] * dp[key], axis=-1, keepdims=True)
            ds[key] = p[key] * (dp[key] - delta)
            head = _head_of(*key)
            dsink_ref[0:1, :] += jnp.where(lax.broadcasted_iota(jnp.int32, (1, 128), 1) == head,
                                           -jnp.sum(p_sink[key] * delta, axis=0, keepdims=True), 0.0)

        dk_acc = [[None, None], [None, None]]
        dv_acc = [[None, None], [None, None]]
        for hkv in range(2):
            dq2 = jnp.zeros((2 * BLK, 128), F32)
            for half in range(2):
                ds_b = _spread_pair(ds, hkv, half, tri)
                dq2 = dq2 + jnp.dot(ds_b, k_var[hkv][half], preferred_element_type=F32)
                dk_acc[hkv][half] = lax.dot_general(ds_b, q2[hkv], TN, preferred_element_type=F32)
                dv_acc[hkv][half] = lax.dot_general(p_b[hkv, half], do2[hkv], TN, preferred_element_type=F32)
            for t in range(2):
                lo = C_Q + (2 * hkv + t) * 128
                dproj_ref[:, lo:lo + 128] = (_rows(dq2, t) * 0.125).astype(BF16)

        low = lax.broadcasted_iota(jnp.int32, (2 * BLK, 128), 1) < 64

        def gather_heads(acc):
            return jnp.where(low, acc[0][0] + pltpu.roll(acc[0][1], 64, axis=1),
                             pltpu.roll(acc[1][0], 64, axis=1) + acc[1][1])

        dk = gather_heads(dk_acc) * 0.125
        dv = gather_heads(dv_acc)
        dproj_ref[:, C_K:C_V] = (dk[BLK:, :] + dkv_ref[:, 0:128]).astype(BF16)
        dproj_ref[:, C_V:C_AG] = (dv[BLK:, :] + dkv_ref[:, 128:256]).astype(BF16)
        dkv_ref[:, 0:128] = dk[:BLK, :]
        dkv_ref[:, 128:256] = dv[:BLK, :]

    rev = lambda w: pl.BlockSpec((BLK, w), lambda s: (NB - 1 - s, 0))
    prev = lambda w: pl.BlockSpec((BLK, w), lambda s: (jnp.maximum(NB - 2 - s, 0), 0))
    halo = pl.BlockSpec((HALO, 512), lambda s: (jnp.maximum((NB - 1 - s) * (BLK // HALO) - 1, 0), 0))
    return pl.pallas_call(
        body, name="bwd_mix", grid=(NB,),
        in_specs=[rev(512), halo, rev(512), rev(512), rev(256), prev(256), rev(512), rev(D),
                  _layer(l, 4, 128, 128), _layer(l, 1, 512), pl.BlockSpec(memory_space=pltpu.SMEM),
                  pl.BlockSpec((None, N_HEADS, BLK, BLK), lambda s: (jnp.minimum(NB - 1 - s, 1), 0, 0, 0)),
                  _whole((BLK, BLK))] + [ANY] * len(deps),
        out_specs=[rev(D_IN), _layer(l, 4, 128, 128),
                   pl.BlockSpec((4, 128), lambda s: (0, 0)), pl.BlockSpec((8, 128), lambda s: (0, 0))],
        out_shape=[jax.ShapeDtypeStruct((S, D_IN), BF16), jax.ShapeDtypeStruct((DEPTH, 4, 128, 128), F32),
                   jax.ShapeDtypeStruct((4, 128), F32), jax.ShapeDtypeStruct((8, 128), F32)],
        input_output_aliases={} if dpw_dest is None else {12 + len(deps): 1},
        scratch_shapes=[pltpu.VMEM((POOL_ROWS, 512), F32)] * 4 + [pltpu.VMEM((BLK, 256), F32)],
        compiler_params=_params(),
    )(pu, pu, pg, q, kv, kv, ag, dcat, pool_w, pool_scale, sinks, bias, tri, *deps)


def _bwd_in_dw(l, dproj, x, g_pre, place_arr, to_bf16, deps=()):
    n_steps = S // TM

    def body(dp_ref, x_ref, g_ref, place_ref, *rest):
        f32_ref, own_ref, dwb_ref, bf16_ref, dw_ref = rest[len(deps):]
        step = pl.program_id(0)

        @pl.when(step == 0)
        def _():
            dw_ref[...] = jnp.zeros_like(dw_ref)
            bf16_ref[...] = f32_ref[...].astype(BF16)

        xt = x_ref[...]
        r = lax.rsqrt(jnp.mean(xt * xt, axis=-1, keepdims=True) + EPS)
        h = (xt * r * g_ref[...]).astype(BF16)
        dw_ref[...] += lax.dot_general(dp_ref[...], h, TN, preferred_element_type=F32)

        @pl.when(step == n_steps - 1)
        def _():
            dwb_ref[...] = dw_ref[...].astype(BF16)
            own_ref[...] = _own_piece(dw_ref, place_ref)

    row = lambda w: pl.BlockSpec((TM, w), lambda i: (i, 0))
    full = _whole
    return pl.pallas_call(
        body, name="bwd_in_dw", grid=(n_steps,),
        in_specs=[row(D_IN), row(D), _layer(l, 1, D), pl.BlockSpec(memory_space=pltpu.SMEM)] + [ANY] * len(deps)
        + [full(to_bf16.shape)],
        out_specs=[full((D_IN // 8, D)), full((D_IN, D)), full(to_bf16.shape)],
        out_shape=[jax.ShapeDtypeStruct((D_IN // 8, D), F32), jax.ShapeDtypeStruct((D_IN, D), BF16),
                   jax.ShapeDtypeStruct(to_bf16.shape, BF16)],
        scratch_shapes=[pltpu.VMEM((D_IN, D), F32)],
        compiler_params=_params(),
    )(dproj, x, g_pre, place_arr, *deps, to_bf16)


def _bwd_in_dx(l, dproj, w_in_t, x, g_pre, dres, deps=(), dw_place=None):
    n_steps = S // TM
    with_dw = dw_place is not None

    def body(dp_ref, w_ref, x_ref, g_ref, dres_ref, *rest):
        place_ref = rest[0] if with_dw else None
        rest = rest[with_dw + len(deps):]
        if with_dw:
            dx_ref, dg_ref, own_ref, dwb_ref, acc_ref, dw_ref = rest
        else:
            dx_ref, dg_ref, acc_ref = rest
        step = pl.program_id(0)

        @pl.when(step == 0)
        def _():
            acc_ref[...] = jnp.zeros_like(acc_ref)
            if with_dw:
                dw_ref[...] = jnp.zeros_like(dw_ref)

        g = g_ref[...]
        halves = [slice(k * (TM // 2), (k + 1) * (TM // 2)) for k in range(2)]
        dh = [jnp.dot(dp_ref[rows, :], w_ref[...], preferred_element_type=F32) for rows in halves]
        h = []
        for rows, dh_k in zip(halves, dh):
            xt = x_ref[rows, :]
            r = lax.rsqrt(jnp.mean(xt * xt, axis=-1, keepdims=True) + EPS)
            xn = xt * r
            acc_ref[...] += _rows8(dh_k * xn)
            a = dh_k * g
            dx_ref[rows, :] = dres_ref[rows, :] + (
                r * a - xt * (r * r * r) * jnp.mean(a * xt, axis=-1, keepdims=True))
            h.append((xn * g).astype(BF16))
        if with_dw:
            dw_ref[...] += lax.dot_general(dp_ref[...], jnp.concatenate(h, axis=0), TN, preferred_element_type=F32)

        @pl.when(step == n_steps - 1)
        def _():
            _store_lane_rows(dg_ref, acc_ref[...])
            if with_dw:
                dwb_ref[...] = dw_ref[...].astype(BF16)
                own_ref[...] = _own_piece(dw_ref, place_ref)

    row = lambda w: pl.BlockSpec((TM, w), lambda i: (i, 0))
    full = _whole
    dw_specs = [full((D_IN // 8, D)), full((D_IN, D))] if with_dw else []
    dw_shapes = [jax.ShapeDtypeStruct((D_IN // 8, D), F32), jax.ShapeDtypeStruct((D_IN, D), BF16)] if with_dw else []
    return pl.pallas_call(
        body, name="bwd_in" if with_dw else "bwd_in_dx", grid=(n_steps,),
        in_specs=[row(D_IN), full((D_IN, D)), row(D), _layer(l, 1, D), row(D)]
        + [pl.BlockSpec(memory_space=pltpu.SMEM)] * with_dw + [ANY] * len(deps),
        out_specs=[row(D), full((8, 128))] + dw_specs,
        out_shape=[jax.ShapeDtypeStruct((S, D), F32), jax.ShapeDtypeStruct((8, 128), F32)] + dw_shapes,
        scratch_shapes=[pltpu.VMEM((8, D), F32)] + [pltpu.VMEM((D_IN, D), F32)] * with_dw,
        compiler_params=_params(),
    )(dproj, w_in_t, x, g_pre, dres, *((dw_place,) if with_dw else ()), *deps)


HBM =pl.BlockSpec(memory_space=pltpu.HBM)
SEM = pl.BlockSpec(memory_space=pltpu.SEMAPHORE)
def _split_copy(collective_id=None):
    return pltpu.CompilerParams(has_side_effects=pltpu.SideEffectType.DATAFLOW_SIDE_EFFECTING,
                                collective_id=collective_id)


SPLIT_COPY = _split_copy()


def _in_hbm(a):
    return pltpu.with_memory_space_constraint(a, pltpu.HBM)

def _place():
    return lax.axis_index("x"), lax.axis_index("y"), lax.axis_index("c")


def _other_chips(x, y):
    return [(1 - x, y), (x, 1 - y), (1 - x, 1 - y)]


def _peer(x, y, c, m):
    return (x ^ (m >> 2), y ^ ((m >> 1) & 1), c ^ (m & 1))


SAME_CORE = (2, 4, 6)


def _place_cast(name, src, chip_arr, tile, layers, deps=()):
    _, n, cols = src.shape
    steps = n // tile
    k = len(layers)

    def body(chip_ref, *refs):
        for s_ref, o_ref in zip(refs[:k], refs[k + len(deps):]):
            o_ref[...] = s_ref[...].astype(BF16)

    def layer_spec(l):
        return pl.BlockSpec((None, tile, cols), lambda i, chip: (l, i, 0))

    return pl.pallas_call(
        body, name=name,
        grid_spec=pltpu.PrefetchScalarGridSpec(
            num_scalar_prefetch=1, grid=(steps,),
            in_specs=[layer_spec(l) for l in layers] + [ANY] * len(deps),
            out_specs=[pl.BlockSpec((tile, cols), lambda i, chip: (chip[0] * steps + i, 0))] * k),
        out_shape=[jax.ShapeDtypeStruct((N_SHARDS * n, cols), BF16)] * k,
        compiler_params=_params(),
    )(chip_arr, *[src] * k, *deps)


def _place_other_half(name, src, place_arr, layer, dest):
    _, n, cols = src.shape

    def body(place_ref, s_ref, dest_ref, o_ref):
        o_ref[...] = s_ref[...].astype(BF16)

    return pl.pallas_call(
        body, name=name,
        grid_spec=pltpu.PrefetchScalarGridSpec(
            num_scalar_prefetch=1, grid=(1,),
            in_specs=[pl.BlockSpec((None, n // 2, cols), lambda i, place: (layer, 1 - place[1], 0)), ANY],
            out_specs=pl.BlockSpec((n // 2, cols), lambda i, place: (place[0] + 1 - 2 * place[1], 0))),
        out_shape=jax.ShapeDtypeStruct((N_SHARDS * n, cols), BF16),
        input_output_aliases={2: 0},
        compiler_params=_params(),
    )(place_arr, src, dest)


def _chip_rows(ref, chip, half=None):
    n = ref.shape[0] // N_SHARDS
    if half is None:
        return ref.at[pl.ds(pl.multiple_of(chip * n, 16), n), :]
    return ref.at[pl.ds(pl.multiple_of(chip * n + half * (n // 2), 16), n // 2), :]


def _gather_start(name, bufs, halved, collective_id):
    n = len(bufs)

    def body(*refs):
        ins, send, recv, token = refs[:n], refs[n:2 * n], refs[2 * n:3 * n], refs[-1]
        x, y, c = _place()
        _handshake([(*chip, c) for chip in _other_chips(x, y)])
        for a, buf in enumerate(ins):
            own = _chip_rows(buf, 2 * x + y, c if a in halved else None)
            for j, chip in enumerate(_other_chips(x, y)):
                pltpu.make_async_remote_copy(src_ref=own, dst_ref=own, send_sem=send[a].at[j], recv_sem=recv[a].at[j],
                                             device_id=(*chip, c), device_id_type=MESH).start()
        token[...] = jnp.zeros_like(token)

    outs = pl.pallas_call(
        body, name=name, in_specs=[HBM] * n,
        out_specs=[SEM] * (2 * n) + [HBM] * n + [pl.BlockSpec(memory_space=pltpu.VMEM)],
        out_shape=[pltpu.SemaphoreType.DMA((3,))] * (2 * n) + [pltpu.HBM(b.shape, b.dtype) for b in bufs]
        + [jax.ShapeDtypeStruct((8, 128), F32)],
        input_output_aliases={a: 2 * n + a for a in range(n)},
        compiler_params=_split_copy(collective_id),
    )(*[_in_hbm(b) for b in bufs])
    return outs[:n], outs[n:2 * n], outs[2 * n:3 * n], outs[-1]


def _gather_start_cast(name, src, layer, collective_id):
    _, n, cols = src.shape
    half = n // 2

    def body(src_ref, send, recv, buf_ref, token, f32_ref, bf16_ref, local):
        x, y, c = _place()
        own = _chip_rows(buf_ref, 2 * x + y, c)

        def cast():
            load = pltpu.make_async_copy(src_ref.at[layer, pl.ds(pl.multiple_of(c * half, 16), half), :], f32_ref,
                                         local.at[0])
            load.start()
            load.wait()
            bf16_ref[...] = f32_ref[...].astype(BF16)
            store = pltpu.make_async_copy(bf16_ref, own, local.at[1])
            store.start()
            store.wait()

        _handshake([(*chip, c) for chip in _other_chips(x, y)], cast)
        for j, chip in enumerate(_other_chips(x, y)):
            pltpu.make_async_remote_copy(src_ref=own, dst_ref=own, send_sem=send.at[j], recv_sem=recv.at[j],
                                         device_id=(*chip, c), device_id_type=MESH).start()
        token[...] = jnp.zeros_like(token)

    outs = pl.pallas_call(
        body, name=name, in_specs=[HBM],
        out_specs=[SEM, SEM, HBM, pl.BlockSpec(memory_space=pltpu.VMEM)],
        out_shape=[pltpu.SemaphoreType.DMA((3,))] * 2 + [pltpu.HBM((N_SHARDS * n, cols), BF16),
                                                         jax.ShapeDtypeStruct((8, 128), F32)],
        scratch_shapes=[pltpu.VMEM((half, cols), F32), pltpu.VMEM((half, cols), BF16), pltpu.SemaphoreType.DMA((2,))],
        compiler_params=_split_copy(collective_id),
    )(_in_hbm(src))
    return outs[:1], outs[1:2], outs[2:3], outs[3]


def _gather_wait(name, buf, send_sem, recv_sem, after, halved=False):
    def body(buf_ref, send_ref, recv_ref, *rest):
        x, y, c = _place()
        half = c if halved else None
        own = _chip_rows(buf_ref, 2 * x + y, half)
        for j, chip in enumerate(_other_chips(x, y)):
            copy = pltpu.make_async_remote_copy(src_ref=own, dst_ref=_chip_rows(buf_ref, 2 * chip[0] + chip[1], half),
                                                send_sem=send_ref.at[j], recv_sem=recv_ref.at[j],
                                                device_id=(*chip, c), device_id_type=MESH)
            copy.wait_send()
            copy.wait_recv()

    return pl.pallas_call(
        body, name=name, in_specs=[HBM, SEM, SEM] + [ANY] * len(after), out_specs=HBM,
        out_shape=pltpu.HBM(buf.shape, buf.dtype), input_output_aliases={0: 0}, compiler_params=SPLIT_COPY,
    )(buf, send_sem, recv_sem, *after)


def _handshake(peers, meanwhile=None):
    barrier = pltpu.get_barrier_semaphore()
    for peer in peers:
        pl.semaphore_signal(barrier, inc=1, device_id=peer, device_id_type=MESH)
    if meanwhile is not None:
        meanwhile()
    pl.semaphore_wait(barrier, len(peers))


def _sibling_handshake(x, y, c):
    _handshake([(x, y, 1 - c)])


def _forward_halves(name, bufs, collective_id):
    n = len(bufs)

    def body(*refs):
        ins, outs, (send_sems, recv_sems) = refs[:n], refs[n:2 * n], refs[2 * n:]
        x, y, c = _place()
        _sibling_handshake(x, y, c)

        def copy(a, j, chip, half):
            rows = 2 * chip[0] + chip[1]
            return pltpu.make_async_remote_copy(
                src_ref=_chip_rows(ins[a], rows, half), dst_ref=_chip_rows(outs[a], rows, half),
                send_sem=send_sems.at[3 * a + j], recv_sem=recv_sems.at[3 * a + j], device_id=(x, y, 1 - c),
                device_id_type=MESH)

        copies = [(a, j, chip) for a in range(n) for j, chip in enumerate(_other_chips(x, y))]
        for a, j, chip in copies:
            copy(a, j, chip, c).start()
        for a, j, chip in copies:
            copy(a, j, chip, c).wait_send()
            copy(a, j, chip, 1 - c).wait_recv()

    return pl.pallas_call(
        body, name=name, in_specs=[ANY] * n, out_specs=[ANY] * n,
        out_shape=[jax.ShapeDtypeStruct(b.shape, b.dtype) for b in bufs],
        input_output_aliases={a: a for a in range(n)},
        scratch_shapes=[pltpu.SemaphoreType.DMA((3 * n,))] * 2,
        compiler_params=pltpu.CompilerParams(collective_id=collective_id),
    )(*bufs)


def _piece_rows(ref, k):
    p = ref.shape[0] // 8
    return ref.at[pl.ds(pl.multiple_of(k * p, 32 // jnp.dtype(ref.dtype).itemsize), p), :]


def _exchange_start(name, arrays, collective_id):
    n = len(arrays)
    zones = [lax.empty((7, a.shape[0] // 8, a.shape[1]), a.dtype) for a in arrays]

    def body(*refs):
        srcs, lands = refs[:n], refs[n:2 * n]
        send, recv = refs[2 * n:3 * n], refs[3 * n:4 * n]
        x, y, c = _place()
        _handshake([_peer(x, y, c, m) for m in range(1, 8)])
        for a, (src, land) in enumerate(zip(srcs, lands)):
            for m in range(1, 8):
                px, py, pc = _peer(x, y, c, m)
                pltpu.make_async_remote_copy(
                    src_ref=_piece_rows(src, 4 * px + 2 * py + pc), dst_ref=land.at[m - 1], send_sem=send[a].at[m - 1],
                    recv_sem=recv[a].at[m - 1], device_id=(px, py, pc), device_id_type=MESH).start()

    outs = pl.pallas_call(
        body, name=name, in_specs=[HBM] * (2 * n), out_specs=[SEM] * (2 * n) + [HBM] * (2 * n),
        out_shape=[pltpu.SemaphoreType.DMA((7,))] * (2 * n) + [pltpu.HBM(a.shape, a.dtype) for a in arrays + zones],
        input_output_aliases={a: 2 * n + a for a in range(2 * n)},
        compiler_params=_split_copy(collective_id),
    )(*[_in_hbm(a) for a in arrays + zones])
    return outs[:n], outs[n:2 * n], outs[2 * n:3 * n], outs[3 * n:4 * n]


def _exchange_wait(name, started, after, which=None):
    which = range(len(started[2])) if which is None else which
    send_sems, recv_sems, arrays, zones = [[group[k] for k in which] for group in started[:4]]
    n = len(arrays)

    def body(*refs):
        srcs, lands = refs[:n], refs[n:2 * n]
        send, recv = refs[2 * n:3 * n], refs[3 * n:4 * n]
        x, y, c = _place()
        for a, (src, land) in enumerate(zip(srcs, lands)):
            for m in range(1, 8):
                px, py, pc = _peer(x, y, c, m)
                copy = pltpu.make_async_remote_copy(
                    src_ref=_piece_rows(src, 4 * px + 2 * py + pc), dst_ref=land.at[m - 1], send_sem=send[a].at[m - 1],
                    recv_sem=recv[a].at[m - 1], device_id=(px, py, pc), device_id_type=MESH)
                copy.wait_send()
                copy.wait_recv()

    outs = pl.pallas_call(
        body, name=name, in_specs=[HBM] * (2 * n) + [SEM] * (2 * n) + [ANY], out_specs=[HBM] * (2 * n),
        out_shape=[pltpu.HBM(a.shape, a.dtype) for a in list(arrays) + list(zones)],
        input_output_aliases={a: a for a in range(2 * n)}, compiler_params=SPLIT_COPY,
    )(*arrays, *zones, *send_sems, *recv_sems, after)
    return outs[n:]


def _sum_pieces(name, weights, place_arr, dests=None):
    steps = 2
    flat = [item for items in weights for item in items]
    n = len(flat)

    def body(place_ref, *refs):
        outs = iter(refs[len(refs) - len(weights):])
        k = 0
        for items in weights:
            out_ref = next(outs)
            for layer, _, _ in items:
                total = refs[k][...]
                for m in range(7):
                    total = total + refs[n + k][m].astype(F32)
                if len(items) == DEPTH:
                    out_ref[layer] = total
                else:
                    out_ref[...] = total
                k += 1

    def out_spec(items):
        _, own, _ = items[0]
        t, cols = own.shape[0] // steps, own.shape[1]
        if len(items) == DEPTH:
            return pl.BlockSpec((DEPTH, t, cols), lambda i, place: (0, place[1] * steps + i, 0))
        layer = items[0][0]
        return pl.BlockSpec((None, t, cols), lambda i, place: (layer, place[1] * steps + i, 0))

    owns = [own for _, own, _ in flat]
    dests = [] if dests is None else list(dests)
    return pl.pallas_call(
        body, name=name,
        grid_spec=pltpu.PrefetchScalarGridSpec(
            num_scalar_prefetch=1, grid=(steps,),
            in_specs=[pl.BlockSpec((o.shape[0] // steps, o.shape[1]), lambda i, place: (i, 0)) for o in owns]
            + [pl.BlockSpec((7, o.shape[0] // steps, o.shape[1]), lambda i, place: (0, i, 0)) for o in owns]
            + [ANY] * len(dests),
            out_specs=[out_spec(items) for items in weights]),
        out_shape=[jax.ShapeDtypeStruct((DEPTH, 2 * items[0][1].shape[0], items[0][1].shape[1]), F32)
                   for items in weights],
        input_output_aliases={1 + 2 * n + k: k for k in range(len(dests))},
        compiler_params=_params(),
    )(place_arr, *owns, *[recv for _, _, recv in flat], *dests)


def _sum_small(name, partials, recvs, place_arr):
    n = len(partials)

    def body(place_ref, *refs):
        for o_ref, r_ref, out_ref in zip(refs[:n], refs[n:2 * n], refs[2 * n:]):
            total = o_ref[...]
            for m in range(7):
                total = total + r_ref[m].astype(F32)
            out_ref[...] = total

    piece = lambda a: pl.BlockSpec((a.shape[0] // 8, a.shape[1]), lambda i, place: (place[0], 0))
    return pl.pallas_call(
        body, name=name,
        grid_spec=pltpu.PrefetchScalarGridSpec(
            num_scalar_prefetch=1, grid=(1,),
            in_specs=[piece(a) for a in partials] + [pl.BlockSpec(r.shape, lambda i, place: (0, 0, 0)) for r in recvs],
            out_specs=[piece(a) for a in partials]),
        out_shape=[jax.ShapeDtypeStruct(a.shape, F32) for a in partials],
        compiler_params=_params(),
    )(place_arr, *partials, *recvs)


def _share(name, bufs, parts, gathered=(), collective_id=None, summed=(), update=None):
    n, n_g, n_s = len(bufs), len(gathered), len(summed)
    total = n + n_g
    made = [target for target, _, _ in summed]
    n_u = 0 if update is None else 7
    n_in = total + 2 * n_s + n_u

    def body(*refs):
        ins, extra, outs = refs[:total], refs[total:total + 2 * n_s], refs[n_in:n_in + total]
        first_scratch = n_in + total + (4 if update else 0)
        send_sems, recv_sems, send_g, recv_g = refs[first_scratch:first_scratch + 4]
        scratch = refs[first_scratch + 4:first_scratch + 5 + 2 * n_s]
        x, y, c = _place()

        def update_own_half(start):
            w_ref, m_ref, v_ref = refs[total + 2 * n_s:total + 2 * n_s + 3]
            dests = refs[n_in + total:n_in + total + 4]
            local, held = refs[first_scratch + 5 + 2 * n_s], refs[first_scratch + 6 + 2 * n_s:]
            layer, rows = update[3], held[0].shape[0]
            loads = [pltpu.make_async_copy(half(src, layer, c), dst, local.at[k])
                     for k, (src, dst) in enumerate(zip((w_ref, m_ref, v_ref), held))]
            if start:
                for load in loads:
                    load.start()
                return
            for load in loads:
                load.wait()
            g = scratch[1]
            for r in range(0, rows, 96):
                tile = slice(r, r + 96)
                held[0][tile, :], held[1][tile, :], held[2][tile, :] = _adamw_math(
                    held[0][tile, :], g[tile, :], held[1][tile, :], held[2][tile, :])
            stores = [pltpu.make_async_copy(src, half(dest, layer, c), local.at[3 + k])
                      for k, (src, dest) in enumerate(zip((*held, g), dests))]
            for store in stores:
                store.start()
            for store in stores:
                store.wait()

        def half(ref, l, which):
            p = ref.shape[1] // 2
            return ref.at[l, pl.ds(pl.multiple_of(which * p, 8), p), :]

        def sums():
            local, acc, got = scratch[0], scratch[1:1 + n_s], scratch[1 + n_s:]
            loads, stores = [], []
            for j, (target, _, _) in enumerate(summed):
                own_rows = extra[2 * j] if target[0] == "half" else _piece_rows(extra[2 * j], 4 * x + 2 * y + c)
                loads += [pltpu.make_async_copy(own_rows, acc[j], local.at[2 * j]),
                          pltpu.make_async_copy(extra[2 * j + 1], got[j], local.at[2 * j + 1])]
            for load in loads:
                load.start()
            if update:
                update_own_half(start=True)
            for j, (target, _, _) in enumerate(summed):
                loads[2 * j].wait()
                loads[2 * j + 1].wait()
                rows = acc[j].shape[0]
                step = min(rows, 96)
                for r in range(0, rows, step):
                    part = acc[j][r:r + step, :]
                    for m in range(7):
                        part = part + got[j][m, r:r + step, :].astype(F32)
                    acc[j][r:r + step, :] = part
                if target[0] == "half":
                    dest = half(outs[target[1]], target[2], c)
                else:
                    dest = _piece_rows(outs[n + target[1]], 4 * x + 2 * y + c)
                stores.append(pltpu.make_async_copy(acc[j], dest, local.at[2 * j]))
                stores[-1].start()
            for store in stores:
                store.wait()

        _handshake([_peer(x, y, c, m) for m in (SAME_CORE if gathered else ()) + (1,)], sums if summed else None)

        def swap(k, which):
            a, l = parts[k]
            held = outs if ("half", a, l) in made else ins
            return pltpu.make_async_remote_copy(
                src_ref=half(held[a], l, which), dst_ref=half(outs[a], l, which), send_sem=send_sems.at[k],
                recv_sem=recv_sems.at[k], device_id=(x, y, 1 - c), device_id_type=MESH)

        def spread(a, m, sender, held, to):
            k = 4 * sender[0] + 2 * sender[1] + sender[2]
            return pltpu.make_async_remote_copy(
                src_ref=_piece_rows(held[n + a], k), dst_ref=_piece_rows(outs[n + a], k),
                send_sem=send_g.at[7 * a + m - 1], recv_sem=recv_g.at[7 * a + m - 1], device_id=to, device_id_type=MESH)

        me, sibling = (x, y, c), (x, y, 1 - c)
        for k in range(len(parts)):
            swap(k, c).start()
        def own(a, m):
            return spread(a, m, me, outs if ("piece", a) in made else ins, _peer(x, y, c, m))

        def handed_on(a, m):
            return spread(a, m + 1, _peer(x, y, c, m), outs, sibling)

        for a in range(n_g):
            for m in SAME_CORE + (1,):
                own(a, m).start()
        if update:
            update_own_half(start=False)
        for a in range(n_g):
            for m in SAME_CORE:
                spread(a, m, _peer(x, y, c, m), ins, _peer(x, y, c, m)).wait_recv()
                handed_on(a, m).start()
        for k in range(len(parts)):
            swap(k, c).wait_send()
            swap(k, 1 - c).wait_recv()
        for a in range(n_g):
            for m in SAME_CORE + (1,):
                own(a, m).wait_send()
            for m in SAME_CORE:
                handed_on(a, m).wait_send()
                spread(a, m + 1, _peer(x, y, c, m + 1), ins, sibling).wait_recv()
            spread(a, 1, sibling, ins, sibling).wait_recv()

    arrays = list(bufs) + list(gathered)
    sum_scratch = []
    if summed:
        sum_scratch = [pltpu.SemaphoreType.DMA((2 * n_s,))] + [pltpu.VMEM(recv.shape[1:], F32) for _, _, recv in summed]
        sum_scratch += [pltpu.VMEM(recv.shape, recv.dtype) for _, _, recv in summed]
    updated, update_scratch = [], []
    if update:
        w, m, v, _, dests = update
        updated = [w, m, v, *dests]
        arrays += list(dests)
        half_rows = (w.shape[1] // 2, w.shape[2])
        update_scratch = [pltpu.SemaphoreType.DMA((7,))] + [pltpu.VMEM(half_rows, F32)] * 3
    return pl.pallas_call(
        body, name=name, in_specs=[ANY] * n_in, out_specs=[ANY] * len(arrays),
        out_shape=[jax.ShapeDtypeStruct(b.shape, F32) for b in arrays],
        input_output_aliases={**{a: a for a in range(total)}, **{n_in - 4 + k: total + k for k in range(4 if update else 0)}},
        scratch_shapes=[pltpu.SemaphoreType.DMA((max(len(parts), 1),))] * 2
        + [pltpu.SemaphoreType.DMA((max(7 * n_g, 1),))] * 2 + sum_scratch + update_scratch,
        compiler_params=pltpu.CompilerParams(collective_id=collective_id, vmem_limit_bytes=VMEM_LIMIT),
    )(*arrays[:total], *[array for _, own, recv in summed for array in (own, recv)], *updated)


def _adamw_math(w, g, m, v):
    nm = ADAM_B1 * m + (1.0 - ADAM_B1) * g
    nv = ADAM_B2 * v + (1.0 - ADAM_B2) * (g * g)
    m_hat = nm / (1.0 - ADAM_B1 ** ADAM_STEP)
    v_hat = nv / (1.0 - ADAM_B2 ** ADAM_STEP)
    return -ADAM_LR * (m_hat / (jnp.sqrt(v_hat) + ADAM_EPS) + ADAM_WD * w), nm, nv


def _adamw(name, w, g, m, v, rows_per_step, first=0, count=None, dests=None, deps=(), small=(), other_half=None):
    layers, rows, cols = w.shape
    count = layers if count is None else count
    dests = () if dests is None else tuple(dests)
    n_in = 4 + len(dests) + len(deps)
    small_shapes = _small_shapes(small[4].shape) if small else []
    prefetch = [] if other_half is None else [other_half]
    steps = (rows if other_half is None else rows // 2) // rows_per_step

    def body(*refs):
        refs = refs[len(prefetch):]
        w_ref, g_ref, m_ref, v_ref = refs[:4]
        d_ref, nm_ref, nv_ref, g_out_ref = refs[n_in + len(small):n_in + len(small) + 4]
        d_ref[...], nm_ref[...], nv_ref[...] = _adamw_math(w_ref[...], g_ref[...], m_ref[...], v_ref[...])
        g_out_ref[...] = g_ref[...]
        if small:
            @pl.when((pl.program_id(0) == 0) & (pl.program_id(1) == 0))
            def _():
                _adamw_small(*refs[n_in:n_in + len(small)], *refs[n_in + len(small) + 4:])

    spec = pl.BlockSpec((1, rows_per_step, cols),
                        lambda l, i, *place: (first + l, (1 - place[0][1]) * steps + i if place else i, 0))
    whole = lambda shape: pl.BlockSpec(shape, lambda l, i, *place: (0,) * len(shape))
    shape = jax.ShapeDtypeStruct(w.shape, F32)
    return pl.pallas_call(
        body, name=name,
        grid_spec=pltpu.PrefetchScalarGridSpec(
            num_scalar_prefetch=len(prefetch), grid=(count, steps),
            in_specs=[spec] * 4 + [ANY] * (len(dests) + len(deps)) + [whole(a.shape) for a in small],
            out_specs=[spec] * 4 + [whole(s) for s in small_shapes],
            scratch_shapes=[pltpu.VMEM((MISC_ROWS, 128), F32)] * 3 if small else []),
        out_shape=[shape] * 4 + [jax.ShapeDtypeStruct(s, F32) for s in small_shapes],
        input_output_aliases={len(prefetch) + 4 + k: k for k in range(len(dests))},
        compiler_params=_params(("arbitrary", "arbitrary")),
    )(*prefetch, w, g, m, v, *dests, *deps, *small)


def _pack_misc(pool_scale, sinks, norm_pre, norm_post):
    sink_rows = jnp.zeros((DEPTH, 8, 128), F32).at[:, 0, 0:N_HEADS].set(sinks).reshape(2 * 8, 128)
    return jnp.concatenate([pool_scale.reshape(8, 128), norm_pre.reshape(16, 128), norm_post.reshape(16, 128),
                            sink_rows, jnp.zeros((8, 128), F32)], axis=0)


def _adamw_small(w_ref, g_ref, m_ref, v_ref, pw_ref, pg_ref, pm_ref, pv_ref, *rest):
    outs, pool_outs, (d_ref, nm_ref, nv_ref) = rest[:17], rest[17:21], rest[21:]
    pool_outs[0][...] = pg_ref[...]
    pool_outs[1][...], pool_outs[2][...], pool_outs[3][...] = _adamw_math(
        pw_ref[...], pg_ref[...], pm_ref[...], pv_ref[...])
    d_ref[...], nm_ref[...], nv_ref[...] = _adamw_math(w_ref[...], g_ref[...], m_ref[...], v_ref[...])
    for k, src in enumerate([g_ref, d_ref, nm_ref, nv_ref]):
        scale, sinks, pre, post = outs[4 * k:4 * k + 4]
        for l in range(DEPTH):
            for j in range(4):
                scale[l:l + 1, j * 128:(j + 1) * 128] = src[MISC_SCALE + 4 * l + j:MISC_SCALE + 4 * l + j + 1, :]
            for j in range(8):
                pre[l:l + 1, j * 128:(j + 1) * 128] = src[MISC_PRE + 8 * l + j:MISC_PRE + 8 * l + j + 1, :]
                post[l:l + 1, j * 128:(j + 1) * 128] = src[MISC_POST + 8 * l + j:MISC_POST + 8 * l + j + 1, :]
            sinks[l:l + 1, :] = src[MISC_SINKS + 8 * l:MISC_SINKS + 8 * l + 1, 0:N_HEADS]
    outs[16][...] = g_ref[MISC_LOSS:MISC_LOSS + 1, 0:1]


def _small_shapes(pool_shape):
    return [(DEPTH, D_POOL), (DEPTH, N_HEADS), (DEPTH, D), (DEPTH, D)] * 4 + [(1, 1)] + [pool_shape] * 4


def kernel(x, w_in, pool_w, pool_scale, attn_sinks, w_out, norm_pre, norm_post, loss_target, m_w_in, m_pool_w, m_pool_scale, m_attn_sinks, m_w_out, m_norm_pre, m_norm_post, v_w_in, v_pool_w, v_pool_scale, v_attn_sinks, v_w_out, v_norm_pre, v_norm_post):
    cx, cy, cc = _place()
    chip_arr = jnp.reshape(2 * cx + cy, (1,)).astype(jnp.int32)
    place_arr = jnp.stack([4 * cx + 2 * cy + cc, cc]).astype(jnp.int32)
    t = lambda a: jnp.transpose(a, (0, 2, 1))
    w_in_t = t(w_in)
    xs, target = x[0], loss_target[0]
    pool_w_b = pool_w.astype(BF16)
    tables = _attention_tables()
    scale3 = pool_scale.reshape(DEPTH, 1, D_POOL)
    pre3 = norm_pre.reshape(DEPTH, 1, D)
    post3 = norm_post.reshape(DEPTH, 1, D)

    first = _gather_start_cast("gather_start_first", w_in_t, 0, ID_GATHER_FIRST)
    wi0 = _place_other_half("place_w_in0_rest", w_in_t, place_arr, 0, first[2][0])
    (wi1,) = _place_cast("place_w_in1", w_in_t, chip_arr, 288, [1], deps=(first[3],))
    wo = _place_cast("place_w_out", w_out, chip_arr, 256, [0, 1], deps=(first[3],))
    rest = _gather_start("gather_start_rest", [wi1, wo[0], wo[1]], halved=(0, 1), collective_id=ID_GATHER_REST)
    send, recv, bufs = [first[k] + rest[k] for k in range(3)]
    bufs = [wi0, *bufs[1:]]
    order = {(0, "in"): 0, (1, "in"): 1, (0, "out"): 2, (1, "out"): 3}

    saved = []
    packed = [_pack_misc(pool_scale, attn_sinks, norm_pre, norm_post),
              _pack_misc(m_pool_scale, m_attn_sinks, m_norm_pre, m_norm_post),
              _pack_misc(v_pool_scale, v_attn_sinks, v_norm_pre, v_norm_post)]
    after = (first[3], rest[3], pool_w_b, *tables, scale3, pre3, post3, *packed)
    below = None
    for l in range(DEPTH):
        k = order[l, "in"]
        halves = [_gather_wait(f"gather_wait_in{l}", bufs[k], send[k], recv[k], after, halved=True)]
        if below is not None:
            halves.append(below[1])
        w_in_l, *w_out_below = _forward_halves(f"forward_w{l}", halves, collective_id=ID_FORWARD[l])
        if below is None:
            pu, pg, q, kv, ag = _fwd_in(l, xs, pre3, w_in_l)
        else:
            saved[l - 1][9] = w_out_below[0]
            y, xs, pu, pg, q, kv, ag = _fwd_in(l, xs, pre3, w_in_l, (below[0], w_out_below[0], below[2]))
            saved[l - 1][7] = y
        cat = _fwd_mix(l, pu, pg, q, kv, ag, pool_w_b, scale3, attn_sinks, tables)
        k = order[l, "out"]
        w_out_l = _gather_wait(f"gather_wait_out{l}", bufs[k], send[k], recv[k], (cat,), halved=l + 1 < DEPTH)
        saved.append([xs, pu, pg, q, kv, ag, cat, None, w_in_l, w_out_l])
        below, after = (cat, w_out_l, post3), (w_out_l,)

    x_in, pu, pg, q, kv, ag, cat, y, w_in_l, w_out_l = saved[1]
    dcat, dw_out1, dw_out1_b, dg_post1, loss, xs = _bwd_out(1, cat, w_out_l, post3, place_arr, x=x_in, target=target)
    ex1_out = _exchange_start("exchange_start_out1", [dw_out1_b], ID_OUT1)
    dproj, dpw, dsc1, dsink1 = _bwd_mix(1, pu, pg, q, kv, ag, dcat, pool_w_b, scale3, attn_sinks, tables,
                                        deps=(ex1_out[2][0],))
    dx, dg_pre1, dw_in1, dw_in1_b = _bwd_in_dx(1, dproj, w_in_l, x_in, pre3, xs, dw_place=place_arr)
    ex1_in = _exchange_start("exchange_start_in1", [dw_in1_b], ID_IN1)

    x_in, pu, pg, q, kv, ag, cat, y, w_in_l, w_out_l = saved[0]
    dcat, dw_out0, dw_out0_b, dg_post0 = _bwd_out(0, cat, w_out_l, post3, place_arr, dxn=dx, y=y, deps=(ex1_in[2][0],))
    ex0_out = _exchange_start("exchange_start_out0", [dw_out0_b], ID_OUT0)
    dproj, dpw, dsc0, dsink0 = _bwd_mix(0, pu, pg, q, kv, ag, dcat, pool_w_b, scale3, attn_sinks, tables,
                                        deps=(ex0_out[2][0],), dpw_dest=dpw)
    flat = lambda a: a.reshape(DEPTH * 4 * 128, 128)
    dw_in0, dw_in0_b, dpw_b = _bwd_in_dw(0, dproj, x_in, pre3, place_arr, flat(dpw))
    ex0_in = _exchange_start("exchange_start_in0", [dpw_b, dw_in0_b], ID_IN0)

    grad_x, dg_pre0 = _bwd_in_dx(0, dproj, w_in_l, x_in, pre3, dx, deps=(ex0_in[2][1],))
    small = [jnp.concatenate([dsc0, dsc1, dg_pre0, dg_pre1, dg_post0, dg_post1, dsink0, dsink1, loss], axis=0)]
    ex_small = _exchange_start("exchange_start_small", small, ID_SMALL)
    (recv_out1,) = _exchange_wait("exchange_wait_out1", ex1_out, ex_small[2][0])
    (recv_in1,) = _exchange_wait("exchange_wait_in1", ex1_in, recv_out1)
    g_in, g_out = _sum_pieces("sum_pieces_1", [[(1, dw_in1, recv_in1)], [(1, dw_out1, recv_out1)]], place_arr)
    (recv_out0,) = _exchange_wait("exchange_wait_out0", ex0_out, g_out)
    (g_out,) = _sum_pieces("sum_pieces_out0", [[(0, dw_out0, recv_out0)]], place_arr, dests=[g_out])
    g_in, g_out = _share("share_a", [g_in, g_out], [(0, 1), (1, 0), (1, 1)], collective_id=ID_SHARE_A)
    m_in_t, v_in_t = t(m_w_in), t(v_w_in)
    d_out, nm_out, nv_out, grad_w_out = _adamw("adamw_w_out", w_out, g_out, m_w_out, v_w_out, 256)
    upd_in = _adamw("adamw_w_in1", w_in_t, g_in, m_in_t, v_in_t, 288, first=1, count=1, deps=(d_out,))
    (recv_pw,) = _exchange_wait("exchange_wait_pool", ex0_in, upd_in[0], which=[0])
    (g_pw,) = _sum_small("sum_pool", [flat(dpw)], [recv_pw], place_arr)

    (recv_in0,) = _exchange_wait("exchange_wait_in0", ex0_in, g_pw, which=[1])
    (recv_misc,) = _exchange_wait("exchange_wait_small", ex_small, recv_in0)
    g_in, g_pw, g_misc, *upd_in = _share(
        "share_b", [g_in], [(0, 0)], [g_pw, lax.empty(small[0].shape, F32)], collective_id=ID_SHARE_B,
        summed=[(("half", 0, 0), dw_in0, recv_in0), (("piece", 1), small[0], recv_misc)],
        update=(w_in_t, m_in_t, v_in_t, 0, upd_in))
    d_in, nm_in, nv_in, grad_w_in_t, *small_out = _adamw(
        "adamw_w_in0", w_in_t, g_in, m_in_t, v_in_t, 144, first=0, count=1, dests=upd_in, other_half=place_arr,
        small=(packed[0], g_misc, packed[1], packed[2], flat(pool_w), g_pw, flat(m_pool_w), flat(v_pool_w)))
    (g_sc, g_sk, g_pre, g_post, d_sc, d_sk, d_pre, d_post,
     m_sc, m_sk, m_pre, m_post, v_sc, v_sk, v_pre, v_post, loss_sum) = small_out[:17]
    g_pw, d_pw, m_pw, v_pw = [a.reshape(pool_w.shape) for a in small_out[17:]]
    return (loss_sum[0, 0], grad_x[None], t(grad_w_in_t), g_pw, g_sc, g_sk, grad_w_out, g_pre, g_post,
            t(d_in), d_pw, d_sc, d_sk, d_out, d_pre, d_post,
            t(nm_in), m_pw, m_sc, m_sk, nm_out, m_pre, m_post,
            t(nv_in), v_pw, v_sc, v_sk, nv_out, v_pre, v_post)
```

```python
import jax
import jax.numpy as jnp
from jax import lax
from jax.experimental import pallas as pl
from jax.experimental.pallas import tpu as pltpu

F32 = jnp.float32
BF16 = jnp.bfloat16

S = 2048
D = 1024
DEPTH = 2
D_POOL = 512
POOL_WINDOWS = (2, 4, 8, 16)
N_HEADS = 8
D_IN = 2304
N_SHARDS = 4
W_IN_SHARD = D_IN // N_SHARDS
W_OUT_SHARD = D // N_SHARDS
BLK = 128
NB = S // BLK
HALO = 16
PAD = 8
EPS = 1e-6
NEG_INF = -1e30
C_PU, C_PG, C_Q, C_K, C_V, C_AG = 0, 512, 1024, 1536, 1664, 1792

ADAM_LR = 0.001
ADAM_B1 = 0.9
ADAM_B2 = 0.999
ADAM_EPS = 1e-08
ADAM_WD = 0.01
ADAM_STEP = 10

TM = 512
VMEM_LIMIT = 56 * 1024 * 1024

NT = (((1,), (1,)), ((), ()))
TN = (((0,), (0,)), ((), ()))

MESH = pl.DeviceIdType.MESH
ANY = pl.BlockSpec(memory_space=pl.ANY)

ID_FORWARD = (0, 1)
(ID_SHARE_A, ID_SHARE_B, ID_GATHER_FIRST, ID_GATHER_REST, ID_OUT1, ID_IN1, ID_OUT0, ID_IN0, ID_SMALL) = range(2, 11)

MISC_SCALE, MISC_PRE, MISC_POST, MISC_SINKS, MISC_LOSS = 0, 8, 24, 40, 56
MISC_ROWS = 64


def _params(sem=("arbitrary",)):
    return pltpu.CompilerParams(dimension_semantics=sem, vmem_limit_bytes=VMEM_LIMIT)


def _sigmoid(v):
    return 1.0 / (1.0 + jnp.exp(-v))


def _rows8(v):
    r, c = v.shape
    return v.reshape(r // 8, 8, c).sum(axis=0)


def _layer(l, *shape):
    zeros = (0,) * len(shape)
    return pl.BlockSpec((None,) + shape, lambda i: (l,) + zeros)


def _whole(shape):
    zeros = (0,) * len(shape)
    return pl.BlockSpec(shape, lambda i: zeros, pipeline_mode=pl.Buffered(1))


def _fwd_in(l, x, g_pre, w_in_t, below=None):
    fused = below is not None

    def body(x_ref, g_ref, w_ref, *rest):
        if fused:
            cat_ref, wo_ref, gp_ref, y_ref, xn_ref = rest[:5]
            y = jnp.dot(cat_ref[...], wo_ref[...], preferred_element_type=F32)
            y_ref[...] = y
            xt = x_ref[...] + y * lax.rsqrt(jnp.mean(y * y, axis=-1, keepdims=True) + EPS) * gp_ref[...]
            xn_ref[...] = xt
        else:
            xt = x_ref[...]
        pu_ref, pg_ref, q_ref, kv_ref, ag_ref = rest[-5:]
        r = lax.rsqrt(jnp.mean(xt * xt, axis=-1, keepdims=True) + EPS)
        h = (xt * r * g_ref[...]).astype(BF16)

        def proj(lo, hi):
            return lax.dot_general(h, w_ref[lo:hi, :], NT, preferred_element_type=F32)

        pu_ref[...] = proj(C_PU, C_PG)
        pg_ref[...] = proj(C_PG, C_Q)
        q_ref[...] = proj(C_Q, C_K).astype(BF16)
        kv_ref[...] = proj(C_K, C_AG).astype(BF16)
        ag_ref[...] = proj(C_AG, D_IN)

    row = lambda w: pl.BlockSpec((TM, w), lambda i: (i, 0))
    act = jax.ShapeDtypeStruct((S, D), F32)
    return pl.pallas_call(
        body, name="fwd_out_in" if fused else "fwd_in", grid=(S // TM,),
        in_specs=[row(D), _layer(l, 1, D), _whole((D_IN, D))]
        + ([row(D), _whole((D, D)), _layer(l - 1, 1, D)] if fused else []),
        out_specs=[row(D)] * (2 * fused) + [row(512), row(512), row(512), row(256), row(512)],
        out_shape=[act] * (2 * fused)
        + [jax.ShapeDtypeStruct((S, 512), F32), jax.ShapeDtypeStruct((S, 512), F32),
           jax.ShapeDtypeStruct((S, 512), BF16), jax.ShapeDtypeStruct((S, 256), BF16),
           jax.ShapeDtypeStruct((S, 512), F32)],
        compiler_params=_params(),
    )(x, g_pre, w_in_t, *(below if fused else ()))


LOG2E = 1.4426950408889634
SCORE_SCALE = 0.125 * LOG2E


def _attention_tables():
    qi = jnp.arange(BLK)[:, None]
    kj = jnp.arange(BLK)[None, :]
    dist = ((qi - kj) % BLK).astype(F32)
    slopes = jnp.exp2(-jnp.arange(1, N_HEADS + 1, dtype=F32))
    bias = -(slopes * LOG2E)[:, None, None] * dist[None]
    first = jnp.where(kj > qi, NEG_INF, bias)
    return jnp.stack([first, bias]), (kj <= qi).astype(BF16)


def _own_block_mask():
    return lax.broadcasted_iota(jnp.int32, (BLK, BLK), 1) <= lax.broadcasted_iota(jnp.int32, (BLK, BLK), 0)


def _merge(full, own):
    return jnp.where(own, full[:, BLK:], full[:, :BLK])


def _spread(v, tri):
    own = v * tri
    return jnp.concatenate([v - own, own], axis=1)


def _head_variants(cur, prev):
    both = jnp.concatenate([prev, cur], axis=0).astype(F32)
    swapped = pltpu.roll(both, 64, axis=1)
    low = lax.broadcasted_iota(jnp.int32, both.shape, 1) < 64
    zero = jnp.zeros_like(both)
    return ((jnp.where(low, both, zero).astype(BF16), jnp.where(low, zero, swapped).astype(BF16)),
            (jnp.where(low, swapped, zero).astype(BF16), jnp.where(low, zero, both).astype(BF16)))


def _head_of(hkv, t, half):
    return hkv * 4 + 2 * t + half


def _rows(v, t):
    return v[t * BLK:(t + 1) * BLK]


def _stack_tiles(ref, hkv, offset=0):
    lo = offset + 2 * hkv * 128
    return jnp.concatenate([ref[:, lo:lo + 128], ref[:, lo + 128:lo + 256]], axis=0)


def _scores(q2, k_var, own):
    s = {}
    for hkv in range(2):
        for half in range(2):
            full = lax.dot_general(q2[hkv], k_var[hkv][half], NT, preferred_element_type=F32)
            for t in range(2):
                s[hkv, t, half] = _merge(_rows(full, t), own)
    return s


def _softmax(s, bias, sink):
    s = s * SCORE_SCALE + bias
    sink2 = sink * LOG2E
    m = jnp.maximum(jnp.max(s, axis=-1, keepdims=True), sink2)
    p = jnp.exp2(s - m)
    e_sink = jnp.exp2(sink2 - m)
    inv = 1.0 / (jnp.sum(p, axis=-1, keepdims=True) + e_sink)
    return p * inv, e_sink * inv


def _spread_pair(v, hkv, half, tri):
    return jnp.concatenate([_spread(v[hkv, t, half].astype(BF16), tri) for t in range(2)], axis=0)


POOL_ROWS = PAD + HALO + BLK


def _window_sums(src_ref, tmp_refs, trailing):
    lo, hi = (PAD, POOL_ROWS) if trailing else (0, HALO + BLK)
    cur = src_ref
    for level in range(len(POOL_WINDOWS)):
        lanes = slice(level * 128, 512)
        shift = -(1 << level) if trailing else (1 << level)
        dst = tmp_refs[level % 2]
        dst[lo:hi, lanes] = cur[lo:hi, lanes] + cur[lo + shift:hi + shift, lanes]
        cur = dst


def _pool_block(ext_ref, tmp_refs, i, g, w):
    lanes = slice(g * 128, (g + 1) * 128)
    rows = slice(PAD + HALO, POOL_ROWS)
    t = (i * BLK + lax.broadcasted_iota(jnp.int32, (BLK, 1), 0)).astype(F32)
    inv = 1.0 / jnp.minimum(t + 1.0, float(w))
    return tmp_refs[g % 2][rows, lanes] * inv - ext_ref[rows, lanes], inv


def _fwd_mix(l, pu, pg, q, kv, ag, pool_w, pool_scale, sinks, tables):
    bias, tri = tables

    def body(pu_ref, pup_ref, pg_ref, q_ref, kv_ref, kvp_ref, ag_ref, pw_ref, sc_ref, sink_ref, bias_ref, tri_ref,
             cat_ref, ext_ref, *tmp_refs):
        i = pl.program_id(0)

        @pl.when(i == 0)
        def _():
            for ref in (ext_ref, *tmp_refs):
                ref[0:PAD, :] = jnp.zeros((PAD, 512), F32)

        ext_ref[PAD:PAD + HALO, :] = jnp.where(i > 0, pup_ref[...], 0.0)
        ext_ref[PAD + HALO:POOL_ROWS, :] = pu_ref[...]
        _window_sums(ext_ref, tmp_refs, True)
        for g, w in enumerate(POOL_WINDOWS):
            lanes = slice(g * 128, (g + 1) * 128)
            pooled, _ = _pool_block(ext_ref, tmp_refs, i, g, w)
            mixed = jnp.dot(pooled.astype(BF16), pw_ref[g], preferred_element_type=F32)
            gate = pg_ref[:, lanes]
            cat_ref[:, lanes] = (mixed * sc_ref[:, lanes] * (gate * _sigmoid(gate))).astype(BF16)

        own = _own_block_mask()
        tri = tri_ref[...]
        k_var = _head_variants(kv_ref[:, 0:128], kvp_ref[:, 0:128])
        v_var = _head_variants(kv_ref[:, 128:256], kvp_ref[:, 128:256])
        s = _scores([_stack_tiles(q_ref, hkv) for hkv in range(2)], k_var, own)
        p = {}
        for (hkv, t, half), s_head in s.items():
            head = _head_of(hkv, t, half)
            p[hkv, t, half], _ = _softmax(s_head, bias_ref[head], sink_ref[l, head])
        for hkv in range(2):
            o2 = jnp.zeros((2 * BLK, 128), F32)
            for half in range(2):
                o2 = o2 + jnp.dot(_spread_pair(p, hkv, half, tri), v_var[hkv][half], preferred_element_type=F32)
            for t in range(2):
                lo = (2 * hkv + t) * 128
                gate = ag_ref[:, lo:lo + 128]
                cat_ref[:, D_POOL + lo:D_POOL + lo + 128] = (_rows(o2, t) * (gate * _sigmoid(gate))).astype(BF16)

    blk = lambda w: pl.BlockSpec((BLK, w), lambda i: (i, 0))
    prev = lambda w: pl.BlockSpec((BLK, w), lambda i: (jnp.maximum(i - 1, 0), 0))
    halo = pl.BlockSpec((HALO, 512), lambda i: (jnp.maximum(i * (BLK // HALO) - 1, 0), 0))
    return pl.pallas_call(
        body, name="fwd_mix", grid=(NB,),
        in_specs=[blk(512), halo, blk(512), blk(512), blk(256), prev(256), blk(512),
                  _layer(l, 4, 128, 128), _layer(l, 1, 512), pl.BlockSpec(memory_space=pltpu.SMEM),
                  pl.BlockSpec((None, N_HEADS, BLK, BLK), lambda i: (jnp.minimum(i, 1), 0, 0, 0)), _whole((BLK, BLK))],
        out_specs=blk(D),
        out_shape=jax.ShapeDtypeStruct((S, D), BF16),
        scratch_shapes=[pltpu.VMEM((POOL_ROWS, 512), F32)] * 3,
        compiler_params=_params(),
    )(pu, pu, pg, q, kv, kv, ag, pool_w, pool_scale, sinks, bias, tri)


def _store_lane_rows(ref, acc):
    total = jnp.sum(acc, axis=0, keepdims=True)
    for k in range(ref.shape[0]):
        ref[k:k + 1, :] = total[:, k * 128:(k + 1) * 128]


def _own_piece(dw_ref, place_ref):
    p = dw_ref.shape[0] // 8
    return dw_ref[pl.ds(pl.multiple_of(place_ref[0] * p, 8), p), :]


def _bwd_out(l, cat, w_out, g_post, place_arr, dxn=None, y=None, x=None, target=None, deps=()):
    last = target is not None
    n_steps = S // TM

    def body(a_ref, b_ref, g_ref, cat_ref, w_ref, place_ref, *rest):
        dcat_ref, own_ref, dwb_ref, dg_ref = rest[len(deps):len(deps) + 4]
        rest = rest[len(deps) + 4:]
        acc_ref, dw_ref = rest[-2:]
        step = pl.program_id(0)

        @pl.when(step == 0)
        def _():
            dw_ref[...] = jnp.zeros_like(dw_ref)
            acc_ref[...] = jnp.zeros_like(acc_ref)

        cat = cat_ref[...]
        g = g_ref[...]
        y = jnp.dot(cat, w_ref[...], preferred_element_type=F32) if last else b_ref[...]
        r = lax.rsqrt(jnp.mean(y * y, axis=-1, keepdims=True) + EPS)
        if last:
            loss_ref, dx_ref, loss_acc_ref = rest[:3]
            err = a_ref[...] + y * r * g - b_ref[...]

            @pl.when(step == 0)
            def _():
                loss_acc_ref[...] = jnp.zeros_like(loss_acc_ref)

            loss_acc_ref[...] += _rows8(err * err)
            dz = err * (1.0 / D)
            dx_ref[...] = dz
        else:
            dz = a_ref[...]
        a = dz * g
        dy = r * a - y * (r * r * r) * jnp.mean(a * y, axis=-1, keepdims=True)
        acc_ref[...] += _rows8(dz * (y * r))
        dyb = dy.astype(BF16)
        dcat_ref[...] = lax.dot_general(dyb, w_ref[...], NT, preferred_element_type=F32)
        dw_ref[...] += lax.dot_general(cat, dyb, TN, preferred_element_type=F32)

        @pl.when(step == n_steps - 1)
        def _():
            _store_lane_rows(dg_ref, acc_ref[...])
            dwb_ref[...] = dw_ref[...].astype(BF16)
            own_ref[...] = _own_piece(dw_ref, place_ref)
            if last:
                loss_ref[...] = jnp.full((8, 128), (0.5 / D) * jnp.sum(loss_acc_ref[...]), F32)

    row = lambda: pl.BlockSpec((TM, D), lambda i: (i, 0))
    full = _whole
    return pl.pallas_call(
        body, name="out_loss_bwd" if last else "bwd_out", grid=(n_steps,),
        in_specs=[row(), row(), _layer(l, 1, D), row(), full((D, D)), pl.BlockSpec(memory_space=pltpu.SMEM)]
        + [ANY] * len(deps),
        out_specs=[row(), full((D // 8, D)), full((D, D)), full((8, 128))] + ([full((8, 128)), row()] if last else []),
        out_shape=[jax.ShapeDtypeStruct((S, D), F32), jax.ShapeDtypeStruct((D // 8, D), F32),
                   jax.ShapeDtypeStruct((D, D), BF16), jax.ShapeDtypeStruct((8, 128), F32)]
        + ([jax.ShapeDtypeStruct((8, 128), F32), jax.ShapeDtypeStruct((S, D), F32)] if last else []),
        scratch_shapes=([pltpu.VMEM((8, D), F32)] if last else []) + [pltpu.VMEM((8, D), F32), pltpu.VMEM((D, D), F32)],
        compiler_params=_params(),
    )(*((x, target) if last else (dxn, y)), g_post, cat, w_out, place_arr, *deps)


def _bwd_mix(l, pu, pg, q, kv, ag, dcat, pool_w, pool_scale, sinks, tables, deps=(), dpw_dest=None):
    bias, tri = tables
    deps = tuple(deps) + (() if dpw_dest is None else (dpw_dest,))

    def body(pu_ref, pup_ref, pg_ref, q_ref, kv_ref, kvp_ref, ag_ref, dcat_ref, pw_ref, sc_ref, sink_ref, bias_ref,
             tri_ref, *rest):
        dproj_ref, dpw_ref, dsc_ref, dsink_ref, ext_ref, dext_ref, tmp_a, tmp_b, dkv_ref = rest[len(deps):]
        tmp_refs = (tmp_a, tmp_b)
        step = pl.program_id(0)
        i = NB - 1 - step

        @pl.when(step == 0)
        def _():
            dpw_ref[...] = jnp.zeros_like(dpw_ref)
            dsc_ref[...] = jnp.zeros_like(dsc_ref)
            dsink_ref[...] = jnp.zeros_like(dsink_ref)
            for ref in (ext_ref, tmp_a, tmp_b):
                ref[0:PAD, :] = jnp.zeros((PAD, 512), F32)
            dext_ref[BLK:POOL_ROWS, :] = jnp.zeros((HALO + PAD, 512), F32)
            dkv_ref[...] = jnp.zeros_like(dkv_ref)

        ext_ref[PAD:PAD + HALO, :] = jnp.where(i > 0, pup_ref[...], 0.0)
        ext_ref[PAD + HALO:POOL_ROWS, :] = pu_ref[...]
        _window_sums(ext_ref, tmp_refs, True)
        dpooled = []
        for g, w in enumerate(POOL_WINDOWS):
            lanes = slice(g * 128, (g + 1) * 128)
            pooled, inv = _pool_block(ext_ref, tmp_refs, i, g, w)
            pooled_b = pooled.astype(BF16)
            mixed = jnp.dot(pooled_b, pw_ref[g], preferred_element_type=F32)
            scale = sc_ref[:, lanes]
            gate = pg_ref[:, lanes]
            sg = _sigmoid(gate)
            dpo = dcat_ref[:, lanes]
            dproj_ref[:, C_PG + g * 128:C_PG + (g + 1) * 128] = (
                dpo * (mixed * scale) * (sg * (1.0 + gate * (1.0 - sg)))).astype(BF16)
            dms = dpo * (gate * sg)
            dsc_ref[g:g + 1, :] += jnp.sum(dms * mixed, axis=0, keepdims=True)
            dmixed = (dms * scale).astype(BF16)
            dpw_ref[g] += lax.dot_general(pooled_b, dmixed, TN, preferred_element_type=F32)
            dpooled.append(lax.dot_general(dmixed, pw_ref[g], NT, preferred_element_type=F32))
            dext_ref[0:BLK, lanes] = dpooled[g] * inv
        _window_sums(dext_ref, tmp_refs, False)
        for g in range(len(POOL_WINDOWS)):
            lanes = slice(g * 128, (g + 1) * 128)
            dproj_ref[:, C_PU + g * 128:C_PU + (g + 1) * 128] = (tmp_refs[g % 2][0:BLK, lanes] - dpooled[g]).astype(BF16)
        dext_ref[BLK:BLK + HALO, :] = dext_ref[0:HALO, :]

        own = _own_block_mask()
        tri = tri_ref[...]
        k_var = _head_variants(kv_ref[:, 0:128], kvp_ref[:, 0:128])
        v_var = _head_variants(kv_ref[:, 128:256], kvp_ref[:, 128:256])
        q2 = [_stack_tiles(q_ref, hkv) for hkv in range(2)]
        s = _scores(q2, k_var, own)
        p, p_sink = {}, {}
        for key, s_head in s.items():
            head = _head_of(*key)
            p[key], p_sink[key] = _softmax(s_head, bias_ref[head], sink_ref[l, head])

        do2, p_b, dp = [], {}, {}
        for hkv in range(2):
            gate = _stack_tiles(ag_ref, hkv)
            sg = _sigmoid(gate)
            dca = _stack_tiles(dcat_ref, hkv, D_POOL)
            do2.append((dca * (gate * sg)).astype(BF16))
            o2 = jnp.zeros((2 * BLK, 128), F32)
            for half in range(2):
                p_b[hkv, half] = _spread_pair(p, hkv, half, tri)
                o2 = o2 + jnp.dot(p_b[hkv, half], v_var[hkv][half], preferred_element_type=F32)
                full = lax.dot_general(do2[hkv], v_var[hkv][half], NT, preferred_element_type=F32)
                for t in range(2):
                    dp[hkv, t, half] = _merge(_rows(full, t), own)
            dag = dca * o2 * (sg * (1.0 + gate * (1.0 - sg)))
            for t in range(2):
                lo = C_AG + (2 * hkv + t) * 128
                dproj_ref[:, lo:lo + 128] = _rows(dag, t).astype(BF16)

        ds = {}
        for key in p:
            delta = jnp.sum(p[key] * dp[key], axis=-1, keepdims=True)
            ds[key] = p[key] * (dp[key] - delta)
            head = _head_of(*key)
            dsink_ref[0:1, :] += jnp.where(lax.broadcasted_iota(jnp.int32, (1, 128), 1) == head,
                                           -jnp.sum(p_sink[key] * delta, axis=0, keepdims=True), 0.0)

        dk_acc = [[None, None], [None, None]]
        dv_acc = [[None, None], [None, None]]
        for hkv in range(2):
            dq2 = jnp.zeros((2 * BLK, 128), F32)
            for half in range(2):
                ds_b = _spread_pair(ds, hkv, half, tri)
                dq2 = dq2 + jnp.dot(ds_b, k_var[hkv][half], preferred_element_type=F32)
                dk_acc[hkv][half] = lax.dot_general(ds_b, q2[hkv], TN, preferred_element_type=F32)
                dv_acc[hkv][half] = lax.dot_general(p_b[hkv, half], do2[hkv], TN, preferred_element_type=F32)
            for t in range(2):
                lo = C_Q + (2 * hkv + t) * 128
                dproj_ref[:, lo:lo + 128] = (_rows(dq2, t) * 0.125).astype(BF16)

        low = lax.broadcasted_iota(jnp.int32, (2 * BLK, 128), 1) < 64

        def gather_heads(acc):
            return jnp.where(low, acc[0][0] + pltpu.roll(acc[0][1], 64, axis=1),
                             pltpu.roll(acc[1][0], 64, axis=1) + acc[1][1])

        dk = gather_heads(dk_acc) * 0.125
        dv = gather_heads(dv_acc)
        dproj_ref[:, C_K:C_V] = (dk[BLK:, :] + dkv_ref[:, 0:128]).astype(BF16)
        dproj_ref[:, C_V:C_AG] = (dv[BLK:, :] + dkv_ref[:, 128:256]).astype(BF16)
        dkv_ref[:, 0:128] = dk[:BLK, :]
        dkv_ref[:, 128:256] = dv[:BLK, :]

    rev = lambda w: pl.BlockSpec((BLK, w), lambda s: (NB - 1 - s, 0))
    prev = lambda w: pl.BlockSpec((BLK, w), lambda s: (jnp.maximum(NB - 2 - s, 0), 0))
    halo = pl.BlockSpec((HALO, 512), lambda s: (jnp.maximum((NB - 1 - s) * (BLK // HALO) - 1, 0), 0))
    return pl.pallas_call(
        body, name="bwd_mix", grid=(NB,),
        in_specs=[rev(512), halo, rev(512), rev(512), rev(256), prev(256), rev(512), rev(D),
                  _layer(l, 4, 128, 128), _layer(l, 1, 512), pl.BlockSpec(memory_space=pltpu.SMEM),
                  pl.BlockSpec((None, N_HEADS, BLK, BLK), lambda s: (jnp.minimum(NB - 1 - s, 1), 0, 0, 0)),
                  _whole((BLK, BLK))] + [ANY] * len(deps),
        out_specs=[rev(D_IN), _layer(l, 4, 128, 128),
                   pl.BlockSpec((4, 128), lambda s: (0, 0)), pl.BlockSpec((8, 128), lambda s: (0, 0))],
        out_shape=[jax.ShapeDtypeStruct((S, D_IN), BF16), jax.ShapeDtypeStruct((DEPTH, 4, 128, 128), F32),
                   jax.ShapeDtypeStruct((4, 128), F32), jax.ShapeDtypeStruct((8, 128), F32)],
        input_output_aliases={} if dpw_dest is None else {12 + len(deps): 1},
        scratch_shapes=[pltpu.VMEM((POOL_ROWS, 512), F32)] * 4 + [pltpu.VMEM((BLK, 256), F32)],
        compiler_params=_params(),
    )(pu, pu, pg, q, kv, kv, ag, dcat, pool_w, pool_scale, sinks, bias, tri, *deps)


def _bwd_in_dw(l, dproj, x, g_pre, place_arr, to_bf16, deps=()):
    n_steps = S // TM

    def body(dp_ref, x_ref, g_ref, place_ref, *rest):
        f32_ref, own_ref, dwb_ref, bf16_ref, dw_ref = rest[len(deps):]
        step = pl.program_id(0)

        @pl.when(step == 0)
        def _():
            dw_ref[...] = jnp.zeros_like(dw_ref)
            bf16_ref[...] = f32_ref[...].astype(BF16)

        xt = x_ref[...]
        r = lax.rsqrt(jnp.mean(xt * xt, axis=-1, keepdims=True) + EPS)
        h = (xt * r * g_ref[...]).astype(BF16)
        dw_ref[...] += lax.dot_general(dp_ref[...], h, TN, preferred_element_type=F32)

        @pl.when(step == n_steps - 1)
        def _():
            dwb_ref[...] = dw_ref[...].astype(BF16)
            own_ref[...] = _own_piece(dw_ref, place_ref)

    row = lambda w: pl.BlockSpec((TM, w), lambda i: (i, 0))
    full = _whole
    return pl.pallas_call(
        body, name="bwd_in_dw", grid=(n_steps,),
        in_specs=[row(D_IN), row(D), _layer(l, 1, D), pl.BlockSpec(memory_space=pltpu.SMEM)] + [ANY] * len(deps)
        + [full(to_bf16.shape)],
        out_specs=[full((D_IN // 8, D)), full((D_IN, D)), full(to_bf16.shape)],
        out_shape=[jax.ShapeDtypeStruct((D_IN // 8, D), F32), jax.ShapeDtypeStruct((D_IN, D), BF16),
                   jax.ShapeDtypeStruct(to_bf16.shape, BF16)],
        scratch_shapes=[pltpu.VMEM((D_IN, D), F32)],
        compiler_params=_params(),
    )(dproj, x, g_pre, place_arr, *deps, to_bf16)


def _bwd_in_dx(l, dproj, w_in_t, x, g_pre, dres, deps=(), dw_place=None):
    n_steps = S // TM
    with_dw = dw_place is not None

    def body(dp_ref, w_ref, x_ref, g_ref, dres_ref, *rest):
        place_ref = rest[0] if with_dw else None
        rest = rest[with_dw + len(deps):]
        if with_dw:
            dx_ref, dg_ref, own_ref, dwb_ref, acc_ref, dw_ref = rest
        else:
            dx_ref, dg_ref, acc_ref = rest
        step = pl.program_id(0)

        @pl.when(step == 0)
        def _():
            acc_ref[...] = jnp.zeros_like(acc_ref)
            if with_dw:
                dw_ref[...] = jnp.zeros_like(dw_ref)

        g = g_ref[...]
        halves = [slice(k * (TM // 2), (k + 1) * (TM // 2)) for k in range(2)]
        dh = [jnp.dot(dp_ref[rows, :], w_ref[...], preferred_element_type=F32) for rows in halves]
        h = []
        for rows, dh_k in zip(halves, dh):
            xt = x_ref[rows, :]
            r = lax.rsqrt(jnp.mean(xt * xt, axis=-1, keepdims=True) + EPS)
            xn = xt * r
            acc_ref[...] += _rows8(dh_k * xn)
            a = dh_k * g
            dx_ref[rows, :] = dres_ref[rows, :] + (
                r * a - xt * (r * r * r) * jnp.mean(a * xt, axis=-1, keepdims=True))
            h.append((xn * g).astype(BF16))
        if with_dw:
            dw_ref[...] += lax.dot_general(dp_ref[...], jnp.concatenate(h, axis=0), TN, preferred_element_type=F32)

        @pl.when(step == n_steps - 1)
        def _():
            _store_lane_rows(dg_ref, acc_ref[...])
            if with_dw:
                dwb_ref[...] = dw_ref[...].astype(BF16)
                own_ref[...] = _own_piece(dw_ref, place_ref)

    row = lambda w: pl.BlockSpec((TM, w), lambda i: (i, 0))
    full = _whole
    dw_specs = [full((D_IN // 8, D)), full((D_IN, D))] if with_dw else []
    dw_shapes = [jax.ShapeDtypeStruct((D_IN // 8, D), F32), jax.ShapeDtypeStruct((D_IN, D), BF16)] if with_dw else []
    return pl.pallas_call(
        body, name="bwd_in" if with_dw else "bwd_in_dx", grid=(n_steps,),
        in_specs=[row(D_IN), full((D_IN, D)), row(D), _layer(l, 1, D), row(D)]
        + [pl.BlockSpec(memory_space=pltpu.SMEM)] * with_dw + [ANY] * len(deps),
        out_specs=[row(D), full((8, 128))] + dw_specs,
        out_shape=[jax.ShapeDtypeStruct((S, D), F32), jax.ShapeDtypeStruct((8, 128), F32)] + dw_shapes,
        scratch_shapes=[pltpu.VMEM((8, D), F32)] + [pltpu.VMEM((D_IN, D), F32)] * with_dw,
        compiler_params=_params(),
    )(dproj, w_in_t, x, g_pre, dres, *((dw_place,) if with_dw else ()), *deps)


HBM =pl.BlockSpec(memory_space=pltpu.HBM)
SEM = pl.BlockSpec(memory_space=pltpu.SEMAPHORE)
def _split_copy(collective_id=None):
    return pltpu.CompilerParams(has_side_effects=pltpu.SideEffectType.DATAFLOW_SIDE_EFFECTING,
                                collective_id=collective_id)


SPLIT_COPY = _split_copy()


def _in_hbm(a):
    return pltpu.with_memory_space_constraint(a, pltpu.HBM)

def _place():
    return lax.axis_index("x"), lax.axis_index("y"), lax.axis_index("c")


def _other_chips(x, y):
    return [(1 - x, y), (x, 1 - y), (1 - x, 1 - y)]


def _peer(x, y, c, m):
    return (x ^ (m >> 2), y ^ ((m >> 1) & 1), c ^ (m & 1))


SAME_CORE = (2, 4, 6)


def _place_cast(name, src, chip_arr, tile, layers, deps=()):
    _, n, cols = src.shape
    steps = n // tile
    k = len(layers)

    def body(chip_ref, *refs):
        for s_ref, o_ref in zip(refs[:k], refs[k + len(deps):]):
            o_ref[...] = s_ref[...].astype(BF16)

    def layer_spec(l):
        return pl.BlockSpec((None, tile, cols), lambda i, chip: (l, i, 0))

    return pl.pallas_call(
        body, name=name,
        grid_spec=pltpu.PrefetchScalarGridSpec(
            num_scalar_prefetch=1, grid=(steps,),
            in_specs=[layer_spec(l) for l in layers] + [ANY] * len(deps),
            out_specs=[pl.BlockSpec((tile, cols), lambda i, chip: (chip[0] * steps + i, 0))] * k),
        out_shape=[jax.ShapeDtypeStruct((N_SHARDS * n, cols), BF16)] * k,
        compiler_params=_params(),
    )(chip_arr, *[src] * k, *deps)


def _place_other_half(name, src, place_arr, layer, dest):
    _, n, cols = src.shape

    def body(place_ref, s_ref, dest_ref, o_ref):
        o_ref[...] = s_ref[...].astype(BF16)

    return pl.pallas_call(
        body, name=name,
        grid_spec=pltpu.PrefetchScalarGridSpec(
            num_scalar_prefetch=1, grid=(1,),
            in_specs=[pl.BlockSpec((None, n // 2, cols), lambda i, place: (layer, 1 - place[1], 0)), ANY],
            out_specs=pl.BlockSpec((n // 2, cols), lambda i, place: (place[0] + 1 - 2 * place[1], 0))),
        out_shape=jax.ShapeDtypeStruct((N_SHARDS * n, cols), BF16),
        input_output_aliases={2: 0},
        compiler_params=_params(),
    )(place_arr, src, dest)


def _chip_rows(ref, chip, half=None):
    n = ref.shape[0] // N_SHARDS
    if half is None:
        return ref.at[pl.ds(pl.multiple_of(chip * n, 16), n), :]
    return ref.at[pl.ds(pl.multiple_of(chip * n + half * (n // 2), 16), n // 2), :]


def _gather_start(name, bufs, halved, collective_id):
    n = len(bufs)

    def body(*refs):
        ins, send, recv, token = refs[:n], refs[n:2 * n], refs[2 * n:3 * n], refs[-1]
        x, y, c = _place()
        _handshake([(*chip, c) for chip in _other_chips(x, y)])
        for a, buf in enumerate(ins):
            own = _chip_rows(buf, 2 * x + y, c if a in halved else None)
            for j, chip in enumerate(_other_chips(x, y)):
                pltpu.make_async_remote_copy(src_ref=own, dst_ref=own, send_sem=send[a].at[j], recv_sem=recv[a].at[j],
                                             device_id=(*chip, c), device_id_type=MESH).start()
        token[...] = jnp.zeros_like(token)

    outs = pl.pallas_call(
        body, name=name, in_specs=[HBM] * n,
        out_specs=[SEM] * (2 * n) + [HBM] * n + [pl.BlockSpec(memory_space=pltpu.VMEM)],
        out_shape=[pltpu.SemaphoreType.DMA((3,))] * (2 * n) + [pltpu.HBM(b.shape, b.dtype) for b in bufs]
        + [jax.ShapeDtypeStruct((8, 128), F32)],
        input_output_aliases={a: 2 * n + a for a in range(n)},
        compiler_params=_split_copy(collective_id),
    )(*[_in_hbm(b) for b in bufs])
    return outs[:n], outs[n:2 * n], outs[2 * n:3 * n], outs[-1]


def _gather_start_cast(name, src, layer, collective_id):
    _, n, cols = src.shape
    half = n // 2

    def body(src_ref, send, recv, buf_ref, token, f32_ref, bf16_ref, local):
        x, y, c = _place()
        own = _chip_rows(buf_ref, 2 * x + y, c)

        def cast():
            load = pltpu.make_async_copy(src_ref.at[layer, pl.ds(pl.multiple_of(c * half, 16), half), :], f32_ref,
                                         local.at[0])
            load.start()
            load.wait()
            bf16_ref[...] = f32_ref[...].astype(BF16)
            store = pltpu.make_async_copy(bf16_ref, own, local.at[1])
            store.start()
            store.wait()

        _handshake([(*chip, c) for chip in _other_chips(x, y)], cast)
        for j, chip in enumerate(_other_chips(x, y)):
            pltpu.make_async_remote_copy(src_ref=own, dst_ref=own, send_sem=send.at[j], recv_sem=recv.at[j],
                                         device_id=(*chip, c), device_id_type=MESH).start()
        token[...] = jnp.zeros_like(token)

    outs = pl.pallas_call(
        body, name=name, in_specs=[HBM],
        out_specs=[SEM, SEM, HBM, pl.BlockSpec(memory_space=pltpu.VMEM)],
        out_shape=[pltpu.SemaphoreType.DMA((3,))] * 2 + [pltpu.HBM((N_SHARDS * n, cols), BF16),
                                                         jax.ShapeDtypeStruct((8, 128), F32)],
        scratch_shapes=[pltpu.VMEM((half, cols), F32), pltpu.VMEM((half, cols), BF16), pltpu.SemaphoreType.DMA((2,))],
        compiler_params=_split_copy(collective_id),
    )(_in_hbm(src))
    return outs[:1], outs[1:2], outs[2:3], outs[3]


def _gather_wait(name, buf, send_sem, recv_sem, after, halved=False):
    def body(buf_ref, send_ref, recv_ref, *rest):
        x, y, c = _place()
        half = c if halved else None
        own = _chip_rows(buf_ref, 2 * x + y, half)
        for j, chip in enumerate(_other_chips(x, y)):
            copy = pltpu.make_async_remote_copy(src_ref=own, dst_ref=_chip_rows(buf_ref, 2 * chip[0] + chip[1], half),
                                                send_sem=send_ref.at[j], recv_sem=recv_ref.at[j],
                                                device_id=(*chip, c), device_id_type=MESH)
            copy.wait_send()
            copy.wait_recv()

    return pl.pallas_call(
        body, name=name, in_specs=[HBM, SEM, SEM] + [ANY] * len(after), out_specs=HBM,
        out_shape=pltpu.HBM(buf.shape, buf.dtype), input_output_aliases={0: 0}, compiler_params=SPLIT_COPY,
    )(buf, send_sem, recv_sem, *after)


def _handshake(peers, meanwhile=None):
    barrier = pltpu.get_barrier_semaphore()
    for peer in peers:
        pl.semaphore_signal(barrier, inc=1, device_id=peer, device_id_type=MESH)
    if meanwhile is not None:
        meanwhile()
    pl.semaphore_wait(barrier, len(peers))


def _sibling_handshake(x, y, c):
    _handshake([(x, y, 1 - c)])


def _forward_halves(name, bufs, collective_id):
    n = len(bufs)

    def body(*refs):
        ins, outs, (send_sems, recv_sems) = refs[:n], refs[n:2 * n], refs[2 * n:]
        x, y, c = _place()
        _sibling_handshake(x, y, c)

        def copy(a, j, chip, half):
            rows = 2 * chip[0] + chip[1]
            return pltpu.make_async_remote_copy(
                src_ref=_chip_rows(ins[a], rows, half), dst_ref=_chip_rows(outs[a], rows, half),
                send_sem=send_sems.at[3 * a + j], recv_sem=recv_sems.at[3 * a + j], device_id=(x, y, 1 - c),
                device_id_type=MESH)

        copies = [(a, j, chip) for a in range(n) for j, chip in enumerate(_other_chips(x, y))]
        for a, j, chip in copies:
            copy(a, j, chip, c).start()
        for a, j, chip in copies:
            copy(a, j, chip, c).wait_send()
            copy(a, j, chip, 1 - c).wait_recv()

    return pl.pallas_call(
        body, name=name, in_specs=[ANY] * n, out_specs=[ANY] * n,
        out_shape=[jax.ShapeDtypeStruct(b.shape, b.dtype) for b in bufs],
        input_output_aliases={a: a for a in range(n)},
        scratch_shapes=[pltpu.SemaphoreType.DMA((3 * n,))] * 2,
        compiler_params=pltpu.CompilerParams(collective_id=collective_id),
    )(*bufs)


def _piece_rows(ref, k):
    p = ref.shape[0] // 8
    return ref.at[pl.ds(pl.multiple_of(k * p, 32 // jnp.dtype(ref.dtype).itemsize), p), :]


def _exchange_start(name, arrays, collective_id):
    n = len(arrays)
    zones = [lax.empty((7, a.shape[0] // 8, a.shape[1]), a.dtype) for a in arrays]

    def body(*refs):
        srcs, lands = refs[:n], refs[n:2 * n]
        send, recv = refs[2 * n:3 * n], refs[3 * n:4 * n]
        x, y, c = _place()
        _handshake([_peer(x, y, c, m) for m in range(1, 8)])
        for a, (src, land) in enumerate(zip(srcs, lands)):
            for m in range(1, 8):
                px, py, pc = _peer(x, y, c, m)
                pltpu.make_async_remote_copy(
                    src_ref=_piece_rows(src, 4 * px + 2 * py + pc), dst_ref=land.at[m - 1], send_sem=send[a].at[m - 1],
                    recv_sem=recv[a].at[m - 1], device_id=(px, py, pc), device_id_type=MESH).start()

    outs = pl.pallas_call(
        body, name=name, in_specs=[HBM] * (2 * n), out_specs=[SEM] * (2 * n) + [HBM] * (2 * n),
        out_shape=[pltpu.SemaphoreType.DMA((7,))] * (2 * n) + [pltpu.HBM(a.shape, a.dtype) for a in arrays + zones],
        input_output_aliases={a: 2 * n + a for a in range(2 * n)},
        compiler_params=_split_copy(collective_id),
    )(*[_in_hbm(a) for a in arrays + zones])
    return outs[:n], outs[n:2 * n], outs[2 * n:3 * n], outs[3 * n:4 * n]


def _exchange_wait(name, started, after, which=None):
    which = range(len(started[2])) if which is None else which
    send_sems, recv_sems, arrays, zones = [[group[k] for k in which] for group in started[:4]]
    n = len(arrays)

    def body(*refs):
        srcs, lands = refs[:n], refs[n:2 * n]
        send, recv = refs[2 * n:3 * n], refs[3 * n:4 * n]
        x, y, c = _place()
        for a, (src, land) in enumerate(zip(srcs, lands)):
            for m in range(1, 8):
                px, py, pc = _peer(x, y, c, m)
                copy = pltpu.make_async_remote_copy(
                    src_ref=_piece_rows(src, 4 * px + 2 * py + pc), dst_ref=land.at[m - 1], send_sem=send[a].at[m - 1],
                    recv_sem=recv[a].at[m - 1], device_id=(px, py, pc), device_id_type=MESH)
                copy.wait_send()
                copy.wait_recv()

    outs = pl.pallas_call(
        body, name=name, in_specs=[HBM] * (2 * n) + [SEM] * (2 * n) + [ANY], out_specs=[HBM] * (2 * n),
        out_shape=[pltpu.HBM(a.shape, a.dtype) for a in list(arrays) + list(zones)],
        input_output_aliases={a: a for a in range(2 * n)}, compiler_params=SPLIT_COPY,
    )(*arrays, *zones, *send_sems, *recv_sems, after)
    return outs[n:]


def _sum_pieces(name, weights, place_arr, dests=None):
    steps = 2
    flat = [item for items in weights for item in items]
    n = len(flat)

    def body(place_ref, *refs):
        outs = iter(refs[len(refs) - len(weights):])
        k = 0
        for items in weights:
            out_ref = next(outs)
            for layer, _, _ in items:
                total = refs[k][...]
                for m in range(7):
                    total = total + refs[n + k][m].astype(F32)
                if len(items) == DEPTH:
                    out_ref[layer] = total
                else:
                    out_ref[...] = total
                k += 1

    def out_spec(items):
        _, own, _ = items[0]
        t, cols = own.shape[0] // steps, own.shape[1]
        if len(items) == DEPTH:
            return pl.BlockSpec((DEPTH, t, cols), lambda i, place: (0, place[1] * steps + i, 0))
        layer = items[0][0]
        return pl.BlockSpec((None, t, cols), lambda i, place: (layer, place[1] * steps + i, 0))

    owns = [own for _, own, _ in flat]
    dests = [] if dests is None else list(dests)
    return pl.pallas_call(
        body, name=name,
        grid_spec=pltpu.PrefetchScalarGridSpec(
            num_scalar_prefetch=1, grid=(steps,),
            in_specs=[pl.BlockSpec((o.shape[0] // steps, o.shape[1]), lambda i, place: (i, 0)) for o in owns]
            + [pl.BlockSpec((7, o.shape[0] // steps, o.shape[1]), lambda i, place: (0, i, 0)) for o in owns]
            + [ANY] * len(dests),
            out_specs=[out_spec(items) for items in weights]),
        out_shape=[jax.ShapeDtypeStruct((DEPTH, 2 * items[0][1].shape[0], items[0][1].shape[1]), F32)
                   for items in weights],
        input_output_aliases={1 + 2 * n + k: k for k in range(len(dests))},
        compiler_params=_params(),
    )(place_arr, *owns, *[recv for _, _, recv in flat], *dests)


def _sum_small(name, partials, recvs, place_arr):
    n = len(partials)

    def body(place_ref, *refs):
        for o_ref, r_ref, out_ref in zip(refs[:n], refs[n:2 * n], refs[2 * n:]):
            total = o_ref[...]
            for m in range(7):
                total = total + r_ref[m].astype(F32)
            out_ref[...] = total

    piece = lambda a: pl.BlockSpec((a.shape[0] // 8, a.shape[1]), lambda i, place: (place[0], 0))
    return pl.pallas_call(
        body, name=name,
        grid_spec=pltpu.PrefetchScalarGridSpec(
            num_scalar_prefetch=1, grid=(1,),
            in_specs=[piece(a) for a in partials] + [pl.BlockSpec(r.shape, lambda i, place: (0, 0, 0)) for r in recvs],
            out_specs=[piece(a) for a in partials]),
        out_shape=[jax.ShapeDtypeStruct(a.shape, F32) for a in partials],
        compiler_params=_params(),
    )(place_arr, *partials, *recvs)


def _share(name, bufs, parts, gathered=(), collective_id=None, summed=()):
    n, n_g, n_s = len(bufs), len(gathered), len(summed)
    total = n + n_g
    made = [target for target, _, _ in summed]

    def body(*refs):
        ins, extra, outs = refs[:total], refs[total:total + 2 * n_s], refs[total + 2 * n_s:2 * total + 2 * n_s]
        send_sems, recv_sems, send_g, recv_g = refs[2 * total + 2 * n_s:2 * total + 2 * n_s + 4]
        scratch = refs[2 * total + 2 * n_s + 4:]
        x, y, c = _place()

        def half(ref, l, which):
            p = ref.shape[1] // 2
            return ref.at[l, pl.ds(pl.multiple_of(which * p, 8), p), :]

        def loads(j):
            local, acc, got = scratch[0], scratch[1:1 + n_s], scratch[1 + n_s:]
            own_rows = extra[2 * j] if made[j][0] == "half" else _piece_rows(extra[2 * j], 4 * x + 2 * y + c)
            return [pltpu.make_async_copy(own_rows, acc[j], local.at[2 * j]),
                    pltpu.make_async_copy(extra[2 * j + 1], got[j], local.at[2 * j + 1])]

        def sum_of(j):
            local, acc, got = scratch[0], scratch[1:1 + n_s], scratch[1 + n_s:]
            for load in loads(j):
                load.wait()
            rows = acc[j].shape[0]
            step = min(rows, 96)
            for r in range(0, rows, step):
                part = acc[j][r:r + step, :]
                for m in range(7):
                    part = part + got[j][m, r:r + step, :].astype(F32)
                acc[j][r:r + step, :] = part
            if made[j][0] == "half":
                dest = half(outs[made[j][1]], made[j][2], c)
            else:
                dest = _piece_rows(outs[n + made[j][1]], 4 * x + 2 * y + c)
            store = pltpu.make_async_copy(acc[j], dest, local.at[2 * j])
            store.start()
            store.wait()

        def early():
            for j in range(n_s):
                for load in loads(j):
                    load.start()
            for j in range(n_s):
                if made[j][0] == "piece":
                    sum_of(j)

        _handshake([_peer(x, y, c, m) for m in (SAME_CORE if gathered else ()) + (1,)], early if summed else None)

        def swap(k, which):
            a, l = parts[k]
            held = outs if ("half", a, l) in made else ins
            return pltpu.make_async_remote_copy(
                src_ref=half(held[a], l, which), dst_ref=half(outs[a], l, which), send_sem=send_sems.at[k],
                recv_sem=recv_sems.at[k], device_id=(x, y, 1 - c), device_id_type=MESH)

        def spread(a, m, sender, held, to):
            k = 4 * sender[0] + 2 * sender[1] + sender[2]
            return pltpu.make_async_remote_copy(
                src_ref=_piece_rows(held[n + a], k), dst_ref=_piece_rows(outs[n + a], k),
                send_sem=send_g.at[7 * a + m - 1], recv_sem=recv_g.at[7 * a + m - 1], device_id=to, device_id_type=MESH)

        me, sibling = (x, y, c), (x, y, 1 - c)

        def own(a, m):
            return spread(a, m, me, outs if ("piece", a) in made else ins, _peer(x, y, c, m))

        def handed_on(a, m):
            return spread(a, m + 1, _peer(x, y, c, m), outs, sibling)

        for a in range(n_g):
            for m in SAME_CORE + (1,):
                own(a, m).start()
        for j in range(n_s):
            if made[j][0] == "half":
                sum_of(j)
        for k in range(len(parts)):
            swap(k, c).start()
        for a in range(n_g):
            for m in SAME_CORE:
                spread(a, m, _peer(x, y, c, m), ins, _peer(x, y, c, m)).wait_recv()
                handed_on(a, m).start()
        for k in range(len(parts)):
            swap(k, c).wait_send()
            swap(k, 1 - c).wait_recv()
        for a in range(n_g):
            for m in SAME_CORE + (1,):
                own(a, m).wait_send()
            for m in SAME_CORE:
                handed_on(a, m).wait_send()
                spread(a, m + 1, _peer(x, y, c, m + 1), ins, sibling).wait_recv()
            spread(a, 1, sibling, ins, sibling).wait_recv()

    arrays = list(bufs) + list(gathered)
    sum_scratch = []
    if summed:
        sum_scratch = [pltpu.SemaphoreType.DMA((2 * n_s,))] + [pltpu.VMEM(recv.shape[1:], F32) for _, _, recv in summed]
        sum_scratch += [pltpu.VMEM(recv.shape, recv.dtype) for _, _, recv in summed]
    return pl.pallas_call(
        body, name=name, in_specs=[ANY] * (total + 2 * n_s), out_specs=[ANY] * total,
        out_shape=[jax.ShapeDtypeStruct(b.shape, F32) for b in arrays],
        input_output_aliases={a: a for a in range(total)},
        scratch_shapes=[pltpu.SemaphoreType.DMA((max(len(parts), 1),))] * 2
        + [pltpu.SemaphoreType.DMA((max(7 * n_g, 1),))] * 2 + sum_scratch,
        compiler_params=pltpu.CompilerParams(collective_id=collective_id, vmem_limit_bytes=VMEM_LIMIT),
    )(*arrays, *[array for _, own, recv in summed for array in (own, recv)])


def _adamw_math(w, g, m, v):
    nm = ADAM_B1 * m + (1.0 - ADAM_B1) * g
    nv = ADAM_B2 * v + (1.0 - ADAM_B2) * (g * g)
    m_hat = nm / (1.0 - ADAM_B1 ** ADAM_STEP)
    v_hat = nv / (1.0 - ADAM_B2 ** ADAM_STEP)
    return -ADAM_LR * (m_hat / (jnp.sqrt(v_hat) + ADAM_EPS) + ADAM_WD * w), nm, nv


def _adamw(name, w, g, m, v, rows_per_step, first=0, count=None, dests=None, deps=(), small=()):
    layers, rows, cols = w.shape
    count = layers if count is None else count
    dests = () if dests is None else tuple(dests)
    n_in = 4 + len(dests) + len(deps)
    small_shapes = _small_shapes(small[4].shape) if small else []

    def body(*refs):
        w_ref, g_ref, m_ref, v_ref = refs[:4]
        d_ref, nm_ref, nv_ref, g_out_ref = refs[n_in + len(small):n_in + len(small) + 4]
        d_ref[...], nm_ref[...], nv_ref[...] = _adamw_math(w_ref[...], g_ref[...], m_ref[...], v_ref[...])
        g_out_ref[...] = g_ref[...]
        if small:
            @pl.when((pl.program_id(0) == 0) & (pl.program_id(1) == 0))
            def _():
                _adamw_small(*refs[n_in:n_in + len(small)], *refs[n_in + len(small) + 4:])

    spec = pl.BlockSpec((1, rows_per_step, cols), lambda l, i: (first + l, i, 0))
    whole = lambda shape: pl.BlockSpec(shape, lambda l, i: (0,) * len(shape))
    shape = jax.ShapeDtypeStruct(w.shape, F32)
    return pl.pallas_call(
        body, name=name, grid=(count, rows // rows_per_step),
        in_specs=[spec] * 4 + [ANY] * (len(dests) + len(deps)) + [whole(a.shape) for a in small],
        out_specs=[spec] * 4 + [whole(s) for s in small_shapes],
        out_shape=[shape] * 4 + [jax.ShapeDtypeStruct(s, F32) for s in small_shapes],
        input_output_aliases={4 + k: k for k in range(len(dests))},
        scratch_shapes=[pltpu.VMEM((MISC_ROWS, 128), F32)] * 3 if small else [],
        compiler_params=_params(("arbitrary", "arbitrary")),
    )(w, g, m, v, *dests, *deps, *small)


def _pack_misc(pool_scale, sinks, norm_pre, norm_post):
    sink_rows = jnp.zeros((DEPTH, 8, 128), F32).at[:, 0, 0:N_HEADS].set(sinks).reshape(2 * 8, 128)
    return jnp.concatenate([pool_scale.reshape(8, 128), norm_pre.reshape(16, 128), norm_post.reshape(16, 128),
                            sink_rows, jnp.zeros((8, 128), F32)], axis=0)


def _adamw_small(w_ref, g_ref, m_ref, v_ref, pw_ref, pg_ref, pm_ref, pv_ref, *rest):
    outs, pool_outs, (d_ref, nm_ref, nv_ref) = rest[:17], rest[17:21], rest[21:]
    pool_outs[0][...] = pg_ref[...]
    pool_outs[1][...], pool_outs[2][...], pool_outs[3][...] = _adamw_math(
        pw_ref[...], pg_ref[...], pm_ref[...], pv_ref[...])
    d_ref[...], nm_ref[...], nv_ref[...] = _adamw_math(w_ref[...], g_ref[...], m_ref[...], v_ref[...])
    for k, src in enumerate([g_ref, d_ref, nm_ref, nv_ref]):
        scale, sinks, pre, post = outs[4 * k:4 * k + 4]
        for l in range(DEPTH):
            for j in range(4):
                scale[l:l + 1, j * 128:(j + 1) * 128] = src[MISC_SCALE + 4 * l + j:MISC_SCALE + 4 * l + j + 1, :]
            for j in range(8):
                pre[l:l + 1, j * 128:(j + 1) * 128] = src[MISC_PRE + 8 * l + j:MISC_PRE + 8 * l + j + 1, :]
                post[l:l + 1, j * 128:(j + 1) * 128] = src[MISC_POST + 8 * l + j:MISC_POST + 8 * l + j + 1, :]
            sinks[l:l + 1, :] = src[MISC_SINKS + 8 * l:MISC_SINKS + 8 * l + 1, 0:N_HEADS]
    outs[16][...] = g_ref[MISC_LOSS:MISC_LOSS + 1, 0:1]


def _small_shapes(pool_shape):
    return [(DEPTH, D_POOL), (DEPTH, N_HEADS), (DEPTH, D), (DEPTH, D)] * 4 + [(1, 1)] + [pool_shape] * 4


def kernel(x, w_in, pool_w, pool_scale, attn_sinks, w_out, norm_pre, norm_post, loss_target, m_w_in, m_pool_w, m_pool_scale, m_attn_sinks, m_w_out, m_norm_pre, m_norm_post, v_w_in, v_pool_w, v_pool_scale, v_attn_sinks, v_w_out, v_norm_pre, v_norm_post):
    cx, cy, cc = _place()
    chip_arr = jnp.reshape(2 * cx + cy, (1,)).astype(jnp.int32)
    place_arr = jnp.stack([4 * cx + 2 * cy + cc, cc]).astype(jnp.int32)
    t = lambda a: jnp.transpose(a, (0, 2, 1))
    w_in_t = t(w_in)
    xs, target = x[0], loss_target[0]
    pool_w_b = pool_w.astype(BF16)
    tables = _attention_tables()
    scale3 = pool_scale.reshape(DEPTH, 1, D_POOL)
    pre3 = norm_pre.reshape(DEPTH, 1, D)
    post3 = norm_post.reshape(DEPTH, 1, D)

    first = _gather_start_cast("gather_start_first", w_in_t, 0, ID_GATHER_FIRST)
    wi0 = _place_other_half("place_w_in0_rest", w_in_t, place_arr, 0, first[2][0])
    (wi1,) = _place_cast("place_w_in1", w_in_t, chip_arr, 288, [1], deps=(first[3],))
    wo = _place_cast("place_w_out", w_out, chip_arr, 256, [0, 1], deps=(first[3],))
    rest = _gather_start("gather_start_rest", [wi1, wo[0], wo[1]], halved=(0, 1), collective_id=ID_GATHER_REST)
    send, recv, bufs = [first[k] + rest[k] for k in range(3)]
    bufs = [wi0, *bufs[1:]]
    order = {(0, "in"): 0, (1, "in"): 1, (0, "out"): 2, (1, "out"): 3}

    saved = []
    packed = [_pack_misc(pool_scale, attn_sinks, norm_pre, norm_post),
              _pack_misc(m_pool_scale, m_attn_sinks, m_norm_pre, m_norm_post),
              _pack_misc(v_pool_scale, v_attn_sinks, v_norm_pre, v_norm_post)]
    after = (first[3], rest[3], pool_w_b, *tables, scale3, pre3, post3, *packed)
    below = None
    for l in range(DEPTH):
        k = order[l, "in"]
        halves = [_gather_wait(f"gather_wait_in{l}", bufs[k], send[k], recv[k], after, halved=True)]
        if below is not None:
            halves.append(below[1])
        w_in_l, *w_out_below = _forward_halves(f"forward_w{l}", halves, collective_id=ID_FORWARD[l])
        if below is None:
            pu, pg, q, kv, ag = _fwd_in(l, xs, pre3, w_in_l)
        else:
            saved[l - 1][9] = w_out_below[0]
            y, xs, pu, pg, q, kv, ag = _fwd_in(l, xs, pre3, w_in_l, (below[0], w_out_below[0], below[2]))
            saved[l - 1][7] = y
        cat = _fwd_mix(l, pu, pg, q, kv, ag, pool_w_b, scale3, attn_sinks, tables)
        k = order[l, "out"]
        w_out_l = _gather_wait(f"gather_wait_out{l}", bufs[k], send[k], recv[k], (cat,), halved=l + 1 < DEPTH)
        saved.append([xs, pu, pg, q, kv, ag, cat, None, w_in_l, w_out_l])
        below, after = (cat, w_out_l, post3), (w_out_l,)

    x_in, pu, pg, q, kv, ag, cat, y, w_in_l, w_out_l = saved[1]
    dcat, dw_out1, dw_out1_b, dg_post1, loss, xs = _bwd_out(1, cat, w_out_l, post3, place_arr, x=x_in, target=target)
    ex1_out = _exchange_start("exchange_start_out1", [dw_out1_b], ID_OUT1)
    dproj, dpw, dsc1, dsink1 = _bwd_mix(1, pu, pg, q, kv, ag, dcat, pool_w_b, scale3, attn_sinks, tables,
                                        deps=(ex1_out[2][0],))
    dx, dg_pre1, dw_in1, dw_in1_b = _bwd_in_dx(1, dproj, w_in_l, x_in, pre3, xs, dw_place=place_arr)
    ex1_in = _exchange_start("exchange_start_in1", [dw_in1_b], ID_IN1)

    x_in, pu, pg, q, kv, ag, cat, y, w_in_l, w_out_l = saved[0]
    dcat, dw_out0, dw_out0_b, dg_post0 = _bwd_out(0, cat, w_out_l, post3, place_arr, dxn=dx, y=y, deps=(ex1_in[2][0],))
    ex0_out = _exchange_start("exchange_start_out0", [dw_out0_b], ID_OUT0)
    dproj, dpw, dsc0, dsink0 = _bwd_mix(0, pu, pg, q, kv, ag, dcat, pool_w_b, scale3, attn_sinks, tables,
                                        deps=(ex0_out[2][0],), dpw_dest=dpw)
    flat = lambda a: a.reshape(DEPTH * 4 * 128, 128)
    dw_in0, dw_in0_b, dpw_b = _bwd_in_dw(0, dproj, x_in, pre3, place_arr, flat(dpw))
    ex0_in = _exchange_start("exchange_start_in0", [dpw_b, dw_in0_b], ID_IN0)

    grad_x, dg_pre0 = _bwd_in_dx(0, dproj, w_in_l, x_in, pre3, dx, deps=(ex0_in[2][1],))
    small = [jnp.concatenate([dsc0, dsc1, dg_pre0, dg_pre1, dg_post0, dg_post1, dsink0, dsink1, loss], axis=0)]
    ex_small = _exchange_start("exchange_start_small", small, ID_SMALL)
    (recv_out1,) = _exchange_wait("exchange_wait_out1", ex1_out, ex_small[2][0])
    (recv_in1,) = _exchange_wait("exchange_wait_in1", ex1_in, recv_out1)
    g_in, g_out = _sum_pieces("sum_pieces_1", [[(1, dw_in1, recv_in1)], [(1, dw_out1, recv_out1)]], place_arr)
    (recv_out0,) = _exchange_wait("exchange_wait_out0", ex0_out, g_out)
    (g_out,) = _sum_pieces("sum_pieces_out0", [[(0, dw_out0, recv_out0)]], place_arr, dests=[g_out])
    g_in, g_out = _share("share_a", [g_in, g_out], [(0, 1), (1, 0), (1, 1)], collective_id=ID_SHARE_A)
    m_in_t, v_in_t = t(m_w_in), t(v_w_in)
    d_out, nm_out, nv_out, grad_w_out = _adamw("adamw_w_out", w_out, g_out, m_w_out, v_w_out, 256)
    upd_in = _adamw("adamw_w_in1", w_in_t, g_in, m_in_t, v_in_t, 288, first=1, count=1, deps=(d_out,))
    (recv_pw,) = _exchange_wait("exchange_wait_pool", ex0_in, upd_in[0], which=[0])
    (g_pw,) = _sum_small("sum_pool", [flat(dpw)], [recv_pw], place_arr)

    (recv_in0,) = _exchange_wait("exchange_wait_in0", ex0_in, g_pw, which=[1])
    (recv_misc,) = _exchange_wait("exchange_wait_small", ex_small, recv_in0)
    g_in, g_pw, g_misc = _share(
        "share_b", [g_in], [(0, 0)], [g_pw, lax.empty(small[0].shape, F32)], collective_id=ID_SHARE_B,
        summed=[(("half", 0, 0), dw_in0, recv_in0), (("piece", 1), small[0], recv_misc)])
    d_in, nm_in, nv_in, grad_w_in_t, *small_out = _adamw(
        "adamw_w_in0", w_in_t, g_in, m_in_t, v_in_t, 288, first=0, count=1, dests=upd_in,
        small=(packed[0], g_misc, packed[1], packed[2], flat(pool_w), g_pw, flat(m_pool_w), flat(v_pool_w)))
    (g_sc, g_sk, g_pre, g_post, d_sc, d_sk, d_pre, d_post,
     m_sc, m_sk, m_pre, m_post, v_sc, v_sk, v_pre, v_post, loss_sum) = small_out[:17]
    g_pw, d_pw, m_pw, v_pw = [a.reshape(pool_w.shape) for a in small_out[17:]]
    return (loss_sum[0, 0], grad_x[None], t(grad_w_in_t), g_pw, g_sc, g_sk, grad_w_out, g_pre, g_post,
            t(d_in), d_pw, d_sc, d_sk, d_out, d_pre, d_post,
            t(nm_in), m_pw, m_sc, m_sk, nm_out, m_pre, m_post,
            t(nv_in), v_pw, v_sc, v_sk, nv_out, v_pre, v_post)
```

```python
import jax
import jax.numpy as jnp
from jax import lax
from jax.experimental import pallas as pl
from jax.experimental.pallas import tpu as pltpu

F32 = jnp.float32
BF16 = jnp.bfloat16

S = 2048
D = 1024
DEPTH = 2
D_POOL = 512
POOL_WINDOWS = (2, 4, 8, 16)
N_HEADS = 8
D_IN = 2304
N_SHARDS = 4
W_IN_SHARD = D_IN // N_SHARDS
W_OUT_SHARD = D // N_SHARDS
BLK = 128
NB = S // BLK
HALO = 16
PAD = 8
EPS = 1e-6
NEG_INF = -1e30
C_PU, C_PG, C_Q, C_K, C_V, C_AG = 0, 512, 1024, 1536, 1664, 1792

ADAM_LR = 0.001
ADAM_B1 = 0.9
ADAM_B2 = 0.999
ADAM_EPS = 1e-08
ADAM_WD = 0.01
ADAM_STEP = 10

TM = 512
VMEM_LIMIT = 56 * 1024 * 1024

NT = (((1,), (1,)), ((), ()))
TN = (((0,), (0,)), ((), ()))

MESH = pl.DeviceIdType.MESH
ANY = pl.BlockSpec(memory_space=pl.ANY)

ID_FORWARD = (0, 1)
(ID_SHARE_A, ID_SHARE_B, ID_GATHER_FIRST, ID_GATHER_REST, ID_OUT1, ID_IN1, ID_OUT0, ID_IN0, ID_SMALL) = range(2, 11)

MISC_SCALE, MISC_PRE, MISC_POST, MISC_SINKS, MISC_LOSS = 0, 8, 24, 40, 56
MISC_ROWS = 64


def _params(sem=("arbitrary",)):
    return pltpu.CompilerParams(dimension_semantics=sem, vmem_limit_bytes=VMEM_LIMIT)


def _sigmoid(v):
    return 1.0 / (1.0 + jnp.exp(-v))


def _rows8(v):
    r, c = v.shape
    return v.reshape(r // 8, 8, c).sum(axis=0)


def _layer(l, *shape):
    zeros = (0,) * len(shape)
    return pl.BlockSpec((None,) + shape, lambda i: (l,) + zeros)


def _whole(shape):
    zeros = (0,) * len(shape)
    return pl.BlockSpec(shape, lambda i: zeros, pipeline_mode=pl.Buffered(1))


def _fwd_in(l, x, g_pre, w_in_t, below=None):
    fused = below is not None

    def body(x_ref, g_ref, w_ref, *rest):
        if fused:
            cat_ref, wo_ref, gp_ref, y_ref, xn_ref = rest[:5]
            y = jnp.dot(cat_ref[...], wo_ref[...], preferred_element_type=F32)
            y_ref[...] = y
            xt = x_ref[...] + y * lax.rsqrt(jnp.mean(y * y, axis=-1, keepdims=True) + EPS) * gp_ref[...]
            xn_ref[...] = xt
        else:
            xt = x_ref[...]
        pu_ref, pg_ref, q_ref, kv_ref, ag_ref = rest[-5:]
        r = lax.rsqrt(jnp.mean(xt * xt, axis=-1, keepdims=True) + EPS)
        h = (xt * r * g_ref[...]).astype(BF16)

        def proj(lo, hi):
            return lax.dot_general(h, w_ref[lo:hi, :], NT, preferred_element_type=F32)

        pu_ref[...] = proj(C_PU, C_PG)
        pg_ref[...] = proj(C_PG, C_Q)
        q_ref[...] = proj(C_Q, C_K).astype(BF16)
        kv_ref[...] = proj(C_K, C_AG).astype(BF16)
        ag_ref[...] = proj(C_AG, D_IN)

    row = lambda w: pl.BlockSpec((TM, w), lambda i: (i, 0))
    act = jax.ShapeDtypeStruct((S, D), F32)
    return pl.pallas_call(
        body, name="fwd_out_in" if fused else "fwd_in", grid=(S // TM,),
        in_specs=[row(D), _layer(l, 1, D), _whole((D_IN, D))]
        + ([row(D), _whole((D, D)), _layer(l - 1, 1, D)] if fused else []),
        out_specs=[row(D)] * (2 * fused) + [row(512), row(512), row(512), row(256), row(512)],
        out_shape=[act] * (2 * fused)
        + [jax.ShapeDtypeStruct((S, 512), F32), jax.ShapeDtypeStruct((S, 512), F32),
           jax.ShapeDtypeStruct((S, 512), BF16), jax.ShapeDtypeStruct((S, 256), BF16),
           jax.ShapeDtypeStruct((S, 512), F32)],
        compiler_params=_params(),
    )(x, g_pre, w_in_t, *(below if fused else ()))


LOG2E = 1.4426950408889634
SCORE_SCALE = 0.125 * LOG2E


def _attention_tables():
    qi = jnp.arange(BLK)[:, None]
    kj = jnp.arange(BLK)[None, :]
    dist = ((qi - kj) % BLK).astype(F32)
    slopes = jnp.exp2(-jnp.arange(1, N_HEADS + 1, dtype=F32))
    bias = -(slopes * LOG2E)[:, None, None] * dist[None]
    first = jnp.where(kj > qi, NEG_INF, bias)
    return jnp.stack([first, bias]), (kj <= qi).astype(BF16)


def _own_block_mask():
    return lax.broadcasted_iota(jnp.int32, (BLK, BLK), 1) <= lax.broadcasted_iota(jnp.int32, (BLK, BLK), 0)


def _merge(full, own):
    return jnp.where(own, full[:, BLK:], full[:, :BLK])


def _spread(v, tri):
    own = v * tri
    return jnp.concatenate([v - own, own], axis=1)


def _head_variants(cur, prev):
    both = jnp.concatenate([prev, cur], axis=0).astype(F32)
    swapped = pltpu.roll(both, 64, axis=1)
    low = lax.broadcasted_iota(jnp.int32, both.shape, 1) < 64
    zero = jnp.zeros_like(both)
    return ((jnp.where(low, both, zero).astype(BF16), jnp.where(low, zero, swapped).astype(BF16)),
            (jnp.where(low, swapped, zero).astype(BF16), jnp.where(low, zero, both).astype(BF16)))


def _head_of(hkv, t, half):
    return hkv * 4 + 2 * t + half


def _rows(v, t):
    return v[t * BLK:(t + 1) * BLK]


def _stack_tiles(ref, hkv, offset=0):
    lo = offset + 2 * hkv * 128
    return jnp.concatenate([ref[:, lo:lo + 128], ref[:, lo + 128:lo + 256]], axis=0)


def _scores(q2, k_var, own):
    s = {}
    for hkv in range(2):
        for half in range(2):
            full = lax.dot_general(q2[hkv], k_var[hkv][half], NT, preferred_element_type=F32)
            for t in range(2):
                s[hkv, t, half] = _merge(_rows(full, t), own)
    return s


def _softmax(s, bias, sink):
    s = s * SCORE_SCALE + bias
    sink2 = sink * LOG2E
    m = jnp.maximum(jnp.max(s, axis=-1, keepdims=True), sink2)
    p = jnp.exp2(s - m)
    e_sink = jnp.exp2(sink2 - m)
    inv = 1.0 / (jnp.sum(p, axis=-1, keepdims=True) + e_sink)
    return p * inv, e_sink * inv


def _spread_pair(v, hkv, half, tri):
    return jnp.concatenate([_spread(v[hkv, t, half].astype(BF16), tri) for t in range(2)], axis=0)


POOL_ROWS = PAD + HALO + BLK


def _window_sums(src_ref, tmp_refs, trailing):
    lo, hi = (PAD, POOL_ROWS) if trailing else (0, HALO + BLK)
    cur = src_ref
    for level in range(len(POOL_WINDOWS)):
        lanes = slice(level * 128, 512)
        shift = -(1 << level) if trailing else (1 << level)
        dst = tmp_refs[level % 2]
        dst[lo:hi, lanes] = cur[lo:hi, lanes] + cur[lo + shift:hi + shift, lanes]
        cur = dst


def _pool_block(ext_ref, tmp_refs, i, g, w):
    lanes = slice(g * 128, (g + 1) * 128)
    rows = slice(PAD + HALO, POOL_ROWS)
    t = (i * BLK + lax.broadcasted_iota(jnp.int32, (BLK, 1), 0)).astype(F32)
    inv = 1.0 / jnp.minimum(t + 1.0, float(w))
    return tmp_refs[g % 2][rows, lanes] * inv - ext_ref[rows, lanes], inv


def _fwd_mix(l, pu, pg, q, kv, ag, pool_w, pool_scale, sinks, tables):
    bias, tri = tables

    def body(pu_ref, pup_ref, pg_ref, q_ref, kv_ref, kvp_ref, ag_ref, pw_ref, sc_ref, sink_ref, bias_ref, tri_ref,
             cat_ref, ext_ref, *tmp_refs):
        i = pl.program_id(0)

        @pl.when(i == 0)
        def _():
            for ref in (ext_ref, *tmp_refs):
                ref[0:PAD, :] = jnp.zeros((PAD, 512), F32)

        ext_ref[PAD:PAD + HALO, :] = jnp.where(i > 0, pup_ref[...], 0.0)
        ext_ref[PAD + HALO:POOL_ROWS, :] = pu_ref[...]
        _window_sums(ext_ref, tmp_refs, True)
        for g, w in enumerate(POOL_WINDOWS):
            lanes = slice(g * 128, (g + 1) * 128)
            pooled, _ = _pool_block(ext_ref, tmp_refs, i, g, w)
            mixed = jnp.dot(pooled.astype(BF16), pw_ref[g], preferred_element_type=F32)
            gate = pg_ref[:, lanes]
            cat_ref[:, lanes] = (mixed * sc_ref[:, lanes] * (gate * _sigmoid(gate))).astype(BF16)

        own = _own_block_mask()
        tri = tri_ref[...]
        k_var = _head_variants(kv_ref[:, 0:128], kvp_ref[:, 0:128])
        v_var = _head_variants(kv_ref[:, 128:256], kvp_ref[:, 128:256])
        s = _scores([_stack_tiles(q_ref, hkv) for hkv in range(2)], k_var, own)
        p = {}
        for (hkv, t, half), s_head in s.items():
            head = _head_of(hkv, t, half)
            p[hkv, t, half], _ = _softmax(s_head, bias_ref[head], sink_ref[l, head])
        for hkv in range(2):
            o2 = jnp.zeros((2 * BLK, 128), F32)
            for half in range(2):
                o2 = o2 + jnp.dot(_spread_pair(p, hkv, half, tri), v_var[hkv][half], preferred_element_type=F32)
            for t in range(2):
                lo = (2 * hkv + t) * 128
                gate = ag_ref[:, lo:lo + 128]
                cat_ref[:, D_POOL + lo:D_POOL + lo + 128] = (_rows(o2, t) * (gate * _sigmoid(gate))).astype(BF16)

    blk = lambda w: pl.BlockSpec((BLK, w), lambda i: (i, 0))
    prev = lambda w: pl.BlockSpec((BLK, w), lambda i: (jnp.maximum(i - 1, 0), 0))
    halo = pl.BlockSpec((HALO, 512), lambda i: (jnp.maximum(i * (BLK // HALO) - 1, 0), 0))
    return pl.pallas_call(
        body, name="fwd_mix", grid=(NB,),
        in_specs=[blk(512), halo, blk(512), blk(512), blk(256), prev(256), blk(512),
                  _layer(l, 4, 128, 128), _layer(l, 1, 512), pl.BlockSpec(memory_space=pltpu.SMEM),
                  pl.BlockSpec((None, N_HEADS, BLK, BLK), lambda i: (jnp.minimum(i, 1), 0, 0, 0)), _whole((BLK, BLK))],
        out_specs=blk(D),
        out_shape=jax.ShapeDtypeStruct((S, D), BF16),
        scratch_shapes=[pltpu.VMEM((POOL_ROWS, 512), F32)] * 3,
        compiler_params=_params(),
    )(pu, pu, pg, q, kv, kv, ag, pool_w, pool_scale, sinks, bias, tri)


def _store_lane_rows(ref, acc):
    total = jnp.sum(acc, axis=0, keepdims=True)
    for k in range(ref.shape[0]):
        ref[k:k + 1, :] = total[:, k * 128:(k + 1) * 128]


def _own_piece(dw_ref, place_ref):
    p = dw_ref.shape[0] // 8
    return dw_ref[pl.ds(pl.multiple_of(place_ref[0] * p, 8), p), :]


def _bwd_out(l, cat, w_out, g_post, place_arr, dxn=None, y=None, x=None, target=None, deps=()):
    last = target is not None
    n_steps = S // TM

    def body(a_ref, b_ref, g_ref, cat_ref, w_ref, place_ref, *rest):
        dcat_ref, own_ref, dwb_ref, dg_ref = rest[len(deps):len(deps) + 4]
        rest = rest[len(deps) + 4:]
        acc_ref, dw_ref = rest[-2:]
        step = pl.program_id(0)

        @pl.when(step == 0)
        def _():
            dw_ref[...] = jnp.zeros_like(dw_ref)
            acc_ref[...] = jnp.zeros_like(acc_ref)

        cat = cat_ref[...]
        g = g_ref[...]
        y = jnp.dot(cat, w_ref[...], preferred_element_type=F32) if last else b_ref[...]
        r = lax.rsqrt(jnp.mean(y * y, axis=-1, keepdims=True) + EPS)
        if last:
            loss_ref, dx_ref, loss_acc_ref = rest[:3]
            err = a_ref[...] + y * r * g - b_ref[...]

            @pl.when(step == 0)
            def _():
                loss_acc_ref[...] = jnp.zeros_like(loss_acc_ref)

            loss_acc_ref[...] += _rows8(err * err)
            dz = err * (1.0 / D)
            dx_ref[...] = dz
        else:
            dz = a_ref[...]
        a = dz * g
        dy = r * a - y * (r * r * r) * jnp.mean(a * y, axis=-1, keepdims=True)
        acc_ref[...] += _rows8(dz * (y * r))
        dyb = dy.astype(BF16)
        dcat_ref[...] = lax.dot_general(dyb, w_ref[...], NT, preferred_element_type=F32)
        dw_ref[...] += lax.dot_general(cat, dyb, TN, preferred_element_type=F32)

        @pl.when(step == n_steps - 1)
        def _():
            _store_lane_rows(dg_ref, acc_ref[...])
            dwb_ref[...] = dw_ref[...].astype(BF16)
            own_ref[...] = _own_piece(dw_ref, place_ref)
            if last:
                loss_ref[...] = jnp.full((8, 128), (0.5 / D) * jnp.sum(loss_acc_ref[...]), F32)

    row = lambda: pl.BlockSpec((TM, D), lambda i: (i, 0))
    full = _whole
    return pl.pallas_call(
        body, name="out_loss_bwd" if last else "bwd_out", grid=(n_steps,),
        in_specs=[row(), row(), _layer(l, 1, D), row(), full((D, D)), pl.BlockSpec(memory_space=pltpu.SMEM)]
        + [ANY] * len(deps),
        out_specs=[row(), full((D // 8, D)), full((D, D)), full((8, 128))] + ([full((8, 128)), row()] if last else []),
        out_shape=[jax.ShapeDtypeStruct((S, D), F32), jax.ShapeDtypeStruct((D // 8, D), F32),
                   jax.ShapeDtypeStruct((D, D), BF16), jax.ShapeDtypeStruct((8, 128), F32)]
        + ([jax.ShapeDtypeStruct((8, 128), F32), jax.ShapeDtypeStruct((S, D), F32)] if last else []),
        scratch_shapes=([pltpu.VMEM((8, D), F32)] if last else []) + [pltpu.VMEM((8, D), F32), pltpu.VMEM((D, D), F32)],
        compiler_params=_params(),
    )(*((x, target) if last else (dxn, y)), g_post, cat, w_out, place_arr, *deps)


def _bwd_mix(l, pu, pg, q, kv, ag, dcat, pool_w, pool_scale, sinks, tables, deps=(), dpw_dest=None):
    bias, tri = tables
    deps = tuple(deps) + (() if dpw_dest is None else (dpw_dest,))

    def body(pu_ref, pup_ref, pg_ref, q_ref, kv_ref, kvp_ref, ag_ref, dcat_ref, pw_ref, sc_ref, sink_ref, bias_ref,
             tri_ref, *rest):
        dproj_ref, dpw_ref, dsc_ref, dsink_ref, ext_ref, dext_ref, tmp_a, tmp_b, dkv_ref = rest[len(deps):]
        tmp_refs = (tmp_a, tmp_b)
        step = pl.program_id(0)
        i = NB - 1 - step

        @pl.when(step == 0)
        def _():
            dpw_ref[...] = jnp.zeros_like(dpw_ref)
            dsc_ref[...] = jnp.zeros_like(dsc_ref)
            dsink_ref[...] = jnp.zeros_like(dsink_ref)
            for ref in (ext_ref, tmp_a, tmp_b):
                ref[0:PAD, :] = jnp.zeros((PAD, 512), F32)
            dext_ref[BLK:POOL_ROWS, :] = jnp.zeros((HALO + PAD, 512), F32)
            dkv_ref[...] = jnp.zeros_like(dkv_ref)

        ext_ref[PAD:PAD + HALO, :] = jnp.where(i > 0, pup_ref[...], 0.0)
        ext_ref[PAD + HALO:POOL_ROWS, :] = pu_ref[...]
        _window_sums(ext_ref, tmp_refs, True)
        dpooled = []
        for g, w in enumerate(POOL_WINDOWS):
            lanes = slice(g * 128, (g + 1) * 128)
            pooled, inv = _pool_block(ext_ref, tmp_refs, i, g, w)
            pooled_b = pooled.astype(BF16)
            mixed = jnp.dot(pooled_b, pw_ref[g], preferred_element_type=F32)
            scale = sc_ref[:, lanes]
            gate = pg_ref[:, lanes]
            sg = _sigmoid(gate)
            dpo = dcat_ref[:, lanes]
            dproj_ref[:, C_PG + g * 128:C_PG + (g + 1) * 128] = (
                dpo * (mixed * scale) * (sg * (1.0 + gate * (1.0 - sg)))).astype(BF16)
            dms = dpo * (gate * sg)
            dsc_ref[g:g + 1, :] += jnp.sum(dms * mixed, axis=0, keepdims=True)
            dmixed = (dms * scale).astype(BF16)
            dpw_ref[g] += lax.dot_general(pooled_b, dmixed, TN, preferred_element_type=F32)
            dpooled.append(lax.dot_general(dmixed, pw_ref[g], NT, preferred_element_type=F32))
            dext_ref[0:BLK, lanes] = dpooled[g] * inv
        _window_sums(dext_ref, tmp_refs, False)
        for g in range(len(POOL_WINDOWS)):
            lanes = slice(g * 128, (g + 1) * 128)
            dproj_ref[:, C_PU + g * 128:C_PU + (g + 1) * 128] = (tmp_refs[g % 2][0:BLK, lanes] - dpooled[g]).astype(BF16)
        dext_ref[BLK:BLK + HALO, :] = dext_ref[0:HALO, :]

        own = _own_block_mask()
        tri = tri_ref[...]
        k_var = _head_variants(kv_ref[:, 0:128], kvp_ref[:, 0:128])
        v_var = _head_variants(kv_ref[:, 128:256], kvp_ref[:, 128:256])
        q2 = [_stack_tiles(q_ref, hkv) for hkv in range(2)]
        s = _scores(q2, k_var, own)
        p, p_sink = {}, {}
        for key, s_head in s.items():
            head = _head_of(*key)
            p[key], p_sink[key] = _softmax(s_head, bias_ref[head], sink_ref[l, head])

        do2, p_b, dp = [], {}, {}
        for hkv in range(2):
            gate = _stack_tiles(ag_ref, hkv)
            sg = _sigmoid(gate)
            dca = _stack_tiles(dcat_ref, hkv, D_POOL)
            do2.append((dca * (gate * sg)).astype(BF16))
            o2 = jnp.zeros((2 * BLK, 128), F32)
            for half in range(2):
                p_b[hkv, half] = _spread_pair(p, hkv, half, tri)
                o2 = o2 + jnp.dot(p_b[hkv, half], v_var[hkv][half], preferred_element_type=F32)
                full = lax.dot_general(do2[hkv], v_var[hkv][half], NT, preferred_element_type=F32)
                for t in range(2):
                    dp[hkv, t, half] = _merge(_rows(full, t), own)
            dag = dca * o2 * (sg * (1.0 + gate * (1.0 - sg)))
            for t in range(2):
                lo = C_AG + (2 * hkv + t) * 128
                dproj_ref[:, lo:lo + 128] = _rows(dag, t).astype(BF16)

        ds = {}
        for key in p:
            delta = jnp.sum(p[key] * dp[key], axis=-1, keepdims=True)
            ds[key] = p[key] * (dp[key] - delta)
            head = _head_of(*key)
            dsink_ref[0:1, :] += jnp.where(lax.broadcasted_iota(jnp.int32, (1, 128), 1) == head,
                                           -jnp.sum(p_sink[key] * delta, axis=0, keepdims=True), 0.0)

        dk_acc = [[None, None], [None, None]]
        dv_acc = [[None, None], [None, None]]
        for hkv in range(2):
            dq2 = jnp.zeros((2 * BLK, 128), F32)
            for half in range(2):
                ds_b = _spread_pair(ds, hkv, half, tri)
                dq2 = dq2 + jnp.dot(ds_b, k_var[hkv][half], preferred_element_type=F32)
                dk_acc[hkv][half] = lax.dot_general(ds_b, q2[hkv], TN, preferred_element_type=F32)
                dv_acc[hkv][half] = lax.dot_general(p_b[hkv, half], do2[hkv], TN, preferred_element_type=F32)
            for t in range(2):
                lo = C_Q + (2 * hkv + t) * 128
                dproj_ref[:, lo:lo + 128] = (_rows(dq2, t) * 0.125).astype(BF16)

        low = lax.broadcasted_iota(jnp.int32, (2 * BLK, 128), 1) < 64

        def gather_heads(acc):
            return jnp.where(low, acc[0][0] + pltpu.roll(acc[0][1], 64, axis=1),
                             pltpu.roll(acc[1][0], 64, axis=1) + acc[1][1])

        dk = gather_heads(dk_acc) * 0.125
        dv = gather_heads(dv_acc)
        dproj_ref[:, C_K:C_V] = (dk[BLK:, :] + dkv_ref[:, 0:128]).astype(BF16)
        dproj_ref[:, C_V:C_AG] = (dv[BLK:, :] + dkv_ref[:, 128:256]).astype(BF16)
        dkv_ref[:, 0:128] = dk[:BLK, :]
        dkv_ref[:, 128:256] = dv[:BLK, :]

    rev = lambda w: pl.BlockSpec((BLK, w), lambda s: (NB - 1 - s, 0))
    prev = lambda w: pl.BlockSpec((BLK, w), lambda s: (jnp.maximum(NB - 2 - s, 0), 0))
    halo = pl.BlockSpec((HALO, 512), lambda s: (jnp.maximum((NB - 1 - s) * (BLK // HALO) - 1, 0), 0))
    return pl.pallas_call(
        body, name="bwd_mix", grid=(NB,),
        in_specs=[rev(512), halo, rev(512), rev(512), rev(256), prev(256), rev(512), rev(D),
                  _layer(l, 4, 128, 128), _layer(l, 1, 512), pl.BlockSpec(memory_space=pltpu.SMEM),
                  pl.BlockSpec((None, N_HEADS, BLK, BLK), lambda s: (jnp.minimum(NB - 1 - s, 1), 0, 0, 0)),
                  _whole((BLK, BLK))] + [ANY] * len(deps),
        out_specs=[rev(D_IN), _layer(l, 4, 128, 128),
                   pl.BlockSpec((4, 128), lambda s: (0, 0)), pl.BlockSpec((8, 128), lambda s: (0, 0))],
        out_shape=[jax.ShapeDtypeStruct((S, D_IN), BF16), jax.ShapeDtypeStruct((DEPTH, 4, 128, 128), F32),
                   jax.ShapeDtypeStruct((4, 128), F32), jax.ShapeDtypeStruct((8, 128), F32)],
        input_output_aliases={} if dpw_dest is None else {12 + len(deps): 1},
        scratch_shapes=[pltpu.VMEM((POOL_ROWS, 512), F32)] * 4 + [pltpu.VMEM((BLK, 256), F32)],
        compiler_params=_params(),
    )(pu, pu, pg, q, kv, kv, ag, dcat, pool_w, pool_scale, sinks, bias, tri, *deps)


def _bwd_in_dw(l, dproj, x, g_pre, place_arr, to_bf16, deps=()):
    n_steps = S // TM

    def body(dp_ref, x_ref, g_ref, place_ref, *rest):
        f32_ref, own_ref, dwb_ref, bf16_ref, dw_ref = rest[len(deps):]
        step = pl.program_id(0)

        @pl.when(step == 0)
        def _():
            dw_ref[...] = jnp.zeros_like(dw_ref)
            bf16_ref[...] = f32_ref[...].astype(BF16)

        xt = x_ref[...]
        r = lax.rsqrt(jnp.mean(xt * xt, axis=-1, keepdims=True) + EPS)
        h = (xt * r * g_ref[...]).astype(BF16)
        dw_ref[...] += lax.dot_general(dp_ref[...], h, TN, preferred_element_type=F32)

        @pl.when(step == n_steps - 1)
        def _():
            dwb_ref[...] = dw_ref[...].astype(BF16)
            own_ref[...] = _own_piece(dw_ref, place_ref)

    row = lambda w: pl.BlockSpec((TM, w), lambda i: (i, 0))
    full = _whole
    return pl.pallas_call(
        body, name="bwd_in_dw", grid=(n_steps,),
        in_specs=[row(D_IN), row(D), _layer(l, 1, D), pl.BlockSpec(memory_space=pltpu.SMEM)] + [ANY] * len(deps)
        + [full(to_bf16.shape)],
        out_specs=[full((D_IN // 8, D)), full((D_IN, D)), full(to_bf16.shape)],
        out_shape=[jax.ShapeDtypeStruct((D_IN // 8, D), F32), jax.ShapeDtypeStruct((D_IN, D), BF16),
                   jax.ShapeDtypeStruct(to_bf16.shape, BF16)],
        scratch_shapes=[pltpu.VMEM((D_IN, D), F32)],
        compiler_params=_params(),
    )(dproj, x, g_pre, place_arr, *deps, to_bf16)


def _bwd_in_dx(l, dproj, w_in_t, x, g_pre, dres, deps=(), dw_place=None):
    n_steps = S // TM
    with_dw = dw_place is not None

    def body(dp_ref, w_ref, x_ref, g_ref, dres_ref, *rest):
        place_ref = rest[0] if with_dw else None
        rest = rest[with_dw + len(deps):]
        if with_dw:
            dx_ref, dg_ref, own_ref, dwb_ref, acc_ref, dw_ref = rest
        else:
            dx_ref, dg_ref, acc_ref = rest
        step = pl.program_id(0)

        @pl.when(step == 0)
        def _():
            acc_ref[...] = jnp.zeros_like(acc_ref)
            if with_dw:
                dw_ref[...] = jnp.zeros_like(dw_ref)

        g = g_ref[...]
        halves = [slice(k * (TM // 2), (k + 1) * (TM // 2)) for k in range(2)]
        dh = [jnp.dot(dp_ref[rows, :], w_ref[...], preferred_element_type=F32) for rows in halves]
        h = []
        for rows, dh_k in zip(halves, dh):
            xt = x_ref[rows, :]
            r = lax.rsqrt(jnp.mean(xt * xt, axis=-1, keepdims=True) + EPS)
            xn = xt * r
            acc_ref[...] += _rows8(dh_k * xn)
            a = dh_k * g
            dx_ref[rows, :] = dres_ref[rows, :] + (
                r * a - xt * (r * r * r) * jnp.mean(a * xt, axis=-1, keepdims=True))
            h.append((xn * g).astype(BF16))
        if with_dw:
            dw_ref[...] += lax.dot_general(dp_ref[...], jnp.concatenate(h, axis=0), TN, preferred_element_type=F32)

        @pl.when(step == n_steps - 1)
        def _():
            _store_lane_rows(dg_ref, acc_ref[...])
            if with_dw:
                dwb_ref[...] = dw_ref[...].astype(BF16)
                own_ref[...] = _own_piece(dw_ref, place_ref)

    row = lambda w: pl.BlockSpec((TM, w), lambda i: (i, 0))
    full = _whole
    dw_specs = [full((D_IN // 8, D)), full((D_IN, D))] if with_dw else []
    dw_shapes = [jax.ShapeDtypeStruct((D_IN // 8, D), F32), jax.ShapeDtypeStruct((D_IN, D), BF16)] if with_dw else []
    return pl.pallas_call(
        body, name="bwd_in" if with_dw else "bwd_in_dx", grid=(n_steps,),
        in_specs=[row(D_IN), full((D_IN, D)), row(D), _layer(l, 1, D), row(D)]
        + [pl.BlockSpec(memory_space=pltpu.SMEM)] * with_dw + [ANY] * len(deps),
        out_specs=[row(D), full((8, 128))] + dw_specs,
        out_shape=[jax.ShapeDtypeStruct((S, D), F32), jax.ShapeDtypeStruct((8, 128), F32)] + dw_shapes,
        scratch_shapes=[pltpu.VMEM((8, D), F32)] + [pltpu.VMEM((D_IN, D), F32)] * with_dw,
        compiler_params=_params(),
    )(dproj, w_in_t, x, g_pre, dres, *((dw_place,) if with_dw else ()), *deps)


HBM =pl.BlockSpec(memory_space=pltpu.HBM)
SEM = pl.BlockSpec(memory_space=pltpu.SEMAPHORE)
def _split_copy(collective_id=None):
    return pltpu.CompilerParams(has_side_effects=pltpu.SideEffectType.DATAFLOW_SIDE_EFFECTING,
                                collective_id=collective_id)


SPLIT_COPY = _split_copy()


def _in_hbm(a):
    return pltpu.with_memory_space_constraint(a, pltpu.HBM)

def _place():
    return lax.axis_index("x"), lax.axis_index("y"), lax.axis_index("c")


def _other_chips(x, y):
    return [(1 - x, y), (x, 1 - y), (1 - x, 1 - y)]


def _peer(x, y, c, m):
    return (x ^ (m >> 2), y ^ ((m >> 1) & 1), c ^ (m & 1))


SAME_CORE = (2, 4, 6)


def _place_cast(name, src, chip_arr, tile, layers, deps=()):
    _, n, cols = src.shape
    steps = n // tile
    k = len(layers)

    def body(chip_ref, *refs):
        for s_ref, o_ref in zip(refs[:k], refs[k + len(deps):]):
            o_ref[...] = s_ref[...].astype(BF16)

    def layer_spec(l):
        return pl.BlockSpec((None, tile, cols), lambda i, chip: (l, i, 0))

    return pl.pallas_call(
        body, name=name,
        grid_spec=pltpu.PrefetchScalarGridSpec(
            num_scalar_prefetch=1, grid=(steps,),
            in_specs=[layer_spec(l) for l in layers] + [ANY] * len(deps),
            out_specs=[pl.BlockSpec((tile, cols), lambda i, chip: (chip[0] * steps + i, 0))] * k),
        out_shape=[jax.ShapeDtypeStruct((N_SHARDS * n, cols), BF16)] * k,
        compiler_params=_params(),
    )(chip_arr, *[src] * k, *deps)


def _place_other_half(name, src, place_arr, layer, dest):
    _, n, cols = src.shape

    def body(place_ref, s_ref, dest_ref, o_ref):
        o_ref[...] = s_ref[...].astype(BF16)

    return pl.pallas_call(
        body, name=name,
        grid_spec=pltpu.PrefetchScalarGridSpec(
            num_scalar_prefetch=1, grid=(1,),
            in_specs=[pl.BlockSpec((None, n // 2, cols), lambda i, place: (layer, 1 - place[1], 0)), ANY],
            out_specs=pl.BlockSpec((n // 2, cols), lambda i, place: (place[0] + 1 - 2 * place[1], 0))),
        out_shape=jax.ShapeDtypeStruct((N_SHARDS * n, cols), BF16),
        input_output_aliases={2: 0},
        compiler_params=_params(),
    )(place_arr, src, dest)


def _chip_rows(ref, chip, half=None):
    n = ref.shape[0] // N_SHARDS
    if half is None:
        return ref.at[pl.ds(pl.multiple_of(chip * n, 16), n), :]
    return ref.at[pl.ds(pl.multiple_of(chip * n + half * (n // 2), 16), n // 2), :]


def _gather_start(name, bufs, halved, collective_id):
    n = len(bufs)

    def body(*refs):
        ins, send, recv, token = refs[:n], refs[n:2 * n], refs[2 * n:3 * n], refs[-1]
        x, y, c = _place()
        _handshake([(*chip, c) for chip in _other_chips(x, y)])
        for a, buf in enumerate(ins):
            own = _chip_rows(buf, 2 * x + y, c if a in halved else None)
            for j, chip in enumerate(_other_chips(x, y)):
                pltpu.make_async_remote_copy(src_ref=own, dst_ref=own, send_sem=send[a].at[j], recv_sem=recv[a].at[j],
                                             device_id=(*chip, c), device_id_type=MESH).start()
        token[...] = jnp.zeros_like(token)

    outs = pl.pallas_call(
        body, name=name, in_specs=[HBM] * n,
        out_specs=[SEM] * (2 * n) + [HBM] * n + [pl.BlockSpec(memory_space=pltpu.VMEM)],
        out_shape=[pltpu.SemaphoreType.DMA((3,))] * (2 * n) + [pltpu.HBM(b.shape, b.dtype) for b in bufs]
        + [jax.ShapeDtypeStruct((8, 128), F32)],
        input_output_aliases={a: 2 * n + a for a in range(n)},
        compiler_params=_split_copy(collective_id),
    )(*[_in_hbm(b) for b in bufs])
    return outs[:n], outs[n:2 * n], outs[2 * n:3 * n], outs[-1]


def _gather_start_cast(name, src, layer, collective_id):
    _, n, cols = src.shape
    half = n // 2

    def body(src_ref, send, recv, buf_ref, token, f32_ref, bf16_ref, local):
        x, y, c = _place()
        own = _chip_rows(buf_ref, 2 * x + y, c)

        def cast():
            load = pltpu.make_async_copy(src_ref.at[layer, pl.ds(pl.multiple_of(c * half, 16), half), :], f32_ref,
                                         local.at[0])
            load.start()
            load.wait()
            bf16_ref[...] = f32_ref[...].astype(BF16)
            store = pltpu.make_async_copy(bf16_ref, own, local.at[1])
            store.start()
            store.wait()

        _handshake([(*chip, c) for chip in _other_chips(x, y)], cast)
        for j, chip in enumerate(_other_chips(x, y)):
            pltpu.make_async_remote_copy(src_ref=own, dst_ref=own, send_sem=send.at[j], recv_sem=recv.at[j],
                                         device_id=(*chip, c), device_id_type=MESH).start()
        token[...] = jnp.zeros_like(token)

    outs = pl.pallas_call(
        body, name=name, in_specs=[HBM],
        out_specs=[SEM, SEM, HBM, pl.BlockSpec(memory_space=pltpu.VMEM)],
        out_shape=[pltpu.SemaphoreType.DMA((3,))] * 2 + [pltpu.HBM((N_SHARDS * n, cols), BF16),
                                                         jax.ShapeDtypeStruct((8, 128), F32)],
        scratch_shapes=[pltpu.VMEM((half, cols), F32), pltpu.VMEM((half, cols), BF16), pltpu.SemaphoreType.DMA((2,))],
        compiler_params=_split_copy(collective_id),
    )(_in_hbm(src))
    return outs[:1], outs[1:2], outs[2:3], outs[3]


def _gather_wait(name, buf, send_sem, recv_sem, after, halved=False):
    def body(buf_ref, send_ref, recv_ref, *rest):
        x, y, c = _place()
        half = c if halved else None
        own = _chip_rows(buf_ref, 2 * x + y, half)
        for j, chip in enumerate(_other_chips(x, y)):
            copy = pltpu.make_async_remote_copy(src_ref=own, dst_ref=_chip_rows(buf_ref, 2 * chip[0] + chip[1], half),
                                                send_sem=send_ref.at[j], recv_sem=recv_ref.at[j],
                                                device_id=(*chip, c), device_id_type=MESH)
            copy.wait_send()
            copy.wait_recv()

    return pl.pallas_call(
        body, name=name, in_specs=[HBM, SEM, SEM] + [ANY] * len(after), out_specs=HBM,
        out_shape=pltpu.HBM(buf.shape, buf.dtype), input_output_aliases={0: 0}, compiler_params=SPLIT_COPY,
    )(buf, send_sem, recv_sem, *after)


def _handshake(peers, meanwhile=None):
    barrier = pltpu.get_barrier_semaphore()
    for peer in peers:
        pl.semaphore_signal(barrier, inc=1, device_id=peer, device_id_type=MESH)
    if meanwhile is not None:
        meanwhile()
    pl.semaphore_wait(barrier, len(peers))


def _sibling_handshake(x, y, c):
    _handshake([(x, y, 1 - c)])


def _forward_halves(name, bufs, collective_id):
    n = len(bufs)

    def body(*refs):
        ins, outs, (send_sems, recv_sems) = refs[:n], refs[n:2 * n], refs[2 * n:]
        x, y, c = _place()
        _sibling_handshake(x, y, c)

        def copy(a, j, chip, half):
            rows = 2 * chip[0] + chip[1]
            return pltpu.make_async_remote_copy(
                src_ref=_chip_rows(ins[a], rows, half), dst_ref=_chip_rows(outs[a], rows, half),
                send_sem=send_sems.at[3 * a + j], recv_sem=recv_sems.at[3 * a + j], device_id=(x, y, 1 - c),
                device_id_type=MESH)

        copies = [(a, j, chip) for a in range(n) for j, chip in enumerate(_other_chips(x, y))]
        for a, j, chip in copies:
            copy(a, j, chip, c).start()
        for a, j, chip in copies:
            copy(a, j, chip, c).wait_send()
            copy(a, j, chip, 1 - c).wait_recv()

    return pl.pallas_call(
        body, name=name, in_specs=[ANY] * n, out_specs=[ANY] * n,
        out_shape=[jax.ShapeDtypeStruct(b.shape, b.dtype) for b in bufs],
        input_output_aliases={a: a for a in range(n)},
        scratch_shapes=[pltpu.SemaphoreType.DMA((3 * n,))] * 2,
        compiler_params=pltpu.CompilerParams(collective_id=collective_id),
    )(*bufs)


def _piece_rows(ref, k):
    p = ref.shape[0] // 8
    return ref.at[pl.ds(pl.multiple_of(k * p, 32 // jnp.dtype(ref.dtype).itemsize), p), :]


def _exchange_start(name, arrays, collective_id):
    n = len(arrays)
    zones = [lax.empty((7, a.shape[0] // 8, a.shape[1]), a.dtype) for a in arrays]

    def body(*refs):
        srcs, lands = refs[:n], refs[n:2 * n]
        send, recv = refs[2 * n:3 * n], refs[3 * n:4 * n]
        x, y, c = _place()
        _handshake([_peer(x, y, c, m) for m in range(1, 8)])
        for a, (src, land) in enumerate(zip(srcs, lands)):
            for m in range(1, 8):
                px, py, pc = _peer(x, y, c, m)
                pltpu.make_async_remote_copy(
                    src_ref=_piece_rows(src, 4 * px + 2 * py + pc), dst_ref=land.at[m - 1], send_sem=send[a].at[m - 1],
                    recv_sem=recv[a].at[m - 1], device_id=(px, py, pc), device_id_type=MESH).start()

    outs = pl.pallas_call(
        body, name=name, in_specs=[HBM] * (2 * n), out_specs=[SEM] * (2 * n) + [HBM] * (2 * n),
        out_shape=[pltpu.SemaphoreType.DMA((7,))] * (2 * n) + [pltpu.HBM(a.shape, a.dtype) for a in arrays + zones],
        input_output_aliases={a: 2 * n + a for a in range(2 * n)},
        compiler_params=_split_copy(collective_id),
    )(*[_in_hbm(a) for a in arrays + zones])
    return outs[:n], outs[n:2 * n], outs[2 * n:3 * n], outs[3 * n:4 * n]


def _exchange_wait(name, started, after, which=None):
    which = range(len(started[2])) if which is None else which
    send_sems, recv_sems, arrays, zones = [[group[k] for k in which] for group in started[:4]]
    n = len(arrays)

    def body(*refs):
        srcs, lands = refs[:n], refs[n:2 * n]
        send, recv = refs[2 * n:3 * n], refs[3 * n:4 * n]
        x, y, c = _place()
        for a, (src, land) in enumerate(zip(srcs, lands)):
            for m in range(1, 8):
                px, py, pc = _peer(x, y, c, m)
                copy = pltpu.make_async_remote_copy(
                    src_ref=_piece_rows(src, 4 * px + 2 * py + pc), dst_ref=land.at[m - 1], send_sem=send[a].at[m - 1],
                    recv_sem=recv[a].at[m - 1], device_id=(px, py, pc), device_id_type=MESH)
                copy.wait_send()
                copy.wait_recv()

    outs = pl.pallas_call(
        body, name=name, in_specs=[HBM] * (2 * n) + [SEM] * (2 * n) + [ANY], out_specs=[HBM] * (2 * n),
        out_shape=[pltpu.HBM(a.shape, a.dtype) for a in list(arrays) + list(zones)],
        input_output_aliases={a: a for a in range(2 * n)}, compiler_params=SPLIT_COPY,
    )(*arrays, *zones, *send_sems, *recv_sems, after)
    return outs[n:]


def _sum_pieces(name, weights, place_arr, dests=None):
    steps = 2
    flat = [item for items in weights for item in items]
    n = len(flat)

    def body(place_ref, *refs):
        outs = iter(refs[len(refs) - len(weights):])
        k = 0
        for items in weights:
            out_ref = next(outs)
            for layer, _, _ in items:
                total = refs[k][...]
                for m in range(7):
                    total = total + refs[n + k][m].astype(F32)
                if len(items) == DEPTH:
                    out_ref[layer] = total
                else:
                    out_ref[...] = total
                k += 1

    def out_spec(items):
        _, own, _ = items[0]
        t, cols = own.shape[0] // steps, own.shape[1]
        if len(items) == DEPTH:
            return pl.BlockSpec((DEPTH, t, cols), lambda i, place: (0, place[1] * steps + i, 0))
        layer = items[0][0]
        return pl.BlockSpec((None, t, cols), lambda i, place: (layer, place[1] * steps + i, 0))

    owns = [own for _, own, _ in flat]
    dests = [] if dests is None else list(dests)
    return pl.pallas_call(
        body, name=name,
        grid_spec=pltpu.PrefetchScalarGridSpec(
            num_scalar_prefetch=1, grid=(steps,),
            in_specs=[pl.BlockSpec((o.shape[0] // steps, o.shape[1]), lambda i, place: (i, 0)) for o in owns]
            + [pl.BlockSpec((7, o.shape[0] // steps, o.shape[1]), lambda i, place: (0, i, 0)) for o in owns]
            + [ANY] * len(dests),
            out_specs=[out_spec(items) for items in weights]),
        out_shape=[jax.ShapeDtypeStruct((DEPTH, 2 * items[0][1].shape[0], items[0][1].shape[1]), F32)
                   for items in weights],
        input_output_aliases={1 + 2 * n + k: k for k in range(len(dests))},
        compiler_params=_params(),
    )(place_arr, *owns, *[recv for _, _, recv in flat], *dests)


def _sum_small(name, partials, recvs, place_arr):
    n = len(partials)

    def body(place_ref, *refs):
        for o_ref, r_ref, out_ref in zip(refs[:n], refs[n:2 * n], refs[2 * n:]):
            total = o_ref[...]
            for m in range(7):
                total = total + r_ref[m].astype(F32)
            out_ref[...] = total

    piece = lambda a: pl.BlockSpec((a.shape[0] // 8, a.shape[1]), lambda i, place: (place[0], 0))
    return pl.pallas_call(
        body, name=name,
        grid_spec=pltpu.PrefetchScalarGridSpec(
            num_scalar_prefetch=1, grid=(1,),
            in_specs=[piece(a) for a in partials] + [pl.BlockSpec(r.shape, lambda i, place: (0, 0, 0)) for r in recvs],
            out_specs=[piece(a) for a in partials]),
        out_shape=[jax.ShapeDtypeStruct(a.shape, F32) for a in partials],
        compiler_params=_params(),
    )(place_arr, *partials, *recvs)


def _share(name, bufs, parts, gathered=(), collective_id=None, summed=()):
    n, n_g, n_s = len(bufs), len(gathered), len(summed)
    total = n + n_g
    made = [target for target, _, _ in summed]
    chunks = [3 if ("half", a, l) in made else 1 for a, l in parts]
    first_sem = [sum(chunks[:k]) for k in range(len(parts))]

    def body(*refs):
        ins, extra, outs = refs[:total], refs[total:total + 2 * n_s], refs[total + 2 * n_s:2 * total + 2 * n_s]
        send_sems, recv_sems, send_g, recv_g = refs[2 * total + 2 * n_s:2 * total + 2 * n_s + 4]
        scratch = refs[2 * total + 2 * n_s + 4:]
        x, y, c = _place()

        def half(ref, l, which):
            p = ref.shape[1] // 2
            return ref.at[l, pl.ds(pl.multiple_of(which * p, 8), p), :]

        def loads(j):
            local, acc, got = scratch[0], scratch[1:1 + n_s], scratch[1 + n_s:]
            own_rows = extra[2 * j] if made[j][0] == "half" else _piece_rows(extra[2 * j], 4 * x + 2 * y + c)
            return [pltpu.make_async_copy(own_rows, acc[j], local.at[2 * j]),
                    pltpu.make_async_copy(extra[2 * j + 1], got[j], local.at[2 * j + 1])]

        def sum_of(j):
            local, acc, got = scratch[0], scratch[1:1 + n_s], scratch[1 + n_s:]
            for load in loads(j):
                load.wait()
            rows = acc[j].shape[0]
            step = min(rows, 96)
            if made[j][0] == "half":
                dest = half(outs[made[j][1]], made[j][2], c)
                k = parts.index(made[j][1:])
                assert rows == chunks[k] * step
            else:
                dest = _piece_rows(outs[n + made[j][1]], 4 * x + 2 * y + c)
            for r in range(0, rows, step):
                part = acc[j][r:r + step, :]
                for m in range(7):
                    part = part + got[j][m, r:r + step, :].astype(F32)
                acc[j][r:r + step, :] = part
                store = pltpu.make_async_copy(acc[j].at[pl.ds(r, step), :], dest.at[pl.ds(r, step), :], local.at[2 * j])
                store.start()
                store.wait()
                if made[j][0] == "half":
                    swap(k, c, r // step).start()

        def early():
            for j in range(n_s):
                for load in loads(j):
                    load.start()
            for j in range(n_s):
                if made[j][0] == "piece":
                    sum_of(j)

        _handshake([_peer(x, y, c, m) for m in (SAME_CORE if gathered else ()) + (1,)], early if summed else None)

        def swap(k, which, chunk=0):
            a, l = parts[k]
            held = outs if ("half", a, l) in made else ins
            rows = held[a].shape[1] // 2 // chunks[k]

            def part(ref):
                return ref.at[l, pl.ds(pl.multiple_of((which * chunks[k] + chunk) * rows, 8), rows), :]

            return pltpu.make_async_remote_copy(
                src_ref=part(held[a]), dst_ref=part(outs[a]), send_sem=send_sems.at[first_sem[k] + chunk],
                recv_sem=recv_sems.at[first_sem[k] + chunk], device_id=(x, y, 1 - c), device_id_type=MESH)

        def spread(a, m, sender, held, to):
            k = 4 * sender[0] + 2 * sender[1] + sender[2]
            return pltpu.make_async_remote_copy(
                src_ref=_piece_rows(held[n + a], k), dst_ref=_piece_rows(outs[n + a], k),
                send_sem=send_g.at[7 * a + m - 1], recv_sem=recv_g.at[7 * a + m - 1], device_id=to, device_id_type=MESH)

        me, sibling = (x, y, c), (x, y, 1 - c)

        def own(a, m):
            return spread(a, m, me, outs if ("piece", a) in made else ins, _peer(x, y, c, m))

        def handed_on(a, m):
            return spread(a, m + 1, _peer(x, y, c, m), outs, sibling)

        for a in range(n_g):
            for m in SAME_CORE + (1,):
                own(a, m).start()
        for j in range(n_s):
            if made[j][0] == "half":
                sum_of(j)
        for k in range(len(parts)):
            if chunks[k] == 1:
                swap(k, c).start()
        for a in range(n_g):
            for m in SAME_CORE:
                spread(a, m, _peer(x, y, c, m), ins, _peer(x, y, c, m)).wait_recv()
                handed_on(a, m).start()
        for k in range(len(parts)):
            for chunk in range(chunks[k]):
                swap(k, c, chunk).wait_send()
                swap(k, 1 - c, chunk).wait_recv()
        for a in range(n_g):
            for m in SAME_CORE + (1,):
                own(a, m).wait_send()
            for m in SAME_CORE:
                handed_on(a, m).wait_send()
                spread(a, m + 1, _peer(x, y, c, m + 1), ins, sibling).wait_recv()
            spread(a, 1, sibling, ins, sibling).wait_recv()

    arrays = list(bufs) + list(gathered)
    sum_scratch = []
    if summed:
        sum_scratch = [pltpu.SemaphoreType.DMA((2 * n_s,))] + [pltpu.VMEM(recv.shape[1:], F32) for _, _, recv in summed]
        sum_scratch += [pltpu.VMEM(recv.shape, recv.dtype) for _, _, recv in summed]
    return pl.pallas_call(
        body, name=name, in_specs=[ANY] * (total + 2 * n_s), out_specs=[ANY] * total,
        out_shape=[jax.ShapeDtypeStruct(b.shape, F32) for b in arrays],
        input_output_aliases={a: a for a in range(total)},
        scratch_shapes=[pltpu.SemaphoreType.DMA((max(sum(chunks), 1),))] * 2
        + [pltpu.SemaphoreType.DMA((max(7 * n_g, 1),))] * 2 + sum_scratch,
        compiler_params=pltpu.CompilerParams(collective_id=collective_id, vmem_limit_bytes=VMEM_LIMIT),
    )(*arrays, *[array for _, own, recv in summed for array in (own, recv)])


def _adamw_math(w, g, m, v):
    nm = ADAM_B1 * m + (1.0 - ADAM_B1) * g
    nv = ADAM_B2 * v + (1.0 - ADAM_B2) * (g * g)
    m_hat = nm / (1.0 - ADAM_B1 ** ADAM_STEP)
    v_hat = nv / (1.0 - ADAM_B2 ** ADAM_STEP)
    return -ADAM_LR * (m_hat / (jnp.sqrt(v_hat) + ADAM_EPS) + ADAM_WD * w), nm, nv


def _adamw(name, w, g, m, v, rows_per_step, first=0, count=None, dests=None, deps=(), small=()):
    layers, rows, cols = w.shape
    count = layers if count is None else count
    dests = () if dests is None else tuple(dests)
    n_in = 4 + len(dests) + len(deps)
    small_shapes = _small_shapes(small[4].shape) if small else []

    def body(*refs):
        w_ref, g_ref, m_ref, v_ref = refs[:4]
        d_ref, nm_ref, nv_ref, g_out_ref = refs[n_in + len(small):n_in + len(small) + 4]
        d_ref[...], nm_ref[...], nv_ref[...] = _adamw_math(w_ref[...], g_ref[...], m_ref[...], v_ref[...])
        g_out_ref[...] = g_ref[...]
        if small:
            @pl.when((pl.program_id(0) == 0) & (pl.program_id(1) == 0))
            def _():
                _adamw_small(*refs[n_in:n_in + len(small)], *refs[n_in + len(small) + 4:])

    spec = pl.BlockSpec((1, rows_per_step, cols), lambda l, i: (first + l, i, 0))
    whole = lambda shape: pl.BlockSpec(shape, lambda l, i: (0,) * len(shape))
    shape = jax.ShapeDtypeStruct(w.shape, F32)
    return pl.pallas_call(
        body, name=name, grid=(count, rows // rows_per_step),
        in_specs=[spec] * 4 + [ANY] * (len(dests) + len(deps)) + [whole(a.shape) for a in small],
        out_specs=[spec] * 4 + [whole(s) for s in small_shapes],
        out_shape=[shape] * 4 + [jax.ShapeDtypeStruct(s, F32) for s in small_shapes],
        input_output_aliases={4 + k: k for k in range(len(dests))},
        scratch_shapes=[pltpu.VMEM((MISC_ROWS, 128), F32)] * 3 if small else [],
        compiler_params=_params(("arbitrary", "arbitrary")),
    )(w, g, m, v, *dests, *deps, *small)


def _pack_misc(pool_scale, sinks, norm_pre, norm_post):
    sink_rows = jnp.zeros((DEPTH, 8, 128), F32).at[:, 0, 0:N_HEADS].set(sinks).reshape(2 * 8, 128)
    return jnp.concatenate([pool_scale.reshape(8, 128), norm_pre.reshape(16, 128), norm_post.reshape(16, 128),
                            sink_rows, jnp.zeros((8, 128), F32)], axis=0)


def _adamw_small(w_ref, g_ref, m_ref, v_ref, pw_ref, pg_ref, pm_ref, pv_ref, *rest):
    outs, pool_outs, (d_ref, nm_ref, nv_ref) = rest[:17], rest[17:21], rest[21:]
    pool_outs[0][...] = pg_ref[...]
    pool_outs[1][...], pool_outs[2][...], pool_outs[3][...] = _adamw_math(
        pw_ref[...], pg_ref[...], pm_ref[...], pv_ref[...])
    d_ref[...], nm_ref[...], nv_ref[...] = _adamw_math(w_ref[...], g_ref[...], m_ref[...], v_ref[...])
    for k, src in enumerate([g_ref, d_ref, nm_ref, nv_ref]):
        scale, sinks, pre, post = outs[4 * k:4 * k + 4]
        for l in range(DEPTH):
            for j in range(4):
                scale[l:l + 1, j * 128:(j + 1) * 128] = src[MISC_SCALE + 4 * l + j:MISC_SCALE + 4 * l + j + 1, :]
            for j in range(8):
                pre[l:l + 1, j * 128:(j + 1) * 128] = src[MISC_PRE + 8 * l + j:MISC_PRE + 8 * l + j + 1, :]
                post[l:l + 1, j * 128:(j + 1) * 128] = src[MISC_POST + 8 * l + j:MISC_POST + 8 * l + j + 1, :]
            sinks[l:l + 1, :] = src[MISC_SINKS + 8 * l:MISC_SINKS + 8 * l + 1, 0:N_HEADS]
    outs[16][...] = g_ref[MISC_LOSS:MISC_LOSS + 1, 0:1]


def _small_shapes(pool_shape):
    return [(DEPTH, D_POOL), (DEPTH, N_HEADS), (DEPTH, D), (DEPTH, D)] * 4 + [(1, 1)] + [pool_shape] * 4


def kernel(x, w_in, pool_w, pool_scale, attn_sinks, w_out, norm_pre, norm_post, loss_target, m_w_in, m_pool_w, m_pool_scale, m_attn_sinks, m_w_out, m_norm_pre, m_norm_post, v_w_in, v_pool_w, v_pool_scale, v_attn_sinks, v_w_out, v_norm_pre, v_norm_post):
    cx, cy, cc = _place()
    chip_arr = jnp.reshape(2 * cx + cy, (1,)).astype(jnp.int32)
    place_arr = jnp.stack([4 * cx + 2 * cy + cc, cc]).astype(jnp.int32)
    t = lambda a: jnp.transpose(a, (0, 2, 1))
    w_in_t = t(w_in)
    xs, target = x[0], loss_target[0]
    pool_w_b = pool_w.astype(BF16)
    tables = _attention_tables()
    scale3 = pool_scale.reshape(DEPTH, 1, D_POOL)
    pre3 = norm_pre.reshape(DEPTH, 1, D)
    post3 = norm_post.reshape(DEPTH, 1, D)

    first = _gather_start_cast("gather_start_first", w_in_t, 0, ID_GATHER_FIRST)
    wi0 = _place_other_half("place_w_in0_rest", w_in_t, place_arr, 0, first[2][0])
    (wi1,) = _place_cast("place_w_in1", w_in_t, chip_arr, 288, [1], deps=(first[3],))
    wo = _place_cast("place_w_out", w_out, chip_arr, 256, [0, 1], deps=(first[3],))
    rest = _gather_start("gather_start_rest", [wi1, wo[0], wo[1]], halved=(0, 1), collective_id=ID_GATHER_REST)
    send, recv, bufs = [first[k] + rest[k] for k in range(3)]
    bufs = [wi0, *bufs[1:]]
    order = {(0, "in"): 0, (1, "in"): 1, (0, "out"): 2, (1, "out"): 3}

    saved = []
    packed = [_pack_misc(pool_scale, attn_sinks, norm_pre, norm_post),
              _pack_misc(m_pool_scale, m_attn_sinks, m_norm_pre, m_norm_post),
              _pack_misc(v_pool_scale, v_attn_sinks, v_norm_pre, v_norm_post)]
    after = (first[3], rest[3], pool_w_b, *tables, scale3, pre3, post3, *packed)
    below = None
    for l in range(DEPTH):
        k = order[l, "in"]
        halves = [_gather_wait(f"gather_wait_in{l}", bufs[k], send[k], recv[k], after, halved=True)]
        if below is not None:
            halves.append(below[1])
        w_in_l, *w_out_below = _forward_halves(f"forward_w{l}", halves, collective_id=ID_FORWARD[l])
        if below is None:
            pu, pg, q, kv, ag = _fwd_in(l, xs, pre3, w_in_l)
        else:
            saved[l - 1][9] = w_out_below[0]
            y, xs, pu, pg, q, kv, ag = _fwd_in(l, xs, pre3, w_in_l, (below[0], w_out_below[0], below[2]))
            saved[l - 1][7] = y
        cat = _fwd_mix(l, pu, pg, q, kv, ag, pool_w_b, scale3, attn_sinks, tables)
        k = order[l, "out"]
        w_out_l = _gather_wait(f"gather_wait_out{l}", bufs[k], send[k], recv[k], (cat,), halved=l + 1 < DEPTH)
        saved.append([xs, pu, pg, q, kv, ag, cat, None, w_in_l, w_out_l])
        below, after = (cat, w_out_l, post3), (w_out_l,)

    x_in, pu, pg, q, kv, ag, cat, y, w_in_l, w_out_l = saved[1]
    dcat, dw_out1, dw_out1_b, dg_post1, loss, xs = _bwd_out(1, cat, w_out_l, post3, place_arr, x=x_in, target=target)
    ex1_out = _exchange_start("exchange_start_out1", [dw_out1_b], ID_OUT1)
    dproj, dpw, dsc1, dsink1 = _bwd_mix(1, pu, pg, q, kv, ag, dcat, pool_w_b, scale3, attn_sinks, tables,
                                        deps=(ex1_out[2][0],))
    dx, dg_pre1, dw_in1, dw_in1_b = _bwd_in_dx(1, dproj, w_in_l, x_in, pre3, xs, dw_place=place_arr)
    ex1_in = _exchange_start("exchange_start_in1", [dw_in1_b], ID_IN1)

    x_in, pu, pg, q, kv, ag, cat, y, w_in_l, w_out_l = saved[0]
    dcat, dw_out0, dw_out0_b, dg_post0 = _bwd_out(0, cat, w_out_l, post3, place_arr, dxn=dx, y=y, deps=(ex1_in[2][0],))
    ex0_out = _exchange_start("exchange_start_out0", [dw_out0_b], ID_OUT0)
    dproj, dpw, dsc0, dsink0 = _bwd_mix(0, pu, pg, q, kv, ag, dcat, pool_w_b, scale3, attn_sinks, tables,
                                        deps=(ex0_out[2][0],), dpw_dest=dpw)
    flat = lambda a: a.reshape(DEPTH * 4 * 128, 128)
    dw_in0, dw_in0_b, dpw_b = _bwd_in_dw(0, dproj, x_in, pre3, place_arr, flat(dpw))
    ex0_in = _exchange_start("exchange_start_in0", [dpw_b, dw_in0_b], ID_IN0)

    grad_x, dg_pre0 = _bwd_in_dx(0, dproj, w_in_l, x_in, pre3, dx, deps=(ex0_in[2][1],))
    small = [jnp.concatenate([dsc0, dsc1, dg_pre0, dg_pre1, dg_post0, dg_post1, dsink0, dsink1, loss], axis=0)]
    ex_small = _exchange_start("exchange_start_small", small, ID_SMALL)
    (recv_out1,) = _exchange_wait("exchange_wait_out1", ex1_out, ex_small[2][0])
    (recv_in1,) = _exchange_wait("exchange_wait_in1", ex1_in, recv_out1)
    g_in, g_out = _sum_pieces("sum_pieces_1", [[(1, dw_in1, recv_in1)], [(1, dw_out1, recv_out1)]], place_arr)
    (recv_out0,) = _exchange_wait("exchange_wait_out0", ex0_out, g_out)
    (g_out,) = _sum_pieces("sum_pieces_out0", [[(0, dw_out0, recv_out0)]], place_arr, dests=[g_out])
    g_in, g_out = _share("share_a", [g_in, g_out], [(0, 1), (1, 0), (1, 1)], collective_id=ID_SHARE_A)
    m_in_t, v_in_t = t(m_w_in), t(v_w_in)
    d_out, nm_out, nv_out, grad_w_out = _adamw("adamw_w_out", w_out, g_out, m_w_out, v_w_out, 256)
    upd_in = _adamw("adamw_w_in1", w_in_t, g_in, m_in_t, v_in_t, 288, first=1, count=1, deps=(d_out,))
    (recv_pw,) = _exchange_wait("exchange_wait_pool", ex0_in, upd_in[0], which=[0])
    (g_pw,) = _sum_small("sum_pool", [flat(dpw)], [recv_pw], place_arr)

    (recv_in0,) = _exchange_wait("exchange_wait_in0", ex0_in, g_pw, which=[1])
    (recv_misc,) = _exchange_wait("exchange_wait_small", ex_small, recv_in0)
    g_in, g_pw, g_misc = _share(
        "share_b", [g_in], [(0, 0)], [g_pw, lax.empty(small[0].shape, F32)], collective_id=ID_SHARE_B,
        summed=[(("half", 0, 0), dw_in0, recv_in0), (("piece", 1), small[0], recv_misc)])
    d_in, nm_in, nv_in, grad_w_in_t, *small_out = _adamw(
        "adamw_w_in0", w_in_t, g_in, m_in_t, v_in_t, 288, first=0, count=1, dests=upd_in,
        small=(packed[0], g_misc, packed[1], packed[2], flat(pool_w), g_pw, flat(m_pool_w), flat(v_pool_w)))
    (g_sc, g_sk, g_pre, g_post, d_sc, d_sk, d_pre, d_post,
     m_sc, m_sk, m_pre, m_post, v_sc, v_sk, v_pre, v_post, loss_sum) = small_out[:17]
    g_pw, d_pw, m_pw, v_pw = [a.reshape(pool_w.shape) for a in small_out[17:]]
    return (loss_sum[0, 0], grad_x[None], t(grad_w_in_t), g_pw, g_sc, g_sk, grad_w_out, g_pre, g_post,
            t(d_in), d_pw, d_sc, d_sk, d_out, d_pre, d_post,
            t(nm_in), m_pw, m_sc, m_sk, nm_out, m_pre, m_post,
            t(nv_in), v_pw, v_sc, v_sk, nv_out, v_pre, v_post)
```

```python
import jax
import jax.numpy as jnp
from jax import lax
from jax.experimental import pallas as pl
from jax.experimental.pallas import tpu as pltpu

F32 = jnp.float32
BF16 = jnp.bfloat16

S = 2048
D = 1024
DEPTH = 2
D_POOL = 512
POOL_WINDOWS = (2, 4, 8, 16)
N_HEADS = 8
D_IN = 2304
N_SHARDS = 4
W_IN_SHARD = D_IN // N_SHARDS
W_OUT_SHARD = D // N_SHARDS
BLK = 128
NB = S // BLK
HALO = 16
PAD = 8
EPS = 1e-6
NEG_INF = -1e30
C_PU, C_PG, C_Q, C_K, C_V, C_AG = 0, 512, 1024, 1536, 1664, 1792

ADAM_LR = 0.001
ADAM_B1 = 0.9
ADAM_B2 = 0.999
ADAM_EPS = 1e-08
ADAM_WD = 0.01
ADAM_STEP = 10

TM = 512
VMEM_LIMIT = 56 * 1024 * 1024

NT = (((1,), (1,)), ((), ()))
TN = (((0,), (0,)), ((), ()))

MESH = pl.DeviceIdType.MESH
ANY = pl.BlockSpec(memory_space=pl.ANY)

ID_FORWARD = (0, 1)
(ID_SHARE_A, ID_SHARE_B, ID_GATHER_FIRST, ID_GATHER_REST, ID_OUT1, ID_IN1, ID_OUT0, ID_IN0, ID_SMALL) = range(2, 11)

MISC_SCALE, MISC_PRE, MISC_POST, MISC_SINKS, MISC_LOSS = 0, 8, 24, 40, 56
MISC_ROWS = 64


def _params(sem=("arbitrary",)):
    return pltpu.CompilerParams(dimension_semantics=sem, vmem_limit_bytes=VMEM_LIMIT)


def _sigmoid(v):
    return 1.0 / (1.0 + jnp.exp(-v))


def _rows8(v):
    r, c = v.shape
    return v.reshape(r // 8, 8, c).sum(axis=0)


def _layer(l, *shape):
    zeros = (0,) * len(shape)
    return pl.BlockSpec((None,) + shape, lambda i: (l,) + zeros)


def _whole(shape):
    zeros = (0,) * len(shape)
    return pl.BlockSpec(shape, lambda i: zeros, pipeline_mode=pl.Buffered(1))


def _fwd_in(l, x, g_pre, w_in_t, below=None):
    fused = below is not None

    def body(x_ref, g_ref, w_ref, *rest):
        if fused:
            cat_ref, wo_ref, gp_ref, y_ref, xn_ref = rest[:5]
            y = jnp.dot(cat_ref[...], wo_ref[...], preferred_element_type=F32)
            y_ref[...] = y
            xt = x_ref[...] + y * lax.rsqrt(jnp.mean(y * y, axis=-1, keepdims=True) + EPS) * gp_ref[...]
            xn_ref[...] = xt
        else:
            xt = x_ref[...]
        pu_ref, pg_ref, q_ref, kv_ref, ag_ref = rest[-5:]
        r = lax.rsqrt(jnp.mean(xt * xt, axis=-1, keepdims=True) + EPS)
        h = (xt * r * g_ref[...]).astype(BF16)

        def proj(lo, hi):
            return lax.dot_general(h, w_ref[lo:hi, :], NT, preferred_element_type=F32)

        pu_ref[...] = proj(C_PU, C_PG)
        pg_ref[...] = proj(C_PG, C_Q)
        q_ref[...] = proj(C_Q, C_K).astype(BF16)
        kv_ref[...] = proj(C_K, C_AG).astype(BF16)
        ag_ref[...] = proj(C_AG, D_IN)

    row = lambda w: pl.BlockSpec((TM, w), lambda i: (i, 0))
    act = jax.ShapeDtypeStruct((S, D), F32)
    return pl.pallas_call(
        body, name="fwd_out_in" if fused else "fwd_in", grid=(S // TM,),
        in_specs=[row(D), _layer(l, 1, D), _whole((D_IN, D))]
        + ([row(D), _whole((D, D)), _layer(l - 1, 1, D)] if fused else []),
        out_specs=[row(D)] * (2 * fused) + [row(512), row(512), row(512), row(256), row(512)],
        out_shape=[act] * (2 * fused)
        + [jax.ShapeDtypeStruct((S, 512), F32), jax.ShapeDtypeStruct((S, 512), F32),
           jax.ShapeDtypeStruct((S, 512), BF16), jax.ShapeDtypeStruct((S, 256), BF16),
           jax.ShapeDtypeStruct((S, 512), F32)],
        compiler_params=_params(),
    )(x, g_pre, w_in_t, *(below if fused else ()))


LOG2E = 1.4426950408889634
SCORE_SCALE = 0.125 * LOG2E


def _attention_tables():
    qi = jnp.arange(BLK)[:, None]
    kj = jnp.arange(BLK)[None, :]
    dist = ((qi - kj) % BLK).astype(F32)
    slopes = jnp.exp2(-jnp.arange(1, N_HEADS + 1, dtype=F32))
    bias = -(slopes * LOG2E)[:, None, None] * dist[None]
    first = jnp.where(kj > qi, NEG_INF, bias)
    return jnp.stack([first, bias]), (kj <= qi).astype(BF16)


def _own_block_mask():
    return lax.broadcasted_iota(jnp.int32, (BLK, BLK), 1) <= lax.broadcasted_iota(jnp.int32, (BLK, BLK), 0)


def _merge(full, own):
    return jnp.where(own, full[:, BLK:], full[:, :BLK])


def _spread(v, tri):
    own = v * tri
    return jnp.concatenate([v - own, own], axis=1)


def _head_variants(cur, prev):
    both = jnp.concatenate([prev, cur], axis=0).astype(F32)
    swapped = pltpu.roll(both, 64, axis=1)
    low = lax.broadcasted_iota(jnp.int32, both.shape, 1) < 64
    zero = jnp.zeros_like(both)
    return ((jnp.where(low, both, zero).astype(BF16), jnp.where(low, zero, swapped).astype(BF16)),
            (jnp.where(low, swapped, zero).astype(BF16), jnp.where(low, zero, both).astype(BF16)))


def _head_of(hkv, t, half):
    return hkv * 4 + 2 * t + half


def _rows(v, t):
    return v[t * BLK:(t + 1) * BLK]


def _stack_tiles(ref, hkv, offset=0):
    lo = offset + 2 * hkv * 128
    return jnp.concatenate([ref[:, lo:lo + 128], ref[:, lo + 128:lo + 256]], axis=0)


def _scores(q2, k_var, own):
    s = {}
    for hkv in range(2):
        for half in range(2):
            full = lax.dot_general(q2[hkv], k_var[hkv][half], NT, preferred_element_type=F32)
            for t in range(2):
                s[hkv, t, half] = _merge(_rows(full, t), own)
    return s


def _softmax(s, bias, sink):
    s = s * SCORE_SCALE + bias
    sink2 = sink * LOG2E
    m = jnp.maximum(jnp.max(s, axis=-1, keepdims=True), sink2)
    p = jnp.exp2(s - m)
    e_sink = jnp.exp2(sink2 - m)
    inv = 1.0 / (jnp.sum(p, axis=-1, keepdims=True) + e_sink)
    return p * inv, e_sink * inv


def _spread_pair(v, hkv, half, tri):
    return jnp.concatenate([_spread(v[hkv, t, half].astype(BF16), tri) for t in range(2)], axis=0)


POOL_ROWS = PAD + HALO + BLK


def _window_sums(src_ref, tmp_refs, trailing):
    lo, hi = (PAD, POOL_ROWS) if trailing else (0, HALO + BLK)
    cur = src_ref
    for level in range(len(POOL_WINDOWS)):
        lanes = slice(level * 128, 512)
        shift = -(1 << level) if trailing else (1 << level)
        dst = tmp_refs[level % 2]
        dst[lo:hi, lanes] = cur[lo:hi, lanes] + cur[lo + shift:hi + shift, lanes]
        cur = dst


def _pool_block(ext_ref, tmp_refs, i, g, w):
    lanes = slice(g * 128, (g + 1) * 128)
    rows = slice(PAD + HALO, POOL_ROWS)
    t = (i * BLK + lax.broadcasted_iota(jnp.int32, (BLK, 1), 0)).astype(F32)
    inv = 1.0 / jnp.minimum(t + 1.0, float(w))
    return tmp_refs[g % 2][rows, lanes] * inv - ext_ref[rows, lanes], inv


def _fwd_mix(l, pu, pg, q, kv, ag, pool_w, pool_scale, sinks, tables):
    bias, tri = tables

    def body(pu_ref, pup_ref, pg_ref, q_ref, kv_ref, kvp_ref, ag_ref, pw_ref, sc_ref, sink_ref, bias_ref, tri_ref,
             cat_ref, ext_ref, *tmp_refs):
        i = pl.program_id(0)

        @pl.when(i == 0)
        def _():
            for ref in (ext_ref, *tmp_refs):
                ref[0:PAD, :] = jnp.zeros((PAD, 512), F32)

        ext_ref[PAD:PAD + HALO, :] = jnp.where(i > 0, pup_ref[...], 0.0)
        ext_ref[PAD + HALO:POOL_ROWS, :] = pu_ref[...]
        _window_sums(ext_ref, tmp_refs, True)
        for g, w in enumerate(POOL_WINDOWS):
            lanes = slice(g * 128, (g + 1) * 128)
            pooled, _ = _pool_block(ext_ref, tmp_refs, i, g, w)
            mixed = jnp.dot(pooled.astype(BF16), pw_ref[g], preferred_element_type=F32)
            gate = pg_ref[:, lanes]
            cat_ref[:, lanes] = (mixed * sc_ref[:, lanes] * (gate * _sigmoid(gate))).astype(BF16)

        own = _own_block_mask()
        tri = tri_ref[...]
        k_var = _head_variants(kv_ref[:, 0:128], kvp_ref[:, 0:128])
        v_var = _head_variants(kv_ref[:, 128:256], kvp_ref[:, 128:256])
        s = _scores([_stack_tiles(q_ref, hkv) for hkv in range(2)], k_var, own)
        p = {}
        for (hkv, t, half), s_head in s.items():
            head = _head_of(hkv, t, half)
            p[hkv, t, half], _ = _softmax(s_head, bias_ref[head], sink_ref[l, head])
        for hkv in range(2):
            o2 = jnp.zeros((2 * BLK, 128), F32)
            for half in range(2):
                o2 = o2 + jnp.dot(_spread_pair(p, hkv, half, tri), v_var[hkv][half], preferred_element_type=F32)
            for t in range(2):
                lo = (2 * hkv + t) * 128
                gate = ag_ref[:, lo:lo + 128]
                cat_ref[:, D_POOL + lo:D_POOL + lo + 128] = (_rows(o2, t) * (gate * _sigmoid(gate))).astype(BF16)

    blk = lambda w: pl.BlockSpec((BLK, w), lambda i: (i, 0))
    prev = lambda w: pl.BlockSpec((BLK, w), lambda i: (jnp.maximum(i - 1, 0), 0))
    halo = pl.BlockSpec((HALO, 512), lambda i: (jnp.maximum(i * (BLK // HALO) - 1, 0), 0))
    return pl.pallas_call(
        body, name="fwd_mix", grid=(NB,),
        in_specs=[blk(512), halo, blk(512), blk(512), blk(256), prev(256), blk(512),
                  _layer(l, 4, 128, 128), _layer(l, 1, 512), pl.BlockSpec(memory_space=pltpu.SMEM),
                  pl.BlockSpec((None, N_HEADS, BLK, BLK), lambda i: (jnp.minimum(i, 1), 0, 0, 0)), _whole((BLK, BLK))],
        out_specs=blk(D),
        out_shape=jax.ShapeDtypeStruct((S, D), BF16),
        scratch_shapes=[pltpu.VMEM((POOL_ROWS, 512), F32)] * 3,
        compiler_params=_params(),
    )(pu, pu, pg, q, kv, kv, ag, pool_w, pool_scale, sinks, bias, tri)


def _accumulate(ref, step, product):
    @pl.when(step == 0)
    def _():
        ref[...] = product()

    @pl.when(step > 0)
    def _():
        ref[...] += product()


def _store_lane_rows(ref, acc):
    total = jnp.sum(acc, axis=0, keepdims=True)
    for k in range(ref.shape[0]):
        ref[k:k + 1, :] = total[:, k * 128:(k + 1) * 128]


def _own_piece(dw_ref, place_ref):
    p = dw_ref.shape[0] // 8
    return dw_ref[pl.ds(pl.multiple_of(place_ref[0] * p, 8), p), :]


def _bwd_out(l, cat, w_out, g_post, place_arr, dxn=None, y=None, x=None, target=None, deps=()):
    last = target is not None
    n_steps = S // TM

    def body(a_ref, b_ref, g_ref, cat_ref, w_ref, place_ref, *rest):
        dcat_ref, own_ref, dwb_ref, dg_ref = rest[len(deps):len(deps) + 4]
        rest = rest[len(deps) + 4:]
        acc_ref, dw_ref = rest[-2:]
        step = pl.program_id(0)

        @pl.when(step == 0)
        def _():
            acc_ref[...] = jnp.zeros_like(acc_ref)

        cat = cat_ref[...]
        g = g_ref[...]
        y = jnp.dot(cat, w_ref[...], preferred_element_type=F32) if last else b_ref[...]
        r = lax.rsqrt(jnp.mean(y * y, axis=-1, keepdims=True) + EPS)
        if last:
            loss_ref, dx_ref, loss_acc_ref = rest[:3]
            err = a_ref[...] + y * r * g - b_ref[...]

            @pl.when(step == 0)
            def _():
                loss_acc_ref[...] = jnp.zeros_like(loss_acc_ref)

            loss_acc_ref[...] += _rows8(err * err)
            dz = err * (1.0 / D)
            dx_ref[...] = dz
        else:
            dz = a_ref[...]
        a = dz * g
        dy = r * a - y * (r * r * r) * jnp.mean(a * y, axis=-1, keepdims=True)
        acc_ref[...] += _rows8(dz * (y * r))
        dyb = dy.astype(BF16)
        dcat_ref[...] = lax.dot_general(dyb, w_ref[...], NT, preferred_element_type=F32)
        _accumulate(dw_ref, step, lambda: lax.dot_general(cat, dyb, TN, preferred_element_type=F32))

        @pl.when(step == n_steps - 1)
        def _():
            _store_lane_rows(dg_ref, acc_ref[...])
            dwb_ref[...] = dw_ref[...].astype(BF16)
            own_ref[...] = _own_piece(dw_ref, place_ref)
            if last:
                loss_ref[...] = jnp.full((8, 128), (0.5 / D) * jnp.sum(loss_acc_ref[...]), F32)

    row = lambda: pl.BlockSpec((TM, D), lambda i: (i, 0))
    full = _whole
    return pl.pallas_call(
        body, name="out_loss_bwd" if last else "bwd_out", grid=(n_steps,),
        in_specs=[row(), row(), _layer(l, 1, D), row(), full((D, D)), pl.BlockSpec(memory_space=pltpu.SMEM)]
        + [ANY] * len(deps),
        out_specs=[row(), full((D // 8, D)), full((D, D)), full((8, 128))] + ([full((8, 128)), row()] if last else []),
        out_shape=[jax.ShapeDtypeStruct((S, D), F32), jax.ShapeDtypeStruct((D // 8, D), F32),
                   jax.ShapeDtypeStruct((D, D), BF16), jax.ShapeDtypeStruct((8, 128), F32)]
        + ([jax.ShapeDtypeStruct((8, 128), F32), jax.ShapeDtypeStruct((S, D), F32)] if last else []),
        scratch_shapes=([pltpu.VMEM((8, D), F32)] if last else []) + [pltpu.VMEM((8, D), F32), pltpu.VMEM((D, D), F32)],
        compiler_params=_params(),
    )(*((x, target) if last else (dxn, y)), g_post, cat, w_out, place_arr, *deps)


def _bwd_mix(l, pu, pg, q, kv, ag, dcat, pool_w, pool_scale, sinks, tables, deps=(), dpw_dest=None):
    bias, tri = tables
    deps = tuple(deps) + (() if dpw_dest is None else (dpw_dest,))

    def body(pu_ref, pup_ref, pg_ref, q_ref, kv_ref, kvp_ref, ag_ref, dcat_ref, pw_ref, sc_ref, sink_ref, bias_ref,
             tri_ref, *rest):
        dproj_ref, dpw_ref, dsc_ref, dsink_ref, ext_ref, dext_ref, tmp_a, tmp_b, dkv_ref = rest[len(deps):]
        tmp_refs = (tmp_a, tmp_b)
        step = pl.program_id(0)
        i = NB - 1 - step

        @pl.when(step == 0)
        def _():
            dpw_ref[...] = jnp.zeros_like(dpw_ref)
            dsc_ref[...] = jnp.zeros_like(dsc_ref)
            dsink_ref[...] = jnp.zeros_like(dsink_ref)
            for ref in (ext_ref, tmp_a, tmp_b):
                ref[0:PAD, :] = jnp.zeros((PAD, 512), F32)
            dext_ref[BLK:POOL_ROWS, :] = jnp.zeros((HALO + PAD, 512), F32)
            dkv_ref[...] = jnp.zeros_like(dkv_ref)

        ext_ref[PAD:PAD + HALO, :] = jnp.where(i > 0, pup_ref[...], 0.0)
        ext_ref[PAD + HALO:POOL_ROWS, :] = pu_ref[...]
        _window_sums(ext_ref, tmp_refs, True)
        dpooled = []
        for g, w in enumerate(POOL_WINDOWS):
            lanes = slice(g * 128, (g + 1) * 128)
            pooled, inv = _pool_block(ext_ref, tmp_refs, i, g, w)
            pooled_b = pooled.astype(BF16)
            mixed = jnp.dot(pooled_b, pw_ref[g], preferred_element_type=F32)
            scale = sc_ref[:, lanes]
            gate = pg_ref[:, lanes]
            sg = _sigmoid(gate)
            dpo = dcat_ref[:, lanes]
            dproj_ref[:, C_PG + g * 128:C_PG + (g + 1) * 128] = (
                dpo * (mixed * scale) * (sg * (1.0 + gate * (1.0 - sg)))).astype(BF16)
            dms = dpo * (gate * sg)
            dsc_ref[g:g + 1, :] += jnp.sum(dms * mixed, axis=0, keepdims=True)
            dmixed = (dms * scale).astype(BF16)
            dpw_ref[g] += lax.dot_general(pooled_b, dmixed, TN, preferred_element_type=F32)
            dpooled.append(lax.dot_general(dmixed, pw_ref[g], NT, preferred_element_type=F32))
            dext_ref[0:BLK, lanes] = dpooled[g] * inv
        _window_sums(dext_ref, tmp_refs, False)
        for g in range(len(POOL_WINDOWS)):
            lanes = slice(g * 128, (g + 1) * 128)
            dproj_ref[:, C_PU + g * 128:C_PU + (g + 1) * 128] = (tmp_refs[g % 2][0:BLK, lanes] - dpooled[g]).astype(BF16)
        dext_ref[BLK:BLK + HALO, :] = dext_ref[0:HALO, :]

        own = _own_block_mask()
        tri = tri_ref[...]
        k_var = _head_variants(kv_ref[:, 0:128], kvp_ref[:, 0:128])
        v_var = _head_variants(kv_ref[:, 128:256], kvp_ref[:, 128:256])
        q2 = [_stack_tiles(q_ref, hkv) for hkv in range(2)]
        s = _scores(q2, k_var, own)
        p, p_sink = {}, {}
        for key, s_head in s.items():
            head = _head_of(*key)
            p[key], p_sink[key] = _softmax(s_head, bias_ref[head], sink_ref[l, head])

        do2, p_b, dp = [], {}, {}
        for hkv in range(2):
            gate = _stack_tiles(ag_ref, hkv)
            sg = _sigmoid(gate)
            dca = _stack_tiles(dcat_ref, hkv, D_POOL)
            do2.append((dca * (gate * sg)).astype(BF16))
            o2 = jnp.zeros((2 * BLK, 128), F32)
            for half in range(2):
                p_b[hkv, half] = _spread_pair(p, hkv, half, tri)
                o2 = o2 + jnp.dot(p_b[hkv, half], v_var[hkv][half], preferred_element_type=F32)
                full = lax.dot_general(do2[hkv], v_var[hkv][half], NT, preferred_element_type=F32)
                for t in range(2):
                    dp[hkv, t, half] = _merge(_rows(full, t), own)
            dag = dca * o2 * (sg * (1.0 + gate * (1.0 - sg)))
            for t in range(2):
                lo = C_AG + (2 * hkv + t) * 128
                dproj_ref[:, lo:lo + 128] = _rows(dag, t).astype(BF16)

        ds = {}
        for key in p:
            delta = jnp.sum(p[key] * dp[key], axis=-1, keepdims=True)
            ds[key] = p[key] * (dp[key] - delta)
            head = _head_of(*key)
            dsink_ref[0:1, :] += jnp.where(lax.broadcasted_iota(jnp.int32, (1, 128), 1) == head,
                                           -jnp.sum(p_sink[key] * delta, axis=0, keepdims=True), 0.0)

        dk_acc = [[None, None], [None, None]]
        dv_acc = [[None, None], [None, None]]
        for hkv in range(2):
            dq2 = jnp.zeros((2 * BLK, 128), F32)
            for half in range(2):
                ds_b = _spread_pair(ds, hkv, half, tri)
                dq2 = dq2 + jnp.dot(ds_b, k_var[hkv][half], preferred_element_type=F32)
                dk_acc[hkv][half] = lax.dot_general(ds_b, q2[hkv], TN, preferred_element_type=F32)
                dv_acc[hkv][half] = lax.dot_general(p_b[hkv, half], do2[hkv], TN, preferred_element_type=F32)
            for t in range(2):
                lo = C_Q + (2 * hkv + t) * 128
                dproj_ref[:, lo:lo + 128] = (_rows(dq2, t) * 0.125).astype(BF16)

        low = lax.broadcasted_iota(jnp.int32, (2 * BLK, 128), 1) < 64

        def gather_heads(acc):
            return jnp.where(low, acc[0][0] + pltpu.roll(acc[0][1], 64, axis=1),
                             pltpu.roll(acc[1][0], 64, axis=1) + acc[1][1])

        dk = gather_heads(dk_acc) * 0.125
        dv = gather_heads(dv_acc)
        dproj_ref[:, C_K:C_V] = (dk[BLK:, :] + dkv_ref[:, 0:128]).astype(BF16)
        dproj_ref[:, C_V:C_AG] = (dv[BLK:, :] + dkv_ref[:, 128:256]).astype(BF16)
        dkv_ref[:, 0:128] = dk[:BLK, :]
        dkv_ref[:, 128:256] = dv[:BLK, :]

    rev = lambda w: pl.BlockSpec((BLK, w), lambda s: (NB - 1 - s, 0))
    prev = lambda w: pl.BlockSpec((BLK, w), lambda s: (jnp.maximum(NB - 2 - s, 0), 0))
    halo = pl.BlockSpec((HALO, 512), lambda s: (jnp.maximum((NB - 1 - s) * (BLK // HALO) - 1, 0), 0))
    return pl.pallas_call(
        body, name="bwd_mix", grid=(NB,),
        in_specs=[rev(512), halo, rev(512), rev(512), rev(256), prev(256), rev(512), rev(D),
                  _layer(l, 4, 128, 128), _layer(l, 1, 512), pl.BlockSpec(memory_space=pltpu.SMEM),
                  pl.BlockSpec((None, N_HEADS, BLK, BLK), lambda s: (jnp.minimum(NB - 1 - s, 1), 0, 0, 0)),
                  _whole((BLK, BLK))] + [ANY] * len(deps),
        out_specs=[rev(D_IN), _layer(l, 4, 128, 128),
                   pl.BlockSpec((4, 128), lambda s: (0, 0)), pl.BlockSpec((8, 128), lambda s: (0, 0))],
        out_shape=[jax.ShapeDtypeStruct((S, D_IN), BF16), jax.ShapeDtypeStruct((DEPTH, 4, 128, 128), F32),
                   jax.ShapeDtypeStruct((4, 128), F32), jax.ShapeDtypeStruct((8, 128), F32)],
        input_output_aliases={} if dpw_dest is None else {12 + len(deps): 1},
        scratch_shapes=[pltpu.VMEM((POOL_ROWS, 512), F32)] * 4 + [pltpu.VMEM((BLK, 256), F32)],
        compiler_params=_params(),
    )(pu, pu, pg, q, kv, kv, ag, dcat, pool_w, pool_scale, sinks, bias, tri, *deps)


def _bwd_in_dw(l, dproj, x, g_pre, place_arr, to_bf16, deps=()):
    n_steps = S // TM

    def body(dp_ref, x_ref, g_ref, place_ref, *rest):
        f32_ref, own_ref, dwb_ref, bf16_ref, dw_ref = rest[len(deps):]
        step = pl.program_id(0)

        @pl.when(step == 0)
        def _():
            bf16_ref[...] = f32_ref[...].astype(BF16)

        xt = x_ref[...]
        r = lax.rsqrt(jnp.mean(xt * xt, axis=-1, keepdims=True) + EPS)
        h = (xt * r * g_ref[...]).astype(BF16)
        _accumulate(dw_ref, step, lambda: lax.dot_general(dp_ref[...], h, TN, preferred_element_type=F32))

        @pl.when(step == n_steps - 1)
        def _():
            dwb_ref[...] = dw_ref[...].astype(BF16)
            own_ref[...] = _own_piece(dw_ref, place_ref)

    row = lambda w: pl.BlockSpec((TM, w), lambda i: (i, 0))
    full = _whole
    return pl.pallas_call(
        body, name="bwd_in_dw", grid=(n_steps,),
        in_specs=[row(D_IN), row(D), _layer(l, 1, D), pl.BlockSpec(memory_space=pltpu.SMEM)] + [ANY] * len(deps)
        + [full(to_bf16.shape)],
        out_specs=[full((D_IN // 8, D)), full((D_IN, D)), full(to_bf16.shape)],
        out_shape=[jax.ShapeDtypeStruct((D_IN // 8, D), F32), jax.ShapeDtypeStruct((D_IN, D), BF16),
                   jax.ShapeDtypeStruct(to_bf16.shape, BF16)],
        scratch_shapes=[pltpu.VMEM((D_IN, D), F32)],
        compiler_params=_params(),
    )(dproj, x, g_pre, place_arr, *deps, to_bf16)


def _bwd_in_dx(l, dproj, w_in_t, x, g_pre, dres, deps=(), dw_place=None):
    n_steps = S // TM
    with_dw = dw_place is not None

    def body(dp_ref, w_ref, x_ref, g_ref, dres_ref, *rest):
        place_ref = rest[0] if with_dw else None
        rest = rest[with_dw + len(deps):]
        if with_dw:
            dx_ref, dg_ref, own_ref, dwb_ref, acc_ref, dw_ref = rest
        else:
            dx_ref, dg_ref, acc_ref = rest
        step = pl.program_id(0)

        @pl.when(step == 0)
        def _():
            acc_ref[...] = jnp.zeros_like(acc_ref)

        g = g_ref[...]
        halves = [slice(k * (TM // 2), (k + 1) * (TM // 2)) for k in range(2)]
        dh = [jnp.dot(dp_ref[rows, :], w_ref[...], preferred_element_type=F32) for rows in halves]
        h = []
        for rows, dh_k in zip(halves, dh):
            xt = x_ref[rows, :]
            r = lax.rsqrt(jnp.mean(xt * xt, axis=-1, keepdims=True) + EPS)
            xn = xt * r
            acc_ref[...] += _rows8(dh_k * xn)
            a = dh_k * g
            dx_ref[rows, :] = dres_ref[rows, :] + (
                r * a - xt * (r * r * r) * jnp.mean(a * xt, axis=-1, keepdims=True))
            h.append((xn * g).astype(BF16))
        if with_dw:
            hb = jnp.concatenate(h, axis=0)
            _accumulate(dw_ref, step, lambda: lax.dot_general(dp_ref[...], hb, TN, preferred_element_type=F32))

        @pl.when(step == n_steps - 1)
        def _():
            _store_lane_rows(dg_ref, acc_ref[...])
            if with_dw:
                dwb_ref[...] = dw_ref[...].astype(BF16)
                own_ref[...] = _own_piece(dw_ref, place_ref)

    row = lambda w: pl.BlockSpec((TM, w), lambda i: (i, 0))
    full = _whole
    dw_specs = [full((D_IN // 8, D)), full((D_IN, D))] if with_dw else []
    dw_shapes = [jax.ShapeDtypeStruct((D_IN // 8, D), F32), jax.ShapeDtypeStruct((D_IN, D), BF16)] if with_dw else []
    return pl.pallas_call(
        body, name="bwd_in" if with_dw else "bwd_in_dx", grid=(n_steps,),
        in_specs=[row(D_IN), full((D_IN, D)), row(D), _layer(l, 1, D), row(D)]
        + [pl.BlockSpec(memory_space=pltpu.SMEM)] * with_dw + [ANY] * len(deps),
        out_specs=[row(D), full((8, 128))] + dw_specs,
        out_shape=[jax.ShapeDtypeStruct((S, D), F32), jax.ShapeDtypeStruct((8, 128), F32)] + dw_shapes,
        scratch_shapes=[pltpu.VMEM((8, D), F32)] + [pltpu.VMEM((D_IN, D), F32)] * with_dw,
        compiler_params=_params(),
    )(dproj, w_in_t, x, g_pre, dres, *((dw_place,) if with_dw else ()), *deps)


HBM =pl.BlockSpec(memory_space=pltpu.HBM)
SEM = pl.BlockSpec(memory_space=pltpu.SEMAPHORE)
def _split_copy(collective_id=None):
    return pltpu.CompilerParams(has_side_effects=pltpu.SideEffectType.DATAFLOW_SIDE_EFFECTING,
                                collective_id=collective_id)


SPLIT_COPY = _split_copy()


def _in_hbm(a):
    return pltpu.with_memory_space_constraint(a, pltpu.HBM)

def _place():
    return lax.axis_index("x"), lax.axis_index("y"), lax.axis_index("c")


def _other_chips(x, y):
    return [(1 - x, y), (x, 1 - y), (1 - x, 1 - y)]


def _peer(x, y, c, m):
    return (x ^ (m >> 2), y ^ ((m >> 1) & 1), c ^ (m & 1))


SAME_CORE = (2, 4, 6)


def _place_cast(name, src, chip_arr, tile, layers, deps=()):
    _, n, cols = src.shape
    steps = n // tile
    k = len(layers)

    def body(chip_ref, *refs):
        for s_ref, o_ref in zip(refs[:k], refs[k + len(deps):]):
            o_ref[...] = s_ref[...].astype(BF16)

    def layer_spec(l):
        return pl.BlockSpec((None, tile, cols), lambda i, chip: (l, i, 0))

    return pl.pallas_call(
        body, name=name,
        grid_spec=pltpu.PrefetchScalarGridSpec(
            num_scalar_prefetch=1, grid=(steps,),
            in_specs=[layer_spec(l) for l in layers] + [ANY] * len(deps),
            out_specs=[pl.BlockSpec((tile, cols), lambda i, chip: (chip[0] * steps + i, 0))] * k),
        out_shape=[jax.ShapeDtypeStruct((N_SHARDS * n, cols), BF16)] * k,
        compiler_params=_params(),
    )(chip_arr, *[src] * k, *deps)


def _place_other_half(name, src, place_arr, layer, dest):
    _, n, cols = src.shape

    def body(place_ref, s_ref, dest_ref, o_ref):
        o_ref[...] = s_ref[...].astype(BF16)

    return pl.pallas_call(
        body, name=name,
        grid_spec=pltpu.PrefetchScalarGridSpec(
            num_scalar_prefetch=1, grid=(1,),
            in_specs=[pl.BlockSpec((None, n // 2, cols), lambda i, place: (layer, 1 - place[1], 0)), ANY],
            out_specs=pl.BlockSpec((n // 2, cols), lambda i, place: (place[0] + 1 - 2 * place[1], 0))),
        out_shape=jax.ShapeDtypeStruct((N_SHARDS * n, cols), BF16),
        input_output_aliases={2: 0},
        compiler_params=_params(),
    )(place_arr, src, dest)


def _chip_rows(ref, chip, half=None):
    n = ref.shape[0] // N_SHARDS
    if half is None:
        return ref.at[pl.ds(pl.multiple_of(chip * n, 16), n), :]
    return ref.at[pl.ds(pl.multiple_of(chip * n + half * (n // 2), 16), n // 2), :]


def _gather_start(name, bufs, halved, collective_id):
    n = len(bufs)

    def body(*refs):
        ins, send, recv, token = refs[:n], refs[n:2 * n], refs[2 * n:3 * n], refs[-1]
        x, y, c = _place()
        _handshake([(*chip, c) for chip in _other_chips(x, y)])
        for a, buf in enumerate(ins):
            own = _chip_rows(buf, 2 * x + y, c if a in halved else None)
            for j, chip in enumerate(_other_chips(x, y)):
                pltpu.make_async_remote_copy(src_ref=own, dst_ref=own, send_sem=send[a].at[j], recv_sem=recv[a].at[j],
                                             device_id=(*chip, c), device_id_type=MESH).start()
        token[...] = jnp.zeros_like(token)

    outs = pl.pallas_call(
        body, name=name, in_specs=[HBM] * n,
        out_specs=[SEM] * (2 * n) + [HBM] * n + [pl.BlockSpec(memory_space=pltpu.VMEM)],
        out_shape=[pltpu.SemaphoreType.DMA((3,))] * (2 * n) + [pltpu.HBM(b.shape, b.dtype) for b in bufs]
        + [jax.ShapeDtypeStruct((8, 128), F32)],
        input_output_aliases={a: 2 * n + a for a in range(n)},
        compiler_params=_split_copy(collective_id),
    )(*[_in_hbm(b) for b in bufs])
    return outs[:n], outs[n:2 * n], outs[2 * n:3 * n], outs[-1]


def _gather_start_cast(name, src, layer, collective_id):
    _, n, cols = src.shape
    half = n // 2

    def body(src_ref, send, recv, buf_ref, token, f32_ref, bf16_ref, local):
        x, y, c = _place()
        own = _chip_rows(buf_ref, 2 * x + y, c)

        def cast():
            load = pltpu.make_async_copy(src_ref.at[layer, pl.ds(pl.multiple_of(c * half, 16), half), :], f32_ref,
                                         local.at[0])
            load.start()
            load.wait()
            bf16_ref[...] = f32_ref[...].astype(BF16)
            store = pltpu.make_async_copy(bf16_ref, own, local.at[1])
            store.start()
            store.wait()

        _handshake([(*chip, c) for chip in _other_chips(x, y)], cast)
        for j, chip in enumerate(_other_chips(x, y)):
            pltpu.make_async_remote_copy(src_ref=own, dst_ref=own, send_sem=send.at[j], recv_sem=recv.at[j],
                                         device_id=(*chip, c), device_id_type=MESH).start()
        token[...] = jnp.zeros_like(token)

    outs = pl.pallas_call(
        body, name=name, in_specs=[HBM],
        out_specs=[SEM, SEM, HBM, pl.BlockSpec(memory_space=pltpu.VMEM)],
        out_shape=[pltpu.SemaphoreType.DMA((3,))] * 2 + [pltpu.HBM((N_SHARDS * n, cols), BF16),
                                                         jax.ShapeDtypeStruct((8, 128), F32)],
        scratch_shapes=[pltpu.VMEM((half, cols), F32), pltpu.VMEM((half, cols), BF16), pltpu.SemaphoreType.DMA((2,))],
        compiler_params=_split_copy(collective_id),
    )(_in_hbm(src))
    return outs[:1], outs[1:2], outs[2:3], outs[3]


def _gather_wait(name, buf, send_sem, recv_sem, after, halved=False):
    def body(buf_ref, send_ref, recv_ref, *rest):
        x, y, c = _place()
        half = c if halved else None
        own = _chip_rows(buf_ref, 2 * x + y, half)
        for j, chip in enumerate(_other_chips(x, y)):
            copy = pltpu.make_async_remote_copy(src_ref=own, dst_ref=_chip_rows(buf_ref, 2 * chip[0] + chip[1], half),
                                                send_sem=send_ref.at[j], recv_sem=recv_ref.at[j],
                                                device_id=(*chip, c), device_id_type=MESH)
            copy.wait_send()
            copy.wait_recv()

    return pl.pallas_call(
        body, name=name, in_specs=[HBM, SEM, SEM] + [ANY] * len(after), out_specs=HBM,
        out_shape=pltpu.HBM(buf.shape, buf.dtype), input_output_aliases={0: 0}, compiler_params=SPLIT_COPY,
    )(buf, send_sem, recv_sem, *after)


def _handshake(peers, meanwhile=None):
    barrier = pltpu.get_barrier_semaphore()
    for peer in peers:
        pl.semaphore_signal(barrier, inc=1, device_id=peer, device_id_type=MESH)
    if meanwhile is not None:
        meanwhile()
    pl.semaphore_wait(barrier, len(peers))


def _sibling_handshake(x, y, c):
    _handshake([(x, y, 1 - c)])


def _forward_halves(name, bufs, collective_id):
    n = len(bufs)

    def body(*refs):
        ins, outs, (send_sems, recv_sems) = refs[:n], refs[n:2 * n], refs[2 * n:]
        x, y, c = _place()
        _sibling_handshake(x, y, c)

        def copy(a, j, chip, half):
            rows = 2 * chip[0] + chip[1]
            return pltpu.make_async_remote_copy(
                src_ref=_chip_rows(ins[a], rows, half), dst_ref=_chip_rows(outs[a], rows, half),
                send_sem=send_sems.at[3 * a + j], recv_sem=recv_sems.at[3 * a + j], device_id=(x, y, 1 - c),
                device_id_type=MESH)

        copies = [(a, j, chip) for a in range(n) for j, chip in enumerate(_other_chips(x, y))]
        for a, j, chip in copies:
            copy(a, j, chip, c).start()
        for a, j, chip in copies:
            copy(a, j, chip, c).wait_send()
            copy(a, j, chip, 1 - c).wait_recv()

    return pl.pallas_call(
        body, name=name, in_specs=[ANY] * n, out_specs=[ANY] * n,
        out_shape=[jax.ShapeDtypeStruct(b.shape, b.dtype) for b in bufs],
        input_output_aliases={a: a for a in range(n)},
        scratch_shapes=[pltpu.SemaphoreType.DMA((3 * n,))] * 2,
        compiler_params=pltpu.CompilerParams(collective_id=collective_id),
    )(*bufs)


def _piece_rows(ref, k):
    p = ref.shape[0] // 8
    return ref.at[pl.ds(pl.multiple_of(k * p, 32 // jnp.dtype(ref.dtype).itemsize), p), :]


def _exchange_start(name, arrays, collective_id):
    n = len(arrays)
    zones = [lax.empty((7, a.shape[0] // 8, a.shape[1]), a.dtype) for a in arrays]

    def body(*refs):
        srcs, lands = refs[:n], refs[n:2 * n]
        send, recv = refs[2 * n:3 * n], refs[3 * n:4 * n]
        x, y, c = _place()
        _handshake([_peer(x, y, c, m) for m in range(1, 8)])
        for a, (src, land) in enumerate(zip(srcs, lands)):
            for m in range(1, 8):
                px, py, pc = _peer(x, y, c, m)
                pltpu.make_async_remote_copy(
                    src_ref=_piece_rows(src, 4 * px + 2 * py + pc), dst_ref=land.at[m - 1], send_sem=send[a].at[m - 1],
                    recv_sem=recv[a].at[m - 1], device_id=(px, py, pc), device_id_type=MESH).start()

    outs = pl.pallas_call(
        body, name=name, in_specs=[HBM] * (2 * n), out_specs=[SEM] * (2 * n) + [HBM] * (2 * n),
        out_shape=[pltpu.SemaphoreType.DMA((7,))] * (2 * n) + [pltpu.HBM(a.shape, a.dtype) for a in arrays + zones],
        input_output_aliases={a: 2 * n + a for a in range(2 * n)},
        compiler_params=_split_copy(collective_id),
    )(*[_in_hbm(a) for a in arrays + zones])
    return outs[:n], outs[n:2 * n], outs[2 * n:3 * n], outs[3 * n:4 * n]


def _exchange_wait(name, started, after, which=None):
    which = range(len(started[2])) if which is None else which
    send_sems, recv_sems, arrays, zones = [[group[k] for k in which] for group in started[:4]]
    n = len(arrays)

    def body(*refs):
        srcs, lands = refs[:n], refs[n:2 * n]
        send, recv = refs[2 * n:3 * n], refs[3 * n:4 * n]
        x, y, c = _place()
        for a, (src, land) in enumerate(zip(srcs, lands)):
            for m in range(1, 8):
                px, py, pc = _peer(x, y, c, m)
                copy = pltpu.make_async_remote_copy(
                    src_ref=_piece_rows(src, 4 * px + 2 * py + pc), dst_ref=land.at[m - 1], send_sem=send[a].at[m - 1],
                    recv_sem=recv[a].at[m - 1], device_id=(px, py, pc), device_id_type=MESH)
                copy.wait_send()
                copy.wait_recv()

    outs = pl.pallas_call(
        body, name=name, in_specs=[HBM] * (2 * n) + [SEM] * (2 * n) + [ANY], out_specs=[HBM] * (2 * n),
        out_shape=[pltpu.HBM(a.shape, a.dtype) for a in list(arrays) + list(zones)],
        input_output_aliases={a: a for a in range(2 * n)}, compiler_params=SPLIT_COPY,
    )(*arrays, *zones, *send_sems, *recv_sems, after)
    return outs[n:]


def _sum_pieces(name, weights, place_arr, dests=None):
    steps = 2
    flat = [item for items in weights for item in items]
    n = len(flat)

    def body(place_ref, *refs):
        outs = iter(refs[len(refs) - len(weights):])
        k = 0
        for items in weights:
            out_ref = next(outs)
            for layer, _, _ in items:
                total = refs[k][...]
                for m in range(7):
                    total = total + refs[n + k][m].astype(F32)
                if len(items) == DEPTH:
                    out_ref[layer] = total
                else:
                    out_ref[...] = total
                k += 1

    def out_spec(items):
        _, own, _ = items[0]
        t, cols = own.shape[0] // steps, own.shape[1]
        if len(items) == DEPTH:
            return pl.BlockSpec((DEPTH, t, cols), lambda i, place: (0, place[1] * steps + i, 0))
        layer = items[0][0]
        return pl.BlockSpec((None, t, cols), lambda i, place: (layer, place[1] * steps + i, 0))

    owns = [own for _, own, _ in flat]
    dests = [] if dests is None else list(dests)
    return pl.pallas_call(
        body, name=name,
        grid_spec=pltpu.PrefetchScalarGridSpec(
            num_scalar_prefetch=1, grid=(steps,),
            in_specs=[pl.BlockSpec((o.shape[0] // steps, o.shape[1]), lambda i, place: (i, 0)) for o in owns]
            + [pl.BlockSpec((7, o.shape[0] // steps, o.shape[1]), lambda i, place: (0, i, 0)) for o in owns]
            + [ANY] * len(dests),
            out_specs=[out_spec(items) for items in weights]),
        out_shape=[jax.ShapeDtypeStruct((DEPTH, 2 * items[0][1].shape[0], items[0][1].shape[1]), F32)
                   for items in weights],
        input_output_aliases={1 + 2 * n + k: k for k in range(len(dests))},
        compiler_params=_params(),
    )(place_arr, *owns, *[recv for _, _, recv in flat], *dests)


def _sum_small(name, partials, recvs, place_arr):
    n = len(partials)

    def body(place_ref, *refs):
        for o_ref, r_ref, out_ref in zip(refs[:n], refs[n:2 * n], refs[2 * n:]):
            total = o_ref[...]
            for m in range(7):
                total = total + r_ref[m].astype(F32)
            out_ref[...] = total

    piece = lambda a: pl.BlockSpec((a.shape[0] // 8, a.shape[1]), lambda i, place: (place[0], 0))
    return pl.pallas_call(
        body, name=name,
        grid_spec=pltpu.PrefetchScalarGridSpec(
            num_scalar_prefetch=1, grid=(1,),
            in_specs=[piece(a) for a in partials] + [pl.BlockSpec(r.shape, lambda i, place: (0, 0, 0)) for r in recvs],
            out_specs=[piece(a) for a in partials]),
        out_shape=[jax.ShapeDtypeStruct(a.shape, F32) for a in partials],
        compiler_params=_params(),
    )(place_arr, *partials, *recvs)


def _share(name, bufs, parts, gathered=(), collective_id=None, summed=()):
    n, n_g, n_s = len(bufs), len(gathered), len(summed)
    total = n + n_g
    made = [target for target, _, _ in summed]

    def body(*refs):
        ins, extra, outs = refs[:total], refs[total:total + 2 * n_s], refs[total + 2 * n_s:2 * total + 2 * n_s]
        send_sems, recv_sems, send_g, recv_g = refs[2 * total + 2 * n_s:2 * total + 2 * n_s + 4]
        scratch = refs[2 * total + 2 * n_s + 4:]
        x, y, c = _place()

        def half(ref, l, which):
            p = ref.shape[1] // 2
            return ref.at[l, pl.ds(pl.multiple_of(which * p, 8), p), :]

        def loads(j):
            local, acc, got = scratch[0], scratch[1:1 + n_s], scratch[1 + n_s:]
            own_rows = extra[2 * j] if made[j][0] == "half" else _piece_rows(extra[2 * j], 4 * x + 2 * y + c)
            return [pltpu.make_async_copy(own_rows, acc[j], local.at[2 * j]),
                    pltpu.make_async_copy(extra[2 * j + 1], got[j], local.at[2 * j + 1])]

        def sum_of(j):
            local, acc, got = scratch[0], scratch[1:1 + n_s], scratch[1 + n_s:]
            for load in loads(j):
                load.wait()
            rows = acc[j].shape[0]
            step = min(rows, 96)
            for r in range(0, rows, step):
                part = acc[j][r:r + step, :]
                for m in range(7):
                    part = part + got[j][m, r:r + step, :].astype(F32)
                acc[j][r:r + step, :] = part
            if made[j][0] == "half":
                dest = half(outs[made[j][1]], made[j][2], c)
            else:
                dest = _piece_rows(outs[n + made[j][1]], 4 * x + 2 * y + c)
            store = pltpu.make_async_copy(acc[j], dest, local.at[2 * j])
            store.start()
            store.wait()

        def early():
            for j in range(n_s):
                for load in loads(j):
                    load.start()
            for j in range(n_s):
                if made[j][0] == "piece":
                    sum_of(j)

        _handshake([_peer(x, y, c, m) for m in (SAME_CORE if gathered else ()) + (1,)], early if summed else None)

        def swap(k, which):
            a, l = parts[k]
            held = outs if ("half", a, l) in made else ins
            return pltpu.make_async_remote_copy(
                src_ref=half(held[a], l, which), dst_ref=half(outs[a], l, which), send_sem=send_sems.at[k],
                recv_sem=recv_sems.at[k], device_id=(x, y, 1 - c), device_id_type=MESH)

        def spread(a, m, sender, held, to):
            k = 4 * sender[0] + 2 * sender[1] + sender[2]
            return pltpu.make_async_remote_copy(
                src_ref=_piece_rows(held[n + a], k), dst_ref=_piece_rows(outs[n + a], k),
                send_sem=send_g.at[7 * a + m - 1], recv_sem=recv_g.at[7 * a + m - 1], device_id=to, device_id_type=MESH)

        me, sibling = (x, y, c), (x, y, 1 - c)

        def own(a, m):
            return spread(a, m, me, outs if ("piece", a) in made else ins, _peer(x, y, c, m))

        def handed_on(a, m):
            return spread(a, m + 1, _peer(x, y, c, m), outs, sibling)

        for a in range(n_g):
            for m in SAME_CORE + (1,):
                own(a, m).start()
        for j in range(n_s):
            if made[j][0] == "half":
                sum_of(j)
        for k in range(len(parts)):
            swap(k, c).start()
        for a in range(n_g):
            for m in SAME_CORE:
                spread(a, m, _peer(x, y, c, m), ins, _peer(x, y, c, m)).wait_recv()
                handed_on(a, m).start()
        for k in range(len(parts)):
            swap(k, c).wait_send()
            swap(k, 1 - c).wait_recv()
        for a in range(n_g):
            for m in SAME_CORE + (1,):
                own(a, m).wait_send()
            for m in SAME_CORE:
                handed_on(a, m).wait_send()
                spread(a, m + 1, _peer(x, y, c, m + 1), ins, sibling).wait_recv()
            spread(a, 1, sibling, ins, sibling).wait_recv()

    arrays = list(bufs) + list(gathered)
    sum_scratch = []
    if summed:
        sum_scratch = [pltpu.SemaphoreType.DMA((2 * n_s,))] + [pltpu.VMEM(recv.shape[1:], F32) for _, _, recv in summed]
        sum_scratch += [pltpu.VMEM(recv.shape, recv.dtype) for _, _, recv in summed]
    return pl.pallas_call(
        body, name=name, in_specs=[ANY] * (total + 2 * n_s), out_specs=[ANY] * total,
        out_shape=[jax.ShapeDtypeStruct(b.shape, F32) for b in arrays],
        input_output_aliases={a: a for a in range(total)},
        scratch_shapes=[pltpu.SemaphoreType.DMA((max(len(parts), 1),))] * 2
        + [pltpu.SemaphoreType.DMA((max(7 * n_g, 1),))] * 2 + sum_scratch,
        compiler_params=pltpu.CompilerParams(collective_id=collective_id, vmem_limit_bytes=VMEM_LIMIT),
    )(*arrays, *[array for _, own, recv in summed for array in (own, recv)])


def _adamw_math(w, g, m, v):
    nm = ADAM_B1 * m + (1.0 - ADAM_B1) * g
    nv = ADAM_B2 * v + (1.0 - ADAM_B2) * (g * g)
    m_hat = nm / (1.0 - ADAM_B1 ** ADAM_STEP)
    v_hat = nv / (1.0 - ADAM_B2 ** ADAM_STEP)
    return -ADAM_LR * (m_hat / (jnp.sqrt(v_hat) + ADAM_EPS) + ADAM_WD * w), nm, nv


def _adamw(name, w, g, m, v, rows_per_step, first=0, count=None, dests=None, deps=(), small=()):
    layers, rows, cols = w.shape
    count = layers if count is None else count
    dests = () if dests is None else tuple(dests)
    n_in = 4 + len(dests) + len(deps)
    small_shapes = _small_shapes(small[4].shape) if small else []

    def body(*refs):
        w_ref, g_ref, m_ref, v_ref = refs[:4]
        d_ref, nm_ref, nv_ref, g_out_ref = refs[n_in + len(small):n_in + len(small) + 4]
        d_ref[...], nm_ref[...], nv_ref[...] = _adamw_math(w_ref[...], g_ref[...], m_ref[...], v_ref[...])
        g_out_ref[...] = g_ref[...]
        if small:
            @pl.when((pl.program_id(0) == 0) & (pl.program_id(1) == 0))
            def _():
                _adamw_small(*refs[n_in:n_in + len(small)], *refs[n_in + len(small) + 4:])

    spec = pl.BlockSpec((1, rows_per_step, cols), lambda l, i: (first + l, i, 0))
    whole = lambda shape: pl.BlockSpec(shape, lambda l, i: (0,) * len(shape))
    shape = jax.ShapeDtypeStruct(w.shape, F32)
    return pl.pallas_call(
        body, name=name, grid=(count, rows // rows_per_step),
        in_specs=[spec] * 4 + [ANY] * (len(dests) + len(deps)) + [whole(a.shape) for a in small],
        out_specs=[spec] * 4 + [whole(s) for s in small_shapes],
        out_shape=[shape] * 4 + [jax.ShapeDtypeStruct(s, F32) for s in small_shapes],
        input_output_aliases={4 + k: k for k in range(len(dests))},
        scratch_shapes=[pltpu.VMEM((MISC_ROWS, 128), F32)] * 3 if small else [],
        compiler_params=_params(("arbitrary", "arbitrary")),
    )(w, g, m, v, *dests, *deps, *small)


def _pack_misc(pool_scale, sinks, norm_pre, norm_post):
    sink_rows = jnp.zeros((DEPTH, 8, 128), F32).at[:, 0, 0:N_HEADS].set(sinks).reshape(2 * 8, 128)
    return jnp.concatenate([pool_scale.reshape(8, 128), norm_pre.reshape(16, 128), norm_post.reshape(16, 128),
                            sink_rows, jnp.zeros((8, 128), F32)], axis=0)


def _adamw_small(w_ref, g_ref, m_ref, v_ref, pw_ref, pg_ref, pm_ref, pv_ref, *rest):
    outs, pool_outs, (d_ref, nm_ref, nv_ref) = rest[:17], rest[17:21], rest[21:]
    pool_outs[0][...] = pg_ref[...]
    pool_outs[1][...], pool_outs[2][...], pool_outs[3][...] = _adamw_math(
        pw_ref[...], pg_ref[...], pm_ref[...], pv_ref[...])
    d_ref[...], nm_ref[...], nv_ref[...] = _adamw_math(w_ref[...], g_ref[...], m_ref[...], v_ref[...])
    for k, src in enumerate([g_ref, d_ref, nm_ref, nv_ref]):
        scale, sinks, pre, post = outs[4 * k:4 * k + 4]
        for l in range(DEPTH):
            for j in range(4):
                scale[l:l + 1, j * 128:(j + 1) * 128] = src[MISC_SCALE + 4 * l + j:MISC_SCALE + 4 * l + j + 1, :]
            for j in range(8):
                pre[l:l + 1, j * 128:(j + 1) * 128] = src[MISC_PRE + 8 * l + j:MISC_PRE + 8 * l + j + 1, :]
                post[l:l + 1, j * 128:(j + 1) * 128] = src[MISC_POST + 8 * l + j:MISC_POST + 8 * l + j + 1, :]
            sinks[l:l + 1, :] = src[MISC_SINKS + 8 * l:MISC_SINKS + 8 * l + 1, 0:N_HEADS]
    outs[16][...] = g_ref[MISC_LOSS:MISC_LOSS + 1, 0:1]


def _small_shapes(pool_shape):
    return [(DEPTH, D_POOL), (DEPTH, N_HEADS), (DEPTH, D), (DEPTH, D)] * 4 + [(1, 1)] + [pool_shape] * 4


def kernel(x, w_in, pool_w, pool_scale, attn_sinks, w_out, norm_pre, norm_post, loss_target, m_w_in, m_pool_w, m_pool_scale, m_attn_sinks, m_w_out, m_norm_pre, m_norm_post, v_w_in, v_pool_w, v_pool_scale, v_attn_sinks, v_w_out, v_norm_pre, v_norm_post):
    cx, cy, cc = _place()
    chip_arr = jnp.reshape(2 * cx + cy, (1,)).astype(jnp.int32)
    place_arr = jnp.stack([4 * cx + 2 * cy + cc, cc]).astype(jnp.int32)
    t = lambda a: jnp.transpose(a, (0, 2, 1))
    w_in_t = t(w_in)
    xs, target = x[0], loss_target[0]
    pool_w_b = pool_w.astype(BF16)
    tables = _attention_tables()
    scale3 = pool_scale.reshape(DEPTH, 1, D_POOL)
    pre3 = norm_pre.reshape(DEPTH, 1, D)
    post3 = norm_post.reshape(DEPTH, 1, D)

    first = _gather_start_cast("gather_start_first", w_in_t, 0, ID_GATHER_FIRST)
    wi0 = _place_other_half("place_w_in0_rest", w_in_t, place_arr, 0, first[2][0])
    (wi1,) = _place_cast("place_w_in1", w_in_t, chip_arr, 288, [1], deps=(first[3],))
    wo = _place_cast("place_w_out", w_out, chip_arr, 256, [0, 1], deps=(first[3],))
    rest = _gather_start("gather_start_rest", [wi1, wo[0], wo[1]], halved=(0, 1), collective_id=ID_GATHER_REST)
    send, recv, bufs = [first[k] + rest[k] for k in range(3)]
    bufs = [wi0, *bufs[1:]]
    order = {(0, "in"): 0, (1, "in"): 1, (0, "out"): 2, (1, "out"): 3}

    saved = []
    packed = [_pack_misc(pool_scale, attn_sinks, norm_pre, norm_post),
              _pack_misc(m_pool_scale, m_attn_sinks, m_norm_pre, m_norm_post),
              _pack_misc(v_pool_scale, v_attn_sinks, v_norm_pre, v_norm_post)]
    after = (first[3], rest[3], pool_w_b, *tables, scale3, pre3, post3, *packed)
    below = None
    for l in range(DEPTH):
        k = order[l, "in"]
        halves = [_gather_wait(f"gather_wait_in{l}", bufs[k], send[k], recv[k], after, halved=True)]
        if below is not None:
            halves.append(below[1])
        w_in_l, *w_out_below = _forward_halves(f"forward_w{l}", halves, collective_id=ID_FORWARD[l])
        if below is None:
            pu, pg, q, kv, ag = _fwd_in(l, xs, pre3, w_in_l)
        else:
            saved[l - 1][9] = w_out_below[0]
            y, xs, pu, pg, q, kv, ag = _fwd_in(l, xs, pre3, w_in_l, (below[0], w_out_below[0], below[2]))
            saved[l - 1][7] = y
        cat = _fwd_mix(l, pu, pg, q, kv, ag, pool_w_b, scale3, attn_sinks, tables)
        k = order[l, "out"]
        w_out_l = _gather_wait(f"gather_wait_out{l}", bufs[k], send[k], recv[k], (cat,), halved=l + 1 < DEPTH)
        saved.append([xs, pu, pg, q, kv, ag, cat, None, w_in_l, w_out_l])
        below, after = (cat, w_out_l, post3), (w_out_l,)

    x_in, pu, pg, q, kv, ag, cat, y, w_in_l, w_out_l = saved[1]
    dcat, dw_out1, dw_out1_b, dg_post1, loss, xs = _bwd_out(1, cat, w_out_l, post3, place_arr, x=x_in, target=target)
    ex1_out = _exchange_start("exchange_start_out1", [dw_out1_b], ID_OUT1)
    dproj, dpw, dsc1, dsink1 = _bwd_mix(1, pu, pg, q, kv, ag, dcat, pool_w_b, scale3, attn_sinks, tables,
                                        deps=(ex1_out[2][0],))
    dx, dg_pre1, dw_in1, dw_in1_b = _bwd_in_dx(1, dproj, w_in_l, x_in, pre3, xs, dw_place=place_arr)
    ex1_in = _exchange_start("exchange_start_in1", [dw_in1_b], ID_IN1)

    x_in, pu, pg, q, kv, ag, cat, y, w_in_l, w_out_l = saved[0]
    dcat, dw_out0, dw_out0_b, dg_post0 = _bwd_out(0, cat, w_out_l, post3, place_arr, dxn=dx, y=y, deps=(ex1_in[2][0],))
    ex0_out = _exchange_start("exchange_start_out0", [dw_out0_b], ID_OUT0)
    dproj, dpw, dsc0, dsink0 = _bwd_mix(0, pu, pg, q, kv, ag, dcat, pool_w_b, scale3, attn_sinks, tables,
                                        deps=(ex0_out[2][0],), dpw_dest=dpw)
    flat = lambda a: a.reshape(DEPTH * 4 * 128, 128)
    dw_in0, dw_in0_b, dpw_b = _bwd_in_dw(0, dproj, x_in, pre3, place_arr, flat(dpw))
    ex0_in = _exchange_start("exchange_start_in0", [dpw_b, dw_in0_b], ID_IN0)

    grad_x, dg_pre0 = _bwd_in_dx(0, dproj, w_in_l, x_in, pre3, dx, deps=(ex0_in[2][1],))
    small = [jnp.concatenate([dsc0, dsc1, dg_pre0, dg_pre1, dg_post0, dg_post1, dsink0, dsink1, loss], axis=0)]
    ex_small = _exchange_start("exchange_start_small", small, ID_SMALL)
    (recv_out1,) = _exchange_wait("exchange_wait_out1", ex1_out, ex_small[2][0])
    (recv_in1,) = _exchange_wait("exchange_wait_in1", ex1_in, recv_out1)
    g_in, g_out = _sum_pieces("sum_pieces_1", [[(1, dw_in1, recv_in1)], [(1, dw_out1, recv_out1)]], place_arr)
    (recv_out0,) = _exchange_wait("exchange_wait_out0", ex0_out, g_out)
    (g_out,) = _sum_pieces("sum_pieces_out0", [[(0, dw_out0, recv_out0)]], place_arr, dests=[g_out])
    g_in, g_out = _share("share_a", [g_in, g_out], [(0, 1), (1, 0), (1, 1)], collective_id=ID_SHARE_A)
    m_in_t, v_in_t = t(m_w_in), t(v_w_in)
    d_out, nm_out, nv_out, grad_w_out = _adamw("adamw_w_out", w_out, g_out, m_w_out, v_w_out, 256)
    upd_in = _adamw("adamw_w_in1", w_in_t, g_in, m_in_t, v_in_t, 288, first=1, count=1, deps=(d_out,))
    (recv_pw,) = _exchange_wait("exchange_wait_pool", ex0_in, upd_in[0], which=[0])
    (g_pw,) = _sum_small("sum_pool", [flat(dpw)], [recv_pw], place_arr)

    (recv_in0,) = _exchange_wait("exchange_wait_in0", ex0_in, g_pw, which=[1])
    (recv_misc,) = _exchange_wait("exchange_wait_small", ex_small, recv_in0)
    g_in, g_pw, g_misc = _share(
        "share_b", [g_in], [(0, 0)], [g_pw, lax.empty(small[0].shape, F32)], collective_id=ID_SHARE_B,
        summed=[(("half", 0, 0), dw_in0, recv_in0), (("piece", 1), small[0], recv_misc)])
    d_in, nm_in, nv_in, grad_w_in_t, *small_out = _adamw(
        "adamw_w_in0", w_in_t, g_in, m_in_t, v_in_t, 288, first=0, count=1, dests=upd_in,
        small=(packed[0], g_misc, packed[1], packed[2], flat(pool_w), g_pw, flat(m_pool_w), flat(v_pool_w)))
    (g_sc, g_sk, g_pre, g_post, d_sc, d_sk, d_pre, d_post,
     m_sc, m_sk, m_pre, m_post, v_sc, v_sk, v_pre, v_post, loss_sum) = small_out[:17]
    g_pw, d_pw, m_pw, v_pw = [a.reshape(pool_w.shape) for a in small_out[17:]]
    return (loss_sum[0, 0], grad_x[None], t(grad_w_in_t), g_pw, g_sc, g_sk, grad_w_out, g_pre, g_post,
            t(d_in), d_pw, d_sc, d_sk, d_out, d_pre, d_post,
            t(nm_in), m_pw, m_sc, m_sk, nm_out, m_pre, m_post,
            t(nv_in), v_pw, v_sc, v_sk, nv_out, v_pre, v_post)
```

```python
import jax
import jax.numpy as jnp
from jax import lax
from jax.experimental import pallas as pl
from jax.experimental.pallas import tpu as pltpu

F32 = jnp.float32
BF16 = jnp.bfloat16

S = 2048
D = 1024
DEPTH = 2
D_POOL = 512
POOL_WINDOWS = (2, 4, 8, 16)
N_HEADS = 8
D_IN = 2304
N_SHARDS = 4
W_IN_SHARD = D_IN // N_SHARDS
W_OUT_SHARD = D // N_SHARDS
BLK = 128
NB = S // BLK
HALO = 16
PAD = 8
EPS = 1e-6
NEG_INF = -1e30
C_PU, C_PG, C_Q, C_K, C_V, C_AG = 0, 512, 1024, 1536, 1664, 1792

ADAM_LR = 0.001
ADAM_B1 = 0.9
ADAM_B2 = 0.999
ADAM_EPS = 1e-08
ADAM_WD = 0.01
ADAM_STEP = 10

TM = 512
VMEM_LIMIT = 56 * 1024 * 1024

NT = (((1,), (1,)), ((), ()))
TN = (((0,), (0,)), ((), ()))

MESH = pl.DeviceIdType.MESH
ANY = pl.BlockSpec(memory_space=pl.ANY)

ID_FORWARD = (0, 1)
(ID_SHARE_A, ID_SHARE_B, ID_GATHER_FIRST, ID_GATHER_REST, ID_OUT1, ID_IN1, ID_OUT0, ID_IN0, ID_SMALL) = range(2, 11)

MISC_SCALE, MISC_PRE, MISC_POST, MISC_SINKS, MISC_LOSS = 0, 8, 24, 40, 56
MISC_ROWS = 64


def _params(sem=("arbitrary",)):
    return pltpu.CompilerParams(dimension_semantics=sem, vmem_limit_bytes=VMEM_LIMIT)


def _sigmoid(v):
    return 1.0 / (1.0 + jnp.exp(-v))


def _rows8(v):
    r, c = v.shape
    return v.reshape(r // 8, 8, c).sum(axis=0)


def _layer(l, *shape):
    zeros = (0,) * len(shape)
    return pl.BlockSpec((None,) + shape, lambda i: (l,) + zeros)


def _whole(shape):
    zeros = (0,) * len(shape)
    return pl.BlockSpec(shape, lambda i: zeros, pipeline_mode=pl.Buffered(1))


def _fwd_in(l, x, g_pre, w_in_t, below=None):
    fused = below is not None

    def body(x_ref, g_ref, w_ref, *rest):
        if fused:
            cat_ref, wo_ref, gp_ref, y_ref, xn_ref = rest[:5]
            y = jnp.dot(cat_ref[...], wo_ref[...], preferred_element_type=F32)
            y_ref[...] = y
            xt = x_ref[...] + y * lax.rsqrt(jnp.mean(y * y, axis=-1, keepdims=True) + EPS) * gp_ref[...]
            xn_ref[...] = xt
        else:
            xt = x_ref[...]
        pu_ref, pg_ref, q_ref, kv_ref, ag_ref = rest[-5:]
        r = lax.rsqrt(jnp.mean(xt * xt, axis=-1, keepdims=True) + EPS)
        h = (xt * r * g_ref[...]).astype(BF16)

        def proj(lo, hi):
            return lax.dot_general(h, w_ref[lo:hi, :], NT, preferred_element_type=F32)

        pu_ref[...] = proj(C_PU, C_PG)
        pg_ref[...] = proj(C_PG, C_Q)
        q_ref[...] = proj(C_Q, C_K).astype(BF16)
        kv_ref[...] = proj(C_K, C_AG).astype(BF16)
        ag_ref[...] = proj(C_AG, D_IN)

    row = lambda w: pl.BlockSpec((TM, w), lambda i: (i, 0))
    act = jax.ShapeDtypeStruct((S, D), F32)
    return pl.pallas_call(
        body, name="fwd_out_in" if fused else "fwd_in", grid=(S // TM,),
        in_specs=[row(D), _layer(l, 1, D), _whole((D_IN, D))]
        + ([row(D), _whole((D, D)), _layer(l - 1, 1, D)] if fused else []),
        out_specs=[row(D)] * (2 * fused) + [row(512), row(512), row(512), row(256), row(512)],
        out_shape=[act] * (2 * fused)
        + [jax.ShapeDtypeStruct((S, 512), F32), jax.ShapeDtypeStruct((S, 512), F32),
           jax.ShapeDtypeStruct((S, 512), BF16), jax.ShapeDtypeStruct((S, 256), BF16),
           jax.ShapeDtypeStruct((S, 512), F32)],
        compiler_params=_params(),
    )(x, g_pre, w_in_t, *(below if fused else ()))


LOG2E = 1.4426950408889634
SCORE_SCALE = 0.125 * LOG2E


def _attention_tables():
    qi = jnp.arange(BLK)[:, None]
    kj = jnp.arange(BLK)[None, :]
    dist = ((qi - kj) % BLK).astype(F32)
    slopes = jnp.exp2(-jnp.arange(1, N_HEADS + 1, dtype=F32))
    bias = -(slopes * LOG2E)[:, None, None] * dist[None]
    first = jnp.where(kj > qi, NEG_INF, bias)
    return jnp.stack([first, bias]), (kj <= qi).astype(BF16)


def _own_block_mask():
    return lax.broadcasted_iota(jnp.int32, (BLK, BLK), 1) <= lax.broadcasted_iota(jnp.int32, (BLK, BLK), 0)


def _merge(full, own):
    return jnp.where(own, full[:, BLK:], full[:, :BLK])


def _spread(v, tri):
    own = v * tri
    return jnp.concatenate([v - own, own], axis=1)


def _head_variants(cur, prev):
    both = jnp.concatenate([prev, cur], axis=0).astype(F32)
    swapped = pltpu.roll(both, 64, axis=1)
    low = lax.broadcasted_iota(jnp.int32, both.shape, 1) < 64
    zero = jnp.zeros_like(both)
    return ((jnp.where(low, both, zero).astype(BF16), jnp.where(low, zero, swapped).astype(BF16)),
            (jnp.where(low, swapped, zero).astype(BF16), jnp.where(low, zero, both).astype(BF16)))


def _head_of(hkv, t, half):
    return hkv * 4 + 2 * t + half


def _rows(v, t):
    return v[t * BLK:(t + 1) * BLK]


def _stack_tiles(ref, hkv, offset=0):
    lo = offset + 2 * hkv * 128
    return jnp.concatenate([ref[:, lo:lo + 128], ref[:, lo + 128:lo + 256]], axis=0)


def _scores(q2, k_var, own):
    s = {}
    for hkv in range(2):
        for half in range(2):
            full = lax.dot_general(q2[hkv], k_var[hkv][half], NT, preferred_element_type=F32)
            for t in range(2):
                s[hkv, t, half] = _merge(_rows(full, t), own)
    return s


def _softmax(s, bias, sink):
    s = s * SCORE_SCALE + bias
    sink2 = sink * LOG2E
    m = jnp.maximum(jnp.max(s, axis=-1, keepdims=True), sink2)
    p = jnp.exp2(s - m)
    e_sink = jnp.exp2(sink2 - m)
    inv = 1.0 / (jnp.sum(p, axis=-1, keepdims=True) + e_sink)
    return p * inv, e_sink * inv


def _spread_pair(v, hkv, half, tri):
    return jnp.concatenate([_spread(v[hkv, t, half].astype(BF16), tri) for t in range(2)], axis=0)


POOL_ROWS = PAD + HALO + BLK


def _window_sums(src_ref, tmp_refs, trailing):
    lo, hi = (PAD, POOL_ROWS) if trailing else (0, HALO + BLK)
    cur = src_ref
    for level in range(len(POOL_WINDOWS)):
        lanes = slice(level * 128, 512)
        shift = -(1 << level) if trailing else (1 << level)
        dst = tmp_refs[level % 2]
        dst[lo:hi, lanes] = cur[lo:hi, lanes] + cur[lo + shift:hi + shift, lanes]
        cur = dst


def _pool_block(ext_ref, tmp_refs, i, g, w):
    lanes = slice(g * 128, (g + 1) * 128)
    rows = slice(PAD + HALO, POOL_ROWS)
    t = (i * BLK + lax.broadcasted_iota(jnp.int32, (BLK, 1), 0)).astype(F32)
    inv = 1.0 / jnp.minimum(t + 1.0, float(w))
    return tmp_refs[g % 2][rows, lanes] * inv - ext_ref[rows, lanes], inv


def _fwd_mix(l, pu, pg, q, kv, ag, pool_w, pool_scale, sinks, tables):
    bias, tri = tables

    def body(pu_ref, pup_ref, pg_ref, q_ref, kv_ref, kvp_ref, ag_ref, pw_ref, sc_ref, sink_ref, bias_ref, tri_ref,
             cat_ref, ext_ref, *tmp_refs):
        i = pl.program_id(0)

        @pl.when(i == 0)
        def _():
            for ref in (ext_ref, *tmp_refs):
                ref[0:PAD, :] = jnp.zeros((PAD, 512), F32)

        ext_ref[PAD:PAD + HALO, :] = jnp.where(i > 0, pup_ref[...], 0.0)
        ext_ref[PAD + HALO:POOL_ROWS, :] = pu_ref[...]
        _window_sums(ext_ref, tmp_refs, True)
        for g, w in enumerate(POOL_WINDOWS):
            lanes = slice(g * 128, (g + 1) * 128)
            pooled, _ = _pool_block(ext_ref, tmp_refs, i, g, w)
            mixed = jnp.dot(pooled.astype(BF16), pw_ref[g], preferred_element_type=F32)
            gate = pg_ref[:, lanes]
            cat_ref[:, lanes] = (mixed * sc_ref[:, lanes] * (gate * _sigmoid(gate))).astype(BF16)

        own = _own_block_mask()
        tri = tri_ref[...]
        k_var = _head_variants(kv_ref[:, 0:128], kvp_ref[:, 0:128])
        v_var = _head_variants(kv_ref[:, 128:256], kvp_ref[:, 128:256])
        s = _scores([_stack_tiles(q_ref, hkv) for hkv in range(2)], k_var, own)
        p = {}
        for (hkv, t, half), s_head in s.items():
            head = _head_of(hkv, t, half)
            p[hkv, t, half], _ = _softmax(s_head, bias_ref[head], sink_ref[l, head])
        for hkv in range(2):
            o2 = jnp.zeros((2 * BLK, 128), F32)
            for half in range(2):
                o2 = o2 + jnp.dot(_spread_pair(p, hkv, half, tri), v_var[hkv][half], preferred_element_type=F32)
            for t in range(2):
                lo = (2 * hkv + t) * 128
                gate = ag_ref[:, lo:lo + 128]
                cat_ref[:, D_POOL + lo:D_POOL + lo + 128] = (_rows(o2, t) * (gate * _sigmoid(gate))).astype(BF16)

    blk = lambda w: pl.BlockSpec((BLK, w), lambda i: (i, 0))
    prev = lambda w: pl.BlockSpec((BLK, w), lambda i: (jnp.maximum(i - 1, 0), 0))
    halo = pl.BlockSpec((HALO, 512), lambda i: (jnp.maximum(i * (BLK // HALO) - 1, 0), 0))
    return pl.pallas_call(
        body, name="fwd_mix", grid=(NB,),
        in_specs=[blk(512), halo, blk(512), blk(512), blk(256), prev(256), blk(512),
                  _layer(l, 4, 128, 128), _layer(l, 1, 512), pl.BlockSpec(memory_space=pltpu.SMEM),
                  pl.BlockSpec((None, N_HEADS, BLK, BLK), lambda i: (jnp.minimum(i, 1), 0, 0, 0)), _whole((BLK, BLK))],
        out_specs=blk(D),
        out_shape=jax.ShapeDtypeStruct((S, D), BF16),
        scratch_shapes=[pltpu.VMEM((POOL_ROWS, 512), F32)] * 3,
        compiler_params=_params(),
    )(pu, pu, pg, q, kv, kv, ag, pool_w, pool_scale, sinks, bias, tri)


def _store_lane_rows(ref, acc):
    total = jnp.sum(acc, axis=0, keepdims=True)
    for k in range(ref.shape[0]):
        ref[k:k + 1, :] = total[:, k * 128:(k + 1) * 128]


def _own_piece(dw_ref, place_ref):
    p = dw_ref.shape[0] // 8
    return dw_ref[pl.ds(pl.multiple_of(place_ref[0] * p, 8), p), :]


def _bwd_out(l, cat, w_out, g_post, place_arr, dxn=None, y=None, x=None, target=None, deps=()):
    last = target is not None
    n_steps = S // TM

    def body(a_ref, b_ref, g_ref, cat_ref, w_ref, place_ref, *rest):
        dcat_ref, own_ref, dwb_ref, dg_ref = rest[len(deps):len(deps) + 4]
        rest = rest[len(deps) + 4:]
        acc_ref, dw_ref = rest[-2:]
        step = pl.program_id(0)

        @pl.when(step == 0)
        def _():
            dw_ref[...] = jnp.zeros_like(dw_ref)
            acc_ref[...] = jnp.zeros_like(acc_ref)

        cat = cat_ref[...]
        g = g_ref[...]
        y = jnp.dot(cat, w_ref[...], preferred_element_type=F32) if last else b_ref[...]
        r = lax.rsqrt(jnp.mean(y * y, axis=-1, keepdims=True) + EPS)
        if last:
            loss_ref, dx_ref, loss_acc_ref = rest[:3]
            err = a_ref[...] + y * r * g - b_ref[...]

            @pl.when(step == 0)
            def _():
                loss_acc_ref[...] = jnp.zeros_like(loss_acc_ref)

            loss_acc_ref[...] += _rows8(err * err)
            dz = err * (1.0 / D)
            dx_ref[...] = dz
        else:
            dz = a_ref[...]
        a = dz * g
        dy = r * a - y * (r * r * r) * jnp.mean(a * y, axis=-1, keepdims=True)
        acc_ref[...] += _rows8(dz * (y * r))
        dyb = dy.astype(BF16)
        dcat_ref[...] = lax.dot_general(dyb, w_ref[...], NT, preferred_element_type=F32)
        dw_ref[...] += lax.dot_general(cat, dyb, TN, preferred_element_type=F32)

        @pl.when(step == n_steps - 1)
        def _():
            _store_lane_rows(dg_ref, acc_ref[...])
            dwb_ref[...] = dw_ref[...].astype(BF16)
            own_ref[...] = _own_piece(dw_ref, place_ref)
            if last:
                loss_ref[...] = jnp.full((8, 128), (0.5 / D) * jnp.sum(loss_acc_ref[...]), F32)

    row = lambda: pl.BlockSpec((TM, D), lambda i: (i, 0))
    full = _whole
    return pl.pallas_call(
        body, name="out_loss_bwd" if last else "bwd_out", grid=(n_steps,),
        in_specs=[row(), row(), _layer(l, 1, D), row(), full((D, D)), pl.BlockSpec(memory_space=pltpu.SMEM)]
        + [ANY] * len(deps),
        out_specs=[row(), full((D // 8, D)), full((D, D)), full((8, 128))] + ([full((8, 128)), row()] if last else []),
        out_shape=[jax.ShapeDtypeStruct((S, D), F32), jax.ShapeDtypeStruct((D // 8, D), F32),
                   jax.ShapeDtypeStruct((D, D), BF16), jax.ShapeDtypeStruct((8, 128), F32)]
        + ([jax.ShapeDtypeStruct((8, 128), F32), jax.ShapeDtypeStruct((S, D), F32)] if last else []),
        scratch_shapes=([pltpu.VMEM((8, D), F32)] if last else []) + [pltpu.VMEM((8, D), F32), pltpu.VMEM((D, D), F32)],
        compiler_params=_params(),
    )(*((x, target) if last else (dxn, y)), g_post, cat, w_out, place_arr, *deps)


def _bwd_mix(l, pu, pg, q, kv, ag, dcat, pool_w, pool_scale, sinks, tables, deps=(), dpw_dest=None):
    bias, tri = tables
    deps = tuple(deps) + (() if dpw_dest is None else (dpw_dest,))

    def body(pu_ref, pup_ref, pg_ref, q_ref, kv_ref, kvp_ref, ag_ref, dcat_ref, pw_ref, sc_ref, sink_ref, bias_ref,
             tri_ref, *rest):
        dproj_ref, dpw_ref, dsc_ref, dsink_ref, ext_ref, dext_ref, tmp_a, tmp_b, dkv_ref = rest[len(deps):]
        tmp_refs = (tmp_a, tmp_b)
        step = pl.program_id(0)
        i = NB - 1 - step

        @pl.when(step == 0)
        def _():
            dpw_ref[...] = jnp.zeros_like(dpw_ref)
            dsc_ref[...] = jnp.zeros_like(dsc_ref)
            dsink_ref[...] = jnp.zeros_like(dsink_ref)
            for ref in (ext_ref, tmp_a, tmp_b):
                ref[0:PAD, :] = jnp.zeros((PAD, 512), F32)
            dext_ref[BLK:POOL_ROWS, :] = jnp.zeros((HALO + PAD, 512), F32)
            dkv_ref[...] = jnp.zeros_like(dkv_ref)

        ext_ref[PAD:PAD + HALO, :] = jnp.where(i > 0, pup_ref[...], 0.0)
        ext_ref[PAD + HALO:POOL_ROWS, :] = pu_ref[...]
        _window_sums(ext_ref, tmp_refs, True)
        dpooled = []
        for g, w in enumerate(POOL_WINDOWS):
            lanes = slice(g * 128, (g + 1) * 128)
            pooled, inv = _pool_block(ext_ref, tmp_refs, i, g, w)
            pooled_b = pooled.astype(BF16)
            mixed = jnp.dot(pooled_b, pw_ref[g], preferred_element_type=F32)
            scale = sc_ref[:, lanes]
            gate = pg_ref[:, lanes]
            sg = _sigmoid(gate)
            dpo = dcat_ref[:, lanes]
            dproj_ref[:, C_PG + g * 128:C_PG + (g + 1) * 128] = (
                dpo * (mixed * scale) * (sg * (1.0 + gate * (1.0 - sg)))).astype(BF16)
            dms = dpo * (gate * sg)
            dsc_ref[g:g + 1, :] += jnp.sum(dms * mixed, axis=0, keepdims=True)
            dmixed = (dms * scale).astype(BF16)
            dpw_ref[g] += lax.dot_general(pooled_b, dmixed, TN, preferred_element_type=F32)
            dpooled.append(lax.dot_general(dmixed, pw_ref[g], NT, preferred_element_type=F32))
            dext_ref[0:BLK, lanes] = dpooled[g] * inv
        _window_sums(dext_ref, tmp_refs, False)
        for g in range(len(POOL_WINDOWS)):
            lanes = slice(g * 128, (g + 1) * 128)
            dproj_ref[:, C_PU + g * 128:C_PU + (g + 1) * 128] = (tmp_refs[g % 2][0:BLK, lanes] - dpooled[g]).astype(BF16)
        dext_ref[BLK:BLK + HALO, :] = dext_ref[0:HALO, :]

        own = _own_block_mask()
        tri = tri_ref[...]
        k_var = _head_variants(kv_ref[:, 0:128], kvp_ref[:, 0:128])
        v_var = _head_variants(kv_ref[:, 128:256], kvp_ref[:, 128:256])
        q2 = [_stack_tiles(q_ref, hkv) for hkv in range(2)]
        s = _scores(q2, k_var, own)
        p, p_sink = {}, {}
        for key, s_head in s.items():
            head = _head_of(*key)
            p[key], p_sink[key] = _softmax(s_head, bias_ref[head], sink_ref[l, head])

        do2, p_b, dp = [], {}, {}
        for hkv in range(2):
            gate = _stack_tiles(ag_ref, hkv)
            sg = _sigmoid(gate)
            dca = _stack_tiles(dcat_ref, hkv, D_POOL)
            do2.append((dca * (gate * sg)).astype(BF16))
            o2 = jnp.zeros((2 * BLK, 128), F32)
            for half in range(2):
                p_b[hkv, half] = _spread_pair(p, hkv, half, tri)
                o2 = o2 + jnp.dot(p_b[hkv, half], v_var[hkv][half], preferred_element_type=F32)
                full = lax.dot_general(do2[hkv], v_var[hkv][half], NT, preferred_element_type=F32)
                for t in range(2):
                    dp[hkv, t, half] = _merge(_rows(full, t), own)
            dag = dca * o2 * (sg * (1.0 + gate * (1.0 - sg)))
            for t in range(2):
                lo = C_AG + (2 * hkv + t) * 128
                dproj_ref[:, lo:lo + 128] = _rows(dag, t).astype(BF16)

        ds = {}
        for key in p:
            delta = jnp.sum(p[key] * dp[key], axis=-1, keepdims=True)
            ds[key] = p[key] * (dp[key] - delta)
            head = _head_of(*key)
            dsink_ref[0:1, :] += jnp.where(lax.broadcasted_iota(jnp.int32, (1, 128), 1) == head,
                                           -jnp.sum(p_sink[key] * delta, axis=0, keepdims=True), 0.0)

        dk_acc = [[None, None], [None, None]]
        dv_acc = [[None, None], [None, None]]
        for hkv in range(2):
            dq2 = jnp.zeros((2 * BLK, 128), F32)
            for half in range(2):
                ds_b = _spread_pair(ds, hkv, half, tri)
                dq2 = dq2 + jnp.dot(ds_b, k_var[hkv][half], preferred_element_type=F32)
                dk_acc[hkv][half] = lax.dot_general(ds_b, q2[hkv], TN, preferred_element_type=F32)
                dv_acc[hkv][half] = lax.dot_general(p_b[hkv, half], do2[hkv], TN, preferred_element_type=F32)
            for t in range(2):
                lo = C_Q + (2 * hkv + t) * 128
                dproj_ref[:, lo:lo + 128] = (_rows(dq2, t) * 0.125).astype(BF16)

        low = lax.broadcasted_iota(jnp.int32, (2 * BLK, 128), 1) < 64

        def gather_heads(acc):
            return jnp.where(low, acc[0][0] + pltpu.roll(acc[0][1], 64, axis=1),
                             pltpu.roll(acc[1][0], 64, axis=1) + acc[1][1])

        dk = gather_heads(dk_acc) * 0.125
        dv = gather_heads(dv_acc)
        dproj_ref[:, C_K:C_V] = (dk[BLK:, :] + dkv_ref[:, 0:128]).astype(BF16)
        dproj_ref[:, C_V:C_AG] = (dv[BLK:, :] + dkv_ref[:, 128:256]).astype(BF16)
        dkv_ref[:, 0:128] = dk[:BLK, :]
        dkv_ref[:, 128:256] = dv[:BLK, :]

    rev = lambda w: pl.BlockSpec((BLK, w), lambda s: (NB - 1 - s, 0))
    prev = lambda w: pl.BlockSpec((BLK, w), lambda s: (jnp.maximum(NB - 2 - s, 0), 0))
    halo = pl.BlockSpec((HALO, 512), lambda s: (jnp.maximum((NB - 1 - s) * (BLK // HALO) - 1, 0), 0))
    return pl.pallas_call(
        body, name="bwd_mix", grid=(NB,),
        in_specs=[rev(512), halo, rev(512), rev(512), rev(256), prev(256), rev(512), rev(D),
                  _layer(l, 4, 128, 128), _layer(l, 1, 512), pl.BlockSpec(memory_space=pltpu.SMEM),
                  pl.BlockSpec((None, N_HEADS, BLK, BLK), lambda s: (jnp.minimum(NB - 1 - s, 1), 0, 0, 0)),
                  _whole((BLK, BLK))] + [ANY] * len(deps),
        out_specs=[rev(D_IN), _layer(l, 4, 128, 128),
                   pl.BlockSpec((4, 128), lambda s: (0, 0)), pl.BlockSpec((8, 128), lambda s: (0, 0))],
        out_shape=[jax.ShapeDtypeStruct((S, D_IN), BF16), jax.ShapeDtypeStruct((DEPTH, 4, 128, 128), F32),
                   jax.ShapeDtypeStruct((4, 128), F32), jax.ShapeDtypeStruct((8, 128), F32)],
        input_output_aliases={} if dpw_dest is None else {12 + len(deps): 1},
        scratch_shapes=[pltpu.VMEM((POOL_ROWS, 512), F32)] * 4 + [pltpu.VMEM((BLK, 256), F32)],
        compiler_params=_params(),
    )(pu, pu, pg, q, kv, kv, ag, dcat, pool_w, pool_scale, sinks, bias, tri, *deps)


def _bwd_in_dw(l, dproj, x, g_pre, place_arr, to_bf16, deps=()):
    n_steps = S // TM

    def body(dp_ref, x_ref, g_ref, place_ref, *rest):
        f32_ref, own_ref, dwb_ref, bf16_ref, dw_ref = rest[len(deps):]
        step = pl.program_id(0)

        @pl.when(step == 0)
        def _():
            dw_ref[...] = jnp.zeros_like(dw_ref)
            bf16_ref[...] = f32_ref[...].astype(BF16)

        xt = x_ref[...]
        r = lax.rsqrt(jnp.mean(xt * xt, axis=-1, keepdims=True) + EPS)
        h = (xt * r * g_ref[...]).astype(BF16)
        dw_ref[...] += lax.dot_general(dp_ref[...], h, TN, preferred_element_type=F32)

        @pl.when(step == n_steps - 1)
        def _():
            dwb_ref[...] = dw_ref[...].astype(BF16)
            own_ref[...] = _own_piece(dw_ref, place_ref)

    row = lambda w: pl.BlockSpec((TM, w), lambda i: (i, 0))
    full = _whole
    return pl.pallas_call(
        body, name="bwd_in_dw", grid=(n_steps,),
        in_specs=[row(D_IN), row(D), _layer(l, 1, D), pl.BlockSpec(memory_space=pltpu.SMEM)] + [ANY] * len(deps)
        + [full(to_bf16.shape)],
        out_specs=[full((D_IN // 8, D)), full((D_IN, D)), full(to_bf16.shape)],
        out_shape=[jax.ShapeDtypeStruct((D_IN // 8, D), F32), jax.ShapeDtypeStruct((D_IN, D), BF16),
                   jax.ShapeDtypeStruct(to_bf16.shape, BF16)],
        scratch_shapes=[pltpu.VMEM((D_IN, D), F32)],
        compiler_params=_params(),
    )(dproj, x, g_pre, place_arr, *deps, to_bf16)


def _bwd_in_dx(l, dproj, w_in_t, x, g_pre, dres, deps=(), dw_place=None):
    n_steps = S // TM
    with_dw = dw_place is not None

    def body(dp_ref, w_ref, x_ref, g_ref, dres_ref, *rest):
        place_ref = rest[0] if with_dw else None
        rest = rest[with_dw + len(deps):]
        if with_dw:
            dx_ref, dg_ref, own_ref, dwb_ref, acc_ref, dw_ref = rest
        else:
            dx_ref, dg_ref, acc_ref = rest
        step = pl.program_id(0)

        @pl.when(step == 0)
        def _():
            acc_ref[...] = jnp.zeros_like(acc_ref)
            if with_dw:
                dw_ref[...] = jnp.zeros_like(dw_ref)

        g = g_ref[...]
        halves = [slice(k * (TM // 2), (k + 1) * (TM // 2)) for k in range(2)]
        dh = [jnp.dot(dp_ref[rows, :], w_ref[...], preferred_element_type=F32) for rows in halves]
        h = []
        for rows, dh_k in zip(halves, dh):
            xt = x_ref[rows, :]
            r = lax.rsqrt(jnp.mean(xt * xt, axis=-1, keepdims=True) + EPS)
            xn = xt * r
            acc_ref[...] += _rows8(dh_k * xn)
            a = dh_k * g
            dx_ref[rows, :] = dres_ref[rows, :] + (
                r * a - xt * (r * r * r) * jnp.mean(a * xt, axis=-1, keepdims=True))
            h.append((xn * g).astype(BF16))
        if with_dw:
            dw_ref[...] += lax.dot_general(dp_ref[...], jnp.concatenate(h, axis=0), TN, preferred_element_type=F32)

        @pl.when(step == n_steps - 1)
        def _():
            _store_lane_rows(dg_ref, acc_ref[...])
            if with_dw:
                dwb_ref[...] = dw_ref[...].astype(BF16)
                own_ref[...] = _own_piece(dw_ref, place_ref)

    row = lambda w: pl.BlockSpec((TM, w), lambda i: (i, 0))
    full = _whole
    dw_specs = [full((D_IN // 8, D)), full((D_IN, D))] if with_dw else []
    dw_shapes = [jax.ShapeDtypeStruct((D_IN // 8, D), F32), jax.ShapeDtypeStruct((D_IN, D), BF16)] if with_dw else []
    return pl.pallas_call(
        body, name="bwd_in" if with_dw else "bwd_in_dx", grid=(n_steps,),
        in_specs=[row(D_IN), full((D_IN, D)), row(D), _layer(l, 1, D), row(D)]
        + [pl.BlockSpec(memory_space=pltpu.SMEM)] * with_dw + [ANY] * len(deps),
        out_specs=[row(D), full((8, 128))] + dw_specs,
        out_shape=[jax.ShapeDtypeStruct((S, D), F32), jax.ShapeDtypeStruct((8, 128), F32)] + dw_shapes,
        scratch_shapes=[pltpu.VMEM((8, D), F32)] + [pltpu.VMEM((D_IN, D), F32)] * with_dw,
        compiler_params=_params(),
    )(dproj, w_in_t, x, g_pre, dres, *((dw_place,) if with_dw else ()), *deps)


HBM =pl.BlockSpec(memory_space=pltpu.HBM)
SEM = pl.BlockSpec(memory_space=pltpu.SEMAPHORE)
def _split_copy(collective_id=None):
    return pltpu.CompilerParams(has_side_effects=pltpu.SideEffectType.DATAFLOW_SIDE_EFFECTING,
                                collective_id=collective_id)


SPLIT_COPY = _split_copy()


def _in_hbm(a):
    return pltpu.with_memory_space_constraint(a, pltpu.HBM)

def _place():
    return lax.axis_index("x"), lax.axis_index("y"), lax.axis_index("c")


def _other_chips(x, y):
    return [(1 - x, y), (x, 1 - y), (1 - x, 1 - y)]


def _peer(x, y, c, m):
    return (x ^ (m >> 2), y ^ ((m >> 1) & 1), c ^ (m & 1))


SAME_CORE = (2, 4, 6)


def _place_cast(name, src, chip_arr, tile, layers, deps=()):
    _, n, cols = src.shape
    steps = n // tile
    k = len(layers)

    def body(chip_ref, *refs):
        for s_ref, o_ref in zip(refs[:k], refs[k + len(deps):]):
            o_ref[...] = s_ref[...].astype(BF16)

    def layer_spec(l):
        return pl.BlockSpec((None, tile, cols), lambda i, chip: (l, i, 0))

    return pl.pallas_call(
        body, name=name,
        grid_spec=pltpu.PrefetchScalarGridSpec(
            num_scalar_prefetch=1, grid=(steps,),
            in_specs=[layer_spec(l) for l in layers] + [ANY] * len(deps),
            out_specs=[pl.BlockSpec((tile, cols), lambda i, chip: (chip[0] * steps + i, 0))] * k),
        out_shape=[jax.ShapeDtypeStruct((N_SHARDS * n, cols), BF16)] * k,
        compiler_params=_params(),
    )(chip_arr, *[src] * k, *deps)


def _place_other_half(name, src, place_arr, layer, dest):
    _, n, cols = src.shape

    def body(place_ref, s_ref, dest_ref, o_ref):
        o_ref[...] = s_ref[...].astype(BF16)

    return pl.pallas_call(
        body, name=name,
        grid_spec=pltpu.PrefetchScalarGridSpec(
            num_scalar_prefetch=1, grid=(1,),
            in_specs=[pl.BlockSpec((None, n // 2, cols), lambda i, place: (layer, 1 - place[1], 0)), ANY],
            out_specs=pl.BlockSpec((n // 2, cols), lambda i, place: (place[0] + 1 - 2 * place[1], 0))),
        out_shape=jax.ShapeDtypeStruct((N_SHARDS * n, cols), BF16),
        input_output_aliases={2: 0},
        compiler_params=_params(),
    )(place_arr, src, dest)


def _chip_rows(ref, chip, half=None):
    n = ref.shape[0] // N_SHARDS
    if half is None:
        return ref.at[pl.ds(pl.multiple_of(chip * n, 16), n), :]
    return ref.at[pl.ds(pl.multiple_of(chip * n + half * (n // 2), 16), n // 2), :]


def _gather_start(name, bufs, halved, collective_id):
    n = len(bufs)

    def body(*refs):
        ins, send, recv, token = refs[:n], refs[n:2 * n], refs[2 * n:3 * n], refs[-1]
        x, y, c = _place()
        _handshake([(*chip, c) for chip in _other_chips(x, y)])
        for a, buf in enumerate(ins):
            own = _chip_rows(buf, 2 * x + y, c if a in halved else None)
            for j, chip in enumerate(_other_chips(x, y)):
                pltpu.make_async_remote_copy(src_ref=own, dst_ref=own, send_sem=send[a].at[j], recv_sem=recv[a].at[j],
                                             device_id=(*chip, c), device_id_type=MESH).start()
        token[...] = jnp.zeros_like(token)

    outs = pl.pallas_call(
        body, name=name, in_specs=[HBM] * n,
        out_specs=[SEM] * (2 * n) + [HBM] * n + [pl.BlockSpec(memory_space=pltpu.VMEM)],
        out_shape=[pltpu.SemaphoreType.DMA((3,))] * (2 * n) + [pltpu.HBM(b.shape, b.dtype) for b in bufs]
        + [jax.ShapeDtypeStruct((8, 128), F32)],
        input_output_aliases={a: 2 * n + a for a in range(n)},
        compiler_params=_split_copy(collective_id),
    )(*[_in_hbm(b) for b in bufs])
    return outs[:n], outs[n:2 * n], outs[2 * n:3 * n], outs[-1]


def _gather_start_cast(name, src, layer, collective_id):
    _, n, cols = src.shape
    half = n // 2

    def body(src_ref, send, recv, buf_ref, token, f32_ref, bf16_ref, local):
        x, y, c = _place()
        own = _chip_rows(buf_ref, 2 * x + y, c)

        def cast():
            load = pltpu.make_async_copy(src_ref.at[layer, pl.ds(pl.multiple_of(c * half, 16), half), :], f32_ref,
                                         local.at[0])
            load.start()
            load.wait()
            bf16_ref[...] = f32_ref[...].astype(BF16)
            store = pltpu.make_async_copy(bf16_ref, own, local.at[1])
            store.start()
            store.wait()

        _handshake([(*chip, c) for chip in _other_chips(x, y)], cast)
        for j, chip in enumerate(_other_chips(x, y)):
            pltpu.make_async_remote_copy(src_ref=own, dst_ref=own, send_sem=send.at[j], recv_sem=recv.at[j],
                                         device_id=(*chip, c), device_id_type=MESH).start()
        token[...] = jnp.zeros_like(token)

    outs = pl.pallas_call(
        body, name=name, in_specs=[HBM],
        out_specs=[SEM, SEM, HBM, pl.BlockSpec(memory_space=pltpu.VMEM)],
        out_shape=[pltpu.SemaphoreType.DMA((3,))] * 2 + [pltpu.HBM((N_SHARDS * n, cols), BF16),
                                                         jax.ShapeDtypeStruct((8, 128), F32)],
        scratch_shapes=[pltpu.VMEM((half, cols), F32), pltpu.VMEM((half, cols), BF16), pltpu.SemaphoreType.DMA((2,))],
        compiler_params=_split_copy(collective_id),
    )(_in_hbm(src))
    return outs[:1], outs[1:2], outs[2:3], outs[3]


def _gather_wait(name, buf, send_sem, recv_sem, after, halved=False):
    def body(buf_ref, send_ref, recv_ref, *rest):
        x, y, c = _place()
        half = c if halved else None
        own = _chip_rows(buf_ref, 2 * x + y, half)
        for j, chip in enumerate(_other_chips(x, y)):
            copy = pltpu.make_async_remote_copy(src_ref=own, dst_ref=_chip_rows(buf_ref, 2 * chip[0] + chip[1], half),
                                                send_sem=send_ref.at[j], recv_sem=recv_ref.at[j],
                                                device_id=(*chip, c), device_id_type=MESH)
            copy.wait_send()
            copy.wait_recv()

    return pl.pallas_call(
        body, name=name, in_specs=[HBM, SEM, SEM] + [ANY] * len(after), out_specs=HBM,
        out_shape=pltpu.HBM(buf.shape, buf.dtype), input_output_aliases={0: 0}, compiler_params=SPLIT_COPY,
    )(buf, send_sem, recv_sem, *after)


def _handshake(peers, meanwhile=None):
    barrier = pltpu.get_barrier_semaphore()
    for peer in peers:
        pl.semaphore_signal(barrier, inc=1, device_id=peer, device_id_type=MESH)
    if meanwhile is not None:
        meanwhile()
    pl.semaphore_wait(barrier, len(peers))


def _sibling_handshake(x, y, c):
    _handshake([(x, y, 1 - c)])


def _forward_halves(name, bufs, collective_id):
    n = len(bufs)

    def body(*refs):
        ins, outs, (send_sems, recv_sems) = refs[:n], refs[n:2 * n], refs[2 * n:]
        x, y, c = _place()
        _sibling_handshake(x, y, c)

        def copy(a, j, chip, half):
            rows = 2 * chip[0] + chip[1]
            return pltpu.make_async_remote_copy(
                src_ref=_chip_rows(ins[a], rows, half), dst_ref=_chip_rows(outs[a], rows, half),
                send_sem=send_sems.at[3 * a + j], recv_sem=recv_sems.at[3 * a + j], device_id=(x, y, 1 - c),
                device_id_type=MESH)

        copies = [(a, j, chip) for a in range(n) for j, chip in enumerate(_other_chips(x, y))]
        for a, j, chip in copies:
            copy(a, j, chip, c).start()
        for a, j, chip in copies:
            copy(a, j, chip, c).wait_send()
            copy(a, j, chip, 1 - c).wait_recv()

    return pl.pallas_call(
        body, name=name, in_specs=[ANY] * n, out_specs=[ANY] * n,
        out_shape=[jax.ShapeDtypeStruct(b.shape, b.dtype) for b in bufs],
        input_output_aliases={a: a for a in range(n)},
        scratch_shapes=[pltpu.SemaphoreType.DMA((3 * n,))] * 2,
        compiler_params=pltpu.CompilerParams(collective_id=collective_id),
    )(*bufs)


def _piece_rows(ref, k):
    p = ref.shape[0] // 8
    return ref.at[pl.ds(pl.multiple_of(k * p, 32 // jnp.dtype(ref.dtype).itemsize), p), :]


def _exchange_start(name, arrays, collective_id):
    n = len(arrays)
    zones = [lax.empty((7, a.shape[0] // 8, a.shape[1]), a.dtype) for a in arrays]

    def body(*refs):
        srcs, lands = refs[:n], refs[n:2 * n]
        send, recv = refs[2 * n:3 * n], refs[3 * n:4 * n]
        x, y, c = _place()
        _handshake([_peer(x, y, c, m) for m in range(1, 8)])
        for a, (src, land) in enumerate(zip(srcs, lands)):
            for m in range(1, 8):
                px, py, pc = _peer(x, y, c, m)
                pltpu.make_async_remote_copy(
                    src_ref=_piece_rows(src, 4 * px + 2 * py + pc), dst_ref=land.at[m - 1], send_sem=send[a].at[m - 1],
                    recv_sem=recv[a].at[m - 1], device_id=(px, py, pc), device_id_type=MESH).start()

    outs = pl.pallas_call(
        body, name=name, in_specs=[HBM] * (2 * n), out_specs=[SEM] * (2 * n) + [HBM] * (2 * n),
        out_shape=[pltpu.SemaphoreType.DMA((7,))] * (2 * n) + [pltpu.HBM(a.shape, a.dtype) for a in arrays + zones],
        input_output_aliases={a: 2 * n + a for a in range(2 * n)},
        compiler_params=_split_copy(collective_id),
    )(*[_in_hbm(a) for a in arrays + zones])
    return outs[:n], outs[n:2 * n], outs[2 * n:3 * n], outs[3 * n:4 * n]


def _exchange_wait(name, started, after, which=None):
    which = range(len(started[2])) if which is None else which
    send_sems, recv_sems, arrays, zones = [[group[k] for k in which] for group in started[:4]]
    n = len(arrays)

    def body(*refs):
        srcs, lands = refs[:n], refs[n:2 * n]
        send, recv = refs[2 * n:3 * n], refs[3 * n:4 * n]
        x, y, c = _place()
        for a, (src, land) in enumerate(zip(srcs, lands)):
            for m in range(1, 8):
                px, py, pc = _peer(x, y, c, m)
                copy = pltpu.make_async_remote_copy(
                    src_ref=_piece_rows(src, 4 * px + 2 * py + pc), dst_ref=land.at[m - 1], send_sem=send[a].at[m - 1],
                    recv_sem=recv[a].at[m - 1], device_id=(px, py, pc), device_id_type=MESH)
                copy.wait_send()
                copy.wait_recv()

    outs = pl.pallas_call(
        body, name=name, in_specs=[HBM] * (2 * n) + [SEM] * (2 * n) + [ANY], out_specs=[HBM] * (2 * n),
        out_shape=[pltpu.HBM(a.shape, a.dtype) for a in list(arrays) + list(zones)],
        input_output_aliases={a: a for a in range(2 * n)}, compiler_params=SPLIT_COPY,
    )(*arrays, *zones, *send_sems, *recv_sems, after)
    return outs[n:]


def _sum_pieces(name, weights, place_arr, dests=None):
    steps = 2
    flat = [item for items in weights for item in items]
    n = len(flat)

    def body(place_ref, *refs):
        outs = iter(refs[len(refs) - len(weights):])
        k = 0
        for items in weights:
            out_ref = next(outs)
            for layer, _, _ in items:
                total = refs[k][...]
                for m in range(7):
                    total = total + refs[n + k][m].astype(F32)
                if len(items) == DEPTH:
                    out_ref[layer] = total
                else:
                    out_ref[...] = total
                k += 1

    def out_spec(items):
        _, own, _ = items[0]
        t, cols = own.shape[0] // steps, own.shape[1]
        if len(items) == DEPTH:
            return pl.BlockSpec((DEPTH, t, cols), lambda i, place: (0, place[1] * steps + i, 0))
        layer = items[0][0]
        return pl.BlockSpec((None, t, cols), lambda i, place: (layer, place[1] * steps + i, 0))

    owns = [own for _, own, _ in flat]
    dests = [] if dests is None else list(dests)
    return pl.pallas_call(
        body, name=name,
        grid_spec=pltpu.PrefetchScalarGridSpec(
            num_scalar_prefetch=1, grid=(steps,),
            in_specs=[pl.BlockSpec((o.shape[0] // steps, o.shape[1]), lambda i, place: (i, 0)) for o in owns]
            + [pl.BlockSpec((7, o.shape[0] // steps, o.shape[1]), lambda i, place: (0, i, 0)) for o in owns]
            + [ANY] * len(dests),
            out_specs=[out_spec(items) for items in weights]),
        out_shape=[jax.ShapeDtypeStruct((DEPTH, 2 * items[0][1].shape[0], items[0][1].shape[1]), F32)
                   for items in weights],
        input_output_aliases={1 + 2 * n + k: k for k in range(len(dests))},
        compiler_params=_params(),
    )(place_arr, *owns, *[recv for _, _, recv in flat], *dests)


def _sum_small(name, partials, recvs, place_arr):
    n = len(partials)

    def body(place_ref, *refs):
        for o_ref, r_ref, out_ref in zip(refs[:n], refs[n:2 * n], refs[2 * n:]):
            total = o_ref[...]
            for m in range(7):
                total = total + r_ref[m].astype(F32)
            out_ref[...] = total

    piece = lambda a: pl.BlockSpec((a.shape[0] // 8, a.shape[1]), lambda i, place: (place[0], 0))
    return pl.pallas_call(
        body, name=name,
        grid_spec=pltpu.PrefetchScalarGridSpec(
            num_scalar_prefetch=1, grid=(1,),
            in_specs=[piece(a) for a in partials] + [pl.BlockSpec(r.shape, lambda i, place: (0, 0, 0)) for r in recvs],
            out_specs=[piece(a) for a in partials]),
        out_shape=[jax.ShapeDtypeStruct(a.shape, F32) for a in partials],
        compiler_params=_params(),
    )(place_arr, *partials, *recvs)


def _share(name, bufs, parts, gathered=(), collective_id=None, summed=()):
    n, n_g, n_s = len(bufs), len(gathered), len(summed)
    total = n + n_g
    made = [target for target, _, _ in summed]

    def body(*refs):
        ins, extra, outs = refs[:total], refs[total:total + 2 * n_s], refs[total + 2 * n_s:2 * total + 2 * n_s]
        send_sems, recv_sems, send_g, recv_g = refs[2 * total + 2 * n_s:2 * total + 2 * n_s + 4]
        scratch = refs[2 * total + 2 * n_s + 4:]
        x, y, c = _place()

        def half(ref, l, which):
            p = ref.shape[1] // 2
            return ref.at[l, pl.ds(pl.multiple_of(which * p, 8), p), :]

        def loads(j):
            local, acc, got = scratch[0], scratch[1:1 + n_s], scratch[1 + n_s:]
            own_rows = extra[2 * j] if made[j][0] == "half" else _piece_rows(extra[2 * j], 4 * x + 2 * y + c)
            return [pltpu.make_async_copy(own_rows, acc[j], local.at[2 * j]),
                    pltpu.make_async_copy(extra[2 * j + 1], got[j], local.at[2 * j + 1])]

        def sum_of(j):
            local, acc, got = scratch[0], scratch[1:1 + n_s], scratch[1 + n_s:]
            for load in loads(j):
                load.wait()
            rows = acc[j].shape[0]
            step = min(rows, 96)
            for r in range(0, rows, step):
                part = acc[j][r:r + step, :]
                for m in range(7):
                    part = part + got[j][m, r:r + step, :].astype(F32)
                acc[j][r:r + step, :] = part
            if made[j][0] == "half":
                dest = half(outs[made[j][1]], made[j][2], c)
            else:
                dest = _piece_rows(outs[n + made[j][1]], 4 * x + 2 * y + c)
            store = pltpu.make_async_copy(acc[j], dest, local.at[2 * j])
            store.start()
            store.wait()

        def early():
            for j in range(n_s):
                for load in loads(j):
                    load.start()
            for j in range(n_s):
                if made[j][0] == "piece":
                    sum_of(j)

        _handshake([_peer(x, y, c, m) for m in (SAME_CORE if gathered else ()) + (1,)], early if summed else None)

        def swap(k, which):
            a, l = parts[k]
            held = outs if ("half", a, l) in made else ins
            return pltpu.make_async_remote_copy(
                src_ref=half(held[a], l, which), dst_ref=half(outs[a], l, which), send_sem=send_sems.at[k],
                recv_sem=recv_sems.at[k], device_id=(x, y, 1 - c), device_id_type=MESH)

        def spread(a, m, sender, held, to):
            k = 4 * sender[0] + 2 * sender[1] + sender[2]
            return pltpu.make_async_remote_copy(
                src_ref=_piece_rows(held[n + a], k), dst_ref=_piece_rows(outs[n + a], k),
                send_sem=send_g.at[7 * a + m - 1], recv_sem=recv_g.at[7 * a + m - 1], device_id=to, device_id_type=MESH)

        me, sibling = (x, y, c), (x, y, 1 - c)

        def own(a, m):
            return spread(a, m, me, outs if ("piece", a) in made else ins, _peer(x, y, c, m))

        def handed_on(a, m):
            return spread(a, m + 1, _peer(x, y, c, m), outs, sibling)

        for a in range(n_g):
            for m in SAME_CORE + (1,):
                own(a, m).start()
        for j in range(n_s):
            if made[j][0] == "half":
                sum_of(j)
        for k in range(len(parts)):
            swap(k, c).start()
        for a in range(n_g):
            for m in SAME_CORE:
                spread(a, m, _peer(x, y, c, m), ins, _peer(x, y, c, m)).wait_recv()
                handed_on(a, m).start()
        for k in range(len(parts)):
            swap(k, c).wait_send()
            swap(k, 1 - c).wait_recv()
        for a in range(n_g):
            for m in SAME_CORE + (1,):
                own(a, m).wait_send()
            for m in SAME_CORE:
                handed_on(a, m).wait_send()
                spread(a, m + 1, _peer(x, y, c, m + 1), ins, sibling).wait_recv()
            spread(a, 1, sibling, ins, sibling).wait_recv()

    arrays = list(bufs) + list(gathered)
    sum_scratch = []
    if summed:
        sum_scratch = [pltpu.SemaphoreType.DMA((2 * n_s,))] + [pltpu.VMEM(recv.shape[1:], F32) for _, _, recv in summed]
        sum_scratch += [pltpu.VMEM(recv.shape, recv.dtype) for _, _, recv in summed]
    return pl.pallas_call(
        body, name=name, in_specs=[ANY] * (total + 2 * n_s), out_specs=[ANY] * total,
        out_shape=[jax.ShapeDtypeStruct(b.shape, F32) for b in arrays],
        input_output_aliases={a: a for a in range(total)},
        scratch_shapes=[pltpu.SemaphoreType.DMA((max(len(parts), 1),))] * 2
        + [pltpu.SemaphoreType.DMA((max(7 * n_g, 1),))] * 2 + sum_scratch,
        compiler_params=pltpu.CompilerParams(collective_id=collective_id, vmem_limit_bytes=VMEM_LIMIT),
    )(*arrays, *[array for _, own, recv in summed for array in (own, recv)])


def _adamw_math(w, g, m, v):
    nm = ADAM_B1 * m + (1.0 - ADAM_B1) * g
    nv = ADAM_B2 * v + (1.0 - ADAM_B2) * (g * g)
    m_hat = nm / (1.0 - ADAM_B1 ** ADAM_STEP)
    v_hat = nv / (1.0 - ADAM_B2 ** ADAM_STEP)
    return -ADAM_LR * (m_hat / (jnp.sqrt(v_hat) + ADAM_EPS) + ADAM_WD * w), nm, nv


def _adamw(name, w, g, m, v, rows_per_step, first=0, count=None, dests=None, deps=(), small=()):
    layers, rows, cols = w.shape
    count = layers if count is None else count
    dests = () if dests is None else tuple(dests)
    n_in = 4 + len(dests) + len(deps)
    small_shapes = _small_shapes(small[4].shape) if small else []

    def body(*refs):
        w_ref, g_ref, m_ref, v_ref = refs[:4]
        d_ref, nm_ref, nv_ref, g_out_ref = refs[n_in + len(small):n_in + len(small) + 4]
        d_ref[...], nm_ref[...], nv_ref[...] = _adamw_math(w_ref[...], g_ref[...], m_ref[...], v_ref[...])
        g_out_ref[...] = g_ref[...]
        if small:
            @pl.when((pl.program_id(0) == 0) & (pl.program_id(1) == 0))
            def _():
                _adamw_small(*refs[n_in:n_in + len(small)], *refs[n_in + len(small) + 4:])

    spec = pl.BlockSpec((1, rows_per_step, cols), lambda l, i: (first + l, i, 0))
    whole = lambda shape: pl.BlockSpec(shape, lambda l, i: (0,) * len(shape))
    shape = jax.ShapeDtypeStruct(w.shape, F32)
    return pl.pallas_call(
        body, name=name, grid=(count, rows // rows_per_step),
        in_specs=[spec] * 4 + [ANY] * (len(dests) + len(deps)) + [whole(a.shape) for a in small],
        out_specs=[spec] * 4 + [whole(s) for s in small_shapes],
        out_shape=[shape] * 4 + [jax.ShapeDtypeStruct(s, F32) for s in small_shapes],
        input_output_aliases={4 + k: k for k in range(len(dests))},
        scratch_shapes=[pltpu.VMEM((MISC_ROWS, 128), F32)] * 3 if small else [],
        compiler_params=_params(("arbitrary", "arbitrary")),
    )(w, g, m, v, *dests, *deps, *small)


RING_ROWS = 72
RING_SLOTS = 3


def _adamw_ring(name, w, g, m, v, layer, dests, small):
    _, rows, cols = w.shape
    n = rows // RING_ROWS
    small_shapes = _small_shapes(small[4].shape)

    def body(*refs):
        srcs, small_in = refs[:4], refs[8:16]
        dsts, small_out = refs[16:20], refs[20:20 + len(small_shapes)]
        in_buf, out_buf, in_sem, out_sem = refs[20 + len(small_shapes):24 + len(small_shapes)]
        small_scratch = refs[24 + len(small_shapes):]

        def chunk(ref, i):
            return ref.at[layer, pl.ds(i * RING_ROWS, RING_ROWS), :]

        def loads(i):
            return [pltpu.make_async_copy(chunk(srcs[k], i), in_buf.at[k, i % RING_SLOTS], in_sem.at[k, i % RING_SLOTS])
                    for k in range(4)]

        def stores(i):
            return [pltpu.make_async_copy(out_buf.at[k, i % RING_SLOTS], chunk(dsts[k], i), out_sem.at[k, i % RING_SLOTS])
                    for k in range(4)]

        for i in range(min(RING_SLOTS, n)):
            for copy in loads(i):
                copy.start()
        for i in range(n):
            slot = i % RING_SLOTS
            for copy in loads(i):
                copy.wait()
            if i >= RING_SLOTS:
                for copy in stores(i - RING_SLOTS):
                    copy.wait()
            g_tile = in_buf[1, slot]
            out_buf[0, slot], out_buf[1, slot], out_buf[2, slot] = _adamw_math(
                in_buf[0, slot], g_tile, in_buf[2, slot], in_buf[3, slot])
            out_buf[3, slot] = g_tile
            for copy in stores(i):
                copy.start()
            if i + RING_SLOTS < n:
                for copy in loads(i + RING_SLOTS):
                    copy.start()
            if i == 1:
                _adamw_small(*small_in, *small_out, *small_scratch)
        for i in range(max(n - RING_SLOTS, 0), n):
            for copy in stores(i):
                copy.wait()

    vmem = pl.BlockSpec(memory_space=pltpu.VMEM)
    shape = jax.ShapeDtypeStruct(w.shape, F32)
    ring = pltpu.VMEM((4, RING_SLOTS, RING_ROWS, cols), F32)
    return pl.pallas_call(
        body, name=name, in_specs=[ANY] * 8 + [vmem] * 8, out_specs=[ANY] * 4 + [vmem] * len(small_shapes),
        out_shape=[shape] * 4 + [jax.ShapeDtypeStruct(s, F32) for s in small_shapes],
        input_output_aliases={4 + k: k for k in range(4)},
        scratch_shapes=[ring, ring, pltpu.SemaphoreType.DMA((4, RING_SLOTS)), pltpu.SemaphoreType.DMA((4, RING_SLOTS))]
        + [pltpu.VMEM((MISC_ROWS, 128), F32)] * 3,
        compiler_params=pltpu.CompilerParams(vmem_limit_bytes=VMEM_LIMIT),
    )(w, g, m, v, *dests, *small)


def _pack_misc(pool_scale, sinks, norm_pre, norm_post):
    sink_rows = jnp.zeros((DEPTH, 8, 128), F32).at[:, 0, 0:N_HEADS].set(sinks).reshape(2 * 8, 128)
    return jnp.concatenate([pool_scale.reshape(8, 128), norm_pre.reshape(16, 128), norm_post.reshape(16, 128),
                            sink_rows, jnp.zeros((8, 128), F32)], axis=0)


def _adamw_small(w_ref, g_ref, m_ref, v_ref, pw_ref, pg_ref, pm_ref, pv_ref, *rest):
    outs, pool_outs, (d_ref, nm_ref, nv_ref) = rest[:17], rest[17:21], rest[21:]
    pool_outs[0][...] = pg_ref[...]
    pool_outs[1][...], pool_outs[2][...], pool_outs[3][...] = _adamw_math(
        pw_ref[...], pg_ref[...], pm_ref[...], pv_ref[...])
    d_ref[...], nm_ref[...], nv_ref[...] = _adamw_math(w_ref[...], g_ref[...], m_ref[...], v_ref[...])
    for k, src in enumerate([g_ref, d_ref, nm_ref, nv_ref]):
        scale, sinks, pre, post = outs[4 * k:4 * k + 4]
        for l in range(DEPTH):
            for j in range(4):
                scale[l:l + 1, j * 128:(j + 1) * 128] = src[MISC_SCALE + 4 * l + j:MISC_SCALE + 4 * l + j + 1, :]
            for j in range(8):
                pre[l:l + 1, j * 128:(j + 1) * 128] = src[MISC_PRE + 8 * l + j:MISC_PRE + 8 * l + j + 1, :]
                post[l:l + 1, j * 128:(j + 1) * 128] = src[MISC_POST + 8 * l + j:MISC_POST + 8 * l + j + 1, :]
            sinks[l:l + 1, :] = src[MISC_SINKS + 8 * l:MISC_SINKS + 8 * l + 1, 0:N_HEADS]
    outs[16][...] = g_ref[MISC_LOSS:MISC_LOSS + 1, 0:1]


def _small_shapes(pool_shape):
    return [(DEPTH, D_POOL), (DEPTH, N_HEADS), (DEPTH, D), (DEPTH, D)] * 4 + [(1, 1)] + [pool_shape] * 4


def kernel(x, w_in, pool_w, pool_scale, attn_sinks, w_out, norm_pre, norm_post, loss_target, m_w_in, m_pool_w, m_pool_scale, m_attn_sinks, m_w_out, m_norm_pre, m_norm_post, v_w_in, v_pool_w, v_pool_scale, v_attn_sinks, v_w_out, v_norm_pre, v_norm_post):
    cx, cy, cc = _place()
    chip_arr = jnp.reshape(2 * cx + cy, (1,)).astype(jnp.int32)
    place_arr = jnp.stack([4 * cx + 2 * cy + cc, cc]).astype(jnp.int32)
    t = lambda a: jnp.transpose(a, (0, 2, 1))
    w_in_t = t(w_in)
    xs, target = x[0], loss_target[0]
    pool_w_b = pool_w.astype(BF16)
    tables = _attention_tables()
    scale3 = pool_scale.reshape(DEPTH, 1, D_POOL)
    pre3 = norm_pre.reshape(DEPTH, 1, D)
    post3 = norm_post.reshape(DEPTH, 1, D)

    first = _gather_start_cast("gather_start_first", w_in_t, 0, ID_GATHER_FIRST)
    wi0 = _place_other_half("place_w_in0_rest", w_in_t, place_arr, 0, first[2][0])
    (wi1,) = _place_cast("place_w_in1", w_in_t, chip_arr, 288, [1], deps=(first[3],))
    wo = _place_cast("place_w_out", w_out, chip_arr, 256, [0, 1], deps=(first[3],))
    rest = _gather_start("gather_start_rest", [wi1, wo[0], wo[1]], halved=(0, 1), collective_id=ID_GATHER_REST)
    send, recv, bufs = [first[k] + rest[k] for k in range(3)]
    bufs = [wi0, *bufs[1:]]
    order = {(0, "in"): 0, (1, "in"): 1, (0, "out"): 2, (1, "out"): 3}

    saved = []
    packed = [_pack_misc(pool_scale, attn_sinks, norm_pre, norm_post),
              _pack_misc(m_pool_scale, m_attn_sinks, m_norm_pre, m_norm_post),
              _pack_misc(v_pool_scale, v_attn_sinks, v_norm_pre, v_norm_post)]
    after = (first[3], rest[3], pool_w_b, *tables, scale3, pre3, post3, *packed)
    below = None
    for l in range(DEPTH):
        k = order[l, "in"]
        halves = [_gather_wait(f"gather_wait_in{l}", bufs[k], send[k], recv[k], after, halved=True)]
        if below is not None:
            halves.append(below[1])
        w_in_l, *w_out_below = _forward_halves(f"forward_w{l}", halves, collective_id=ID_FORWARD[l])
        if below is None:
            pu, pg, q, kv, ag = _fwd_in(l, xs, pre3, w_in_l)
        else:
            saved[l - 1][9] = w_out_below[0]
            y, xs, pu, pg, q, kv, ag = _fwd_in(l, xs, pre3, w_in_l, (below[0], w_out_below[0], below[2]))
            saved[l - 1][7] = y
        cat = _fwd_mix(l, pu, pg, q, kv, ag, pool_w_b, scale3, attn_sinks, tables)
        k = order[l, "out"]
        w_out_l = _gather_wait(f"gather_wait_out{l}", bufs[k], send[k], recv[k], (cat,), halved=l + 1 < DEPTH)
        saved.append([xs, pu, pg, q, kv, ag, cat, None, w_in_l, w_out_l])
        below, after = (cat, w_out_l, post3), (w_out_l,)

    x_in, pu, pg, q, kv, ag, cat, y, w_in_l, w_out_l = saved[1]
    dcat, dw_out1, dw_out1_b, dg_post1, loss, xs = _bwd_out(1, cat, w_out_l, post3, place_arr, x=x_in, target=target)
    ex1_out = _exchange_start("exchange_start_out1", [dw_out1_b], ID_OUT1)
    dproj, dpw, dsc1, dsink1 = _bwd_mix(1, pu, pg, q, kv, ag, dcat, pool_w_b, scale3, attn_sinks, tables,
                                        deps=(ex1_out[2][0],))
    dx, dg_pre1, dw_in1, dw_in1_b = _bwd_in_dx(1, dproj, w_in_l, x_in, pre3, xs, dw_place=place_arr)
    ex1_in = _exchange_start("exchange_start_in1", [dw_in1_b], ID_IN1)

    x_in, pu, pg, q, kv, ag, cat, y, w_in_l, w_out_l = saved[0]
    dcat, dw_out0, dw_out0_b, dg_post0 = _bwd_out(0, cat, w_out_l, post3, place_arr, dxn=dx, y=y, deps=(ex1_in[2][0],))
    ex0_out = _exchange_start("exchange_start_out0", [dw_out0_b], ID_OUT0)
    dproj, dpw, dsc0, dsink0 = _bwd_mix(0, pu, pg, q, kv, ag, dcat, pool_w_b, scale3, attn_sinks, tables,
                                        deps=(ex0_out[2][0],), dpw_dest=dpw)
    flat = lambda a: a.reshape(DEPTH * 4 * 128, 128)
    dw_in0, dw_in0_b, dpw_b = _bwd_in_dw(0, dproj, x_in, pre3, place_arr, flat(dpw))
    ex0_in = _exchange_start("exchange_start_in0", [dpw_b, dw_in0_b], ID_IN0)

    grad_x, dg_pre0 = _bwd_in_dx(0, dproj, w_in_l, x_in, pre3, dx, deps=(ex0_in[2][1],))
    small = [jnp.concatenate([dsc0, dsc1, dg_pre0, dg_pre1, dg_post0, dg_post1, dsink0, dsink1, loss], axis=0)]
    ex_small = _exchange_start("exchange_start_small", small, ID_SMALL)
    (recv_out1,) = _exchange_wait("exchange_wait_out1", ex1_out, ex_small[2][0])
    (recv_in1,) = _exchange_wait("exchange_wait_in1", ex1_in, recv_out1)
    g_in, g_out = _sum_pieces("sum_pieces_1", [[(1, dw_in1, recv_in1)], [(1, dw_out1, recv_out1)]], place_arr)
    (recv_out0,) = _exchange_wait("exchange_wait_out0", ex0_out, g_out)
    (g_out,) = _sum_pieces("sum_pieces_out0", [[(0, dw_out0, recv_out0)]], place_arr, dests=[g_out])
    g_in, g_out = _share("share_a", [g_in, g_out], [(0, 1), (1, 0), (1, 1)], collective_id=ID_SHARE_A)
    m_in_t, v_in_t = t(m_w_in), t(v_w_in)
    d_out, nm_out, nv_out, grad_w_out = _adamw("adamw_w_out", w_out, g_out, m_w_out, v_w_out, 256)
    upd_in = _adamw("adamw_w_in1", w_in_t, g_in, m_in_t, v_in_t, 288, first=1, count=1, deps=(d_out,))
    (recv_pw,) = _exchange_wait("exchange_wait_pool", ex0_in, upd_in[0], which=[0])
    (g_pw,) = _sum_small("sum_pool", [flat(dpw)], [recv_pw], place_arr)

    (recv_in0,) = _exchange_wait("exchange_wait_in0", ex0_in, g_pw, which=[1])
    (recv_misc,) = _exchange_wait("exchange_wait_small", ex_small, recv_in0)
    g_in, g_pw, g_misc = _share(
        "share_b", [g_in], [(0, 0)], [g_pw, lax.empty(small[0].shape, F32)], collective_id=ID_SHARE_B,
        summed=[(("half", 0, 0), dw_in0, recv_in0), (("piece", 1), small[0], recv_misc)])
    d_in, nm_in, nv_in, grad_w_in_t, *small_out = _adamw_ring(
        "adamw_w_in0", w_in_t, g_in, m_in_t, v_in_t, 0, upd_in,
        (packed[0], g_misc, packed[1], packed[2], flat(pool_w), g_pw, flat(m_pool_w), flat(v_pool_w)))
    (g_sc, g_sk, g_pre, g_post, d_sc, d_sk, d_pre, d_post,
     m_sc, m_sk, m_pre, m_post, v_sc, v_sk, v_pre, v_post, loss_sum) = small_out[:17]
    g_pw, d_pw, m_pw, v_pw = [a.reshape(pool_w.shape) for a in small_out[17:]]
    return (loss_sum[0, 0], grad_x[None], t(grad_w_in_t), g_pw, g_sc, g_sk, grad_w_out, g_pre, g_post,
            t(d_in), d_pw, d_sc, d_sk, d_out, d_pre, d_post,
            t(nm_in), m_pw, m_sc, m_sk, nm_out, m_pre, m_post,
            t(nv_in), v_pw, v_sc, v_sk, nv_out, v_pre, v_post)
```

```python
import jax
import jax.numpy as jnp
from jax import lax
from jax.experimental import pallas as pl
from jax.experimental.pallas import tpu as pltpu

F32 = jnp.float32
BF16 = jnp.bfloat16

S = 2048
D = 1024
DEPTH = 2
D_POOL = 512
POOL_WINDOWS = (2, 4, 8, 16)
N_HEADS = 8
D_IN = 2304
N_SHARDS = 4
W_IN_SHARD = D_IN // N_SHARDS
W_OUT_SHARD = D // N_SHARDS
BLK = 128
NB = S // BLK
HALO = 16
PAD = 8
EPS = 1e-6
NEG_INF = -1e30
C_PU, C_PG, C_Q, C_K, C_V, C_AG = 0, 512, 1024, 1536, 1664, 1792

ADAM_LR = 0.001
ADAM_B1 = 0.9
ADAM_B2 = 0.999
ADAM_EPS = 1e-08
ADAM_WD = 0.01
ADAM_STEP = 10

TM = 512
VMEM_LIMIT = 56 * 1024 * 1024

NT = (((1,), (1,)), ((), ()))
TN = (((0,), (0,)), ((), ()))

MESH = pl.DeviceIdType.MESH
ANY = pl.BlockSpec(memory_space=pl.ANY)

ID_FORWARD = (0, 1)
(ID_SHARE_A, ID_SHARE_B, ID_GATHER_FIRST, ID_GATHER_REST, ID_OUT1, ID_IN1, ID_OUT0, ID_IN0, ID_SMALL) = range(2, 11)

MISC_SCALE, MISC_PRE, MISC_POST, MISC_SINKS, MISC_LOSS = 0, 8, 24, 40, 56
MISC_ROWS = 64


def _params(sem=("arbitrary",)):
    return pltpu.CompilerParams(dimension_semantics=sem, vmem_limit_bytes=VMEM_LIMIT)


def _sigmoid(v):
    return 1.0 / (1.0 + jnp.exp(-v))


def _rows8(v):
    r, c = v.shape
    return v.reshape(r // 8, 8, c).sum(axis=0)


def _layer(l, *shape):
    zeros = (0,) * len(shape)
    return pl.BlockSpec((None,) + shape, lambda i: (l,) + zeros)


def _whole(shape):
    zeros = (0,) * len(shape)
    return pl.BlockSpec(shape, lambda i: zeros, pipeline_mode=pl.Buffered(1))


def _fwd_in(l, x, g_pre, w_in_t, below=None):
    fused = below is not None

    def body(x_ref, g_ref, w_ref, *rest):
        if fused:
            cat_ref, wo_ref, gp_ref, y_ref, xn_ref = rest[:5]
            y = jnp.dot(cat_ref[...], wo_ref[...], preferred_element_type=F32)
            y_ref[...] = y
            xt = x_ref[...] + y * lax.rsqrt(jnp.mean(y * y, axis=-1, keepdims=True) + EPS) * gp_ref[...]
            xn_ref[...] = xt
        else:
            xt = x_ref[...]
        pu_ref, pg_ref, q_ref, kv_ref, ag_ref = rest[-5:]
        r = lax.rsqrt(jnp.mean(xt * xt, axis=-1, keepdims=True) + EPS)
        h = (xt * r * g_ref[...]).astype(BF16)

        def proj(lo, hi):
            return lax.dot_general(h, w_ref[lo:hi, :], NT, preferred_element_type=F32)

        pu_ref[...] = proj(C_PU, C_PG)
        pg_ref[...] = proj(C_PG, C_Q)
        q_ref[...] = proj(C_Q, C_K).astype(BF16)
        kv_ref[...] = proj(C_K, C_AG).astype(BF16)
        ag_ref[...] = proj(C_AG, D_IN)

    row = lambda w: pl.BlockSpec((TM, w), lambda i: (i, 0))
    act = jax.ShapeDtypeStruct((S, D), F32)
    return pl.pallas_call(
        body, name="fwd_out_in" if fused else "fwd_in", grid=(S // TM,),
        in_specs=[row(D), _layer(l, 1, D), _whole((D_IN, D))]
        + ([row(D), _whole((D, D)), _layer(l - 1, 1, D)] if fused else []),
        out_specs=[row(D)] * (2 * fused) + [row(512), row(512), row(512), row(256), row(512)],
        out_shape=[act] * (2 * fused)
        + [jax.ShapeDtypeStruct((S, 512), F32), jax.ShapeDtypeStruct((S, 512), F32),
           jax.ShapeDtypeStruct((S, 512), BF16), jax.ShapeDtypeStruct((S, 256), BF16),
           jax.ShapeDtypeStruct((S, 512), F32)],
        compiler_params=_params(),
    )(x, g_pre, w_in_t, *(below if fused else ()))


LOG2E = 1.4426950408889634
SCORE_SCALE = 0.125 * LOG2E


def _attention_tables():
    qi = jnp.arange(BLK)[:, None]
    kj = jnp.arange(BLK)[None, :]
    dist = ((qi - kj) % BLK).astype(F32)
    slopes = jnp.exp2(-jnp.arange(1, N_HEADS + 1, dtype=F32))
    bias = -(slopes * LOG2E)[:, None, None] * dist[None]
    first = jnp.where(kj > qi, NEG_INF, bias)
    return jnp.stack([first, bias]), (kj <= qi).astype(BF16)


def _own_block_mask():
    return lax.broadcasted_iota(jnp.int32, (BLK, BLK), 1) <= lax.broadcasted_iota(jnp.int32, (BLK, BLK), 0)


def _merge(full, own):
    return jnp.where(own, full[:, BLK:], full[:, :BLK])


def _spread(v, tri):
    own = v * tri
    return jnp.concatenate([v - own, own], axis=1)


def _head_variants(cur, prev):
    both = jnp.concatenate([prev, cur], axis=0).astype(F32)
    swapped = pltpu.roll(both, 64, axis=1)
    low = lax.broadcasted_iota(jnp.int32, both.shape, 1) < 64
    zero = jnp.zeros_like(both)
    return ((jnp.where(low, both, zero).astype(BF16), jnp.where(low, zero, swapped).astype(BF16)),
            (jnp.where(low, swapped, zero).astype(BF16), jnp.where(low, zero, both).astype(BF16)))


def _head_of(hkv, t, half):
    return hkv * 4 + 2 * t + half


def _rows(v, t):
    return v[t * BLK:(t + 1) * BLK]


def _stack_tiles(ref, hkv, offset=0):
    lo = offset + 2 * hkv * 128
    return jnp.concatenate([ref[:, lo:lo + 128], ref[:, lo + 128:lo + 256]], axis=0)


def _scores(q2, k_var, own):
    s = {}
    for hkv in range(2):
        for half in range(2):
            full = lax.dot_general(q2[hkv], k_var[hkv][half], NT, preferred_element_type=F32)
            for t in range(2):
                s[hkv, t, half] = _merge(_rows(full, t), own)
    return s


def _softmax(s, bias, sink):
    s = s * SCORE_SCALE + bias
    sink2 = sink * LOG2E
    m = jnp.maximum(jnp.max(s, axis=-1, keepdims=True), sink2)
    p = jnp.exp2(s - m)
    e_sink = jnp.exp2(sink2 - m)
    inv = 1.0 / (jnp.sum(p, axis=-1, keepdims=True) + e_sink)
    return p * inv, e_sink * inv


def _spread_pair(v, hkv, half, tri):
    return jnp.concatenate([_spread(v[hkv, t, half].astype(BF16), tri) for t in range(2)], axis=0)


POOL_ROWS = PAD + HALO + BLK


def _window_sums(src_ref, tmp_refs, trailing):
    lo, hi = (PAD, POOL_ROWS) if trailing else (0, HALO + BLK)
    cur = src_ref
    for level in range(len(POOL_WINDOWS)):
        lanes = slice(level * 128, 512)
        shift = -(1 << level) if trailing else (1 << level)
        dst = tmp_refs[level % 2]
        dst[lo:hi, lanes] = cur[lo:hi, lanes] + cur[lo + shift:hi + shift, lanes]
        cur = dst


def _pool_block(ext_ref, tmp_refs, i, g, w):
    lanes = slice(g * 128, (g + 1) * 128)
    rows = slice(PAD + HALO, POOL_ROWS)
    t = (i * BLK + lax.broadcasted_iota(jnp.int32, (BLK, 1), 0)).astype(F32)
    inv = 1.0 / jnp.minimum(t + 1.0, float(w))
    return tmp_refs[g % 2][rows, lanes] * inv - ext_ref[rows, lanes], inv


def _fwd_mix(l, pu, pg, q, kv, ag, pool_w, pool_scale, sinks, tables):
    bias, tri = tables

    def body(pu_ref, pup_ref, pg_ref, q_ref, kv_ref, kvp_ref, ag_ref, pw_ref, sc_ref, sink_ref, bias_ref, tri_ref,
             cat_ref, ext_ref, *tmp_refs):
        i = pl.program_id(0)

        @pl.when(i == 0)
        def _():
            for ref in (ext_ref, *tmp_refs):
                ref[0:PAD, :] = jnp.zeros((PAD, 512), F32)

        ext_ref[PAD:PAD + HALO, :] = jnp.where(i > 0, pup_ref[...], 0.0)
        ext_ref[PAD + HALO:POOL_ROWS, :] = pu_ref[...]
        _window_sums(ext_ref, tmp_refs, True)
        for g, w in enumerate(POOL_WINDOWS):
            lanes = slice(g * 128, (g + 1) * 128)
            pooled, _ = _pool_block(ext_ref, tmp_refs, i, g, w)
            mixed = jnp.dot(pooled.astype(BF16), pw_ref[g], preferred_element_type=F32)
            gate = pg_ref[:, lanes]
            cat_ref[:, lanes] = (mixed * sc_ref[:, lanes] * (gate * _sigmoid(gate))).astype(BF16)

        own = _own_block_mask()
        tri = tri_ref[...]
        k_var = _head_variants(kv_ref[:, 0:128], kvp_ref[:, 0:128])
        v_var = _head_variants(kv_ref[:, 128:256], kvp_ref[:, 128:256])
        s = _scores([_stack_tiles(q_ref, hkv) for hkv in range(2)], k_var, own)
        p = {}
        for (hkv, t, half), s_head in s.items():
            head = _head_of(hkv, t, half)
            p[hkv, t, half], _ = _softmax(s_head, bias_ref[head], sink_ref[l, head])
        for hkv in range(2):
            o2 = jnp.zeros((2 * BLK, 128), F32)
            for half in range(2):
                o2 = o2 + jnp.dot(_spread_pair(p, hkv, half, tri), v_var[hkv][half], preferred_element_type=F32)
            for t in range(2):
                lo = (2 * hkv + t) * 128
                gate = ag_ref[:, lo:lo + 128]
                cat_ref[:, D_POOL + lo:D_POOL + lo + 128] = (_rows(o2, t) * (gate * _sigmoid(gate))).astype(BF16)

    blk = lambda w: pl.BlockSpec((BLK, w), lambda i: (i, 0))
    prev = lambda w: pl.BlockSpec((BLK, w), lambda i: (jnp.maximum(i - 1, 0), 0))
    halo = pl.BlockSpec((HALO, 512), lambda i: (jnp.maximum(i * (BLK // HALO) - 1, 0), 0))
    return pl.pallas_call(
        body, name="fwd_mix", grid=(NB,),
        in_specs=[blk(512), halo, blk(512), blk(512), blk(256), prev(256), blk(512),
                  _layer(l, 4, 128, 128), _layer(l, 1, 512), pl.BlockSpec(memory_space=pltpu.SMEM),
                  pl.BlockSpec((None, N_HEADS, BLK, BLK), lambda i: (jnp.minimum(i, 1), 0, 0, 0)), _whole((BLK, BLK))],
        out_specs=blk(D),
        out_shape=jax.ShapeDtypeStruct((S, D), BF16),
        scratch_shapes=[pltpu.VMEM((POOL_ROWS, 512), F32)] * 3,
        compiler_params=_params(),
    )(pu, pu, pg, q, kv, kv, ag, pool_w, pool_scale, sinks, bias, tri)


def _store_lane_rows(ref, acc):
    total = jnp.sum(acc, axis=0, keepdims=True)
    for k in range(ref.shape[0]):
        ref[k:k + 1, :] = total[:, k * 128:(k + 1) * 128]


def _own_piece(dw_ref, place_ref):
    p = dw_ref.shape[0] // 8
    return dw_ref[pl.ds(pl.multiple_of(place_ref[0] * p, 8), p), :]


def _bwd_out(l, cat, w_out, g_post, place_arr, dxn=None, y=None, x=None, target=None, deps=()):
    last = target is not None
    n_steps = S // TM

    def body(a_ref, b_ref, g_ref, cat_ref, w_ref, place_ref, *rest):
        dcat_ref, own_ref, dwb_ref, dg_ref = rest[len(deps):len(deps) + 4]
        rest = rest[len(deps) + 4:]
        acc_ref, dw_ref = rest[-2:]
        step = pl.program_id(0)

        @pl.when(step == 0)
        def _():
            dw_ref[...] = jnp.zeros_like(dw_ref)
            acc_ref[...] = jnp.zeros_like(acc_ref)

        cat = cat_ref[...]
        g = g_ref[...]
        y = jnp.dot(cat, w_ref[...], preferred_element_type=F32) if last else b_ref[...]
        r = lax.rsqrt(jnp.mean(y * y, axis=-1, keepdims=True) + EPS)
        if last:
            loss_ref, dx_ref, loss_acc_ref = rest[:3]
            err = a_ref[...] + y * r * g - b_ref[...]

            @pl.when(step == 0)
            def _():
                loss_acc_ref[...] = jnp.zeros_like(loss_acc_ref)

            loss_acc_ref[...] += _rows8(err * err)
            dz = err * (1.0 / D)
            dx_ref[...] = dz
        else:
            dz = a_ref[...]
        a = dz * g
        dy = r * a - y * (r * r * r) * jnp.mean(a * y, axis=-1, keepdims=True)
        acc_ref[...] += _rows8(dz * (y * r))
        dyb = dy.astype(BF16)
        dcat_ref[...] = lax.dot_general(dyb, w_ref[...], NT, preferred_element_type=F32)
        dw_ref[...] += lax.dot_general(cat, dyb, TN, preferred_element_type=F32)

        @pl.when(step == n_steps - 1)
        def _():
            _store_lane_rows(dg_ref, acc_ref[...])
            dwb_ref[...] = dw_ref[...].astype(BF16)
            own_ref[...] = _own_piece(dw_ref, place_ref)
            if last:
                loss_ref[...] = jnp.full((8, 128), (0.5 / D) * jnp.sum(loss_acc_ref[...]), F32)

    row = lambda: pl.BlockSpec((TM, D), lambda i: (i, 0))
    full = _whole
    return pl.pallas_call(
        body, name="out_loss_bwd" if last else "bwd_out", grid=(n_steps,),
        in_specs=[row(), row(), _layer(l, 1, D), row(), full((D, D)), pl.BlockSpec(memory_space=pltpu.SMEM)]
        + [ANY] * len(deps),
        out_specs=[row(), full((D // 8, D)), full((D, D)), full((8, 128))] + ([full((8, 128)), row()] if last else []),
        out_shape=[jax.ShapeDtypeStruct((S, D), F32), jax.ShapeDtypeStruct((D // 8, D), F32),
                   jax.ShapeDtypeStruct((D, D), BF16), jax.ShapeDtypeStruct((8, 128), F32)]
        + ([jax.ShapeDtypeStruct((8, 128), F32), jax.ShapeDtypeStruct((S, D), F32)] if last else []),
        scratch_shapes=([pltpu.VMEM((8, D), F32)] if last else []) + [pltpu.VMEM((8, D), F32), pltpu.VMEM((D, D), F32)],
        compiler_params=_params(),
    )(*((x, target) if last else (dxn, y)), g_post, cat, w_out, place_arr, *deps)


def _bwd_mix(l, pu, pg, q, kv, ag, dcat, pool_w, pool_scale, sinks, tables, deps=(), dpw_dest=None):
    bias, tri = tables
    deps = tuple(deps) + (() if dpw_dest is None else (dpw_dest,))

    def body(pu_ref, pup_ref, pg_ref, q_ref, kv_ref, kvp_ref, ag_ref, dcat_ref, pw_ref, sc_ref, sink_ref, bias_ref,
             tri_ref, *rest):
        dproj_ref, dpw_ref, dsc_ref, dsink_ref, ext_ref, dext_ref, tmp_a, tmp_b, dkv_ref = rest[len(deps):]
        tmp_refs = (tmp_a, tmp_b)
        step = pl.program_id(0)
        i = NB - 1 - step

        @pl.when(step == 0)
        def _():
            dpw_ref[...] = jnp.zeros_like(dpw_ref)
            dsc_ref[...] = jnp.zeros_like(dsc_ref)
            dsink_ref[...] = jnp.zeros_like(dsink_ref)
            for ref in (ext_ref, tmp_a, tmp_b):
                ref[0:PAD, :] = jnp.zeros((PAD, 512), F32)
            dext_ref[BLK:POOL_ROWS, :] = jnp.zeros((HALO + PAD, 512), F32)
            dkv_ref[...] = jnp.zeros_like(dkv_ref)

        ext_ref[PAD:PAD + HALO, :] = jnp.where(i > 0, pup_ref[...], 0.0)
        ext_ref[PAD + HALO:POOL_ROWS, :] = pu_ref[...]
        _window_sums(ext_ref, tmp_refs, True)
        dpooled = []
        for g, w in enumerate(POOL_WINDOWS):
            lanes = slice(g * 128, (g + 1) * 128)
            pooled, inv = _pool_block(ext_ref, tmp_refs, i, g, w)
            pooled_b = pooled.astype(BF16)
            mixed = jnp.dot(pooled_b, pw_ref[g], preferred_element_type=F32)
            scale = sc_ref[:, lanes]
            gate = pg_ref[:, lanes]
            sg = _sigmoid(gate)
            dpo = dcat_ref[:, lanes]
            dproj_ref[:, C_PG + g * 128:C_PG + (g + 1) * 128] = (
                dpo * (mixed * scale) * (sg * (1.0 + gate * (1.0 - sg)))).astype(BF16)
            dms = dpo * (gate * sg)
            dsc_ref[g:g + 1, :] += jnp.sum(dms * mixed, axis=0, keepdims=True)
            dmixed = (dms * scale).astype(BF16)
            dpw_ref[g] += lax.dot_general(pooled_b, dmixed, TN, preferred_element_type=F32)
            dpooled.append(lax.dot_general(dmixed, pw_ref[g], NT, preferred_element_type=F32))
            dext_ref[0:BLK, lanes] = dpooled[g] * inv
        _window_sums(dext_ref, tmp_refs, False)
        for g in range(len(POOL_WINDOWS)):
            lanes = slice(g * 128, (g + 1) * 128)
            dproj_ref[:, C_PU + g * 128:C_PU + (g + 1) * 128] = (tmp_refs[g % 2][0:BLK, lanes] - dpooled[g]).astype(BF16)
        dext_ref[BLK:BLK + HALO, :] = dext_ref[0:HALO, :]

        own = _own_block_mask()
        tri = tri_ref[...]
        k_var = _head_variants(kv_ref[:, 0:128], kvp_ref[:, 0:128])
        v_var = _head_variants(kv_ref[:, 128:256], kvp_ref[:, 128:256])
        q2 = [_stack_tiles(q_ref, hkv) for hkv in range(2)]
        s = _scores(q2, k_var, own)
        p, p_sink = {}, {}
        for key, s_head in s.items():
            head = _head_of(*key)
            p[key], p_sink[key] = _softmax(s_head, bias_ref[head], sink_ref[l, head])

        do2, p_b, dp = [], {}, {}
        for hkv in range(2):
            gate = _stack_tiles(ag_ref, hkv)
            sg = _sigmoid(gate)
            dca = _stack_tiles(dcat_ref, hkv, D_POOL)
            do2.append((dca * (gate * sg)).astype(BF16))
            o2 = jnp.zeros((2 * BLK, 128), F32)
            for half in range(2):
                p_b[hkv, half] = _spread_pair(p, hkv, half, tri)
                o2 = o2 + jnp.dot(p_b[hkv, half], v_var[hkv][half], preferred_element_type=F32)
                full = lax.dot_general(do2[hkv], v_var[hkv][half], NT, preferred_element_type=F32)
                for t in range(2):
                    dp[hkv, t, half] = _merge(_rows(full, t), own)
            dag = dca * o2 * (sg * (1.0 + gate * (1.0 - sg)))
            for t in range(2):
                lo = C_AG + (2 * hkv + t) * 128
                dproj_ref[:, lo:lo + 128] = _rows(dag, t).astype(BF16)

        ds = {}
        for key in p:
            delta = jnp.sum(p[key] * dp[key], axis=-1, keepdims=True)
            ds[key] = p[key] * (dp[key] - delta)
            head = _head_of(*key)
            dsink_ref[0:1, :] += jnp.where(lax.broadcasted_iota(jnp.int32, (1, 128), 1) == head,
                                           -jnp.sum(p_sink[key] * delta, axis=0, keepdims=True), 0.0)

        dk_acc = [[None, None], [None, None]]
        dv_acc = [[None, None], [None, None]]
        for hkv in range(2):
            dq2 = jnp.zeros((2 * BLK, 128), F32)
            for half in range(2):
                ds_b = _spread_pair(ds, hkv, half, tri)
                dq2 = dq2 + jnp.dot(ds_b, k_var[hkv][half], preferred_element_type=F32)
                dk_acc[hkv][half] = lax.dot_general(ds_b, q2[hkv], TN, preferred_element_type=F32)
                dv_acc[hkv][half] = lax.dot_general(p_b[hkv, half], do2[hkv], TN, preferred_element_type=F32)
            for t in range(2):
                lo = C_Q + (2 * hkv + t) * 128
                dproj_ref[:, lo:lo + 128] = (_rows(dq2, t) * 0.125).astype(BF16)

        low = lax.broadcasted_iota(jnp.int32, (2 * BLK, 128), 1) < 64

        def gather_heads(acc):
            return jnp.where(low, acc[0][0] + pltpu.roll(acc[0][1], 64, axis=1),
                             pltpu.roll(acc[1][0], 64, axis=1) + acc[1][1])

        dk = gather_heads(dk_acc) * 0.125
        dv = gather_heads(dv_acc)
        dproj_ref[:, C_K:C_V] = (dk[BLK:, :] + dkv_ref[:, 0:128]).astype(BF16)
        dproj_ref[:, C_V:C_AG] = (dv[BLK:, :] + dkv_ref[:, 128:256]).astype(BF16)
        dkv_ref[:, 0:128] = dk[:BLK, :]
        dkv_ref[:, 128:256] = dv[:BLK, :]

    rev = lambda w: pl.BlockSpec((BLK, w), lambda s: (NB - 1 - s, 0))
    prev = lambda w: pl.BlockSpec((BLK, w), lambda s: (jnp.maximum(NB - 2 - s, 0), 0))
    halo = pl.BlockSpec((HALO, 512), lambda s: (jnp.maximum((NB - 1 - s) * (BLK // HALO) - 1, 0), 0))
    return pl.pallas_call(
        body, name="bwd_mix", grid=(NB,),
        in_specs=[rev(512), halo, rev(512), rev(512), rev(256), prev(256), rev(512), rev(D),
                  _layer(l, 4, 128, 128), _layer(l, 1, 512), pl.BlockSpec(memory_space=pltpu.SMEM),
                  pl.BlockSpec((None, N_HEADS, BLK, BLK), lambda s: (jnp.minimum(NB - 1 - s, 1), 0, 0, 0)),
                  _whole((BLK, BLK))] + [ANY] * len(deps),
        out_specs=[rev(D_IN), _layer(l, 4, 128, 128),
                   pl.BlockSpec((4, 128), lambda s: (0, 0)), pl.BlockSpec((8, 128), lambda s: (0, 0))],
        out_shape=[jax.ShapeDtypeStruct((S, D_IN), BF16), jax.ShapeDtypeStruct((DEPTH, 4, 128, 128), F32),
                   jax.ShapeDtypeStruct((4, 128), F32), jax.ShapeDtypeStruct((8, 128), F32)],
        input_output_aliases={} if dpw_dest is None else {12 + len(deps): 1},
        scratch_shapes=[pltpu.VMEM((POOL_ROWS, 512), F32)] * 4 + [pltpu.VMEM((BLK, 256), F32)],
        compiler_params=_params(),
    )(pu, pu, pg, q, kv, kv, ag, dcat, pool_w, pool_scale, sinks, bias, tri, *deps)


def _bwd_in_dw(l, dproj, x, g_pre, place_arr, to_bf16, deps=()):
    n_steps = S // TM

    def body(dp_ref, x_ref, g_ref, place_ref, *rest):
        f32_ref, own_ref, dwb_ref, bf16_ref, dw_ref = rest[len(deps):]
        step = pl.program_id(0)

        @pl.when(step == 0)
        def _():
            dw_ref[...] = jnp.zeros_like(dw_ref)
            bf16_ref[...] = f32_ref[...].astype(BF16)

        xt = x_ref[...]
        r = lax.rsqrt(jnp.mean(xt * xt, axis=-1, keepdims=True) + EPS)
        h = (xt * r * g_ref[...]).astype(BF16)
        dw_ref[...] += lax.dot_general(dp_ref[...], h, TN, preferred_element_type=F32)

        @pl.when(step == n_steps - 1)
        def _():
            dwb_ref[...] = dw_ref[...].astype(BF16)
            own_ref[...] = _own_piece(dw_ref, place_ref)

    row = lambda w: pl.BlockSpec((TM, w), lambda i: (i, 0))
    full = _whole
    return pl.pallas_call(
        body, name="bwd_in_dw", grid=(n_steps,),
        in_specs=[row(D_IN), row(D), _layer(l, 1, D), pl.BlockSpec(memory_space=pltpu.SMEM)] + [ANY] * len(deps)
        + [full(to_bf16.shape)],
        out_specs=[full((D_IN // 8, D)), full((D_IN, D)), full(to_bf16.shape)],
        out_shape=[jax.ShapeDtypeStruct((D_IN // 8, D), F32), jax.ShapeDtypeStruct((D_IN, D), BF16),
                   jax.ShapeDtypeStruct(to_bf16.shape, BF16)],
        scratch_shapes=[pltpu.VMEM((D_IN, D), F32)],
        compiler_params=_params(),
    )(dproj, x, g_pre, place_arr, *deps, to_bf16)


def _bwd_in_dx(l, dproj, w_in_t, x, g_pre, dres, deps=(), dw_place=None):
    n_steps = S // TM
    with_dw = dw_place is not None

    def body(dp_ref, w_ref, x_ref, g_ref, dres_ref, *rest):
        place_ref = rest[0] if with_dw else None
        rest = rest[with_dw + len(deps):]
        if with_dw:
            dx_ref, dg_ref, own_ref, dwb_ref, acc_ref, dw_ref = rest
        else:
            dx_ref, dg_ref, acc_ref = rest
        step = pl.program_id(0)

        @pl.when(step == 0)
        def _():
            acc_ref[...] = jnp.zeros_like(acc_ref)
            if with_dw:
                dw_ref[...] = jnp.zeros_like(dw_ref)

        g = g_ref[...]
        halves = [slice(k * (TM // 2), (k + 1) * (TM // 2)) for k in range(2)]
        dh = [jnp.dot(dp_ref[rows, :], w_ref[...], preferred_element_type=F32) for rows in halves]
        h = []
        for rows, dh_k in zip(halves, dh):
            xt = x_ref[rows, :]
            r = lax.rsqrt(jnp.mean(xt * xt, axis=-1, keepdims=True) + EPS)
            xn = xt * r
            acc_ref[...] += _rows8(dh_k * xn)
            a = dh_k * g
            dx_ref[rows, :] = dres_ref[rows, :] + (
                r * a - xt * (r * r * r) * jnp.mean(a * xt, axis=-1, keepdims=True))
            h.append((xn * g).astype(BF16))
        if with_dw:
            dw_ref[...] += lax.dot_general(dp_ref[...], jnp.concatenate(h, axis=0), TN, preferred_element_type=F32)

        @pl.when(step == n_steps - 1)
        def _():
            _store_lane_rows(dg_ref, acc_ref[...])
            if with_dw:
                dwb_ref[...] = dw_ref[...].astype(BF16)
                own_ref[...] = _own_piece(dw_ref, place_ref)

    row = lambda w: pl.BlockSpec((TM, w), lambda i: (i, 0))
    full = _whole
    dw_specs = [full((D_IN // 8, D)), full((D_IN, D))] if with_dw else []
    dw_shapes = [jax.ShapeDtypeStruct((D_IN // 8, D), F32), jax.ShapeDtypeStruct((D_IN, D), BF16)] if with_dw else []
    return pl.pallas_call(
        body, name="bwd_in" if with_dw else "bwd_in_dx", grid=(n_steps,),
        in_specs=[row(D_IN), full((D_IN, D)), row(D), _layer(l, 1, D), row(D)]
        + [pl.BlockSpec(memory_space=pltpu.SMEM)] * with_dw + [ANY] * len(deps),
        out_specs=[row(D), full((8, 128))] + dw_specs,
        out_shape=[jax.ShapeDtypeStruct((S, D), F32), jax.ShapeDtypeStruct((8, 128), F32)] + dw_shapes,
        scratch_shapes=[pltpu.VMEM((8, D), F32)] + [pltpu.VMEM((D_IN, D), F32)] * with_dw,
        compiler_params=_params(),
    )(dproj, w_in_t, x, g_pre, dres, *((dw_place,) if with_dw else ()), *deps)


HBM =pl.BlockSpec(memory_space=pltpu.HBM)
SEM = pl.BlockSpec(memory_space=pltpu.SEMAPHORE)
def _split_copy(collective_id=None):
    return pltpu.CompilerParams(has_side_effects=pltpu.SideEffectType.DATAFLOW_SIDE_EFFECTING,
                                collective_id=collective_id)


SPLIT_COPY = _split_copy()


def _in_hbm(a):
    return pltpu.with_memory_space_constraint(a, pltpu.HBM)

def _place():
    return lax.axis_index("x"), lax.axis_index("y"), lax.axis_index("c")


def _other_chips(x, y):
    return [(1 - x, y), (x, 1 - y), (1 - x, 1 - y)]


def _peer(x, y, c, m):
    return (x ^ (m >> 2), y ^ ((m >> 1) & 1), c ^ (m & 1))


SAME_CORE = (2, 4, 6)


def _place_cast(name, src, chip_arr, tile, layers, deps=()):
    _, n, cols = src.shape
    steps = n // tile
    k = len(layers)

    def body(chip_ref, *refs):
        for s_ref, o_ref in zip(refs[:k], refs[k + len(deps):]):
            o_ref[...] = s_ref[...].astype(BF16)

    def layer_spec(l):
        return pl.BlockSpec((None, tile, cols), lambda i, chip: (l, i, 0))

    return pl.pallas_call(
        body, name=name,
        grid_spec=pltpu.PrefetchScalarGridSpec(
            num_scalar_prefetch=1, grid=(steps,),
            in_specs=[layer_spec(l) for l in layers] + [ANY] * len(deps),
            out_specs=[pl.BlockSpec((tile, cols), lambda i, chip: (chip[0] * steps + i, 0))] * k),
        out_shape=[jax.ShapeDtypeStruct((N_SHARDS * n, cols), BF16)] * k,
        compiler_params=_params(),
    )(chip_arr, *[src] * k, *deps)


def _place_other_half(name, src, place_arr, layer, dest):
    _, n, cols = src.shape

    def body(place_ref, s_ref, dest_ref, o_ref):
        o_ref[...] = s_ref[...].astype(BF16)

    return pl.pallas_call(
        body, name=name,
        grid_spec=pltpu.PrefetchScalarGridSpec(
            num_scalar_prefetch=1, grid=(1,),
            in_specs=[pl.BlockSpec((None, n // 2, cols), lambda i, place: (layer, 1 - place[1], 0)), ANY],
            out_specs=pl.BlockSpec((n // 2, cols), lambda i, place: (place[0] + 1 - 2 * place[1], 0))),
        out_shape=jax.ShapeDtypeStruct((N_SHARDS * n, cols), BF16),
        input_output_aliases={2: 0},
        compiler_params=_params(),
    )(place_arr, src, dest)


def _chip_rows(ref, chip, half=None):
    n = ref.shape[0] // N_SHARDS
    if half is None:
        return ref.at[pl.ds(pl.multiple_of(chip * n, 16), n), :]
    return ref.at[pl.ds(pl.multiple_of(chip * n + half * (n // 2), 16), n // 2), :]


def _gather_start(name, bufs, halved, collective_id):
    n = len(bufs)

    def body(*refs):
        ins, send, recv, token = refs[:n], refs[n:2 * n], refs[2 * n:3 * n], refs[-1]
        x, y, c = _place()
        _handshake([(*chip, c) for chip in _other_chips(x, y)])
        for a, buf in enumerate(ins):
            own = _chip_rows(buf, 2 * x + y, c if a in halved else None)
            for j, chip in enumerate(_other_chips(x, y)):
                pltpu.make_async_remote_copy(src_ref=own, dst_ref=own, send_sem=send[a].at[j], recv_sem=recv[a].at[j],
                                             device_id=(*chip, c), device_id_type=MESH).start()
        token[...] = jnp.zeros_like(token)

    outs = pl.pallas_call(
        body, name=name, in_specs=[HBM] * n,
        out_specs=[SEM] * (2 * n) + [HBM] * n + [pl.BlockSpec(memory_space=pltpu.VMEM)],
        out_shape=[pltpu.SemaphoreType.DMA((3,))] * (2 * n) + [pltpu.HBM(b.shape, b.dtype) for b in bufs]
        + [jax.ShapeDtypeStruct((8, 128), F32)],
        input_output_aliases={a: 2 * n + a for a in range(n)},
        compiler_params=_split_copy(collective_id),
    )(*[_in_hbm(b) for b in bufs])
    return outs[:n], outs[n:2 * n], outs[2 * n:3 * n], outs[-1]


def _gather_start_cast(name, src, layer, collective_id):
    _, n, cols = src.shape
    half = n // 2

    def body(src_ref, send, recv, buf_ref, token, f32_ref, bf16_ref, local):
        x, y, c = _place()
        own = _chip_rows(buf_ref, 2 * x + y, c)

        def cast():
            load = pltpu.make_async_copy(src_ref.at[layer, pl.ds(pl.multiple_of(c * half, 16), half), :], f32_ref,
                                         local.at[0])
            load.start()
            load.wait()
            bf16_ref[...] = f32_ref[...].astype(BF16)
            store = pltpu.make_async_copy(bf16_ref, own, local.at[1])
            store.start()
            store.wait()

        _handshake([(*chip, c) for chip in _other_chips(x, y)], cast)
        for j, chip in enumerate(_other_chips(x, y)):
            pltpu.make_async_remote_copy(src_ref=own, dst_ref=own, send_sem=send.at[j], recv_sem=recv.at[j],
                                         device_id=(*chip, c), device_id_type=MESH).start()
        token[...] = jnp.zeros_like(token)

    outs = pl.pallas_call(
        body, name=name, in_specs=[HBM],
        out_specs=[SEM, SEM, HBM, pl.BlockSpec(memory_space=pltpu.VMEM)],
        out_shape=[pltpu.SemaphoreType.DMA((3,))] * 2 + [pltpu.HBM((N_SHARDS * n, cols), BF16),
                                                         jax.ShapeDtypeStruct((8, 128), F32)],
        scratch_shapes=[pltpu.VMEM((half, cols), F32), pltpu.VMEM((half, cols), BF16), pltpu.SemaphoreType.DMA((2,))],
        compiler_params=_split_copy(collective_id),
    )(_in_hbm(src))
    return outs[:1], outs[1:2], outs[2:3], outs[3]


def _gather_wait(name, buf, send_sem, recv_sem, after, halved=False):
    def body(buf_ref, send_ref, recv_ref, *rest):
        x, y, c = _place()
        half = c if halved else None
        own = _chip_rows(buf_ref, 2 * x + y, half)
        for j, chip in enumerate(_other_chips(x, y)):
            copy = pltpu.make_async_remote_copy(src_ref=own, dst_ref=_chip_rows(buf_ref, 2 * chip[0] + chip[1], half),
                                                send_sem=send_ref.at[j], recv_sem=recv_ref.at[j],
                                                device_id=(*chip, c), device_id_type=MESH)
            copy.wait_send()
            copy.wait_recv()

    return pl.pallas_call(
        body, name=name, in_specs=[HBM, SEM, SEM] + [ANY] * len(after), out_specs=HBM,
        out_shape=pltpu.HBM(buf.shape, buf.dtype), input_output_aliases={0: 0}, compiler_params=SPLIT_COPY,
    )(buf, send_sem, recv_sem, *after)


def _handshake(peers, meanwhile=None):
    barrier = pltpu.get_barrier_semaphore()
    for peer in peers:
        pl.semaphore_signal(barrier, inc=1, device_id=peer, device_id_type=MESH)
    if meanwhile is not None:
        meanwhile()
    pl.semaphore_wait(barrier, len(peers))


def _sibling_handshake(x, y, c):
    _handshake([(x, y, 1 - c)])


def _forward_halves(name, bufs, collective_id):
    n = len(bufs)

    def body(*refs):
        ins, outs, (send_sems, recv_sems) = refs[:n], refs[n:2 * n], refs[2 * n:]
        x, y, c = _place()
        _sibling_handshake(x, y, c)

        def copy(a, j, chip, half):
            rows = 2 * chip[0] + chip[1]
            return pltpu.make_async_remote_copy(
                src_ref=_chip_rows(ins[a], rows, half), dst_ref=_chip_rows(outs[a], rows, half),
                send_sem=send_sems.at[3 * a + j], recv_sem=recv_sems.at[3 * a + j], device_id=(x, y, 1 - c),
                device_id_type=MESH)

        copies = [(a, j, chip) for a in range(n) for j, chip in enumerate(_other_chips(x, y))]
        for a, j, chip in copies:
            copy(a, j, chip, c).start()
        for a, j, chip in copies:
            copy(a, j, chip, c).wait_send()
            copy(a, j, chip, 1 - c).wait_recv()

    return pl.pallas_call(
        body, name=name, in_specs=[ANY] * n, out_specs=[ANY] * n,
        out_shape=[jax.ShapeDtypeStruct(b.shape, b.dtype) for b in bufs],
        input_output_aliases={a: a for a in range(n)},
        scratch_shapes=[pltpu.SemaphoreType.DMA((3 * n,))] * 2,
        compiler_params=pltpu.CompilerParams(collective_id=collective_id),
    )(*bufs)


def _piece_rows(ref, k):
    p = ref.shape[0] // 8
    return ref.at[pl.ds(pl.multiple_of(k * p, 32 // jnp.dtype(ref.dtype).itemsize), p), :]


def _exchange_start(name, arrays, collective_id):
    n = len(arrays)
    zones = [lax.empty((7, a.shape[0] // 8, a.shape[1]), a.dtype) for a in arrays]

    def body(*refs):
        srcs, lands = refs[:n], refs[n:2 * n]
        send, recv = refs[2 * n:3 * n], refs[3 * n:4 * n]
        x, y, c = _place()
        _handshake([_peer(x, y, c, m) for m in range(1, 8)])
        for a, (src, land) in enumerate(zip(srcs, lands)):
            for m in range(1, 8):
                px, py, pc = _peer(x, y, c, m)
                pltpu.make_async_remote_copy(
                    src_ref=_piece_rows(src, 4 * px + 2 * py + pc), dst_ref=land.at[m - 1], send_sem=send[a].at[m - 1],
                    recv_sem=recv[a].at[m - 1], device_id=(px, py, pc), device_id_type=MESH).start()

    outs = pl.pallas_call(
        body, name=name, in_specs=[HBM] * (2 * n), out_specs=[SEM] * (2 * n) + [HBM] * (2 * n),
        out_shape=[pltpu.SemaphoreType.DMA((7,))] * (2 * n) + [pltpu.HBM(a.shape, a.dtype) for a in arrays + zones],
        input_output_aliases={a: 2 * n + a for a in range(2 * n)},
        compiler_params=_split_copy(collective_id),
    )(*[_in_hbm(a) for a in arrays + zones])
    return outs[:n], outs[n:2 * n], outs[2 * n:3 * n], outs[3 * n:4 * n]


def _exchange_wait(name, started, after, which=None):
    which = range(len(started[2])) if which is None else which
    send_sems, recv_sems, arrays, zones = [[group[k] for k in which] for group in started[:4]]
    n = len(arrays)

    def body(*refs):
        srcs, lands = refs[:n], refs[n:2 * n]
        send, recv = refs[2 * n:3 * n], refs[3 * n:4 * n]
        x, y, c = _place()
        for a, (src, land) in enumerate(zip(srcs, lands)):
            for m in range(1, 8):
                px, py, pc = _peer(x, y, c, m)
                copy = pltpu.make_async_remote_copy(
                    src_ref=_piece_rows(src, 4 * px + 2 * py + pc), dst_ref=land.at[m - 1], send_sem=send[a].at[m - 1],
                    recv_sem=recv[a].at[m - 1], device_id=(px, py, pc), device_id_type=MESH)
                copy.wait_send()
                copy.wait_recv()

    outs = pl.pallas_call(
        body, name=name, in_specs=[HBM] * (2 * n) + [SEM] * (2 * n) + [ANY], out_specs=[HBM] * (2 * n),
        out_shape=[pltpu.HBM(a.shape, a.dtype) for a in list(arrays) + list(zones)],
        input_output_aliases={a: a for a in range(2 * n)}, compiler_params=SPLIT_COPY,
    )(*arrays, *zones, *send_sems, *recv_sems, after)
    return outs[n:]


def _sum_pieces(name, weights, place_arr, dests=None):
    steps = 2
    flat = [item for items in weights for item in items]
    n = len(flat)

    def body(place_ref, *refs):
        outs = iter(refs[len(refs) - len(weights):])
        k = 0
        for items in weights:
            out_ref = next(outs)
            for layer, _, _ in items:
                total = refs[k][...]
                for m in range(7):
                    total = total + refs[n + k][m].astype(F32)
                if len(items) == DEPTH:
                    out_ref[layer] = total
                else:
                    out_ref[...] = total
                k += 1

    def out_spec(items):
        _, own, _ = items[0]
        t, cols = own.shape[0] // steps, own.shape[1]
        if len(items) == DEPTH:
            return pl.BlockSpec((DEPTH, t, cols), lambda i, place: (0, place[1] * steps + i, 0))
        layer = items[0][0]
        return pl.BlockSpec((None, t, cols), lambda i, place: (layer, place[1] * steps + i, 0))

    owns = [own for _, own, _ in flat]
    dests = [] if dests is None else list(dests)
    return pl.pallas_call(
        body, name=name,
        grid_spec=pltpu.PrefetchScalarGridSpec(
            num_scalar_prefetch=1, grid=(steps,),
            in_specs=[pl.BlockSpec((o.shape[0] // steps, o.shape[1]), lambda i, place: (i, 0)) for o in owns]
            + [pl.BlockSpec((7, o.shape[0] // steps, o.shape[1]), lambda i, place: (0, i, 0)) for o in owns]
            + [ANY] * len(dests),
            out_specs=[out_spec(items) for items in weights]),
        out_shape=[jax.ShapeDtypeStruct((DEPTH, 2 * items[0][1].shape[0], items[0][1].shape[1]), F32)
                   for items in weights],
        input_output_aliases={1 + 2 * n + k: k for k in range(len(dests))},
        compiler_params=_params(),
    )(place_arr, *owns, *[recv for _, _, recv in flat], *dests)


def _sum_small(name, partials, recvs, place_arr):
    n = len(partials)

    def body(place_ref, *refs):
        for o_ref, r_ref, out_ref in zip(refs[:n], refs[n:2 * n], refs[2 * n:]):
            total = o_ref[...]
            for m in range(7):
                total = total + r_ref[m].astype(F32)
            out_ref[...] = total

    piece = lambda a: pl.BlockSpec((a.shape[0] // 8, a.shape[1]), lambda i, place: (place[0], 0))
    return pl.pallas_call(
        body, name=name,
        grid_spec=pltpu.PrefetchScalarGridSpec(
            num_scalar_prefetch=1, grid=(1,),
            in_specs=[piece(a) for a in partials] + [pl.BlockSpec(r.shape, lambda i, place: (0, 0, 0)) for r in recvs],
            out_specs=[piece(a) for a in partials]),
        out_shape=[jax.ShapeDtypeStruct(a.shape, F32) for a in partials],
        compiler_params=_params(),
    )(place_arr, *partials, *recvs)


def _share(name, bufs, parts, gathered=(), collective_id=None, summed=()):
    n, n_g, n_s = len(bufs), len(gathered), len(summed)
    total = n + n_g
    made = [target for target, _, _ in summed]

    def body(*refs):
        ins, extra, outs = refs[:total], refs[total:total + 2 * n_s], refs[total + 2 * n_s:2 * total + 2 * n_s]
        send_sems, recv_sems, send_g, recv_g = refs[2 * total + 2 * n_s:2 * total + 2 * n_s + 4]
        scratch = refs[2 * total + 2 * n_s + 4:]
        x, y, c = _place()

        def half(ref, l, which):
            p = ref.shape[1] // 2
            return ref.at[l, pl.ds(pl.multiple_of(which * p, 8), p), :]

        def loads(j):
            local, acc, got = scratch[0], scratch[1:1 + n_s], scratch[1 + n_s:]
            own_rows = extra[2 * j] if made[j][0] == "half" else _piece_rows(extra[2 * j], 4 * x + 2 * y + c)
            return [pltpu.make_async_copy(own_rows, acc[j], local.at[2 * j]),
                    pltpu.make_async_copy(extra[2 * j + 1], got[j], local.at[2 * j + 1])]

        def sum_of(j):
            local, acc, got = scratch[0], scratch[1:1 + n_s], scratch[1 + n_s:]
            for load in loads(j):
                load.wait()
            rows = acc[j].shape[0]
            step = min(rows, 96)
            for r in range(0, rows, step):
                part = acc[j][r:r + step, :]
                for m in range(7):
                    part = part + got[j][m, r:r + step, :].astype(F32)
                acc[j][r:r + step, :] = part
            if made[j][0] == "half":
                dest = half(outs[made[j][1]], made[j][2], c)
            else:
                dest = _piece_rows(outs[n + made[j][1]], 4 * x + 2 * y + c)
            store = pltpu.make_async_copy(acc[j], dest, local.at[2 * j])
            store.start()
            store.wait()

        def early():
            for j in range(n_s):
                for load in loads(j):
                    load.start()
            for j in range(n_s):
                if made[j][0] == "piece":
                    sum_of(j)

        _handshake([_peer(x, y, c, m) for m in (SAME_CORE if gathered else ()) + (1,)], early if summed else None)

        def swap(k, which):
            a, l = parts[k]
            held = outs if ("half", a, l) in made else ins
            return pltpu.make_async_remote_copy(
                src_ref=half(held[a], l, which), dst_ref=half(outs[a], l, which), send_sem=send_sems.at[k],
                recv_sem=recv_sems.at[k], device_id=(x, y, 1 - c), device_id_type=MESH)

        def spread(a, m, sender, held, to):
            k = 4 * sender[0] + 2 * sender[1] + sender[2]
            return pltpu.make_async_remote_copy(
                src_ref=_piece_rows(held[n + a], k), dst_ref=_piece_rows(outs[n + a], k),
                send_sem=send_g.at[7 * a + m - 1], recv_sem=recv_g.at[7 * a + m - 1], device_id=to, device_id_type=MESH)

        me, sibling = (x, y, c), (x, y, 1 - c)

        def own(a, m):
            return spread(a, m, me, outs if ("piece", a) in made else ins, _peer(x, y, c, m))

        def handed_on(a, m):
            return spread(a, m + 1, _peer(x, y, c, m), outs, sibling)

        for a in range(n_g):
            for m in SAME_CORE + (1,):
                own(a, m).start()
        for j in range(n_s):
            if made[j][0] == "half":
                sum_of(j)
        for k in range(len(parts)):
            swap(k, c).start()
        for a in range(n_g):
            for m in SAME_CORE:
                spread(a, m, _peer(x, y, c, m), ins, _peer(x, y, c, m)).wait_recv()
                handed_on(a, m).start()
        for k in range(len(parts)):
            swap(k, c).wait_send()
            swap(k, 1 - c).wait_recv()
        for a in range(n_g):
            for m in SAME_CORE + (1,):
                own(a, m).wait_send()
            for m in SAME_CORE:
                handed_on(a, m).wait_send()
                spread(a, m + 1, _peer(x, y, c, m + 1), ins, sibling).wait_recv()
            spread(a, 1, sibling, ins, sibling).wait_recv()

    arrays = list(bufs) + list(gathered)
    sum_scratch = []
    if summed:
        sum_scratch = [pltpu.SemaphoreType.DMA((2 * n_s,))] + [pltpu.VMEM(recv.shape[1:], F32) for _, _, recv in summed]
        sum_scratch += [pltpu.VMEM(recv.shape, recv.dtype) for _, _, recv in summed]
    return pl.pallas_call(
        body, name=name, in_specs=[ANY] * (total + 2 * n_s), out_specs=[ANY] * total,
        out_shape=[jax.ShapeDtypeStruct(b.shape, F32) for b in arrays],
        input_output_aliases={a: a for a in range(total)},
        scratch_shapes=[pltpu.SemaphoreType.DMA((max(len(parts), 1),))] * 2
        + [pltpu.SemaphoreType.DMA((max(7 * n_g, 1),))] * 2 + sum_scratch,
        compiler_params=pltpu.CompilerParams(collective_id=collective_id, vmem_limit_bytes=VMEM_LIMIT),
    )(*arrays, *[array for _, own, recv in summed for array in (own, recv)])


def _adamw_math(w, g, m, v):
    nm = ADAM_B1 * m + (1.0 - ADAM_B1) * g
    nv = ADAM_B2 * v + (1.0 - ADAM_B2) * (g * g)
    m_hat = nm / (1.0 - ADAM_B1 ** ADAM_STEP)
    v_hat = nv / (1.0 - ADAM_B2 ** ADAM_STEP)
    return -ADAM_LR * (m_hat / (jnp.sqrt(v_hat) + ADAM_EPS) + ADAM_WD * w), nm, nv


def _adamw(name, w, g, m, v, rows_per_step, first=0, count=None, dests=None, deps=(), small=()):
    layers, rows, cols = w.shape
    count = layers if count is None else count
    dests = () if dests is None else tuple(dests)
    n_in = 4 + len(dests) + len(deps)
    small_shapes = _small_shapes(small[4].shape) if small else []

    def body(*refs):
        w_ref, g_ref, m_ref, v_ref = refs[:4]
        d_ref, nm_ref, nv_ref, g_out_ref = refs[n_in + len(small):n_in + len(small) + 4]
        d_ref[...], nm_ref[...], nv_ref[...] = _adamw_math(w_ref[...], g_ref[...], m_ref[...], v_ref[...])
        g_out_ref[...] = g_ref[...]
        if small:
            @pl.when((pl.program_id(0) == 0) & (pl.program_id(1) == 0))
            def _():
                _adamw_small(*refs[n_in:n_in + len(small)], *refs[n_in + len(small) + 4:])

    spec = pl.BlockSpec((1, rows_per_step, cols), lambda l, i: (first + l, i, 0))
    whole = lambda shape: pl.BlockSpec(shape, lambda l, i: (0,) * len(shape))
    shape = jax.ShapeDtypeStruct(w.shape, F32)
    return pl.pallas_call(
        body, name=name, grid=(count, rows // rows_per_step),
        in_specs=[spec] * 4 + [ANY] * (len(dests) + len(deps)) + [whole(a.shape) for a in small],
        out_specs=[spec] * 4 + [whole(s) for s in small_shapes],
        out_shape=[shape] * 4 + [jax.ShapeDtypeStruct(s, F32) for s in small_shapes],
        input_output_aliases={4 + k: k for k in range(len(dests))},
        scratch_shapes=[pltpu.VMEM((MISC_ROWS, 128), F32)] * 3 if small else [],
        compiler_params=_params(("arbitrary", "arbitrary")),
    )(w, g, m, v, *dests, *deps, *small)


RING_ROWS = 144
RING_SLOTS = 3


def _adamw_ring(name, w, g, m, v, layer, dests, small):
    _, rows, cols = w.shape
    n = rows // RING_ROWS
    small_shapes = _small_shapes(small[4].shape)

    def body(*refs):
        srcs, small_in = refs[:4], refs[8:16]
        dsts, small_out = refs[16:20], refs[20:20 + len(small_shapes)]
        in_buf, out_buf, in_sem, out_sem = refs[20 + len(small_shapes):24 + len(small_shapes)]
        small_scratch = refs[24 + len(small_shapes):]

        def chunk(ref, i):
            return ref.at[layer, pl.ds(i * RING_ROWS, RING_ROWS), :]

        def loads(i):
            return [pltpu.make_async_copy(chunk(srcs[k], i), in_buf.at[k, i % RING_SLOTS], in_sem.at[k, i % RING_SLOTS])
                    for k in range(4)]

        def stores(i):
            return [pltpu.make_async_copy(out_buf.at[k, i % RING_SLOTS], chunk(dsts[k], i), out_sem.at[k, i % RING_SLOTS])
                    for k in range(4)]

        for i in range(min(RING_SLOTS, n)):
            for copy in loads(i):
                copy.start()
        for i in range(n):
            slot = i % RING_SLOTS
            for copy in loads(i):
                copy.wait()
            if i >= RING_SLOTS:
                for copy in stores(i - RING_SLOTS):
                    copy.wait()
            g_tile = in_buf[1, slot]
            out_buf[0, slot], out_buf[1, slot], out_buf[2, slot] = _adamw_math(
                in_buf[0, slot], g_tile, in_buf[2, slot], in_buf[3, slot])
            out_buf[3, slot] = g_tile
            for copy in stores(i):
                copy.start()
            if i + RING_SLOTS < n:
                for copy in loads(i + RING_SLOTS):
                    copy.start()
            if i == n - 1:
                _adamw_small(*small_in, *small_out, *small_scratch)
        for i in range(max(n - RING_SLOTS, 0), n):
            for copy in stores(i):
                copy.wait()

    vmem = pl.BlockSpec(memory_space=pltpu.VMEM)
    shape = jax.ShapeDtypeStruct(w.shape, F32)
    ring = pltpu.VMEM((4, RING_SLOTS, RING_ROWS, cols), F32)
    return pl.pallas_call(
        body, name=name, in_specs=[ANY] * 8 + [vmem] * 8, out_specs=[ANY] * 4 + [vmem] * len(small_shapes),
        out_shape=[shape] * 4 + [jax.ShapeDtypeStruct(s, F32) for s in small_shapes],
        input_output_aliases={4 + k: k for k in range(4)},
        scratch_shapes=[ring, ring, pltpu.SemaphoreType.DMA((4, RING_SLOTS)), pltpu.SemaphoreType.DMA((4, RING_SLOTS))]
        + [pltpu.VMEM((MISC_ROWS, 128), F32)] * 3,
        compiler_params=pltpu.CompilerParams(vmem_limit_bytes=VMEM_LIMIT),
    )(w, g, m, v, *dests, *small)


def _pack_misc(pool_scale, sinks, norm_pre, norm_post):
    sink_rows = jnp.zeros((DEPTH, 8, 128), F32).at[:, 0, 0:N_HEADS].set(sinks).reshape(2 * 8, 128)
    return jnp.concatenate([pool_scale.reshape(8, 128), norm_pre.reshape(16, 128), norm_post.reshape(16, 128),
                            sink_rows, jnp.zeros((8, 128), F32)], axis=0)


def _adamw_small(w_ref, g_ref, m_ref, v_ref, pw_ref, pg_ref, pm_ref, pv_ref, *rest):
    outs, pool_outs, (d_ref, nm_ref, nv_ref) = rest[:17], rest[17:21], rest[21:]
    pool_outs[0][...] = pg_ref[...]
    pool_outs[1][...], pool_outs[2][...], pool_outs[3][...] = _adamw_math(
        pw_ref[...], pg_ref[...], pm_ref[...], pv_ref[...])
    d_ref[...], nm_ref[...], nv_ref[...] = _adamw_math(w_ref[...], g_ref[...], m_ref[...], v_ref[...])
    for k, src in enumerate([g_ref, d_ref, nm_ref, nv_ref]):
        scale, sinks, pre, post = outs[4 * k:4 * k + 4]
        for l in range(DEPTH):
            for j in range(4):
                scale[l:l + 1, j * 128:(j + 1) * 128] = src[MISC_SCALE + 4 * l + j:MISC_SCALE + 4 * l + j + 1, :]
            for j in range(8):
                pre[l:l + 1, j * 128:(j + 1) * 128] = src[MISC_PRE + 8 * l + j:MISC_PRE + 8 * l + j + 1, :]
                post[l:l + 1, j * 128:(j + 1) * 128] = src[MISC_POST + 8 * l + j:MISC_POST + 8 * l + j + 1, :]
            sinks[l:l + 1, :] = src[MISC_SINKS + 8 * l:MISC_SINKS + 8 * l + 1, 0:N_HEADS]
    outs[16][...] = g_ref[MISC_LOSS:MISC_LOSS + 1, 0:1]


def _small_shapes(pool_shape):
    return [(DEPTH, D_POOL), (DEPTH, N_HEADS), (DEPTH, D), (DEPTH, D)] * 4 + [(1, 1)] + [pool_shape] * 4


def kernel(x, w_in, pool_w, pool_scale, attn_sinks, w_out, norm_pre, norm_post, loss_target, m_w_in, m_pool_w, m_pool_scale, m_attn_sinks, m_w_out, m_norm_pre, m_norm_post, v_w_in, v_pool_w, v_pool_scale, v_attn_sinks, v_w_out, v_norm_pre, v_norm_post):
    cx, cy, cc = _place()
    chip_arr = jnp.reshape(2 * cx + cy, (1,)).astype(jnp.int32)
    place_arr = jnp.stack([4 * cx + 2 * cy + cc, cc]).astype(jnp.int32)
    t = lambda a: jnp.transpose(a, (0, 2, 1))
    w_in_t = t(w_in)
    xs, target = x[0], loss_target[0]
    pool_w_b = pool_w.astype(BF16)
    tables = _attention_tables()
    scale3 = pool_scale.reshape(DEPTH, 1, D_POOL)
    pre3 = norm_pre.reshape(DEPTH, 1, D)
    post3 = norm_post.reshape(DEPTH, 1, D)

    first = _gather_start_cast("gather_start_first", w_in_t, 0, ID_GATHER_FIRST)
    wi0 = _place_other_half("place_w_in0_rest", w_in_t, place_arr, 0, first[2][0])
    (wi1,) = _place_cast("place_w_in1", w_in_t, chip_arr, 288, [1], deps=(first[3],))
    wo = _place_cast("place_w_out", w_out, chip_arr, 256, [0, 1], deps=(first[3],))
    rest = _gather_start("gather_start_rest", [wi1, wo[0], wo[1]], halved=(0, 1), collective_id=ID_GATHER_REST)
    send, recv, bufs = [first[k] + rest[k] for k in range(3)]
    bufs = [wi0, *bufs[1:]]
    order = {(0, "in"): 0, (1, "in"): 1, (0, "out"): 2, (1, "out"): 3}

    saved = []
    packed = [_pack_misc(pool_scale, attn_sinks, norm_pre, norm_post),
              _pack_misc(m_pool_scale, m_attn_sinks, m_norm_pre, m_norm_post),
              _pack_misc(v_pool_scale, v_attn_sinks, v_norm_pre, v_norm_post)]
    after = (first[3], rest[3], pool_w_b, *tables, scale3, pre3, post3, *packed)
    below = None
    for l in range(DEPTH):
        k = order[l, "in"]
        halves = [_gather_wait(f"gather_wait_in{l}", bufs[k], send[k], recv[k], after, halved=True)]
        if below is not None:
            halves.append(below[1])
        w_in_l, *w_out_below = _forward_halves(f"forward_w{l}", halves, collective_id=ID_FORWARD[l])
        if below is None:
            pu, pg, q, kv, ag = _fwd_in(l, xs, pre3, w_in_l)
        else:
            saved[l - 1][9] = w_out_below[0]
            y, xs, pu, pg, q, kv, ag = _fwd_in(l, xs, pre3, w_in_l, (below[0], w_out_below[0], below[2]))
            saved[l - 1][7] = y
        cat = _fwd_mix(l, pu, pg, q, kv, ag, pool_w_b, scale3, attn_sinks, tables)
        k = order[l, "out"]
        w_out_l = _gather_wait(f"gather_wait_out{l}", bufs[k], send[k], recv[k], (cat,), halved=l + 1 < DEPTH)
        saved.append([xs, pu, pg, q, kv, ag, cat, None, w_in_l, w_out_l])
        below, after = (cat, w_out_l, post3), (w_out_l,)

    x_in, pu, pg, q, kv, ag, cat, y, w_in_l, w_out_l = saved[1]
    dcat, dw_out1, dw_out1_b, dg_post1, loss, xs = _bwd_out(1, cat, w_out_l, post3, place_arr, x=x_in, target=target)
    ex1_out = _exchange_start("exchange_start_out1", [dw_out1_b], ID_OUT1)
    dproj, dpw, dsc1, dsink1 = _bwd_mix(1, pu, pg, q, kv, ag, dcat, pool_w_b, scale3, attn_sinks, tables,
                                        deps=(ex1_out[2][0],))
    dx, dg_pre1, dw_in1, dw_in1_b = _bwd_in_dx(1, dproj, w_in_l, x_in, pre3, xs, dw_place=place_arr)
    ex1_in = _exchange_start("exchange_start_in1", [dw_in1_b], ID_IN1)

    x_in, pu, pg, q, kv, ag, cat, y, w_in_l, w_out_l = saved[0]
    dcat, dw_out0, dw_out0_b, dg_post0 = _bwd_out(0, cat, w_out_l, post3, place_arr, dxn=dx, y=y, deps=(ex1_in[2][0],))
    ex0_out = _exchange_start("exchange_start_out0", [dw_out0_b], ID_OUT0)
    dproj, dpw, dsc0, dsink0 = _bwd_mix(0, pu, pg, q, kv, ag, dcat, pool_w_b, scale3, attn_sinks, tables,
                                        deps=(ex0_out[2][0],), dpw_dest=dpw)
    flat = lambda a: a.reshape(DEPTH * 4 * 128, 128)
    dw_in0, dw_in0_b, dpw_b = _bwd_in_dw(0, dproj, x_in, pre3, place_arr, flat(dpw))
    ex0_in = _exchange_start("exchange_start_in0", [dpw_b, dw_in0_b], ID_IN0)

    grad_x, dg_pre0 = _bwd_in_dx(0, dproj, w_in_l, x_in, pre3, dx, deps=(ex0_in[2][1],))
    small = [jnp.concatenate([dsc0, dsc1, dg_pre0, dg_pre1, dg_post0, dg_post1, dsink0, dsink1, loss], axis=0)]
    ex_small = _exchange_start("exchange_start_small", small, ID_SMALL)
    (recv_out1,) = _exchange_wait("exchange_wait_out1", ex1_out, ex_small[2][0])
    (recv_in1,) = _exchange_wait("exchange_wait_in1", ex1_in, recv_out1)
    g_in, g_out = _sum_pieces("sum_pieces_1", [[(1, dw_in1, recv_in1)], [(1, dw_out1, recv_out1)]], place_arr)
    (recv_out0,) = _exchange_wait("exchange_wait_out0", ex0_out, g_out)
    (g_out,) = _sum_pieces("sum_pieces_out0", [[(0, dw_out0, recv_out0)]], place_arr, dests=[g_out])
    g_in, g_out = _share("share_a", [g_in, g_out], [(0, 1), (1, 0), (1, 1)], collective_id=ID_SHARE_A)
    m_in_t, v_in_t = t(m_w_in), t(v_w_in)
    d_out, nm_out, nv_out, grad_w_out = _adamw("adamw_w_out", w_out, g_out, m_w_out, v_w_out, 256)
    upd_in = _adamw("adamw_w_in1", w_in_t, g_in, m_in_t, v_in_t, 288, first=1, count=1, deps=(d_out,))
    (recv_pw,) = _exchange_wait("exchange_wait_pool", ex0_in, upd_in[0], which=[0])
    (g_pw,) = _sum_small("sum_pool", [flat(dpw)], [recv_pw], place_arr)

    (recv_in0,) = _exchange_wait("exchange_wait_in0", ex0_in, g_pw, which=[1])
    (recv_misc,) = _exchange_wait("exchange_wait_small", ex_small, recv_in0)
    g_in, g_pw, g_misc = _share(
        "share_b", [g_in], [(0, 0)], [g_pw, lax.empty(small[0].shape, F32)], collective_id=ID_SHARE_B,
        summed=[(("half", 0, 0), dw_in0, recv_in0), (("piece", 1), small[0], recv_misc)])
    d_in, nm_in, nv_in, grad_w_in_t, *small_out = _adamw_ring(
        "adamw_w_in0", w_in_t, g_in, m_in_t, v_in_t, 0, upd_in,
        (packed[0], g_misc, packed[1], packed[2], flat(pool_w), g_pw, flat(m_pool_w), flat(v_pool_w)))
    (g_sc, g_sk, g_pre, g_post, d_sc, d_sk, d_pre, d_post,
     m_sc, m_sk, m_pre, m_post, v_sc, v_sk, v_pre, v_post, loss_sum) = small_out[:17]
    g_pw, d_pw, m_pw, v_pw = [a.reshape(pool_w.shape) for a in small_out[17:]]
    return (loss_sum[0, 0], grad_x[None], t(grad_w_in_t), g_pw, g_sc, g_sk, grad_w_out, g_pre, g_post,
            t(d_in), d_pw, d_sc, d_sk, d_out, d_pre, d_post,
            t(nm_in), m_pw, m_sc, m_sk, nm_out, m_pre, m_post,
            t(nv_in), v_pw, v_sc, v_sk, nv_out, v_pre, v_post)
```

```python
import jax
import jax.numpy as jnp
from jax import lax
from jax.experimental import pallas as pl
from jax.experimental.pallas import tpu as pltpu

F32 = jnp.float32
BF16 = jnp.bfloat16

S = 2048
D = 1024
DEPTH = 2
D_POOL = 512
POOL_WINDOWS = (2, 4, 8, 16)
N_HEADS = 8
D_IN = 2304
N_SHARDS = 4
W_IN_SHARD = D_IN // N_SHARDS
W_OUT_SHARD = D // N_SHARDS
BLK = 128
NB = S // BLK
HALO = 16
PAD = 8
EPS = 1e-6
NEG_INF = -1e30
C_PU, C_PG, C_Q, C_K, C_V, C_AG = 0, 512, 1024, 1536, 1664, 1792

ADAM_LR = 0.001
ADAM_B1 = 0.9
ADAM_B2 = 0.999
ADAM_EPS = 1e-08
ADAM_WD = 0.01
ADAM_STEP = 10

TM = 512
VMEM_LIMIT = 56 * 1024 * 1024

NT = (((1,), (1,)), ((), ()))
TN = (((0,), (0,)), ((), ()))

MESH = pl.DeviceIdType.MESH
ANY = pl.BlockSpec(memory_space=pl.ANY)

ID_FORWARD = (0, 1)
(ID_SHARE_A, ID_SHARE_B, ID_GATHER_FIRST, ID_GATHER_REST, ID_OUT1, ID_IN1, ID_OUT0, ID_IN0, ID_SMALL) = range(2, 11)

MISC_SCALE, MISC_PRE, MISC_POST, MISC_SINKS, MISC_LOSS = 0, 8, 24, 40, 56
MISC_ROWS = 64


def _params(sem=("arbitrary",)):
    return pltpu.CompilerParams(dimension_semantics=sem, vmem_limit_bytes=VMEM_LIMIT)


def _sigmoid(v):
    return 1.0 / (1.0 + jnp.exp(-v))


def _rows8(v):
    r, c = v.shape
    return v.reshape(r // 8, 8, c).sum(axis=0)


def _layer(l, *shape):
    zeros = (0,) * len(shape)
    return pl.BlockSpec((None,) + shape, lambda i: (l,) + zeros)


def _whole(shape):
    zeros = (0,) * len(shape)
    return pl.BlockSpec(shape, lambda i: zeros, pipeline_mode=pl.Buffered(1))


def _fwd_in(l, x, g_pre, w_in_t, below=None):
    fused = below is not None

    def body(x_ref, g_ref, w_ref, *rest):
        if fused:
            cat_ref, wo_ref, gp_ref, y_ref, xn_ref = rest[:5]
            y = jnp.dot(cat_ref[...], wo_ref[...], preferred_element_type=F32)
            y_ref[...] = y
            xt = x_ref[...] + y * lax.rsqrt(jnp.mean(y * y, axis=-1, keepdims=True) + EPS) * gp_ref[...]
            xn_ref[...] = xt
        else:
            xt = x_ref[...]
        pu_ref, pg_ref, q_ref, kv_ref, ag_ref = rest[-5:]
        r = lax.rsqrt(jnp.mean(xt * xt, axis=-1, keepdims=True) + EPS)
        h = (xt * r * g_ref[...]).astype(BF16)

        def proj(lo, hi):
            return lax.dot_general(h, w_ref[lo:hi, :], NT, preferred_element_type=F32)

        pu_ref[...] = proj(C_PU, C_PG)
        pg_ref[...] = proj(C_PG, C_Q)
        q_ref[...] = proj(C_Q, C_K).astype(BF16)
        kv_ref[...] = proj(C_K, C_AG).astype(BF16)
        ag_ref[...] = proj(C_AG, D_IN)

    row = lambda w: pl.BlockSpec((TM, w), lambda i: (i, 0))
    act = jax.ShapeDtypeStruct((S, D), F32)
    return pl.pallas_call(
        body, name="fwd_out_in" if fused else "fwd_in", grid=(S // TM,),
        in_specs=[row(D), _layer(l, 1, D), _whole((D_IN, D))]
        + ([row(D), _whole((D, D)), _layer(l - 1, 1, D)] if fused else []),
        out_specs=[row(D)] * (2 * fused) + [row(512), row(512), row(512), row(256), row(512)],
        out_shape=[act] * (2 * fused)
        + [jax.ShapeDtypeStruct((S, 512), F32), jax.ShapeDtypeStruct((S, 512), F32),
           jax.ShapeDtypeStruct((S, 512), BF16), jax.ShapeDtypeStruct((S, 256), BF16),
           jax.ShapeDtypeStruct((S, 512), F32)],
        compiler_params=_params(),
    )(x, g_pre, w_in_t, *(below if fused else ()))


LOG2E = 1.4426950408889634
SCORE_SCALE = 0.125 * LOG2E


def _attention_tables():
    qi = jnp.arange(BLK)[:, None]
    kj = jnp.arange(BLK)[None, :]
    dist = ((qi - kj) % BLK).astype(F32)
    slopes = jnp.exp2(-jnp.arange(1, N_HEADS + 1, dtype=F32))
    bias = -(slopes * LOG2E)[:, None, None] * dist[None]
    first = jnp.where(kj > qi, NEG_INF, bias)
    return jnp.stack([first, bias]), (kj <= qi).astype(BF16)


def _own_block_mask():
    return lax.broadcasted_iota(jnp.int32, (BLK, BLK), 1) <= lax.broadcasted_iota(jnp.int32, (BLK, BLK), 0)


def _merge(full, own):
    return jnp.where(own, full[:, BLK:], full[:, :BLK])


def _spread(v, tri):
    own = v * tri
    return jnp.concatenate([v - own, own], axis=1)


def _head_variants(cur, prev):
    both = jnp.concatenate([prev, cur], axis=0).astype(F32)
    swapped = pltpu.roll(both, 64, axis=1)
    low = lax.broadcasted_iota(jnp.int32, both.shape, 1) < 64
    zero = jnp.zeros_like(both)
    return ((jnp.where(low, both, zero).astype(BF16), jnp.where(low, zero, swapped).astype(BF16)),
            (jnp.where(low, swapped, zero).astype(BF16), jnp.where(low, zero, both).astype(BF16)))


def _head_of(hkv, t, half):
    return hkv * 4 + 2 * t + half


def _rows(v, t):
    return v[t * BLK:(t + 1) * BLK]


def _stack_tiles(ref, hkv, offset=0):
    lo = offset + 2 * hkv * 128
    return jnp.concatenate([ref[:, lo:lo + 128], ref[:, lo + 128:lo + 256]], axis=0)


def _scores(q2, k_var, own):
    s = {}
    for hkv in range(2):
        for half in range(2):
            full = lax.dot_general(q2[hkv], k_var[hkv][half], NT, preferred_element_type=F32)
            for t in range(2):
                s[hkv, t, half] = _merge(_rows(full, t), own)
    return s


def _softmax(s, bias, sink):
    s = s * SCORE_SCALE + bias
    sink2 = sink * LOG2E
    m = jnp.maximum(jnp.max(s, axis=-1, keepdims=True), sink2)
    p = jnp.exp2(s - m)
    e_sink = jnp.exp2(sink2 - m)
    inv = 1.0 / (jnp.sum(p, axis=-1, keepdims=True) + e_sink)
    return p * inv, e_sink * inv


def _spread_pair(v, hkv, half, tri):
    return jnp.concatenate([_spread(v[hkv, t, half].astype(BF16), tri) for t in range(2)], axis=0)


POOL_ROWS = PAD + HALO + BLK


def _window_sums(src_ref, tmp_refs, trailing):
    lo, hi = (PAD, POOL_ROWS) if trailing else (0, HALO + BLK)
    cur = src_ref
    for level in range(len(POOL_WINDOWS)):
        lanes = slice(level * 128, 512)
        shift = -(1 << level) if trailing else (1 << level)
        dst = tmp_refs[level % 2]
        dst[lo:hi, lanes] = cur[lo:hi, lanes] + cur[lo + shift:hi + shift, lanes]
        cur = dst


def _pool_block(ext_ref, tmp_refs, i, g, w):
    lanes = slice(g * 128, (g + 1) * 128)
    rows = slice(PAD + HALO, POOL_ROWS)
    t = (i * BLK + lax.broadcasted_iota(jnp.int32, (BLK, 1), 0)).astype(F32)
    inv = 1.0 / jnp.minimum(t + 1.0, float(w))
    return tmp_refs[g % 2][rows, lanes] * inv - ext_ref[rows, lanes], inv


def _fwd_mix(l, pu, pg, q, kv, ag, pool_w, pool_scale, sinks, tables):
    bias, tri = tables

    def body(pu_ref, pup_ref, pg_ref, q_ref, kv_ref, kvp_ref, ag_ref, pw_ref, sc_ref, sink_ref, bias_ref, tri_ref,
             cat_ref, ext_ref, *tmp_refs):
        i = pl.program_id(0)

        @pl.when(i == 0)
        def _():
            for ref in (ext_ref, *tmp_refs):
                ref[0:PAD, :] = jnp.zeros((PAD, 512), F32)

        ext_ref[PAD:PAD + HALO, :] = jnp.where(i > 0, pup_ref[...], 0.0)
        ext_ref[PAD + HALO:POOL_ROWS, :] = pu_ref[...]
        _window_sums(ext_ref, tmp_refs, True)
        for g, w in enumerate(POOL_WINDOWS):
            lanes = slice(g * 128, (g + 1) * 128)
            pooled, _ = _pool_block(ext_ref, tmp_refs, i, g, w)
            mixed = jnp.dot(pooled.astype(BF16), pw_ref[g], preferred_element_type=F32)
            gate = pg_ref[:, lanes]
            cat_ref[:, lanes] = (mixed * sc_ref[:, lanes] * (gate * _sigmoid(gate))).astype(BF16)

        own = _own_block_mask()
        tri = tri_ref[...]
        k_var = _head_variants(kv_ref[:, 0:128], kvp_ref[:, 0:128])
        v_var = _head_variants(kv_ref[:, 128:256], kvp_ref[:, 128:256])
        s = _scores([_stack_tiles(q_ref, hkv) for hkv in range(2)], k_var, own)
        p = {}
        for (hkv, t, half), s_head in s.items():
            head = _head_of(hkv, t, half)
            p[hkv, t, half], _ = _softmax(s_head, bias_ref[head], sink_ref[l, head])
        for hkv in range(2):
            o2 = jnp.zeros((2 * BLK, 128), F32)
            for half in range(2):
                o2 = o2 + jnp.dot(_spread_pair(p, hkv, half, tri), v_var[hkv][half], preferred_element_type=F32)
            for t in range(2):
                lo = (2 * hkv + t) * 128
                gate = ag_ref[:, lo:lo + 128]
                cat_ref[:, D_POOL + lo:D_POOL + lo + 128] = (_rows(o2, t) * (gate * _sigmoid(gate))).astype(BF16)

    blk = lambda w: pl.BlockSpec((BLK, w), lambda i: (i, 0))
    prev = lambda w: pl.BlockSpec((BLK, w), lambda i: (jnp.maximum(i - 1, 0), 0))
    halo = pl.BlockSpec((HALO, 512), lambda i: (jnp.maximum(i * (BLK // HALO) - 1, 0), 0))
    return pl.pallas_call(
        body, name="fwd_mix", grid=(NB,),
        in_specs=[blk(512), halo, blk(512), blk(512), blk(256), prev(256), blk(512),
                  _layer(l, 4, 128, 128), _layer(l, 1, 512), pl.BlockSpec(memory_space=pltpu.SMEM),
                  pl.BlockSpec((None, N_HEADS, BLK, BLK), lambda i: (jnp.minimum(i, 1), 0, 0, 0)), _whole((BLK, BLK))],
        out_specs=blk(D),
        out_shape=jax.ShapeDtypeStruct((S, D), BF16),
        scratch_shapes=[pltpu.VMEM((POOL_ROWS, 512), F32)] * 3,
        compiler_params=_params(),
    )(pu, pu, pg, q, kv, kv, ag, pool_w, pool_scale, sinks, bias, tri)


def _store_lane_rows(ref, acc):
    total = jnp.sum(acc, axis=0, keepdims=True)
    for k in range(ref.shape[0]):
        ref[k:k + 1, :] = total[:, k * 128:(k + 1) * 128]


def _own_piece(dw_ref, place_ref):
    p = dw_ref.shape[0] // 8
    return dw_ref[pl.ds(pl.multiple_of(place_ref[0] * p, 8), p), :]


def _bwd_out(l, cat, w_out, g_post, place_arr, dxn=None, y=None, x=None, target=None, deps=()):
    last = target is not None
    n_steps = S // TM

    def body(a_ref, b_ref, g_ref, cat_ref, w_ref, place_ref, *rest):
        dcat_ref, own_ref, dwb_ref, dg_ref = rest[len(deps):len(deps) + 4]
        rest = rest[len(deps) + 4:]
        acc_ref, dw_ref = rest[-2:]
        step = pl.program_id(0)

        @pl.when(step == 0)
        def _():
            dw_ref[...] = jnp.zeros_like(dw_ref)
            acc_ref[...] = jnp.zeros_like(acc_ref)

        cat = cat_ref[...]
        g = g_ref[...]
        y = jnp.dot(cat, w_ref[...], preferred_element_type=F32) if last else b_ref[...]
        r = lax.rsqrt(jnp.mean(y * y, axis=-1, keepdims=True) + EPS)
        if last:
            loss_ref, dx_ref, loss_acc_ref = rest[:3]
            err = a_ref[...] + y * r * g - b_ref[...]

            @pl.when(step == 0)
            def _():
                loss_acc_ref[...] = jnp.zeros_like(loss_acc_ref)

            loss_acc_ref[...] += _rows8(err * err)
            dz = err * (1.0 / D)
            dx_ref[...] = dz
        else:
            dz = a_ref[...]
        a = dz * g
        dy = r * a - y * (r * r * r) * jnp.mean(a * y, axis=-1, keepdims=True)
        acc_ref[...] += _rows8(dz * (y * r))
        dyb = dy.astype(BF16)
        dcat_ref[...] = lax.dot_general(dyb, w_ref[...], NT, preferred_element_type=F32)
        dw_ref[...] += lax.dot_general(cat, dyb, TN, preferred_element_type=F32)

        @pl.when(step == n_steps - 1)
        def _():
            _store_lane_rows(dg_ref, acc_ref[...])
            dwb_ref[...] = dw_ref[...].astype(BF16)
            own_ref[...] = _own_piece(dw_ref, place_ref)
            if last:
                loss_ref[...] = jnp.full((8, 128), (0.5 / D) * jnp.sum(loss_acc_ref[...]), F32)

    row = lambda: pl.BlockSpec((TM, D), lambda i: (i, 0))
    full = _whole
    return pl.pallas_call(
        body, name="out_loss_bwd" if last else "bwd_out", grid=(n_steps,),
        in_specs=[row(), row(), _layer(l, 1, D), row(), full((D, D)), pl.BlockSpec(memory_space=pltpu.SMEM)]
        + [ANY] * len(deps),
        out_specs=[row(), full((D // 8, D)), full((D, D)), full((8, 128))] + ([full((8, 128)), row()] if last else []),
        out_shape=[jax.ShapeDtypeStruct((S, D), F32), jax.ShapeDtypeStruct((D // 8, D), F32),
                   jax.ShapeDtypeStruct((D, D), BF16), jax.ShapeDtypeStruct((8, 128), F32)]
        + ([jax.ShapeDtypeStruct((8, 128), F32), jax.ShapeDtypeStruct((S, D), F32)] if last else []),
        scratch_shapes=([pltpu.VMEM((8, D), F32)] if last else []) + [pltpu.VMEM((8, D), F32), pltpu.VMEM((D, D), F32)],
        compiler_params=_params(),
    )(*((x, target) if last else (dxn, y)), g_post, cat, w_out, place_arr, *deps)


def _bwd_mix(l, pu, pg, q, kv, ag, dcat, pool_w, pool_scale, sinks, tables, deps=(), dpw_dest=None):
    bias, tri = tables
    deps = tuple(deps) + (() if dpw_dest is None else (dpw_dest,))

    def body(pu_ref, pup_ref, pg_ref, q_ref, kv_ref, kvp_ref, ag_ref, dcat_ref, pw_ref, sc_ref, sink_ref, bias_ref,
             tri_ref, *rest):
        dproj_ref, dpw_ref, dsc_ref, dsink_ref, ext_ref, dext_ref, tmp_a, tmp_b, dkv_ref = rest[len(deps):]
        tmp_refs = (tmp_a, tmp_b)
        step = pl.program_id(0)
        i = NB - 1 - step

        @pl.when(step == 0)
        def _():
            dpw_ref[...] = jnp.zeros_like(dpw_ref)
            dsc_ref[...] = jnp.zeros_like(dsc_ref)
            dsink_ref[...] = jnp.zeros_like(dsink_ref)
            for ref in (ext_ref, tmp_a, tmp_b):
                ref[0:PAD, :] = jnp.zeros((PAD, 512), F32)
            dext_ref[BLK:POOL_ROWS, :] = jnp.zeros((HALO + PAD, 512), F32)
            dkv_ref[...] = jnp.zeros_like(dkv_ref)

        ext_ref[PAD:PAD + HALO, :] = jnp.where(i > 0, pup_ref[...], 0.0)
        ext_ref[PAD + HALO:POOL_ROWS, :] = pu_ref[...]
        _window_sums(ext_ref, tmp_refs, True)
        dpooled = []
        for g, w in enumerate(POOL_WINDOWS):
            lanes = slice(g * 128, (g + 1) * 128)
            pooled, inv = _pool_block(ext_ref, tmp_refs, i, g, w)
            pooled_b = pooled.astype(BF16)
            mixed = jnp.dot(pooled_b, pw_ref[g], preferred_element_type=F32)
            scale = sc_ref[:, lanes]
            gate = pg_ref[:, lanes]
            sg = _sigmoid(gate)
            dpo = dcat_ref[:, lanes]
            dproj_ref[:, C_PG + g * 128:C_PG + (g + 1) * 128] = (
                dpo * (mixed * scale) * (sg * (1.0 + gate * (1.0 - sg)))).astype(BF16)
            dms = dpo * (gate * sg)
            dsc_ref[g:g + 1, :] += jnp.sum(dms * mixed, axis=0, keepdims=True)
            dmixed = (dms * scale).astype(BF16)
            dpw_ref[g] += lax.dot_general(pooled_b, dmixed, TN, preferred_element_type=F32)
            dpooled.append(lax.dot_general(dmixed, pw_ref[g], NT, preferred_element_type=F32))
            dext_ref[0:BLK, lanes] = dpooled[g] * inv
        _window_sums(dext_ref, tmp_refs, False)
        for g in range(len(POOL_WINDOWS)):
            lanes = slice(g * 128, (g + 1) * 128)
            dproj_ref[:, C_PU + g * 128:C_PU + (g + 1) * 128] = (tmp_refs[g % 2][0:BLK, lanes] - dpooled[g]).astype(BF16)
        dext_ref[BLK:BLK + HALO, :] = dext_ref[0:HALO, :]

        own = _own_block_mask()
        tri = tri_ref[...]
        k_var = _head_variants(kv_ref[:, 0:128], kvp_ref[:, 0:128])
        v_var = _head_variants(kv_ref[:, 128:256], kvp_ref[:, 128:256])
        q2 = [_stack_tiles(q_ref, hkv) for hkv in range(2)]
        s = _scores(q2, k_var, own)
        p, p_sink = {}, {}
        for key, s_head in s.items():
            head = _head_of(*key)
            p[key], p_sink[key] = _softmax(s_head, bias_ref[head], sink_ref[l, head])

        do2, p_b, dp = [], {}, {}
        for hkv in range(2):
            gate = _stack_tiles(ag_ref, hkv)
            sg = _sigmoid(gate)
            dca = _stack_tiles(dcat_ref, hkv, D_POOL)
            do2.append((dca * (gate * sg)).astype(BF16))
            o2 = jnp.zeros((2 * BLK, 128), F32)
            for half in range(2):
                p_b[hkv, half] = _spread_pair(p, hkv, half, tri)
                o2 = o2 + jnp.dot(p_b[hkv, half], v_var[hkv][half], preferred_element_type=F32)
                full = lax.dot_general(do2[hkv], v_var[hkv][half], NT, preferred_element_type=F32)
                for t in range(2):
                    dp[hkv, t, half] = _merge(_rows(full, t), own)
            dag = dca * o2 * (sg * (1.0 + gate * (1.0 - sg)))
            for t in range(2):
                lo = C_AG + (2 * hkv + t) * 128
                dproj_ref[:, lo:lo + 128] = _rows(dag, t).astype(BF16)

        ds = {}
        for key in p:
            delta = jnp.sum(p[key] * dp[key], axis=-1, keepdims=True)
            ds[key] = p[key] * (dp[key] - delta)
            head = _head_of(*key)
            dsink_ref[0:1, :] += jnp.where(lax.broadcasted_iota(jnp.int32, (1, 128), 1) == head,
                                           -jnp.sum(p_sink[key] * delta, axis=0, keepdims=True), 0.0)

        dk_acc = [[None, None], [None, None]]
        dv_acc = [[None, None], [None, None]]
        for hkv in range(2):
            dq2 = jnp.zeros((2 * BLK, 128), F32)
            for half in range(2):
                ds_b = _spread_pair(ds, hkv, half, tri)
                dq2 = dq2 + jnp.dot(ds_b, k_var[hkv][half], preferred_element_type=F32)
                dk_acc[hkv][half] = lax.dot_general(ds_b, q2[hkv], TN, preferred_element_type=F32)
                dv_acc[hkv][half] = lax.dot_general(p_b[hkv, half], do2[hkv], TN, preferred_element_type=F32)
            for t in range(2):
                lo = C_Q + (2 * hkv + t) * 128
                dproj_ref[:, lo:lo + 128] = (_rows(dq2, t) * 0.125).astype(BF16)

        low = lax.broadcasted_iota(jnp.int32, (2 * BLK, 128), 1) < 64

        def gather_heads(acc):
            return jnp.where(low, acc[0][0] + pltpu.roll(acc[0][1], 64, axis=1),
                             pltpu.roll(acc[1][0], 64, axis=1) + acc[1][1])

        dk = gather_heads(dk_acc) * 0.125
        dv = gather_heads(dv_acc)
        dproj_ref[:, C_K:C_V] = (dk[BLK:, :] + dkv_ref[:, 0:128]).astype(BF16)
        dproj_ref[:, C_V:C_AG] = (dv[BLK:, :] + dkv_ref[:, 128:256]).astype(BF16)
        dkv_ref[:, 0:128] = dk[:BLK, :]
        dkv_ref[:, 128:256] = dv[:BLK, :]

    rev = lambda w: pl.BlockSpec((BLK, w), lambda s: (NB - 1 - s, 0))
    prev = lambda w: pl.BlockSpec((BLK, w), lambda s: (jnp.maximum(NB - 2 - s, 0), 0))
    halo = pl.BlockSpec((HALO, 512), lambda s: (jnp.maximum((NB - 1 - s) * (BLK // HALO) - 1, 0), 0))
    return pl.pallas_call(
        body, name="bwd_mix", grid=(NB,),
        in_specs=[rev(512), halo, rev(512), rev(512), rev(256), prev(256), rev(512), rev(D),
                  _layer(l, 4, 128, 128), _layer(l, 1, 512), pl.BlockSpec(memory_space=pltpu.SMEM),
                  pl.BlockSpec((None, N_HEADS, BLK, BLK), lambda s: (jnp.minimum(NB - 1 - s, 1), 0, 0, 0)),
                  _whole((BLK, BLK))] + [ANY] * len(deps),
        out_specs=[rev(D_IN), _layer(l, 4, 128, 128),
                   pl.BlockSpec((4, 128), lambda s: (0, 0)), pl.BlockSpec((8, 128), lambda s: (0, 0))],
        out_shape=[jax.ShapeDtypeStruct((S, D_IN), BF16), jax.ShapeDtypeStruct((DEPTH, 4, 128, 128), F32),
                   jax.ShapeDtypeStruct((4, 128), F32), jax.ShapeDtypeStruct((8, 128), F32)],
        input_output_aliases={} if dpw_dest is None else {12 + len(deps): 1},
        scratch_shapes=[pltpu.VMEM((POOL_ROWS, 512), F32)] * 4 + [pltpu.VMEM((BLK, 256), F32)],
        compiler_params=_params(),
    )(pu, pu, pg, q, kv, kv, ag, dcat, pool_w, pool_scale, sinks, bias, tri, *deps)


def _bwd_in_dw(l, dproj, x, g_pre, place_arr, to_bf16, deps=()):
    n_steps = S // TM

    def body(dp_ref, x_ref, g_ref, place_ref, *rest):
        f32_ref, own_ref, dwb_ref, bf16_ref, dw_ref = rest[len(deps):]
        step = pl.program_id(0)

        @pl.when(step == 0)
        def _():
            dw_ref[...] = jnp.zeros_like(dw_ref)
            bf16_ref[...] = f32_ref[...].astype(BF16)

        xt = x_ref[...]
        r = lax.rsqrt(jnp.mean(xt * xt, axis=-1, keepdims=True) + EPS)
        h = (xt * r * g_ref[...]).astype(BF16)
        dw_ref[...] += lax.dot_general(dp_ref[...], h, TN, preferred_element_type=F32)

        @pl.when(step == n_steps - 1)
        def _():
            dwb_ref[...] = dw_ref[...].astype(BF16)
            own_ref[...] = _own_piece(dw_ref, place_ref)

    row = lambda w: pl.BlockSpec((TM, w), lambda i: (i, 0))
    full = _whole
    return pl.pallas_call(
        body, name="bwd_in_dw", grid=(n_steps,),
        in_specs=[row(D_IN), row(D), _layer(l, 1, D), pl.BlockSpec(memory_space=pltpu.SMEM)] + [ANY] * len(deps)
        + [full(to_bf16.shape)],
        out_specs=[full((D_IN // 8, D)), full((D_IN, D)), full(to_bf16.shape)],
        out_shape=[jax.ShapeDtypeStruct((D_IN // 8, D), F32), jax.ShapeDtypeStruct((D_IN, D), BF16),
                   jax.ShapeDtypeStruct(to_bf16.shape, BF16)],
        scratch_shapes=[pltpu.VMEM((D_IN, D), F32)],
        compiler_params=_params(),
    )(dproj, x, g_pre, place_arr, *deps, to_bf16)


def _bwd_in_dx(l, dproj, w_in_t, x, g_pre, dres, deps=(), dw_place=None):
    n_steps = S // TM
    with_dw = dw_place is not None

    def body(dp_ref, w_ref, x_ref, g_ref, dres_ref, *rest):
        place_ref = rest[0] if with_dw else None
        rest = rest[with_dw + len(deps):]
        if with_dw:
            dx_ref, dg_ref, own_ref, dwb_ref, acc_ref, dw_ref = rest
        else:
            dx_ref, dg_ref, acc_ref = rest
        step = pl.program_id(0)

        @pl.when(step == 0)
        def _():
            acc_ref[...] = jnp.zeros_like(acc_ref)
            if with_dw:
                dw_ref[...] = jnp.zeros_like(dw_ref)

        g = g_ref[...]
        halves = [slice(k * (TM // 2), (k + 1) * (TM // 2)) for k in range(2)]
        dh = [jnp.dot(dp_ref[rows, :], w_ref[...], preferred_element_type=F32) for rows in halves]
        h = []
        for rows, dh_k in zip(halves, dh):
            xt = x_ref[rows, :]
            r = lax.rsqrt(jnp.mean(xt * xt, axis=-1, keepdims=True) + EPS)
            xn = xt * r
            acc_ref[...] += _rows8(dh_k * xn)
            a = dh_k * g
            dx_ref[rows, :] = dres_ref[rows, :] + (
                r * a - xt * (r * r * r) * jnp.mean(a * xt, axis=-1, keepdims=True))
            h.append((xn * g).astype(BF16))
        if with_dw:
            dw_ref[...] += lax.dot_general(dp_ref[...], jnp.concatenate(h, axis=0), TN, preferred_element_type=F32)

        @pl.when(step == n_steps - 1)
        def _():
            _store_lane_rows(dg_ref, acc_ref[...])
            if with_dw:
                dwb_ref[...] = dw_ref[...].astype(BF16)
                own_ref[...] = _own_piece(dw_ref, place_ref)

    row = lambda w: pl.BlockSpec((TM, w), lambda i: (i, 0))
    full = _whole
    dw_specs = [full((D_IN // 8, D)), full((D_IN, D))] if with_dw else []
    dw_shapes = [jax.ShapeDtypeStruct((D_IN // 8, D), F32), jax.ShapeDtypeStruct((D_IN, D), BF16)] if with_dw else []
    return pl.pallas_call(
        body, name="bwd_in" if with_dw else "bwd_in_dx", grid=(n_steps,),
        in_specs=[row(D_IN), full((D_IN, D)), row(D), _layer(l, 1, D), row(D)]
        + [pl.BlockSpec(memory_space=pltpu.SMEM)] * with_dw + [ANY] * len(deps),
        out_specs=[row(D), full((8, 128))] + dw_specs,
        out_shape=[jax.ShapeDtypeStruct((S, D), F32), jax.ShapeDtypeStruct((8, 128), F32)] + dw_shapes,
        scratch_shapes=[pltpu.VMEM((8, D), F32)] + [pltpu.VMEM((D_IN, D), F32)] * with_dw,
        compiler_params=_params(),
    )(dproj, w_in_t, x, g_pre, dres, *((dw_place,) if with_dw else ()), *deps)


HBM =pl.BlockSpec(memory_space=pltpu.HBM)
SEM = pl.BlockSpec(memory_space=pltpu.SEMAPHORE)
def _split_copy(collective_id=None):
    return pltpu.CompilerParams(has_side_effects=pltpu.SideEffectType.DATAFLOW_SIDE_EFFECTING,
                                collective_id=collective_id)


SPLIT_COPY = _split_copy()


def _in_hbm(a):
    return pltpu.with_memory_space_constraint(a, pltpu.HBM)

def _place():
    return lax.axis_index("x"), lax.axis_index("y"), lax.axis_index("c")


def _other_chips(x, y):
    return [(1 - x, y), (x, 1 - y), (1 - x, 1 - y)]


def _peer(x, y, c, m):
    return (x ^ (m >> 2), y ^ ((m >> 1) & 1), c ^ (m & 1))


SAME_CORE = (2, 4, 6)


def _place_cast(name, src, chip_arr, tile, layers, deps=()):
    _, n, cols = src.shape
    steps = n // tile
    k = len(layers)

    def body(chip_ref, *refs):
        for s_ref, o_ref in zip(refs[:k], refs[k + len(deps):]):
            o_ref[...] = s_ref[...].astype(BF16)

    def layer_spec(l):
        return pl.BlockSpec((None, tile, cols), lambda i, chip: (l, i, 0))

    return pl.pallas_call(
        body, name=name,
        grid_spec=pltpu.PrefetchScalarGridSpec(
            num_scalar_prefetch=1, grid=(steps,),
            in_specs=[layer_spec(l) for l in layers] + [ANY] * len(deps),
            out_specs=[pl.BlockSpec((tile, cols), lambda i, chip: (chip[0] * steps + i, 0))] * k),
        out_shape=[jax.ShapeDtypeStruct((N_SHARDS * n, cols), BF16)] * k,
        compiler_params=_params(),
    )(chip_arr, *[src] * k, *deps)


def _place_other_half(name, src, place_arr, layer, dest):
    _, n, cols = src.shape

    def body(place_ref, s_ref, dest_ref, o_ref):
        o_ref[...] = s_ref[...].astype(BF16)

    return pl.pallas_call(
        body, name=name,
        grid_spec=pltpu.PrefetchScalarGridSpec(
            num_scalar_prefetch=1, grid=(1,),
            in_specs=[pl.BlockSpec((None, n // 2, cols), lambda i, place: (layer, 1 - place[1], 0)), ANY],
            out_specs=pl.BlockSpec((n // 2, cols), lambda i, place: (place[0] + 1 - 2 * place[1], 0))),
        out_shape=jax.ShapeDtypeStruct((N_SHARDS * n, cols), BF16),
        input_output_aliases={2: 0},
        compiler_params=_params(),
    )(place_arr, src, dest)


def _chip_rows(ref, chip, half=None):
    n = ref.shape[0] // N_SHARDS
    if half is None:
        return ref.at[pl.ds(pl.multiple_of(chip * n, 16), n), :]
    return ref.at[pl.ds(pl.multiple_of(chip * n + half * (n // 2), 16), n // 2), :]


def _gather_start(name, bufs, halved, collective_id):
    n = len(bufs)

    def body(*refs):
        ins, send, recv, token = refs[:n], refs[n:2 * n], refs[2 * n:3 * n], refs[-1]
        x, y, c = _place()
        _handshake([(*chip, c) for chip in _other_chips(x, y)])
        for a, buf in enumerate(ins):
            own = _chip_rows(buf, 2 * x + y, c if a in halved else None)
            for j, chip in enumerate(_other_chips(x, y)):
                pltpu.make_async_remote_copy(src_ref=own, dst_ref=own, send_sem=send[a].at[j], recv_sem=recv[a].at[j],
                                             device_id=(*chip, c), device_id_type=MESH).start()
        token[...] = jnp.zeros_like(token)

    outs = pl.pallas_call(
        body, name=name, in_specs=[HBM] * n,
        out_specs=[SEM] * (2 * n) + [HBM] * n + [pl.BlockSpec(memory_space=pltpu.VMEM)],
        out_shape=[pltpu.SemaphoreType.DMA((3,))] * (2 * n) + [pltpu.HBM(b.shape, b.dtype) for b in bufs]
        + [jax.ShapeDtypeStruct((8, 128), F32)],
        input_output_aliases={a: 2 * n + a for a in range(n)},
        compiler_params=_split_copy(collective_id),
    )(*[_in_hbm(b) for b in bufs])
    return outs[:n], outs[n:2 * n], outs[2 * n:3 * n], outs[-1]


def _gather_start_cast(name, src, layer, collective_id):
    _, n, cols = src.shape
    half = n // 2

    def body(src_ref, send, recv, buf_ref, token, f32_ref, bf16_ref, local):
        x, y, c = _place()
        own = _chip_rows(buf_ref, 2 * x + y, c)

        def cast():
            load = pltpu.make_async_copy(src_ref.at[layer, pl.ds(pl.multiple_of(c * half, 16), half), :], f32_ref,
                                         local.at[0])
            load.start()
            load.wait()
            bf16_ref[...] = f32_ref[...].astype(BF16)
            store = pltpu.make_async_copy(bf16_ref, own, local.at[1])
            store.start()
            store.wait()

        _handshake([(*chip, c) for chip in _other_chips(x, y)], cast)
        for j, chip in enumerate(_other_chips(x, y)):
            pltpu.make_async_remote_copy(src_ref=own, dst_ref=own, send_sem=send.at[j], recv_sem=recv.at[j],
                                         device_id=(*chip, c), device_id_type=MESH).start()
        token[...] = jnp.zeros_like(token)

    outs = pl.pallas_call(
        body, name=name, in_specs=[HBM],
        out_specs=[SEM, SEM, HBM, pl.BlockSpec(memory_space=pltpu.VMEM)],
        out_shape=[pltpu.SemaphoreType.DMA((3,))] * 2 + [pltpu.HBM((N_SHARDS * n, cols), BF16),
                                                         jax.ShapeDtypeStruct((8, 128), F32)],
        scratch_shapes=[pltpu.VMEM((half, cols), F32), pltpu.VMEM((half, cols), BF16), pltpu.SemaphoreType.DMA((2,))],
        compiler_params=_split_copy(collective_id),
    )(_in_hbm(src))
    return outs[:1], outs[1:2], outs[2:3], outs[3]


def _gather_wait(name, buf, send_sem, recv_sem, after, halved=False):
    def body(buf_ref, send_ref, recv_ref, *rest):
        x, y, c = _place()
        half = c if halved else None
        own = _chip_rows(buf_ref, 2 * x + y, half)
        for j, chip in enumerate(_other_chips(x, y)):
            copy = pltpu.make_async_remote_copy(src_ref=own, dst_ref=_chip_rows(buf_ref, 2 * chip[0] + chip[1], half),
                                                send_sem=send_ref.at[j], recv_sem=recv_ref.at[j],
                                                device_id=(*chip, c), device_id_type=MESH)
            copy.wait_send()
            copy.wait_recv()

    return pl.pallas_call(
        body, name=name, in_specs=[HBM, SEM, SEM] + [ANY] * len(after), out_specs=HBM,
        out_shape=pltpu.HBM(buf.shape, buf.dtype), input_output_aliases={0: 0}, compiler_params=SPLIT_COPY,
    )(buf, send_sem, recv_sem, *after)


def _handshake(peers, meanwhile=None):
    barrier = pltpu.get_barrier_semaphore()
    for peer in peers:
        pl.semaphore_signal(barrier, inc=1, device_id=peer, device_id_type=MESH)
    if meanwhile is not None:
        meanwhile()
    pl.semaphore_wait(barrier, len(peers))


def _sibling_handshake(x, y, c):
    _handshake([(x, y, 1 - c)])


def _forward_halves(name, bufs, collective_id):
    n = len(bufs)

    def body(*refs):
        ins, outs, (send_sems, recv_sems) = refs[:n], refs[n:2 * n], refs[2 * n:]
        x, y, c = _place()
        _sibling_handshake(x, y, c)

        def copy(a, j, chip, half):
            rows = 2 * chip[0] + chip[1]
            return pltpu.make_async_remote_copy(
                src_ref=_chip_rows(ins[a], rows, half), dst_ref=_chip_rows(outs[a], rows, half),
                send_sem=send_sems.at[3 * a + j], recv_sem=recv_sems.at[3 * a + j], device_id=(x, y, 1 - c),
                device_id_type=MESH)

        copies = [(a, j, chip) for a in range(n) for j, chip in enumerate(_other_chips(x, y))]
        for a, j, chip in copies:
            copy(a, j, chip, c).start()
        for a, j, chip in copies:
            copy(a, j, chip, c).wait_send()
            copy(a, j, chip, 1 - c).wait_recv()

    return pl.pallas_call(
        body, name=name, in_specs=[ANY] * n, out_specs=[ANY] * n,
        out_shape=[jax.ShapeDtypeStruct(b.shape, b.dtype) for b in bufs],
        input_output_aliases={a: a for a in range(n)},
        scratch_shapes=[pltpu.SemaphoreType.DMA((3 * n,))] * 2,
        compiler_params=pltpu.CompilerParams(collective_id=collective_id),
    )(*bufs)


def _piece_rows(ref, k):
    p = ref.shape[0] // 8
    return ref.at[pl.ds(pl.multiple_of(k * p, 32 // jnp.dtype(ref.dtype).itemsize), p), :]


def _exchange_start(name, arrays, collective_id):
    n = len(arrays)
    zones = [lax.empty((7, a.shape[0] // 8, a.shape[1]), a.dtype) for a in arrays]

    def body(*refs):
        srcs, lands = refs[:n], refs[n:2 * n]
        send, recv = refs[2 * n:3 * n], refs[3 * n:4 * n]
        x, y, c = _place()
        _handshake([_peer(x, y, c, m) for m in range(1, 8)])
        for a, (src, land) in enumerate(zip(srcs, lands)):
            for m in range(1, 8):
                px, py, pc = _peer(x, y, c, m)
                pltpu.make_async_remote_copy(
                    src_ref=_piece_rows(src, 4 * px + 2 * py + pc), dst_ref=land.at[m - 1], send_sem=send[a].at[m - 1],
                    recv_sem=recv[a].at[m - 1], device_id=(px, py, pc), device_id_type=MESH).start()

    outs = pl.pallas_call(
        body, name=name, in_specs=[HBM] * (2 * n), out_specs=[SEM] * (2 * n) + [HBM] * (2 * n),
        out_shape=[pltpu.SemaphoreType.DMA((7,))] * (2 * n) + [pltpu.HBM(a.shape, a.dtype) for a in arrays + zones],
        input_output_aliases={a: 2 * n + a for a in range(2 * n)},
        compiler_params=_split_copy(collective_id),
    )(*[_in_hbm(a) for a in arrays + zones])
    return outs[:n], outs[n:2 * n], outs[2 * n:3 * n], outs[3 * n:4 * n]


def _exchange_wait(name, started, after, which=None):
    which = range(len(started[2])) if which is None else which
    send_sems, recv_sems, arrays, zones = [[group[k] for k in which] for group in started[:4]]
    n = len(arrays)

    def body(*refs):
        srcs, lands = refs[:n], refs[n:2 * n]
        send, recv = refs[2 * n:3 * n], refs[3 * n:4 * n]
        x, y, c = _place()
        for a, (src, land) in enumerate(zip(srcs, lands)):
            for m in range(1, 8):
                px, py, pc = _peer(x, y, c, m)
                copy = pltpu.make_async_remote_copy(
                    src_ref=_piece_rows(src, 4 * px + 2 * py + pc), dst_ref=land.at[m - 1], send_sem=send[a].at[m - 1],
                    recv_sem=recv[a].at[m - 1], device_id=(px, py, pc), device_id_type=MESH)
                copy.wait_send()
                copy.wait_recv()

    outs = pl.pallas_call(
        body, name=name, in_specs=[HBM] * (2 * n) + [SEM] * (2 * n) + [ANY], out_specs=[HBM] * (2 * n),
        out_shape=[pltpu.HBM(a.shape, a.dtype) for a in list(arrays) + list(zones)],
        input_output_aliases={a: a for a in range(2 * n)}, compiler_params=SPLIT_COPY,
    )(*arrays, *zones, *send_sems, *recv_sems, after)
    return outs[n:]


def _sum_pieces(name, weights, place_arr, dests=None):
    steps = 2
    flat = [item for items in weights for item in items]
    n = len(flat)

    def body(place_ref, *refs):
        outs = iter(refs[len(refs) - len(weights):])
        k = 0
        for items in weights:
            out_ref = next(outs)
            for layer, _, _ in items:
                total = refs[k][...]
                for m in range(7):
                    total = total + refs[n + k][m].astype(F32)
                if len(items) == DEPTH:
                    out_ref[layer] = total
                else:
                    out_ref[...] = total
                k += 1

    def out_spec(items):
        _, own, _ = items[0]
        t, cols = own.shape[0] // steps, own.shape[1]
        if len(items) == DEPTH:
            return pl.BlockSpec((DEPTH, t, cols), lambda i, place: (0, place[1] * steps + i, 0))
        layer = items[0][0]
        return pl.BlockSpec((None, t, cols), lambda i, place: (layer, place[1] * steps + i, 0))

    owns = [own for _, own, _ in flat]
    dests = [] if dests is None else list(dests)
    return pl.pallas_call(
        body, name=name,
        grid_spec=pltpu.PrefetchScalarGridSpec(
            num_scalar_prefetch=1, grid=(steps,),
            in_specs=[pl.BlockSpec((o.shape[0] // steps, o.shape[1]), lambda i, place: (i, 0)) for o in owns]
            + [pl.BlockSpec((7, o.shape[0] // steps, o.shape[1]), lambda i, place: (0, i, 0)) for o in owns]
            + [ANY] * len(dests),
            out_specs=[out_spec(items) for items in weights]),
        out_shape=[jax.ShapeDtypeStruct((DEPTH, 2 * items[0][1].shape[0], items[0][1].shape[1]), F32)
                   for items in weights],
        input_output_aliases={1 + 2 * n + k: k for k in range(len(dests))},
        compiler_params=_params(),
    )(place_arr, *owns, *[recv for _, _, recv in flat], *dests)


def _sum_small(name, partials, recvs, place_arr):
    n = len(partials)

    def body(place_ref, *refs):
        for o_ref, r_ref, out_ref in zip(refs[:n], refs[n:2 * n], refs[2 * n:]):
            total = o_ref[...]
            for m in range(7):
                total = total + r_ref[m].astype(F32)
            out_ref[...] = total

    piece = lambda a: pl.BlockSpec((a.shape[0] // 8, a.shape[1]), lambda i, place: (place[0], 0))
    return pl.pallas_call(
        body, name=name,
        grid_spec=pltpu.PrefetchScalarGridSpec(
            num_scalar_prefetch=1, grid=(1,),
            in_specs=[piece(a) for a in partials] + [pl.BlockSpec(r.shape, lambda i, place: (0, 0, 0)) for r in recvs],
            out_specs=[piece(a) for a in partials]),
        out_shape=[jax.ShapeDtypeStruct(a.shape, F32) for a in partials],
        compiler_params=_params(),
    )(place_arr, *partials, *recvs)


def _share(name, bufs, parts, gathered=(), collective_id=None, summed=()):
    n, n_g, n_s = len(bufs), len(gathered), len(summed)
    total = n + n_g
    made = [target for target, _, _ in summed]

    def body(*refs):
        ins, extra, outs = refs[:total], refs[total:total + 2 * n_s], refs[total + 2 * n_s:2 * total + 2 * n_s]
        send_sems, recv_sems, send_g, recv_g = refs[2 * total + 2 * n_s:2 * total + 2 * n_s + 4]
        scratch = refs[2 * total + 2 * n_s + 4:]
        x, y, c = _place()

        def half(ref, l, which):
            p = ref.shape[1] // 2
            return ref.at[l, pl.ds(pl.multiple_of(which * p, 8), p), :]

        def loads(j):
            local, acc, got = scratch[0], scratch[1:1 + n_s], scratch[1 + n_s:]
            own_rows = extra[2 * j] if made[j][0] == "half" else _piece_rows(extra[2 * j], 4 * x + 2 * y + c)
            return [pltpu.make_async_copy(own_rows, acc[j], local.at[2 * j]),
                    pltpu.make_async_copy(extra[2 * j + 1], got[j], local.at[2 * j + 1])]

        def sum_of(j):
            local, acc, got = scratch[0], scratch[1:1 + n_s], scratch[1 + n_s:]
            for load in loads(j):
                load.wait()
            rows = acc[j].shape[0]
            step = rows if rows <= 128 else 96
            for r in range(0, rows, step):
                part = acc[j][r:r + step, :]
                for m in range(7):
                    part = part + got[j][m, r:r + step, :].astype(F32)
                acc[j][r:r + step, :] = part
            if made[j][0] == "half":
                dest = half(outs[made[j][1]], made[j][2], c)
            else:
                dest = _piece_rows(outs[n + made[j][1]], 4 * x + 2 * y + c)
            store = pltpu.make_async_copy(acc[j], dest, local.at[2 * j])
            store.start()
            store.wait()

        def early():
            for j in range(n_s):
                for load in loads(j):
                    load.start()
            for j in range(n_s):
                if made[j][0] == "piece":
                    sum_of(j)

        _handshake([_peer(x, y, c, m) for m in (SAME_CORE if gathered else ()) + (1,)], early if summed else None)

        def swap(k, which):
            a, l = parts[k]
            held = outs if ("half", a, l) in made else ins
            return pltpu.make_async_remote_copy(
                src_ref=half(held[a], l, which), dst_ref=half(outs[a], l, which), send_sem=send_sems.at[k],
                recv_sem=recv_sems.at[k], device_id=(x, y, 1 - c), device_id_type=MESH)

        def spread(a, m, sender, held, to):
            k = 4 * sender[0] + 2 * sender[1] + sender[2]
            return pltpu.make_async_remote_copy(
                src_ref=_piece_rows(held[n + a], k), dst_ref=_piece_rows(outs[n + a], k),
                send_sem=send_g.at[7 * a + m - 1], recv_sem=recv_g.at[7 * a + m - 1], device_id=to, device_id_type=MESH)

        me, sibling = (x, y, c), (x, y, 1 - c)

        def own(a, m):
            return spread(a, m, me, outs if ("piece", a) in made else ins, _peer(x, y, c, m))

        def handed_on(a, m):
            return spread(a, m + 1, _peer(x, y, c, m), outs, sibling)

        for a in range(n_g):
            for m in SAME_CORE + (1,):
                own(a, m).start()
        for j in range(n_s):
            if made[j][0] == "half":
                sum_of(j)
        for k in range(len(parts)):
            swap(k, c).start()
        for a in range(n_g):
            for m in SAME_CORE:
                spread(a, m, _peer(x, y, c, m), ins, _peer(x, y, c, m)).wait_recv()
                handed_on(a, m).start()
        for k in range(len(parts)):
            swap(k, c).wait_send()
            swap(k, 1 - c).wait_recv()
        for a in range(n_g):
            for m in SAME_CORE + (1,):
                own(a, m).wait_send()
            for m in SAME_CORE:
                handed_on(a, m).wait_send()
                spread(a, m + 1, _peer(x, y, c, m + 1), ins, sibling).wait_recv()
            spread(a, 1, sibling, ins, sibling).wait_recv()

    arrays = list(bufs) + list(gathered)
    sum_scratch = []
    if summed:
        sum_scratch = [pltpu.SemaphoreType.DMA((2 * n_s,))] + [pltpu.VMEM(recv.shape[1:], F32) for _, _, recv in summed]
        sum_scratch += [pltpu.VMEM(recv.shape, recv.dtype) for _, _, recv in summed]
    return pl.pallas_call(
        body, name=name, in_specs=[ANY] * (total + 2 * n_s), out_specs=[ANY] * total,
        out_shape=[jax.ShapeDtypeStruct(b.shape, F32) for b in arrays],
        input_output_aliases={a: a for a in range(total)},
        scratch_shapes=[pltpu.SemaphoreType.DMA((max(len(parts), 1),))] * 2
        + [pltpu.SemaphoreType.DMA((max(7 * n_g, 1),))] * 2 + sum_scratch,
        compiler_params=pltpu.CompilerParams(collective_id=collective_id, vmem_limit_bytes=VMEM_LIMIT),
    )(*arrays, *[array for _, own, recv in summed for array in (own, recv)])


def _adamw_math(w, g, m, v):
    nm = ADAM_B1 * m + (1.0 - ADAM_B1) * g
    nv = ADAM_B2 * v + (1.0 - ADAM_B2) * (g * g)
    m_hat = nm / (1.0 - ADAM_B1 ** ADAM_STEP)
    v_hat = nv / (1.0 - ADAM_B2 ** ADAM_STEP)
    return -ADAM_LR * (m_hat / (jnp.sqrt(v_hat) + ADAM_EPS) + ADAM_WD * w), nm, nv


def _adamw(name, w, g, m, v, rows_per_step, first=0, count=None, dests=None, deps=(), small=()):
    layers, rows, cols = w.shape
    count = layers if count is None else count
    dests = () if dests is None else tuple(dests)
    n_in = 4 + len(dests) + len(deps)
    small_shapes = _small_shapes(small[4].shape) if small else []

    def body(*refs):
        w_ref, g_ref, m_ref, v_ref = refs[:4]
        d_ref, nm_ref, nv_ref, g_out_ref = refs[n_in + len(small):n_in + len(small) + 4]
        d_ref[...], nm_ref[...], nv_ref[...] = _adamw_math(w_ref[...], g_ref[...], m_ref[...], v_ref[...])
        g_out_ref[...] = g_ref[...]
        if small:
            @pl.when((pl.program_id(0) == 0) & (pl.program_id(1) == 0))
            def _():
                _adamw_small(*refs[n_in:n_in + len(small)], *refs[n_in + len(small) + 4:])

    spec = pl.BlockSpec((1, rows_per_step, cols), lambda l, i: (first + l, i, 0))
    whole = lambda shape: pl.BlockSpec(shape, lambda l, i: (0,) * len(shape))
    shape = jax.ShapeDtypeStruct(w.shape, F32)
    return pl.pallas_call(
        body, name=name, grid=(count, rows // rows_per_step),
        in_specs=[spec] * 4 + [ANY] * (len(dests) + len(deps)) + [whole(a.shape) for a in small],
        out_specs=[spec] * 4 + [whole(s) for s in small_shapes],
        out_shape=[shape] * 4 + [jax.ShapeDtypeStruct(s, F32) for s in small_shapes],
        input_output_aliases={4 + k: k for k in range(len(dests))},
        scratch_shapes=[pltpu.VMEM((MISC_ROWS, 128), F32)] * 3 if small else [],
        compiler_params=_params(("arbitrary", "arbitrary")),
    )(w, g, m, v, *dests, *deps, *small)


def _pack_misc(pool_scale, sinks, norm_pre, norm_post):
    sink_rows = jnp.zeros((DEPTH, 8, 128), F32).at[:, 0, 0:N_HEADS].set(sinks).reshape(2 * 8, 128)
    return jnp.concatenate([pool_scale.reshape(8, 128), norm_pre.reshape(16, 128), norm_post.reshape(16, 128),
                            sink_rows, jnp.zeros((8, 128), F32)], axis=0)


def _adamw_small(w_ref, g_ref, m_ref, v_ref, pw_ref, pg_ref, pm_ref, pv_ref, *rest):
    outs, pool_outs, (d_ref, nm_ref, nv_ref) = rest[:17], rest[17:21], rest[21:]
    pool_outs[0][...] = pg_ref[...]
    pool_outs[1][...], pool_outs[2][...], pool_outs[3][...] = _adamw_math(
        pw_ref[...], pg_ref[...], pm_ref[...], pv_ref[...])
    d_ref[...], nm_ref[...], nv_ref[...] = _adamw_math(w_ref[...], g_ref[...], m_ref[...], v_ref[...])
    for k, src in enumerate([g_ref, d_ref, nm_ref, nv_ref]):
        scale, sinks, pre, post = outs[4 * k:4 * k + 4]
        for l in range(DEPTH):
            for j in range(4):
                scale[l:l + 1, j * 128:(j + 1) * 128] = src[MISC_SCALE + 4 * l + j:MISC_SCALE + 4 * l + j + 1, :]
            for j in range(8):
                pre[l:l + 1, j * 128:(j + 1) * 128] = src[MISC_PRE + 8 * l + j:MISC_PRE + 8 * l + j + 1, :]
                post[l:l + 1, j * 128:(j + 1) * 128] = src[MISC_POST + 8 * l + j:MISC_POST + 8 * l + j + 1, :]
            sinks[l:l + 1, :] = src[MISC_SINKS + 8 * l:MISC_SINKS + 8 * l + 1, 0:N_HEADS]
    outs[16][...] = g_ref[MISC_LOSS:MISC_LOSS + 1, 0:1]


def _small_shapes(pool_shape):
    return [(DEPTH, D_POOL), (DEPTH, N_HEADS), (DEPTH, D), (DEPTH, D)] * 4 + [(1, 1)] + [pool_shape] * 4


def kernel(x, w_in, pool_w, pool_scale, attn_sinks, w_out, norm_pre, norm_post, loss_target, m_w_in, m_pool_w, m_pool_scale, m_attn_sinks, m_w_out, m_norm_pre, m_norm_post, v_w_in, v_pool_w, v_pool_scale, v_attn_sinks, v_w_out, v_norm_pre, v_norm_post):
    cx, cy, cc = _place()
    chip_arr = jnp.reshape(2 * cx + cy, (1,)).astype(jnp.int32)
    place_arr = jnp.stack([4 * cx + 2 * cy + cc, cc]).astype(jnp.int32)
    t = lambda a: jnp.transpose(a, (0, 2, 1))
    w_in_t = t(w_in)
    xs, target = x[0], loss_target[0]
    pool_w_b = pool_w.astype(BF16)
    tables = _attention_tables()
    scale3 = pool_scale.reshape(DEPTH, 1, D_POOL)
    pre3 = norm_pre.reshape(DEPTH, 1, D)
    post3 = norm_post.reshape(DEPTH, 1, D)

    first = _gather_start_cast("gather_start_first", w_in_t, 0, ID_GATHER_FIRST)
    wi0 = _place_other_half("place_w_in0_rest", w_in_t, place_arr, 0, first[2][0])
    (wi1,) = _place_cast("place_w_in1", w_in_t, chip_arr, 288, [1], deps=(first[3],))
    wo = _place_cast("place_w_out", w_out, chip_arr, 256, [0, 1], deps=(first[3],))
    rest = _gather_start("gather_start_rest", [wi1, wo[0], wo[1]], halved=(0, 1), collective_id=ID_GATHER_REST)
    send, recv, bufs = [first[k] + rest[k] for k in range(3)]
    bufs = [wi0, *bufs[1:]]
    order = {(0, "in"): 0, (1, "in"): 1, (0, "out"): 2, (1, "out"): 3}

    saved = []
    packed = [_pack_misc(pool_scale, attn_sinks, norm_pre, norm_post),
              _pack_misc(m_pool_scale, m_attn_sinks, m_norm_pre, m_norm_post),
              _pack_misc(v_pool_scale, v_attn_sinks, v_norm_pre, v_norm_post)]
    after = (first[3], rest[3], pool_w_b, *tables, scale3, pre3, post3, *packed)
    below = None
    for l in range(DEPTH):
        k = order[l, "in"]
        halves = [_gather_wait(f"gather_wait_in{l}", bufs[k], send[k], recv[k], after, halved=True)]
        if below is not None:
            halves.append(below[1])
        w_in_l, *w_out_below = _forward_halves(f"forward_w{l}", halves, collective_id=ID_FORWARD[l])
        if below is None:
            pu, pg, q, kv, ag = _fwd_in(l, xs, pre3, w_in_l)
        else:
            saved[l - 1][9] = w_out_below[0]
            y, xs, pu, pg, q, kv, ag = _fwd_in(l, xs, pre3, w_in_l, (below[0], w_out_below[0], below[2]))
            saved[l - 1][7] = y
        cat = _fwd_mix(l, pu, pg, q, kv, ag, pool_w_b, scale3, attn_sinks, tables)
        k = order[l, "out"]
        w_out_l = _gather_wait(f"gather_wait_out{l}", bufs[k], send[k], recv[k], (cat,), halved=l + 1 < DEPTH)
        saved.append([xs, pu, pg, q, kv, ag, cat, None, w_in_l, w_out_l])
        below, after = (cat, w_out_l, post3), (w_out_l,)

    x_in, pu, pg, q, kv, ag, cat, y, w_in_l, w_out_l = saved[1]
    dcat, dw_out1, dw_out1_b, dg_post1, loss, xs = _bwd_out(1, cat, w_out_l, post3, place_arr, x=x_in, target=target)
    ex1_out = _exchange_start("exchange_start_out1", [dw_out1_b], ID_OUT1)
    dproj, dpw, dsc1, dsink1 = _bwd_mix(1, pu, pg, q, kv, ag, dcat, pool_w_b, scale3, attn_sinks, tables,
                                        deps=(ex1_out[2][0],))
    dx, dg_pre1, dw_in1, dw_in1_b = _bwd_in_dx(1, dproj, w_in_l, x_in, pre3, xs, dw_place=place_arr)
    ex1_in = _exchange_start("exchange_start_in1", [dw_in1_b], ID_IN1)

    x_in, pu, pg, q, kv, ag, cat, y, w_in_l, w_out_l = saved[0]
    dcat, dw_out0, dw_out0_b, dg_post0 = _bwd_out(0, cat, w_out_l, post3, place_arr, dxn=dx, y=y, deps=(ex1_in[2][0],))
    ex0_out = _exchange_start("exchange_start_out0", [dw_out0_b], ID_OUT0)
    dproj, dpw, dsc0, dsink0 = _bwd_mix(0, pu, pg, q, kv, ag, dcat, pool_w_b, scale3, attn_sinks, tables,
                                        deps=(ex0_out[2][0],), dpw_dest=dpw)
    flat = lambda a: a.reshape(DEPTH * 4 * 128, 128)
    dw_in0, dw_in0_b, dpw_b = _bwd_in_dw(0, dproj, x_in, pre3, place_arr, flat(dpw))
    ex0_in = _exchange_start("exchange_start_in0", [dpw_b, dw_in0_b], ID_IN0)

    grad_x, dg_pre0 = _bwd_in_dx(0, dproj, w_in_l, x_in, pre3, dx, deps=(ex0_in[2][1],))
    small = [jnp.concatenate([dsc0, dsc1, dg_pre0, dg_pre1, dg_post0, dg_post1, dsink0, dsink1, loss], axis=0)]
    ex_small = _exchange_start("exchange_start_small", small, ID_SMALL)
    (recv_out1,) = _exchange_wait("exchange_wait_out1", ex1_out, ex_small[2][0])
    (recv_in1,) = _exchange_wait("exchange_wait_in1", ex1_in, recv_out1)
    (recv_out0,) = _exchange_wait("exchange_wait_out0", ex0_out, recv_in1)
    g_in, g_out = _share(
        "share_a", [lax.empty(w_in_t.shape, F32), lax.empty(w_out.shape, F32)], [(0, 1), (1, 0), (1, 1)],
        collective_id=ID_SHARE_A, summed=[(("half", 0, 1), dw_in1, recv_in1), (("half", 1, 1), dw_out1, recv_out1),
                                          (("half", 1, 0), dw_out0, recv_out0)])
    m_in_t, v_in_t = t(m_w_in), t(v_w_in)
    d_out, nm_out, nv_out, grad_w_out = _adamw("adamw_w_out", w_out, g_out, m_w_out, v_w_out, 256)
    upd_in = _adamw("adamw_w_in1", w_in_t, g_in, m_in_t, v_in_t, 288, first=1, count=1, deps=(d_out,))
    (recv_pw,) = _exchange_wait("exchange_wait_pool", ex0_in, upd_in[0], which=[0])
    (g_pw,) = _sum_small("sum_pool", [flat(dpw)], [recv_pw], place_arr)

    (recv_in0,) = _exchange_wait("exchange_wait_in0", ex0_in, g_pw, which=[1])
    (recv_misc,) = _exchange_wait("exchange_wait_small", ex_small, recv_in0)
    g_in, g_pw, g_misc = _share(
        "share_b", [g_in], [(0, 0)], [g_pw, lax.empty(small[0].shape, F32)], collective_id=ID_SHARE_B,
        summed=[(("half", 0, 0), dw_in0, recv_in0), (("piece", 1), small[0], recv_misc)])
    d_in, nm_in, nv_in, grad_w_in_t, *small_out = _adamw(
        "adamw_w_in0", w_in_t, g_in, m_in_t, v_in_t, 288, first=0, count=1, dests=upd_in,
        small=(packed[0], g_misc, packed[1], packed[2], flat(pool_w), g_pw, flat(m_pool_w), flat(v_pool_w)))
    (g_sc, g_sk, g_pre, g_post, d_sc, d_sk, d_pre, d_post,
     m_sc, m_sk, m_pre, m_post, v_sc, v_sk, v_pre, v_post, loss_sum) = small_out[:17]
    g_pw, d_pw, m_pw, v_pw = [a.reshape(pool_w.shape) for a in small_out[17:]]
    return (loss_sum[0, 0], grad_x[None], t(grad_w_in_t), g_pw, g_sc, g_sk, grad_w_out, g_pre, g_post,
            t(d_in), d_pw, d_sc, d_sk, d_out, d_pre, d_post,
            t(nm_in), m_pw, m_sc, m_sk, nm_out, m_pre, m_post,
            t(nv_in), v_pw, v_sc, v_sk, nv_out, v_pre, v_post)
```

```python
import jax
import jax.numpy as jnp
from jax import lax
from jax.experimental import pallas as pl
from jax.experimental.pallas import tpu as pltpu

F32 = jnp.float32
BF16 = jnp.bfloat16

S = 2048
D = 1024
DEPTH = 2
D_POOL = 512
POOL_WINDOWS = (2, 4, 8, 16)
N_HEADS = 8
D_IN = 2304
N_SHARDS = 4
W_IN_SHARD = D_IN // N_SHARDS
W_OUT_SHARD = D // N_SHARDS
BLK = 128
NB = S // BLK
HALO = 16
PAD = 8
EPS = 1e-6
NEG_INF = -1e30
C_PU, C_PG, C_Q, C_K, C_V, C_AG = 0, 512, 1024, 1536, 1664, 1792

ADAM_LR = 0.001
ADAM_B1 = 0.9
ADAM_B2 = 0.999
ADAM_EPS = 1e-08
ADAM_WD = 0.01
ADAM_STEP = 10

TM = 512
VMEM_LIMIT = 56 * 1024 * 1024

NT = (((1,), (1,)), ((), ()))
TN = (((0,), (0,)), ((), ()))

MESH = pl.DeviceIdType.MESH
ANY = pl.BlockSpec(memory_space=pl.ANY)

ID_FORWARD = (0, 1)
(ID_SHARE_A, ID_SHARE_B, ID_GATHER_FIRST, ID_GATHER_REST, ID_OUT1, ID_IN1, ID_OUT0, ID_IN0, ID_SMALL) = range(2, 11)

MISC_SCALE, MISC_PRE, MISC_POST, MISC_SINKS, MISC_LOSS = 0, 8, 24, 40, 56
MISC_ROWS = 64


def _params(sem=("arbitrary",)):
    return pltpu.CompilerParams(dimension_semantics=sem, vmem_limit_bytes=VMEM_LIMIT)


def _sigmoid(v):
    return 1.0 / (1.0 + jnp.exp(-v))


def _rows8(v):
    r, c = v.shape
    return v.reshape(r // 8, 8, c).sum(axis=0)


def _layer(l, *shape):
    zeros = (0,) * len(shape)
    return pl.BlockSpec((None,) + shape, lambda i: (l,) + zeros)


def _whole(shape):
    zeros = (0,) * len(shape)
    return pl.BlockSpec(shape, lambda i: zeros, pipeline_mode=pl.Buffered(1))


def _fwd_in(l, x, g_pre, w_in_t, below=None):
    fused = below is not None

    def body(x_ref, g_ref, w_ref, *rest):
        if fused:
            cat_ref, wo_ref, gp_ref, y_ref, xn_ref = rest[:5]
            y = jnp.dot(cat_ref[...], wo_ref[...], preferred_element_type=F32)
            y_ref[...] = y
            xt = x_ref[...] + y * lax.rsqrt(jnp.mean(y * y, axis=-1, keepdims=True) + EPS) * gp_ref[...]
            xn_ref[...] = xt
        else:
            xt = x_ref[...]
        pu_ref, pg_ref, q_ref, kv_ref, ag_ref = rest[-5:]
        r = lax.rsqrt(jnp.mean(xt * xt, axis=-1, keepdims=True) + EPS)
        h = (xt * r * g_ref[...]).astype(BF16)

        def proj(lo, hi):
            return lax.dot_general(h, w_ref[lo:hi, :], NT, preferred_element_type=F32)

        pu_ref[...] = proj(C_PU, C_PG)
        pg_ref[...] = proj(C_PG, C_Q)
        q_ref[...] = proj(C_Q, C_K).astype(BF16)
        kv_ref[...] = proj(C_K, C_AG).astype(BF16)
        ag_ref[...] = proj(C_AG, D_IN)

    row = lambda w: pl.BlockSpec((TM, w), lambda i: (i, 0))
    act = jax.ShapeDtypeStruct((S, D), F32)
    return pl.pallas_call(
        body, name="fwd_out_in" if fused else "fwd_in", grid=(S // TM,),
        in_specs=[row(D), _layer(l, 1, D), _whole((D_IN, D))]
        + ([row(D), _whole((D, D)), _layer(l - 1, 1, D)] if fused else []),
        out_specs=[row(D)] * (2 * fused) + [row(512), row(512), row(512), row(256), row(512)],
        out_shape=[act] * (2 * fused)
        + [jax.ShapeDtypeStruct((S, 512), F32), jax.ShapeDtypeStruct((S, 512), F32),
           jax.ShapeDtypeStruct((S, 512), BF16), jax.ShapeDtypeStruct((S, 256), BF16),
           jax.ShapeDtypeStruct((S, 512), F32)],
        compiler_params=_params(),
    )(x, g_pre, w_in_t, *(below if fused else ()))


LOG2E = 1.4426950408889634
SCORE_SCALE = 0.125 * LOG2E


def _attention_tables():
    qi = jnp.arange(BLK)[:, None]
    kj = jnp.arange(BLK)[None, :]
    dist = ((qi - kj) % BLK).astype(F32)
    slopes = jnp.exp2(-jnp.arange(1, N_HEADS + 1, dtype=F32))
    bias = -(slopes * LOG2E)[:, None, None] * dist[None]
    first = jnp.where(kj > qi, NEG_INF, bias)
    return jnp.stack([first, bias]), (kj <= qi).astype(BF16)


def _own_block_mask():
    return lax.broadcasted_iota(jnp.int32, (BLK, BLK), 1) <= lax.broadcasted_iota(jnp.int32, (BLK, BLK), 0)


def _merge(full, own):
    return jnp.where(own, full[:, BLK:], full[:, :BLK])


def _spread(v, tri):
    own = v * tri
    return jnp.concatenate([v - own, own], axis=1)


def _head_variants(cur, prev):
    both = jnp.concatenate([prev, cur], axis=0).astype(F32)
    swapped = pltpu.roll(both, 64, axis=1)
    low = lax.broadcasted_iota(jnp.int32, both.shape, 1) < 64
    zero = jnp.zeros_like(both)
    return ((jnp.where(low, both, zero).astype(BF16), jnp.where(low, zero, swapped).astype(BF16)),
            (jnp.where(low, swapped, zero).astype(BF16), jnp.where(low, zero, both).astype(BF16)))


def _head_of(hkv, t, half):
    return hkv * 4 + 2 * t + half


def _rows(v, t):
    return v[t * BLK:(t + 1) * BLK]


def _stack_tiles(ref, hkv, offset=0):
    lo = offset + 2 * hkv * 128
    return jnp.concatenate([ref[:, lo:lo + 128], ref[:, lo + 128:lo + 256]], axis=0)


def _scores(q2, k_var, own):
    s = {}
    for hkv in range(2):
        for half in range(2):
            full = lax.dot_general(q2[hkv], k_var[hkv][half], NT, preferred_element_type=F32)
            for t in range(2):
                s[hkv, t, half] = _merge(_rows(full, t), own)
    return s


def _softmax(s, bias, sink):
    s = s * SCORE_SCALE + bias
    sink2 = sink * LOG2E
    m = jnp.maximum(jnp.max(s, axis=-1, keepdims=True), sink2)
    p = jnp.exp2(s - m)
    e_sink = jnp.exp2(sink2 - m)
    inv = 1.0 / (jnp.sum(p, axis=-1, keepdims=True) + e_sink)
    return p * inv, e_sink * inv


def _spread_pair(v, hkv, half, tri):
    return jnp.concatenate([_spread(v[hkv, t, half].astype(BF16), tri) for t in range(2)], axis=0)


POOL_ROWS = PAD + HALO + BLK


def _window_sums(src_ref, tmp_refs, trailing):
    lo, hi = (PAD, POOL_ROWS) if trailing else (0, HALO + BLK)
    cur = src_ref
    for level in range(len(POOL_WINDOWS)):
        lanes = slice(level * 128, 512)
        shift = -(1 << level) if trailing else (1 << level)
        dst = tmp_refs[level % 2]
        dst[lo:hi, lanes] = cur[lo:hi, lanes] + cur[lo + shift:hi + shift, lanes]
        cur = dst


def _pool_block(ext_ref, tmp_refs, i, g, w):
    lanes = slice(g * 128, (g + 1) * 128)
    rows = slice(PAD + HALO, POOL_ROWS)
    t = (i * BLK + lax.broadcasted_iota(jnp.int32, (BLK, 1), 0)).astype(F32)
    inv = 1.0 / jnp.minimum(t + 1.0, float(w))
    return tmp_refs[g % 2][rows, lanes] * inv - ext_ref[rows, lanes], inv


def _fwd_mix(l, pu, pg, q, kv, ag, pool_w, pool_scale, sinks, tables):
    bias, tri = tables

    def body(pu_ref, pup_ref, pg_ref, q_ref, kv_ref, kvp_ref, ag_ref, pw_ref, sc_ref, sink_ref, bias_ref, tri_ref,
             cat_ref, ext_ref, *tmp_refs):
        i = pl.program_id(0)

        @pl.when(i == 0)
        def _():
            for ref in (ext_ref, *tmp_refs):
                ref[0:PAD, :] = jnp.zeros((PAD, 512), F32)

        ext_ref[PAD:PAD + HALO, :] = jnp.where(i > 0, pup_ref[...], 0.0)
        ext_ref[PAD + HALO:POOL_ROWS, :] = pu_ref[...]
        _window_sums(ext_ref, tmp_refs, True)
        for g, w in enumerate(POOL_WINDOWS):
            lanes = slice(g * 128, (g + 1) * 128)
            pooled, _ = _pool_block(ext_ref, tmp_refs, i, g, w)
            mixed = jnp.dot(pooled.astype(BF16), pw_ref[g], preferred_element_type=F32)
            gate = pg_ref[:, lanes]
            cat_ref[:, lanes] = (mixed * sc_ref[:, lanes] * (gate * _sigmoid(gate))).astype(BF16)

        own = _own_block_mask()
        tri = tri_ref[...]
        k_var = _head_variants(kv_ref[:, 0:128], kvp_ref[:, 0:128])
        v_var = _head_variants(kv_ref[:, 128:256], kvp_ref[:, 128:256])
        s = _scores([_stack_tiles(q_ref, hkv) for hkv in range(2)], k_var, own)
        p = {}
        for (hkv, t, half), s_head in s.items():
            head = _head_of(hkv, t, half)
            p[hkv, t, half], _ = _softmax(s_head, bias_ref[head], sink_ref[l, head])
        for hkv in range(2):
            o2 = jnp.zeros((2 * BLK, 128), F32)
            for half in range(2):
                o2 = o2 + jnp.dot(_spread_pair(p, hkv, half, tri), v_var[hkv][half], preferred_element_type=F32)
            for t in range(2):
                lo = (2 * hkv + t) * 128
                gate = ag_ref[:, lo:lo + 128]
                cat_ref[:, D_POOL + lo:D_POOL + lo + 128] = (_rows(o2, t) * (gate * _sigmoid(gate))).astype(BF16)

    blk = lambda w: pl.BlockSpec((BLK, w), lambda i: (i, 0))
    prev = lambda w: pl.BlockSpec((BLK, w), lambda i: (jnp.maximum(i - 1, 0), 0))
    halo = pl.BlockSpec((HALO, 512), lambda i: (jnp.maximum(i * (BLK // HALO) - 1, 0), 0))
    return pl.pallas_call(
        body, name="fwd_mix", grid=(NB,),
        in_specs=[blk(512), halo, blk(512), blk(512), blk(256), prev(256), blk(512),
                  _layer(l, 4, 128, 128), _layer(l, 1, 512), pl.BlockSpec(memory_space=pltpu.SMEM),
                  pl.BlockSpec((None, N_HEADS, BLK, BLK), lambda i: (jnp.minimum(i, 1), 0, 0, 0)), _whole((BLK, BLK))],
        out_specs=blk(D),
        out_shape=jax.ShapeDtypeStruct((S, D), BF16),
        scratch_shapes=[pltpu.VMEM((POOL_ROWS, 512), F32)] * 3,
        compiler_params=_params(),
    )(pu, pu, pg, q, kv, kv, ag, pool_w, pool_scale, sinks, bias, tri)


def _store_lane_rows(ref, acc):
    total = jnp.sum(acc, axis=0, keepdims=True)
    for k in range(ref.shape[0]):
        ref[k:k + 1, :] = total[:, k * 128:(k + 1) * 128]


def _own_piece(dw_ref, place_ref):
    p = dw_ref.shape[0] // 8
    return dw_ref[pl.ds(pl.multiple_of(place_ref[0] * p, 8), p), :]


def _bwd_out(l, cat, w_out, g_post, place_arr, dxn=None, y=None, x=None, target=None, deps=()):
    last = target is not None
    n_steps = S // TM

    def body(a_ref, b_ref, g_ref, cat_ref, w_ref, place_ref, *rest):
        dcat_ref, own_ref, dwb_ref, dg_ref = rest[len(deps):len(deps) + 4]
        rest = rest[len(deps) + 4:]
        acc_ref, dw_ref = rest[-2:]
        step = pl.program_id(0)

        @pl.when(step == 0)
        def _():
            dw_ref[...] = jnp.zeros_like(dw_ref)
            acc_ref[...] = jnp.zeros_like(acc_ref)

        cat = cat_ref[...]
        g = g_ref[...]
        y = jnp.dot(cat, w_ref[...], preferred_element_type=F32) if last else b_ref[...]
        r = lax.rsqrt(jnp.mean(y * y, axis=-1, keepdims=True) + EPS)
        if last:
            loss_ref, dx_ref, loss_acc_ref = rest[:3]
            err = a_ref[...] + y * r * g - b_ref[...]

            @pl.when(step == 0)
            def _():
                loss_acc_ref[...] = jnp.zeros_like(loss_acc_ref)

            loss_acc_ref[...] += _rows8(err * err)
            dz = err * (1.0 / D)
            dx_ref[...] = dz
        else:
            dz = a_ref[...]
        a = dz * g
        dy = r * a - y * (r * r * r) * jnp.mean(a * y, axis=-1, keepdims=True)
        acc_ref[...] += _rows8(dz * (y * r))
        dyb = dy.astype(BF16)
        dcat_ref[...] = lax.dot_general(dyb, w_ref[...], NT, preferred_element_type=F32)
        dw_ref[...] += lax.dot_general(cat, dyb, TN, preferred_element_type=F32)

        @pl.when(step == n_steps - 1)
        def _():
            _store_lane_rows(dg_ref, acc_ref[...])
            dwb_ref[...] = dw_ref[...].astype(BF16)
            own_ref[...] = _own_piece(dw_ref, place_ref)
            if last:
                loss_ref[...] = jnp.full((8, 128), (0.5 / D) * jnp.sum(loss_acc_ref[...]), F32)

    row = lambda: pl.BlockSpec((TM, D), lambda i: (i, 0))
    full = _whole
    return pl.pallas_call(
        body, name="out_loss_bwd" if last else "bwd_out", grid=(n_steps,),
        in_specs=[row(), row(), _layer(l, 1, D), row(), full((D, D)), pl.BlockSpec(memory_space=pltpu.SMEM)]
        + [ANY] * len(deps),
        out_specs=[row(), full((D // 8, D)), full((D, D)), full((8, 128))] + ([full((8, 128)), row()] if last else []),
        out_shape=[jax.ShapeDtypeStruct((S, D), F32), jax.ShapeDtypeStruct((D // 8, D), F32),
                   jax.ShapeDtypeStruct((D, D), BF16), jax.ShapeDtypeStruct((8, 128), F32)]
        + ([jax.ShapeDtypeStruct((8, 128), F32), jax.ShapeDtypeStruct((S, D), F32)] if last else []),
        scratch_shapes=([pltpu.VMEM((8, D), F32)] if last else []) + [pltpu.VMEM((8, D), F32), pltpu.VMEM((D, D), F32)],
        compiler_params=_params(),
    )(*((x, target) if last else (dxn, y)), g_post, cat, w_out, place_arr, *deps)


def _bwd_mix(l, pu, pg, q, kv, ag, dcat, pool_w, pool_scale, sinks, tables, deps=(), dpw_dest=None):
    bias, tri = tables
    deps = tuple(deps) + (() if dpw_dest is None else (dpw_dest,))

    def body(pu_ref, pup_ref, pg_ref, q_ref, kv_ref, kvp_ref, ag_ref, dcat_ref, pw_ref, sc_ref, sink_ref, bias_ref,
             tri_ref, *rest):
        dproj_ref, dpw_ref, dsc_ref, dsink_ref, ext_ref, dext_ref, tmp_a, tmp_b, dkv_ref = rest[len(deps):]
        tmp_refs = (tmp_a, tmp_b)
        step = pl.program_id(0)
        i = NB - 1 - step

        @pl.when(step == 0)
        def _():
            dpw_ref[...] = jnp.zeros_like(dpw_ref)
            dsc_ref[...] = jnp.zeros_like(dsc_ref)
            dsink_ref[...] = jnp.zeros_like(dsink_ref)
            for ref in (ext_ref, tmp_a, tmp_b):
                ref[0:PAD, :] = jnp.zeros((PAD, 512), F32)
            dext_ref[BLK:POOL_ROWS, :] = jnp.zeros((HALO + PAD, 512), F32)
            dkv_ref[...] = jnp.zeros_like(dkv_ref)

        ext_ref[PAD:PAD + HALO, :] = jnp.where(i > 0, pup_ref[...], 0.0)
        ext_ref[PAD + HALO:POOL_ROWS, :] = pu_ref[...]
        _window_sums(ext_ref, tmp_refs, True)
        dpooled = []
        for g, w in enumerate(POOL_WINDOWS):
            lanes = slice(g * 128, (g + 1) * 128)
            pooled, inv = _pool_block(ext_ref, tmp_refs, i, g, w)
            pooled_b = pooled.astype(BF16)
            mixed = jnp.dot(pooled_b, pw_ref[g], preferred_element_type=F32)
            scale = sc_ref[:, lanes]
            gate = pg_ref[:, lanes]
            sg = _sigmoid(gate)
            dpo = dcat_ref[:, lanes]
            dproj_ref[:, C_PG + g * 128:C_PG + (g + 1) * 128] = (
                dpo * (mixed * scale) * (sg * (1.0 + gate * (1.0 - sg)))).astype(BF16)
            dms = dpo * (gate * sg)
            dsc_ref[g:g + 1, :] += jnp.sum(dms * mixed, axis=0, keepdims=True)
            dmixed = (dms * scale).astype(BF16)
            dpw_ref[g] += lax.dot_general(pooled_b, dmixed, TN, preferred_element_type=F32)
            dpooled.append(lax.dot_general(dmixed, pw_ref[g], NT, preferred_element_type=F32))
            dext_ref[0:BLK, lanes] = dpooled[g] * inv
        _window_sums(dext_ref, tmp_refs, False)
        for g in range(len(POOL_WINDOWS)):
            lanes = slice(g * 128, (g + 1) * 128)
            dproj_ref[:, C_PU + g * 128:C_PU + (g + 1) * 128] = (tmp_refs[g % 2][0:BLK, lanes] - dpooled[g]).astype(BF16)
        dext_ref[BLK:BLK + HALO, :] = dext_ref[0:HALO, :]

        own = _own_block_mask()
        tri = tri_ref[...]
        k_var = _head_variants(kv_ref[:, 0:128], kvp_ref[:, 0:128])
        v_var = _head_variants(kv_ref[:, 128:256], kvp_ref[:, 128:256])
        q2 = [_stack_tiles(q_ref, hkv) for hkv in range(2)]
        s = _scores(q2, k_var, own)
        p, p_sink = {}, {}
        for key, s_head in s.items():
            head = _head_of(*key)
            p[key], p_sink[key] = _softmax(s_head, bias_ref[head], sink_ref[l, head])

        do2, p_b, dp = [], {}, {}
        for hkv in range(2):
            gate = _stack_tiles(ag_ref, hkv)
            sg = _sigmoid(gate)
            dca = _stack_tiles(dcat_ref, hkv, D_POOL)
            do2.append((dca * (gate * sg)).astype(BF16))
            o2 = jnp.zeros((2 * BLK, 128), F32)
            for half in range(2):
                p_b[hkv, half] = _spread_pair(p, hkv, half, tri)
                o2 = o2 + jnp.dot(p_b[hkv, half], v_var[hkv][half], preferred_element_type=F32)
                full = lax.dot_general(do2[hkv], v_var[hkv][half], NT, preferred_element_type=F32)
                for t in range(2):
                    dp[hkv, t, half] = _merge(_rows(full, t), own)
            dag = dca * o2 * (sg * (1.0 + gate * (1.0 - sg)))
            for t in range(2):
                lo = C_AG + (2 * hkv + t) * 128
                dproj_ref[:, lo:lo + 128] = _rows(dag, t).astype(BF16)

        ds = {}
        for key in p:
            delta = jnp.sum(p[key] * dp[key], axis=-1, keepdims=True)
            ds[key] = p[key] * (dp[key] - delta)
            head = _head_of(*key)
            dsink_ref[0:1, :] += jnp.where(lax.broadcasted_iota(jnp.int32, (1, 128), 1) == head,
                                           -jnp.sum(p_sink[key] * delta, axis=0, keepdims=True), 0.0)

        dk_acc = [[None, None], [None, None]]
        dv_acc = [[None, None], [None, None]]
        for hkv in range(2):
            dq2 = jnp.zeros((2 * BLK, 128), F32)
            for half in range(2):
                ds_b = _spread_pair(ds, hkv, half, tri)
                dq2 = dq2 + jnp.dot(ds_b, k_var[hkv][half], preferred_element_type=F32)
                dk_acc[hkv][half] = lax.dot_general(ds_b, q2[hkv], TN, preferred_element_type=F32)
                dv_acc[hkv][half] = lax.dot_general(p_b[hkv, half], do2[hkv], TN, preferred_element_type=F32)
            for t in range(2):
                lo = C_Q + (2 * hkv + t) * 128
                dproj_ref[:, lo:lo + 128] = (_rows(dq2, t) * 0.125).astype(BF16)

        low = lax.broadcasted_iota(jnp.int32, (2 * BLK, 128), 1) < 64

        def gather_heads(acc):
            return jnp.where(low, acc[0][0] + pltpu.roll(acc[0][1], 64, axis=1),
                             pltpu.roll(acc[1][0], 64, axis=1) + acc[1][1])

        dk = gather_heads(dk_acc) * 0.125
        dv = gather_heads(dv_acc)
        dproj_ref[:, C_K:C_V] = (dk[BLK:, :] + dkv_ref[:, 0:128]).astype(BF16)
        dproj_ref[:, C_V:C_AG] = (dv[BLK:, :] + dkv_ref[:, 128:256]).astype(BF16)
        dkv_ref[:, 0:128] = dk[:BLK, :]
        dkv_ref[:, 128:256] = dv[:BLK, :]

    rev = lambda w: pl.BlockSpec((BLK, w), lambda s: (NB - 1 - s, 0))
    prev = lambda w: pl.BlockSpec((BLK, w), lambda s: (jnp.maximum(NB - 2 - s, 0), 0))
    halo = pl.BlockSpec((HALO, 512), lambda s: (jnp.maximum((NB - 1 - s) * (BLK // HALO) - 1, 0), 0))
    return pl.pallas_call(
        body, name="bwd_mix", grid=(NB,),
        in_specs=[rev(512), halo, rev(512), rev(512), rev(256), prev(256), rev(512), rev(D),
                  _layer(l, 4, 128, 128), _layer(l, 1, 512), pl.BlockSpec(memory_space=pltpu.SMEM),
                  pl.BlockSpec((None, N_HEADS, BLK, BLK), lambda s: (jnp.minimum(NB - 1 - s, 1), 0, 0, 0)),
                  _whole((BLK, BLK))] + [ANY] * len(deps),
        out_specs=[rev(D_IN), _layer(l, 4, 128, 128),
                   pl.BlockSpec((4, 128), lambda s: (0, 0)), pl.BlockSpec((8, 128), lambda s: (0, 0))],
        out_shape=[jax.ShapeDtypeStruct((S, D_IN), BF16), jax.ShapeDtypeStruct((DEPTH, 4, 128, 128), F32),
                   jax.ShapeDtypeStruct((4, 128), F32), jax.ShapeDtypeStruct((8, 128), F32)],
        input_output_aliases={} if dpw_dest is None else {12 + len(deps): 1},
        scratch_shapes=[pltpu.VMEM((POOL_ROWS, 512), F32)] * 4 + [pltpu.VMEM((BLK, 256), F32)],
        compiler_params=_params(),
    )(pu, pu, pg, q, kv, kv, ag, dcat, pool_w, pool_scale, sinks, bias, tri, *deps)


def _bwd_in_dw(l, dproj, x, g_pre, place_arr, to_bf16, deps=()):
    n_steps = S // TM

    def body(dp_ref, x_ref, g_ref, place_ref, *rest):
        f32_ref, own_ref, dwb_ref, bf16_ref, dw_ref = rest[len(deps):]
        step = pl.program_id(0)

        @pl.when(step == 0)
        def _():
            dw_ref[...] = jnp.zeros_like(dw_ref)
            bf16_ref[...] = f32_ref[...].astype(BF16)

        xt = x_ref[...]
        r = lax.rsqrt(jnp.mean(xt * xt, axis=-1, keepdims=True) + EPS)
        h = (xt * r * g_ref[...]).astype(BF16)
        dw_ref[...] += lax.dot_general(dp_ref[...], h, TN, preferred_element_type=F32)

        @pl.when(step == n_steps - 1)
        def _():
            dwb_ref[...] = dw_ref[...].astype(BF16)
            own_ref[...] = _own_piece(dw_ref, place_ref)

    row = lambda w: pl.BlockSpec((TM, w), lambda i: (i, 0))
    full = _whole
    return pl.pallas_call(
        body, name="bwd_in_dw", grid=(n_steps,),
        in_specs=[row(D_IN), row(D), _layer(l, 1, D), pl.BlockSpec(memory_space=pltpu.SMEM)] + [ANY] * len(deps)
        + [full(to_bf16.shape)],
        out_specs=[full((D_IN // 8, D)), full((D_IN, D)), full(to_bf16.shape)],
        out_shape=[jax.ShapeDtypeStruct((D_IN // 8, D), F32), jax.ShapeDtypeStruct((D_IN, D), BF16),
                   jax.ShapeDtypeStruct(to_bf16.shape, BF16)],
        scratch_shapes=[pltpu.VMEM((D_IN, D), F32)],
        compiler_params=_params(),
    )(dproj, x, g_pre, place_arr, *deps, to_bf16)


def _bwd_in_dx(l, dproj, w_in_t, x, g_pre, dres, deps=(), dw_place=None):
    n_steps = S // TM
    with_dw = dw_place is not None

    def body(dp_ref, w_ref, x_ref, g_ref, dres_ref, *rest):
        place_ref = rest[0] if with_dw else None
        rest = rest[with_dw + len(deps):]
        if with_dw:
            dx_ref, dg_ref, own_ref, dwb_ref, acc_ref, dw_ref = rest
        else:
            dx_ref, dg_ref, acc_ref = rest
        step = pl.program_id(0)

        @pl.when(step == 0)
        def _():
            acc_ref[...] = jnp.zeros_like(acc_ref)
            if with_dw:
                dw_ref[...] = jnp.zeros_like(dw_ref)

        g = g_ref[...]
        halves = [slice(k * (TM // 2), (k + 1) * (TM // 2)) for k in range(2)]
        dh = [jnp.dot(dp_ref[rows, :], w_ref[...], preferred_element_type=F32) for rows in halves]
        h = []
        for rows, dh_k in zip(halves, dh):
            xt = x_ref[rows, :]
            r = lax.rsqrt(jnp.mean(xt * xt, axis=-1, keepdims=True) + EPS)
            xn = xt * r
            acc_ref[...] += _rows8(dh_k * xn)
            a = dh_k * g
            dx_ref[rows, :] = dres_ref[rows, :] + (
                r * a - xt * (r * r * r) * jnp.mean(a * xt, axis=-1, keepdims=True))
            h.append((xn * g).astype(BF16))
        if with_dw:
            dw_ref[...] += lax.dot_general(dp_ref[...], jnp.concatenate(h, axis=0), TN, preferred_element_type=F32)

        @pl.when(step == n_steps - 1)
        def _():
            _store_lane_rows(dg_ref, acc_ref[...])
            if with_dw:
                dwb_ref[...] = dw_ref[...].astype(BF16)
                own_ref[...] = _own_piece(dw_ref, place_ref)

    row = lambda w: pl.BlockSpec((TM, w), lambda i: (i, 0))
    full = _whole
    dw_specs = [full((D_IN // 8, D)), full((D_IN, D))] if with_dw else []
    dw_shapes = [jax.ShapeDtypeStruct((D_IN // 8, D), F32), jax.ShapeDtypeStruct((D_IN, D), BF16)] if with_dw else []
    return pl.pallas_call(
        body, name="bwd_in" if with_dw else "bwd_in_dx", grid=(n_steps,),
        in_specs=[row(D_IN), full((D_IN, D)), row(D), _layer(l, 1, D), row(D)]
        + [pl.BlockSpec(memory_space=pltpu.SMEM)] * with_dw + [ANY] * len(deps),
        out_specs=[row(D), full((8, 128))] + dw_specs,
        out_shape=[jax.ShapeDtypeStruct((S, D), F32), jax.ShapeDtypeStruct((8, 128), F32)] + dw_shapes,
        scratch_shapes=[pltpu.VMEM((8, D), F32)] + [pltpu.VMEM((D_IN, D), F32)] * with_dw,
        compiler_params=_params(),
    )(dproj, w_in_t, x, g_pre, dres, *((dw_place,) if with_dw else ()), *deps)


HBM =pl.BlockSpec(memory_space=pltpu.HBM)
SEM = pl.BlockSpec(memory_space=pltpu.SEMAPHORE)
def _split_copy(collective_id=None):
    return pltpu.CompilerParams(has_side_effects=pltpu.SideEffectType.DATAFLOW_SIDE_EFFECTING,
                                collective_id=collective_id)


SPLIT_COPY = _split_copy()


def _in_hbm(a):
    return pltpu.with_memory_space_constraint(a, pltpu.HBM)

def _place():
    return lax.axis_index("x"), lax.axis_index("y"), lax.axis_index("c")


def _other_chips(x, y):
    return [(1 - x, y), (x, 1 - y), (1 - x, 1 - y)]


def _peer(x, y, c, m):
    return (x ^ (m >> 2), y ^ ((m >> 1) & 1), c ^ (m & 1))


SAME_CORE = (2, 4, 6)


def _place_cast(name, src, chip_arr, tile, layers, deps=()):
    _, n, cols = src.shape
    steps = n // tile
    k = len(layers)

    def body(chip_ref, *refs):
        for s_ref, o_ref in zip(refs[:k], refs[k + len(deps):]):
            o_ref[...] = s_ref[...].astype(BF16)

    def layer_spec(l):
        return pl.BlockSpec((None, tile, cols), lambda i, chip: (l, i, 0))

    return pl.pallas_call(
        body, name=name,
        grid_spec=pltpu.PrefetchScalarGridSpec(
            num_scalar_prefetch=1, grid=(steps,),
            in_specs=[layer_spec(l) for l in layers] + [ANY] * len(deps),
            out_specs=[pl.BlockSpec((tile, cols), lambda i, chip: (chip[0] * steps + i, 0))] * k),
        out_shape=[jax.ShapeDtypeStruct((N_SHARDS * n, cols), BF16)] * k,
        compiler_params=_params(),
    )(chip_arr, *[src] * k, *deps)


def _place_other_half(name, src, place_arr, layer, dest):
    _, n, cols = src.shape

    def body(place_ref, s_ref, dest_ref, o_ref):
        o_ref[...] = s_ref[...].astype(BF16)

    return pl.pallas_call(
        body, name=name,
        grid_spec=pltpu.PrefetchScalarGridSpec(
            num_scalar_prefetch=1, grid=(1,),
            in_specs=[pl.BlockSpec((None, n // 2, cols), lambda i, place: (layer, 1 - place[1], 0)), ANY],
            out_specs=pl.BlockSpec((n // 2, cols), lambda i, place: (place[0] + 1 - 2 * place[1], 0))),
        out_shape=jax.ShapeDtypeStruct((N_SHARDS * n, cols), BF16),
        input_output_aliases={2: 0},
        compiler_params=_params(),
    )(place_arr, src, dest)


def _chip_rows(ref, chip, half=None):
    n = ref.shape[0] // N_SHARDS
    if half is None:
        return ref.at[pl.ds(pl.multiple_of(chip * n, 16), n), :]
    return ref.at[pl.ds(pl.multiple_of(chip * n + half * (n // 2), 16), n // 2), :]


def _gather_start(name, bufs, halved, collective_id):
    n = len(bufs)

    def body(*refs):
        ins, send, recv, token = refs[:n], refs[n:2 * n], refs[2 * n:3 * n], refs[-1]
        x, y, c = _place()
        _handshake([(*chip, c) for chip in _other_chips(x, y)])
        for a, buf in enumerate(ins):
            own = _chip_rows(buf, 2 * x + y, c if a in halved else None)
            for j, chip in enumerate(_other_chips(x, y)):
                pltpu.make_async_remote_copy(src_ref=own, dst_ref=own, send_sem=send[a].at[j], recv_sem=recv[a].at[j],
                                             device_id=(*chip, c), device_id_type=MESH).start()
        token[...] = jnp.zeros_like(token)

    outs = pl.pallas_call(
        body, name=name, in_specs=[HBM] * n,
        out_specs=[SEM] * (2 * n) + [HBM] * n + [pl.BlockSpec(memory_space=pltpu.VMEM)],
        out_shape=[pltpu.SemaphoreType.DMA((3,))] * (2 * n) + [pltpu.HBM(b.shape, b.dtype) for b in bufs]
        + [jax.ShapeDtypeStruct((8, 128), F32)],
        input_output_aliases={a: 2 * n + a for a in range(n)},
        compiler_params=_split_copy(collective_id),
    )(*[_in_hbm(b) for b in bufs])
    return outs[:n], outs[n:2 * n], outs[2 * n:3 * n], outs[-1]


def _gather_start_cast(name, src, layer, collective_id):
    _, n, cols = src.shape
    half = n // 2

    def body(src_ref, send, recv, buf_ref, token, f32_ref, bf16_ref, local):
        x, y, c = _place()
        own = _chip_rows(buf_ref, 2 * x + y, c)

        def cast():
            load = pltpu.make_async_copy(src_ref.at[layer, pl.ds(pl.multiple_of(c * half, 16), half), :], f32_ref,
                                         local.at[0])
            load.start()
            load.wait()
            bf16_ref[...] = f32_ref[...].astype(BF16)
            store = pltpu.make_async_copy(bf16_ref, own, local.at[1])
            store.start()
            store.wait()

        _handshake([(*chip, c) for chip in _other_chips(x, y)], cast)
        for j, chip in enumerate(_other_chips(x, y)):
            pltpu.make_async_remote_copy(src_ref=own, dst_ref=own, send_sem=send.at[j], recv_sem=recv.at[j],
                                         device_id=(*chip, c), device_id_type=MESH).start()
        token[...] = jnp.zeros_like(token)

    outs = pl.pallas_call(
        body, name=name, in_specs=[HBM],
        out_specs=[SEM, SEM, HBM, pl.BlockSpec(memory_space=pltpu.VMEM)],
        out_shape=[pltpu.SemaphoreType.DMA((3,))] * 2 + [pltpu.HBM((N_SHARDS * n, cols), BF16),
                                                         jax.ShapeDtypeStruct((8, 128), F32)],
        scratch_shapes=[pltpu.VMEM((half, cols), F32), pltpu.VMEM((half, cols), BF16), pltpu.SemaphoreType.DMA((2,))],
        compiler_params=_split_copy(collective_id),
    )(_in_hbm(src))
    return outs[:1], outs[1:2], outs[2:3], outs[3]


def _gather_wait(name, bufs, send_sems, recv_sems, after, halved=False):
    n = len(bufs)

    def body(*refs):
        x, y, c = _place()
        half = c if halved else None
        for buf_ref, send_ref, recv_ref in zip(refs[:n], refs[n:2 * n], refs[2 * n:3 * n]):
            own = _chip_rows(buf_ref, 2 * x + y, half)
            for j, chip in enumerate(_other_chips(x, y)):
                copy = pltpu.make_async_remote_copy(
                    src_ref=own, dst_ref=_chip_rows(buf_ref, 2 * chip[0] + chip[1], half), send_sem=send_ref.at[j],
                    recv_sem=recv_ref.at[j], device_id=(*chip, c), device_id_type=MESH)
                copy.wait_send()
                copy.wait_recv()

    return pl.pallas_call(
        body, name=name, in_specs=[HBM] * n + [SEM] * (2 * n) + [ANY] * len(after), out_specs=[HBM] * n,
        out_shape=[pltpu.HBM(b.shape, b.dtype) for b in bufs], input_output_aliases={a: a for a in range(n)},
        compiler_params=SPLIT_COPY,
    )(*bufs, *send_sems, *recv_sems, *after)


def _handshake(peers, meanwhile=None):
    barrier = pltpu.get_barrier_semaphore()
    for peer in peers:
        pl.semaphore_signal(barrier, inc=1, device_id=peer, device_id_type=MESH)
    if meanwhile is not None:
        meanwhile()
    pl.semaphore_wait(barrier, len(peers))


def _sibling_handshake(x, y, c):
    _handshake([(x, y, 1 - c)])


def _forward_halves(name, bufs, collective_id):
    n = len(bufs)

    def body(*refs):
        ins, outs, (send_sems, recv_sems) = refs[:n], refs[n:2 * n], refs[2 * n:]
        x, y, c = _place()
        _sibling_handshake(x, y, c)

        def copy(a, j, chip, half):
            rows = 2 * chip[0] + chip[1]
            return pltpu.make_async_remote_copy(
                src_ref=_chip_rows(ins[a], rows, half), dst_ref=_chip_rows(outs[a], rows, half),
                send_sem=send_sems.at[3 * a + j], recv_sem=recv_sems.at[3 * a + j], device_id=(x, y, 1 - c),
                device_id_type=MESH)

        copies = [(a, j, chip) for a in range(n) for j, chip in enumerate(_other_chips(x, y))]
        for a, j, chip in copies:
            copy(a, j, chip, c).start()
        for a, j, chip in copies:
            copy(a, j, chip, c).wait_send()
            copy(a, j, chip, 1 - c).wait_recv()

    return pl.pallas_call(
        body, name=name, in_specs=[ANY] * n, out_specs=[ANY] * n,
        out_shape=[jax.ShapeDtypeStruct(b.shape, b.dtype) for b in bufs],
        input_output_aliases={a: a for a in range(n)},
        scratch_shapes=[pltpu.SemaphoreType.DMA((3 * n,))] * 2,
        compiler_params=pltpu.CompilerParams(collective_id=collective_id),
    )(*bufs)


def _piece_rows(ref, k):
    p = ref.shape[0] // 8
    return ref.at[pl.ds(pl.multiple_of(k * p, 32 // jnp.dtype(ref.dtype).itemsize), p), :]


def _exchange_start(name, arrays, collective_id):
    n = len(arrays)
    zones = [lax.empty((7, a.shape[0] // 8, a.shape[1]), a.dtype) for a in arrays]

    def body(*refs):
        srcs, lands = refs[:n], refs[n:2 * n]
        send, recv = refs[2 * n:3 * n], refs[3 * n:4 * n]
        x, y, c = _place()
        _handshake([_peer(x, y, c, m) for m in range(1, 8)])
        for a, (src, land) in enumerate(zip(srcs, lands)):
            for m in range(1, 8):
                px, py, pc = _peer(x, y, c, m)
                pltpu.make_async_remote_copy(
                    src_ref=_piece_rows(src, 4 * px + 2 * py + pc), dst_ref=land.at[m - 1], send_sem=send[a].at[m - 1],
                    recv_sem=recv[a].at[m - 1], device_id=(px, py, pc), device_id_type=MESH).start()

    outs = pl.pallas_call(
        body, name=name, in_specs=[HBM] * (2 * n), out_specs=[SEM] * (2 * n) + [HBM] * (2 * n),
        out_shape=[pltpu.SemaphoreType.DMA((7,))] * (2 * n) + [pltpu.HBM(a.shape, a.dtype) for a in arrays + zones],
        input_output_aliases={a: 2 * n + a for a in range(2 * n)},
        compiler_params=_split_copy(collective_id),
    )(*[_in_hbm(a) for a in arrays + zones])
    return outs[:n], outs[n:2 * n], outs[2 * n:3 * n], outs[3 * n:4 * n]


def _exchange_wait(name, started, after, which=None):
    which = range(len(started[2])) if which is None else which
    send_sems, recv_sems, arrays, zones = [[group[k] for k in which] for group in started[:4]]
    n = len(arrays)

    def body(*refs):
        srcs, lands = refs[:n], refs[n:2 * n]
        send, recv = refs[2 * n:3 * n], refs[3 * n:4 * n]
        x, y, c = _place()
        for a, (src, land) in enumerate(zip(srcs, lands)):
            for m in range(1, 8):
                px, py, pc = _peer(x, y, c, m)
                copy = pltpu.make_async_remote_copy(
                    src_ref=_piece_rows(src, 4 * px + 2 * py + pc), dst_ref=land.at[m - 1], send_sem=send[a].at[m - 1],
                    recv_sem=recv[a].at[m - 1], device_id=(px, py, pc), device_id_type=MESH)
                copy.wait_send()
                copy.wait_recv()

    outs = pl.pallas_call(
        body, name=name, in_specs=[HBM] * (2 * n) + [SEM] * (2 * n) + [ANY], out_specs=[HBM] * (2 * n),
        out_shape=[pltpu.HBM(a.shape, a.dtype) for a in list(arrays) + list(zones)],
        input_output_aliases={a: a for a in range(2 * n)}, compiler_params=SPLIT_COPY,
    )(*arrays, *zones, *send_sems, *recv_sems, after)
    return outs[n:]


def _sum_small(name, partials, recvs, place_arr):
    n = len(partials)

    def body(place_ref, *refs):
        for o_ref, r_ref, out_ref in zip(refs[:n], refs[n:2 * n], refs[2 * n:]):
            total = o_ref[...]
            for m in range(7):
                total = total + r_ref[m].astype(F32)
            out_ref[...] = total

    piece = lambda a: pl.BlockSpec((a.shape[0] // 8, a.shape[1]), lambda i, place: (place[0], 0))
    return pl.pallas_call(
        body, name=name,
        grid_spec=pltpu.PrefetchScalarGridSpec(
            num_scalar_prefetch=1, grid=(1,),
            in_specs=[piece(a) for a in partials] + [pl.BlockSpec(r.shape, lambda i, place: (0, 0, 0)) for r in recvs],
            out_specs=[piece(a) for a in partials]),
        out_shape=[jax.ShapeDtypeStruct(a.shape, F32) for a in partials],
        compiler_params=_params(),
    )(place_arr, *partials, *recvs)


def _share(name, bufs, parts, gathered=(), collective_id=None, summed=()):
    n, n_g, n_s = len(bufs), len(gathered), len(summed)
    total = n + n_g
    made = [target for target, _, _ in summed]

    def body(*refs):
        ins, extra, outs = refs[:total], refs[total:total + 2 * n_s], refs[total + 2 * n_s:2 * total + 2 * n_s]
        send_sems, recv_sems, send_g, recv_g = refs[2 * total + 2 * n_s:2 * total + 2 * n_s + 4]
        scratch = refs[2 * total + 2 * n_s + 4:]
        x, y, c = _place()

        def half(ref, l, which):
            p = ref.shape[1] // 2
            return ref.at[l, pl.ds(pl.multiple_of(which * p, 8), p), :]

        def loads(j):
            local, acc, got = scratch[0], scratch[1:1 + n_s], scratch[1 + n_s:]
            own_rows = extra[2 * j] if made[j][0] == "half" else _piece_rows(extra[2 * j], 4 * x + 2 * y + c)
            return [pltpu.make_async_copy(own_rows, acc[j], local.at[2 * j]),
                    pltpu.make_async_copy(extra[2 * j + 1], got[j], local.at[2 * j + 1])]

        def sum_of(j):
            local, acc, got = scratch[0], scratch[1:1 + n_s], scratch[1 + n_s:]
            for load in loads(j):
                load.wait()
            rows = acc[j].shape[0]
            step = rows if rows <= 128 else 96
            for r in range(0, rows, step):
                part = acc[j][r:r + step, :]
                for m in range(7):
                    part = part + got[j][m, r:r + step, :].astype(F32)
                acc[j][r:r + step, :] = part
            if made[j][0] == "half":
                dest = half(outs[made[j][1]], made[j][2], c)
            else:
                dest = _piece_rows(outs[n + made[j][1]], 4 * x + 2 * y + c)
            store = pltpu.make_async_copy(acc[j], dest, local.at[2 * j])
            store.start()
            store.wait()

        def early():
            for j in range(n_s):
                for load in loads(j):
                    load.start()
            for j in range(n_s):
                if made[j][0] == "piece":
                    sum_of(j)

        _handshake([_peer(x, y, c, m) for m in (SAME_CORE if gathered else ()) + (1,)], early if summed else None)

        def swap(k, which):
            a, l = parts[k]
            held = outs if ("half", a, l) in made else ins
            return pltpu.make_async_remote_copy(
                src_ref=half(held[a], l, which), dst_ref=half(outs[a], l, which), send_sem=send_sems.at[k],
                recv_sem=recv_sems.at[k], device_id=(x, y, 1 - c), device_id_type=MESH)

        def spread(a, m, sender, held, to):
            k = 4 * sender[0] + 2 * sender[1] + sender[2]
            return pltpu.make_async_remote_copy(
                src_ref=_piece_rows(held[n + a], k), dst_ref=_piece_rows(outs[n + a], k),
                send_sem=send_g.at[7 * a + m - 1], recv_sem=recv_g.at[7 * a + m - 1], device_id=to, device_id_type=MESH)

        me, sibling = (x, y, c), (x, y, 1 - c)

        def own(a, m):
            return spread(a, m, me, outs if ("piece", a) in made else ins, _peer(x, y, c, m))

        def handed_on(a, m):
            return spread(a, m + 1, _peer(x, y, c, m), outs, sibling)

        for a in range(n_g):
            for m in SAME_CORE + (1,):
                own(a, m).start()
        for j in range(n_s):
            if made[j][0] == "half":
                sum_of(j)
        for k in range(len(parts)):
            swap(k, c).start()
        for a in range(n_g):
            for m in SAME_CORE:
                spread(a, m, _peer(x, y, c, m), ins, _peer(x, y, c, m)).wait_recv()
                handed_on(a, m).start()
        for k in range(len(parts)):
            swap(k, c).wait_send()
            swap(k, 1 - c).wait_recv()
        for a in range(n_g):
            for m in SAME_CORE + (1,):
                own(a, m).wait_send()
            for m in SAME_CORE:
                handed_on(a, m).wait_send()
                spread(a, m + 1, _peer(x, y, c, m + 1), ins, sibling).wait_recv()
            spread(a, 1, sibling, ins, sibling).wait_recv()

    arrays = list(bufs) + list(gathered)
    sum_scratch = []
    if summed:
        sum_scratch = [pltpu.SemaphoreType.DMA((2 * n_s,))] + [pltpu.VMEM(recv.shape[1:], F32) for _, _, recv in summed]
        sum_scratch += [pltpu.VMEM(recv.shape, recv.dtype) for _, _, recv in summed]
    return pl.pallas_call(
        body, name=name, in_specs=[ANY] * (total + 2 * n_s), out_specs=[ANY] * total,
        out_shape=[jax.ShapeDtypeStruct(b.shape, F32) for b in arrays],
        input_output_aliases={a: a for a in range(total)},
        scratch_shapes=[pltpu.SemaphoreType.DMA((max(len(parts), 1),))] * 2
        + [pltpu.SemaphoreType.DMA((max(7 * n_g, 1),))] * 2 + sum_scratch,
        compiler_params=pltpu.CompilerParams(collective_id=collective_id, vmem_limit_bytes=VMEM_LIMIT),
    )(*arrays, *[array for _, own, recv in summed for array in (own, recv)])


def _adamw_math(w, g, m, v):
    nm = ADAM_B1 * m + (1.0 - ADAM_B1) * g
    nv = ADAM_B2 * v + (1.0 - ADAM_B2) * (g * g)
    m_hat = nm / (1.0 - ADAM_B1 ** ADAM_STEP)
    v_hat = nv / (1.0 - ADAM_B2 ** ADAM_STEP)
    return -ADAM_LR * (m_hat / (jnp.sqrt(v_hat) + ADAM_EPS) + ADAM_WD * w), nm, nv


def _adamw(name, w, g, m, v, rows_per_step, first=0, count=None, dests=None, deps=(), small=()):
    layers, rows, cols = w.shape
    count = layers if count is None else count
    dests = () if dests is None else tuple(dests)
    n_in = 4 + len(dests) + len(deps)
    small_shapes = _small_shapes(small[4].shape) if small else []

    def body(*refs):
        w_ref, g_ref, m_ref, v_ref = refs[:4]
        d_ref, nm_ref, nv_ref, g_out_ref = refs[n_in + len(small):n_in + len(small) + 4]
        d_ref[...], nm_ref[...], nv_ref[...] = _adamw_math(w_ref[...], g_ref[...], m_ref[...], v_ref[...])
        g_out_ref[...] = g_ref[...]
        if small:
            @pl.when((pl.program_id(0) == 0) & (pl.program_id(1) == 0))
            def _():
                _adamw_small(*refs[n_in:n_in + len(small)], *refs[n_in + len(small) + 4:])

    spec = pl.BlockSpec((1, rows_per_step, cols), lambda l, i: (first + l, i, 0))
    whole = lambda shape: pl.BlockSpec(shape, lambda l, i: (0,) * len(shape))
    shape = jax.ShapeDtypeStruct(w.shape, F32)
    return pl.pallas_call(
        body, name=name, grid=(count, rows // rows_per_step),
        in_specs=[spec] * 4 + [ANY] * (len(dests) + len(deps)) + [whole(a.shape) for a in small],
        out_specs=[spec] * 4 + [whole(s) for s in small_shapes],
        out_shape=[shape] * 4 + [jax.ShapeDtypeStruct(s, F32) for s in small_shapes],
        input_output_aliases={4 + k: k for k in range(len(dests))},
        scratch_shapes=[pltpu.VMEM((MISC_ROWS, 128), F32)] * 3 if small else [],
        compiler_params=_params(("arbitrary", "arbitrary")),
    )(w, g, m, v, *dests, *deps, *small)


def _pack_misc(pool_scale, sinks, norm_pre, norm_post):
    sink_rows = jnp.zeros((DEPTH, 8, 128), F32).at[:, 0, 0:N_HEADS].set(sinks).reshape(2 * 8, 128)
    return jnp.concatenate([pool_scale.reshape(8, 128), norm_pre.reshape(16, 128), norm_post.reshape(16, 128),
                            sink_rows, jnp.zeros((8, 128), F32)], axis=0)


def _adamw_small(w_ref, g_ref, m_ref, v_ref, pw_ref, pg_ref, pm_ref, pv_ref, *rest):
    outs, pool_outs, (d_ref, nm_ref, nv_ref) = rest[:17], rest[17:21], rest[21:]
    pool_outs[0][...] = pg_ref[...]
    pool_outs[1][...], pool_outs[2][...], pool_outs[3][...] = _adamw_math(
        pw_ref[...], pg_ref[...], pm_ref[...], pv_ref[...])
    d_ref[...], nm_ref[...], nv_ref[...] = _adamw_math(w_ref[...], g_ref[...], m_ref[...], v_ref[...])
    for k, src in enumerate([g_ref, d_ref, nm_ref, nv_ref]):
        scale, sinks, pre, post = outs[4 * k:4 * k + 4]
        for l in range(DEPTH):
            for j in range(4):
                scale[l:l + 1, j * 128:(j + 1) * 128] = src[MISC_SCALE + 4 * l + j:MISC_SCALE + 4 * l + j + 1, :]
            for j in range(8):
                pre[l:l + 1, j * 128:(j + 1) * 128] = src[MISC_PRE + 8 * l + j:MISC_PRE + 8 * l + j + 1, :]
                post[l:l + 1, j * 128:(j + 1) * 128] = src[MISC_POST + 8 * l + j:MISC_POST + 8 * l + j + 1, :]
            sinks[l:l + 1, :] = src[MISC_SINKS + 8 * l:MISC_SINKS + 8 * l + 1, 0:N_HEADS]
    outs[16][...] = g_ref[MISC_LOSS:MISC_LOSS + 1, 0:1]


def _small_shapes(pool_shape):
    return [(DEPTH, D_POOL), (DEPTH, N_HEADS), (DEPTH, D), (DEPTH, D)] * 4 + [(1, 1)] + [pool_shape] * 4


def kernel(x, w_in, pool_w, pool_scale, attn_sinks, w_out, norm_pre, norm_post, loss_target, m_w_in, m_pool_w, m_pool_scale, m_attn_sinks, m_w_out, m_norm_pre, m_norm_post, v_w_in, v_pool_w, v_pool_scale, v_attn_sinks, v_w_out, v_norm_pre, v_norm_post):
    cx, cy, cc = _place()
    chip_arr = jnp.reshape(2 * cx + cy, (1,)).astype(jnp.int32)
    place_arr = jnp.stack([4 * cx + 2 * cy + cc, cc]).astype(jnp.int32)
    t = lambda a: jnp.transpose(a, (0, 2, 1))
    w_in_t = t(w_in)
    xs, target = x[0], loss_target[0]
    pool_w_b = pool_w.astype(BF16)
    tables = _attention_tables()
    scale3 = pool_scale.reshape(DEPTH, 1, D_POOL)
    pre3 = norm_pre.reshape(DEPTH, 1, D)
    post3 = norm_post.reshape(DEPTH, 1, D)

    first = _gather_start_cast("gather_start_first", w_in_t, 0, ID_GATHER_FIRST)
    wi0 = _place_other_half("place_w_in0_rest", w_in_t, place_arr, 0, first[2][0])
    (wi1,) = _place_cast("place_w_in1", w_in_t, chip_arr, 288, [1], deps=(first[3],))
    wo = _place_cast("place_w_out", w_out, chip_arr, 256, [0, 1], deps=(first[3],))
    rest = _gather_start("gather_start_rest", [wi1, wo[0], wo[1]], halved=(0, 1), collective_id=ID_GATHER_REST)
    send, recv, bufs = [first[k] + rest[k] for k in range(3)]
    bufs = [wi0, *bufs[1:]]
    order = {(0, "in"): 0, (1, "in"): 1, (0, "out"): 2, (1, "out"): 3}

    saved = []
    packed = [_pack_misc(pool_scale, attn_sinks, norm_pre, norm_post),
              _pack_misc(m_pool_scale, m_attn_sinks, m_norm_pre, m_norm_post),
              _pack_misc(v_pool_scale, v_attn_sinks, v_norm_pre, v_norm_post)]
    after = (first[3], rest[3], pool_w_b, *tables, scale3, pre3, post3, *packed)
    below = None
    for l in range(DEPTH):
        ks = [order[l, "in"]] + ([] if below is None else [order[l - 1, "out"]])
        halves = _gather_wait(f"gather_wait_in{l}", [bufs[k] for k in ks], [send[k] for k in ks], [recv[k] for k in ks],
                              after, halved=True)
        w_in_l, *w_out_below = _forward_halves(f"forward_w{l}", halves, collective_id=ID_FORWARD[l])
        if below is None:
            pu, pg, q, kv, ag = _fwd_in(l, xs, pre3, w_in_l)
        else:
            saved[l - 1][9] = w_out_below[0]
            y, xs, pu, pg, q, kv, ag = _fwd_in(l, xs, pre3, w_in_l, (below[0], w_out_below[0], below[2]))
            saved[l - 1][7] = y
        cat = _fwd_mix(l, pu, pg, q, kv, ag, pool_w_b, scale3, attn_sinks, tables)
        w_out_l = None
        if l + 1 == DEPTH:
            k = order[l, "out"]
            (w_out_l,) = _gather_wait(f"gather_wait_out{l}", [bufs[k]], [send[k]], [recv[k]], (cat,))
        saved.append([xs, pu, pg, q, kv, ag, cat, None, w_in_l, w_out_l])
        below, after = (cat, None, post3), (cat,)

    x_in, pu, pg, q, kv, ag, cat, y, w_in_l, w_out_l = saved[1]
    dcat, dw_out1, dw_out1_b, dg_post1, loss, xs = _bwd_out(1, cat, w_out_l, post3, place_arr, x=x_in, target=target)
    ex1_out = _exchange_start("exchange_start_out1", [dw_out1_b], ID_OUT1)
    dproj, dpw, dsc1, dsink1 = _bwd_mix(1, pu, pg, q, kv, ag, dcat, pool_w_b, scale3, attn_sinks, tables,
                                        deps=(ex1_out[2][0],))
    dx, dg_pre1, dw_in1, dw_in1_b = _bwd_in_dx(1, dproj, w_in_l, x_in, pre3, xs, dw_place=place_arr)
    ex1_in = _exchange_start("exchange_start_in1", [dw_in1_b], ID_IN1)

    x_in, pu, pg, q, kv, ag, cat, y, w_in_l, w_out_l = saved[0]
    dcat, dw_out0, dw_out0_b, dg_post0 = _bwd_out(0, cat, w_out_l, post3, place_arr, dxn=dx, y=y, deps=(ex1_in[2][0],))
    ex0_out = _exchange_start("exchange_start_out0", [dw_out0_b], ID_OUT0)
    dproj, dpw, dsc0, dsink0 = _bwd_mix(0, pu, pg, q, kv, ag, dcat, pool_w_b, scale3, attn_sinks, tables,
                                        deps=(ex0_out[2][0],), dpw_dest=dpw)
    flat = lambda a: a.reshape(DEPTH * 4 * 128, 128)
    dw_in0, dw_in0_b, dpw_b = _bwd_in_dw(0, dproj, x_in, pre3, place_arr, flat(dpw))
    ex0_in = _exchange_start("exchange_start_in0", [dpw_b, dw_in0_b], ID_IN0)

    grad_x, dg_pre0 = _bwd_in_dx(0, dproj, w_in_l, x_in, pre3, dx, deps=(ex0_in[2][1],))
    small = [jnp.concatenate([dsc0, dsc1, dg_pre0, dg_pre1, dg_post0, dg_post1, dsink0, dsink1, loss], axis=0)]
    ex_small = _exchange_start("exchange_start_small", small, ID_SMALL)
    (recv_out1,) = _exchange_wait("exchange_wait_out1", ex1_out, ex_small[2][0])
    (recv_in1,) = _exchange_wait("exchange_wait_in1", ex1_in, recv_out1)
    (recv_out0,) = _exchange_wait("exchange_wait_out0", ex0_out, recv_in1)
    g_in, g_out = _share(
        "share_a", [lax.empty(w_in_t.shape, F32), lax.empty(w_out.shape, F32)], [(0, 1), (1, 0), (1, 1)],
        collective_id=ID_SHARE_A, summed=[(("half", 0, 1), dw_in1, recv_in1), (("half", 1, 1), dw_out1, recv_out1),
                                          (("half", 1, 0), dw_out0, recv_out0)])
    m_in_t, v_in_t = t(m_w_in), t(v_w_in)
    d_out, nm_out, nv_out, grad_w_out = _adamw("adamw_w_out", w_out, g_out, m_w_out, v_w_out, 256)
    upd_in = _adamw("adamw_w_in1", w_in_t, g_in, m_in_t, v_in_t, 288, first=1, count=1, deps=(d_out,))
    (recv_pw,) = _exchange_wait("exchange_wait_pool", ex0_in, upd_in[0], which=[0])
    (g_pw,) = _sum_small("sum_pool", [flat(dpw)], [recv_pw], place_arr)

    (recv_in0,) = _exchange_wait("exchange_wait_in0", ex0_in, g_pw, which=[1])
    (recv_misc,) = _exchange_wait("exchange_wait_small", ex_small, recv_in0)
    g_in, g_pw, g_misc = _share(
        "share_b", [g_in], [(0, 0)], [g_pw, lax.empty(small[0].shape, F32)], collective_id=ID_SHARE_B,
        summed=[(("half", 0, 0), dw_in0, recv_in0), (("piece", 1), small[0], recv_misc)])
    d_in, nm_in, nv_in, grad_w_in_t, *small_out = _adamw(
        "adamw_w_in0", w_in_t, g_in, m_in_t, v_in_t, 288, first=0, count=1, dests=upd_in,
        small=(packed[0], g_misc, packed[1], packed[2], flat(pool_w), g_pw, flat(m_pool_w), flat(v_pool_w)))
    (g_sc, g_sk, g_pre, g_post, d_sc, d_sk, d_pre, d_post,
     m_sc, m_sk, m_pre, m_post, v_sc, v_sk, v_pre, v_post, loss_sum) = small_out[:17]
    g_pw, d_pw, m_pw, v_pw = [a.reshape(pool_w.shape) for a in small_out[17:]]
    return (loss_sum[0, 0], grad_x[None], t(grad_w_in_t), g_pw, g_sc, g_sk, grad_w_out, g_pre, g_post,
            t(d_in), d_pw, d_sc, d_sk, d_out, d_pre, d_post,
            t(nm_in), m_pw, m_sc, m_sk, nm_out, m_pre, m_post,
            t(nv_in), v_pw, v_sc, v_sk, nv_out, v_pre, v_post)
```

```python
import jax
import jax.numpy as jnp
from jax import lax
from jax.experimental import pallas as pl
from jax.experimental.pallas import tpu as pltpu

F32 = jnp.float32
BF16 = jnp.bfloat16

S = 2048
D = 1024
DEPTH = 2
D_POOL = 512
POOL_WINDOWS = (2, 4, 8, 16)
N_HEADS = 8
D_IN = 2304
N_SHARDS = 4
W_IN_SHARD = D_IN // N_SHARDS
W_OUT_SHARD = D // N_SHARDS
BLK = 128
NB = S // BLK
HALO = 16
PAD = 8
EPS = 1e-6
NEG_INF = -1e30
C_PU, C_PG, C_Q, C_K, C_V, C_AG = 0, 512, 1024, 1536, 1664, 1792

ADAM_LR = 0.001
ADAM_B1 = 0.9
ADAM_B2 = 0.999
ADAM_EPS = 1e-08
ADAM_WD = 0.01
ADAM_STEP = 10

TM = 512
VMEM_LIMIT = 56 * 1024 * 1024

NT = (((1,), (1,)), ((), ()))
TN = (((0,), (0,)), ((), ()))

MESH = pl.DeviceIdType.MESH
ANY = pl.BlockSpec(memory_space=pl.ANY)

ID_FORWARD = (0, 1)
(ID_SHARE_A, ID_SHARE_B, ID_GATHER_FIRST, ID_GATHER_REST, ID_OUT1, ID_IN1, ID_OUT0, ID_IN0, ID_SMALL) = range(2, 11)

MISC_SCALE, MISC_PRE, MISC_POST, MISC_SINKS, MISC_LOSS = 0, 8, 24, 40, 56
MISC_ROWS = 64


def _params(sem=("arbitrary",)):
    return pltpu.CompilerParams(dimension_semantics=sem, vmem_limit_bytes=VMEM_LIMIT)


def _sigmoid(v):
    return 1.0 / (1.0 + jnp.exp(-v))


def _rows8(v):
    r, c = v.shape
    return v.reshape(r // 8, 8, c).sum(axis=0)


def _layer(l, *shape):
    zeros = (0,) * len(shape)
    return pl.BlockSpec((None,) + shape, lambda i: (l,) + zeros)


def _whole(shape):
    zeros = (0,) * len(shape)
    return pl.BlockSpec(shape, lambda i: zeros, pipeline_mode=pl.Buffered(1))


def _fwd_in(l, x, g_pre, w_in_t, below=None):
    fused = below is not None

    def body(x_ref, g_ref, w_ref, *rest):
        if fused:
            cat_ref, wo_ref, gp_ref, y_ref, xn_ref = rest[:5]
            y = jnp.dot(cat_ref[...], wo_ref[...], preferred_element_type=F32)
            y_ref[...] = y
            xt = x_ref[...] + y * lax.rsqrt(jnp.mean(y * y, axis=-1, keepdims=True) + EPS) * gp_ref[...]
            xn_ref[...] = xt
        else:
            xt = x_ref[...]
        pu_ref, pg_ref, q_ref, kv_ref, ag_ref = rest[-5:]
        r = lax.rsqrt(jnp.mean(xt * xt, axis=-1, keepdims=True) + EPS)
        h = (xt * r * g_ref[...]).astype(BF16)

        def proj(lo, hi):
            return lax.dot_general(h, w_ref[lo:hi, :], NT, preferred_element_type=F32)

        pu_ref[...] = proj(C_PU, C_PG)
        pg_ref[...] = proj(C_PG, C_Q)
        q_ref[...] = proj(C_Q, C_K).astype(BF16)
        kv_ref[...] = proj(C_K, C_AG).astype(BF16)
        ag_ref[...] = proj(C_AG, D_IN)

    row = lambda w: pl.BlockSpec((TM, w), lambda i: (i, 0))
    act = jax.ShapeDtypeStruct((S, D), F32)
    return pl.pallas_call(
        body, name="fwd_out_in" if fused else "fwd_in", grid=(S // TM,),
        in_specs=[row(D), _layer(l, 1, D), _whole((D_IN, D))]
        + ([row(D), _whole((D, D)), _layer(l - 1, 1, D)] if fused else []),
        out_specs=[row(D)] * (2 * fused) + [row(512), row(512), row(512), row(256), row(512)],
        out_shape=[act] * (2 * fused)
        + [jax.ShapeDtypeStruct((S, 512), F32), jax.ShapeDtypeStruct((S, 512), F32),
           jax.ShapeDtypeStruct((S, 512), BF16), jax.ShapeDtypeStruct((S, 256), BF16),
           jax.ShapeDtypeStruct((S, 512), F32)],
        compiler_params=_params(),
    )(x, g_pre, w_in_t, *(below if fused else ()))


LOG2E = 1.4426950408889634
SCORE_SCALE = 0.125 * LOG2E


def _attention_tables():
    qi = jnp.arange(BLK)[:, None]
    kj = jnp.arange(BLK)[None, :]
    dist = ((qi - kj) % BLK).astype(F32)
    slopes = jnp.exp2(-jnp.arange(1, N_HEADS + 1, dtype=F32))
    bias = -(slopes * LOG2E)[:, None, None] * dist[None]
    first = jnp.where(kj > qi, NEG_INF, bias)
    return jnp.stack([first, bias]), (kj <= qi).astype(BF16)


def _own_block_mask():
    return lax.broadcasted_iota(jnp.int32, (BLK, BLK), 1) <= lax.broadcasted_iota(jnp.int32, (BLK, BLK), 0)


def _merge(full, own):
    return jnp.where(own, full[:, BLK:], full[:, :BLK])


def _spread(v, tri):
    own = v * tri
    return jnp.concatenate([v - own, own], axis=1)


def _head_variants(cur, prev):
    both = jnp.concatenate([prev, cur], axis=0).astype(F32)
    swapped = pltpu.roll(both, 64, axis=1)
    low = lax.broadcasted_iota(jnp.int32, both.shape, 1) < 64
    zero = jnp.zeros_like(both)
    return ((jnp.where(low, both, zero).astype(BF16), jnp.where(low, zero, swapped).astype(BF16)),
            (jnp.where(low, swapped, zero).astype(BF16), jnp.where(low, zero, both).astype(BF16)))


def _head_of(hkv, t, half):
    return hkv * 4 + 2 * t + half


def _rows(v, t):
    return v[t * BLK:(t + 1) * BLK]


def _stack_tiles(ref, hkv, offset=0):
    lo = offset + 2 * hkv * 128
    return jnp.concatenate([ref[:, lo:lo + 128], ref[:, lo + 128:lo + 256]], axis=0)


def _scores(q2, k_var, own):
    s = {}
    for hkv in range(2):
        for half in range(2):
            full = lax.dot_general(q2[hkv], k_var[hkv][half], NT, preferred_element_type=F32)
            for t in range(2):
                s[hkv, t, half] = _merge(_rows(full, t), own)
    return s


def _softmax(s, bias, sink):
    s = s * SCORE_SCALE + bias
    sink2 = sink * LOG2E
    m = jnp.maximum(jnp.max(s, axis=-1, keepdims=True), sink2)
    p = jnp.exp2(s - m)
    e_sink = jnp.exp2(sink2 - m)
    inv = 1.0 / (jnp.sum(p, axis=-1, keepdims=True) + e_sink)
    return p * inv, e_sink * inv


def _spread_pair(v, hkv, half, tri):
    return jnp.concatenate([_spread(v[hkv, t, half].astype(BF16), tri) for t in range(2)], axis=0)


POOL_ROWS = PAD + HALO + BLK


def _window_sums(src_ref, tmp_refs, trailing):
    lo, hi = (PAD, POOL_ROWS) if trailing else (0, HALO + BLK)
    cur = src_ref
    for level in range(len(POOL_WINDOWS)):
        lanes = slice(level * 128, 512)
        shift = -(1 << level) if trailing else (1 << level)
        dst = tmp_refs[level % 2]
        dst[lo:hi, lanes] = cur[lo:hi, lanes] + cur[lo + shift:hi + shift, lanes]
        cur = dst


def _pool_block(ext_ref, tmp_refs, i, g, w):
    lanes = slice(g * 128, (g + 1) * 128)
    rows = slice(PAD + HALO, POOL_ROWS)
    t = (i * BLK + lax.broadcasted_iota(jnp.int32, (BLK, 1), 0)).astype(F32)
    inv = 1.0 / jnp.minimum(t + 1.0, float(w))
    return tmp_refs[g % 2][rows, lanes] * inv - ext_ref[rows, lanes], inv


def _fwd_mix(l, pu, pg, q, kv, ag, pool_w, pool_scale, sinks, tables):
    bias, tri = tables

    def body(pu_ref, pup_ref, pg_ref, q_ref, kv_ref, kvp_ref, ag_ref, pw_ref, sc_ref, sink_ref, bias_ref, tri_ref,
             cat_ref, ext_ref, *tmp_refs):
        i = pl.program_id(0)

        @pl.when(i == 0)
        def _():
            for ref in (ext_ref, *tmp_refs):
                ref[0:PAD, :] = jnp.zeros((PAD, 512), F32)

        ext_ref[PAD:PAD + HALO, :] = jnp.where(i > 0, pup_ref[...], 0.0)
        ext_ref[PAD + HALO:POOL_ROWS, :] = pu_ref[...]
        _window_sums(ext_ref, tmp_refs, True)
        for g, w in enumerate(POOL_WINDOWS):
            lanes = slice(g * 128, (g + 1) * 128)
            pooled, _ = _pool_block(ext_ref, tmp_refs, i, g, w)
            mixed = jnp.dot(pooled.astype(BF16), pw_ref[g], preferred_element_type=F32)
            gate = pg_ref[:, lanes]
            cat_ref[:, lanes] = (mixed * sc_ref[:, lanes] * (gate * _sigmoid(gate))).astype(BF16)

        own = _own_block_mask()
        tri = tri_ref[...]
        k_var = _head_variants(kv_ref[:, 0:128], kvp_ref[:, 0:128])
        v_var = _head_variants(kv_ref[:, 128:256], kvp_ref[:, 128:256])
        s = _scores([_stack_tiles(q_ref, hkv) for hkv in range(2)], k_var, own)
        p = {}
        for (hkv, t, half), s_head in s.items():
            head = _head_of(hkv, t, half)
            p[hkv, t, half], _ = _softmax(s_head, bias_ref[head], sink_ref[l, head])
        for hkv in range(2):
            o2 = jnp.zeros((2 * BLK, 128), F32)
            for half in range(2):
                o2 = o2 + jnp.dot(_spread_pair(p, hkv, half, tri), v_var[hkv][half], preferred_element_type=F32)
            for t in range(2):
                lo = (2 * hkv + t) * 128
                gate = ag_ref[:, lo:lo + 128]
                cat_ref[:, D_POOL + lo:D_POOL + lo + 128] = (_rows(o2, t) * (gate * _sigmoid(gate))).astype(BF16)

    blk = lambda w: pl.BlockSpec((BLK, w), lambda i: (i, 0))
    prev = lambda w: pl.BlockSpec((BLK, w), lambda i: (jnp.maximum(i - 1, 0), 0))
    halo = pl.BlockSpec((HALO, 512), lambda i: (jnp.maximum(i * (BLK // HALO) - 1, 0), 0))
    return pl.pallas_call(
        body, name="fwd_mix", grid=(NB,),
        in_specs=[blk(512), halo, blk(512), blk(512), blk(256), prev(256), blk(512),
                  _layer(l, 4, 128, 128), _layer(l, 1, 512), pl.BlockSpec(memory_space=pltpu.SMEM),
                  pl.BlockSpec((None, N_HEADS, BLK, BLK), lambda i: (jnp.minimum(i, 1), 0, 0, 0)), _whole((BLK, BLK))],
        out_specs=blk(D),
        out_shape=jax.ShapeDtypeStruct((S, D), BF16),
        scratch_shapes=[pltpu.VMEM((POOL_ROWS, 512), F32)] * 3,
        compiler_params=_params(),
    )(pu, pu, pg, q, kv, kv, ag, pool_w, pool_scale, sinks, bias, tri)


def _store_lane_rows(ref, acc):
    total = jnp.sum(acc, axis=0, keepdims=True)
    for k in range(ref.shape[0]):
        ref[k:k + 1, :] = total[:, k * 128:(k + 1) * 128]


def _own_piece(dw_ref, place_ref):
    p = dw_ref.shape[0] // 8
    return dw_ref[pl.ds(pl.multiple_of(place_ref[0] * p, 8), p), :]


def _bwd_out(l, cat, w_out, g_post, place_arr, dxn=None, y=None, x=None, target=None, deps=()):
    last = target is not None
    n_steps = S // TM

    def body(a_ref, b_ref, g_ref, cat_ref, w_ref, place_ref, *rest):
        dcat_ref, own_ref, dwb_ref, dg_ref = rest[len(deps):len(deps) + 4]
        rest = rest[len(deps) + 4:]
        acc_ref, dw_ref = rest[-2:]
        step = pl.program_id(0)

        @pl.when(step == 0)
        def _():
            dw_ref[...] = jnp.zeros_like(dw_ref)
            acc_ref[...] = jnp.zeros_like(acc_ref)

        cat = cat_ref[...]
        g = g_ref[...]
        y = jnp.dot(cat, w_ref[...], preferred_element_type=F32) if last else b_ref[...]
        r = lax.rsqrt(jnp.mean(y * y, axis=-1, keepdims=True) + EPS)
        if last:
            loss_ref, dx_ref, loss_acc_ref = rest[:3]
            err = a_ref[...] + y * r * g - b_ref[...]

            @pl.when(step == 0)
            def _():
                loss_acc_ref[...] = jnp.zeros_like(loss_acc_ref)

            loss_acc_ref[...] += _rows8(err * err)
            dz = err * (1.0 / D)
            dx_ref[...] = dz
        else:
            dz = a_ref[...]
        a = dz * g
        dy = r * a - y * (r * r * r) * jnp.mean(a * y, axis=-1, keepdims=True)
        acc_ref[...] += _rows8(dz * (y * r))
        dyb = dy.astype(BF16)
        dcat_ref[...] = lax.dot_general(dyb, w_ref[...], NT, preferred_element_type=F32)
        dw_ref[...] += lax.dot_general(cat, dyb, TN, preferred_element_type=F32)

        @pl.when(step == n_steps - 1)
        def _():
            _store_lane_rows(dg_ref, acc_ref[...])
            dwb_ref[...] = dw_ref[...].astype(BF16)
            own_ref[...] = _own_piece(dw_ref, place_ref)
            if last:
                loss_ref[...] = jnp.full((8, 128), (0.5 / D) * jnp.sum(loss_acc_ref[...]), F32)

    row = lambda: pl.BlockSpec((TM, D), lambda i: (i, 0))
    full = _whole
    return pl.pallas_call(
        body, name="out_loss_bwd" if last else "bwd_out", grid=(n_steps,),
        in_specs=[row(), row(), _layer(l, 1, D), row(), full((D, D)), pl.BlockSpec(memory_space=pltpu.SMEM)]
        + [ANY] * len(deps),
        out_specs=[row(), full((D // 8, D)), full((D, D)), full((8, 128))] + ([full((8, 128)), row()] if last else []),
        out_shape=[jax.ShapeDtypeStruct((S, D), F32), jax.ShapeDtypeStruct((D // 8, D), F32),
                   jax.ShapeDtypeStruct((D, D), BF16), jax.ShapeDtypeStruct((8, 128), F32)]
        + ([jax.ShapeDtypeStruct((8, 128), F32), jax.ShapeDtypeStruct((S, D), F32)] if last else []),
        scratch_shapes=([pltpu.VMEM((8, D), F32)] if last else []) + [pltpu.VMEM((8, D), F32), pltpu.VMEM((D, D), F32)],
        compiler_params=_params(),
    )(*((x, target) if last else (dxn, y)), g_post, cat, w_out, place_arr, *deps)


def _bwd_mix(l, pu, pg, q, kv, ag, dcat, pool_w, pool_scale, sinks, tables, deps=(), dpw_dest=None):
    bias, tri = tables
    deps = tuple(deps) + (() if dpw_dest is None else (dpw_dest,))

    def body(pu_ref, pup_ref, pg_ref, q_ref, kv_ref, kvp_ref, ag_ref, dcat_ref, pw_ref, sc_ref, sink_ref, bias_ref,
             tri_ref, *rest):
        dproj_ref, dpw_ref, dsc_ref, dsink_ref, ext_ref, dext_ref, tmp_a, tmp_b, dkv_ref = rest[len(deps):]
        tmp_refs = (tmp_a, tmp_b)
        step = pl.program_id(0)
        i = NB - 1 - step

        @pl.when(step == 0)
        def _():
            dpw_ref[...] = jnp.zeros_like(dpw_ref)
            dsc_ref[...] = jnp.zeros_like(dsc_ref)
            dsink_ref[...] = jnp.zeros_like(dsink_ref)
            for ref in (ext_ref, tmp_a, tmp_b):
                ref[0:PAD, :] = jnp.zeros((PAD, 512), F32)
            dext_ref[BLK:POOL_ROWS, :] = jnp.zeros((HALO + PAD, 512), F32)
            dkv_ref[...] = jnp.zeros_like(dkv_ref)

        ext_ref[PAD:PAD + HALO, :] = jnp.where(i > 0, pup_ref[...], 0.0)
        ext_ref[PAD + HALO:POOL_ROWS, :] = pu_ref[...]
        _window_sums(ext_ref, tmp_refs, True)
        dpooled = []
        for g, w in enumerate(POOL_WINDOWS):
            lanes = slice(g * 128, (g + 1) * 128)
            pooled, inv = _pool_block(ext_ref, tmp_refs, i, g, w)
            pooled_b = pooled.astype(BF16)
            mixed = jnp.dot(pooled_b, pw_ref[g], preferred_element_type=F32)
            scale = sc_ref[:, lanes]
            gate = pg_ref[:, lanes]
            sg = _sigmoid(gate)
            dpo = dcat_ref[:, lanes]
            dproj_ref[:, C_PG + g * 128:C_PG + (g + 1) * 128] = (
                dpo * (mixed * scale) * (sg * (1.0 + gate * (1.0 - sg)))).astype(BF16)
            dms = dpo * (gate * sg)
            dsc_ref[g:g + 1, :] += jnp.sum(dms * mixed, axis=0, keepdims=True)
            dmixed = (dms * scale).astype(BF16)
            dpw_ref[g] += lax.dot_general(pooled_b, dmixed, TN, preferred_element_type=F32)
            dpooled.append(lax.dot_general(dmixed, pw_ref[g], NT, preferred_element_type=F32))
            dext_ref[0:BLK, lanes] = dpooled[g] * inv
        _window_sums(dext_ref, tmp_refs, False)
        for g in range(len(POOL_WINDOWS)):
            lanes = slice(g * 128, (g + 1) * 128)
            dproj_ref[:, C_PU + g * 128:C_PU + (g + 1) * 128] = (tmp_refs[g % 2][0:BLK, lanes] - dpooled[g]).astype(BF16)
        dext_ref[BLK:BLK + HALO, :] = dext_ref[0:HALO, :]

        own = _own_block_mask()
        tri = tri_ref[...]
        k_var = _head_variants(kv_ref[:, 0:128], kvp_ref[:, 0:128])
        v_var = _head_variants(kv_ref[:, 128:256], kvp_ref[:, 128:256])
        q2 = [_stack_tiles(q_ref, hkv) for hkv in range(2)]
        s = _scores(q2, k_var, own)
        p, p_sink = {}, {}
        for key, s_head in s.items():
            head = _head_of(*key)
            p[key], p_sink[key] = _softmax(s_head, bias_ref[head], sink_ref[l, head])

        do2, p_b, dp = [], {}, {}
        for hkv in range(2):
            gate = _stack_tiles(ag_ref, hkv)
            sg = _sigmoid(gate)
            dca = _stack_tiles(dcat_ref, hkv, D_POOL)
            do2.append((dca * (gate * sg)).astype(BF16))
            o2 = jnp.zeros((2 * BLK, 128), F32)
            for half in range(2):
                p_b[hkv, half] = _spread_pair(p, hkv, half, tri)
                o2 = o2 + jnp.dot(p_b[hkv, half], v_var[hkv][half], preferred_element_type=F32)
                full = lax.dot_general(do2[hkv], v_var[hkv][half], NT, preferred_element_type=F32)
                for t in range(2):
                    dp[hkv, t, half] = _merge(_rows(full, t), own)
            dag = dca * o2 * (sg * (1.0 + gate * (1.0 - sg)))
            for t in range(2):
                lo = C_AG + (2 * hkv + t) * 128
                dproj_ref[:, lo:lo + 128] = _rows(dag, t).astype(BF16)

        ds = {}
        for key in p:
            delta = jnp.sum(p[key] * dp[key], axis=-1, keepdims=True)
            ds[key] = p[key] * (dp[key] - delta)
            head = _head_of(*key)
            dsink_ref[0:1, :] += jnp.where(lax.broadcasted_iota(jnp.int32, (1, 128), 1) == head,
                                           -jnp.sum(p_sink[key] * delta, axis=0, keepdims=True), 0.0)

        dk_acc = [[None, None], [None, None]]
        dv_acc = [[None, None], [None, None]]
        for hkv in range(2):
            dq2 = jnp.zeros((2 * BLK, 128), F32)
            for half in range(2):
                ds_b = _spread_pair(ds, hkv, half, tri)
                dq2 = dq2 + jnp.dot(ds_b, k_var[hkv][half], preferred_element_type=F32)
                dk_acc[hkv][half] = lax.dot_general(ds_b, q2[hkv], TN, preferred_element_type=F32)
                dv_acc[hkv][half] = lax.dot_general(p_b[hkv, half], do2[hkv], TN, preferred_element_type=F32)
            for t in range(2):
                lo = C_Q + (2 * hkv + t) * 128
                dproj_ref[:, lo:lo + 128] = (_rows(dq2, t) * 0.125).astype(BF16)

        low = lax.broadcasted_iota(jnp.int32, (2 * BLK, 128), 1) < 64

        def gather_heads(acc):
            return jnp.where(low, acc[0][0] + pltpu.roll(acc[0][1], 64, axis=1),
                             pltpu.roll(acc[1][0], 64, axis=1) + acc[1][1])

        dk = gather_heads(dk_acc) * 0.125
        dv = gather_heads(dv_acc)
        dproj_ref[:, C_K:C_V] = (dk[BLK:, :] + dkv_ref[:, 0:128]).astype(BF16)
        dproj_ref[:, C_V:C_AG] = (dv[BLK:, :] + dkv_ref[:, 128:256]).astype(BF16)
        dkv_ref[:, 0:128] = dk[:BLK, :]
        dkv_ref[:, 128:256] = dv[:BLK, :]

    rev = lambda w: pl.BlockSpec((BLK, w), lambda s: (NB - 1 - s, 0))
    prev = lambda w: pl.BlockSpec((BLK, w), lambda s: (jnp.maximum(NB - 2 - s, 0), 0))
    halo = pl.BlockSpec((HALO, 512), lambda s: (jnp.maximum((NB - 1 - s) * (BLK // HALO) - 1, 0), 0))
    return pl.pallas_call(
        body, name="bwd_mix", grid=(NB,),
        in_specs=[rev(512), halo, rev(512), rev(512), rev(256), prev(256), rev(512), rev(D),
                  _layer(l, 4, 128, 128), _layer(l, 1, 512), pl.BlockSpec(memory_space=pltpu.SMEM),
                  pl.BlockSpec((None, N_HEADS, BLK, BLK), lambda s: (jnp.minimum(NB - 1 - s, 1), 0, 0, 0)),
                  _whole((BLK, BLK))] + [ANY] * len(deps),
        out_specs=[rev(D_IN), _layer(l, 4, 128, 128),
                   pl.BlockSpec((4, 128), lambda s: (0, 0)), pl.BlockSpec((8, 128), lambda s: (0, 0))],
        out_shape=[jax.ShapeDtypeStruct((S, D_IN), BF16), jax.ShapeDtypeStruct((DEPTH, 4, 128, 128), F32),
                   jax.ShapeDtypeStruct((4, 128), F32), jax.ShapeDtypeStruct((8, 128), F32)],
        input_output_aliases={} if dpw_dest is None else {12 + len(deps): 1},
        scratch_shapes=[pltpu.VMEM((POOL_ROWS, 512), F32)] * 4 + [pltpu.VMEM((BLK, 256), F32)],
        compiler_params=_params(),
    )(pu, pu, pg, q, kv, kv, ag, dcat, pool_w, pool_scale, sinks, bias, tri, *deps)


def _bwd_in_dw(l, dproj, x, g_pre, place_arr, to_bf16, deps=()):
    n_steps = S // TM

    def body(dp_ref, x_ref, g_ref, place_ref, *rest):
        f32_ref, own_ref, dwb_ref, bf16_ref, dw_ref = rest[len(deps):]
        step = pl.program_id(0)

        @pl.when(step == 0)
        def _():
            dw_ref[...] = jnp.zeros_like(dw_ref)
            bf16_ref[...] = f32_ref[...].astype(BF16)

        xt = x_ref[...]
        r = lax.rsqrt(jnp.mean(xt * xt, axis=-1, keepdims=True) + EPS)
        h = (xt * r * g_ref[...]).astype(BF16)
        dw_ref[...] += lax.dot_general(dp_ref[...], h, TN, preferred_element_type=F32)

        @pl.when(step == n_steps - 1)
        def _():
            dwb_ref[...] = dw_ref[...].astype(BF16)
            own_ref[...] = _own_piece(dw_ref, place_ref)

    row = lambda w: pl.BlockSpec((TM, w), lambda i: (i, 0))
    full = _whole
    return pl.pallas_call(
        body, name="bwd_in_dw", grid=(n_steps,),
        in_specs=[row(D_IN), row(D), _layer(l, 1, D), pl.BlockSpec(memory_space=pltpu.SMEM)] + [ANY] * len(deps)
        + [full(to_bf16.shape)],
        out_specs=[full((D_IN // 8, D)), full((D_IN, D)), full(to_bf16.shape)],
        out_shape=[jax.ShapeDtypeStruct((D_IN // 8, D), F32), jax.ShapeDtypeStruct((D_IN, D), BF16),
                   jax.ShapeDtypeStruct(to_bf16.shape, BF16)],
        scratch_shapes=[pltpu.VMEM((D_IN, D), F32)],
        compiler_params=_params(),
    )(dproj, x, g_pre, place_arr, *deps, to_bf16)


def _bwd_in_dx(l, dproj, w_in_t, x, g_pre, dres, deps=(), dw_place=None):
    n_steps = S // TM
    with_dw = dw_place is not None

    def body(dp_ref, w_ref, x_ref, g_ref, dres_ref, *rest):
        place_ref = rest[0] if with_dw else None
        rest = rest[with_dw + len(deps):]
        if with_dw:
            dx_ref, dg_ref, own_ref, dwb_ref, acc_ref, dw_ref = rest
        else:
            dx_ref, dg_ref, acc_ref = rest
        step = pl.program_id(0)

        @pl.when(step == 0)
        def _():
            acc_ref[...] = jnp.zeros_like(acc_ref)
            if with_dw:
                dw_ref[...] = jnp.zeros_like(dw_ref)

        g = g_ref[...]
        halves = [slice(k * (TM // 2), (k + 1) * (TM // 2)) for k in range(2)]
        dh = [jnp.dot(dp_ref[rows, :], w_ref[...], preferred_element_type=F32) for rows in halves]
        h = []
        for rows, dh_k in zip(halves, dh):
            xt = x_ref[rows, :]
            r = lax.rsqrt(jnp.mean(xt * xt, axis=-1, keepdims=True) + EPS)
            xn = xt * r
            acc_ref[...] += _rows8(dh_k * xn)
            a = dh_k * g
            dx_ref[rows, :] = dres_ref[rows, :] + (
                r * a - xt * (r * r * r) * jnp.mean(a * xt, axis=-1, keepdims=True))
            h.append((xn * g).astype(BF16))
        if with_dw:
            dw_ref[...] += lax.dot_general(dp_ref[...], jnp.concatenate(h, axis=0), TN, preferred_element_type=F32)

        @pl.when(step == n_steps - 1)
        def _():
            _store_lane_rows(dg_ref, acc_ref[...])
            if with_dw:
                dwb_ref[...] = dw_ref[...].astype(BF16)
                own_ref[...] = _own_piece(dw_ref, place_ref)

    row = lambda w: pl.BlockSpec((TM, w), lambda i: (i, 0))
    full = _whole
    dw_specs = [full((D_IN // 8, D)), full((D_IN, D))] if with_dw else []
    dw_shapes = [jax.ShapeDtypeStruct((D_IN // 8, D), F32), jax.ShapeDtypeStruct((D_IN, D), BF16)] if with_dw else []
    return pl.pallas_call(
        body, name="bwd_in" if with_dw else "bwd_in_dx", grid=(n_steps,),
        in_specs=[row(D_IN), full((D_IN, D)), row(D), _layer(l, 1, D), row(D)]
        + [pl.BlockSpec(memory_space=pltpu.SMEM)] * with_dw + [ANY] * len(deps),
        out_specs=[row(D), full((8, 128))] + dw_specs,
        out_shape=[jax.ShapeDtypeStruct((S, D), F32), jax.ShapeDtypeStruct((8, 128), F32)] + dw_shapes,
        scratch_shapes=[pltpu.VMEM((8, D), F32)] + [pltpu.VMEM((D_IN, D), F32)] * with_dw,
        compiler_params=_params(),
    )(dproj, w_in_t, x, g_pre, dres, *((dw_place,) if with_dw else ()), *deps)


HBM =pl.BlockSpec(memory_space=pltpu.HBM)
SEM = pl.BlockSpec(memory_space=pltpu.SEMAPHORE)
def _split_copy(collective_id=None):
    return pltpu.CompilerParams(has_side_effects=pltpu.SideEffectType.DATAFLOW_SIDE_EFFECTING,
                                collective_id=collective_id)


SPLIT_COPY = _split_copy()


def _in_hbm(a):
    return pltpu.with_memory_space_constraint(a, pltpu.HBM)

def _place():
    return lax.axis_index("x"), lax.axis_index("y"), lax.axis_index("c")


def _other_chips(x, y):
    return [(1 - x, y), (x, 1 - y), (1 - x, 1 - y)]


def _peer(x, y, c, m):
    return (x ^ (m >> 2), y ^ ((m >> 1) & 1), c ^ (m & 1))


SAME_CORE = (2, 4, 6)


def _place_cast(name, src, chip_arr, tile, layers, deps=()):
    _, n, cols = src.shape
    steps = n // tile
    k = len(layers)

    def body(chip_ref, *refs):
        for s_ref, o_ref in zip(refs[:k], refs[k + len(deps):]):
            o_ref[...] = s_ref[...].astype(BF16)

    def layer_spec(l):
        return pl.BlockSpec((None, tile, cols), lambda i, chip: (l, i, 0))

    return pl.pallas_call(
        body, name=name,
        grid_spec=pltpu.PrefetchScalarGridSpec(
            num_scalar_prefetch=1, grid=(steps,),
            in_specs=[layer_spec(l) for l in layers] + [ANY] * len(deps),
            out_specs=[pl.BlockSpec((tile, cols), lambda i, chip: (chip[0] * steps + i, 0))] * k),
        out_shape=[jax.ShapeDtypeStruct((N_SHARDS * n, cols), BF16)] * k,
        compiler_params=_params(),
    )(chip_arr, *[src] * k, *deps)


def _place_other_half(name, src, place_arr, layer, dest):
    _, n, cols = src.shape

    def body(place_ref, s_ref, dest_ref, o_ref):
        o_ref[...] = s_ref[...].astype(BF16)

    return pl.pallas_call(
        body, name=name,
        grid_spec=pltpu.PrefetchScalarGridSpec(
            num_scalar_prefetch=1, grid=(1,),
            in_specs=[pl.BlockSpec((None, n // 2, cols), lambda i, place: (layer, 1 - place[1], 0)), ANY],
            out_specs=pl.BlockSpec((n // 2, cols), lambda i, place: (place[0] + 1 - 2 * place[1], 0))),
        out_shape=jax.ShapeDtypeStruct((N_SHARDS * n, cols), BF16),
        input_output_aliases={2: 0},
        compiler_params=_params(),
    )(place_arr, src, dest)


def _chip_rows(ref, chip, half=None):
    n = ref.shape[0] // N_SHARDS
    if half is None:
        return ref.at[pl.ds(pl.multiple_of(chip * n, 16), n), :]
    return ref.at[pl.ds(pl.multiple_of(chip * n + half * (n // 2), 16), n // 2), :]


def _gather_start(name, bufs, halved, collective_id):
    n = len(bufs)

    def body(*refs):
        ins, send, recv, token = refs[:n], refs[n:2 * n], refs[2 * n:3 * n], refs[-1]
        x, y, c = _place()
        _handshake([(*chip, c) for chip in _other_chips(x, y)])
        for a, buf in enumerate(ins):
            own = _chip_rows(buf, 2 * x + y, c if a in halved else None)
            for j, chip in enumerate(_other_chips(x, y)):
                pltpu.make_async_remote_copy(src_ref=own, dst_ref=own, send_sem=send[a].at[j], recv_sem=recv[a].at[j],
                                             device_id=(*chip, c), device_id_type=MESH).start()
        token[...] = jnp.zeros_like(token)

    outs = pl.pallas_call(
        body, name=name, in_specs=[HBM] * n,
        out_specs=[SEM] * (2 * n) + [HBM] * n + [pl.BlockSpec(memory_space=pltpu.VMEM)],
        out_shape=[pltpu.SemaphoreType.DMA((3,))] * (2 * n) + [pltpu.HBM(b.shape, b.dtype) for b in bufs]
        + [jax.ShapeDtypeStruct((8, 128), F32)],
        input_output_aliases={a: 2 * n + a for a in range(n)},
        compiler_params=_split_copy(collective_id),
    )(*[_in_hbm(b) for b in bufs])
    return outs[:n], outs[n:2 * n], outs[2 * n:3 * n], outs[-1]


def _gather_start_cast(name, src, layer, collective_id):
    _, n, cols = src.shape
    half = n // 2

    def body(src_ref, send, recv, buf_ref, token, f32_ref, bf16_ref, local):
        x, y, c = _place()
        own = _chip_rows(buf_ref, 2 * x + y, c)

        def cast():
            load = pltpu.make_async_copy(src_ref.at[layer, pl.ds(pl.multiple_of(c * half, 16), half), :], f32_ref,
                                         local.at[0])
            load.start()
            load.wait()
            bf16_ref[...] = f32_ref[...].astype(BF16)
            store = pltpu.make_async_copy(bf16_ref, own, local.at[1])
            store.start()
            store.wait()

        _handshake([(*chip, c) for chip in _other_chips(x, y)], cast)
        for j, chip in enumerate(_other_chips(x, y)):
            pltpu.make_async_remote_copy(src_ref=own, dst_ref=own, send_sem=send.at[j], recv_sem=recv.at[j],
                                         device_id=(*chip, c), device_id_type=MESH).start()
        token[...] = jnp.zeros_like(token)

    outs = pl.pallas_call(
        body, name=name, in_specs=[HBM],
        out_specs=[SEM, SEM, HBM, pl.BlockSpec(memory_space=pltpu.VMEM)],
        out_shape=[pltpu.SemaphoreType.DMA((3,))] * 2 + [pltpu.HBM((N_SHARDS * n, cols), BF16),
                                                         jax.ShapeDtypeStruct((8, 128), F32)],
        scratch_shapes=[pltpu.VMEM((half, cols), F32), pltpu.VMEM((half, cols), BF16), pltpu.SemaphoreType.DMA((2,))],
        compiler_params=_split_copy(collective_id),
    )(_in_hbm(src))
    return outs[:1], outs[1:2], outs[2:3], outs[3]


def _gather_wait(name, bufs, send_sems, recv_sems, after, halved=False):
    n = len(bufs)

    def body(*refs):
        x, y, c = _place()
        half = c if halved else None
        for buf_ref, send_ref, recv_ref in zip(refs[:n], refs[n:2 * n], refs[2 * n:3 * n]):
            own = _chip_rows(buf_ref, 2 * x + y, half)
            for j, chip in enumerate(_other_chips(x, y)):
                copy = pltpu.make_async_remote_copy(
                    src_ref=own, dst_ref=_chip_rows(buf_ref, 2 * chip[0] + chip[1], half), send_sem=send_ref.at[j],
                    recv_sem=recv_ref.at[j], device_id=(*chip, c), device_id_type=MESH)
                copy.wait_send()
                copy.wait_recv()

    return pl.pallas_call(
        body, name=name, in_specs=[HBM] * n + [SEM] * (2 * n) + [ANY] * len(after), out_specs=[HBM] * n,
        out_shape=[pltpu.HBM(b.shape, b.dtype) for b in bufs], input_output_aliases={a: a for a in range(n)},
        compiler_params=SPLIT_COPY,
    )(*bufs, *send_sems, *recv_sems, *after)


def _handshake(peers, meanwhile=None):
    barrier = pltpu.get_barrier_semaphore()
    for peer in peers:
        pl.semaphore_signal(barrier, inc=1, device_id=peer, device_id_type=MESH)
    if meanwhile is not None:
        meanwhile()
    pl.semaphore_wait(barrier, len(peers))


def _sibling_handshake(x, y, c):
    _handshake([(x, y, 1 - c)])


def _forward_halves(name, bufs, collective_id):
    n = len(bufs)

    def body(*refs):
        ins, outs, (send_sems, recv_sems) = refs[:n], refs[n:2 * n], refs[2 * n:]
        x, y, c = _place()
        _sibling_handshake(x, y, c)

        def copy(a, j, chip, half):
            rows = 2 * chip[0] + chip[1]
            return pltpu.make_async_remote_copy(
                src_ref=_chip_rows(ins[a], rows, half), dst_ref=_chip_rows(outs[a], rows, half),
                send_sem=send_sems.at[3 * a + j], recv_sem=recv_sems.at[3 * a + j], device_id=(x, y, 1 - c),
                device_id_type=MESH)

        copies = [(a, j, chip) for a in range(n) for j, chip in enumerate(_other_chips(x, y))]
        for a, j, chip in copies:
            copy(a, j, chip, c).start()
        for a, j, chip in copies:
            copy(a, j, chip, c).wait_send()
            copy(a, j, chip, 1 - c).wait_recv()

    return pl.pallas_call(
        body, name=name, in_specs=[ANY] * n, out_specs=[ANY] * n,
        out_shape=[jax.ShapeDtypeStruct(b.shape, b.dtype) for b in bufs],
        input_output_aliases={a: a for a in range(n)},
        scratch_shapes=[pltpu.SemaphoreType.DMA((3 * n,))] * 2,
        compiler_params=pltpu.CompilerParams(collective_id=collective_id),
    )(*bufs)


def _piece_rows(ref, k):
    p = ref.shape[0] // 8
    return ref.at[pl.ds(pl.multiple_of(k * p, 32 // jnp.dtype(ref.dtype).itemsize), p), :]


def _exchange_start(name, arrays, collective_id):
    n = len(arrays)
    zones = [lax.empty((7, a.shape[0] // 8, a.shape[1]), a.dtype) for a in arrays]

    def body(*refs):
        srcs, lands = refs[:n], refs[n:2 * n]
        send, recv = refs[2 * n:3 * n], refs[3 * n:4 * n]
        x, y, c = _place()
        _handshake([_peer(x, y, c, m) for m in range(1, 8)])
        for a, (src, land) in enumerate(zip(srcs, lands)):
            for m in range(1, 8):
                px, py, pc = _peer(x, y, c, m)
                pltpu.make_async_remote_copy(
                    src_ref=_piece_rows(src, 4 * px + 2 * py + pc), dst_ref=land.at[m - 1], send_sem=send[a].at[m - 1],
                    recv_sem=recv[a].at[m - 1], device_id=(px, py, pc), device_id_type=MESH).start()

    outs = pl.pallas_call(
        body, name=name, in_specs=[HBM] * (2 * n), out_specs=[SEM] * (2 * n) + [HBM] * (2 * n),
        out_shape=[pltpu.SemaphoreType.DMA((7,))] * (2 * n) + [pltpu.HBM(a.shape, a.dtype) for a in arrays + zones],
        input_output_aliases={a: 2 * n + a for a in range(2 * n)},
        compiler_params=_split_copy(collective_id),
    )(*[_in_hbm(a) for a in arrays + zones])
    return outs[:n], outs[n:2 * n], outs[2 * n:3 * n], outs[3 * n:4 * n]


def _exchange_wait(name, started, after, which=None):
    which = range(len(started[2])) if which is None else which
    send_sems, recv_sems, arrays, zones = [[group[k] for k in which] for group in started[:4]]
    n = len(arrays)

    def body(*refs):
        srcs, lands = refs[:n], refs[n:2 * n]
        send, recv = refs[2 * n:3 * n], refs[3 * n:4 * n]
        x, y, c = _place()
        for a, (src, land) in enumerate(zip(srcs, lands)):
            for m in range(1, 8):
                px, py, pc = _peer(x, y, c, m)
                copy = pltpu.make_async_remote_copy(
                    src_ref=_piece_rows(src, 4 * px + 2 * py + pc), dst_ref=land.at[m - 1], send_sem=send[a].at[m - 1],
                    recv_sem=recv[a].at[m - 1], device_id=(px, py, pc), device_id_type=MESH)
                copy.wait_send()
                copy.wait_recv()

    outs = pl.pallas_call(
        body, name=name, in_specs=[HBM] * (2 * n) + [SEM] * (2 * n) + [ANY], out_specs=[HBM] * (2 * n),
        out_shape=[pltpu.HBM(a.shape, a.dtype) for a in list(arrays) + list(zones)],
        input_output_aliases={a: a for a in range(2 * n)}, compiler_params=SPLIT_COPY,
    )(*arrays, *zones, *send_sems, *recv_sems, after)
    return outs[n:]


def _sum_small(name, partials, recvs, place_arr):
    n = len(partials)

    def body(place_ref, *refs):
        for o_ref, r_ref, out_ref in zip(refs[:n], refs[n:2 * n], refs[2 * n:]):
            total = o_ref[...]
            for m in range(7):
                total = total + r_ref[m].astype(F32)
            out_ref[...] = total

    piece = lambda a: pl.BlockSpec((a.shape[0] // 8, a.shape[1]), lambda i, place: (place[0], 0))
    return pl.pallas_call(
        body, name=name,
        grid_spec=pltpu.PrefetchScalarGridSpec(
            num_scalar_prefetch=1, grid=(1,),
            in_specs=[piece(a) for a in partials] + [pl.BlockSpec(r.shape, lambda i, place: (0, 0, 0)) for r in recvs],
            out_specs=[piece(a) for a in partials]),
        out_shape=[jax.ShapeDtypeStruct(a.shape, F32) for a in partials],
        compiler_params=_params(),
    )(place_arr, *partials, *recvs)


def _share(name, bufs, parts, gathered=(), collective_id=None, summed=()):
    n, n_g, n_s = len(bufs), len(gathered), len(summed)
    total = n + n_g
    made = [target for target, _, _ in summed]

    def body(*refs):
        ins, extra, outs = refs[:total], refs[total:total + 2 * n_s], refs[total + 2 * n_s:2 * total + 2 * n_s]
        send_sems, recv_sems, send_g, recv_g = refs[2 * total + 2 * n_s:2 * total + 2 * n_s + 4]
        scratch = refs[2 * total + 2 * n_s + 4:]
        x, y, c = _place()

        def half(ref, l, which):
            p = ref.shape[1] // 2
            return ref.at[l, pl.ds(pl.multiple_of(which * p, 8), p), :]

        def loads(j):
            local, acc, got = scratch[0], scratch[1:1 + n_s], scratch[1 + n_s:]
            own_rows = extra[2 * j] if made[j][0] == "half" else _piece_rows(extra[2 * j], 4 * x + 2 * y + c)
            return [pltpu.make_async_copy(own_rows, acc[j], local.at[2 * j]),
                    pltpu.make_async_copy(extra[2 * j + 1], got[j], local.at[2 * j + 1])]

        def sum_of(j):
            local, acc, got = scratch[0], scratch[1:1 + n_s], scratch[1 + n_s:]
            for load in loads(j):
                load.wait()
            rows = acc[j].shape[0]
            step = rows if rows <= 128 else 96
            for r in range(0, rows, step):
                part = acc[j][r:r + step, :]
                for m in range(7):
                    part = part + got[j][m, r:r + step, :].astype(F32)
                acc[j][r:r + step, :] = part
            if made[j][0] == "half":
                dest = half(outs[made[j][1]], made[j][2], c)
            else:
                dest = _piece_rows(outs[n + made[j][1]], 4 * x + 2 * y + c)
            store = pltpu.make_async_copy(acc[j], dest, local.at[2 * j])
            store.start()
            store.wait()

        def early():
            for j in range(n_s):
                for load in loads(j):
                    load.start()
            for j in range(n_s):
                if made[j][0] == "piece":
                    sum_of(j)

        _handshake([_peer(x, y, c, m) for m in (SAME_CORE if gathered else ()) + (1,)], early if summed else None)

        def swap(k, which):
            a, l = parts[k]
            held = outs if ("half", a, l) in made else ins
            return pltpu.make_async_remote_copy(
                src_ref=half(held[a], l, which), dst_ref=half(outs[a], l, which), send_sem=send_sems.at[k],
                recv_sem=recv_sems.at[k], device_id=(x, y, 1 - c), device_id_type=MESH)

        def spread(a, m, sender, held, to):
            k = 4 * sender[0] + 2 * sender[1] + sender[2]
            return pltpu.make_async_remote_copy(
                src_ref=_piece_rows(held[n + a], k), dst_ref=_piece_rows(outs[n + a], k),
                send_sem=send_g.at[7 * a + m - 1], recv_sem=recv_g.at[7 * a + m - 1], device_id=to, device_id_type=MESH)

        me, sibling = (x, y, c), (x, y, 1 - c)

        def own(a, m):
            return spread(a, m, me, outs if ("piece", a) in made else ins, _peer(x, y, c, m))

        def handed_on(a, m):
            return spread(a, m + 1, _peer(x, y, c, m), outs, sibling)

        for a in range(n_g):
            for m in SAME_CORE + (1,):
                own(a, m).start()
        for j in range(n_s):
            if made[j][0] == "half":
                sum_of(j)
        for k in range(len(parts)):
            swap(k, c).start()
        for a in range(n_g):
            for m in SAME_CORE:
                spread(a, m, _peer(x, y, c, m), ins, _peer(x, y, c, m)).wait_recv()
                handed_on(a, m).start()
        for k in range(len(parts)):
            swap(k, c).wait_send()
            swap(k, 1 - c).wait_recv()
        for a in range(n_g):
            for m in SAME_CORE + (1,):
                own(a, m).wait_send()
            for m in SAME_CORE:
                handed_on(a, m).wait_send()
                spread(a, m + 1, _peer(x, y, c, m + 1), ins, sibling).wait_recv()
            spread(a, 1, sibling, ins, sibling).wait_recv()

    arrays = list(bufs) + list(gathered)
    sum_scratch = []
    if summed:
        sum_scratch = [pltpu.SemaphoreType.DMA((2 * n_s,))] + [pltpu.VMEM(recv.shape[1:], F32) for _, _, recv in summed]
        sum_scratch += [pltpu.VMEM(recv.shape, recv.dtype) for _, _, recv in summed]
    return pl.pallas_call(
        body, name=name, in_specs=[ANY] * (total + 2 * n_s), out_specs=[ANY] * total,
        out_shape=[jax.ShapeDtypeStruct(b.shape, F32) for b in arrays],
        input_output_aliases={a: a for a in range(total)},
        scratch_shapes=[pltpu.SemaphoreType.DMA((max(len(parts), 1),))] * 2
        + [pltpu.SemaphoreType.DMA((max(7 * n_g, 1),))] * 2 + sum_scratch,
        compiler_params=pltpu.CompilerParams(collective_id=collective_id, vmem_limit_bytes=VMEM_LIMIT),
    )(*arrays, *[array for _, own, recv in summed for array in (own, recv)])


def _adamw_math(w, g, m, v):
    nm = ADAM_B1 * m + (1.0 - ADAM_B1) * g
    nv = ADAM_B2 * v + (1.0 - ADAM_B2) * (g * g)
    m_hat = nm / (1.0 - ADAM_B1 ** ADAM_STEP)
    v_hat = nv / (1.0 - ADAM_B2 ** ADAM_STEP)
    return -ADAM_LR * (m_hat / (jnp.sqrt(v_hat) + ADAM_EPS) + ADAM_WD * w), nm, nv


def _adamw(name, w, g, m, v, rows_per_step, first=0, count=None, dests=None, deps=(), small=()):
    layers, rows, cols = w.shape
    count = layers if count is None else count
    dests = () if dests is None else tuple(dests)
    n_in = 4 + len(dests) + len(deps)
    small_shapes = _small_shapes(small[4].shape) if small else []

    def body(*refs):
        w_ref, g_ref, m_ref, v_ref = refs[:4]
        d_ref, nm_ref, nv_ref, g_out_ref = refs[n_in + len(small):n_in + len(small) + 4]
        d_ref[...], nm_ref[...], nv_ref[...] = _adamw_math(w_ref[...], g_ref[...], m_ref[...], v_ref[...])
        g_out_ref[...] = g_ref[...]
        if small:
            @pl.when((pl.program_id(0) == 0) & (pl.program_id(1) == 0))
            def _():
                _adamw_small(*refs[n_in:n_in + len(small)], *refs[n_in + len(small) + 4:])

    spec = pl.BlockSpec((1, rows_per_step, cols), lambda l, i: (first + l, i, 0))
    whole = lambda shape: pl.BlockSpec(shape, lambda l, i: (0,) * len(shape))
    shape = jax.ShapeDtypeStruct(w.shape, F32)
    return pl.pallas_call(
        body, name=name, grid=(count, rows // rows_per_step),
        in_specs=[spec] * 4 + [ANY] * (len(dests) + len(deps)) + [whole(a.shape) for a in small],
        out_specs=[spec] * 4 + [whole(s) for s in small_shapes],
        out_shape=[shape] * 4 + [jax.ShapeDtypeStruct(s, F32) for s in small_shapes],
        input_output_aliases={4 + k: k for k in range(len(dests))},
        scratch_shapes=[pltpu.VMEM((MISC_ROWS, 128), F32)] * 3 if small else [],
        compiler_params=_params(("arbitrary", "arbitrary")),
    )(w, g, m, v, *dests, *deps, *small)


def _pack_misc(pool_scale, sinks, norm_pre, norm_post):
    sink_rows = jnp.zeros((DEPTH, 8, 128), F32).at[:, 0, 0:N_HEADS].set(sinks).reshape(2 * 8, 128)
    return jnp.concatenate([pool_scale.reshape(8, 128), norm_pre.reshape(16, 128), norm_post.reshape(16, 128),
                            sink_rows, jnp.zeros((8, 128), F32)], axis=0)


def _adamw_small(w_ref, g_ref, m_ref, v_ref, pw_ref, pg_ref, pm_ref, pv_ref, *rest):
    outs, pool_outs, (d_ref, nm_ref, nv_ref) = rest[:17], rest[17:21], rest[21:]
    pool_outs[0][...] = pg_ref[...]
    pool_outs[1][...], pool_outs[2][...], pool_outs[3][...] = _adamw_math(
        pw_ref[...], pg_ref[...], pm_ref[...], pv_ref[...])
    d_ref[...], nm_ref[...], nv_ref[...] = _adamw_math(w_ref[...], g_ref[...], m_ref[...], v_ref[...])
    for k, src in enumerate([g_ref, d_ref, nm_ref, nv_ref]):
        scale, sinks, pre, post = outs[4 * k:4 * k + 4]
        for l in range(DEPTH):
            for j in range(4):
                scale[l:l + 1, j * 128:(j + 1) * 128] = src[MISC_SCALE + 4 * l + j:MISC_SCALE + 4 * l + j + 1, :]
            for j in range(8):
                pre[l:l + 1, j * 128:(j + 1) * 128] = src[MISC_PRE + 8 * l + j:MISC_PRE + 8 * l + j + 1, :]
                post[l:l + 1, j * 128:(j + 1) * 128] = src[MISC_POST + 8 * l + j:MISC_POST + 8 * l + j + 1, :]
            sinks[l:l + 1, :] = src[MISC_SINKS + 8 * l:MISC_SINKS + 8 * l + 1, 0:N_HEADS]
    outs[16][...] = g_ref[MISC_LOSS:MISC_LOSS + 1, 0:1]


def _small_shapes(pool_shape):
    return [(DEPTH, D_POOL), (DEPTH, N_HEADS), (DEPTH, D), (DEPTH, D)] * 4 + [(1, 1)] + [pool_shape] * 4


def kernel(x, w_in, pool_w, pool_scale, attn_sinks, w_out, norm_pre, norm_post, loss_target, m_w_in, m_pool_w, m_pool_scale, m_attn_sinks, m_w_out, m_norm_pre, m_norm_post, v_w_in, v_pool_w, v_pool_scale, v_attn_sinks, v_w_out, v_norm_pre, v_norm_post):
    cx, cy, cc = _place()
    chip_arr = jnp.reshape(2 * cx + cy, (1,)).astype(jnp.int32)
    place_arr = jnp.stack([4 * cx + 2 * cy + cc, cc]).astype(jnp.int32)
    t = lambda a: jnp.transpose(a, (0, 2, 1))
    w_in_t = t(w_in)
    xs, target = x[0], loss_target[0]
    pool_w_b = pool_w.astype(BF16)
    tables = _attention_tables()
    scale3 = pool_scale.reshape(DEPTH, 1, D_POOL)
    pre3 = norm_pre.reshape(DEPTH, 1, D)
    post3 = norm_post.reshape(DEPTH, 1, D)

    first = _gather_start_cast("gather_start_first", w_in_t, 0, ID_GATHER_FIRST)
    wi0 = _place_other_half("place_w_in0_rest", w_in_t, place_arr, 0, first[2][0])
    (wi1,) = _place_cast("place_w_in1", w_in_t, chip_arr, 288, [1], deps=(first[3],))
    wo = _place_cast("place_w_out", w_out, chip_arr, 256, [0, 1], deps=(first[3],))
    rest = _gather_start("gather_start_rest", [wi1, wo[0], wo[1]], halved=(0, 1), collective_id=ID_GATHER_REST)
    send, recv, bufs = [first[k] + rest[k] for k in range(3)]
    bufs = [wi0, *bufs[1:]]
    order = {(0, "in"): 0, (1, "in"): 1, (0, "out"): 2, (1, "out"): 3}

    saved = []
    packed = [_pack_misc(pool_scale, attn_sinks, norm_pre, norm_post),
              _pack_misc(m_pool_scale, m_attn_sinks, m_norm_pre, m_norm_post),
              _pack_misc(v_pool_scale, v_attn_sinks, v_norm_pre, v_norm_post)]
    after = (first[3], rest[3], pool_w_b, *tables, scale3, pre3, post3, *packed)
    below = None
    for l in range(DEPTH):
        ks = [order[l, "in"]] + ([] if below is None else [order[l - 1, "out"]])
        halves = _gather_wait(f"gather_wait_in{l}", [bufs[k] for k in ks], [send[k] for k in ks], [recv[k] for k in ks],
                              after, halved=True)
        w_in_l, *w_out_below = _forward_halves(f"forward_w{l}", halves, collective_id=ID_FORWARD[l])
        if below is None:
            pu, pg, q, kv, ag = _fwd_in(l, xs, pre3, w_in_l)
        else:
            saved[l - 1][9] = w_out_below[0]
            y, xs, pu, pg, q, kv, ag = _fwd_in(l, xs, pre3, w_in_l, (below[0], w_out_below[0], below[2]))
            saved[l - 1][7] = y
        cat = _fwd_mix(l, pu, pg, q, kv, ag, pool_w_b, scale3, attn_sinks, tables)
        w_out_l = None
        if l + 1 == DEPTH:
            k = order[l, "out"]
            (w_out_l,) = _gather_wait(f"gather_wait_out{l}", [bufs[k]], [send[k]], [recv[k]], (cat,))
        saved.append([xs, pu, pg, q, kv, ag, cat, None, w_in_l, w_out_l])
        below, after = (cat, None, post3), (cat,)

    x_in, pu, pg, q, kv, ag, cat, y, w_in_l, w_out_l = saved[1]
    dcat, dw_out1, dw_out1_b, dg_post1, loss, xs = _bwd_out(1, cat, w_out_l, post3, place_arr, x=x_in, target=target)
    ex1_out = _exchange_start("exchange_start_out1", [dw_out1_b], ID_OUT1)
    dproj, dpw, dsc1, dsink1 = _bwd_mix(1, pu, pg, q, kv, ag, dcat, pool_w_b, scale3, attn_sinks, tables,
                                        deps=(ex1_out[2][0],))
    dx, dg_pre1, dw_in1, dw_in1_b = _bwd_in_dx(1, dproj, w_in_l, x_in, pre3, xs, dw_place=place_arr)
    ex1_in = _exchange_start("exchange_start_in1", [dw_in1_b], ID_IN1)

    x_in, pu, pg, q, kv, ag, cat, y, w_in_l, w_out_l = saved[0]
    dcat, dw_out0, dw_out0_b, dg_post0 = _bwd_out(0, cat, w_out_l, post3, place_arr, dxn=dx, y=y, deps=(ex1_in[2][0],))
    ex0_out = _exchange_start("exchange_start_out0", [dw_out0_b], ID_OUT0)
    dproj, dpw, dsc0, dsink0 = _bwd_mix(0, pu, pg, q, kv, ag, dcat, pool_w_b, scale3, attn_sinks, tables,
                                        deps=(ex0_out[2][0],), dpw_dest=dpw)
    flat = lambda a: a.reshape(DEPTH * 4 * 128, 128)
    dw_in0, dw_in0_b, dpw_b = _bwd_in_dw(0, dproj, x_in, pre3, place_arr, flat(dpw))
    ex0_in = _exchange_start("exchange_start_in0", [dpw_b, dw_in0_b], ID_IN0)

    grad_x, dg_pre0 = _bwd_in_dx(0, dproj, w_in_l, x_in, pre3, dx, deps=(ex0_in[2][1],))
    small = [jnp.concatenate([dsc0, dsc1, dg_pre0, dg_pre1, dg_post0, dg_post1, dsink0, dsink1, loss], axis=0)]
    ex_small = _exchange_start("exchange_start_small", small, ID_SMALL)
    earlier = [list(ex1_out[k]) + list(ex1_in[k]) + list(ex0_out[k]) for k in range(4)]
    recv_out1, recv_in1, recv_out0 = _exchange_wait("exchange_wait_earlier", earlier, ex_small[2][0])
    g_in, g_out = _share(
        "share_a", [lax.empty(w_in_t.shape, F32), lax.empty(w_out.shape, F32)], [(0, 1), (1, 0), (1, 1)],
        collective_id=ID_SHARE_A, summed=[(("half", 0, 1), dw_in1, recv_in1), (("half", 1, 1), dw_out1, recv_out1),
                                          (("half", 1, 0), dw_out0, recv_out0)])
    m_in_t, v_in_t = t(m_w_in), t(v_w_in)
    d_out, nm_out, nv_out, grad_w_out = _adamw("adamw_w_out", w_out, g_out, m_w_out, v_w_out, 256)
    upd_in = _adamw("adamw_w_in1", w_in_t, g_in, m_in_t, v_in_t, 288, first=1, count=1, deps=(d_out,))
    (recv_pw,) = _exchange_wait("exchange_wait_pool", ex0_in, upd_in[0], which=[0])
    (g_pw,) = _sum_small("sum_pool", [flat(dpw)], [recv_pw], place_arr)

    (recv_in0,) = _exchange_wait("exchange_wait_in0", ex0_in, g_pw, which=[1])
    (recv_misc,) = _exchange_wait("exchange_wait_small", ex_small, recv_in0)
    g_in, g_pw, g_misc = _share(
        "share_b", [g_in], [(0, 0)], [g_pw, lax.empty(small[0].shape, F32)], collective_id=ID_SHARE_B,
        summed=[(("half", 0, 0), dw_in0, recv_in0), (("piece", 1), small[0], recv_misc)])
    d_in, nm_in, nv_in, grad_w_in_t, *small_out = _adamw(
        "adamw_w_in0", w_in_t, g_in, m_in_t, v_in_t, 288, first=0, count=1, dests=upd_in,
        small=(packed[0], g_misc, packed[1], packed[2], flat(pool_w), g_pw, flat(m_pool_w), flat(v_pool_w)))
    (g_sc, g_sk, g_pre, g_post, d_sc, d_sk, d_pre, d_post,
     m_sc, m_sk, m_pre, m_post, v_sc, v_sk, v_pre, v_post, loss_sum) = small_out[:17]
    g_pw, d_pw, m_pw, v_pw = [a.reshape(pool_w.shape) for a in small_out[17:]]
    return (loss_sum[0, 0], grad_x[None], t(grad_w_in_t), g_pw, g_sc, g_sk, grad_w_out, g_pre, g_post,
            t(d_in), d_pw, d_sc, d_sk, d_out, d_pre, d_post,
            t(nm_in), m_pw, m_sc, m_sk, nm_out, m_pre, m_post,
            t(nv_in), v_pw, v_sc, v_sk, nv_out, v_pre, v_post)
```
